```python
import math
import jax, jax.numpy as jnp
from jax import lax
import numpy as np

D_MODEL = 1024
BATCH = 16
SEQ = 2048
DEPTH = 1

MLA_HEADS = 8
MLA_NOPE = 64
MLA_ROPE = 32
MLA_V = 64
Q_LORA = 384
KV_LORA = 256
ROPE_THETA = 10000.0
MLA_QK = MLA_NOPE + MLA_ROPE
MLA_WIDTH = MLA_HEADS * MLA_V

DIL_HEADS = 8
DIL_HEAD_DIM = 64
DIL_PATTERNS = ((128, 1), (512, 4), (2048, 16))
DIL_WIDTH = DIL_HEADS * DIL_HEAD_DIM

BLOCK = 128
N_BRANCH = 2
D_FF = 4 * D_MODEL
LN_EPS = 1e-5
RMS_EPS = 1e-6
NEG = -1e30
ALPHA = (2 * DEPTH) ** 0.25
BETA = (8 * DEPTH) ** -0.25

SPLITS = (Q_LORA,
          Q_LORA + KV_LORA,
          Q_LORA + KV_LORA + MLA_ROPE,
          Q_LORA + KV_LORA + MLA_ROPE + 3 * DIL_WIDTH)
IN_WIDTH = Q_LORA + KV_LORA + MLA_ROPE + 3 * DIL_WIDTH + N_BRANCH * D_MODEL

kernel_name = 'hybrid_mla_dilated_gated_deepnorm'


def layer_norm(x, g, b):
    xf = x.astype(jnp.float32)
    mu = jnp.mean(xf, axis=-1, keepdims=True)
    var = jnp.mean(jnp.square(xf - mu), axis=-1, keepdims=True)
    y = (xf - mu) * lax.rsqrt(var + LN_EPS) * g.astype(jnp.float32) + b.astype(jnp.float32)
    return y.astype(x.dtype)


def rms_norm(x, g):
    xf = x.astype(jnp.float32)
    y = xf * lax.rsqrt(jnp.mean(jnp.square(xf), axis=-1, keepdims=True) + RMS_EPS)
    return (y * g.astype(jnp.float32)).astype(x.dtype)


def apply_rope(t, pos):
    half = t.shape[-1] // 2
    inv = jnp.power(ROPE_THETA, -jnp.arange(half, dtype=jnp.float32) / half)
    ang = pos.astype(jnp.float32)[:, None] * inv[None, :]
    cos = jnp.cos(ang)[None, :, None, :]
    sin = jnp.sin(ang)[None, :, None, :]
    tf = t.astype(jnp.float32)
    t1, t2 = tf[..., :half], tf[..., half:]
    return jnp.concatenate([t1 * cos - t2 * sin, t1 * sin + t2 * cos], axis=-1).astype(t.dtype)


def alibi_slopes(n):
    return jnp.asarray([2.0 ** (-8.0 * (i + 1) / n) for i in range(n)], dtype=jnp.float32)


def mla_attention(q_a, kv_a, k_r, g_q_a, w_uq, g_kv_a, w_ukv):
    B, S, _ = q_a.shape
    pos = jnp.arange(S)
    q = (rms_norm(q_a, g_q_a) @ w_uq).reshape(B, S, MLA_HEADS, MLA_QK)
    q = jnp.concatenate([q[..., :MLA_NOPE], apply_rope(q[..., MLA_NOPE:], pos)], axis=-1)
    kv = (rms_norm(kv_a, g_kv_a) @ w_ukv).reshape(B, S, MLA_HEADS, MLA_NOPE + MLA_V)
    k_rope = apply_rope(k_r[:, :, None, :], pos)
    k = jnp.concatenate([kv[..., :MLA_NOPE],
                         jnp.broadcast_to(k_rope, (B, S, MLA_HEADS, MLA_ROPE))], axis=-1)
    v = kv[..., MLA_NOPE:]
    scale = MLA_QK ** -0.5
    nb = S // BLOCK
    q_blocks = q.reshape(B, nb, BLOCK, MLA_HEADS, MLA_QK).transpose(1, 0, 2, 3, 4)
    kpos = jnp.arange(S)

    def one_block(args):
        qb, i = args
        s = jnp.einsum('bqhd,bkhd->bhqk', qb, k).astype(jnp.float32) * scale
        qpos = i * BLOCK + jnp.arange(BLOCK)
        s = jnp.where((kpos[None, :] <= qpos[:, None])[None, None], s, NEG)
        p = jax.nn.softmax(s, axis=-1).astype(v.dtype)
        return jnp.einsum('bhqk,bkhd->bqhd', p, v)

    o = lax.map(one_block, (q_blocks, jnp.arange(nb)))
    return o.transpose(1, 0, 2, 3, 4).reshape(B, S, MLA_WIDTH)


def dilated_pattern(q, k, v, window, dilation, slopes):
    B, S, H, Dh = q.shape
    L = S // dilation
    J = window // dilation
    C = min(BLOCK, L)
    nb = -(-L // C)
    Lp = nb * C

    def to_classes(t):
        t = t.reshape(B, L, dilation, H, Dh).transpose(0, 2, 1, 3, 4)
        t = jnp.pad(t, ((0, 0), (0, 0), (0, Lp - L), (0, 0), (0, 0)))
        return t.reshape(B, dilation, nb, C, H, Dh)

    def with_prev(t):
        prev = jnp.pad(t, ((0, 0), (0, 0), (1, 0), (0, 0), (0, 0), (0, 0)))[:, :, :-1]
        return jnp.concatenate([prev, t], axis=3)

    def from_classes(t):
        X = t.shape[-1]
        t = t.reshape(B, dilation, Lp, H, X)[:, :, :L]
        return t.transpose(0, 2, 1, 3, 4).reshape(B, S, H, X)

    qc = to_classes(q)
    kc = with_prev(to_classes(k))
    vc = with_prev(to_classes(v))
    s = jnp.einsum('brnqhd,brnkhd->brnhqk', qc, kc).astype(jnp.float32) * (Dh ** -0.5)
    qi = jnp.arange(C)[:, None] + C
    ki = jnp.arange(2 * C)[None, :]
    dist = qi - ki
    valid = (dist >= 0) & (dist <= J)
    key_exists = (jnp.arange(nb)[:, None] > 0) | (jnp.arange(2 * C)[None, :] >= C)
    mask = valid[None] & key_exists[:, None, :]
    bias = -slopes[:, None, None] * (dilation * dist).astype(jnp.float32)[None]
    s = jnp.where(mask[None, None, :, None], s + bias, NEG)
    m = jnp.max(s, axis=-1)
    p = jnp.exp(s - m[..., None])
    den = jnp.sum(p, axis=-1)
    o = jnp.einsum('brnhqk,brnkhd->brnqhd', p, vc.astype(jnp.float32))
    den_t = den.transpose(0, 1, 2, 4, 3)[..., None]
    m_t = m.transpose(0, 1, 2, 4, 3)[..., None]
    return from_classes(o / den_t), from_classes(m_t), from_classes(den_t)


def dilated_attention(q, k, v):
    slopes = alibi_slopes(DIL_HEADS)
    res = [dilated_pattern(q, k, v, w, d, slopes) for (w, d) in DIL_PATTERNS]
    m_all = res[0][1]
    for r in res[1:]:
        m_all = jnp.maximum(m_all, r[1])
    num = 0.0
    tot = 0.0
    for o, m, den in res:
        wgt = den * jnp.exp(m - m_all)
        num = num + wgt * o
        tot = tot + wgt
    return (num / tot).astype(q.dtype)


def hybrid_layer(x, w_in, b_gate, g_q_a, w_uq, g_kv_a, w_ukv, w_o_mla, w_o_dil,
                 w_out, ln1_g, ln1_b, w_ff1, w_ff2, ln2_g, ln2_b):
    B, S, D = x.shape
    proj = x @ w_in
    q_a, kv_a, k_r, qkv_d, gates = jnp.split(proj, SPLITS, axis=-1)
    y_a = mla_attention(q_a, kv_a, k_r, g_q_a, w_uq, g_kv_a, w_ukv) @ w_o_mla
    qkv_d = qkv_d.reshape(B, S, 3, DIL_HEADS, DIL_HEAD_DIM)
    o_b = dilated_attention(qkv_d[:, :, 0], qkv_d[:, :, 1], qkv_d[:, :, 2])
    y_b = o_b.reshape(B, S, DIL_WIDTH) @ w_o_dil
    g = jax.nn.sigmoid(gates.reshape(B, S, N_BRANCH, D) + b_gate)
    mixed = (g[:, :, 0] * y_a + g[:, :, 1] * y_b) @ w_out
    h = layer_norm(ALPHA * x + mixed, ln1_g, ln1_b)
    f = jnp.square(jax.nn.relu(h @ w_ff1)) @ w_ff2
    return layer_norm(ALPHA * h + f, ln2_g, ln2_b)


def _fwd_setup_inputs(seed: int = 0) -> dict:
    key = jax.random.key(seed)
    ks = jax.random.split(key, 20)
    f32 = jnp.float32

    def nrm(k, shape, fan_in, scale=1.0):
        return jax.random.normal(k, shape, f32) * (fan_in ** -0.5) * scale

    def gain(k, shape):
        return 1.0 + 0.02 * jax.random.normal(k, shape, f32)

    def small(k, shape):
        return 0.02 * jax.random.normal(k, shape, f32)

    L_ = DEPTH
    return {
        'x': jax.random.normal(ks[0], (BATCH, SEQ, D_MODEL), f32),
        'w_in': nrm(ks[1], (L_, D_MODEL, IN_WIDTH), D_MODEL),
        'b_gate': small(ks[2], (L_, N_BRANCH, D_MODEL)),
        'g_q_a': gain(ks[3], (L_, Q_LORA)),
        'w_uq': nrm(ks[4], (L_, Q_LORA, MLA_HEADS * MLA_QK), Q_LORA),
        'g_kv_a': gain(ks[5], (L_, KV_LORA)),
        'w_ukv': nrm(ks[6], (L_, KV_LORA, MLA_HEADS * (MLA_NOPE + MLA_V)), KV_LORA),
        'w_o_mla': nrm(ks[7], (L_, MLA_WIDTH, D_MODEL), MLA_WIDTH, BETA),
        'w_o_dil': nrm(ks[8], (L_, DIL_WIDTH, D_MODEL), DIL_WIDTH, BETA),
        'w_out': nrm(ks[9], (L_, D_MODEL, D_MODEL), D_MODEL, BETA),
        'ln1_g': gain(ks[10], (L_, D_MODEL)),
        'ln1_b': small(ks[11], (L_, D_MODEL)),
        'w_ff1': nrm(ks[12], (L_, D_MODEL, D_FF), D_MODEL, BETA),
        'w_ff2': nrm(ks[13], (L_, D_FF, D_MODEL), D_FF, BETA),
        'ln2_g': gain(ks[14], (L_, D_MODEL)),
        'ln2_b': small(ks[15], (L_, D_MODEL)),
    }


def _fwd_reference(x, w_in, b_gate, g_q_a, w_uq, g_kv_a, w_ukv, w_o_mla, w_o_dil,
              w_out, ln1_g, ln1_b, w_ff1, w_ff2, ln2_g, ln2_b):
    for l in range(DEPTH):
        x = hybrid_layer(x, w_in[l], b_gate[l], g_q_a[l], w_uq[l], g_kv_a[l], w_ukv[l],
                         w_o_mla[l], w_o_dil[l], w_out[l], ln1_g[l], ln1_b[l],
                         w_ff1[l], w_ff2[l], ln2_g[l], ln2_b[l])
    return x


import jax as _jax
import jax.numpy as _jnp

TWIN_FORMAT = 'train_step'
FWD_PARAMS = ['x', 'w_in', 'b_gate', 'g_q_a', 'w_uq', 'g_kv_a', 'w_ukv', 'w_o_mla', 'w_o_dil', 'w_out', 'ln1_g', 'ln1_b', 'w_ff1', 'w_ff2', 'ln2_g', 'ln2_b']
TWIN_WEIGHTS = ['w_in', 'b_gate', 'g_q_a', 'w_uq', 'g_kv_a', 'w_ukv', 'w_o_mla', 'w_o_dil', 'w_out', 'ln1_g', 'ln1_b', 'w_ff1', 'w_ff2', 'ln2_g', 'ln2_b']
TWIN_DIFF_INPUT = 'x'
TWIN_INPUTS = ['x', 'w_in', 'b_gate', 'g_q_a', 'w_uq', 'g_kv_a', 'w_ukv', 'w_o_mla', 'w_o_dil', 'w_out', 'ln1_g', 'ln1_b', 'w_ff1', 'w_ff2', 'ln2_g', 'ln2_b', 'loss_target', 'm_w_in', 'm_b_gate', 'm_g_q_a', 'm_w_uq', 'm_g_kv_a', 'm_w_ukv', 'm_w_o_mla', 'm_w_o_dil', 'm_w_out', 'm_ln1_g', 'm_ln1_b', 'm_w_ff1', 'm_w_ff2', 'm_ln2_g', 'm_ln2_b', 'v_w_in', 'v_b_gate', 'v_g_q_a', 'v_w_uq', 'v_g_kv_a', 'v_w_ukv', 'v_w_o_mla', 'v_w_o_dil', 'v_w_out', 'v_ln1_g', 'v_ln1_b', 'v_w_ff1', 'v_w_ff2', 'v_ln2_g', 'v_ln2_b']
TWIN_OUTPUTS = ['loss', 'grad_x', 'grad_w_in', 'grad_b_gate', 'grad_g_q_a', 'grad_w_uq', 'grad_g_kv_a', 'grad_w_ukv', 'grad_w_o_mla', 'grad_w_o_dil', 'grad_w_out', 'grad_ln1_g', 'grad_ln1_b', 'grad_w_ff1', 'grad_w_ff2', 'grad_ln2_g', 'grad_ln2_b', 'delta_w_in', 'delta_b_gate', 'delta_g_q_a', 'delta_w_uq', 'delta_g_kv_a', 'delta_w_ukv', 'delta_w_o_mla', 'delta_w_o_dil', 'delta_w_out', 'delta_ln1_g', 'delta_ln1_b', 'delta_w_ff1', 'delta_w_ff2', 'delta_ln2_g', 'delta_ln2_b', 'new_m_w_in', 'new_m_b_gate', 'new_m_g_q_a', 'new_m_w_uq', 'new_m_g_kv_a', 'new_m_w_ukv', 'new_m_w_o_mla', 'new_m_w_o_dil', 'new_m_w_out', 'new_m_ln1_g', 'new_m_ln1_b', 'new_m_w_ff1', 'new_m_w_ff2', 'new_m_ln2_g', 'new_m_ln2_b', 'new_v_w_in', 'new_v_b_gate', 'new_v_g_q_a', 'new_v_w_uq', 'new_v_g_kv_a', 'new_v_w_ukv', 'new_v_w_o_mla', 'new_v_w_o_dil', 'new_v_w_out', 'new_v_ln1_g', 'new_v_ln1_b', 'new_v_w_ff1', 'new_v_w_ff2', 'new_v_ln2_g', 'new_v_ln2_b']
TWIN_LEAF_KINDS = {'loss': 'loss', 'grad_x': 'grad_x', 'grad_w_in': 'grad_w', 'grad_b_gate': 'grad_w', 'grad_g_q_a': 'grad_w', 'grad_w_uq': 'grad_w', 'grad_g_kv_a': 'grad_w', 'grad_w_ukv': 'grad_w', 'grad_w_o_mla': 'grad_w', 'grad_w_o_dil': 'grad_w', 'grad_w_out': 'grad_w', 'grad_ln1_g': 'grad_w', 'grad_ln1_b': 'grad_w', 'grad_w_ff1': 'grad_w', 'grad_w_ff2': 'grad_w', 'grad_ln2_g': 'grad_w', 'grad_ln2_b': 'grad_w', 'delta_w_in': 'delta_w', 'delta_b_gate': 'delta_w', 'delta_g_q_a': 'delta_w', 'delta_w_uq': 'delta_w', 'delta_g_kv_a': 'delta_w', 'delta_w_ukv': 'delta_w', 'delta_w_o_mla': 'delta_w', 'delta_w_o_dil': 'delta_w', 'delta_w_out': 'delta_w', 'delta_ln1_g': 'delta_w', 'delta_ln1_b': 'delta_w', 'delta_w_ff1': 'delta_w', 'delta_w_ff2': 'delta_w', 'delta_ln2_g': 'delta_w', 'delta_ln2_b': 'delta_w', 'new_m_w_in': 'new_m', 'new_m_b_gate': 'new_m', 'new_m_g_q_a': 'new_m', 'new_m_w_uq': 'new_m', 'new_m_g_kv_a': 'new_m', 'new_m_w_ukv': 'new_m', 'new_m_w_o_mla': 'new_m', 'new_m_w_o_dil': 'new_m', 'new_m_w_out': 'new_m', 'new_m_ln1_g': 'new_m', 'new_m_ln1_b': 'new_m', 'new_m_w_ff1': 'new_m', 'new_m_w_ff2': 'new_m', 'new_m_ln2_g': 'new_m', 'new_m_ln2_b': 'new_m', 'new_v_w_in': 'new_v', 'new_v_b_gate': 'new_v', 'new_v_g_q_a': 'new_v', 'new_v_w_uq': 'new_v', 'new_v_g_kv_a': 'new_v', 'new_v_w_ukv': 'new_v', 'new_v_w_o_mla': 'new_v', 'new_v_w_o_dil': 'new_v', 'new_v_w_out': 'new_v', 'new_v_ln1_g': 'new_v', 'new_v_ln1_b': 'new_v', 'new_v_w_ff1': 'new_v', 'new_v_w_ff2': 'new_v', 'new_v_ln2_g': 'new_v', 'new_v_ln2_b': 'new_v'}


def _forward(args):
    return _fwd_reference(*[args[k] for k in FWD_PARAMS])


def _output_shape():
    out = _jax.eval_shape(lambda: _forward(_fwd_setup_inputs(0)))
    return out.shape, out.dtype

N_MICROBATCH = 1
ADAM_LR = 0.001
ADAM_B1 = 0.9
ADAM_B2 = 0.999
ADAM_EPS = 1e-08
ADAM_WD = 0.01
ADAM_STEP = 10
PER_EXAMPLE_BATCH_AXIS = {'x': 0, 'loss_target': 0}
SHARED_INPUTS = []
_WEIGHT_DTYPES = {'w_in': _jnp.float32, 'b_gate': _jnp.float32, 'g_q_a': _jnp.float32, 'w_uq': _jnp.float32, 'g_kv_a': _jnp.float32, 'w_ukv': _jnp.float32, 'w_o_mla': _jnp.float32, 'w_o_dil': _jnp.float32, 'w_out': _jnp.float32, 'ln1_g': _jnp.float32, 'ln1_b': _jnp.float32, 'w_ff1': _jnp.float32, 'w_ff2': _jnp.float32, 'ln2_g': _jnp.float32, 'ln2_b': _jnp.float32}
MOMENT_SCALE = {'w_in': 9.250682e-03, 'b_gate': 3.555829e-03, 'g_q_a': 8.369401e-03, 'w_uq': 6.019568e-03, 'g_kv_a': 1.600812e-02, 'w_ukv': 7.606682e-03, 'w_o_mla': 1.061081e-02, 'w_o_dil': 1.899286e-02, 'w_out': 2.175607e-02, 'ln1_g': 9.988610e-01, 'ln1_b': 4.627998e-01, 'w_ff1': 3.653354e-02, 'w_ff2': 8.189108e-02, 'ln2_g': 3.201204e+01, 'ln2_b': 2.660276e+00}


def _to_microbatches(a, axis):
    t = _jnp.moveaxis(a, axis, 0)
    t = t.reshape((N_MICROBATCH, t.shape[0] // N_MICROBATCH) + t.shape[1:])
    return _jnp.moveaxis(t, 1, axis + 1)


def setup_inputs(seed: int = 0) -> dict:
    inp = _fwd_setup_inputs(seed)
    key = _jax.random.fold_in(_jax.random.key(seed), 7919)
    shape, _ = _output_shape()
    out = dict(inp)
    out["loss_target"] = _jax.random.normal(_jax.random.fold_in(key, 0), shape, _jnp.float32)
    for i, name in enumerate(TWIN_WEIGHTS):
        w = inp[name].astype(_jnp.float32)
        if MOMENT_SCALE is None:
            s = _jnp.sqrt(_jnp.mean(_jnp.square(w)) + 1e-30)
        else:
            s = MOMENT_SCALE[name]
        km, kv = _jax.random.split(_jax.random.fold_in(key, i + 1))
        out[name] = w
        out["m_" + name] = s * _jax.random.normal(km, w.shape, _jnp.float32)
        out["v_" + name] = (s * s) * _jax.random.uniform(kv, w.shape, _jnp.float32, 0.5, 1.5)
    if N_MICROBATCH > 1:
        for name, axis in PER_EXAMPLE_BATCH_AXIS.items():
            out[name] = _to_microbatches(out[name], axis)
    return {'x': out['x'], 'w_in': out['w_in'], 'b_gate': out['b_gate'], 'g_q_a': out['g_q_a'], 'w_uq': out['w_uq'], 'g_kv_a': out['g_kv_a'], 'w_ukv': out['w_ukv'], 'w_o_mla': out['w_o_mla'], 'w_o_dil': out['w_o_dil'], 'w_out': out['w_out'], 'ln1_g': out['ln1_g'], 'ln1_b': out['ln1_b'], 'w_ff1': out['w_ff1'], 'w_ff2': out['w_ff2'], 'ln2_g': out['ln2_g'], 'ln2_b': out['ln2_b'], 'loss_target': out['loss_target'], 'm_w_in': out['m_w_in'], 'm_b_gate': out['m_b_gate'], 'm_g_q_a': out['m_g_q_a'], 'm_w_uq': out['m_w_uq'], 'm_g_kv_a': out['m_g_kv_a'], 'm_w_ukv': out['m_w_ukv'], 'm_w_o_mla': out['m_w_o_mla'], 'm_w_o_dil': out['m_w_o_dil'], 'm_w_out': out['m_w_out'], 'm_ln1_g': out['m_ln1_g'], 'm_ln1_b': out['m_ln1_b'], 'm_w_ff1': out['m_w_ff1'], 'm_w_ff2': out['m_w_ff2'], 'm_ln2_g': out['m_ln2_g'], 'm_ln2_b': out['m_ln2_b'], 'v_w_in': out['v_w_in'], 'v_b_gate': out['v_b_gate'], 'v_g_q_a': out['v_g_q_a'], 'v_w_uq': out['v_w_uq'], 'v_g_kv_a': out['v_g_kv_a'], 'v_w_ukv': out['v_w_ukv'], 'v_w_o_mla': out['v_w_o_mla'], 'v_w_o_dil': out['v_w_o_dil'], 'v_w_out': out['v_w_out'], 'v_ln1_g': out['v_ln1_g'], 'v_ln1_b': out['v_ln1_b'], 'v_w_ff1': out['v_w_ff1'], 'v_w_ff2': out['v_w_ff2'], 'v_ln2_g': out['v_ln2_g'], 'v_ln2_b': out['v_ln2_b']}


def _loss(weights, diff, rest, loss_target):
    with _jax.named_scope("forward"):
        args = {**rest, TWIN_DIFF_INPUT: diff, **{k: w.astype(_WEIGHT_DTYPES[k]) for k, w in weights.items()}}
        y = _forward(args)
    with _jax.named_scope("loss_head"):
        err = _jnp.square(y.astype(_jnp.float32) - loss_target)
        return 0.5 * _jnp.sum(_jnp.mean(err, axis=-1)) if err.ndim else 0.5 * err


def _adamw(w, g, m, v):
    m = ADAM_B1 * m + (1.0 - ADAM_B1) * g
    v = ADAM_B2 * v + (1.0 - ADAM_B2) * _jnp.square(g)
    m_hat = m / (1.0 - ADAM_B1 ** ADAM_STEP)
    v_hat = v / (1.0 - ADAM_B2 ** ADAM_STEP)
    delta = -ADAM_LR * (m_hat / (_jnp.sqrt(v_hat) + ADAM_EPS) + ADAM_WD * w)
    return delta, m, v


def reference(x, w_in, b_gate, g_q_a, w_uq, g_kv_a, w_ukv, w_o_mla, w_o_dil, w_out, ln1_g, ln1_b, w_ff1, w_ff2, ln2_g, ln2_b, loss_target, m_w_in, m_b_gate, m_g_q_a, m_w_uq, m_g_kv_a, m_w_ukv, m_w_o_mla, m_w_o_dil, m_w_out, m_ln1_g, m_ln1_b, m_w_ff1, m_w_ff2, m_ln2_g, m_ln2_b, v_w_in, v_b_gate, v_g_q_a, v_w_uq, v_g_kv_a, v_w_ukv, v_w_o_mla, v_w_o_dil, v_w_out, v_ln1_g, v_ln1_b, v_w_ff1, v_w_ff2, v_ln2_g, v_ln2_b):
    given = dict(x=x, w_in=w_in, b_gate=b_gate, g_q_a=g_q_a, w_uq=w_uq, g_kv_a=g_kv_a, w_ukv=w_ukv, w_o_mla=w_o_mla, w_o_dil=w_o_dil, w_out=w_out, ln1_g=ln1_g, ln1_b=ln1_b, w_ff1=w_ff1, w_ff2=w_ff2, ln2_g=ln2_g, ln2_b=ln2_b, loss_target=loss_target, m_w_in=m_w_in, m_b_gate=m_b_gate, m_g_q_a=m_g_q_a, m_w_uq=m_w_uq, m_g_kv_a=m_g_kv_a, m_w_ukv=m_w_ukv, m_w_o_mla=m_w_o_mla, m_w_o_dil=m_w_o_dil, m_w_out=m_w_out, m_ln1_g=m_ln1_g, m_ln1_b=m_ln1_b, m_w_ff1=m_w_ff1, m_w_ff2=m_w_ff2, m_ln2_g=m_ln2_g, m_ln2_b=m_ln2_b, v_w_in=v_w_in, v_b_gate=v_b_gate, v_g_q_a=v_g_q_a, v_w_uq=v_w_uq, v_g_kv_a=v_g_kv_a, v_w_ukv=v_w_ukv, v_w_o_mla=v_w_o_mla, v_w_o_dil=v_w_o_dil, v_w_out=v_w_out, v_ln1_g=v_ln1_g, v_ln1_b=v_ln1_b, v_w_ff1=v_w_ff1, v_w_ff2=v_w_ff2, v_ln2_g=v_ln2_g, v_ln2_b=v_ln2_b)
    weights = {n: given[n] for n in TWIN_WEIGHTS}
    shared = {n: given[n] for n in SHARED_INPUTS}
    per_example = {n: given[n] for n in ['x']}
    grad_fn = _jax.value_and_grad(_loss, argnums=(0, 1))

    def one_microbatch(ex, loss_target):
        ex = dict(ex)
        diff = ex.pop(TWIN_DIFF_INPUT)
        return grad_fn(weights, diff, {**shared, **ex}, loss_target)

    if N_MICROBATCH == 1:
        loss, (grad_w, grad_x) = one_microbatch(per_example, given["loss_target"])
    else:
        def body(carry, xs):
            loss_sum, grad_sum = carry
            l_k, (gw_k, gx_k) = one_microbatch(xs[0], xs[1])
            with _jax.named_scope("update"):
                return (loss_sum + l_k, _jax.tree.map(_jnp.add, grad_sum, gw_k)), gx_k

        init = (_jnp.zeros((), _jnp.float32), _jax.tree.map(_jnp.zeros_like, weights))
        (loss, grad_w), grad_x = _jax.lax.scan(body, init, (per_example, given["loss_target"]))
    with _jax.named_scope("update"):
        delta_w, new_m, new_v = {}, {}, {}
        for n in TWIN_WEIGHTS:
            delta_w[n], new_m[n], new_v[n] = _adamw(weights[n], grad_w[n], given["m_" + n], given["v_" + n])
    return (loss, grad_x, *[grad_w[n] for n in TWIN_WEIGHTS], *[delta_w[n] for n in TWIN_WEIGHTS],
            *[new_m[n] for n in TWIN_WEIGHTS], *[new_v[n] for n in TWIN_WEIGHTS])
```

```python
import functools
import math

import jax
import jax.numpy as jnp
from jax import lax
from jax.experimental import pallas as pl
from jax.experimental.pallas import tpu as pltpu

F32 = jnp.float32
BF16 = jnp.bfloat16
MESH = pl.DeviceIdType.MESH

D_MODEL = 1024
N_HEADS = 8
LANES = 128
NOPE, ROPE, V_DIM = 64, 32, 64
MLA_QK = NOPE + ROPE
Q_LORA, KV_LORA = 384, 256
DIL_DIM = 64
DIL_PATTERNS = ((128, 1), (512, 4), (2048, 16))
D_FF = 4096
IN_WIDTH = 4256
LN_EPS, RMS_EPS = 1e-5, 1e-6
NEG = -1e30
ALPHA = 2.0 ** 0.25
ROPE_THETA = 10000.0
LR, B1, B2, ADAM_EPS, WD, ADAM_STEP = 0.001, 0.9, 0.999, 1e-8, 0.01, 10

P_GATE, P_QD, P_KD, P_VD, P_LORA, P_KR, P_WIDTH = 0, 2048, 3072, 4096, 5120, 5760, 6144
LORA_W = Q_LORA + KV_LORA
KR_LANE = NOPE

ATT_T = 256
ROW_T = 512
VMEM_LIMIT = 56 * 1024 * 1024

NN = (((1,), (0,)), ((), ()))
NT = (((1,), (1,)), ((), ()))
TN = (((0,), (0,)), ((), ()))


def _params(sem=None, **kw):
    return pltpu.CompilerParams(dimension_semantics=sem, vmem_limit_bytes=VMEM_LIMIT, **kw)


def _matmul(a, b, *, mode, name, tm, tn, tk, out_dtypes=(F32,), extras=(), epilogue=None):
    if mode == "nn":
        (m, k), (k2, n) = a.shape, b.shape
    elif mode == "nt":
        (m, k), (n, k2) = a.shape, b.shape
    else:
        (k, m), (k2, n) = a.shape, b.shape
    assert k == k2, (a.shape, b.shape, mode)
    tm, tn, tk = min(tm, m), min(tn, n), min(tk, k)
    assert m % tm == 0 and n % tn == 0 and k % tk == 0, (name, m, n, k, tm, tn, tk)
    nk = k // tk
    n_ex, n_out = len(extras), len(out_dtypes)
    dims = {"nn": NN, "nt": NT, "tn": TN}[mode]

    def body(*refs):
        a_ref, b_ref = refs[:2]
        ex_refs = refs[2:2 + n_ex]
        out_refs = refs[2 + n_ex:2 + n_ex + n_out]
        part = lax.dot_general(a_ref[...].astype(BF16), b_ref[...].astype(BF16), dims, preferred_element_type=F32)

        def finish(acc):
            outs = epilogue(acc, *[r[...] for r in ex_refs]) if epilogue is not None else (acc,)
            for r, o in zip(out_refs, outs):
                r[...] = o.astype(r.dtype)

        if nk == 1:
            finish(part)
        else:
            acc_ref = refs[-1]
            kk = pl.program_id(2)

            @pl.when(kk == 0)
            def _():
                acc_ref[...] = part

            @pl.when(kk > 0)
            def _():
                acc_ref[...] += part

            @pl.when(kk == nk - 1)
            def _():
                finish(acc_ref[...])

    a_spec = {"nn": pl.BlockSpec((tm, tk), lambda i, j, kk: (i, kk)),
              "nt": pl.BlockSpec((tm, tk), lambda i, j, kk: (i, kk)),
              "tn": pl.BlockSpec((tk, tm), lambda i, j, kk: (kk, i))}[mode]
    b_spec = {"nn": pl.BlockSpec((tk, tn), lambda i, j, kk: (kk, j)),
              "nt": pl.BlockSpec((tn, tk), lambda i, j, kk: (j, kk)),
              "tn": pl.BlockSpec((tk, tn), lambda i, j, kk: (kk, j))}[mode]
    tile = pl.BlockSpec((tm, tn), lambda i, j, kk: (i, j))
    outs = pl.pallas_call(
        body, name=name,
        grid=(m // tm, n // tn, nk),
        in_specs=[a_spec, b_spec] + [tile] * n_ex,
        out_specs=[tile] * n_out,
        out_shape=[jax.ShapeDtypeStruct((m, n), dt) for dt in out_dtypes],
        scratch_shapes=[pltpu.VMEM((tm, tn), F32)] if nk > 1 else [],
        compiler_params=_params(("parallel", "parallel", "arbitrary")),
    )(a, b, *extras)
    return outs[0] if n_out == 1 else outs


def _rowwise(fn, *, name, rows, seq, ins, outs, sums=()):
    tm = min(ROW_T, seq)
    n_pos = seq // tm
    n_in, n_out, n_sum = len(ins), len(outs), len(sums)

    def body(*refs):
        vals = fn(*[r[...] for r in refs[:n_in]])
        for r, v in zip(refs[n_in:n_in + n_out], vals[:n_out]):
            r[...] = v.astype(r.dtype)
        first = pl.program_id(0) == 0
        for r, v in zip(refs[n_in + n_out:], vals[n_out:]):
            @pl.when(first)
            def _(r=r, v=v):
                r[...] = v

            @pl.when(jnp.logical_not(first))
            def _(r=r, v=v):
                r[...] += v

    def spec(arr, width, col, kind):
        if kind == "row":
            return pl.BlockSpec((tm, width), lambda i, col=col: (i, col))
        if kind == "pos":
            return pl.BlockSpec((tm, width), lambda i, col=col: (i % n_pos, col))
        return pl.BlockSpec(arr.shape, lambda i: (0,) * arr.ndim)

    res = pl.pallas_call(
        body, name=name,
        grid=(rows // tm,),
        in_specs=[spec(*t) for t in ins],
        out_specs=[pl.BlockSpec((tm, w), lambda i: (i, 0)) for w, _ in outs]
        + [pl.BlockSpec((1, w), lambda i: (0, 0)) for w in sums],
        out_shape=[jax.ShapeDtypeStruct((rows, w), dt) for w, dt in outs]
        + [jax.ShapeDtypeStruct((1, w), F32) for w in sums],
        compiler_params=_params(("arbitrary",)),
    )(*[t[0] for t in ins])
    return res


def _colsum(v):
    return jnp.sum(v, axis=0, keepdims=True)


def _rope_fwd(t, c, s_up, s_dn):
    return t * c + pltpu.roll(t, LANES - 16, 1) * s_up + pltpu.roll(t, 16, 1) * s_dn


def _rope_bwd(d, c, s_up, s_dn):
    return d * c + pltpu.roll(d * s_up, 16, 1) + pltpu.roll(d * s_dn, LANES - 16, 1)


def _rope_tables(seq):
    half = ROPE // 2
    inv = jnp.power(ROPE_THETA, -jnp.arange(half, dtype=F32) / half)
    ang = jnp.arange(seq, dtype=F32)[:, None] * inv[None, :]
    cos, sin = jnp.cos(ang), jnp.sin(ang)
    zeros = jnp.zeros((seq, half), F32)
    lo, hi = jnp.ones((seq, KR_LANE), F32), jnp.ones((seq, LANES - KR_LANE - ROPE), F32)
    c = jnp.concatenate([lo, cos, cos, hi], axis=1)
    c_rope_only = jnp.concatenate([0 * lo, cos, cos, 0 * hi], axis=1)
    s_up = jnp.concatenate([0 * lo, -sin, zeros, 0 * hi], axis=1)
    s_dn = jnp.concatenate([0 * lo, zeros, sin, 0 * hi], axis=1)
    return c, s_up, s_dn, c_rope_only


def _rms(x, g):
    r = lax.rsqrt(jnp.mean(x * x, axis=1, keepdims=True) + RMS_EPS)
    return x * r * g


def _rms_bwd(x, g, dy):
    r = lax.rsqrt(jnp.mean(x * x, axis=1, keepdims=True) + RMS_EPS)
    xh = x * r
    dxh = dy * g
    dx = r * (dxh - xh * jnp.mean(dxh * xh, axis=1, keepdims=True))
    return dx, _colsum(dy * xh)


def _ln_stats(x):
    mu = jnp.mean(x, axis=1, keepdims=True)
    xc = x - mu
    r = lax.rsqrt(jnp.mean(xc * xc, axis=1, keepdims=True) + LN_EPS)
    return xc * r, r


def _ln_bwd(xh, r, g, dy):
    dxh = dy * g
    dx = r * (dxh - jnp.mean(dxh, axis=1, keepdims=True) - xh * jnp.mean(dxh * xh, axis=1, keepdims=True))
    return dx, _colsum(dy * xh), _colsum(dy)


def _bias_spec(bias, n_grid):
    per_head = bias.shape[0] > 1
    if n_grid == 3:
        return pl.BlockSpec((1,) + bias.shape[1:], lambda b, h, i: (h if per_head else 0, 0, 0, 0))
    return pl.BlockSpec((1,) + bias.shape[1:], lambda b, h: (h if per_head else 0, 0, 0, 0))


def _attn_fwd(q, qb0, k, kb0, v, vb0, bias, scale, *, name, batch, seq):
    t = ATT_T
    nq = seq // t
    rows = batch * seq

    def body(q_ref, k_ref, v_ref, bias_ref, o_ref, lse_ref, kb, vb):
        i = pl.program_id(2)

        @pl.when(i == 0)
        def _():
            kb[...] = k_ref[...].astype(BF16)
            vb[...] = v_ref[...].astype(BF16)

        qt = q_ref[...].astype(BF16)

        def step(j, carry):
            m, l, acc = carry
            off = pl.multiple_of(j * t, t)
            kt = kb[pl.ds(off, t), :]
            vt = vb[pl.ds(off, t), :]
            s = lax.dot_general(qt, kt, NT, preferred_element_type=F32) * scale + bias_ref[0, i - j]
            m_new = jnp.maximum(m, jnp.max(s, axis=1, keepdims=True))
            p = jnp.exp(s - m_new)
            a = jnp.exp(m - m_new)
            l = a * l + jnp.sum(p, axis=1, keepdims=True)
            acc = a * acc + lax.dot_general(p.astype(BF16), vt, NN, preferred_element_type=F32)
            return m_new, l, acc

        init = (jnp.full((t, 1), NEG, F32), jnp.zeros((t, 1), F32), jnp.zeros((t, LANES), F32))
        m, l, acc = lax.fori_loop(0, i + 1, step, init)
        o_ref[...] = acc / l
        lse_ref[...] = jnp.broadcast_to(m + jnp.log(l), (t, LANES))

    slab = lambda b0: pl.BlockSpec((seq, LANES), lambda b, h, i, b0=b0: (b, b0 + h))
    tile_out = pl.BlockSpec((t, LANES), lambda b, h, i: (b * nq + i, h))
    return pl.pallas_call(
        body, name=name,
        grid=(batch, N_HEADS, nq),
        in_specs=[pl.BlockSpec((t, LANES), lambda b, h, i: (b * nq + i, qb0 + h)), slab(kb0), slab(vb0), _bias_spec(bias, 3)],
        out_specs=[tile_out, tile_out],
        out_shape=[jax.ShapeDtypeStruct((rows, N_HEADS * LANES), F32)] * 2,
        scratch_shapes=[pltpu.VMEM((seq, LANES), BF16)] * 2,
        compiler_params=_params(("arbitrary", "arbitrary", "arbitrary")),
    )(q, k, v, bias)


def _attn_bwd(q, qb0, k, kb0, v, vb0, o, do, lse, bias, scale, *, name, batch, seq, out_dtype):
    t = ATT_T
    nq = seq // t
    rows = batch * seq

    def body(q_ref, k_ref, v_ref, o_ref, do_ref, lse_ref, bias_ref, dq_ref, dk_ref, dv_ref, qb, kb, vb, dob, dka, dva):
        qb[...] = q_ref[...].astype(BF16)
        kb[...] = k_ref[...].astype(BF16)
        vb[...] = v_ref[...].astype(BF16)
        dob[...] = do_ref[...].astype(BF16)
        dka[...] = jnp.zeros_like(dka)
        dva[...] = jnp.zeros_like(dva)

        def q_loop(qi, _):
            qo = pl.multiple_of(qi * t, t)
            qt = qb[pl.ds(qo, t), :]
            dot = dob[pl.ds(qo, t), :]
            lse_t = lse_ref[pl.ds(qo, t), 0:1]
            delta = jnp.sum(o_ref[pl.ds(qo, t), :] * do_ref[pl.ds(qo, t), :], axis=1, keepdims=True)

            def k_loop(kj, dq):
                ko = pl.multiple_of(kj * t, t)
                kt = kb[pl.ds(ko, t), :]
                vt = vb[pl.ds(ko, t), :]
                s = lax.dot_general(qt, kt, NT, preferred_element_type=F32) * scale + bias_ref[0, qi - kj]
                p = jnp.exp(s - lse_t)
                dp = lax.dot_general(dot, vt, NT, preferred_element_type=F32)
                ds = (p * (dp - delta) * scale).astype(BF16)
                dka[pl.ds(ko, t), :] += lax.dot_general(ds, qt, TN, preferred_element_type=F32)
                dva[pl.ds(ko, t), :] += lax.dot_general(p.astype(BF16), dot, TN, preferred_element_type=F32)
                return dq + lax.dot_general(ds, kt, NN, preferred_element_type=F32)

            dq = lax.fori_loop(0, qi + 1, k_loop, jnp.zeros((t, LANES), F32))
            dq_ref[pl.ds(qo, t), :] = dq.astype(dq_ref.dtype)
            return 0

        lax.fori_loop(0, nq, q_loop, 0)
        dk_ref[...] = dka[...].astype(dk_ref.dtype)
        dv_ref[...] = dva[...].astype(dv_ref.dtype)

    slab = lambda b0: pl.BlockSpec((seq, LANES), lambda b, h, b0=b0: (b, b0 + h))
    return pl.pallas_call(
        body, name=name,
        grid=(batch, N_HEADS),
        in_specs=[slab(qb0), slab(kb0), slab(vb0), slab(0), slab(0), slab(0), _bias_spec(bias, 2)],
        out_specs=[slab(0)] * 3,
        out_shape=[jax.ShapeDtypeStruct((rows, N_HEADS * LANES), out_dtype)] * 3,
        scratch_shapes=[pltpu.VMEM((seq, LANES), BF16)] * 4 + [pltpu.VMEM((seq, LANES), F32)] * 2,
        compiler_params=_params(("arbitrary", "arbitrary")),
    )(q, k, v, o, do, lse, bias)


def _tile_dist(seq):
    n = seq // ATT_T
    d = jnp.arange(n, dtype=jnp.int32)[:, None, None] * ATT_T
    return d + jnp.arange(ATT_T, dtype=jnp.int32)[None, :, None] - jnp.arange(ATT_T, dtype=jnp.int32)[None, None, :]


def _causal_bias(seq):
    return jnp.where(_tile_dist(seq) >= 0, 0.0, NEG).astype(F32)[None]


def _dilated_bias(seq):
    dist = _tile_dist(seq)
    count = jnp.zeros(dist.shape, F32)
    for window, dilation in DIL_PATTERNS:
        count += ((dist >= 0) & (dist <= window) & (dist % dilation == 0)).astype(F32)
    slopes = jnp.asarray([2.0 ** (-8.0 * (i + 1) / N_HEADS) for i in range(N_HEADS)], F32)
    alibi = -slopes[:, None, None, None] * dist.astype(F32)[None]
    return jnp.where(count[None] > 0, jnp.log(jnp.maximum(count, 1.0))[None] + alibi, NEG).astype(F32)


def _pad_heads(w, width):
    kdim, n = w.shape[0], w.shape[1] // width
    return jnp.pad(w.reshape(kdim, n, width), ((0, 0), (0, 0), (0, LANES - width))).reshape(kdim, n * LANES)


def _unpad_heads(w, width):
    kdim, n = w.shape[0], w.shape[1] // LANES
    return w.reshape(kdim, n, LANES)[:, :, :width].reshape(kdim, n * width)


def _pad_w_in(w):
    q_a_kv_a = w[:, :LORA_W]
    k_r = jnp.pad(w[:, LORA_W:LORA_W + ROPE], ((0, 0), (KR_LANE, LANES - KR_LANE - ROPE)))
    qkv_d = _pad_heads(w[:, LORA_W + ROPE:LORA_W + ROPE + 3 * N_HEADS * DIL_DIM], DIL_DIM)
    gates = w[:, LORA_W + ROPE + 3 * N_HEADS * DIL_DIM:]
    tail = jnp.zeros((w.shape[0], P_WIDTH - P_KR - LANES), w.dtype)
    return jnp.concatenate([gates, qkv_d, q_a_kv_a, k_r, tail], axis=1)


def _unpad_w_in(g):
    gates = g[:, P_GATE:P_QD]
    qkv_d = _unpad_heads(g[:, P_QD:P_LORA], DIL_DIM)
    q_a_kv_a = g[:, P_LORA:P_KR]
    k_r = g[:, P_KR + KR_LANE:P_KR + KR_LANE + ROPE]
    return jnp.concatenate([q_a_kv_a, k_r, qkv_d, gates], axis=1)


def _split_ukv(w):
    w3 = w.reshape(w.shape[0], N_HEADS, NOPE + V_DIM)
    return (_pad_heads(w3[:, :, :NOPE].reshape(w.shape[0], -1), NOPE),
            _pad_heads(w3[:, :, NOPE:].reshape(w.shape[0], -1), V_DIM))


def _merge_ukv(g_k, g_v):
    kdim = g_k.shape[0]
    k3 = _unpad_heads(g_k, NOPE).reshape(kdim, N_HEADS, NOPE)
    v3 = _unpad_heads(g_v, V_DIM).reshape(kdim, N_HEADS, V_DIM)
    return jnp.concatenate([k3, v3], axis=2).reshape(kdim, N_HEADS * (NOPE + V_DIM))


def _pad_rows(w, width):
    return _pad_heads(w.T, width).T


def _unpad_rows(g, width):
    return _unpad_heads(g.T, width).T


def _local_step(x3, target3, w_in, w_uq, w_ukv, w_o_mla, w_o_dil, w_out, w_ff1, w_ff2,
                b_gate, g_q_a, g_kv_a, ln1_g, ln1_b, ln2_g, ln2_b):
    batch, seq, _ = x3.shape
    rows = batch * seq
    x = x3.reshape(rows, D_MODEL)
    target = target3.reshape(rows, D_MODEL)
    row = functools.partial(_rowwise, rows=rows, seq=seq)
    mm = _matmul

    w_in_p = _pad_w_in(w_in)
    w_uq_p = _pad_heads(w_uq, MLA_QK)
    w_uk_p, w_uv_p = _split_ukv(w_ukv)
    w_oa_p = _pad_rows(w_o_mla, V_DIM)
    w_ob_p = _pad_rows(w_o_dil, DIL_DIM)
    b0, b1 = b_gate[0:1], b_gate[1:2]
    rope_c, rope_up, rope_dn, rope_c_only = _rope_tables(seq)
    bias_mla, bias_dil = _causal_bias(seq), _dilated_bias(seq)
    scale_mla, scale_dil = MLA_QK ** -0.5, DIL_DIM ** -0.5
    qd0, kd0, vd0, lora0, kr0 = P_QD // LANES, P_KD // LANES, P_VD // LANES, P_LORA // LORA_W, P_KR // LANES

    proj = mm(x, w_in_p, mode="nn", name="proj", tm=1024, tn=1536, tk=1024)

    def prep(lora, gq, gkv):
        return _rms(lora[:, :Q_LORA], gq), _rms(lora[:, Q_LORA:], gkv)

    qn, kvn = row(prep, name="mla_rms", ins=[(proj, LORA_W, lora0, "row"), (g_q_a, 0, 0, "full"), (g_kv_a, 0, 0, "full")],
                  outs=[(Q_LORA, BF16), (KV_LORA, BF16)])
    q_lin = mm(qn, w_uq_p, mode="nn", name="q_up", tm=1024, tn=1024, tk=Q_LORA)
    k_lin = mm(kvn, w_uk_p, mode="nn", name="k_up", tm=1024, tn=1024, tk=KV_LORA)
    v_a = mm(kvn, w_uv_p, mode="nn", name="v_up", tm=1024, tn=1024, tk=KV_LORA, out_dtypes=(BF16,))

    def rope_qk(ql, kl, kr, c, up, dn):
        k_rot = _rope_fwd(kr, c, up, dn)
        qs = [_rope_fwd(ql[:, h * LANES:(h + 1) * LANES], c, up, dn) for h in range(N_HEADS)]
        ks = [kl[:, h * LANES:(h + 1) * LANES] + k_rot for h in range(N_HEADS)]
        return jnp.concatenate(qs, axis=1), jnp.concatenate(ks, axis=1)

    pos = lambda tab: (tab, LANES, 0, "pos")
    q_a, k_a = row(rope_qk, name="rope_qk",
                   ins=[(q_lin, D_MODEL, 0, "row"), (k_lin, D_MODEL, 0, "row"), (proj, LANES, kr0, "row"), pos(rope_c), pos(rope_up), pos(rope_dn)],
                   outs=[(N_HEADS * LANES, BF16), (N_HEADS * LANES, BF16)])
    o_a, lse_a = _attn_fwd(q_a, 0, k_a, 0, v_a, 0, bias_mla, scale_mla, name="mla_fwd", batch=batch, seq=seq)
    o_b, lse_b = _attn_fwd(proj, qd0, proj, kd0, proj, vd0, bias_dil, scale_dil, name="dil_fwd", batch=batch, seq=seq)
    y_a = mm(o_a, w_oa_p, mode="nn", name="o_mla", tm=1024, tn=1024, tk=1024)
    y_b = mm(o_b, w_ob_p, mode="nn", name="o_dil", tm=1024, tn=1024, tk=1024)

    def gate(t0, t1, c0, c1, ya, yb):
        return (jax.nn.sigmoid(t0 + c0) * ya + jax.nn.sigmoid(t1 + c1) * yb,)

    gate_ins = [(proj, D_MODEL, 0, "row"), (proj, D_MODEL, 1, "row"), (b0, 0, 0, "full"), (b1, 0, 0, "full")]
    (u,) = row(gate, name="gate", ins=gate_ins + [(y_a, D_MODEL, 0, "row"), (y_b, D_MODEL, 0, "row")], outs=[(D_MODEL, BF16)])
    mixed = mm(u, w_out, mode="nn", name="mix", tm=1024, tn=1024, tk=1024)

    def ln1(xv, mv, g, b):
        r1 = ALPHA * xv + mv
        xh, _ = _ln_stats(r1)
        return r1, xh * g + b

    r1, h = row(ln1, name="ln1", ins=[(x, D_MODEL, 0, "row"), (mixed, D_MODEL, 0, "row"), (ln1_g, 0, 0, "full"), (ln1_b, 0, 0, "full")],
                outs=[(D_MODEL, F32), (D_MODEL, F32)])

    def relu2(acc):
        r = jnp.maximum(acc, 0.0)
        return acc, r * r

    a_ff, z = mm(h, w_ff1, mode="nn", name="ff1", tm=1024, tn=1024, tk=1024, out_dtypes=(F32, BF16), epilogue=relu2)
    f = mm(z, w_ff2, mode="nn", name="ff2", tm=1024, tn=1024, tk=1024)

    def ln2_loss(hv, fv, tv, g, b):
        xh, r = _ln_stats(ALPHA * hv + fv)
        err = xh * g + b - tv
        dy = err * (1.0 / D_MODEL)
        dr2, dg, db = _ln_bwd(xh, r, g, dy)
        loss = jnp.sum(_colsum(err * err), axis=1, keepdims=True) * (0.5 / D_MODEL)
        return dr2, jnp.broadcast_to(loss, (1, LANES)), dg, db

    dr2, loss_l, d_ln2_g, d_ln2_b = row(
        ln2_loss, name="ln2_loss",
        ins=[(h, D_MODEL, 0, "row"), (f, D_MODEL, 0, "row"), (target, D_MODEL, 0, "row"), (ln2_g, 0, 0, "full"), (ln2_b, 0, 0, "full")],
        outs=[(D_MODEL, F32)], sums=[LANES, D_MODEL, D_MODEL])

    d_w_ff2 = mm(z, dr2, mode="tn", name="d_w_ff2", tm=1024, tn=1024, tk=512)
    da = mm(dr2, w_ff2, mode="nt", name="d_ff_act", tm=1024, tn=1024, tk=1024, out_dtypes=(BF16,), extras=(a_ff,),
            epilogue=lambda acc, av: (acc * (2.0 * jnp.maximum(av, 0.0)),))
    d_w_ff1 = mm(h, da, mode="tn", name="d_w_ff1", tm=1024, tn=1024, tk=512)
    dh = mm(da, w_ff1, mode="nt", name="d_h", tm=1024, tn=1024, tk=1024, extras=(dr2,), epilogue=lambda acc, rv: (acc + ALPHA * rv,))

    def ln1_bwd(dhv, r1v, g):
        xh, r = _ln_stats(r1v)
        return _ln_bwd(xh, r, g, dhv)

    dr1, d_ln1_g, d_ln1_b = row(ln1_bwd, name="ln1_bwd", ins=[(dh, D_MODEL, 0, "row"), (r1, D_MODEL, 0, "row"), (ln1_g, 0, 0, "full")],
                                outs=[(D_MODEL, F32)], sums=[D_MODEL, D_MODEL])
    d_w_out = mm(u, dr1, mode="tn", name="d_w_out", tm=1024, tn=1024, tk=512)
    du = mm(dr1, w_out, mode="nt", name="d_u", tm=1024, tn=1024, tk=1024)

    def gate_bwd(t0, t1, c0, c1, ya, yb, duv):
        s0, s1 = jax.nn.sigmoid(t0 + c0), jax.nn.sigmoid(t1 + c1)
        dt0 = duv * ya * s0 * (1.0 - s0)
        dt1 = duv * yb * s1 * (1.0 - s1)
        return duv * s0, duv * s1, jnp.concatenate([dt0, dt1], axis=1), jnp.concatenate([_colsum(dt0), _colsum(dt1)], axis=1)

    dy_a, dy_b, d_gates, d_b_gate = row(
        gate_bwd, name="gate_bwd", ins=gate_ins + [(y_a, D_MODEL, 0, "row"), (y_b, D_MODEL, 0, "row"), (du, D_MODEL, 0, "row")],
        outs=[(D_MODEL, BF16), (D_MODEL, BF16), (2 * D_MODEL, BF16)], sums=[2 * D_MODEL])
    d_w_oa_p = mm(o_a, dy_a, mode="tn", name="d_w_o_mla", tm=1024, tn=1024, tk=512)
    d_w_ob_p = mm(o_b, dy_b, mode="tn", name="d_w_o_dil", tm=1024, tn=1024, tk=512)
    do_a = mm(dy_a, w_oa_p, mode="nt", name="d_o_mla", tm=1024, tn=1024, tk=1024)
    do_b = mm(dy_b, w_ob_p, mode="nt", name="d_o_dil", tm=1024, tn=1024, tk=1024)
    dq_a, dk_a, dv_a = _attn_bwd(q_a, 0, k_a, 0, v_a, 0, o_a, do_a, lse_a, bias_mla, scale_mla,
                                 name="mla_bwd", batch=batch, seq=seq, out_dtype=F32)
    dq_d, dk_d, dv_d = _attn_bwd(proj, qd0, proj, kd0, proj, vd0, o_b, do_b, lse_b, bias_dil, scale_dil,
                                 name="dil_bwd", batch=batch, seq=seq, out_dtype=BF16)

    def mla_post(dq, dk, c, up, dn, c_only):
        dqs = [_rope_bwd(dq[:, h * LANES:(h + 1) * LANES], c, up, dn) for h in range(N_HEADS)]
        dk_sum = dk[:, :LANES]
        for h in range(1, N_HEADS):
            dk_sum = dk_sum + dk[:, h * LANES:(h + 1) * LANES]
        return jnp.concatenate(dqs, axis=1), _rope_bwd(dk_sum, c_only, up, dn)

    dq_lin, d_kr = row(mla_post, name="mla_unrope",
                       ins=[(dq_a, D_MODEL, 0, "row"), (dk_a, D_MODEL, 0, "row"), pos(rope_c), pos(rope_up), pos(rope_dn), pos(rope_c_only)],
                       outs=[(N_HEADS * LANES, BF16), (LANES, BF16)])
    d_w_uq_p = mm(qn, dq_lin, mode="tn", name="d_w_uq", tm=Q_LORA, tn=1024, tk=512)
    d_w_uk_p = mm(kvn, dk_a, mode="tn", name="d_w_uk", tm=KV_LORA, tn=1024, tk=512)
    d_w_uv_p = mm(kvn, dv_a, mode="tn", name="d_w_uv", tm=KV_LORA, tn=1024, tk=512)
    d_qn = mm(dq_lin, w_uq_p, mode="nt", name="d_qn", tm=1024, tn=Q_LORA, tk=1024)
    d_kvn_k = mm(dk_a, w_uk_p, mode="nt", name="d_kvn_k", tm=1024, tn=KV_LORA, tk=1024)
    d_kvn = mm(dv_a, w_uv_p, mode="nt", name="d_kvn", tm=1024, tn=KV_LORA, tk=1024, extras=(d_kvn_k,), epilogue=lambda acc, e: (acc + e,))

    def rms_bwd(lora, dq, dkv, gq, gkv):
        dxq, dgq = _rms_bwd(lora[:, :Q_LORA], gq, dq)
        dxk, dgk = _rms_bwd(lora[:, Q_LORA:], gkv, dkv)
        return jnp.concatenate([dxq, dxk], axis=1), dgq, dgk

    d_lora, d_g_q_a, d_g_kv_a = row(
        rms_bwd, name="mla_rms_bwd",
        ins=[(proj, LORA_W, lora0, "row"), (d_qn, Q_LORA, 0, "row"), (d_kvn, KV_LORA, 0, "row"), (g_q_a, 0, 0, "full"), (g_kv_a, 0, 0, "full")],
        outs=[(LORA_W, BF16)], sums=[Q_LORA, KV_LORA])
    d_proj = jnp.concatenate([d_gates, dq_d, dk_d, dv_d, d_lora, d_kr, jnp.zeros((rows, P_WIDTH - P_KR - LANES), BF16)], axis=1)
    d_w_in_p = mm(x, d_proj, mode="tn", name="d_w_in", tm=1024, tn=1536, tk=512)
    grad_x = mm(d_proj, w_in_p, mode="nt", name="d_x", tm=1024, tn=1024, tk=1536, extras=(dr1,), epilogue=lambda acc, rv: (acc + ALPHA * rv,))

    grads = dict(
        w_in=_unpad_w_in(d_w_in_p), w_uq=_unpad_heads(d_w_uq_p, MLA_QK), w_ukv=_merge_ukv(d_w_uk_p, d_w_uv_p),
        w_o_mla=_unpad_rows(d_w_oa_p, V_DIM), w_o_dil=_unpad_rows(d_w_ob_p, DIL_DIM), w_out=d_w_out, w_ff1=d_w_ff1, w_ff2=d_w_ff2,
        b_gate=d_b_gate.reshape(2, D_MODEL), g_q_a=d_g_q_a, g_kv_a=d_g_kv_a, ln1_g=d_ln1_g, ln1_b=d_ln1_b, ln2_g=d_ln2_g, ln2_b=d_ln2_b)
    return loss_l[0, 0], grad_x.reshape(batch, seq, D_MODEL), grads


BIG = (("w_in", (1024, 1064), 1), ("w_uq", (384, 192), 1), ("w_ukv", (256, 256), 1), ("w_o_mla", (512, 256), 1),
       ("w_o_dil", (512, 256), 1), ("w_out", (256, 1024), 0), ("w_ff1", (1024, 1024), 1), ("w_ff2", (1024, 1024), 0))
SMALL = (("b_gate", 2 * D_MODEL), ("g_q_a", Q_LORA), ("g_kv_a", KV_LORA), ("ln1_g", D_MODEL), ("ln1_b", D_MODEL),
         ("ln2_g", D_MODEL), ("ln2_b", D_MODEL))
N_CHIPS = 4
N_DEV = 8
PACK_ROWS = sum(s[0] * s[1] for _, s, _ in BIG) // LANES
HALF_ROWS = PACK_ROWS // 2
ANY = pl.BlockSpec(memory_space=pl.ANY)


def _place():
    x, y, c = lax.axis_index("x"), lax.axis_index("y"), lax.axis_index("c")
    return x, y, c, ((1 - x, y), (x, 1 - y), (1 - x, 1 - y))


def _pack_shards(shards):
    return jnp.concatenate([t.reshape(-1, LANES) for t in shards], axis=0)


def _unpack_shard(flat):
    out, r0 = {}, 0
    for name, shape, _ in BIG:
        n = shape[0] * shape[1] // LANES
        out[name] = flat[r0:r0 + n].reshape(shape)
        r0 += n
    return out


def _unpack_full(gathered):
    out, r0 = {}, 0
    for name, (k, n), axis in BIG:
        rows = k * n // LANES
        part = gathered[:, r0:r0 + rows].reshape(N_CHIPS, k, n)
        out[name] = part.transpose(1, 0, 2).reshape(k, N_CHIPS * n) if axis == 1 else part.reshape(N_CHIPS * k, n)
        r0 += rows
    return out


def _pack_full(grads):
    parts = []
    for name, (k, n), axis in BIG:
        g = grads[name]
        g = g.reshape(k, N_CHIPS, n).transpose(1, 0, 2) if axis == 1 else g.reshape(N_CHIPS, k, n)
        parts.append(g.reshape(N_CHIPS, -1, LANES))
    return jnp.concatenate(parts, axis=1)


def _all_gather_shards(mine):
    def body(src, out, send, recv, local_sem):
        x, y, c, chips = _place()
        rows = pl.ds(pl.multiple_of(c * HALF_ROWS, 16), HALF_ROWS)
        other = pl.ds(pl.multiple_of((1 - c) * HALF_ROWS, 16), HALF_ROWS)
        own = pltpu.make_async_copy(src, out.at[2 * x + y], local_sem)
        own.start()

        def over_ici(j, cx, cy, to):
            return pltpu.make_async_remote_copy(src_ref=src.at[rows], dst_ref=out.at[2 * cx + cy, rows], send_sem=send.at[j],
                                                recv_sem=recv.at[j], device_id=(*to, c), device_id_type=MESH)

        def over_d2d(j, cx, cy, r):
            return pltpu.make_async_remote_copy(src_ref=out.at[2 * cx + cy, r], dst_ref=out.at[2 * cx + cy, r], send_sem=send.at[3 + j],
                                                recv_sem=recv.at[3 + j], device_id=(x, y, 1 - c), device_id_type=MESH)

        first = [over_ici(j, x, y, chip) for j, chip in enumerate(chips)]
        for cp in first:
            cp.start()
        passed = []
        for j, (cx, cy) in enumerate(chips):
            over_ici(j, cx, cy, (cx, cy)).wait_recv()
            passed.append(over_d2d(j, cx, cy, rows))
            passed[-1].start()
        for j, (cx, cy) in enumerate(chips):
            over_d2d(j, cx, cy, other).wait_recv()
        for cp in first + passed:
            cp.wait_send()
        own.wait()

    return pl.pallas_call(
        body, name="gather_weights", in_specs=[ANY], out_specs=ANY,
        out_shape=jax.ShapeDtypeStruct((N_CHIPS,) + mine.shape, mine.dtype),
        scratch_shapes=[pltpu.SemaphoreType.DMA((6,)), pltpu.SemaphoreType.DMA((6,)), pltpu.SemaphoreType.DMA],
        compiler_params=pltpu.CompilerParams(has_side_effects=True),
    )(mine)


def _pair_split(g):
    def body(src, mine, theirs, send, recv, local_sem):
        x, y, c, _ = _place()
        keep = pltpu.make_async_copy(src.at[:, pl.ds(pl.multiple_of(c * HALF_ROWS, 8), HALF_ROWS)], mine, local_sem)
        keep.start()
        give = pltpu.make_async_remote_copy(src_ref=src.at[:, pl.ds(pl.multiple_of((1 - c) * HALF_ROWS, 8), HALF_ROWS)], dst_ref=theirs,
                                            send_sem=send, recv_sem=recv, device_id=(x, y, 1 - c), device_id_type=MESH)
        give.start()
        give.wait()
        keep.wait()

    half = jax.ShapeDtypeStruct((N_CHIPS, HALF_ROWS, LANES), g.dtype)
    return pl.pallas_call(
        body, name="pair_split", in_specs=[ANY], out_specs=[ANY, ANY], out_shape=[half, half],
        scratch_shapes=[pltpu.SemaphoreType.DMA, pltpu.SemaphoreType.DMA, pltpu.SemaphoreType.DMA],
        compiler_params=pltpu.CompilerParams(has_side_effects=True),
    )(g)


def _chip_exchange(p):
    def body(src, out, send, recv, local_sem):
        x, y, c, chips = _place()
        keep = pltpu.make_async_copy(src.at[2 * x + y], out.at[3], local_sem)
        keep.start()
        sends = [pltpu.make_async_remote_copy(src_ref=src.at[2 * cx + cy], dst_ref=out.at[j], send_sem=send.at[j], recv_sem=recv.at[j],
                                              device_id=(cx, cy, c), device_id_type=MESH) for j, (cx, cy) in enumerate(chips)]
        for cp in sends:
            cp.start()
        for cp in sends:
            cp.wait()
        keep.wait()

    return pl.pallas_call(
        body, name="chip_exchange", in_specs=[ANY], out_specs=ANY, out_shape=jax.ShapeDtypeStruct(p.shape, p.dtype),
        scratch_shapes=[pltpu.SemaphoreType.DMA((3,)), pltpu.SemaphoreType.DMA((3,)), pltpu.SemaphoreType.DMA],
        compiler_params=pltpu.CompilerParams(has_side_effects=True),
    )(p)


def _pair_join(t):
    def body(src, out, send, recv, local_sem):
        x, y, c, _ = _place()
        rows = pl.ds(pl.multiple_of(c * HALF_ROWS, 8), HALF_ROWS)
        keep = pltpu.make_async_copy(src, out.at[rows], local_sem)
        keep.start()
        give = pltpu.make_async_remote_copy(src_ref=src, dst_ref=out.at[rows], send_sem=send, recv_sem=recv,
                                            device_id=(x, y, 1 - c), device_id_type=MESH)
        give.start()
        give.wait()
        keep.wait()

    return pl.pallas_call(
        body, name="pair_join", in_specs=[ANY], out_specs=ANY, out_shape=jax.ShapeDtypeStruct((PACK_ROWS, LANES), t.dtype),
        scratch_shapes=[pltpu.SemaphoreType.DMA, pltpu.SemaphoreType.DMA, pltpu.SemaphoreType.DMA],
        compiler_params=pltpu.CompilerParams(has_side_effects=True),
    )(t)


def _sum_all_devices(vec, name):
    n_rows = vec.shape[0]

    def body(v_ref, out_ref, buf, send, recv):
        x, y, c, _ = _place()
        me = 4 * x + 2 * y + c
        buf[me] = v_ref[...]
        flips = [(a, b, d) for a in (0, 1) for b in (0, 1) for d in (0, 1)][1:]
        copies = []
        for r, (a, b, d) in enumerate(flips):
            px, py, pc = (1 - x if a else x), (1 - y if b else y), (1 - c if d else c)
            copies.append(pltpu.make_async_remote_copy(src_ref=v_ref, dst_ref=buf.at[me], send_sem=send.at[r], recv_sem=recv.at[r],
                                                       device_id=(px, py, pc), device_id_type=MESH))
            copies[-1].start()
        for r, (a, b, d) in enumerate(flips):
            px, py, pc = (1 - x if a else x), (1 - y if b else y), (1 - c if d else c)
            pltpu.make_async_remote_copy(src_ref=v_ref, dst_ref=buf.at[4 * px + 2 * py + pc], send_sem=send.at[r], recv_sem=recv.at[r],
                                         device_id=(px, py, pc), device_id_type=MESH).wait_recv()
        for cp in copies:
            cp.wait_send()
        total = buf[0]
        for k in range(1, N_DEV):
            total = total + buf[k]
        out_ref[...] = total

    vmem = pl.BlockSpec(memory_space=pltpu.VMEM)
    return pl.pallas_call(
        body, name=name, in_specs=[vmem], out_specs=vmem, out_shape=jax.ShapeDtypeStruct(vec.shape, F32),
        scratch_shapes=[pltpu.VMEM((N_DEV, n_rows, LANES), F32), pltpu.SemaphoreType.DMA((N_DEV - 1,)), pltpu.SemaphoreType.DMA((N_DEV - 1,))],
        compiler_params=pltpu.CompilerParams(has_side_effects=True),
    )(vec)


def _flat_tile(rows):
    for t in (3760, 1880, 940):
        if rows % t == 0:
            return t
    raise ValueError(rows)


def _pair_add(a, b):
    rows = a.shape[0]
    t = _flat_tile(rows)

    def body(a_ref, b_ref, o_ref):
        o_ref[...] = (a_ref[...] + b_ref[...]).astype(BF16)

    spec = pl.BlockSpec((t, LANES), lambda i: (i, 0))
    return pl.pallas_call(body, name="pair_add", grid=(rows // t,), in_specs=[spec, spec], out_specs=spec,
                          out_shape=jax.ShapeDtypeStruct(a.shape, BF16), compiler_params=_params(("parallel",)))(a, b)


def _chip_sum(parts):
    t = _flat_tile(HALF_ROWS)

    def body(p0, p1, p2, p3, o_ref):
        o_ref[...] = ((p0[0].astype(F32) + p1[0].astype(F32)) + p2[0].astype(F32)) + p3[0].astype(F32)

    return pl.pallas_call(
        body, name="chip_sum", grid=(HALF_ROWS // t,),
        in_specs=[pl.BlockSpec((1, t, LANES), lambda i, k=k: (k, i, 0)) for k in range(N_CHIPS)],
        out_specs=pl.BlockSpec((t, LANES), lambda i: (i, 0)),
        out_shape=jax.ShapeDtypeStruct((HALF_ROWS, LANES), F32), compiler_params=_params(("parallel",)))(parts, parts, parts, parts)


def _reduce_scatter(grads):
    mine, theirs = _pair_split(_pack_full(grads))
    partial = _pair_add(mine.reshape(-1, LANES), theirs.reshape(-1, LANES)).reshape(N_CHIPS, HALF_ROWS, LANES)
    return _unpack_shard(_pair_join(_chip_sum(_chip_exchange(partial))))


def _adamw(w, g, m, v, name):
    rows, width = w.shape
    t = rows
    for cand in (256, 128, 64, 32, 16, 8):
        if rows % cand == 0:
            t = cand
            break

    def body(w_ref, g_ref, m_ref, v_ref, d_ref, nm_ref, nv_ref):
        gv = g_ref[...]
        nm = B1 * m_ref[...] + (1.0 - B1) * gv
        nv = B2 * v_ref[...] + (1.0 - B2) * (gv * gv)
        m_hat = nm / (1.0 - B1 ** ADAM_STEP)
        v_hat = nv / (1.0 - B2 ** ADAM_STEP)
        d_ref[...] = -LR * (m_hat / (jnp.sqrt(v_hat) + ADAM_EPS) + WD * w_ref[...])
        nm_ref[...] = nm
        nv_ref[...] = nv

    spec = pl.BlockSpec((t, width), lambda i: (i, 0))
    return pl.pallas_call(body, name=name, grid=(rows // t,), in_specs=[spec] * 4, out_specs=[spec] * 3,
                          out_shape=[jax.ShapeDtypeStruct(w.shape, F32)] * 3, compiler_params=_params(("parallel",)))(w, g, m, v)


def _rows8(a):
    a = a.reshape(-1, LANES)
    return jnp.pad(a, ((0, -a.shape[0] % 8), (0, 0)))


def _pack_small(vals):
    return jnp.concatenate([_rows8(vals[name]) for name, _ in SMALL], axis=0)


def _unpack_small(packed, shapes):
    out, r0 = {}, 0
    for name, _ in SMALL:
        n = math.prod(shapes[name])
        out[name] = packed[r0:r0 + n // LANES].reshape(shapes[name])
        r0 += -(-n // (8 * LANES)) * 8
    return out


def kernel(x, w_in, b_gate, g_q_a, w_uq, g_kv_a, w_ukv, w_o_mla, w_o_dil, w_out, ln1_g, ln1_b, w_ff1, w_ff2, ln2_g, ln2_b, loss_target, m_w_in, m_b_gate, m_g_q_a, m_w_uq, m_g_kv_a, m_w_ukv, m_w_o_mla, m_w_o_dil, m_w_out, m_ln1_g, m_ln1_b, m_w_ff1, m_w_ff2, m_ln2_g, m_ln2_b, v_w_in, v_b_gate, v_g_q_a, v_w_uq, v_g_kv_a, v_w_ukv, v_w_o_mla, v_w_o_dil, v_w_out, v_ln1_g, v_ln1_b, v_w_ff1, v_w_ff2, v_ln2_g, v_ln2_b):
    order = ("w_in", "b_gate", "g_q_a", "w_uq", "g_kv_a", "w_ukv", "w_o_mla", "w_o_dil", "w_out", "ln1_g", "ln1_b", "w_ff1", "w_ff2", "ln2_g", "ln2_b")
    w = dict(w_in=w_in, b_gate=b_gate, g_q_a=g_q_a, w_uq=w_uq, g_kv_a=g_kv_a, w_ukv=w_ukv, w_o_mla=w_o_mla, w_o_dil=w_o_dil, w_out=w_out,
             ln1_g=ln1_g, ln1_b=ln1_b, w_ff1=w_ff1, w_ff2=w_ff2, ln2_g=ln2_g, ln2_b=ln2_b)
    m = dict(w_in=m_w_in, b_gate=m_b_gate, g_q_a=m_g_q_a, w_uq=m_w_uq, g_kv_a=m_g_kv_a, w_ukv=m_w_ukv, w_o_mla=m_w_o_mla, w_o_dil=m_w_o_dil,
             w_out=m_w_out, ln1_g=m_ln1_g, ln1_b=m_ln1_b, w_ff1=m_w_ff1, w_ff2=m_w_ff2, ln2_g=m_ln2_g, ln2_b=m_ln2_b)
    v = dict(w_in=v_w_in, b_gate=v_b_gate, g_q_a=v_g_q_a, w_uq=v_w_uq, g_kv_a=v_g_kv_a, w_ukv=v_w_ukv, w_o_mla=v_w_o_mla, w_o_dil=v_w_o_dil,
             w_out=v_w_out, ln1_g=v_ln1_g, ln1_b=v_ln1_b, w_ff1=v_w_ff1, w_ff2=v_w_ff2, ln2_g=v_ln2_g, ln2_b=v_ln2_b)
    chip = 2 * lax.axis_index("x") + lax.axis_index("y")
    south = (lax.axis_index("c") == 0).astype(F32)
    gate_w = D_MODEL // N_CHIPS

    full = _unpack_full(_all_gather_shards(_pack_shards([w[n][0] for n, _, _ in BIG]).astype(BF16)))
    b_mine = lax.dynamic_update_slice(jnp.zeros((2, D_MODEL), F32), b_gate[0] * south, (0, chip * gate_w))
    b_full = _sum_all_devices(b_mine.reshape(-1, LANES), "gather_b_gate").reshape(2, D_MODEL)

    loss_part, grad_x, grads = _local_step(
        x, loss_target, full["w_in"], full["w_uq"], full["w_ukv"], full["w_o_mla"], full["w_o_dil"], full["w_out"], full["w_ff1"], full["w_ff2"],
        b_full, g_q_a, g_kv_a, ln1_g, ln1_b, ln2_g, ln2_b)

    g_out = _reduce_scatter(grads)
    small_in = jnp.concatenate([_pack_small(grads), jnp.broadcast_to(loss_part, (8, LANES))], axis=0)
    small = _sum_all_devices(small_in, "sum_small")
    loss = small[small_in.shape[0] - 8, 0]
    g_out.update(_unpack_small(small, {name: (1, n) for name, n in SMALL}))
    g_out["b_gate"] = lax.dynamic_slice(g_out["b_gate"].reshape(2, D_MODEL), (0, chip * gate_w), (2, gate_w))

    delta, new_m, new_v = {}, {}, {}
    for name, shape, _ in BIG:
        delta[name], new_m[name], new_v[name] = _adamw(w[name][0], g_out[name], m[name][0], v[name][0], "adamw_" + name)
    small_shapes = {name: w[name].shape for name, _ in SMALL}
    packed = _adamw(_pack_small(w), _pack_small(g_out), _pack_small(m), _pack_small(v), "adamw_small")
    for d, t in zip((delta, new_m, new_v), packed):
        d.update(_unpack_small(t, small_shapes))

    lead = lambda d: [d[name].reshape(w[name].shape) for name in order]
    return (loss, grad_x, *lead(g_out), *lead(delta), *lead(new_m), *lead(new_v))
```

```python
import functools
import math

import jax
import jax.numpy as jnp
from jax import lax
from jax.experimental import pallas as pl
from jax.experimental.pallas import tpu as pltpu

F32 = jnp.float32
BF16 = jnp.bfloat16
MESH = pl.DeviceIdType.MESH

D_MODEL = 1024
N_HEADS = 8
LANES = 128
NOPE, ROPE, V_DIM = 64, 32, 64
MLA_QK = NOPE + ROPE
Q_LORA, KV_LORA = 384, 256
DIL_DIM = 64
DIL_PATTERNS = ((128, 1), (512, 4), (2048, 16))
D_FF = 4096
IN_WIDTH = 4256
LN_EPS, RMS_EPS = 1e-5, 1e-6
NEG = -1e30
ALPHA = 2.0 ** 0.25
ROPE_THETA = 10000.0
LR, B1, B2, ADAM_EPS, WD, ADAM_STEP = 0.001, 0.9, 0.999, 1e-8, 0.01, 10

P_GATE, P_QD, P_KD, P_VD, P_LORA, P_KR, P_WIDTH = 0, 2048, 3072, 4096, 5120, 5760, 6144
LORA_W = Q_LORA + KV_LORA
KR_LANE = NOPE

ATT_T = 512
ROW_T = 512
VMEM_LIMIT = 56 * 1024 * 1024

NN = (((1,), (0,)), ((), ()))
NT = (((1,), (1,)), ((), ()))
TN = (((0,), (0,)), ((), ()))


def _params(sem=None, **kw):
    return pltpu.CompilerParams(dimension_semantics=sem, vmem_limit_bytes=VMEM_LIMIT, **kw)


def _matmul(a, b, *, mode, name, tm, tn, tk, out_dtypes=(F32,), extras=(), epilogue=None):
    if mode == "nn":
        (m, k), (k2, n) = a.shape, b.shape
    elif mode == "nt":
        (m, k), (n, k2) = a.shape, b.shape
    else:
        (k, m), (k2, n) = a.shape, b.shape
    assert k == k2, (a.shape, b.shape, mode)
    tm, tn, tk = min(tm, m), min(tn, n), min(tk, k)
    assert m % tm == 0 and n % tn == 0 and k % tk == 0, (name, m, n, k, tm, tn, tk)
    nk = k // tk
    n_ex, n_out = len(extras), len(out_dtypes)
    dims = {"nn": NN, "nt": NT, "tn": TN}[mode]

    def body(*refs):
        a_ref, b_ref = refs[:2]
        ex_refs = refs[2:2 + n_ex]
        out_refs = refs[2 + n_ex:2 + n_ex + n_out]
        part = lax.dot_general(a_ref[...].astype(BF16), b_ref[...].astype(BF16), dims, preferred_element_type=F32)

        def finish(acc):
            outs = epilogue(acc, *[r[...] for r in ex_refs]) if epilogue is not None else (acc,)
            for r, o in zip(out_refs, outs):
                r[...] = o.astype(r.dtype)

        if nk == 1:
            finish(part)
        else:
            acc_ref = refs[-1]
            kk = pl.program_id(2)

            @pl.when(kk == 0)
            def _():
                acc_ref[...] = part

            @pl.when(kk > 0)
            def _():
                acc_ref[...] += part

            @pl.when(kk == nk - 1)
            def _():
                finish(acc_ref[...])

    a_spec = {"nn": pl.BlockSpec((tm, tk), lambda i, j, kk: (i, kk)),
              "nt": pl.BlockSpec((tm, tk), lambda i, j, kk: (i, kk)),
              "tn": pl.BlockSpec((tk, tm), lambda i, j, kk: (kk, i))}[mode]
    b_spec = {"nn": pl.BlockSpec((tk, tn), lambda i, j, kk: (kk, j)),
              "nt": pl.BlockSpec((tn, tk), lambda i, j, kk: (j, kk)),
              "tn": pl.BlockSpec((tk, tn), lambda i, j, kk: (kk, j))}[mode]
    tile = pl.BlockSpec((tm, tn), lambda i, j, kk: (i, j))
    outs = pl.pallas_call(
        body, name=name,
        grid=(m // tm, n // tn, nk),
        in_specs=[a_spec, b_spec] + [tile] * n_ex,
        out_specs=[tile] * n_out,
        out_shape=[jax.ShapeDtypeStruct((m, n), dt) for dt in out_dtypes],
        scratch_shapes=[pltpu.VMEM((tm, tn), F32)] if nk > 1 else [],
        compiler_params=_params(("parallel", "parallel", "arbitrary")),
    )(a, b, *extras)
    return outs[0] if n_out == 1 else outs


def _rowwise(fn, *, name, rows, seq, ins, outs, sums=()):
    tm = min(ROW_T, seq)
    n_pos = seq // tm
    n_in, n_out, n_sum = len(ins), len(outs), len(sums)

    def body(*refs):
        vals = fn(*[r[...] for r in refs[:n_in]])
        for r, v in zip(refs[n_in:n_in + n_out], vals[:n_out]):
            r[...] = v.astype(r.dtype)
        first = pl.program_id(0) == 0
        for r, v in zip(refs[n_in + n_out:], vals[n_out:]):
            @pl.when(first)
            def _(r=r, v=v):
                r[...] = v

            @pl.when(jnp.logical_not(first))
            def _(r=r, v=v):
                r[...] += v

    def spec(arr, width, col, kind):
        if kind == "row":
            return pl.BlockSpec((tm, width), lambda i, col=col: (i, col))
        if kind == "pos":
            return pl.BlockSpec((tm, width), lambda i, col=col: (i % n_pos, col))
        return pl.BlockSpec(arr.shape, lambda i: (0,) * arr.ndim)

    res = pl.pallas_call(
        body, name=name,
        grid=(rows // tm,),
        in_specs=[spec(*t) for t in ins],
        out_specs=[pl.BlockSpec((tm, w), lambda i: (i, 0)) for w, _ in outs]
        + [pl.BlockSpec((1, w), lambda i: (0, 0)) for w in sums],
        out_shape=[jax.ShapeDtypeStruct((rows, w), dt) for w, dt in outs]
        + [jax.ShapeDtypeStruct((1, w), F32) for w in sums],
        compiler_params=_params(("arbitrary",)),
    )(*[t[0] for t in ins])
    return res


def _colsum(v):
    return jnp.sum(v, axis=0, keepdims=True)


def _rope_fwd(t, c, s_up, s_dn):
    return t * c + pltpu.roll(t, LANES - 16, 1) * s_up + pltpu.roll(t, 16, 1) * s_dn


def _rope_bwd(d, c, s_up, s_dn):
    return d * c + pltpu.roll(d * s_up, 16, 1) + pltpu.roll(d * s_dn, LANES - 16, 1)


def _rope_tables(seq):
    half = ROPE // 2
    inv = jnp.power(ROPE_THETA, -jnp.arange(half, dtype=F32) / half)
    ang = jnp.arange(seq, dtype=F32)[:, None] * inv[None, :]
    cos, sin = jnp.cos(ang), jnp.sin(ang)
    zeros = jnp.zeros((seq, half), F32)
    lo, hi = jnp.ones((seq, KR_LANE), F32), jnp.ones((seq, LANES - KR_LANE - ROPE), F32)
    c = jnp.concatenate([lo, cos, cos, hi], axis=1)
    c_rope_only = jnp.concatenate([0 * lo, cos, cos, 0 * hi], axis=1)
    s_up = jnp.concatenate([0 * lo, -sin, zeros, 0 * hi], axis=1)
    s_dn = jnp.concatenate([0 * lo, zeros, sin, 0 * hi], axis=1)
    return c, s_up, s_dn, c_rope_only


def _rms(x, g):
    r = lax.rsqrt(jnp.mean(x * x, axis=1, keepdims=True) + RMS_EPS)
    return x * r * g


def _rms_bwd(x, g, dy):
    r = lax.rsqrt(jnp.mean(x * x, axis=1, keepdims=True) + RMS_EPS)
    xh = x * r
    dxh = dy * g
    dx = r * (dxh - xh * jnp.mean(dxh * xh, axis=1, keepdims=True))
    return dx, _colsum(dy * xh)


def _ln_stats(x):
    mu = jnp.mean(x, axis=1, keepdims=True)
    xc = x - mu
    r = lax.rsqrt(jnp.mean(xc * xc, axis=1, keepdims=True) + LN_EPS)
    return xc * r, r


def _ln_bwd(xh, r, g, dy):
    dxh = dy * g
    dx = r * (dxh - jnp.mean(dxh, axis=1, keepdims=True) - xh * jnp.mean(dxh * xh, axis=1, keepdims=True))
    return dx, _colsum(dy * xh), _colsum(dy)


def _bias_spec(bias):
    per_head = bias.shape[0] > 1
    return pl.BlockSpec((1,) + bias.shape[1:], lambda b, h: (h if per_head else 0, 0, 0, 0))


def _attn_fwd(q, qb0, k, kb0, v, vb0, bias, scale, *, name, batch, seq):
    t = ATT_T
    nq = seq // t
    rows = batch * seq

    def body(q_ref, k_ref, v_ref, bias_ref, o_ref, lse_ref, qb, kb, vb):
        qb[...] = q_ref[...].astype(BF16)
        kb[...] = k_ref[...].astype(BF16)
        vb[...] = v_ref[...].astype(BF16)
        for i in range(nq):
            qt = qb[i * t:(i + 1) * t, :]
            logits = [lax.dot_general(qt, kb[j * t:(j + 1) * t, :], NT, preferred_element_type=F32) * scale + bias_ref[0, i - j]
                      for j in range(i + 1)]
            top = functools.reduce(jnp.maximum, logits)
            m = jnp.max(top, axis=1, keepdims=True)
            ps = [jnp.exp(s - m) for s in logits]
            l = jnp.sum(functools.reduce(jnp.add, ps), axis=1, keepdims=True)
            acc = functools.reduce(jnp.add, [lax.dot_general(p.astype(BF16), vb[j * t:(j + 1) * t, :], NN, preferred_element_type=F32)
                                             for j, p in enumerate(ps)])
            o_ref[i * t:(i + 1) * t, :] = acc / l
            lse_ref[i * t:(i + 1) * t, :] = jnp.broadcast_to(m + jnp.log(l), (t, LANES))

    slab = lambda b0: pl.BlockSpec((seq, LANES), lambda b, h, b0=b0: (b, b0 + h))
    return pl.pallas_call(
        body, name=name,
        grid=(batch, N_HEADS),
        in_specs=[slab(qb0), slab(kb0), slab(vb0), _bias_spec(bias)],
        out_specs=[slab(0), slab(0)],
        out_shape=[jax.ShapeDtypeStruct((rows, N_HEADS * LANES), F32)] * 2,
        scratch_shapes=[pltpu.VMEM((seq, LANES), BF16)] * 3,
        compiler_params=_params(("arbitrary", "arbitrary")),
    )(q, k, v, bias)


def _attn_bwd(q, qb0, k, kb0, v, vb0, o, do, lse, bias, scale, *, name, batch, seq, out_dtype):
    t = ATT_T
    nq = seq // t
    rows = batch * seq

    def body(q_ref, k_ref, v_ref, o_ref, do_ref, lse_ref, bias_ref, dq_ref, dk_ref, dv_ref, qb, kb, vb, dob, dka, dva):
        qb[...] = q_ref[...].astype(BF16)
        kb[...] = k_ref[...].astype(BF16)
        vb[...] = v_ref[...].astype(BF16)
        dob[...] = do_ref[...].astype(BF16)
        for i in range(nq):
            at = slice(i * t, (i + 1) * t)
            qt, dot = qb[at, :], dob[at, :]
            lse_t = lse_ref[at, 0:1]
            delta = jnp.sum(o_ref[at, :] * do_ref[at, :], axis=1, keepdims=True)
            dq = None
            for j in range(i + 1):
                kat = slice(j * t, (j + 1) * t)
                kt, vt = kb[kat, :], vb[kat, :]
                p = jnp.exp(lax.dot_general(qt, kt, NT, preferred_element_type=F32) * scale + bias_ref[0, i - j] - lse_t)
                dp = lax.dot_general(dot, vt, NT, preferred_element_type=F32)
                ds = (p * (dp - delta) * scale).astype(BF16)
                dk_part = lax.dot_general(ds, qt, TN, preferred_element_type=F32)
                dv_part = lax.dot_general(p.astype(BF16), dot, TN, preferred_element_type=F32)
                if i == j:
                    dka[kat, :] = dk_part
                    dva[kat, :] = dv_part
                else:
                    dka[kat, :] += dk_part
                    dva[kat, :] += dv_part
                dq_part = lax.dot_general(ds, kt, NN, preferred_element_type=F32)
                dq = dq_part if dq is None else dq + dq_part
            dq_ref[at, :] = dq.astype(dq_ref.dtype)
        dk_ref[...] = dka[...].astype(dk_ref.dtype)
        dv_ref[...] = dva[...].astype(dv_ref.dtype)

    slab = lambda b0: pl.BlockSpec((seq, LANES), lambda b, h, b0=b0: (b, b0 + h))
    return pl.pallas_call(
        body, name=name,
        grid=(batch, N_HEADS),
        in_specs=[slab(qb0), slab(kb0), slab(vb0), slab(0), slab(0), slab(0), _bias_spec(bias)],
        out_specs=[slab(0)] * 3,
        out_shape=[jax.ShapeDtypeStruct((rows, N_HEADS * LANES), out_dtype)] * 3,
        scratch_shapes=[pltpu.VMEM((seq, LANES), BF16)] * 4 + [pltpu.VMEM((seq, LANES), F32)] * 2,
        compiler_params=_params(("arbitrary", "arbitrary")),
    )(q, k, v, o, do, lse, bias)


def _tile_dist(seq):
    n = seq // ATT_T
    d = jnp.arange(n, dtype=jnp.int32)[:, None, None] * ATT_T
    return d + jnp.arange(ATT_T, dtype=jnp.int32)[None, :, None] - jnp.arange(ATT_T, dtype=jnp.int32)[None, None, :]


def _causal_bias(seq):
    return jnp.where(_tile_dist(seq) >= 0, 0.0, NEG).astype(F32)[None]


def _dilated_bias(seq):
    dist = _tile_dist(seq)
    count = jnp.zeros(dist.shape, F32)
    for window, dilation in DIL_PATTERNS:
        count += ((dist >= 0) & (dist <= window) & (dist % dilation == 0)).astype(F32)
    slopes = jnp.asarray([2.0 ** (-8.0 * (i + 1) / N_HEADS) for i in range(N_HEADS)], F32)
    alibi = -slopes[:, None, None, None] * dist.astype(F32)[None]
    return jnp.where(count[None] > 0, jnp.log(jnp.maximum(count, 1.0))[None] + alibi, NEG).astype(F32)


def _pad_heads(w, width):
    kdim, n = w.shape[0], w.shape[1] // width
    return jnp.pad(w.reshape(kdim, n, width), ((0, 0), (0, 0), (0, LANES - width))).reshape(kdim, n * LANES)


def _unpad_heads(w, width):
    kdim, n = w.shape[0], w.shape[1] // LANES
    return w.reshape(kdim, n, LANES)[:, :, :width].reshape(kdim, n * width)


def _pad_w_in(w):
    q_a_kv_a = w[:, :LORA_W]
    k_r = jnp.pad(w[:, LORA_W:LORA_W + ROPE], ((0, 0), (KR_LANE, LANES - KR_LANE - ROPE)))
    qkv_d = _pad_heads(w[:, LORA_W + ROPE:LORA_W + ROPE + 3 * N_HEADS * DIL_DIM], DIL_DIM)
    gates = w[:, LORA_W + ROPE + 3 * N_HEADS * DIL_DIM:]
    tail = jnp.zeros((w.shape[0], P_WIDTH - P_KR - LANES), w.dtype)
    return jnp.concatenate([gates, qkv_d, q_a_kv_a, k_r, tail], axis=1)


def _unpad_w_in(g):
    gates = g[:, P_GATE:P_QD]
    qkv_d = _unpad_heads(g[:, P_QD:P_LORA], DIL_DIM)
    q_a_kv_a = g[:, P_LORA:P_KR]
    k_r = g[:, P_KR + KR_LANE:P_KR + KR_LANE + ROPE]
    return jnp.concatenate([q_a_kv_a, k_r, qkv_d, gates], axis=1)


def _split_ukv(w):
    w3 = w.reshape(w.shape[0], N_HEADS, NOPE + V_DIM)
    return (_pad_heads(w3[:, :, :NOPE].reshape(w.shape[0], -1), NOPE),
            _pad_heads(w3[:, :, NOPE:].reshape(w.shape[0], -1), V_DIM))


def _merge_ukv(g_k, g_v):
    kdim = g_k.shape[0]
    k3 = _unpad_heads(g_k, NOPE).reshape(kdim, N_HEADS, NOPE)
    v3 = _unpad_heads(g_v, V_DIM).reshape(kdim, N_HEADS, V_DIM)
    return jnp.concatenate([k3, v3], axis=2).reshape(kdim, N_HEADS * (NOPE + V_DIM))


def _pad_rows(w, width):
    return _pad_heads(w.T, width).T


def _unpad_rows(g, width):
    return _unpad_heads(g.T, width).T


def _local_step(x3, target3, w_in, w_uq, w_ukv, w_o_mla, w_o_dil, w_out, w_ff1, w_ff2,
                b_gate, g_q_a, g_kv_a, ln1_g, ln1_b, ln2_g, ln2_b):
    batch, seq, _ = x3.shape
    rows = batch * seq
    x = x3.reshape(rows, D_MODEL)
    target = target3.reshape(rows, D_MODEL)
    row = functools.partial(_rowwise, rows=rows, seq=seq)
    mm = _matmul

    w_in_p = _pad_w_in(w_in)
    w_uq_p = _pad_heads(w_uq, MLA_QK)
    w_uk_p, w_uv_p = _split_ukv(w_ukv)
    w_oa_p = _pad_rows(w_o_mla, V_DIM)
    w_ob_p = _pad_rows(w_o_dil, DIL_DIM)
    b0, b1 = b_gate[0:1], b_gate[1:2]
    rope_c, rope_up, rope_dn, rope_c_only = _rope_tables(seq)
    bias_mla, bias_dil = _causal_bias(seq), _dilated_bias(seq)
    scale_mla, scale_dil = MLA_QK ** -0.5, DIL_DIM ** -0.5
    qd0, kd0, vd0, lora0, kr0 = P_QD // LANES, P_KD // LANES, P_VD // LANES, P_LORA // LORA_W, P_KR // LANES

    proj = mm(x, w_in_p, mode="nn", name="proj", tm=1024, tn=1536, tk=1024)

    def prep(lora, gq, gkv):
        return _rms(lora[:, :Q_LORA], gq), _rms(lora[:, Q_LORA:], gkv)

    qn, kvn = row(prep, name="mla_rms", ins=[(proj, LORA_W, lora0, "row"), (g_q_a, 0, 0, "full"), (g_kv_a, 0, 0, "full")],
                  outs=[(Q_LORA, BF16), (KV_LORA, BF16)])
    q_lin = mm(qn, w_uq_p, mode="nn", name="q_up", tm=1024, tn=1024, tk=Q_LORA)
    k_lin = mm(kvn, w_uk_p, mode="nn", name="k_up", tm=1024, tn=1024, tk=KV_LORA)
    v_a = mm(kvn, w_uv_p, mode="nn", name="v_up", tm=1024, tn=1024, tk=KV_LORA, out_dtypes=(BF16,))

    def rope_qk(ql, kl, kr, c, up, dn):
        k_rot = _rope_fwd(kr, c, up, dn)
        qs = [_rope_fwd(ql[:, h * LANES:(h + 1) * LANES], c, up, dn) for h in range(N_HEADS)]
        ks = [kl[:, h * LANES:(h + 1) * LANES] + k_rot for h in range(N_HEADS)]
        return jnp.concatenate(qs, axis=1), jnp.concatenate(ks, axis=1)

    pos = lambda tab: (tab, LANES, 0, "pos")
    q_a, k_a = row(rope_qk, name="rope_qk",
                   ins=[(q_lin, D_MODEL, 0, "row"), (k_lin, D_MODEL, 0, "row"), (proj, LANES, kr0, "row"), pos(rope_c), pos(rope_up), pos(rope_dn)],
                   outs=[(N_HEADS * LANES, BF16), (N_HEADS * LANES, BF16)])
    o_a, lse_a = _attn_fwd(q_a, 0, k_a, 0, v_a, 0, bias_mla, scale_mla, name="mla_fwd", batch=batch, seq=seq)
    o_b, lse_b = _attn_fwd(proj, qd0, proj, kd0, proj, vd0, bias_dil, scale_dil, name="dil_fwd", batch=batch, seq=seq)
    y_a = mm(o_a, w_oa_p, mode="nn", name="o_mla", tm=1024, tn=1024, tk=1024)
    y_b = mm(o_b, w_ob_p, mode="nn", name="o_dil", tm=1024, tn=1024, tk=1024)

    def gate(t0, t1, c0, c1, ya, yb):
        return (jax.nn.sigmoid(t0 + c0) * ya + jax.nn.sigmoid(t1 + c1) * yb,)

    gate_ins = [(proj, D_MODEL, 0, "row"), (proj, D_MODEL, 1, "row"), (b0, 0, 0, "full"), (b1, 0, 0, "full")]
    (u,) = row(gate, name="gate", ins=gate_ins + [(y_a, D_MODEL, 0, "row"), (y_b, D_MODEL, 0, "row")], outs=[(D_MODEL, BF16)])
    mixed = mm(u, w_out, mode="nn", name="mix", tm=1024, tn=1024, tk=1024)

    def ln1(xv, mv, g, b):
        r1 = ALPHA * xv + mv
        xh, _ = _ln_stats(r1)
        return r1, xh * g + b

    r1, h = row(ln1, name="ln1", ins=[(x, D_MODEL, 0, "row"), (mixed, D_MODEL, 0, "row"), (ln1_g, 0, 0, "full"), (ln1_b, 0, 0, "full")],
                outs=[(D_MODEL, F32), (D_MODEL, F32)])

    def relu2(acc):
        r = jnp.maximum(acc, 0.0)
        return acc, r * r

    a_ff, z = mm(h, w_ff1, mode="nn", name="ff1", tm=1024, tn=1024, tk=1024, out_dtypes=(F32, BF16), epilogue=relu2)
    f = mm(z, w_ff2, mode="nn", name="ff2", tm=1024, tn=1024, tk=1024)

    def ln2_loss(hv, fv, tv, g, b):
        xh, r = _ln_stats(ALPHA * hv + fv)
        err = xh * g + b - tv
        dy = err * (1.0 / D_MODEL)
        dr2, dg, db = _ln_bwd(xh, r, g, dy)
        loss = jnp.sum(_colsum(err * err), axis=1, keepdims=True) * (0.5 / D_MODEL)
        return dr2, jnp.broadcast_to(loss, (1, LANES)), dg, db

    dr2, loss_l, d_ln2_g, d_ln2_b = row(
        ln2_loss, name="ln2_loss",
        ins=[(h, D_MODEL, 0, "row"), (f, D_MODEL, 0, "row"), (target, D_MODEL, 0, "row"), (ln2_g, 0, 0, "full"), (ln2_b, 0, 0, "full")],
        outs=[(D_MODEL, F32)], sums=[LANES, D_MODEL, D_MODEL])

    d_w_ff2 = mm(z, dr2, mode="tn", name="d_w_ff2", tm=1024, tn=1024, tk=512)
    da = mm(dr2, w_ff2, mode="nt", name="d_ff_act", tm=1024, tn=1024, tk=1024, out_dtypes=(BF16,), extras=(a_ff,),
            epilogue=lambda acc, av: (acc * (2.0 * jnp.maximum(av, 0.0)),))
    d_w_ff1 = mm(h, da, mode="tn", name="d_w_ff1", tm=1024, tn=1024, tk=512)
    dh = mm(da, w_ff1, mode="nt", name="d_h", tm=1024, tn=1024, tk=1024, extras=(dr2,), epilogue=lambda acc, rv: (acc + ALPHA * rv,))

    def ln1_bwd(dhv, r1v, g):
        xh, r = _ln_stats(r1v)
        return _ln_bwd(xh, r, g, dhv)

    dr1, d_ln1_g, d_ln1_b = row(ln1_bwd, name="ln1_bwd", ins=[(dh, D_MODEL, 0, "row"), (r1, D_MODEL, 0, "row"), (ln1_g, 0, 0, "full")],
                                outs=[(D_MODEL, F32)], sums=[D_MODEL, D_MODEL])
    d_w_out = mm(u, dr1, mode="tn", name="d_w_out", tm=1024, tn=1024, tk=512)
    du = mm(dr1, w_out, mode="nt", name="d_u", tm=1024, tn=1024, tk=1024)

    def gate_bwd(t0, t1, c0, c1, ya, yb, duv):
        s0, s1 = jax.nn.sigmoid(t0 + c0), jax.nn.sigmoid(t1 + c1)
        dt0 = duv * ya * s0 * (1.0 - s0)
        dt1 = duv * yb * s1 * (1.0 - s1)
        return duv * s0, duv * s1, jnp.concatenate([dt0, dt1], axis=1), jnp.concatenate([_colsum(dt0), _colsum(dt1)], axis=1)

    dy_a, dy_b, d_gates, d_b_gate = row(
        gate_bwd, name="gate_bwd", ins=gate_ins + [(y_a, D_MODEL, 0, "row"), (y_b, D_MODEL, 0, "row"), (du, D_MODEL, 0, "row")],
        outs=[(D_MODEL, BF16), (D_MODEL, BF16), (2 * D_MODEL, BF16)], sums=[2 * D_MODEL])
    d_w_oa_p = mm(o_a, dy_a, mode="tn", name="d_w_o_mla", tm=1024, tn=1024, tk=512)
    d_w_ob_p = mm(o_b, dy_b, mode="tn", name="d_w_o_dil", tm=1024, tn=1024, tk=512)
    do_a = mm(dy_a, w_oa_p, mode="nt", name="d_o_mla", tm=1024, tn=1024, tk=1024)
    do_b = mm(dy_b, w_ob_p, mode="nt", name="d_o_dil", tm=1024, tn=1024, tk=1024)
    dq_a, dk_a, dv_a = _attn_bwd(q_a, 0, k_a, 0, v_a, 0, o_a, do_a, lse_a, bias_mla, scale_mla,
                                 name="mla_bwd", batch=batch, seq=seq, out_dtype=F32)
    dq_d, dk_d, dv_d = _attn_bwd(proj, qd0, proj, kd0, proj, vd0, o_b, do_b, lse_b, bias_dil, scale_dil,
                                 name="dil_bwd", batch=batch, seq=seq, out_dtype=BF16)

    def mla_post(dq, dk, c, up, dn, c_only):
        dqs = [_rope_bwd(dq[:, h * LANES:(h + 1) * LANES], c, up, dn) for h in range(N_HEADS)]
        dk_sum = dk[:, :LANES]
        for h in range(1, N_HEADS):
            dk_sum = dk_sum + dk[:, h * LANES:(h + 1) * LANES]
        return jnp.concatenate(dqs, axis=1), _rope_bwd(dk_sum, c_only, up, dn)

    dq_lin, d_kr = row(mla_post, name="mla_unrope",
                       ins=[(dq_a, D_MODEL, 0, "row"), (dk_a, D_MODEL, 0, "row"), pos(rope_c), pos(rope_up), pos(rope_dn), pos(rope_c_only)],
                       outs=[(N_HEADS * LANES, BF16), (LANES, BF16)])
    d_w_uq_p = mm(qn, dq_lin, mode="tn", name="d_w_uq", tm=Q_LORA, tn=1024, tk=512)
    d_w_uk_p = mm(kvn, dk_a, mode="tn", name="d_w_uk", tm=KV_LORA, tn=1024, tk=512)
    d_w_uv_p = mm(kvn, dv_a, mode="tn", name="d_w_uv", tm=KV_LORA, tn=1024, tk=512)
    d_qn = mm(dq_lin, w_uq_p, mode="nt", name="d_qn", tm=1024, tn=Q_LORA, tk=1024)
    d_kvn_k = mm(dk_a, w_uk_p, mode="nt", name="d_kvn_k", tm=1024, tn=KV_LORA, tk=1024)
    d_kvn = mm(dv_a, w_uv_p, mode="nt", name="d_kvn", tm=1024, tn=KV_LORA, tk=1024, extras=(d_kvn_k,), epilogue=lambda acc, e: (acc + e,))

    def rms_bwd(lora, dq, dkv, gq, gkv):
        dxq, dgq = _rms_bwd(lora[:, :Q_LORA], gq, dq)
        dxk, dgk = _rms_bwd(lora[:, Q_LORA:], gkv, dkv)
        return jnp.concatenate([dxq, dxk], axis=1), dgq, dgk

    d_lora, d_g_q_a, d_g_kv_a = row(
        rms_bwd, name="mla_rms_bwd",
        ins=[(proj, LORA_W, lora0, "row"), (d_qn, Q_LORA, 0, "row"), (d_kvn, KV_LORA, 0, "row"), (g_q_a, 0, 0, "full"), (g_kv_a, 0, 0, "full")],
        outs=[(LORA_W, BF16)], sums=[Q_LORA, KV_LORA])
    d_proj = jnp.concatenate([d_gates, dq_d, dk_d, dv_d, d_lora, d_kr, jnp.zeros((rows, P_WIDTH - P_KR - LANES), BF16)], axis=1)
    d_w_in_p = mm(x, d_proj, mode="tn", name="d_w_in", tm=1024, tn=1536, tk=512)
    grad_x = mm(d_proj, w_in_p, mode="nt", name="d_x", tm=1024, tn=1024, tk=1536, extras=(dr1,), epilogue=lambda acc, rv: (acc + ALPHA * rv,))

    grads = dict(
        w_in=_unpad_w_in(d_w_in_p), w_uq=_unpad_heads(d_w_uq_p, MLA_QK), w_ukv=_merge_ukv(d_w_uk_p, d_w_uv_p),
        w_o_mla=_unpad_rows(d_w_oa_p, V_DIM), w_o_dil=_unpad_rows(d_w_ob_p, DIL_DIM), w_out=d_w_out, w_ff1=d_w_ff1, w_ff2=d_w_ff2,
        b_gate=d_b_gate.reshape(2, D_MODEL), g_q_a=d_g_q_a, g_kv_a=d_g_kv_a, ln1_g=d_ln1_g, ln1_b=d_ln1_b, ln2_g=d_ln2_g, ln2_b=d_ln2_b)
    return loss_l[0, 0], grad_x.reshape(batch, seq, D_MODEL), grads


BIG = (("w_in", (1024, 1064), 1), ("w_uq", (384, 192), 1), ("w_ukv", (256, 256), 1), ("w_o_mla", (512, 256), 1),
       ("w_o_dil", (512, 256), 1), ("w_out", (256, 1024), 0), ("w_ff1", (1024, 1024), 1), ("w_ff2", (1024, 1024), 0))
SMALL = (("b_gate", 2 * D_MODEL), ("g_q_a", Q_LORA), ("g_kv_a", KV_LORA), ("ln1_g", D_MODEL), ("ln1_b", D_MODEL),
         ("ln2_g", D_MODEL), ("ln2_b", D_MODEL))
N_CHIPS = 4
N_DEV = 8
PACK_ROWS = sum(s[0] * s[1] for _, s, _ in BIG) // LANES
HALF_ROWS = PACK_ROWS // 2
HALF_CHUNKS = 5
CHUNK_ROWS = HALF_ROWS // HALF_CHUNKS
ANY = pl.BlockSpec(memory_space=pl.ANY)


def _place():
    x, y, c = lax.axis_index("x"), lax.axis_index("y"), lax.axis_index("c")
    return x, y, c, ((1 - x, y), (x, 1 - y), (1 - x, 1 - y))


def _pack_shards(shards):
    return jnp.concatenate([t.reshape(-1, LANES) for t in shards], axis=0)


def _unpack_shard(flat):
    out, r0 = {}, 0
    for name, shape, _ in BIG:
        n = shape[0] * shape[1] // LANES
        out[name] = flat[r0:r0 + n].reshape(shape)
        r0 += n
    return out


def _unpack_full(gathered):
    out, r0 = {}, 0
    for name, (k, n), axis in BIG:
        rows = k * n // LANES
        part = gathered[:, r0:r0 + rows].reshape(N_CHIPS, k, n)
        out[name] = part.transpose(1, 0, 2).reshape(k, N_CHIPS * n) if axis == 1 else part.reshape(N_CHIPS * k, n)
        r0 += rows
    return out


def _pack_full(grads):
    parts = []
    for name, (k, n), axis in BIG:
        g = grads[name]
        g = g.reshape(k, N_CHIPS, n).transpose(1, 0, 2) if axis == 1 else g.reshape(N_CHIPS, k, n)
        parts.append(g.reshape(N_CHIPS, -1, LANES))
    return jnp.concatenate(parts, axis=1)


def _chunks(ref_of, base, n):
    return [ref_of(pl.ds(pl.multiple_of(base + k * CHUNK_ROWS, 16), CHUNK_ROWS)) for k in range(n)]


def _start_chunked(srcs, dsts, send, recv, to):
    for s, d in zip(srcs, dsts):
        pltpu.make_async_remote_copy(src_ref=s, dst_ref=d, send_sem=send, recv_sem=recv, device_id=to, device_id_type=MESH).start()


def _start_local(srcs, dsts, sem):
    for s, d in zip(srcs, dsts):
        pltpu.make_async_copy(s, d, sem).start()


def _all_gather_shards(mine):
    def body(src, out, send, recv, local_sem):
        x, y, c, chips = _place()
        mine_at, other_at = c * HALF_ROWS, (1 - c) * HALF_ROWS
        half = lambda at: pl.ds(pl.multiple_of(at, 16), HALF_ROWS)
        sibling = (x, y, 1 - c)
        own = out.at[2 * x + y]
        _start_local(_chunks(lambda r: src.at[r], 0, 2 * HALF_CHUNKS), _chunks(lambda r: own.at[r], 0, 2 * HALF_CHUNKS), local_sem)

        def whole(j, slab, at, to):
            return pltpu.make_async_remote_copy(src_ref=src.at[half(at)], dst_ref=slab.at[half(at)], send_sem=send.at[j], recv_sem=recv.at[j],
                                                device_id=to, device_id_type=MESH)

        for j, (cx, cy) in enumerate(chips):
            _start_chunked(_chunks(lambda r: src.at[r], mine_at, HALF_CHUNKS), _chunks(lambda r: own.at[r], mine_at, HALF_CHUNKS),
                           send.at[j], recv.at[j], (cx, cy, c))
        for j, (cx, cy) in enumerate(chips):
            slab = out.at[2 * cx + cy]
            whole(j, slab, mine_at, (cx, cy, c)).wait_recv()
            _start_chunked(_chunks(lambda r: slab.at[r], mine_at, HALF_CHUNKS), _chunks(lambda r: slab.at[r], mine_at, HALF_CHUNKS),
                           send.at[3 + j], recv.at[3 + j], sibling)
        for j, (cx, cy) in enumerate(chips):
            whole(3 + j, out.at[2 * cx + cy], other_at, sibling).wait_recv()
        for j in range(6):
            whole(j, own, mine_at, sibling).wait_send()
        pltpu.make_async_copy(src, own, local_sem).wait()

    return pl.pallas_call(
        body, name="gather_weights", in_specs=[ANY], out_specs=ANY,
        out_shape=jax.ShapeDtypeStruct((N_CHIPS,) + mine.shape, mine.dtype),
        scratch_shapes=[pltpu.SemaphoreType.DMA((6,)), pltpu.SemaphoreType.DMA((6,)), pltpu.SemaphoreType.DMA],
        compiler_params=pltpu.CompilerParams(has_side_effects=True),
    )(mine)


def _pair_split(g):
    def body(src, theirs, send, recv):
        x, y, c, _ = _place()
        give_at = (1 - c) * HALF_ROWS
        for j in range(N_CHIPS):
            _start_chunked(_chunks(lambda r: src.at[j, r], give_at, HALF_CHUNKS), _chunks(lambda r: theirs.at[j, r], 0, HALF_CHUNKS),
                           send, recv, (x, y, 1 - c))
        pltpu.make_async_remote_copy(src_ref=src.at[:, pl.ds(pl.multiple_of(give_at, 16), HALF_ROWS)], dst_ref=theirs, send_sem=send,
                                     recv_sem=recv, device_id=(x, y, 1 - c), device_id_type=MESH).wait()

    return pl.pallas_call(
        body, name="pair_split", in_specs=[ANY], out_specs=ANY, out_shape=jax.ShapeDtypeStruct((N_CHIPS, HALF_ROWS, LANES), g.dtype),
        scratch_shapes=[pltpu.SemaphoreType.DMA, pltpu.SemaphoreType.DMA],
        compiler_params=pltpu.CompilerParams(has_side_effects=True),
    )(g)


def _chip_exchange(p):
    def body(src, out, send, recv, local_sem):
        x, y, c, chips = _place()
        mine = src.at[2 * x + y]
        _start_local(_chunks(lambda r: mine.at[r], 0, HALF_CHUNKS), _chunks(lambda r: out.at[3, r], 0, HALF_CHUNKS), local_sem)
        for j, (cx, cy) in enumerate(chips):
            _start_chunked(_chunks(lambda r: src.at[2 * cx + cy, r], 0, HALF_CHUNKS), _chunks(lambda r: out.at[j, r], 0, HALF_CHUNKS),
                           send.at[j], recv.at[j], (cx, cy, c))
        for j, (cx, cy) in enumerate(chips):
            pltpu.make_async_remote_copy(src_ref=src.at[2 * cx + cy], dst_ref=out.at[j], send_sem=send.at[j], recv_sem=recv.at[j],
                                         device_id=(cx, cy, c), device_id_type=MESH).wait()
        pltpu.make_async_copy(mine, out.at[3], local_sem).wait()

    return pl.pallas_call(
        body, name="chip_exchange", in_specs=[ANY], out_specs=ANY, out_shape=jax.ShapeDtypeStruct(p.shape, p.dtype),
        scratch_shapes=[pltpu.SemaphoreType.DMA((3,)), pltpu.SemaphoreType.DMA((3,)), pltpu.SemaphoreType.DMA],
        compiler_params=pltpu.CompilerParams(has_side_effects=True),
    )(p)


def _pair_join(t):
    def body(src, out, send, recv, local_sem):
        x, y, c, _ = _place()
        at = c * HALF_ROWS
        srcs = _chunks(lambda r: src.at[r], 0, HALF_CHUNKS)
        _start_local(srcs, _chunks(lambda r: out.at[r], at, HALF_CHUNKS), local_sem)
        _start_chunked(srcs, _chunks(lambda r: out.at[r], at, HALF_CHUNKS), send, recv, (x, y, 1 - c))
        rows = pl.ds(pl.multiple_of(at, 16), HALF_ROWS)
        pltpu.make_async_remote_copy(src_ref=src, dst_ref=out.at[rows], send_sem=send, recv_sem=recv,
                                     device_id=(x, y, 1 - c), device_id_type=MESH).wait()
        pltpu.make_async_copy(src, out.at[rows], local_sem).wait()

    return pl.pallas_call(
        body, name="pair_join", in_specs=[ANY], out_specs=ANY, out_shape=jax.ShapeDtypeStruct((PACK_ROWS, LANES), t.dtype),
        scratch_shapes=[pltpu.SemaphoreType.DMA, pltpu.SemaphoreType.DMA, pltpu.SemaphoreType.DMA],
        compiler_params=pltpu.CompilerParams(has_side_effects=True),
    )(t)


def _sum_all_devices(vec, name):
    n_rows = vec.shape[0]

    def body(v_ref, out_ref, buf, send, recv):
        x, y, c, _ = _place()
        me = 4 * x + 2 * y + c
        buf[me] = v_ref[...]
        flips = [(a, b, d) for a in (0, 1) for b in (0, 1) for d in (0, 1)][1:]
        copies = []
        for r, (a, b, d) in enumerate(flips):
            px, py, pc = (1 - x if a else x), (1 - y if b else y), (1 - c if d else c)
            copies.append(pltpu.make_async_remote_copy(src_ref=v_ref, dst_ref=buf.at[me], send_sem=send.at[r], recv_sem=recv.at[r],
                                                       device_id=(px, py, pc), device_id_type=MESH))
            copies[-1].start()
        for r, (a, b, d) in enumerate(flips):
            px, py, pc = (1 - x if a else x), (1 - y if b else y), (1 - c if d else c)
            pltpu.make_async_remote_copy(src_ref=v_ref, dst_ref=buf.at[4 * px + 2 * py + pc], send_sem=send.at[r], recv_sem=recv.at[r],
                                         device_id=(px, py, pc), device_id_type=MESH).wait_recv()
        for cp in copies:
            cp.wait_send()
        total = buf[0]
        for k in range(1, N_DEV):
            total = total + buf[k]
        out_ref[...] = total

    vmem = pl.BlockSpec(memory_space=pltpu.VMEM)
    return pl.pallas_call(
        body, name=name, in_specs=[vmem], out_specs=vmem, out_shape=jax.ShapeDtypeStruct(vec.shape, F32),
        scratch_shapes=[pltpu.VMEM((N_DEV, n_rows, LANES), F32), pltpu.SemaphoreType.DMA((N_DEV - 1,)), pltpu.SemaphoreType.DMA((N_DEV - 1,))],
        compiler_params=pltpu.CompilerParams(has_side_effects=True),
    )(vec)


def _flat_tile(rows):
    for t in (3760, 1880, 940):
        if rows % t == 0:
            return t
    raise ValueError(rows)


def _pair_add(g, theirs):
    t = _flat_tile(HALF_ROWS)
    n = HALF_ROWS // t

    def body(c_ref, a_ref, b_ref, o_ref):
        o_ref[...] = (a_ref[...] + b_ref[...]).astype(BF16)

    half = pl.BlockSpec((1, t, LANES), lambda j, i, c_ref: (j, i, 0))
    return pl.pallas_call(
        body, name="pair_add",
        grid_spec=pltpu.PrefetchScalarGridSpec(
            num_scalar_prefetch=1, grid=(N_CHIPS, n),
            in_specs=[pl.BlockSpec((1, t, LANES), lambda j, i, c_ref: (j, c_ref[0] * n + i, 0)), half], out_specs=half),
        out_shape=jax.ShapeDtypeStruct(theirs.shape, BF16), compiler_params=_params(("parallel", "parallel")),
    )(lax.axis_index("c").astype(jnp.int32).reshape(1), g, theirs)


def _chip_sum(parts):
    t = _flat_tile(HALF_ROWS)

    def body(p0, p1, p2, p3, o_ref):
        o_ref[...] = ((p0[0].astype(F32) + p1[0].astype(F32)) + p2[0].astype(F32)) + p3[0].astype(F32)

    return pl.pallas_call(
        body, name="chip_sum", grid=(HALF_ROWS // t,),
        in_specs=[pl.BlockSpec((1, t, LANES), lambda i, k=k: (k, i, 0)) for k in range(N_CHIPS)],
        out_specs=pl.BlockSpec((t, LANES), lambda i: (i, 0)),
        out_shape=jax.ShapeDtypeStruct((HALF_ROWS, LANES), F32), compiler_params=_params(("parallel",)))(parts, parts, parts, parts)


def _reduce_scatter(grads):
    packed = _pack_full(grads)
    partial = _pair_add(packed, _pair_split(packed))
    return _unpack_shard(_pair_join(_chip_sum(_chip_exchange(partial))))


def _adamw(w, g, m, v, name):
    rows, width = w.shape
    t = rows
    for cand in (256, 128, 64, 32, 16, 8):
        if rows % cand == 0:
            t = cand
            break

    def body(w_ref, g_ref, m_ref, v_ref, d_ref, nm_ref, nv_ref):
        gv = g_ref[...]
        nm = B1 * m_ref[...] + (1.0 - B1) * gv
        nv = B2 * v_ref[...] + (1.0 - B2) * (gv * gv)
        m_hat = nm / (1.0 - B1 ** ADAM_STEP)
        v_hat = nv / (1.0 - B2 ** ADAM_STEP)
        d_ref[...] = -LR * (m_hat / (jnp.sqrt(v_hat) + ADAM_EPS) + WD * w_ref[...])
        nm_ref[...] = nm
        nv_ref[...] = nv

    spec = pl.BlockSpec((t, width), lambda i: (i, 0))
    return pl.pallas_call(body, name=name, grid=(rows // t,), in_specs=[spec] * 4, out_specs=[spec] * 3,
                          out_shape=[jax.ShapeDtypeStruct(w.shape, F32)] * 3, compiler_params=_params(("parallel",)))(w, g, m, v)


def _rows8(a):
    a = a.reshape(-1, LANES)
    return jnp.pad(a, ((0, -a.shape[0] % 8), (0, 0)))


def _pack_small(vals):
    return jnp.concatenate([_rows8(vals[name]) for name, _ in SMALL], axis=0)


def _unpack_small(packed, shapes):
    out, r0 = {}, 0
    for name, _ in SMALL:
        n = math.prod(shapes[name])
        out[name] = packed[r0:r0 + n // LANES].reshape(shapes[name])
        r0 += -(-n // (8 * LANES)) * 8
    return out


def kernel(x, w_in, b_gate, g_q_a, w_uq, g_kv_a, w_ukv, w_o_mla, w_o_dil, w_out, ln1_g, ln1_b, w_ff1, w_ff2, ln2_g, ln2_b, loss_target, m_w_in, m_b_gate, m_g_q_a, m_w_uq, m_g_kv_a, m_w_ukv, m_w_o_mla, m_w_o_dil, m_w_out, m_ln1_g, m_ln1_b, m_w_ff1, m_w_ff2, m_ln2_g, m_ln2_b, v_w_in, v_b_gate, v_g_q_a, v_w_uq, v_g_kv_a, v_w_ukv, v_w_o_mla, v_w_o_dil, v_w_out, v_ln1_g, v_ln1_b, v_w_ff1, v_w_ff2, v_ln2_g, v_ln2_b):
    order = ("w_in", "b_gate", "g_q_a", "w_uq", "g_kv_a", "w_ukv", "w_o_mla", "w_o_dil", "w_out", "ln1_g", "ln1_b", "w_ff1", "w_ff2", "ln2_g", "ln2_b")
    w = dict(w_in=w_in, b_gate=b_gate, g_q_a=g_q_a, w_uq=w_uq, g_kv_a=g_kv_a, w_ukv=w_ukv, w_o_mla=w_o_mla, w_o_dil=w_o_dil, w_out=w_out,
             ln1_g=ln1_g, ln1_b=ln1_b, w_ff1=w_ff1, w_ff2=w_ff2, ln2_g=ln2_g, ln2_b=ln2_b)
    m = dict(w_in=m_w_in, b_gate=m_b_gate, g_q_a=m_g_q_a, w_uq=m_w_uq, g_kv_a=m_g_kv_a, w_ukv=m_w_ukv, w_o_mla=m_w_o_mla, w_o_dil=m_w_o_dil,
             w_out=m_w_out, ln1_g=m_ln1_g, ln1_b=m_ln1_b, w_ff1=m_w_ff1, w_ff2=m_w_ff2, ln2_g=m_ln2_g, ln2_b=m_ln2_b)
    v = dict(w_in=v_w_in, b_gate=v_b_gate, g_q_a=v_g_q_a, w_uq=v_w_uq, g_kv_a=v_g_kv_a, w_ukv=v_w_ukv, w_o_mla=v_w_o_mla, w_o_dil=v_w_o_dil,
             w_out=v_w_out, ln1_g=v_ln1_g, ln1_b=v_ln1_b, w_ff1=v_w_ff1, w_ff2=v_w_ff2, ln2_g=v_ln2_g, ln2_b=v_ln2_b)
    chip = 2 * lax.axis_index("x") + lax.axis_index("y")
    south = (lax.axis_index("c") == 0).astype(F32)
    gate_w = D_MODEL // N_CHIPS

    full = _unpack_full(_all_gather_shards(_pack_shards([w[n][0] for n, _, _ in BIG]).astype(BF16)))
    b_mine = lax.dynamic_update_slice(jnp.zeros((2, D_MODEL), F32), b_gate[0] * south, (0, chip * gate_w))
    b_full = _sum_all_devices(b_mine.reshape(-1, LANES), "gather_b_gate").reshape(2, D_MODEL)

    loss_part, grad_x, grads = _local_step(
        x, loss_target, full["w_in"], full["w_uq"], full["w_ukv"], full["w_o_mla"], full["w_o_dil"], full["w_out"], full["w_ff1"], full["w_ff2"],
        b_full, g_q_a, g_kv_a, ln1_g, ln1_b, ln2_g, ln2_b)

    g_out = _reduce_scatter(grads)
    small_in = jnp.concatenate([_pack_small(grads), jnp.broadcast_to(loss_part, (8, LANES))], axis=0)
    small = _sum_all_devices(small_in, "sum_small")
    loss = small[small_in.shape[0] - 8, 0]
    g_out.update(_unpack_small(small, {name: (1, n) for name, n in SMALL}))
    g_out["b_gate"] = lax.dynamic_slice(g_out["b_gate"].reshape(2, D_MODEL), (0, chip * gate_w), (2, gate_w))

    delta, new_m, new_v = {}, {}, {}
    for name, shape, _ in BIG:
        delta[name], new_m[name], new_v[name] = _adamw(w[name][0], g_out[name], m[name][0], v[name][0], "adamw_" + name)
    small_shapes = {name: w[name].shape for name, _ in SMALL}
    packed = _adamw(_pack_small(w), _pack_small(g_out), _pack_small(m), _pack_small(v), "adamw_small")
    for d, t in zip((delta, new_m, new_v), packed):
        d.update(_unpack_small(t, small_shapes))

    lead = lambda d: [d[name].reshape(w[name].shape) for name in order]
    return (loss, grad_x, *lead(g_out), *lead(delta), *lead(new_m), *lead(new_v))
```

```python
import functools
import math

import jax
import jax.numpy as jnp
from jax import lax
from jax.experimental import pallas as pl
from jax.experimental.pallas import tpu as pltpu

F32 = jnp.float32
BF16 = jnp.bfloat16
MESH = pl.DeviceIdType.MESH

D_MODEL = 1024
N_HEADS = 8
LANES = 128
NOPE, ROPE, V_DIM = 64, 32, 64
MLA_QK = NOPE + ROPE
Q_LORA, KV_LORA = 384, 256
DIL_DIM = 64
DIL_PATTERNS = ((128, 1), (512, 4), (2048, 16))
D_FF = 4096
N_CHIPS = 4
N_DEV = 8
IN_WIDTH = 4256
LN_EPS, RMS_EPS = 1e-5, 1e-6
NEG = -1e30
ALPHA = 2.0 ** 0.25
ROPE_THETA = 10000.0
LR, B1, B2, ADAM_EPS, WD, ADAM_STEP = 0.001, 0.9, 0.999, 1e-8, 0.01, 10

P_GATE, P_QD, P_KD, P_VD, P_LORA, P_KR, P_WIDTH = 0, 2048, 3072, 4096, 5120, 5760, 6144
LORA_W = Q_LORA + KV_LORA
KR_LANE = NOPE

ATT_T = 512
ROW_T = 512
VMEM_LIMIT = 56 * 1024 * 1024

NN = (((1,), (0,)), ((), ()))
NT = (((1,), (1,)), ((), ()))
TN = (((0,), (0,)), ((), ()))


def _params(sem=None, **kw):
    return pltpu.CompilerParams(dimension_semantics=sem, vmem_limit_bytes=VMEM_LIMIT, **kw)


def _matmul(a, b, *, mode, name, tm, tn, tk, out_dtypes=(F32,), extras=(), epilogue=None, b_shards=False, out_shards=False):
    if b_shards:
        n_sh, rows_b, cols_b = b.shape
        b_shape = (rows_b, n_sh * cols_b)
    else:
        b_shape = b.shape
    if mode == "nn":
        (m, k), (k2, n) = a.shape, b_shape
    elif mode == "nt":
        (m, k), (n, k2) = a.shape, b_shape
    else:
        (k, m), (k2, n) = a.shape, b_shape
    assert k == k2, (a.shape, b.shape, mode)
    tm, tn, tk = min(tm, m), min(tn, n), min(tk, k)
    assert m % tm == 0 and n % tn == 0 and k % tk == 0, (name, m, n, k, tm, tn, tk)
    nk = k // tk
    n_ex, n_out = len(extras), len(out_dtypes)
    dims = {"nn": NN, "nt": NT, "tn": TN}[mode]

    def body(*refs):
        a_ref, b_ref = refs[:2]
        ex_refs = refs[2:2 + n_ex]
        out_refs = refs[2 + n_ex:2 + n_ex + n_out]
        part = lax.dot_general(a_ref[...].astype(BF16), b_ref[...].astype(BF16), dims, preferred_element_type=F32)

        def finish(acc):
            outs = epilogue(acc, *[r[...] for r in ex_refs]) if epilogue is not None else (acc,)
            for r, o in zip(out_refs, outs):
                r[...] = o.astype(r.dtype)

        if nk == 1:
            finish(part)
        else:
            acc_ref = refs[-1]
            kk = pl.program_id(2)

            @pl.when(kk == 0)
            def _():
                acc_ref[...] = part

            @pl.when(kk > 0)
            def _():
                acc_ref[...] += part

            @pl.when(kk == nk - 1)
            def _():
                finish(acc_ref[...])

    a_spec = {"nn": pl.BlockSpec((tm, tk), lambda i, j, kk: (i, kk)),
              "nt": pl.BlockSpec((tm, tk), lambda i, j, kk: (i, kk)),
              "tn": pl.BlockSpec((tk, tm), lambda i, j, kk: (kk, i))}[mode]
    b_spec = {"nn": pl.BlockSpec((tk, tn), lambda i, j, kk: (kk, j)),
              "nt": pl.BlockSpec((tn, tk), lambda i, j, kk: (j, kk)),
              "tn": pl.BlockSpec((tk, tn), lambda i, j, kk: (kk, j))}[mode]
    tile = pl.BlockSpec((tm, tn), lambda i, j, kk: (i, j))
    out_spec, out_dims = tile, (m, n)
    if b_shards and mode == "nn":
        per = cols_b // tn
        b_spec = pl.BlockSpec((None, tk, tn), lambda i, j, kk: (j // per, kk, j % per))
    elif b_shards:
        assert mode == "nt"
        per = cols_b // tk
        b_spec = pl.BlockSpec((None, tn, tk), lambda i, j, kk: (kk // per, j, kk % per))
    if out_shards:
        assert not extras and epilogue is None
        per_out = n // N_CHIPS // tn
        out_spec = pl.BlockSpec((None, tm, tn), lambda i, j, kk: (j // per_out, i, j % per_out))
        out_dims = (N_CHIPS, m, n // N_CHIPS)
    outs = pl.pallas_call(
        body, name=name,
        grid=(m // tm, n // tn, nk),
        in_specs=[a_spec, b_spec] + [tile] * n_ex,
        out_specs=[out_spec] * n_out,
        out_shape=[jax.ShapeDtypeStruct(out_dims, dt) for dt in out_dtypes],
        scratch_shapes=[pltpu.VMEM((tm, tn), F32)] if nk > 1 else [],
        compiler_params=_params(("parallel", "parallel", "arbitrary")),
    )(a, b, *extras)
    return outs[0] if n_out == 1 else outs


def _rowwise(fn, *, name, rows, seq, ins, outs, sums=()):
    tm = min(ROW_T, seq)
    n_pos = seq // tm
    n_in, n_out, n_sum = len(ins), len(outs), len(sums)

    def body(*refs):
        vals = fn(*[r[...] for r in refs[:n_in]])
        for r, v in zip(refs[n_in:n_in + n_out], vals[:n_out]):
            r[...] = v.astype(r.dtype)
        first = pl.program_id(0) == 0
        for r, v in zip(refs[n_in + n_out:], vals[n_out:]):
            @pl.when(first)
            def _(r=r, v=v):
                r[...] = v

            @pl.when(jnp.logical_not(first))
            def _(r=r, v=v):
                r[...] += v

    def spec(arr, width, col, kind):
        if kind == "row":
            return pl.BlockSpec((tm, width), lambda i, col=col: (i, col))
        if kind == "pos":
            return pl.BlockSpec((tm, width), lambda i, col=col: (i % n_pos, col))
        return pl.BlockSpec(arr.shape, lambda i: (0,) * arr.ndim)

    res = pl.pallas_call(
        body, name=name,
        grid=(rows // tm,),
        in_specs=[spec(*t) for t in ins],
        out_specs=[pl.BlockSpec((tm, w), lambda i: (i, 0)) for w, _ in outs]
        + [pl.BlockSpec((1, w), lambda i: (0, 0)) for w in sums],
        out_shape=[jax.ShapeDtypeStruct((rows, w), dt) for w, dt in outs]
        + [jax.ShapeDtypeStruct((1, w), F32) for w in sums],
        compiler_params=_params(("arbitrary",)),
    )(*[t[0] for t in ins])
    return res


def _colsum(v):
    return jnp.sum(v, axis=0, keepdims=True)


def _rope_fwd(t, c, s_up, s_dn):
    return t * c + pltpu.roll(t, LANES - 16, 1) * s_up + pltpu.roll(t, 16, 1) * s_dn


def _rope_bwd(d, c, s_up, s_dn):
    return d * c + pltpu.roll(d * s_up, 16, 1) + pltpu.roll(d * s_dn, LANES - 16, 1)


def _rope_tables(seq):
    half = ROPE // 2
    inv = jnp.power(ROPE_THETA, -jnp.arange(half, dtype=F32) / half)
    ang = jnp.arange(seq, dtype=F32)[:, None] * inv[None, :]
    cos, sin = jnp.cos(ang), jnp.sin(ang)
    zeros = jnp.zeros((seq, half), F32)
    lo, hi = jnp.ones((seq, KR_LANE), F32), jnp.ones((seq, LANES - KR_LANE - ROPE), F32)
    c = jnp.concatenate([lo, cos, cos, hi], axis=1)
    c_rope_only = jnp.concatenate([0 * lo, cos, cos, 0 * hi], axis=1)
    s_up = jnp.concatenate([0 * lo, -sin, zeros, 0 * hi], axis=1)
    s_dn = jnp.concatenate([0 * lo, zeros, sin, 0 * hi], axis=1)
    return c, s_up, s_dn, c_rope_only


def _rms(x, g):
    r = lax.rsqrt(jnp.mean(x * x, axis=1, keepdims=True) + RMS_EPS)
    return x * r * g


def _rms_bwd(x, g, dy):
    r = lax.rsqrt(jnp.mean(x * x, axis=1, keepdims=True) + RMS_EPS)
    xh = x * r
    dxh = dy * g
    dx = r * (dxh - xh * jnp.mean(dxh * xh, axis=1, keepdims=True))
    return dx, _colsum(dy * xh)


def _ln_stats(x):
    mu = jnp.mean(x, axis=1, keepdims=True)
    xc = x - mu
    r = lax.rsqrt(jnp.mean(xc * xc, axis=1, keepdims=True) + LN_EPS)
    return xc * r, r


def _ln_bwd(xh, r, g, dy):
    dxh = dy * g
    dx = r * (dxh - jnp.mean(dxh, axis=1, keepdims=True) - xh * jnp.mean(dxh * xh, axis=1, keepdims=True))
    return dx, _colsum(dy * xh), _colsum(dy)


def _bias_spec(bias):
    per_head = bias.shape[0] > 1
    return pl.BlockSpec((1,) + bias.shape[1:], lambda b, h: (h if per_head else 0, 0, 0, 0))


def _attn_fwd(q, qb0, k, kb0, v, vb0, bias, scale, *, name, batch, seq):
    t = ATT_T
    nq = seq // t
    rows = batch * seq

    def body(q_ref, k_ref, v_ref, bias_ref, o_ref, lse_ref, qb, kb, vb):
        qb[...] = q_ref[...].astype(BF16)
        kb[...] = k_ref[...].astype(BF16)
        vb[...] = v_ref[...].astype(BF16)
        for i in range(nq):
            qt = qb[i * t:(i + 1) * t, :]
            logits = [lax.dot_general(qt, kb[j * t:(j + 1) * t, :], NT, preferred_element_type=F32) * scale + bias_ref[0, i - j]
                      for j in range(i + 1)]
            top = functools.reduce(jnp.maximum, logits)
            m = jnp.max(top, axis=1, keepdims=True)
            ps = [jnp.exp(s - m) for s in logits]
            l = jnp.sum(functools.reduce(jnp.add, ps), axis=1, keepdims=True)
            acc = functools.reduce(jnp.add, [lax.dot_general(p.astype(BF16), vb[j * t:(j + 1) * t, :], NN, preferred_element_type=F32)
                                             for j, p in enumerate(ps)])
            o_ref[i * t:(i + 1) * t, :] = acc / l
            lse_ref[i * t:(i + 1) * t, :] = jnp.broadcast_to(m + jnp.log(l), (t, LANES))

    slab = lambda b0: pl.BlockSpec((seq, LANES), lambda b, h, b0=b0: (b, b0 + h))
    return pl.pallas_call(
        body, name=name,
        grid=(batch, N_HEADS),
        in_specs=[slab(qb0), slab(kb0), slab(vb0), _bias_spec(bias)],
        out_specs=[slab(0), slab(0)],
        out_shape=[jax.ShapeDtypeStruct((rows, N_HEADS * LANES), F32)] * 2,
        scratch_shapes=[pltpu.VMEM((seq, LANES), BF16)] * 3,
        compiler_params=_params(("arbitrary", "arbitrary")),
    )(q, k, v, bias)


def _attn_bwd(q, qb0, k, kb0, v, vb0, o, do, lse, bias, scale, *, name, batch, seq, out_dtype):
    t = ATT_T
    nq = seq // t
    rows = batch * seq

    def body(q_ref, k_ref, v_ref, o_ref, do_ref, lse_ref, bias_ref, dq_ref, dk_ref, dv_ref, qb, kb, vb, dob, dka, dva):
        qb[...] = q_ref[...].astype(BF16)
        kb[...] = k_ref[...].astype(BF16)
        vb[...] = v_ref[...].astype(BF16)
        dob[...] = do_ref[...].astype(BF16)
        for i in range(nq):
            at = slice(i * t, (i + 1) * t)
            qt, dot = qb[at, :], dob[at, :]
            lse_t = lse_ref[at, 0:1]
            delta = jnp.sum(o_ref[at, :] * do_ref[at, :], axis=1, keepdims=True)
            dq = None
            for j in range(i + 1):
                kat = slice(j * t, (j + 1) * t)
                kt, vt = kb[kat, :], vb[kat, :]
                p = jnp.exp(lax.dot_general(qt, kt, NT, preferred_element_type=F32) * scale + bias_ref[0, i - j] - lse_t)
                dp = lax.dot_general(dot, vt, NT, preferred_element_type=F32)
                ds = (p * (dp - delta) * scale).astype(BF16)
                dk_part = lax.dot_general(ds, qt, TN, preferred_element_type=F32)
                dv_part = lax.dot_general(p.astype(BF16), dot, TN, preferred_element_type=F32)
                if i == j:
                    dka[kat, :] = dk_part
                    dva[kat, :] = dv_part
                else:
                    dka[kat, :] += dk_part
                    dva[kat, :] += dv_part
                dq_part = lax.dot_general(ds, kt, NN, preferred_element_type=F32)
                dq = dq_part if dq is None else dq + dq_part
            dq_ref[at, :] = dq.astype(dq_ref.dtype)
        dk_ref[...] = dka[...].astype(dk_ref.dtype)
        dv_ref[...] = dva[...].astype(dv_ref.dtype)

    slab = lambda b0: pl.BlockSpec((seq, LANES), lambda b, h, b0=b0: (b, b0 + h))
    return pl.pallas_call(
        body, name=name,
        grid=(batch, N_HEADS),
        in_specs=[slab(qb0), slab(kb0), slab(vb0), slab(0), slab(0), slab(0), _bias_spec(bias)],
        out_specs=[slab(0)] * 3,
        out_shape=[jax.ShapeDtypeStruct((rows, N_HEADS * LANES), out_dtype)] * 3,
        scratch_shapes=[pltpu.VMEM((seq, LANES), BF16)] * 4 + [pltpu.VMEM((seq, LANES), F32)] * 2,
        compiler_params=_params(("arbitrary", "arbitrary")),
    )(q, k, v, o, do, lse, bias)


def _tile_dist(seq):
    n = seq // ATT_T
    d = jnp.arange(n, dtype=jnp.int32)[:, None, None] * ATT_T
    return d + jnp.arange(ATT_T, dtype=jnp.int32)[None, :, None] - jnp.arange(ATT_T, dtype=jnp.int32)[None, None, :]


def _causal_bias(seq):
    return jnp.where(_tile_dist(seq) >= 0, 0.0, NEG).astype(F32)[None]


def _dilated_bias(seq):
    dist = _tile_dist(seq)
    count = jnp.zeros(dist.shape, F32)
    for window, dilation in DIL_PATTERNS:
        count += ((dist >= 0) & (dist <= window) & (dist % dilation == 0)).astype(F32)
    slopes = jnp.asarray([2.0 ** (-8.0 * (i + 1) / N_HEADS) for i in range(N_HEADS)], F32)
    alibi = -slopes[:, None, None, None] * dist.astype(F32)[None]
    return jnp.where(count[None] > 0, jnp.log(jnp.maximum(count, 1.0))[None] + alibi, NEG).astype(F32)


def _pad_heads(w, width):
    kdim, n = w.shape[0], w.shape[1] // width
    return jnp.pad(w.reshape(kdim, n, width), ((0, 0), (0, 0), (0, LANES - width))).reshape(kdim, n * LANES)


def _unpad_heads(w, width):
    kdim, n = w.shape[0], w.shape[1] // LANES
    return w.reshape(kdim, n, LANES)[:, :, :width].reshape(kdim, n * width)


def _pad_w_in(w):
    q_a_kv_a = w[:, :LORA_W]
    k_r = jnp.pad(w[:, LORA_W:LORA_W + ROPE], ((0, 0), (KR_LANE, LANES - KR_LANE - ROPE)))
    qkv_d = _pad_heads(w[:, LORA_W + ROPE:LORA_W + ROPE + 3 * N_HEADS * DIL_DIM], DIL_DIM)
    gates = w[:, LORA_W + ROPE + 3 * N_HEADS * DIL_DIM:]
    tail = jnp.zeros((w.shape[0], P_WIDTH - P_KR - LANES), w.dtype)
    return jnp.concatenate([gates, qkv_d, q_a_kv_a, k_r, tail], axis=1)


def _unpad_w_in(g):
    gates = g[:, P_GATE:P_QD]
    qkv_d = _unpad_heads(g[:, P_QD:P_LORA], DIL_DIM)
    q_a_kv_a = g[:, P_LORA:P_KR]
    k_r = g[:, P_KR + KR_LANE:P_KR + KR_LANE + ROPE]
    return jnp.concatenate([q_a_kv_a, k_r, qkv_d, gates], axis=1)


def _split_ukv(w):
    w3 = w.reshape(w.shape[0], N_HEADS, NOPE + V_DIM)
    return (_pad_heads(w3[:, :, :NOPE].reshape(w.shape[0], -1), NOPE),
            _pad_heads(w3[:, :, NOPE:].reshape(w.shape[0], -1), V_DIM))


def _merge_ukv(g_k, g_v):
    kdim = g_k.shape[0]
    k3 = _unpad_heads(g_k, NOPE).reshape(kdim, N_HEADS, NOPE)
    v3 = _unpad_heads(g_v, V_DIM).reshape(kdim, N_HEADS, V_DIM)
    return jnp.concatenate([k3, v3], axis=2).reshape(kdim, N_HEADS * (NOPE + V_DIM))


def _pad_rows(w, width):
    return _pad_heads(w.T, width).T


def _unpad_rows(g, width):
    return _unpad_heads(g.T, width).T


def _join_cols(w):
    return w.transpose(1, 0, 2).reshape(w.shape[1], N_CHIPS * w.shape[2])


def _split_cols(g):
    return g.reshape(g.shape[0], N_CHIPS, g.shape[1] // N_CHIPS).transpose(1, 0, 2)


def _local_step(x3, target3, wg, b_gate, g_q_a, g_kv_a, ln1_g, ln1_b, ln2_g, ln2_b):
    w_in, w_uq, w_ukv, w_o_mla, w_o_dil = (_join_cols(wg[n]) for n in ("w_in", "w_uq", "w_ukv", "w_o_mla", "w_o_dil"))
    w_out, w_ff1, w_ff2 = wg["w_out"].reshape(D_MODEL, D_MODEL), wg["w_ff1"], wg["w_ff2"].reshape(D_FF, D_MODEL)
    batch, seq, _ = x3.shape
    rows = batch * seq
    x = x3.reshape(rows, D_MODEL)
    target = target3.reshape(rows, D_MODEL)
    row = functools.partial(_rowwise, rows=rows, seq=seq)
    mm = _matmul

    w_in_p = _pad_w_in(w_in)
    w_uq_p = _pad_heads(w_uq, MLA_QK)
    w_uk_p, w_uv_p = _split_ukv(w_ukv)
    w_oa_p = _pad_rows(w_o_mla, V_DIM)
    w_ob_p = _pad_rows(w_o_dil, DIL_DIM)
    b0, b1 = b_gate[0:1], b_gate[1:2]
    rope_c, rope_up, rope_dn, rope_c_only = _rope_tables(seq)
    bias_mla, bias_dil = _causal_bias(seq), _dilated_bias(seq)
    scale_mla, scale_dil = MLA_QK ** -0.5, DIL_DIM ** -0.5
    qd0, kd0, vd0, lora0, kr0 = P_QD // LANES, P_KD // LANES, P_VD // LANES, P_LORA // LORA_W, P_KR // LANES

    proj = mm(x, w_in_p, mode="nn", name="proj", tm=1024, tn=1536, tk=1024)

    def prep(lora, gq, gkv):
        return _rms(lora[:, :Q_LORA], gq), _rms(lora[:, Q_LORA:], gkv)

    qn, kvn = row(prep, name="mla_rms", ins=[(proj, LORA_W, lora0, "row"), (g_q_a, 0, 0, "full"), (g_kv_a, 0, 0, "full")],
                  outs=[(Q_LORA, BF16), (KV_LORA, BF16)])
    q_lin = mm(qn, w_uq_p, mode="nn", name="q_up", tm=1024, tn=1024, tk=Q_LORA)
    k_lin = mm(kvn, w_uk_p, mode="nn", name="k_up", tm=1024, tn=1024, tk=KV_LORA)
    v_a = mm(kvn, w_uv_p, mode="nn", name="v_up", tm=1024, tn=1024, tk=KV_LORA, out_dtypes=(BF16,))

    def rope_qk(ql, kl, kr, c, up, dn):
        k_rot = _rope_fwd(kr, c, up, dn)
        qs = [_rope_fwd(ql[:, h * LANES:(h + 1) * LANES], c, up, dn) for h in range(N_HEADS)]
        ks = [kl[:, h * LANES:(h + 1) * LANES] + k_rot for h in range(N_HEADS)]
        return jnp.concatenate(qs, axis=1), jnp.concatenate(ks, axis=1)

    pos = lambda tab: (tab, LANES, 0, "pos")
    q_a, k_a = row(rope_qk, name="rope_qk",
                   ins=[(q_lin, D_MODEL, 0, "row"), (k_lin, D_MODEL, 0, "row"), (proj, LANES, kr0, "row"), pos(rope_c), pos(rope_up), pos(rope_dn)],
                   outs=[(N_HEADS * LANES, BF16), (N_HEADS * LANES, BF16)])
    o_a, lse_a = _attn_fwd(q_a, 0, k_a, 0, v_a, 0, bias_mla, scale_mla, name="mla_fwd", batch=batch, seq=seq)
    o_b, lse_b = _attn_fwd(proj, qd0, proj, kd0, proj, vd0, bias_dil, scale_dil, name="dil_fwd", batch=batch, seq=seq)
    y_a = mm(o_a, w_oa_p, mode="nn", name="o_mla", tm=1024, tn=1024, tk=1024)
    y_b = mm(o_b, w_ob_p, mode="nn", name="o_dil", tm=1024, tn=1024, tk=1024)

    def gate(t0, t1, c0, c1, ya, yb):
        return (jax.nn.sigmoid(t0 + c0) * ya + jax.nn.sigmoid(t1 + c1) * yb,)

    gate_ins = [(proj, D_MODEL, 0, "row"), (proj, D_MODEL, 1, "row"), (b0, 0, 0, "full"), (b1, 0, 0, "full")]
    (u,) = row(gate, name="gate", ins=gate_ins + [(y_a, D_MODEL, 0, "row"), (y_b, D_MODEL, 0, "row")], outs=[(D_MODEL, BF16)])
    mixed = mm(u, w_out, mode="nn", name="mix", tm=1024, tn=1024, tk=1024)

    def ln1(xv, mv, g, b):
        r1 = ALPHA * xv + mv
        xh, _ = _ln_stats(r1)
        return r1, xh * g + b

    r1, h = row(ln1, name="ln1", ins=[(x, D_MODEL, 0, "row"), (mixed, D_MODEL, 0, "row"), (ln1_g, 0, 0, "full"), (ln1_b, 0, 0, "full")],
                outs=[(D_MODEL, F32), (D_MODEL, F32)])

    def relu2(acc):
        r = jnp.maximum(acc, 0.0)
        return acc, r * r

    a_ff, z = mm(h, w_ff1, mode="nn", name="ff1", tm=1024, tn=1024, tk=1024, out_dtypes=(F32, BF16), epilogue=relu2, b_shards=True)
    f = mm(z, w_ff2, mode="nn", name="ff2", tm=1024, tn=1024, tk=1024)

    def ln2_loss(hv, fv, tv, g, b):
        xh, r = _ln_stats(ALPHA * hv + fv)
        err = xh * g + b - tv
        dy = err * (1.0 / D_MODEL)
        dr2, dg, db = _ln_bwd(xh, r, g, dy)
        loss = jnp.sum(_colsum(err * err), axis=1, keepdims=True) * (0.5 / D_MODEL)
        return dr2, jnp.broadcast_to(loss, (1, LANES)), dg, db

    dr2, loss_l, d_ln2_g, d_ln2_b = row(
        ln2_loss, name="ln2_loss",
        ins=[(h, D_MODEL, 0, "row"), (f, D_MODEL, 0, "row"), (target, D_MODEL, 0, "row"), (ln2_g, 0, 0, "full"), (ln2_b, 0, 0, "full")],
        outs=[(D_MODEL, F32)], sums=[LANES, D_MODEL, D_MODEL])

    d_w_ff2 = mm(z, dr2, mode="tn", name="d_w_ff2", tm=1024, tn=1024, tk=512)
    da = mm(dr2, w_ff2, mode="nt", name="d_ff_act", tm=1024, tn=1024, tk=1024, out_dtypes=(BF16,), extras=(a_ff,),
            epilogue=lambda acc, av: (acc * (2.0 * jnp.maximum(av, 0.0)),))
    d_w_ff1 = mm(h, da, mode="tn", name="d_w_ff1", tm=1024, tn=1024, tk=512, out_shards=True)
    dh = mm(da, w_ff1, mode="nt", name="d_h", tm=1024, tn=1024, tk=1024, extras=(dr2,), epilogue=lambda acc, rv: (acc + ALPHA * rv,), b_shards=True)

    def ln1_bwd(dhv, r1v, g):
        xh, r = _ln_stats(r1v)
        return _ln_bwd(xh, r, g, dhv)

    dr1, d_ln1_g, d_ln1_b = row(ln1_bwd, name="ln1_bwd", ins=[(dh, D_MODEL, 0, "row"), (r1, D_MODEL, 0, "row"), (ln1_g, 0, 0, "full")],
                                outs=[(D_MODEL, F32)], sums=[D_MODEL, D_MODEL])
    d_w_out = mm(u, dr1, mode="tn", name="d_w_out", tm=1024, tn=1024, tk=512)
    du = mm(dr1, w_out, mode="nt", name="d_u", tm=1024, tn=1024, tk=1024)

    def gate_bwd(t0, t1, c0, c1, ya, yb, duv):
        s0, s1 = jax.nn.sigmoid(t0 + c0), jax.nn.sigmoid(t1 + c1)
        dt0 = duv * ya * s0 * (1.0 - s0)
        dt1 = duv * yb * s1 * (1.0 - s1)
        return duv * s0, duv * s1, jnp.concatenate([dt0, dt1], axis=1), jnp.concatenate([_colsum(dt0), _colsum(dt1)], axis=1)

    dy_a, dy_b, d_gates, d_b_gate = row(
        gate_bwd, name="gate_bwd", ins=gate_ins + [(y_a, D_MODEL, 0, "row"), (y_b, D_MODEL, 0, "row"), (du, D_MODEL, 0, "row")],
        outs=[(D_MODEL, BF16), (D_MODEL, BF16), (2 * D_MODEL, BF16)], sums=[2 * D_MODEL])
    d_w_oa_p = mm(o_a, dy_a, mode="tn", name="d_w_o_mla", tm=1024, tn=1024, tk=512)
    d_w_ob_p = mm(o_b, dy_b, mode="tn", name="d_w_o_dil", tm=1024, tn=1024, tk=512)
    do_a = mm(dy_a, w_oa_p, mode="nt", name="d_o_mla", tm=1024, tn=1024, tk=1024)
    do_b = mm(dy_b, w_ob_p, mode="nt", name="d_o_dil", tm=1024, tn=1024, tk=1024)
    dq_a, dk_a, dv_a = _attn_bwd(q_a, 0, k_a, 0, v_a, 0, o_a, do_a, lse_a, bias_mla, scale_mla,
                                 name="mla_bwd", batch=batch, seq=seq, out_dtype=F32)
    dq_d, dk_d, dv_d = _attn_bwd(proj, qd0, proj, kd0, proj, vd0, o_b, do_b, lse_b, bias_dil, scale_dil,
                                 name="dil_bwd", batch=batch, seq=seq, out_dtype=BF16)

    def mla_post(dq, dk, c, up, dn, c_only):
        dqs = [_rope_bwd(dq[:, h * LANES:(h + 1) * LANES], c, up, dn) for h in range(N_HEADS)]
        dk_sum = dk[:, :LANES]
        for h in range(1, N_HEADS):
            dk_sum = dk_sum + dk[:, h * LANES:(h + 1) * LANES]
        return jnp.concatenate(dqs, axis=1), _rope_bwd(dk_sum, c_only, up, dn)

    dq_lin, d_kr = row(mla_post, name="mla_unrope",
                       ins=[(dq_a, D_MODEL, 0, "row"), (dk_a, D_MODEL, 0, "row"), pos(rope_c), pos(rope_up), pos(rope_dn), pos(rope_c_only)],
                       outs=[(N_HEADS * LANES, BF16), (LANES, BF16)])
    d_w_uq_p = mm(qn, dq_lin, mode="tn", name="d_w_uq", tm=Q_LORA, tn=1024, tk=512)
    d_w_uk_p = mm(kvn, dk_a, mode="tn", name="d_w_uk", tm=KV_LORA, tn=1024, tk=512)
    d_w_uv_p = mm(kvn, dv_a, mode="tn", name="d_w_uv", tm=KV_LORA, tn=1024, tk=512)
    d_qn = mm(dq_lin, w_uq_p, mode="nt", name="d_qn", tm=1024, tn=Q_LORA, tk=1024)
    d_kvn_k = mm(dk_a, w_uk_p, mode="nt", name="d_kvn_k", tm=1024, tn=KV_LORA, tk=1024)
    d_kvn = mm(dv_a, w_uv_p, mode="nt", name="d_kvn", tm=1024, tn=KV_LORA, tk=1024, extras=(d_kvn_k,), epilogue=lambda acc, e: (acc + e,))

    def rms_bwd(lora, dq, dkv, gq, gkv):
        dxq, dgq = _rms_bwd(lora[:, :Q_LORA], gq, dq)
        dxk, dgk = _rms_bwd(lora[:, Q_LORA:], gkv, dkv)
        return jnp.concatenate([dxq, dxk], axis=1), dgq, dgk

    d_lora, d_g_q_a, d_g_kv_a = row(
        rms_bwd, name="mla_rms_bwd",
        ins=[(proj, LORA_W, lora0, "row"), (d_qn, Q_LORA, 0, "row"), (d_kvn, KV_LORA, 0, "row"), (g_q_a, 0, 0, "full"), (g_kv_a, 0, 0, "full")],
        outs=[(LORA_W, BF16)], sums=[Q_LORA, KV_LORA])
    d_proj = jnp.concatenate([d_gates, dq_d, dk_d, dv_d, d_lora, d_kr, jnp.zeros((rows, P_WIDTH - P_KR - LANES), BF16)], axis=1)
    d_w_in_p = mm(x, d_proj, mode="tn", name="d_w_in", tm=1024, tn=1536, tk=512)
    grad_x = mm(d_proj, w_in_p, mode="nt", name="d_x", tm=1024, tn=1024, tk=1536, extras=(dr1,), epilogue=lambda acc, rv: (acc + ALPHA * rv,))

    grads = dict(
        w_in=_split_cols(_unpad_w_in(d_w_in_p)), w_uq=_split_cols(_unpad_heads(d_w_uq_p, MLA_QK)), w_ukv=_split_cols(_merge_ukv(d_w_uk_p, d_w_uv_p)),
        w_o_mla=_split_cols(_unpad_rows(d_w_oa_p, V_DIM)), w_o_dil=_split_cols(_unpad_rows(d_w_ob_p, DIL_DIM)),
        w_out=d_w_out.reshape(N_CHIPS, D_MODEL // N_CHIPS, D_MODEL), w_ff1=d_w_ff1, w_ff2=d_w_ff2.reshape(N_CHIPS, D_FF // N_CHIPS, D_MODEL),
        b_gate=d_b_gate.reshape(2, D_MODEL), g_q_a=d_g_q_a, g_kv_a=d_g_kv_a, ln1_g=d_ln1_g, ln1_b=d_ln1_b, ln2_g=d_ln2_g, ln2_b=d_ln2_b)
    return loss_l[0, 0], grad_x.reshape(batch, seq, D_MODEL), grads


BIG = (("w_in", (1024, 1064)), ("w_uq", (384, 192)), ("w_ukv", (256, 256)), ("w_o_mla", (512, 256)),
       ("w_o_dil", (512, 256)), ("w_out", (256, 1024)), ("w_ff1", (1024, 1024)), ("w_ff2", (1024, 1024)))
SMALL = (("b_gate", 2 * D_MODEL), ("g_q_a", Q_LORA), ("g_kv_a", KV_LORA), ("ln1_g", D_MODEL), ("ln1_b", D_MODEL),
         ("ln2_g", D_MODEL), ("ln2_b", D_MODEL))
N_BIG = len(BIG)
D2D_CHUNKS = 4
ANY = pl.BlockSpec(memory_space=pl.ANY)
SIDE_EFFECTS = pltpu.CompilerParams(has_side_effects=True)


def _place():
    x, y, c = lax.axis_index("x"), lax.axis_index("y"), lax.axis_index("c")
    return x, y, c, ((1 - x, y), (x, 1 - y), (1 - x, 1 - y))


def _rows(at, n):
    return pl.ds(pl.multiple_of(at, 16), n)


def _row_chunks(at, n):
    size = n // D2D_CHUNKS
    return [_rows(at + i * size, size) for i in range(D2D_CHUNKS)]


def _remote(src, dst, send, recv, to):
    return pltpu.make_async_remote_copy(src_ref=src, dst_ref=dst, send_sem=send, recv_sem=recv, device_id=to, device_id_type=MESH)


def _gather_weights(shards):
    def body(*refs):
        srcs, outs, (send, recv) = refs[:N_BIG], refs[N_BIG:2 * N_BIG], refs[2 * N_BIG:]
        x, y, c, chips = _place()
        sibling = (x, y, 1 - c)
        halves = [s.shape[0] // 2 for s in shards]

        def over_ici(t, j, cx, cy, to):
            r = _rows(c * halves[t], halves[t])
            return _remote(srcs[t].at[r], outs[t].at[2 * cx + cy, r], send.at[t, j], recv.at[t, j], (*to, c))

        def over_d2d(t, j, slot, r):
            src = srcs[t].at[r] if j == 6 else outs[t].at[slot, r]
            return _remote(src, outs[t].at[slot, r], send.at[t, j], recv.at[t, j], sibling)

        for t in range(N_BIG):
            for j, chip in enumerate(chips):
                over_ici(t, j, x, y, chip).start()
        for t in range(N_BIG):
            for r in _row_chunks(0, 2 * halves[t]):
                over_d2d(t, 6, 2 * x + y, r).start()
        for t in range(N_BIG):
            for j, (cx, cy) in enumerate(chips):
                over_ici(t, j, cx, cy, (cx, cy)).wait_recv()
                for r in _row_chunks(c * halves[t], halves[t]):
                    over_d2d(t, 3 + j, 2 * cx + cy, r).start()
        for t in range(N_BIG):
            over_d2d(t, 6, 2 * x + y, _rows(0, 2 * halves[t])).wait()
            for j, (cx, cy) in enumerate(chips):
                over_d2d(t, 3 + j, 2 * cx + cy, _rows((1 - c) * halves[t], halves[t])).wait_recv()
                over_d2d(t, 3 + j, 2 * cx + cy, _rows(c * halves[t], halves[t])).wait_send()
                over_ici(t, j, x, y, (cx, cy)).wait_send()

    return pl.pallas_call(
        body, name="gather_weights", in_specs=[ANY] * N_BIG, out_specs=[ANY] * N_BIG,
        out_shape=[jax.ShapeDtypeStruct((N_CHIPS,) + s.shape, s.dtype) for s in shards],
        scratch_shapes=[pltpu.SemaphoreType.DMA((N_BIG, 7)), pltpu.SemaphoreType.DMA((N_BIG, 7))],
        compiler_params=SIDE_EFFECTS,
    )(*shards)


def _pair_split(grads):
    def body(*refs):
        srcs, outs, (send, recv) = refs[:N_BIG], refs[N_BIG:2 * N_BIG], refs[2 * N_BIG:]
        x, y, c, _ = _place()
        for t in range(N_BIG):
            half = grads[t].shape[1] // 2
            give = _rows((1 - c) * half, half)
            for s in range(N_CHIPS):
                _remote(srcs[t].at[s, give], outs[t].at[s], send.at[t], recv.at[t], (x, y, 1 - c)).start()
        for t in range(N_BIG):
            half = grads[t].shape[1] // 2
            _remote(srcs[t].at[:, _rows((1 - c) * half, half)], outs[t], send.at[t], recv.at[t], (x, y, 1 - c)).wait()

    return pl.pallas_call(
        body, name="pair_split", in_specs=[ANY] * N_BIG, out_specs=[ANY] * N_BIG,
        out_shape=[jax.ShapeDtypeStruct((N_CHIPS, g.shape[1] // 2, g.shape[2]), g.dtype) for g in grads],
        scratch_shapes=[pltpu.SemaphoreType.DMA((N_BIG,)), pltpu.SemaphoreType.DMA((N_BIG,))],
        compiler_params=SIDE_EFFECTS,
    )(*grads)


def _chip_exchange(parts):
    def body(*refs):
        srcs, outs, (send, recv) = refs[:N_BIG], refs[N_BIG:2 * N_BIG], refs[2 * N_BIG:]
        x, y, c, chips = _place()
        copies = [_remote(srcs[t].at[2 * cx + cy], outs[t].at[j], send.at[t, j], recv.at[t, j], (cx, cy, c))
                  for t in range(N_BIG) for j, (cx, cy) in enumerate(chips)]
        for cp in copies:
            cp.start()
        for cp in copies:
            cp.wait()

    return pl.pallas_call(
        body, name="chip_exchange", in_specs=[ANY] * N_BIG, out_specs=[ANY] * N_BIG,
        out_shape=[jax.ShapeDtypeStruct((3,) + p.shape[1:], p.dtype) for p in parts],
        scratch_shapes=[pltpu.SemaphoreType.DMA((N_BIG, 3)), pltpu.SemaphoreType.DMA((N_BIG, 3))],
        compiler_params=SIDE_EFFECTS,
    )(*parts)


def _pair_join(totals):
    def body(*refs):
        srcs, outs, (send, recv) = refs[:N_BIG], refs[N_BIG:2 * N_BIG], refs[2 * N_BIG:]
        x, y, c, _ = _place()
        for t in range(N_BIG):
            for r in _row_chunks(0, totals[t].shape[0]):
                _remote(srcs[t].at[r], outs[t].at[r], send.at[t], recv.at[t], (x, y, 1 - c)).start()
        for t in range(N_BIG):
            _remote(srcs[t], outs[t], send.at[t], recv.at[t], (x, y, 1 - c)).wait()

    return pl.pallas_call(
        body, name="pair_join", in_specs=[ANY] * N_BIG, out_specs=[ANY] * N_BIG,
        out_shape=[jax.ShapeDtypeStruct(t.shape, t.dtype) for t in totals],
        scratch_shapes=[pltpu.SemaphoreType.DMA((N_BIG,)), pltpu.SemaphoreType.DMA((N_BIG,))],
        compiler_params=SIDE_EFFECTS,
    )(*totals)


def _sum_all_devices(vec, name):
    n_rows = vec.shape[0]

    def body(v_ref, out_ref, buf, send, recv):
        x, y, c, _ = _place()
        me = 4 * x + 2 * y + c
        buf[me] = v_ref[...]
        flips = [(a, b, d) for a in (0, 1) for b in (0, 1) for d in (0, 1)][1:]
        copies = []
        for r, (a, b, d) in enumerate(flips):
            px, py, pc = (1 - x if a else x), (1 - y if b else y), (1 - c if d else c)
            copies.append(pltpu.make_async_remote_copy(src_ref=v_ref, dst_ref=buf.at[me], send_sem=send.at[r], recv_sem=recv.at[r],
                                                       device_id=(px, py, pc), device_id_type=MESH))
            copies[-1].start()
        for r, (a, b, d) in enumerate(flips):
            px, py, pc = (1 - x if a else x), (1 - y if b else y), (1 - c if d else c)
            pltpu.make_async_remote_copy(src_ref=v_ref, dst_ref=buf.at[4 * px + 2 * py + pc], send_sem=send.at[r], recv_sem=recv.at[r],
                                         device_id=(px, py, pc), device_id_type=MESH).wait_recv()
        for cp in copies:
            cp.wait_send()
        total = buf[0]
        for k in range(1, N_DEV):
            total = total + buf[k]
        out_ref[...] = total

    vmem = pl.BlockSpec(memory_space=pltpu.VMEM)
    return pl.pallas_call(
        body, name=name, in_specs=[vmem], out_specs=vmem, out_shape=jax.ShapeDtypeStruct(vec.shape, F32),
        scratch_shapes=[pltpu.VMEM((N_DEV, n_rows, LANES), F32), pltpu.SemaphoreType.DMA((N_DEV - 1,)), pltpu.SemaphoreType.DMA((N_DEV - 1,))],
        compiler_params=pltpu.CompilerParams(has_side_effects=True),
    )(vec)


def _half_tile(half, width):
    t = half
    while t * width * 4 > (2 << 20) and t % 32 == 0:
        t //= 2
    return t


def _pair_add(g, theirs, core, name):
    _, half, width = theirs.shape
    t = _half_tile(half, width)
    n = half // t

    def body(c_ref, a_ref, b_ref, o_ref):
        o_ref[...] = (a_ref[...] + b_ref[...]).astype(BF16)

    tile = pl.BlockSpec((1, t, width), lambda j, i, c_ref: (j, i, 0))
    return pl.pallas_call(
        body, name=name,
        grid_spec=pltpu.PrefetchScalarGridSpec(
            num_scalar_prefetch=1, grid=(N_CHIPS, n),
            in_specs=[pl.BlockSpec((1, t, width), lambda j, i, c_ref: (j, c_ref[0] * n + i, 0)), tile], out_specs=tile),
        out_shape=jax.ShapeDtypeStruct(theirs.shape, BF16), compiler_params=_params(("parallel", "parallel")),
    )(core, g, theirs)


def _chip_sum(part, others, chip, name):
    _, half, width = part.shape
    t = _half_tile(half, width)

    def body(s_ref, mine, p0, p1, p2, o_ref):
        o_ref[...] = ((mine[0].astype(F32) + p0[0].astype(F32)) + p1[0].astype(F32)) + p2[0].astype(F32)

    return pl.pallas_call(
        body, name=name,
        grid_spec=pltpu.PrefetchScalarGridSpec(
            num_scalar_prefetch=1, grid=(half // t,),
            in_specs=[pl.BlockSpec((1, t, width), lambda i, s_ref: (s_ref[0], i, 0))]
            + [pl.BlockSpec((1, t, width), lambda i, s_ref, j=j: (j, i, 0)) for j in range(3)],
            out_specs=pl.BlockSpec((t, width), lambda i, s_ref: (i, 0))),
        out_shape=jax.ShapeDtypeStruct((half, width), F32), compiler_params=_params(("parallel",)),
    )(chip, part, others, others, others)


def _reduce_scatter(grads, core, chip):
    names = [n for n, _ in BIG]
    gs = [grads[n] for n in names]
    theirs = _pair_split(gs)
    parts = [_pair_add(g, th, core, "pair_add_" + n) for g, th, n in zip(gs, theirs, names)]
    others = _chip_exchange(parts)
    totals = [_chip_sum(p, o, chip, "chip_sum_" + n) for p, o, n in zip(parts, others, names)]
    halves = _pair_join(totals)
    south = core[0] == 0
    return {n: jnp.where(south, jnp.concatenate([t, h], axis=0), jnp.concatenate([h, t], axis=0)) for n, t, h in zip(names, totals, halves)}


def _adamw(w, g, m, v, name):
    rows, width = w.shape
    t = rows
    for cand in (256, 128, 64, 32, 16, 8):
        if rows % cand == 0:
            t = cand
            break

    def body(w_ref, g_ref, m_ref, v_ref, d_ref, nm_ref, nv_ref):
        gv = g_ref[...]
        nm = B1 * m_ref[...] + (1.0 - B1) * gv
        nv = B2 * v_ref[...] + (1.0 - B2) * (gv * gv)
        m_hat = nm / (1.0 - B1 ** ADAM_STEP)
        v_hat = nv / (1.0 - B2 ** ADAM_STEP)
        d_ref[...] = -LR * (m_hat / (jnp.sqrt(v_hat) + ADAM_EPS) + WD * w_ref[...])
        nm_ref[...] = nm
        nv_ref[...] = nv

    spec = pl.BlockSpec((t, width), lambda i: (i, 0))
    return pl.pallas_call(body, name=name, grid=(rows // t,), in_specs=[spec] * 4, out_specs=[spec] * 3,
                          out_shape=[jax.ShapeDtypeStruct(w.shape, F32)] * 3, compiler_params=_params(("parallel",)))(w, g, m, v)


def _rows8(a):
    a = a.reshape(-1, LANES)
    return jnp.pad(a, ((0, -a.shape[0] % 8), (0, 0)))


def _pack_small(vals):
    return jnp.concatenate([_rows8(vals[name]) for name, _ in SMALL], axis=0)


def _unpack_small(packed, shapes):
    out, r0 = {}, 0
    for name, _ in SMALL:
        n = math.prod(shapes[name])
        out[name] = packed[r0:r0 + n // LANES].reshape(shapes[name])
        r0 += -(-n // (8 * LANES)) * 8
    return out


def kernel(x, w_in, b_gate, g_q_a, w_uq, g_kv_a, w_ukv, w_o_mla, w_o_dil, w_out, ln1_g, ln1_b, w_ff1, w_ff2, ln2_g, ln2_b, loss_target, m_w_in, m_b_gate, m_g_q_a, m_w_uq, m_g_kv_a, m_w_ukv, m_w_o_mla, m_w_o_dil, m_w_out, m_ln1_g, m_ln1_b, m_w_ff1, m_w_ff2, m_ln2_g, m_ln2_b, v_w_in, v_b_gate, v_g_q_a, v_w_uq, v_g_kv_a, v_w_ukv, v_w_o_mla, v_w_o_dil, v_w_out, v_ln1_g, v_ln1_b, v_w_ff1, v_w_ff2, v_ln2_g, v_ln2_b):
    order = ("w_in", "b_gate", "g_q_a", "w_uq", "g_kv_a", "w_ukv", "w_o_mla", "w_o_dil", "w_out", "ln1_g", "ln1_b", "w_ff1", "w_ff2", "ln2_g", "ln2_b")
    w = dict(w_in=w_in, b_gate=b_gate, g_q_a=g_q_a, w_uq=w_uq, g_kv_a=g_kv_a, w_ukv=w_ukv, w_o_mla=w_o_mla, w_o_dil=w_o_dil, w_out=w_out,
             ln1_g=ln1_g, ln1_b=ln1_b, w_ff1=w_ff1, w_ff2=w_ff2, ln2_g=ln2_g, ln2_b=ln2_b)
    m = dict(w_in=m_w_in, b_gate=m_b_gate, g_q_a=m_g_q_a, w_uq=m_w_uq, g_kv_a=m_g_kv_a, w_ukv=m_w_ukv, w_o_mla=m_w_o_mla, w_o_dil=m_w_o_dil,
             w_out=m_w_out, ln1_g=m_ln1_g, ln1_b=m_ln1_b, w_ff1=m_w_ff1, w_ff2=m_w_ff2, ln2_g=m_ln2_g, ln2_b=m_ln2_b)
    v = dict(w_in=v_w_in, b_gate=v_b_gate, g_q_a=v_g_q_a, w_uq=v_w_uq, g_kv_a=v_g_kv_a, w_ukv=v_w_ukv, w_o_mla=v_w_o_mla, w_o_dil=v_w_o_dil,
             w_out=v_w_out, ln1_g=v_ln1_g, ln1_b=v_ln1_b, w_ff1=v_w_ff1, w_ff2=v_w_ff2, ln2_g=v_ln2_g, ln2_b=v_ln2_b)
    chip = 2 * lax.axis_index("x") + lax.axis_index("y")
    south = (lax.axis_index("c") == 0).astype(F32)
    gate_w = D_MODEL // N_CHIPS

    core = lax.axis_index("c").astype(jnp.int32).reshape(1)
    gathered = _gather_weights([w[n][0].astype(BF16) for n, _ in BIG])
    b_mine = lax.dynamic_update_slice(jnp.zeros((2, D_MODEL), F32), b_gate[0] * south, (0, chip * gate_w))
    b_full = _sum_all_devices(b_mine.reshape(-1, LANES), "gather_b_gate").reshape(2, D_MODEL)

    loss_part, grad_x, grads = _local_step(x, loss_target, {n: g for (n, _), g in zip(BIG, gathered)}, b_full, g_q_a, g_kv_a, ln1_g, ln1_b, ln2_g, ln2_b)

    g_out = _reduce_scatter(grads, core, chip.astype(jnp.int32).reshape(1))
    small_in = jnp.concatenate([_pack_small(grads), jnp.broadcast_to(loss_part, (8, LANES))], axis=0)
    small = _sum_all_devices(small_in, "sum_small")
    loss = small[small_in.shape[0] - 8, 0]
    g_out.update(_unpack_small(small, {name: (1, n) for name, n in SMALL}))
    g_out["b_gate"] = lax.dynamic_slice(g_out["b_gate"].reshape(2, D_MODEL), (0, chip * gate_w), (2, gate_w))

    delta, new_m, new_v = {}, {}, {}
    for name, _ in BIG:
        delta[name], new_m[name], new_v[name] = _adamw(w[name][0], g_out[name], m[name][0], v[name][0], "adamw_" + name)
    small_shapes = {name: w[name].shape for name, _ in SMALL}
    packed = _adamw(_pack_small(w), _pack_small(g_out), _pack_small(m), _pack_small(v), "adamw_small")
    for d, t in zip((delta, new_m, new_v), packed):
        d.update(_unpack_small(t, small_shapes))

    lead = lambda d: [d[name].reshape(w[name].shape) for name in order]
    return (loss, grad_x, *lead(g_out), *lead(delta), *lead(new_m), *lead(new_v))
```

```python
import functools
import math

import jax
import jax.numpy as jnp
from jax import lax
from jax.experimental import pallas as pl
from jax.experimental.pallas import tpu as pltpu

F32 = jnp.float32
BF16 = jnp.bfloat16
MESH = pl.DeviceIdType.MESH

D_MODEL = 1024
N_HEADS = 8
LANES = 128
NOPE, ROPE, V_DIM = 64, 32, 64
MLA_QK = NOPE + ROPE
Q_LORA, KV_LORA = 384, 256
DIL_DIM = 64
DIL_PATTERNS = ((128, 1), (512, 4), (2048, 16))
D_FF = 4096
N_CHIPS = 4
N_DEV = 8
IN_WIDTH = 4256
LN_EPS, RMS_EPS = 1e-5, 1e-6
NEG = -1e30
ALPHA = 2.0 ** 0.25
ROPE_THETA = 10000.0
LR, B1, B2, ADAM_EPS, WD, ADAM_STEP = 0.001, 0.9, 0.999, 1e-8, 0.01, 10

P_GATE, P_QD, P_KD, P_VD, P_LORA, P_KR, P_WIDTH = 0, 2048, 3072, 4096, 5120, 5760, 6144
LORA_W = Q_LORA + KV_LORA
KR_LANE = NOPE

ATT_T = 512
ROW_T = 512
VMEM_LIMIT = 56 * 1024 * 1024

NN = (((1,), (0,)), ((), ()))
NT = (((1,), (1,)), ((), ()))
TN = (((0,), (0,)), ((), ()))


def _params(sem=None, **kw):
    return pltpu.CompilerParams(dimension_semantics=sem, vmem_limit_bytes=VMEM_LIMIT, **kw)


def _matmul(a, b, *, mode, name, tm, tn, tk, out_dtypes=(F32,), extras=(), epilogue=None, b_shards=False, out_shards=False, after=None):
    if b_shards:
        n_sh, rows_b, cols_b = b.shape
        b_shape = (rows_b, n_sh * cols_b)
    else:
        b_shape = b.shape
    if mode == "nn":
        (m, k), (k2, n) = a.shape, b_shape
    elif mode == "nt":
        (m, k), (n, k2) = a.shape, b_shape
    else:
        (k, m), (k2, n) = a.shape, b_shape
    assert k == k2, (a.shape, b.shape, mode)
    tm, tn, tk = min(tm, m), min(tn, n), min(tk, k)
    assert m % tm == 0 and n % tn == 0 and k % tk == 0, (name, m, n, k, tm, tn, tk)
    nk = k // tk
    n_ex, n_out = len(extras), len(out_dtypes)
    n_in = 2 + n_ex + (after is not None)
    dims = {"nn": NN, "nt": NT, "tn": TN}[mode]

    def body(*refs):
        a_ref, b_ref = refs[:2]
        ex_refs = refs[2:2 + n_ex]
        out_refs = refs[n_in:n_in + n_out]
        part = lax.dot_general(a_ref[...].astype(BF16), b_ref[...].astype(BF16), dims, preferred_element_type=F32)

        def finish(acc):
            outs = epilogue(acc, *[r[...] for r in ex_refs]) if epilogue is not None else (acc,)
            for r, o in zip(out_refs, outs):
                r[...] = o.astype(r.dtype)

        if nk == 1:
            finish(part)
        else:
            acc_ref = refs[-1]
            kk = pl.program_id(2)

            @pl.when(kk == 0)
            def _():
                acc_ref[...] = part

            @pl.when(kk > 0)
            def _():
                acc_ref[...] += part

            @pl.when(kk == nk - 1)
            def _():
                finish(acc_ref[...])

    a_spec = {"nn": pl.BlockSpec((tm, tk), lambda i, j, kk: (i, kk)),
              "nt": pl.BlockSpec((tm, tk), lambda i, j, kk: (i, kk)),
              "tn": pl.BlockSpec((tk, tm), lambda i, j, kk: (kk, i))}[mode]
    b_spec = {"nn": pl.BlockSpec((tk, tn), lambda i, j, kk: (kk, j)),
              "nt": pl.BlockSpec((tn, tk), lambda i, j, kk: (j, kk)),
              "tn": pl.BlockSpec((tk, tn), lambda i, j, kk: (kk, j))}[mode]
    tile = pl.BlockSpec((tm, tn), lambda i, j, kk: (i, j))
    out_spec, out_dims = tile, (m, n)
    if b_shards and mode == "nn":
        per = cols_b // tn
        b_spec = pl.BlockSpec((None, tk, tn), lambda i, j, kk: (j // per, kk, j % per))
    elif b_shards:
        assert mode == "nt"
        per = cols_b // tk
        b_spec = pl.BlockSpec((None, tn, tk), lambda i, j, kk: (kk // per, j, kk % per))
    if out_shards:
        assert not extras and epilogue is None
        per_out = n // N_CHIPS // tn
        out_spec = pl.BlockSpec((None, tm, tn), lambda i, j, kk: (j // per_out, i, j % per_out))
        out_dims = (N_CHIPS, m, n // N_CHIPS)
    outs = pl.pallas_call(
        body, name=name,
        grid=(m // tm, n // tn, nk),
        in_specs=[a_spec, b_spec] + [tile] * n_ex + [pl.BlockSpec(memory_space=pl.ANY)] * (after is not None),
        out_specs=[out_spec] * n_out,
        out_shape=[jax.ShapeDtypeStruct(out_dims, dt) for dt in out_dtypes],
        scratch_shapes=[pltpu.VMEM((tm, tn), F32)] if nk > 1 else [],
        compiler_params=_params(("parallel", "parallel", "arbitrary")),
    )(a, b, *extras, *([after] if after is not None else []))
    return outs[0] if n_out == 1 else outs


def _rowwise(fn, *, name, rows, seq, ins, outs, sums=()):
    tm = min(ROW_T, seq)
    n_pos = seq // tm
    n_in, n_out, n_sum = len(ins), len(outs), len(sums)

    def body(*refs):
        vals = fn(*[r[...] for r in refs[:n_in]])
        for r, v in zip(refs[n_in:n_in + n_out], vals[:n_out]):
            r[...] = v.astype(r.dtype)
        first = pl.program_id(0) == 0
        for r, v in zip(refs[n_in + n_out:], vals[n_out:]):
            @pl.when(first)
            def _(r=r, v=v):
                r[...] = v

            @pl.when(jnp.logical_not(first))
            def _(r=r, v=v):
                r[...] += v

    def spec(arr, width, col, kind):
        if kind == "row":
            return pl.BlockSpec((tm, width), lambda i, col=col: (i, col))
        if kind == "pos":
            return pl.BlockSpec((tm, width), lambda i, col=col: (i % n_pos, col))
        return pl.BlockSpec(arr.shape, lambda i: (0,) * arr.ndim)

    res = pl.pallas_call(
        body, name=name,
        grid=(rows // tm,),
        in_specs=[spec(*t) for t in ins],
        out_specs=[pl.BlockSpec((tm, w), lambda i: (i, 0)) for w, _ in outs]
        + [pl.BlockSpec((1, w), lambda i: (0, 0)) for w in sums],
        out_shape=[jax.ShapeDtypeStruct((rows, w), dt) for w, dt in outs]
        + [jax.ShapeDtypeStruct((1, w), F32) for w in sums],
        compiler_params=_params(("arbitrary",)),
    )(*[t[0] for t in ins])
    return res


def _colsum(v):
    return jnp.sum(v, axis=0, keepdims=True)


def _rope_fwd(t, c, s_up, s_dn):
    return t * c + pltpu.roll(t, LANES - 16, 1) * s_up + pltpu.roll(t, 16, 1) * s_dn


def _rope_bwd(d, c, s_up, s_dn):
    return d * c + pltpu.roll(d * s_up, 16, 1) + pltpu.roll(d * s_dn, LANES - 16, 1)


def _rope_tables(seq):
    half = ROPE // 2
    inv = jnp.power(ROPE_THETA, -jnp.arange(half, dtype=F32) / half)
    ang = jnp.arange(seq, dtype=F32)[:, None] * inv[None, :]
    cos, sin = jnp.cos(ang), jnp.sin(ang)
    zeros = jnp.zeros((seq, half), F32)
    lo, hi = jnp.ones((seq, KR_LANE), F32), jnp.ones((seq, LANES - KR_LANE - ROPE), F32)
    c = jnp.concatenate([lo, cos, cos, hi], axis=1)
    c_rope_only = jnp.concatenate([0 * lo, cos, cos, 0 * hi], axis=1)
    s_up = jnp.concatenate([0 * lo, -sin, zeros, 0 * hi], axis=1)
    s_dn = jnp.concatenate([0 * lo, zeros, sin, 0 * hi], axis=1)
    return c, s_up, s_dn, c_rope_only


def _rms(x, g):
    r = lax.rsqrt(jnp.mean(x * x, axis=1, keepdims=True) + RMS_EPS)
    return x * r * g


def _rms_bwd(x, g, dy):
    r = lax.rsqrt(jnp.mean(x * x, axis=1, keepdims=True) + RMS_EPS)
    xh = x * r
    dxh = dy * g
    dx = r * (dxh - xh * jnp.mean(dxh * xh, axis=1, keepdims=True))
    return dx, _colsum(dy * xh)


def _ln_stats(x):
    mu = jnp.mean(x, axis=1, keepdims=True)
    xc = x - mu
    r = lax.rsqrt(jnp.mean(xc * xc, axis=1, keepdims=True) + LN_EPS)
    return xc * r, r


def _ln_bwd(xh, r, g, dy):
    dxh = dy * g
    dx = r * (dxh - jnp.mean(dxh, axis=1, keepdims=True) - xh * jnp.mean(dxh * xh, axis=1, keepdims=True))
    return dx, _colsum(dy * xh), _colsum(dy)


def _bias_spec(bias):
    per_head = bias.shape[0] > 1
    return pl.BlockSpec((1,) + bias.shape[1:], lambda b, h: (h if per_head else 0, 0, 0, 0))


def _attn_fwd(q, qb0, k, kb0, v, vb0, bias, scale, *, name, batch, seq):
    t = ATT_T
    nq = seq // t
    rows = batch * seq

    def body(q_ref, k_ref, v_ref, bias_ref, o_ref, lse_ref, qb, kb, vb):
        qb[...] = q_ref[...].astype(BF16)
        kb[...] = k_ref[...].astype(BF16)
        vb[...] = v_ref[...].astype(BF16)
        for i in range(nq):
            qt = qb[i * t:(i + 1) * t, :]
            logits = [lax.dot_general(qt, kb[j * t:(j + 1) * t, :], NT, preferred_element_type=F32) * scale + bias_ref[0, i - j]
                      for j in range(i + 1)]
            top = functools.reduce(jnp.maximum, logits)
            m = jnp.max(top, axis=1, keepdims=True)
            ps = [jnp.exp(s - m) for s in logits]
            l = jnp.sum(functools.reduce(jnp.add, ps), axis=1, keepdims=True)
            acc = functools.reduce(jnp.add, [lax.dot_general(p.astype(BF16), vb[j * t:(j + 1) * t, :], NN, preferred_element_type=F32)
                                             for j, p in enumerate(ps)])
            o_ref[i * t:(i + 1) * t, :] = acc / l
            lse_ref[i * t:(i + 1) * t, :] = jnp.broadcast_to(m + jnp.log(l), (t, LANES))

    slab = lambda b0: pl.BlockSpec((seq, LANES), lambda b, h, b0=b0: (b, b0 + h))
    return pl.pallas_call(
        body, name=name,
        grid=(batch, N_HEADS),
        in_specs=[slab(qb0), slab(kb0), slab(vb0), _bias_spec(bias)],
        out_specs=[slab(0), slab(0)],
        out_shape=[jax.ShapeDtypeStruct((rows, N_HEADS * LANES), F32)] * 2,
        scratch_shapes=[pltpu.VMEM((seq, LANES), BF16)] * 3,
        compiler_params=_params(("arbitrary", "arbitrary")),
    )(q, k, v, bias)


def _attn_bwd(q, qb0, k, kb0, v, vb0, o, do, lse, bias, scale, *, name, batch, seq, out_dtype):
    t = ATT_T
    nq = seq // t
    rows = batch * seq

    def body(q_ref, k_ref, v_ref, o_ref, do_ref, lse_ref, bias_ref, dq_ref, dk_ref, dv_ref, qb, kb, vb, dob, dka, dva):
        qb[...] = q_ref[...].astype(BF16)
        kb[...] = k_ref[...].astype(BF16)
        vb[...] = v_ref[...].astype(BF16)
        dob[...] = do_ref[...].astype(BF16)
        for i in range(nq):
            at = slice(i * t, (i + 1) * t)
            qt, dot = qb[at, :], dob[at, :]
            lse_t = lse_ref[at, 0:1]
            delta = jnp.sum(o_ref[at, :] * do_ref[at, :], axis=1, keepdims=True)
            dq = None
            for j in range(i + 1):
                kat = slice(j * t, (j + 1) * t)
                kt, vt = kb[kat, :], vb[kat, :]
                p = jnp.exp(lax.dot_general(qt, kt, NT, preferred_element_type=F32) * scale + bias_ref[0, i - j] - lse_t)
                dp = lax.dot_general(dot, vt, NT, preferred_element_type=F32)
                ds = (p * (dp - delta) * scale).astype(BF16)
                dk_part = lax.dot_general(ds, qt, TN, preferred_element_type=F32)
                dv_part = lax.dot_general(p.astype(BF16), dot, TN, preferred_element_type=F32)
                if i == j:
                    dka[kat, :] = dk_part
                    dva[kat, :] = dv_part
                else:
                    dka[kat, :] += dk_part
                    dva[kat, :] += dv_part
                dq_part = lax.dot_general(ds, kt, NN, preferred_element_type=F32)
                dq = dq_part if dq is None else dq + dq_part
            dq_ref[at, :] = dq.astype(dq_ref.dtype)
        dk_ref[...] = dka[...].astype(dk_ref.dtype)
        dv_ref[...] = dva[...].astype(dv_ref.dtype)

    slab = lambda b0: pl.BlockSpec((seq, LANES), lambda b, h, b0=b0: (b, b0 + h))
    return pl.pallas_call(
        body, name=name,
        grid=(batch, N_HEADS),
        in_specs=[slab(qb0), slab(kb0), slab(vb0), slab(0), slab(0), slab(0), _bias_spec(bias)],
        out_specs=[slab(0)] * 3,
        out_shape=[jax.ShapeDtypeStruct((rows, N_HEADS * LANES), out_dtype)] * 3,
        scratch_shapes=[pltpu.VMEM((seq, LANES), BF16)] * 4 + [pltpu.VMEM((seq, LANES), F32)] * 2,
        compiler_params=_params(("arbitrary", "arbitrary")),
    )(q, k, v, o, do, lse, bias)


def _tile_dist(seq):
    n = seq // ATT_T
    d = jnp.arange(n, dtype=jnp.int32)[:, None, None] * ATT_T
    return d + jnp.arange(ATT_T, dtype=jnp.int32)[None, :, None] - jnp.arange(ATT_T, dtype=jnp.int32)[None, None, :]


def _causal_bias(seq):
    return jnp.where(_tile_dist(seq) >= 0, 0.0, NEG).astype(F32)[None]


def _dilated_bias(seq):
    dist = _tile_dist(seq)
    count = jnp.zeros(dist.shape, F32)
    for window, dilation in DIL_PATTERNS:
        count += ((dist >= 0) & (dist <= window) & (dist % dilation == 0)).astype(F32)
    slopes = jnp.asarray([2.0 ** (-8.0 * (i + 1) / N_HEADS) for i in range(N_HEADS)], F32)
    alibi = -slopes[:, None, None, None] * dist.astype(F32)[None]
    return jnp.where(count[None] > 0, jnp.log(jnp.maximum(count, 1.0))[None] + alibi, NEG).astype(F32)


def _pad_heads(w, width):
    kdim, n = w.shape[0], w.shape[1] // width
    return jnp.pad(w.reshape(kdim, n, width), ((0, 0), (0, 0), (0, LANES - width))).reshape(kdim, n * LANES)


def _unpad_heads(w, width):
    kdim, n = w.shape[0], w.shape[1] // LANES
    return w.reshape(kdim, n, LANES)[:, :, :width].reshape(kdim, n * width)


def _pad_w_in(w):
    q_a_kv_a = w[:, :LORA_W]
    k_r = jnp.pad(w[:, LORA_W:LORA_W + ROPE], ((0, 0), (KR_LANE, LANES - KR_LANE - ROPE)))
    qkv_d = _pad_heads(w[:, LORA_W + ROPE:LORA_W + ROPE + 3 * N_HEADS * DIL_DIM], DIL_DIM)
    gates = w[:, LORA_W + ROPE + 3 * N_HEADS * DIL_DIM:]
    tail = jnp.zeros((w.shape[0], P_WIDTH - P_KR - LANES), w.dtype)
    return jnp.concatenate([gates, qkv_d, q_a_kv_a, k_r, tail], axis=1)


def _unpad_w_in(g):
    gates = g[:, P_GATE:P_QD]
    qkv_d = _unpad_heads(g[:, P_QD:P_LORA], DIL_DIM)
    q_a_kv_a = g[:, P_LORA:P_KR]
    k_r = g[:, P_KR + KR_LANE:P_KR + KR_LANE + ROPE]
    return jnp.concatenate([q_a_kv_a, k_r, qkv_d, gates], axis=1)


def _split_ukv(w):
    w3 = w.reshape(w.shape[0], N_HEADS, NOPE + V_DIM)
    return (_pad_heads(w3[:, :, :NOPE].reshape(w.shape[0], -1), NOPE),
            _pad_heads(w3[:, :, NOPE:].reshape(w.shape[0], -1), V_DIM))


def _merge_ukv(g_k, g_v):
    kdim = g_k.shape[0]
    k3 = _unpad_heads(g_k, NOPE).reshape(kdim, N_HEADS, NOPE)
    v3 = _unpad_heads(g_v, V_DIM).reshape(kdim, N_HEADS, V_DIM)
    return jnp.concatenate([k3, v3], axis=2).reshape(kdim, N_HEADS * (NOPE + V_DIM))


def _pad_rows(w, width):
    return _pad_heads(w.T, width).T


def _unpad_rows(g, width):
    return _unpad_heads(g.T, width).T


def _join_cols(w):
    return w.transpose(1, 0, 2).reshape(w.shape[1], N_CHIPS * w.shape[2])


def _split_cols(g):
    return g.reshape(g.shape[0], N_CHIPS, g.shape[1] // N_CHIPS).transpose(1, 0, 2)


def _local_step(x3, target3, wg, b_gate, g_q_a, g_kv_a, ln1_g, ln1_b, ln2_g, ln2_b, token=None, late_weights=None, early_grads=None):
    w_in, w_uq, w_ukv = (_join_cols(wg[n]) for n in ("w_in", "w_uq", "w_ukv"))
    batch, seq, _ = x3.shape
    rows = batch * seq
    x = x3.reshape(rows, D_MODEL)
    target = target3.reshape(rows, D_MODEL)
    row = functools.partial(_rowwise, rows=rows, seq=seq)
    mm = _matmul

    w_in_p = _pad_w_in(w_in)
    w_uq_p = _pad_heads(w_uq, MLA_QK)
    w_uk_p, w_uv_p = _split_ukv(w_ukv)
    b0, b1 = b_gate[0:1], b_gate[1:2]
    rope_c, rope_up, rope_dn, rope_c_only = _rope_tables(seq)
    bias_mla, bias_dil = _causal_bias(seq), _dilated_bias(seq)
    scale_mla, scale_dil = MLA_QK ** -0.5, DIL_DIM ** -0.5
    qd0, kd0, vd0, lora0, kr0 = P_QD // LANES, P_KD // LANES, P_VD // LANES, P_LORA // LORA_W, P_KR // LANES

    proj = mm(x, w_in_p, mode="nn", name="proj", tm=1024, tn=1536, tk=1024, after=token)

    def prep(lora, gq, gkv):
        return _rms(lora[:, :Q_LORA], gq), _rms(lora[:, Q_LORA:], gkv)

    qn, kvn = row(prep, name="mla_rms", ins=[(proj, LORA_W, lora0, "row"), (g_q_a, 0, 0, "full"), (g_kv_a, 0, 0, "full")],
                  outs=[(Q_LORA, BF16), (KV_LORA, BF16)])
    q_lin = mm(qn, w_uq_p, mode="nn", name="q_up", tm=1024, tn=1024, tk=Q_LORA)
    k_lin = mm(kvn, w_uk_p, mode="nn", name="k_up", tm=1024, tn=1024, tk=KV_LORA)
    v_a = mm(kvn, w_uv_p, mode="nn", name="v_up", tm=1024, tn=1024, tk=KV_LORA, out_dtypes=(BF16,))

    def rope_qk(ql, kl, kr, c, up, dn):
        k_rot = _rope_fwd(kr, c, up, dn)
        qs = [_rope_fwd(ql[:, h * LANES:(h + 1) * LANES], c, up, dn) for h in range(N_HEADS)]
        ks = [kl[:, h * LANES:(h + 1) * LANES] + k_rot for h in range(N_HEADS)]
        return jnp.concatenate(qs, axis=1), jnp.concatenate(ks, axis=1)

    pos = lambda tab: (tab, LANES, 0, "pos")
    q_a, k_a = row(rope_qk, name="rope_qk",
                   ins=[(q_lin, D_MODEL, 0, "row"), (k_lin, D_MODEL, 0, "row"), (proj, LANES, kr0, "row"), pos(rope_c), pos(rope_up), pos(rope_dn)],
                   outs=[(N_HEADS * LANES, BF16), (N_HEADS * LANES, BF16)])
    o_a, lse_a = _attn_fwd(q_a, 0, k_a, 0, v_a, 0, bias_mla, scale_mla, name="mla_fwd", batch=batch, seq=seq)
    o_b, lse_b = _attn_fwd(proj, qd0, proj, kd0, proj, vd0, bias_dil, scale_dil, name="dil_fwd", batch=batch, seq=seq)
    late = wg if late_weights is None else late_weights(o_b)
    w_oa_p = _pad_rows(_join_cols(late["w_o_mla"]), V_DIM)
    w_ob_p = _pad_rows(_join_cols(late["w_o_dil"]), DIL_DIM)
    w_out, w_ff1, w_ff2 = late["w_out"].reshape(D_MODEL, D_MODEL), late["w_ff1"], late["w_ff2"].reshape(D_FF, D_MODEL)
    y_a = mm(o_a, w_oa_p, mode="nn", name="o_mla", tm=1024, tn=1024, tk=1024)
    y_b = mm(o_b, w_ob_p, mode="nn", name="o_dil", tm=1024, tn=1024, tk=1024)

    def gate(t0, t1, c0, c1, ya, yb):
        return (jax.nn.sigmoid(t0 + c0) * ya + jax.nn.sigmoid(t1 + c1) * yb,)

    gate_ins = [(proj, D_MODEL, 0, "row"), (proj, D_MODEL, 1, "row"), (b0, 0, 0, "full"), (b1, 0, 0, "full")]
    (u,) = row(gate, name="gate", ins=gate_ins + [(y_a, D_MODEL, 0, "row"), (y_b, D_MODEL, 0, "row")], outs=[(D_MODEL, BF16)])
    mixed = mm(u, w_out, mode="nn", name="mix", tm=1024, tn=1024, tk=1024)

    def ln1(xv, mv, g, b):
        r1 = ALPHA * xv + mv
        xh, _ = _ln_stats(r1)
        return r1, xh * g + b

    r1, h = row(ln1, name="ln1", ins=[(x, D_MODEL, 0, "row"), (mixed, D_MODEL, 0, "row"), (ln1_g, 0, 0, "full"), (ln1_b, 0, 0, "full")],
                outs=[(D_MODEL, F32), (D_MODEL, F32)])

    def relu2(acc):
        r = jnp.maximum(acc, 0.0)
        return acc, r * r

    a_ff, z = mm(h, w_ff1, mode="nn", name="ff1", tm=1024, tn=1024, tk=1024, out_dtypes=(F32, BF16), epilogue=relu2, b_shards=True)
    f = mm(z, w_ff2, mode="nn", name="ff2", tm=1024, tn=1024, tk=1024)

    def ln2_loss(hv, fv, tv, g, b):
        xh, r = _ln_stats(ALPHA * hv + fv)
        err = xh * g + b - tv
        dy = err * (1.0 / D_MODEL)
        dr2, dg, db = _ln_bwd(xh, r, g, dy)
        loss = jnp.sum(_colsum(err * err), axis=1, keepdims=True) * (0.5 / D_MODEL)
        return dr2, jnp.broadcast_to(loss, (1, LANES)), dg, db

    dr2, loss_l, d_ln2_g, d_ln2_b = row(
        ln2_loss, name="ln2_loss",
        ins=[(h, D_MODEL, 0, "row"), (f, D_MODEL, 0, "row"), (target, D_MODEL, 0, "row"), (ln2_g, 0, 0, "full"), (ln2_b, 0, 0, "full")],
        outs=[(D_MODEL, F32)], sums=[LANES, D_MODEL, D_MODEL])

    d_w_ff2 = mm(z, dr2, mode="tn", name="d_w_ff2", tm=1024, tn=1024, tk=1024)
    da = mm(dr2, w_ff2, mode="nt", name="d_ff_act", tm=1024, tn=1024, tk=1024, out_dtypes=(BF16,), extras=(a_ff,),
            epilogue=lambda acc, av: (acc * (2.0 * jnp.maximum(av, 0.0)),))
    d_w_ff1 = mm(h, da, mode="tn", name="d_w_ff1", tm=1024, tn=1024, tk=1024, out_shards=True)
    dh = mm(da, w_ff1, mode="nt", name="d_h", tm=1024, tn=1024, tk=1024, extras=(dr2,), epilogue=lambda acc, rv: (acc + ALPHA * rv,), b_shards=True)

    def ln1_bwd(dhv, r1v, g):
        xh, r = _ln_stats(r1v)
        return _ln_bwd(xh, r, g, dhv)

    dr1, d_ln1_g, d_ln1_b = row(ln1_bwd, name="ln1_bwd", ins=[(dh, D_MODEL, 0, "row"), (r1, D_MODEL, 0, "row"), (ln1_g, 0, 0, "full")],
                                outs=[(D_MODEL, F32)], sums=[D_MODEL, D_MODEL])
    d_w_out = mm(u, dr1, mode="tn", name="d_w_out", tm=1024, tn=1024, tk=1024)
    du = mm(dr1, w_out, mode="nt", name="d_u", tm=1024, tn=1024, tk=1024)

    def gate_bwd(t0, t1, c0, c1, ya, yb, duv):
        s0, s1 = jax.nn.sigmoid(t0 + c0), jax.nn.sigmoid(t1 + c1)
        dt0 = duv * ya * s0 * (1.0 - s0)
        dt1 = duv * yb * s1 * (1.0 - s1)
        return duv * s0, duv * s1, jnp.concatenate([dt0, dt1], axis=1), jnp.concatenate([_colsum(dt0), _colsum(dt1)], axis=1)

    dy_a, dy_b, d_gates, d_b_gate = row(
        gate_bwd, name="gate_bwd", ins=gate_ins + [(y_a, D_MODEL, 0, "row"), (y_b, D_MODEL, 0, "row"), (du, D_MODEL, 0, "row")],
        outs=[(D_MODEL, BF16), (D_MODEL, BF16), (2 * D_MODEL, BF16)], sums=[2 * D_MODEL])
    d_w_oa_p = mm(o_a, dy_a, mode="tn", name="d_w_o_mla", tm=1024, tn=1024, tk=1024)
    d_w_ob_p = mm(o_b, dy_b, mode="tn", name="d_w_o_dil", tm=1024, tn=1024, tk=1024)
    grads = dict(w_o_mla=_split_cols(_unpad_rows(d_w_oa_p, V_DIM)), w_o_dil=_split_cols(_unpad_rows(d_w_ob_p, DIL_DIM)),
                 w_out=d_w_out.reshape(N_CHIPS, D_MODEL // N_CHIPS, D_MODEL), w_ff1=d_w_ff1, w_ff2=d_w_ff2.reshape(N_CHIPS, D_FF // N_CHIPS, D_MODEL))
    sent = None if early_grads is None else early_grads(grads)
    do_a = mm(dy_a, w_oa_p, mode="nt", name="d_o_mla", tm=1024, tn=1024, tk=1024, after=sent)
    do_b = mm(dy_b, w_ob_p, mode="nt", name="d_o_dil", tm=1024, tn=1024, tk=1024)
    dq_a, dk_a, dv_a = _attn_bwd(q_a, 0, k_a, 0, v_a, 0, o_a, do_a, lse_a, bias_mla, scale_mla,
                                 name="mla_bwd", batch=batch, seq=seq, out_dtype=F32)
    dq_d, dk_d, dv_d = _attn_bwd(proj, qd0, proj, kd0, proj, vd0, o_b, do_b, lse_b, bias_dil, scale_dil,
                                 name="dil_bwd", batch=batch, seq=seq, out_dtype=BF16)

    def mla_post(dq, dk, c, up, dn, c_only):
        dqs = [_rope_bwd(dq[:, h * LANES:(h + 1) * LANES], c, up, dn) for h in range(N_HEADS)]
        dk_sum = dk[:, :LANES]
        for h in range(1, N_HEADS):
            dk_sum = dk_sum + dk[:, h * LANES:(h + 1) * LANES]
        return jnp.concatenate(dqs, axis=1), _rope_bwd(dk_sum, c_only, up, dn)

    dq_lin, d_kr = row(mla_post, name="mla_unrope",
                       ins=[(dq_a, D_MODEL, 0, "row"), (dk_a, D_MODEL, 0, "row"), pos(rope_c), pos(rope_up), pos(rope_dn), pos(rope_c_only)],
                       outs=[(N_HEADS * LANES, BF16), (LANES, BF16)])
    d_w_uq_p = mm(qn, dq_lin, mode="tn", name="d_w_uq", tm=Q_LORA, tn=1024, tk=1024)
    d_w_uk_p = mm(kvn, dk_a, mode="tn", name="d_w_uk", tm=KV_LORA, tn=1024, tk=1024)
    d_w_uv_p = mm(kvn, dv_a, mode="tn", name="d_w_uv", tm=KV_LORA, tn=1024, tk=1024)
    d_qn = mm(dq_lin, w_uq_p, mode="nt", name="d_qn", tm=1024, tn=Q_LORA, tk=1024)
    d_kvn_k = mm(dk_a, w_uk_p, mode="nt", name="d_kvn_k", tm=1024, tn=KV_LORA, tk=1024)
    d_kvn = mm(dv_a, w_uv_p, mode="nt", name="d_kvn", tm=1024, tn=KV_LORA, tk=1024, extras=(d_kvn_k,), epilogue=lambda acc, e: (acc + e,))

    def rms_bwd(lora, dq, dkv, gq, gkv):
        dxq, dgq = _rms_bwd(lora[:, :Q_LORA], gq, dq)
        dxk, dgk = _rms_bwd(lora[:, Q_LORA:], gkv, dkv)
        return jnp.concatenate([dxq, dxk], axis=1), dgq, dgk

    d_lora, d_g_q_a, d_g_kv_a = row(
        rms_bwd, name="mla_rms_bwd",
        ins=[(proj, LORA_W, lora0, "row"), (d_qn, Q_LORA, 0, "row"), (d_kvn, KV_LORA, 0, "row"), (g_q_a, 0, 0, "full"), (g_kv_a, 0, 0, "full")],
        outs=[(LORA_W, BF16)], sums=[Q_LORA, KV_LORA])
    d_proj = jnp.concatenate([d_gates, dq_d, dk_d, dv_d, d_lora, d_kr, jnp.zeros((rows, P_WIDTH - P_KR - LANES), BF16)], axis=1)
    d_w_in_p = mm(x, d_proj, mode="tn", name="d_w_in", tm=1024, tn=1536, tk=1024)
    grad_x = mm(d_proj, w_in_p, mode="nt", name="d_x", tm=1024, tn=1024, tk=1536, extras=(dr1,), epilogue=lambda acc, rv: (acc + ALPHA * rv,))

    grads.update(
        w_in=_split_cols(_unpad_w_in(d_w_in_p)), w_uq=_split_cols(_unpad_heads(d_w_uq_p, MLA_QK)), w_ukv=_split_cols(_merge_ukv(d_w_uk_p, d_w_uv_p)),
        b_gate=d_b_gate.reshape(2, D_MODEL), g_q_a=d_g_q_a, g_kv_a=d_g_kv_a, ln1_g=d_ln1_g, ln1_b=d_ln1_b, ln2_g=d_ln2_g, ln2_b=d_ln2_b)
    return loss_l[0, 0], grad_x.reshape(batch, seq, D_MODEL), grads


BIG = (("w_in", (1024, 1064)), ("w_uq", (384, 192)), ("w_ukv", (256, 256)), ("w_o_mla", (512, 256)),
       ("w_o_dil", (512, 256)), ("w_out", (256, 1024)), ("w_ff1", (1024, 1024)), ("w_ff2", (1024, 1024)))
SMALL = (("b_gate", 2 * D_MODEL), ("g_q_a", Q_LORA), ("g_kv_a", KV_LORA), ("ln1_g", D_MODEL), ("ln1_b", D_MODEL),
         ("ln2_g", D_MODEL), ("ln2_b", D_MODEL))
N_BIG = len(BIG)
D2D_CHUNKS = 4
ANY = pl.BlockSpec(memory_space=pl.ANY)
SIDE_EFFECTS = pltpu.CompilerParams(has_side_effects=True)


def _place():
    x, y, c = lax.axis_index("x"), lax.axis_index("y"), lax.axis_index("c")
    return x, y, c, ((1 - x, y), (x, 1 - y), (1 - x, 1 - y))


def _rows(at, n):
    return pl.ds(pl.multiple_of(at, 16), n)


def _row_chunks(at, n):
    size = n // D2D_CHUNKS
    return [_rows(at + i * size, size) for i in range(D2D_CHUNKS)]


def _remote(src, dst, send, recv, to):
    return pltpu.make_async_remote_copy(src_ref=src, dst_ref=dst, send_sem=send, recv_sem=recv, device_id=to, device_id_type=MESH)


def _gather_weights(shards, name):
    n = len(shards)

    def body(*refs):
        srcs, outs, (send, recv) = refs[:n], refs[n:2 * n], refs[2 * n:]
        x, y, c, chips = _place()
        sibling = (x, y, 1 - c)
        halves = [s.shape[0] // 2 for s in shards]

        def over_ici(t, j, cx, cy, to):
            r = _rows(c * halves[t], halves[t])
            return _remote(srcs[t].at[r], outs[t].at[2 * cx + cy, r], send.at[t, j], recv.at[t, j], (*to, c))

        def over_d2d(t, j, slot, r):
            src = srcs[t].at[r] if j == 6 else outs[t].at[slot, r]
            return _remote(src, outs[t].at[slot, r], send.at[t, j], recv.at[t, j], sibling)

        for t in range(n):
            for j, chip in enumerate(chips):
                over_ici(t, j, x, y, chip).start()
        for t in range(n):
            for r in _row_chunks(0, 2 * halves[t]):
                over_d2d(t, 6, 2 * x + y, r).start()
        for t in range(n):
            for j, (cx, cy) in enumerate(chips):
                over_ici(t, j, cx, cy, (cx, cy)).wait_recv()
                for r in _row_chunks(c * halves[t], halves[t]):
                    over_d2d(t, 3 + j, 2 * cx + cy, r).start()
        for t in range(n):
            over_d2d(t, 6, 2 * x + y, _rows(0, 2 * halves[t])).wait()
            for j, (cx, cy) in enumerate(chips):
                over_d2d(t, 3 + j, 2 * cx + cy, _rows((1 - c) * halves[t], halves[t])).wait_recv()
                over_d2d(t, 3 + j, 2 * cx + cy, _rows(c * halves[t], halves[t])).wait_send()
                over_ici(t, j, x, y, (cx, cy)).wait_send()

    return pl.pallas_call(
        body, name=name, in_specs=[ANY] * n, out_specs=[ANY] * n,
        out_shape=[jax.ShapeDtypeStruct((N_CHIPS,) + s.shape, s.dtype) for s in shards],
        scratch_shapes=[pltpu.SemaphoreType.DMA((n, 7)), pltpu.SemaphoreType.DMA((n, 7))],
        compiler_params=SIDE_EFFECTS,
    )(*shards)


def _pair_split(grads, name):
    n = len(grads)

    def body(*refs):
        srcs, outs, (send, recv) = refs[:n], refs[n:2 * n], refs[2 * n:]
        x, y, c, _ = _place()
        for t in range(n):
            half = grads[t].shape[1] // 2
            give = _rows((1 - c) * half, half)
            for s in range(N_CHIPS):
                _remote(srcs[t].at[s, give], outs[t].at[s], send.at[t], recv.at[t], (x, y, 1 - c)).start()
        for t in range(n):
            half = grads[t].shape[1] // 2
            _remote(srcs[t].at[:, _rows((1 - c) * half, half)], outs[t], send.at[t], recv.at[t], (x, y, 1 - c)).wait()

    return pl.pallas_call(
        body, name=name, in_specs=[ANY] * n, out_specs=[ANY] * n,
        out_shape=[jax.ShapeDtypeStruct((N_CHIPS, g.shape[1] // 2, g.shape[2]), g.dtype) for g in grads],
        scratch_shapes=[pltpu.SemaphoreType.DMA((n,)), pltpu.SemaphoreType.DMA((n,))],
        compiler_params=SIDE_EFFECTS,
    )(*grads)


def _chip_exchange(parts, name):
    n = len(parts)

    def body(*refs):
        srcs, outs, (send, recv) = refs[:n], refs[n:2 * n], refs[2 * n:]
        x, y, c, chips = _place()
        copies = [_remote(srcs[t].at[2 * cx + cy], outs[t].at[j], send.at[t, j], recv.at[t, j], (cx, cy, c))
                  for t in range(n) for j, (cx, cy) in enumerate(chips)]
        for cp in copies:
            cp.start()
        for cp in copies:
            cp.wait()

    return pl.pallas_call(
        body, name=name, in_specs=[ANY] * n, out_specs=[ANY] * n,
        out_shape=[jax.ShapeDtypeStruct((3,) + p.shape[1:], p.dtype) for p in parts],
        scratch_shapes=[pltpu.SemaphoreType.DMA((n, 3)), pltpu.SemaphoreType.DMA((n, 3))],
        compiler_params=SIDE_EFFECTS,
    )(*parts)


def _pair_join(totals):
    n = len(totals)

    def body(*refs):
        srcs, outs, (send, recv) = refs[:n], refs[n:2 * n], refs[2 * n:]
        x, y, c, _ = _place()
        for t in range(n):
            for r in _row_chunks(0, totals[t].shape[0]):
                _remote(srcs[t].at[r], outs[t].at[r], send.at[t], recv.at[t], (x, y, 1 - c)).start()
        for t in range(n):
            _remote(srcs[t], outs[t], send.at[t], recv.at[t], (x, y, 1 - c)).wait()

    return pl.pallas_call(
        body, name="pair_join", in_specs=[ANY] * n, out_specs=[ANY] * n,
        out_shape=[jax.ShapeDtypeStruct(t.shape, t.dtype) for t in totals],
        scratch_shapes=[pltpu.SemaphoreType.DMA((n,)), pltpu.SemaphoreType.DMA((n,))],
        compiler_params=SIDE_EFFECTS,
    )(*totals)


HBM = pl.BlockSpec(memory_space=pltpu.HBM)
SEM = pl.BlockSpec(memory_space=pltpu.SEMAPHORE)
SPLIT = pltpu.CompilerParams(has_side_effects=pltpu.SideEffectType.DATAFLOW_SIDE_EFFECTING)


def _in_hbm(a):
    return pltpu.with_memory_space_constraint(a, pltpu.HBM)


def _split_copies(kind, srcs, lands):
    x, y, c, chips = _place()
    out = []
    for t in range(len(srcs)):
        for j, (cx, cy) in enumerate(chips):
            if kind == "gather":
                r = _rows(c * (srcs[t].shape[0] // 2), srcs[t].shape[0] // 2)
                out.append((t, j, srcs[t].at[r], lands[t].at[2 * x + y, r], (cx, cy, c)))
            else:
                out.append((t, j, srcs[t].at[2 * cx + cy], lands[t].at[j], (cx, cy, c)))
    return out


def _split_start(kind, srcs, land_shapes, name):
    n = len(srcs)

    def body(*refs):
        src_refs, land_refs, sems, token = refs[:n], refs[n:2 * n], refs[2 * n:2 * n + 6], refs[-1]
        for t, j, s, d, to in _split_copies(kind, src_refs, land_refs):
            _remote(s, d, sems[j], sems[3 + j], to).start()
        token[...] = jnp.zeros_like(token)

    lands = [_in_hbm(lax.empty(s.shape, s.dtype)) for s in land_shapes]
    thru = [pltpu.HBM(a.shape, a.dtype) for a in list(srcs) + lands]
    res = pl.pallas_call(
        body, name=name,
        out_shape=(*[pltpu.SemaphoreType.DMA(())] * 6, *thru, jax.ShapeDtypeStruct((8, LANES), F32)),
        in_specs=[HBM] * (2 * n), out_specs=(*[SEM] * 6, *[HBM] * (2 * n), pl.BlockSpec(memory_space=pltpu.VMEM)),
        input_output_aliases={i: 6 + i for i in range(2 * n)}, compiler_params=SPLIT,
    )(*[_in_hbm(s) for s in srcs], *lands)
    return res[:6], res[6:6 + n], res[6 + n:6 + 2 * n], res[-1]


def _split_wait(kind, sems, srcs, lands, after, name):
    n = len(srcs)

    def body(*refs):
        src_refs, land_refs, sem_refs = refs[:n], refs[n:2 * n], refs[2 * n:2 * n + 6]
        for t, j, s, d, to in _split_copies(kind, src_refs, land_refs):
            cp = _remote(s, d, sem_refs[j], sem_refs[3 + j], to)
            cp.wait_send()
            cp.wait_recv()

    res = pl.pallas_call(
        body, name=name, out_shape=[pltpu.HBM(a.shape, a.dtype) for a in list(srcs) + list(lands)],
        in_specs=[HBM] * (2 * n) + [SEM] * 6 + [ANY], out_specs=[HBM] * (2 * n),
        input_output_aliases={i: i for i in range(2 * n)}, compiler_params=SPLIT,
    )(*srcs, *lands, *sems, after)
    return res[:n], res[n:]


def _gather_finish(shards, lands, name):
    n = len(shards)

    def body(*refs):
        srcs, outs, (send, recv) = refs[:n], refs[2 * n:3 * n], refs[3 * n:]
        x, y, c, chips = _place()
        sibling = (x, y, 1 - c)
        for t in range(n):
            half = shards[t].shape[0] // 2
            for r in _row_chunks(0, 2 * half):
                _remote(srcs[t].at[r], outs[t].at[2 * x + y, r], send.at[t, 3], recv.at[t, 3], sibling).start()
            for j, (cx, cy) in enumerate(chips):
                for r in _row_chunks(c * half, half):
                    _remote(outs[t].at[2 * cx + cy, r], outs[t].at[2 * cx + cy, r], send.at[t, j], recv.at[t, j], sibling).start()
        for t in range(n):
            half = shards[t].shape[0] // 2
            mine, other = _rows(c * half, half), _rows((1 - c) * half, half)
            _remote(srcs[t], outs[t].at[2 * x + y], send.at[t, 3], recv.at[t, 3], sibling).wait()
            for j, (cx, cy) in enumerate(chips):
                slab = outs[t].at[2 * cx + cy]
                _remote(slab.at[mine], slab.at[mine], send.at[t, j], recv.at[t, j], sibling).wait_send()
                _remote(slab.at[other], slab.at[other], send.at[t, j], recv.at[t, j], sibling).wait_recv()

    return pl.pallas_call(
        body, name=name, in_specs=[ANY] * (2 * n), out_specs=[ANY] * n,
        out_shape=[jax.ShapeDtypeStruct(a.shape, a.dtype) for a in lands],
        scratch_shapes=[pltpu.SemaphoreType.DMA((n, 4)), pltpu.SemaphoreType.DMA((n, 4))],
        input_output_aliases={n + t: t for t in range(n)}, compiler_params=SIDE_EFFECTS,
    )(*shards, *lands)


def _sum_all_devices(vec, name):
    n_rows = vec.shape[0]

    def body(v_ref, out_ref, buf, send, recv):
        x, y, c, _ = _place()
        me = 4 * x + 2 * y + c
        buf[me] = v_ref[...]
        flips = [(a, b, d) for a in (0, 1) for b in (0, 1) for d in (0, 1)][1:]
        copies = []
        for r, (a, b, d) in enumerate(flips):
            px, py, pc = (1 - x if a else x), (1 - y if b else y), (1 - c if d else c)
            copies.append(pltpu.make_async_remote_copy(src_ref=v_ref, dst_ref=buf.at[me], send_sem=send.at[r], recv_sem=recv.at[r],
                                                       device_id=(px, py, pc), device_id_type=MESH))
            copies[-1].start()
        for r, (a, b, d) in enumerate(flips):
            px, py, pc = (1 - x if a else x), (1 - y if b else y), (1 - c if d else c)
            pltpu.make_async_remote_copy(src_ref=v_ref, dst_ref=buf.at[4 * px + 2 * py + pc], send_sem=send.at[r], recv_sem=recv.at[r],
                                         device_id=(px, py, pc), device_id_type=MESH).wait_recv()
        for cp in copies:
            cp.wait_send()
        total = buf[0]
        for k in range(1, N_DEV):
            total = total + buf[k]
        out_ref[...] = total

    vmem = pl.BlockSpec(memory_space=pltpu.VMEM)
    return pl.pallas_call(
        body, name=name, in_specs=[vmem], out_specs=vmem, out_shape=jax.ShapeDtypeStruct(vec.shape, F32),
        scratch_shapes=[pltpu.VMEM((N_DEV, n_rows, LANES), F32), pltpu.SemaphoreType.DMA((N_DEV - 1,)), pltpu.SemaphoreType.DMA((N_DEV - 1,))],
        compiler_params=pltpu.CompilerParams(has_side_effects=True),
    )(vec)


def _half_tile(half, width):
    t = half
    while t * width * 4 > (2 << 20) and t % 32 == 0:
        t //= 2
    return t


def _pair_add(g, theirs, core, name):
    _, half, width = theirs.shape
    t = _half_tile(half, width)
    n = half // t

    def body(c_ref, a_ref, b_ref, o_ref):
        o_ref[...] = (a_ref[...] + b_ref[...]).astype(BF16)

    tile = pl.BlockSpec((1, t, width), lambda j, i, c_ref: (j, i, 0))
    return pl.pallas_call(
        body, name=name,
        grid_spec=pltpu.PrefetchScalarGridSpec(
            num_scalar_prefetch=1, grid=(N_CHIPS, n),
            in_specs=[pl.BlockSpec((1, t, width), lambda j, i, c_ref: (j, c_ref[0] * n + i, 0)), tile], out_specs=tile),
        out_shape=jax.ShapeDtypeStruct(theirs.shape, BF16), compiler_params=_params(("parallel", "parallel")),
    )(core, g, theirs)


def _chip_sum(part, others, chip, name):
    _, half, width = part.shape
    t = _half_tile(half, width)

    def body(s_ref, mine, p0, p1, p2, o_ref):
        o_ref[...] = ((mine[0].astype(F32) + p0[0].astype(F32)) + p1[0].astype(F32)) + p2[0].astype(F32)

    return pl.pallas_call(
        body, name=name,
        grid_spec=pltpu.PrefetchScalarGridSpec(
            num_scalar_prefetch=1, grid=(half // t,),
            in_specs=[pl.BlockSpec((1, t, width), lambda i, s_ref: (s_ref[0], i, 0))]
            + [pl.BlockSpec((1, t, width), lambda i, s_ref, j=j: (j, i, 0)) for j in range(3)],
            out_specs=pl.BlockSpec((t, width), lambda i, s_ref: (i, 0))),
        out_shape=jax.ShapeDtypeStruct((half, width), F32), compiler_params=_params(("parallel",)),
    )(chip, part, others, others, others)


EARLY = ("w_in", "w_uq", "w_ukv")
LATE = ("w_o_mla", "w_o_dil", "w_out", "w_ff1", "w_ff2")


def _chip_partials(grads, names, core, tag):
    gs = [grads[n] for n in names]
    theirs = _pair_split(gs, "pair_split_" + tag)
    return [_pair_add(g, th, core, "pair_add_" + n) for g, th, n in zip(gs, theirs, names)]


def _finish_reduce(parts, others, core, chip):
    names = [n for n, _ in BIG]
    totals = [_chip_sum(parts[n], others[n], chip, "chip_sum_" + n) for n in names]
    halves = _pair_join(totals)
    south = core[0] == 0
    return {n: jnp.where(south, jnp.concatenate([t, h], axis=0), jnp.concatenate([h, t], axis=0)) for n, t, h in zip(names, totals, halves)}


def _adamw(w, g, m, v, name):
    rows, width = w.shape
    t = rows
    for cand in (256, 128, 64, 32, 16, 8):
        if rows % cand == 0:
            t = cand
            break

    def body(w_ref, g_ref, m_ref, v_ref, d_ref, nm_ref, nv_ref):
        gv = g_ref[...]
        nm = B1 * m_ref[...] + (1.0 - B1) * gv
        nv = B2 * v_ref[...] + (1.0 - B2) * (gv * gv)
        m_hat = nm / (1.0 - B1 ** ADAM_STEP)
        v_hat = nv / (1.0 - B2 ** ADAM_STEP)
        d_ref[...] = -LR * (m_hat / (jnp.sqrt(v_hat) + ADAM_EPS) + WD * w_ref[...])
        nm_ref[...] = nm
        nv_ref[...] = nv

    spec = pl.BlockSpec((t, width), lambda i: (i, 0))
    return pl.pallas_call(body, name=name, grid=(rows // t,), in_specs=[spec] * 4, out_specs=[spec] * 3,
                          out_shape=[jax.ShapeDtypeStruct(w.shape, F32)] * 3, compiler_params=_params(("parallel",)))(w, g, m, v)


def _rows8(a):
    a = a.reshape(-1, LANES)
    return jnp.pad(a, ((0, -a.shape[0] % 8), (0, 0)))


def _pack_small(vals):
    return jnp.concatenate([_rows8(vals[name]) for name, _ in SMALL], axis=0)


def _unpack_small(packed, shapes):
    out, r0 = {}, 0
    for name, _ in SMALL:
        n = math.prod(shapes[name])
        out[name] = packed[r0:r0 + n // LANES].reshape(shapes[name])
        r0 += -(-n // (8 * LANES)) * 8
    return out


def kernel(x, w_in, b_gate, g_q_a, w_uq, g_kv_a, w_ukv, w_o_mla, w_o_dil, w_out, ln1_g, ln1_b, w_ff1, w_ff2, ln2_g, ln2_b, loss_target, m_w_in, m_b_gate, m_g_q_a, m_w_uq, m_g_kv_a, m_w_ukv, m_w_o_mla, m_w_o_dil, m_w_out, m_ln1_g, m_ln1_b, m_w_ff1, m_w_ff2, m_ln2_g, m_ln2_b, v_w_in, v_b_gate, v_g_q_a, v_w_uq, v_g_kv_a, v_w_ukv, v_w_o_mla, v_w_o_dil, v_w_out, v_ln1_g, v_ln1_b, v_w_ff1, v_w_ff2, v_ln2_g, v_ln2_b):
    order = ("w_in", "b_gate", "g_q_a", "w_uq", "g_kv_a", "w_ukv", "w_o_mla", "w_o_dil", "w_out", "ln1_g", "ln1_b", "w_ff1", "w_ff2", "ln2_g", "ln2_b")
    w = dict(w_in=w_in, b_gate=b_gate, g_q_a=g_q_a, w_uq=w_uq, g_kv_a=g_kv_a, w_ukv=w_ukv, w_o_mla=w_o_mla, w_o_dil=w_o_dil, w_out=w_out,
             ln1_g=ln1_g, ln1_b=ln1_b, w_ff1=w_ff1, w_ff2=w_ff2, ln2_g=ln2_g, ln2_b=ln2_b)
    m = dict(w_in=m_w_in, b_gate=m_b_gate, g_q_a=m_g_q_a, w_uq=m_w_uq, g_kv_a=m_g_kv_a, w_ukv=m_w_ukv, w_o_mla=m_w_o_mla, w_o_dil=m_w_o_dil,
             w_out=m_w_out, ln1_g=m_ln1_g, ln1_b=m_ln1_b, w_ff1=m_w_ff1, w_ff2=m_w_ff2, ln2_g=m_ln2_g, ln2_b=m_ln2_b)
    v = dict(w_in=v_w_in, b_gate=v_b_gate, g_q_a=v_g_q_a, w_uq=v_w_uq, g_kv_a=v_g_kv_a, w_ukv=v_w_ukv, w_o_mla=v_w_o_mla, w_o_dil=v_w_o_dil,
             w_out=v_w_out, ln1_g=v_ln1_g, ln1_b=v_ln1_b, w_ff1=v_w_ff1, w_ff2=v_w_ff2, ln2_g=v_ln2_g, ln2_b=v_ln2_b)
    chip = 2 * lax.axis_index("x") + lax.axis_index("y")
    south = (lax.axis_index("c") == 0).astype(F32)
    gate_w = D_MODEL // N_CHIPS

    core = lax.axis_index("c").astype(jnp.int32).reshape(1)
    shards = {n: w[n][0].astype(BF16) for n, _ in BIG}
    first = dict(zip(EARLY, _gather_weights([shards[n] for n in EARLY], "gather_early")))
    late_shards = [shards[n] for n in LATE]
    g_sems, g_srcs, g_lands, g_token = _split_start(
        "gather", late_shards, [jax.ShapeDtypeStruct((N_CHIPS,) + s.shape, BF16) for s in late_shards], "gather_late_start")
    b_mine = lax.dynamic_update_slice(jnp.zeros((2, D_MODEL), F32), b_gate[0] * south, (0, chip * gate_w))
    b_full = _sum_all_devices(b_mine.reshape(-1, LANES), "gather_b_gate").reshape(2, D_MODEL)

    def late_weights(after):
        srcs, lands = _split_wait("gather", g_sems, g_srcs, g_lands, after, "gather_late_wait")
        return dict(zip(LATE, _gather_finish(srcs, lands, "gather_late_finish")))

    sent = {}

    def early_grads(grads_late):
        parts_late = _chip_partials(grads_late, LATE, core, "late")
        shapes = [jax.ShapeDtypeStruct((3,) + p.shape[1:], BF16) for p in parts_late]
        sent["sems"], sent["srcs"], sent["lands"], token = _split_start("scatter", parts_late, shapes, "exchange_late_start")
        return token

    loss_part, grad_x, grads = _local_step(x, loss_target, first, b_full, g_q_a, g_kv_a, ln1_g, ln1_b, ln2_g, ln2_b,
                                           token=g_token, late_weights=late_weights, early_grads=early_grads)

    parts_late, others_late = _split_wait("scatter", sent["sems"], sent["srcs"], sent["lands"], grad_x, "exchange_late_wait")
    parts, others = dict(zip(LATE, parts_late)), dict(zip(LATE, others_late))
    parts_early = _chip_partials(grads, EARLY, core, "early")
    parts.update(zip(EARLY, parts_early))
    others.update(zip(EARLY, _chip_exchange(parts_early, "exchange_early")))
    g_out = _finish_reduce(parts, others, core, chip.astype(jnp.int32).reshape(1))
    small_in = jnp.concatenate([_pack_small(grads), jnp.broadcast_to(loss_part, (8, LANES))], axis=0)
    small = _sum_all_devices(small_in, "sum_small")
    loss = small[small_in.shape[0] - 8, 0]
    g_out.update(_unpack_small(small, {name: (1, n) for name, n in SMALL}))
    g_out["b_gate"] = lax.dynamic_slice(g_out["b_gate"].reshape(2, D_MODEL), (0, chip * gate_w), (2, gate_w))

    delta, new_m, new_v = {}, {}, {}
    for name, _ in BIG:
        delta[name], new_m[name], new_v[name] = _adamw(w[name][0], g_out[name], m[name][0], v[name][0], "adamw_" + name)
    small_shapes = {name: w[name].shape for name, _ in SMALL}
    packed = _adamw(_pack_small(w), _pack_small(g_out), _pack_small(m), _pack_small(v), "adamw_small")
    for d, t in zip((delta, new_m, new_v), packed):
        d.update(_unpack_small(t, small_shapes))

    lead = lambda d: [d[name].reshape(w[name].shape) for name in order]
    return (loss, grad_x, *lead(g_out), *lead(delta), *lead(new_m), *lead(new_v))
```

```python
import functools
import math

import jax
import jax.numpy as jnp
from jax import lax
from jax.experimental import pallas as pl
from jax.experimental.pallas import tpu as pltpu

F32 = jnp.float32
BF16 = jnp.bfloat16
MESH = pl.DeviceIdType.MESH

D_MODEL = 1024
N_HEADS = 8
LANES = 128
NOPE, ROPE, V_DIM = 64, 32, 64
MLA_QK = NOPE + ROPE
Q_LORA, KV_LORA = 384, 256
DIL_DIM = 64
DIL_PATTERNS = ((128, 1), (512, 4), (2048, 16))
D_FF = 4096
N_CHIPS = 4
N_DEV = 8
IN_WIDTH = 4256
LN_EPS, RMS_EPS = 1e-5, 1e-6
NEG = -1e30
ALPHA = 2.0 ** 0.25
ROPE_THETA = 10000.0
LR, B1, B2, ADAM_EPS, WD, ADAM_STEP = 0.001, 0.9, 0.999, 1e-8, 0.01, 10

P_GATE, P_QD, P_KD, P_VD, P_LORA, P_KR, P_WIDTH = 0, 2048, 3072, 4096, 5120, 5760, 6144
LORA_W = Q_LORA + KV_LORA
KR_LANE = NOPE

ATT_T = 512
ROW_T = 512
VMEM_LIMIT = 56 * 1024 * 1024

NN = (((1,), (0,)), ((), ()))
NT = (((1,), (1,)), ((), ()))
TN = (((0,), (0,)), ((), ()))


def _params(sem=None, **kw):
    return pltpu.CompilerParams(dimension_semantics=sem, vmem_limit_bytes=VMEM_LIMIT, **kw)


def _matmul(a, b, *, mode, name, tm, tn, tk, out_dtypes=(F32,), extras=(), epilogue=None, b_shards=False, out_shards=False, after=None):
    if b_shards:
        n_sh, rows_b, cols_b = b.shape
        b_shape = (rows_b, n_sh * cols_b)
    else:
        b_shape = b.shape
    if mode == "nn":
        (m, k), (k2, n) = a.shape, b_shape
    elif mode == "nt":
        (m, k), (n, k2) = a.shape, b_shape
    else:
        (k, m), (k2, n) = a.shape, b_shape
    assert k == k2, (a.shape, b.shape, mode)
    tm, tn, tk = min(tm, m), min(tn, n), min(tk, k)
    assert m % tm == 0 and n % tn == 0 and k % tk == 0, (name, m, n, k, tm, tn, tk)
    nk = k // tk
    n_ex, n_out = len(extras), len(out_dtypes)
    n_in = 2 + n_ex + (after is not None)
    dims = {"nn": NN, "nt": NT, "tn": TN}[mode]

    def body(*refs):
        a_ref, b_ref = refs[:2]
        ex_refs = refs[2:2 + n_ex]
        out_refs = refs[n_in:n_in + n_out]
        part = lax.dot_general(a_ref[...].astype(BF16), b_ref[...].astype(BF16), dims, preferred_element_type=F32)

        def finish(acc):
            outs = epilogue(acc, *[r[...] for r in ex_refs]) if epilogue is not None else (acc,)
            for r, o in zip(out_refs, outs):
                r[...] = o.astype(r.dtype)

        if nk == 1:
            finish(part)
        else:
            acc_ref = refs[-1]
            kk = pl.program_id(2)

            @pl.when(kk == 0)
            def _():
                acc_ref[...] = part

            @pl.when(kk > 0)
            def _():
                acc_ref[...] += part

            @pl.when(kk == nk - 1)
            def _():
                finish(acc_ref[...])

    a_spec = {"nn": pl.BlockSpec((tm, tk), lambda i, j, kk: (i, kk)),
              "nt": pl.BlockSpec((tm, tk), lambda i, j, kk: (i, kk)),
              "tn": pl.BlockSpec((tk, tm), lambda i, j, kk: (kk, i))}[mode]
    b_spec = {"nn": pl.BlockSpec((tk, tn), lambda i, j, kk: (kk, j)),
              "nt": pl.BlockSpec((tn, tk), lambda i, j, kk: (j, kk)),
              "tn": pl.BlockSpec((tk, tn), lambda i, j, kk: (kk, j))}[mode]
    tile = pl.BlockSpec((tm, tn), lambda i, j, kk: (i, j))
    out_spec, out_dims = tile, (m, n)
    if b_shards and mode == "nn":
        per = cols_b // tn
        b_spec = pl.BlockSpec((None, tk, tn), lambda i, j, kk: (j // per, kk, j % per))
    elif b_shards:
        assert mode == "nt"
        per = cols_b // tk
        b_spec = pl.BlockSpec((None, tn, tk), lambda i, j, kk: (kk // per, j, kk % per))
    if out_shards:
        assert not extras and epilogue is None
        per_out = n // N_CHIPS // tn
        out_spec = pl.BlockSpec((None, tm, tn), lambda i, j, kk: (j // per_out, i, j % per_out))
        out_dims = (N_CHIPS, m, n // N_CHIPS)
    outs = pl.pallas_call(
        body, name=name,
        grid=(m // tm, n // tn, nk),
        in_specs=[a_spec, b_spec] + [tile] * n_ex + [pl.BlockSpec(memory_space=pl.ANY)] * (after is not None),
        out_specs=[out_spec] * n_out,
        out_shape=[jax.ShapeDtypeStruct(out_dims, dt) for dt in out_dtypes],
        scratch_shapes=[pltpu.VMEM((tm, tn), F32)] if nk > 1 else [],
        compiler_params=_params(("parallel", "parallel", "arbitrary")),
    )(a, b, *extras, *([after] if after is not None else []))
    return outs[0] if n_out == 1 else outs


def _rowwise(fn, *, name, rows, seq, ins, outs, sums=()):
    tm = min(ROW_T, seq)
    n_pos = seq // tm
    n_in, n_out, n_sum = len(ins), len(outs), len(sums)

    def body(*refs):
        vals = fn(*[r[...] for r in refs[:n_in]])
        for r, v in zip(refs[n_in:n_in + n_out], vals[:n_out]):
            r[...] = v.astype(r.dtype)
        first = pl.program_id(0) == 0
        for r, v in zip(refs[n_in + n_out:], vals[n_out:]):
            @pl.when(first)
            def _(r=r, v=v):
                r[...] = v

            @pl.when(jnp.logical_not(first))
            def _(r=r, v=v):
                r[...] += v

    def spec(arr, width, col, kind):
        if kind == "row":
            return pl.BlockSpec((tm, width), lambda i, col=col: (i, col))
        if kind == "pos":
            return pl.BlockSpec((tm, width), lambda i, col=col: (i % n_pos, col))
        return pl.BlockSpec(arr.shape, lambda i: (0,) * arr.ndim)

    res = pl.pallas_call(
        body, name=name,
        grid=(rows // tm,),
        in_specs=[spec(*t) for t in ins],
        out_specs=[pl.BlockSpec((tm, w), lambda i: (i, 0)) for w, _ in outs]
        + [pl.BlockSpec((1, w), lambda i: (0, 0)) for w in sums],
        out_shape=[jax.ShapeDtypeStruct((rows, w), dt) for w, dt in outs]
        + [jax.ShapeDtypeStruct((1, w), F32) for w in sums],
        compiler_params=_params(("arbitrary",)),
    )(*[t[0] for t in ins])
    return res


def _colsum(v):
    return jnp.sum(v, axis=0, keepdims=True)


def _rope_fwd(t, c, s_up, s_dn):
    return t * c + pltpu.roll(t, LANES - 16, 1) * s_up + pltpu.roll(t, 16, 1) * s_dn


def _rope_bwd(d, c, s_up, s_dn):
    return d * c + pltpu.roll(d * s_up, 16, 1) + pltpu.roll(d * s_dn, LANES - 16, 1)


def _rope_tables(seq):
    half = ROPE // 2
    inv = jnp.power(ROPE_THETA, -jnp.arange(half, dtype=F32) / half)
    ang = jnp.arange(seq, dtype=F32)[:, None] * inv[None, :]
    cos, sin = jnp.cos(ang), jnp.sin(ang)
    zeros = jnp.zeros((seq, half), F32)
    lo, hi = jnp.ones((seq, KR_LANE), F32), jnp.ones((seq, LANES - KR_LANE - ROPE), F32)
    c = jnp.concatenate([lo, cos, cos, hi], axis=1)
    c_rope_only = jnp.concatenate([0 * lo, cos, cos, 0 * hi], axis=1)
    s_up = jnp.concatenate([0 * lo, -sin, zeros, 0 * hi], axis=1)
    s_dn = jnp.concatenate([0 * lo, zeros, sin, 0 * hi], axis=1)
    return c, s_up, s_dn, c_rope_only


def _rms(x, g):
    r = lax.rsqrt(jnp.mean(x * x, axis=1, keepdims=True) + RMS_EPS)
    return x * r * g


def _rms_bwd(x, g, dy):
    r = lax.rsqrt(jnp.mean(x * x, axis=1, keepdims=True) + RMS_EPS)
    xh = x * r
    dxh = dy * g
    dx = r * (dxh - xh * jnp.mean(dxh * xh, axis=1, keepdims=True))
    return dx, _colsum(dy * xh)


def _ln_stats(x):
    mu = jnp.mean(x, axis=1, keepdims=True)
    xc = x - mu
    r = lax.rsqrt(jnp.mean(xc * xc, axis=1, keepdims=True) + LN_EPS)
    return xc * r, r


def _ln_bwd(xh, r, g, dy):
    dxh = dy * g
    dx = r * (dxh - jnp.mean(dxh, axis=1, keepdims=True) - xh * jnp.mean(dxh * xh, axis=1, keepdims=True))
    return dx, _colsum(dy * xh), _colsum(dy)


def _bias_spec(bias):
    per_head = bias.shape[0] > 1
    return pl.BlockSpec((1,) + bias.shape[1:], lambda b, h: (h if per_head else 0, 0, 0, 0))


def _attn_fwd(q, qb0, k, kb0, v, vb0, bias, scale, *, name, batch, seq):
    t = ATT_T
    nq = seq // t
    rows = batch * seq

    def body(q_ref, k_ref, v_ref, bias_ref, o_ref, lse_ref, qb, kb, vb):
        qb[...] = q_ref[...].astype(BF16)
        kb[...] = k_ref[...].astype(BF16)
        vb[...] = v_ref[...].astype(BF16)
        for i in range(nq):
            qt = qb[i * t:(i + 1) * t, :]
            logits = [lax.dot_general(qt, kb[j * t:(j + 1) * t, :], NT, preferred_element_type=F32) * scale + bias_ref[0, i - j]
                      for j in range(i + 1)]
            top = functools.reduce(jnp.maximum, logits)
            m = jnp.max(top, axis=1, keepdims=True)
            ps = [jnp.exp(s - m) for s in logits]
            l = jnp.sum(functools.reduce(jnp.add, ps), axis=1, keepdims=True)
            acc = functools.reduce(jnp.add, [lax.dot_general(p.astype(BF16), vb[j * t:(j + 1) * t, :], NN, preferred_element_type=F32)
                                             for j, p in enumerate(ps)])
            o_ref[i * t:(i + 1) * t, :] = acc / l
            lse_ref[i * t:(i + 1) * t, :] = jnp.broadcast_to(m + jnp.log(l), (t, LANES))

    slab = lambda b0: pl.BlockSpec((seq, LANES), lambda b, h, b0=b0: (b, b0 + h))
    return pl.pallas_call(
        body, name=name,
        grid=(batch, N_HEADS),
        in_specs=[slab(qb0), slab(kb0), slab(vb0), _bias_spec(bias)],
        out_specs=[slab(0), slab(0)],
        out_shape=[jax.ShapeDtypeStruct((rows, N_HEADS * LANES), F32)] * 2,
        scratch_shapes=[pltpu.VMEM((seq, LANES), BF16)] * 3,
        compiler_params=_params(("arbitrary", "arbitrary")),
    )(q, k, v, bias)


def _attn_bwd(q, qb0, k, kb0, v, vb0, o, do, lse, bias, scale, *, name, batch, seq, out_dtype):
    t = ATT_T
    nq = seq // t
    rows = batch * seq

    def body(q_ref, k_ref, v_ref, o_ref, do_ref, lse_ref, bias_ref, dq_ref, dk_ref, dv_ref, qb, kb, vb, dob, dka, dva):
        qb[...] = q_ref[...].astype(BF16)
        kb[...] = k_ref[...].astype(BF16)
        vb[...] = v_ref[...].astype(BF16)
        dob[...] = do_ref[...].astype(BF16)
        for i in range(nq):
            at = slice(i * t, (i + 1) * t)
            qt, dot = qb[at, :], dob[at, :]
            lse_t = lse_ref[at, 0:1]
            delta = jnp.sum(o_ref[at, :] * do_ref[at, :], axis=1, keepdims=True)
            dq = None
            for j in range(i + 1):
                kat = slice(j * t, (j + 1) * t)
                kt, vt = kb[kat, :], vb[kat, :]
                p = jnp.exp(lax.dot_general(qt, kt, NT, preferred_element_type=F32) * scale + bias_ref[0, i - j] - lse_t)
                dp = lax.dot_general(dot, vt, NT, preferred_element_type=F32)
                ds = (p * (dp - delta) * scale).astype(BF16)
                dk_part = lax.dot_general(ds, qt, TN, preferred_element_type=F32)
                dv_part = lax.dot_general(p.astype(BF16), dot, TN, preferred_element_type=F32)
                if i == j:
                    dka[kat, :] = dk_part
                    dva[kat, :] = dv_part
                else:
                    dka[kat, :] += dk_part
                    dva[kat, :] += dv_part
                dq_part = lax.dot_general(ds, kt, NN, preferred_element_type=F32)
                dq = dq_part if dq is None else dq + dq_part
            dq_ref[at, :] = dq.astype(dq_ref.dtype)
        dk_ref[...] = dka[...].astype(dk_ref.dtype)
        dv_ref[...] = dva[...].astype(dv_ref.dtype)

    slab = lambda b0: pl.BlockSpec((seq, LANES), lambda b, h, b0=b0: (b, b0 + h))
    return pl.pallas_call(
        body, name=name,
        grid=(batch, N_HEADS),
        in_specs=[slab(qb0), slab(kb0), slab(vb0), slab(0), slab(0), slab(0), _bias_spec(bias)],
        out_specs=[slab(0)] * 3,
        out_shape=[jax.ShapeDtypeStruct((rows, N_HEADS * LANES), out_dtype)] * 3,
        scratch_shapes=[pltpu.VMEM((seq, LANES), BF16)] * 4 + [pltpu.VMEM((seq, LANES), F32)] * 2,
        compiler_params=_params(("arbitrary", "arbitrary")),
    )(q, k, v, o, do, lse, bias)


def _tile_dist(seq):
    n = seq // ATT_T
    d = jnp.arange(n, dtype=jnp.int32)[:, None, None] * ATT_T
    return d + jnp.arange(ATT_T, dtype=jnp.int32)[None, :, None] - jnp.arange(ATT_T, dtype=jnp.int32)[None, None, :]


def _causal_bias(seq):
    return jnp.where(_tile_dist(seq) >= 0, 0.0, NEG).astype(F32)[None]


def _dilated_bias(seq):
    dist = _tile_dist(seq)
    count = jnp.zeros(dist.shape, F32)
    for window, dilation in DIL_PATTERNS:
        count += ((dist >= 0) & (dist <= window) & (dist % dilation == 0)).astype(F32)
    slopes = jnp.asarray([2.0 ** (-8.0 * (i + 1) / N_HEADS) for i in range(N_HEADS)], F32)
    alibi = -slopes[:, None, None, None] * dist.astype(F32)[None]
    return jnp.where(count[None] > 0, jnp.log(jnp.maximum(count, 1.0))[None] + alibi, NEG).astype(F32)


def _pad_heads(w, width):
    kdim, n = w.shape[0], w.shape[1] // width
    return jnp.pad(w.reshape(kdim, n, width), ((0, 0), (0, 0), (0, LANES - width))).reshape(kdim, n * LANES)


def _unpad_heads(w, width):
    kdim, n = w.shape[0], w.shape[1] // LANES
    return w.reshape(kdim, n, LANES)[:, :, :width].reshape(kdim, n * width)


def _pad_w_in(w):
    q_a_kv_a = w[:, :LORA_W]
    k_r = jnp.pad(w[:, LORA_W:LORA_W + ROPE], ((0, 0), (KR_LANE, LANES - KR_LANE - ROPE)))
    qkv_d = _pad_heads(w[:, LORA_W + ROPE:LORA_W + ROPE + 3 * N_HEADS * DIL_DIM], DIL_DIM)
    gates = w[:, LORA_W + ROPE + 3 * N_HEADS * DIL_DIM:]
    tail = jnp.zeros((w.shape[0], P_WIDTH - P_KR - LANES), w.dtype)
    return jnp.concatenate([gates, qkv_d, q_a_kv_a, k_r, tail], axis=1)


def _unpad_w_in(g):
    gates = g[:, P_GATE:P_QD]
    qkv_d = _unpad_heads(g[:, P_QD:P_LORA], DIL_DIM)
    q_a_kv_a = g[:, P_LORA:P_KR]
    k_r = g[:, P_KR + KR_LANE:P_KR + KR_LANE + ROPE]
    return jnp.concatenate([q_a_kv_a, k_r, qkv_d, gates], axis=1)


def _split_ukv(w):
    w3 = w.reshape(w.shape[0], N_HEADS, NOPE + V_DIM)
    return (_pad_heads(w3[:, :, :NOPE].reshape(w.shape[0], -1), NOPE),
            _pad_heads(w3[:, :, NOPE:].reshape(w.shape[0], -1), V_DIM))


def _merge_ukv(g_k, g_v):
    kdim = g_k.shape[0]
    k3 = _unpad_heads(g_k, NOPE).reshape(kdim, N_HEADS, NOPE)
    v3 = _unpad_heads(g_v, V_DIM).reshape(kdim, N_HEADS, V_DIM)
    return jnp.concatenate([k3, v3], axis=2).reshape(kdim, N_HEADS * (NOPE + V_DIM))


def _pad_rows(w, width):
    return _pad_heads(w.T, width).T


def _unpad_rows(g, width):
    return _unpad_heads(g.T, width).T


def _join_cols(w):
    return w.transpose(1, 0, 2).reshape(w.shape[1], N_CHIPS * w.shape[2])


def _split_cols(g):
    return g.reshape(g.shape[0], N_CHIPS, g.shape[1] // N_CHIPS).transpose(1, 0, 2)


def _local_step(x3, target3, wg, b_gate, g_q_a, g_kv_a, ln1_g, ln1_b, ln2_g, ln2_b, token=None, late_weights=None, early_grads=None):
    w_in, w_uq, w_ukv = (_join_cols(wg[n]) for n in ("w_in", "w_uq", "w_ukv"))
    batch, seq, _ = x3.shape
    rows = batch * seq
    x = x3.reshape(rows, D_MODEL)
    target = target3.reshape(rows, D_MODEL)
    row = functools.partial(_rowwise, rows=rows, seq=seq)
    mm = _matmul

    w_in_p = _pad_w_in(w_in)
    w_uq_p = _pad_heads(w_uq, MLA_QK)
    w_uk_p, w_uv_p = _split_ukv(w_ukv)
    b0, b1 = b_gate[0:1], b_gate[1:2]
    rope_c, rope_up, rope_dn, rope_c_only = _rope_tables(seq)
    bias_mla, bias_dil = _causal_bias(seq), _dilated_bias(seq)
    scale_mla, scale_dil = MLA_QK ** -0.5, DIL_DIM ** -0.5
    qd0, kd0, vd0, lora0, kr0 = P_QD // LANES, P_KD // LANES, P_VD // LANES, P_LORA // LORA_W, P_KR // LANES

    proj = mm(x, w_in_p, mode="nn", name="proj", tm=1024, tn=1536, tk=1024, after=token)

    def prep(lora, gq, gkv):
        return _rms(lora[:, :Q_LORA], gq), _rms(lora[:, Q_LORA:], gkv)

    qn, kvn = row(prep, name="mla_rms", ins=[(proj, LORA_W, lora0, "row"), (g_q_a, 0, 0, "full"), (g_kv_a, 0, 0, "full")],
                  outs=[(Q_LORA, BF16), (KV_LORA, BF16)])
    q_lin = mm(qn, w_uq_p, mode="nn", name="q_up", tm=1024, tn=1024, tk=Q_LORA)
    k_lin = mm(kvn, w_uk_p, mode="nn", name="k_up", tm=1024, tn=1024, tk=KV_LORA)
    v_a = mm(kvn, w_uv_p, mode="nn", name="v_up", tm=1024, tn=1024, tk=KV_LORA, out_dtypes=(BF16,))

    def rope_qk(ql, kl, kr, c, up, dn):
        k_rot = _rope_fwd(kr, c, up, dn)
        qs = [_rope_fwd(ql[:, h * LANES:(h + 1) * LANES], c, up, dn) for h in range(N_HEADS)]
        ks = [kl[:, h * LANES:(h + 1) * LANES] + k_rot for h in range(N_HEADS)]
        return jnp.concatenate(qs, axis=1), jnp.concatenate(ks, axis=1)

    pos = lambda tab: (tab, LANES, 0, "pos")
    q_a, k_a = row(rope_qk, name="rope_qk",
                   ins=[(q_lin, D_MODEL, 0, "row"), (k_lin, D_MODEL, 0, "row"), (proj, LANES, kr0, "row"), pos(rope_c), pos(rope_up), pos(rope_dn)],
                   outs=[(N_HEADS * LANES, BF16), (N_HEADS * LANES, BF16)])
    o_a, lse_a = _attn_fwd(q_a, 0, k_a, 0, v_a, 0, bias_mla, scale_mla, name="mla_fwd", batch=batch, seq=seq)
    o_b, lse_b = _attn_fwd(proj, qd0, proj, kd0, proj, vd0, bias_dil, scale_dil, name="dil_fwd", batch=batch, seq=seq)
    late = wg if late_weights is None else late_weights(o_b)
    w_oa_p = _pad_rows(_join_cols(late["w_o_mla"]), V_DIM)
    w_ob_p = _pad_rows(_join_cols(late["w_o_dil"]), DIL_DIM)
    w_out, w_ff1, w_ff2 = late["w_out"].reshape(D_MODEL, D_MODEL), late["w_ff1"], late["w_ff2"].reshape(D_FF, D_MODEL)
    y_a = mm(o_a, w_oa_p, mode="nn", name="o_mla", tm=1024, tn=1024, tk=1024)
    y_b = mm(o_b, w_ob_p, mode="nn", name="o_dil", tm=1024, tn=1024, tk=1024)

    def gate(t0, t1, c0, c1, ya, yb):
        return (jax.nn.sigmoid(t0 + c0) * ya + jax.nn.sigmoid(t1 + c1) * yb,)

    gate_ins = [(proj, D_MODEL, 0, "row"), (proj, D_MODEL, 1, "row"), (b0, 0, 0, "full"), (b1, 0, 0, "full")]
    (u,) = row(gate, name="gate", ins=gate_ins + [(y_a, D_MODEL, 0, "row"), (y_b, D_MODEL, 0, "row")], outs=[(D_MODEL, BF16)])
    mixed = mm(u, w_out, mode="nn", name="mix", tm=1024, tn=1024, tk=1024)

    def ln1(xv, mv, g, b):
        r1 = ALPHA * xv + mv
        xh, _ = _ln_stats(r1)
        return r1, xh * g + b

    r1, h = row(ln1, name="ln1", ins=[(x, D_MODEL, 0, "row"), (mixed, D_MODEL, 0, "row"), (ln1_g, 0, 0, "full"), (ln1_b, 0, 0, "full")],
                outs=[(D_MODEL, F32), (D_MODEL, F32)])

    def relu2(acc):
        r = jnp.maximum(acc, 0.0)
        return acc, r * r

    a_ff, z = mm(h, w_ff1, mode="nn", name="ff1", tm=1024, tn=1024, tk=1024, out_dtypes=(F32, BF16), epilogue=relu2, b_shards=True)
    f = mm(z, w_ff2, mode="nn", name="ff2", tm=1024, tn=1024, tk=1024)

    def ln2_loss(hv, fv, tv, g, b):
        xh, r = _ln_stats(ALPHA * hv + fv)
        err = xh * g + b - tv
        dy = err * (1.0 / D_MODEL)
        dr2, dg, db = _ln_bwd(xh, r, g, dy)
        loss = jnp.sum(_colsum(err * err), axis=1, keepdims=True) * (0.5 / D_MODEL)
        return dr2, jnp.broadcast_to(loss, (1, LANES)), dg, db

    dr2, loss_l, d_ln2_g, d_ln2_b = row(
        ln2_loss, name="ln2_loss",
        ins=[(h, D_MODEL, 0, "row"), (f, D_MODEL, 0, "row"), (target, D_MODEL, 0, "row"), (ln2_g, 0, 0, "full"), (ln2_b, 0, 0, "full")],
        outs=[(D_MODEL, F32)], sums=[LANES, D_MODEL, D_MODEL])

    d_w_ff2 = mm(z, dr2, mode="tn", name="d_w_ff2", tm=1024, tn=1024, tk=1024)
    da = mm(dr2, w_ff2, mode="nt", name="d_ff_act", tm=1024, tn=1024, tk=1024, out_dtypes=(BF16,), extras=(a_ff,),
            epilogue=lambda acc, av: (acc * (2.0 * jnp.maximum(av, 0.0)),))
    d_w_ff1 = mm(h, da, mode="tn", name="d_w_ff1", tm=1024, tn=1024, tk=1024, out_shards=True)
    dh = mm(da, w_ff1, mode="nt", name="d_h", tm=1024, tn=1024, tk=1024, extras=(dr2,), epilogue=lambda acc, rv: (acc + ALPHA * rv,), b_shards=True)

    def ln1_bwd(dhv, r1v, g):
        xh, r = _ln_stats(r1v)
        return _ln_bwd(xh, r, g, dhv)

    dr1, d_ln1_g, d_ln1_b = row(ln1_bwd, name="ln1_bwd", ins=[(dh, D_MODEL, 0, "row"), (r1, D_MODEL, 0, "row"), (ln1_g, 0, 0, "full")],
                                outs=[(D_MODEL, F32)], sums=[D_MODEL, D_MODEL])
    d_w_out = mm(u, dr1, mode="tn", name="d_w_out", tm=1024, tn=1024, tk=1024)
    du = mm(dr1, w_out, mode="nt", name="d_u", tm=1024, tn=1024, tk=1024)

    def gate_bwd(t0, t1, c0, c1, ya, yb, duv):
        s0, s1 = jax.nn.sigmoid(t0 + c0), jax.nn.sigmoid(t1 + c1)
        dt0 = duv * ya * s0 * (1.0 - s0)
        dt1 = duv * yb * s1 * (1.0 - s1)
        return duv * s0, duv * s1, jnp.concatenate([dt0, dt1], axis=1), jnp.concatenate([_colsum(dt0), _colsum(dt1)], axis=1)

    dy_a, dy_b, d_gates, d_b_gate = row(
        gate_bwd, name="gate_bwd", ins=gate_ins + [(y_a, D_MODEL, 0, "row"), (y_b, D_MODEL, 0, "row"), (du, D_MODEL, 0, "row")],
        outs=[(D_MODEL, BF16), (D_MODEL, BF16), (2 * D_MODEL, BF16)], sums=[2 * D_MODEL])
    d_w_oa_p = mm(o_a, dy_a, mode="tn", name="d_w_o_mla", tm=1024, tn=1024, tk=1024)
    d_w_ob_p = mm(o_b, dy_b, mode="tn", name="d_w_o_dil", tm=1024, tn=1024, tk=1024)
    grads = dict(w_o_mla=_split_cols(_unpad_rows(d_w_oa_p, V_DIM)), w_o_dil=_split_cols(_unpad_rows(d_w_ob_p, DIL_DIM)),
                 w_out=d_w_out.reshape(N_CHIPS, D_MODEL // N_CHIPS, D_MODEL), w_ff1=d_w_ff1, w_ff2=d_w_ff2.reshape(N_CHIPS, D_FF // N_CHIPS, D_MODEL))
    sent = None if early_grads is None else early_grads(grads)
    do_a = mm(dy_a, w_oa_p, mode="nt", name="d_o_mla", tm=1024, tn=1024, tk=1024, after=sent)
    do_b = mm(dy_b, w_ob_p, mode="nt", name="d_o_dil", tm=1024, tn=1024, tk=1024)
    dq_a, dk_a, dv_a = _attn_bwd(q_a, 0, k_a, 0, v_a, 0, o_a, do_a, lse_a, bias_mla, scale_mla,
                                 name="mla_bwd", batch=batch, seq=seq, out_dtype=F32)
    dq_d, dk_d, dv_d = _attn_bwd(proj, qd0, proj, kd0, proj, vd0, o_b, do_b, lse_b, bias_dil, scale_dil,
                                 name="dil_bwd", batch=batch, seq=seq, out_dtype=BF16)

    def mla_post(dq, dk, c, up, dn, c_only):
        dqs = [_rope_bwd(dq[:, h * LANES:(h + 1) * LANES], c, up, dn) for h in range(N_HEADS)]
        dk_sum = dk[:, :LANES]
        for h in range(1, N_HEADS):
            dk_sum = dk_sum + dk[:, h * LANES:(h + 1) * LANES]
        return jnp.concatenate(dqs, axis=1), _rope_bwd(dk_sum, c_only, up, dn)

    dq_lin, d_kr = row(mla_post, name="mla_unrope",
                       ins=[(dq_a, D_MODEL, 0, "row"), (dk_a, D_MODEL, 0, "row"), pos(rope_c), pos(rope_up), pos(rope_dn), pos(rope_c_only)],
                       outs=[(N_HEADS * LANES, BF16), (LANES, BF16)])
    d_w_uq_p = mm(qn, dq_lin, mode="tn", name="d_w_uq", tm=Q_LORA, tn=1024, tk=1024)
    d_w_uk_p = mm(kvn, dk_a, mode="tn", name="d_w_uk", tm=KV_LORA, tn=1024, tk=1024)
    d_w_uv_p = mm(kvn, dv_a, mode="tn", name="d_w_uv", tm=KV_LORA, tn=1024, tk=1024)
    d_qn = mm(dq_lin, w_uq_p, mode="nt", name="d_qn", tm=1024, tn=Q_LORA, tk=1024)
    d_kvn_k = mm(dk_a, w_uk_p, mode="nt", name="d_kvn_k", tm=1024, tn=KV_LORA, tk=1024)
    d_kvn = mm(dv_a, w_uv_p, mode="nt", name="d_kvn", tm=1024, tn=KV_LORA, tk=1024, extras=(d_kvn_k,), epilogue=lambda acc, e: (acc + e,))

    def rms_bwd(lora, dq, dkv, gq, gkv):
        dxq, dgq = _rms_bwd(lora[:, :Q_LORA], gq, dq)
        dxk, dgk = _rms_bwd(lora[:, Q_LORA:], gkv, dkv)
        return jnp.concatenate([dxq, dxk], axis=1), dgq, dgk

    d_lora, d_g_q_a, d_g_kv_a = row(
        rms_bwd, name="mla_rms_bwd",
        ins=[(proj, LORA_W, lora0, "row"), (d_qn, Q_LORA, 0, "row"), (d_kvn, KV_LORA, 0, "row"), (g_q_a, 0, 0, "full"), (g_kv_a, 0, 0, "full")],
        outs=[(LORA_W, BF16)], sums=[Q_LORA, KV_LORA])
    d_proj = jnp.concatenate([d_gates, dq_d, dk_d, dv_d, d_lora, d_kr, jnp.zeros((rows, P_WIDTH - P_KR - LANES), BF16)], axis=1)
    d_w_in_p = mm(x, d_proj, mode="tn", name="d_w_in", tm=1024, tn=1536, tk=1024)
    grad_x = mm(d_proj, w_in_p, mode="nt", name="d_x", tm=1024, tn=1024, tk=1536, extras=(dr1,), epilogue=lambda acc, rv: (acc + ALPHA * rv,))

    grads.update(
        w_in=_split_cols(_unpad_w_in(d_w_in_p)), w_uq=_split_cols(_unpad_heads(d_w_uq_p, MLA_QK)), w_ukv=_split_cols(_merge_ukv(d_w_uk_p, d_w_uv_p)),
        b_gate=d_b_gate.reshape(2, D_MODEL), g_q_a=d_g_q_a, g_kv_a=d_g_kv_a, ln1_g=d_ln1_g, ln1_b=d_ln1_b, ln2_g=d_ln2_g, ln2_b=d_ln2_b)
    return loss_l, grad_x.reshape(batch, seq, D_MODEL), grads


BIG = (("w_in", (1024, 1064)), ("w_uq", (384, 192)), ("w_ukv", (256, 256)), ("w_o_mla", (512, 256)),
       ("w_o_dil", (512, 256)), ("w_out", (256, 1024)), ("w_ff1", (1024, 1024)), ("w_ff2", (1024, 1024)))
SMALL = (("b_gate", 2 * D_MODEL), ("g_q_a", Q_LORA), ("g_kv_a", KV_LORA), ("ln1_g", D_MODEL), ("ln1_b", D_MODEL),
         ("ln2_g", D_MODEL), ("ln2_b", D_MODEL))
N_BIG = len(BIG)
D2D_CHUNKS = 4
ANY = pl.BlockSpec(memory_space=pl.ANY)
SIDE_EFFECTS = pltpu.CompilerParams(has_side_effects=True)


def _place():
    x, y, c = lax.axis_index("x"), lax.axis_index("y"), lax.axis_index("c")
    return x, y, c, ((1 - x, y), (x, 1 - y), (1 - x, 1 - y))


def _rows(at, n):
    return pl.ds(pl.multiple_of(at, 16), n)


def _row_chunks(at, n):
    size = n // D2D_CHUNKS
    return [_rows(at + i * size, size) for i in range(D2D_CHUNKS)]


def _remote(src, dst, send, recv, to):
    return pltpu.make_async_remote_copy(src_ref=src, dst_ref=dst, send_sem=send, recv_sem=recv, device_id=to, device_id_type=MESH)


def _gather_weights(shards, name):
    n = len(shards)

    def body(*refs):
        srcs, outs, (send, recv) = refs[:n], refs[n:2 * n], refs[2 * n:]
        x, y, c, chips = _place()
        sibling = (x, y, 1 - c)
        halves = [s.shape[0] // 2 for s in shards]

        def over_ici(t, j, cx, cy, to):
            r = _rows(c * halves[t], halves[t])
            return _remote(srcs[t].at[r], outs[t].at[2 * cx + cy, r], send.at[t, j], recv.at[t, j], (*to, c))

        def over_d2d(t, j, slot, r):
            src = srcs[t].at[r] if j == 6 else outs[t].at[slot, r]
            return _remote(src, outs[t].at[slot, r], send.at[t, j], recv.at[t, j], sibling)

        for t in range(n):
            for j, chip in enumerate(chips):
                over_ici(t, j, x, y, chip).start()
        for t in range(n):
            for r in _row_chunks(0, 2 * halves[t]):
                over_d2d(t, 6, 2 * x + y, r).start()
        for t in range(n):
            for j, (cx, cy) in enumerate(chips):
                over_ici(t, j, cx, cy, (cx, cy)).wait_recv()
                for r in _row_chunks(c * halves[t], halves[t]):
                    over_d2d(t, 3 + j, 2 * cx + cy, r).start()
        for t in range(n):
            over_d2d(t, 6, 2 * x + y, _rows(0, 2 * halves[t])).wait()
            for j, (cx, cy) in enumerate(chips):
                over_d2d(t, 3 + j, 2 * cx + cy, _rows((1 - c) * halves[t], halves[t])).wait_recv()
                over_d2d(t, 3 + j, 2 * cx + cy, _rows(c * halves[t], halves[t])).wait_send()
                over_ici(t, j, x, y, (cx, cy)).wait_send()

    return pl.pallas_call(
        body, name=name, in_specs=[ANY] * n, out_specs=[ANY] * n,
        out_shape=[jax.ShapeDtypeStruct((N_CHIPS,) + s.shape, s.dtype) for s in shards],
        scratch_shapes=[pltpu.SemaphoreType.DMA((n, 7)), pltpu.SemaphoreType.DMA((n, 7))],
        compiler_params=SIDE_EFFECTS,
    )(*shards)


def _pair_split(grads, name):
    n = len(grads)

    def body(*refs):
        srcs, outs, (send, recv) = refs[:n], refs[n:2 * n], refs[2 * n:]
        x, y, c, _ = _place()
        for t in range(n):
            half = grads[t].shape[1] // 2
            give = _rows((1 - c) * half, half)
            for s in range(N_CHIPS):
                _remote(srcs[t].at[s, give], outs[t].at[s], send.at[t], recv.at[t], (x, y, 1 - c)).start()
        for t in range(n):
            half = grads[t].shape[1] // 2
            _remote(srcs[t].at[:, _rows((1 - c) * half, half)], outs[t], send.at[t], recv.at[t], (x, y, 1 - c)).wait()

    return pl.pallas_call(
        body, name=name, in_specs=[ANY] * n, out_specs=[ANY] * n,
        out_shape=[jax.ShapeDtypeStruct((N_CHIPS, g.shape[1] // 2, g.shape[2]), g.dtype) for g in grads],
        scratch_shapes=[pltpu.SemaphoreType.DMA((n,)), pltpu.SemaphoreType.DMA((n,))],
        compiler_params=SIDE_EFFECTS,
    )(*grads)


def _chip_exchange(parts, name):
    n = len(parts)

    def body(*refs):
        srcs, outs, (send, recv) = refs[:n], refs[n:2 * n], refs[2 * n:]
        x, y, c, chips = _place()
        copies = [_remote(srcs[t].at[2 * cx + cy], outs[t].at[j], send.at[t, j], recv.at[t, j], (cx, cy, c))
                  for t in range(n) for j, (cx, cy) in enumerate(chips)]
        for cp in copies:
            cp.start()
        for cp in copies:
            cp.wait()

    return pl.pallas_call(
        body, name=name, in_specs=[ANY] * n, out_specs=[ANY] * n,
        out_shape=[jax.ShapeDtypeStruct((3,) + p.shape[1:], p.dtype) for p in parts],
        scratch_shapes=[pltpu.SemaphoreType.DMA((n, 3)), pltpu.SemaphoreType.DMA((n, 3))],
        compiler_params=SIDE_EFFECTS,
    )(*parts)


def _pair_join(totals):
    n = len(totals)

    def body(*refs):
        srcs, outs, (send, recv) = refs[:n], refs[n:2 * n], refs[2 * n:]
        x, y, c, _ = _place()
        for t in range(n):
            for r in _row_chunks(0, totals[t].shape[0]):
                _remote(srcs[t].at[r], outs[t].at[r], send.at[t], recv.at[t], (x, y, 1 - c)).start()
        for t in range(n):
            _remote(srcs[t], outs[t], send.at[t], recv.at[t], (x, y, 1 - c)).wait()

    return pl.pallas_call(
        body, name="pair_join", in_specs=[ANY] * n, out_specs=[ANY] * n,
        out_shape=[jax.ShapeDtypeStruct(t.shape, t.dtype) for t in totals],
        scratch_shapes=[pltpu.SemaphoreType.DMA((n,)), pltpu.SemaphoreType.DMA((n,))],
        compiler_params=SIDE_EFFECTS,
    )(*totals)


HBM = pl.BlockSpec(memory_space=pltpu.HBM)
SEM = pl.BlockSpec(memory_space=pltpu.SEMAPHORE)
SPLIT = pltpu.CompilerParams(has_side_effects=pltpu.SideEffectType.DATAFLOW_SIDE_EFFECTING)


def _in_hbm(a):
    return pltpu.with_memory_space_constraint(a, pltpu.HBM)


def _split_copies(kind, srcs, lands):
    x, y, c, chips = _place()
    out = []
    for t in range(len(srcs)):
        for j, (cx, cy) in enumerate(chips):
            if kind == "gather":
                r = _rows(c * (srcs[t].shape[0] // 2), srcs[t].shape[0] // 2)
                out.append((t, j, srcs[t].at[r], lands[t].at[2 * x + y, r], (cx, cy, c)))
            else:
                out.append((t, j, srcs[t].at[2 * cx + cy], lands[t].at[j], (cx, cy, c)))
    return out


def _split_start(kind, srcs, land_shapes, name):
    n = len(srcs)

    def body(*refs):
        src_refs, land_refs, sems, token = refs[:n], refs[n:2 * n], refs[2 * n:2 * n + 6], refs[-1]
        for t, j, s, d, to in _split_copies(kind, src_refs, land_refs):
            _remote(s, d, sems[j], sems[3 + j], to).start()
        token[...] = jnp.zeros_like(token)

    lands = [_in_hbm(lax.empty(s.shape, s.dtype)) for s in land_shapes]
    thru = [pltpu.HBM(a.shape, a.dtype) for a in list(srcs) + lands]
    res = pl.pallas_call(
        body, name=name,
        out_shape=(*[pltpu.SemaphoreType.DMA(())] * 6, *thru, jax.ShapeDtypeStruct((8, LANES), F32)),
        in_specs=[HBM] * (2 * n), out_specs=(*[SEM] * 6, *[HBM] * (2 * n), pl.BlockSpec(memory_space=pltpu.VMEM)),
        input_output_aliases={i: 6 + i for i in range(2 * n)}, compiler_params=SPLIT,
    )(*[_in_hbm(s) for s in srcs], *lands)
    return res[:6], res[6:6 + n], res[6 + n:6 + 2 * n], res[-1]


def _split_wait(kind, sems, srcs, lands, after, name):
    n = len(srcs)

    def body(*refs):
        src_refs, land_refs, sem_refs = refs[:n], refs[n:2 * n], refs[2 * n:2 * n + 6]
        for t, j, s, d, to in _split_copies(kind, src_refs, land_refs):
            cp = _remote(s, d, sem_refs[j], sem_refs[3 + j], to)
            cp.wait_send()
            cp.wait_recv()

    res = pl.pallas_call(
        body, name=name, out_shape=[pltpu.HBM(a.shape, a.dtype) for a in list(srcs) + list(lands)],
        in_specs=[HBM] * (2 * n) + [SEM] * 6 + [ANY], out_specs=[HBM] * (2 * n),
        input_output_aliases={i: i for i in range(2 * n)}, compiler_params=SPLIT,
    )(*srcs, *lands, *sems, after)
    return res[:n], res[n:]


def _gather_finish(shards, lands, name):
    n = len(shards)

    def body(*refs):
        srcs, outs, (send, recv) = refs[:n], refs[2 * n:3 * n], refs[3 * n:]
        x, y, c, chips = _place()
        sibling = (x, y, 1 - c)
        for t in range(n):
            half = shards[t].shape[0] // 2
            for r in _row_chunks(0, 2 * half):
                _remote(srcs[t].at[r], outs[t].at[2 * x + y, r], send.at[t, 3], recv.at[t, 3], sibling).start()
            for j, (cx, cy) in enumerate(chips):
                for r in _row_chunks(c * half, half):
                    _remote(outs[t].at[2 * cx + cy, r], outs[t].at[2 * cx + cy, r], send.at[t, j], recv.at[t, j], sibling).start()
        for t in range(n):
            half = shards[t].shape[0] // 2
            mine, other = _rows(c * half, half), _rows((1 - c) * half, half)
            _remote(srcs[t], outs[t].at[2 * x + y], send.at[t, 3], recv.at[t, 3], sibling).wait()
            for j, (cx, cy) in enumerate(chips):
                slab = outs[t].at[2 * cx + cy]
                _remote(slab.at[mine], slab.at[mine], send.at[t, j], recv.at[t, j], sibling).wait_send()
                _remote(slab.at[other], slab.at[other], send.at[t, j], recv.at[t, j], sibling).wait_recv()

    return pl.pallas_call(
        body, name=name, in_specs=[ANY] * (2 * n), out_specs=[ANY] * n,
        out_shape=[jax.ShapeDtypeStruct(a.shape, a.dtype) for a in lands],
        scratch_shapes=[pltpu.SemaphoreType.DMA((n, 4)), pltpu.SemaphoreType.DMA((n, 4))],
        input_output_aliases={n + t: t for t in range(n)}, compiler_params=SIDE_EFFECTS,
    )(*shards, *lands)


def _sum_all_devices(vec, name):
    n_rows = vec.shape[0]

    def body(v_ref, out_ref, buf, send, recv):
        x, y, c, _ = _place()
        me = 4 * x + 2 * y + c
        buf[me] = v_ref[...]
        flips = [(a, b, d) for a in (0, 1) for b in (0, 1) for d in (0, 1)][1:]
        copies = []
        for r, (a, b, d) in enumerate(flips):
            px, py, pc = (1 - x if a else x), (1 - y if b else y), (1 - c if d else c)
            copies.append(pltpu.make_async_remote_copy(src_ref=v_ref, dst_ref=buf.at[me], send_sem=send.at[r], recv_sem=recv.at[r],
                                                       device_id=(px, py, pc), device_id_type=MESH))
            copies[-1].start()
        for r, (a, b, d) in enumerate(flips):
            px, py, pc = (1 - x if a else x), (1 - y if b else y), (1 - c if d else c)
            pltpu.make_async_remote_copy(src_ref=v_ref, dst_ref=buf.at[4 * px + 2 * py + pc], send_sem=send.at[r], recv_sem=recv.at[r],
                                         device_id=(px, py, pc), device_id_type=MESH).wait_recv()
        for cp in copies:
            cp.wait_send()
        total = buf[0]
        for k in range(1, N_DEV):
            total = total + buf[k]
        out_ref[...] = total

    vmem = pl.BlockSpec(memory_space=pltpu.VMEM)
    return pl.pallas_call(
        body, name=name, in_specs=[vmem], out_specs=vmem, out_shape=jax.ShapeDtypeStruct(vec.shape, F32),
        scratch_shapes=[pltpu.VMEM((N_DEV, n_rows, LANES), F32), pltpu.SemaphoreType.DMA((N_DEV - 1,)), pltpu.SemaphoreType.DMA((N_DEV - 1,))],
        compiler_params=pltpu.CompilerParams(has_side_effects=True),
    )(vec)


def _half_tile(half, width):
    t = half
    while t * width * 4 > (2 << 20) and t % 32 == 0:
        t //= 2
    return t


def _pair_add(g, theirs, core, name):
    _, half, width = theirs.shape
    t = _half_tile(half, width)
    n = half // t

    def body(c_ref, a_ref, b_ref, o_ref):
        o_ref[...] = (a_ref[...] + b_ref[...]).astype(BF16)

    tile = pl.BlockSpec((1, t, width), lambda j, i, c_ref: (j, i, 0))
    return pl.pallas_call(
        body, name=name,
        grid_spec=pltpu.PrefetchScalarGridSpec(
            num_scalar_prefetch=1, grid=(N_CHIPS, n),
            in_specs=[pl.BlockSpec((1, t, width), lambda j, i, c_ref: (j, c_ref[0] * n + i, 0)), tile], out_specs=tile),
        out_shape=jax.ShapeDtypeStruct(theirs.shape, BF16), compiler_params=_params(("parallel", "parallel")),
    )(core, g, theirs)


def _chip_sum(part, others, chip, name):
    _, half, width = part.shape
    t = _half_tile(half, width)

    def body(s_ref, mine, p0, p1, p2, o_ref):
        o_ref[...] = ((mine[0].astype(F32) + p0[0].astype(F32)) + p1[0].astype(F32)) + p2[0].astype(F32)

    return pl.pallas_call(
        body, name=name,
        grid_spec=pltpu.PrefetchScalarGridSpec(
            num_scalar_prefetch=1, grid=(half // t,),
            in_specs=[pl.BlockSpec((1, t, width), lambda i, s_ref: (s_ref[0], i, 0))]
            + [pl.BlockSpec((1, t, width), lambda i, s_ref, j=j: (j, i, 0)) for j in range(3)],
            out_specs=pl.BlockSpec((t, width), lambda i, s_ref: (i, 0))),
        out_shape=jax.ShapeDtypeStruct((half, width), F32), compiler_params=_params(("parallel",)),
    )(chip, part, others, others, others)


EARLY = ("w_in", "w_uq", "w_ukv")
LATE = ("w_o_mla", "w_o_dil", "w_out", "w_ff1", "w_ff2")


def _chip_partials(grads, names, core, tag):
    gs = [grads[n] for n in names]
    theirs = _pair_split(gs, "pair_split_" + tag)
    return [_pair_add(g, th, core, "pair_add_" + n) for g, th, n in zip(gs, theirs, names)]


def _finish_reduce(parts, others, chip):
    names = [n for n, _ in BIG]
    totals = [_chip_sum(parts[n], others[n], chip, "chip_sum_" + n) for n in names]
    return dict(zip(names, zip(totals, _pair_join(totals))))


def _sum_small(vals):
    n_in = len(vals)
    n_rows = sum(a.shape[0] * a.shape[1] // LANES for a in vals)
    pad_rows = -(-n_rows // 8) * 8

    def chunks(refs):
        return [(ref, a, j) for ref in refs for a in range(ref.shape[0]) for j in range(ref.shape[1] // LANES)]

    def body(*refs):
        ins, outs, (buf, send, recv) = refs[:n_in], refs[n_in:2 * n_in], refs[2 * n_in:]
        x, y, c, _ = _place()
        me = 4 * x + 2 * y + c
        for r, (ref, a, j) in enumerate(chunks(ins)):
            buf[me, r:r + 1, :] = ref[a:a + 1, j * LANES:(j + 1) * LANES]
        if pad_rows > n_rows:
            buf[me, n_rows:pad_rows, :] = jnp.zeros((pad_rows - n_rows, LANES), F32)
        flips = [(a, b, d) for a in (0, 1) for b in (0, 1) for d in (0, 1)][1:]
        peers = [((1 - x if a else x), (1 - y if b else y), (1 - c if d else c)) for a, b, d in flips]
        copies = [_remote(buf.at[me], buf.at[me], send.at[r], recv.at[r], peer) for r, peer in enumerate(peers)]
        for cp in copies:
            cp.start()
        for r, (px, py, pc) in enumerate(peers):
            _remote(buf.at[me], buf.at[4 * px + 2 * py + pc], send.at[r], recv.at[r], (px, py, pc)).wait_recv()
        for cp in copies:
            cp.wait_send()
        total = buf[0]
        for k in range(1, N_DEV):
            total = total + buf[k]
        for r, (ref, a, j) in enumerate(chunks(outs)):
            ref[a:a + 1, j * LANES:(j + 1) * LANES] = total[r:r + 1, :]

    vmem = pl.BlockSpec(memory_space=pltpu.VMEM)
    return pl.pallas_call(
        body, name="sum_small", in_specs=[vmem] * n_in, out_specs=[vmem] * n_in,
        out_shape=[jax.ShapeDtypeStruct(a.shape, F32) for a in vals],
        scratch_shapes=[pltpu.VMEM((N_DEV, pad_rows, LANES), F32), pltpu.SemaphoreType.DMA((N_DEV - 1,)), pltpu.SemaphoreType.DMA((N_DEV - 1,))],
        compiler_params=SIDE_EFFECTS,
    )(*vals)


def _adam_math(w, g, m, v):
    nm = B1 * m + (1.0 - B1) * g
    nv = B2 * v + (1.0 - B2) * (g * g)
    m_hat = nm / (1.0 - B1 ** ADAM_STEP)
    v_hat = nv / (1.0 - B2 ** ADAM_STEP)
    return -LR * (m_hat / (jnp.sqrt(v_hat) + ADAM_EPS) + WD * w), nm, nv


def _adamw_big(w, mine, theirs, m, v, core, name, side_by_side=False):
    rows, width = w.shape
    if side_by_side:
        t = next(c for c in (152, 96, 64, 32, 16, 8) if rows % c == 0)
        hb = None
        half_spec = pl.BlockSpec((t, width // 2), lambda i, c_ref: (i, 0))
    else:
        t = next(c for c in (256, 128, 64, 32, 16, 8) if (rows // 2) % c == 0)
        hb = rows // 2 // t
        half_spec = pl.BlockSpec((t, width), lambda i, c_ref: (i % hb, 0))

    def body(c_ref, w_ref, a_ref, b_ref, m_ref, v_ref, g_ref, d_ref, nm_ref, nv_ref):
        south = c_ref[0] == 0
        if side_by_side:
            g = jnp.where(south, jnp.concatenate([a_ref[...], b_ref[...]], axis=1), jnp.concatenate([b_ref[...], a_ref[...]], axis=1))
        else:
            g = jnp.where((pl.program_id(0) < hb) == south, a_ref[...], b_ref[...])
        g_ref[...] = g
        d_ref[...], nm_ref[...], nv_ref[...] = _adam_math(w_ref[...], g, m_ref[...], v_ref[...])

    spec = pl.BlockSpec((t, width), lambda i, c_ref: (i, 0))
    return pl.pallas_call(
        body, name=name,
        grid_spec=pltpu.PrefetchScalarGridSpec(num_scalar_prefetch=1, grid=(rows // t,),
                                               in_specs=[spec, half_spec, half_spec, spec, spec], out_specs=[spec] * 4),
        out_shape=[jax.ShapeDtypeStruct(w.shape, F32)] * 4, compiler_params=_params(("parallel",)),
    )(core, w, mine, theirs, m, v)


def _adamw_small(ws, gs, ms, vs):
    n = len(ws)

    def body(*refs):
        for t in range(n):
            w_ref, g_ref, m_ref, v_ref = (refs[k * n + t] for k in range(4))
            d, nm, nv = _adam_math(w_ref[...], g_ref[...], m_ref[...], v_ref[...])
            refs[4 * n + t][...] = d
            refs[5 * n + t][...] = nm
            refs[6 * n + t][...] = nv

    vmem = pl.BlockSpec(memory_space=pltpu.VMEM)
    res = pl.pallas_call(body, name="adamw_small", in_specs=[vmem] * (4 * n), out_specs=[vmem] * (3 * n),
                         out_shape=[jax.ShapeDtypeStruct(a.shape, F32) for a in ws] * 3)(*ws, *gs, *ms, *vs)
    return res[:n], res[n:2 * n], res[2 * n:]


def kernel(x, w_in, b_gate, g_q_a, w_uq, g_kv_a, w_ukv, w_o_mla, w_o_dil, w_out, ln1_g, ln1_b, w_ff1, w_ff2, ln2_g, ln2_b, loss_target, m_w_in, m_b_gate, m_g_q_a, m_w_uq, m_g_kv_a, m_w_ukv, m_w_o_mla, m_w_o_dil, m_w_out, m_ln1_g, m_ln1_b, m_w_ff1, m_w_ff2, m_ln2_g, m_ln2_b, v_w_in, v_b_gate, v_g_q_a, v_w_uq, v_g_kv_a, v_w_ukv, v_w_o_mla, v_w_o_dil, v_w_out, v_ln1_g, v_ln1_b, v_w_ff1, v_w_ff2, v_ln2_g, v_ln2_b):
    order = ("w_in", "b_gate", "g_q_a", "w_uq", "g_kv_a", "w_ukv", "w_o_mla", "w_o_dil", "w_out", "ln1_g", "ln1_b", "w_ff1", "w_ff2", "ln2_g", "ln2_b")
    w = dict(w_in=w_in, b_gate=b_gate, g_q_a=g_q_a, w_uq=w_uq, g_kv_a=g_kv_a, w_ukv=w_ukv, w_o_mla=w_o_mla, w_o_dil=w_o_dil, w_out=w_out,
             ln1_g=ln1_g, ln1_b=ln1_b, w_ff1=w_ff1, w_ff2=w_ff2, ln2_g=ln2_g, ln2_b=ln2_b)
    m = dict(w_in=m_w_in, b_gate=m_b_gate, g_q_a=m_g_q_a, w_uq=m_w_uq, g_kv_a=m_g_kv_a, w_ukv=m_w_ukv, w_o_mla=m_w_o_mla, w_o_dil=m_w_o_dil,
             w_out=m_w_out, ln1_g=m_ln1_g, ln1_b=m_ln1_b, w_ff1=m_w_ff1, w_ff2=m_w_ff2, ln2_g=m_ln2_g, ln2_b=m_ln2_b)
    v = dict(w_in=v_w_in, b_gate=v_b_gate, g_q_a=v_g_q_a, w_uq=v_w_uq, g_kv_a=v_g_kv_a, w_ukv=v_w_ukv, w_o_mla=v_w_o_mla, w_o_dil=v_w_o_dil,
             w_out=v_w_out, ln1_g=v_ln1_g, ln1_b=v_ln1_b, w_ff1=v_w_ff1, w_ff2=v_w_ff2, ln2_g=v_ln2_g, ln2_b=v_ln2_b)
    chip = 2 * lax.axis_index("x") + lax.axis_index("y")
    south = (lax.axis_index("c") == 0).astype(F32)
    gate_w = D_MODEL // N_CHIPS

    core = lax.axis_index("c").astype(jnp.int32).reshape(1)
    shards = {n: w[n][0].astype(BF16) for n, _ in BIG}
    first = dict(zip(EARLY, _gather_weights([shards[n] for n in EARLY], "gather_early")))
    late_shards = [shards[n] for n in LATE]
    g_sems, g_srcs, g_lands, g_token = _split_start(
        "gather", late_shards, [jax.ShapeDtypeStruct((N_CHIPS,) + s.shape, BF16) for s in late_shards], "gather_late_start")
    b_mine = lax.dynamic_update_slice(jnp.zeros((2, D_MODEL), F32), b_gate[0] * south, (0, chip * gate_w))
    b_full = _sum_all_devices(b_mine.reshape(-1, LANES), "gather_b_gate").reshape(2, D_MODEL)

    def late_weights(after):
        srcs, lands = _split_wait("gather", g_sems, g_srcs, g_lands, after, "gather_late_wait")
        return dict(zip(LATE, _gather_finish(srcs, lands, "gather_late_finish")))

    sent = {}

    def early_grads(grads_late):
        parts_late = _chip_partials(grads_late, LATE, core, "late")
        shapes = [jax.ShapeDtypeStruct((3,) + p.shape[1:], BF16) for p in parts_late]
        sent["sems"], sent["srcs"], sent["lands"], token = _split_start("scatter", parts_late, shapes, "exchange_late_start")
        return token

    loss_part, grad_x, grads = _local_step(x, loss_target, first, b_full, g_q_a, g_kv_a, ln1_g, ln1_b, ln2_g, ln2_b,
                                           token=g_token, late_weights=late_weights, early_grads=early_grads)

    parts_late, others_late = _split_wait("scatter", sent["sems"], sent["srcs"], sent["lands"], grad_x, "exchange_late_wait")
    parts, others = dict(zip(LATE, parts_late)), dict(zip(LATE, others_late))
    parts_early = _chip_partials(grads, EARLY, core, "early")
    parts.update(zip(EARLY, parts_early))
    others.update(zip(EARLY, _chip_exchange(parts_early, "exchange_early")))
    halves = _finish_reduce(parts, others, chip.astype(jnp.int32).reshape(1))

    g_out, delta, new_m, new_v = {}, {}, {}, {}
    for name, _ in BIG:
        flip = name in ("w_in", "w_uq")
        turn = (lambda a: a.T) if flip else (lambda a: a)
        res = _adamw_big(turn(w[name][0]), turn(halves[name][0]), turn(halves[name][1]), turn(m[name][0]), turn(v[name][0]), core,
                         "adamw_" + name, side_by_side=flip)
        g_out[name], delta[name], new_m[name], new_v[name] = (turn(r) for r in res)

    small_names = [name for name, _ in SMALL]
    sums = _sum_small([grads[name] for name in small_names] + [loss_part])
    loss = sums[-1][0, 0]
    g_small = dict(zip(small_names, sums))
    g_small["b_gate"] = lax.dynamic_slice(g_small["b_gate"], (0, chip * gate_w), (2, gate_w))
    flat = lambda a: a.reshape(-1, a.shape[-1])
    res = _adamw_small(*[[flat(d[name]) for name in small_names] for d in (w, g_small, m, v)])
    g_out.update(g_small)
    for d, r in zip((delta, new_m, new_v), res):
        d.update(zip(small_names, r))

    lead = lambda d: [d[name].reshape(w[name].shape) for name in order]
    return (loss, grad_x, *lead(g_out), *lead(delta), *lead(new_m), *lead(new_v))
```

```python
import functools
import math

import jax
import jax.numpy as jnp
from jax import lax
from jax.experimental import pallas as pl
from jax.experimental.pallas import tpu as pltpu

F32 = jnp.float32
BF16 = jnp.bfloat16
MESH = pl.DeviceIdType.MESH

D_MODEL = 1024
N_HEADS = 8
LANES = 128
NOPE, ROPE, V_DIM = 64, 32, 64
MLA_QK = NOPE + ROPE
Q_LORA, KV_LORA = 384, 256
DIL_DIM = 64
DIL_PATTERNS = ((128, 1), (512, 4), (2048, 16))
D_FF = 4096
N_CHIPS = 4
N_DEV = 8
IN_WIDTH = 4256
LN_EPS, RMS_EPS = 1e-5, 1e-6
NEG = -1e30
ALPHA = 2.0 ** 0.25
ROPE_THETA = 10000.0
LR, B1, B2, ADAM_EPS, WD, ADAM_STEP = 0.001, 0.9, 0.999, 1e-8, 0.01, 10

P_GATE, P_QD, P_KD, P_VD, P_LORA, P_KR, P_WIDTH = 0, 2048, 3072, 4096, 5120, 5760, 6144
LORA_W = Q_LORA + KV_LORA
KR_LANE = NOPE

ATT_T = 512
ROW_T = 512
VMEM_LIMIT = 56 * 1024 * 1024

NN = (((1,), (0,)), ((), ()))
NT = (((1,), (1,)), ((), ()))
TN = (((0,), (0,)), ((), ()))


def _params(sem=None, **kw):
    return pltpu.CompilerParams(dimension_semantics=sem, vmem_limit_bytes=VMEM_LIMIT, **kw)


def _matmul(a, b, *, mode, name, tm, tn, tk, out_dtypes=(F32,), extras=(), epilogue=None, b_shards=False, out_shards=False, after=None):
    if b_shards:
        n_sh, rows_b, cols_b = b.shape
        b_shape = (rows_b, n_sh * cols_b)
    else:
        b_shape = b.shape
    if mode == "nn":
        (m, k), (k2, n) = a.shape, b_shape
    elif mode == "nt":
        (m, k), (n, k2) = a.shape, b_shape
    else:
        (k, m), (k2, n) = a.shape, b_shape
    assert k == k2, (a.shape, b.shape, mode)
    tm, tn, tk = min(tm, m), min(tn, n), min(tk, k)
    assert m % tm == 0 and n % tn == 0 and k % tk == 0, (name, m, n, k, tm, tn, tk)
    nk = k // tk
    n_ex, n_out = len(extras), len(out_dtypes)
    n_in = 2 + n_ex + (after is not None)
    dims = {"nn": NN, "nt": NT, "tn": TN}[mode]

    def body(*refs):
        a_ref, b_ref = refs[:2]
        ex_refs = refs[2:2 + n_ex]
        out_refs = refs[n_in:n_in + n_out]
        part = lax.dot_general(a_ref[...].astype(BF16), b_ref[...].astype(BF16), dims, preferred_element_type=F32)

        def finish(acc):
            outs = epilogue(acc, *[r[...] for r in ex_refs]) if epilogue is not None else (acc,)
            for r, o in zip(out_refs, outs):
                r[...] = o.astype(r.dtype)

        if nk == 1:
            finish(part)
        else:
            acc_ref = refs[-1]
            kk = pl.program_id(2)

            @pl.when(kk == 0)
            def _():
                acc_ref[...] = part

            @pl.when(kk > 0)
            def _():
                acc_ref[...] += part

            @pl.when(kk == nk - 1)
            def _():
                finish(acc_ref[...])

    a_spec = {"nn": pl.BlockSpec((tm, tk), lambda i, j, kk: (i, kk)),
              "nt": pl.BlockSpec((tm, tk), lambda i, j, kk: (i, kk)),
              "tn": pl.BlockSpec((tk, tm), lambda i, j, kk: (kk, i))}[mode]
    b_spec = {"nn": pl.BlockSpec((tk, tn), lambda i, j, kk: (kk, j)),
              "nt": pl.BlockSpec((tn, tk), lambda i, j, kk: (j, kk)),
              "tn": pl.BlockSpec((tk, tn), lambda i, j, kk: (kk, j))}[mode]
    tile = pl.BlockSpec((tm, tn), lambda i, j, kk: (i, j))
    out_spec, out_dims = tile, (m, n)
    if b_shards and mode == "nn":
        per = cols_b // tn
        b_spec = pl.BlockSpec((None, tk, tn), lambda i, j, kk: (j // per, kk, j % per))
    elif b_shards:
        assert mode == "nt"
        per = cols_b // tk
        b_spec = pl.BlockSpec((None, tn, tk), lambda i, j, kk: (kk // per, j, kk % per))
    if out_shards:
        assert not extras and epilogue is None
        per_out = n // N_CHIPS // tn
        out_spec = pl.BlockSpec((None, tm, tn), lambda i, j, kk: (j // per_out, i, j % per_out))
        out_dims = (N_CHIPS, m, n // N_CHIPS)
    outs = pl.pallas_call(
        body, name=name,
        grid=(m // tm, n // tn, nk),
        in_specs=[a_spec, b_spec] + [tile] * n_ex + [pl.BlockSpec(memory_space=pl.ANY)] * (after is not None),
        out_specs=[out_spec] * n_out,
        out_shape=[jax.ShapeDtypeStruct(out_dims, dt) for dt in out_dtypes],
        scratch_shapes=[pltpu.VMEM((tm, tn), F32)] if nk > 1 else [],
        compiler_params=_params(("parallel", "parallel", "arbitrary")),
    )(a, b, *extras, *([after] if after is not None else []))
    return outs[0] if n_out == 1 else outs


def _rowwise(fn, *, name, rows, seq, ins, outs, sums=()):
    tm = min(ROW_T, seq)
    n_pos = seq // tm
    n_in, n_out, n_sum = len(ins), len(outs), len(sums)

    def body(*refs):
        vals = fn(*[r[...] for r in refs[:n_in]])
        for r, v in zip(refs[n_in:n_in + n_out], vals[:n_out]):
            r[...] = v.astype(r.dtype)
        first = pl.program_id(0) == 0
        for r, v in zip(refs[n_in + n_out:], vals[n_out:]):
            @pl.when(first)
            def _(r=r, v=v):
                r[...] = v

            @pl.when(jnp.logical_not(first))
            def _(r=r, v=v):
                r[...] += v

    def spec(arr, width, col, kind):
        if kind == "row":
            return pl.BlockSpec((tm, width), lambda i, col=col: (i, col))
        if kind == "pos":
            return pl.BlockSpec((tm, width), lambda i, col=col: (i % n_pos, col))
        return pl.BlockSpec(arr.shape, lambda i: (0,) * arr.ndim)

    res = pl.pallas_call(
        body, name=name,
        grid=(rows // tm,),
        in_specs=[spec(*t) for t in ins],
        out_specs=[pl.BlockSpec((tm, w), lambda i: (i, 0)) for w, _ in outs]
        + [pl.BlockSpec((1, w), lambda i: (0, 0)) for w in sums],
        out_shape=[jax.ShapeDtypeStruct((rows, w), dt) for w, dt in outs]
        + [jax.ShapeDtypeStruct((1, w), F32) for w in sums],
        compiler_params=_params(("arbitrary",)),
    )(*[t[0] for t in ins])
    return res


def _colsum(v):
    return jnp.sum(v, axis=0, keepdims=True)


def _rope_fwd(t, c, s_up, s_dn):
    return t * c + pltpu.roll(t, LANES - 16, 1) * s_up + pltpu.roll(t, 16, 1) * s_dn


def _rope_bwd(d, c, s_up, s_dn):
    return d * c + pltpu.roll(d * s_up, 16, 1) + pltpu.roll(d * s_dn, LANES - 16, 1)


def _rope_tables(seq):
    half = ROPE // 2
    inv = jnp.power(ROPE_THETA, -jnp.arange(half, dtype=F32) / half)
    ang = jnp.arange(seq, dtype=F32)[:, None] * inv[None, :]
    cos, sin = jnp.cos(ang), jnp.sin(ang)
    zeros = jnp.zeros((seq, half), F32)
    lo, hi = jnp.ones((seq, KR_LANE), F32), jnp.ones((seq, LANES - KR_LANE - ROPE), F32)
    c = jnp.concatenate([lo, cos, cos, hi], axis=1)
    c_rope_only = jnp.concatenate([0 * lo, cos, cos, 0 * hi], axis=1)
    s_up = jnp.concatenate([0 * lo, -sin, zeros, 0 * hi], axis=1)
    s_dn = jnp.concatenate([0 * lo, zeros, sin, 0 * hi], axis=1)
    return c, s_up, s_dn, c_rope_only


def _rms(x, g):
    r = lax.rsqrt(jnp.mean(x * x, axis=1, keepdims=True) + RMS_EPS)
    return x * r * g


def _rms_bwd(x, g, dy):
    r = lax.rsqrt(jnp.mean(x * x, axis=1, keepdims=True) + RMS_EPS)
    xh = x * r
    dxh = dy * g
    dx = r * (dxh - xh * jnp.mean(dxh * xh, axis=1, keepdims=True))
    return dx, _colsum(dy * xh)


def _ln_stats(x):
    mu = jnp.mean(x, axis=1, keepdims=True)
    xc = x - mu
    r = lax.rsqrt(jnp.mean(xc * xc, axis=1, keepdims=True) + LN_EPS)
    return xc * r, r


def _ln_bwd(xh, r, g, dy):
    dxh = dy * g
    dx = r * (dxh - jnp.mean(dxh, axis=1, keepdims=True) - xh * jnp.mean(dxh * xh, axis=1, keepdims=True))
    return dx, _colsum(dy * xh), _colsum(dy)


def _bias_spec(bias):
    per_head = bias.shape[0] > 1
    return pl.BlockSpec((1,) + bias.shape[1:], lambda b, h: (h if per_head else 0, 0, 0, 0))


def _attn_fwd(q, qb0, k, kb0, v, vb0, bias, scale, *, name, batch, seq):
    t = ATT_T
    nq = seq // t
    rows = batch * seq

    def body(q_ref, k_ref, v_ref, bias_ref, o_ref, lse_ref, qb, kb, vb):
        qb[...] = q_ref[...].astype(BF16)
        kb[...] = k_ref[...].astype(BF16)
        vb[...] = v_ref[...].astype(BF16)
        for i in range(nq):
            qt = qb[i * t:(i + 1) * t, :]
            logits = [lax.dot_general(qt, kb[j * t:(j + 1) * t, :], NT, preferred_element_type=F32) * scale + bias_ref[0, i - j]
                      for j in range(i + 1)]
            top = functools.reduce(jnp.maximum, logits)
            m = jnp.max(top, axis=1, keepdims=True)
            ps = [jnp.exp(s - m) for s in logits]
            l = jnp.sum(functools.reduce(jnp.add, ps), axis=1, keepdims=True)
            acc = functools.reduce(jnp.add, [lax.dot_general(p.astype(BF16), vb[j * t:(j + 1) * t, :], NN, preferred_element_type=F32)
                                             for j, p in enumerate(ps)])
            o_ref[i * t:(i + 1) * t, :] = acc / l
            lse_ref[i * t:(i + 1) * t, :] = jnp.broadcast_to(m + jnp.log(l), (t, LANES))

    slab = lambda b0: pl.BlockSpec((seq, LANES), lambda b, h, b0=b0: (b, b0 + h))
    return pl.pallas_call(
        body, name=name,
        grid=(batch, N_HEADS),
        in_specs=[slab(qb0), slab(kb0), slab(vb0), _bias_spec(bias)],
        out_specs=[slab(0), slab(0)],
        out_shape=[jax.ShapeDtypeStruct((rows, N_HEADS * LANES), F32)] * 2,
        scratch_shapes=[pltpu.VMEM((seq, LANES), BF16)] * 3,
        compiler_params=_params(("arbitrary", "arbitrary")),
    )(q, k, v, bias)


def _attn_bwd(q, qb0, k, kb0, v, vb0, o, do, lse, bias, scale, *, name, batch, seq, out_dtype, after=None):
    t = ATT_T
    nq = seq // t
    rows = batch * seq

    def body(q_ref, k_ref, v_ref, o_ref, do_ref, lse_ref, bias_ref, *rest):
        dq_ref, dk_ref, dv_ref, qb, kb, vb, dob, qtb, dotb, dka, dva = rest[after is not None:]
        qb[...] = q_ref[...].astype(BF16)
        kb[...] = k_ref[...].astype(BF16)
        vb[...] = v_ref[...].astype(BF16)
        dob[...] = do_ref[...].astype(BF16)
        qtb[...] = q_ref[...].astype(F32).T.astype(BF16)
        dotb[...] = do_ref[...].T.astype(BF16)
        for i in range(nq):
            at = slice(i * t, (i + 1) * t)
            qt, dot = qb[at, :], dob[at, :]
            lse_t = lse_ref[at, 0:1]
            delta = jnp.sum(o_ref[at, :] * do_ref[at, :], axis=1, keepdims=True)
            dq = None
            for j in range(i + 1):
                kat = slice(j * t, (j + 1) * t)
                kt, vt = kb[kat, :], vb[kat, :]
                p = jnp.exp(lax.dot_general(qt, kt, NT, preferred_element_type=F32) * scale + bias_ref[0, i - j] - lse_t)
                dp = lax.dot_general(dot, vt, NT, preferred_element_type=F32)
                ds = (p * (dp - delta) * scale).astype(BF16)
                dk_part = lax.dot_general(qtb[:, at], ds, NN, preferred_element_type=F32)
                dv_part = lax.dot_general(dotb[:, at], p.astype(BF16), NN, preferred_element_type=F32)
                if i == j:
                    dka[:, kat] = dk_part
                    dva[:, kat] = dv_part
                else:
                    dka[:, kat] += dk_part
                    dva[:, kat] += dv_part
                dq_part = lax.dot_general(ds, kt, NN, preferred_element_type=F32)
                dq = dq_part if dq is None else dq + dq_part
            dq_ref[at, :] = dq.astype(dq_ref.dtype)
        dk_ref[...] = dka[...].T.astype(dk_ref.dtype)
        dv_ref[...] = dva[...].T.astype(dv_ref.dtype)

    slab = lambda b0: pl.BlockSpec((seq, LANES), lambda b, h, b0=b0: (b, b0 + h))
    return pl.pallas_call(
        body, name=name,
        grid=(batch, N_HEADS),
        in_specs=[slab(qb0), slab(kb0), slab(vb0), slab(0), slab(0), slab(0), _bias_spec(bias)] + [pl.BlockSpec(memory_space=pl.ANY)] * (after is not None),
        out_specs=[slab(0)] * 3,
        out_shape=[jax.ShapeDtypeStruct((rows, N_HEADS * LANES), out_dtype)] * 3,
        scratch_shapes=[pltpu.VMEM((seq, LANES), BF16)] * 4 + [pltpu.VMEM((LANES, seq), BF16)] * 2 + [pltpu.VMEM((LANES, seq), F32)] * 2,
        compiler_params=_params(("arbitrary", "arbitrary")),
    )(q, k, v, o, do, lse, bias, *([after] if after is not None else []))


def _tile_dist(seq):
    n = seq // ATT_T
    d = jnp.arange(n, dtype=jnp.int32)[:, None, None] * ATT_T
    return d + jnp.arange(ATT_T, dtype=jnp.int32)[None, :, None] - jnp.arange(ATT_T, dtype=jnp.int32)[None, None, :]


def _causal_bias(seq):
    return jnp.where(_tile_dist(seq) >= 0, 0.0, NEG).astype(F32)[None]


def _dilated_bias(seq):
    dist = _tile_dist(seq)
    count = jnp.zeros(dist.shape, F32)
    for window, dilation in DIL_PATTERNS:
        count += ((dist >= 0) & (dist <= window) & (dist % dilation == 0)).astype(F32)
    slopes = jnp.asarray([2.0 ** (-8.0 * (i + 1) / N_HEADS) for i in range(N_HEADS)], F32)
    alibi = -slopes[:, None, None, None] * dist.astype(F32)[None]
    return jnp.where(count[None] > 0, jnp.log(jnp.maximum(count, 1.0))[None] + alibi, NEG).astype(F32)


def _pad_heads(w, width):
    kdim, n = w.shape[0], w.shape[1] // width
    return jnp.pad(w.reshape(kdim, n, width), ((0, 0), (0, 0), (0, LANES - width))).reshape(kdim, n * LANES)


def _unpad_heads(w, width):
    kdim, n = w.shape[0], w.shape[1] // LANES
    return w.reshape(kdim, n, LANES)[:, :, :width].reshape(kdim, n * width)


def _pad_w_in(w):
    q_a_kv_a = w[:, :LORA_W]
    k_r = jnp.pad(w[:, LORA_W:LORA_W + ROPE], ((0, 0), (KR_LANE, LANES - KR_LANE - ROPE)))
    qkv_d = _pad_heads(w[:, LORA_W + ROPE:LORA_W + ROPE + 3 * N_HEADS * DIL_DIM], DIL_DIM)
    gates = w[:, LORA_W + ROPE + 3 * N_HEADS * DIL_DIM:]
    tail = jnp.zeros((w.shape[0], P_WIDTH - P_KR - LANES), w.dtype)
    return jnp.concatenate([gates, qkv_d, q_a_kv_a, k_r, tail], axis=1)


def _unpad_w_in(g):
    gates = g[:, P_GATE:P_QD]
    qkv_d = _unpad_heads(g[:, P_QD:P_LORA], DIL_DIM)
    q_a_kv_a = g[:, P_LORA:P_KR]
    k_r = g[:, P_KR + KR_LANE:P_KR + KR_LANE + ROPE]
    return jnp.concatenate([q_a_kv_a, k_r, qkv_d, gates], axis=1)


def _split_ukv(w):
    w3 = w.reshape(w.shape[0], N_HEADS, NOPE + V_DIM)
    return (_pad_heads(w3[:, :, :NOPE].reshape(w.shape[0], -1), NOPE),
            _pad_heads(w3[:, :, NOPE:].reshape(w.shape[0], -1), V_DIM))


def _merge_ukv(g_k, g_v):
    kdim = g_k.shape[0]
    k3 = _unpad_heads(g_k, NOPE).reshape(kdim, N_HEADS, NOPE)
    v3 = _unpad_heads(g_v, V_DIM).reshape(kdim, N_HEADS, V_DIM)
    return jnp.concatenate([k3, v3], axis=2).reshape(kdim, N_HEADS * (NOPE + V_DIM))


def _pad_rows(w, width):
    return _pad_heads(w.T, width).T


def _unpad_rows(g, width):
    return _unpad_heads(g.T, width).T


def _join_cols(w):
    return w.transpose(1, 0, 2).reshape(w.shape[1], N_CHIPS * w.shape[2])


def _split_cols(g):
    return g.reshape(g.shape[0], N_CHIPS, g.shape[1] // N_CHIPS).transpose(1, 0, 2)


def _local_step(x3, target3, wg, b_gate, g_q_a, g_kv_a, ln1_g, ln1_b, ln2_g, ln2_b, token=None, late_weights=None, early_grads=None, early_grads_go=None):
    w_in, w_uq, w_ukv = (_join_cols(wg[n]) for n in ("w_in", "w_uq", "w_ukv"))
    batch, seq, _ = x3.shape
    rows = batch * seq
    x = x3.reshape(rows, D_MODEL)
    target = target3.reshape(rows, D_MODEL)
    row = functools.partial(_rowwise, rows=rows, seq=seq)
    mm = _matmul

    w_in_p = _pad_w_in(w_in)
    w_uq_p = _pad_heads(w_uq, MLA_QK)
    w_uk_p, w_uv_p = _split_ukv(w_ukv)
    b0, b1 = b_gate[0:1], b_gate[1:2]
    rope_c, rope_up, rope_dn, rope_c_only = _rope_tables(seq)
    bias_mla, bias_dil = _causal_bias(seq), _dilated_bias(seq)
    scale_mla, scale_dil = MLA_QK ** -0.5, DIL_DIM ** -0.5
    qd0, kd0, vd0, lora0, kr0 = P_QD // LANES, P_KD // LANES, P_VD // LANES, P_LORA // LORA_W, P_KR // LANES

    proj = mm(x, w_in_p, mode="nn", name="proj", tm=1024, tn=1536, tk=1024, after=token)

    def prep(lora, gq, gkv):
        return _rms(lora[:, :Q_LORA], gq), _rms(lora[:, Q_LORA:], gkv)

    qn, kvn = row(prep, name="mla_rms", ins=[(proj, LORA_W, lora0, "row"), (g_q_a, 0, 0, "full"), (g_kv_a, 0, 0, "full")],
                  outs=[(Q_LORA, BF16), (KV_LORA, BF16)])
    q_lin = mm(qn, w_uq_p, mode="nn", name="q_up", tm=1024, tn=1024, tk=Q_LORA)
    k_lin = mm(kvn, w_uk_p, mode="nn", name="k_up", tm=1024, tn=1024, tk=KV_LORA)
    v_a = mm(kvn, w_uv_p, mode="nn", name="v_up", tm=1024, tn=1024, tk=KV_LORA, out_dtypes=(BF16,))

    def rope_qk(ql, kl, kr, c, up, dn):
        k_rot = _rope_fwd(kr, c, up, dn)
        qs = [_rope_fwd(ql[:, h * LANES:(h + 1) * LANES], c, up, dn) for h in range(N_HEADS)]
        ks = [kl[:, h * LANES:(h + 1) * LANES] + k_rot for h in range(N_HEADS)]
        return jnp.concatenate(qs, axis=1), jnp.concatenate(ks, axis=1)

    pos = lambda tab: (tab, LANES, 0, "pos")
    q_a, k_a = row(rope_qk, name="rope_qk",
                   ins=[(q_lin, D_MODEL, 0, "row"), (k_lin, D_MODEL, 0, "row"), (proj, LANES, kr0, "row"), pos(rope_c), pos(rope_up), pos(rope_dn)],
                   outs=[(N_HEADS * LANES, BF16), (N_HEADS * LANES, BF16)])
    o_a, lse_a = _attn_fwd(q_a, 0, k_a, 0, v_a, 0, bias_mla, scale_mla, name="mla_fwd", batch=batch, seq=seq)
    o_b, lse_b = _attn_fwd(proj, qd0, proj, kd0, proj, vd0, bias_dil, scale_dil, name="dil_fwd", batch=batch, seq=seq)
    late = wg if late_weights is None else late_weights(o_b)
    w_oa_p = _pad_rows(_join_cols(late["w_o_mla"]), V_DIM)
    w_ob_p = _pad_rows(_join_cols(late["w_o_dil"]), DIL_DIM)
    w_out, w_ff1, w_ff2 = late["w_out"].reshape(D_MODEL, D_MODEL), late["w_ff1"], late["w_ff2"].reshape(D_FF, D_MODEL)
    y_a = mm(o_a, w_oa_p, mode="nn", name="o_mla", tm=1024, tn=1024, tk=1024)
    y_b = mm(o_b, w_ob_p, mode="nn", name="o_dil", tm=1024, tn=1024, tk=1024)

    def gate(t0, t1, c0, c1, ya, yb):
        return (jax.nn.sigmoid(t0 + c0) * ya + jax.nn.sigmoid(t1 + c1) * yb,)

    gate_ins = [(proj, D_MODEL, 0, "row"), (proj, D_MODEL, 1, "row"), (b0, 0, 0, "full"), (b1, 0, 0, "full")]
    (u,) = row(gate, name="gate", ins=gate_ins + [(y_a, D_MODEL, 0, "row"), (y_b, D_MODEL, 0, "row")], outs=[(D_MODEL, BF16)])
    mixed = mm(u, w_out, mode="nn", name="mix", tm=1024, tn=1024, tk=1024)

    def ln1(xv, mv, g, b):
        r1 = ALPHA * xv + mv
        xh, _ = _ln_stats(r1)
        return r1, xh * g + b

    r1, h = row(ln1, name="ln1", ins=[(x, D_MODEL, 0, "row"), (mixed, D_MODEL, 0, "row"), (ln1_g, 0, 0, "full"), (ln1_b, 0, 0, "full")],
                outs=[(D_MODEL, F32), (D_MODEL, F32)])

    def relu2(acc):
        r = jnp.maximum(acc, 0.0)
        return acc, r * r

    a_ff, z = mm(h, w_ff1, mode="nn", name="ff1", tm=1024, tn=1024, tk=1024, out_dtypes=(F32, BF16), epilogue=relu2, b_shards=True)
    f = mm(z, w_ff2, mode="nn", name="ff2", tm=1024, tn=1024, tk=1024)

    def ln2_loss(hv, fv, tv, g, b):
        xh, r = _ln_stats(ALPHA * hv + fv)
        err = xh * g + b - tv
        dy = err * (1.0 / D_MODEL)
        dr2, dg, db = _ln_bwd(xh, r, g, dy)
        loss = jnp.sum(_colsum(err * err), axis=1, keepdims=True) * (0.5 / D_MODEL)
        return dr2, jnp.broadcast_to(loss, (1, LANES)), dg, db

    dr2, loss_l, d_ln2_g, d_ln2_b = row(
        ln2_loss, name="ln2_loss",
        ins=[(h, D_MODEL, 0, "row"), (f, D_MODEL, 0, "row"), (target, D_MODEL, 0, "row"), (ln2_g, 0, 0, "full"), (ln2_b, 0, 0, "full")],
        outs=[(D_MODEL, F32)], sums=[LANES, D_MODEL, D_MODEL])

    d_w_ff2 = mm(z, dr2, mode="tn", name="d_w_ff2", tm=1024, tn=1024, tk=1024)
    da = mm(dr2, w_ff2, mode="nt", name="d_ff_act", tm=1024, tn=1024, tk=1024, out_dtypes=(BF16,), extras=(a_ff,),
            epilogue=lambda acc, av: (acc * (2.0 * jnp.maximum(av, 0.0)),))
    d_w_ff1 = mm(h, da, mode="tn", name="d_w_ff1", tm=1024, tn=1024, tk=1024, out_shards=True)
    dh = mm(da, w_ff1, mode="nt", name="d_h", tm=1024, tn=1024, tk=1024, extras=(dr2,), epilogue=lambda acc, rv: (acc + ALPHA * rv,), b_shards=True)

    def ln1_bwd(dhv, r1v, g):
        xh, r = _ln_stats(r1v)
        return _ln_bwd(xh, r, g, dhv)

    dr1, d_ln1_g, d_ln1_b = row(ln1_bwd, name="ln1_bwd", ins=[(dh, D_MODEL, 0, "row"), (r1, D_MODEL, 0, "row"), (ln1_g, 0, 0, "full")],
                                outs=[(D_MODEL, F32)], sums=[D_MODEL, D_MODEL])
    d_w_out = mm(u, dr1, mode="tn", name="d_w_out", tm=1024, tn=1024, tk=1024)
    du = mm(dr1, w_out, mode="nt", name="d_u", tm=1024, tn=1024, tk=1024)

    def gate_bwd(t0, t1, c0, c1, ya, yb, duv):
        s0, s1 = jax.nn.sigmoid(t0 + c0), jax.nn.sigmoid(t1 + c1)
        dt0 = duv * ya * s0 * (1.0 - s0)
        dt1 = duv * yb * s1 * (1.0 - s1)
        return duv * s0, duv * s1, jnp.concatenate([dt0, dt1], axis=1), jnp.concatenate([_colsum(dt0), _colsum(dt1)], axis=1)

    dy_a, dy_b, d_gates, d_b_gate = row(
        gate_bwd, name="gate_bwd", ins=gate_ins + [(y_a, D_MODEL, 0, "row"), (y_b, D_MODEL, 0, "row"), (du, D_MODEL, 0, "row")],
        outs=[(D_MODEL, BF16), (D_MODEL, BF16), (2 * D_MODEL, BF16)], sums=[2 * D_MODEL])
    d_w_oa_p = mm(o_a, dy_a, mode="tn", name="d_w_o_mla", tm=1024, tn=1024, tk=1024)
    d_w_ob_p = mm(o_b, dy_b, mode="tn", name="d_w_o_dil", tm=1024, tn=1024, tk=1024)
    grads = dict(w_o_mla=_split_cols(_unpad_rows(d_w_oa_p, V_DIM)), w_o_dil=_split_cols(_unpad_rows(d_w_ob_p, DIL_DIM)),
                 w_out=d_w_out.reshape(N_CHIPS, D_MODEL // N_CHIPS, D_MODEL), w_ff1=d_w_ff1, w_ff2=d_w_ff2.reshape(N_CHIPS, D_FF // N_CHIPS, D_MODEL))
    sent = None if early_grads is None else early_grads(grads)
    do_a = mm(dy_a, w_oa_p, mode="nt", name="d_o_mla", tm=1024, tn=1024, tk=1024, after=sent)
    do_b = mm(dy_b, w_ob_p, mode="nt", name="d_o_dil", tm=1024, tn=1024, tk=1024)
    going = None if early_grads_go is None else early_grads_go(do_b)
    dq_a, dk_a, dv_a = _attn_bwd(q_a, 0, k_a, 0, v_a, 0, o_a, do_a, lse_a, bias_mla, scale_mla,
                                 name="mla_bwd", batch=batch, seq=seq, out_dtype=F32, after=going)
    dq_d, dk_d, dv_d = _attn_bwd(proj, qd0, proj, kd0, proj, vd0, o_b, do_b, lse_b, bias_dil, scale_dil,
                                 name="dil_bwd", batch=batch, seq=seq, out_dtype=BF16)

    def mla_post(dq, dk, c, up, dn, c_only):
        dqs = [_rope_bwd(dq[:, h * LANES:(h + 1) * LANES], c, up, dn) for h in range(N_HEADS)]
        dk_sum = dk[:, :LANES]
        for h in range(1, N_HEADS):
            dk_sum = dk_sum + dk[:, h * LANES:(h + 1) * LANES]
        return jnp.concatenate(dqs, axis=1), _rope_bwd(dk_sum, c_only, up, dn)

    dq_lin, d_kr = row(mla_post, name="mla_unrope",
                       ins=[(dq_a, D_MODEL, 0, "row"), (dk_a, D_MODEL, 0, "row"), pos(rope_c), pos(rope_up), pos(rope_dn), pos(rope_c_only)],
                       outs=[(N_HEADS * LANES, BF16), (LANES, BF16)])
    d_w_uq_p = mm(qn, dq_lin, mode="tn", name="d_w_uq", tm=Q_LORA, tn=1024, tk=1024)
    d_w_uk_p = mm(kvn, dk_a, mode="tn", name="d_w_uk", tm=KV_LORA, tn=1024, tk=1024)
    d_w_uv_p = mm(kvn, dv_a, mode="tn", name="d_w_uv", tm=KV_LORA, tn=1024, tk=1024)
    d_qn = mm(dq_lin, w_uq_p, mode="nt", name="d_qn", tm=1024, tn=Q_LORA, tk=1024)
    d_kvn_k = mm(dk_a, w_uk_p, mode="nt", name="d_kvn_k", tm=1024, tn=KV_LORA, tk=1024)
    d_kvn = mm(dv_a, w_uv_p, mode="nt", name="d_kvn", tm=1024, tn=KV_LORA, tk=1024, extras=(d_kvn_k,), epilogue=lambda acc, e: (acc + e,))

    def rms_bwd(lora, dq, dkv, gq, gkv):
        dxq, dgq = _rms_bwd(lora[:, :Q_LORA], gq, dq)
        dxk, dgk = _rms_bwd(lora[:, Q_LORA:], gkv, dkv)
        return jnp.concatenate([dxq, dxk], axis=1), dgq, dgk

    d_lora, d_g_q_a, d_g_kv_a = row(
        rms_bwd, name="mla_rms_bwd",
        ins=[(proj, LORA_W, lora0, "row"), (d_qn, Q_LORA, 0, "row"), (d_kvn, KV_LORA, 0, "row"), (g_q_a, 0, 0, "full"), (g_kv_a, 0, 0, "full")],
        outs=[(LORA_W, BF16)], sums=[Q_LORA, KV_LORA])
    d_proj = jnp.concatenate([d_gates, dq_d, dk_d, dv_d, d_lora, d_kr, jnp.zeros((rows, P_WIDTH - P_KR - LANES), BF16)], axis=1)
    d_w_in_p = mm(x, d_proj, mode="tn", name="d_w_in", tm=1024, tn=1536, tk=1024)
    grad_x = mm(d_proj, w_in_p, mode="nt", name="d_x", tm=1024, tn=1024, tk=1536, extras=(dr1,), epilogue=lambda acc, rv: (acc + ALPHA * rv,))

    grads.update(
        w_in=_split_cols(_unpad_w_in(d_w_in_p)), w_uq=_split_cols(_unpad_heads(d_w_uq_p, MLA_QK)), w_ukv=_split_cols(_merge_ukv(d_w_uk_p, d_w_uv_p)),
        b_gate=d_b_gate.reshape(2, D_MODEL), g_q_a=d_g_q_a, g_kv_a=d_g_kv_a, ln1_g=d_ln1_g, ln1_b=d_ln1_b, ln2_g=d_ln2_g, ln2_b=d_ln2_b)
    return loss_l, grad_x.reshape(batch, seq, D_MODEL), grads


BIG = (("w_in", (1024, 1064)), ("w_uq", (384, 192)), ("w_ukv", (256, 256)), ("w_o_mla", (512, 256)),
       ("w_o_dil", (512, 256)), ("w_out", (256, 1024)), ("w_ff1", (1024, 1024)), ("w_ff2", (1024, 1024)))
SMALL = (("b_gate", 2 * D_MODEL), ("g_q_a", Q_LORA), ("g_kv_a", KV_LORA), ("ln1_g", D_MODEL), ("ln1_b", D_MODEL),
         ("ln2_g", D_MODEL), ("ln2_b", D_MODEL))
N_BIG = len(BIG)
D2D_CHUNKS = 4
ANY = pl.BlockSpec(memory_space=pl.ANY)
SIDE_EFFECTS = pltpu.CompilerParams(has_side_effects=True)


def _place():
    x, y, c = lax.axis_index("x"), lax.axis_index("y"), lax.axis_index("c")
    return x, y, c, ((1 - x, y), (x, 1 - y), (1 - x, 1 - y))


def _rows(at, n):
    return pl.ds(pl.multiple_of(at, 16), n)


def _row_chunks(at, n):
    size = n // D2D_CHUNKS
    return [_rows(at + i * size, size) for i in range(D2D_CHUNKS)]


def _remote(src, dst, send, recv, to):
    return pltpu.make_async_remote_copy(src_ref=src, dst_ref=dst, send_sem=send, recv_sem=recv, device_id=to, device_id_type=MESH)


def _gather_weights(shards, name):
    n = len(shards)

    def body(*refs):
        srcs, outs, (send, recv) = refs[:n], refs[n:2 * n], refs[2 * n:]
        x, y, c, chips = _place()
        sibling = (x, y, 1 - c)
        halves = [s.shape[0] // 2 for s in shards]

        def over_ici(t, j, cx, cy, to):
            r = _rows(c * halves[t], halves[t])
            return _remote(srcs[t].at[r], outs[t].at[2 * cx + cy, r], send.at[t, j], recv.at[t, j], (*to, c))

        def over_d2d(t, j, slot, r):
            src = srcs[t].at[r] if j == 6 else outs[t].at[slot, r]
            return _remote(src, outs[t].at[slot, r], send.at[t, j], recv.at[t, j], sibling)

        for t in range(n):
            for j, chip in enumerate(chips):
                over_ici(t, j, x, y, chip).start()
        for t in range(n):
            for r in _row_chunks(0, 2 * halves[t]):
                over_d2d(t, 6, 2 * x + y, r).start()
        for t in range(n):
            for j, (cx, cy) in enumerate(chips):
                over_ici(t, j, cx, cy, (cx, cy)).wait_recv()
                for r in _row_chunks(c * halves[t], halves[t]):
                    over_d2d(t, 3 + j, 2 * cx + cy, r).start()
        for t in range(n):
            over_d2d(t, 6, 2 * x + y, _rows(0, 2 * halves[t])).wait()
            for j, (cx, cy) in enumerate(chips):
                over_d2d(t, 3 + j, 2 * cx + cy, _rows((1 - c) * halves[t], halves[t])).wait_recv()
                over_d2d(t, 3 + j, 2 * cx + cy, _rows(c * halves[t], halves[t])).wait_send()
                over_ici(t, j, x, y, (cx, cy)).wait_send()

    return pl.pallas_call(
        body, name=name, in_specs=[ANY] * n, out_specs=[ANY] * n,
        out_shape=[jax.ShapeDtypeStruct((N_CHIPS,) + s.shape, s.dtype) for s in shards],
        scratch_shapes=[pltpu.SemaphoreType.DMA((n, 7)), pltpu.SemaphoreType.DMA((n, 7))],
        compiler_params=SIDE_EFFECTS,
    )(*shards)


def _pair_split(grads, name):
    n = len(grads)

    def body(*refs):
        srcs, outs, (send, recv) = refs[:n], refs[n:2 * n], refs[2 * n:]
        x, y, c, _ = _place()
        for t in range(n):
            half = grads[t].shape[1] // 2
            give = _rows((1 - c) * half, half)
            for s in range(N_CHIPS):
                _remote(srcs[t].at[s, give], outs[t].at[s], send.at[t], recv.at[t], (x, y, 1 - c)).start()
        for t in range(n):
            half = grads[t].shape[1] // 2
            _remote(srcs[t].at[:, _rows((1 - c) * half, half)], outs[t], send.at[t], recv.at[t], (x, y, 1 - c)).wait()

    return pl.pallas_call(
        body, name=name, in_specs=[ANY] * n, out_specs=[ANY] * n,
        out_shape=[jax.ShapeDtypeStruct((N_CHIPS, g.shape[1] // 2, g.shape[2]), g.dtype) for g in grads],
        scratch_shapes=[pltpu.SemaphoreType.DMA((n,)), pltpu.SemaphoreType.DMA((n,))],
        compiler_params=SIDE_EFFECTS,
    )(*grads)


def _pair_join(totals, name):
    n = len(totals)

    def body(*refs):
        srcs, outs, (send, recv) = refs[:n], refs[n:2 * n], refs[2 * n:]
        x, y, c, _ = _place()
        for t in range(n):
            for r in _row_chunks(0, totals[t].shape[0]):
                _remote(srcs[t].at[r], outs[t].at[r], send.at[t], recv.at[t], (x, y, 1 - c)).start()
        for t in range(n):
            _remote(srcs[t], outs[t], send.at[t], recv.at[t], (x, y, 1 - c)).wait()

    return pl.pallas_call(
        body, name=name, in_specs=[ANY] * n, out_specs=[ANY] * n,
        out_shape=[jax.ShapeDtypeStruct(t.shape, t.dtype) for t in totals],
        scratch_shapes=[pltpu.SemaphoreType.DMA((n,)), pltpu.SemaphoreType.DMA((n,))],
        compiler_params=SIDE_EFFECTS,
    )(*totals)


HBM = pl.BlockSpec(memory_space=pltpu.HBM)
SEM = pl.BlockSpec(memory_space=pltpu.SEMAPHORE)
SPLIT = pltpu.CompilerParams(has_side_effects=pltpu.SideEffectType.DATAFLOW_SIDE_EFFECTING)


def _in_hbm(a):
    return pltpu.with_memory_space_constraint(a, pltpu.HBM)


def _split_copies(kind, srcs, lands):
    x, y, c, chips = _place()
    out = []
    for t in range(len(srcs)):
        if kind == "pair":
            half = srcs[t].shape[1] // 2
            out += [(t, s % 3, srcs[t].at[s, _rows((1 - c) * half, half)], lands[t].at[s], (x, y, 1 - c)) for s in range(N_CHIPS)]
            continue
        for j, (cx, cy) in enumerate(chips):
            if kind == "gather":
                r = _rows(c * (srcs[t].shape[0] // 2), srcs[t].shape[0] // 2)
                out.append((t, j, srcs[t].at[r], lands[t].at[2 * x + y, r], (cx, cy, c)))
            else:
                out.append((t, j, srcs[t].at[2 * cx + cy], lands[t].at[j], (cx, cy, c)))
    return out


def _split_start(kind, srcs, land_shapes, name):
    n = len(srcs)

    def body(*refs):
        src_refs, land_refs, sems, token = refs[:n], refs[n:2 * n], refs[2 * n:2 * n + 6], refs[-1]
        for t, j, s, d, to in _split_copies(kind, src_refs, land_refs):
            _remote(s, d, sems[j], sems[3 + j], to).start()
        token[...] = jnp.zeros_like(token)

    lands = [_in_hbm(lax.empty(s.shape, s.dtype)) for s in land_shapes]
    thru = [pltpu.HBM(a.shape, a.dtype) for a in list(srcs) + lands]
    res = pl.pallas_call(
        body, name=name,
        out_shape=(*[pltpu.SemaphoreType.DMA(())] * 6, *thru, jax.ShapeDtypeStruct((8, LANES), F32)),
        in_specs=[HBM] * (2 * n), out_specs=(*[SEM] * 6, *[HBM] * (2 * n), pl.BlockSpec(memory_space=pltpu.VMEM)),
        input_output_aliases={i: 6 + i for i in range(2 * n)}, compiler_params=SPLIT,
    )(*[_in_hbm(s) for s in srcs], *lands)
    return res[:6], res[6:6 + n], res[6 + n:6 + 2 * n], res[-1]


def _split_wait(kind, sems, srcs, lands, after, name):
    n = len(srcs)

    def body(*refs):
        src_refs, land_refs, sem_refs = refs[:n], refs[n:2 * n], refs[2 * n:2 * n + 6]
        for t, j, s, d, to in _split_copies(kind, src_refs, land_refs):
            cp = _remote(s, d, sem_refs[j], sem_refs[3 + j], to)
            cp.wait_send()
            cp.wait_recv()

    res = pl.pallas_call(
        body, name=name, out_shape=[pltpu.HBM(a.shape, a.dtype) for a in list(srcs) + list(lands)],
        in_specs=[HBM] * (2 * n) + [SEM] * 6 + [ANY], out_specs=[HBM] * (2 * n),
        input_output_aliases={i: i for i in range(2 * n)}, compiler_params=SPLIT,
    )(*srcs, *lands, *sems, after)
    return res[:n], res[n:]


def _gather_finish(shards, lands, name):
    n = len(shards)

    def body(*refs):
        srcs, outs, (send, recv) = refs[:n], refs[2 * n:3 * n], refs[3 * n:]
        x, y, c, chips = _place()
        sibling = (x, y, 1 - c)
        for t in range(n):
            half = shards[t].shape[0] // 2
            for r in _row_chunks(0, 2 * half):
                _remote(srcs[t].at[r], outs[t].at[2 * x + y, r], send.at[t, 3], recv.at[t, 3], sibling).start()
            for j, (cx, cy) in enumerate(chips):
                for r in _row_chunks(c * half, half):
                    _remote(outs[t].at[2 * cx + cy, r], outs[t].at[2 * cx + cy, r], send.at[t, j], recv.at[t, j], sibling).start()
        for t in range(n):
            half = shards[t].shape[0] // 2
            mine, other = _rows(c * half, half), _rows((1 - c) * half, half)
            _remote(srcs[t], outs[t].at[2 * x + y], send.at[t, 3], recv.at[t, 3], sibling).wait()
            for j, (cx, cy) in enumerate(chips):
                slab = outs[t].at[2 * cx + cy]
                _remote(slab.at[mine], slab.at[mine], send.at[t, j], recv.at[t, j], sibling).wait_send()
                _remote(slab.at[other], slab.at[other], send.at[t, j], recv.at[t, j], sibling).wait_recv()

    return pl.pallas_call(
        body, name=name, in_specs=[ANY] * (2 * n), out_specs=[ANY] * n,
        out_shape=[jax.ShapeDtypeStruct(a.shape, a.dtype) for a in lands],
        scratch_shapes=[pltpu.SemaphoreType.DMA((n, 4)), pltpu.SemaphoreType.DMA((n, 4))],
        input_output_aliases={n + t: t for t in range(n)}, compiler_params=SIDE_EFFECTS,
    )(*shards, *lands)


def _sum_all_devices(vec, name):
    n_rows = vec.shape[0]

    def body(v_ref, out_ref, buf, send, recv):
        x, y, c, _ = _place()
        me = 4 * x + 2 * y + c
        buf[me] = v_ref[...]
        flips = [(a, b, d) for a in (0, 1) for b in (0, 1) for d in (0, 1)][1:]
        copies = []
        for r, (a, b, d) in enumerate(flips):
            px, py, pc = (1 - x if a else x), (1 - y if b else y), (1 - c if d else c)
            copies.append(pltpu.make_async_remote_copy(src_ref=v_ref, dst_ref=buf.at[me], send_sem=send.at[r], recv_sem=recv.at[r],
                                                       device_id=(px, py, pc), device_id_type=MESH))
            copies[-1].start()
        for r, (a, b, d) in enumerate(flips):
            px, py, pc = (1 - x if a else x), (1 - y if b else y), (1 - c if d else c)
            pltpu.make_async_remote_copy(src_ref=v_ref, dst_ref=buf.at[4 * px + 2 * py + pc], send_sem=send.at[r], recv_sem=recv.at[r],
                                         device_id=(px, py, pc), device_id_type=MESH).wait_recv()
        for cp in copies:
            cp.wait_send()
        total = buf[0]
        for k in range(1, N_DEV):
            total = total + buf[k]
        out_ref[...] = total

    vmem = pl.BlockSpec(memory_space=pltpu.VMEM)
    return pl.pallas_call(
        body, name=name, in_specs=[vmem], out_specs=vmem, out_shape=jax.ShapeDtypeStruct(vec.shape, F32),
        scratch_shapes=[pltpu.VMEM((N_DEV, n_rows, LANES), F32), pltpu.SemaphoreType.DMA((N_DEV - 1,)), pltpu.SemaphoreType.DMA((N_DEV - 1,))],
        compiler_params=pltpu.CompilerParams(has_side_effects=True),
    )(vec)


def _half_tile(half, width):
    t = half
    while t * width * 4 > (2 << 20) and t % 32 == 0:
        t //= 2
    return t


def _pair_add(g, theirs, core, name):
    _, half, width = theirs.shape
    t = _half_tile(half, width)
    n = half // t

    def body(c_ref, a_ref, b_ref, o_ref):
        o_ref[...] = (a_ref[...] + b_ref[...]).astype(BF16)

    tile = pl.BlockSpec((1, t, width), lambda j, i, c_ref: (j, i, 0))
    return pl.pallas_call(
        body, name=name,
        grid_spec=pltpu.PrefetchScalarGridSpec(
            num_scalar_prefetch=1, grid=(N_CHIPS, n),
            in_specs=[pl.BlockSpec((1, t, width), lambda j, i, c_ref: (j, c_ref[0] * n + i, 0)), tile], out_specs=tile),
        out_shape=jax.ShapeDtypeStruct(theirs.shape, BF16), compiler_params=_params(("parallel", "parallel")),
    )(core, g, theirs)


def _chip_sum(part, others, chip, name, after=None):
    _, half, width = part.shape
    t = _half_tile(half, width)

    def body(s_ref, mine, p0, p1, p2, *rest):
        o_ref = rest[-1]
        o_ref[...] = ((mine[0].astype(F32) + p0[0].astype(F32)) + p1[0].astype(F32)) + p2[0].astype(F32)

    return pl.pallas_call(
        body, name=name,
        grid_spec=pltpu.PrefetchScalarGridSpec(
            num_scalar_prefetch=1, grid=(half // t,),
            in_specs=[pl.BlockSpec((1, t, width), lambda i, s_ref: (s_ref[0], i, 0))]
            + [pl.BlockSpec((1, t, width), lambda i, s_ref, j=j: (j, i, 0)) for j in range(3)] + [pl.BlockSpec(memory_space=pl.ANY)] * (after is not None),
            out_specs=pl.BlockSpec((t, width), lambda i, s_ref: (i, 0))),
        out_shape=jax.ShapeDtypeStruct((half, width), F32), compiler_params=_params(("parallel",)),
    )(chip, part, others, others, others, *([after] if after is not None else []))


EARLY = ("w_in", "w_uq", "w_ukv")
LATE = ("w_o_mla", "w_o_dil", "w_out", "w_ff1", "w_ff2")


def _chip_partials(grads, names, core, tag):
    gs = [grads[n] for n in names]
    theirs = _pair_split(gs, "pair_split_" + tag)
    return [_pair_add(g, th, core, "pair_add_" + n) for g, th, n in zip(gs, theirs, names)]


def _sum_small(vals):
    n_in = len(vals)
    n_rows = sum(a.shape[0] * a.shape[1] // LANES for a in vals)
    pad_rows = -(-n_rows // 8) * 8

    def chunks(refs):
        return [(ref, a, j) for ref in refs for a in range(ref.shape[0]) for j in range(ref.shape[1] // LANES)]

    def body(*refs):
        ins, outs, (buf, send, recv) = refs[:n_in], refs[n_in:2 * n_in], refs[2 * n_in:]
        x, y, c, _ = _place()
        me = 4 * x + 2 * y + c
        for r, (ref, a, j) in enumerate(chunks(ins)):
            buf[me, r:r + 1, :] = ref[a:a + 1, j * LANES:(j + 1) * LANES]
        if pad_rows > n_rows:
            buf[me, n_rows:pad_rows, :] = jnp.zeros((pad_rows - n_rows, LANES), F32)
        flips = [(a, b, d) for a in (0, 1) for b in (0, 1) for d in (0, 1)][1:]
        peers = [((1 - x if a else x), (1 - y if b else y), (1 - c if d else c)) for a, b, d in flips]
        copies = [_remote(buf.at[me], buf.at[me], send.at[r], recv.at[r], peer) for r, peer in enumerate(peers)]
        for cp in copies:
            cp.start()
        for r, (px, py, pc) in enumerate(peers):
            _remote(buf.at[me], buf.at[4 * px + 2 * py + pc], send.at[r], recv.at[r], (px, py, pc)).wait_recv()
        for cp in copies:
            cp.wait_send()
        total = buf[0]
        for k in range(1, N_DEV):
            total = total + buf[k]
        for r, (ref, a, j) in enumerate(chunks(outs)):
            ref[a:a + 1, j * LANES:(j + 1) * LANES] = total[r:r + 1, :]

    vmem = pl.BlockSpec(memory_space=pltpu.VMEM)
    return pl.pallas_call(
        body, name="sum_small", in_specs=[vmem] * n_in, out_specs=[vmem] * n_in,
        out_shape=[jax.ShapeDtypeStruct(a.shape, F32) for a in vals],
        scratch_shapes=[pltpu.VMEM((N_DEV, pad_rows, LANES), F32), pltpu.SemaphoreType.DMA((N_DEV - 1,)), pltpu.SemaphoreType.DMA((N_DEV - 1,))],
        compiler_params=SIDE_EFFECTS,
    )(*vals)


def _adam_math(w, g, m, v):
    nm = B1 * m + (1.0 - B1) * g
    nv = B2 * v + (1.0 - B2) * (g * g)
    m_hat = nm / (1.0 - B1 ** ADAM_STEP)
    v_hat = nv / (1.0 - B2 ** ADAM_STEP)
    return -LR * (m_hat / (jnp.sqrt(v_hat) + ADAM_EPS) + WD * w), nm, nv


def _adamw_big(w, mine, theirs, m, v, core, name, side_by_side=False):
    rows, width = w.shape
    if side_by_side:
        t = next(c for c in (152, 96, 64, 32, 16, 8) if rows % c == 0)
        hb = None
        half_spec = pl.BlockSpec((t, width // 2), lambda i, c_ref: (i, 0))
    else:
        t = next(c for c in (256, 128, 64, 32, 16, 8) if (rows // 2) % c == 0)
        hb = rows // 2 // t
        half_spec = pl.BlockSpec((t, width), lambda i, c_ref: (i % hb, 0))

    def body(c_ref, w_ref, a_ref, b_ref, m_ref, v_ref, g_ref, d_ref, nm_ref, nv_ref):
        south = c_ref[0] == 0
        if side_by_side:
            g = jnp.where(south, jnp.concatenate([a_ref[...], b_ref[...]], axis=1), jnp.concatenate([b_ref[...], a_ref[...]], axis=1))
        else:
            g = jnp.where((pl.program_id(0) < hb) == south, a_ref[...], b_ref[...])
        g_ref[...] = g
        d_ref[...], nm_ref[...], nv_ref[...] = _adam_math(w_ref[...], g, m_ref[...], v_ref[...])

    spec = pl.BlockSpec((t, width), lambda i, c_ref: (i, 0))
    return pl.pallas_call(
        body, name=name,
        grid_spec=pltpu.PrefetchScalarGridSpec(num_scalar_prefetch=1, grid=(rows // t,),
                                               in_specs=[spec, half_spec, half_spec, spec, spec], out_specs=[spec] * 4),
        out_shape=[jax.ShapeDtypeStruct(w.shape, F32)] * 4, compiler_params=_params(("parallel",)),
    )(core, w, mine, theirs, m, v)


def _adamw_small(ws, gs, ms, vs):
    n = len(ws)

    def body(*refs):
        for t in range(n):
            w_ref, g_ref, m_ref, v_ref = (refs[k * n + t] for k in range(4))
            d, nm, nv = _adam_math(w_ref[...], g_ref[...], m_ref[...], v_ref[...])
            refs[4 * n + t][...] = d
            refs[5 * n + t][...] = nm
            refs[6 * n + t][...] = nv

    vmem = pl.BlockSpec(memory_space=pltpu.VMEM)
    res = pl.pallas_call(body, name="adamw_small", in_specs=[vmem] * (4 * n), out_specs=[vmem] * (3 * n),
                         out_shape=[jax.ShapeDtypeStruct(a.shape, F32) for a in ws] * 3)(*ws, *gs, *ms, *vs)
    return res[:n], res[n:2 * n], res[2 * n:]


def kernel(x, w_in, b_gate, g_q_a, w_uq, g_kv_a, w_ukv, w_o_mla, w_o_dil, w_out, ln1_g, ln1_b, w_ff1, w_ff2, ln2_g, ln2_b, loss_target, m_w_in, m_b_gate, m_g_q_a, m_w_uq, m_g_kv_a, m_w_ukv, m_w_o_mla, m_w_o_dil, m_w_out, m_ln1_g, m_ln1_b, m_w_ff1, m_w_ff2, m_ln2_g, m_ln2_b, v_w_in, v_b_gate, v_g_q_a, v_w_uq, v_g_kv_a, v_w_ukv, v_w_o_mla, v_w_o_dil, v_w_out, v_ln1_g, v_ln1_b, v_w_ff1, v_w_ff2, v_ln2_g, v_ln2_b):
    order = ("w_in", "b_gate", "g_q_a", "w_uq", "g_kv_a", "w_ukv", "w_o_mla", "w_o_dil", "w_out", "ln1_g", "ln1_b", "w_ff1", "w_ff2", "ln2_g", "ln2_b")
    w = dict(w_in=w_in, b_gate=b_gate, g_q_a=g_q_a, w_uq=w_uq, g_kv_a=g_kv_a, w_ukv=w_ukv, w_o_mla=w_o_mla, w_o_dil=w_o_dil, w_out=w_out,
             ln1_g=ln1_g, ln1_b=ln1_b, w_ff1=w_ff1, w_ff2=w_ff2, ln2_g=ln2_g, ln2_b=ln2_b)
    m = dict(w_in=m_w_in, b_gate=m_b_gate, g_q_a=m_g_q_a, w_uq=m_w_uq, g_kv_a=m_g_kv_a, w_ukv=m_w_ukv, w_o_mla=m_w_o_mla, w_o_dil=m_w_o_dil,
             w_out=m_w_out, ln1_g=m_ln1_g, ln1_b=m_ln1_b, w_ff1=m_w_ff1, w_ff2=m_w_ff2, ln2_g=m_ln2_g, ln2_b=m_ln2_b)
    v = dict(w_in=v_w_in, b_gate=v_b_gate, g_q_a=v_g_q_a, w_uq=v_w_uq, g_kv_a=v_g_kv_a, w_ukv=v_w_ukv, w_o_mla=v_w_o_mla, w_o_dil=v_w_o_dil,
             w_out=v_w_out, ln1_g=v_ln1_g, ln1_b=v_ln1_b, w_ff1=v_w_ff1, w_ff2=v_w_ff2, ln2_g=v_ln2_g, ln2_b=v_ln2_b)
    chip = 2 * lax.axis_index("x") + lax.axis_index("y")
    south = (lax.axis_index("c") == 0).astype(F32)
    gate_w = D_MODEL // N_CHIPS

    core = lax.axis_index("c").astype(jnp.int32).reshape(1)
    shards = {n: w[n][0].astype(BF16) for n, _ in BIG}
    first = dict(zip(EARLY, _gather_weights([shards[n] for n in EARLY], "gather_early")))
    late_shards = [shards[n] for n in LATE]
    g_sems, g_srcs, g_lands, g_token = _split_start(
        "gather", late_shards, [jax.ShapeDtypeStruct((N_CHIPS,) + s.shape, BF16) for s in late_shards], "gather_late_start")
    b_mine = lax.dynamic_update_slice(jnp.zeros((2, D_MODEL), F32), b_gate[0] * south, (0, chip * gate_w))
    b_full = _sum_all_devices(b_mine.reshape(-1, LANES), "gather_b_gate").reshape(2, D_MODEL)

    def late_weights(after):
        srcs, lands = _split_wait("gather", g_sems, g_srcs, g_lands, after, "gather_late_wait")
        return dict(zip(LATE, _gather_finish(srcs, lands, "gather_late_finish")))

    sent = {}
    exchange_shapes = lambda parts: [jax.ShapeDtypeStruct((3,) + p.shape[1:], BF16) for p in parts]

    def early_grads(grads_late):
        gs = [grads_late[n] for n in LATE]
        shapes = [jax.ShapeDtypeStruct((N_CHIPS, g.shape[1] // 2, g.shape[2]), F32) for g in gs]
        sent["pair"] = _split_start("pair", gs, shapes, "pair_split_late_start")
        return sent["pair"][-1]

    def early_grads_go(after):
        gs, theirs = _split_wait("pair", *sent["pair"][:3], after, "pair_split_late_wait")
        parts = [_pair_add(g, th, core, "pair_add_" + n) for g, th, n in zip(gs, theirs, LATE)]
        sent["late"] = _split_start("scatter", parts, exchange_shapes(parts), "exchange_late_start")
        return sent["late"][-1]

    loss_part, grad_x, grads = _local_step(x, loss_target, first, b_full, g_q_a, g_kv_a, ln1_g, ln1_b, ln2_g, ln2_b, token=g_token,
                                           late_weights=late_weights, early_grads=early_grads, early_grads_go=early_grads_go)

    g_out, delta, new_m, new_v = {}, {}, {}, {}
    chip1 = chip.astype(jnp.int32).reshape(1)

    def finish(names, parts, others, tag):
        totals = [_chip_sum(p, o, chip1, "chip_sum_" + n) for n, p, o in zip(names, parts, others)]
        for n, mine, theirs in zip(names, totals, _pair_join(totals, "pair_join_" + tag)):
            flip = n in ("w_in", "w_uq")
            turn = (lambda a: a.T) if flip else (lambda a: a)
            res = _adamw_big(turn(w[n][0]), turn(mine), turn(theirs), turn(m[n][0]), turn(v[n][0]), core, "adamw_" + n, side_by_side=flip)
            g_out[n], delta[n], new_m[n], new_v[n] = (turn(r) for r in res)

    parts_early = _chip_partials(grads, EARLY, core, "early")
    e_sems, e_srcs, e_lands, e_token = _split_start("scatter", parts_early, exchange_shapes(parts_early), "exchange_early_start")
    finish(LATE, *_split_wait("scatter", *sent["late"][:3], e_token, "exchange_late_wait"), "late")

    small_names = [name for name, _ in SMALL]
    sums = _sum_small([grads[name] for name in small_names] + [loss_part])
    loss = sums[-1][0, 0]
    g_small = dict(zip(small_names, sums))
    g_small["b_gate"] = lax.dynamic_slice(g_small["b_gate"], (0, chip * gate_w), (2, gate_w))
    flat = lambda a: a.reshape(-1, a.shape[-1])
    res = _adamw_small(*[[flat(d[name]) for name in small_names] for d in (w, g_small, m, v)])
    g_out.update(g_small)
    for d, r in zip((delta, new_m, new_v), res):
        d.update(zip(small_names, r))
    finish(EARLY, *_split_wait("scatter", e_sems, e_srcs, e_lands, res[0][0], "exchange_early_wait"), "early")

    lead = lambda d: [d[name].reshape(w[name].shape) for name in order]
    return (loss, grad_x, *lead(g_out), *lead(delta), *lead(new_m), *lead(new_v))
```

```python
import functools
import math

import jax
import jax.numpy as jnp
from jax import lax
from jax.experimental import pallas as pl
from jax.experimental.pallas import tpu as pltpu

F32 = jnp.float32
BF16 = jnp.bfloat16
MESH = pl.DeviceIdType.MESH

D_MODEL = 1024
N_HEADS = 8
LANES = 128
NOPE, ROPE, V_DIM = 64, 32, 64
MLA_QK = NOPE + ROPE
Q_LORA, KV_LORA = 384, 256
DIL_DIM = 64
DIL_PATTERNS = ((128, 1), (512, 4), (2048, 16))
D_FF = 4096
N_CHIPS = 4
N_DEV = 8
IN_WIDTH = 4256
LN_EPS, RMS_EPS = 1e-5, 1e-6
NEG = -1e30
ALPHA = 2.0 ** 0.25
ROPE_THETA = 10000.0
LR, B1, B2, ADAM_EPS, WD, ADAM_STEP = 0.001, 0.9, 0.999, 1e-8, 0.01, 10

P_GATE, P_QD, P_KD, P_VD, P_LORA, P_KR, P_WIDTH = 0, 2048, 3072, 4096, 5120, 5760, 6144
LORA_W = Q_LORA + KV_LORA
KR_LANE = NOPE

ATT_T = 512
ROW_T = 512
VMEM_LIMIT = 56 * 1024 * 1024

NN = (((1,), (0,)), ((), ()))
NT = (((1,), (1,)), ((), ()))
TN = (((0,), (0,)), ((), ()))


def _params(sem=None, **kw):
    return pltpu.CompilerParams(dimension_semantics=sem, vmem_limit_bytes=VMEM_LIMIT, **kw)


def _matmul(a, b, *, mode, name, tm, tn, tk, out_dtypes=(F32,), extras=(), epilogue=None, b_shards=False, out_shards=False, after=None):
    if b_shards:
        n_sh, rows_b, cols_b = b.shape
        b_shape = (rows_b, n_sh * cols_b)
    else:
        b_shape = b.shape
    if mode == "nn":
        (m, k), (k2, n) = a.shape, b_shape
    elif mode == "nt":
        (m, k), (n, k2) = a.shape, b_shape
    else:
        (k, m), (k2, n) = a.shape, b_shape
    assert k == k2, (a.shape, b.shape, mode)
    tm, tn, tk = min(tm, m), min(tn, n), min(tk, k)
    assert m % tm == 0 and n % tn == 0 and k % tk == 0, (name, m, n, k, tm, tn, tk)
    nk = k // tk
    n_ex, n_out = len(extras), len(out_dtypes)
    n_in = 2 + n_ex + (after is not None)
    dims = {"nn": NN, "nt": NT, "tn": TN}[mode]

    def body(*refs):
        a_ref, b_ref = refs[:2]
        ex_refs = refs[2:2 + n_ex]
        out_refs = refs[n_in:n_in + n_out]
        part = lax.dot_general(a_ref[...].astype(BF16), b_ref[...].astype(BF16), dims, preferred_element_type=F32)

        def finish(acc):
            outs = epilogue(acc, *[r[...] for r in ex_refs]) if epilogue is not None else (acc,)
            for r, o in zip(out_refs, outs):
                r[...] = o.astype(r.dtype)

        if nk == 1:
            finish(part)
        else:
            acc_ref = refs[-1]
            kk = pl.program_id(2)

            @pl.when(kk == 0)
            def _():
                acc_ref[...] = part

            @pl.when(kk > 0)
            def _():
                acc_ref[...] += part

            @pl.when(kk == nk - 1)
            def _():
                finish(acc_ref[...])

    a_spec = {"nn": pl.BlockSpec((tm, tk), lambda i, j, kk: (i, kk)),
              "nt": pl.BlockSpec((tm, tk), lambda i, j, kk: (i, kk)),
              "tn": pl.BlockSpec((tk, tm), lambda i, j, kk: (kk, i))}[mode]
    b_spec = {"nn": pl.BlockSpec((tk, tn), lambda i, j, kk: (kk, j)),
              "nt": pl.BlockSpec((tn, tk), lambda i, j, kk: (j, kk)),
              "tn": pl.BlockSpec((tk, tn), lambda i, j, kk: (kk, j))}[mode]
    tile = pl.BlockSpec((tm, tn), lambda i, j, kk: (i, j))
    out_spec, out_dims = tile, (m, n)
    if b_shards and mode == "nn":
        per = cols_b // tn
        b_spec = pl.BlockSpec((None, tk, tn), lambda i, j, kk: (j // per, kk, j % per))
    elif b_shards:
        assert mode == "nt"
        per = cols_b // tk
        b_spec = pl.BlockSpec((None, tn, tk), lambda i, j, kk: (kk // per, j, kk % per))
    if out_shards:
        assert not extras and epilogue is None
        per_out = n // N_CHIPS // tn
        out_spec = pl.BlockSpec((None, tm, tn), lambda i, j, kk: (j // per_out, i, j % per_out))
        out_dims = (N_CHIPS, m, n // N_CHIPS)
    outs = pl.pallas_call(
        body, name=name,
        grid=(m // tm, n // tn, nk),
        in_specs=[a_spec, b_spec] + [tile] * n_ex + [pl.BlockSpec(memory_space=pl.ANY)] * (after is not None),
        out_specs=[out_spec] * n_out,
        out_shape=[jax.ShapeDtypeStruct(out_dims, dt) for dt in out_dtypes],
        scratch_shapes=[pltpu.VMEM((tm, tn), F32)] if nk > 1 else [],
        compiler_params=_params(("parallel", "parallel", "arbitrary")),
    )(a, b, *extras, *([after] if after is not None else []))
    return outs[0] if n_out == 1 else outs


def _rowwise(fn, *, name, rows, seq, ins, outs, sums=()):
    tm = min(ROW_T, seq)
    n_pos = seq // tm
    n_in, n_out, n_sum = len(ins), len(outs), len(sums)

    def body(*refs):
        vals = fn(*[r[...] for r in refs[:n_in]])
        for r, v in zip(refs[n_in:n_in + n_out], vals[:n_out]):
            r[...] = v.astype(r.dtype)
        first = pl.program_id(0) == 0
        for r, v in zip(refs[n_in + n_out:], vals[n_out:]):
            @pl.when(first)
            def _(r=r, v=v):
                r[...] = v

            @pl.when(jnp.logical_not(first))
            def _(r=r, v=v):
                r[...] += v

    def spec(arr, width, col, kind):
        if kind == "row":
            return pl.BlockSpec((tm, width), lambda i, col=col: (i, col))
        if kind == "pos":
            return pl.BlockSpec((tm, width), lambda i, col=col: (i % n_pos, col))
        return pl.BlockSpec(arr.shape, lambda i: (0,) * arr.ndim)

    res = pl.pallas_call(
        body, name=name,
        grid=(rows // tm,),
        in_specs=[spec(*t) for t in ins],
        out_specs=[pl.BlockSpec((tm, w), lambda i: (i, 0)) for w, _ in outs]
        + [pl.BlockSpec((1, w), lambda i: (0, 0)) for w in sums],
        out_shape=[jax.ShapeDtypeStruct((rows, w), dt) for w, dt in outs]
        + [jax.ShapeDtypeStruct((1, w), F32) for w in sums],
        compiler_params=_params(("arbitrary",)),
    )(*[t[0] for t in ins])
    return res


def _colsum(v):
    return jnp.sum(v, axis=0, keepdims=True)


def _rope_fwd(t, c, s_up, s_dn):
    return t * c + pltpu.roll(t, LANES - 16, 1) * s_up + pltpu.roll(t, 16, 1) * s_dn


def _rope_bwd(d, c, s_up, s_dn):
    return d * c + pltpu.roll(d * s_up, 16, 1) + pltpu.roll(d * s_dn, LANES - 16, 1)


def _rope_tables(seq):
    half = ROPE // 2
    inv = jnp.power(ROPE_THETA, -jnp.arange(half, dtype=F32) / half)
    ang = jnp.arange(seq, dtype=F32)[:, None] * inv[None, :]
    cos, sin = jnp.cos(ang), jnp.sin(ang)
    zeros = jnp.zeros((seq, half), F32)
    lo, hi = jnp.ones((seq, KR_LANE), F32), jnp.ones((seq, LANES - KR_LANE - ROPE), F32)
    c = jnp.concatenate([lo, cos, cos, hi], axis=1)
    c_rope_only = jnp.concatenate([0 * lo, cos, cos, 0 * hi], axis=1)
    s_up = jnp.concatenate([0 * lo, -sin, zeros, 0 * hi], axis=1)
    s_dn = jnp.concatenate([0 * lo, zeros, sin, 0 * hi], axis=1)
    return c, s_up, s_dn, c_rope_only


def _rms(x, g):
    r = lax.rsqrt(jnp.mean(x * x, axis=1, keepdims=True) + RMS_EPS)
    return x * r * g


def _rms_bwd(x, g, dy):
    r = lax.rsqrt(jnp.mean(x * x, axis=1, keepdims=True) + RMS_EPS)
    xh = x * r
    dxh = dy * g
    dx = r * (dxh - xh * jnp.mean(dxh * xh, axis=1, keepdims=True))
    return dx, _colsum(dy * xh)


def _ln_stats(x):
    mu = jnp.mean(x, axis=1, keepdims=True)
    xc = x - mu
    r = lax.rsqrt(jnp.mean(xc * xc, axis=1, keepdims=True) + LN_EPS)
    return xc * r, r


def _ln_bwd(xh, r, g, dy):
    dxh = dy * g
    dx = r * (dxh - jnp.mean(dxh, axis=1, keepdims=True) - xh * jnp.mean(dxh * xh, axis=1, keepdims=True))
    return dx, _colsum(dy * xh), _colsum(dy)


def _bias_spec(bias):
    per_head = bias.shape[0] > 1
    return pl.BlockSpec((1,) + bias.shape[1:], lambda b, h: (h if per_head else 0, 0, 0, 0))


def _attn_fwd(q, qb0, k, kb0, v, vb0, bias, scale, *, name, batch, seq):
    t = ATT_T
    nq = seq // t
    rows = batch * seq

    def body(q_ref, k_ref, v_ref, bias_ref, o_ref, lse_ref, qb, kb, vb):
        qb[...] = q_ref[...].astype(BF16)
        kb[...] = k_ref[...].astype(BF16)
        vb[...] = v_ref[...].astype(BF16)
        for i in range(nq):
            qt = qb[i * t:(i + 1) * t, :]
            logits = [lax.dot_general(qt, kb[j * t:(j + 1) * t, :], NT, preferred_element_type=F32) * scale + bias_ref[0, i - j]
                      for j in range(i + 1)]
            top = functools.reduce(jnp.maximum, logits)
            m = jnp.max(top, axis=1, keepdims=True)
            ps = [jnp.exp(s - m) for s in logits]
            l = jnp.sum(functools.reduce(jnp.add, ps), axis=1, keepdims=True)
            acc = functools.reduce(jnp.add, [lax.dot_general(p.astype(BF16), vb[j * t:(j + 1) * t, :], NN, preferred_element_type=F32)
                                             for j, p in enumerate(ps)])
            o_ref[i * t:(i + 1) * t, :] = acc / l
            lse_ref[i * t:(i + 1) * t, :] = jnp.broadcast_to(m + jnp.log(l), (t, LANES))

    slab = lambda b0: pl.BlockSpec((seq, LANES), lambda b, h, b0=b0: (b, b0 + h))
    return pl.pallas_call(
        body, name=name,
        grid=(batch, N_HEADS),
        in_specs=[slab(qb0), slab(kb0), slab(vb0), _bias_spec(bias)],
        out_specs=[slab(0), slab(0)],
        out_shape=[jax.ShapeDtypeStruct((rows, N_HEADS * LANES), F32)] * 2,
        scratch_shapes=[pltpu.VMEM((seq, LANES), BF16)] * 3,
        compiler_params=_params(("arbitrary", "arbitrary")),
    )(q, k, v, bias)


def _attn_bwd(q, qb0, k, kb0, v, vb0, o, do, lse, bias, scale, *, name, batch, seq, out_dtype, after=None):
    t = ATT_T
    nq = seq // t
    rows = batch * seq

    def body(q_ref, k_ref, v_ref, o_ref, do_ref, lse_ref, bias_ref, *rest):
        dq_ref, dk_ref, dv_ref, qb, kb, vb, dob, qtb, dotb, dka, dva = rest[after is not None:]
        qb[...] = q_ref[...].astype(BF16)
        kb[...] = k_ref[...].astype(BF16)
        vb[...] = v_ref[...].astype(BF16)
        dob[...] = do_ref[...].astype(BF16)
        qtb[...] = q_ref[...].astype(F32).T.astype(BF16)
        dotb[...] = do_ref[...].T.astype(BF16)
        for i in range(nq):
            at = slice(i * t, (i + 1) * t)
            qt, dot = qb[at, :], dob[at, :]
            lse_t = lse_ref[at, 0:1]
            delta = jnp.sum(o_ref[at, :] * do_ref[at, :], axis=1, keepdims=True)
            dq = None
            for j in range(i + 1):
                kat = slice(j * t, (j + 1) * t)
                kt, vt = kb[kat, :], vb[kat, :]
                p = jnp.exp(lax.dot_general(qt, kt, NT, preferred_element_type=F32) * scale + bias_ref[0, i - j] - lse_t)
                dp = lax.dot_general(dot, vt, NT, preferred_element_type=F32)
                ds = (p * (dp - delta) * scale).astype(BF16)
                dk_part = lax.dot_general(qtb[:, at], ds, NN, preferred_element_type=F32)
                dv_part = lax.dot_general(dotb[:, at], p.astype(BF16), NN, preferred_element_type=F32)
                if i == j:
                    dka[:, kat] = dk_part
                    dva[:, kat] = dv_part
                else:
                    dka[:, kat] += dk_part
                    dva[:, kat] += dv_part
                dq_part = lax.dot_general(ds, kt, NN, preferred_element_type=F32)
                dq = dq_part if dq is None else dq + dq_part
            dq_ref[at, :] = dq.astype(dq_ref.dtype)
        dk_ref[...] = dka[...].T.astype(dk_ref.dtype)
        dv_ref[...] = dva[...].T.astype(dv_ref.dtype)

    slab = lambda b0: pl.BlockSpec((seq, LANES), lambda b, h, b0=b0: (b, b0 + h))
    return pl.pallas_call(
        body, name=name,
        grid=(batch, N_HEADS),
        in_specs=[slab(qb0), slab(kb0), slab(vb0), slab(0), slab(0), slab(0), _bias_spec(bias)] + [pl.BlockSpec(memory_space=pl.ANY)] * (after is not None),
        out_specs=[slab(0)] * 3,
        out_shape=[jax.ShapeDtypeStruct((rows, N_HEADS * LANES), out_dtype)] * 3,
        scratch_shapes=[pltpu.VMEM((seq, LANES), BF16)] * 4 + [pltpu.VMEM((LANES, seq), BF16)] * 2 + [pltpu.VMEM((LANES, seq), F32)] * 2,
        compiler_params=_params(("arbitrary", "arbitrary")),
    )(q, k, v, o, do, lse, bias, *([after] if after is not None else []))


def _tile_dist(seq):
    n = seq // ATT_T
    d = jnp.arange(n, dtype=jnp.int32)[:, None, None] * ATT_T
    return d + jnp.arange(ATT_T, dtype=jnp.int32)[None, :, None] - jnp.arange(ATT_T, dtype=jnp.int32)[None, None, :]


def _causal_bias(seq):
    return jnp.where(_tile_dist(seq) >= 0, 0.0, NEG).astype(F32)[None]


def _dilated_bias(seq):
    dist = _tile_dist(seq)
    count = jnp.zeros(dist.shape, F32)
    for window, dilation in DIL_PATTERNS:
        count += ((dist >= 0) & (dist <= window) & (dist % dilation == 0)).astype(F32)
    slopes = jnp.asarray([2.0 ** (-8.0 * (i + 1) / N_HEADS) for i in range(N_HEADS)], F32)
    alibi = -slopes[:, None, None, None] * dist.astype(F32)[None]
    return jnp.where(count[None] > 0, jnp.log(jnp.maximum(count, 1.0))[None] + alibi, NEG).astype(F32)


def _pad_heads(w, width):
    kdim, n = w.shape[0], w.shape[1] // width
    return jnp.pad(w.reshape(kdim, n, width), ((0, 0), (0, 0), (0, LANES - width))).reshape(kdim, n * LANES)


def _unpad_heads(w, width):
    kdim, n = w.shape[0], w.shape[1] // LANES
    return w.reshape(kdim, n, LANES)[:, :, :width].reshape(kdim, n * width)


def _pad_head_rows(w, width):
    n, kdim = w.shape[0] // width, w.shape[1]
    return jnp.pad(w.reshape(n, width, kdim), ((0, 0), (0, LANES - width), (0, 0))).reshape(n * LANES, kdim)


def _unpad_head_rows(w, width):
    n, kdim = w.shape[0] // LANES, w.shape[1]
    return w.reshape(n, LANES, kdim)[:, :width].reshape(n * width, kdim)


def _pad_w_in_t(wt):
    n_qkv = 3 * N_HEADS * DIL_DIM
    zeros = lambda n: jnp.zeros((n, wt.shape[1]), wt.dtype)
    return jnp.concatenate([wt[LORA_W + ROPE + n_qkv:], _pad_head_rows(wt[LORA_W + ROPE:LORA_W + ROPE + n_qkv], DIL_DIM), wt[:LORA_W],
                            zeros(KR_LANE), wt[LORA_W:LORA_W + ROPE], zeros(LANES - KR_LANE - ROPE), zeros(P_WIDTH - P_KR - LANES)], axis=0)


def _unpad_w_in_t(gt):
    return jnp.concatenate([gt[P_LORA:P_KR], gt[P_KR + KR_LANE:P_KR + KR_LANE + ROPE], _unpad_head_rows(gt[P_QD:P_LORA], DIL_DIM),
                            gt[P_GATE:P_QD]], axis=0)


def _split_ukv(w):
    w3 = w.reshape(w.shape[0], N_HEADS, NOPE + V_DIM)
    return (_pad_heads(w3[:, :, :NOPE].reshape(w.shape[0], -1), NOPE),
            _pad_heads(w3[:, :, NOPE:].reshape(w.shape[0], -1), V_DIM))


def _merge_ukv(g_k, g_v):
    kdim = g_k.shape[0]
    k3 = _unpad_heads(g_k, NOPE).reshape(kdim, N_HEADS, NOPE)
    v3 = _unpad_heads(g_v, V_DIM).reshape(kdim, N_HEADS, V_DIM)
    return jnp.concatenate([k3, v3], axis=2).reshape(kdim, N_HEADS * (NOPE + V_DIM))


def _pad_rows(w, width):
    return _pad_heads(w.T, width).T


def _unpad_rows(g, width):
    return _unpad_heads(g.T, width).T


def _join_cols(w):
    return w.transpose(1, 0, 2).reshape(w.shape[1], N_CHIPS * w.shape[2])


def _split_cols(g):
    return g.reshape(g.shape[0], N_CHIPS, g.shape[1] // N_CHIPS).transpose(1, 0, 2)


def _local_step(x3, target3, wg, b_gate, g_q_a, g_kv_a, ln1_g, ln1_b, ln2_g, ln2_b, token=None, late_weights=None, early_grads=None,
                early_grads_go=None, last_grads=None):
    w_in_pt = _pad_w_in_t(wg["w_in"].reshape(IN_WIDTH, D_MODEL))
    w_uq_pt = _pad_head_rows(wg["w_uq"].reshape(N_HEADS * MLA_QK, Q_LORA), MLA_QK)
    w_ukv = _join_cols(wg["w_ukv"])
    batch, seq, _ = x3.shape
    rows = batch * seq
    x = x3.reshape(rows, D_MODEL)
    target = target3.reshape(rows, D_MODEL)
    row = functools.partial(_rowwise, rows=rows, seq=seq)
    mm = _matmul

    w_uk_p, w_uv_p = _split_ukv(w_ukv)
    b0, b1 = b_gate[0:1], b_gate[1:2]
    rope_c, rope_up, rope_dn, rope_c_only = _rope_tables(seq)
    bias_mla, bias_dil = _causal_bias(seq), _dilated_bias(seq)
    scale_mla, scale_dil = MLA_QK ** -0.5, DIL_DIM ** -0.5
    qd0, kd0, vd0, lora0, kr0 = P_QD // LANES, P_KD // LANES, P_VD // LANES, P_LORA // LORA_W, P_KR // LANES

    proj = mm(x, w_in_pt, mode="nt", name="proj", tm=1024, tn=1536, tk=1024, after=token)

    def prep(lora, gq, gkv):
        return _rms(lora[:, :Q_LORA], gq), _rms(lora[:, Q_LORA:], gkv)

    qn, kvn = row(prep, name="mla_rms", ins=[(proj, LORA_W, lora0, "row"), (g_q_a, 0, 0, "full"), (g_kv_a, 0, 0, "full")],
                  outs=[(Q_LORA, BF16), (KV_LORA, BF16)])
    q_lin = mm(qn, w_uq_pt, mode="nt", name="q_up", tm=1024, tn=1024, tk=Q_LORA)
    k_lin = mm(kvn, w_uk_p, mode="nn", name="k_up", tm=1024, tn=1024, tk=KV_LORA)
    v_a = mm(kvn, w_uv_p, mode="nn", name="v_up", tm=1024, tn=1024, tk=KV_LORA, out_dtypes=(BF16,))

    def rope_qk(ql, kl, kr, c, up, dn):
        k_rot = _rope_fwd(kr, c, up, dn)
        qs = [_rope_fwd(ql[:, h * LANES:(h + 1) * LANES], c, up, dn) for h in range(N_HEADS)]
        ks = [kl[:, h * LANES:(h + 1) * LANES] + k_rot for h in range(N_HEADS)]
        return jnp.concatenate(qs, axis=1), jnp.concatenate(ks, axis=1)

    pos = lambda tab: (tab, LANES, 0, "pos")
    q_a, k_a = row(rope_qk, name="rope_qk",
                   ins=[(q_lin, D_MODEL, 0, "row"), (k_lin, D_MODEL, 0, "row"), (proj, LANES, kr0, "row"), pos(rope_c), pos(rope_up), pos(rope_dn)],
                   outs=[(N_HEADS * LANES, BF16), (N_HEADS * LANES, BF16)])
    o_a, lse_a = _attn_fwd(q_a, 0, k_a, 0, v_a, 0, bias_mla, scale_mla, name="mla_fwd", batch=batch, seq=seq)
    o_b, lse_b = _attn_fwd(proj, qd0, proj, kd0, proj, vd0, bias_dil, scale_dil, name="dil_fwd", batch=batch, seq=seq)
    late = wg if late_weights is None else late_weights(o_b)
    w_oa_p = _pad_rows(_join_cols(late["w_o_mla"]), V_DIM)
    w_ob_p = _pad_rows(_join_cols(late["w_o_dil"]), DIL_DIM)
    w_out, w_ff1, w_ff2 = late["w_out"].reshape(D_MODEL, D_MODEL), late["w_ff1"], late["w_ff2"].reshape(D_FF, D_MODEL)
    y_a = mm(o_a, w_oa_p, mode="nn", name="o_mla", tm=1024, tn=1024, tk=1024)
    y_b = mm(o_b, w_ob_p, mode="nn", name="o_dil", tm=1024, tn=1024, tk=1024)

    def gate(t0, t1, c0, c1, ya, yb):
        return (jax.nn.sigmoid(t0 + c0) * ya + jax.nn.sigmoid(t1 + c1) * yb,)

    gate_ins = [(proj, D_MODEL, 0, "row"), (proj, D_MODEL, 1, "row"), (b0, 0, 0, "full"), (b1, 0, 0, "full")]
    (u,) = row(gate, name="gate", ins=gate_ins + [(y_a, D_MODEL, 0, "row"), (y_b, D_MODEL, 0, "row")], outs=[(D_MODEL, BF16)])
    mixed = mm(u, w_out, mode="nn", name="mix", tm=1024, tn=1024, tk=1024)

    def ln1(xv, mv, g, b):
        r1 = ALPHA * xv + mv
        xh, _ = _ln_stats(r1)
        return r1, xh * g + b

    r1, h = row(ln1, name="ln1", ins=[(x, D_MODEL, 0, "row"), (mixed, D_MODEL, 0, "row"), (ln1_g, 0, 0, "full"), (ln1_b, 0, 0, "full")],
                outs=[(D_MODEL, F32), (D_MODEL, F32)])

    def relu2(acc):
        r = jnp.maximum(acc, 0.0)
        return acc, r * r

    a_ff, z = mm(h, w_ff1, mode="nn", name="ff1", tm=1024, tn=1024, tk=1024, out_dtypes=(F32, BF16), epilogue=relu2, b_shards=True)
    f = mm(z, w_ff2, mode="nn", name="ff2", tm=1024, tn=1024, tk=1024)

    def ln2_loss(hv, fv, tv, g, b):
        xh, r = _ln_stats(ALPHA * hv + fv)
        err = xh * g + b - tv
        dy = err * (1.0 / D_MODEL)
        dr2, dg, db = _ln_bwd(xh, r, g, dy)
        loss = jnp.sum(_colsum(err * err), axis=1, keepdims=True) * (0.5 / D_MODEL)
        return dr2, jnp.broadcast_to(loss, (1, LANES)), dg, db

    dr2, loss_l, d_ln2_g, d_ln2_b = row(
        ln2_loss, name="ln2_loss",
        ins=[(h, D_MODEL, 0, "row"), (f, D_MODEL, 0, "row"), (target, D_MODEL, 0, "row"), (ln2_g, 0, 0, "full"), (ln2_b, 0, 0, "full")],
        outs=[(D_MODEL, F32)], sums=[LANES, D_MODEL, D_MODEL])

    d_w_ff2 = mm(z, dr2, mode="tn", name="d_w_ff2", tm=1024, tn=1024, tk=1024)
    da = mm(dr2, w_ff2, mode="nt", name="d_ff_act", tm=1024, tn=1024, tk=1024, out_dtypes=(BF16,), extras=(a_ff,),
            epilogue=lambda acc, av: (acc * (2.0 * jnp.maximum(av, 0.0)),))
    d_w_ff1 = mm(h, da, mode="tn", name="d_w_ff1", tm=1024, tn=1024, tk=1024, out_shards=True)
    dh = mm(da, w_ff1, mode="nt", name="d_h", tm=1024, tn=1024, tk=1024, extras=(dr2,), epilogue=lambda acc, rv: (acc + ALPHA * rv,), b_shards=True)

    def ln1_bwd(dhv, r1v, g):
        xh, r = _ln_stats(r1v)
        return _ln_bwd(xh, r, g, dhv)

    dr1, d_ln1_g, d_ln1_b = row(ln1_bwd, name="ln1_bwd", ins=[(dh, D_MODEL, 0, "row"), (r1, D_MODEL, 0, "row"), (ln1_g, 0, 0, "full")],
                                outs=[(D_MODEL, F32)], sums=[D_MODEL, D_MODEL])
    d_w_out = mm(u, dr1, mode="tn", name="d_w_out", tm=1024, tn=1024, tk=1024)
    du = mm(dr1, w_out, mode="nt", name="d_u", tm=1024, tn=1024, tk=1024)

    def gate_bwd(t0, t1, c0, c1, ya, yb, duv):
        s0, s1 = jax.nn.sigmoid(t0 + c0), jax.nn.sigmoid(t1 + c1)
        dt0 = duv * ya * s0 * (1.0 - s0)
        dt1 = duv * yb * s1 * (1.0 - s1)
        return duv * s0, duv * s1, jnp.concatenate([dt0, dt1], axis=1), jnp.concatenate([_colsum(dt0), _colsum(dt1)], axis=1)

    dy_a, dy_b, d_gates, d_b_gate = row(
        gate_bwd, name="gate_bwd", ins=gate_ins + [(y_a, D_MODEL, 0, "row"), (y_b, D_MODEL, 0, "row"), (du, D_MODEL, 0, "row")],
        outs=[(D_MODEL, BF16), (D_MODEL, BF16), (2 * D_MODEL, BF16)], sums=[2 * D_MODEL])
    d_w_oa_p = mm(o_a, dy_a, mode="tn", name="d_w_o_mla", tm=1024, tn=1024, tk=1024)
    d_w_ob_p = mm(o_b, dy_b, mode="tn", name="d_w_o_dil", tm=1024, tn=1024, tk=1024)
    grads = dict(w_o_mla=_split_cols(_unpad_rows(d_w_oa_p, V_DIM)), w_o_dil=_split_cols(_unpad_rows(d_w_ob_p, DIL_DIM)),
                 w_out=d_w_out.reshape(N_CHIPS, D_MODEL // N_CHIPS, D_MODEL), w_ff1=d_w_ff1, w_ff2=d_w_ff2.reshape(N_CHIPS, D_FF // N_CHIPS, D_MODEL))
    sent = None if early_grads is None else early_grads(grads)
    do_a = mm(dy_a, w_oa_p, mode="nt", name="d_o_mla", tm=1024, tn=1024, tk=1024, after=sent)
    do_b = mm(dy_b, w_ob_p, mode="nt", name="d_o_dil", tm=1024, tn=1024, tk=1024)
    going = None if early_grads_go is None else early_grads_go(do_b)
    dq_a, dk_a, dv_a = _attn_bwd(q_a, 0, k_a, 0, v_a, 0, o_a, do_a, lse_a, bias_mla, scale_mla,
                                 name="mla_bwd", batch=batch, seq=seq, out_dtype=F32, after=going)
    dq_d, dk_d, dv_d = _attn_bwd(proj, qd0, proj, kd0, proj, vd0, o_b, do_b, lse_b, bias_dil, scale_dil,
                                 name="dil_bwd", batch=batch, seq=seq, out_dtype=BF16)

    def mla_post(dq, dk, c, up, dn, c_only):
        dqs = [_rope_bwd(dq[:, h * LANES:(h + 1) * LANES], c, up, dn) for h in range(N_HEADS)]
        dk_sum = dk[:, :LANES]
        for h in range(1, N_HEADS):
            dk_sum = dk_sum + dk[:, h * LANES:(h + 1) * LANES]
        return jnp.concatenate(dqs, axis=1), _rope_bwd(dk_sum, c_only, up, dn)

    dq_lin, d_kr = row(mla_post, name="mla_unrope",
                       ins=[(dq_a, D_MODEL, 0, "row"), (dk_a, D_MODEL, 0, "row"), pos(rope_c), pos(rope_up), pos(rope_dn), pos(rope_c_only)],
                       outs=[(N_HEADS * LANES, BF16), (LANES, BF16)])
    d_w_uq_pt = mm(dq_lin, qn, mode="tn", name="d_w_uq", tm=1024, tn=Q_LORA, tk=1024)
    d_w_uk_p = mm(kvn, dk_a, mode="tn", name="d_w_uk", tm=KV_LORA, tn=1024, tk=1024)
    d_w_uv_p = mm(kvn, dv_a, mode="tn", name="d_w_uv", tm=KV_LORA, tn=1024, tk=1024)
    d_qn = mm(dq_lin, w_uq_pt, mode="nn", name="d_qn", tm=1024, tn=Q_LORA, tk=1024)
    d_kvn_k = mm(dk_a, w_uk_p, mode="nt", name="d_kvn_k", tm=1024, tn=KV_LORA, tk=1024)
    d_kvn = mm(dv_a, w_uv_p, mode="nt", name="d_kvn", tm=1024, tn=KV_LORA, tk=1024, extras=(d_kvn_k,), epilogue=lambda acc, e: (acc + e,))

    def rms_bwd(lora, dq, dkv, gq, gkv):
        dxq, dgq = _rms_bwd(lora[:, :Q_LORA], gq, dq)
        dxk, dgk = _rms_bwd(lora[:, Q_LORA:], gkv, dkv)
        return jnp.concatenate([dxq, dxk], axis=1), dgq, dgk

    d_lora, d_g_q_a, d_g_kv_a = row(
        rms_bwd, name="mla_rms_bwd",
        ins=[(proj, LORA_W, lora0, "row"), (d_qn, Q_LORA, 0, "row"), (d_kvn, KV_LORA, 0, "row"), (g_q_a, 0, 0, "full"), (g_kv_a, 0, 0, "full")],
        outs=[(LORA_W, BF16)], sums=[Q_LORA, KV_LORA])
    d_proj = jnp.concatenate([d_gates, dq_d, dk_d, dv_d, d_lora, d_kr, jnp.zeros((rows, P_WIDTH - P_KR - LANES), BF16)], axis=1)
    d_w_in_pt = mm(d_proj, x, mode="tn", name="d_w_in", tm=1536, tn=1024, tk=1024)
    grads.update(w_in=_unpad_w_in_t(d_w_in_pt).reshape(N_CHIPS, IN_WIDTH // N_CHIPS, D_MODEL),
                 w_uq=_unpad_head_rows(d_w_uq_pt, MLA_QK).reshape(N_CHIPS, N_HEADS * MLA_QK // N_CHIPS, Q_LORA),
                 w_ukv=_split_cols(_merge_ukv(d_w_uk_p, d_w_uv_p)))
    leaving = None if last_grads is None else last_grads(grads)
    grad_x = mm(d_proj, w_in_pt, mode="nn", name="d_x", tm=1024, tn=1024, tk=1536, extras=(dr1,), epilogue=lambda acc, rv: (acc + ALPHA * rv,),
                after=leaving)

    grads.update(
        b_gate=d_b_gate.reshape(2, D_MODEL), g_q_a=d_g_q_a, g_kv_a=d_g_kv_a, ln1_g=d_ln1_g, ln1_b=d_ln1_b, ln2_g=d_ln2_g, ln2_b=d_ln2_b)
    return loss_l, grad_x.reshape(batch, seq, D_MODEL), grads


BIG = ("w_in", "w_uq", "w_ukv", "w_o_mla", "w_o_dil", "w_out", "w_ff1", "w_ff2")
SMALL = (("b_gate", 2 * D_MODEL), ("g_q_a", Q_LORA), ("g_kv_a", KV_LORA), ("ln1_g", D_MODEL), ("ln1_b", D_MODEL),
         ("ln2_g", D_MODEL), ("ln2_b", D_MODEL))
TRANSPOSED = ("w_in", "w_uq")
D2D_PIECES = (4, 2, 1)
ANY = pl.BlockSpec(memory_space=pl.ANY)
SIDE_EFFECTS = pltpu.CompilerParams(has_side_effects=True)


def _place():
    x, y, c = lax.axis_index("x"), lax.axis_index("y"), lax.axis_index("c")
    return x, y, c, ((1 - x, y), (x, 1 - y), (1 - x, 1 - y))


def _half_axis(shape):
    return 0 if shape[0] % 32 == 0 else 1


def _half_shape(shape):
    return (shape[0] // 2, shape[1]) if _half_axis(shape) == 0 else (shape[0], shape[1] // 2)


def _window(ref, lead, shape, which=None, pieces=False):
    axis = _half_axis(shape)
    size = shape[axis] if which is None else shape[axis] // 2
    base = 0 if which is None else which * size
    tile = (16, LANES)[axis]
    count = next(c for c in D2D_PIECES if size % (tile * c) == 0) if pieces else 1
    step = size // count
    spans = [pl.ds(pl.multiple_of(base + i * step, tile), step) for i in range(count)]
    refs = [ref.at[(*lead, s)] if axis == 0 else ref.at[(*lead, slice(None), s)] for s in spans]
    return refs if pieces else refs[0]


def _remote(src, dst, send, recv, to):
    return pltpu.make_async_remote_copy(src_ref=src, dst_ref=dst, send_sem=send, recv_sem=recv, device_id=to, device_id_type=MESH)


def _gather_weights(shards, name):
    n = len(shards)

    def body(*refs):
        srcs, outs, (send, recv) = refs[:n], refs[n:2 * n], refs[2 * n:]
        x, y, c, chips = _place()
        sibling, mine = (x, y, 1 - c), 2 * x + y
        shapes = [s.shape for s in shards]

        def over_ici(t, j, to):
            return _remote(_window(srcs[t], (), shapes[t], c), _window(outs[t], (mine,), shapes[t], c), send.at[t, j], recv.at[t, j], (*to, c))

        def over_d2d(t, j, slot, which, pieces=False):
            from_to = [_window(outs[t], (slot,), shapes[t], which, pieces)] * 2
            if j == 6:
                from_to[0] = _window(srcs[t], (), shapes[t], which, pieces)
            if pieces:
                return [_remote(a, b, send.at[t, j], recv.at[t, j], sibling) for a, b in zip(*from_to)]
            return _remote(*from_to, send.at[t, j], recv.at[t, j], sibling)

        for t in range(n):
            for j, chip in enumerate(chips):
                over_ici(t, j, chip).start()
        for t in range(n):
            for cp in over_d2d(t, 6, mine, None, pieces=True):
                cp.start()
        for t in range(n):
            for j, (cx, cy) in enumerate(chips):
                over_ici(t, j, (cx, cy)).wait_recv()
                for cp in over_d2d(t, 3 + j, 2 * cx + cy, c, pieces=True):
                    cp.start()
        for t in range(n):
            over_d2d(t, 6, mine, None).wait()
            for j, (cx, cy) in enumerate(chips):
                over_d2d(t, 3 + j, 2 * cx + cy, 1 - c).wait_recv()
                over_d2d(t, 3 + j, 2 * cx + cy, c).wait_send()
                over_ici(t, j, (cx, cy)).wait_send()

    return pl.pallas_call(
        body, name=name, in_specs=[ANY] * n, out_specs=[ANY] * n,
        out_shape=[jax.ShapeDtypeStruct((N_CHIPS,) + s.shape, s.dtype) for s in shards],
        scratch_shapes=[pltpu.SemaphoreType.DMA((n, 7)), pltpu.SemaphoreType.DMA((n, 7))],
        compiler_params=SIDE_EFFECTS,
    )(*shards)


def _pair_split(grads, name):
    n = len(grads)

    def body(*refs):
        srcs, outs, (send, recv) = refs[:n], refs[n:2 * n], refs[2 * n:]
        x, y, c, _ = _place()
        for t in range(n):
            for s in range(N_CHIPS):
                _remote(_window(srcs[t], (s,), grads[t].shape[1:], 1 - c), outs[t].at[s], send.at[t], recv.at[t], (x, y, 1 - c)).start()
        for t in range(n):
            _remote(_window(srcs[t], (slice(None),), grads[t].shape[1:], 1 - c), outs[t], send.at[t], recv.at[t], (x, y, 1 - c)).wait()

    return pl.pallas_call(
        body, name=name, in_specs=[ANY] * n, out_specs=[ANY] * n,
        out_shape=[jax.ShapeDtypeStruct((N_CHIPS,) + _half_shape(g.shape[1:]), g.dtype) for g in grads],
        scratch_shapes=[pltpu.SemaphoreType.DMA((n,)), pltpu.SemaphoreType.DMA((n,))],
        compiler_params=SIDE_EFFECTS,
    )(*grads)


def _pair_join(totals, name):
    n = len(totals)

    def body(*refs):
        srcs, outs, (send, recv) = refs[:n], refs[n:2 * n], refs[2 * n:]
        x, y, c, _ = _place()
        for t in range(n):
            for a, b in zip(_window(srcs[t], (), totals[t].shape, None, True), _window(outs[t], (), totals[t].shape, None, True)):
                _remote(a, b, send.at[t], recv.at[t], (x, y, 1 - c)).start()
        for t in range(n):
            _remote(srcs[t], outs[t], send.at[t], recv.at[t], (x, y, 1 - c)).wait()

    return pl.pallas_call(
        body, name=name, in_specs=[ANY] * n, out_specs=[ANY] * n,
        out_shape=[jax.ShapeDtypeStruct(t.shape, t.dtype) for t in totals],
        scratch_shapes=[pltpu.SemaphoreType.DMA((n,)), pltpu.SemaphoreType.DMA((n,))],
        compiler_params=SIDE_EFFECTS,
    )(*totals)


HBM = pl.BlockSpec(memory_space=pltpu.HBM)
SEM = pl.BlockSpec(memory_space=pltpu.SEMAPHORE)
SPLIT = pltpu.CompilerParams(has_side_effects=pltpu.SideEffectType.DATAFLOW_SIDE_EFFECTING)


def _in_hbm(a):
    return pltpu.with_memory_space_constraint(a, pltpu.HBM)


def _split_copies(kind, srcs, lands):
    x, y, c, chips = _place()
    out = []
    for t in range(len(srcs)):
        if kind == "pair":
            out += [(t, s % 3, _window(srcs[t], (s,), srcs[t].shape[1:], 1 - c), lands[t].at[s], (x, y, 1 - c)) for s in range(N_CHIPS)]
            continue
        for j, (cx, cy) in enumerate(chips):
            if kind == "gather":
                shape = srcs[t].shape
                out.append((t, j, _window(srcs[t], (), shape, c), _window(lands[t], (2 * x + y,), shape, c), (cx, cy, c)))
            else:
                out.append((t, j, srcs[t].at[2 * cx + cy], lands[t].at[j], (cx, cy, c)))
    return out


def _split_start(kind, srcs, land_shapes, name):
    n = len(srcs)

    def body(*refs):
        src_refs, land_refs, sems, token = refs[:n], refs[n:2 * n], refs[2 * n:2 * n + 6], refs[-1]
        for t, j, s, d, to in _split_copies(kind, src_refs, land_refs):
            _remote(s, d, sems[j], sems[3 + j], to).start()
        token[...] = jnp.zeros_like(token)

    lands = [_in_hbm(lax.empty(s.shape, s.dtype)) for s in land_shapes]
    thru = [pltpu.HBM(a.shape, a.dtype) for a in list(srcs) + lands]
    res = pl.pallas_call(
        body, name=name,
        out_shape=(*[pltpu.SemaphoreType.DMA(())] * 6, *thru, jax.ShapeDtypeStruct((8, LANES), F32)),
        in_specs=[HBM] * (2 * n), out_specs=(*[SEM] * 6, *[HBM] * (2 * n), pl.BlockSpec(memory_space=pltpu.VMEM)),
        input_output_aliases={i: 6 + i for i in range(2 * n)}, compiler_params=SPLIT,
    )(*[_in_hbm(s) for s in srcs], *lands)
    return res[:6], res[6:6 + n], res[6 + n:6 + 2 * n], res[-1]


def _split_wait(kind, sems, srcs, lands, after, name):
    n = len(srcs)

    def body(*refs):
        src_refs, land_refs, sem_refs = refs[:n], refs[n:2 * n], refs[2 * n:2 * n + 6]
        for t, j, s, d, to in _split_copies(kind, src_refs, land_refs):
            cp = _remote(s, d, sem_refs[j], sem_refs[3 + j], to)
            cp.wait_send()
            cp.wait_recv()

    res = pl.pallas_call(
        body, name=name, out_shape=[pltpu.HBM(a.shape, a.dtype) for a in list(srcs) + list(lands)],
        in_specs=[HBM] * (2 * n) + [SEM] * 6 + [ANY], out_specs=[HBM] * (2 * n),
        input_output_aliases={i: i for i in range(2 * n)}, compiler_params=SPLIT,
    )(*srcs, *lands, *sems, after)
    return res[:n], res[n:]


def _gather_finish(shards, lands, name):
    n = len(shards)

    def body(*refs):
        srcs, outs, (send, recv) = refs[:n], refs[2 * n:3 * n], refs[3 * n:]
        x, y, c, chips = _place()
        sibling = (x, y, 1 - c)
        for t in range(n):
            shape = shards[t].shape
            for a, b in zip(_window(srcs[t], (), shape, None, True), _window(outs[t], (2 * x + y,), shape, None, True)):
                _remote(a, b, send.at[t, 3], recv.at[t, 3], sibling).start()
            for j, (cx, cy) in enumerate(chips):
                for a in _window(outs[t], (2 * cx + cy,), shape, c, True):
                    _remote(a, a, send.at[t, j], recv.at[t, j], sibling).start()
        for t in range(n):
            shape = shards[t].shape
            _remote(srcs[t], outs[t].at[2 * x + y], send.at[t, 3], recv.at[t, 3], sibling).wait()
            for j, (cx, cy) in enumerate(chips):
                mine, other = (_window(outs[t], (2 * cx + cy,), shape, which) for which in (c, 1 - c))
                _remote(mine, mine, send.at[t, j], recv.at[t, j], sibling).wait_send()
                _remote(other, other, send.at[t, j], recv.at[t, j], sibling).wait_recv()

    return pl.pallas_call(
        body, name=name, in_specs=[ANY] * (2 * n), out_specs=[ANY] * n,
        out_shape=[jax.ShapeDtypeStruct(a.shape, a.dtype) for a in lands],
        scratch_shapes=[pltpu.SemaphoreType.DMA((n, 4)), pltpu.SemaphoreType.DMA((n, 4))],
        input_output_aliases={n + t: t for t in range(n)}, compiler_params=SIDE_EFFECTS,
    )(*shards, *lands)


def _sum_all_devices(vec, name):
    n_rows = vec.shape[0]

    def body(v_ref, out_ref, buf, send, recv):
        x, y, c, _ = _place()
        me = 4 * x + 2 * y + c
        buf[me] = v_ref[...]
        flips = [(a, b, d) for a in (0, 1) for b in (0, 1) for d in (0, 1)][1:]
        copies = []
        for r, (a, b, d) in enumerate(flips):
            px, py, pc = (1 - x if a else x), (1 - y if b else y), (1 - c if d else c)
            copies.append(pltpu.make_async_remote_copy(src_ref=v_ref, dst_ref=buf.at[me], send_sem=send.at[r], recv_sem=recv.at[r],
                                                       device_id=(px, py, pc), device_id_type=MESH))
            copies[-1].start()
        for r, (a, b, d) in enumerate(flips):
            px, py, pc = (1 - x if a else x), (1 - y if b else y), (1 - c if d else c)
            pltpu.make_async_remote_copy(src_ref=v_ref, dst_ref=buf.at[4 * px + 2 * py + pc], send_sem=send.at[r], recv_sem=recv.at[r],
                                         device_id=(px, py, pc), device_id_type=MESH).wait_recv()
        for cp in copies:
            cp.wait_send()
        total = buf[0]
        for k in range(1, N_DEV):
            total = total + buf[k]
        out_ref[...] = total

    vmem = pl.BlockSpec(memory_space=pltpu.VMEM)
    return pl.pallas_call(
        body, name=name, in_specs=[vmem], out_specs=vmem, out_shape=jax.ShapeDtypeStruct(vec.shape, F32),
        scratch_shapes=[pltpu.VMEM((N_DEV, n_rows, LANES), F32), pltpu.SemaphoreType.DMA((N_DEV - 1,)), pltpu.SemaphoreType.DMA((N_DEV - 1,))],
        compiler_params=pltpu.CompilerParams(has_side_effects=True),
    )(vec)


def _half_tile(half, width):
    t = half
    while t * width * 4 > (2 << 20) and t % 32 == 0:
        t //= 2
    return t


def _pair_add(g, theirs, core, name):
    _, half, width = theirs.shape
    t = _half_tile(half, width)
    n = half // t

    def body(c_ref, a_ref, b_ref, o_ref):
        o_ref[...] = (a_ref[...] + b_ref[...]).astype(BF16)

    tile = pl.BlockSpec((1, t, width), lambda j, i, c_ref: (j, i, 0))
    if _half_axis(g.shape[1:]) == 0:
        mine = pl.BlockSpec((1, t, width), lambda j, i, c_ref: (j, c_ref[0] * n + i, 0))
    else:
        mine = pl.BlockSpec((1, t, width), lambda j, i, c_ref: (j, i, c_ref[0]))
    return pl.pallas_call(
        body, name=name,
        grid_spec=pltpu.PrefetchScalarGridSpec(num_scalar_prefetch=1, grid=(N_CHIPS, n), in_specs=[mine, tile], out_specs=tile),
        out_shape=jax.ShapeDtypeStruct(theirs.shape, BF16), compiler_params=_params(("parallel", "parallel")),
    )(core, g, theirs)


def _chip_sum(part, others, chip, name, after=None):
    _, half, width = part.shape
    t = _half_tile(half, width)

    def body(s_ref, mine, p0, p1, p2, *rest):
        o_ref = rest[-1]
        o_ref[...] = ((mine[0].astype(F32) + p0[0].astype(F32)) + p1[0].astype(F32)) + p2[0].astype(F32)

    return pl.pallas_call(
        body, name=name,
        grid_spec=pltpu.PrefetchScalarGridSpec(
            num_scalar_prefetch=1, grid=(half // t,),
            in_specs=[pl.BlockSpec((1, t, width), lambda i, s_ref: (s_ref[0], i, 0))]
            + [pl.BlockSpec((1, t, width), lambda i, s_ref, j=j: (j, i, 0)) for j in range(3)] + [pl.BlockSpec(memory_space=pl.ANY)] * (after is not None),
            out_specs=pl.BlockSpec((t, width), lambda i, s_ref: (i, 0))),
        out_shape=jax.ShapeDtypeStruct((half, width), F32), compiler_params=_params(("parallel",)),
    )(chip, part, others, others, others, *([after] if after is not None else []))


EARLY = ("w_in", "w_uq", "w_ukv")
LATE = ("w_o_mla", "w_o_dil", "w_out", "w_ff1", "w_ff2")


def _chip_partials(grads, names, core, tag):
    gs = [grads[n] for n in names]
    theirs = _pair_split(gs, "pair_split_" + tag)
    return [_pair_add(g, th, core, "pair_add_" + n) for g, th, n in zip(gs, theirs, names)]


def _sum_small(vals):
    n_in = len(vals)
    n_rows = sum(a.shape[0] * a.shape[1] // LANES for a in vals)
    pad_rows = -(-n_rows // 8) * 8

    def chunks(refs):
        return [(ref, a, j) for ref in refs for a in range(ref.shape[0]) for j in range(ref.shape[1] // LANES)]

    def body(*refs):
        ins, outs, (buf, send, recv) = refs[:n_in], refs[n_in:2 * n_in], refs[2 * n_in:]
        x, y, c, _ = _place()
        me = 4 * x + 2 * y + c
        for r, (ref, a, j) in enumerate(chunks(ins)):
            buf[me, r:r + 1, :] = ref[a:a + 1, j * LANES:(j + 1) * LANES]
        if pad_rows > n_rows:
            buf[me, n_rows:pad_rows, :] = jnp.zeros((pad_rows - n_rows, LANES), F32)
        flips = [(a, b, d) for a in (0, 1) for b in (0, 1) for d in (0, 1)][1:]
        peers = [((1 - x if a else x), (1 - y if b else y), (1 - c if d else c)) for a, b, d in flips]
        copies = [_remote(buf.at[me], buf.at[me], send.at[r], recv.at[r], peer) for r, peer in enumerate(peers)]
        for cp in copies:
            cp.start()
        for r, (px, py, pc) in enumerate(peers):
            _remote(buf.at[me], buf.at[4 * px + 2 * py + pc], send.at[r], recv.at[r], (px, py, pc)).wait_recv()
        for cp in copies:
            cp.wait_send()
        total = buf[0]
        for k in range(1, N_DEV):
            total = total + buf[k]
        for r, (ref, a, j) in enumerate(chunks(outs)):
            ref[a:a + 1, j * LANES:(j + 1) * LANES] = total[r:r + 1, :]

    vmem = pl.BlockSpec(memory_space=pltpu.VMEM)
    return pl.pallas_call(
        body, name="sum_small", in_specs=[vmem] * n_in, out_specs=[vmem] * n_in,
        out_shape=[jax.ShapeDtypeStruct(a.shape, F32) for a in vals],
        scratch_shapes=[pltpu.VMEM((N_DEV, pad_rows, LANES), F32), pltpu.SemaphoreType.DMA((N_DEV - 1,)), pltpu.SemaphoreType.DMA((N_DEV - 1,))],
        compiler_params=SIDE_EFFECTS,
    )(*vals)


def _adam_math(w, g, m, v):
    nm = B1 * m + (1.0 - B1) * g
    nv = B2 * v + (1.0 - B2) * (g * g)
    m_hat = nm / (1.0 - B1 ** ADAM_STEP)
    v_hat = nv / (1.0 - B2 ** ADAM_STEP)
    return -LR * (m_hat / (jnp.sqrt(v_hat) + ADAM_EPS) + WD * w), nm, nv


def _adamw_big(w, mine, theirs, m, v, core, name, side_by_side=False):
    rows, width = w.shape
    if side_by_side:
        t = next(c for c in (152, 96, 64, 32, 16, 8) if rows % c == 0)
        hb = None
        half_spec = pl.BlockSpec((t, width // 2), lambda i, c_ref: (i, 0))
    else:
        t = next(c for c in (256, 128, 64, 32, 16, 8) if (rows // 2) % c == 0)
        hb = rows // 2 // t
        half_spec = pl.BlockSpec((t, width), lambda i, c_ref: (i % hb, 0))

    def body(c_ref, w_ref, a_ref, b_ref, m_ref, v_ref, g_ref, d_ref, nm_ref, nv_ref):
        south = c_ref[0] == 0
        if side_by_side:
            g = jnp.where(south, jnp.concatenate([a_ref[...], b_ref[...]], axis=1), jnp.concatenate([b_ref[...], a_ref[...]], axis=1))
        else:
            g = jnp.where((pl.program_id(0) < hb) == south, a_ref[...], b_ref[...])
        g_ref[...] = g
        d_ref[...], nm_ref[...], nv_ref[...] = _adam_math(w_ref[...], g, m_ref[...], v_ref[...])

    spec = pl.BlockSpec((t, width), lambda i, c_ref: (i, 0))
    return pl.pallas_call(
        body, name=name,
        grid_spec=pltpu.PrefetchScalarGridSpec(num_scalar_prefetch=1, grid=(rows // t,),
                                               in_specs=[spec, half_spec, half_spec, spec, spec], out_specs=[spec] * 4),
        out_shape=[jax.ShapeDtypeStruct(w.shape, F32)] * 4, compiler_params=_params(("parallel",)),
    )(core, w, mine, theirs, m, v)


def _adamw_small(ws, gs, ms, vs):
    n = len(ws)

    def body(*refs):
        for t in range(n):
            w_ref, g_ref, m_ref, v_ref = (refs[k * n + t] for k in range(4))
            d, nm, nv = _adam_math(w_ref[...], g_ref[...], m_ref[...], v_ref[...])
            refs[4 * n + t][...] = d
            refs[5 * n + t][...] = nm
            refs[6 * n + t][...] = nv

    vmem = pl.BlockSpec(memory_space=pltpu.VMEM)
    res = pl.pallas_call(body, name="adamw_small", in_specs=[vmem] * (4 * n), out_specs=[vmem] * (3 * n),
                         out_shape=[jax.ShapeDtypeStruct(a.shape, F32) for a in ws] * 3)(*ws, *gs, *ms, *vs)
    return res[:n], res[n:2 * n], res[2 * n:]


def kernel(x, w_in, b_gate, g_q_a, w_uq, g_kv_a, w_ukv, w_o_mla, w_o_dil, w_out, ln1_g, ln1_b, w_ff1, w_ff2, ln2_g, ln2_b, loss_target, m_w_in, m_b_gate, m_g_q_a, m_w_uq, m_g_kv_a, m_w_ukv, m_w_o_mla, m_w_o_dil, m_w_out, m_ln1_g, m_ln1_b, m_w_ff1, m_w_ff2, m_ln2_g, m_ln2_b, v_w_in, v_b_gate, v_g_q_a, v_w_uq, v_g_kv_a, v_w_ukv, v_w_o_mla, v_w_o_dil, v_w_out, v_ln1_g, v_ln1_b, v_w_ff1, v_w_ff2, v_ln2_g, v_ln2_b):
    order = ("w_in", "b_gate", "g_q_a", "w_uq", "g_kv_a", "w_ukv", "w_o_mla", "w_o_dil", "w_out", "ln1_g", "ln1_b", "w_ff1", "w_ff2", "ln2_g", "ln2_b")
    w = dict(w_in=w_in, b_gate=b_gate, g_q_a=g_q_a, w_uq=w_uq, g_kv_a=g_kv_a, w_ukv=w_ukv, w_o_mla=w_o_mla, w_o_dil=w_o_dil, w_out=w_out,
             ln1_g=ln1_g, ln1_b=ln1_b, w_ff1=w_ff1, w_ff2=w_ff2, ln2_g=ln2_g, ln2_b=ln2_b)
    m = dict(w_in=m_w_in, b_gate=m_b_gate, g_q_a=m_g_q_a, w_uq=m_w_uq, g_kv_a=m_g_kv_a, w_ukv=m_w_ukv, w_o_mla=m_w_o_mla, w_o_dil=m_w_o_dil,
             w_out=m_w_out, ln1_g=m_ln1_g, ln1_b=m_ln1_b, w_ff1=m_w_ff1, w_ff2=m_w_ff2, ln2_g=m_ln2_g, ln2_b=m_ln2_b)
    v = dict(w_in=v_w_in, b_gate=v_b_gate, g_q_a=v_g_q_a, w_uq=v_w_uq, g_kv_a=v_g_kv_a, w_ukv=v_w_ukv, w_o_mla=v_w_o_mla, w_o_dil=v_w_o_dil,
             w_out=v_w_out, ln1_g=v_ln1_g, ln1_b=v_ln1_b, w_ff1=v_w_ff1, w_ff2=v_w_ff2, ln2_g=v_ln2_g, ln2_b=v_ln2_b)
    chip = 2 * lax.axis_index("x") + lax.axis_index("y")
    south = (lax.axis_index("c") == 0).astype(F32)
    gate_w = D_MODEL // N_CHIPS

    core = lax.axis_index("c").astype(jnp.int32).reshape(1)
    turn = lambda n, a: a.T if n in TRANSPOSED else a
    shards = {n: turn(n, w[n][0]).astype(BF16) for n in BIG}
    first = dict(zip(EARLY, _gather_weights([shards[n] for n in EARLY], "gather_early")))
    late_shards = [shards[n] for n in LATE]
    g_sems, g_srcs, g_lands, g_token = _split_start(
        "gather", late_shards, [jax.ShapeDtypeStruct((N_CHIPS,) + s.shape, BF16) for s in late_shards], "gather_late_start")
    b_mine = lax.dynamic_update_slice(jnp.zeros((2, D_MODEL), F32), b_gate[0] * south, (0, chip * gate_w))
    b_full = _sum_all_devices(b_mine.reshape(-1, LANES), "gather_b_gate").reshape(2, D_MODEL)

    def late_weights(after):
        srcs, lands = _split_wait("gather", g_sems, g_srcs, g_lands, after, "gather_late_wait")
        return dict(zip(LATE, _gather_finish(srcs, lands, "gather_late_finish")))

    sent = {}
    exchange_shapes = lambda parts: [jax.ShapeDtypeStruct((3,) + p.shape[1:], BF16) for p in parts]

    def early_grads(grads_late):
        gs = [grads_late[n] for n in LATE]
        shapes = [jax.ShapeDtypeStruct((N_CHIPS,) + _half_shape(g.shape[1:]), F32) for g in gs]
        sent["pair"] = _split_start("pair", gs, shapes, "pair_split_late_start")
        return sent["pair"][-1]

    def early_grads_go(after):
        gs, theirs = _split_wait("pair", *sent["pair"][:3], after, "pair_split_late_wait")
        parts = [_pair_add(g, th, core, "pair_add_" + n) for g, th, n in zip(gs, theirs, LATE)]
        sent["late"] = _split_start("scatter", parts, exchange_shapes(parts), "exchange_late_start")
        return sent["late"][-1]

    def last_grads(grads_early):
        parts = _chip_partials(grads_early, EARLY, core, "early")
        sent["early"] = _split_start("scatter", parts, exchange_shapes(parts), "exchange_early_start")
        return sent["early"][-1]

    loss_part, grad_x, grads = _local_step(x, loss_target, first, b_full, g_q_a, g_kv_a, ln1_g, ln1_b, ln2_g, ln2_b, token=g_token,
                                           late_weights=late_weights, early_grads=early_grads, early_grads_go=early_grads_go,
                                           last_grads=last_grads)

    g_out, delta, new_m, new_v = {}, {}, {}, {}
    chip1 = chip.astype(jnp.int32).reshape(1)

    def finish(names, parts, others, tag):
        totals = [_chip_sum(p, o, chip1, "chip_sum_" + n) for n, p, o in zip(names, parts, others)]
        for n, mine, theirs in zip(names, totals, _pair_join(totals, "pair_join_" + tag)):
            res = _adamw_big(turn(n, w[n][0]), mine, theirs, turn(n, m[n][0]), turn(n, v[n][0]), core, "adamw_" + n,
                             side_by_side=mine.shape[0] == shards[n].shape[0])
            g_out[n], delta[n], new_m[n], new_v[n] = (turn(n, r) for r in res)

    finish(LATE, *_split_wait("scatter", *sent["late"][:3], grad_x, "exchange_late_wait"), "late")
    small_names = [name for name, _ in SMALL]
    sums = _sum_small([grads[name] for name in small_names] + [loss_part])
    loss = sums[-1][0, 0]
    g_small = dict(zip(small_names, sums))
    g_small["b_gate"] = lax.dynamic_slice(g_small["b_gate"], (0, chip * gate_w), (2, gate_w))
    flat = lambda a: a.reshape(-1, a.shape[-1])
    res = _adamw_small(*[[flat(d[name]) for name in small_names] for d in (w, g_small, m, v)])
    g_out.update(g_small)
    for d, r in zip((delta, new_m, new_v), res):
        d.update(zip(small_names, r))
    done = res[0][0][0:1, 0:1]
    for n in LATE:
        done = done + delta[n][0:1, 0:1]
    finish(EARLY, *_split_wait("scatter", *sent["early"][:3], done, "exchange_early_wait"), "early")


    lead = lambda d: [d[name].reshape(w[name].shape) for name in order]
    return (loss, grad_x, *lead(g_out), *lead(delta), *lead(new_m), *lead(new_v))
```

```python
import functools
import math

import jax
import jax.numpy as jnp
from jax import lax
from jax.experimental import pallas as pl
from jax.experimental.pallas import tpu as pltpu

F32 = jnp.float32
BF16 = jnp.bfloat16
MESH = pl.DeviceIdType.MESH

D_MODEL = 1024
N_HEADS = 8
LANES = 128
NOPE, ROPE, V_DIM = 64, 32, 64
MLA_QK = NOPE + ROPE
Q_LORA, KV_LORA = 384, 256
DIL_DIM = 64
DIL_PATTERNS = ((128, 1), (512, 4), (2048, 16))
D_FF = 4096
N_CHIPS = 4
N_DEV = 8
IN_WIDTH = 4256
LN_EPS, RMS_EPS = 1e-5, 1e-6
NEG = -1e30
ALPHA = 2.0 ** 0.25
ROPE_THETA = 10000.0
LR, B1, B2, ADAM_EPS, WD, ADAM_STEP = 0.001, 0.9, 0.999, 1e-8, 0.01, 10

P_GATE, P_QD, P_KD, P_VD, P_LORA, P_KR, P_WIDTH = 0, 2048, 3072, 4096, 5120, 5760, 6144
LORA_W = Q_LORA + KV_LORA
KR_LANE = NOPE

ATT_T = 512
ROW_T = 512
VMEM_LIMIT = 56 * 1024 * 1024

NN = (((1,), (0,)), ((), ()))
NT = (((1,), (1,)), ((), ()))
TN = (((0,), (0,)), ((), ()))


def _params(sem=None, **kw):
    return pltpu.CompilerParams(dimension_semantics=sem, vmem_limit_bytes=VMEM_LIMIT, **kw)


def _matmul(a, b, *, mode, name, tm, tn, tk, out_dtypes=(F32,), extras=(), epilogue=None, b_shards=False, out_shards=False, after=None):
    pieces = list(a) if isinstance(a, (list, tuple)) else [a]
    n_pc = len(pieces)
    a_shape = (pieces[0].shape[0], sum(p.shape[1] for p in pieces))
    if b_shards:
        n_sh, rows_b, cols_b = b.shape
        b_shape = (rows_b, n_sh * cols_b)
    else:
        b_shape = b.shape
    if mode == "nn":
        (m, k), (k2, n) = a_shape, b_shape
    elif mode == "nt":
        (m, k), (n, k2) = a_shape, b_shape
    else:
        (k, m), (k2, n) = a_shape, b_shape
    assert k == k2, (a_shape, b.shape, mode)
    tm, tn, tk = min(tm, m), min(tn, n), min(tk, k)
    assert m % tm == 0 and n % tn == 0 and k % tk == 0, (name, m, n, k, tm, tn, tk)
    nk = k // tk
    n_ex, n_out = len(extras), len(out_dtypes)
    n_in = n_pc + 1 + n_ex + (after is not None)
    dims = {"nn": NN, "nt": NT, "tn": TN}[mode]
    col_tile = tm if mode == "tn" else tk
    blocks = [p.shape[1] // col_tile for p in pieces]
    firsts = [sum(blocks[:p]) for p in range(n_pc)]
    assert all(p.shape[1] % col_tile == 0 for p in pieces), (name, col_tile)

    def body(*refs):
        a_refs, b_ref = refs[:n_pc], refs[n_pc]
        ex_refs = refs[n_pc + 1:n_pc + 1 + n_ex]
        out_refs = refs[n_in:n_in + n_out]

        def finish(acc):
            outs = epilogue(acc, *[r[...] for r in ex_refs]) if epilogue is not None else (acc,)
            for r, o in zip(out_refs, outs):
                r[...] = o.astype(r.dtype)

        kk = pl.program_id(2)

        def step(a_ref):
            part = lax.dot_general(a_ref[...].astype(BF16), b_ref[...].astype(BF16), dims, preferred_element_type=F32)
            if nk == 1:
                finish(part)
                return
            acc_ref = refs[-1]

            @pl.when(kk == 0)
            def _():
                acc_ref[...] = part

            @pl.when(kk > 0)
            def _():
                acc_ref[...] += part

            @pl.when(kk == nk - 1)
            def _():
                finish(acc_ref[...])

        if n_pc == 1:
            step(a_refs[0])
        else:
            at = pl.program_id(0) if mode == "tn" else kk
            for p in range(n_pc):
                pl.when(jnp.logical_and(at >= firsts[p], at < firsts[p] + blocks[p]))(functools.partial(step, a_refs[p]))

    def a_spec_of(p):
        col = lambda at: jnp.clip(at - firsts[p], 0, blocks[p] - 1) if n_pc > 1 else at
        if mode == "tn":
            return pl.BlockSpec((tk, tm), lambda i, j, kk: (kk, col(i)))
        return pl.BlockSpec((tm, tk), lambda i, j, kk: (i, col(kk)))

    b_spec = {"nn": pl.BlockSpec((tk, tn), lambda i, j, kk: (kk, j)),
              "nt": pl.BlockSpec((tn, tk), lambda i, j, kk: (j, kk)),
              "tn": pl.BlockSpec((tk, tn), lambda i, j, kk: (kk, j))}[mode]
    tile = pl.BlockSpec((tm, tn), lambda i, j, kk: (i, j))
    out_spec, out_dims = tile, (m, n)
    if b_shards and mode == "nn":
        per = cols_b // tn
        b_spec = pl.BlockSpec((None, tk, tn), lambda i, j, kk: (j // per, kk, j % per))
    elif b_shards:
        assert mode == "nt"
        per = cols_b // tk
        b_spec = pl.BlockSpec((None, tn, tk), lambda i, j, kk: (kk // per, j, kk % per))
    if out_shards:
        assert not extras and epilogue is None
        per_out = n // N_CHIPS // tn
        out_spec = pl.BlockSpec((None, tm, tn), lambda i, j, kk: (j // per_out, i, j % per_out))
        out_dims = (N_CHIPS, m, n // N_CHIPS)
    outs = pl.pallas_call(
        body, name=name,
        grid=(m // tm, n // tn, nk),
        in_specs=[a_spec_of(p) for p in range(n_pc)] + [b_spec] + [tile] * n_ex + [pl.BlockSpec(memory_space=pl.ANY)] * (after is not None),
        out_specs=[out_spec] * n_out,
        out_shape=[jax.ShapeDtypeStruct(out_dims, dt) for dt in out_dtypes],
        scratch_shapes=[pltpu.VMEM((tm, tn), F32)] if nk > 1 else [],
        compiler_params=_params(("parallel", "parallel", "arbitrary")),
    )(*pieces, b, *extras, *([after] if after is not None else []))
    return outs[0] if n_out == 1 else outs


def _rowwise(fn, *, name, rows, seq, ins, outs, sums=()):
    tm = min(ROW_T, seq)
    n_pos = seq // tm
    n_in, n_out, n_sum = len(ins), len(outs), len(sums)

    def body(*refs):
        vals = fn(*[r[...] for r in refs[:n_in]])
        for r, v in zip(refs[n_in:n_in + n_out], vals[:n_out]):
            r[...] = v.astype(r.dtype)
        first = pl.program_id(0) == 0
        for r, v in zip(refs[n_in + n_out:], vals[n_out:]):
            @pl.when(first)
            def _(r=r, v=v):
                r[...] = v

            @pl.when(jnp.logical_not(first))
            def _(r=r, v=v):
                r[...] += v

    def spec(arr, width, col, kind):
        if kind == "row":
            return pl.BlockSpec((tm, width), lambda i, col=col: (i, col))
        if kind == "pos":
            return pl.BlockSpec((tm, width), lambda i, col=col: (i % n_pos, col))
        return pl.BlockSpec(arr.shape, lambda i: (0,) * arr.ndim)

    res = pl.pallas_call(
        body, name=name,
        grid=(rows // tm,),
        in_specs=[spec(*t) for t in ins],
        out_specs=[pl.BlockSpec((tm, w), lambda i: (i, 0)) for w, _ in outs]
        + [pl.BlockSpec((1, w), lambda i: (0, 0)) for w in sums],
        out_shape=[jax.ShapeDtypeStruct((rows, w), dt) for w, dt in outs]
        + [jax.ShapeDtypeStruct((1, w), F32) for w in sums],
        compiler_params=_params(("arbitrary",)),
    )(*[t[0] for t in ins])
    return res


def _colsum(v):
    return jnp.sum(v, axis=0, keepdims=True)


def _rope_fwd(t, c, s_up, s_dn):
    return t * c + pltpu.roll(t, LANES - 16, 1) * s_up + pltpu.roll(t, 16, 1) * s_dn


def _rope_bwd(d, c, s_up, s_dn):
    return d * c + pltpu.roll(d * s_up, 16, 1) + pltpu.roll(d * s_dn, LANES - 16, 1)


def _rope_tables(seq):
    half = ROPE // 2
    inv = jnp.power(ROPE_THETA, -jnp.arange(half, dtype=F32) / half)
    ang = jnp.arange(seq, dtype=F32)[:, None] * inv[None, :]
    cos, sin = jnp.cos(ang), jnp.sin(ang)
    zeros = jnp.zeros((seq, half), F32)
    lo, hi = jnp.ones((seq, KR_LANE), F32), jnp.ones((seq, LANES - KR_LANE - ROPE), F32)
    c = jnp.concatenate([lo, cos, cos, hi], axis=1)
    c_rope_only = jnp.concatenate([0 * lo, cos, cos, 0 * hi], axis=1)
    s_up = jnp.concatenate([0 * lo, -sin, zeros, 0 * hi], axis=1)
    s_dn = jnp.concatenate([0 * lo, zeros, sin, 0 * hi], axis=1)
    return c, s_up, s_dn, c_rope_only


def _rms(x, g):
    r = lax.rsqrt(jnp.mean(x * x, axis=1, keepdims=True) + RMS_EPS)
    return x * r * g


def _rms_bwd(x, g, dy):
    r = lax.rsqrt(jnp.mean(x * x, axis=1, keepdims=True) + RMS_EPS)
    xh = x * r
    dxh = dy * g
    dx = r * (dxh - xh * jnp.mean(dxh * xh, axis=1, keepdims=True))
    return dx, _colsum(dy * xh)


def _ln_stats(x):
    mu = jnp.mean(x, axis=1, keepdims=True)
    xc = x - mu
    r = lax.rsqrt(jnp.mean(xc * xc, axis=1, keepdims=True) + LN_EPS)
    return xc * r, r


def _ln_bwd(xh, r, g, dy):
    dxh = dy * g
    dx = r * (dxh - jnp.mean(dxh, axis=1, keepdims=True) - xh * jnp.mean(dxh * xh, axis=1, keepdims=True))
    return dx, _colsum(dy * xh), _colsum(dy)


def _bias_spec(bias):
    per_head = bias.shape[0] > 1
    return pl.BlockSpec((1,) + bias.shape[1:], lambda b, h: (h if per_head else 0, 0, 0, 0))


def _biased(s, bias_ref, delta):
    return s + bias_ref[0, delta] if delta < bias_ref.shape[1] else s


def _attn_fwd(q, qb0, k, kb0, v, vb0, bias, scale, *, name, batch, seq, after=None):
    t = ATT_T
    nq = seq // t
    rows = batch * seq

    def body(q_ref, k_ref, v_ref, bias_ref, *rest):
        o_ref, lse_ref, qb, kb, vb = rest[after is not None:]
        qb[...] = q_ref[...].astype(BF16)
        kb[...] = k_ref[...].astype(BF16)
        vb[...] = v_ref[...].astype(BF16)
        for i in range(nq):
            qt = qb[i * t:(i + 1) * t, :]
            logits = [_biased(lax.dot_general(qt, kb[j * t:(j + 1) * t, :], NT, preferred_element_type=F32) * scale, bias_ref, i - j)
                      for j in range(i + 1)]
            top = functools.reduce(jnp.maximum, logits)
            m = jnp.max(top, axis=1, keepdims=True)
            ps = [jnp.exp(s - m) for s in logits]
            l = jnp.sum(functools.reduce(jnp.add, ps), axis=1, keepdims=True)
            acc = functools.reduce(jnp.add, [lax.dot_general(p.astype(BF16), vb[j * t:(j + 1) * t, :], NN, preferred_element_type=F32)
                                             for j, p in enumerate(ps)])
            o_ref[i * t:(i + 1) * t, :] = acc / l
            lse_ref[i * t:(i + 1) * t, :] = jnp.broadcast_to(m + jnp.log(l), (t, LANES))

    slab = lambda b0: pl.BlockSpec((seq, LANES), lambda b, h, b0=b0: (b, b0 + h))
    return pl.pallas_call(
        body, name=name,
        grid=(batch, N_HEADS),
        in_specs=[slab(qb0), slab(kb0), slab(vb0), _bias_spec(bias)] + [pl.BlockSpec(memory_space=pl.ANY)] * (after is not None),
        out_specs=[slab(0), slab(0)],
        out_shape=[jax.ShapeDtypeStruct((rows, N_HEADS * LANES), F32)] * 2,
        scratch_shapes=[pltpu.VMEM((seq, LANES), BF16)] * 3,
        compiler_params=_params(("arbitrary", "arbitrary")),
    )(q, k, v, bias, *([after] if after is not None else []))


def _attn_bwd(q, qb0, k, kb0, v, vb0, o, do, lse, bias, scale, *, name, batch, seq, out_dtype, after=None):
    t = ATT_T
    nq = seq // t
    rows = batch * seq

    def body(q_ref, k_ref, v_ref, o_ref, do_ref, lse_ref, bias_ref, *rest):
        dq_ref, dk_ref, dv_ref, qb, kb, vb, dob, qtb, dotb, dka, dva = rest[after is not None:]
        qb[...] = q_ref[...].astype(BF16)
        kb[...] = k_ref[...].astype(BF16)
        vb[...] = v_ref[...].astype(BF16)
        dob[...] = do_ref[...].astype(BF16)
        qtb[...] = q_ref[...].astype(F32).T.astype(BF16)
        dotb[...] = do_ref[...].T.astype(BF16)
        for i in range(nq):
            at = slice(i * t, (i + 1) * t)
            qt, dot = qb[at, :], dob[at, :]
            lse_t = lse_ref[at, 0:1]
            delta = jnp.sum(o_ref[at, :] * do_ref[at, :], axis=1, keepdims=True)
            dq = None
            for j in range(i + 1):
                kat = slice(j * t, (j + 1) * t)
                kt, vt = kb[kat, :], vb[kat, :]
                p = jnp.exp(_biased(lax.dot_general(qt, kt, NT, preferred_element_type=F32) * scale, bias_ref, i - j) - lse_t)
                dp = lax.dot_general(dot, vt, NT, preferred_element_type=F32)
                ds = (p * (dp - delta) * scale).astype(BF16)
                dk_part = lax.dot_general(qtb[:, at], ds, NN, preferred_element_type=F32)
                dv_part = lax.dot_general(dotb[:, at], p.astype(BF16), NN, preferred_element_type=F32)
                if i == j:
                    dka[:, kat] = dk_part
                    dva[:, kat] = dv_part
                else:
                    dka[:, kat] += dk_part
                    dva[:, kat] += dv_part
                dq_part = lax.dot_general(ds, kt, NN, preferred_element_type=F32)
                dq = dq_part if dq is None else dq + dq_part
            dq_ref[at, :] = dq.astype(dq_ref.dtype)
        dk_ref[...] = dka[...].T.astype(dk_ref.dtype)
        dv_ref[...] = dva[...].T.astype(dv_ref.dtype)

    slab = lambda b0: pl.BlockSpec((seq, LANES), lambda b, h, b0=b0: (b, b0 + h))
    return pl.pallas_call(
        body, name=name,
        grid=(batch, N_HEADS),
        in_specs=[slab(qb0), slab(kb0), slab(vb0), slab(0), slab(0), slab(0), _bias_spec(bias)] + [pl.BlockSpec(memory_space=pl.ANY)] * (after is not None),
        out_specs=[slab(0)] * 3,
        out_shape=[jax.ShapeDtypeStruct((rows, N_HEADS * LANES), out_dtype)] * 3,
        scratch_shapes=[pltpu.VMEM((seq, LANES), BF16)] * 4 + [pltpu.VMEM((LANES, seq), BF16)] * 2 + [pltpu.VMEM((LANES, seq), F32)] * 2,
        compiler_params=_params(("arbitrary", "arbitrary")),
    )(q, k, v, o, do, lse, bias, *([after] if after is not None else []))


def _tile_dist(seq):
    n = seq // ATT_T
    d = jnp.arange(n, dtype=jnp.int32)[:, None, None] * ATT_T
    return d + jnp.arange(ATT_T, dtype=jnp.int32)[None, :, None] - jnp.arange(ATT_T, dtype=jnp.int32)[None, None, :]


def _causal_bias(seq):
    return jnp.where(_tile_dist(seq)[:1] >= 0, 0.0, NEG).astype(F32)[None]


def _dilated_bias(seq):
    dist = _tile_dist(seq)
    count = jnp.zeros(dist.shape, F32)
    for window, dilation in DIL_PATTERNS:
        count += ((dist >= 0) & (dist <= window) & (dist % dilation == 0)).astype(F32)
    slopes = jnp.asarray([2.0 ** (-8.0 * (i + 1) / N_HEADS) for i in range(N_HEADS)], F32)
    alibi = -slopes[:, None, None, None] * dist.astype(F32)[None]
    return jnp.where(count[None] > 0, jnp.log(jnp.maximum(count, 1.0))[None] + alibi, NEG).astype(F32)


def _pad_heads(w, width):
    kdim, n = w.shape[0], w.shape[1] // width
    return jnp.pad(w.reshape(kdim, n, width), ((0, 0), (0, 0), (0, LANES - width))).reshape(kdim, n * LANES)


def _unpad_heads(w, width):
    kdim, n = w.shape[0], w.shape[1] // LANES
    return w.reshape(kdim, n, LANES)[:, :, :width].reshape(kdim, n * width)


def _pad_head_rows(w, width):
    n, kdim = w.shape[0] // width, w.shape[1]
    return jnp.pad(w.reshape(n, width, kdim), ((0, 0), (0, LANES - width), (0, 0))).reshape(n * LANES, kdim)


def _unpad_head_rows(w, width):
    n, kdim = w.shape[0] // LANES, w.shape[1]
    return w.reshape(n, LANES, kdim)[:, :width].reshape(n * width, kdim)


def _pad_w_in_t(wt):
    n_qkv = 3 * N_HEADS * DIL_DIM
    zeros = lambda n: jnp.zeros((n, wt.shape[1]), wt.dtype)
    return jnp.concatenate([wt[LORA_W + ROPE + n_qkv:], _pad_head_rows(wt[LORA_W + ROPE:LORA_W + ROPE + n_qkv], DIL_DIM), wt[:LORA_W],
                            zeros(KR_LANE), wt[LORA_W:LORA_W + ROPE], zeros(LANES - KR_LANE - ROPE), zeros(P_WIDTH - P_KR - LANES)], axis=0)


def _unpad_w_in_t(gt):
    return jnp.concatenate([gt[P_LORA:P_KR], gt[P_KR + KR_LANE:P_KR + KR_LANE + ROPE], _unpad_head_rows(gt[P_QD:P_LORA], DIL_DIM),
                            gt[P_GATE:P_QD]], axis=0)


def _split_ukv(w):
    w3 = w.reshape(w.shape[0], N_HEADS, NOPE + V_DIM)
    return (_pad_heads(w3[:, :, :NOPE].reshape(w.shape[0], -1), NOPE),
            _pad_heads(w3[:, :, NOPE:].reshape(w.shape[0], -1), V_DIM))


def _merge_ukv(g_k, g_v):
    kdim = g_k.shape[0]
    k3 = _unpad_heads(g_k, NOPE).reshape(kdim, N_HEADS, NOPE)
    v3 = _unpad_heads(g_v, V_DIM).reshape(kdim, N_HEADS, V_DIM)
    return jnp.concatenate([k3, v3], axis=2).reshape(kdim, N_HEADS * (NOPE + V_DIM))


def _pad_rows(w, width):
    return _pad_heads(w.T, width).T


def _unpad_rows(g, width):
    return _unpad_heads(g.T, width).T


def _join_cols(w):
    return w.transpose(1, 0, 2).reshape(w.shape[1], N_CHIPS * w.shape[2])


def _split_cols(g):
    return g.reshape(g.shape[0], N_CHIPS, g.shape[1] // N_CHIPS).transpose(1, 0, 2)


def _local_step(x3, target3, wg, b_gate, g_q_a, g_kv_a, ln1_g, ln1_b, ln2_g, ln2_b, token=None, late_arrived=None, late_weights=None,
                early_grads=None, early_grads_go=None, last_grads=None):
    w_in_pt = _pad_w_in_t(wg["w_in"].reshape(IN_WIDTH, D_MODEL))
    w_uq_pt = _pad_head_rows(wg["w_uq"].reshape(N_HEADS * MLA_QK, Q_LORA), MLA_QK)
    w_ukv = _join_cols(wg["w_ukv"])
    batch, seq, _ = x3.shape
    rows = batch * seq
    x = x3.reshape(rows, D_MODEL)
    target = target3.reshape(rows, D_MODEL)
    row = functools.partial(_rowwise, rows=rows, seq=seq)
    mm = _matmul

    w_uk_p, w_uv_p = _split_ukv(w_ukv)
    b0, b1 = b_gate[0:1], b_gate[1:2]
    rope_c, rope_up, rope_dn, rope_c_only = _rope_tables(seq)
    bias_mla, bias_dil = _causal_bias(seq), _dilated_bias(seq)
    scale_mla, scale_dil = MLA_QK ** -0.5, DIL_DIM ** -0.5
    qd0, kd0, vd0, lora0, kr0 = P_QD // LANES, P_KD // LANES, P_VD // LANES, P_LORA // LORA_W, P_KR // LANES

    proj = mm(x, w_in_pt, mode="nt", name="proj", tm=1024, tn=1536, tk=1024, after=token)

    def prep(lora, gq, gkv):
        return _rms(lora[:, :Q_LORA], gq), _rms(lora[:, Q_LORA:], gkv)

    qn, kvn = row(prep, name="mla_rms", ins=[(proj, LORA_W, lora0, "row"), (g_q_a, 0, 0, "full"), (g_kv_a, 0, 0, "full")],
                  outs=[(Q_LORA, BF16), (KV_LORA, BF16)])
    q_lin = mm(qn, w_uq_pt, mode="nt", name="q_up", tm=1024, tn=1024, tk=Q_LORA)
    k_lin = mm(kvn, w_uk_p, mode="nn", name="k_up", tm=1024, tn=1024, tk=KV_LORA)
    v_a = mm(kvn, w_uv_p, mode="nn", name="v_up", tm=1024, tn=1024, tk=KV_LORA, out_dtypes=(BF16,))

    def rope_qk(ql, kl, kr, c, up, dn):
        k_rot = _rope_fwd(kr, c, up, dn)
        qs = [_rope_fwd(ql[:, h * LANES:(h + 1) * LANES], c, up, dn) for h in range(N_HEADS)]
        ks = [kl[:, h * LANES:(h + 1) * LANES] + k_rot for h in range(N_HEADS)]
        return jnp.concatenate(qs, axis=1), jnp.concatenate(ks, axis=1)

    pos = lambda tab: (tab, LANES, 0, "pos")
    q_a, k_a = row(rope_qk, name="rope_qk",
                   ins=[(q_lin, D_MODEL, 0, "row"), (k_lin, D_MODEL, 0, "row"), (proj, LANES, kr0, "row"), pos(rope_c), pos(rope_up), pos(rope_dn)],
                   outs=[(N_HEADS * LANES, BF16), (N_HEADS * LANES, BF16)])
    o_a, lse_a = _attn_fwd(q_a, 0, k_a, 0, v_a, 0, bias_mla, scale_mla, name="mla_fwd", batch=batch, seq=seq)
    arrived = None if late_arrived is None else late_arrived(o_a)
    o_b, lse_b = _attn_fwd(proj, qd0, proj, kd0, proj, vd0, bias_dil, scale_dil, name="dil_fwd", batch=batch, seq=seq, after=arrived)
    late = wg if late_weights is None else late_weights(o_b)
    w_oa_p = _pad_rows(_join_cols(late["w_o_mla"]), V_DIM)
    w_ob_p = _pad_rows(_join_cols(late["w_o_dil"]), DIL_DIM)
    w_out, w_ff1, w_ff2 = late["w_out"].reshape(D_MODEL, D_MODEL), late["w_ff1"], late["w_ff2"].reshape(D_FF, D_MODEL)
    y_a = mm(o_a, w_oa_p, mode="nn", name="o_mla", tm=1024, tn=1024, tk=1024)
    y_b = mm(o_b, w_ob_p, mode="nn", name="o_dil", tm=1024, tn=1024, tk=1024)

    def gate(t0, t1, c0, c1, ya, yb):
        return (jax.nn.sigmoid(t0 + c0) * ya + jax.nn.sigmoid(t1 + c1) * yb,)

    gate_ins = [(proj, D_MODEL, 0, "row"), (proj, D_MODEL, 1, "row"), (b0, 0, 0, "full"), (b1, 0, 0, "full")]
    (u,) = row(gate, name="gate", ins=gate_ins + [(y_a, D_MODEL, 0, "row"), (y_b, D_MODEL, 0, "row")], outs=[(D_MODEL, BF16)])
    mixed = mm(u, w_out, mode="nn", name="mix", tm=1024, tn=1024, tk=1024)

    def ln1(xv, mv, g, b):
        r1 = ALPHA * xv + mv
        xh, _ = _ln_stats(r1)
        return r1, xh * g + b

    r1, h = row(ln1, name="ln1", ins=[(x, D_MODEL, 0, "row"), (mixed, D_MODEL, 0, "row"), (ln1_g, 0, 0, "full"), (ln1_b, 0, 0, "full")],
                outs=[(D_MODEL, F32), (D_MODEL, F32)])

    def relu2(acc):
        r = jnp.maximum(acc, 0.0)
        return acc, r * r

    a_ff, z = mm(h, w_ff1, mode="nn", name="ff1", tm=1024, tn=1024, tk=1024, out_dtypes=(F32, BF16), epilogue=relu2, b_shards=True)
    f = mm(z, w_ff2, mode="nn", name="ff2", tm=1024, tn=1024, tk=1024)

    def ln2_loss(hv, fv, tv, g, b):
        xh, r = _ln_stats(ALPHA * hv + fv)
        err = xh * g + b - tv
        dy = err * (1.0 / D_MODEL)
        dr2, dg, db = _ln_bwd(xh, r, g, dy)
        loss = jnp.sum(_colsum(err * err), axis=1, keepdims=True) * (0.5 / D_MODEL)
        return dr2, jnp.broadcast_to(loss, (1, LANES)), dg, db

    dr2, loss_l, d_ln2_g, d_ln2_b = row(
        ln2_loss, name="ln2_loss",
        ins=[(h, D_MODEL, 0, "row"), (f, D_MODEL, 0, "row"), (target, D_MODEL, 0, "row"), (ln2_g, 0, 0, "full"), (ln2_b, 0, 0, "full")],
        outs=[(D_MODEL, F32)], sums=[LANES, D_MODEL, D_MODEL])

    d_w_ff2 = mm(z, dr2, mode="tn", name="d_w_ff2", tm=1024, tn=1024, tk=1024)
    da = mm(dr2, w_ff2, mode="nt", name="d_ff_act", tm=1024, tn=1024, tk=1024, out_dtypes=(BF16,), extras=(a_ff,),
            epilogue=lambda acc, av: (acc * (2.0 * jnp.maximum(av, 0.0)),))
    d_w_ff1 = mm(h, da, mode="tn", name="d_w_ff1", tm=1024, tn=1024, tk=1024, out_shards=True)
    dh = mm(da, w_ff1, mode="nt", name="d_h", tm=1024, tn=1024, tk=1024, extras=(dr2,), epilogue=lambda acc, rv: (acc + ALPHA * rv,), b_shards=True)

    def ln1_bwd(dhv, r1v, g):
        xh, r = _ln_stats(r1v)
        return _ln_bwd(xh, r, g, dhv)

    dr1, d_ln1_g, d_ln1_b = row(ln1_bwd, name="ln1_bwd", ins=[(dh, D_MODEL, 0, "row"), (r1, D_MODEL, 0, "row"), (ln1_g, 0, 0, "full")],
                                outs=[(D_MODEL, F32)], sums=[D_MODEL, D_MODEL])
    d_w_out = mm(u, dr1, mode="tn", name="d_w_out", tm=1024, tn=1024, tk=1024)
    du = mm(dr1, w_out, mode="nt", name="d_u", tm=1024, tn=1024, tk=1024)

    def gate_bwd(t0, t1, c0, c1, ya, yb, duv):
        s0, s1 = jax.nn.sigmoid(t0 + c0), jax.nn.sigmoid(t1 + c1)
        dt0 = duv * ya * s0 * (1.0 - s0)
        dt1 = duv * yb * s1 * (1.0 - s1)
        return duv * s0, duv * s1, jnp.concatenate([dt0, dt1], axis=1), jnp.concatenate([_colsum(dt0), _colsum(dt1)], axis=1)

    dy_a, dy_b, d_gates, d_b_gate = row(
        gate_bwd, name="gate_bwd", ins=gate_ins + [(y_a, D_MODEL, 0, "row"), (y_b, D_MODEL, 0, "row"), (du, D_MODEL, 0, "row")],
        outs=[(D_MODEL, BF16), (D_MODEL, BF16), (2 * D_MODEL, BF16)], sums=[2 * D_MODEL])
    d_w_oa_p = mm(o_a, dy_a, mode="tn", name="d_w_o_mla", tm=1024, tn=1024, tk=1024)
    d_w_ob_p = mm(o_b, dy_b, mode="tn", name="d_w_o_dil", tm=1024, tn=1024, tk=1024)
    grads = dict(w_o_mla=_split_cols(_unpad_rows(d_w_oa_p, V_DIM)), w_o_dil=_split_cols(_unpad_rows(d_w_ob_p, DIL_DIM)),
                 w_out=d_w_out.reshape(N_CHIPS, D_MODEL // N_CHIPS, D_MODEL), w_ff1=d_w_ff1, w_ff2=d_w_ff2.reshape(N_CHIPS, D_FF // N_CHIPS, D_MODEL))
    sent = None if early_grads is None else early_grads(grads)
    do_a = mm(dy_a, w_oa_p, mode="nt", name="d_o_mla", tm=1024, tn=1024, tk=1024, after=sent)
    do_b = mm(dy_b, w_ob_p, mode="nt", name="d_o_dil", tm=1024, tn=1024, tk=1024)
    dq_a, dk_a, dv_a = _attn_bwd(q_a, 0, k_a, 0, v_a, 0, o_a, do_a, lse_a, bias_mla, scale_mla,
                                 name="mla_bwd", batch=batch, seq=seq, out_dtype=F32)
    going = None if early_grads_go is None else early_grads_go(dq_a)
    dq_d, dk_d, dv_d = _attn_bwd(proj, qd0, proj, kd0, proj, vd0, o_b, do_b, lse_b, bias_dil, scale_dil,
                                 name="dil_bwd", batch=batch, seq=seq, out_dtype=BF16, after=going)

    def mla_post(dq, dk, c, up, dn, c_only):
        dqs = [_rope_bwd(dq[:, h * LANES:(h + 1) * LANES], c, up, dn) for h in range(N_HEADS)]
        dk_sum = dk[:, :LANES]
        for h in range(1, N_HEADS):
            dk_sum = dk_sum + dk[:, h * LANES:(h + 1) * LANES]
        return jnp.concatenate(dqs, axis=1), _rope_bwd(dk_sum, c_only, up, dn)

    dq_lin, d_kr = row(mla_post, name="mla_unrope",
                       ins=[(dq_a, D_MODEL, 0, "row"), (dk_a, D_MODEL, 0, "row"), pos(rope_c), pos(rope_up), pos(rope_dn), pos(rope_c_only)],
                       outs=[(N_HEADS * LANES, BF16), (LANES, BF16)])
    d_w_uq_pt = mm(dq_lin, qn, mode="tn", name="d_w_uq", tm=1024, tn=Q_LORA, tk=1024)
    d_w_uk_p = mm(kvn, dk_a, mode="tn", name="d_w_uk", tm=KV_LORA, tn=1024, tk=1024)
    d_w_uv_p = mm(kvn, dv_a, mode="tn", name="d_w_uv", tm=KV_LORA, tn=1024, tk=1024)
    d_qn = mm(dq_lin, w_uq_pt, mode="nn", name="d_qn", tm=1024, tn=Q_LORA, tk=1024)
    d_kvn_k = mm(dk_a, w_uk_p, mode="nt", name="d_kvn_k", tm=1024, tn=KV_LORA, tk=1024)
    d_kvn = mm(dv_a, w_uv_p, mode="nt", name="d_kvn", tm=1024, tn=KV_LORA, tk=1024, extras=(d_kvn_k,), epilogue=lambda acc, e: (acc + e,))

    def rms_bwd(lora, dq, dkv, dkr, gq, gkv):
        dxq, dgq = _rms_bwd(lora[:, :Q_LORA], gq, dq)
        dxk, dgk = _rms_bwd(lora[:, Q_LORA:], gkv, dkv)
        tail = jnp.zeros((dxq.shape[0], P_WIDTH - P_KR - LANES), F32)
        return jnp.concatenate([dxq, dxk, dkr.astype(F32), tail], axis=1), dgq, dgk

    d_tail, d_g_q_a, d_g_kv_a = row(
        rms_bwd, name="mla_rms_bwd",
        ins=[(proj, LORA_W, lora0, "row"), (d_qn, Q_LORA, 0, "row"), (d_kvn, KV_LORA, 0, "row"), (d_kr, LANES, 0, "row"),
             (g_q_a, 0, 0, "full"), (g_kv_a, 0, 0, "full")],
        outs=[(P_WIDTH - P_LORA, BF16)], sums=[Q_LORA, KV_LORA])
    d_proj = [d_gates, dq_d, dk_d, dv_d, d_tail]
    d_w_in_pt = mm(d_proj, x, mode="tn", name="d_w_in", tm=1024, tn=1024, tk=1024)
    grads.update(w_in=_unpad_w_in_t(d_w_in_pt).reshape(N_CHIPS, IN_WIDTH // N_CHIPS, D_MODEL),
                 w_uq=_unpad_head_rows(d_w_uq_pt, MLA_QK).reshape(N_CHIPS, N_HEADS * MLA_QK // N_CHIPS, Q_LORA),
                 w_ukv=_split_cols(_merge_ukv(d_w_uk_p, d_w_uv_p)))
    leaving = None if last_grads is None else last_grads(grads)
    grad_x = mm(d_proj, w_in_pt, mode="nn", name="d_x", tm=1024, tn=1024, tk=1024, extras=(dr1,), epilogue=lambda acc, rv: (acc + ALPHA * rv,),
                after=leaving)

    grads.update(
        b_gate=d_b_gate.reshape(2, D_MODEL), g_q_a=d_g_q_a, g_kv_a=d_g_kv_a, ln1_g=d_ln1_g, ln1_b=d_ln1_b, ln2_g=d_ln2_g, ln2_b=d_ln2_b)
    return loss_l, grad_x.reshape(batch, seq, D_MODEL), grads


BIG = ("w_in", "w_uq", "w_ukv", "w_o_mla", "w_o_dil", "w_out", "w_ff1", "w_ff2")
SMALL = (("b_gate", 2 * D_MODEL), ("g_q_a", Q_LORA), ("g_kv_a", KV_LORA), ("ln1_g", D_MODEL), ("ln1_b", D_MODEL),
         ("ln2_g", D_MODEL), ("ln2_b", D_MODEL))
TRANSPOSED = ("w_in", "w_uq")
D2D_PIECES = (4, 2, 1)
ANY = pl.BlockSpec(memory_space=pl.ANY)
SIDE_EFFECTS = pltpu.CompilerParams(has_side_effects=True)


def _place():
    x, y, c = lax.axis_index("x"), lax.axis_index("y"), lax.axis_index("c")
    return x, y, c, ((1 - x, y), (x, 1 - y), (1 - x, 1 - y))


def _half_axis(shape):
    return 0 if shape[0] % 32 == 0 else 1


def _half_shape(shape):
    return (shape[0] // 2, shape[1]) if _half_axis(shape) == 0 else (shape[0], shape[1] // 2)


def _window(ref, lead, shape, which=None, pieces=False):
    axis = _half_axis(shape)
    size = shape[axis] if which is None else shape[axis] // 2
    base = 0 if which is None else which * size
    tile = (16, LANES)[axis]
    count = next(c for c in D2D_PIECES if size % (tile * c) == 0) if pieces else 1
    step = size // count
    spans = [pl.ds(pl.multiple_of(base + i * step, tile), step) for i in range(count)]
    refs = [ref.at[(*lead, s)] if axis == 0 else ref.at[(*lead, slice(None), s)] for s in spans]
    return refs if pieces else refs[0]


def _remote(src, dst, send, recv, to):
    return pltpu.make_async_remote_copy(src_ref=src, dst_ref=dst, send_sem=send, recv_sem=recv, device_id=to, device_id_type=MESH)


def _gather_weights(shards, name):
    n = len(shards)

    def body(*refs):
        srcs, outs, (send, recv) = refs[:n], refs[n:2 * n], refs[2 * n:]
        x, y, c, chips = _place()
        sibling, mine = (x, y, 1 - c), 2 * x + y
        shapes = [s.shape for s in shards]

        def over_ici(t, j, to):
            return _remote(_window(srcs[t], (), shapes[t], c), _window(outs[t], (mine,), shapes[t], c), send.at[t, j], recv.at[t, j], (*to, c))

        def over_d2d(t, j, slot, which, pieces=False):
            from_to = [_window(outs[t], (slot,), shapes[t], which, pieces)] * 2
            if j == 6:
                from_to[0] = _window(srcs[t], (), shapes[t], which, pieces)
            if pieces:
                return [_remote(a, b, send.at[t, j], recv.at[t, j], sibling) for a, b in zip(*from_to)]
            return _remote(*from_to, send.at[t, j], recv.at[t, j], sibling)

        for t in range(n):
            for j, chip in enumerate(chips):
                over_ici(t, j, chip).start()
        for t in range(n):
            for cp in over_d2d(t, 6, mine, None, pieces=True):
                cp.start()
        for t in range(n):
            for j, (cx, cy) in enumerate(chips):
                over_ici(t, j, (cx, cy)).wait_recv()
                for cp in over_d2d(t, 3 + j, 2 * cx + cy, c, pieces=True):
                    cp.start()
        for t in range(n):
            over_d2d(t, 6, mine, None).wait()
            for j, (cx, cy) in enumerate(chips):
                over_d2d(t, 3 + j, 2 * cx + cy, 1 - c).wait_recv()
                over_d2d(t, 3 + j, 2 * cx + cy, c).wait_send()
                over_ici(t, j, (cx, cy)).wait_send()

    return pl.pallas_call(
        body, name=name, in_specs=[ANY] * n, out_specs=[ANY] * n,
        out_shape=[jax.ShapeDtypeStruct((N_CHIPS,) + s.shape, s.dtype) for s in shards],
        scratch_shapes=[pltpu.SemaphoreType.DMA((n, 7)), pltpu.SemaphoreType.DMA((n, 7))],
        compiler_params=SIDE_EFFECTS,
    )(*shards)


def _pair_split(grads, name):
    n = len(grads)

    def body(*refs):
        srcs, outs, (send, recv) = refs[:n], refs[n:2 * n], refs[2 * n:]
        x, y, c, _ = _place()
        for t in range(n):
            for s in range(N_CHIPS):
                _remote(_window(srcs[t], (s,), grads[t].shape[1:], 1 - c), outs[t].at[s], send.at[t], recv.at[t], (x, y, 1 - c)).start()
        for t in range(n):
            _remote(_window(srcs[t], (slice(None),), grads[t].shape[1:], 1 - c), outs[t], send.at[t], recv.at[t], (x, y, 1 - c)).wait()

    return pl.pallas_call(
        body, name=name, in_specs=[ANY] * n, out_specs=[ANY] * n,
        out_shape=[jax.ShapeDtypeStruct((N_CHIPS,) + _half_shape(g.shape[1:]), g.dtype) for g in grads],
        scratch_shapes=[pltpu.SemaphoreType.DMA((n,)), pltpu.SemaphoreType.DMA((n,))],
        compiler_params=SIDE_EFFECTS,
    )(*grads)


def _pair_join(totals, name):
    n = len(totals)

    def body(*refs):
        srcs, outs, (send, recv) = refs[:n], refs[n:2 * n], refs[2 * n:]
        x, y, c, _ = _place()
        for t in range(n):
            for a, b in zip(_window(srcs[t], (), totals[t].shape, None, True), _window(outs[t], (), totals[t].shape, None, True)):
                _remote(a, b, send.at[t], recv.at[t], (x, y, 1 - c)).start()
        for t in range(n):
            _remote(srcs[t], outs[t], send.at[t], recv.at[t], (x, y, 1 - c)).wait()

    return pl.pallas_call(
        body, name=name, in_specs=[ANY] * n, out_specs=[ANY] * n,
        out_shape=[jax.ShapeDtypeStruct(t.shape, t.dtype) for t in totals],
        scratch_shapes=[pltpu.SemaphoreType.DMA((n,)), pltpu.SemaphoreType.DMA((n,))],
        compiler_params=SIDE_EFFECTS,
    )(*totals)


HBM = pl.BlockSpec(memory_space=pltpu.HBM)
SEM = pl.BlockSpec(memory_space=pltpu.SEMAPHORE)
SPLIT = pltpu.CompilerParams(has_side_effects=pltpu.SideEffectType.DATAFLOW_SIDE_EFFECTING)


def _in_hbm(a):
    return pltpu.with_memory_space_constraint(a, pltpu.HBM)


def _split_copies(kind, srcs, lands):
    x, y, c, chips = _place()
    out = []
    for t in range(len(srcs)):
        if kind == "pair":
            out += [(t, s % 3, _window(srcs[t], (s,), srcs[t].shape[1:], 1 - c), lands[t].at[s], (x, y, 1 - c)) for s in range(N_CHIPS)]
            continue
        if kind == "forward":
            shape, sibling = srcs[t].shape, (x, y, 1 - c)
            out += [(t, 0, a, b, sibling) for a, b in zip(_window(srcs[t], (), shape, None, True), _window(lands[t], (2 * x + y,), shape, None, True))]
            out += [(t, j, a, a, sibling) for j, (cx, cy) in enumerate(chips) for a in _window(lands[t], (2 * cx + cy,), shape, c, True)]
            continue
        for j, (cx, cy) in enumerate(chips):
            if kind == "gather":
                shape = srcs[t].shape
                out.append((t, j, _window(srcs[t], (), shape, c), _window(lands[t], (2 * x + y,), shape, c), (cx, cy, c)))
            else:
                out.append((t, j, srcs[t].at[2 * cx + cy], lands[t].at[j], (cx, cy, c)))
    return out


def _split_start(kind, srcs, land_shapes, name, lands=None):
    n = len(srcs)

    def body(*refs):
        src_refs, land_refs, sems, token = refs[:n], refs[n:2 * n], refs[2 * n:2 * n + 6], refs[-1]
        for t, j, s, d, to in _split_copies(kind, src_refs, land_refs):
            _remote(s, d, sems[j], sems[3 + j], to).start()
        token[...] = jnp.zeros_like(token)

    lands = [_in_hbm(lax.empty(s.shape, s.dtype)) for s in land_shapes] if lands is None else list(lands)
    thru = [pltpu.HBM(a.shape, a.dtype) for a in list(srcs) + lands]
    res = pl.pallas_call(
        body, name=name,
        out_shape=(*[pltpu.SemaphoreType.DMA(())] * 6, *thru, jax.ShapeDtypeStruct((8, LANES), F32)),
        in_specs=[HBM] * (2 * n), out_specs=(*[SEM] * 6, *[HBM] * (2 * n), pl.BlockSpec(memory_space=pltpu.VMEM)),
        input_output_aliases={i: 6 + i for i in range(2 * n)}, compiler_params=SPLIT,
    )(*[_in_hbm(s) for s in srcs], *lands)
    return res[:6], res[6:6 + n], res[6 + n:6 + 2 * n], res[-1]


def _split_wait(kind, sems, srcs, lands, after, name):
    n = len(srcs)

    def body(*refs):
        src_refs, land_refs, sem_refs = refs[:n], refs[n:2 * n], refs[2 * n:2 * n + 6]
        for t, j, s, d, to in _split_copies(kind, src_refs, land_refs):
            cp = _remote(s, d, sem_refs[j], sem_refs[3 + j], to)
            cp.wait_send()
            cp.wait_recv()

    res = pl.pallas_call(
        body, name=name, out_shape=[pltpu.HBM(a.shape, a.dtype) for a in list(srcs) + list(lands)],
        in_specs=[HBM] * (2 * n) + [SEM] * 6 + [ANY], out_specs=[HBM] * (2 * n),
        input_output_aliases={i: i for i in range(2 * n)}, compiler_params=SPLIT,
    )(*srcs, *lands, *sems, after)
    return res[:n], res[n:]


def _sum_all_devices(vec, name):
    n_rows = vec.shape[0]

    def body(v_ref, out_ref, buf, send, recv):
        x, y, c, _ = _place()
        me = 4 * x + 2 * y + c
        buf[me] = v_ref[...]
        flips = [(a, b, d) for a in (0, 1) for b in (0, 1) for d in (0, 1)][1:]
        copies = []
        for r, (a, b, d) in enumerate(flips):
            px, py, pc = (1 - x if a else x), (1 - y if b else y), (1 - c if d else c)
            copies.append(pltpu.make_async_remote_copy(src_ref=v_ref, dst_ref=buf.at[me], send_sem=send.at[r], recv_sem=recv.at[r],
                                                       device_id=(px, py, pc), device_id_type=MESH))
            copies[-1].start()
        for r, (a, b, d) in enumerate(flips):
            px, py, pc = (1 - x if a else x), (1 - y if b else y), (1 - c if d else c)
            pltpu.make_async_remote_copy(src_ref=v_ref, dst_ref=buf.at[4 * px + 2 * py + pc], send_sem=send.at[r], recv_sem=recv.at[r],
                                         device_id=(px, py, pc), device_id_type=MESH).wait_recv()
        for cp in copies:
            cp.wait_send()
        total = buf[0]
        for k in range(1, N_DEV):
            total = total + buf[k]
        out_ref[...] = total

    vmem = pl.BlockSpec(memory_space=pltpu.VMEM)
    return pl.pallas_call(
        body, name=name, in_specs=[vmem], out_specs=vmem, out_shape=jax.ShapeDtypeStruct(vec.shape, F32),
        scratch_shapes=[pltpu.VMEM((N_DEV, n_rows, LANES), F32), pltpu.SemaphoreType.DMA((N_DEV - 1,)), pltpu.SemaphoreType.DMA((N_DEV - 1,))],
        compiler_params=pltpu.CompilerParams(has_side_effects=True),
    )(vec)


def _half_tile(half, width):
    t = half
    while t * width * 4 > (2 << 20) and t % 32 == 0:
        t //= 2
    return t


def _pair_add(g, theirs, core, name):
    _, half, width = theirs.shape
    t = _half_tile(half, width)
    n = half // t

    def body(c_ref, a_ref, b_ref, o_ref):
        o_ref[...] = (a_ref[...] + b_ref[...]).astype(BF16)

    tile = pl.BlockSpec((1, t, width), lambda j, i, c_ref: (j, i, 0))
    if _half_axis(g.shape[1:]) == 0:
        mine = pl.BlockSpec((1, t, width), lambda j, i, c_ref: (j, c_ref[0] * n + i, 0))
    else:
        mine = pl.BlockSpec((1, t, width), lambda j, i, c_ref: (j, i, c_ref[0]))
    return pl.pallas_call(
        body, name=name,
        grid_spec=pltpu.PrefetchScalarGridSpec(num_scalar_prefetch=1, grid=(N_CHIPS, n), in_specs=[mine, tile], out_specs=tile),
        out_shape=jax.ShapeDtypeStruct(theirs.shape, BF16), compiler_params=_params(("parallel", "parallel")),
    )(core, g, theirs)


def _chip_sum(part, others, chip, name, after=None):
    _, half, width = part.shape
    t = _half_tile(half, width)

    def body(s_ref, mine, p0, p1, p2, *rest):
        o_ref = rest[-1]
        o_ref[...] = ((mine[0].astype(F32) + p0[0].astype(F32)) + p1[0].astype(F32)) + p2[0].astype(F32)

    return pl.pallas_call(
        body, name=name,
        grid_spec=pltpu.PrefetchScalarGridSpec(
            num_scalar_prefetch=1, grid=(half // t,),
            in_specs=[pl.BlockSpec((1, t, width), lambda i, s_ref: (s_ref[0], i, 0))]
            + [pl.BlockSpec((1, t, width), lambda i, s_ref, j=j: (j, i, 0)) for j in range(3)] + [pl.BlockSpec(memory_space=pl.ANY)] * (after is not None),
            out_specs=pl.BlockSpec((t, width), lambda i, s_ref: (i, 0))),
        out_shape=jax.ShapeDtypeStruct((half, width), F32), compiler_params=_params(("parallel",)),
    )(chip, part, others, others, others, *([after] if after is not None else []))


EARLY = ("w_in", "w_uq", "w_ukv")
LATE = ("w_o_mla", "w_o_dil", "w_out", "w_ff1", "w_ff2")


def _chip_partials(grads, names, core, tag):
    gs = [grads[n] for n in names]
    theirs = _pair_split(gs, "pair_split_" + tag)
    return [_pair_add(g, th, core, "pair_add_" + n) for g, th, n in zip(gs, theirs, names)]


def _sum_small(vals):
    n_in = len(vals)
    n_rows = sum(a.shape[0] * a.shape[1] // LANES for a in vals)
    pad_rows = -(-n_rows // 8) * 8

    def chunks(refs):
        return [(ref, a, j) for ref in refs for a in range(ref.shape[0]) for j in range(ref.shape[1] // LANES)]

    def body(*refs):
        ins, outs, (buf, send, recv) = refs[:n_in], refs[n_in:2 * n_in], refs[2 * n_in:]
        x, y, c, _ = _place()
        me = 4 * x + 2 * y + c
        for r, (ref, a, j) in enumerate(chunks(ins)):
            buf[me, r:r + 1, :] = ref[a:a + 1, j * LANES:(j + 1) * LANES]
        if pad_rows > n_rows:
            buf[me, n_rows:pad_rows, :] = jnp.zeros((pad_rows - n_rows, LANES), F32)
        flips = [(a, b, d) for a in (0, 1) for b in (0, 1) for d in (0, 1)][1:]
        peers = [((1 - x if a else x), (1 - y if b else y), (1 - c if d else c)) for a, b, d in flips]
        copies = [_remote(buf.at[me], buf.at[me], send.at[r], recv.at[r], peer) for r, peer in enumerate(peers)]
        for cp in copies:
            cp.start()
        for r, (px, py, pc) in enumerate(peers):
            _remote(buf.at[me], buf.at[4 * px + 2 * py + pc], send.at[r], recv.at[r], (px, py, pc)).wait_recv()
        for cp in copies:
            cp.wait_send()
        total = buf[0]
        for k in range(1, N_DEV):
            total = total + buf[k]
        for r, (ref, a, j) in enumerate(chunks(outs)):
            ref[a:a + 1, j * LANES:(j + 1) * LANES] = total[r:r + 1, :]

    vmem = pl.BlockSpec(memory_space=pltpu.VMEM)
    return pl.pallas_call(
        body, name="sum_small", in_specs=[vmem] * n_in, out_specs=[vmem] * n_in,
        out_shape=[jax.ShapeDtypeStruct(a.shape, F32) for a in vals],
        scratch_shapes=[pltpu.VMEM((N_DEV, pad_rows, LANES), F32), pltpu.SemaphoreType.DMA((N_DEV - 1,)), pltpu.SemaphoreType.DMA((N_DEV - 1,))],
        compiler_params=SIDE_EFFECTS,
    )(*vals)


def _adam_math(w, g, m, v):
    nm = B1 * m + (1.0 - B1) * g
    nv = B2 * v + (1.0 - B2) * (g * g)
    m_hat = nm / (1.0 - B1 ** ADAM_STEP)
    v_hat = nv / (1.0 - B2 ** ADAM_STEP)
    return -LR * (m_hat / (jnp.sqrt(v_hat) + ADAM_EPS) + WD * w), nm, nv


def _adamw_big(w, mine, theirs, m, v, core, name, side_by_side=False):
    rows, width = w.shape
    if side_by_side:
        t = next(c for c in (152, 96, 64, 32, 16, 8) if rows % c == 0)
        hb = None
        half_spec = pl.BlockSpec((t, width // 2), lambda i, c_ref: (i, 0))
    else:
        t = next(c for c in (256, 128, 64, 32, 16, 8) if (rows // 2) % c == 0)
        hb = rows // 2 // t
        half_spec = pl.BlockSpec((t, width), lambda i, c_ref: (i % hb, 0))

    def body(c_ref, w_ref, a_ref, b_ref, m_ref, v_ref, g_ref, d_ref, nm_ref, nv_ref):
        south = c_ref[0] == 0
        if side_by_side:
            g = jnp.where(south, jnp.concatenate([a_ref[...], b_ref[...]], axis=1), jnp.concatenate([b_ref[...], a_ref[...]], axis=1))
        else:
            g = jnp.where((pl.program_id(0) < hb) == south, a_ref[...], b_ref[...])
        g_ref[...] = g
        d_ref[...], nm_ref[...], nv_ref[...] = _adam_math(w_ref[...], g, m_ref[...], v_ref[...])

    spec = pl.BlockSpec((t, width), lambda i, c_ref: (i, 0))
    return pl.pallas_call(
        body, name=name,
        grid_spec=pltpu.PrefetchScalarGridSpec(num_scalar_prefetch=1, grid=(rows // t,),
                                               in_specs=[spec, half_spec, half_spec, spec, spec], out_specs=[spec] * 4),
        out_shape=[jax.ShapeDtypeStruct(w.shape, F32)] * 4, compiler_params=_params(("parallel",)),
    )(core, w, mine, theirs, m, v)


def _adamw_small(ws, gs, ms, vs):
    n = len(ws)

    def body(*refs):
        for t in range(n):
            w_ref, g_ref, m_ref, v_ref = (refs[k * n + t] for k in range(4))
            d, nm, nv = _adam_math(w_ref[...], g_ref[...], m_ref[...], v_ref[...])
            refs[4 * n + t][...] = d
            refs[5 * n + t][...] = nm
            refs[6 * n + t][...] = nv

    vmem = pl.BlockSpec(memory_space=pltpu.VMEM)
    res = pl.pallas_call(body, name="adamw_small", in_specs=[vmem] * (4 * n), out_specs=[vmem] * (3 * n),
                         out_shape=[jax.ShapeDtypeStruct(a.shape, F32) for a in ws] * 3)(*ws, *gs, *ms, *vs)
    return res[:n], res[n:2 * n], res[2 * n:]


def kernel(x, w_in, b_gate, g_q_a, w_uq, g_kv_a, w_ukv, w_o_mla, w_o_dil, w_out, ln1_g, ln1_b, w_ff1, w_ff2, ln2_g, ln2_b, loss_target, m_w_in, m_b_gate, m_g_q_a, m_w_uq, m_g_kv_a, m_w_ukv, m_w_o_mla, m_w_o_dil, m_w_out, m_ln1_g, m_ln1_b, m_w_ff1, m_w_ff2, m_ln2_g, m_ln2_b, v_w_in, v_b_gate, v_g_q_a, v_w_uq, v_g_kv_a, v_w_ukv, v_w_o_mla, v_w_o_dil, v_w_out, v_ln1_g, v_ln1_b, v_w_ff1, v_w_ff2, v_ln2_g, v_ln2_b):
    order = ("w_in", "b_gate", "g_q_a", "w_uq", "g_kv_a", "w_ukv", "w_o_mla", "w_o_dil", "w_out", "ln1_g", "ln1_b", "w_ff1", "w_ff2", "ln2_g", "ln2_b")
    w = dict(w_in=w_in, b_gate=b_gate, g_q_a=g_q_a, w_uq=w_uq, g_kv_a=g_kv_a, w_ukv=w_ukv, w_o_mla=w_o_mla, w_o_dil=w_o_dil, w_out=w_out,
             ln1_g=ln1_g, ln1_b=ln1_b, w_ff1=w_ff1, w_ff2=w_ff2, ln2_g=ln2_g, ln2_b=ln2_b)
    m = dict(w_in=m_w_in, b_gate=m_b_gate, g_q_a=m_g_q_a, w_uq=m_w_uq, g_kv_a=m_g_kv_a, w_ukv=m_w_ukv, w_o_mla=m_w_o_mla, w_o_dil=m_w_o_dil,
             w_out=m_w_out, ln1_g=m_ln1_g, ln1_b=m_ln1_b, w_ff1=m_w_ff1, w_ff2=m_w_ff2, ln2_g=m_ln2_g, ln2_b=m_ln2_b)
    v = dict(w_in=v_w_in, b_gate=v_b_gate, g_q_a=v_g_q_a, w_uq=v_w_uq, g_kv_a=v_g_kv_a, w_ukv=v_w_ukv, w_o_mla=v_w_o_mla, w_o_dil=v_w_o_dil,
             w_out=v_w_out, ln1_g=v_ln1_g, ln1_b=v_ln1_b, w_ff1=v_w_ff1, w_ff2=v_w_ff2, ln2_g=v_ln2_g, ln2_b=v_ln2_b)
    chip = 2 * lax.axis_index("x") + lax.axis_index("y")
    south = (lax.axis_index("c") == 0).astype(F32)
    gate_w = D_MODEL // N_CHIPS

    core = lax.axis_index("c").astype(jnp.int32).reshape(1)
    turn = lambda n, a: a.T if n in TRANSPOSED else a
    shards = {n: turn(n, w[n][0]).astype(BF16) for n in BIG}
    first = dict(zip(EARLY, _gather_weights([shards[n] for n in EARLY], "gather_early")))
    late_shards = [shards[n] for n in LATE]
    g_sems, g_srcs, g_lands, g_token = _split_start(
        "gather", late_shards, [jax.ShapeDtypeStruct((N_CHIPS,) + s.shape, BF16) for s in late_shards], "gather_late_start")
    b_mine = lax.dynamic_update_slice(jnp.zeros((2, D_MODEL), F32), b_gate[0] * south, (0, chip * gate_w))
    b_full = _sum_all_devices(b_mine.reshape(-1, LANES), "gather_b_gate").reshape(2, D_MODEL)

    sent = {}

    def late_arrived(after):
        srcs, lands = _split_wait("gather", g_sems, g_srcs, g_lands, after, "gather_late_wait")
        sent["forward"] = _split_start("forward", srcs, None, "gather_late_forward_start", lands=lands)
        return sent["forward"][-1]

    def late_weights(after):
        return dict(zip(LATE, _split_wait("forward", *sent["forward"][:3], after, "gather_late_forward_wait")[1]))

    exchange_shapes = lambda parts: [jax.ShapeDtypeStruct((3,) + p.shape[1:], BF16) for p in parts]

    def early_grads(grads_late):
        gs = [grads_late[n] for n in LATE]
        shapes = [jax.ShapeDtypeStruct((N_CHIPS,) + _half_shape(g.shape[1:]), F32) for g in gs]
        sent["pair"] = _split_start("pair", gs, shapes, "pair_split_late_start")
        return sent["pair"][-1]

    def early_grads_go(after):
        gs, theirs = _split_wait("pair", *sent["pair"][:3], after, "pair_split_late_wait")
        parts = [_pair_add(g, th, core, "pair_add_" + n) for g, th, n in zip(gs, theirs, LATE)]
        sent["late"] = _split_start("scatter", parts, exchange_shapes(parts), "exchange_late_start")
        return sent["late"][-1]

    def last_grads(grads_early):
        parts = _chip_partials(grads_early, EARLY, core, "early")
        sent["early"] = _split_start("scatter", parts, exchange_shapes(parts), "exchange_early_start")
        return sent["early"][-1]

    loss_part, grad_x, grads = _local_step(x, loss_target, first, b_full, g_q_a, g_kv_a, ln1_g, ln1_b, ln2_g, ln2_b, token=g_token,
                                           late_arrived=late_arrived, late_weights=late_weights, early_grads=early_grads, early_grads_go=early_grads_go,
                                           last_grads=last_grads)

    g_out, delta, new_m, new_v = {}, {}, {}, {}
    chip1 = chip.astype(jnp.int32).reshape(1)

    def finish(names, parts, others, tag):
        totals = [_chip_sum(p, o, chip1, "chip_sum_" + n) for n, p, o in zip(names, parts, others)]
        for n, mine, theirs in zip(names, totals, _pair_join(totals, "pair_join_" + tag)):
            res = _adamw_big(turn(n, w[n][0]), mine, theirs, turn(n, m[n][0]), turn(n, v[n][0]), core, "adamw_" + n,
                             side_by_side=mine.shape[0] == shards[n].shape[0])
            g_out[n], delta[n], new_m[n], new_v[n] = (turn(n, r) for r in res)

    finish(LATE, *_split_wait("scatter", *sent["late"][:3], grad_x, "exchange_late_wait"), "late")
    small_names = [name for name, _ in SMALL]
    sums = _sum_small([grads[name] for name in small_names] + [loss_part])
    loss = sums[-1][0, 0]
    g_small = dict(zip(small_names, sums))
    g_small["b_gate"] = lax.dynamic_slice(g_small["b_gate"], (0, chip * gate_w), (2, gate_w))
    flat = lambda a: a.reshape(-1, a.shape[-1])
    res = _adamw_small(*[[flat(d[name]) for name in small_names] for d in (w, g_small, m, v)])
    g_out.update(g_small)
    for d, r in zip((delta, new_m, new_v), res):
        d.update(zip(small_names, r))
    done = res[0][0][0:1, 0:1]
    for n in LATE:
        done = done + delta[n][0:1, 0:1]
    finish(EARLY, *_split_wait("scatter", *sent["early"][:3], done, "exchange_early_wait"), "early")


    lead = lambda d: [d[name].reshape(w[name].shape) for name in order]
    return (loss, grad_x, *lead(g_out), *lead(delta), *lead(new_m), *lead(new_v))
```

```python
import functools
import math

import jax
import jax.numpy as jnp
from jax import lax
from jax.experimental import pallas as pl
from jax.experimental.pallas import tpu as pltpu

F32 = jnp.float32
BF16 = jnp.bfloat16
MESH = pl.DeviceIdType.MESH

D_MODEL = 1024
N_HEADS = 8
LANES = 128
NOPE, ROPE, V_DIM = 64, 32, 64
MLA_QK = NOPE + ROPE
Q_LORA, KV_LORA = 384, 256
DIL_DIM = 64
DIL_PATTERNS = ((128, 1), (512, 4), (2048, 16))
D_FF = 4096
N_CHIPS = 4
N_DEV = 8
IN_WIDTH = 4256
LN_EPS, RMS_EPS = 1e-5, 1e-6
NEG = -1e30
ALPHA = 2.0 ** 0.25
ROPE_THETA = 10000.0
LR, B1, B2, ADAM_EPS, WD, ADAM_STEP = 0.001, 0.9, 0.999, 1e-8, 0.01, 10

P_GATE, P_QD, P_KD, P_VD, P_LORA, P_KR, P_WIDTH = 0, 2048, 3072, 4096, 5120, 5760, 6144
LORA_W = Q_LORA + KV_LORA
KR_LANE = NOPE

ATT_T = 512
ROW_T = 512
VMEM_LIMIT = 56 * 1024 * 1024

NN = (((1,), (0,)), ((), ()))
NT = (((1,), (1,)), ((), ()))
TN = (((0,), (0,)), ((), ()))


def _params(sem=None, **kw):
    return pltpu.CompilerParams(dimension_semantics=sem, vmem_limit_bytes=VMEM_LIMIT, **kw)


def _matmul(a, b, *, mode, name, tm, tn, tk, out_dtypes=(F32,), extras=(), epilogue=None, b_shards=False, out_shards=False, after=None):
    pieces = list(a) if isinstance(a, (list, tuple)) else [a]
    n_pc = len(pieces)
    a_shape = (pieces[0].shape[0], sum(p.shape[1] for p in pieces))
    if b_shards:
        n_sh, rows_b, cols_b = b.shape
        b_shape = (rows_b, n_sh * cols_b)
    else:
        b_shape = b.shape
    if mode == "nn":
        (m, k), (k2, n) = a_shape, b_shape
    elif mode == "nt":
        (m, k), (n, k2) = a_shape, b_shape
    else:
        (k, m), (k2, n) = a_shape, b_shape
    assert k == k2, (a_shape, b.shape, mode)
    tm, tn, tk = min(tm, m), min(tn, n), min(tk, k)
    assert m % tm == 0 and n % tn == 0 and k % tk == 0, (name, m, n, k, tm, tn, tk)
    nk = k // tk
    n_ex, n_out = len(extras), len(out_dtypes)
    n_in = n_pc + 1 + n_ex + (after is not None)
    dims = {"nn": NN, "nt": NT, "tn": TN}[mode]
    col_tile = tm if mode == "tn" else tk
    blocks = [p.shape[1] // col_tile for p in pieces]
    firsts = [sum(blocks[:p]) for p in range(n_pc)]
    assert all(p.shape[1] % col_tile == 0 for p in pieces), (name, col_tile)

    def body(*refs):
        a_refs, b_ref = refs[:n_pc], refs[n_pc]
        ex_refs = refs[n_pc + 1:n_pc + 1 + n_ex]
        out_refs = refs[n_in:n_in + n_out]

        def finish(acc):
            outs = epilogue(acc, *[r[...] for r in ex_refs]) if epilogue is not None else (acc,)
            for r, o in zip(out_refs, outs):
                r[...] = o.astype(r.dtype)

        kk = pl.program_id(2)

        def step(a_ref):
            part = lax.dot_general(a_ref[...].astype(BF16), b_ref[...].astype(BF16), dims, preferred_element_type=F32)
            if nk == 1:
                finish(part)
                return
            acc_ref = refs[-1]

            @pl.when(kk == 0)
            def _():
                acc_ref[...] = part

            @pl.when(kk > 0)
            def _():
                acc_ref[...] += part

            @pl.when(kk == nk - 1)
            def _():
                finish(acc_ref[...])

        if n_pc == 1:
            step(a_refs[0])
        else:
            at = pl.program_id(0) if mode == "tn" else kk
            for p in range(n_pc):
                pl.when(jnp.logical_and(at >= firsts[p], at < firsts[p] + blocks[p]))(functools.partial(step, a_refs[p]))

    def a_spec_of(p):
        if n_pc == 1:
            return pl.BlockSpec((tk, tm), lambda i, j, kk: (kk, i)) if mode == "tn" else pl.BlockSpec((tm, tk), lambda i, j, kk: (i, kk))
        col = lambda at: jnp.clip(at - firsts[p], 0, blocks[p] - 1)
        mine = lambda at: jnp.logical_and(at >= firsts[p], at < firsts[p] + blocks[p])
        if mode == "tn":
            return pl.BlockSpec((tk, tm), lambda i, j, kk: (jnp.where(mine(i), kk, 0), col(i)))
        return pl.BlockSpec((tm, tk), lambda i, j, kk: (i, col(kk)))

    b_spec = {"nn": pl.BlockSpec((tk, tn), lambda i, j, kk: (kk, j)),
              "nt": pl.BlockSpec((tn, tk), lambda i, j, kk: (j, kk)),
              "tn": pl.BlockSpec((tk, tn), lambda i, j, kk: (kk, j))}[mode]
    tile = pl.BlockSpec((tm, tn), lambda i, j, kk: (i, j))
    out_spec, out_dims = tile, (m, n)
    if b_shards and mode == "nn":
        per = cols_b // tn
        b_spec = pl.BlockSpec((None, tk, tn), lambda i, j, kk: (j // per, kk, j % per))
    elif b_shards:
        assert mode == "nt"
        per = cols_b // tk
        b_spec = pl.BlockSpec((None, tn, tk), lambda i, j, kk: (kk // per, j, kk % per))
    if out_shards:
        assert not extras and epilogue is None
        per_out = n // N_CHIPS // tn
        out_spec = pl.BlockSpec((None, tm, tn), lambda i, j, kk: (j // per_out, i, j % per_out))
        out_dims = (N_CHIPS, m, n // N_CHIPS)
    outs = pl.pallas_call(
        body, name=name,
        grid=(m // tm, n // tn, nk),
        in_specs=[a_spec_of(p) for p in range(n_pc)] + [b_spec] + [tile] * n_ex + [pl.BlockSpec(memory_space=pl.ANY)] * (after is not None),
        out_specs=[out_spec] * n_out,
        out_shape=[jax.ShapeDtypeStruct(out_dims, dt) for dt in out_dtypes],
        scratch_shapes=[pltpu.VMEM((tm, tn), F32)] if nk > 1 else [],
        compiler_params=_params(("parallel", "parallel", "arbitrary")),
    )(*pieces, b, *extras, *([after] if after is not None else []))
    return outs[0] if n_out == 1 else outs


def _rowwise(fn, *, name, rows, seq, ins, outs, sums=()):
    tm = min(ROW_T, seq)
    n_pos = seq // tm
    n_in, n_out, n_sum = len(ins), len(outs), len(sums)

    def body(*refs):
        vals = fn(*[r[...] for r in refs[:n_in]])
        for r, v in zip(refs[n_in:n_in + n_out], vals[:n_out]):
            r[...] = v.astype(r.dtype)
        first = pl.program_id(0) == 0
        for r, v in zip(refs[n_in + n_out:], vals[n_out:]):
            @pl.when(first)
            def _(r=r, v=v):
                r[...] = v

            @pl.when(jnp.logical_not(first))
            def _(r=r, v=v):
                r[...] += v

    def spec(arr, width, col, kind):
        if kind == "row":
            return pl.BlockSpec((tm, width), lambda i, col=col: (i, col))
        if kind == "pos":
            return pl.BlockSpec((tm, width), lambda i, col=col: (i % n_pos, col))
        return pl.BlockSpec(arr.shape, lambda i: (0,) * arr.ndim)

    res = pl.pallas_call(
        body, name=name,
        grid=(rows // tm,),
        in_specs=[spec(*t) for t in ins],
        out_specs=[pl.BlockSpec((tm, w), lambda i: (i, 0)) for w, _ in outs]
        + [pl.BlockSpec((1, w), lambda i: (0, 0)) for w in sums],
        out_shape=[jax.ShapeDtypeStruct((rows, w), dt) for w, dt in outs]
        + [jax.ShapeDtypeStruct((1, w), F32) for w in sums],
        compiler_params=_params(("arbitrary",)),
    )(*[t[0] for t in ins])
    return res


def _colsum(v):
    return jnp.sum(v, axis=0, keepdims=True)


def _rope_fwd(t, c, s_up, s_dn):
    return t * c + pltpu.roll(t, LANES - 16, 1) * s_up + pltpu.roll(t, 16, 1) * s_dn


def _rope_bwd(d, c, s_up, s_dn):
    return d * c + pltpu.roll(d * s_up, 16, 1) + pltpu.roll(d * s_dn, LANES - 16, 1)


def _rope_tables(seq):
    half = ROPE // 2
    inv = jnp.power(ROPE_THETA, -jnp.arange(half, dtype=F32) / half)
    ang = jnp.arange(seq, dtype=F32)[:, None] * inv[None, :]
    cos, sin = jnp.cos(ang), jnp.sin(ang)
    zeros = jnp.zeros((seq, half), F32)
    lo, hi = jnp.ones((seq, KR_LANE), F32), jnp.ones((seq, LANES - KR_LANE - ROPE), F32)
    c = jnp.concatenate([lo, cos, cos, hi], axis=1)
    c_rope_only = jnp.concatenate([0 * lo, cos, cos, 0 * hi], axis=1)
    s_up = jnp.concatenate([0 * lo, -sin, zeros, 0 * hi], axis=1)
    s_dn = jnp.concatenate([0 * lo, zeros, sin, 0 * hi], axis=1)
    return c, s_up, s_dn, c_rope_only


def _rms(x, g):
    r = lax.rsqrt(jnp.mean(x * x, axis=1, keepdims=True) + RMS_EPS)
    return x * r * g


def _rms_bwd(x, g, dy):
    r = lax.rsqrt(jnp.mean(x * x, axis=1, keepdims=True) + RMS_EPS)
    xh = x * r
    dxh = dy * g
    dx = r * (dxh - xh * jnp.mean(dxh * xh, axis=1, keepdims=True))
    return dx, _colsum(dy * xh)


def _ln_stats(x):
    mu = jnp.mean(x, axis=1, keepdims=True)
    xc = x - mu
    r = lax.rsqrt(jnp.mean(xc * xc, axis=1, keepdims=True) + LN_EPS)
    return xc * r, r


def _ln_bwd(xh, r, g, dy):
    dxh = dy * g
    dx = r * (dxh - jnp.mean(dxh, axis=1, keepdims=True) - xh * jnp.mean(dxh * xh, axis=1, keepdims=True))
    return dx, _colsum(dy * xh), _colsum(dy)


def _table_specs(tables):
    whole = lambda a: pl.BlockSpec(a.shape, lambda b, h: (0,) * a.ndim)
    if len(tables) == 1:
        return [whole(tables[0])]
    return [whole(tables[0]), whole(tables[1]), pl.BlockSpec((None, 1, LANES), lambda b, h: (h, 0, 0))]


def _biased(s, table_refs, delta):
    if delta < table_refs[0].shape[0]:
        s = s + table_refs[0][delta]
    if len(table_refs) == 3:
        s = s - table_refs[2][0:1, 0:1] * table_refs[1][delta]
    return s


def _attn_fwd(q, qb0, k, kb0, v, vb0, tables, scale, *, name, batch, seq, after=None):
    t = ATT_T
    nq = seq // t
    rows = batch * seq
    n_tab = len(tables)

    def body(q_ref, k_ref, v_ref, *rest):
        table_refs = rest[:n_tab]
        o_ref, lse_ref, qb, kb, vtb = rest[n_tab + (after is not None):]
        qb[...] = q_ref[...].astype(BF16)
        kb[...] = k_ref[...].astype(BF16)
        vtb[...] = v_ref[...].astype(F32).T.astype(BF16)
        for i in range(nq):
            qt = qb[i * t:(i + 1) * t, :]
            logits = [_biased(lax.dot_general(kb[j * t:(j + 1) * t, :], qt, NT, preferred_element_type=F32) * scale, table_refs, i - j)
                      for j in range(i + 1)]
            m = jnp.max(functools.reduce(jnp.maximum, logits), axis=0, keepdims=True)
            ps = [jnp.exp(s - m) for s in logits]
            l = jnp.sum(functools.reduce(jnp.add, ps), axis=0, keepdims=True)
            acc = functools.reduce(jnp.add, [lax.dot_general(vtb[:, j * t:(j + 1) * t], p.astype(BF16), NN, preferred_element_type=F32)
                                             for j, p in enumerate(ps)])
            o_ref[i * t:(i + 1) * t, :] = (acc / l).T
            lse_ref[i * t:(i + 1) * t, :] = jnp.broadcast_to(m + jnp.log(l), (LANES, t)).T

    slab = lambda b0: pl.BlockSpec((seq, LANES), lambda b, h, b0=b0: (b, b0 + h))
    return pl.pallas_call(
        body, name=name,
        grid=(batch, N_HEADS),
        in_specs=[slab(qb0), slab(kb0), slab(vb0)] + _table_specs(tables) + [pl.BlockSpec(memory_space=pl.ANY)] * (after is not None),
        out_specs=[slab(0), slab(0)],
        out_shape=[jax.ShapeDtypeStruct((rows, N_HEADS * LANES), F32)] * 2,
        scratch_shapes=[pltpu.VMEM((seq, LANES), BF16)] * 2 + [pltpu.VMEM((LANES, seq), BF16)],
        compiler_params=_params(("arbitrary", "arbitrary")),
    )(q, k, v, *tables, *([after] if after is not None else []))


def _attn_bwd(q, qb0, k, kb0, v, vb0, o, do, lse, tables, scale, *, name, batch, seq, out_dtype, after=None):
    t = ATT_T
    nq = seq // t
    rows = batch * seq
    n_tab = len(tables)

    def body(q_ref, k_ref, v_ref, o_ref, do_ref, lse_ref, *rest):
        table_refs = rest[:n_tab]
        dq_ref, dk_ref, dv_ref, qb, kb, vb, dob, qtb, dotb, dka, dva = rest[n_tab + (after is not None):]
        qb[...] = q_ref[...].astype(BF16)
        kb[...] = k_ref[...].astype(BF16)
        vb[...] = v_ref[...].astype(BF16)
        dob[...] = do_ref[...].astype(BF16)
        qtb[...] = q_ref[...].astype(F32).T.astype(BF16)
        dotb[...] = do_ref[...].T.astype(BF16)
        for i in range(nq):
            at = slice(i * t, (i + 1) * t)
            qt, dot = qb[at, :], dob[at, :]
            lse_t = lse_ref[at, 0:1]
            delta = jnp.sum(o_ref[at, :] * do_ref[at, :], axis=1, keepdims=True)
            dq = None
            for j in range(i + 1):
                kat = slice(j * t, (j + 1) * t)
                kt, vt = kb[kat, :], vb[kat, :]
                p = jnp.exp(_biased(lax.dot_general(qt, kt, NT, preferred_element_type=F32) * scale, table_refs, i - j) - lse_t)
                dp = lax.dot_general(dot, vt, NT, preferred_element_type=F32)
                ds = (p * (dp - delta) * scale).astype(BF16)
                dk_part = lax.dot_general(qtb[:, at], ds, NN, preferred_element_type=F32)
                dv_part = lax.dot_general(dotb[:, at], p.astype(BF16), NN, preferred_element_type=F32)
                if i == j:
                    dka[:, kat] = dk_part
                    dva[:, kat] = dv_part
                else:
                    dka[:, kat] += dk_part
                    dva[:, kat] += dv_part
                dq_part = lax.dot_general(ds, kt, NN, preferred_element_type=F32)
                dq = dq_part if dq is None else dq + dq_part
            dq_ref[at, :] = dq.astype(dq_ref.dtype)
        dk_ref[...] = dka[...].T.astype(dk_ref.dtype)
        dv_ref[...] = dva[...].T.astype(dv_ref.dtype)

    slab = lambda b0: pl.BlockSpec((seq, LANES), lambda b, h, b0=b0: (b, b0 + h))
    return pl.pallas_call(
        body, name=name,
        grid=(batch, N_HEADS),
        in_specs=[slab(qb0), slab(kb0), slab(vb0), slab(0), slab(0), slab(0)] + _table_specs(tables)
        + [pl.BlockSpec(memory_space=pl.ANY)] * (after is not None),
        out_specs=[slab(0)] * 3,
        out_shape=[jax.ShapeDtypeStruct((rows, N_HEADS * LANES), out_dtype)] * 3,
        scratch_shapes=[pltpu.VMEM((seq, LANES), BF16)] * 4 + [pltpu.VMEM((LANES, seq), BF16)] * 2 + [pltpu.VMEM((LANES, seq), F32)] * 2,
        compiler_params=_params(("arbitrary", "arbitrary")),
    )(q, k, v, o, do, lse, *tables, *([after] if after is not None else []))


def _attention_tables(seq):
    n = seq // ATT_T
    pos = jnp.arange(ATT_T, dtype=jnp.int32)
    dist = jnp.arange(n, dtype=jnp.int32)[:, None, None] * ATT_T + pos[None, :, None] - pos[None, None, :]
    causal = jnp.where(dist[:1] >= 0, 0.0, NEG).astype(F32)
    count = jnp.zeros(dist.shape, F32)
    for window, dilation in DIL_PATTERNS:
        count += ((dist >= 0) & (dist <= window) & (dist % dilation == 0)).astype(F32)
    held = jnp.where(count > 0, jnp.log(jnp.maximum(count, 1.0)), NEG).astype(F32)
    slopes = jnp.asarray([2.0 ** (-8.0 * (i + 1) / N_HEADS) for i in range(N_HEADS)], F32)
    slopes = jnp.broadcast_to(slopes[:, None, None], (N_HEADS, 1, LANES))
    turned = lambda a: jnp.swapaxes(a, 1, 2)
    far = dist.astype(F32)
    return ((causal,), (turned(causal),)), ((held, far, slopes), (turned(held), turned(far), slopes))


def _pad_heads(w, width):
    kdim, n = w.shape[0], w.shape[1] // width
    return jnp.pad(w.reshape(kdim, n, width), ((0, 0), (0, 0), (0, LANES - width))).reshape(kdim, n * LANES)


def _unpad_heads(w, width):
    kdim, n = w.shape[0], w.shape[1] // LANES
    return w.reshape(kdim, n, LANES)[:, :, :width].reshape(kdim, n * width)


def _pad_head_rows(w, width):
    n, kdim = w.shape[0] // width, w.shape[1]
    return jnp.pad(w.reshape(n, width, kdim), ((0, 0), (0, LANES - width), (0, 0))).reshape(n * LANES, kdim)


def _unpad_head_rows(w, width):
    n, kdim = w.shape[0] // LANES, w.shape[1]
    return w.reshape(n, LANES, kdim)[:, :width].reshape(n * width, kdim)


def _pad_w_in_t(wt):
    n_qkv = 3 * N_HEADS * DIL_DIM
    zeros = lambda n: jnp.zeros((n, wt.shape[1]), wt.dtype)
    return jnp.concatenate([wt[LORA_W + ROPE + n_qkv:], _pad_head_rows(wt[LORA_W + ROPE:LORA_W + ROPE + n_qkv], DIL_DIM), wt[:LORA_W],
                            zeros(KR_LANE), wt[LORA_W:LORA_W + ROPE], zeros(LANES - KR_LANE - ROPE), zeros(P_WIDTH - P_KR - LANES)], axis=0)


def _unpad_w_in_t(gt):
    return jnp.concatenate([gt[P_LORA:P_KR], gt[P_KR + KR_LANE:P_KR + KR_LANE + ROPE], _unpad_head_rows(gt[P_QD:P_LORA], DIL_DIM),
                            gt[P_GATE:P_QD]], axis=0)


def _split_ukv(w):
    w3 = w.reshape(w.shape[0], N_HEADS, NOPE + V_DIM)
    return (_pad_heads(w3[:, :, :NOPE].reshape(w.shape[0], -1), NOPE),
            _pad_heads(w3[:, :, NOPE:].reshape(w.shape[0], -1), V_DIM))


def _merge_ukv(g_k, g_v):
    kdim = g_k.shape[0]
    k3 = _unpad_heads(g_k, NOPE).reshape(kdim, N_HEADS, NOPE)
    v3 = _unpad_heads(g_v, V_DIM).reshape(kdim, N_HEADS, V_DIM)
    return jnp.concatenate([k3, v3], axis=2).reshape(kdim, N_HEADS * (NOPE + V_DIM))


def _pad_rows(w, width):
    return _pad_heads(w.T, width).T


def _unpad_rows(g, width):
    return _unpad_heads(g.T, width).T


def _join_cols(w):
    return w.transpose(1, 0, 2).reshape(w.shape[1], N_CHIPS * w.shape[2])


def _split_cols(g):
    return g.reshape(g.shape[0], N_CHIPS, g.shape[1] // N_CHIPS).transpose(1, 0, 2)


def _local_step(x3, target3, wg, b_gate, g_q_a, g_kv_a, ln1_g, ln1_b, ln2_g, ln2_b, token=None, late_arrived=None, late_weights=None,
                early_grads=None, early_grads_go=None, last_grads=None):
    w_in_pt = _pad_w_in_t(wg["w_in"].reshape(IN_WIDTH, D_MODEL))
    w_uq_pt = _pad_head_rows(wg["w_uq"].reshape(N_HEADS * MLA_QK, Q_LORA), MLA_QK)
    w_ukv = _join_cols(wg["w_ukv"])
    batch, seq, _ = x3.shape
    rows = batch * seq
    x = x3.reshape(rows, D_MODEL)
    target = target3.reshape(rows, D_MODEL)
    row = functools.partial(_rowwise, rows=rows, seq=seq)
    mm = _matmul

    w_uk_p, w_uv_p = _split_ukv(w_ukv)
    b0, b1 = b_gate[0:1], b_gate[1:2]
    rope_c, rope_up, rope_dn, rope_c_only = _rope_tables(seq)
    (mla_bwd_tables, mla_fwd_tables), (dil_bwd_tables, dil_fwd_tables) = _attention_tables(seq)
    scale_mla, scale_dil = MLA_QK ** -0.5, DIL_DIM ** -0.5
    qd0, kd0, vd0, lora0, kr0 = P_QD // LANES, P_KD // LANES, P_VD // LANES, P_LORA // LORA_W, P_KR // LANES

    proj = mm(x, w_in_pt, mode="nt", name="proj", tm=1024, tn=1536, tk=1024, after=token)

    def prep(lora, gq, gkv):
        return _rms(lora[:, :Q_LORA], gq), _rms(lora[:, Q_LORA:], gkv)

    qn, kvn = row(prep, name="mla_rms", ins=[(proj, LORA_W, lora0, "row"), (g_q_a, 0, 0, "full"), (g_kv_a, 0, 0, "full")],
                  outs=[(Q_LORA, BF16), (KV_LORA, BF16)])
    q_lin = mm(qn, w_uq_pt, mode="nt", name="q_up", tm=1024, tn=1024, tk=Q_LORA)
    k_lin = mm(kvn, w_uk_p, mode="nn", name="k_up", tm=1024, tn=1024, tk=KV_LORA)
    v_a = mm(kvn, w_uv_p, mode="nn", name="v_up", tm=1024, tn=1024, tk=KV_LORA, out_dtypes=(BF16,))

    def rope_qk(ql, kl, kr, c, up, dn):
        k_rot = _rope_fwd(kr, c, up, dn)
        qs = [_rope_fwd(ql[:, h * LANES:(h + 1) * LANES], c, up, dn) for h in range(N_HEADS)]
        ks = [kl[:, h * LANES:(h + 1) * LANES] + k_rot for h in range(N_HEADS)]
        return jnp.concatenate(qs, axis=1), jnp.concatenate(ks, axis=1)

    pos = lambda tab: (tab, LANES, 0, "pos")
    q_a, k_a = row(rope_qk, name="rope_qk",
                   ins=[(q_lin, D_MODEL, 0, "row"), (k_lin, D_MODEL, 0, "row"), (proj, LANES, kr0, "row"), pos(rope_c), pos(rope_up), pos(rope_dn)],
                   outs=[(N_HEADS * LANES, BF16), (N_HEADS * LANES, BF16)])
    o_a, lse_a = _attn_fwd(q_a, 0, k_a, 0, v_a, 0, mla_fwd_tables, scale_mla, name="mla_fwd", batch=batch, seq=seq)
    arrived = None if late_arrived is None else late_arrived(o_a)
    o_b, lse_b = _attn_fwd(proj, qd0, proj, kd0, proj, vd0, dil_fwd_tables, scale_dil, name="dil_fwd", batch=batch, seq=seq, after=arrived)
    late = wg if late_weights is None else late_weights(o_b)
    w_oa_p = _pad_rows(_join_cols(late["w_o_mla"]), V_DIM)
    w_ob_p = _pad_rows(_join_cols(late["w_o_dil"]), DIL_DIM)
    w_out, w_ff1, w_ff2 = late["w_out"].reshape(D_MODEL, D_MODEL), late["w_ff1"], late["w_ff2"].reshape(D_FF, D_MODEL)
    y_a = mm(o_a, w_oa_p, mode="nn", name="o_mla", tm=1024, tn=1024, tk=1024)
    y_b = mm(o_b, w_ob_p, mode="nn", name="o_dil", tm=1024, tn=1024, tk=1024)

    def gate(t0, t1, c0, c1, ya, yb):
        return (jax.nn.sigmoid(t0 + c0) * ya + jax.nn.sigmoid(t1 + c1) * yb,)

    gate_ins = [(proj, D_MODEL, 0, "row"), (proj, D_MODEL, 1, "row"), (b0, 0, 0, "full"), (b1, 0, 0, "full")]
    (u,) = row(gate, name="gate", ins=gate_ins + [(y_a, D_MODEL, 0, "row"), (y_b, D_MODEL, 0, "row")], outs=[(D_MODEL, BF16)])
    mixed = mm(u, w_out, mode="nn", name="mix", tm=1024, tn=1024, tk=1024)

    def ln1(xv, mv, g, b):
        r1 = ALPHA * xv + mv
        xh, _ = _ln_stats(r1)
        return r1, xh * g + b

    r1, h = row(ln1, name="ln1", ins=[(x, D_MODEL, 0, "row"), (mixed, D_MODEL, 0, "row"), (ln1_g, 0, 0, "full"), (ln1_b, 0, 0, "full")],
                outs=[(D_MODEL, F32), (D_MODEL, F32)])

    def relu2(acc):
        r = jnp.maximum(acc, 0.0)
        return acc, r * r

    a_ff, z = mm(h, w_ff1, mode="nn", name="ff1", tm=1024, tn=1024, tk=1024, out_dtypes=(F32, BF16), epilogue=relu2, b_shards=True)
    f = mm(z, w_ff2, mode="nn", name="ff2", tm=1024, tn=1024, tk=2048)

    def ln2_loss(hv, fv, tv, g, b):
        xh, r = _ln_stats(ALPHA * hv + fv)
        err = xh * g + b - tv
        dy = err * (1.0 / D_MODEL)
        dr2, dg, db = _ln_bwd(xh, r, g, dy)
        loss = jnp.sum(_colsum(err * err), axis=1, keepdims=True) * (0.5 / D_MODEL)
        return dr2, jnp.broadcast_to(loss, (1, LANES)), dg, db

    dr2, loss_l, d_ln2_g, d_ln2_b = row(
        ln2_loss, name="ln2_loss",
        ins=[(h, D_MODEL, 0, "row"), (f, D_MODEL, 0, "row"), (target, D_MODEL, 0, "row"), (ln2_g, 0, 0, "full"), (ln2_b, 0, 0, "full")],
        outs=[(D_MODEL, F32)], sums=[LANES, D_MODEL, D_MODEL])

    d_w_ff2 = mm(z, dr2, mode="tn", name="d_w_ff2", tm=1024, tn=1024, tk=2048)
    da = mm(dr2, w_ff2, mode="nt", name="d_ff_act", tm=1024, tn=1024, tk=1024, out_dtypes=(BF16,), extras=(a_ff,),
            epilogue=lambda acc, av: (acc * (2.0 * jnp.maximum(av, 0.0)),))
    d_w_ff1 = mm(h, da, mode="tn", name="d_w_ff1", tm=1024, tn=1024, tk=2048, out_shards=True)
    dh = mm(da, w_ff1, mode="nt", name="d_h", tm=1024, tn=1024, tk=1024, extras=(dr2,), epilogue=lambda acc, rv: (acc + ALPHA * rv,), b_shards=True)

    def ln1_bwd(dhv, r1v, g):
        xh, r = _ln_stats(r1v)
        return _ln_bwd(xh, r, g, dhv)

    dr1, d_ln1_g, d_ln1_b = row(ln1_bwd, name="ln1_bwd", ins=[(dh, D_MODEL, 0, "row"), (r1, D_MODEL, 0, "row"), (ln1_g, 0, 0, "full")],
                                outs=[(D_MODEL, F32)], sums=[D_MODEL, D_MODEL])
    d_w_out = mm(u, dr1, mode="tn", name="d_w_out", tm=1024, tn=1024, tk=1024)
    du = mm(dr1, w_out, mode="nt", name="d_u", tm=1024, tn=1024, tk=1024)

    def gate_bwd(t0, t1, c0, c1, ya, yb, duv):
        s0, s1 = jax.nn.sigmoid(t0 + c0), jax.nn.sigmoid(t1 + c1)
        dt0 = duv * ya * s0 * (1.0 - s0)
        dt1 = duv * yb * s1 * (1.0 - s1)
        return duv * s0, duv * s1, jnp.concatenate([dt0, dt1], axis=1), jnp.concatenate([_colsum(dt0), _colsum(dt1)], axis=1)

    dy_a, dy_b, d_gates, d_b_gate = row(
        gate_bwd, name="gate_bwd", ins=gate_ins + [(y_a, D_MODEL, 0, "row"), (y_b, D_MODEL, 0, "row"), (du, D_MODEL, 0, "row")],
        outs=[(D_MODEL, BF16), (D_MODEL, BF16), (2 * D_MODEL, BF16)], sums=[2 * D_MODEL])
    d_w_oa_p = mm(o_a, dy_a, mode="tn", name="d_w_o_mla", tm=1024, tn=1024, tk=1024)
    d_w_ob_p = mm(o_b, dy_b, mode="tn", name="d_w_o_dil", tm=1024, tn=1024, tk=1024)
    grads = dict(w_o_mla=_split_cols(_unpad_rows(d_w_oa_p, V_DIM)), w_o_dil=_split_cols(_unpad_rows(d_w_ob_p, DIL_DIM)),
                 w_out=d_w_out.reshape(N_CHIPS, D_MODEL // N_CHIPS, D_MODEL), w_ff1=d_w_ff1, w_ff2=d_w_ff2.reshape(N_CHIPS, D_FF // N_CHIPS, D_MODEL))
    sent = None if early_grads is None else early_grads(grads)
    do_a = mm(dy_a, w_oa_p, mode="nt", name="d_o_mla", tm=1024, tn=1024, tk=1024, after=sent)
    do_b = mm(dy_b, w_ob_p, mode="nt", name="d_o_dil", tm=1024, tn=1024, tk=1024)
    dq_a, dk_a, dv_a = _attn_bwd(q_a, 0, k_a, 0, v_a, 0, o_a, do_a, lse_a, mla_bwd_tables, scale_mla,
                                 name="mla_bwd", batch=batch, seq=seq, out_dtype=F32)
    going = None if early_grads_go is None else early_grads_go(dq_a)
    dq_d, dk_d, dv_d = _attn_bwd(proj, qd0, proj, kd0, proj, vd0, o_b, do_b, lse_b, dil_bwd_tables, scale_dil,
                                 name="dil_bwd", batch=batch, seq=seq, out_dtype=BF16, after=going)

    def mla_post(dq, dk, c, up, dn, c_only):
        dqs = [_rope_bwd(dq[:, h * LANES:(h + 1) * LANES], c, up, dn) for h in range(N_HEADS)]
        dk_sum = dk[:, :LANES]
        for h in range(1, N_HEADS):
            dk_sum = dk_sum + dk[:, h * LANES:(h + 1) * LANES]
        return jnp.concatenate(dqs, axis=1), _rope_bwd(dk_sum, c_only, up, dn)

    dq_lin, d_kr = row(mla_post, name="mla_unrope",
                       ins=[(dq_a, D_MODEL, 0, "row"), (dk_a, D_MODEL, 0, "row"), pos(rope_c), pos(rope_up), pos(rope_dn), pos(rope_c_only)],
                       outs=[(N_HEADS * LANES, BF16), (LANES, BF16)])
    d_w_uq_pt = mm(dq_lin, qn, mode="tn", name="d_w_uq", tm=1024, tn=Q_LORA, tk=1024)
    d_w_uk_p = mm(kvn, dk_a, mode="tn", name="d_w_uk", tm=KV_LORA, tn=1024, tk=1024)
    d_w_uv_p = mm(kvn, dv_a, mode="tn", name="d_w_uv", tm=KV_LORA, tn=1024, tk=1024)
    d_qn = mm(dq_lin, w_uq_pt, mode="nn", name="d_qn", tm=1024, tn=Q_LORA, tk=1024)
    d_kvn_k = mm(dk_a, w_uk_p, mode="nt", name="d_kvn_k", tm=1024, tn=KV_LORA, tk=1024)
    d_kvn = mm(dv_a, w_uv_p, mode="nt", name="d_kvn", tm=1024, tn=KV_LORA, tk=1024, extras=(d_kvn_k,), epilogue=lambda acc, e: (acc + e,))

    def rms_bwd(lora, dq, dkv, dkr, gq, gkv):
        dxq, dgq = _rms_bwd(lora[:, :Q_LORA], gq, dq)
        dxk, dgk = _rms_bwd(lora[:, Q_LORA:], gkv, dkv)
        tail = jnp.zeros((dxq.shape[0], P_WIDTH - P_KR - LANES), F32)
        return jnp.concatenate([dxq, dxk, dkr.astype(F32), tail], axis=1), dgq, dgk

    d_tail, d_g_q_a, d_g_kv_a = row(
        rms_bwd, name="mla_rms_bwd",
        ins=[(proj, LORA_W, lora0, "row"), (d_qn, Q_LORA, 0, "row"), (d_kvn, KV_LORA, 0, "row"), (d_kr, LANES, 0, "row"),
             (g_q_a, 0, 0, "full"), (g_kv_a, 0, 0, "full")],
        outs=[(P_WIDTH - P_LORA, BF16)], sums=[Q_LORA, KV_LORA])
    d_proj = [d_gates, dq_d, dk_d, dv_d, d_tail]
    d_w_in_pt = mm(d_proj, x, mode="tn", name="d_w_in", tm=1024, tn=1024, tk=1024)
    grads.update(w_in=_unpad_w_in_t(d_w_in_pt).reshape(N_CHIPS, IN_WIDTH // N_CHIPS, D_MODEL),
                 w_uq=_unpad_head_rows(d_w_uq_pt, MLA_QK).reshape(N_CHIPS, N_HEADS * MLA_QK // N_CHIPS, Q_LORA),
                 w_ukv=_split_cols(_merge_ukv(d_w_uk_p, d_w_uv_p)))
    leaving = None if last_grads is None else last_grads(grads)
    grad_x = mm(d_proj, w_in_pt, mode="nn", name="d_x", tm=1024, tn=1024, tk=1024, extras=(dr1,), epilogue=lambda acc, rv: (acc + ALPHA * rv,),
                after=leaving)

    grads.update(
        b_gate=d_b_gate.reshape(2, D_MODEL), g_q_a=d_g_q_a, g_kv_a=d_g_kv_a, ln1_g=d_ln1_g, ln1_b=d_ln1_b, ln2_g=d_ln2_g, ln2_b=d_ln2_b)
    return loss_l, grad_x.reshape(batch, seq, D_MODEL), grads


BIG = ("w_in", "w_uq", "w_ukv", "w_o_mla", "w_o_dil", "w_out", "w_ff1", "w_ff2")
SMALL = (("b_gate", 2 * D_MODEL), ("g_q_a", Q_LORA), ("g_kv_a", KV_LORA), ("ln1_g", D_MODEL), ("ln1_b", D_MODEL),
         ("ln2_g", D_MODEL), ("ln2_b", D_MODEL))
TRANSPOSED = ("w_in", "w_uq")
D2D_PIECES = (4, 2, 1)
ANY = pl.BlockSpec(memory_space=pl.ANY)
SIDE_EFFECTS = pltpu.CompilerParams(has_side_effects=True)


def _place():
    x, y, c = lax.axis_index("x"), lax.axis_index("y"), lax.axis_index("c")
    return x, y, c, ((1 - x, y), (x, 1 - y), (1 - x, 1 - y))


def _half_axis(shape):
    return 0 if shape[0] % 32 == 0 else 1


def _half_shape(shape):
    return (shape[0] // 2, shape[1]) if _half_axis(shape) == 0 else (shape[0], shape[1] // 2)


def _window(ref, lead, shape, which=None, pieces=False):
    axis = _half_axis(shape)
    size = shape[axis] if which is None else shape[axis] // 2
    base = 0 if which is None else which * size
    tile = (16, LANES)[axis]
    count = next(c for c in D2D_PIECES if size % (tile * c) == 0) if pieces else 1
    step = size // count
    spans = [pl.ds(pl.multiple_of(base + i * step, tile), step) for i in range(count)]
    refs = [ref.at[(*lead, s)] if axis == 0 else ref.at[(*lead, slice(None), s)] for s in spans]
    return refs if pieces else refs[0]


def _remote(src, dst, send, recv, to):
    return pltpu.make_async_remote_copy(src_ref=src, dst_ref=dst, send_sem=send, recv_sem=recv, device_id=to, device_id_type=MESH)


def _gather_weights(shards, name):
    n = len(shards)

    def body(*refs):
        srcs, outs, (send, recv) = refs[:n], refs[n:2 * n], refs[2 * n:]
        x, y, c, chips = _place()
        sibling, mine = (x, y, 1 - c), 2 * x + y
        shapes = [s.shape for s in shards]

        def over_ici(t, j, to):
            return _remote(_window(srcs[t], (), shapes[t], c), _window(outs[t], (mine,), shapes[t], c), send.at[t, j], recv.at[t, j], (*to, c))

        def over_d2d(t, j, slot, which, pieces=False):
            from_to = [_window(outs[t], (slot,), shapes[t], which, pieces)] * 2
            if j == 6:
                from_to[0] = _window(srcs[t], (), shapes[t], which, pieces)
            if pieces:
                return [_remote(a, b, send.at[t, j], recv.at[t, j], sibling) for a, b in zip(*from_to)]
            return _remote(*from_to, send.at[t, j], recv.at[t, j], sibling)

        for t in range(n):
            for j, chip in enumerate(chips):
                over_ici(t, j, chip).start()
        for t in range(n):
            for cp in over_d2d(t, 6, mine, None, pieces=True):
                cp.start()
        for t in range(n):
            for j, (cx, cy) in enumerate(chips):
                over_ici(t, j, (cx, cy)).wait_recv()
                for cp in over_d2d(t, 3 + j, 2 * cx + cy, c, pieces=True):
                    cp.start()
        for t in range(n):
            over_d2d(t, 6, mine, None).wait()
            for j, (cx, cy) in enumerate(chips):
                over_d2d(t, 3 + j, 2 * cx + cy, 1 - c).wait_recv()
                over_d2d(t, 3 + j, 2 * cx + cy, c).wait_send()
                over_ici(t, j, (cx, cy)).wait_send()

    return pl.pallas_call(
        body, name=name, in_specs=[ANY] * n, out_specs=[ANY] * n,
        out_shape=[jax.ShapeDtypeStruct((N_CHIPS,) + s.shape, s.dtype) for s in shards],
        scratch_shapes=[pltpu.SemaphoreType.DMA((n, 7)), pltpu.SemaphoreType.DMA((n, 7))],
        compiler_params=SIDE_EFFECTS,
    )(*shards)


def _pair_split(grads, name):
    n = len(grads)

    def body(*refs):
        srcs, outs, (send, recv) = refs[:n], refs[n:2 * n], refs[2 * n:]
        x, y, c, _ = _place()
        for t in range(n):
            for s in range(N_CHIPS):
                _remote(_window(srcs[t], (s,), grads[t].shape[1:], 1 - c), outs[t].at[s], send.at[t], recv.at[t], (x, y, 1 - c)).start()
        for t in range(n):
            _remote(_window(srcs[t], (slice(None),), grads[t].shape[1:], 1 - c), outs[t], send.at[t], recv.at[t], (x, y, 1 - c)).wait()

    return pl.pallas_call(
        body, name=name, in_specs=[ANY] * n, out_specs=[ANY] * n,
        out_shape=[jax.ShapeDtypeStruct((N_CHIPS,) + _half_shape(g.shape[1:]), g.dtype) for g in grads],
        scratch_shapes=[pltpu.SemaphoreType.DMA((n,)), pltpu.SemaphoreType.DMA((n,))],
        compiler_params=SIDE_EFFECTS,
    )(*grads)


def _pair_join(totals, name):
    n = len(totals)

    def body(*refs):
        srcs, outs, (send, recv) = refs[:n], refs[n:2 * n], refs[2 * n:]
        x, y, c, _ = _place()
        for t in range(n):
            for a, b in zip(_window(srcs[t], (), totals[t].shape, None, True), _window(outs[t], (), totals[t].shape, None, True)):
                _remote(a, b, send.at[t], recv.at[t], (x, y, 1 - c)).start()
        for t in range(n):
            _remote(srcs[t], outs[t], send.at[t], recv.at[t], (x, y, 1 - c)).wait()

    return pl.pallas_call(
        body, name=name, in_specs=[ANY] * n, out_specs=[ANY] * n,
        out_shape=[jax.ShapeDtypeStruct(t.shape, t.dtype) for t in totals],
        scratch_shapes=[pltpu.SemaphoreType.DMA((n,)), pltpu.SemaphoreType.DMA((n,))],
        compiler_params=SIDE_EFFECTS,
    )(*totals)


HBM = pl.BlockSpec(memory_space=pltpu.HBM)
SEM = pl.BlockSpec(memory_space=pltpu.SEMAPHORE)
SPLIT = pltpu.CompilerParams(has_side_effects=pltpu.SideEffectType.DATAFLOW_SIDE_EFFECTING)


def _in_hbm(a):
    return pltpu.with_memory_space_constraint(a, pltpu.HBM)


def _split_copies(kind, srcs, lands):
    x, y, c, chips = _place()
    out = []
    for t in range(len(srcs)):
        if kind == "pair":
            out += [(t, s % 3, _window(srcs[t], (s,), srcs[t].shape[1:], 1 - c), lands[t].at[s], (x, y, 1 - c)) for s in range(N_CHIPS)]
            continue
        if kind == "forward":
            shape, sibling = srcs[t].shape, (x, y, 1 - c)
            out += [(t, 0, a, b, sibling) for a, b in zip(_window(srcs[t], (), shape, None, True), _window(lands[t], (2 * x + y,), shape, None, True))]
            out += [(t, j, a, a, sibling) for j, (cx, cy) in enumerate(chips) for a in _window(lands[t], (2 * cx + cy,), shape, c, True)]
            continue
        for j, (cx, cy) in enumerate(chips):
            if kind == "gather":
                shape = srcs[t].shape
                out.append((t, j, _window(srcs[t], (), shape, c), _window(lands[t], (2 * x + y,), shape, c), (cx, cy, c)))
            else:
                out.append((t, j, srcs[t].at[2 * cx + cy], lands[t].at[j], (cx, cy, c)))
    return out


def _split_start(kind, srcs, land_shapes, name, lands=None):
    n = len(srcs)

    def body(*refs):
        src_refs, land_refs, sems, token = refs[:n], refs[n:2 * n], refs[2 * n:2 * n + 6], refs[-1]
        for t, j, s, d, to in _split_copies(kind, src_refs, land_refs):
            _remote(s, d, sems[j], sems[3 + j], to).start()
        token[...] = jnp.zeros_like(token)

    lands = [_in_hbm(lax.empty(s.shape, s.dtype)) for s in land_shapes] if lands is None else list(lands)
    thru = [pltpu.HBM(a.shape, a.dtype) for a in list(srcs) + lands]
    res = pl.pallas_call(
        body, name=name,
        out_shape=(*[pltpu.SemaphoreType.DMA(())] * 6, *thru, jax.ShapeDtypeStruct((8, LANES), F32)),
        in_specs=[HBM] * (2 * n), out_specs=(*[SEM] * 6, *[HBM] * (2 * n), pl.BlockSpec(memory_space=pltpu.VMEM)),
        input_output_aliases={i: 6 + i for i in range(2 * n)}, compiler_params=SPLIT,
    )(*[_in_hbm(s) for s in srcs], *lands)
    return res[:6], res[6:6 + n], res[6 + n:6 + 2 * n], res[-1]


def _split_wait(kind, sems, srcs, lands, after, name):
    n = len(srcs)

    def body(*refs):
        src_refs, land_refs, sem_refs = refs[:n], refs[n:2 * n], refs[2 * n:2 * n + 6]
        for t, j, s, d, to in _split_copies(kind, src_refs, land_refs):
            cp = _remote(s, d, sem_refs[j], sem_refs[3 + j], to)
            cp.wait_send()
            cp.wait_recv()

    res = pl.pallas_call(
        body, name=name, out_shape=[pltpu.HBM(a.shape, a.dtype) for a in list(srcs) + list(lands)],
        in_specs=[HBM] * (2 * n) + [SEM] * 6 + [ANY], out_specs=[HBM] * (2 * n),
        input_output_aliases={i: i for i in range(2 * n)}, compiler_params=SPLIT,
    )(*srcs, *lands, *sems, after)
    return res[:n], res[n:]


def _sum_all_devices(vec, name):
    n_rows = vec.shape[0]

    def body(v_ref, out_ref, buf, send, recv):
        x, y, c, _ = _place()
        me = 4 * x + 2 * y + c
        buf[me] = v_ref[...]
        flips = [(a, b, d) for a in (0, 1) for b in (0, 1) for d in (0, 1)][1:]
        copies = []
        for r, (a, b, d) in enumerate(flips):
            px, py, pc = (1 - x if a else x), (1 - y if b else y), (1 - c if d else c)
            copies.append(pltpu.make_async_remote_copy(src_ref=v_ref, dst_ref=buf.at[me], send_sem=send.at[r], recv_sem=recv.at[r],
                                                       device_id=(px, py, pc), device_id_type=MESH))
            copies[-1].start()
        for r, (a, b, d) in enumerate(flips):
            px, py, pc = (1 - x if a else x), (1 - y if b else y), (1 - c if d else c)
            pltpu.make_async_remote_copy(src_ref=v_ref, dst_ref=buf.at[4 * px + 2 * py + pc], send_sem=send.at[r], recv_sem=recv.at[r],
                                         device_id=(px, py, pc), device_id_type=MESH).wait_recv()
        for cp in copies:
            cp.wait_send()
        total = buf[0]
        for k in range(1, N_DEV):
            total = total + buf[k]
        out_ref[...] = total

    vmem = pl.BlockSpec(memory_space=pltpu.VMEM)
    return pl.pallas_call(
        body, name=name, in_specs=[vmem], out_specs=vmem, out_shape=jax.ShapeDtypeStruct(vec.shape, F32),
        scratch_shapes=[pltpu.VMEM((N_DEV, n_rows, LANES), F32), pltpu.SemaphoreType.DMA((N_DEV - 1,)), pltpu.SemaphoreType.DMA((N_DEV - 1,))],
        compiler_params=pltpu.CompilerParams(has_side_effects=True),
    )(vec)


def _half_tile(half, width):
    t = half
    while t * width * 4 > (2 << 20) and t % 32 == 0:
        t //= 2
    return t


def _pair_add(g, theirs, core, name):
    _, half, width = theirs.shape
    t = _half_tile(half, width)
    n = half // t

    def body(c_ref, a_ref, b_ref, o_ref):
        o_ref[...] = (a_ref[...] + b_ref[...]).astype(BF16)

    tile = pl.BlockSpec((1, t, width), lambda j, i, c_ref: (j, i, 0))
    if _half_axis(g.shape[1:]) == 0:
        mine = pl.BlockSpec((1, t, width), lambda j, i, c_ref: (j, c_ref[0] * n + i, 0))
    else:
        mine = pl.BlockSpec((1, t, width), lambda j, i, c_ref: (j, i, c_ref[0]))
    return pl.pallas_call(
        body, name=name,
        grid_spec=pltpu.PrefetchScalarGridSpec(num_scalar_prefetch=1, grid=(N_CHIPS, n), in_specs=[mine, tile], out_specs=tile),
        out_shape=jax.ShapeDtypeStruct(theirs.shape, BF16), compiler_params=_params(("parallel", "parallel")),
    )(core, g, theirs)


def _chip_sum(part, others, chip, name, after=None):
    _, half, width = part.shape
    t = _half_tile(half, width)

    def body(s_ref, mine, p0, p1, p2, *rest):
        o_ref = rest[-1]
        o_ref[...] = ((mine[0].astype(F32) + p0[0].astype(F32)) + p1[0].astype(F32)) + p2[0].astype(F32)

    return pl.pallas_call(
        body, name=name,
        grid_spec=pltpu.PrefetchScalarGridSpec(
            num_scalar_prefetch=1, grid=(half // t,),
            in_specs=[pl.BlockSpec((1, t, width), lambda i, s_ref: (s_ref[0], i, 0))]
            + [pl.BlockSpec((1, t, width), lambda i, s_ref, j=j: (j, i, 0)) for j in range(3)] + [pl.BlockSpec(memory_space=pl.ANY)] * (after is not None),
            out_specs=pl.BlockSpec((t, width), lambda i, s_ref: (i, 0))),
        out_shape=jax.ShapeDtypeStruct((half, width), F32), compiler_params=_params(("parallel",)),
    )(chip, part, others, others, others, *([after] if after is not None else []))


EARLY = ("w_in", "w_uq", "w_ukv")
LATE = ("w_o_mla", "w_o_dil", "w_out", "w_ff1", "w_ff2")


def _chip_partials(grads, names, core, tag):
    gs = [grads[n] for n in names]
    theirs = _pair_split(gs, "pair_split_" + tag)
    return [_pair_add(g, th, core, "pair_add_" + n) for g, th, n in zip(gs, theirs, names)]


def _sum_small(vals):
    n_in = len(vals)
    n_rows = sum(a.shape[0] * a.shape[1] // LANES for a in vals)
    pad_rows = -(-n_rows // 8) * 8

    def chunks(refs):
        return [(ref, a, j) for ref in refs for a in range(ref.shape[0]) for j in range(ref.shape[1] // LANES)]

    def body(*refs):
        ins, outs, (buf, send, recv) = refs[:n_in], refs[n_in:2 * n_in], refs[2 * n_in:]
        x, y, c, _ = _place()
        me = 4 * x + 2 * y + c
        for r, (ref, a, j) in enumerate(chunks(ins)):
            buf[me, r:r + 1, :] = ref[a:a + 1, j * LANES:(j + 1) * LANES]
        if pad_rows > n_rows:
            buf[me, n_rows:pad_rows, :] = jnp.zeros((pad_rows - n_rows, LANES), F32)
        flips = [(a, b, d) for a in (0, 1) for b in (0, 1) for d in (0, 1)][1:]
        peers = [((1 - x if a else x), (1 - y if b else y), (1 - c if d else c)) for a, b, d in flips]
        copies = [_remote(buf.at[me], buf.at[me], send.at[r], recv.at[r], peer) for r, peer in enumerate(peers)]
        for cp in copies:
            cp.start()
        for r, (px, py, pc) in enumerate(peers):
            _remote(buf.at[me], buf.at[4 * px + 2 * py + pc], send.at[r], recv.at[r], (px, py, pc)).wait_recv()
        for cp in copies:
            cp.wait_send()
        total = buf[0]
        for k in range(1, N_DEV):
            total = total + buf[k]
        for r, (ref, a, j) in enumerate(chunks(outs)):
            ref[a:a + 1, j * LANES:(j + 1) * LANES] = total[r:r + 1, :]

    vmem = pl.BlockSpec(memory_space=pltpu.VMEM)
    return pl.pallas_call(
        body, name="sum_small", in_specs=[vmem] * n_in, out_specs=[vmem] * n_in,
        out_shape=[jax.ShapeDtypeStruct(a.shape, F32) for a in vals],
        scratch_shapes=[pltpu.VMEM((N_DEV, pad_rows, LANES), F32), pltpu.SemaphoreType.DMA((N_DEV - 1,)), pltpu.SemaphoreType.DMA((N_DEV - 1,))],
        compiler_params=SIDE_EFFECTS,
    )(*vals)


def _adam_math(w, g, m, v):
    nm = B1 * m + (1.0 - B1) * g
    nv = B2 * v + (1.0 - B2) * (g * g)
    m_hat = nm / (1.0 - B1 ** ADAM_STEP)
    v_hat = nv / (1.0 - B2 ** ADAM_STEP)
    return -LR * (m_hat / (jnp.sqrt(v_hat) + ADAM_EPS) + WD * w), nm, nv


def _adamw_big(w, mine, theirs, m, v, core, name, side_by_side=False):
    rows, width = w.shape
    if side_by_side:
        t = next(c for c in (152, 96, 64, 32, 16, 8) if rows % c == 0)
        hb = None
        half_spec = pl.BlockSpec((t, width // 2), lambda i, c_ref: (i, 0))
    else:
        t = next(c for c in (256, 128, 64, 32, 16, 8) if (rows // 2) % c == 0)
        hb = rows // 2 // t
        half_spec = pl.BlockSpec((t, width), lambda i, c_ref: (i % hb, 0))

    def body(c_ref, w_ref, a_ref, b_ref, m_ref, v_ref, g_ref, d_ref, nm_ref, nv_ref):
        south = c_ref[0] == 0
        if side_by_side:
            g = jnp.where(south, jnp.concatenate([a_ref[...], b_ref[...]], axis=1), jnp.concatenate([b_ref[...], a_ref[...]], axis=1))
        else:
            g = jnp.where((pl.program_id(0) < hb) == south, a_ref[...], b_ref[...])
        g_ref[...] = g
        d_ref[...], nm_ref[...], nv_ref[...] = _adam_math(w_ref[...], g, m_ref[...], v_ref[...])

    spec = pl.BlockSpec((t, width), lambda i, c_ref: (i, 0))
    return pl.pallas_call(
        body, name=name,
        grid_spec=pltpu.PrefetchScalarGridSpec(num_scalar_prefetch=1, grid=(rows // t,),
                                               in_specs=[spec, half_spec, half_spec, spec, spec], out_specs=[spec] * 4),
        out_shape=[jax.ShapeDtypeStruct(w.shape, F32)] * 4, compiler_params=_params(("parallel",)),
    )(core, w, mine, theirs, m, v)


def _adamw_small(ws, gs, ms, vs):
    n = len(ws)

    def body(*refs):
        for t in range(n):
            w_ref, g_ref, m_ref, v_ref = (refs[k * n + t] for k in range(4))
            d, nm, nv = _adam_math(w_ref[...], g_ref[...], m_ref[...], v_ref[...])
            refs[4 * n + t][...] = d
            refs[5 * n + t][...] = nm
            refs[6 * n + t][...] = nv

    vmem = pl.BlockSpec(memory_space=pltpu.VMEM)
    res = pl.pallas_call(body, name="adamw_small", in_specs=[vmem] * (4 * n), out_specs=[vmem] * (3 * n),
                         out_shape=[jax.ShapeDtypeStruct(a.shape, F32) for a in ws] * 3)(*ws, *gs, *ms, *vs)
    return res[:n], res[n:2 * n], res[2 * n:]


def kernel(x, w_in, b_gate, g_q_a, w_uq, g_kv_a, w_ukv, w_o_mla, w_o_dil, w_out, ln1_g, ln1_b, w_ff1, w_ff2, ln2_g, ln2_b, loss_target, m_w_in, m_b_gate, m_g_q_a, m_w_uq, m_g_kv_a, m_w_ukv, m_w_o_mla, m_w_o_dil, m_w_out, m_ln1_g, m_ln1_b, m_w_ff1, m_w_ff2, m_ln2_g, m_ln2_b, v_w_in, v_b_gate, v_g_q_a, v_w_uq, v_g_kv_a, v_w_ukv, v_w_o_mla, v_w_o_dil, v_w_out, v_ln1_g, v_ln1_b, v_w_ff1, v_w_ff2, v_ln2_g, v_ln2_b):
    order = ("w_in", "b_gate", "g_q_a", "w_uq", "g_kv_a", "w_ukv", "w_o_mla", "w_o_dil", "w_out", "ln1_g", "ln1_b", "w_ff1", "w_ff2", "ln2_g", "ln2_b")
    w = dict(w_in=w_in, b_gate=b_gate, g_q_a=g_q_a, w_uq=w_uq, g_kv_a=g_kv_a, w_ukv=w_ukv, w_o_mla=w_o_mla, w_o_dil=w_o_dil, w_out=w_out,
             ln1_g=ln1_g, ln1_b=ln1_b, w_ff1=w_ff1, w_ff2=w_ff2, ln2_g=ln2_g, ln2_b=ln2_b)
    m = dict(w_in=m_w_in, b_gate=m_b_gate, g_q_a=m_g_q_a, w_uq=m_w_uq, g_kv_a=m_g_kv_a, w_ukv=m_w_ukv, w_o_mla=m_w_o_mla, w_o_dil=m_w_o_dil,
             w_out=m_w_out, ln1_g=m_ln1_g, ln1_b=m_ln1_b, w_ff1=m_w_ff1, w_ff2=m_w_ff2, ln2_g=m_ln2_g, ln2_b=m_ln2_b)
    v = dict(w_in=v_w_in, b_gate=v_b_gate, g_q_a=v_g_q_a, w_uq=v_w_uq, g_kv_a=v_g_kv_a, w_ukv=v_w_ukv, w_o_mla=v_w_o_mla, w_o_dil=v_w_o_dil,
             w_out=v_w_out, ln1_g=v_ln1_g, ln1_b=v_ln1_b, w_ff1=v_w_ff1, w_ff2=v_w_ff2, ln2_g=v_ln2_g, ln2_b=v_ln2_b)
    chip = 2 * lax.axis_index("x") + lax.axis_index("y")
    south = (lax.axis_index("c") == 0).astype(F32)
    gate_w = D_MODEL // N_CHIPS

    core = lax.axis_index("c").astype(jnp.int32).reshape(1)
    turn = lambda n, a: a.T if n in TRANSPOSED else a
    shards = {n: turn(n, w[n][0]).astype(BF16) for n in BIG}
    first = dict(zip(EARLY, _gather_weights([shards[n] for n in EARLY], "gather_early")))
    late_shards = [shards[n] for n in LATE]
    g_sems, g_srcs, g_lands, g_token = _split_start(
        "gather", late_shards, [jax.ShapeDtypeStruct((N_CHIPS,) + s.shape, BF16) for s in late_shards], "gather_late_start")
    b_mine = lax.dynamic_update_slice(jnp.zeros((2, D_MODEL), F32), b_gate[0] * south, (0, chip * gate_w))
    b_full = _sum_all_devices(b_mine.reshape(-1, LANES), "gather_b_gate").reshape(2, D_MODEL)

    sent = {}

    def late_arrived(after):
        srcs, lands = _split_wait("gather", g_sems, g_srcs, g_lands, after, "gather_late_wait")
        sent["forward"] = _split_start("forward", srcs, None, "gather_late_forward_start", lands=lands)
        return sent["forward"][-1]

    def late_weights(after):
        return dict(zip(LATE, _split_wait("forward", *sent["forward"][:3], after, "gather_late_forward_wait")[1]))

    exchange_shapes = lambda parts: [jax.ShapeDtypeStruct((3,) + p.shape[1:], BF16) for p in parts]

    def early_grads(grads_late):
        gs = [grads_late[n] for n in LATE]
        shapes = [jax.ShapeDtypeStruct((N_CHIPS,) + _half_shape(g.shape[1:]), F32) for g in gs]
        sent["pair"] = _split_start("pair", gs, shapes, "pair_split_late_start")
        return sent["pair"][-1]

    def early_grads_go(after):
        gs, theirs = _split_wait("pair", *sent["pair"][:3], after, "pair_split_late_wait")
        parts = [_pair_add(g, th, core, "pair_add_" + n) for g, th, n in zip(gs, theirs, LATE)]
        sent["late"] = _split_start("scatter", parts, exchange_shapes(parts), "exchange_late_start")
        return sent["late"][-1]

    def last_grads(grads_early):
        parts = _chip_partials(grads_early, EARLY, core, "early")
        sent["early"] = _split_start("scatter", parts, exchange_shapes(parts), "exchange_early_start")
        return sent["early"][-1]

    loss_part, grad_x, grads = _local_step(x, loss_target, first, b_full, g_q_a, g_kv_a, ln1_g, ln1_b, ln2_g, ln2_b, token=g_token,
                                           late_arrived=late_arrived, late_weights=late_weights, early_grads=early_grads, early_grads_go=early_grads_go,
                                           last_grads=last_grads)

    g_out, delta, new_m, new_v = {}, {}, {}, {}
    chip1 = chip.astype(jnp.int32).reshape(1)

    def finish(names, parts, others, tag):
        totals = [_chip_sum(p, o, chip1, "chip_sum_" + n) for n, p, o in zip(names, parts, others)]
        for n, mine, theirs in zip(names, totals, _pair_join(totals, "pair_join_" + tag)):
            res = _adamw_big(turn(n, w[n][0]), mine, theirs, turn(n, m[n][0]), turn(n, v[n][0]), core, "adamw_" + n,
                             side_by_side=mine.shape[0] == shards[n].shape[0])
            g_out[n], delta[n], new_m[n], new_v[n] = (turn(n, r) for r in res)

    finish(LATE, *_split_wait("scatter", *sent["late"][:3], grad_x, "exchange_late_wait"), "late")
    small_names = [name for name, _ in SMALL]
    sums = _sum_small([grads[name] for name in small_names] + [loss_part])
    loss = sums[-1][0, 0]
    g_small = dict(zip(small_names, sums))
    g_small["b_gate"] = lax.dynamic_slice(g_small["b_gate"], (0, chip * gate_w), (2, gate_w))
    flat = lambda a: a.reshape(-1, a.shape[-1])
    res = _adamw_small(*[[flat(d[name]) for name in small_names] for d in (w, g_small, m, v)])
    g_out.update(g_small)
    for d, r in zip((delta, new_m, new_v), res):
        d.update(zip(small_names, r))
    done = res[0][0][0:1, 0:1]
    for n in LATE:
        done = done + delta[n][0:1, 0:1]
    finish(EARLY, *_split_wait("scatter", *sent["early"][:3], done, "exchange_early_wait"), "early")


    lead = lambda d: [d[name].reshape(w[name].shape) for name in order]
    return (loss, grad_x, *lead(g_out), *lead(delta), *lead(new_m), *lead(new_v))
```

```python
import functools
import math

import jax
import jax.numpy as jnp
from jax import lax
from jax.experimental import pallas as pl
from jax.experimental.pallas import tpu as pltpu

F32 = jnp.float32
BF16 = jnp.bfloat16
MESH = pl.DeviceIdType.MESH

D_MODEL = 1024
N_HEADS = 8
LANES = 128
NOPE, ROPE, V_DIM = 64, 32, 64
MLA_QK = NOPE + ROPE
Q_LORA, KV_LORA = 384, 256
DIL_DIM = 64
DIL_PATTERNS = ((128, 1), (512, 4), (2048, 16))
D_FF = 4096
N_CHIPS = 4
N_DEV = 8
IN_WIDTH = 4256
LN_EPS, RMS_EPS = 1e-5, 1e-6
NEG = -1e30
LOG2E, LN2 = 1.4426950408889634, 0.6931471805599453
ALPHA = 2.0 ** 0.25
ROPE_THETA = 10000.0
LR, B1, B2, ADAM_EPS, WD, ADAM_STEP = 0.001, 0.9, 0.999, 1e-8, 0.01, 10

P_LORA, P_KR, P_GATE, P_HALF = 0, 640, 1024, 3072
P_QD, P_KD, P_VD = 0, 1024, 2048
LORA_W = Q_LORA + KV_LORA
KR_LANE = NOPE

ATT_T = 512
ROW_T = 512
VMEM_LIMIT = 56 * 1024 * 1024

NN = (((1,), (0,)), ((), ()))
NT = (((1,), (1,)), ((), ()))
TN = (((0,), (0,)), ((), ()))


def _params(sem=None, **kw):
    return pltpu.CompilerParams(dimension_semantics=sem, vmem_limit_bytes=VMEM_LIMIT, **kw)


def _matmul(a, b, *, mode, name, tm, tn, tk, out_dtypes=(F32,), extras=(), epilogue=None, b_shards=False, out_shards=False, after=None):
    pieces = list(a) if isinstance(a, (list, tuple)) else [a]
    n_pc = len(pieces)
    a_shape = (pieces[0].shape[0], sum(p.shape[1] for p in pieces))
    if b_shards:
        n_sh, rows_b, cols_b = b.shape
        b_shape = (rows_b, n_sh * cols_b)
    else:
        b_shape = b.shape
    if mode == "nn":
        (m, k), (k2, n) = a_shape, b_shape
    elif mode == "nt":
        (m, k), (n, k2) = a_shape, b_shape
    else:
        (k, m), (k2, n) = a_shape, b_shape
    assert k == k2, (a_shape, b.shape, mode)
    tm, tn, tk = min(tm, m), min(tn, n), min(tk, k)
    assert m % tm == 0 and n % tn == 0 and k % tk == 0, (name, m, n, k, tm, tn, tk)
    nk = k // tk
    n_ex, n_out = len(extras), len(out_dtypes)
    n_in = n_pc + 1 + n_ex + (after is not None)
    dims = {"nn": NN, "nt": NT, "tn": TN}[mode]
    col_tile = tm if mode == "tn" else tk
    blocks = [p.shape[1] // col_tile for p in pieces]
    firsts = [sum(blocks[:p]) for p in range(n_pc)]
    assert all(p.shape[1] % col_tile == 0 for p in pieces), (name, col_tile)

    def body(*refs):
        a_refs, b_ref = refs[:n_pc], refs[n_pc]
        ex_refs = refs[n_pc + 1:n_pc + 1 + n_ex]
        out_refs = refs[n_in:n_in + n_out]

        def finish(acc):
            outs = epilogue(acc, *[r[...] for r in ex_refs]) if epilogue is not None else (acc,)
            for r, o in zip(out_refs, outs):
                r[...] = o.astype(r.dtype)

        kk = pl.program_id(2)

        def step(a_ref):
            part = lax.dot_general(a_ref[...].astype(BF16), b_ref[...].astype(BF16), dims, preferred_element_type=F32)
            if nk == 1:
                finish(part)
                return
            acc_ref = refs[-1]

            @pl.when(kk == 0)
            def _():
                acc_ref[...] = part

            @pl.when(kk > 0)
            def _():
                acc_ref[...] += part

            @pl.when(kk == nk - 1)
            def _():
                finish(acc_ref[...])

        if n_pc == 1:
            step(a_refs[0])
        else:
            at = pl.program_id(0) if mode == "tn" else kk
            for p in range(n_pc):
                pl.when(jnp.logical_and(at >= firsts[p], at < firsts[p] + blocks[p]))(functools.partial(step, a_refs[p]))

    def a_spec_of(p):
        if n_pc == 1:
            return pl.BlockSpec((tk, tm), lambda i, j, kk: (kk, i)) if mode == "tn" else pl.BlockSpec((tm, tk), lambda i, j, kk: (i, kk))
        col = lambda at: jnp.clip(at - firsts[p], 0, blocks[p] - 1)
        mine = lambda at: jnp.logical_and(at >= firsts[p], at < firsts[p] + blocks[p])
        if mode == "tn":
            return pl.BlockSpec((tk, tm), lambda i, j, kk: (jnp.where(mine(i), kk, 0), col(i)))
        return pl.BlockSpec((tm, tk), lambda i, j, kk: (i, col(kk)))

    b_spec = {"nn": pl.BlockSpec((tk, tn), lambda i, j, kk: (kk, j)),
              "nt": pl.BlockSpec((tn, tk), lambda i, j, kk: (j, kk)),
              "tn": pl.BlockSpec((tk, tn), lambda i, j, kk: (kk, j))}[mode]
    tile = pl.BlockSpec((tm, tn), lambda i, j, kk: (i, j))
    out_spec, out_dims = tile, (m, n)
    if b_shards and mode == "nn":
        per = cols_b // tn
        b_spec = pl.BlockSpec((None, tk, tn), lambda i, j, kk: (j // per, kk, j % per))
    elif b_shards:
        assert mode == "nt"
        per = cols_b // tk
        b_spec = pl.BlockSpec((None, tn, tk), lambda i, j, kk: (kk // per, j, kk % per))
    if out_shards:
        assert not extras and epilogue is None
        per_out = n // N_CHIPS // tn
        out_spec = pl.BlockSpec((None, tm, tn), lambda i, j, kk: (j // per_out, i, j % per_out))
        out_dims = (N_CHIPS, m, n // N_CHIPS)
    outs = pl.pallas_call(
        body, name=name,
        grid=(m // tm, n // tn, nk),
        in_specs=[a_spec_of(p) for p in range(n_pc)] + [b_spec] + [tile] * n_ex + [pl.BlockSpec(memory_space=pl.ANY)] * (after is not None),
        out_specs=[out_spec] * n_out,
        out_shape=[jax.ShapeDtypeStruct(out_dims, dt) for dt in out_dtypes],
        scratch_shapes=[pltpu.VMEM((tm, tn), F32)] if nk > 1 else [],
        compiler_params=_params(("parallel", "parallel", "arbitrary")),
    )(*pieces, b, *extras, *([after] if after is not None else []))
    return outs[0] if n_out == 1 else outs


def _rowwise(fn, *, name, rows, seq, ins, outs, sums=()):
    tm = min(ROW_T, seq)
    n_pos = seq // tm
    n_in, n_out, n_sum = len(ins), len(outs), len(sums)

    def body(*refs):
        vals = fn(*[r[...] for r in refs[:n_in]])
        for r, v in zip(refs[n_in:n_in + n_out], vals[:n_out]):
            r[...] = v.astype(r.dtype)
        first = pl.program_id(0) == 0
        for r, v in zip(refs[n_in + n_out:], vals[n_out:]):
            @pl.when(first)
            def _(r=r, v=v):
                r[...] = v

            @pl.when(jnp.logical_not(first))
            def _(r=r, v=v):
                r[...] += v

    def spec(arr, width, col, kind):
        if kind == "row":
            return pl.BlockSpec((tm, width), lambda i, col=col: (i, col))
        if kind == "pos":
            return pl.BlockSpec((tm, width), lambda i, col=col: (i % n_pos, col))
        return pl.BlockSpec(arr.shape, lambda i: (0,) * arr.ndim)

    res = pl.pallas_call(
        body, name=name,
        grid=(rows // tm,),
        in_specs=[spec(*t) for t in ins],
        out_specs=[pl.BlockSpec((tm, w), lambda i: (i, 0)) for w, _ in outs]
        + [pl.BlockSpec((1, w), lambda i: (0, 0)) for w in sums],
        out_shape=[jax.ShapeDtypeStruct((rows, w), dt) for w, dt in outs]
        + [jax.ShapeDtypeStruct((1, w), F32) for w in sums],
        compiler_params=_params(("arbitrary",)),
    )(*[t[0] for t in ins])
    return res


def _colsum(v):
    return jnp.sum(v, axis=0, keepdims=True)


def _rope_fwd(t, c, s_up, s_dn):
    return t * c + pltpu.roll(t, LANES - 16, 1) * s_up + pltpu.roll(t, 16, 1) * s_dn


def _rope_bwd(d, c, s_up, s_dn):
    return d * c + pltpu.roll(d * s_up, 16, 1) + pltpu.roll(d * s_dn, LANES - 16, 1)


def _rope_tables(seq):
    half = ROPE // 2
    inv = jnp.power(ROPE_THETA, -jnp.arange(half, dtype=F32) / half)
    ang = jnp.arange(seq, dtype=F32)[:, None] * inv[None, :]
    cos, sin = jnp.cos(ang), jnp.sin(ang)
    zeros = jnp.zeros((seq, half), F32)
    lo, hi = jnp.ones((seq, KR_LANE), F32), jnp.ones((seq, LANES - KR_LANE - ROPE), F32)
    c = jnp.concatenate([lo, cos, cos, hi], axis=1)
    c_rope_only = jnp.concatenate([0 * lo, cos, cos, 0 * hi], axis=1)
    s_up = jnp.concatenate([0 * lo, -sin, zeros, 0 * hi], axis=1)
    s_dn = jnp.concatenate([0 * lo, zeros, sin, 0 * hi], axis=1)
    return c, s_up, s_dn, c_rope_only


def _rms(x, g):
    r = lax.rsqrt(jnp.mean(x * x, axis=1, keepdims=True) + RMS_EPS)
    return x * r * g


def _rms_bwd(x, g, dy):
    r = lax.rsqrt(jnp.mean(x * x, axis=1, keepdims=True) + RMS_EPS)
    xh = x * r
    dxh = dy * g
    dx = r * (dxh - xh * jnp.mean(dxh * xh, axis=1, keepdims=True))
    return dx, _colsum(dy * xh)


def _ln_stats(x):
    mu = jnp.mean(x, axis=1, keepdims=True)
    xc = x - mu
    r = lax.rsqrt(jnp.mean(xc * xc, axis=1, keepdims=True) + LN_EPS)
    return xc * r, r


def _ln_bwd(xh, r, g, dy):
    dxh = dy * g
    dx = r * (dxh - jnp.mean(dxh, axis=1, keepdims=True) - xh * jnp.mean(dxh * xh, axis=1, keepdims=True))
    return dx, _colsum(dy * xh), _colsum(dy)


def _table_specs(tables):
    whole = lambda a: pl.BlockSpec(a.shape, lambda b, h: (0,) * a.ndim)
    if len(tables) == 1:
        return [whole(tables[0])]
    return [whole(tables[0]), whole(tables[1]), pl.BlockSpec((None, 1, LANES), lambda b, h: (h, 0, 0))]


def _biased(s, table_refs, delta):
    if delta < table_refs[0].shape[0]:
        s = s + table_refs[0][delta]
    if len(table_refs) == 3:
        s = s - table_refs[2][0:1, 0:1] * table_refs[1][delta]
    return s


def _attn_fwd(q, qb0, k, kb0, v, vb0, tables, scale, *, name, batch, seq, after=None):
    t = ATT_T
    nq = seq // t
    rows = batch * seq
    n_tab = len(tables)

    def body(q_ref, k_ref, v_ref, *rest):
        table_refs = rest[:n_tab]
        o_ref, lse_ref, qb, kb, vtb = rest[n_tab + (after is not None):]
        qb[...] = q_ref[...].astype(BF16)
        kb[...] = k_ref[...].astype(BF16)
        vtb[...] = v_ref[...].astype(F32).T.astype(BF16)
        for i in range(nq):
            qt = qb[i * t:(i + 1) * t, :]
            logits = [_biased(lax.dot_general(kb[j * t:(j + 1) * t, :], qt, NT, preferred_element_type=F32) * (scale * LOG2E), table_refs, i - j)
                      for j in range(i + 1)]
            m = jnp.max(functools.reduce(jnp.maximum, logits), axis=0, keepdims=True)
            ps = [jnp.exp2(s - m) for s in logits]
            l = jnp.sum(functools.reduce(jnp.add, ps), axis=0, keepdims=True)
            acc = functools.reduce(jnp.add, [lax.dot_general(vtb[:, j * t:(j + 1) * t], p.astype(BF16), NN, preferred_element_type=F32)
                                             for j, p in enumerate(ps)])
            o_ref[i * t:(i + 1) * t, :] = (acc / l).T
            lse_ref[i * t:(i + 1) * t, :] = jnp.broadcast_to((m + jnp.log2(l)) * LN2, (LANES, t)).T

    slab = lambda b0: pl.BlockSpec((seq, LANES), lambda b, h, b0=b0: (b, b0 + h))
    return pl.pallas_call(
        body, name=name,
        grid=(batch, N_HEADS),
        in_specs=[slab(qb0), slab(kb0), slab(vb0)] + _table_specs(tables) + [pl.BlockSpec(memory_space=pl.ANY)] * (after is not None),
        out_specs=[slab(0), slab(0)],
        out_shape=[jax.ShapeDtypeStruct((rows, N_HEADS * LANES), F32)] * 2,
        scratch_shapes=[pltpu.VMEM((seq, LANES), BF16)] * 2 + [pltpu.VMEM((LANES, seq), BF16)],
        compiler_params=_params(("arbitrary", "arbitrary")),
    )(q, k, v, *tables, *([after] if after is not None else []))


def _attn_bwd(q, qb0, k, kb0, v, vb0, o, do, lse, tables, scale, *, name, batch, seq, out_dtype, after=None):
    t = ATT_T
    nq = seq // t
    rows = batch * seq
    n_tab = len(tables)

    def body(q_ref, k_ref, v_ref, o_ref, do_ref, lse_ref, *rest):
        table_refs = rest[:n_tab]
        dq_ref, dk_ref, dv_ref, qb, kb, vb, dob, qtb, dotb, dka, dva = rest[n_tab + (after is not None):]
        qb[...] = q_ref[...].astype(BF16)
        kb[...] = k_ref[...].astype(BF16)
        vb[...] = v_ref[...].astype(BF16)
        dob[...] = do_ref[...].astype(BF16)
        qtb[...] = q_ref[...].astype(F32).T.astype(BF16)
        dotb[...] = do_ref[...].T.astype(BF16)
        for i in range(nq):
            at = slice(i * t, (i + 1) * t)
            qt, dot = qb[at, :], dob[at, :]
            lse_t = lse_ref[at, 0:1] * LOG2E
            delta = jnp.sum(o_ref[at, :] * do_ref[at, :], axis=1, keepdims=True)
            dq = None
            for j in range(i + 1):
                kat = slice(j * t, (j + 1) * t)
                kt, vt = kb[kat, :], vb[kat, :]
                p = jnp.exp2(_biased(lax.dot_general(qt, kt, NT, preferred_element_type=F32) * (scale * LOG2E), table_refs, i - j) - lse_t)
                dp = lax.dot_general(dot, vt, NT, preferred_element_type=F32)
                ds = (p * (dp - delta) * scale).astype(BF16)
                dk_part = lax.dot_general(qtb[:, at], ds, NN, preferred_element_type=F32)
                dv_part = lax.dot_general(dotb[:, at], p.astype(BF16), NN, preferred_element_type=F32)
                if i == j:
                    dka[:, kat] = dk_part
                    dva[:, kat] = dv_part
                else:
                    dka[:, kat] += dk_part
                    dva[:, kat] += dv_part
                dq_part = lax.dot_general(ds, kt, NN, preferred_element_type=F32)
                dq = dq_part if dq is None else dq + dq_part
            dq_ref[at, :] = dq.astype(dq_ref.dtype)
        dk_ref[...] = dka[...].T.astype(dk_ref.dtype)
        dv_ref[...] = dva[...].T.astype(dv_ref.dtype)

    slab = lambda b0: pl.BlockSpec((seq, LANES), lambda b, h, b0=b0: (b, b0 + h))
    return pl.pallas_call(
        body, name=name,
        grid=(batch, N_HEADS),
        in_specs=[slab(qb0), slab(kb0), slab(vb0), slab(0), slab(0), slab(0)] + _table_specs(tables)
        + [pl.BlockSpec(memory_space=pl.ANY)] * (after is not None),
        out_specs=[slab(0)] * 3,
        out_shape=[jax.ShapeDtypeStruct((rows, N_HEADS * LANES), out_dtype)] * 3,
        scratch_shapes=[pltpu.VMEM((seq, LANES), BF16)] * 4 + [pltpu.VMEM((LANES, seq), BF16)] * 2 + [pltpu.VMEM((LANES, seq), F32)] * 2,
        compiler_params=_params(("arbitrary", "arbitrary")),
    )(q, k, v, o, do, lse, *tables, *([after] if after is not None else []))


def _attention_tables(seq):
    n = seq // ATT_T
    pos = jnp.arange(ATT_T, dtype=jnp.int32)
    dist = jnp.arange(n, dtype=jnp.int32)[:, None, None] * ATT_T + pos[None, :, None] - pos[None, None, :]
    causal = jnp.where(dist[:1] >= 0, 0.0, NEG).astype(F32)
    count = jnp.zeros(dist.shape, F32)
    for window, dilation in DIL_PATTERNS:
        count += ((dist >= 0) & (dist <= window) & (dist % dilation == 0)).astype(F32)
    held = jnp.where(count > 0, jnp.log2(jnp.maximum(count, 1.0)), NEG).astype(F32)
    slopes = jnp.asarray([2.0 ** (-8.0 * (i + 1) / N_HEADS) for i in range(N_HEADS)], F32)
    slopes = jnp.broadcast_to(slopes[:, None, None], (N_HEADS, 1, LANES))
    turned = lambda a: jnp.swapaxes(a, 1, 2)
    far = dist.astype(F32) * LOG2E
    return ((causal,), (turned(causal),)), ((held, far, slopes), (turned(held), turned(far), slopes))


def _pad_heads(w, width):
    kdim, n = w.shape[0], w.shape[1] // width
    return jnp.pad(w.reshape(kdim, n, width), ((0, 0), (0, 0), (0, LANES - width))).reshape(kdim, n * LANES)


def _unpad_heads(w, width):
    kdim, n = w.shape[0], w.shape[1] // LANES
    return w.reshape(kdim, n, LANES)[:, :, :width].reshape(kdim, n * width)


def _pad_head_rows(w, width):
    n, kdim = w.shape[0] // width, w.shape[1]
    return jnp.pad(w.reshape(n, width, kdim), ((0, 0), (0, LANES - width), (0, 0))).reshape(n * LANES, kdim)


def _unpad_head_rows(w, width):
    n, kdim = w.shape[0] // LANES, w.shape[1]
    return w.reshape(n, LANES, kdim)[:, :width].reshape(n * width, kdim)


def _pad_w_in_t(wt):
    n_qkv = 3 * N_HEADS * DIL_DIM
    zeros = lambda n: jnp.zeros((n, wt.shape[1]), wt.dtype)
    main = jnp.concatenate([wt[:LORA_W], zeros(KR_LANE), wt[LORA_W:LORA_W + ROPE], zeros(P_GATE - P_KR - KR_LANE - ROPE),
                            wt[LORA_W + ROPE + n_qkv:]], axis=0)
    return main, _pad_head_rows(wt[LORA_W + ROPE:LORA_W + ROPE + n_qkv], DIL_DIM)


def _unpad_w_in_t(gt):
    return jnp.concatenate([gt[P_LORA:P_KR], gt[P_KR + KR_LANE:P_KR + KR_LANE + ROPE], _unpad_head_rows(gt[P_HALF:], DIL_DIM),
                            gt[P_GATE:P_HALF]], axis=0)


def _split_ukv(w):
    w3 = w.reshape(w.shape[0], N_HEADS, NOPE + V_DIM)
    return (_pad_heads(w3[:, :, :NOPE].reshape(w.shape[0], -1), NOPE),
            _pad_heads(w3[:, :, NOPE:].reshape(w.shape[0], -1), V_DIM))


def _merge_ukv(g_k, g_v):
    kdim = g_k.shape[0]
    k3 = _unpad_heads(g_k, NOPE).reshape(kdim, N_HEADS, NOPE)
    v3 = _unpad_heads(g_v, V_DIM).reshape(kdim, N_HEADS, V_DIM)
    return jnp.concatenate([k3, v3], axis=2).reshape(kdim, N_HEADS * (NOPE + V_DIM))


def _pad_rows(w, width):
    return _pad_heads(w.T, width).T


def _unpad_rows(g, width):
    return _unpad_heads(g.T, width).T


def _join_cols(w):
    return w.transpose(1, 0, 2).reshape(w.shape[1], N_CHIPS * w.shape[2])


def _split_cols(g):
    return g.reshape(g.shape[0], N_CHIPS, g.shape[1] // N_CHIPS).transpose(1, 0, 2)


def _local_step(x3, target3, wg, b_gate, g_q_a, g_kv_a, ln1_g, ln1_b, ln2_g, ln2_b, token=None, late_arrived=None, late_weights=None,
                early_grads=None, early_grads_go=None, last_grads=None):
    w_main_t, w_dil_t = _pad_w_in_t(wg["w_in"].reshape(IN_WIDTH, D_MODEL))
    w_uq_pt = _pad_head_rows(wg["w_uq"].reshape(N_HEADS * MLA_QK, Q_LORA), MLA_QK)
    w_ukv = _join_cols(wg["w_ukv"])
    batch, seq, _ = x3.shape
    rows = batch * seq
    x = x3.reshape(rows, D_MODEL)
    target = target3.reshape(rows, D_MODEL)
    row = functools.partial(_rowwise, rows=rows, seq=seq)
    mm = _matmul

    w_uk_p, w_uv_p = _split_ukv(w_ukv)
    b0, b1 = b_gate[0:1], b_gate[1:2]
    rope_c, rope_up, rope_dn, rope_c_only = _rope_tables(seq)
    (mla_bwd_tables, mla_fwd_tables), (dil_bwd_tables, dil_fwd_tables) = _attention_tables(seq)
    scale_mla, scale_dil = MLA_QK ** -0.5, DIL_DIM ** -0.5
    qd0, kd0, vd0, lora0, kr0, gate0 = P_QD // LANES, P_KD // LANES, P_VD // LANES, P_LORA // LORA_W, P_KR // LANES, P_GATE // D_MODEL

    proj = mm(x, w_main_t, mode="nt", name="proj", tm=1024, tn=1536, tk=1024, after=token)
    proj_d = mm(x, w_dil_t, mode="nt", name="proj_dil", tm=1024, tn=1536, tk=1024, out_dtypes=(BF16,))

    def prep(lora, gq, gkv):
        return _rms(lora[:, :Q_LORA], gq), _rms(lora[:, Q_LORA:], gkv)

    qn, kvn = row(prep, name="mla_rms", ins=[(proj, LORA_W, lora0, "row"), (g_q_a, 0, 0, "full"), (g_kv_a, 0, 0, "full")],
                  outs=[(Q_LORA, BF16), (KV_LORA, BF16)])
    q_lin = mm(qn, w_uq_pt, mode="nt", name="q_up", tm=1024, tn=1024, tk=Q_LORA)
    k_lin = mm(kvn, w_uk_p, mode="nn", name="k_up", tm=1024, tn=1024, tk=KV_LORA)
    v_a = mm(kvn, w_uv_p, mode="nn", name="v_up", tm=1024, tn=1024, tk=KV_LORA, out_dtypes=(BF16,))

    def rope_qk(ql, kl, kr, c, up, dn):
        k_rot = _rope_fwd(kr, c, up, dn)
        qs = [_rope_fwd(ql[:, h * LANES:(h + 1) * LANES], c, up, dn) for h in range(N_HEADS)]
        ks = [kl[:, h * LANES:(h + 1) * LANES] + k_rot for h in range(N_HEADS)]
        return jnp.concatenate(qs, axis=1), jnp.concatenate(ks, axis=1)

    pos = lambda tab: (tab, LANES, 0, "pos")
    q_a, k_a = row(rope_qk, name="rope_qk",
                   ins=[(q_lin, D_MODEL, 0, "row"), (k_lin, D_MODEL, 0, "row"), (proj, LANES, kr0, "row"), pos(rope_c), pos(rope_up), pos(rope_dn)],
                   outs=[(N_HEADS * LANES, BF16), (N_HEADS * LANES, BF16)])
    o_a, lse_a = _attn_fwd(q_a, 0, k_a, 0, v_a, 0, mla_fwd_tables, scale_mla, name="mla_fwd", batch=batch, seq=seq)
    arrived = None if late_arrived is None else late_arrived(o_a)
    o_b, lse_b = _attn_fwd(proj_d, qd0, proj_d, kd0, proj_d, vd0, dil_fwd_tables, scale_dil, name="dil_fwd", batch=batch, seq=seq, after=arrived)
    late = wg if late_weights is None else late_weights(o_b)
    w_oa_p = _pad_rows(_join_cols(late["w_o_mla"]), V_DIM)
    w_ob_p = _pad_rows(_join_cols(late["w_o_dil"]), DIL_DIM)
    w_out, w_ff1, w_ff2 = late["w_out"].reshape(D_MODEL, D_MODEL), late["w_ff1"], late["w_ff2"].reshape(D_FF, D_MODEL)
    y_a = mm(o_a, w_oa_p, mode="nn", name="o_mla", tm=1024, tn=1024, tk=1024)
    y_b = mm(o_b, w_ob_p, mode="nn", name="o_dil", tm=1024, tn=1024, tk=1024)

    def gate(t0, t1, c0, c1, ya, yb):
        return (jax.nn.sigmoid(t0 + c0) * ya + jax.nn.sigmoid(t1 + c1) * yb,)

    gate_ins = [(proj, D_MODEL, gate0, "row"), (proj, D_MODEL, gate0 + 1, "row"), (b0, 0, 0, "full"), (b1, 0, 0, "full")]
    (u,) = row(gate, name="gate", ins=gate_ins + [(y_a, D_MODEL, 0, "row"), (y_b, D_MODEL, 0, "row")], outs=[(D_MODEL, BF16)])
    mixed = mm(u, w_out, mode="nn", name="mix", tm=1024, tn=1024, tk=1024)

    def ln1(xv, mv, g, b):
        r1 = ALPHA * xv + mv
        xh, _ = _ln_stats(r1)
        return r1, xh * g + b

    r1, h = row(ln1, name="ln1", ins=[(x, D_MODEL, 0, "row"), (mixed, D_MODEL, 0, "row"), (ln1_g, 0, 0, "full"), (ln1_b, 0, 0, "full")],
                outs=[(D_MODEL, F32), (D_MODEL, F32)])

    def relu2(acc):
        r = jnp.maximum(acc, 0.0)
        return acc, r * r

    a_ff, z = mm(h, w_ff1, mode="nn", name="ff1", tm=1024, tn=1024, tk=1024, out_dtypes=(BF16, BF16), epilogue=relu2, b_shards=True)
    f = mm(z, w_ff2, mode="nn", name="ff2", tm=1024, tn=1024, tk=2048)

    def ln2_loss(hv, fv, tv, g, b):
        xh, r = _ln_stats(ALPHA * hv + fv)
        err = xh * g + b - tv
        dy = err * (1.0 / D_MODEL)
        dr2, dg, db = _ln_bwd(xh, r, g, dy)
        loss = jnp.sum(_colsum(err * err), axis=1, keepdims=True) * (0.5 / D_MODEL)
        return dr2, jnp.broadcast_to(loss, (1, LANES)), dg, db

    dr2, loss_l, d_ln2_g, d_ln2_b = row(
        ln2_loss, name="ln2_loss",
        ins=[(h, D_MODEL, 0, "row"), (f, D_MODEL, 0, "row"), (target, D_MODEL, 0, "row"), (ln2_g, 0, 0, "full"), (ln2_b, 0, 0, "full")],
        outs=[(D_MODEL, F32)], sums=[LANES, D_MODEL, D_MODEL])

    d_w_ff2 = mm(z, dr2, mode="tn", name="d_w_ff2", tm=1024, tn=1024, tk=2048)
    da = mm(dr2, w_ff2, mode="nt", name="d_ff_act", tm=1024, tn=1024, tk=1024, out_dtypes=(BF16,), extras=(a_ff,),
            epilogue=lambda acc, av: (acc * (2.0 * jnp.maximum(av.astype(F32), 0.0)),))
    d_w_ff1 = mm(h, da, mode="tn", name="d_w_ff1", tm=1024, tn=1024, tk=2048, out_shards=True)
    dh = mm(da, w_ff1, mode="nt", name="d_h", tm=1024, tn=1024, tk=1024, extras=(dr2,), epilogue=lambda acc, rv: (acc + ALPHA * rv,), b_shards=True)

    def ln1_bwd(dhv, r1v, g):
        xh, r = _ln_stats(r1v)
        return _ln_bwd(xh, r, g, dhv)

    dr1, d_ln1_g, d_ln1_b = row(ln1_bwd, name="ln1_bwd", ins=[(dh, D_MODEL, 0, "row"), (r1, D_MODEL, 0, "row"), (ln1_g, 0, 0, "full")],
                                outs=[(D_MODEL, F32)], sums=[D_MODEL, D_MODEL])
    d_w_out = mm(u, dr1, mode="tn", name="d_w_out", tm=1024, tn=1024, tk=1024)
    du = mm(dr1, w_out, mode="nt", name="d_u", tm=1024, tn=1024, tk=1024)

    def gate_bwd(t0, t1, c0, c1, ya, yb, duv):
        s0, s1 = jax.nn.sigmoid(t0 + c0), jax.nn.sigmoid(t1 + c1)
        dt0 = duv * ya * s0 * (1.0 - s0)
        dt1 = duv * yb * s1 * (1.0 - s1)
        return duv * s0, duv * s1, jnp.concatenate([dt0, dt1], axis=1), jnp.concatenate([_colsum(dt0), _colsum(dt1)], axis=1)

    dy_a, dy_b, d_gates, d_b_gate = row(
        gate_bwd, name="gate_bwd", ins=gate_ins + [(y_a, D_MODEL, 0, "row"), (y_b, D_MODEL, 0, "row"), (du, D_MODEL, 0, "row")],
        outs=[(D_MODEL, BF16), (D_MODEL, BF16), (2 * D_MODEL, BF16)], sums=[2 * D_MODEL])
    d_w_oa_p = mm(o_a, dy_a, mode="tn", name="d_w_o_mla", tm=1024, tn=1024, tk=1024)
    d_w_ob_p = mm(o_b, dy_b, mode="tn", name="d_w_o_dil", tm=1024, tn=1024, tk=1024)
    grads = dict(w_o_mla=_split_cols(_unpad_rows(d_w_oa_p, V_DIM)), w_o_dil=_split_cols(_unpad_rows(d_w_ob_p, DIL_DIM)),
                 w_out=d_w_out.reshape(N_CHIPS, D_MODEL // N_CHIPS, D_MODEL), w_ff1=d_w_ff1, w_ff2=d_w_ff2.reshape(N_CHIPS, D_FF // N_CHIPS, D_MODEL))
    sent = None if early_grads is None else early_grads(grads)
    do_a = mm(dy_a, w_oa_p, mode="nt", name="d_o_mla", tm=1024, tn=1024, tk=1024, after=sent)
    do_b = mm(dy_b, w_ob_p, mode="nt", name="d_o_dil", tm=1024, tn=1024, tk=1024)
    dq_a, dk_a, dv_a = _attn_bwd(q_a, 0, k_a, 0, v_a, 0, o_a, do_a, lse_a, mla_bwd_tables, scale_mla,
                                 name="mla_bwd", batch=batch, seq=seq, out_dtype=F32)
    going = None if early_grads_go is None else early_grads_go(dq_a)
    dq_d, dk_d, dv_d = _attn_bwd(proj_d, qd0, proj_d, kd0, proj_d, vd0, o_b, do_b, lse_b, dil_bwd_tables, scale_dil,
                                 name="dil_bwd", batch=batch, seq=seq, out_dtype=BF16, after=going)

    def mla_post(dq, dk, c, up, dn, c_only):
        dqs = [_rope_bwd(dq[:, h * LANES:(h + 1) * LANES], c, up, dn) for h in range(N_HEADS)]
        dk_sum = dk[:, :LANES]
        for h in range(1, N_HEADS):
            dk_sum = dk_sum + dk[:, h * LANES:(h + 1) * LANES]
        return jnp.concatenate(dqs, axis=1), _rope_bwd(dk_sum, c_only, up, dn)

    dq_lin, d_kr = row(mla_post, name="mla_unrope",
                       ins=[(dq_a, D_MODEL, 0, "row"), (dk_a, D_MODEL, 0, "row"), pos(rope_c), pos(rope_up), pos(rope_dn), pos(rope_c_only)],
                       outs=[(N_HEADS * LANES, BF16), (LANES, BF16)])
    d_w_uq_pt = mm(dq_lin, qn, mode="tn", name="d_w_uq", tm=1024, tn=Q_LORA, tk=1024)
    d_w_uk_p = mm(kvn, dk_a, mode="tn", name="d_w_uk", tm=KV_LORA, tn=1024, tk=1024)
    d_w_uv_p = mm(kvn, dv_a, mode="tn", name="d_w_uv", tm=KV_LORA, tn=1024, tk=1024)
    d_qn = mm(dq_lin, w_uq_pt, mode="nn", name="d_qn", tm=1024, tn=Q_LORA, tk=1024)
    d_kvn_k = mm(dk_a, w_uk_p, mode="nt", name="d_kvn_k", tm=1024, tn=KV_LORA, tk=1024)
    d_kvn = mm(dv_a, w_uv_p, mode="nt", name="d_kvn", tm=1024, tn=KV_LORA, tk=1024, extras=(d_kvn_k,), epilogue=lambda acc, e: (acc + e,))

    def rms_bwd(lora, dq, dkv, dkr, gq, gkv):
        dxq, dgq = _rms_bwd(lora[:, :Q_LORA], gq, dq)
        dxk, dgk = _rms_bwd(lora[:, Q_LORA:], gkv, dkv)
        tail = jnp.zeros((dxq.shape[0], P_GATE - P_KR - LANES), F32)
        return jnp.concatenate([dxq, dxk, dkr.astype(F32), tail], axis=1), dgq, dgk

    d_tail, d_g_q_a, d_g_kv_a = row(
        rms_bwd, name="mla_rms_bwd",
        ins=[(proj, LORA_W, lora0, "row"), (d_qn, Q_LORA, 0, "row"), (d_kvn, KV_LORA, 0, "row"), (d_kr, LANES, 0, "row"),
             (g_q_a, 0, 0, "full"), (g_kv_a, 0, 0, "full")],
        outs=[(P_GATE, BF16)], sums=[Q_LORA, KV_LORA])
    d_proj = [d_tail, d_gates, dq_d, dk_d, dv_d]
    d_w_in_pt = mm(d_proj, x, mode="tn", name="d_w_in", tm=1024, tn=1024, tk=1024)
    grads.update(w_in=_unpad_w_in_t(d_w_in_pt).reshape(N_CHIPS, IN_WIDTH // N_CHIPS, D_MODEL),
                 w_uq=_unpad_head_rows(d_w_uq_pt, MLA_QK).reshape(N_CHIPS, N_HEADS * MLA_QK // N_CHIPS, Q_LORA),
                 w_ukv=_split_cols(_merge_ukv(d_w_uk_p, d_w_uv_p)))
    leaving = None if last_grads is None else last_grads(grads)
    grad_x = mm(d_proj, jnp.concatenate([w_main_t, w_dil_t], axis=0), mode="nn", name="d_x", tm=1024, tn=1024, tk=1024, extras=(dr1,), epilogue=lambda acc, rv: (acc + ALPHA * rv,),
                after=leaving)

    grads.update(
        b_gate=d_b_gate.reshape(2, D_MODEL), g_q_a=d_g_q_a, g_kv_a=d_g_kv_a, ln1_g=d_ln1_g, ln1_b=d_ln1_b, ln2_g=d_ln2_g, ln2_b=d_ln2_b)
    return loss_l, grad_x.reshape(batch, seq, D_MODEL), grads


BIG = ("w_in", "w_uq", "w_ukv", "w_o_mla", "w_o_dil", "w_out", "w_ff1", "w_ff2")
SMALL = (("b_gate", 2 * D_MODEL), ("g_q_a", Q_LORA), ("g_kv_a", KV_LORA), ("ln1_g", D_MODEL), ("ln1_b", D_MODEL),
         ("ln2_g", D_MODEL), ("ln2_b", D_MODEL))
TRANSPOSED = ("w_in", "w_uq")
D2D_PIECES = (4, 2, 1)
ANY = pl.BlockSpec(memory_space=pl.ANY)
SIDE_EFFECTS = pltpu.CompilerParams(has_side_effects=True)


def _place():
    x, y, c = lax.axis_index("x"), lax.axis_index("y"), lax.axis_index("c")
    return x, y, c, ((1 - x, y), (x, 1 - y), (1 - x, 1 - y))


def _half_axis(shape):
    return 0 if shape[0] % 32 == 0 else 1


def _half_shape(shape):
    return (shape[0] // 2, shape[1]) if _half_axis(shape) == 0 else (shape[0], shape[1] // 2)


def _window(ref, lead, shape, which=None, pieces=False):
    axis = _half_axis(shape)
    size = shape[axis] if which is None else shape[axis] // 2
    base = 0 if which is None else which * size
    tile = (16, LANES)[axis]
    count = next(c for c in D2D_PIECES if size % (tile * c) == 0) if pieces else 1
    step = size // count
    spans = [pl.ds(pl.multiple_of(base + i * step, tile), step) for i in range(count)]
    refs = [ref.at[(*lead, s)] if axis == 0 else ref.at[(*lead, slice(None), s)] for s in spans]
    return refs if pieces else refs[0]


def _remote(src, dst, send, recv, to):
    return pltpu.make_async_remote_copy(src_ref=src, dst_ref=dst, send_sem=send, recv_sem=recv, device_id=to, device_id_type=MESH)


def _gather_weights(shards, name):
    n = len(shards)

    def body(*refs):
        srcs, outs, (send, recv) = refs[:n], refs[n:2 * n], refs[2 * n:]
        x, y, c, chips = _place()
        sibling, mine = (x, y, 1 - c), 2 * x + y
        shapes = [s.shape for s in shards]

        def over_ici(t, j, to):
            return _remote(_window(srcs[t], (), shapes[t], c), _window(outs[t], (mine,), shapes[t], c), send.at[t, j], recv.at[t, j], (*to, c))

        def over_d2d(t, j, slot, which, pieces=False):
            from_to = [_window(outs[t], (slot,), shapes[t], which, pieces)] * 2
            if j == 6:
                from_to[0] = _window(srcs[t], (), shapes[t], which, pieces)
            if pieces:
                return [_remote(a, b, send.at[t, j], recv.at[t, j], sibling) for a, b in zip(*from_to)]
            return _remote(*from_to, send.at[t, j], recv.at[t, j], sibling)

        for t in range(n):
            for j, chip in enumerate(chips):
                over_ici(t, j, chip).start()
        for t in range(n):
            for cp in over_d2d(t, 6, mine, None, pieces=True):
                cp.start()
        for t in range(n):
            for j, (cx, cy) in enumerate(chips):
                over_ici(t, j, (cx, cy)).wait_recv()
                for cp in over_d2d(t, 3 + j, 2 * cx + cy, c, pieces=True):
                    cp.start()
        for t in range(n):
            over_d2d(t, 6, mine, None).wait()
            for j, (cx, cy) in enumerate(chips):
                over_d2d(t, 3 + j, 2 * cx + cy, 1 - c).wait_recv()
                over_d2d(t, 3 + j, 2 * cx + cy, c).wait_send()
                over_ici(t, j, (cx, cy)).wait_send()

    return pl.pallas_call(
        body, name=name, in_specs=[ANY] * n, out_specs=[ANY] * n,
        out_shape=[jax.ShapeDtypeStruct((N_CHIPS,) + s.shape, s.dtype) for s in shards],
        scratch_shapes=[pltpu.SemaphoreType.DMA((n, 7)), pltpu.SemaphoreType.DMA((n, 7))],
        compiler_params=SIDE_EFFECTS,
    )(*shards)


def _pair_split(grads, name):
    n = len(grads)

    def body(*refs):
        srcs, outs, (send, recv) = refs[:n], refs[n:2 * n], refs[2 * n:]
        x, y, c, _ = _place()
        for t in range(n):
            for s in range(N_CHIPS):
                _remote(_window(srcs[t], (s,), grads[t].shape[1:], 1 - c), outs[t].at[s], send.at[t], recv.at[t], (x, y, 1 - c)).start()
        for t in range(n):
            _remote(_window(srcs[t], (slice(None),), grads[t].shape[1:], 1 - c), outs[t], send.at[t], recv.at[t], (x, y, 1 - c)).wait()

    return pl.pallas_call(
        body, name=name, in_specs=[ANY] * n, out_specs=[ANY] * n,
        out_shape=[jax.ShapeDtypeStruct((N_CHIPS,) + _half_shape(g.shape[1:]), g.dtype) for g in grads],
        scratch_shapes=[pltpu.SemaphoreType.DMA((n,)), pltpu.SemaphoreType.DMA((n,))],
        compiler_params=SIDE_EFFECTS,
    )(*grads)


def _pair_join(totals, name):
    n = len(totals)

    def body(*refs):
        srcs, outs, (send, recv) = refs[:n], refs[n:2 * n], refs[2 * n:]
        x, y, c, _ = _place()
        for t in range(n):
            for a, b in zip(_window(srcs[t], (), totals[t].shape, None, True), _window(outs[t], (), totals[t].shape, None, True)):
                _remote(a, b, send.at[t], recv.at[t], (x, y, 1 - c)).start()
        for t in range(n):
            _remote(srcs[t], outs[t], send.at[t], recv.at[t], (x, y, 1 - c)).wait()

    return pl.pallas_call(
        body, name=name, in_specs=[ANY] * n, out_specs=[ANY] * n,
        out_shape=[jax.ShapeDtypeStruct(t.shape, t.dtype) for t in totals],
        scratch_shapes=[pltpu.SemaphoreType.DMA((n,)), pltpu.SemaphoreType.DMA((n,))],
        compiler_params=SIDE_EFFECTS,
    )(*totals)


HBM = pl.BlockSpec(memory_space=pltpu.HBM)
SEM = pl.BlockSpec(memory_space=pltpu.SEMAPHORE)
SPLIT = pltpu.CompilerParams(has_side_effects=pltpu.SideEffectType.DATAFLOW_SIDE_EFFECTING)


def _in_hbm(a):
    return pltpu.with_memory_space_constraint(a, pltpu.HBM)


def _split_copies(kind, srcs, lands):
    x, y, c, chips = _place()
    out = []
    for t in range(len(srcs)):
        if kind == "pair":
            out += [(t, s % 3, _window(srcs[t], (s,), srcs[t].shape[1:], 1 - c), lands[t].at[s], (x, y, 1 - c)) for s in range(N_CHIPS)]
            continue
        if kind == "forward":
            shape, sibling = srcs[t].shape, (x, y, 1 - c)
            out += [(t, 0, a, b, sibling) for a, b in zip(_window(srcs[t], (), shape, None, True), _window(lands[t], (2 * x + y,), shape, None, True))]
            out += [(t, j, a, a, sibling) for j, (cx, cy) in enumerate(chips) for a in _window(lands[t], (2 * cx + cy,), shape, c, True)]
            continue
        for j, (cx, cy) in enumerate(chips):
            if kind == "gather":
                shape = srcs[t].shape
                out.append((t, j, _window(srcs[t], (), shape, c), _window(lands[t], (2 * x + y,), shape, c), (cx, cy, c)))
            else:
                out.append((t, j, srcs[t].at[2 * cx + cy], lands[t].at[j], (cx, cy, c)))
    return out


def _split_start(kind, srcs, land_shapes, name, lands=None):
    n = len(srcs)

    def body(*refs):
        src_refs, land_refs, sems, token = refs[:n], refs[n:2 * n], refs[2 * n:2 * n + 6], refs[-1]
        for t, j, s, d, to in _split_copies(kind, src_refs, land_refs):
            _remote(s, d, sems[j], sems[3 + j], to).start()
        token[...] = jnp.zeros_like(token)

    lands = [_in_hbm(lax.empty(s.shape, s.dtype)) for s in land_shapes] if lands is None else list(lands)
    thru = [pltpu.HBM(a.shape, a.dtype) for a in list(srcs) + lands]
    res = pl.pallas_call(
        body, name=name,
        out_shape=(*[pltpu.SemaphoreType.DMA(())] * 6, *thru, jax.ShapeDtypeStruct((8, LANES), F32)),
        in_specs=[HBM] * (2 * n), out_specs=(*[SEM] * 6, *[HBM] * (2 * n), pl.BlockSpec(memory_space=pltpu.VMEM)),
        input_output_aliases={i: 6 + i for i in range(2 * n)}, compiler_params=SPLIT,
    )(*[_in_hbm(s) for s in srcs], *lands)
    return res[:6], res[6:6 + n], res[6 + n:6 + 2 * n], res[-1]


def _split_wait(kind, sems, srcs, lands, after, name):
    n = len(srcs)

    def body(*refs):
        src_refs, land_refs, sem_refs = refs[:n], refs[n:2 * n], refs[2 * n:2 * n + 6]
        for t, j, s, d, to in _split_copies(kind, src_refs, land_refs):
            cp = _remote(s, d, sem_refs[j], sem_refs[3 + j], to)
            cp.wait_send()
            cp.wait_recv()

    res = pl.pallas_call(
        body, name=name, out_shape=[pltpu.HBM(a.shape, a.dtype) for a in list(srcs) + list(lands)],
        in_specs=[HBM] * (2 * n) + [SEM] * 6 + [ANY], out_specs=[HBM] * (2 * n),
        input_output_aliases={i: i for i in range(2 * n)}, compiler_params=SPLIT,
    )(*srcs, *lands, *sems, after)
    return res[:n], res[n:]


def _sum_all_devices(vec, name):
    n_rows = vec.shape[0]

    def body(v_ref, out_ref, buf, send, recv):
        x, y, c, _ = _place()
        me = 4 * x + 2 * y + c
        buf[me] = v_ref[...]
        flips = [(a, b, d) for a in (0, 1) for b in (0, 1) for d in (0, 1)][1:]
        copies = []
        for r, (a, b, d) in enumerate(flips):
            px, py, pc = (1 - x if a else x), (1 - y if b else y), (1 - c if d else c)
            copies.append(pltpu.make_async_remote_copy(src_ref=v_ref, dst_ref=buf.at[me], send_sem=send.at[r], recv_sem=recv.at[r],
                                                       device_id=(px, py, pc), device_id_type=MESH))
            copies[-1].start()
        for r, (a, b, d) in enumerate(flips):
            px, py, pc = (1 - x if a else x), (1 - y if b else y), (1 - c if d else c)
            pltpu.make_async_remote_copy(src_ref=v_ref, dst_ref=buf.at[4 * px + 2 * py + pc], send_sem=send.at[r], recv_sem=recv.at[r],
                                         device_id=(px, py, pc), device_id_type=MESH).wait_recv()
        for cp in copies:
            cp.wait_send()
        total = buf[0]
        for k in range(1, N_DEV):
            total = total + buf[k]
        out_ref[...] = total

    vmem = pl.BlockSpec(memory_space=pltpu.VMEM)
    return pl.pallas_call(
        body, name=name, in_specs=[vmem], out_specs=vmem, out_shape=jax.ShapeDtypeStruct(vec.shape, F32),
        scratch_shapes=[pltpu.VMEM((N_DEV, n_rows, LANES), F32), pltpu.SemaphoreType.DMA((N_DEV - 1,)), pltpu.SemaphoreType.DMA((N_DEV - 1,))],
        compiler_params=pltpu.CompilerParams(has_side_effects=True),
    )(vec)


def _half_tile(half, width):
    t = half
    while t * width * 4 > (2 << 20) and t % 32 == 0:
        t //= 2
    return t


def _pair_add(g, theirs, core, name):
    _, half, width = theirs.shape
    t = _half_tile(half, width)
    n = half // t

    def body(c_ref, a_ref, b_ref, o_ref):
        o_ref[...] = (a_ref[...] + b_ref[...]).astype(BF16)

    tile = pl.BlockSpec((1, t, width), lambda j, i, c_ref: (j, i, 0))
    if _half_axis(g.shape[1:]) == 0:
        mine = pl.BlockSpec((1, t, width), lambda j, i, c_ref: (j, c_ref[0] * n + i, 0))
    else:
        mine = pl.BlockSpec((1, t, width), lambda j, i, c_ref: (j, i, c_ref[0]))
    return pl.pallas_call(
        body, name=name,
        grid_spec=pltpu.PrefetchScalarGridSpec(num_scalar_prefetch=1, grid=(N_CHIPS, n), in_specs=[mine, tile], out_specs=tile),
        out_shape=jax.ShapeDtypeStruct(theirs.shape, BF16), compiler_params=_params(("parallel", "parallel")),
    )(core, g, theirs)


def _chip_sum(part, others, chip, name, after=None):
    _, half, width = part.shape
    t = _half_tile(half, width)

    def body(s_ref, mine, p0, p1, p2, *rest):
        o_ref = rest[-1]
        o_ref[...] = ((mine[0].astype(F32) + p0[0].astype(F32)) + p1[0].astype(F32)) + p2[0].astype(F32)

    return pl.pallas_call(
        body, name=name,
        grid_spec=pltpu.PrefetchScalarGridSpec(
            num_scalar_prefetch=1, grid=(half // t,),
            in_specs=[pl.BlockSpec((1, t, width), lambda i, s_ref: (s_ref[0], i, 0))]
            + [pl.BlockSpec((1, t, width), lambda i, s_ref, j=j: (j, i, 0)) for j in range(3)] + [pl.BlockSpec(memory_space=pl.ANY)] * (after is not None),
            out_specs=pl.BlockSpec((t, width), lambda i, s_ref: (i, 0))),
        out_shape=jax.ShapeDtypeStruct((half, width), F32), compiler_params=_params(("parallel",)),
    )(chip, part, others, others, others, *([after] if after is not None else []))


EARLY = ("w_in", "w_uq", "w_ukv")
LATE = ("w_o_mla", "w_o_dil", "w_out", "w_ff1", "w_ff2")


def _chip_partials(grads, names, core, tag):
    gs = [grads[n] for n in names]
    theirs = _pair_split(gs, "pair_split_" + tag)
    return [_pair_add(g, th, core, "pair_add_" + n) for g, th, n in zip(gs, theirs, names)]


def _sum_small(vals):
    n_in = len(vals)
    n_rows = sum(a.shape[0] * a.shape[1] // LANES for a in vals)
    pad_rows = -(-n_rows // 8) * 8

    def chunks(refs):
        return [(ref, a, j) for ref in refs for a in range(ref.shape[0]) for j in range(ref.shape[1] // LANES)]

    def body(*refs):
        ins, outs, (buf, send, recv) = refs[:n_in], refs[n_in:2 * n_in], refs[2 * n_in:]
        x, y, c, _ = _place()
        me = 4 * x + 2 * y + c
        for r, (ref, a, j) in enumerate(chunks(ins)):
            buf[me, r:r + 1, :] = ref[a:a + 1, j * LANES:(j + 1) * LANES]
        if pad_rows > n_rows:
            buf[me, n_rows:pad_rows, :] = jnp.zeros((pad_rows - n_rows, LANES), F32)
        flips = [(a, b, d) for a in (0, 1) for b in (0, 1) for d in (0, 1)][1:]
        peers = [((1 - x if a else x), (1 - y if b else y), (1 - c if d else c)) for a, b, d in flips]
        copies = [_remote(buf.at[me], buf.at[me], send.at[r], recv.at[r], peer) for r, peer in enumerate(peers)]
        for cp in copies:
            cp.start()
        for r, (px, py, pc) in enumerate(peers):
            _remote(buf.at[me], buf.at[4 * px + 2 * py + pc], send.at[r], recv.at[r], (px, py, pc)).wait_recv()
        for cp in copies:
            cp.wait_send()
        total = buf[0]
        for k in range(1, N_DEV):
            total = total + buf[k]
        for r, (ref, a, j) in enumerate(chunks(outs)):
            ref[a:a + 1, j * LANES:(j + 1) * LANES] = total[r:r + 1, :]

    vmem = pl.BlockSpec(memory_space=pltpu.VMEM)
    return pl.pallas_call(
        body, name="sum_small", in_specs=[vmem] * n_in, out_specs=[vmem] * n_in,
        out_shape=[jax.ShapeDtypeStruct(a.shape, F32) for a in vals],
        scratch_shapes=[pltpu.VMEM((N_DEV, pad_rows, LANES), F32), pltpu.SemaphoreType.DMA((N_DEV - 1,)), pltpu.SemaphoreType.DMA((N_DEV - 1,))],
        compiler_params=SIDE_EFFECTS,
    )(*vals)


def _adam_math(w, g, m, v):
    nm = B1 * m + (1.0 - B1) * g
    nv = B2 * v + (1.0 - B2) * (g * g)
    m_hat = nm / (1.0 - B1 ** ADAM_STEP)
    v_hat = nv / (1.0 - B2 ** ADAM_STEP)
    return -LR * (m_hat / (jnp.sqrt(v_hat) + ADAM_EPS) + WD * w), nm, nv


def _adamw_big(w, mine, theirs, m, v, core, name, side_by_side=False):
    rows, width = w.shape
    if side_by_side:
        t = next(c for c in (152, 96, 64, 32, 16, 8) if rows % c == 0)
        hb = None
        half_spec = pl.BlockSpec((t, width // 2), lambda i, c_ref: (i, 0))
    else:
        t = next(c for c in (256, 128, 64, 32, 16, 8) if (rows // 2) % c == 0)
        hb = rows // 2 // t
        half_spec = pl.BlockSpec((t, width), lambda i, c_ref: (i % hb, 0))

    def body(c_ref, w_ref, a_ref, b_ref, m_ref, v_ref, g_ref, d_ref, nm_ref, nv_ref):
        south = c_ref[0] == 0
        if side_by_side:
            g = jnp.where(south, jnp.concatenate([a_ref[...], b_ref[...]], axis=1), jnp.concatenate([b_ref[...], a_ref[...]], axis=1))
        else:
            g = jnp.where((pl.program_id(0) < hb) == south, a_ref[...], b_ref[...])
        g_ref[...] = g
        d_ref[...], nm_ref[...], nv_ref[...] = _adam_math(w_ref[...], g, m_ref[...], v_ref[...])

    spec = pl.BlockSpec((t, width), lambda i, c_ref: (i, 0))
    return pl.pallas_call(
        body, name=name,
        grid_spec=pltpu.PrefetchScalarGridSpec(num_scalar_prefetch=1, grid=(rows // t,),
                                               in_specs=[spec, half_spec, half_spec, spec, spec], out_specs=[spec] * 4),
        out_shape=[jax.ShapeDtypeStruct(w.shape, F32)] * 4, compiler_params=_params(("parallel",)),
    )(core, w, mine, theirs, m, v)


def _adamw_small(ws, gs, ms, vs):
    n = len(ws)

    def body(*refs):
        for t in range(n):
            w_ref, g_ref, m_ref, v_ref = (refs[k * n + t] for k in range(4))
            d, nm, nv = _adam_math(w_ref[...], g_ref[...], m_ref[...], v_ref[...])
            refs[4 * n + t][...] = d
            refs[5 * n + t][...] = nm
            refs[6 * n + t][...] = nv

    vmem = pl.BlockSpec(memory_space=pltpu.VMEM)
    res = pl.pallas_call(body, name="adamw_small", in_specs=[vmem] * (4 * n), out_specs=[vmem] * (3 * n),
                         out_shape=[jax.ShapeDtypeStruct(a.shape, F32) for a in ws] * 3)(*ws, *gs, *ms, *vs)
    return res[:n], res[n:2 * n], res[2 * n:]


def kernel(x, w_in, b_gate, g_q_a, w_uq, g_kv_a, w_ukv, w_o_mla, w_o_dil, w_out, ln1_g, ln1_b, w_ff1, w_ff2, ln2_g, ln2_b, loss_target, m_w_in, m_b_gate, m_g_q_a, m_w_uq, m_g_kv_a, m_w_ukv, m_w_o_mla, m_w_o_dil, m_w_out, m_ln1_g, m_ln1_b, m_w_ff1, m_w_ff2, m_ln2_g, m_ln2_b, v_w_in, v_b_gate, v_g_q_a, v_w_uq, v_g_kv_a, v_w_ukv, v_w_o_mla, v_w_o_dil, v_w_out, v_ln1_g, v_ln1_b, v_w_ff1, v_w_ff2, v_ln2_g, v_ln2_b):
    order = ("w_in", "b_gate", "g_q_a", "w_uq", "g_kv_a", "w_ukv", "w_o_mla", "w_o_dil", "w_out", "ln1_g", "ln1_b", "w_ff1", "w_ff2", "ln2_g", "ln2_b")
    w = dict(w_in=w_in, b_gate=b_gate, g_q_a=g_q_a, w_uq=w_uq, g_kv_a=g_kv_a, w_ukv=w_ukv, w_o_mla=w_o_mla, w_o_dil=w_o_dil, w_out=w_out,
             ln1_g=ln1_g, ln1_b=ln1_b, w_ff1=w_ff1, w_ff2=w_ff2, ln2_g=ln2_g, ln2_b=ln2_b)
    m = dict(w_in=m_w_in, b_gate=m_b_gate, g_q_a=m_g_q_a, w_uq=m_w_uq, g_kv_a=m_g_kv_a, w_ukv=m_w_ukv, w_o_mla=m_w_o_mla, w_o_dil=m_w_o_dil,
             w_out=m_w_out, ln1_g=m_ln1_g, ln1_b=m_ln1_b, w_ff1=m_w_ff1, w_ff2=m_w_ff2, ln2_g=m_ln2_g, ln2_b=m_ln2_b)
    v = dict(w_in=v_w_in, b_gate=v_b_gate, g_q_a=v_g_q_a, w_uq=v_w_uq, g_kv_a=v_g_kv_a, w_ukv=v_w_ukv, w_o_mla=v_w_o_mla, w_o_dil=v_w_o_dil,
             w_out=v_w_out, ln1_g=v_ln1_g, ln1_b=v_ln1_b, w_ff1=v_w_ff1, w_ff2=v_w_ff2, ln2_g=v_ln2_g, ln2_b=v_ln2_b)
    chip = 2 * lax.axis_index("x") + lax.axis_index("y")
    south = (lax.axis_index("c") == 0).astype(F32)
    gate_w = D_MODEL // N_CHIPS

    core = lax.axis_index("c").astype(jnp.int32).reshape(1)
    turn = lambda n, a: a.T if n in TRANSPOSED else a
    shards = {n: turn(n, w[n][0]).astype(BF16) for n in BIG}
    first = dict(zip(EARLY, _gather_weights([shards[n] for n in EARLY], "gather_early")))
    late_shards = [shards[n] for n in LATE]
    g_sems, g_srcs, g_lands, g_token = _split_start(
        "gather", late_shards, [jax.ShapeDtypeStruct((N_CHIPS,) + s.shape, BF16) for s in late_shards], "gather_late_start")
    b_mine = lax.dynamic_update_slice(jnp.zeros((2, D_MODEL), F32), b_gate[0] * south, (0, chip * gate_w))
    b_full = _sum_all_devices(b_mine.reshape(-1, LANES), "gather_b_gate").reshape(2, D_MODEL)

    sent = {}

    def late_arrived(after):
        srcs, lands = _split_wait("gather", g_sems, g_srcs, g_lands, after, "gather_late_wait")
        sent["forward"] = _split_start("forward", srcs, None, "gather_late_forward_start", lands=lands)
        return sent["forward"][-1]

    def late_weights(after):
        return dict(zip(LATE, _split_wait("forward", *sent["forward"][:3], after, "gather_late_forward_wait")[1]))

    exchange_shapes = lambda parts: [jax.ShapeDtypeStruct((3,) + p.shape[1:], BF16) for p in parts]

    def early_grads(grads_late):
        gs = [grads_late[n] for n in LATE]
        shapes = [jax.ShapeDtypeStruct((N_CHIPS,) + _half_shape(g.shape[1:]), F32) for g in gs]
        sent["pair"] = _split_start("pair", gs, shapes, "pair_split_late_start")
        return sent["pair"][-1]

    def early_grads_go(after):
        gs, theirs = _split_wait("pair", *sent["pair"][:3], after, "pair_split_late_wait")
        parts = [_pair_add(g, th, core, "pair_add_" + n) for g, th, n in zip(gs, theirs, LATE)]
        sent["late"] = _split_start("scatter", parts, exchange_shapes(parts), "exchange_late_start")
        return sent["late"][-1]

    def last_grads(grads_early):
        parts = _chip_partials(grads_early, EARLY, core, "early")
        sent["early"] = _split_start("scatter", parts, exchange_shapes(parts), "exchange_early_start")
        return sent["early"][-1]

    loss_part, grad_x, grads = _local_step(x, loss_target, first, b_full, g_q_a, g_kv_a, ln1_g, ln1_b, ln2_g, ln2_b, token=g_token,
                                           late_arrived=late_arrived, late_weights=late_weights, early_grads=early_grads, early_grads_go=early_grads_go,
                                           last_grads=last_grads)

    g_out, delta, new_m, new_v = {}, {}, {}, {}
    chip1 = chip.astype(jnp.int32).reshape(1)

    def finish(names, parts, others, tag):
        totals = [_chip_sum(p, o, chip1, "chip_sum_" + n) for n, p, o in zip(names, parts, others)]
        for n, mine, theirs in zip(names, totals, _pair_join(totals, "pair_join_" + tag)):
            res = _adamw_big(turn(n, w[n][0]), mine, theirs, turn(n, m[n][0]), turn(n, v[n][0]), core, "adamw_" + n,
                             side_by_side=mine.shape[0] == shards[n].shape[0])
            g_out[n], delta[n], new_m[n], new_v[n] = (turn(n, r) for r in res)

    finish(LATE, *_split_wait("scatter", *sent["late"][:3], grad_x, "exchange_late_wait"), "late")
    small_names = [name for name, _ in SMALL]
    sums = _sum_small([grads[name] for name in small_names] + [loss_part])
    loss = sums[-1][0, 0]
    g_small = dict(zip(small_names, sums))
    g_small["b_gate"] = lax.dynamic_slice(g_small["b_gate"], (0, chip * gate_w), (2, gate_w))
    flat = lambda a: a.reshape(-1, a.shape[-1])
    res = _adamw_small(*[[flat(d[name]) for name in small_names] for d in (w, g_small, m, v)])
    g_out.update(g_small)
    for d, r in zip((delta, new_m, new_v), res):
        d.update(zip(small_names, r))
    done = res[0][0][0:1, 0:1]
    for n in LATE:
        done = done + delta[n][0:1, 0:1]
    finish(EARLY, *_split_wait("scatter", *sent["early"][:3], done, "exchange_early_wait"), "early")


    lead = lambda d: [d[name].reshape(w[name].shape) for name in order]
    return (loss, grad_x, *lead(g_out), *lead(delta), *lead(new_m), *lead(new_v))
```

```python
import functools
import math

import jax
import jax.numpy as jnp
from jax import lax
from jax.experimental import pallas as pl
from jax.experimental.pallas import tpu as pltpu

F32 = jnp.float32
BF16 = jnp.bfloat16
MESH = pl.DeviceIdType.MESH

D_MODEL = 1024
N_HEADS = 8
LANES = 128
NOPE, ROPE, V_DIM = 64, 32, 64
MLA_QK = NOPE + ROPE
Q_LORA, KV_LORA = 384, 256
DIL_DIM = 64
DIL_PATTERNS = ((128, 1), (512, 4), (2048, 16))
D_FF = 4096
N_CHIPS = 4
N_DEV = 8
IN_WIDTH = 4256
LN_EPS, RMS_EPS = 1e-5, 1e-6
NEG = -1e30
LOG2E, LN2 = 1.4426950408889634, 0.6931471805599453
ALPHA = 2.0 ** 0.25
ROPE_THETA = 10000.0
LR, B1, B2, ADAM_EPS, WD, ADAM_STEP = 0.001, 0.9, 0.999, 1e-8, 0.01, 10

P_LORA, P_KR, P_GATE, P_HALF = 0, 640, 1024, 3072
DIL_GROUP = 4 * LANES
P_DIL = N_HEADS // 2 * DIL_GROUP
LORA_W = Q_LORA + KV_LORA
KR_LANE = NOPE

ATT_T = 512
ROW_T = 512
VMEM_LIMIT = 56 * 1024 * 1024

NN = (((1,), (0,)), ((), ()))
NT = (((1,), (1,)), ((), ()))
TN = (((0,), (0,)), ((), ()))


def _params(sem=None, **kw):
    return pltpu.CompilerParams(dimension_semantics=sem, vmem_limit_bytes=VMEM_LIMIT, **kw)


def _matmul(a, b, *, mode, name, tm, tn, tk, out_dtypes=(F32,), extras=(), epilogue=None, b_shards=False, out_shards=False, after=None):
    pieces = list(a) if isinstance(a, (list, tuple)) else [a]
    n_pc = len(pieces)
    a_shape = (pieces[0].shape[0], sum(p.shape[1] for p in pieces))
    if b_shards:
        n_sh, rows_b, cols_b = b.shape
        b_shape = (rows_b, n_sh * cols_b)
    else:
        b_shape = b.shape
    if mode == "nn":
        (m, k), (k2, n) = a_shape, b_shape
    elif mode == "nt":
        (m, k), (n, k2) = a_shape, b_shape
    else:
        (k, m), (k2, n) = a_shape, b_shape
    assert k == k2, (a_shape, b.shape, mode)
    tm, tn, tk = min(tm, m), min(tn, n), min(tk, k)
    assert m % tm == 0 and n % tn == 0 and k % tk == 0, (name, m, n, k, tm, tn, tk)
    nk = k // tk
    n_ex, n_out = len(extras), len(out_dtypes)
    n_in = n_pc + 1 + n_ex + (after is not None)
    dims = {"nn": NN, "nt": NT, "tn": TN}[mode]
    col_tile = tm if mode == "tn" else tk
    blocks = [p.shape[1] // col_tile for p in pieces]
    firsts = [sum(blocks[:p]) for p in range(n_pc)]
    assert all(p.shape[1] % col_tile == 0 for p in pieces), (name, col_tile)

    def body(*refs):
        a_refs, b_ref = refs[:n_pc], refs[n_pc]
        ex_refs = refs[n_pc + 1:n_pc + 1 + n_ex]
        out_refs = refs[n_in:n_in + n_out]

        def finish(acc):
            outs = epilogue(acc, *[r[...] for r in ex_refs]) if epilogue is not None else (acc,)
            for r, o in zip(out_refs, outs):
                r[...] = o.astype(r.dtype)

        kk = pl.program_id(2)

        def step(a_ref):
            part = lax.dot_general(a_ref[...].astype(BF16), b_ref[...].astype(BF16), dims, preferred_element_type=F32)
            if nk == 1:
                finish(part)
                return
            acc_ref = refs[-1]

            @pl.when(kk == 0)
            def _():
                acc_ref[...] = part

            @pl.when(kk > 0)
            def _():
                acc_ref[...] += part

            @pl.when(kk == nk - 1)
            def _():
                finish(acc_ref[...])

        if n_pc == 1:
            step(a_refs[0])
        else:
            at = pl.program_id(0) if mode == "tn" else kk
            for p in range(n_pc):
                pl.when(jnp.logical_and(at >= firsts[p], at < firsts[p] + blocks[p]))(functools.partial(step, a_refs[p]))

    def a_spec_of(p):
        if n_pc == 1:
            return pl.BlockSpec((tk, tm), lambda i, j, kk: (kk, i)) if mode == "tn" else pl.BlockSpec((tm, tk), lambda i, j, kk: (i, kk))
        col = lambda at: jnp.clip(at - firsts[p], 0, blocks[p] - 1)
        mine = lambda at: jnp.logical_and(at >= firsts[p], at < firsts[p] + blocks[p])
        if mode == "tn":
            return pl.BlockSpec((tk, tm), lambda i, j, kk: (jnp.where(mine(i), kk, 0), col(i)))
        return pl.BlockSpec((tm, tk), lambda i, j, kk: (i, col(kk)))

    b_spec = {"nn": pl.BlockSpec((tk, tn), lambda i, j, kk: (kk, j)),
              "nt": pl.BlockSpec((tn, tk), lambda i, j, kk: (j, kk)),
              "tn": pl.BlockSpec((tk, tn), lambda i, j, kk: (kk, j))}[mode]
    tile = pl.BlockSpec((tm, tn), lambda i, j, kk: (i, j))
    out_spec, out_dims = tile, (m, n)
    if b_shards and mode == "nn":
        per = cols_b // tn
        b_spec = pl.BlockSpec((None, tk, tn), lambda i, j, kk: (j // per, kk, j % per))
    elif b_shards:
        assert mode == "nt"
        per = cols_b // tk
        b_spec = pl.BlockSpec((None, tn, tk), lambda i, j, kk: (kk // per, j, kk % per))
    if out_shards:
        assert not extras and epilogue is None
        per_out = n // N_CHIPS // tn
        out_spec = pl.BlockSpec((None, tm, tn), lambda i, j, kk: (j // per_out, i, j % per_out))
        out_dims = (N_CHIPS, m, n // N_CHIPS)
    outs = pl.pallas_call(
        body, name=name,
        grid=(m // tm, n // tn, nk),
        in_specs=[a_spec_of(p) for p in range(n_pc)] + [b_spec] + [tile] * n_ex + [pl.BlockSpec(memory_space=pl.ANY)] * (after is not None),
        out_specs=[out_spec] * n_out,
        out_shape=[jax.ShapeDtypeStruct(out_dims, dt) for dt in out_dtypes],
        scratch_shapes=[pltpu.VMEM((tm, tn), F32)] if nk > 1 else [],
        compiler_params=_params(("parallel", "parallel", "arbitrary")),
    )(*pieces, b, *extras, *([after] if after is not None else []))
    return outs[0] if n_out == 1 else outs


def _rowwise(fn, *, name, rows, seq, ins, outs, sums=()):
    tm = min(ROW_T, seq)
    n_pos = seq // tm
    n_in, n_out, n_sum = len(ins), len(outs), len(sums)

    def body(*refs):
        vals = fn(*[r[...] for r in refs[:n_in]])
        for r, v in zip(refs[n_in:n_in + n_out], vals[:n_out]):
            r[...] = v.astype(r.dtype)
        first = pl.program_id(0) == 0
        for r, v in zip(refs[n_in + n_out:], vals[n_out:]):
            @pl.when(first)
            def _(r=r, v=v):
                r[...] = v

            @pl.when(jnp.logical_not(first))
            def _(r=r, v=v):
                r[...] += v

    def spec(arr, width, col, kind):
        if kind == "row":
            return pl.BlockSpec((tm, width), lambda i, col=col: (i, col))
        if kind == "pos":
            return pl.BlockSpec((tm, width), lambda i, col=col: (i % n_pos, col))
        return pl.BlockSpec(arr.shape, lambda i: (0,) * arr.ndim)

    res = pl.pallas_call(
        body, name=name,
        grid=(rows // tm,),
        in_specs=[spec(*t) for t in ins],
        out_specs=[pl.BlockSpec((tm, w), lambda i: (i, 0)) for w, _ in outs]
        + [pl.BlockSpec((1, w), lambda i: (0, 0)) for w in sums],
        out_shape=[jax.ShapeDtypeStruct((rows, w), dt) for w, dt in outs]
        + [jax.ShapeDtypeStruct((1, w), F32) for w in sums],
        compiler_params=_params(("arbitrary",)),
    )(*[t[0] for t in ins])
    return res


def _colsum(v):
    return jnp.sum(v, axis=0, keepdims=True)


def _rope_fwd(t, c, s_up, s_dn):
    return t * c + pltpu.roll(t, LANES - 16, 1) * s_up + pltpu.roll(t, 16, 1) * s_dn


def _rope_bwd(d, c, s_up, s_dn):
    return d * c + pltpu.roll(d * s_up, 16, 1) + pltpu.roll(d * s_dn, LANES - 16, 1)


def _rope_tables(seq):
    half = ROPE // 2
    inv = jnp.power(ROPE_THETA, -jnp.arange(half, dtype=F32) / half)
    ang = jnp.arange(seq, dtype=F32)[:, None] * inv[None, :]
    cos, sin = jnp.cos(ang), jnp.sin(ang)
    zeros = jnp.zeros((seq, half), F32)
    lo, hi = jnp.ones((seq, KR_LANE), F32), jnp.ones((seq, LANES - KR_LANE - ROPE), F32)
    c = jnp.concatenate([lo, cos, cos, hi], axis=1)
    c_rope_only = jnp.concatenate([0 * lo, cos, cos, 0 * hi], axis=1)
    s_up = jnp.concatenate([0 * lo, -sin, zeros, 0 * hi], axis=1)
    s_dn = jnp.concatenate([0 * lo, zeros, sin, 0 * hi], axis=1)
    return c, s_up, s_dn, c_rope_only


def _rms(x, g):
    r = lax.rsqrt(jnp.mean(x * x, axis=1, keepdims=True) + RMS_EPS)
    return x * r * g


def _rms_bwd(x, g, dy):
    r = lax.rsqrt(jnp.mean(x * x, axis=1, keepdims=True) + RMS_EPS)
    xh = x * r
    dxh = dy * g
    dx = r * (dxh - xh * jnp.mean(dxh * xh, axis=1, keepdims=True))
    return dx, _colsum(dy * xh)


def _ln_stats(x):
    mu = jnp.mean(x, axis=1, keepdims=True)
    xc = x - mu
    r = lax.rsqrt(jnp.mean(xc * xc, axis=1, keepdims=True) + LN_EPS)
    return xc * r, r


def _ln_bwd(xh, r, g, dy):
    dxh = dy * g
    dx = r * (dxh - jnp.mean(dxh, axis=1, keepdims=True) - xh * jnp.mean(dxh * xh, axis=1, keepdims=True))
    return dx, _colsum(dy * xh), _colsum(dy)


def _table_specs(tables, sub):
    whole = lambda a: pl.BlockSpec(a.shape, lambda b, g: (0,) * a.ndim)
    if len(tables) == 1:
        return [whole(tables[0])]
    return [whole(tables[0]), whole(tables[1]), pl.BlockSpec((sub, 1, LANES), lambda b, g: (g, 0, 0))]


def _biased(s, table_refs, delta, head):
    if delta < table_refs[0].shape[0]:
        s = s + table_refs[0][delta]
    if len(table_refs) == 3:
        s = s - table_refs[2][head, 0:1, 0:1] * table_refs[1][delta]
    return s


def _lane_masks(sub):
    lane = lax.broadcasted_iota(jnp.int32, (1, LANES), 1)
    return [(lane // (LANES // sub) == a).astype(F32) for a in range(sub)]


def _attn_fwd(q, qb0, k, kb0, v, vb0, tables, scale, *, name, batch, seq, sub=1, stride=1, after=None):
    t = ATT_T
    nq = seq // t
    rows = batch * seq
    n_tab = len(tables)

    def body(q_ref, k_ref, v_ref, *rest):
        table_refs = rest[:n_tab]
        o_ref, lse_ref, kb, vtb = rest[n_tab + (after is not None):][:4]
        qbs = rest[n_tab + (after is not None) + 4:]
        masks = _lane_masks(sub)
        for a in range(sub):
            qbs[a][...] = (q_ref[...].astype(F32) * masks[a]).astype(BF16) if sub > 1 else q_ref[...].astype(BF16)
        kb[...] = k_ref[...].astype(BF16)
        vtb[...] = v_ref[...].astype(F32).T.astype(BF16)
        for i in range(nq):
            out_t = None
            for a in range(sub):
                qt = qbs[a][i * t:(i + 1) * t, :]
                logits = [_biased(lax.dot_general(kb[j * t:(j + 1) * t, :], qt, NT, preferred_element_type=F32) * (scale * LOG2E), table_refs, i - j, a)
                          for j in range(i + 1)]
                m = jnp.max(functools.reduce(jnp.maximum, logits), axis=0, keepdims=True)
                ps = [jnp.exp2(s - m) for s in logits]
                l = jnp.sum(functools.reduce(jnp.add, ps), axis=0, keepdims=True)
                acc = functools.reduce(jnp.add, [lax.dot_general(vtb[:, j * t:(j + 1) * t], p.astype(BF16), NN, preferred_element_type=F32)
                                                 for j, p in enumerate(ps)])
                part = acc / l if sub == 1 else (acc / l) * masks[a].T
                out_t = part if out_t is None else out_t + part
                lse_ref[i * t:(i + 1) * t, a * LANES:(a + 1) * LANES] = jnp.broadcast_to((m + jnp.log2(l)) * LN2, (LANES, t)).T
            o_ref[i * t:(i + 1) * t, :] = out_t.T

    slab = lambda b0, step: pl.BlockSpec((seq, LANES), lambda b, g: (b, b0 + step * g))
    groups = N_HEADS // sub
    return pl.pallas_call(
        body, name=name,
        grid=(batch, groups),
        in_specs=[slab(qb0, stride), slab(kb0, stride), slab(vb0, stride)] + _table_specs(tables, sub)
        + [pl.BlockSpec(memory_space=pl.ANY)] * (after is not None),
        out_specs=[slab(0, 1), pl.BlockSpec((seq, sub * LANES), lambda b, g: (b, g))],
        out_shape=[jax.ShapeDtypeStruct((rows, groups * LANES), F32), jax.ShapeDtypeStruct((rows, N_HEADS * LANES), F32)],
        scratch_shapes=[pltpu.VMEM((seq, LANES), BF16), pltpu.VMEM((LANES, seq), BF16)] + [pltpu.VMEM((seq, LANES), BF16)] * sub,
        compiler_params=_params(("arbitrary", "arbitrary")),
    )(q, k, v, *tables, *([after] if after is not None else []))


def _attn_bwd(q, qb0, k, kb0, v, vb0, o, do, lse, tables, scale, *, name, batch, seq, out_dtype, sub=1, stride=1, after=None):
    t = ATT_T
    nq = seq // t
    rows = batch * seq
    n_tab = len(tables)
    groups = N_HEADS // sub
    n_out = 3 if sub == 1 else 1

    def body(q_ref, k_ref, v_ref, o_ref, do_ref, lse_ref, *rest):
        table_refs = rest[:n_tab]
        rest = rest[n_tab + (after is not None):]
        out_refs, (kb, vb, dka, dva), per_head = rest[:n_out], rest[n_out:n_out + 4], rest[n_out + 4:]
        qbs, dobs, qtbs, dotbs = (per_head[g * sub:(g + 1) * sub] for g in range(4))
        masks = _lane_masks(sub)
        kb[...] = k_ref[...].astype(BF16)
        vb[...] = v_ref[...].astype(BF16)
        for a in range(sub):
            qa = q_ref[...].astype(F32) * masks[a] if sub > 1 else q_ref[...].astype(F32)
            doa = do_ref[...] * masks[a] if sub > 1 else do_ref[...]
            qbs[a][...] = qa.astype(BF16)
            dobs[a][...] = doa.astype(BF16)
            qtbs[a][...] = qa.T.astype(BF16)
            dotbs[a][...] = doa.T.astype(BF16)
        first = [True] * nq
        for i in range(nq):
            at = slice(i * t, (i + 1) * t)
            dq_all = None
            for a in range(sub):
                qt, dot = qbs[a][at, :], dobs[a][at, :]
                lse_t = lse_ref[at, a * LANES:a * LANES + 1] * LOG2E
                od = o_ref[at, :] * do_ref[at, :]
                delta = jnp.sum(od * masks[a] if sub > 1 else od, axis=1, keepdims=True)
                dq = None
                for j in range(i + 1):
                    kat = slice(j * t, (j + 1) * t)
                    kt, vt = kb[kat, :], vb[kat, :]
                    p = jnp.exp2(_biased(lax.dot_general(qt, kt, NT, preferred_element_type=F32) * (scale * LOG2E), table_refs, i - j, a) - lse_t)
                    dp = lax.dot_general(dot, vt, NT, preferred_element_type=F32)
                    ds = (p * (dp - delta) * scale).astype(BF16)
                    dk_part = lax.dot_general(qtbs[a][:, at], ds, NN, preferred_element_type=F32)
                    dv_part = lax.dot_general(dotbs[a][:, at], p.astype(BF16), NN, preferred_element_type=F32)
                    if first[j]:
                        dka[:, kat] = dk_part
                        dva[:, kat] = dv_part
                        first[j] = False
                    else:
                        dka[:, kat] += dk_part
                        dva[:, kat] += dv_part
                    dq_part = lax.dot_general(ds, kt, NN, preferred_element_type=F32)
                    dq = dq_part if dq is None else dq + dq_part
                dq = dq * masks[a] if sub > 1 else dq
                dq_all = dq if dq_all is None else dq_all + dq
            out_refs[0][at, 0:LANES] = dq_all.astype(out_refs[0].dtype)
        if sub == 1:
            out_refs[1][...] = dka[...].T.astype(out_refs[1].dtype)
            out_refs[2][...] = dva[...].T.astype(out_refs[2].dtype)
        else:
            out_refs[0][:, LANES:2 * LANES] = dka[...].T.astype(out_refs[0].dtype)
            out_refs[0][:, 2 * LANES:3 * LANES] = dva[...].T.astype(out_refs[0].dtype)
            out_refs[0][:, 3 * LANES:] = jnp.zeros((seq, LANES), out_refs[0].dtype)

    slab = lambda b0, step: pl.BlockSpec((seq, LANES), lambda b, g: (b, b0 + step * g))
    if sub == 1:
        out_specs = [slab(0, 1)] * 3
        out_shape = [jax.ShapeDtypeStruct((rows, N_HEADS * LANES), out_dtype)] * 3
    else:
        out_specs = [pl.BlockSpec((seq, 4 * LANES), lambda b, g: (b, g))]
        out_shape = [jax.ShapeDtypeStruct((rows, groups * 4 * LANES), out_dtype)]
    res = pl.pallas_call(
        body, name=name,
        grid=(batch, groups),
        in_specs=[slab(qb0, stride), slab(kb0, stride), slab(vb0, stride), slab(0, 1), slab(0, 1),
                  pl.BlockSpec((seq, sub * LANES), lambda b, g: (b, g))] + _table_specs(tables, sub)
        + [pl.BlockSpec(memory_space=pl.ANY)] * (after is not None),
        out_specs=out_specs, out_shape=out_shape,
        scratch_shapes=[pltpu.VMEM((seq, LANES), BF16)] * 2 + [pltpu.VMEM((LANES, seq), F32)] * 2
        + [pltpu.VMEM((seq, LANES), BF16)] * (2 * sub) + [pltpu.VMEM((LANES, seq), BF16)] * (2 * sub),
        compiler_params=_params(("arbitrary", "arbitrary")),
    )(q, k, v, o, do, lse, *tables, *([after] if after is not None else []))
    return res if sub == 1 else res[0]


def _attention_tables(seq):
    n = seq // ATT_T
    pos = jnp.arange(ATT_T, dtype=jnp.int32)
    dist = jnp.arange(n, dtype=jnp.int32)[:, None, None] * ATT_T + pos[None, :, None] - pos[None, None, :]
    causal = jnp.where(dist[:1] >= 0, 0.0, NEG).astype(F32)
    count = jnp.zeros(dist.shape, F32)
    for window, dilation in DIL_PATTERNS:
        count += ((dist >= 0) & (dist <= window) & (dist % dilation == 0)).astype(F32)
    held = jnp.where(count > 0, jnp.log2(jnp.maximum(count, 1.0)), NEG).astype(F32)
    slopes = jnp.asarray([2.0 ** (-8.0 * (i + 1) / N_HEADS) for i in range(N_HEADS)], F32)
    slopes = jnp.broadcast_to(slopes[:, None, None], (N_HEADS, 1, LANES))
    turned = lambda a: jnp.swapaxes(a, 1, 2)
    far = dist.astype(F32) * LOG2E
    return ((causal,), (turned(causal),)), ((held, far, slopes), (turned(held), turned(far), slopes))


def _pad_heads(w, width):
    kdim, n = w.shape[0], w.shape[1] // width
    return jnp.pad(w.reshape(kdim, n, width), ((0, 0), (0, 0), (0, LANES - width))).reshape(kdim, n * LANES)


def _unpad_heads(w, width):
    kdim, n = w.shape[0], w.shape[1] // LANES
    return w.reshape(kdim, n, LANES)[:, :, :width].reshape(kdim, n * width)


def _pad_head_rows(w, width):
    n, kdim = w.shape[0] // width, w.shape[1]
    return jnp.pad(w.reshape(n, width, kdim), ((0, 0), (0, LANES - width), (0, 0))).reshape(n * LANES, kdim)


def _unpad_head_rows(w, width):
    n, kdim = w.shape[0] // LANES, w.shape[1]
    return w.reshape(n, LANES, kdim)[:, :width].reshape(n * width, kdim)


def _pad_w_in_t(wt):
    n_qkv, pair = 3 * N_HEADS * DIL_DIM, 2 * DIL_DIM
    zeros = lambda n: jnp.zeros((n, wt.shape[1]), wt.dtype)
    main = jnp.concatenate([wt[:LORA_W], zeros(KR_LANE), wt[LORA_W:LORA_W + ROPE], zeros(P_GATE - P_KR - KR_LANE - ROPE),
                            wt[LORA_W + ROPE + n_qkv:]], axis=0)
    qkv = wt[LORA_W + ROPE:LORA_W + ROPE + n_qkv].reshape(3, N_HEADS // 2, pair, wt.shape[1]).transpose(1, 0, 2, 3)
    dil = jnp.pad(qkv, ((0, 0), (0, 1), (0, 0), (0, 0))).reshape(P_DIL, wt.shape[1])
    return main, dil


def _unpad_w_in_t(gt):
    qkv = gt[P_HALF:].reshape(N_HEADS // 2, 4, 2 * DIL_DIM, gt.shape[1])[:, :3].transpose(1, 0, 2, 3).reshape(3 * N_HEADS * DIL_DIM, gt.shape[1])
    return jnp.concatenate([gt[P_LORA:P_KR], gt[P_KR + KR_LANE:P_KR + KR_LANE + ROPE], qkv, gt[P_GATE:P_HALF]], axis=0)


def _split_ukv(w):
    w3 = w.reshape(w.shape[0], N_HEADS, NOPE + V_DIM)
    return (_pad_heads(w3[:, :, :NOPE].reshape(w.shape[0], -1), NOPE),
            _pad_heads(w3[:, :, NOPE:].reshape(w.shape[0], -1), V_DIM))


def _merge_ukv(g_k, g_v):
    kdim = g_k.shape[0]
    k3 = _unpad_heads(g_k, NOPE).reshape(kdim, N_HEADS, NOPE)
    v3 = _unpad_heads(g_v, V_DIM).reshape(kdim, N_HEADS, V_DIM)
    return jnp.concatenate([k3, v3], axis=2).reshape(kdim, N_HEADS * (NOPE + V_DIM))


def _pad_rows(w, width):
    return _pad_heads(w.T, width).T


def _unpad_rows(g, width):
    return _unpad_heads(g.T, width).T


def _join_cols(w):
    return w.transpose(1, 0, 2).reshape(w.shape[1], N_CHIPS * w.shape[2])


def _split_cols(g):
    return g.reshape(g.shape[0], N_CHIPS, g.shape[1] // N_CHIPS).transpose(1, 0, 2)


def _local_step(x3, target3, wg, b_gate, g_q_a, g_kv_a, ln1_g, ln1_b, ln2_g, ln2_b, token=None, late_arrived=None, late_weights=None,
                early_grads=None, early_grads_go=None, last_grads=None):
    w_main_t, w_dil_t = _pad_w_in_t(wg["w_in"].reshape(IN_WIDTH, D_MODEL))
    w_uq_pt = _pad_head_rows(wg["w_uq"].reshape(N_HEADS * MLA_QK, Q_LORA), MLA_QK)
    w_ukv = _join_cols(wg["w_ukv"])
    batch, seq, _ = x3.shape
    rows = batch * seq
    x = x3.reshape(rows, D_MODEL)
    target = target3.reshape(rows, D_MODEL)
    row = functools.partial(_rowwise, rows=rows, seq=seq)
    mm = _matmul

    w_uk_p, w_uv_p = _split_ukv(w_ukv)
    b0, b1 = b_gate[0:1], b_gate[1:2]
    rope_c, rope_up, rope_dn, rope_c_only = _rope_tables(seq)
    (mla_bwd_tables, mla_fwd_tables), (dil_bwd_tables, dil_fwd_tables) = _attention_tables(seq)
    scale_mla, scale_dil = MLA_QK ** -0.5, DIL_DIM ** -0.5
    lora0, kr0, gate0 = P_LORA // LORA_W, P_KR // LANES, P_GATE // D_MODEL

    proj = mm(x, w_main_t, mode="nt", name="proj", tm=1024, tn=1536, tk=1024, after=token)
    proj_d = mm(x, w_dil_t, mode="nt", name="proj_dil", tm=1024, tn=1024, tk=1024, out_dtypes=(BF16,))

    def prep(lora, gq, gkv):
        return _rms(lora[:, :Q_LORA], gq), _rms(lora[:, Q_LORA:], gkv)

    qn, kvn = row(prep, name="mla_rms", ins=[(proj, LORA_W, lora0, "row"), (g_q_a, 0, 0, "full"), (g_kv_a, 0, 0, "full")],
                  outs=[(Q_LORA, BF16), (KV_LORA, BF16)])
    q_lin = mm(qn, w_uq_pt, mode="nt", name="q_up", tm=1024, tn=1024, tk=Q_LORA)
    k_lin = mm(kvn, w_uk_p, mode="nn", name="k_up", tm=1024, tn=1024, tk=KV_LORA)
    v_a = mm(kvn, w_uv_p, mode="nn", name="v_up", tm=1024, tn=1024, tk=KV_LORA, out_dtypes=(BF16,))

    def rope_qk(ql, kl, kr, c, up, dn):
        k_rot = _rope_fwd(kr, c, up, dn)
        qs = [_rope_fwd(ql[:, h * LANES:(h + 1) * LANES], c, up, dn) for h in range(N_HEADS)]
        ks = [kl[:, h * LANES:(h + 1) * LANES] + k_rot for h in range(N_HEADS)]
        return jnp.concatenate(qs, axis=1), jnp.concatenate(ks, axis=1)

    pos = lambda tab: (tab, LANES, 0, "pos")
    q_a, k_a = row(rope_qk, name="rope_qk",
                   ins=[(q_lin, D_MODEL, 0, "row"), (k_lin, D_MODEL, 0, "row"), (proj, LANES, kr0, "row"), pos(rope_c), pos(rope_up), pos(rope_dn)],
                   outs=[(N_HEADS * LANES, BF16), (N_HEADS * LANES, BF16)])
    o_a, lse_a = _attn_fwd(q_a, 0, k_a, 0, v_a, 0, mla_fwd_tables, scale_mla, name="mla_fwd", batch=batch, seq=seq)
    arrived = None if late_arrived is None else late_arrived(o_a)
    o_b, lse_b = _attn_fwd(proj_d, 0, proj_d, 1, proj_d, 2, dil_fwd_tables, scale_dil, name="dil_fwd", batch=batch, seq=seq, sub=2, stride=4, after=arrived)
    late = wg if late_weights is None else late_weights(o_b)
    w_oa_p = _pad_rows(_join_cols(late["w_o_mla"]), V_DIM)
    w_ob = _join_cols(late["w_o_dil"])
    w_out, w_ff1, w_ff2 = late["w_out"].reshape(D_MODEL, D_MODEL), late["w_ff1"], late["w_ff2"].reshape(D_FF, D_MODEL)
    y_a = mm(o_a, w_oa_p, mode="nn", name="o_mla", tm=1024, tn=1024, tk=1024)
    y_b = mm(o_b, w_ob, mode="nn", name="o_dil", tm=1024, tn=1024, tk=1024)

    def gate(t0, t1, c0, c1, ya, yb):
        return (jax.nn.sigmoid(t0 + c0) * ya + jax.nn.sigmoid(t1 + c1) * yb,)

    gate_ins = [(proj, D_MODEL, gate0, "row"), (proj, D_MODEL, gate0 + 1, "row"), (b0, 0, 0, "full"), (b1, 0, 0, "full")]
    (u,) = row(gate, name="gate", ins=gate_ins + [(y_a, D_MODEL, 0, "row"), (y_b, D_MODEL, 0, "row")], outs=[(D_MODEL, BF16)])
    mixed = mm(u, w_out, mode="nn", name="mix", tm=1024, tn=1024, tk=1024)

    def ln1(xv, mv, g, b):
        r1 = ALPHA * xv + mv
        xh, _ = _ln_stats(r1)
        return r1, xh * g + b

    r1, h = row(ln1, name="ln1", ins=[(x, D_MODEL, 0, "row"), (mixed, D_MODEL, 0, "row"), (ln1_g, 0, 0, "full"), (ln1_b, 0, 0, "full")],
                outs=[(D_MODEL, F32), (D_MODEL, F32)])

    def relu2(acc):
        r = jnp.maximum(acc, 0.0)
        return acc, r * r

    a_ff, z = mm(h, w_ff1, mode="nn", name="ff1", tm=1024, tn=1024, tk=1024, out_dtypes=(BF16, BF16), epilogue=relu2, b_shards=True)
    f = mm(z, w_ff2, mode="nn", name="ff2", tm=1024, tn=1024, tk=2048)

    def ln2_loss(hv, fv, tv, g, b):
        xh, r = _ln_stats(ALPHA * hv + fv)
        err = xh * g + b - tv
        dy = err * (1.0 / D_MODEL)
        dr2, dg, db = _ln_bwd(xh, r, g, dy)
        loss = jnp.sum(_colsum(err * err), axis=1, keepdims=True) * (0.5 / D_MODEL)
        return dr2, jnp.broadcast_to(loss, (1, LANES)), dg, db

    dr2, loss_l, d_ln2_g, d_ln2_b = row(
        ln2_loss, name="ln2_loss",
        ins=[(h, D_MODEL, 0, "row"), (f, D_MODEL, 0, "row"), (target, D_MODEL, 0, "row"), (ln2_g, 0, 0, "full"), (ln2_b, 0, 0, "full")],
        outs=[(D_MODEL, F32)], sums=[LANES, D_MODEL, D_MODEL])

    d_w_ff2 = mm(z, dr2, mode="tn", name="d_w_ff2", tm=1024, tn=1024, tk=2048)
    da = mm(dr2, w_ff2, mode="nt", name="d_ff_act", tm=1024, tn=1024, tk=1024, out_dtypes=(BF16,), extras=(a_ff,),
            epilogue=lambda acc, av: (acc * (2.0 * jnp.maximum(av.astype(F32), 0.0)),))
    d_w_ff1 = mm(h, da, mode="tn", name="d_w_ff1", tm=1024, tn=1024, tk=2048, out_shards=True)
    dh = mm(da, w_ff1, mode="nt", name="d_h", tm=1024, tn=1024, tk=1024, extras=(dr2,), epilogue=lambda acc, rv: (acc + ALPHA * rv,), b_shards=True)

    def ln1_bwd(dhv, r1v, g):
        xh, r = _ln_stats(r1v)
        return _ln_bwd(xh, r, g, dhv)

    dr1, d_ln1_g, d_ln1_b = row(ln1_bwd, name="ln1_bwd", ins=[(dh, D_MODEL, 0, "row"), (r1, D_MODEL, 0, "row"), (ln1_g, 0, 0, "full")],
                                outs=[(D_MODEL, F32)], sums=[D_MODEL, D_MODEL])
    d_w_out = mm(u, dr1, mode="tn", name="d_w_out", tm=1024, tn=1024, tk=1024)
    du = mm(dr1, w_out, mode="nt", name="d_u", tm=1024, tn=1024, tk=1024)

    def gate_bwd(t0, t1, c0, c1, ya, yb, duv):
        s0, s1 = jax.nn.sigmoid(t0 + c0), jax.nn.sigmoid(t1 + c1)
        dt0 = duv * ya * s0 * (1.0 - s0)
        dt1 = duv * yb * s1 * (1.0 - s1)
        return duv * s0, duv * s1, jnp.concatenate([dt0, dt1], axis=1), jnp.concatenate([_colsum(dt0), _colsum(dt1)], axis=1)

    dy_a, dy_b, d_gates, d_b_gate = row(
        gate_bwd, name="gate_bwd", ins=gate_ins + [(y_a, D_MODEL, 0, "row"), (y_b, D_MODEL, 0, "row"), (du, D_MODEL, 0, "row")],
        outs=[(D_MODEL, BF16), (D_MODEL, BF16), (2 * D_MODEL, BF16)], sums=[2 * D_MODEL])
    d_w_oa_p = mm(o_a, dy_a, mode="tn", name="d_w_o_mla", tm=1024, tn=1024, tk=1024)
    d_w_ob = mm(o_b, dy_b, mode="tn", name="d_w_o_dil", tm=1024, tn=1024, tk=1024)
    grads = dict(w_o_mla=_split_cols(_unpad_rows(d_w_oa_p, V_DIM)), w_o_dil=_split_cols(d_w_ob),
                 w_out=d_w_out.reshape(N_CHIPS, D_MODEL // N_CHIPS, D_MODEL), w_ff1=d_w_ff1, w_ff2=d_w_ff2.reshape(N_CHIPS, D_FF // N_CHIPS, D_MODEL))
    sent = None if early_grads is None else early_grads(grads)
    do_a = mm(dy_a, w_oa_p, mode="nt", name="d_o_mla", tm=1024, tn=1024, tk=1024, after=sent)
    do_b = mm(dy_b, w_ob, mode="nt", name="d_o_dil", tm=1024, tn=1024, tk=1024)
    dq_a, dk_a, dv_a = _attn_bwd(q_a, 0, k_a, 0, v_a, 0, o_a, do_a, lse_a, mla_bwd_tables, scale_mla,
                                 name="mla_bwd", batch=batch, seq=seq, out_dtype=F32)
    going = None if early_grads_go is None else early_grads_go(dq_a)
    d_qkv_d = _attn_bwd(proj_d, 0, proj_d, 1, proj_d, 2, o_b, do_b, lse_b, dil_bwd_tables, scale_dil,
                        name="dil_bwd", batch=batch, seq=seq, out_dtype=BF16, sub=2, stride=4, after=going)

    def mla_post(dq, dk, c, up, dn, c_only):
        dqs = [_rope_bwd(dq[:, h * LANES:(h + 1) * LANES], c, up, dn) for h in range(N_HEADS)]
        dk_sum = dk[:, :LANES]
        for h in range(1, N_HEADS):
            dk_sum = dk_sum + dk[:, h * LANES:(h + 1) * LANES]
        return jnp.concatenate(dqs, axis=1), _rope_bwd(dk_sum, c_only, up, dn)

    dq_lin, d_kr = row(mla_post, name="mla_unrope",
                       ins=[(dq_a, D_MODEL, 0, "row"), (dk_a, D_MODEL, 0, "row"), pos(rope_c), pos(rope_up), pos(rope_dn), pos(rope_c_only)],
                       outs=[(N_HEADS * LANES, BF16), (LANES, BF16)])
    d_w_uq_pt = mm(dq_lin, qn, mode="tn", name="d_w_uq", tm=1024, tn=Q_LORA, tk=1024)
    d_w_uk_p = mm(kvn, dk_a, mode="tn", name="d_w_uk", tm=KV_LORA, tn=1024, tk=1024)
    d_w_uv_p = mm(kvn, dv_a, mode="tn", name="d_w_uv", tm=KV_LORA, tn=1024, tk=1024)
    d_qn = mm(dq_lin, w_uq_pt, mode="nn", name="d_qn", tm=1024, tn=Q_LORA, tk=1024)
    d_kvn_k = mm(dk_a, w_uk_p, mode="nt", name="d_kvn_k", tm=1024, tn=KV_LORA, tk=1024)
    d_kvn = mm(dv_a, w_uv_p, mode="nt", name="d_kvn", tm=1024, tn=KV_LORA, tk=1024, extras=(d_kvn_k,), epilogue=lambda acc, e: (acc + e,))

    def rms_bwd(lora, dq, dkv, dkr, gq, gkv):
        dxq, dgq = _rms_bwd(lora[:, :Q_LORA], gq, dq)
        dxk, dgk = _rms_bwd(lora[:, Q_LORA:], gkv, dkv)
        tail = jnp.zeros((dxq.shape[0], P_GATE - P_KR - LANES), F32)
        return jnp.concatenate([dxq, dxk, dkr.astype(F32), tail], axis=1), dgq, dgk

    d_tail, d_g_q_a, d_g_kv_a = row(
        rms_bwd, name="mla_rms_bwd",
        ins=[(proj, LORA_W, lora0, "row"), (d_qn, Q_LORA, 0, "row"), (d_kvn, KV_LORA, 0, "row"), (d_kr, LANES, 0, "row"),
             (g_q_a, 0, 0, "full"), (g_kv_a, 0, 0, "full")],
        outs=[(P_GATE, BF16)], sums=[Q_LORA, KV_LORA])
    d_proj = [d_tail, d_gates, d_qkv_d]
    d_w_in_pt = mm(d_proj, x, mode="tn", name="d_w_in", tm=1024, tn=1024, tk=1024)
    grads.update(w_in=_unpad_w_in_t(d_w_in_pt).reshape(N_CHIPS, IN_WIDTH // N_CHIPS, D_MODEL),
                 w_uq=_unpad_head_rows(d_w_uq_pt, MLA_QK).reshape(N_CHIPS, N_HEADS * MLA_QK // N_CHIPS, Q_LORA),
                 w_ukv=_split_cols(_merge_ukv(d_w_uk_p, d_w_uv_p)))
    leaving = None if last_grads is None else last_grads(grads)
    grad_x = mm(d_proj, jnp.concatenate([w_main_t, w_dil_t], axis=0), mode="nn", name="d_x", tm=1024, tn=1024, tk=1024, extras=(dr1,), epilogue=lambda acc, rv: (acc + ALPHA * rv,),
                after=leaving)

    grads.update(
        b_gate=d_b_gate.reshape(2, D_MODEL), g_q_a=d_g_q_a, g_kv_a=d_g_kv_a, ln1_g=d_ln1_g, ln1_b=d_ln1_b, ln2_g=d_ln2_g, ln2_b=d_ln2_b)
    return loss_l, grad_x.reshape(batch, seq, D_MODEL), grads


BIG = ("w_in", "w_uq", "w_ukv", "w_o_mla", "w_o_dil", "w_out", "w_ff1", "w_ff2")
SMALL = (("b_gate", 2 * D_MODEL), ("g_q_a", Q_LORA), ("g_kv_a", KV_LORA), ("ln1_g", D_MODEL), ("ln1_b", D_MODEL),
         ("ln2_g", D_MODEL), ("ln2_b", D_MODEL))
TRANSPOSED = ("w_in", "w_uq")
D2D_PIECES = (4, 2, 1)
ANY = pl.BlockSpec(memory_space=pl.ANY)
SIDE_EFFECTS = pltpu.CompilerParams(has_side_effects=True)


def _place():
    x, y, c = lax.axis_index("x"), lax.axis_index("y"), lax.axis_index("c")
    return x, y, c, ((1 - x, y), (x, 1 - y), (1 - x, 1 - y))


def _half_axis(shape):
    return 0 if shape[0] % 32 == 0 else 1


def _half_shape(shape):
    return (shape[0] // 2, shape[1]) if _half_axis(shape) == 0 else (shape[0], shape[1] // 2)


def _window(ref, lead, shape, which=None, pieces=False):
    axis = _half_axis(shape)
    size = shape[axis] if which is None else shape[axis] // 2
    base = 0 if which is None else which * size
    tile = (16, LANES)[axis]
    count = next(c for c in D2D_PIECES if size % (tile * c) == 0) if pieces else 1
    step = size // count
    spans = [pl.ds(pl.multiple_of(base + i * step, tile), step) for i in range(count)]
    refs = [ref.at[(*lead, s)] if axis == 0 else ref.at[(*lead, slice(None), s)] for s in spans]
    return refs if pieces else refs[0]


def _remote(src, dst, send, recv, to):
    return pltpu.make_async_remote_copy(src_ref=src, dst_ref=dst, send_sem=send, recv_sem=recv, device_id=to, device_id_type=MESH)


def _gather_weights(shards, name):
    n = len(shards)

    def body(*refs):
        srcs, outs, (send, recv) = refs[:n], refs[n:2 * n], refs[2 * n:]
        x, y, c, chips = _place()
        sibling, mine = (x, y, 1 - c), 2 * x + y
        shapes = [s.shape for s in shards]

        def over_ici(t, j, to):
            return _remote(_window(srcs[t], (), shapes[t], c), _window(outs[t], (mine,), shapes[t], c), send.at[t, j], recv.at[t, j], (*to, c))

        def over_d2d(t, j, slot, which, pieces=False):
            from_to = [_window(outs[t], (slot,), shapes[t], which, pieces)] * 2
            if j == 6:
                from_to[0] = _window(srcs[t], (), shapes[t], which, pieces)
            if pieces:
                return [_remote(a, b, send.at[t, j], recv.at[t, j], sibling) for a, b in zip(*from_to)]
            return _remote(*from_to, send.at[t, j], recv.at[t, j], sibling)

        for t in range(n):
            for j, chip in enumerate(chips):
                over_ici(t, j, chip).start()
        for t in range(n):
            for cp in over_d2d(t, 6, mine, None, pieces=True):
                cp.start()
        for t in range(n):
            for j, (cx, cy) in enumerate(chips):
                over_ici(t, j, (cx, cy)).wait_recv()
                for cp in over_d2d(t, 3 + j, 2 * cx + cy, c, pieces=True):
                    cp.start()
        for t in range(n):
            over_d2d(t, 6, mine, None).wait()
            for j, (cx, cy) in enumerate(chips):
                over_d2d(t, 3 + j, 2 * cx + cy, 1 - c).wait_recv()
                over_d2d(t, 3 + j, 2 * cx + cy, c).wait_send()
                over_ici(t, j, (cx, cy)).wait_send()

    return pl.pallas_call(
        body, name=name, in_specs=[ANY] * n, out_specs=[ANY] * n,
        out_shape=[jax.ShapeDtypeStruct((N_CHIPS,) + s.shape, s.dtype) for s in shards],
        scratch_shapes=[pltpu.SemaphoreType.DMA((n, 7)), pltpu.SemaphoreType.DMA((n, 7))],
        compiler_params=SIDE_EFFECTS,
    )(*shards)


def _pair_split(grads, name):
    n = len(grads)

    def body(*refs):
        srcs, outs, (send, recv) = refs[:n], refs[n:2 * n], refs[2 * n:]
        x, y, c, _ = _place()
        for t in range(n):
            for s in range(N_CHIPS):
                _remote(_window(srcs[t], (s,), grads[t].shape[1:], 1 - c), outs[t].at[s], send.at[t], recv.at[t], (x, y, 1 - c)).start()
        for t in range(n):
            _remote(_window(srcs[t], (slice(None),), grads[t].shape[1:], 1 - c), outs[t], send.at[t], recv.at[t], (x, y, 1 - c)).wait()

    return pl.pallas_call(
        body, name=name, in_specs=[ANY] * n, out_specs=[ANY] * n,
        out_shape=[jax.ShapeDtypeStruct((N_CHIPS,) + _half_shape(g.shape[1:]), g.dtype) for g in grads],
        scratch_shapes=[pltpu.SemaphoreType.DMA((n,)), pltpu.SemaphoreType.DMA((n,))],
        compiler_params=SIDE_EFFECTS,
    )(*grads)


def _pair_join(totals, name):
    n = len(totals)

    def body(*refs):
        srcs, outs, (send, recv) = refs[:n], refs[n:2 * n], refs[2 * n:]
        x, y, c, _ = _place()
        for t in range(n):
            for a, b in zip(_window(srcs[t], (), totals[t].shape, None, True), _window(outs[t], (), totals[t].shape, None, True)):
                _remote(a, b, send.at[t], recv.at[t], (x, y, 1 - c)).start()
        for t in range(n):
            _remote(srcs[t], outs[t], send.at[t], recv.at[t], (x, y, 1 - c)).wait()

    return pl.pallas_call(
        body, name=name, in_specs=[ANY] * n, out_specs=[ANY] * n,
        out_shape=[jax.ShapeDtypeStruct(t.shape, t.dtype) for t in totals],
        scratch_shapes=[pltpu.SemaphoreType.DMA((n,)), pltpu.SemaphoreType.DMA((n,))],
        compiler_params=SIDE_EFFECTS,
    )(*totals)


HBM = pl.BlockSpec(memory_space=pltpu.HBM)
SEM = pl.BlockSpec(memory_space=pltpu.SEMAPHORE)
SPLIT = pltpu.CompilerParams(has_side_effects=pltpu.SideEffectType.DATAFLOW_SIDE_EFFECTING)


def _in_hbm(a):
    return pltpu.with_memory_space_constraint(a, pltpu.HBM)


def _split_copies(kind, srcs, lands):
    x, y, c, chips = _place()
    out = []
    for t in range(len(srcs)):
        if kind == "pair":
            out += [(t, s % 3, _window(srcs[t], (s,), srcs[t].shape[1:], 1 - c), lands[t].at[s], (x, y, 1 - c)) for s in range(N_CHIPS)]
            continue
        if kind == "forward":
            shape, sibling = srcs[t].shape, (x, y, 1 - c)
            out += [(t, 0, a, b, sibling) for a, b in zip(_window(srcs[t], (), shape, None, True), _window(lands[t], (2 * x + y,), shape, None, True))]
            out += [(t, j, a, a, sibling) for j, (cx, cy) in enumerate(chips) for a in _window(lands[t], (2 * cx + cy,), shape, c, True)]
            continue
        for j, (cx, cy) in enumerate(chips):
            if kind == "gather":
                shape = srcs[t].shape
                out.append((t, j, _window(srcs[t], (), shape, c), _window(lands[t], (2 * x + y,), shape, c), (cx, cy, c)))
            else:
                out.append((t, j, srcs[t].at[2 * cx + cy], lands[t].at[j], (cx, cy, c)))
    return out


def _split_start(kind, srcs, land_shapes, name, lands=None):
    n = len(srcs)

    def body(*refs):
        src_refs, land_refs, sems, token = refs[:n], refs[n:2 * n], refs[2 * n:2 * n + 6], refs[-1]
        for t, j, s, d, to in _split_copies(kind, src_refs, land_refs):
            _remote(s, d, sems[j], sems[3 + j], to).start()
        token[...] = jnp.zeros_like(token)

    lands = [_in_hbm(lax.empty(s.shape, s.dtype)) for s in land_shapes] if lands is None else list(lands)
    thru = [pltpu.HBM(a.shape, a.dtype) for a in list(srcs) + lands]
    res = pl.pallas_call(
        body, name=name,
        out_shape=(*[pltpu.SemaphoreType.DMA(())] * 6, *thru, jax.ShapeDtypeStruct((8, LANES), F32)),
        in_specs=[HBM] * (2 * n), out_specs=(*[SEM] * 6, *[HBM] * (2 * n), pl.BlockSpec(memory_space=pltpu.VMEM)),
        input_output_aliases={i: 6 + i for i in range(2 * n)}, compiler_params=SPLIT,
    )(*[_in_hbm(s) for s in srcs], *lands)
    return res[:6], res[6:6 + n], res[6 + n:6 + 2 * n], res[-1]


def _split_wait(kind, sems, srcs, lands, after, name):
    n = len(srcs)

    def body(*refs):
        src_refs, land_refs, sem_refs = refs[:n], refs[n:2 * n], refs[2 * n:2 * n + 6]
        for t, j, s, d, to in _split_copies(kind, src_refs, land_refs):
            cp = _remote(s, d, sem_refs[j], sem_refs[3 + j], to)
            cp.wait_send()
            cp.wait_recv()

    res = pl.pallas_call(
        body, name=name, out_shape=[pltpu.HBM(a.shape, a.dtype) for a in list(srcs) + list(lands)],
        in_specs=[HBM] * (2 * n) + [SEM] * 6 + [ANY], out_specs=[HBM] * (2 * n),
        input_output_aliases={i: i for i in range(2 * n)}, compiler_params=SPLIT,
    )(*srcs, *lands, *sems, after)
    return res[:n], res[n:]


def _sum_all_devices(vec, name):
    n_rows = vec.shape[0]

    def body(v_ref, out_ref, buf, send, recv):
        x, y, c, _ = _place()
        me = 4 * x + 2 * y + c
        buf[me] = v_ref[...]
        flips = [(a, b, d) for a in (0, 1) for b in (0, 1) for d in (0, 1)][1:]
        copies = []
        for r, (a, b, d) in enumerate(flips):
            px, py, pc = (1 - x if a else x), (1 - y if b else y), (1 - c if d else c)
            copies.append(pltpu.make_async_remote_copy(src_ref=v_ref, dst_ref=buf.at[me], send_sem=send.at[r], recv_sem=recv.at[r],
                                                       device_id=(px, py, pc), device_id_type=MESH))
            copies[-1].start()
        for r, (a, b, d) in enumerate(flips):
            px, py, pc = (1 - x if a else x), (1 - y if b else y), (1 - c if d else c)
            pltpu.make_async_remote_copy(src_ref=v_ref, dst_ref=buf.at[4 * px + 2 * py + pc], send_sem=send.at[r], recv_sem=recv.at[r],
                                         device_id=(px, py, pc), device_id_type=MESH).wait_recv()
        for cp in copies:
            cp.wait_send()
        total = buf[0]
        for k in range(1, N_DEV):
            total = total + buf[k]
        out_ref[...] = total

    vmem = pl.BlockSpec(memory_space=pltpu.VMEM)
    return pl.pallas_call(
        body, name=name, in_specs=[vmem], out_specs=vmem, out_shape=jax.ShapeDtypeStruct(vec.shape, F32),
        scratch_shapes=[pltpu.VMEM((N_DEV, n_rows, LANES), F32), pltpu.SemaphoreType.DMA((N_DEV - 1,)), pltpu.SemaphoreType.DMA((N_DEV - 1,))],
        compiler_params=pltpu.CompilerParams(has_side_effects=True),
    )(vec)


def _half_tile(half, width):
    t = half
    while t * width * 4 > (2 << 20) and t % 32 == 0:
        t //= 2
    return t


def _pair_add(g, theirs, core, name):
    _, half, width = theirs.shape
    t = _half_tile(half, width)
    n = half // t

    def body(c_ref, a_ref, b_ref, o_ref):
        o_ref[...] = (a_ref[...] + b_ref[...]).astype(BF16)

    tile = pl.BlockSpec((1, t, width), lambda j, i, c_ref: (j, i, 0))
    if _half_axis(g.shape[1:]) == 0:
        mine = pl.BlockSpec((1, t, width), lambda j, i, c_ref: (j, c_ref[0] * n + i, 0))
    else:
        mine = pl.BlockSpec((1, t, width), lambda j, i, c_ref: (j, i, c_ref[0]))
    return pl.pallas_call(
        body, name=name,
        grid_spec=pltpu.PrefetchScalarGridSpec(num_scalar_prefetch=1, grid=(N_CHIPS, n), in_specs=[mine, tile], out_specs=tile),
        out_shape=jax.ShapeDtypeStruct(theirs.shape, BF16), compiler_params=_params(("parallel", "parallel")),
    )(core, g, theirs)


def _chip_sum(part, others, chip, name, after=None):
    _, half, width = part.shape
    t = _half_tile(half, width)

    def body(s_ref, mine, p0, p1, p2, *rest):
        o_ref = rest[-1]
        o_ref[...] = ((mine[0].astype(F32) + p0[0].astype(F32)) + p1[0].astype(F32)) + p2[0].astype(F32)

    return pl.pallas_call(
        body, name=name,
        grid_spec=pltpu.PrefetchScalarGridSpec(
            num_scalar_prefetch=1, grid=(half // t,),
            in_specs=[pl.BlockSpec((1, t, width), lambda i, s_ref: (s_ref[0], i, 0))]
            + [pl.BlockSpec((1, t, width), lambda i, s_ref, j=j: (j, i, 0)) for j in range(3)] + [pl.BlockSpec(memory_space=pl.ANY)] * (after is not None),
            out_specs=pl.BlockSpec((t, width), lambda i, s_ref: (i, 0))),
        out_shape=jax.ShapeDtypeStruct((half, width), F32), compiler_params=_params(("parallel",)),
    )(chip, part, others, others, others, *([after] if after is not None else []))


EARLY = ("w_in", "w_uq", "w_ukv")
LATE = ("w_o_mla", "w_o_dil", "w_out", "w_ff1", "w_ff2")


def _chip_partials(grads, names, core, tag):
    gs = [grads[n] for n in names]
    theirs = _pair_split(gs, "pair_split_" + tag)
    return [_pair_add(g, th, core, "pair_add_" + n) for g, th, n in zip(gs, theirs, names)]


def _sum_small(vals):
    n_in = len(vals)
    n_rows = sum(a.shape[0] * a.shape[1] // LANES for a in vals)
    pad_rows = -(-n_rows // 8) * 8

    def chunks(refs):
        return [(ref, a, j) for ref in refs for a in range(ref.shape[0]) for j in range(ref.shape[1] // LANES)]

    def body(*refs):
        ins, outs, (buf, send, recv) = refs[:n_in], refs[n_in:2 * n_in], refs[2 * n_in:]
        x, y, c, _ = _place()
        me = 4 * x + 2 * y + c
        for r, (ref, a, j) in enumerate(chunks(ins)):
            buf[me, r:r + 1, :] = ref[a:a + 1, j * LANES:(j + 1) * LANES]
        if pad_rows > n_rows:
            buf[me, n_rows:pad_rows, :] = jnp.zeros((pad_rows - n_rows, LANES), F32)
        flips = [(a, b, d) for a in (0, 1) for b in (0, 1) for d in (0, 1)][1:]
        peers = [((1 - x if a else x), (1 - y if b else y), (1 - c if d else c)) for a, b, d in flips]
        copies = [_remote(buf.at[me], buf.at[me], send.at[r], recv.at[r], peer) for r, peer in enumerate(peers)]
        for cp in copies:
            cp.start()
        for r, (px, py, pc) in enumerate(peers):
            _remote(buf.at[me], buf.at[4 * px + 2 * py + pc], send.at[r], recv.at[r], (px, py, pc)).wait_recv()
        for cp in copies:
            cp.wait_send()
        total = buf[0]
        for k in range(1, N_DEV):
            total = total + buf[k]
        for r, (ref, a, j) in enumerate(chunks(outs)):
            ref[a:a + 1, j * LANES:(j + 1) * LANES] = total[r:r + 1, :]

    vmem = pl.BlockSpec(memory_space=pltpu.VMEM)
    return pl.pallas_call(
        body, name="sum_small", in_specs=[vmem] * n_in, out_specs=[vmem] * n_in,
        out_shape=[jax.ShapeDtypeStruct(a.shape, F32) for a in vals],
        scratch_shapes=[pltpu.VMEM((N_DEV, pad_rows, LANES), F32), pltpu.SemaphoreType.DMA((N_DEV - 1,)), pltpu.SemaphoreType.DMA((N_DEV - 1,))],
        compiler_params=SIDE_EFFECTS,
    )(*vals)


def _adam_math(w, g, m, v):
    nm = B1 * m + (1.0 - B1) * g
    nv = B2 * v + (1.0 - B2) * (g * g)
    m_hat = nm / (1.0 - B1 ** ADAM_STEP)
    v_hat = nv / (1.0 - B2 ** ADAM_STEP)
    return -LR * (m_hat / (jnp.sqrt(v_hat) + ADAM_EPS) + WD * w), nm, nv


def _adamw_big(w, mine, theirs, m, v, core, name, side_by_side=False):
    rows, width = w.shape
    if side_by_side:
        t = next(c for c in (152, 96, 64, 32, 16, 8) if rows % c == 0)
        hb = None
        half_spec = pl.BlockSpec((t, width // 2), lambda i, c_ref: (i, 0))
    else:
        t = next(c for c in (256, 128, 64, 32, 16, 8) if (rows // 2) % c == 0)
        hb = rows // 2 // t
        half_spec = pl.BlockSpec((t, width), lambda i, c_ref: (i % hb, 0))

    def body(c_ref, w_ref, a_ref, b_ref, m_ref, v_ref, g_ref, d_ref, nm_ref, nv_ref):
        south = c_ref[0] == 0
        if side_by_side:
            g = jnp.where(south, jnp.concatenate([a_ref[...], b_ref[...]], axis=1), jnp.concatenate([b_ref[...], a_ref[...]], axis=1))
        else:
            g = jnp.where((pl.program_id(0) < hb) == south, a_ref[...], b_ref[...])
        g_ref[...] = g
        d_ref[...], nm_ref[...], nv_ref[...] = _adam_math(w_ref[...], g, m_ref[...], v_ref[...])

    spec = pl.BlockSpec((t, width), lambda i, c_ref: (i, 0))
    return pl.pallas_call(
        body, name=name,
        grid_spec=pltpu.PrefetchScalarGridSpec(num_scalar_prefetch=1, grid=(rows // t,),
                                               in_specs=[spec, half_spec, half_spec, spec, spec], out_specs=[spec] * 4),
        out_shape=[jax.ShapeDtypeStruct(w.shape, F32)] * 4, compiler_params=_params(("parallel",)),
    )(core, w, mine, theirs, m, v)


def _adamw_small(ws, gs, ms, vs):
    n = len(ws)

    def body(*refs):
        for t in range(n):
            w_ref, g_ref, m_ref, v_ref = (refs[k * n + t] for k in range(4))
            d, nm, nv = _adam_math(w_ref[...], g_ref[...], m_ref[...], v_ref[...])
            refs[4 * n + t][...] = d
            refs[5 * n + t][...] = nm
            refs[6 * n + t][...] = nv

    vmem = pl.BlockSpec(memory_space=pltpu.VMEM)
    res = pl.pallas_call(body, name="adamw_small", in_specs=[vmem] * (4 * n), out_specs=[vmem] * (3 * n),
                         out_shape=[jax.ShapeDtypeStruct(a.shape, F32) for a in ws] * 3)(*ws, *gs, *ms, *vs)
    return res[:n], res[n:2 * n], res[2 * n:]


def kernel(x, w_in, b_gate, g_q_a, w_uq, g_kv_a, w_ukv, w_o_mla, w_o_dil, w_out, ln1_g, ln1_b, w_ff1, w_ff2, ln2_g, ln2_b, loss_target, m_w_in, m_b_gate, m_g_q_a, m_w_uq, m_g_kv_a, m_w_ukv, m_w_o_mla, m_w_o_dil, m_w_out, m_ln1_g, m_ln1_b, m_w_ff1, m_w_ff2, m_ln2_g, m_ln2_b, v_w_in, v_b_gate, v_g_q_a, v_w_uq, v_g_kv_a, v_w_ukv, v_w_o_mla, v_w_o_dil, v_w_out, v_ln1_g, v_ln1_b, v_w_ff1, v_w_ff2, v_ln2_g, v_ln2_b):
    order = ("w_in", "b_gate", "g_q_a", "w_uq", "g_kv_a", "w_ukv", "w_o_mla", "w_o_dil", "w_out", "ln1_g", "ln1_b", "w_ff1", "w_ff2", "ln2_g", "ln2_b")
    w = dict(w_in=w_in, b_gate=b_gate, g_q_a=g_q_a, w_uq=w_uq, g_kv_a=g_kv_a, w_ukv=w_ukv, w_o_mla=w_o_mla, w_o_dil=w_o_dil, w_out=w_out,
             ln1_g=ln1_g, ln1_b=ln1_b, w_ff1=w_ff1, w_ff2=w_ff2, ln2_g=ln2_g, ln2_b=ln2_b)
    m = dict(w_in=m_w_in, b_gate=m_b_gate, g_q_a=m_g_q_a, w_uq=m_w_uq, g_kv_a=m_g_kv_a, w_ukv=m_w_ukv, w_o_mla=m_w_o_mla, w_o_dil=m_w_o_dil,
             w_out=m_w_out, ln1_g=m_ln1_g, ln1_b=m_ln1_b, w_ff1=m_w_ff1, w_ff2=m_w_ff2, ln2_g=m_ln2_g, ln2_b=m_ln2_b)
    v = dict(w_in=v_w_in, b_gate=v_b_gate, g_q_a=v_g_q_a, w_uq=v_w_uq, g_kv_a=v_g_kv_a, w_ukv=v_w_ukv, w_o_mla=v_w_o_mla, w_o_dil=v_w_o_dil,
             w_out=v_w_out, ln1_g=v_ln1_g, ln1_b=v_ln1_b, w_ff1=v_w_ff1, w_ff2=v_w_ff2, ln2_g=v_ln2_g, ln2_b=v_ln2_b)
    chip = 2 * lax.axis_index("x") + lax.axis_index("y")
    south = (lax.axis_index("c") == 0).astype(F32)
    gate_w = D_MODEL // N_CHIPS

    core = lax.axis_index("c").astype(jnp.int32).reshape(1)
    turn = lambda n, a: a.T if n in TRANSPOSED else a
    shards = {n: turn(n, w[n][0]).astype(BF16) for n in BIG}
    first = dict(zip(EARLY, _gather_weights([shards[n] for n in EARLY], "gather_early")))
    late_shards = [shards[n] for n in LATE]
    g_sems, g_srcs, g_lands, g_token = _split_start(
        "gather", late_shards, [jax.ShapeDtypeStruct((N_CHIPS,) + s.shape, BF16) for s in late_shards], "gather_late_start")
    b_mine = lax.dynamic_update_slice(jnp.zeros((2, D_MODEL), F32), b_gate[0] * south, (0, chip * gate_w))
    b_full = _sum_all_devices(b_mine.reshape(-1, LANES), "gather_b_gate").reshape(2, D_MODEL)

    sent = {}

    def late_arrived(after):
        srcs, lands = _split_wait("gather", g_sems, g_srcs, g_lands, after, "gather_late_wait")
        sent["forward"] = _split_start("forward", srcs, None, "gather_late_forward_start", lands=lands)
        return sent["forward"][-1]

    def late_weights(after):
        return dict(zip(LATE, _split_wait("forward", *sent["forward"][:3], after, "gather_late_forward_wait")[1]))

    exchange_shapes = lambda parts: [jax.ShapeDtypeStruct((3,) + p.shape[1:], BF16) for p in parts]

    def early_grads(grads_late):
        gs = [grads_late[n] for n in LATE]
        shapes = [jax.ShapeDtypeStruct((N_CHIPS,) + _half_shape(g.shape[1:]), F32) for g in gs]
        sent["pair"] = _split_start("pair", gs, shapes, "pair_split_late_start")
        return sent["pair"][-1]

    def early_grads_go(after):
        gs, theirs = _split_wait("pair", *sent["pair"][:3], after, "pair_split_late_wait")
        parts = [_pair_add(g, th, core, "pair_add_" + n) for g, th, n in zip(gs, theirs, LATE)]
        sent["late"] = _split_start("scatter", parts, exchange_shapes(parts), "exchange_late_start")
        return sent["late"][-1]

    def last_grads(grads_early):
        parts = _chip_partials(grads_early, EARLY, core, "early")
        sent["early"] = _split_start("scatter", parts, exchange_shapes(parts), "exchange_early_start")
        return sent["early"][-1]

    loss_part, grad_x, grads = _local_step(x, loss_target, first, b_full, g_q_a, g_kv_a, ln1_g, ln1_b, ln2_g, ln2_b, token=g_token,
                                           late_arrived=late_arrived, late_weights=late_weights, early_grads=early_grads, early_grads_go=early_grads_go,
                                           last_grads=last_grads)

    g_out, delta, new_m, new_v = {}, {}, {}, {}
    chip1 = chip.astype(jnp.int32).reshape(1)

    def finish(names, parts, others, tag):
        totals = [_chip_sum(p, o, chip1, "chip_sum_" + n) for n, p, o in zip(names, parts, others)]
        for n, mine, theirs in zip(names, totals, _pair_join(totals, "pair_join_" + tag)):
            res = _adamw_big(turn(n, w[n][0]), mine, theirs, turn(n, m[n][0]), turn(n, v[n][0]), core, "adamw_" + n,
                             side_by_side=mine.shape[0] == shards[n].shape[0])
            g_out[n], delta[n], new_m[n], new_v[n] = (turn(n, r) for r in res)

    finish(LATE, *_split_wait("scatter", *sent["late"][:3], grad_x, "exchange_late_wait"), "late")
    small_names = [name for name, _ in SMALL]
    sums = _sum_small([grads[name] for name in small_names] + [loss_part])
    loss = sums[-1][0, 0]
    g_small = dict(zip(small_names, sums))
    g_small["b_gate"] = lax.dynamic_slice(g_small["b_gate"], (0, chip * gate_w), (2, gate_w))
    flat = lambda a: a.reshape(-1, a.shape[-1])
    res = _adamw_small(*[[flat(d[name]) for name in small_names] for d in (w, g_small, m, v)])
    g_out.update(g_small)
    for d, r in zip((delta, new_m, new_v), res):
        d.update(zip(small_names, r))
    done = res[0][0][0:1, 0:1]
    for n in LATE:
        done = done + delta[n][0:1, 0:1]
    finish(EARLY, *_split_wait("scatter", *sent["early"][:3], done, "exchange_early_wait"), "early")


    lead = lambda d: [d[name].reshape(w[name].shape) for name in order]
    return (loss, grad_x, *lead(g_out), *lead(delta), *lead(new_m), *lead(new_v))
```

```python
import functools
import math

import jax
import jax.numpy as jnp
from jax import lax
from jax.experimental import pallas as pl
from jax.experimental.pallas import tpu as pltpu

F32 = jnp.float32
BF16 = jnp.bfloat16
MESH = pl.DeviceIdType.MESH

D_MODEL = 1024
N_HEADS = 8
LANES = 128
NOPE, ROPE, V_DIM = 64, 32, 64
MLA_QK = NOPE + ROPE
Q_LORA, KV_LORA = 384, 256
DIL_DIM = 64
DIL_PATTERNS = ((128, 1), (512, 4), (2048, 16))
D_FF = 4096
N_CHIPS = 4
N_DEV = 8
IN_WIDTH = 4256
LN_EPS, RMS_EPS = 1e-5, 1e-6
NEG = -1e30
LOG2E, LN2 = 1.4426950408889634, 0.6931471805599453
ALPHA = 2.0 ** 0.25
ROPE_THETA = 10000.0
LR, B1, B2, ADAM_EPS, WD, ADAM_STEP = 0.001, 0.9, 0.999, 1e-8, 0.01, 10

P_LORA, P_KR, P_GATE, P_HALF = 0, 640, 1024, 3072
DIL_GROUP = 4 * LANES
P_DIL = N_HEADS // 2 * DIL_GROUP
LORA_W = Q_LORA + KV_LORA
KR_LANE = NOPE

ATT_T = 512
ROW_T = 512
VMEM_LIMIT = 56 * 1024 * 1024

NN = (((1,), (0,)), ((), ()))
NT = (((1,), (1,)), ((), ()))
TN = (((0,), (0,)), ((), ()))


def _params(sem=None, **kw):
    return pltpu.CompilerParams(dimension_semantics=sem, vmem_limit_bytes=VMEM_LIMIT, **kw)


def _matmul(a, b, *, mode, name, tm, tn, tk, out_dtypes=(F32,), extras=(), epilogue=None, b_shards=False, out_shards=False, after=None):
    pieces = list(a) if isinstance(a, (list, tuple)) else [a]
    n_pc = len(pieces)
    a_shape = (pieces[0].shape[0], sum(p.shape[1] for p in pieces))
    if b_shards:
        n_sh, rows_b, cols_b = b.shape
        b_shape = (rows_b, n_sh * cols_b)
    else:
        b_shape = b.shape
    if mode == "nn":
        (m, k), (k2, n) = a_shape, b_shape
    elif mode == "nt":
        (m, k), (n, k2) = a_shape, b_shape
    else:
        (k, m), (k2, n) = a_shape, b_shape
    assert k == k2, (a_shape, b.shape, mode)
    tm, tn, tk = min(tm, m), min(tn, n), min(tk, k)
    assert m % tm == 0 and n % tn == 0 and k % tk == 0, (name, m, n, k, tm, tn, tk)
    nk = k // tk
    n_ex, n_out = len(extras), len(out_dtypes)
    n_in = n_pc + 1 + n_ex + (after is not None)
    dims = {"nn": NN, "nt": NT, "tn": TN}[mode]
    col_tile = tm if mode == "tn" else tk
    blocks = [p.shape[1] // col_tile for p in pieces]
    firsts = [sum(blocks[:p]) for p in range(n_pc)]
    assert all(p.shape[1] % col_tile == 0 for p in pieces), (name, col_tile)

    def body(*refs):
        a_refs, b_ref = refs[:n_pc], refs[n_pc]
        ex_refs = refs[n_pc + 1:n_pc + 1 + n_ex]
        out_refs = refs[n_in:n_in + n_out]

        def finish(acc):
            outs = epilogue(acc, *[r[...] for r in ex_refs]) if epilogue is not None else (acc,)
            for r, o in zip(out_refs, outs):
                r[...] = o.astype(r.dtype)

        kk = pl.program_id(2)

        def step(a_ref):
            part = lax.dot_general(a_ref[...].astype(BF16), b_ref[...].astype(BF16), dims, preferred_element_type=F32)
            if nk == 1:
                finish(part)
                return
            acc_ref = refs[-1]

            @pl.when(kk == 0)
            def _():
                acc_ref[...] = part

            @pl.when(kk > 0)
            def _():
                acc_ref[...] += part

            @pl.when(kk == nk - 1)
            def _():
                finish(acc_ref[...])

        if n_pc == 1:
            step(a_refs[0])
        else:
            at = pl.program_id(0) if mode == "tn" else kk
            for p in range(n_pc):
                pl.when(jnp.logical_and(at >= firsts[p], at < firsts[p] + blocks[p]))(functools.partial(step, a_refs[p]))

    def a_spec_of(p):
        if n_pc == 1:
            return pl.BlockSpec((tk, tm), lambda i, j, kk: (kk, i)) if mode == "tn" else pl.BlockSpec((tm, tk), lambda i, j, kk: (i, kk))
        col = lambda at: jnp.clip(at - firsts[p], 0, blocks[p] - 1)
        mine = lambda at: jnp.logical_and(at >= firsts[p], at < firsts[p] + blocks[p])
        if mode == "tn":
            return pl.BlockSpec((tk, tm), lambda i, j, kk: (jnp.where(mine(i), kk, 0), col(i)))
        return pl.BlockSpec((tm, tk), lambda i, j, kk: (i, col(kk)))

    b_spec = {"nn": pl.BlockSpec((tk, tn), lambda i, j, kk: (kk, j)),
              "nt": pl.BlockSpec((tn, tk), lambda i, j, kk: (j, kk)),
              "tn": pl.BlockSpec((tk, tn), lambda i, j, kk: (kk, j))}[mode]
    tile = pl.BlockSpec((tm, tn), lambda i, j, kk: (i, j))
    out_spec, out_dims = tile, (m, n)
    if b_shards and mode == "nn":
        per = cols_b // tn
        b_spec = pl.BlockSpec((None, tk, tn), lambda i, j, kk: (j // per, kk, j % per))
    elif b_shards:
        assert mode == "nt"
        per = cols_b // tk
        b_spec = pl.BlockSpec((None, tn, tk), lambda i, j, kk: (kk // per, j, kk % per))
    if out_shards:
        assert not extras and epilogue is None
        per_out = n // N_CHIPS // tn
        out_spec = pl.BlockSpec((None, tm, tn), lambda i, j, kk: (j // per_out, i, j % per_out))
        out_dims = (N_CHIPS, m, n // N_CHIPS)
    outs = pl.pallas_call(
        body, name=name,
        grid=(m // tm, n // tn, nk),
        in_specs=[a_spec_of(p) for p in range(n_pc)] + [b_spec] + [tile] * n_ex + [pl.BlockSpec(memory_space=pl.ANY)] * (after is not None),
        out_specs=[out_spec] * n_out,
        out_shape=[jax.ShapeDtypeStruct(out_dims, dt) for dt in out_dtypes],
        scratch_shapes=[pltpu.VMEM((tm, tn), F32)] if nk > 1 else [],
        compiler_params=_params(("parallel", "parallel", "arbitrary")),
    )(*pieces, b, *extras, *([after] if after is not None else []))
    return outs[0] if n_out == 1 else outs


def _rowwise(fn, *, name, rows, seq, ins, outs, sums=()):
    tm = min(ROW_T, seq)
    n_pos = seq // tm
    n_in, n_out, n_sum = len(ins), len(outs), len(sums)

    def body(*refs):
        vals = fn(*[r[...] for r in refs[:n_in]])
        for r, v in zip(refs[n_in:n_in + n_out], vals[:n_out]):
            r[...] = v.astype(r.dtype)
        first = pl.program_id(0) == 0
        for r, v in zip(refs[n_in + n_out:], vals[n_out:]):
            @pl.when(first)
            def _(r=r, v=v):
                r[...] = v

            @pl.when(jnp.logical_not(first))
            def _(r=r, v=v):
                r[...] += v

    def spec(arr, width, col, kind):
        if kind == "row":
            return pl.BlockSpec((tm, width), lambda i, col=col: (i, col))
        if kind == "pos":
            return pl.BlockSpec((tm, width), lambda i, col=col: (i % n_pos, col))
        return pl.BlockSpec(arr.shape, lambda i: (0,) * arr.ndim)

    res = pl.pallas_call(
        body, name=name,
        grid=(rows // tm,),
        in_specs=[spec(*t) for t in ins],
        out_specs=[pl.BlockSpec((tm, w), lambda i: (i, 0)) for w, _ in outs]
        + [pl.BlockSpec((1, w), lambda i: (0, 0)) for w in sums],
        out_shape=[jax.ShapeDtypeStruct((rows, w), dt) for w, dt in outs]
        + [jax.ShapeDtypeStruct((1, w), F32) for w in sums],
        compiler_params=_params(("arbitrary",)),
    )(*[t[0] for t in ins])
    return res


def _colsum(v):
    return jnp.sum(v, axis=0, keepdims=True)


def _rope_fwd(t, c, s_up, s_dn):
    return t * c + pltpu.roll(t, LANES - 16, 1) * s_up + pltpu.roll(t, 16, 1) * s_dn


def _rope_bwd(d, c, s_up, s_dn):
    return d * c + pltpu.roll(d * s_up, 16, 1) + pltpu.roll(d * s_dn, LANES - 16, 1)


def _rope_tables(seq):
    half = ROPE // 2
    inv = jnp.power(ROPE_THETA, -jnp.arange(half, dtype=F32) / half)
    ang = jnp.arange(seq, dtype=F32)[:, None] * inv[None, :]
    cos, sin = jnp.cos(ang), jnp.sin(ang)
    zeros = jnp.zeros((seq, half), F32)
    lo, hi = jnp.ones((seq, KR_LANE), F32), jnp.ones((seq, LANES - KR_LANE - ROPE), F32)
    c = jnp.concatenate([lo, cos, cos, hi], axis=1)
    c_rope_only = jnp.concatenate([0 * lo, cos, cos, 0 * hi], axis=1)
    s_up = jnp.concatenate([0 * lo, -sin, zeros, 0 * hi], axis=1)
    s_dn = jnp.concatenate([0 * lo, zeros, sin, 0 * hi], axis=1)
    return c, s_up, s_dn, c_rope_only


def _rms(x, g):
    r = lax.rsqrt(jnp.mean(x * x, axis=1, keepdims=True) + RMS_EPS)
    return x * r * g


def _rms_bwd(x, g, dy):
    r = lax.rsqrt(jnp.mean(x * x, axis=1, keepdims=True) + RMS_EPS)
    xh = x * r
    dxh = dy * g
    dx = r * (dxh - xh * jnp.mean(dxh * xh, axis=1, keepdims=True))
    return dx, _colsum(dy * xh)


def _ln_stats(x):
    mu = jnp.mean(x, axis=1, keepdims=True)
    xc = x - mu
    r = lax.rsqrt(jnp.mean(xc * xc, axis=1, keepdims=True) + LN_EPS)
    return xc * r, r


def _ln_bwd(xh, r, g, dy):
    dxh = dy * g
    dx = r * (dxh - jnp.mean(dxh, axis=1, keepdims=True) - xh * jnp.mean(dxh * xh, axis=1, keepdims=True))
    return dx, _colsum(dy * xh), _colsum(dy)


def _table_specs(tables, sub):
    whole = lambda a: pl.BlockSpec(a.shape, lambda b, g: (0,) * a.ndim)
    if len(tables) == 1:
        return [whole(tables[0])]
    return [whole(tables[0]), whole(tables[1]), pl.BlockSpec((sub, 1, LANES), lambda b, g: (g, 0, 0))]


def _biased(s, table_refs, delta, head):
    if delta < table_refs[0].shape[0]:
        s = s + table_refs[0][delta]
    if len(table_refs) == 3:
        s = s - table_refs[2][head, 0:1, 0:1] * table_refs[1][delta]
    return s


def _lane_masks(sub):
    lane = lax.broadcasted_iota(jnp.int32, (1, LANES), 1)
    return [(lane // (LANES // sub) == a).astype(F32) for a in range(sub)]


def _attn_fwd(q, qb0, k, kb0, v, vb0, tables, scale, *, name, batch, seq, sub=1, stride=1, after=None):
    t = ATT_T
    nq = seq // t
    rows = batch * seq
    n_tab = len(tables)

    def body(q_ref, k_ref, v_ref, *rest):
        table_refs = rest[:n_tab]
        o_ref, lse_ref, kb, vtb = rest[n_tab + (after is not None):][:4]
        qbs = rest[n_tab + (after is not None) + 4:]
        masks = _lane_masks(sub)
        for a in range(sub):
            qbs[a][...] = (q_ref[...].astype(F32) * masks[a]).astype(BF16) if sub > 1 else q_ref[...].astype(BF16)
        kb[...] = k_ref[...].astype(BF16)
        vtb[...] = v_ref[...].astype(F32).T.astype(BF16)
        for i in range(nq):
            out_t = None
            for a in range(sub):
                qt = qbs[a][i * t:(i + 1) * t, :]
                logits = [_biased(lax.dot_general(kb[j * t:(j + 1) * t, :], qt, NT, preferred_element_type=F32) * (scale * LOG2E), table_refs, i - j, a)
                          for j in range(i + 1)]
                m = jnp.max(functools.reduce(jnp.maximum, logits), axis=0, keepdims=True)
                ps = [jnp.exp2(s - m) for s in logits]
                l = jnp.sum(functools.reduce(jnp.add, ps), axis=0, keepdims=True)
                acc = functools.reduce(jnp.add, [lax.dot_general(vtb[:, j * t:(j + 1) * t], p.astype(BF16), NN, preferred_element_type=F32)
                                                 for j, p in enumerate(ps)])
                part = acc / l if sub == 1 else (acc / l) * masks[a].T
                out_t = part if out_t is None else out_t + part
                lse_ref[i * t:(i + 1) * t, a * LANES:(a + 1) * LANES] = jnp.broadcast_to((m + jnp.log2(l)) * LN2, (LANES, t)).T
            o_ref[i * t:(i + 1) * t, :] = out_t.T

    slab = lambda b0, step: pl.BlockSpec((seq, LANES), lambda b, g: (b, b0 + step * g))
    groups = N_HEADS // sub
    return pl.pallas_call(
        body, name=name,
        grid=(batch, groups),
        in_specs=[slab(qb0, stride), slab(kb0, stride), slab(vb0, stride)] + _table_specs(tables, sub)
        + [pl.BlockSpec(memory_space=pl.ANY)] * (after is not None),
        out_specs=[slab(0, 1), pl.BlockSpec((seq, sub * LANES), lambda b, g: (b, g))],
        out_shape=[jax.ShapeDtypeStruct((rows, groups * LANES), F32), jax.ShapeDtypeStruct((rows, N_HEADS * LANES), F32)],
        scratch_shapes=[pltpu.VMEM((seq, LANES), BF16), pltpu.VMEM((LANES, seq), BF16)] + [pltpu.VMEM((seq, LANES), BF16)] * sub,
        compiler_params=_params(("arbitrary", "arbitrary")),
    )(q, k, v, *tables, *([after] if after is not None else []))


def _attn_bwd(q, qb0, k, kb0, v, vb0, o, do, lse, tables, scale, *, name, batch, seq, out_dtype, sub=1, stride=1, after=None):
    t = ATT_T
    nq = seq // t
    rows = batch * seq
    n_tab = len(tables)
    groups = N_HEADS // sub
    n_out = 3 if sub == 1 else 1

    def body(q_ref, k_ref, v_ref, o_ref, do_ref, lse_ref, *rest):
        table_refs = rest[:n_tab]
        rest = rest[n_tab + (after is not None):]
        out_refs, (kb, vb, dka, dva), per_head = rest[:n_out], rest[n_out:n_out + 4], rest[n_out + 4:]
        qbs, dobs, qtbs, dotbs = (per_head[g * sub:(g + 1) * sub] for g in range(4))
        masks = _lane_masks(sub)
        kb[...] = k_ref[...].astype(BF16)
        vb[...] = v_ref[...].astype(BF16)
        for a in range(sub):
            qa = q_ref[...].astype(F32) * masks[a] if sub > 1 else q_ref[...].astype(F32)
            doa = do_ref[...] * masks[a] if sub > 1 else do_ref[...]
            qbs[a][...] = qa.astype(BF16)
            dobs[a][...] = doa.astype(BF16)
            qtbs[a][...] = qa.T.astype(BF16)
            dotbs[a][...] = doa.T.astype(BF16)
        first = [True] * nq
        for i in range(nq):
            at = slice(i * t, (i + 1) * t)
            dq_all = None
            for a in range(sub):
                qt, dot = qbs[a][at, :], dobs[a][at, :]
                lse_t = lse_ref[at, a * LANES:a * LANES + 1] * LOG2E
                od = o_ref[at, :] * do_ref[at, :]
                delta = jnp.sum(od * masks[a] if sub > 1 else od, axis=1, keepdims=True)
                dq = None
                for j in range(i + 1):
                    kat = slice(j * t, (j + 1) * t)
                    kt, vt = kb[kat, :], vb[kat, :]
                    p = jnp.exp2(_biased(lax.dot_general(qt, kt, NT, preferred_element_type=F32) * (scale * LOG2E), table_refs, i - j, a) - lse_t)
                    dp = lax.dot_general(dot, vt, NT, preferred_element_type=F32)
                    ds = (p * (dp - delta) * scale).astype(BF16)
                    dk_part = lax.dot_general(qtbs[a][:, at], ds, NN, preferred_element_type=F32)
                    dv_part = lax.dot_general(dotbs[a][:, at], p.astype(BF16), NN, preferred_element_type=F32)
                    if first[j]:
                        dka[:, kat] = dk_part
                        dva[:, kat] = dv_part
                        first[j] = False
                    else:
                        dka[:, kat] += dk_part
                        dva[:, kat] += dv_part
                    dq_part = lax.dot_general(ds, kt, NN, preferred_element_type=F32)
                    dq = dq_part if dq is None else dq + dq_part
                dq = dq * masks[a] if sub > 1 else dq
                dq_all = dq if dq_all is None else dq_all + dq
            out_refs[0][at, 0:LANES] = dq_all.astype(out_refs[0].dtype)
        if sub == 1:
            out_refs[1][...] = dka[...].T.astype(out_refs[1].dtype)
            out_refs[2][...] = dva[...].T.astype(out_refs[2].dtype)
        else:
            out_refs[0][:, LANES:2 * LANES] = dka[...].T.astype(out_refs[0].dtype)
            out_refs[0][:, 2 * LANES:3 * LANES] = dva[...].T.astype(out_refs[0].dtype)
            out_refs[0][:, 3 * LANES:] = jnp.zeros((seq, LANES), out_refs[0].dtype)

    slab = lambda b0, step: pl.BlockSpec((seq, LANES), lambda b, g: (b, b0 + step * g))
    if sub == 1:
        out_specs = [slab(0, 1)] * 3
        out_shape = [jax.ShapeDtypeStruct((rows, N_HEADS * LANES), out_dtype)] * 3
    else:
        out_specs = [pl.BlockSpec((seq, 4 * LANES), lambda b, g: (b, g))]
        out_shape = [jax.ShapeDtypeStruct((rows, groups * 4 * LANES), out_dtype)]
    res = pl.pallas_call(
        body, name=name,
        grid=(batch, groups),
        in_specs=[slab(qb0, stride), slab(kb0, stride), slab(vb0, stride), slab(0, 1), slab(0, 1),
                  pl.BlockSpec((seq, sub * LANES), lambda b, g: (b, g))] + _table_specs(tables, sub)
        + [pl.BlockSpec(memory_space=pl.ANY)] * (after is not None),
        out_specs=out_specs, out_shape=out_shape,
        scratch_shapes=[pltpu.VMEM((seq, LANES), BF16)] * 2 + [pltpu.VMEM((LANES, seq), F32)] * 2
        + [pltpu.VMEM((seq, LANES), BF16)] * (2 * sub) + [pltpu.VMEM((LANES, seq), BF16)] * (2 * sub),
        compiler_params=_params(("arbitrary", "arbitrary")),
    )(q, k, v, o, do, lse, *tables, *([after] if after is not None else []))
    return res if sub == 1 else res[0]


def _attention_tables(seq):
    n = seq // ATT_T
    pos = jnp.arange(ATT_T, dtype=jnp.int32)
    dist = jnp.arange(n, dtype=jnp.int32)[:, None, None] * ATT_T + pos[None, :, None] - pos[None, None, :]
    causal = jnp.where(dist[:1] >= 0, 0.0, NEG).astype(F32)
    count = jnp.zeros(dist.shape, F32)
    for window, dilation in DIL_PATTERNS:
        count += ((dist >= 0) & (dist <= window) & (dist % dilation == 0)).astype(F32)
    held = jnp.where(count > 0, jnp.log2(jnp.maximum(count, 1.0)), NEG).astype(F32)
    slopes = jnp.asarray([2.0 ** (-8.0 * (i + 1) / N_HEADS) for i in range(N_HEADS)], F32)
    slopes = jnp.broadcast_to(slopes[:, None, None], (N_HEADS, 1, LANES))
    turned = lambda a: jnp.swapaxes(a, 1, 2)
    far = dist.astype(F32) * LOG2E
    return ((causal,), (turned(causal),)), ((held, far, slopes), (turned(held), turned(far), slopes))


def _pad_heads(w, width):
    kdim, n = w.shape[0], w.shape[1] // width
    return jnp.pad(w.reshape(kdim, n, width), ((0, 0), (0, 0), (0, LANES - width))).reshape(kdim, n * LANES)


def _unpad_heads(w, width):
    kdim, n = w.shape[0], w.shape[1] // LANES
    return w.reshape(kdim, n, LANES)[:, :, :width].reshape(kdim, n * width)


def _pad_head_rows(w, width):
    n, kdim = w.shape[0] // width, w.shape[1]
    return jnp.pad(w.reshape(n, width, kdim), ((0, 0), (0, LANES - width), (0, 0))).reshape(n * LANES, kdim)


def _unpad_head_rows(w, width):
    n, kdim = w.shape[0] // LANES, w.shape[1]
    return w.reshape(n, LANES, kdim)[:, :width].reshape(n * width, kdim)


def _pad_w_in_t(wt):
    n_qkv, pair = 3 * N_HEADS * DIL_DIM, 2 * DIL_DIM
    zeros = lambda n: jnp.zeros((n, wt.shape[1]), wt.dtype)
    main = jnp.concatenate([wt[:LORA_W], zeros(KR_LANE), wt[LORA_W:LORA_W + ROPE], zeros(P_GATE - P_KR - KR_LANE - ROPE),
                            wt[LORA_W + ROPE + n_qkv:]], axis=0)
    qkv = wt[LORA_W + ROPE:LORA_W + ROPE + n_qkv].reshape(3, N_HEADS // 2, pair, wt.shape[1]).transpose(1, 0, 2, 3)
    dil = jnp.pad(qkv, ((0, 0), (0, 1), (0, 0), (0, 0))).reshape(P_DIL, wt.shape[1])
    return main, dil


def _unpad_w_in_t(gt):
    qkv = gt[P_HALF:].reshape(N_HEADS // 2, 4, 2 * DIL_DIM, gt.shape[1])[:, :3].transpose(1, 0, 2, 3).reshape(3 * N_HEADS * DIL_DIM, gt.shape[1])
    return jnp.concatenate([gt[P_LORA:P_KR], gt[P_KR + KR_LANE:P_KR + KR_LANE + ROPE], qkv, gt[P_GATE:P_HALF]], axis=0)


def _split_ukv(w):
    w3 = w.reshape(w.shape[0], N_HEADS, NOPE + V_DIM)
    return (_pad_heads(w3[:, :, :NOPE].reshape(w.shape[0], -1), NOPE),
            _pad_heads(w3[:, :, NOPE:].reshape(w.shape[0], -1), V_DIM))


def _merge_ukv(g_k, g_v):
    kdim = g_k.shape[0]
    k3 = _unpad_heads(g_k, NOPE).reshape(kdim, N_HEADS, NOPE)
    v3 = _unpad_heads(g_v, V_DIM).reshape(kdim, N_HEADS, V_DIM)
    return jnp.concatenate([k3, v3], axis=2).reshape(kdim, N_HEADS * (NOPE + V_DIM))


def _pad_rows(w, width):
    return _pad_heads(w.T, width).T


def _unpad_rows(g, width):
    return _unpad_heads(g.T, width).T


def _join_cols(w):
    return w.transpose(1, 0, 2).reshape(w.shape[1], N_CHIPS * w.shape[2])


def _split_cols(g):
    return g.reshape(g.shape[0], N_CHIPS, g.shape[1] // N_CHIPS).transpose(1, 0, 2)


def _local_step(x3, target3, wg, b_gate, g_q_a, g_kv_a, ln1_g, ln1_b, ln2_g, ln2_b, token=None, late_arrived=None, late_weights=None,
                early_grads=None, early_grads_go=None, last_grads=None, tables=None):
    w_main_t, w_dil_t = _pad_w_in_t(wg["w_in"].reshape(IN_WIDTH, D_MODEL))
    w_uq_pt = _pad_head_rows(wg["w_uq"].reshape(N_HEADS * MLA_QK, Q_LORA), MLA_QK)
    w_ukv = _join_cols(wg["w_ukv"])
    batch, seq, _ = x3.shape
    rows = batch * seq
    x = x3.reshape(rows, D_MODEL)
    target = target3.reshape(rows, D_MODEL)
    row = functools.partial(_rowwise, rows=rows, seq=seq)
    mm = _matmul

    w_uk_p, w_uv_p = _split_ukv(w_ukv)
    b0, b1 = b_gate[0:1], b_gate[1:2]
    rope_c, rope_up, rope_dn, rope_c_only = _rope_tables(seq)
    (mla_bwd_tables, mla_fwd_tables), (dil_bwd_tables, dil_fwd_tables) = _attention_tables(seq) if tables is None else tables
    scale_mla, scale_dil = MLA_QK ** -0.5, DIL_DIM ** -0.5
    lora0, kr0, gate0 = P_LORA // LORA_W, P_KR // LANES, P_GATE // D_MODEL

    proj = mm(x, w_main_t, mode="nt", name="proj", tm=1024, tn=1536, tk=1024, after=token)
    proj_d = mm(x, w_dil_t, mode="nt", name="proj_dil", tm=1024, tn=1024, tk=1024, out_dtypes=(BF16,))

    def prep(lora, gq, gkv):
        return _rms(lora[:, :Q_LORA], gq), _rms(lora[:, Q_LORA:], gkv)

    qn, kvn = row(prep, name="mla_rms", ins=[(proj, LORA_W, lora0, "row"), (g_q_a, 0, 0, "full"), (g_kv_a, 0, 0, "full")],
                  outs=[(Q_LORA, BF16), (KV_LORA, BF16)])
    q_lin = mm(qn, w_uq_pt, mode="nt", name="q_up", tm=1024, tn=1024, tk=Q_LORA)
    k_lin = mm(kvn, w_uk_p, mode="nn", name="k_up", tm=1024, tn=1024, tk=KV_LORA)
    v_a = mm(kvn, w_uv_p, mode="nn", name="v_up", tm=1024, tn=1024, tk=KV_LORA, out_dtypes=(BF16,))

    def rope_qk(ql, kl, kr, c, up, dn):
        k_rot = _rope_fwd(kr, c, up, dn)
        qs = [_rope_fwd(ql[:, h * LANES:(h + 1) * LANES], c, up, dn) for h in range(N_HEADS)]
        ks = [kl[:, h * LANES:(h + 1) * LANES] + k_rot for h in range(N_HEADS)]
        return jnp.concatenate(qs, axis=1), jnp.concatenate(ks, axis=1)

    pos = lambda tab: (tab, LANES, 0, "pos")
    q_a, k_a = row(rope_qk, name="rope_qk",
                   ins=[(q_lin, D_MODEL, 0, "row"), (k_lin, D_MODEL, 0, "row"), (proj, LANES, kr0, "row"), pos(rope_c), pos(rope_up), pos(rope_dn)],
                   outs=[(N_HEADS * LANES, BF16), (N_HEADS * LANES, BF16)])
    o_a, lse_a = _attn_fwd(q_a, 0, k_a, 0, v_a, 0, mla_fwd_tables, scale_mla, name="mla_fwd", batch=batch, seq=seq)
    arrived = None if late_arrived is None else late_arrived(o_a)
    o_b, lse_b = _attn_fwd(proj_d, 0, proj_d, 1, proj_d, 2, dil_fwd_tables, scale_dil, name="dil_fwd", batch=batch, seq=seq, sub=2, stride=4, after=arrived)
    late = wg if late_weights is None else late_weights(o_b)
    w_oa_p = _pad_rows(_join_cols(late["w_o_mla"]), V_DIM)
    w_ob = _join_cols(late["w_o_dil"])
    w_out, w_ff1, w_ff2 = late["w_out"].reshape(D_MODEL, D_MODEL), late["w_ff1"], late["w_ff2"].reshape(D_FF, D_MODEL)
    y_a = mm(o_a, w_oa_p, mode="nn", name="o_mla", tm=1024, tn=1024, tk=1024, out_dtypes=(BF16,))
    y_b = mm(o_b, w_ob, mode="nn", name="o_dil", tm=1024, tn=1024, tk=1024, out_dtypes=(BF16,))

    def gate(t0, t1, c0, c1, ya, yb):
        return (jax.nn.sigmoid(t0 + c0) * ya + jax.nn.sigmoid(t1 + c1) * yb,)

    gate_ins = [(proj, D_MODEL, gate0, "row"), (proj, D_MODEL, gate0 + 1, "row"), (b0, 0, 0, "full"), (b1, 0, 0, "full")]
    (u,) = row(gate, name="gate", ins=gate_ins + [(y_a, D_MODEL, 0, "row"), (y_b, D_MODEL, 0, "row")], outs=[(D_MODEL, BF16)])
    mixed = mm(u, w_out, mode="nn", name="mix", tm=1024, tn=1024, tk=1024)

    def ln1(xv, mv, g, b):
        r1 = ALPHA * xv + mv
        xh, _ = _ln_stats(r1)
        return r1, xh * g + b

    r1, h = row(ln1, name="ln1", ins=[(x, D_MODEL, 0, "row"), (mixed, D_MODEL, 0, "row"), (ln1_g, 0, 0, "full"), (ln1_b, 0, 0, "full")],
                outs=[(D_MODEL, F32), (D_MODEL, F32)])

    def relu2(acc):
        r = jnp.maximum(acc, 0.0)
        return acc, r * r

    a_ff, z = mm(h, w_ff1, mode="nn", name="ff1", tm=1024, tn=1024, tk=1024, out_dtypes=(BF16, BF16), epilogue=relu2, b_shards=True)
    f = mm(z, w_ff2, mode="nn", name="ff2", tm=1024, tn=1024, tk=2048)

    def ln2_loss(hv, fv, tv, g, b):
        xh, r = _ln_stats(ALPHA * hv + fv)
        err = xh * g + b - tv
        dy = err * (1.0 / D_MODEL)
        dr2, dg, db = _ln_bwd(xh, r, g, dy)
        loss = jnp.sum(_colsum(err * err), axis=1, keepdims=True) * (0.5 / D_MODEL)
        return dr2, jnp.broadcast_to(loss, (1, LANES)), dg, db

    dr2, loss_l, d_ln2_g, d_ln2_b = row(
        ln2_loss, name="ln2_loss",
        ins=[(h, D_MODEL, 0, "row"), (f, D_MODEL, 0, "row"), (target, D_MODEL, 0, "row"), (ln2_g, 0, 0, "full"), (ln2_b, 0, 0, "full")],
        outs=[(D_MODEL, F32)], sums=[LANES, D_MODEL, D_MODEL])

    d_w_ff2 = mm(z, dr2, mode="tn", name="d_w_ff2", tm=1024, tn=1024, tk=2048)
    da = mm(dr2, w_ff2, mode="nt", name="d_ff_act", tm=1024, tn=1024, tk=1024, out_dtypes=(BF16,), extras=(a_ff,),
            epilogue=lambda acc, av: (acc * (2.0 * jnp.maximum(av.astype(F32), 0.0)),))
    d_w_ff1 = mm(h, da, mode="tn", name="d_w_ff1", tm=1024, tn=1024, tk=2048, out_shards=True)
    dh = mm(da, w_ff1, mode="nt", name="d_h", tm=1024, tn=1024, tk=1024, extras=(dr2,), epilogue=lambda acc, rv: (acc + ALPHA * rv,), b_shards=True)

    def ln1_bwd(dhv, r1v, g):
        xh, r = _ln_stats(r1v)
        return _ln_bwd(xh, r, g, dhv)

    dr1, d_ln1_g, d_ln1_b = row(ln1_bwd, name="ln1_bwd", ins=[(dh, D_MODEL, 0, "row"), (r1, D_MODEL, 0, "row"), (ln1_g, 0, 0, "full")],
                                outs=[(D_MODEL, F32)], sums=[D_MODEL, D_MODEL])
    d_w_out = mm(u, dr1, mode="tn", name="d_w_out", tm=1024, tn=1024, tk=1024)
    du = mm(dr1, w_out, mode="nt", name="d_u", tm=1024, tn=1024, tk=1024, out_dtypes=(BF16,))

    def gate_bwd(t0, t1, c0, c1, ya, yb, duv):
        s0, s1 = jax.nn.sigmoid(t0 + c0), jax.nn.sigmoid(t1 + c1)
        dt0 = duv * ya * s0 * (1.0 - s0)
        dt1 = duv * yb * s1 * (1.0 - s1)
        return duv * s0, duv * s1, jnp.concatenate([dt0, dt1], axis=1), jnp.concatenate([_colsum(dt0), _colsum(dt1)], axis=1)

    dy_a, dy_b, d_gates, d_b_gate = row(
        gate_bwd, name="gate_bwd", ins=gate_ins + [(y_a, D_MODEL, 0, "row"), (y_b, D_MODEL, 0, "row"), (du, D_MODEL, 0, "row")],
        outs=[(D_MODEL, BF16), (D_MODEL, BF16), (2 * D_MODEL, BF16)], sums=[2 * D_MODEL])
    d_w_oa_p = mm(o_a, dy_a, mode="tn", name="d_w_o_mla", tm=1024, tn=1024, tk=1024)
    d_w_ob = mm(o_b, dy_b, mode="tn", name="d_w_o_dil", tm=1024, tn=1024, tk=1024)
    grads = dict(w_o_mla=_split_cols(_unpad_rows(d_w_oa_p, V_DIM)), w_o_dil=_split_cols(d_w_ob),
                 w_out=d_w_out.reshape(N_CHIPS, D_MODEL // N_CHIPS, D_MODEL), w_ff1=d_w_ff1, w_ff2=d_w_ff2.reshape(N_CHIPS, D_FF // N_CHIPS, D_MODEL))
    sent = None if early_grads is None else early_grads(grads)
    do_a = mm(dy_a, w_oa_p, mode="nt", name="d_o_mla", tm=1024, tn=1024, tk=1024, after=sent)
    do_b = mm(dy_b, w_ob, mode="nt", name="d_o_dil", tm=1024, tn=1024, tk=1024)
    dq_a, dk_a, dv_a = _attn_bwd(q_a, 0, k_a, 0, v_a, 0, o_a, do_a, lse_a, mla_bwd_tables, scale_mla,
                                 name="mla_bwd", batch=batch, seq=seq, out_dtype=F32)
    going = None if early_grads_go is None else early_grads_go(dq_a)
    d_qkv_d = _attn_bwd(proj_d, 0, proj_d, 1, proj_d, 2, o_b, do_b, lse_b, dil_bwd_tables, scale_dil,
                        name="dil_bwd", batch=batch, seq=seq, out_dtype=BF16, sub=2, stride=4, after=going)

    def mla_post(dq, dk, c, up, dn, c_only):
        dqs = [_rope_bwd(dq[:, h * LANES:(h + 1) * LANES], c, up, dn) for h in range(N_HEADS)]
        dk_sum = dk[:, :LANES]
        for h in range(1, N_HEADS):
            dk_sum = dk_sum + dk[:, h * LANES:(h + 1) * LANES]
        return jnp.concatenate(dqs, axis=1), _rope_bwd(dk_sum, c_only, up, dn)

    dq_lin, d_kr = row(mla_post, name="mla_unrope",
                       ins=[(dq_a, D_MODEL, 0, "row"), (dk_a, D_MODEL, 0, "row"), pos(rope_c), pos(rope_up), pos(rope_dn), pos(rope_c_only)],
                       outs=[(N_HEADS * LANES, BF16), (LANES, BF16)])
    d_w_uq_pt = mm(dq_lin, qn, mode="tn", name="d_w_uq", tm=1024, tn=Q_LORA, tk=1024)
    d_w_uk_p = mm(kvn, dk_a, mode="tn", name="d_w_uk", tm=KV_LORA, tn=1024, tk=1024)
    d_w_uv_p = mm(kvn, dv_a, mode="tn", name="d_w_uv", tm=KV_LORA, tn=1024, tk=1024)
    d_qn = mm(dq_lin, w_uq_pt, mode="nn", name="d_qn", tm=1024, tn=Q_LORA, tk=1024)
    d_kvn_k = mm(dk_a, w_uk_p, mode="nt", name="d_kvn_k", tm=1024, tn=KV_LORA, tk=1024)
    d_kvn = mm(dv_a, w_uv_p, mode="nt", name="d_kvn", tm=1024, tn=KV_LORA, tk=1024, extras=(d_kvn_k,), epilogue=lambda acc, e: (acc + e,))

    def rms_bwd(lora, dq, dkv, dkr, gq, gkv):
        dxq, dgq = _rms_bwd(lora[:, :Q_LORA], gq, dq)
        dxk, dgk = _rms_bwd(lora[:, Q_LORA:], gkv, dkv)
        tail = jnp.zeros((dxq.shape[0], P_GATE - P_KR - LANES), F32)
        return jnp.concatenate([dxq, dxk, dkr.astype(F32), tail], axis=1), dgq, dgk

    d_tail, d_g_q_a, d_g_kv_a = row(
        rms_bwd, name="mla_rms_bwd",
        ins=[(proj, LORA_W, lora0, "row"), (d_qn, Q_LORA, 0, "row"), (d_kvn, KV_LORA, 0, "row"), (d_kr, LANES, 0, "row"),
             (g_q_a, 0, 0, "full"), (g_kv_a, 0, 0, "full")],
        outs=[(P_GATE, BF16)], sums=[Q_LORA, KV_LORA])
    d_proj = [d_tail, d_gates, d_qkv_d]
    d_w_in_pt = mm(d_proj, x, mode="tn", name="d_w_in", tm=1024, tn=1024, tk=1024)
    grads.update(w_in=_unpad_w_in_t(d_w_in_pt).reshape(N_CHIPS, IN_WIDTH // N_CHIPS, D_MODEL),
                 w_uq=_unpad_head_rows(d_w_uq_pt, MLA_QK).reshape(N_CHIPS, N_HEADS * MLA_QK // N_CHIPS, Q_LORA),
                 w_ukv=_split_cols(_merge_ukv(d_w_uk_p, d_w_uv_p)))
    leaving = None if last_grads is None else last_grads(grads)
    grad_x = mm(d_proj, jnp.concatenate([w_main_t, w_dil_t], axis=0), mode="nn", name="d_x", tm=1024, tn=1024, tk=1024, extras=(dr1,), epilogue=lambda acc, rv: (acc + ALPHA * rv,),
                after=leaving)

    grads.update(
        b_gate=d_b_gate.reshape(2, D_MODEL), g_q_a=d_g_q_a, g_kv_a=d_g_kv_a, ln1_g=d_ln1_g, ln1_b=d_ln1_b, ln2_g=d_ln2_g, ln2_b=d_ln2_b)
    return loss_l, grad_x.reshape(batch, seq, D_MODEL), grads


BIG = ("w_in", "w_uq", "w_ukv", "w_o_mla", "w_o_dil", "w_out", "w_ff1", "w_ff2")
SMALL = (("b_gate", 2 * D_MODEL), ("g_q_a", Q_LORA), ("g_kv_a", KV_LORA), ("ln1_g", D_MODEL), ("ln1_b", D_MODEL),
         ("ln2_g", D_MODEL), ("ln2_b", D_MODEL))
TRANSPOSED = ("w_in", "w_uq")
D2D_PIECES = (4, 2, 1)
ANY = pl.BlockSpec(memory_space=pl.ANY)
SIDE_EFFECTS = pltpu.CompilerParams(has_side_effects=True)


def _place():
    x, y, c = lax.axis_index("x"), lax.axis_index("y"), lax.axis_index("c")
    return x, y, c, ((1 - x, y), (x, 1 - y), (1 - x, 1 - y))


def _half_axis(shape):
    return 0 if shape[0] % 32 == 0 else 1


def _half_shape(shape):
    return (shape[0] // 2, shape[1]) if _half_axis(shape) == 0 else (shape[0], shape[1] // 2)


def _window(ref, lead, shape, which=None, pieces=False):
    axis = _half_axis(shape)
    size = shape[axis] if which is None else shape[axis] // 2
    base = 0 if which is None else which * size
    tile = (16, LANES)[axis]
    count = next(c for c in D2D_PIECES if size % (tile * c) == 0) if pieces else 1
    step = size // count
    spans = [pl.ds(pl.multiple_of(base + i * step, tile), step) for i in range(count)]
    refs = [ref.at[(*lead, s)] if axis == 0 else ref.at[(*lead, slice(None), s)] for s in spans]
    return refs if pieces else refs[0]


def _remote(src, dst, send, recv, to):
    return pltpu.make_async_remote_copy(src_ref=src, dst_ref=dst, send_sem=send, recv_sem=recv, device_id=to, device_id_type=MESH)


def _pair_split(grads, name):
    n = len(grads)

    def body(*refs):
        srcs, outs, (send, recv) = refs[:n], refs[n:2 * n], refs[2 * n:]
        x, y, c, _ = _place()
        for t in range(n):
            for s in range(N_CHIPS):
                _remote(_window(srcs[t], (s,), grads[t].shape[1:], 1 - c), outs[t].at[s], send.at[t], recv.at[t], (x, y, 1 - c)).start()
        for t in range(n):
            _remote(_window(srcs[t], (slice(None),), grads[t].shape[1:], 1 - c), outs[t], send.at[t], recv.at[t], (x, y, 1 - c)).wait()

    return pl.pallas_call(
        body, name=name, in_specs=[ANY] * n, out_specs=[ANY] * n,
        out_shape=[jax.ShapeDtypeStruct((N_CHIPS,) + _half_shape(g.shape[1:]), g.dtype) for g in grads],
        scratch_shapes=[pltpu.SemaphoreType.DMA((n,)), pltpu.SemaphoreType.DMA((n,))],
        compiler_params=SIDE_EFFECTS,
    )(*grads)


HBM = pl.BlockSpec(memory_space=pltpu.HBM)
SEM = pl.BlockSpec(memory_space=pltpu.SEMAPHORE)
SPLIT = pltpu.CompilerParams(has_side_effects=pltpu.SideEffectType.DATAFLOW_SIDE_EFFECTING)


def _in_hbm(a):
    return pltpu.with_memory_space_constraint(a, pltpu.HBM)


def _split_copies(kind, srcs, lands):
    x, y, c, chips = _place()
    out = []
    for t in range(len(srcs)):
        if kind == "pair":
            out += [(t, s % 3, _window(srcs[t], (s,), srcs[t].shape[1:], 1 - c), lands[t].at[s], (x, y, 1 - c)) for s in range(N_CHIPS)]
            continue
        if kind == "join":
            out += [(t, 0, a, b, (x, y, 1 - c)) for a, b in zip(_window(srcs[t], (), srcs[t].shape, None, True), _window(lands[t], (), srcs[t].shape, None, True))]
            continue
        if kind == "forward":
            shape, sibling = srcs[t].shape, (x, y, 1 - c)
            out += [(t, 0, a, b, sibling) for a, b in zip(_window(srcs[t], (), shape, None, True), _window(lands[t], (2 * x + y,), shape, None, True))]
            out += [(t, j, a, a, sibling) for j, (cx, cy) in enumerate(chips) for a in _window(lands[t], (2 * cx + cy,), shape, c, True)]
            continue
        for j, (cx, cy) in enumerate(chips):
            if kind == "gather":
                shape = srcs[t].shape
                out.append((t, j, _window(srcs[t], (), shape, c), _window(lands[t], (2 * x + y,), shape, c), (cx, cy, c)))
            else:
                out.append((t, j, srcs[t].at[2 * cx + cy], lands[t].at[j], (cx, cy, c)))
    return out


def _split_start(kind, srcs, land_shapes, name, lands=None, after=None):
    n = len(srcs)

    def body(*refs):
        src_refs, land_refs, sems, token = refs[:n], refs[n:2 * n], refs[-7 - 2 * n:-1 - 2 * n], refs[-1]
        for t, j, s, d, to in _split_copies(kind, src_refs, land_refs):
            _remote(s, d, sems[j], sems[3 + j], to).start()
        token[...] = jnp.zeros_like(token)

    lands = [_in_hbm(lax.empty(s.shape, s.dtype)) for s in land_shapes] if lands is None else list(lands)
    thru = [pltpu.HBM(a.shape, a.dtype) for a in list(srcs) + lands]
    res = pl.pallas_call(
        body, name=name,
        out_shape=(*[pltpu.SemaphoreType.DMA(())] * 6, *thru, jax.ShapeDtypeStruct((8, LANES), F32)),
        in_specs=[HBM] * (2 * n) + [ANY] * (after is not None), out_specs=(*[SEM] * 6, *[HBM] * (2 * n), pl.BlockSpec(memory_space=pltpu.VMEM)),
        input_output_aliases={i: 6 + i for i in range(2 * n)}, compiler_params=SPLIT,
    )(*[_in_hbm(s) for s in srcs], *lands, *([after] if after is not None else []))
    return res[:6], res[6:6 + n], res[6 + n:6 + 2 * n], res[-1]


def _split_wait(kind, sems, srcs, lands, after, name):
    n = len(srcs)

    def body(*refs):
        src_refs, land_refs, sem_refs = refs[:n], refs[n:2 * n], refs[2 * n:2 * n + 6]
        for t, j, s, d, to in _split_copies(kind, src_refs, land_refs):
            cp = _remote(s, d, sem_refs[j], sem_refs[3 + j], to)
            cp.wait_send()
            cp.wait_recv()

    res = pl.pallas_call(
        body, name=name, out_shape=[pltpu.HBM(a.shape, a.dtype) for a in list(srcs) + list(lands)],
        in_specs=[HBM] * (2 * n) + [SEM] * 6 + [ANY], out_specs=[HBM] * (2 * n),
        input_output_aliases={i: i for i in range(2 * n)}, compiler_params=SPLIT,
    )(*srcs, *lands, *sems, after)
    return res[:n], res[n:]


def _sum_all_devices(vec, name):
    n_rows = vec.shape[0]

    def body(v_ref, out_ref, buf, send, recv):
        x, y, c, _ = _place()
        me = 4 * x + 2 * y + c
        buf[me] = v_ref[...]
        flips = [(a, b, d) for a in (0, 1) for b in (0, 1) for d in (0, 1)][1:]
        copies = []
        for r, (a, b, d) in enumerate(flips):
            px, py, pc = (1 - x if a else x), (1 - y if b else y), (1 - c if d else c)
            copies.append(pltpu.make_async_remote_copy(src_ref=v_ref, dst_ref=buf.at[me], send_sem=send.at[r], recv_sem=recv.at[r],
                                                       device_id=(px, py, pc), device_id_type=MESH))
            copies[-1].start()
        for r, (a, b, d) in enumerate(flips):
            px, py, pc = (1 - x if a else x), (1 - y if b else y), (1 - c if d else c)
            pltpu.make_async_remote_copy(src_ref=v_ref, dst_ref=buf.at[4 * px + 2 * py + pc], send_sem=send.at[r], recv_sem=recv.at[r],
                                         device_id=(px, py, pc), device_id_type=MESH).wait_recv()
        for cp in copies:
            cp.wait_send()
        total = buf[0]
        for k in range(1, N_DEV):
            total = total + buf[k]
        out_ref[...] = total

    vmem = pl.BlockSpec(memory_space=pltpu.VMEM)
    return pl.pallas_call(
        body, name=name, in_specs=[vmem], out_specs=vmem, out_shape=jax.ShapeDtypeStruct(vec.shape, F32),
        scratch_shapes=[pltpu.VMEM((N_DEV, n_rows, LANES), F32), pltpu.SemaphoreType.DMA((N_DEV - 1,)), pltpu.SemaphoreType.DMA((N_DEV - 1,))],
        compiler_params=pltpu.CompilerParams(has_side_effects=True),
    )(vec)


def _half_tile(half, width):
    t = half
    while t * width * 4 > (2 << 20) and t % 32 == 0:
        t //= 2
    return t


def _pair_add(g, theirs, core, name):
    _, half, width = theirs.shape
    t = _half_tile(half, width)
    n = half // t

    def body(c_ref, a_ref, b_ref, o_ref):
        o_ref[...] = (a_ref[...] + b_ref[...]).astype(BF16)

    tile = pl.BlockSpec((1, t, width), lambda j, i, c_ref: (j, i, 0))
    if _half_axis(g.shape[1:]) == 0:
        mine = pl.BlockSpec((1, t, width), lambda j, i, c_ref: (j, c_ref[0] * n + i, 0))
    else:
        mine = pl.BlockSpec((1, t, width), lambda j, i, c_ref: (j, i, c_ref[0]))
    return pl.pallas_call(
        body, name=name,
        grid_spec=pltpu.PrefetchScalarGridSpec(num_scalar_prefetch=1, grid=(N_CHIPS, n), in_specs=[mine, tile], out_specs=tile),
        out_shape=jax.ShapeDtypeStruct(theirs.shape, BF16), compiler_params=_params(("parallel", "parallel")),
    )(core, g, theirs)


def _chip_sum(part, others, chip, name, after=None):
    _, half, width = part.shape
    t = _half_tile(half, width)

    def body(s_ref, mine, p0, p1, p2, *rest):
        o_ref = rest[-1]
        o_ref[...] = ((mine[0].astype(F32) + p0[0].astype(F32)) + p1[0].astype(F32)) + p2[0].astype(F32)

    return pl.pallas_call(
        body, name=name,
        grid_spec=pltpu.PrefetchScalarGridSpec(
            num_scalar_prefetch=1, grid=(half // t,),
            in_specs=[pl.BlockSpec((1, t, width), lambda i, s_ref: (s_ref[0], i, 0))]
            + [pl.BlockSpec((1, t, width), lambda i, s_ref, j=j: (j, i, 0)) for j in range(3)] + [pl.BlockSpec(memory_space=pl.ANY)] * (after is not None),
            out_specs=pl.BlockSpec((t, width), lambda i, s_ref: (i, 0))),
        out_shape=jax.ShapeDtypeStruct((half, width), F32), compiler_params=_params(("parallel",)),
    )(chip, part, others, others, others, *([after] if after is not None else []))


EARLY = ("w_in", "w_uq", "w_ukv")
LATE = ("w_o_mla", "w_o_dil", "w_out", "w_ff1", "w_ff2")


def _chip_partials(grads, names, core, tag):
    gs = [grads[n] for n in names]
    theirs = _pair_split(gs, "pair_split_" + tag)
    return [_pair_add(g, th, core, "pair_add_" + n) for g, th, n in zip(gs, theirs, names)]


def _sum_small(vals):
    n_in = len(vals)
    n_rows = sum(a.shape[0] * a.shape[1] // LANES for a in vals)
    pad_rows = -(-n_rows // 8) * 8

    def chunks(refs):
        return [(ref, a, j) for ref in refs for a in range(ref.shape[0]) for j in range(ref.shape[1] // LANES)]

    def body(*refs):
        ins, outs, (buf, send, recv) = refs[:n_in], refs[n_in:2 * n_in], refs[2 * n_in:]
        x, y, c, _ = _place()
        me = 4 * x + 2 * y + c
        for r, (ref, a, j) in enumerate(chunks(ins)):
            buf[me, r:r + 1, :] = ref[a:a + 1, j * LANES:(j + 1) * LANES]
        if pad_rows > n_rows:
            buf[me, n_rows:pad_rows, :] = jnp.zeros((pad_rows - n_rows, LANES), F32)
        flips = [(a, b, d) for a in (0, 1) for b in (0, 1) for d in (0, 1)][1:]
        peers = [((1 - x if a else x), (1 - y if b else y), (1 - c if d else c)) for a, b, d in flips]
        copies = [_remote(buf.at[me], buf.at[me], send.at[r], recv.at[r], peer) for r, peer in enumerate(peers)]
        for cp in copies:
            cp.start()
        for r, (px, py, pc) in enumerate(peers):
            _remote(buf.at[me], buf.at[4 * px + 2 * py + pc], send.at[r], recv.at[r], (px, py, pc)).wait_recv()
        for cp in copies:
            cp.wait_send()
        total = buf[0]
        for k in range(1, N_DEV):
            total = total + buf[k]
        for r, (ref, a, j) in enumerate(chunks(outs)):
            ref[a:a + 1, j * LANES:(j + 1) * LANES] = total[r:r + 1, :]

    vmem = pl.BlockSpec(memory_space=pltpu.VMEM)
    return pl.pallas_call(
        body, name="sum_small", in_specs=[vmem] * n_in, out_specs=[vmem] * n_in,
        out_shape=[jax.ShapeDtypeStruct(a.shape, F32) for a in vals],
        scratch_shapes=[pltpu.VMEM((N_DEV, pad_rows, LANES), F32), pltpu.SemaphoreType.DMA((N_DEV - 1,)), pltpu.SemaphoreType.DMA((N_DEV - 1,))],
        compiler_params=SIDE_EFFECTS,
    )(*vals)


def _adam_math(w, g, m, v):
    nm = B1 * m + (1.0 - B1) * g
    nv = B2 * v + (1.0 - B2) * (g * g)
    m_hat = nm / (1.0 - B1 ** ADAM_STEP)
    v_hat = nv / (1.0 - B2 ** ADAM_STEP)
    return -LR * (m_hat / (jnp.sqrt(v_hat) + ADAM_EPS) + WD * w), nm, nv


def _adamw_big(w, mine, theirs, m, v, core, name, side_by_side=False):
    rows, width = w.shape
    if side_by_side:
        t = next(c for c in (152, 96, 64, 32, 16, 8) if rows % c == 0)
        hb = None
        half_spec = pl.BlockSpec((t, width // 2), lambda i, c_ref: (i, 0))
    else:
        t = next(c for c in (256, 128, 64, 32, 16, 8) if (rows // 2) % c == 0)
        hb = rows // 2 // t
        half_spec = pl.BlockSpec((t, width), lambda i, c_ref: (i % hb, 0))

    def body(c_ref, w_ref, a_ref, b_ref, m_ref, v_ref, g_ref, d_ref, nm_ref, nv_ref):
        south = c_ref[0] == 0
        if side_by_side:
            g = jnp.where(south, jnp.concatenate([a_ref[...], b_ref[...]], axis=1), jnp.concatenate([b_ref[...], a_ref[...]], axis=1))
        else:
            g = jnp.where((pl.program_id(0) < hb) == south, a_ref[...], b_ref[...])
        g_ref[...] = g
        d_ref[...], nm_ref[...], nv_ref[...] = _adam_math(w_ref[...], g, m_ref[...], v_ref[...])

    spec = pl.BlockSpec((t, width), lambda i, c_ref: (i, 0))
    return pl.pallas_call(
        body, name=name,
        grid_spec=pltpu.PrefetchScalarGridSpec(num_scalar_prefetch=1, grid=(rows // t,),
                                               in_specs=[spec, half_spec, half_spec, spec, spec], out_specs=[spec] * 4),
        out_shape=[jax.ShapeDtypeStruct(w.shape, F32)] * 4, compiler_params=_params(("parallel",)),
    )(core, w, mine, theirs, m, v)


def _adamw_small(ws, gs, ms, vs):
    n = len(ws)

    def body(*refs):
        for t in range(n):
            w_ref, g_ref, m_ref, v_ref = (refs[k * n + t] for k in range(4))
            d, nm, nv = _adam_math(w_ref[...], g_ref[...], m_ref[...], v_ref[...])
            refs[4 * n + t][...] = d
            refs[5 * n + t][...] = nm
            refs[6 * n + t][...] = nv

    vmem = pl.BlockSpec(memory_space=pltpu.VMEM)
    res = pl.pallas_call(body, name="adamw_small", in_specs=[vmem] * (4 * n), out_specs=[vmem] * (3 * n),
                         out_shape=[jax.ShapeDtypeStruct(a.shape, F32) for a in ws] * 3)(*ws, *gs, *ms, *vs)
    return res[:n], res[n:2 * n], res[2 * n:]


def kernel(x, w_in, b_gate, g_q_a, w_uq, g_kv_a, w_ukv, w_o_mla, w_o_dil, w_out, ln1_g, ln1_b, w_ff1, w_ff2, ln2_g, ln2_b, loss_target, m_w_in, m_b_gate, m_g_q_a, m_w_uq, m_g_kv_a, m_w_ukv, m_w_o_mla, m_w_o_dil, m_w_out, m_ln1_g, m_ln1_b, m_w_ff1, m_w_ff2, m_ln2_g, m_ln2_b, v_w_in, v_b_gate, v_g_q_a, v_w_uq, v_g_kv_a, v_w_ukv, v_w_o_mla, v_w_o_dil, v_w_out, v_ln1_g, v_ln1_b, v_w_ff1, v_w_ff2, v_ln2_g, v_ln2_b):
    order = ("w_in", "b_gate", "g_q_a", "w_uq", "g_kv_a", "w_ukv", "w_o_mla", "w_o_dil", "w_out", "ln1_g", "ln1_b", "w_ff1", "w_ff2", "ln2_g", "ln2_b")
    w = dict(w_in=w_in, b_gate=b_gate, g_q_a=g_q_a, w_uq=w_uq, g_kv_a=g_kv_a, w_ukv=w_ukv, w_o_mla=w_o_mla, w_o_dil=w_o_dil, w_out=w_out,
             ln1_g=ln1_g, ln1_b=ln1_b, w_ff1=w_ff1, w_ff2=w_ff2, ln2_g=ln2_g, ln2_b=ln2_b)
    m = dict(w_in=m_w_in, b_gate=m_b_gate, g_q_a=m_g_q_a, w_uq=m_w_uq, g_kv_a=m_g_kv_a, w_ukv=m_w_ukv, w_o_mla=m_w_o_mla, w_o_dil=m_w_o_dil,
             w_out=m_w_out, ln1_g=m_ln1_g, ln1_b=m_ln1_b, w_ff1=m_w_ff1, w_ff2=m_w_ff2, ln2_g=m_ln2_g, ln2_b=m_ln2_b)
    v = dict(w_in=v_w_in, b_gate=v_b_gate, g_q_a=v_g_q_a, w_uq=v_w_uq, g_kv_a=v_g_kv_a, w_ukv=v_w_ukv, w_o_mla=v_w_o_mla, w_o_dil=v_w_o_dil,
             w_out=v_w_out, ln1_g=v_ln1_g, ln1_b=v_ln1_b, w_ff1=v_w_ff1, w_ff2=v_w_ff2, ln2_g=v_ln2_g, ln2_b=v_ln2_b)
    chip = 2 * lax.axis_index("x") + lax.axis_index("y")
    south = (lax.axis_index("c") == 0).astype(F32)
    gate_w = D_MODEL // N_CHIPS

    core = lax.axis_index("c").astype(jnp.int32).reshape(1)
    turn = lambda n, a: a.T if n in TRANSPOSED else a
    shards = {n: turn(n, w[n][0]).astype(BF16) for n in BIG}
    early_shards, late_shards = [shards[n] for n in EARLY], [shards[n] for n in LATE]
    gathered = lambda group: [jax.ShapeDtypeStruct((N_CHIPS,) + s.shape, BF16) for s in group]
    e_sems, e_srcs, e_lands, e_token = _split_start("gather", early_shards, gathered(early_shards), "gather_early_start")
    g_sems, g_srcs, g_lands, g_token = _split_start("gather", late_shards, gathered(late_shards), "gather_late_start", after=e_token)
    tables = _attention_tables(x.shape[1])
    e_srcs, e_lands = _split_wait("gather", e_sems, e_srcs, e_lands, tables[1][0][1], "gather_early_wait")
    e_forward = _split_start("forward", e_srcs, None, "gather_early_forward_start", lands=e_lands)
    first = dict(zip(EARLY, _split_wait("forward", *e_forward[:3], e_forward[-1], "gather_early_forward_wait")[1]))
    b_mine = lax.dynamic_update_slice(jnp.zeros((2, D_MODEL), F32), b_gate[0] * south, (0, chip * gate_w))
    b_full = _sum_all_devices(b_mine.reshape(-1, LANES), "gather_b_gate").reshape(2, D_MODEL)

    sent = {}

    def late_arrived(after):
        srcs, lands = _split_wait("gather", g_sems, g_srcs, g_lands, after, "gather_late_wait")
        sent["forward"] = _split_start("forward", srcs, None, "gather_late_forward_start", lands=lands)
        return sent["forward"][-1]

    def late_weights(after):
        return dict(zip(LATE, _split_wait("forward", *sent["forward"][:3], after, "gather_late_forward_wait")[1]))

    exchange_shapes = lambda parts: [jax.ShapeDtypeStruct((3,) + p.shape[1:], BF16) for p in parts]

    def early_grads(grads_late):
        gs = [grads_late[n] for n in LATE]
        shapes = [jax.ShapeDtypeStruct((N_CHIPS,) + _half_shape(g.shape[1:]), F32) for g in gs]
        sent["pair"] = _split_start("pair", gs, shapes, "pair_split_late_start")
        return sent["pair"][-1]

    def early_grads_go(after):
        gs, theirs = _split_wait("pair", *sent["pair"][:3], after, "pair_split_late_wait")
        parts = [_pair_add(g, th, core, "pair_add_" + n) for g, th, n in zip(gs, theirs, LATE)]
        sent["late"] = _split_start("scatter", parts, exchange_shapes(parts), "exchange_late_start")
        return sent["late"][-1]

    def last_grads(grads_early):
        parts = _chip_partials(grads_early, EARLY, core, "early")
        sent["early"] = _split_start("scatter", parts, exchange_shapes(parts), "exchange_early_start")
        return sent["early"][-1]

    loss_part, grad_x, grads = _local_step(x, loss_target, first, b_full, g_q_a, g_kv_a, ln1_g, ln1_b, ln2_g, ln2_b, token=g_token,
                                           late_arrived=late_arrived, late_weights=late_weights, early_grads=early_grads, early_grads_go=early_grads_go,
                                           last_grads=last_grads, tables=tables)

    g_out, delta, new_m, new_v = {}, {}, {}, {}
    chip1 = chip.astype(jnp.int32).reshape(1)

    def sum_and_send(names, parts, others, tag):
        totals = [_chip_sum(p, o, chip1, "chip_sum_" + n) for n, p, o in zip(names, parts, others)]
        return _split_start("join", totals, [jax.ShapeDtypeStruct(t.shape, F32) for t in totals], "pair_join_" + tag + "_start")

    def adam(names, joined, after, tag):
        totals, halves = _split_wait("join", *joined[:3], after, "pair_join_" + tag + "_wait")
        for n, mine, theirs in zip(names, totals, halves):
            res = _adamw_big(turn(n, w[n][0]), mine, theirs, turn(n, m[n][0]), turn(n, v[n][0]), core, "adamw_" + n,
                             side_by_side=mine.shape[0] == shards[n].shape[0])
            g_out[n], delta[n], new_m[n], new_v[n] = (turn(n, r) for r in res)

    late_joined = sum_and_send(LATE, *_split_wait("scatter", *sent["late"][:3], grad_x, "exchange_late_wait"), "late")
    early_joined = sum_and_send(EARLY, *_split_wait("scatter", *sent["early"][:3], late_joined[-1], "exchange_early_wait"), "early")
    small_names = [name for name, _ in SMALL]
    sums = _sum_small([grads[name] for name in small_names] + [loss_part])
    loss = sums[-1][0, 0]
    g_small = dict(zip(small_names, sums))
    g_small["b_gate"] = lax.dynamic_slice(g_small["b_gate"], (0, chip * gate_w), (2, gate_w))
    flat = lambda a: a.reshape(-1, a.shape[-1])
    res = _adamw_small(*[[flat(d[name]) for name in small_names] for d in (w, g_small, m, v)])
    g_out.update(g_small)
    for d, r in zip((delta, new_m, new_v), res):
        d.update(zip(small_names, r))
    adam(LATE, late_joined, res[0][0], "late")
    adam(EARLY, early_joined, delta[LATE[-1]], "early")

    lead = lambda d: [d[name].reshape(w[name].shape) for name in order]
    return (loss, grad_x, *lead(g_out), *lead(delta), *lead(new_m), *lead(new_v))
```

```python
import functools
import math

import jax
import jax.numpy as jnp
from jax import lax
from jax.experimental import pallas as pl
from jax.experimental.pallas import tpu as pltpu

F32 = jnp.float32
BF16 = jnp.bfloat16
MESH = pl.DeviceIdType.MESH

D_MODEL = 1024
N_HEADS = 8
LANES = 128
NOPE, ROPE, V_DIM = 64, 32, 64
MLA_QK = NOPE + ROPE
Q_LORA, KV_LORA = 384, 256
DIL_DIM = 64
DIL_PATTERNS = ((128, 1), (512, 4), (2048, 16))
D_FF = 4096
N_CHIPS = 4
N_DEV = 8
IN_WIDTH = 4256
LN_EPS, RMS_EPS = 1e-5, 1e-6
NEG = -1e30
LOG2E, LN2 = 1.4426950408889634, 0.6931471805599453
ALPHA = 2.0 ** 0.25
ROPE_THETA = 10000.0
LR, B1, B2, ADAM_EPS, WD, ADAM_STEP = 0.001, 0.9, 0.999, 1e-8, 0.01, 10

P_LORA, P_KR, P_GATE, P_HALF = 0, 640, 1024, 3072
DIL_GROUP = 4 * LANES
P_DIL = N_HEADS // 2 * DIL_GROUP
LORA_W = Q_LORA + KV_LORA
KR_LANE = NOPE

ATT_T = 512
ROW_T = 512
VMEM_LIMIT = 56 * 1024 * 1024

NN = (((1,), (0,)), ((), ()))
NT = (((1,), (1,)), ((), ()))
TN = (((0,), (0,)), ((), ()))


def _params(sem=None, **kw):
    return pltpu.CompilerParams(dimension_semantics=sem, vmem_limit_bytes=VMEM_LIMIT, **kw)


def _matmul(a, b, *, mode, name, tm, tn, tk, out_dtypes=(F32,), extras=(), epilogue=None, b_shards=False, out_shards=False, after=None):
    pieces = list(a) if isinstance(a, (list, tuple)) else [a]
    n_pc = len(pieces)
    a_shape = (pieces[0].shape[0], sum(p.shape[1] for p in pieces))
    if b_shards:
        n_sh, rows_b, cols_b = b.shape
        b_shape = (rows_b, n_sh * cols_b)
    else:
        b_shape = b.shape
    if mode == "nn":
        (m, k), (k2, n) = a_shape, b_shape
    elif mode == "nt":
        (m, k), (n, k2) = a_shape, b_shape
    else:
        (k, m), (k2, n) = a_shape, b_shape
    assert k == k2, (a_shape, b.shape, mode)
    tm, tn, tk = min(tm, m), min(tn, n), min(tk, k)
    assert m % tm == 0 and n % tn == 0 and k % tk == 0, (name, m, n, k, tm, tn, tk)
    nk = k // tk
    n_ex, n_out = len(extras), len(out_dtypes)
    n_in = n_pc + 1 + n_ex + (after is not None)
    dims = {"nn": NN, "nt": NT, "tn": TN}[mode]
    col_tile = tm if mode == "tn" else tk
    blocks = [p.shape[1] // col_tile for p in pieces]
    firsts = [sum(blocks[:p]) for p in range(n_pc)]
    assert all(p.shape[1] % col_tile == 0 for p in pieces), (name, col_tile)

    def body(*refs):
        a_refs, b_ref = refs[:n_pc], refs[n_pc]
        ex_refs = refs[n_pc + 1:n_pc + 1 + n_ex]
        out_refs = refs[n_in:n_in + n_out]

        def finish(acc):
            outs = epilogue(acc, *[r[...] for r in ex_refs]) if epilogue is not None else (acc,)
            for r, o in zip(out_refs, outs):
                r[...] = o.astype(r.dtype)

        kk = pl.program_id(2)

        def step(a_ref):
            part = lax.dot_general(a_ref[...].astype(BF16), b_ref[...].astype(BF16), dims, preferred_element_type=F32)
            if nk == 1:
                finish(part)
                return
            acc_ref = refs[-1]

            @pl.when(kk == 0)
            def _():
                acc_ref[...] = part

            @pl.when(kk > 0)
            def _():
                acc_ref[...] += part

            @pl.when(kk == nk - 1)
            def _():
                finish(acc_ref[...])

        if n_pc == 1:
            step(a_refs[0])
        else:
            at = pl.program_id(0) if mode == "tn" else kk
            for p in range(n_pc):
                pl.when(jnp.logical_and(at >= firsts[p], at < firsts[p] + blocks[p]))(functools.partial(step, a_refs[p]))

    def a_spec_of(p):
        if n_pc == 1:
            return pl.BlockSpec((tk, tm), lambda i, j, kk: (kk, i)) if mode == "tn" else pl.BlockSpec((tm, tk), lambda i, j, kk: (i, kk))
        col = lambda at: jnp.clip(at - firsts[p], 0, blocks[p] - 1)
        mine = lambda at: jnp.logical_and(at >= firsts[p], at < firsts[p] + blocks[p])
        if mode == "tn":
            return pl.BlockSpec((tk, tm), lambda i, j, kk: (jnp.where(mine(i), kk, 0), col(i)))
        return pl.BlockSpec((tm, tk), lambda i, j, kk: (i, col(kk)))

    b_spec = {"nn": pl.BlockSpec((tk, tn), lambda i, j, kk: (kk, j)),
              "nt": pl.BlockSpec((tn, tk), lambda i, j, kk: (j, kk)),
              "tn": pl.BlockSpec((tk, tn), lambda i, j, kk: (kk, j))}[mode]
    tile = pl.BlockSpec((tm, tn), lambda i, j, kk: (i, j))
    out_spec, out_dims = tile, (m, n)
    if b_shards and mode == "nn":
        per = cols_b // tn
        b_spec = pl.BlockSpec((None, tk, tn), lambda i, j, kk: (j // per, kk, j % per))
    elif b_shards:
        assert mode == "nt"
        per = cols_b // tk
        b_spec = pl.BlockSpec((None, tn, tk), lambda i, j, kk: (kk // per, j, kk % per))
    if out_shards:
        assert not extras and epilogue is None
        per_out = n // N_CHIPS // tn
        out_spec = pl.BlockSpec((None, tm, tn), lambda i, j, kk: (j // per_out, i, j % per_out))
        out_dims = (N_CHIPS, m, n // N_CHIPS)
    outs = pl.pallas_call(
        body, name=name,
        grid=(m // tm, n // tn, nk),
        in_specs=[a_spec_of(p) for p in range(n_pc)] + [b_spec] + [tile] * n_ex + [pl.BlockSpec(memory_space=pl.ANY)] * (after is not None),
        out_specs=[out_spec] * n_out,
        out_shape=[jax.ShapeDtypeStruct(out_dims, dt) for dt in out_dtypes],
        scratch_shapes=[pltpu.VMEM((tm, tn), F32)] if nk > 1 else [],
        compiler_params=_params(("parallel", "parallel", "arbitrary")),
    )(*pieces, b, *extras, *([after] if after is not None else []))
    return outs[0] if n_out == 1 else outs


def _rowwise(fn, *, name, rows, seq, ins, outs, sums=()):
    tm = min(ROW_T, seq)
    n_pos = seq // tm
    n_in, n_out, n_sum = len(ins), len(outs), len(sums)

    def body(*refs):
        vals = fn(*[r[...] for r in refs[:n_in]])
        for r, v in zip(refs[n_in:n_in + n_out], vals[:n_out]):
            r[...] = v.astype(r.dtype)
        first = pl.program_id(0) == 0
        for r, v in zip(refs[n_in + n_out:], vals[n_out:]):
            @pl.when(first)
            def _(r=r, v=v):
                r[...] = v

            @pl.when(jnp.logical_not(first))
            def _(r=r, v=v):
                r[...] += v

    def spec(arr, width, col, kind):
        if kind == "row":
            return pl.BlockSpec((tm, width), lambda i, col=col: (i, col))
        if kind == "pos":
            return pl.BlockSpec((tm, width), lambda i, col=col: (i % n_pos, col))
        return pl.BlockSpec(arr.shape, lambda i: (0,) * arr.ndim)

    res = pl.pallas_call(
        body, name=name,
        grid=(rows // tm,),
        in_specs=[spec(*t) for t in ins],
        out_specs=[pl.BlockSpec((tm, w), lambda i: (i, 0)) for w, _ in outs]
        + [pl.BlockSpec((1, w), lambda i: (0, 0)) for w in sums],
        out_shape=[jax.ShapeDtypeStruct((rows, w), dt) for w, dt in outs]
        + [jax.ShapeDtypeStruct((1, w), F32) for w in sums],
        compiler_params=_params(("arbitrary",)),
    )(*[t[0] for t in ins])
    return res


def _colsum(v):
    return jnp.sum(v, axis=0, keepdims=True)


def _rope_fwd(t, c, s_up, s_dn):
    return t * c + pltpu.roll(t, LANES - 16, 1) * s_up + pltpu.roll(t, 16, 1) * s_dn


def _rope_bwd(d, c, s_up, s_dn):
    return d * c + pltpu.roll(d * s_up, 16, 1) + pltpu.roll(d * s_dn, LANES - 16, 1)


def _rope_tables(seq):
    half = ROPE // 2
    inv = jnp.power(ROPE_THETA, -jnp.arange(half, dtype=F32) / half)
    ang = jnp.arange(seq, dtype=F32)[:, None] * inv[None, :]
    cos, sin = jnp.cos(ang), jnp.sin(ang)
    zeros = jnp.zeros((seq, half), F32)
    lo, hi = jnp.ones((seq, KR_LANE), F32), jnp.ones((seq, LANES - KR_LANE - ROPE), F32)
    c = jnp.concatenate([lo, cos, cos, hi], axis=1)
    c_rope_only = jnp.concatenate([0 * lo, cos, cos, 0 * hi], axis=1)
    s_up = jnp.concatenate([0 * lo, -sin, zeros, 0 * hi], axis=1)
    s_dn = jnp.concatenate([0 * lo, zeros, sin, 0 * hi], axis=1)
    return c, s_up, s_dn, c_rope_only


def _rms(x, g):
    r = lax.rsqrt(jnp.mean(x * x, axis=1, keepdims=True) + RMS_EPS)
    return x * r * g


def _rms_bwd(x, g, dy):
    r = lax.rsqrt(jnp.mean(x * x, axis=1, keepdims=True) + RMS_EPS)
    xh = x * r
    dxh = dy * g
    dx = r * (dxh - xh * jnp.mean(dxh * xh, axis=1, keepdims=True))
    return dx, _colsum(dy * xh)


def _ln_stats(x):
    mu = jnp.mean(x, axis=1, keepdims=True)
    xc = x - mu
    r = lax.rsqrt(jnp.mean(xc * xc, axis=1, keepdims=True) + LN_EPS)
    return xc * r, r


def _ln_bwd(xh, r, g, dy):
    dxh = dy * g
    dx = r * (dxh - jnp.mean(dxh, axis=1, keepdims=True) - xh * jnp.mean(dxh * xh, axis=1, keepdims=True))
    return dx, _colsum(dy * xh), _colsum(dy)


def _table_specs(tables, sub):
    whole = lambda a: pl.BlockSpec(a.shape, lambda b, g: (0,) * a.ndim)
    if len(tables) == 1:
        return [whole(tables[0])]
    return [whole(tables[0]), whole(tables[1]), pl.BlockSpec((sub, 1, LANES), lambda b, g: (g, 0, 0))]


def _biased(s, table_refs, delta, head):
    if delta < table_refs[0].shape[0]:
        s = s + table_refs[0][delta]
    if len(table_refs) == 3:
        s = s - table_refs[2][head, 0:1, 0:1] * table_refs[1][delta]
    return s


def _lane_masks(sub):
    lane = lax.broadcasted_iota(jnp.int32, (1, LANES), 1)
    return [(lane // (LANES // sub) == a).astype(F32) for a in range(sub)]


def _attn_fwd(q, qb0, k, kb0, v, vb0, tables, scale, *, name, batch, seq, sub=1, stride=1, after=None):
    t = ATT_T
    nq = seq // t
    rows = batch * seq
    n_tab = len(tables)

    def body(q_ref, k_ref, v_ref, *rest):
        table_refs = rest[:n_tab]
        o_ref, lse_ref, kb, vtb = rest[n_tab + (after is not None):][:4]
        qbs = rest[n_tab + (after is not None) + 4:]
        masks = _lane_masks(sub)
        for a in range(sub):
            qbs[a][...] = (q_ref[...].astype(F32) * masks[a]).astype(BF16) if sub > 1 else q_ref[...].astype(BF16)
        kb[...] = k_ref[...].astype(BF16)
        vtb[...] = v_ref[...].astype(F32).T.astype(BF16)
        for i in range(nq):
            out_t = None
            for a in range(sub):
                qt = qbs[a][i * t:(i + 1) * t, :]
                logits = [_biased(lax.dot_general(kb[j * t:(j + 1) * t, :], qt, NT, preferred_element_type=F32) * (scale * LOG2E), table_refs, i - j, a)
                          for j in range(i + 1)]
                m = jnp.max(functools.reduce(jnp.maximum, logits), axis=0, keepdims=True)
                ps = [jnp.exp2(s - m) for s in logits]
                l = jnp.sum(functools.reduce(jnp.add, ps), axis=0, keepdims=True)
                acc = functools.reduce(jnp.add, [lax.dot_general(vtb[:, j * t:(j + 1) * t], p.astype(BF16), NN, preferred_element_type=F32)
                                                 for j, p in enumerate(ps)])
                part = acc / l if sub == 1 else (acc / l) * masks[a].T
                out_t = part if out_t is None else out_t + part
                lse_ref[i * t:(i + 1) * t, a * LANES:(a + 1) * LANES] = jnp.broadcast_to((m + jnp.log2(l)) * LN2, (LANES, t)).T
            o_ref[i * t:(i + 1) * t, :] = out_t.T

    slab = lambda b0, step: pl.BlockSpec((seq, LANES), lambda b, g: (b, b0 + step * g))
    groups = N_HEADS // sub
    return pl.pallas_call(
        body, name=name,
        grid=(batch, groups),
        in_specs=[slab(qb0, stride), slab(kb0, stride), slab(vb0, stride)] + _table_specs(tables, sub)
        + [pl.BlockSpec(memory_space=pl.ANY)] * (after is not None),
        out_specs=[slab(0, 1), pl.BlockSpec((seq, sub * LANES), lambda b, g: (b, g))],
        out_shape=[jax.ShapeDtypeStruct((rows, groups * LANES), F32), jax.ShapeDtypeStruct((rows, N_HEADS * LANES), F32)],
        scratch_shapes=[pltpu.VMEM((seq, LANES), BF16), pltpu.VMEM((LANES, seq), BF16)] + [pltpu.VMEM((seq, LANES), BF16)] * sub,
        compiler_params=_params(("arbitrary", "arbitrary")),
    )(q, k, v, *tables, *([after] if after is not None else []))


def _attn_bwd(q, qb0, k, kb0, v, vb0, o, do, lse, tables, scale, *, name, batch, seq, out_dtype, sub=1, stride=1, after=None):
    t = ATT_T
    nq = seq // t
    rows = batch * seq
    n_tab = len(tables)
    groups = N_HEADS // sub
    n_out = 3 if sub == 1 else 1

    def body(q_ref, k_ref, v_ref, o_ref, do_ref, lse_ref, *rest):
        table_refs = rest[:n_tab]
        rest = rest[n_tab + (after is not None):]
        out_refs, (kb, vb, dka, dva), per_head = rest[:n_out], rest[n_out:n_out + 4], rest[n_out + 4:]
        qbs, dobs, qtbs, dotbs = (per_head[g * sub:(g + 1) * sub] for g in range(4))
        masks = _lane_masks(sub)
        kb[...] = k_ref[...].astype(BF16)
        vb[...] = v_ref[...].astype(BF16)
        for a in range(sub):
            qa = q_ref[...].astype(F32) * masks[a] if sub > 1 else q_ref[...].astype(F32)
            doa = do_ref[...] * masks[a] if sub > 1 else do_ref[...]
            qbs[a][...] = qa.astype(BF16)
            dobs[a][...] = doa.astype(BF16)
            qtbs[a][...] = qa.T.astype(BF16)
            dotbs[a][...] = doa.T.astype(BF16)
        first = [True] * nq
        for i in range(nq):
            at = slice(i * t, (i + 1) * t)
            dq_all = None
            for a in range(sub):
                qt, dot = qbs[a][at, :], dobs[a][at, :]
                lse_t = lse_ref[at, a * LANES:a * LANES + 1] * LOG2E
                od = o_ref[at, :] * do_ref[at, :]
                delta = jnp.sum(od * masks[a] if sub > 1 else od, axis=1, keepdims=True)
                dq = None
                for j in range(i + 1):
                    kat = slice(j * t, (j + 1) * t)
                    kt, vt = kb[kat, :], vb[kat, :]
                    p = jnp.exp2(_biased(lax.dot_general(qt, kt, NT, preferred_element_type=F32) * (scale * LOG2E), table_refs, i - j, a) - lse_t)
                    dp = lax.dot_general(dot, vt, NT, preferred_element_type=F32)
                    ds = (p * (dp - delta) * scale).astype(BF16)
                    dk_part = lax.dot_general(qtbs[a][:, at], ds, NN, preferred_element_type=F32)
                    dv_part = lax.dot_general(dotbs[a][:, at], p.astype(BF16), NN, preferred_element_type=F32)
                    if first[j]:
                        dka[:, kat] = dk_part
                        dva[:, kat] = dv_part
                        first[j] = False
                    else:
                        dka[:, kat] += dk_part
                        dva[:, kat] += dv_part
                    dq_part = lax.dot_general(ds, kt, NN, preferred_element_type=F32)
                    dq = dq_part if dq is None else dq + dq_part
                dq = dq * masks[a] if sub > 1 else dq
                dq_all = dq if dq_all is None else dq_all + dq
            out_refs[0][at, 0:LANES] = dq_all.astype(out_refs[0].dtype)
        if sub == 1:
            out_refs[1][...] = dka[...].T.astype(out_refs[1].dtype)
            out_refs[2][...] = dva[...].T.astype(out_refs[2].dtype)
        else:
            out_refs[0][:, LANES:2 * LANES] = dka[...].T.astype(out_refs[0].dtype)
            out_refs[0][:, 2 * LANES:3 * LANES] = dva[...].T.astype(out_refs[0].dtype)
            out_refs[0][:, 3 * LANES:] = jnp.zeros((seq, LANES), out_refs[0].dtype)

    slab = lambda b0, step: pl.BlockSpec((seq, LANES), lambda b, g: (b, b0 + step * g))
    if sub == 1:
        out_specs = [slab(0, 1)] * 3
        out_shape = [jax.ShapeDtypeStruct((rows, N_HEADS * LANES), out_dtype)] * 3
    else:
        out_specs = [pl.BlockSpec((seq, 4 * LANES), lambda b, g: (b, g))]
        out_shape = [jax.ShapeDtypeStruct((rows, groups * 4 * LANES), out_dtype)]
    res = pl.pallas_call(
        body, name=name,
        grid=(batch, groups),
        in_specs=[slab(qb0, stride), slab(kb0, stride), slab(vb0, stride), slab(0, 1), slab(0, 1),
                  pl.BlockSpec((seq, sub * LANES), lambda b, g: (b, g))] + _table_specs(tables, sub)
        + [pl.BlockSpec(memory_space=pl.ANY)] * (after is not None),
        out_specs=out_specs, out_shape=out_shape,
        scratch_shapes=[pltpu.VMEM((seq, LANES), BF16)] * 2 + [pltpu.VMEM((LANES, seq), F32)] * 2
        + [pltpu.VMEM((seq, LANES), BF16)] * (2 * sub) + [pltpu.VMEM((LANES, seq), BF16)] * (2 * sub),
        compiler_params=_params(("arbitrary", "arbitrary")),
    )(q, k, v, o, do, lse, *tables, *([after] if after is not None else []))
    return res if sub == 1 else res[0]


def _attention_tables(seq, zero=0):
    n = seq // ATT_T
    pos = jnp.arange(ATT_T, dtype=jnp.int32) + zero
    dist = jnp.arange(n, dtype=jnp.int32)[:, None, None] * ATT_T + pos[None, :, None] - pos[None, None, :]
    causal = jnp.where(dist[:1] >= 0, 0.0, NEG).astype(F32)
    count = jnp.zeros(dist.shape, F32)
    for window, dilation in DIL_PATTERNS:
        count += ((dist >= 0) & (dist <= window) & (dist % dilation == 0)).astype(F32)
    held = jnp.where(count > 0, jnp.log2(jnp.maximum(count, 1.0)), NEG).astype(F32)
    slopes = jnp.asarray([2.0 ** (-8.0 * (i + 1) / N_HEADS) for i in range(N_HEADS)], F32)
    slopes = jnp.broadcast_to(slopes[:, None, None], (N_HEADS, 1, LANES))
    turned = lambda a: jnp.swapaxes(a, 1, 2)
    far = dist.astype(F32) * LOG2E
    return ((causal,), (turned(causal),)), ((held, far, slopes), (turned(held), turned(far), slopes))


def _pad_heads(w, width):
    kdim, n = w.shape[0], w.shape[1] // width
    return jnp.pad(w.reshape(kdim, n, width), ((0, 0), (0, 0), (0, LANES - width))).reshape(kdim, n * LANES)


def _unpad_heads(w, width):
    kdim, n = w.shape[0], w.shape[1] // LANES
    return w.reshape(kdim, n, LANES)[:, :, :width].reshape(kdim, n * width)


def _pad_head_rows(w, width):
    n, kdim = w.shape[0] // width, w.shape[1]
    return jnp.pad(w.reshape(n, width, kdim), ((0, 0), (0, LANES - width), (0, 0))).reshape(n * LANES, kdim)


def _unpad_head_rows(w, width):
    n, kdim = w.shape[0] // LANES, w.shape[1]
    return w.reshape(n, LANES, kdim)[:, :width].reshape(n * width, kdim)


def _pad_w_in_t(wt):
    n_qkv, pair = 3 * N_HEADS * DIL_DIM, 2 * DIL_DIM
    zeros = lambda n: jnp.zeros((n, wt.shape[1]), wt.dtype)
    main = jnp.concatenate([wt[:LORA_W], zeros(KR_LANE), wt[LORA_W:LORA_W + ROPE], zeros(P_GATE - P_KR - KR_LANE - ROPE),
                            wt[LORA_W + ROPE + n_qkv:]], axis=0)
    qkv = wt[LORA_W + ROPE:LORA_W + ROPE + n_qkv].reshape(3, N_HEADS // 2, pair, wt.shape[1]).transpose(1, 0, 2, 3)
    dil = jnp.pad(qkv, ((0, 0), (0, 1), (0, 0), (0, 0))).reshape(P_DIL, wt.shape[1])
    return main, dil


def _unpad_w_in_t(gt):
    qkv = gt[P_HALF:].reshape(N_HEADS // 2, 4, 2 * DIL_DIM, gt.shape[1])[:, :3].transpose(1, 0, 2, 3).reshape(3 * N_HEADS * DIL_DIM, gt.shape[1])
    return jnp.concatenate([gt[P_LORA:P_KR], gt[P_KR + KR_LANE:P_KR + KR_LANE + ROPE], qkv, gt[P_GATE:P_HALF]], axis=0)


def _split_ukv(w):
    w3 = w.reshape(w.shape[0], N_HEADS, NOPE + V_DIM)
    return (_pad_heads(w3[:, :, :NOPE].reshape(w.shape[0], -1), NOPE),
            _pad_heads(w3[:, :, NOPE:].reshape(w.shape[0], -1), V_DIM))


def _merge_ukv(g_k, g_v):
    kdim = g_k.shape[0]
    k3 = _unpad_heads(g_k, NOPE).reshape(kdim, N_HEADS, NOPE)
    v3 = _unpad_heads(g_v, V_DIM).reshape(kdim, N_HEADS, V_DIM)
    return jnp.concatenate([k3, v3], axis=2).reshape(kdim, N_HEADS * (NOPE + V_DIM))


def _pad_rows(w, width):
    return _pad_heads(w.T, width).T


def _unpad_rows(g, width):
    return _unpad_heads(g.T, width).T


def _join_cols(w):
    return w.transpose(1, 0, 2).reshape(w.shape[1], N_CHIPS * w.shape[2])


def _split_cols(g):
    return g.reshape(g.shape[0], N_CHIPS, g.shape[1] // N_CHIPS).transpose(1, 0, 2)


def _local_step(x3, target3, wg, b_gate, g_q_a, g_kv_a, ln1_g, ln1_b, ln2_g, ln2_b, token=None, late_arrived=None, late_weights=None,
                early_grads=None, early_grads_go=None, last_grads=None, tables=None):
    w_main_t, w_dil_t = _pad_w_in_t(wg["w_in"].reshape(IN_WIDTH, D_MODEL))
    w_uq_pt = _pad_head_rows(wg["w_uq"].reshape(N_HEADS * MLA_QK, Q_LORA), MLA_QK)
    w_ukv = _join_cols(wg["w_ukv"])
    batch, seq, _ = x3.shape
    rows = batch * seq
    x = x3.reshape(rows, D_MODEL)
    target = target3.reshape(rows, D_MODEL)
    row = functools.partial(_rowwise, rows=rows, seq=seq)
    mm = _matmul

    w_uk_p, w_uv_p = _split_ukv(w_ukv)
    b0, b1 = b_gate[0:1], b_gate[1:2]
    rope_c, rope_up, rope_dn, rope_c_only = _rope_tables(seq)
    (mla_bwd_tables, mla_fwd_tables), (dil_bwd_tables, dil_fwd_tables) = _attention_tables(seq) if tables is None else tables
    scale_mla, scale_dil = MLA_QK ** -0.5, DIL_DIM ** -0.5
    lora0, kr0, gate0 = P_LORA // LORA_W, P_KR // LANES, P_GATE // D_MODEL

    proj = mm(x, w_main_t, mode="nt", name="proj", tm=1024, tn=1536, tk=1024, after=token)
    proj_d = mm(x, w_dil_t, mode="nt", name="proj_dil", tm=1024, tn=1024, tk=1024, out_dtypes=(BF16,))

    def prep(lora, gq, gkv):
        return _rms(lora[:, :Q_LORA], gq), _rms(lora[:, Q_LORA:], gkv)

    qn, kvn = row(prep, name="mla_rms", ins=[(proj, LORA_W, lora0, "row"), (g_q_a, 0, 0, "full"), (g_kv_a, 0, 0, "full")],
                  outs=[(Q_LORA, BF16), (KV_LORA, BF16)])
    q_lin = mm(qn, w_uq_pt, mode="nt", name="q_up", tm=1024, tn=1024, tk=Q_LORA)
    k_lin = mm(kvn, w_uk_p, mode="nn", name="k_up", tm=1024, tn=1024, tk=KV_LORA)
    v_a = mm(kvn, w_uv_p, mode="nn", name="v_up", tm=1024, tn=1024, tk=KV_LORA, out_dtypes=(BF16,))

    def rope_qk(ql, kl, kr, c, up, dn):
        k_rot = _rope_fwd(kr, c, up, dn)
        qs = [_rope_fwd(ql[:, h * LANES:(h + 1) * LANES], c, up, dn) for h in range(N_HEADS)]
        ks = [kl[:, h * LANES:(h + 1) * LANES] + k_rot for h in range(N_HEADS)]
        return jnp.concatenate(qs, axis=1), jnp.concatenate(ks, axis=1)

    pos = lambda tab: (tab, LANES, 0, "pos")
    q_a, k_a = row(rope_qk, name="rope_qk",
                   ins=[(q_lin, D_MODEL, 0, "row"), (k_lin, D_MODEL, 0, "row"), (proj, LANES, kr0, "row"), pos(rope_c), pos(rope_up), pos(rope_dn)],
                   outs=[(N_HEADS * LANES, BF16), (N_HEADS * LANES, BF16)])
    o_a, lse_a = _attn_fwd(q_a, 0, k_a, 0, v_a, 0, mla_fwd_tables, scale_mla, name="mla_fwd", batch=batch, seq=seq)
    arrived = None if late_arrived is None else late_arrived(o_a)
    o_b, lse_b = _attn_fwd(proj_d, 0, proj_d, 1, proj_d, 2, dil_fwd_tables, scale_dil, name="dil_fwd", batch=batch, seq=seq, sub=2, stride=4, after=arrived)
    late = wg if late_weights is None else late_weights(o_b)
    w_oa_p = _pad_rows(_join_cols(late["w_o_mla"]), V_DIM)
    w_ob = _join_cols(late["w_o_dil"])
    w_out, w_ff1, w_ff2 = late["w_out"].reshape(D_MODEL, D_MODEL), late["w_ff1"], late["w_ff2"].reshape(D_FF, D_MODEL)
    y_a = mm(o_a, w_oa_p, mode="nn", name="o_mla", tm=1024, tn=1024, tk=1024, out_dtypes=(BF16,))
    y_b = mm(o_b, w_ob, mode="nn", name="o_dil", tm=1024, tn=1024, tk=1024, out_dtypes=(BF16,))

    def gate(t0, t1, c0, c1, ya, yb):
        return (jax.nn.sigmoid(t0 + c0) * ya + jax.nn.sigmoid(t1 + c1) * yb,)

    gate_ins = [(proj, D_MODEL, gate0, "row"), (proj, D_MODEL, gate0 + 1, "row"), (b0, 0, 0, "full"), (b1, 0, 0, "full")]
    (u,) = row(gate, name="gate", ins=gate_ins + [(y_a, D_MODEL, 0, "row"), (y_b, D_MODEL, 0, "row")], outs=[(D_MODEL, BF16)])
    mixed = mm(u, w_out, mode="nn", name="mix", tm=1024, tn=1024, tk=1024)

    def ln1(xv, mv, g, b):
        r1 = ALPHA * xv + mv
        xh, _ = _ln_stats(r1)
        return r1, xh * g + b

    r1, h = row(ln1, name="ln1", ins=[(x, D_MODEL, 0, "row"), (mixed, D_MODEL, 0, "row"), (ln1_g, 0, 0, "full"), (ln1_b, 0, 0, "full")],
                outs=[(D_MODEL, F32), (D_MODEL, F32)])

    def relu2(acc):
        r = jnp.maximum(acc, 0.0)
        return acc, r * r

    a_ff, z = mm(h, w_ff1, mode="nn", name="ff1", tm=1024, tn=1024, tk=1024, out_dtypes=(BF16, BF16), epilogue=relu2, b_shards=True)
    f = mm(z, w_ff2, mode="nn", name="ff2", tm=1024, tn=1024, tk=2048)

    def ln2_loss(hv, fv, tv, g, b):
        xh, r = _ln_stats(ALPHA * hv + fv)
        err = xh * g + b - tv
        dy = err * (1.0 / D_MODEL)
        dr2, dg, db = _ln_bwd(xh, r, g, dy)
        loss = jnp.sum(_colsum(err * err), axis=1, keepdims=True) * (0.5 / D_MODEL)
        return dr2, jnp.broadcast_to(loss, (1, LANES)), dg, db

    dr2, loss_l, d_ln2_g, d_ln2_b = row(
        ln2_loss, name="ln2_loss",
        ins=[(h, D_MODEL, 0, "row"), (f, D_MODEL, 0, "row"), (target, D_MODEL, 0, "row"), (ln2_g, 0, 0, "full"), (ln2_b, 0, 0, "full")],
        outs=[(D_MODEL, F32)], sums=[LANES, D_MODEL, D_MODEL])

    d_w_ff2 = mm(z, dr2, mode="tn", name="d_w_ff2", tm=1024, tn=1024, tk=2048)
    da = mm(dr2, w_ff2, mode="nt", name="d_ff_act", tm=1024, tn=1024, tk=1024, out_dtypes=(BF16,), extras=(a_ff,),
            epilogue=lambda acc, av: (acc * (2.0 * jnp.maximum(av.astype(F32), 0.0)),))
    d_w_ff1 = mm(h, da, mode="tn", name="d_w_ff1", tm=1024, tn=1024, tk=2048, out_shards=True)
    dh = mm(da, w_ff1, mode="nt", name="d_h", tm=1024, tn=1024, tk=1024, extras=(dr2,), epilogue=lambda acc, rv: (acc + ALPHA * rv,), b_shards=True)

    def ln1_bwd(dhv, r1v, g):
        xh, r = _ln_stats(r1v)
        return _ln_bwd(xh, r, g, dhv)

    dr1, d_ln1_g, d_ln1_b = row(ln1_bwd, name="ln1_bwd", ins=[(dh, D_MODEL, 0, "row"), (r1, D_MODEL, 0, "row"), (ln1_g, 0, 0, "full")],
                                outs=[(D_MODEL, F32)], sums=[D_MODEL, D_MODEL])
    d_w_out = mm(u, dr1, mode="tn", name="d_w_out", tm=1024, tn=1024, tk=1024)
    du = mm(dr1, w_out, mode="nt", name="d_u", tm=1024, tn=1024, tk=1024, out_dtypes=(BF16,))

    def gate_bwd(t0, t1, c0, c1, ya, yb, duv):
        s0, s1 = jax.nn.sigmoid(t0 + c0), jax.nn.sigmoid(t1 + c1)
        dt0 = duv * ya * s0 * (1.0 - s0)
        dt1 = duv * yb * s1 * (1.0 - s1)
        return duv * s0, duv * s1, jnp.concatenate([dt0, dt1], axis=1), jnp.concatenate([_colsum(dt0), _colsum(dt1)], axis=1)

    dy_a, dy_b, d_gates, d_b_gate = row(
        gate_bwd, name="gate_bwd", ins=gate_ins + [(y_a, D_MODEL, 0, "row"), (y_b, D_MODEL, 0, "row"), (du, D_MODEL, 0, "row")],
        outs=[(D_MODEL, BF16), (D_MODEL, BF16), (2 * D_MODEL, BF16)], sums=[2 * D_MODEL])
    d_w_oa_p = mm(o_a, dy_a, mode="tn", name="d_w_o_mla", tm=1024, tn=1024, tk=1024)
    d_w_ob = mm(o_b, dy_b, mode="tn", name="d_w_o_dil", tm=1024, tn=1024, tk=1024)
    grads = dict(w_o_mla=_split_cols(_unpad_rows(d_w_oa_p, V_DIM)), w_o_dil=_split_cols(d_w_ob),
                 w_out=d_w_out.reshape(N_CHIPS, D_MODEL // N_CHIPS, D_MODEL), w_ff1=d_w_ff1, w_ff2=d_w_ff2.reshape(N_CHIPS, D_FF // N_CHIPS, D_MODEL))
    sent = None if early_grads is None else early_grads(grads)
    do_a = mm(dy_a, w_oa_p, mode="nt", name="d_o_mla", tm=1024, tn=1024, tk=1024, after=sent)
    do_b = mm(dy_b, w_ob, mode="nt", name="d_o_dil", tm=1024, tn=1024, tk=1024)
    dq_a, dk_a, dv_a = _attn_bwd(q_a, 0, k_a, 0, v_a, 0, o_a, do_a, lse_a, mla_bwd_tables, scale_mla,
                                 name="mla_bwd", batch=batch, seq=seq, out_dtype=F32)
    going = None if early_grads_go is None else early_grads_go(dq_a)
    d_qkv_d = _attn_bwd(proj_d, 0, proj_d, 1, proj_d, 2, o_b, do_b, lse_b, dil_bwd_tables, scale_dil,
                        name="dil_bwd", batch=batch, seq=seq, out_dtype=BF16, sub=2, stride=4, after=going)

    def mla_post(dq, dk, c, up, dn, c_only):
        dqs = [_rope_bwd(dq[:, h * LANES:(h + 1) * LANES], c, up, dn) for h in range(N_HEADS)]
        dk_sum = dk[:, :LANES]
        for h in range(1, N_HEADS):
            dk_sum = dk_sum + dk[:, h * LANES:(h + 1) * LANES]
        return jnp.concatenate(dqs, axis=1), _rope_bwd(dk_sum, c_only, up, dn)

    dq_lin, d_kr = row(mla_post, name="mla_unrope",
                       ins=[(dq_a, D_MODEL, 0, "row"), (dk_a, D_MODEL, 0, "row"), pos(rope_c), pos(rope_up), pos(rope_dn), pos(rope_c_only)],
                       outs=[(N_HEADS * LANES, BF16), (LANES, BF16)])
    d_w_uq_pt = mm(dq_lin, qn, mode="tn", name="d_w_uq", tm=1024, tn=Q_LORA, tk=1024)
    d_w_uk_p = mm(kvn, dk_a, mode="tn", name="d_w_uk", tm=KV_LORA, tn=1024, tk=1024)
    d_w_uv_p = mm(kvn, dv_a, mode="tn", name="d_w_uv", tm=KV_LORA, tn=1024, tk=1024)
    d_qn = mm(dq_lin, w_uq_pt, mode="nn", name="d_qn", tm=1024, tn=Q_LORA, tk=1024)
    d_kvn_k = mm(dk_a, w_uk_p, mode="nt", name="d_kvn_k", tm=1024, tn=KV_LORA, tk=1024)
    d_kvn = mm(dv_a, w_uv_p, mode="nt", name="d_kvn", tm=1024, tn=KV_LORA, tk=1024, extras=(d_kvn_k,), epilogue=lambda acc, e: (acc + e,))

    def rms_bwd(lora, dq, dkv, dkr, gq, gkv):
        dxq, dgq = _rms_bwd(lora[:, :Q_LORA], gq, dq)
        dxk, dgk = _rms_bwd(lora[:, Q_LORA:], gkv, dkv)
        tail = jnp.zeros((dxq.shape[0], P_GATE - P_KR - LANES), F32)
        return jnp.concatenate([dxq, dxk, dkr.astype(F32), tail], axis=1), dgq, dgk

    d_tail, d_g_q_a, d_g_kv_a = row(
        rms_bwd, name="mla_rms_bwd",
        ins=[(proj, LORA_W, lora0, "row"), (d_qn, Q_LORA, 0, "row"), (d_kvn, KV_LORA, 0, "row"), (d_kr, LANES, 0, "row"),
             (g_q_a, 0, 0, "full"), (g_kv_a, 0, 0, "full")],
        outs=[(P_GATE, BF16)], sums=[Q_LORA, KV_LORA])
    d_proj = [d_tail, d_gates, d_qkv_d]
    d_w_in_pt = mm(d_proj, x, mode="tn", name="d_w_in", tm=1024, tn=1024, tk=1024)
    grads.update(w_in=_unpad_w_in_t(d_w_in_pt).reshape(N_CHIPS, IN_WIDTH // N_CHIPS, D_MODEL),
                 w_uq=_unpad_head_rows(d_w_uq_pt, MLA_QK).reshape(N_CHIPS, N_HEADS * MLA_QK // N_CHIPS, Q_LORA),
                 w_ukv=_split_cols(_merge_ukv(d_w_uk_p, d_w_uv_p)))
    leaving = None if last_grads is None else last_grads(grads)
    grad_x = mm(d_proj, jnp.concatenate([w_main_t, w_dil_t], axis=0), mode="nn", name="d_x", tm=1024, tn=1024, tk=1024, extras=(dr1,), epilogue=lambda acc, rv: (acc + ALPHA * rv,),
                after=leaving)

    grads.update(
        b_gate=d_b_gate.reshape(2, D_MODEL), g_q_a=d_g_q_a, g_kv_a=d_g_kv_a, ln1_g=d_ln1_g, ln1_b=d_ln1_b, ln2_g=d_ln2_g, ln2_b=d_ln2_b)
    return loss_l, grad_x.reshape(batch, seq, D_MODEL), grads


BIG = ("w_in", "w_uq", "w_ukv", "w_o_mla", "w_o_dil", "w_out", "w_ff1", "w_ff2")
SMALL = (("b_gate", 2 * D_MODEL), ("g_q_a", Q_LORA), ("g_kv_a", KV_LORA), ("ln1_g", D_MODEL), ("ln1_b", D_MODEL),
         ("ln2_g", D_MODEL), ("ln2_b", D_MODEL))
TRANSPOSED = ("w_in", "w_uq")
D2D_PIECES = (4, 2, 1)
ANY = pl.BlockSpec(memory_space=pl.ANY)
SIDE_EFFECTS = pltpu.CompilerParams(has_side_effects=True)


def _place():
    x, y, c = lax.axis_index("x"), lax.axis_index("y"), lax.axis_index("c")
    return x, y, c, ((1 - x, y), (x, 1 - y), (1 - x, 1 - y))


def _half_axis(shape):
    return 0 if shape[0] % 32 == 0 else 1


def _half_shape(shape):
    return (shape[0] // 2, shape[1]) if _half_axis(shape) == 0 else (shape[0], shape[1] // 2)


def _window(ref, lead, shape, which=None, pieces=False):
    axis = _half_axis(shape)
    size = shape[axis] if which is None else shape[axis] // 2
    base = 0 if which is None else which * size
    tile = (16, LANES)[axis]
    count = next(c for c in D2D_PIECES if size % (tile * c) == 0) if pieces else 1
    step = size // count
    spans = [pl.ds(pl.multiple_of(base + i * step, tile), step) for i in range(count)]
    refs = [ref.at[(*lead, s)] if axis == 0 else ref.at[(*lead, slice(None), s)] for s in spans]
    return refs if pieces else refs[0]


def _remote(src, dst, send, recv, to):
    return pltpu.make_async_remote_copy(src_ref=src, dst_ref=dst, send_sem=send, recv_sem=recv, device_id=to, device_id_type=MESH)


def _pair_split(grads, name):
    n = len(grads)

    def body(*refs):
        srcs, outs, (send, recv) = refs[:n], refs[n:2 * n], refs[2 * n:]
        x, y, c, _ = _place()
        for t in range(n):
            for s in range(N_CHIPS):
                _remote(_window(srcs[t], (s,), grads[t].shape[1:], 1 - c), outs[t].at[s], send.at[t], recv.at[t], (x, y, 1 - c)).start()
        for t in range(n):
            _remote(_window(srcs[t], (slice(None),), grads[t].shape[1:], 1 - c), outs[t], send.at[t], recv.at[t], (x, y, 1 - c)).wait()

    return pl.pallas_call(
        body, name=name, in_specs=[ANY] * n, out_specs=[ANY] * n,
        out_shape=[jax.ShapeDtypeStruct((N_CHIPS,) + _half_shape(g.shape[1:]), g.dtype) for g in grads],
        scratch_shapes=[pltpu.SemaphoreType.DMA((n,)), pltpu.SemaphoreType.DMA((n,))],
        compiler_params=SIDE_EFFECTS,
    )(*grads)


HBM = pl.BlockSpec(memory_space=pltpu.HBM)
SEM = pl.BlockSpec(memory_space=pltpu.SEMAPHORE)
SPLIT = pltpu.CompilerParams(has_side_effects=pltpu.SideEffectType.DATAFLOW_SIDE_EFFECTING)


def _in_hbm(a):
    return pltpu.with_memory_space_constraint(a, pltpu.HBM)


def _split_copies(kind, srcs, lands):
    x, y, c, chips = _place()
    out = []
    for t in range(len(srcs)):
        if kind == "pair":
            out += [(t, s % 3, _window(srcs[t], (s,), srcs[t].shape[1:], 1 - c), lands[t].at[s], (x, y, 1 - c)) for s in range(N_CHIPS)]
            continue
        if kind == "join":
            out += [(t, 0, a, b, (x, y, 1 - c)) for a, b in zip(_window(srcs[t], (), srcs[t].shape, None, True), _window(lands[t], (), srcs[t].shape, None, True))]
            continue
        if kind == "forward":
            shape, sibling = srcs[t].shape, (x, y, 1 - c)
            out += [(t, 0, a, b, sibling) for a, b in zip(_window(srcs[t], (), shape, None, True), _window(lands[t], (2 * x + y,), shape, None, True))]
            out += [(t, j, a, a, sibling) for j, (cx, cy) in enumerate(chips) for a in _window(lands[t], (2 * cx + cy,), shape, c, True)]
            continue
        for j, (cx, cy) in enumerate(chips):
            if kind == "gather":
                shape = srcs[t].shape
                out.append((t, j, _window(srcs[t], (), shape, c), _window(lands[t], (2 * x + y,), shape, c), (cx, cy, c)))
            else:
                out.append((t, j, srcs[t].at[2 * cx + cy], lands[t].at[j], (cx, cy, c)))
    return out


def _split_start(kind, srcs, land_shapes, name, lands=None, after=None):
    n = len(srcs)

    def body(*refs):
        src_refs, land_refs, sems, token = refs[:n], refs[n:2 * n], refs[-7 - 2 * n:-1 - 2 * n], refs[-1]
        for t, j, s, d, to in _split_copies(kind, src_refs, land_refs):
            _remote(s, d, sems[j], sems[3 + j], to).start()
        token[...] = jnp.zeros_like(token)

    lands = [_in_hbm(lax.empty(s.shape, s.dtype)) for s in land_shapes] if lands is None else list(lands)
    thru = [pltpu.HBM(a.shape, a.dtype) for a in list(srcs) + lands]
    res = pl.pallas_call(
        body, name=name,
        out_shape=(*[pltpu.SemaphoreType.DMA(())] * 6, *thru, jax.ShapeDtypeStruct((8, LANES), F32)),
        in_specs=[HBM] * (2 * n) + [ANY] * (after is not None), out_specs=(*[SEM] * 6, *[HBM] * (2 * n), pl.BlockSpec(memory_space=pltpu.VMEM)),
        input_output_aliases={i: 6 + i for i in range(2 * n)}, compiler_params=SPLIT,
    )(*[_in_hbm(s) for s in srcs], *lands, *([after] if after is not None else []))
    return res[:6], res[6:6 + n], res[6 + n:6 + 2 * n], res[-1]


def _split_wait(kind, sems, srcs, lands, after, name):
    n = len(srcs)

    def body(*refs):
        src_refs, land_refs, sem_refs = refs[:n], refs[n:2 * n], refs[2 * n:2 * n + 6]
        for t, j, s, d, to in _split_copies(kind, src_refs, land_refs):
            cp = _remote(s, d, sem_refs[j], sem_refs[3 + j], to)
            cp.wait_send()
            cp.wait_recv()

    res = pl.pallas_call(
        body, name=name, out_shape=[pltpu.HBM(a.shape, a.dtype) for a in list(srcs) + list(lands)],
        in_specs=[HBM] * (2 * n) + [SEM] * 6 + [ANY], out_specs=[HBM] * (2 * n),
        input_output_aliases={i: i for i in range(2 * n)}, compiler_params=SPLIT,
    )(*srcs, *lands, *sems, after)
    return res[:n], res[n:]


def _sum_all_devices(vec, name):
    n_rows = vec.shape[0]

    def body(v_ref, out_ref, buf, send, recv):
        x, y, c, _ = _place()
        me = 4 * x + 2 * y + c
        buf[me] = v_ref[...]
        flips = [(a, b, d) for a in (0, 1) for b in (0, 1) for d in (0, 1)][1:]
        copies = []
        for r, (a, b, d) in enumerate(flips):
            px, py, pc = (1 - x if a else x), (1 - y if b else y), (1 - c if d else c)
            copies.append(pltpu.make_async_remote_copy(src_ref=v_ref, dst_ref=buf.at[me], send_sem=send.at[r], recv_sem=recv.at[r],
                                                       device_id=(px, py, pc), device_id_type=MESH))
            copies[-1].start()
        for r, (a, b, d) in enumerate(flips):
            px, py, pc = (1 - x if a else x), (1 - y if b else y), (1 - c if d else c)
            pltpu.make_async_remote_copy(src_ref=v_ref, dst_ref=buf.at[4 * px + 2 * py + pc], send_sem=send.at[r], recv_sem=recv.at[r],
                                         device_id=(px, py, pc), device_id_type=MESH).wait_recv()
        for cp in copies:
            cp.wait_send()
        total = buf[0]
        for k in range(1, N_DEV):
            total = total + buf[k]
        out_ref[...] = total

    vmem = pl.BlockSpec(memory_space=pltpu.VMEM)
    return pl.pallas_call(
        body, name=name, in_specs=[vmem], out_specs=vmem, out_shape=jax.ShapeDtypeStruct(vec.shape, F32),
        scratch_shapes=[pltpu.VMEM((N_DEV, n_rows, LANES), F32), pltpu.SemaphoreType.DMA((N_DEV - 1,)), pltpu.SemaphoreType.DMA((N_DEV - 1,))],
        compiler_params=pltpu.CompilerParams(has_side_effects=True),
    )(vec)


def _half_tile(half, width):
    t = half
    while t * width * 4 > (2 << 20) and t % 32 == 0:
        t //= 2
    return t


def _pair_add(g, theirs, core, name):
    _, half, width = theirs.shape
    t = _half_tile(half, width)
    n = half // t

    def body(c_ref, a_ref, b_ref, o_ref):
        o_ref[...] = (a_ref[...] + b_ref[...]).astype(BF16)

    tile = pl.BlockSpec((1, t, width), lambda j, i, c_ref: (j, i, 0))
    if _half_axis(g.shape[1:]) == 0:
        mine = pl.BlockSpec((1, t, width), lambda j, i, c_ref: (j, c_ref[0] * n + i, 0))
    else:
        mine = pl.BlockSpec((1, t, width), lambda j, i, c_ref: (j, i, c_ref[0]))
    return pl.pallas_call(
        body, name=name,
        grid_spec=pltpu.PrefetchScalarGridSpec(num_scalar_prefetch=1, grid=(N_CHIPS, n), in_specs=[mine, tile], out_specs=tile),
        out_shape=jax.ShapeDtypeStruct(theirs.shape, BF16), compiler_params=_params(("parallel", "parallel")),
    )(core, g, theirs)


def _chip_sum(part, others, chip, name, after=None):
    _, half, width = part.shape
    t = _half_tile(half, width)

    def body(s_ref, mine, p0, p1, p2, *rest):
        o_ref = rest[-1]
        o_ref[...] = ((mine[0].astype(F32) + p0[0].astype(F32)) + p1[0].astype(F32)) + p2[0].astype(F32)

    return pl.pallas_call(
        body, name=name,
        grid_spec=pltpu.PrefetchScalarGridSpec(
            num_scalar_prefetch=1, grid=(half // t,),
            in_specs=[pl.BlockSpec((1, t, width), lambda i, s_ref: (s_ref[0], i, 0))]
            + [pl.BlockSpec((1, t, width), lambda i, s_ref, j=j: (j, i, 0)) for j in range(3)] + [pl.BlockSpec(memory_space=pl.ANY)] * (after is not None),
            out_specs=pl.BlockSpec((t, width), lambda i, s_ref: (i, 0))),
        out_shape=jax.ShapeDtypeStruct((half, width), F32), compiler_params=_params(("parallel",)),
    )(chip, part, others, others, others, *([after] if after is not None else []))


EARLY = ("w_in", "w_uq", "w_ukv")
LATE = ("w_o_mla", "w_o_dil", "w_out", "w_ff1", "w_ff2")


def _chip_partials(grads, names, core, tag):
    gs = [grads[n] for n in names]
    theirs = _pair_split(gs, "pair_split_" + tag)
    return [_pair_add(g, th, core, "pair_add_" + n) for g, th, n in zip(gs, theirs, names)]


def _sum_small(vals):
    n_in = len(vals)
    n_rows = sum(a.shape[0] * a.shape[1] // LANES for a in vals)
    pad_rows = -(-n_rows // 8) * 8

    def chunks(refs):
        return [(ref, a, j) for ref in refs for a in range(ref.shape[0]) for j in range(ref.shape[1] // LANES)]

    def body(*refs):
        ins, outs, (buf, send, recv) = refs[:n_in], refs[n_in:2 * n_in], refs[2 * n_in:]
        x, y, c, _ = _place()
        me = 4 * x + 2 * y + c
        for r, (ref, a, j) in enumerate(chunks(ins)):
            buf[me, r:r + 1, :] = ref[a:a + 1, j * LANES:(j + 1) * LANES]
        if pad_rows > n_rows:
            buf[me, n_rows:pad_rows, :] = jnp.zeros((pad_rows - n_rows, LANES), F32)
        flips = [(a, b, d) for a in (0, 1) for b in (0, 1) for d in (0, 1)][1:]
        peers = [((1 - x if a else x), (1 - y if b else y), (1 - c if d else c)) for a, b, d in flips]
        copies = [_remote(buf.at[me], buf.at[me], send.at[r], recv.at[r], peer) for r, peer in enumerate(peers)]
        for cp in copies:
            cp.start()
        for r, (px, py, pc) in enumerate(peers):
            _remote(buf.at[me], buf.at[4 * px + 2 * py + pc], send.at[r], recv.at[r], (px, py, pc)).wait_recv()
        for cp in copies:
            cp.wait_send()
        total = buf[0]
        for k in range(1, N_DEV):
            total = total + buf[k]
        for r, (ref, a, j) in enumerate(chunks(outs)):
            ref[a:a + 1, j * LANES:(j + 1) * LANES] = total[r:r + 1, :]

    vmem = pl.BlockSpec(memory_space=pltpu.VMEM)
    return pl.pallas_call(
        body, name="sum_small", in_specs=[vmem] * n_in, out_specs=[vmem] * n_in,
        out_shape=[jax.ShapeDtypeStruct(a.shape, F32) for a in vals],
        scratch_shapes=[pltpu.VMEM((N_DEV, pad_rows, LANES), F32), pltpu.SemaphoreType.DMA((N_DEV - 1,)), pltpu.SemaphoreType.DMA((N_DEV - 1,))],
        compiler_params=SIDE_EFFECTS,
    )(*vals)


def _adam_math(w, g, m, v):
    nm = B1 * m + (1.0 - B1) * g
    nv = B2 * v + (1.0 - B2) * (g * g)
    m_hat = nm / (1.0 - B1 ** ADAM_STEP)
    v_hat = nv / (1.0 - B2 ** ADAM_STEP)
    return -LR * (m_hat / (jnp.sqrt(v_hat) + ADAM_EPS) + WD * w), nm, nv


def _adamw_big(w, mine, theirs, m, v, core, name, side_by_side=False):
    rows, width = w.shape
    if side_by_side:
        t = next(c for c in (152, 96, 64, 32, 16, 8) if rows % c == 0)
        hb = None
        half_spec = pl.BlockSpec((t, width // 2), lambda i, c_ref: (i, 0))
    else:
        t = next(c for c in (256, 128, 64, 32, 16, 8) if (rows // 2) % c == 0)
        hb = rows // 2 // t
        half_spec = pl.BlockSpec((t, width), lambda i, c_ref: (i % hb, 0))

    def body(c_ref, w_ref, a_ref, b_ref, m_ref, v_ref, g_ref, d_ref, nm_ref, nv_ref):
        south = c_ref[0] == 0
        if side_by_side:
            g = jnp.where(south, jnp.concatenate([a_ref[...], b_ref[...]], axis=1), jnp.concatenate([b_ref[...], a_ref[...]], axis=1))
        else:
            g = jnp.where((pl.program_id(0) < hb) == south, a_ref[...], b_ref[...])
        g_ref[...] = g
        d_ref[...], nm_ref[...], nv_ref[...] = _adam_math(w_ref[...], g, m_ref[...], v_ref[...])

    spec = pl.BlockSpec((t, width), lambda i, c_ref: (i, 0))
    return pl.pallas_call(
        body, name=name,
        grid_spec=pltpu.PrefetchScalarGridSpec(num_scalar_prefetch=1, grid=(rows // t,),
                                               in_specs=[spec, half_spec, half_spec, spec, spec], out_specs=[spec] * 4),
        out_shape=[jax.ShapeDtypeStruct(w.shape, F32)] * 4, compiler_params=_params(("parallel",)),
    )(core, w, mine, theirs, m, v)


def _adamw_small(ws, gs, ms, vs):
    n = len(ws)

    def body(*refs):
        for t in range(n):
            w_ref, g_ref, m_ref, v_ref = (refs[k * n + t] for k in range(4))
            d, nm, nv = _adam_math(w_ref[...], g_ref[...], m_ref[...], v_ref[...])
            refs[4 * n + t][...] = d
            refs[5 * n + t][...] = nm
            refs[6 * n + t][...] = nv

    vmem = pl.BlockSpec(memory_space=pltpu.VMEM)
    res = pl.pallas_call(body, name="adamw_small", in_specs=[vmem] * (4 * n), out_specs=[vmem] * (3 * n),
                         out_shape=[jax.ShapeDtypeStruct(a.shape, F32) for a in ws] * 3)(*ws, *gs, *ms, *vs)
    return res[:n], res[n:2 * n], res[2 * n:]


def kernel(x, w_in, b_gate, g_q_a, w_uq, g_kv_a, w_ukv, w_o_mla, w_o_dil, w_out, ln1_g, ln1_b, w_ff1, w_ff2, ln2_g, ln2_b, loss_target, m_w_in, m_b_gate, m_g_q_a, m_w_uq, m_g_kv_a, m_w_ukv, m_w_o_mla, m_w_o_dil, m_w_out, m_ln1_g, m_ln1_b, m_w_ff1, m_w_ff2, m_ln2_g, m_ln2_b, v_w_in, v_b_gate, v_g_q_a, v_w_uq, v_g_kv_a, v_w_ukv, v_w_o_mla, v_w_o_dil, v_w_out, v_ln1_g, v_ln1_b, v_w_ff1, v_w_ff2, v_ln2_g, v_ln2_b):
    order = ("w_in", "b_gate", "g_q_a", "w_uq", "g_kv_a", "w_ukv", "w_o_mla", "w_o_dil", "w_out", "ln1_g", "ln1_b", "w_ff1", "w_ff2", "ln2_g", "ln2_b")
    w = dict(w_in=w_in, b_gate=b_gate, g_q_a=g_q_a, w_uq=w_uq, g_kv_a=g_kv_a, w_ukv=w_ukv, w_o_mla=w_o_mla, w_o_dil=w_o_dil, w_out=w_out,
             ln1_g=ln1_g, ln1_b=ln1_b, w_ff1=w_ff1, w_ff2=w_ff2, ln2_g=ln2_g, ln2_b=ln2_b)
    m = dict(w_in=m_w_in, b_gate=m_b_gate, g_q_a=m_g_q_a, w_uq=m_w_uq, g_kv_a=m_g_kv_a, w_ukv=m_w_ukv, w_o_mla=m_w_o_mla, w_o_dil=m_w_o_dil,
             w_out=m_w_out, ln1_g=m_ln1_g, ln1_b=m_ln1_b, w_ff1=m_w_ff1, w_ff2=m_w_ff2, ln2_g=m_ln2_g, ln2_b=m_ln2_b)
    v = dict(w_in=v_w_in, b_gate=v_b_gate, g_q_a=v_g_q_a, w_uq=v_w_uq, g_kv_a=v_g_kv_a, w_ukv=v_w_ukv, w_o_mla=v_w_o_mla, w_o_dil=v_w_o_dil,
             w_out=v_w_out, ln1_g=v_ln1_g, ln1_b=v_ln1_b, w_ff1=v_w_ff1, w_ff2=v_w_ff2, ln2_g=v_ln2_g, ln2_b=v_ln2_b)
    chip = 2 * lax.axis_index("x") + lax.axis_index("y")
    south = (lax.axis_index("c") == 0).astype(F32)
    gate_w = D_MODEL // N_CHIPS

    core = lax.axis_index("c").astype(jnp.int32).reshape(1)
    turn = lambda n, a: a.T if n in TRANSPOSED else a
    shards = {n: turn(n, w[n][0]).astype(BF16) for n in BIG}
    early_shards, late_shards = [shards[n] for n in EARLY], [shards[n] for n in LATE]
    gathered = lambda group: [jax.ShapeDtypeStruct((N_CHIPS,) + s.shape, BF16) for s in group]
    e_sems, e_srcs, e_lands, e_token = _split_start("gather", early_shards, gathered(early_shards), "gather_early_start")
    g_sems, g_srcs, g_lands, g_token = _split_start("gather", late_shards, gathered(late_shards), "gather_late_start", after=e_token)
    tables = _attention_tables(x.shape[1], g_token[0, 0].astype(jnp.int32))
    b_mine = lax.dynamic_update_slice(jnp.zeros((2, D_MODEL), F32), b_gate[0] * south + g_token[0, 0], (0, chip * gate_w))
    b_full = _sum_all_devices(b_mine.reshape(-1, LANES), "gather_b_gate").reshape(2, D_MODEL)
    behind = b_full[0:1, 0:1] + sum(t[0, 0:1, 0:1] for pair in tables for group in pair for t in group[:2])
    e_srcs, e_lands = _split_wait("gather", e_sems, e_srcs, e_lands, behind, "gather_early_wait")
    e_forward = _split_start("forward", e_srcs, None, "gather_early_forward_start", lands=e_lands)
    first = dict(zip(EARLY, _split_wait("forward", *e_forward[:3], e_forward[-1], "gather_early_forward_wait")[1]))

    sent = {}

    def late_arrived(after):
        srcs, lands = _split_wait("gather", g_sems, g_srcs, g_lands, after, "gather_late_wait")
        sent["forward"] = _split_start("forward", srcs, None, "gather_late_forward_start", lands=lands)
        return sent["forward"][-1]

    def late_weights(after):
        return dict(zip(LATE, _split_wait("forward", *sent["forward"][:3], after, "gather_late_forward_wait")[1]))

    exchange_shapes = lambda parts: [jax.ShapeDtypeStruct((3,) + p.shape[1:], BF16) for p in parts]

    def early_grads(grads_late):
        gs = [grads_late[n] for n in LATE]
        shapes = [jax.ShapeDtypeStruct((N_CHIPS,) + _half_shape(g.shape[1:]), F32) for g in gs]
        sent["pair"] = _split_start("pair", gs, shapes, "pair_split_late_start")
        return sent["pair"][-1]

    def early_grads_go(after):
        gs, theirs = _split_wait("pair", *sent["pair"][:3], after, "pair_split_late_wait")
        parts = [_pair_add(g, th, core, "pair_add_" + n) for g, th, n in zip(gs, theirs, LATE)]
        sent["late"] = _split_start("scatter", parts, exchange_shapes(parts), "exchange_late_start")
        return sent["late"][-1]

    def last_grads(grads_early):
        parts = _chip_partials(grads_early, EARLY, core, "early")
        sent["early"] = _split_start("scatter", parts, exchange_shapes(parts), "exchange_early_start")
        return sent["early"][-1]

    loss_part, grad_x, grads = _local_step(x, loss_target, first, b_full, g_q_a, g_kv_a, ln1_g, ln1_b, ln2_g, ln2_b, token=g_token,
                                           late_arrived=late_arrived, late_weights=late_weights, early_grads=early_grads, early_grads_go=early_grads_go,
                                           last_grads=last_grads, tables=tables)

    g_out, delta, new_m, new_v = {}, {}, {}, {}
    chip1 = chip.astype(jnp.int32).reshape(1)

    def sum_and_send(names, parts, others, tag):
        totals = [_chip_sum(p, o, chip1, "chip_sum_" + n) for n, p, o in zip(names, parts, others)]
        return _split_start("join", totals, [jax.ShapeDtypeStruct(t.shape, F32) for t in totals], "pair_join_" + tag + "_start")

    def adam(names, joined, after, tag):
        totals, halves = _split_wait("join", *joined[:3], after, "pair_join_" + tag + "_wait")
        for n, mine, theirs in zip(names, totals, halves):
            res = _adamw_big(turn(n, w[n][0]), mine, theirs, turn(n, m[n][0]), turn(n, v[n][0]), core, "adamw_" + n,
                             side_by_side=mine.shape[0] == shards[n].shape[0])
            g_out[n], delta[n], new_m[n], new_v[n] = (turn(n, r) for r in res)

    late_joined = sum_and_send(LATE, *_split_wait("scatter", *sent["late"][:3], grad_x, "exchange_late_wait"), "late")
    early_joined = sum_and_send(EARLY, *_split_wait("scatter", *sent["early"][:3], late_joined[-1], "exchange_early_wait"), "early")
    small_names = [name for name, _ in SMALL]
    sums = _sum_small([grads[name] for name in small_names] + [loss_part])
    loss = sums[-1][0, 0]
    g_small = dict(zip(small_names, sums))
    g_small["b_gate"] = lax.dynamic_slice(g_small["b_gate"], (0, chip * gate_w), (2, gate_w))
    flat = lambda a: a.reshape(-1, a.shape[-1])
    res = _adamw_small(*[[flat(d[name]) for name in small_names] for d in (w, g_small, m, v)])
    g_out.update(g_small)
    for d, r in zip((delta, new_m, new_v), res):
        d.update(zip(small_names, r))
    adam(LATE, late_joined, res[0][0], "late")
    adam(EARLY, early_joined, delta[LATE[-1]], "early")

    lead = lambda d: [d[name].reshape(w[name].shape) for name in order]
    return (loss, grad_x, *lead(g_out), *lead(delta), *lead(new_m), *lead(new_v))
```

```python
import functools
import math

import jax
import jax.numpy as jnp
from jax import lax
from jax.experimental import pallas as pl
from jax.experimental.pallas import tpu as pltpu

F32 = jnp.float32
BF16 = jnp.bfloat16
MESH = pl.DeviceIdType.MESH

D_MODEL = 1024
N_HEADS = 8
LANES = 128
NOPE, ROPE, V_DIM = 64, 32, 64
MLA_QK = NOPE + ROPE
Q_LORA, KV_LORA = 384, 256
DIL_DIM = 64
DIL_PATTERNS = ((128, 1), (512, 4), (2048, 16))
D_FF = 4096
N_CHIPS = 4
N_DEV = 8
IN_WIDTH = 4256
LN_EPS, RMS_EPS = 1e-5, 1e-6
NEG = -1e30
LOG2E, LN2 = 1.4426950408889634, 0.6931471805599453
ALPHA = 2.0 ** 0.25
ROPE_THETA = 10000.0
LR, B1, B2, ADAM_EPS, WD, ADAM_STEP = 0.001, 0.9, 0.999, 1e-8, 0.01, 10

P_LORA, P_KR, P_GATE, P_HALF = 0, 640, 1024, 3072
DIL_GROUP = 4 * LANES
P_DIL = N_HEADS // 2 * DIL_GROUP
LORA_W = Q_LORA + KV_LORA
KR_LANE = NOPE

ATT_T = 512
ROW_T = 512
VMEM_LIMIT = 56 * 1024 * 1024

NN = (((1,), (0,)), ((), ()))
NT = (((1,), (1,)), ((), ()))
TN = (((0,), (0,)), ((), ()))


def _params(sem=None, **kw):
    return pltpu.CompilerParams(dimension_semantics=sem, vmem_limit_bytes=VMEM_LIMIT, **kw)


def _matmul(a, b, *, mode, name, tm, tn, tk, out_dtypes=(F32,), extras=(), epilogue=None, b_shards=False, out_shards=False, after=None):
    pieces = list(a) if isinstance(a, (list, tuple)) else [a]
    n_pc = len(pieces)
    a_shape = (pieces[0].shape[0], sum(p.shape[1] for p in pieces))
    if b_shards:
        n_sh, rows_b, cols_b = b.shape
        b_shape = (rows_b, n_sh * cols_b)
    else:
        b_shape = b.shape
    if mode == "nn":
        (m, k), (k2, n) = a_shape, b_shape
    elif mode == "nt":
        (m, k), (n, k2) = a_shape, b_shape
    else:
        (k, m), (k2, n) = a_shape, b_shape
    assert k == k2, (a_shape, b.shape, mode)
    tm, tn, tk = min(tm, m), min(tn, n), min(tk, k)
    assert m % tm == 0 and n % tn == 0 and k % tk == 0, (name, m, n, k, tm, tn, tk)
    nk = k // tk
    n_ex, n_out = len(extras), len(out_dtypes)
    n_in = n_pc + 1 + n_ex + (after is not None)
    dims = {"nn": NN, "nt": NT, "tn": TN}[mode]
    col_tile = tm if mode == "tn" else tk
    blocks = [p.shape[1] // col_tile for p in pieces]
    firsts = [sum(blocks[:p]) for p in range(n_pc)]
    assert all(p.shape[1] % col_tile == 0 for p in pieces), (name, col_tile)

    def body(*refs):
        a_refs, b_ref = refs[:n_pc], refs[n_pc]
        ex_refs = refs[n_pc + 1:n_pc + 1 + n_ex]
        out_refs = refs[n_in:n_in + n_out]

        def finish(acc):
            outs = epilogue(acc, *[r[...] for r in ex_refs]) if epilogue is not None else (acc,)
            for r, o in zip(out_refs, outs):
                r[...] = o.astype(r.dtype)

        kk = pl.program_id(2)

        def step(a_ref):
            part = lax.dot_general(a_ref[...].astype(BF16), b_ref[...].astype(BF16), dims, preferred_element_type=F32)
            if nk == 1:
                finish(part)
                return
            acc_ref = refs[-1]

            @pl.when(kk == 0)
            def _():
                acc_ref[...] = part

            @pl.when(kk > 0)
            def _():
                acc_ref[...] += part

            @pl.when(kk == nk - 1)
            def _():
                finish(acc_ref[...])

        if n_pc == 1:
            step(a_refs[0])
        else:
            at = pl.program_id(0) if mode == "tn" else kk
            for p in range(n_pc):
                pl.when(jnp.logical_and(at >= firsts[p], at < firsts[p] + blocks[p]))(functools.partial(step, a_refs[p]))

    def a_spec_of(p):
        if n_pc == 1:
            return pl.BlockSpec((tk, tm), lambda i, j, kk: (kk, i)) if mode == "tn" else pl.BlockSpec((tm, tk), lambda i, j, kk: (i, kk))
        col = lambda at: jnp.clip(at - firsts[p], 0, blocks[p] - 1)
        mine = lambda at: jnp.logical_and(at >= firsts[p], at < firsts[p] + blocks[p])
        if mode == "tn":
            return pl.BlockSpec((tk, tm), lambda i, j, kk: (jnp.where(mine(i), kk, 0), col(i)))
        return pl.BlockSpec((tm, tk), lambda i, j, kk: (i, col(kk)))

    b_spec = {"nn": pl.BlockSpec((tk, tn), lambda i, j, kk: (kk, j)),
              "nt": pl.BlockSpec((tn, tk), lambda i, j, kk: (j, kk)),
              "tn": pl.BlockSpec((tk, tn), lambda i, j, kk: (kk, j))}[mode]
    tile = pl.BlockSpec((tm, tn), lambda i, j, kk: (i, j))
    out_spec, out_dims = tile, (m, n)
    if b_shards and mode == "nn":
        per = cols_b // tn
        b_spec = pl.BlockSpec((None, tk, tn), lambda i, j, kk: (j // per, kk, j % per))
    elif b_shards:
        assert mode == "nt"
        per = cols_b // tk
        b_spec = pl.BlockSpec((None, tn, tk), lambda i, j, kk: (kk // per, j, kk % per))
    if out_shards:
        assert not extras and epilogue is None
        per_out = n // N_CHIPS // tn
        out_spec = pl.BlockSpec((None, tm, tn), lambda i, j, kk: (j // per_out, i, j % per_out))
        out_dims = (N_CHIPS, m, n // N_CHIPS)
    outs = pl.pallas_call(
        body, name=name,
        grid=(m // tm, n // tn, nk),
        in_specs=[a_spec_of(p) for p in range(n_pc)] + [b_spec] + [tile] * n_ex + [pl.BlockSpec(memory_space=pl.ANY)] * (after is not None),
        out_specs=[out_spec] * n_out,
        out_shape=[jax.ShapeDtypeStruct(out_dims, dt) for dt in out_dtypes],
        scratch_shapes=[pltpu.VMEM((tm, tn), F32)] if nk > 1 else [],
        compiler_params=_params(("parallel", "parallel", "arbitrary")),
    )(*pieces, b, *extras, *([after] if after is not None else []))
    return outs[0] if n_out == 1 else outs


def _rowwise(fn, *, name, rows, seq, ins, outs, sums=()):
    tm = min(ROW_T, seq)
    n_pos = seq // tm
    n_in, n_out, n_sum = len(ins), len(outs), len(sums)

    def body(*refs):
        vals = fn(*[r[...] for r in refs[:n_in]])
        for r, v in zip(refs[n_in:n_in + n_out], vals[:n_out]):
            r[...] = v.astype(r.dtype)
        first = pl.program_id(0) == 0
        for r, v in zip(refs[n_in + n_out:], vals[n_out:]):
            @pl.when(first)
            def _(r=r, v=v):
                r[...] = v

            @pl.when(jnp.logical_not(first))
            def _(r=r, v=v):
                r[...] += v

    def spec(arr, width, col, kind):
        if kind == "row":
            return pl.BlockSpec((tm, width), lambda i, col=col: (i, col))
        if kind == "pos":
            return pl.BlockSpec((tm, width), lambda i, col=col: (i % n_pos, col))
        return pl.BlockSpec(arr.shape, lambda i: (0,) * arr.ndim)

    res = pl.pallas_call(
        body, name=name,
        grid=(rows // tm,),
        in_specs=[spec(*t) for t in ins],
        out_specs=[pl.BlockSpec((tm, w), lambda i: (i, 0)) for w, _ in outs]
        + [pl.BlockSpec((1, w), lambda i: (0, 0)) for w in sums],
        out_shape=[jax.ShapeDtypeStruct((rows, w), dt) for w, dt in outs]
        + [jax.ShapeDtypeStruct((1, w), F32) for w in sums],
        compiler_params=_params(("arbitrary",)),
    )(*[t[0] for t in ins])
    return res


def _colsum(v):
    return jnp.sum(v, axis=0, keepdims=True)


def _rope_fwd(t, c, s_up, s_dn):
    return t * c + pltpu.roll(t, LANES - 16, 1) * s_up + pltpu.roll(t, 16, 1) * s_dn


def _rope_bwd(d, c, s_up, s_dn):
    return d * c + pltpu.roll(d * s_up, 16, 1) + pltpu.roll(d * s_dn, LANES - 16, 1)


def _rope_tables(seq):
    half = ROPE // 2
    inv = jnp.power(ROPE_THETA, -jnp.arange(half, dtype=F32) / half)
    ang = jnp.arange(seq, dtype=F32)[:, None] * inv[None, :]
    cos, sin = jnp.cos(ang), jnp.sin(ang)
    zeros = jnp.zeros((seq, half), F32)
    lo, hi = jnp.ones((seq, KR_LANE), F32), jnp.ones((seq, LANES - KR_LANE - ROPE), F32)
    c = jnp.concatenate([lo, cos, cos, hi], axis=1)
    c_rope_only = jnp.concatenate([0 * lo, cos, cos, 0 * hi], axis=1)
    s_up = jnp.concatenate([0 * lo, -sin, zeros, 0 * hi], axis=1)
    s_dn = jnp.concatenate([0 * lo, zeros, sin, 0 * hi], axis=1)
    return c, s_up, s_dn, c_rope_only


def _rms(x, g):
    r = lax.rsqrt(jnp.mean(x * x, axis=1, keepdims=True) + RMS_EPS)
    return x * r * g


def _rms_bwd(x, g, dy):
    r = lax.rsqrt(jnp.mean(x * x, axis=1, keepdims=True) + RMS_EPS)
    xh = x * r
    dxh = dy * g
    dx = r * (dxh - xh * jnp.mean(dxh * xh, axis=1, keepdims=True))
    return dx, _colsum(dy * xh)


def _ln_stats(x):
    mu = jnp.mean(x, axis=1, keepdims=True)
    xc = x - mu
    r = lax.rsqrt(jnp.mean(xc * xc, axis=1, keepdims=True) + LN_EPS)
    return xc * r, r


def _ln_bwd(xh, r, g, dy):
    dxh = dy * g
    dx = r * (dxh - jnp.mean(dxh, axis=1, keepdims=True) - xh * jnp.mean(dxh * xh, axis=1, keepdims=True))
    return dx, _colsum(dy * xh), _colsum(dy)


def _table_specs(tables, sub):
    whole = lambda a: pl.BlockSpec(a.shape, lambda b, g: (0,) * a.ndim)
    if len(tables) == 1:
        return [whole(tables[0])]
    return [whole(tables[0]), whole(tables[1]), pl.BlockSpec((sub, 1, LANES), lambda b, g: (g, 0, 0))]


def _biased(s, table_refs, delta, head):
    if delta < table_refs[0].shape[0]:
        s = s + table_refs[0][delta]
    if len(table_refs) == 3:
        s = s - table_refs[2][head, 0:1, 0:1] * table_refs[1][delta]
    return s


def _lane_masks(sub):
    lane = lax.broadcasted_iota(jnp.int32, (1, LANES), 1)
    return [(lane // (LANES // sub) == a).astype(F32) for a in range(sub)]


def _attn_fwd(q, qb0, k, kb0, v, vb0, tables, scale, *, name, batch, seq, sub=1, stride=1, after=None):
    t = ATT_T
    nq = seq // t
    rows = batch * seq
    n_tab = len(tables)

    def body(q_ref, k_ref, v_ref, *rest):
        table_refs = rest[:n_tab]
        o_ref, lse_ref, kb, vtb = rest[n_tab + (after is not None):][:4]
        qbs = rest[n_tab + (after is not None) + 4:]
        masks = _lane_masks(sub)
        for a in range(sub):
            qbs[a][...] = (q_ref[...].astype(F32) * masks[a]).astype(BF16) if sub > 1 else q_ref[...].astype(BF16)
        kb[...] = k_ref[...].astype(BF16)
        vtb[...] = v_ref[...].astype(F32).T.astype(BF16)
        for i in range(nq):
            out_t = None
            for a in range(sub):
                qt = qbs[a][i * t:(i + 1) * t, :]
                logits = [_biased(lax.dot_general(kb[j * t:(j + 1) * t, :], qt, NT, preferred_element_type=F32) * (scale * LOG2E), table_refs, i - j, a)
                          for j in range(i + 1)]
                m = jnp.max(functools.reduce(jnp.maximum, logits), axis=0, keepdims=True)
                ps = [jnp.exp2(s - m) for s in logits]
                l = jnp.sum(functools.reduce(jnp.add, ps), axis=0, keepdims=True)
                acc = functools.reduce(jnp.add, [lax.dot_general(vtb[:, j * t:(j + 1) * t], p.astype(BF16), NN, preferred_element_type=F32)
                                                 for j, p in enumerate(ps)])
                part = acc / l if sub == 1 else (acc / l) * masks[a].T
                out_t = part if out_t is None else out_t + part
                lse_ref[i * t:(i + 1) * t, a * LANES:(a + 1) * LANES] = jnp.broadcast_to((m + jnp.log2(l)) * LN2, (LANES, t)).T
            o_ref[i * t:(i + 1) * t, :] = out_t.T

    slab = lambda b0, step: pl.BlockSpec((seq, LANES), lambda b, g: (b, b0 + step * g))
    groups = N_HEADS // sub
    return pl.pallas_call(
        body, name=name,
        grid=(batch, groups),
        in_specs=[slab(qb0, stride), slab(kb0, stride), slab(vb0, stride)] + _table_specs(tables, sub)
        + [pl.BlockSpec(memory_space=pl.ANY)] * (after is not None),
        out_specs=[slab(0, 1), pl.BlockSpec((seq, sub * LANES), lambda b, g: (b, g))],
        out_shape=[jax.ShapeDtypeStruct((rows, groups * LANES), F32), jax.ShapeDtypeStruct((rows, N_HEADS * LANES), F32)],
        scratch_shapes=[pltpu.VMEM((seq, LANES), BF16), pltpu.VMEM((LANES, seq), BF16)] + [pltpu.VMEM((seq, LANES), BF16)] * sub,
        compiler_params=_params(("arbitrary", "arbitrary")),
    )(q, k, v, *tables, *([after] if after is not None else []))


def _attn_bwd(q, qb0, k, kb0, v, vb0, o, do, lse, tables, scale, *, name, batch, seq, out_dtype, sub=1, stride=1, after=None):
    t = ATT_T
    nq = seq // t
    rows = batch * seq
    n_tab = len(tables)
    groups = N_HEADS // sub
    n_out = 3 if sub == 1 else 1

    def body(q_ref, k_ref, v_ref, o_ref, do_ref, lse_ref, *rest):
        table_refs = rest[:n_tab]
        rest = rest[n_tab + (after is not None):]
        out_refs, (kb, vb, dka, dva), per_head = rest[:n_out], rest[n_out:n_out + 4], rest[n_out + 4:]
        qbs, dobs, qtbs, dotbs = (per_head[g * sub:(g + 1) * sub] for g in range(4))
        masks = _lane_masks(sub)
        kb[...] = k_ref[...].astype(BF16)
        vb[...] = v_ref[...].astype(BF16)
        for a in range(sub):
            qa = q_ref[...].astype(F32) * masks[a] if sub > 1 else q_ref[...].astype(F32)
            doa = do_ref[...] * masks[a] if sub > 1 else do_ref[...]
            qbs[a][...] = qa.astype(BF16)
            dobs[a][...] = doa.astype(BF16)
            qtbs[a][...] = qa.T.astype(BF16)
            dotbs[a][...] = doa.T.astype(BF16)
        first = [True] * nq
        for i in range(nq):
            at = slice(i * t, (i + 1) * t)
            dq_all = None
            for a in range(sub):
                qt, dot = qbs[a][at, :], dobs[a][at, :]
                lse_t = lse_ref[at, a * LANES:a * LANES + 1] * LOG2E
                od = o_ref[at, :] * do_ref[at, :]
                delta = jnp.sum(od * masks[a] if sub > 1 else od, axis=1, keepdims=True)
                dq = None
                for j in range(i + 1):
                    kat = slice(j * t, (j + 1) * t)
                    kt, vt = kb[kat, :], vb[kat, :]
                    p = jnp.exp2(_biased(lax.dot_general(qt, kt, NT, preferred_element_type=F32) * (scale * LOG2E), table_refs, i - j, a) - lse_t)
                    dp = lax.dot_general(dot, vt, NT, preferred_element_type=F32)
                    ds = (p * (dp - delta) * scale).astype(BF16)
                    dk_part = lax.dot_general(qtbs[a][:, at], ds, NN, preferred_element_type=F32)
                    dv_part = lax.dot_general(dotbs[a][:, at], p.astype(BF16), NN, preferred_element_type=F32)
                    if first[j]:
                        dka[:, kat] = dk_part
                        dva[:, kat] = dv_part
                        first[j] = False
                    else:
                        dka[:, kat] += dk_part
                        dva[:, kat] += dv_part
                    dq_part = lax.dot_general(ds, kt, NN, preferred_element_type=F32)
                    dq = dq_part if dq is None else dq + dq_part
                dq = dq * masks[a] if sub > 1 else dq
                dq_all = dq if dq_all is None else dq_all + dq
            out_refs[0][at, 0:LANES] = dq_all.astype(out_refs[0].dtype)
        if sub == 1:
            out_refs[1][...] = dka[...].T.astype(out_refs[1].dtype)
            out_refs[2][...] = dva[...].T.astype(out_refs[2].dtype)
        else:
            out_refs[0][:, LANES:2 * LANES] = dka[...].T.astype(out_refs[0].dtype)
            out_refs[0][:, 2 * LANES:3 * LANES] = dva[...].T.astype(out_refs[0].dtype)
            out_refs[0][:, 3 * LANES:] = jnp.zeros((seq, LANES), out_refs[0].dtype)

    slab = lambda b0, step: pl.BlockSpec((seq, LANES), lambda b, g: (b, b0 + step * g))
    if sub == 1:
        out_specs = [slab(0, 1)] * 3
        out_shape = [jax.ShapeDtypeStruct((rows, N_HEADS * LANES), out_dtype)] * 3
    else:
        out_specs = [pl.BlockSpec((seq, 4 * LANES), lambda b, g: (b, g))]
        out_shape = [jax.ShapeDtypeStruct((rows, groups * 4 * LANES), out_dtype)]
    res = pl.pallas_call(
        body, name=name,
        grid=(batch, groups),
        in_specs=[slab(qb0, stride), slab(kb0, stride), slab(vb0, stride), slab(0, 1), slab(0, 1),
                  pl.BlockSpec((seq, sub * LANES), lambda b, g: (b, g))] + _table_specs(tables, sub)
        + [pl.BlockSpec(memory_space=pl.ANY)] * (after is not None),
        out_specs=out_specs, out_shape=out_shape,
        scratch_shapes=[pltpu.VMEM((seq, LANES), BF16)] * 2 + [pltpu.VMEM((LANES, seq), F32)] * 2
        + [pltpu.VMEM((seq, LANES), BF16)] * (2 * sub) + [pltpu.VMEM((LANES, seq), BF16)] * (2 * sub),
        compiler_params=_params(("arbitrary", "arbitrary")),
    )(q, k, v, o, do, lse, *tables, *([after] if after is not None else []))
    return res if sub == 1 else res[0]


def _attention_tables(seq, zero=0):
    n = seq // ATT_T
    pos = jnp.arange(ATT_T, dtype=jnp.int32) + zero
    dist = jnp.arange(n, dtype=jnp.int32)[:, None, None] * ATT_T + pos[None, :, None] - pos[None, None, :]
    causal = jnp.where(dist[:1] >= 0, 0.0, NEG).astype(F32)
    count = jnp.zeros(dist.shape, F32)
    for window, dilation in DIL_PATTERNS:
        count += ((dist >= 0) & (dist <= window) & (dist % dilation == 0)).astype(F32)
    held = jnp.where(count > 0, jnp.log2(jnp.maximum(count, 1.0)), NEG).astype(F32)
    slopes = jnp.asarray([2.0 ** (-8.0 * (i + 1) / N_HEADS) for i in range(N_HEADS)], F32)
    slopes = jnp.broadcast_to(slopes[:, None, None], (N_HEADS, 1, LANES))
    turned = lambda a: jnp.swapaxes(a, 1, 2)
    far = dist.astype(F32) * LOG2E
    return ((causal,), (turned(causal),)), ((held, far, slopes), (turned(held), turned(far), slopes))


def _pad_heads(w, width):
    kdim, n = w.shape[0], w.shape[1] // width
    return jnp.pad(w.reshape(kdim, n, width), ((0, 0), (0, 0), (0, LANES - width))).reshape(kdim, n * LANES)


def _unpad_heads(w, width):
    kdim, n = w.shape[0], w.shape[1] // LANES
    return w.reshape(kdim, n, LANES)[:, :, :width].reshape(kdim, n * width)


def _pad_head_rows(w, width):
    n, kdim = w.shape[0] // width, w.shape[1]
    return jnp.pad(w.reshape(n, width, kdim), ((0, 0), (0, LANES - width), (0, 0))).reshape(n * LANES, kdim)


def _unpad_head_rows(w, width):
    n, kdim = w.shape[0] // LANES, w.shape[1]
    return w.reshape(n, LANES, kdim)[:, :width].reshape(n * width, kdim)


def _pad_w_in_t(wt):
    n_qkv, pair = 3 * N_HEADS * DIL_DIM, 2 * DIL_DIM
    zeros = lambda n: jnp.zeros((n, wt.shape[1]), wt.dtype)
    main = jnp.concatenate([wt[:LORA_W], zeros(KR_LANE), wt[LORA_W:LORA_W + ROPE], zeros(P_GATE - P_KR - KR_LANE - ROPE),
                            wt[LORA_W + ROPE + n_qkv:]], axis=0)
    qkv = wt[LORA_W + ROPE:LORA_W + ROPE + n_qkv].reshape(3, N_HEADS // 2, pair, wt.shape[1]).transpose(1, 0, 2, 3)
    dil = jnp.pad(qkv, ((0, 0), (0, 1), (0, 0), (0, 0))).reshape(P_DIL, wt.shape[1])
    return main, dil


def _unpad_w_in_t(gt):
    qkv = gt[P_HALF:].reshape(N_HEADS // 2, 4, 2 * DIL_DIM, gt.shape[1])[:, :3].transpose(1, 0, 2, 3).reshape(3 * N_HEADS * DIL_DIM, gt.shape[1])
    return jnp.concatenate([gt[P_LORA:P_KR], gt[P_KR + KR_LANE:P_KR + KR_LANE + ROPE], qkv, gt[P_GATE:P_HALF]], axis=0)


def _split_ukv(w):
    w3 = w.reshape(w.shape[0], N_HEADS, NOPE + V_DIM)
    return (_pad_heads(w3[:, :, :NOPE].reshape(w.shape[0], -1), NOPE),
            _pad_heads(w3[:, :, NOPE:].reshape(w.shape[0], -1), V_DIM))


def _merge_ukv(g_k, g_v):
    kdim = g_k.shape[0]
    k3 = _unpad_heads(g_k, NOPE).reshape(kdim, N_HEADS, NOPE)
    v3 = _unpad_heads(g_v, V_DIM).reshape(kdim, N_HEADS, V_DIM)
    return jnp.concatenate([k3, v3], axis=2).reshape(kdim, N_HEADS * (NOPE + V_DIM))


def _pad_rows(w, width):
    return _pad_heads(w.T, width).T


def _unpad_rows(g, width):
    return _unpad_heads(g.T, width).T


def _join_cols(w):
    return w.transpose(1, 0, 2).reshape(w.shape[1], N_CHIPS * w.shape[2])


def _split_cols(g):
    return g.reshape(g.shape[0], N_CHIPS, g.shape[1] // N_CHIPS).transpose(1, 0, 2)


def _local_step(x3, target3, wg, b_gate, g_q_a, g_kv_a, ln1_g, ln1_b, ln2_g, ln2_b, token=None, late_arrived=None, late_weights=None,
                early_grads=None, early_grads_go=None, last_grads=None, tables=None):
    w_main_t, w_dil_t = _pad_w_in_t(wg["w_in"].reshape(IN_WIDTH, D_MODEL))
    w_uq_pt = _pad_head_rows(wg["w_uq"].reshape(N_HEADS * MLA_QK, Q_LORA), MLA_QK)
    w_ukv = _join_cols(wg["w_ukv"])
    batch, seq, _ = x3.shape
    rows = batch * seq
    x = x3.reshape(rows, D_MODEL)
    target = target3.reshape(rows, D_MODEL)
    row = functools.partial(_rowwise, rows=rows, seq=seq)
    mm = _matmul

    w_uk_p, w_uv_p = _split_ukv(w_ukv)
    b0, b1 = b_gate[0:1], b_gate[1:2]
    rope_c, rope_up, rope_dn, rope_c_only = _rope_tables(seq)
    (mla_bwd_tables, mla_fwd_tables), (dil_bwd_tables, dil_fwd_tables) = _attention_tables(seq) if tables is None else tables
    scale_mla, scale_dil = MLA_QK ** -0.5, DIL_DIM ** -0.5
    lora0, kr0, gate0 = P_LORA // LORA_W, P_KR // LANES, P_GATE // D_MODEL

    proj = mm(x, w_main_t, mode="nt", name="proj", tm=1024, tn=1536, tk=1024, after=token)
    proj_d = mm(x, w_dil_t, mode="nt", name="proj_dil", tm=1024, tn=1024, tk=1024, out_dtypes=(BF16,))

    def prep(lora, gq, gkv):
        return _rms(lora[:, :Q_LORA], gq), _rms(lora[:, Q_LORA:], gkv)

    qn, kvn = row(prep, name="mla_rms", ins=[(proj, LORA_W, lora0, "row"), (g_q_a, 0, 0, "full"), (g_kv_a, 0, 0, "full")],
                  outs=[(Q_LORA, BF16), (KV_LORA, BF16)])
    q_lin = mm(qn, w_uq_pt, mode="nt", name="q_up", tm=1024, tn=1024, tk=Q_LORA)
    k_lin = mm(kvn, w_uk_p, mode="nn", name="k_up", tm=1024, tn=1024, tk=KV_LORA)
    v_a = mm(kvn, w_uv_p, mode="nn", name="v_up", tm=1024, tn=1024, tk=KV_LORA, out_dtypes=(BF16,))

    def rope_qk(ql, kl, kr, c, up, dn):
        k_rot = _rope_fwd(kr, c, up, dn)
        qs = [_rope_fwd(ql[:, h * LANES:(h + 1) * LANES], c, up, dn) for h in range(N_HEADS)]
        ks = [kl[:, h * LANES:(h + 1) * LANES] + k_rot for h in range(N_HEADS)]
        return jnp.concatenate(qs, axis=1), jnp.concatenate(ks, axis=1)

    pos = lambda tab: (tab, LANES, 0, "pos")
    q_a, k_a = row(rope_qk, name="rope_qk",
                   ins=[(q_lin, D_MODEL, 0, "row"), (k_lin, D_MODEL, 0, "row"), (proj, LANES, kr0, "row"), pos(rope_c), pos(rope_up), pos(rope_dn)],
                   outs=[(N_HEADS * LANES, BF16), (N_HEADS * LANES, BF16)])
    o_a, lse_a = _attn_fwd(q_a, 0, k_a, 0, v_a, 0, mla_fwd_tables, scale_mla, name="mla_fwd", batch=batch, seq=seq)
    arrived = None if late_arrived is None else late_arrived(o_a)
    o_b, lse_b = _attn_fwd(proj_d, 0, proj_d, 1, proj_d, 2, dil_fwd_tables, scale_dil, name="dil_fwd", batch=batch, seq=seq, sub=2, stride=4, after=arrived)
    late = wg if late_weights is None else late_weights(o_b)
    w_oa_p = _pad_rows(_join_cols(late["w_o_mla"]), V_DIM)
    w_ob = _join_cols(late["w_o_dil"])
    w_out, w_ff1, w_ff2 = late["w_out"].reshape(D_MODEL, D_MODEL), late["w_ff1"], late["w_ff2"].reshape(D_FF, D_MODEL)
    y_a = mm(o_a, w_oa_p, mode="nn", name="o_mla", tm=1024, tn=1024, tk=1024, out_dtypes=(BF16,))
    y_b = mm(o_b, w_ob, mode="nn", name="o_dil", tm=1024, tn=1024, tk=1024, out_dtypes=(BF16,))

    def gate(t0, t1, c0, c1, ya, yb):
        return (jax.nn.sigmoid(t0 + c0) * ya + jax.nn.sigmoid(t1 + c1) * yb,)

    gate_ins = [(proj, D_MODEL, gate0, "row"), (proj, D_MODEL, gate0 + 1, "row"), (b0, 0, 0, "full"), (b1, 0, 0, "full")]
    (u,) = row(gate, name="gate", ins=gate_ins + [(y_a, D_MODEL, 0, "row"), (y_b, D_MODEL, 0, "row")], outs=[(D_MODEL, BF16)])
    mixed = mm(u, w_out, mode="nn", name="mix", tm=1024, tn=1024, tk=1024)

    def ln1(xv, mv, g, b):
        r1 = ALPHA * xv + mv
        xh, _ = _ln_stats(r1)
        return r1, xh * g + b

    r1, h = row(ln1, name="ln1", ins=[(x, D_MODEL, 0, "row"), (mixed, D_MODEL, 0, "row"), (ln1_g, 0, 0, "full"), (ln1_b, 0, 0, "full")],
                outs=[(D_MODEL, F32), (D_MODEL, F32)])

    def relu2(acc):
        r = jnp.maximum(acc, 0.0)
        return acc, r * r

    a_ff, z = mm(h, w_ff1, mode="nn", name="ff1", tm=1024, tn=1024, tk=1024, out_dtypes=(BF16, BF16), epilogue=relu2, b_shards=True)
    f = mm(z, w_ff2, mode="nn", name="ff2", tm=1024, tn=1024, tk=2048)

    def ln2_loss(hv, fv, tv, g, b):
        xh, r = _ln_stats(ALPHA * hv + fv)
        err = xh * g + b - tv
        dy = err * (1.0 / D_MODEL)
        dr2, dg, db = _ln_bwd(xh, r, g, dy)
        loss = jnp.sum(_colsum(err * err), axis=1, keepdims=True) * (0.5 / D_MODEL)
        return dr2, jnp.broadcast_to(loss, (1, LANES)), dg, db

    dr2, loss_l, d_ln2_g, d_ln2_b = row(
        ln2_loss, name="ln2_loss",
        ins=[(h, D_MODEL, 0, "row"), (f, D_MODEL, 0, "row"), (target, D_MODEL, 0, "row"), (ln2_g, 0, 0, "full"), (ln2_b, 0, 0, "full")],
        outs=[(D_MODEL, F32)], sums=[LANES, D_MODEL, D_MODEL])

    d_w_ff2 = mm(z, dr2, mode="tn", name="d_w_ff2", tm=1024, tn=1024, tk=2048)
    da = mm(dr2, w_ff2, mode="nt", name="d_ff_act", tm=1024, tn=1024, tk=1024, out_dtypes=(BF16,), extras=(a_ff,),
            epilogue=lambda acc, av: (acc * (2.0 * jnp.maximum(av.astype(F32), 0.0)),))
    d_w_ff1 = mm(h, da, mode="tn", name="d_w_ff1", tm=1024, tn=1024, tk=2048, out_shards=True)
    dh = mm(da, w_ff1, mode="nt", name="d_h", tm=1024, tn=1024, tk=1024, extras=(dr2,), epilogue=lambda acc, rv: (acc + ALPHA * rv,), b_shards=True)

    def ln1_bwd(dhv, r1v, g):
        xh, r = _ln_stats(r1v)
        return _ln_bwd(xh, r, g, dhv)

    dr1, d_ln1_g, d_ln1_b = row(ln1_bwd, name="ln1_bwd", ins=[(dh, D_MODEL, 0, "row"), (r1, D_MODEL, 0, "row"), (ln1_g, 0, 0, "full")],
                                outs=[(D_MODEL, F32)], sums=[D_MODEL, D_MODEL])
    d_w_out = mm(u, dr1, mode="tn", name="d_w_out", tm=1024, tn=1024, tk=1024)
    du = mm(dr1, w_out, mode="nt", name="d_u", tm=1024, tn=1024, tk=1024, out_dtypes=(BF16,))

    def gate_bwd(t0, t1, c0, c1, ya, yb, duv):
        s0, s1 = jax.nn.sigmoid(t0 + c0), jax.nn.sigmoid(t1 + c1)
        dt0 = duv * ya * s0 * (1.0 - s0)
        dt1 = duv * yb * s1 * (1.0 - s1)
        return duv * s0, duv * s1, jnp.concatenate([dt0, dt1], axis=1), jnp.concatenate([_colsum(dt0), _colsum(dt1)], axis=1)

    dy_a, dy_b, d_gates, d_b_gate = row(
        gate_bwd, name="gate_bwd", ins=gate_ins + [(y_a, D_MODEL, 0, "row"), (y_b, D_MODEL, 0, "row"), (du, D_MODEL, 0, "row")],
        outs=[(D_MODEL, BF16), (D_MODEL, BF16), (2 * D_MODEL, BF16)], sums=[2 * D_MODEL])
    d_w_oa_p = mm(o_a, dy_a, mode="tn", name="d_w_o_mla", tm=1024, tn=1024, tk=1024)
    d_w_ob = mm(o_b, dy_b, mode="tn", name="d_w_o_dil", tm=1024, tn=1024, tk=1024)
    grads = dict(w_o_mla=_split_cols(_unpad_rows(d_w_oa_p, V_DIM)), w_o_dil=_split_cols(d_w_ob),
                 w_out=d_w_out.reshape(N_CHIPS, D_MODEL // N_CHIPS, D_MODEL), w_ff1=d_w_ff1, w_ff2=d_w_ff2.reshape(N_CHIPS, D_FF // N_CHIPS, D_MODEL))
    sent = None if early_grads is None else early_grads(grads)
    do_a = mm(dy_a, w_oa_p, mode="nt", name="d_o_mla", tm=1024, tn=1024, tk=1024, after=sent)
    do_b = mm(dy_b, w_ob, mode="nt", name="d_o_dil", tm=1024, tn=1024, tk=1024)
    dq_a, dk_a, dv_a = _attn_bwd(q_a, 0, k_a, 0, v_a, 0, o_a, do_a, lse_a, mla_bwd_tables, scale_mla,
                                 name="mla_bwd", batch=batch, seq=seq, out_dtype=F32)
    going = None if early_grads_go is None else early_grads_go(dq_a)
    d_qkv_d = _attn_bwd(proj_d, 0, proj_d, 1, proj_d, 2, o_b, do_b, lse_b, dil_bwd_tables, scale_dil,
                        name="dil_bwd", batch=batch, seq=seq, out_dtype=BF16, sub=2, stride=4, after=going)

    def mla_post(dq, dk, c, up, dn, c_only):
        dqs = [_rope_bwd(dq[:, h * LANES:(h + 1) * LANES], c, up, dn) for h in range(N_HEADS)]
        dk_sum = dk[:, :LANES]
        for h in range(1, N_HEADS):
            dk_sum = dk_sum + dk[:, h * LANES:(h + 1) * LANES]
        return jnp.concatenate(dqs, axis=1), _rope_bwd(dk_sum, c_only, up, dn)

    dq_lin, d_kr = row(mla_post, name="mla_unrope",
                       ins=[(dq_a, D_MODEL, 0, "row"), (dk_a, D_MODEL, 0, "row"), pos(rope_c), pos(rope_up), pos(rope_dn), pos(rope_c_only)],
                       outs=[(N_HEADS * LANES, BF16), (LANES, BF16)])
    d_w_uq_pt = mm(dq_lin, qn, mode="tn", name="d_w_uq", tm=1024, tn=Q_LORA, tk=1024)
    d_w_uk_p = mm(kvn, dk_a, mode="tn", name="d_w_uk", tm=KV_LORA, tn=1024, tk=1024)
    d_w_uv_p = mm(kvn, dv_a, mode="tn", name="d_w_uv", tm=KV_LORA, tn=1024, tk=1024)
    d_qn = mm(dq_lin, w_uq_pt, mode="nn", name="d_qn", tm=1024, tn=Q_LORA, tk=1024)
    d_kvn_k = mm(dk_a, w_uk_p, mode="nt", name="d_kvn_k", tm=1024, tn=KV_LORA, tk=1024)
    d_kvn = mm(dv_a, w_uv_p, mode="nt", name="d_kvn", tm=1024, tn=KV_LORA, tk=1024, extras=(d_kvn_k,), epilogue=lambda acc, e: (acc + e,))

    def rms_bwd(lora, dq, dkv, dkr, gq, gkv):
        dxq, dgq = _rms_bwd(lora[:, :Q_LORA], gq, dq)
        dxk, dgk = _rms_bwd(lora[:, Q_LORA:], gkv, dkv)
        tail = jnp.zeros((dxq.shape[0], P_GATE - P_KR - LANES), F32)
        return jnp.concatenate([dxq, dxk, dkr.astype(F32), tail], axis=1), dgq, dgk

    d_tail, d_g_q_a, d_g_kv_a = row(
        rms_bwd, name="mla_rms_bwd",
        ins=[(proj, LORA_W, lora0, "row"), (d_qn, Q_LORA, 0, "row"), (d_kvn, KV_LORA, 0, "row"), (d_kr, LANES, 0, "row"),
             (g_q_a, 0, 0, "full"), (g_kv_a, 0, 0, "full")],
        outs=[(P_GATE, BF16)], sums=[Q_LORA, KV_LORA])
    d_proj = [d_tail, d_gates, d_qkv_d]
    d_w_in_pt = mm(d_proj, x, mode="tn", name="d_w_in", tm=1024, tn=1024, tk=1024)
    grads.update(w_in=_unpad_w_in_t(d_w_in_pt).reshape(N_CHIPS, IN_WIDTH // N_CHIPS, D_MODEL),
                 w_uq=_unpad_head_rows(d_w_uq_pt, MLA_QK).reshape(N_CHIPS, N_HEADS * MLA_QK // N_CHIPS, Q_LORA),
                 w_ukv=_split_cols(_merge_ukv(d_w_uk_p, d_w_uv_p)))
    leaving = None if last_grads is None else last_grads(grads)
    grad_x = mm(d_proj, jnp.concatenate([w_main_t, w_dil_t], axis=0), mode="nn", name="d_x", tm=1024, tn=1024, tk=1024, extras=(dr1,), epilogue=lambda acc, rv: (acc + ALPHA * rv,),
                after=leaving)

    grads.update(
        b_gate=d_b_gate.reshape(2, D_MODEL), g_q_a=d_g_q_a, g_kv_a=d_g_kv_a, ln1_g=d_ln1_g, ln1_b=d_ln1_b, ln2_g=d_ln2_g, ln2_b=d_ln2_b)
    return loss_l, grad_x.reshape(batch, seq, D_MODEL), grads


BIG = ("w_in", "w_uq", "w_ukv", "w_o_mla", "w_o_dil", "w_out", "w_ff1", "w_ff2")
SMALL = (("b_gate", 2 * D_MODEL), ("g_q_a", Q_LORA), ("g_kv_a", KV_LORA), ("ln1_g", D_MODEL), ("ln1_b", D_MODEL),
         ("ln2_g", D_MODEL), ("ln2_b", D_MODEL))
TRANSPOSED = ("w_in", "w_uq")
D2D_PIECES = (4, 2, 1)
ANY = pl.BlockSpec(memory_space=pl.ANY)
SIDE_EFFECTS = pltpu.CompilerParams(has_side_effects=True)


def _place():
    x, y, c = lax.axis_index("x"), lax.axis_index("y"), lax.axis_index("c")
    return x, y, c, ((1 - x, y), (x, 1 - y), (1 - x, 1 - y))


def _half_axis(shape):
    return 0 if shape[0] % 32 == 0 else 1


def _half_shape(shape):
    return (shape[0] // 2, shape[1]) if _half_axis(shape) == 0 else (shape[0], shape[1] // 2)


def _window(ref, lead, shape, which=None, pieces=False):
    axis = _half_axis(shape)
    size = shape[axis] if which is None else shape[axis] // 2
    base = 0 if which is None else which * size
    tile = (16, LANES)[axis]
    count = next(c for c in D2D_PIECES if size % (tile * c) == 0) if pieces else 1
    step = size // count
    spans = [pl.ds(pl.multiple_of(base + i * step, tile), step) for i in range(count)]
    refs = [ref.at[(*lead, s)] if axis == 0 else ref.at[(*lead, slice(None), s)] for s in spans]
    return refs if pieces else refs[0]


def _remote(src, dst, send, recv, to):
    return pltpu.make_async_remote_copy(src_ref=src, dst_ref=dst, send_sem=send, recv_sem=recv, device_id=to, device_id_type=MESH)


def _pair_split(grads, name):
    n = len(grads)

    def body(*refs):
        srcs, outs, (send, recv) = refs[:n], refs[n:2 * n], refs[2 * n:]
        x, y, c, _ = _place()
        for t in range(n):
            for s in range(N_CHIPS):
                _remote(_window(srcs[t], (s,), grads[t].shape[1:], 1 - c), outs[t].at[s], send.at[t], recv.at[t], (x, y, 1 - c)).start()
        for t in range(n):
            _remote(_window(srcs[t], (slice(None),), grads[t].shape[1:], 1 - c), outs[t], send.at[t], recv.at[t], (x, y, 1 - c)).wait()

    return pl.pallas_call(
        body, name=name, in_specs=[ANY] * n, out_specs=[ANY] * n,
        out_shape=[jax.ShapeDtypeStruct((N_CHIPS,) + _half_shape(g.shape[1:]), g.dtype) for g in grads],
        scratch_shapes=[pltpu.SemaphoreType.DMA((n,)), pltpu.SemaphoreType.DMA((n,))],
        compiler_params=SIDE_EFFECTS,
    )(*grads)


HBM = pl.BlockSpec(memory_space=pltpu.HBM)
SEM = pl.BlockSpec(memory_space=pltpu.SEMAPHORE)
SPLIT = pltpu.CompilerParams(has_side_effects=pltpu.SideEffectType.DATAFLOW_SIDE_EFFECTING)


def _in_hbm(a):
    return pltpu.with_memory_space_constraint(a, pltpu.HBM)


def _split_copies(kind, srcs, lands):
    x, y, c, chips = _place()
    out = []
    for t in range(len(srcs)):
        if kind == "pair":
            out += [(t, s % 3, _window(srcs[t], (s,), srcs[t].shape[1:], 1 - c), lands[t].at[s], (x, y, 1 - c)) for s in range(N_CHIPS)]
            continue
        if kind == "join":
            out += [(t, 0, a, b, (x, y, 1 - c)) for a, b in zip(_window(srcs[t], (), srcs[t].shape, None, True), _window(lands[t], (), srcs[t].shape, None, True))]
            continue
        if kind == "forward":
            shape, sibling = srcs[t].shape, (x, y, 1 - c)
            out += [(t, 0, a, b, sibling) for a, b in zip(_window(srcs[t], (), shape, None, True), _window(lands[t], (2 * x + y,), shape, None, True))]
            out += [(t, j, a, a, sibling) for j, (cx, cy) in enumerate(chips) for a in _window(lands[t], (2 * cx + cy,), shape, c, True)]
            continue
        for j, (cx, cy) in enumerate(chips):
            if kind == "gather":
                shape = srcs[t].shape
                out.append((t, j, _window(srcs[t], (), shape, c), _window(lands[t], (2 * x + y,), shape, c), (cx, cy, c)))
            else:
                out.append((t, j, srcs[t].at[2 * cx + cy], lands[t].at[j], (cx, cy, c)))
    return out


def _split_start(kind, srcs, land_shapes, name, lands=None, after=None):
    n = len(srcs)

    def body(*refs):
        src_refs, land_refs, sems, token = refs[:n], refs[n:2 * n], refs[-7 - 2 * n:-1 - 2 * n], refs[-1]
        for t, j, s, d, to in _split_copies(kind, src_refs, land_refs):
            _remote(s, d, sems[j], sems[3 + j], to).start()
        token[...] = jnp.zeros_like(token)

    lands = [_in_hbm(lax.empty(s.shape, s.dtype)) for s in land_shapes] if lands is None else list(lands)
    thru = [pltpu.HBM(a.shape, a.dtype) for a in list(srcs) + lands]
    res = pl.pallas_call(
        body, name=name,
        out_shape=(*[pltpu.SemaphoreType.DMA(())] * 6, *thru, jax.ShapeDtypeStruct((8, LANES), F32)),
        in_specs=[HBM] * (2 * n) + [ANY] * (after is not None), out_specs=(*[SEM] * 6, *[HBM] * (2 * n), pl.BlockSpec(memory_space=pltpu.VMEM)),
        input_output_aliases={i: 6 + i for i in range(2 * n)}, compiler_params=SPLIT,
    )(*[_in_hbm(s) for s in srcs], *lands, *([after] if after is not None else []))
    return res[:6], res[6:6 + n], res[6 + n:6 + 2 * n], res[-1]


def _split_wait(kind, sems, srcs, lands, after, name):
    n = len(srcs)

    def body(*refs):
        src_refs, land_refs, sem_refs = refs[:n], refs[n:2 * n], refs[2 * n:2 * n + 6]
        for t, j, s, d, to in _split_copies(kind, src_refs, land_refs):
            cp = _remote(s, d, sem_refs[j], sem_refs[3 + j], to)
            cp.wait_send()
            cp.wait_recv()

    res = pl.pallas_call(
        body, name=name, out_shape=[pltpu.HBM(a.shape, a.dtype) for a in list(srcs) + list(lands)],
        in_specs=[HBM] * (2 * n) + [SEM] * 6 + [ANY], out_specs=[HBM] * (2 * n),
        input_output_aliases={i: i for i in range(2 * n)}, compiler_params=SPLIT,
    )(*srcs, *lands, *sems, after)
    return res[:n], res[n:]


def _sum_all_devices(vec, name):
    n_rows = vec.shape[0]

    def body(v_ref, out_ref, buf, send, recv):
        x, y, c, _ = _place()
        me = 4 * x + 2 * y + c
        buf[me] = v_ref[...]
        flips = [(a, b, d) for a in (0, 1) for b in (0, 1) for d in (0, 1)][1:]
        copies = []
        for r, (a, b, d) in enumerate(flips):
            px, py, pc = (1 - x if a else x), (1 - y if b else y), (1 - c if d else c)
            copies.append(pltpu.make_async_remote_copy(src_ref=v_ref, dst_ref=buf.at[me], send_sem=send.at[r], recv_sem=recv.at[r],
                                                       device_id=(px, py, pc), device_id_type=MESH))
            copies[-1].start()
        for r, (a, b, d) in enumerate(flips):
            px, py, pc = (1 - x if a else x), (1 - y if b else y), (1 - c if d else c)
            pltpu.make_async_remote_copy(src_ref=v_ref, dst_ref=buf.at[4 * px + 2 * py + pc], send_sem=send.at[r], recv_sem=recv.at[r],
                                         device_id=(px, py, pc), device_id_type=MESH).wait_recv()
        for cp in copies:
            cp.wait_send()
        total = buf[0]
        for k in range(1, N_DEV):
            total = total + buf[k]
        out_ref[...] = total

    vmem = pl.BlockSpec(memory_space=pltpu.VMEM)
    return pl.pallas_call(
        body, name=name, in_specs=[vmem], out_specs=vmem, out_shape=jax.ShapeDtypeStruct(vec.shape, F32),
        scratch_shapes=[pltpu.VMEM((N_DEV, n_rows, LANES), F32), pltpu.SemaphoreType.DMA((N_DEV - 1,)), pltpu.SemaphoreType.DMA((N_DEV - 1,))],
        compiler_params=pltpu.CompilerParams(has_side_effects=True),
    )(vec)


def _half_tile(half, width):
    t = half
    while t * width * 4 > (2 << 20) and t % 32 == 0:
        t //= 2
    return t


def _pair_add(g, theirs, core, name):
    _, half, width = theirs.shape
    t = _half_tile(half, width)
    n = half // t

    def body(c_ref, a_ref, b_ref, o_ref):
        o_ref[...] = (a_ref[...] + b_ref[...]).astype(BF16)

    tile = pl.BlockSpec((1, t, width), lambda j, i, c_ref: (j, i, 0))
    if _half_axis(g.shape[1:]) == 0:
        mine = pl.BlockSpec((1, t, width), lambda j, i, c_ref: (j, c_ref[0] * n + i, 0))
    else:
        mine = pl.BlockSpec((1, t, width), lambda j, i, c_ref: (j, i, c_ref[0]))
    return pl.pallas_call(
        body, name=name,
        grid_spec=pltpu.PrefetchScalarGridSpec(num_scalar_prefetch=1, grid=(N_CHIPS, n), in_specs=[mine, tile], out_specs=tile),
        out_shape=jax.ShapeDtypeStruct(theirs.shape, BF16), compiler_params=_params(("parallel", "parallel")),
    )(core, g, theirs)


def _chip_sum(part, others, chip, name, after=None):
    _, half, width = part.shape
    t = _half_tile(half, width)

    def body(s_ref, mine, p0, p1, p2, *rest):
        o_ref = rest[-1]
        o_ref[...] = ((mine[0].astype(F32) + p0[0].astype(F32)) + p1[0].astype(F32)) + p2[0].astype(F32)

    return pl.pallas_call(
        body, name=name,
        grid_spec=pltpu.PrefetchScalarGridSpec(
            num_scalar_prefetch=1, grid=(half // t,),
            in_specs=[pl.BlockSpec((1, t, width), lambda i, s_ref: (s_ref[0], i, 0))]
            + [pl.BlockSpec((1, t, width), lambda i, s_ref, j=j: (j, i, 0)) for j in range(3)] + [pl.BlockSpec(memory_space=pl.ANY)] * (after is not None),
            out_specs=pl.BlockSpec((t, width), lambda i, s_ref: (i, 0))),
        out_shape=jax.ShapeDtypeStruct((half, width), F32), compiler_params=_params(("parallel",)),
    )(chip, part, others, others, others, *([after] if after is not None else []))


EARLY = ("w_in", "w_uq", "w_ukv")
LATE = ("w_o_mla", "w_o_dil", "w_out", "w_ff1", "w_ff2")


def _chip_partials(grads, names, core, tag):
    gs = [grads[n] for n in names]
    theirs = _pair_split(gs, "pair_split_" + tag)
    return [_pair_add(g, th, core, "pair_add_" + n) for g, th, n in zip(gs, theirs, names)]


def _sum_small(vals):
    n_in = len(vals)
    n_rows = sum(a.shape[0] * a.shape[1] // LANES for a in vals)
    pad_rows = -(-n_rows // 8) * 8

    def chunks(refs):
        return [(ref, a, j) for ref in refs for a in range(ref.shape[0]) for j in range(ref.shape[1] // LANES)]

    def body(*refs):
        ins, outs, (buf, send, recv) = refs[:n_in], refs[n_in:2 * n_in], refs[2 * n_in:]
        x, y, c, _ = _place()
        me = 4 * x + 2 * y + c
        for r, (ref, a, j) in enumerate(chunks(ins)):
            buf[me, r:r + 1, :] = ref[a:a + 1, j * LANES:(j + 1) * LANES]
        if pad_rows > n_rows:
            buf[me, n_rows:pad_rows, :] = jnp.zeros((pad_rows - n_rows, LANES), F32)
        flips = [(a, b, d) for a in (0, 1) for b in (0, 1) for d in (0, 1)][1:]
        peers = [((1 - x if a else x), (1 - y if b else y), (1 - c if d else c)) for a, b, d in flips]
        copies = [_remote(buf.at[me], buf.at[me], send.at[r], recv.at[r], peer) for r, peer in enumerate(peers)]
        for cp in copies:
            cp.start()
        for r, (px, py, pc) in enumerate(peers):
            _remote(buf.at[me], buf.at[4 * px + 2 * py + pc], send.at[r], recv.at[r], (px, py, pc)).wait_recv()
        for cp in copies:
            cp.wait_send()
        total = buf[0]
        for k in range(1, N_DEV):
            total = total + buf[k]
        for r, (ref, a, j) in enumerate(chunks(outs)):
            ref[a:a + 1, j * LANES:(j + 1) * LANES] = total[r:r + 1, :]

    vmem = pl.BlockSpec(memory_space=pltpu.VMEM)
    return pl.pallas_call(
        body, name="sum_small", in_specs=[vmem] * n_in, out_specs=[vmem] * n_in,
        out_shape=[jax.ShapeDtypeStruct(a.shape, F32) for a in vals],
        scratch_shapes=[pltpu.VMEM((N_DEV, pad_rows, LANES), F32), pltpu.SemaphoreType.DMA((N_DEV - 1,)), pltpu.SemaphoreType.DMA((N_DEV - 1,))],
        compiler_params=SIDE_EFFECTS,
    )(*vals)


def _adam_math(w, g, m, v):
    nm = B1 * m + (1.0 - B1) * g
    nv = B2 * v + (1.0 - B2) * (g * g)
    m_hat = nm / (1.0 - B1 ** ADAM_STEP)
    v_hat = nv / (1.0 - B2 ** ADAM_STEP)
    return -LR * (m_hat / (jnp.sqrt(v_hat) + ADAM_EPS) + WD * w), nm, nv


def _adamw_big(w, mine, theirs, m, v, core, name, side_by_side=False):
    rows, width = w.shape
    if side_by_side:
        t = next(c for c in (152, 96, 64, 32, 16, 8) if rows % c == 0)
        hb = None
        half_spec = pl.BlockSpec((t, width // 2), lambda i, c_ref: (i, 0))
    else:
        t = next(c for c in (256, 128, 64, 32, 16, 8) if (rows // 2) % c == 0)
        hb = rows // 2 // t
        half_spec = pl.BlockSpec((t, width), lambda i, c_ref: (i % hb, 0))

    def body(c_ref, w_ref, a_ref, b_ref, m_ref, v_ref, g_ref, d_ref, nm_ref, nv_ref):
        south = c_ref[0] == 0
        if side_by_side:
            g = jnp.where(south, jnp.concatenate([a_ref[...], b_ref[...]], axis=1), jnp.concatenate([b_ref[...], a_ref[...]], axis=1))
        else:
            g = jnp.where((pl.program_id(0) < hb) == south, a_ref[...], b_ref[...])
        g_ref[...] = g
        d_ref[...], nm_ref[...], nv_ref[...] = _adam_math(w_ref[...], g, m_ref[...], v_ref[...])

    spec = pl.BlockSpec((t, width), lambda i, c_ref: (i, 0))
    return pl.pallas_call(
        body, name=name,
        grid_spec=pltpu.PrefetchScalarGridSpec(num_scalar_prefetch=1, grid=(rows // t,),
                                               in_specs=[spec, half_spec, half_spec, spec, spec], out_specs=[spec] * 4),
        out_shape=[jax.ShapeDtypeStruct(w.shape, F32)] * 4, compiler_params=_params(("parallel",)),
    )(core, w, mine, theirs, m, v)


def _adamw_small(ws, gs, ms, vs):
    n = len(ws)

    def body(*refs):
        for t in range(n):
            w_ref, g_ref, m_ref, v_ref = (refs[k * n + t] for k in range(4))
            d, nm, nv = _adam_math(w_ref[...], g_ref[...], m_ref[...], v_ref[...])
            refs[4 * n + t][...] = d
            refs[5 * n + t][...] = nm
            refs[6 * n + t][...] = nv

    vmem = pl.BlockSpec(memory_space=pltpu.VMEM)
    res = pl.pallas_call(body, name="adamw_small", in_specs=[vmem] * (4 * n), out_specs=[vmem] * (3 * n),
                         out_shape=[jax.ShapeDtypeStruct(a.shape, F32) for a in ws] * 3)(*ws, *gs, *ms, *vs)
    return res[:n], res[n:2 * n], res[2 * n:]


def kernel(x, w_in, b_gate, g_q_a, w_uq, g_kv_a, w_ukv, w_o_mla, w_o_dil, w_out, ln1_g, ln1_b, w_ff1, w_ff2, ln2_g, ln2_b, loss_target, m_w_in, m_b_gate, m_g_q_a, m_w_uq, m_g_kv_a, m_w_ukv, m_w_o_mla, m_w_o_dil, m_w_out, m_ln1_g, m_ln1_b, m_w_ff1, m_w_ff2, m_ln2_g, m_ln2_b, v_w_in, v_b_gate, v_g_q_a, v_w_uq, v_g_kv_a, v_w_ukv, v_w_o_mla, v_w_o_dil, v_w_out, v_ln1_g, v_ln1_b, v_w_ff1, v_w_ff2, v_ln2_g, v_ln2_b):
    order = ("w_in", "b_gate", "g_q_a", "w_uq", "g_kv_a", "w_ukv", "w_o_mla", "w_o_dil", "w_out", "ln1_g", "ln1_b", "w_ff1", "w_ff2", "ln2_g", "ln2_b")
    w = dict(w_in=w_in, b_gate=b_gate, g_q_a=g_q_a, w_uq=w_uq, g_kv_a=g_kv_a, w_ukv=w_ukv, w_o_mla=w_o_mla, w_o_dil=w_o_dil, w_out=w_out,
             ln1_g=ln1_g, ln1_b=ln1_b, w_ff1=w_ff1, w_ff2=w_ff2, ln2_g=ln2_g, ln2_b=ln2_b)
    m = dict(w_in=m_w_in, b_gate=m_b_gate, g_q_a=m_g_q_a, w_uq=m_w_uq, g_kv_a=m_g_kv_a, w_ukv=m_w_ukv, w_o_mla=m_w_o_mla, w_o_dil=m_w_o_dil,
             w_out=m_w_out, ln1_g=m_ln1_g, ln1_b=m_ln1_b, w_ff1=m_w_ff1, w_ff2=m_w_ff2, ln2_g=m_ln2_g, ln2_b=m_ln2_b)
    v = dict(w_in=v_w_in, b_gate=v_b_gate, g_q_a=v_g_q_a, w_uq=v_w_uq, g_kv_a=v_g_kv_a, w_ukv=v_w_ukv, w_o_mla=v_w_o_mla, w_o_dil=v_w_o_dil,
             w_out=v_w_out, ln1_g=v_ln1_g, ln1_b=v_ln1_b, w_ff1=v_w_ff1, w_ff2=v_w_ff2, ln2_g=v_ln2_g, ln2_b=v_ln2_b)
    chip = 2 * lax.axis_index("x") + lax.axis_index("y")
    south = (lax.axis_index("c") == 0).astype(F32)
    gate_w = D_MODEL // N_CHIPS

    core = lax.axis_index("c").astype(jnp.int32).reshape(1)
    turn = lambda n, a: a.T if n in TRANSPOSED else a
    shards = {n: turn(n, w[n][0]).astype(BF16) for n in BIG}
    b_mine = lax.dynamic_update_slice(jnp.zeros((2, D_MODEL), F32), b_gate[0] * south, (0, chip * gate_w))
    b_full = _sum_all_devices(b_mine.reshape(-1, LANES), "gather_b_gate").reshape(2, D_MODEL)
    early_shards, late_shards = [shards[n] for n in EARLY], [shards[n] for n in LATE]
    gathered = lambda group: [jax.ShapeDtypeStruct((N_CHIPS,) + s.shape, BF16) for s in group]
    e_sems, e_srcs, e_lands, e_token = _split_start("gather", early_shards, gathered(early_shards), "gather_early_start", after=b_full)
    g_sems, g_srcs, g_lands, g_token = _split_start("gather", late_shards, gathered(late_shards), "gather_late_start", after=e_token)
    tables = _attention_tables(x.shape[1], g_token[0, 0].astype(jnp.int32))
    behind = sum(t[0, 0:1, 0:1] for pair in tables for group in pair for t in group[:2])
    e_srcs, e_lands = _split_wait("gather", e_sems, e_srcs, e_lands, behind, "gather_early_wait")
    e_forward = _split_start("forward", e_srcs, None, "gather_early_forward_start", lands=e_lands)
    first = dict(zip(EARLY, _split_wait("forward", *e_forward[:3], e_forward[-1], "gather_early_forward_wait")[1]))

    sent = {}

    def late_arrived(after):
        srcs, lands = _split_wait("gather", g_sems, g_srcs, g_lands, after, "gather_late_wait")
        sent["forward"] = _split_start("forward", srcs, None, "gather_late_forward_start", lands=lands)
        return sent["forward"][-1]

    def late_weights(after):
        return dict(zip(LATE, _split_wait("forward", *sent["forward"][:3], after, "gather_late_forward_wait")[1]))

    exchange_shapes = lambda parts: [jax.ShapeDtypeStruct((3,) + p.shape[1:], BF16) for p in parts]

    def early_grads(grads_late):
        gs = [grads_late[n] for n in LATE]
        shapes = [jax.ShapeDtypeStruct((N_CHIPS,) + _half_shape(g.shape[1:]), F32) for g in gs]
        sent["pair"] = _split_start("pair", gs, shapes, "pair_split_late_start")
        return sent["pair"][-1]

    def early_grads_go(after):
        gs, theirs = _split_wait("pair", *sent["pair"][:3], after, "pair_split_late_wait")
        parts = [_pair_add(g, th, core, "pair_add_" + n) for g, th, n in zip(gs, theirs, LATE)]
        sent["late"] = _split_start("scatter", parts, exchange_shapes(parts), "exchange_late_start")
        return sent["late"][-1]

    def last_grads(grads_early):
        parts = _chip_partials(grads_early, EARLY, core, "early")
        sent["early"] = _split_start("scatter", parts, exchange_shapes(parts), "exchange_early_start")
        return sent["early"][-1]

    loss_part, grad_x, grads = _local_step(x, loss_target, first, b_full, g_q_a, g_kv_a, ln1_g, ln1_b, ln2_g, ln2_b, token=g_token,
                                           late_arrived=late_arrived, late_weights=late_weights, early_grads=early_grads, early_grads_go=early_grads_go,
                                           last_grads=last_grads, tables=tables)

    g_out, delta, new_m, new_v = {}, {}, {}, {}
    chip1 = chip.astype(jnp.int32).reshape(1)

    def sum_and_send(names, parts, others, tag):
        totals = [_chip_sum(p, o, chip1, "chip_sum_" + n) for n, p, o in zip(names, parts, others)]
        return _split_start("join", totals, [jax.ShapeDtypeStruct(t.shape, F32) for t in totals], "pair_join_" + tag + "_start")

    def adam(names, joined, after, tag):
        totals, halves = _split_wait("join", *joined[:3], after, "pair_join_" + tag + "_wait")
        for n, mine, theirs in zip(names, totals, halves):
            res = _adamw_big(turn(n, w[n][0]), mine, theirs, turn(n, m[n][0]), turn(n, v[n][0]), core, "adamw_" + n,
                             side_by_side=mine.shape[0] == shards[n].shape[0])
            g_out[n], delta[n], new_m[n], new_v[n] = (turn(n, r) for r in res)

    late_joined = sum_and_send(LATE, *_split_wait("scatter", *sent["late"][:3], grad_x, "exchange_late_wait"), "late")
    early_joined = sum_and_send(EARLY, *_split_wait("scatter", *sent["early"][:3], late_joined[-1], "exchange_early_wait"), "early")
    small_names = [name for name, _ in SMALL]
    sums = _sum_small([grads[name] for name in small_names] + [loss_part])
    loss = sums[-1][0, 0]
    g_small = dict(zip(small_names, sums))
    g_small["b_gate"] = lax.dynamic_slice(g_small["b_gate"], (0, chip * gate_w), (2, gate_w))
    flat = lambda a: a.reshape(-1, a.shape[-1])
    res = _adamw_small(*[[flat(d[name]) for name in small_names] for d in (w, g_small, m, v)])
    g_out.update(g_small)
    for d, r in zip((delta, new_m, new_v), res):
        d.update(zip(small_names, r))
    adam(LATE, late_joined, res[0][0], "late")
    adam(EARLY, early_joined, delta[LATE[-1]], "early")

    lead = lambda d: [d[name].reshape(w[name].shape) for name in order]
    return (loss, grad_x, *lead(g_out), *lead(delta), *lead(new_m), *lead(new_v))
```

```python
import functools
import math

import jax
import jax.numpy as jnp
from jax import lax
from jax.experimental import pallas as pl
from jax.experimental.pallas import tpu as pltpu

F32 = jnp.float32
BF16 = jnp.bfloat16
MESH = pl.DeviceIdType.MESH

D_MODEL = 1024
N_HEADS = 8
LANES = 128
NOPE, ROPE, V_DIM = 64, 32, 64
MLA_QK = NOPE + ROPE
Q_LORA, KV_LORA = 384, 256
DIL_DIM = 64
DIL_PATTERNS = ((128, 1), (512, 4), (2048, 16))
D_FF = 4096
N_CHIPS = 4
N_DEV = 8
IN_WIDTH = 4256
LN_EPS, RMS_EPS = 1e-5, 1e-6
NEG = -1e30
LOG2E, LN2 = 1.4426950408889634, 0.6931471805599453
ALPHA = 2.0 ** 0.25
ROPE_THETA = 10000.0
LR, B1, B2, ADAM_EPS, WD, ADAM_STEP = 0.001, 0.9, 0.999, 1e-8, 0.01, 10

P_LORA, P_KR, P_GATE, P_HALF = 0, 640, 1024, 3072
DIL_GROUP = 4 * LANES
P_DIL = N_HEADS // 2 * DIL_GROUP
LORA_W = Q_LORA + KV_LORA
KR_LANE = NOPE

ATT_T = 512
ROW_T = 512
VMEM_LIMIT = 56 * 1024 * 1024

NN = (((1,), (0,)), ((), ()))
NT = (((1,), (1,)), ((), ()))
TN = (((0,), (0,)), ((), ()))


def _params(sem=None, **kw):
    return pltpu.CompilerParams(dimension_semantics=sem, vmem_limit_bytes=VMEM_LIMIT, **kw)


def _matmul(a, b, *, mode, name, tm, tn, tk, out_dtypes=(F32,), extras=(), epilogue=None, b_shards=False, out_shards=False, after=None):
    pieces = list(a) if isinstance(a, (list, tuple)) else [a]
    n_pc = len(pieces)
    a_shape = (pieces[0].shape[0], sum(p.shape[1] for p in pieces))
    if b_shards:
        n_sh, rows_b, cols_b = b.shape
        b_shape = (rows_b, n_sh * cols_b)
    else:
        b_shape = b.shape
    if mode == "nn":
        (m, k), (k2, n) = a_shape, b_shape
    elif mode == "nt":
        (m, k), (n, k2) = a_shape, b_shape
    else:
        (k, m), (k2, n) = a_shape, b_shape
    assert k == k2, (a_shape, b.shape, mode)
    tm, tn, tk = min(tm, m), min(tn, n), min(tk, k)
    assert m % tm == 0 and n % tn == 0 and k % tk == 0, (name, m, n, k, tm, tn, tk)
    nk = k // tk
    n_ex, n_out = len(extras), len(out_dtypes)
    n_in = n_pc + 1 + n_ex + (after is not None)
    dims = {"nn": NN, "nt": NT, "tn": TN}[mode]
    col_tile = tm if mode == "tn" else tk
    blocks = [p.shape[1] // col_tile for p in pieces]
    firsts = [sum(blocks[:p]) for p in range(n_pc)]
    assert all(p.shape[1] % col_tile == 0 for p in pieces), (name, col_tile)

    def body(*refs):
        a_refs, b_ref = refs[:n_pc], refs[n_pc]
        ex_refs = refs[n_pc + 1:n_pc + 1 + n_ex]
        out_refs = refs[n_in:n_in + n_out]

        def finish(acc):
            outs = epilogue(acc, *[r[...] for r in ex_refs]) if epilogue is not None else (acc,)
            for r, o in zip(out_refs, outs):
                r[...] = o.astype(r.dtype)

        kk = pl.program_id(2)

        def step(a_ref):
            part = lax.dot_general(a_ref[...].astype(BF16), b_ref[...].astype(BF16), dims, preferred_element_type=F32)
            if nk == 1:
                finish(part)
                return
            acc_ref = refs[-1]

            @pl.when(kk == 0)
            def _():
                acc_ref[...] = part

            @pl.when(kk > 0)
            def _():
                acc_ref[...] += part

            @pl.when(kk == nk - 1)
            def _():
                finish(acc_ref[...])

        if n_pc == 1:
            step(a_refs[0])
        else:
            at = pl.program_id(0) if mode == "tn" else kk
            for p in range(n_pc):
                pl.when(jnp.logical_and(at >= firsts[p], at < firsts[p] + blocks[p]))(functools.partial(step, a_refs[p]))

    def a_spec_of(p):
        if n_pc == 1:
            return pl.BlockSpec((tk, tm), lambda i, j, kk: (kk, i)) if mode == "tn" else pl.BlockSpec((tm, tk), lambda i, j, kk: (i, kk))
        col = lambda at: jnp.clip(at - firsts[p], 0, blocks[p] - 1)
        mine = lambda at: jnp.logical_and(at >= firsts[p], at < firsts[p] + blocks[p])
        if mode == "tn":
            return pl.BlockSpec((tk, tm), lambda i, j, kk: (jnp.where(mine(i), kk, 0), col(i)))
        return pl.BlockSpec((tm, tk), lambda i, j, kk: (i, col(kk)))

    b_spec = {"nn": pl.BlockSpec((tk, tn), lambda i, j, kk: (kk, j)),
              "nt": pl.BlockSpec((tn, tk), lambda i, j, kk: (j, kk)),
              "tn": pl.BlockSpec((tk, tn), lambda i, j, kk: (kk, j))}[mode]
    tile = pl.BlockSpec((tm, tn), lambda i, j, kk: (i, j))
    out_spec, out_dims = tile, (m, n)
    if b_shards and mode == "nn":
        per = cols_b // tn
        b_spec = pl.BlockSpec((None, tk, tn), lambda i, j, kk: (j // per, kk, j % per))
    elif b_shards:
        assert mode == "nt"
        per = cols_b // tk
        b_spec = pl.BlockSpec((None, tn, tk), lambda i, j, kk: (kk // per, j, kk % per))
    if out_shards:
        assert not extras and epilogue is None
        per_out = n // N_CHIPS // tn
        out_spec = pl.BlockSpec((None, tm, tn), lambda i, j, kk: (j // per_out, i, j % per_out))
        out_dims = (N_CHIPS, m, n // N_CHIPS)
    outs = pl.pallas_call(
        body, name=name,
        grid=(m // tm, n // tn, nk),
        in_specs=[a_spec_of(p) for p in range(n_pc)] + [b_spec] + [tile] * n_ex + [pl.BlockSpec(memory_space=pl.ANY)] * (after is not None),
        out_specs=[out_spec] * n_out,
        out_shape=[jax.ShapeDtypeStruct(out_dims, dt) for dt in out_dtypes],
        scratch_shapes=[pltpu.VMEM((tm, tn), F32)] if nk > 1 else [],
        compiler_params=_params(("parallel", "parallel", "arbitrary")),
    )(*pieces, b, *extras, *([after] if after is not None else []))
    return outs[0] if n_out == 1 else outs


def _rowwise(fn, *, name, rows, seq, ins, outs, sums=()):
    tm = min(ROW_T, seq)
    n_pos = seq // tm
    n_in, n_out, n_sum = len(ins), len(outs), len(sums)

    def body(*refs):
        vals = fn(*[r[...] for r in refs[:n_in]])
        for r, v in zip(refs[n_in:n_in + n_out], vals[:n_out]):
            r[...] = v.astype(r.dtype)
        first = pl.program_id(0) == 0
        for r, v in zip(refs[n_in + n_out:], vals[n_out:]):
            @pl.when(first)
            def _(r=r, v=v):
                r[...] = v

            @pl.when(jnp.logical_not(first))
            def _(r=r, v=v):
                r[...] += v

    def spec(arr, width, col, kind):
        if kind == "row":
            return pl.BlockSpec((tm, width), lambda i, col=col: (i, col))
        if kind == "pos":
            return pl.BlockSpec((tm, width), lambda i, col=col: (i % n_pos, col))
        return pl.BlockSpec(arr.shape, lambda i: (0,) * arr.ndim)

    res = pl.pallas_call(
        body, name=name,
        grid=(rows // tm,),
        in_specs=[spec(*t) for t in ins],
        out_specs=[pl.BlockSpec((tm, w), lambda i: (i, 0)) for w, _ in outs]
        + [pl.BlockSpec((1, w), lambda i: (0, 0)) for w in sums],
        out_shape=[jax.ShapeDtypeStruct((rows, w), dt) for w, dt in outs]
        + [jax.ShapeDtypeStruct((1, w), F32) for w in sums],
        compiler_params=_params(("arbitrary",)),
    )(*[t[0] for t in ins])
    return res


def _colsum(v):
    return jnp.sum(v, axis=0, keepdims=True)


def _rope_fwd(t, c, s_up, s_dn):
    return t * c + pltpu.roll(t, LANES - 16, 1) * s_up + pltpu.roll(t, 16, 1) * s_dn


def _rope_bwd(d, c, s_up, s_dn):
    return d * c + pltpu.roll(d * s_up, 16, 1) + pltpu.roll(d * s_dn, LANES - 16, 1)


def _rope_tables(seq):
    half = ROPE // 2
    inv = jnp.power(ROPE_THETA, -jnp.arange(half, dtype=F32) / half)
    ang = jnp.arange(seq, dtype=F32)[:, None] * inv[None, :]
    cos, sin = jnp.cos(ang), jnp.sin(ang)
    zeros = jnp.zeros((seq, half), F32)
    lo, hi = jnp.ones((seq, KR_LANE), F32), jnp.ones((seq, LANES - KR_LANE - ROPE), F32)
    c = jnp.concatenate([lo, cos, cos, hi], axis=1)
    c_rope_only = jnp.concatenate([0 * lo, cos, cos, 0 * hi], axis=1)
    s_up = jnp.concatenate([0 * lo, -sin, zeros, 0 * hi], axis=1)
    s_dn = jnp.concatenate([0 * lo, zeros, sin, 0 * hi], axis=1)
    return c, s_up, s_dn, c_rope_only


def _rms(x, g):
    r = lax.rsqrt(jnp.mean(x * x, axis=1, keepdims=True) + RMS_EPS)
    return x * r * g


def _rms_bwd(x, g, dy):
    r = lax.rsqrt(jnp.mean(x * x, axis=1, keepdims=True) + RMS_EPS)
    xh = x * r
    dxh = dy * g
    dx = r * (dxh - xh * jnp.mean(dxh * xh, axis=1, keepdims=True))
    return dx, _colsum(dy * xh)


def _ln_stats(x):
    mu = jnp.mean(x, axis=1, keepdims=True)
    xc = x - mu
    r = lax.rsqrt(jnp.mean(xc * xc, axis=1, keepdims=True) + LN_EPS)
    return xc * r, r


def _ln_bwd(xh, r, g, dy):
    dxh = dy * g
    dx = r * (dxh - jnp.mean(dxh, axis=1, keepdims=True) - xh * jnp.mean(dxh * xh, axis=1, keepdims=True))
    return dx, _colsum(dy * xh), _colsum(dy)


def _table_specs(tables, sub):
    whole = lambda a: pl.BlockSpec(a.shape, lambda b, g: (0,) * a.ndim)
    if len(tables) == 1:
        return [whole(tables[0])]
    return [whole(tables[0]), whole(tables[1]), pl.BlockSpec((sub, 1, LANES), lambda b, g: (g, 0, 0))]


def _biased(s, table_refs, delta, head):
    if delta < table_refs[0].shape[0]:
        s = s + table_refs[0][delta]
    if len(table_refs) == 3:
        s = s - table_refs[2][head, 0:1, 0:1] * table_refs[1][delta]
    return s


def _lane_masks(sub):
    lane = lax.broadcasted_iota(jnp.int32, (1, LANES), 1)
    return [(lane // (LANES // sub) == a).astype(F32) for a in range(sub)]


def _attn_fwd(q, qb0, k, kb0, v, vb0, tables, scale, *, name, batch, seq, sub=1, stride=1, after=None):
    t = ATT_T
    nq = seq // t
    rows = batch * seq
    n_tab = len(tables)

    def body(q_ref, k_ref, v_ref, *rest):
        table_refs = rest[:n_tab]
        o_ref, lse_ref, kb, vtb = rest[n_tab + (after is not None):][:4]
        qbs = rest[n_tab + (after is not None) + 4:]
        masks = _lane_masks(sub)
        for a in range(sub):
            qbs[a][...] = (q_ref[...].astype(F32) * masks[a]).astype(BF16) if sub > 1 else q_ref[...].astype(BF16)
        kb[...] = k_ref[...].astype(BF16)
        vtb[...] = v_ref[...].astype(F32).T.astype(BF16)
        for i in range(nq):
            out_t = None
            for a in range(sub):
                qt = qbs[a][i * t:(i + 1) * t, :]
                logits = [_biased(lax.dot_general(kb[j * t:(j + 1) * t, :], qt, NT, preferred_element_type=F32) * (scale * LOG2E), table_refs, i - j, a)
                          for j in range(i + 1)]
                m = jnp.max(functools.reduce(jnp.maximum, logits), axis=0, keepdims=True)
                ps = [jnp.exp2(s - m) for s in logits]
                l = jnp.sum(functools.reduce(jnp.add, ps), axis=0, keepdims=True)
                acc = functools.reduce(jnp.add, [lax.dot_general(vtb[:, j * t:(j + 1) * t], p.astype(BF16), NN, preferred_element_type=F32)
                                                 for j, p in enumerate(ps)])
                part = acc / l if sub == 1 else (acc / l) * masks[a].T
                out_t = part if out_t is None else out_t + part
                lse_ref[i * t:(i + 1) * t, a * LANES:(a + 1) * LANES] = jnp.broadcast_to((m + jnp.log2(l)) * LN2, (LANES, t)).T
            o_ref[i * t:(i + 1) * t, :] = out_t.T

    slab = lambda b0, step: pl.BlockSpec((seq, LANES), lambda b, g: (b, b0 + step * g))
    groups = N_HEADS // sub
    return pl.pallas_call(
        body, name=name,
        grid=(batch, groups),
        in_specs=[slab(qb0, stride), slab(kb0, stride), slab(vb0, stride)] + _table_specs(tables, sub)
        + [pl.BlockSpec(memory_space=pl.ANY)] * (after is not None),
        out_specs=[slab(0, 1), pl.BlockSpec((seq, sub * LANES), lambda b, g: (b, g))],
        out_shape=[jax.ShapeDtypeStruct((rows, groups * LANES), F32), jax.ShapeDtypeStruct((rows, N_HEADS * LANES), F32)],
        scratch_shapes=[pltpu.VMEM((seq, LANES), BF16), pltpu.VMEM((LANES, seq), BF16)] + [pltpu.VMEM((seq, LANES), BF16)] * sub,
        compiler_params=_params(("arbitrary", "arbitrary")),
    )(q, k, v, *tables, *([after] if after is not None else []))


def _attn_bwd(q, qb0, k, kb0, v, vb0, o, do, lse, tables, scale, *, name, batch, seq, out_dtype, sub=1, stride=1, after=None):
    t = ATT_T
    nq = seq // t
    rows = batch * seq
    n_tab = len(tables)
    groups = N_HEADS // sub
    n_out = 3 if sub == 1 else 1

    def body(q_ref, k_ref, v_ref, o_ref, do_ref, lse_ref, *rest):
        table_refs = rest[:n_tab]
        rest = rest[n_tab + (after is not None):]
        out_refs, (kb, vb, dka, dva), per_head = rest[:n_out], rest[n_out:n_out + 4], rest[n_out + 4:]
        qbs, dobs, qtbs, dotbs = (per_head[g * sub:(g + 1) * sub] for g in range(4))
        masks = _lane_masks(sub)
        kb[...] = k_ref[...].astype(BF16)
        vb[...] = v_ref[...].astype(BF16)
        for a in range(sub):
            qa = q_ref[...].astype(F32) * masks[a] if sub > 1 else q_ref[...].astype(F32)
            doa = do_ref[...] * masks[a] if sub > 1 else do_ref[...]
            qbs[a][...] = qa.astype(BF16)
            dobs[a][...] = doa.astype(BF16)
            qtbs[a][...] = qa.T.astype(BF16)
            dotbs[a][...] = doa.T.astype(BF16)
        first = [True] * nq
        for i in range(nq):
            at = slice(i * t, (i + 1) * t)
            dq_all = None
            for a in range(sub):
                qt, dot = qbs[a][at, :], dobs[a][at, :]
                lse_t = lse_ref[at, a * LANES:a * LANES + 1] * LOG2E
                od = o_ref[at, :] * do_ref[at, :]
                delta = jnp.sum(od * masks[a] if sub > 1 else od, axis=1, keepdims=True)
                dq = None
                for j in range(i + 1):
                    kat = slice(j * t, (j + 1) * t)
                    kt, vt = kb[kat, :], vb[kat, :]
                    p = jnp.exp2(_biased(lax.dot_general(qt, kt, NT, preferred_element_type=F32) * (scale * LOG2E), table_refs, i - j, a) - lse_t)
                    dp = lax.dot_general(dot, vt, NT, preferred_element_type=F32)
                    ds = (p * (dp - delta) * scale).astype(BF16)
                    dk_part = lax.dot_general(qtbs[a][:, at], ds, NN, preferred_element_type=F32)
                    dv_part = lax.dot_general(dotbs[a][:, at], p.astype(BF16), NN, preferred_element_type=F32)
                    if first[j]:
                        dka[:, kat] = dk_part
                        dva[:, kat] = dv_part
                        first[j] = False
                    else:
                        dka[:, kat] += dk_part
                        dva[:, kat] += dv_part
                    dq_part = lax.dot_general(ds, kt, NN, preferred_element_type=F32)
                    dq = dq_part if dq is None else dq + dq_part
                dq = dq * masks[a] if sub > 1 else dq
                dq_all = dq if dq_all is None else dq_all + dq
            out_refs[0][at, 0:LANES] = dq_all.astype(out_refs[0].dtype)
        if sub == 1:
            out_refs[1][...] = dka[...].T.astype(out_refs[1].dtype)
            out_refs[2][...] = dva[...].T.astype(out_refs[2].dtype)
        else:
            out_refs[0][:, LANES:2 * LANES] = dka[...].T.astype(out_refs[0].dtype)
            out_refs[0][:, 2 * LANES:3 * LANES] = dva[...].T.astype(out_refs[0].dtype)
            out_refs[0][:, 3 * LANES:] = jnp.zeros((seq, LANES), out_refs[0].dtype)

    slab = lambda b0, step: pl.BlockSpec((seq, LANES), lambda b, g: (b, b0 + step * g))
    if sub == 1:
        out_specs = [slab(0, 1)] * 3
        out_shape = [jax.ShapeDtypeStruct((rows, N_HEADS * LANES), out_dtype)] * 3
    else:
        out_specs = [pl.BlockSpec((seq, 4 * LANES), lambda b, g: (b, g))]
        out_shape = [jax.ShapeDtypeStruct((rows, groups * 4 * LANES), out_dtype)]
    res = pl.pallas_call(
        body, name=name,
        grid=(batch, groups),
        in_specs=[slab(qb0, stride), slab(kb0, stride), slab(vb0, stride), slab(0, 1), slab(0, 1),
                  pl.BlockSpec((seq, sub * LANES), lambda b, g: (b, g))] + _table_specs(tables, sub)
        + [pl.BlockSpec(memory_space=pl.ANY)] * (after is not None),
        out_specs=out_specs, out_shape=out_shape,
        scratch_shapes=[pltpu.VMEM((seq, LANES), BF16)] * 2 + [pltpu.VMEM((LANES, seq), F32)] * 2
        + [pltpu.VMEM((seq, LANES), BF16)] * (2 * sub) + [pltpu.VMEM((LANES, seq), BF16)] * (2 * sub),
        compiler_params=_params(("arbitrary", "arbitrary")),
    )(q, k, v, o, do, lse, *tables, *([after] if after is not None else []))
    return res if sub == 1 else res[0]


def _attention_tables(seq):
    n = seq // ATT_T
    pos = jnp.arange(ATT_T, dtype=jnp.int32)
    tile = jnp.arange(n, dtype=jnp.int32)[:, None, None] * ATT_T

    def of(dist):
        count = jnp.zeros(dist.shape, F32)
        for window, dilation in DIL_PATTERNS:
            count += ((dist >= 0) & (dist <= window) & (dist % dilation == 0)).astype(F32)
        held = jnp.where(count > 0, jnp.log2(jnp.maximum(count, 1.0)), NEG).astype(F32)
        return jnp.where(dist[:1] >= 0, 0.0, NEG).astype(F32), held, dist.astype(F32) * LOG2E

    slopes = jnp.asarray([2.0 ** (-8.0 * (i + 1) / N_HEADS) for i in range(N_HEADS)], F32)
    slopes = jnp.broadcast_to(slopes[:, None, None], (N_HEADS, 1, LANES))
    (causal, held, far), (causal_t, held_t, far_t) = of(tile + pos[None, :, None] - pos[None, None, :]), of(tile + pos[None, None, :] - pos[None, :, None])
    return ((causal,), (causal_t,)), ((held, far, slopes), (held_t, far_t, slopes))


def _pad_heads(w, width):
    kdim, n = w.shape[0], w.shape[1] // width
    return jnp.pad(w.reshape(kdim, n, width), ((0, 0), (0, 0), (0, LANES - width))).reshape(kdim, n * LANES)


def _unpad_heads(w, width):
    kdim, n = w.shape[0], w.shape[1] // LANES
    return w.reshape(kdim, n, LANES)[:, :, :width].reshape(kdim, n * width)


def _pad_head_rows(w, width):
    n, kdim = w.shape[0] // width, w.shape[1]
    return jnp.pad(w.reshape(n, width, kdim), ((0, 0), (0, LANES - width), (0, 0))).reshape(n * LANES, kdim)


def _unpad_head_rows(w, width):
    n, kdim = w.shape[0] // LANES, w.shape[1]
    return w.reshape(n, LANES, kdim)[:, :width].reshape(n * width, kdim)


def _pad_w_in_t(wt):
    n_qkv, pair = 3 * N_HEADS * DIL_DIM, 2 * DIL_DIM
    zeros = lambda n: jnp.zeros((n, wt.shape[1]), wt.dtype)
    main = jnp.concatenate([wt[:LORA_W], zeros(KR_LANE), wt[LORA_W:LORA_W + ROPE], zeros(P_GATE - P_KR - KR_LANE - ROPE),
                            wt[LORA_W + ROPE + n_qkv:]], axis=0)
    qkv = wt[LORA_W + ROPE:LORA_W + ROPE + n_qkv].reshape(3, N_HEADS // 2, pair, wt.shape[1]).transpose(1, 0, 2, 3)
    dil = jnp.pad(qkv, ((0, 0), (0, 1), (0, 0), (0, 0))).reshape(P_DIL, wt.shape[1])
    return main, dil


def _unpad_w_in_t(gt):
    qkv = gt[P_HALF:].reshape(N_HEADS // 2, 4, 2 * DIL_DIM, gt.shape[1])[:, :3].transpose(1, 0, 2, 3).reshape(3 * N_HEADS * DIL_DIM, gt.shape[1])
    return jnp.concatenate([gt[P_LORA:P_KR], gt[P_KR + KR_LANE:P_KR + KR_LANE + ROPE], qkv, gt[P_GATE:P_HALF]], axis=0)


def _split_ukv(w):
    w3 = w.reshape(w.shape[0], N_HEADS, NOPE + V_DIM)
    return (_pad_heads(w3[:, :, :NOPE].reshape(w.shape[0], -1), NOPE),
            _pad_heads(w3[:, :, NOPE:].reshape(w.shape[0], -1), V_DIM))


def _merge_ukv(g_k, g_v):
    kdim = g_k.shape[0]
    k3 = _unpad_heads(g_k, NOPE).reshape(kdim, N_HEADS, NOPE)
    v3 = _unpad_heads(g_v, V_DIM).reshape(kdim, N_HEADS, V_DIM)
    return jnp.concatenate([k3, v3], axis=2).reshape(kdim, N_HEADS * (NOPE + V_DIM))


def _pad_rows(w, width):
    return _pad_heads(w.T, width).T


def _unpad_rows(g, width):
    return _unpad_heads(g.T, width).T


def _join_cols(w):
    return w.transpose(1, 0, 2).reshape(w.shape[1], N_CHIPS * w.shape[2])


def _split_cols(g):
    return g.reshape(g.shape[0], N_CHIPS, g.shape[1] // N_CHIPS).transpose(1, 0, 2)


def _local_step(x3, target3, wg, b_gate, g_q_a, g_kv_a, ln1_g, ln1_b, ln2_g, ln2_b, token=None, late_arrived=None, late_weights=None,
                early_grads=None, early_grads_go=None, last_grads=None, tables=None):
    w_main_t, w_dil_t = _pad_w_in_t(wg["w_in"].reshape(IN_WIDTH, D_MODEL))
    w_uq_pt = _pad_head_rows(wg["w_uq"].reshape(N_HEADS * MLA_QK, Q_LORA), MLA_QK)
    w_ukv = _join_cols(wg["w_ukv"])
    batch, seq, _ = x3.shape
    rows = batch * seq
    x = x3.reshape(rows, D_MODEL)
    target = target3.reshape(rows, D_MODEL)
    row = functools.partial(_rowwise, rows=rows, seq=seq)
    mm = _matmul

    w_uk_p, w_uv_p = _split_ukv(w_ukv)
    b0, b1 = b_gate[0:1], b_gate[1:2]
    rope_c, rope_up, rope_dn, rope_c_only = _rope_tables(seq)
    (mla_bwd_tables, mla_fwd_tables), (dil_bwd_tables, dil_fwd_tables) = _attention_tables(seq) if tables is None else tables
    scale_mla, scale_dil = MLA_QK ** -0.5, DIL_DIM ** -0.5
    lora0, kr0, gate0 = P_LORA // LORA_W, P_KR // LANES, P_GATE // D_MODEL

    proj = mm(x, w_main_t, mode="nt", name="proj", tm=1024, tn=1536, tk=1024, after=token)
    proj_d = mm(x, w_dil_t, mode="nt", name="proj_dil", tm=1024, tn=1024, tk=1024, out_dtypes=(BF16,))

    def prep(lora, gq, gkv):
        return _rms(lora[:, :Q_LORA], gq), _rms(lora[:, Q_LORA:], gkv)

    qn, kvn = row(prep, name="mla_rms", ins=[(proj, LORA_W, lora0, "row"), (g_q_a, 0, 0, "full"), (g_kv_a, 0, 0, "full")],
                  outs=[(Q_LORA, BF16), (KV_LORA, BF16)])
    q_lin = mm(qn, w_uq_pt, mode="nt", name="q_up", tm=1024, tn=1024, tk=Q_LORA)
    k_lin = mm(kvn, w_uk_p, mode="nn", name="k_up", tm=1024, tn=1024, tk=KV_LORA)
    v_a = mm(kvn, w_uv_p, mode="nn", name="v_up", tm=1024, tn=1024, tk=KV_LORA, out_dtypes=(BF16,))

    def rope_qk(ql, kl, kr, c, up, dn):
        k_rot = _rope_fwd(kr, c, up, dn)
        qs = [_rope_fwd(ql[:, h * LANES:(h + 1) * LANES], c, up, dn) for h in range(N_HEADS)]
        ks = [kl[:, h * LANES:(h + 1) * LANES] + k_rot for h in range(N_HEADS)]
        return jnp.concatenate(qs, axis=1), jnp.concatenate(ks, axis=1)

    pos = lambda tab: (tab, LANES, 0, "pos")
    q_a, k_a = row(rope_qk, name="rope_qk",
                   ins=[(q_lin, D_MODEL, 0, "row"), (k_lin, D_MODEL, 0, "row"), (proj, LANES, kr0, "row"), pos(rope_c), pos(rope_up), pos(rope_dn)],
                   outs=[(N_HEADS * LANES, BF16), (N_HEADS * LANES, BF16)])
    o_a, lse_a = _attn_fwd(q_a, 0, k_a, 0, v_a, 0, mla_fwd_tables, scale_mla, name="mla_fwd", batch=batch, seq=seq)
    arrived = None if late_arrived is None else late_arrived(o_a)
    o_b, lse_b = _attn_fwd(proj_d, 0, proj_d, 1, proj_d, 2, dil_fwd_tables, scale_dil, name="dil_fwd", batch=batch, seq=seq, sub=2, stride=4, after=arrived)
    late = wg if late_weights is None else late_weights(o_b)
    w_oa_p = _pad_rows(_join_cols(late["w_o_mla"]), V_DIM)
    w_ob = _join_cols(late["w_o_dil"])
    w_out, w_ff1, w_ff2 = late["w_out"].reshape(D_MODEL, D_MODEL), late["w_ff1"], late["w_ff2"].reshape(D_FF, D_MODEL)
    y_a = mm(o_a, w_oa_p, mode="nn", name="o_mla", tm=1024, tn=1024, tk=1024, out_dtypes=(BF16,))
    y_b = mm(o_b, w_ob, mode="nn", name="o_dil", tm=1024, tn=1024, tk=1024, out_dtypes=(BF16,))

    def gate(t0, t1, c0, c1, ya, yb):
        return (jax.nn.sigmoid(t0 + c0) * ya + jax.nn.sigmoid(t1 + c1) * yb,)

    gate_ins = [(proj, D_MODEL, gate0, "row"), (proj, D_MODEL, gate0 + 1, "row"), (b0, 0, 0, "full"), (b1, 0, 0, "full")]
    (u,) = row(gate, name="gate", ins=gate_ins + [(y_a, D_MODEL, 0, "row"), (y_b, D_MODEL, 0, "row")], outs=[(D_MODEL, BF16)])
    mixed = mm(u, w_out, mode="nn", name="mix", tm=1024, tn=1024, tk=1024)

    def ln1(xv, mv, g, b):
        r1 = ALPHA * xv + mv
        xh, _ = _ln_stats(r1)
        return r1, xh * g + b

    r1, h = row(ln1, name="ln1", ins=[(x, D_MODEL, 0, "row"), (mixed, D_MODEL, 0, "row"), (ln1_g, 0, 0, "full"), (ln1_b, 0, 0, "full")],
                outs=[(D_MODEL, F32), (D_MODEL, F32)])

    def relu2(acc):
        r = jnp.maximum(acc, 0.0)
        return acc, r * r

    a_ff, z = mm(h, w_ff1, mode="nn", name="ff1", tm=1024, tn=1024, tk=1024, out_dtypes=(BF16, BF16), epilogue=relu2, b_shards=True)
    f = mm(z, w_ff2, mode="nn", name="ff2", tm=1024, tn=1024, tk=2048)

    def ln2_loss(hv, fv, tv, g, b):
        xh, r = _ln_stats(ALPHA * hv + fv)
        err = xh * g + b - tv
        dy = err * (1.0 / D_MODEL)
        dr2, dg, db = _ln_bwd(xh, r, g, dy)
        loss = jnp.sum(_colsum(err * err), axis=1, keepdims=True) * (0.5 / D_MODEL)
        return dr2, jnp.broadcast_to(loss, (1, LANES)), dg, db

    dr2, loss_l, d_ln2_g, d_ln2_b = row(
        ln2_loss, name="ln2_loss",
        ins=[(h, D_MODEL, 0, "row"), (f, D_MODEL, 0, "row"), (target, D_MODEL, 0, "row"), (ln2_g, 0, 0, "full"), (ln2_b, 0, 0, "full")],
        outs=[(D_MODEL, F32)], sums=[LANES, D_MODEL, D_MODEL])

    d_w_ff2 = mm(z, dr2, mode="tn", name="d_w_ff2", tm=1024, tn=1024, tk=2048)
    da = mm(dr2, w_ff2, mode="nt", name="d_ff_act", tm=1024, tn=1024, tk=1024, out_dtypes=(BF16,), extras=(a_ff,),
            epilogue=lambda acc, av: (acc * (2.0 * jnp.maximum(av.astype(F32), 0.0)),))
    d_w_ff1 = mm(h, da, mode="tn", name="d_w_ff1", tm=1024, tn=1024, tk=2048, out_shards=True)
    dh = mm(da, w_ff1, mode="nt", name="d_h", tm=1024, tn=1024, tk=1024, extras=(dr2,), epilogue=lambda acc, rv: (acc + ALPHA * rv,), b_shards=True)

    def ln1_bwd(dhv, r1v, g):
        xh, r = _ln_stats(r1v)
        return _ln_bwd(xh, r, g, dhv)

    dr1, d_ln1_g, d_ln1_b = row(ln1_bwd, name="ln1_bwd", ins=[(dh, D_MODEL, 0, "row"), (r1, D_MODEL, 0, "row"), (ln1_g, 0, 0, "full")],
                                outs=[(D_MODEL, F32)], sums=[D_MODEL, D_MODEL])
    d_w_out = mm(u, dr1, mode="tn", name="d_w_out", tm=1024, tn=1024, tk=1024)
    du = mm(dr1, w_out, mode="nt", name="d_u", tm=1024, tn=1024, tk=1024, out_dtypes=(BF16,))

    def gate_bwd(t0, t1, c0, c1, ya, yb, duv):
        s0, s1 = jax.nn.sigmoid(t0 + c0), jax.nn.sigmoid(t1 + c1)
        dt0 = duv * ya * s0 * (1.0 - s0)
        dt1 = duv * yb * s1 * (1.0 - s1)
        return duv * s0, duv * s1, jnp.concatenate([dt0, dt1], axis=1), jnp.concatenate([_colsum(dt0), _colsum(dt1)], axis=1)

    dy_a, dy_b, d_gates, d_b_gate = row(
        gate_bwd, name="gate_bwd", ins=gate_ins + [(y_a, D_MODEL, 0, "row"), (y_b, D_MODEL, 0, "row"), (du, D_MODEL, 0, "row")],
        outs=[(D_MODEL, BF16), (D_MODEL, BF16), (2 * D_MODEL, BF16)], sums=[2 * D_MODEL])
    d_w_oa_p = mm(o_a, dy_a, mode="tn", name="d_w_o_mla", tm=1024, tn=1024, tk=1024)
    d_w_ob = mm(o_b, dy_b, mode="tn", name="d_w_o_dil", tm=1024, tn=1024, tk=1024)
    grads = dict(w_o_mla=_split_cols(_unpad_rows(d_w_oa_p, V_DIM)), w_o_dil=_split_cols(d_w_ob),
                 w_out=d_w_out.reshape(N_CHIPS, D_MODEL // N_CHIPS, D_MODEL), w_ff1=d_w_ff1, w_ff2=d_w_ff2.reshape(N_CHIPS, D_FF // N_CHIPS, D_MODEL))
    sent = None if early_grads is None else early_grads(grads)
    do_a = mm(dy_a, w_oa_p, mode="nt", name="d_o_mla", tm=1024, tn=1024, tk=1024, after=sent)
    do_b = mm(dy_b, w_ob, mode="nt", name="d_o_dil", tm=1024, tn=1024, tk=1024)
    dq_a, dk_a, dv_a = _attn_bwd(q_a, 0, k_a, 0, v_a, 0, o_a, do_a, lse_a, mla_bwd_tables, scale_mla,
                                 name="mla_bwd", batch=batch, seq=seq, out_dtype=F32)
    going = None if early_grads_go is None else early_grads_go(dq_a)
    d_qkv_d = _attn_bwd(proj_d, 0, proj_d, 1, proj_d, 2, o_b, do_b, lse_b, dil_bwd_tables, scale_dil,
                        name="dil_bwd", batch=batch, seq=seq, out_dtype=BF16, sub=2, stride=4, after=going)

    def mla_post(dq, dk, c, up, dn, c_only):
        dqs = [_rope_bwd(dq[:, h * LANES:(h + 1) * LANES], c, up, dn) for h in range(N_HEADS)]
        dk_sum = dk[:, :LANES]
        for h in range(1, N_HEADS):
            dk_sum = dk_sum + dk[:, h * LANES:(h + 1) * LANES]
        return jnp.concatenate(dqs, axis=1), _rope_bwd(dk_sum, c_only, up, dn)

    dq_lin, d_kr = row(mla_post, name="mla_unrope",
                       ins=[(dq_a, D_MODEL, 0, "row"), (dk_a, D_MODEL, 0, "row"), pos(rope_c), pos(rope_up), pos(rope_dn), pos(rope_c_only)],
                       outs=[(N_HEADS * LANES, BF16), (LANES, BF16)])
    d_w_uq_pt = mm(dq_lin, qn, mode="tn", name="d_w_uq", tm=1024, tn=Q_LORA, tk=1024)
    d_w_uk_p = mm(kvn, dk_a, mode="tn", name="d_w_uk", tm=KV_LORA, tn=1024, tk=1024)
    d_w_uv_p = mm(kvn, dv_a, mode="tn", name="d_w_uv", tm=KV_LORA, tn=1024, tk=1024)
    d_qn = mm(dq_lin, w_uq_pt, mode="nn", name="d_qn", tm=1024, tn=Q_LORA, tk=1024)
    d_kvn_k = mm(dk_a, w_uk_p, mode="nt", name="d_kvn_k", tm=1024, tn=KV_LORA, tk=1024)
    d_kvn = mm(dv_a, w_uv_p, mode="nt", name="d_kvn", tm=1024, tn=KV_LORA, tk=1024, extras=(d_kvn_k,), epilogue=lambda acc, e: (acc + e,))

    def rms_bwd(lora, dq, dkv, dkr, gq, gkv):
        dxq, dgq = _rms_bwd(lora[:, :Q_LORA], gq, dq)
        dxk, dgk = _rms_bwd(lora[:, Q_LORA:], gkv, dkv)
        tail = jnp.zeros((dxq.shape[0], P_GATE - P_KR - LANES), F32)
        return jnp.concatenate([dxq, dxk, dkr.astype(F32), tail], axis=1), dgq, dgk

    d_tail, d_g_q_a, d_g_kv_a = row(
        rms_bwd, name="mla_rms_bwd",
        ins=[(proj, LORA_W, lora0, "row"), (d_qn, Q_LORA, 0, "row"), (d_kvn, KV_LORA, 0, "row"), (d_kr, LANES, 0, "row"),
             (g_q_a, 0, 0, "full"), (g_kv_a, 0, 0, "full")],
        outs=[(P_GATE, BF16)], sums=[Q_LORA, KV_LORA])
    d_proj = [d_tail, d_gates, d_qkv_d]
    d_w_in_pt = mm(d_proj, x, mode="tn", name="d_w_in", tm=1024, tn=1024, tk=1024)
    grads.update(w_in=_unpad_w_in_t(d_w_in_pt).reshape(N_CHIPS, IN_WIDTH // N_CHIPS, D_MODEL),
                 w_uq=_unpad_head_rows(d_w_uq_pt, MLA_QK).reshape(N_CHIPS, N_HEADS * MLA_QK // N_CHIPS, Q_LORA),
                 w_ukv=_split_cols(_merge_ukv(d_w_uk_p, d_w_uv_p)))
    leaving = None if last_grads is None else last_grads(grads)
    grad_x = mm(d_proj, jnp.concatenate([w_main_t, w_dil_t], axis=0), mode="nn", name="d_x", tm=1024, tn=1024, tk=1024, extras=(dr1,), epilogue=lambda acc, rv: (acc + ALPHA * rv,),
                after=leaving)

    grads.update(
        b_gate=d_b_gate.reshape(2, D_MODEL), g_q_a=d_g_q_a, g_kv_a=d_g_kv_a, ln1_g=d_ln1_g, ln1_b=d_ln1_b, ln2_g=d_ln2_g, ln2_b=d_ln2_b)
    return loss_l, grad_x.reshape(batch, seq, D_MODEL), grads


BIG = ("w_in", "w_uq", "w_ukv", "w_o_mla", "w_o_dil", "w_out", "w_ff1", "w_ff2")
SMALL = (("b_gate", 2 * D_MODEL), ("g_q_a", Q_LORA), ("g_kv_a", KV_LORA), ("ln1_g", D_MODEL), ("ln1_b", D_MODEL),
         ("ln2_g", D_MODEL), ("ln2_b", D_MODEL))
TRANSPOSED = ("w_in", "w_uq")
D2D_PIECES = (4, 2, 1)
ANY = pl.BlockSpec(memory_space=pl.ANY)
SIDE_EFFECTS = pltpu.CompilerParams(has_side_effects=True)


def _place():
    x, y, c = lax.axis_index("x"), lax.axis_index("y"), lax.axis_index("c")
    return x, y, c, ((1 - x, y), (x, 1 - y), (1 - x, 1 - y))


def _half_axis(shape):
    return 0 if shape[0] % 32 == 0 else 1


def _half_shape(shape):
    return (shape[0] // 2, shape[1]) if _half_axis(shape) == 0 else (shape[0], shape[1] // 2)


def _window(ref, lead, shape, which=None, pieces=False):
    axis = _half_axis(shape)
    size = shape[axis] if which is None else shape[axis] // 2
    base = 0 if which is None else which * size
    tile = (16, LANES)[axis]
    count = next(c for c in D2D_PIECES if size % (tile * c) == 0) if pieces else 1
    step = size // count
    spans = [pl.ds(pl.multiple_of(base + i * step, tile), step) for i in range(count)]
    refs = [ref.at[(*lead, s)] if axis == 0 else ref.at[(*lead, slice(None), s)] for s in spans]
    return refs if pieces else refs[0]


def _remote(src, dst, send, recv, to):
    return pltpu.make_async_remote_copy(src_ref=src, dst_ref=dst, send_sem=send, recv_sem=recv, device_id=to, device_id_type=MESH)


def _pair_split(grads, name):
    n = len(grads)

    def body(*refs):
        srcs, outs, (send, recv) = refs[:n], refs[n:2 * n], refs[2 * n:]
        x, y, c, _ = _place()
        for t in range(n):
            for s in range(N_CHIPS):
                _remote(_window(srcs[t], (s,), grads[t].shape[1:], 1 - c), outs[t].at[s], send.at[t], recv.at[t], (x, y, 1 - c)).start()
        for t in range(n):
            _remote(_window(srcs[t], (slice(None),), grads[t].shape[1:], 1 - c), outs[t], send.at[t], recv.at[t], (x, y, 1 - c)).wait()

    return pl.pallas_call(
        body, name=name, in_specs=[ANY] * n, out_specs=[ANY] * n,
        out_shape=[jax.ShapeDtypeStruct((N_CHIPS,) + _half_shape(g.shape[1:]), g.dtype) for g in grads],
        scratch_shapes=[pltpu.SemaphoreType.DMA((n,)), pltpu.SemaphoreType.DMA((n,))],
        compiler_params=SIDE_EFFECTS,
    )(*grads)


HBM = pl.BlockSpec(memory_space=pltpu.HBM)
SEM = pl.BlockSpec(memory_space=pltpu.SEMAPHORE)
SPLIT = pltpu.CompilerParams(has_side_effects=pltpu.SideEffectType.DATAFLOW_SIDE_EFFECTING)


def _in_hbm(a):
    return pltpu.with_memory_space_constraint(a, pltpu.HBM)


def _split_copies(kind, srcs, lands):
    x, y, c, chips = _place()
    out = []
    for t in range(len(srcs)):
        if kind == "pair":
            out += [(t, s % 3, _window(srcs[t], (s,), srcs[t].shape[1:], 1 - c), lands[t].at[s], (x, y, 1 - c)) for s in range(N_CHIPS)]
            continue
        if kind == "join":
            out += [(t, 0, a, b, (x, y, 1 - c)) for a, b in zip(_window(srcs[t], (), srcs[t].shape, None, True), _window(lands[t], (), srcs[t].shape, None, True))]
            continue
        if kind == "forward":
            shape, sibling = srcs[t].shape, (x, y, 1 - c)
            out += [(t, 0, a, b, sibling) for a, b in zip(_window(srcs[t], (), shape, None, True), _window(lands[t], (2 * x + y,), shape, None, True))]
            out += [(t, j, a, a, sibling) for j, (cx, cy) in enumerate(chips) for a in _window(lands[t], (2 * cx + cy,), shape, c, True)]
            continue
        for j, (cx, cy) in enumerate(chips):
            if kind == "gather":
                shape = srcs[t].shape
                out.append((t, j, _window(srcs[t], (), shape, c), _window(lands[t], (2 * x + y,), shape, c), (cx, cy, c)))
            else:
                out.append((t, j, srcs[t].at[2 * cx + cy], lands[t].at[j], (cx, cy, c)))
    return out


def _split_start(kind, srcs, land_shapes, name, lands=None, after=None):
    n = len(srcs)

    def body(*refs):
        src_refs, land_refs, sems, token = refs[:n], refs[n:2 * n], refs[-7 - 2 * n:-1 - 2 * n], refs[-1]
        for t, j, s, d, to in _split_copies(kind, src_refs, land_refs):
            _remote(s, d, sems[j], sems[3 + j], to).start()
        token[...] = jnp.zeros_like(token)

    lands = [_in_hbm(lax.empty(s.shape, s.dtype)) for s in land_shapes] if lands is None else list(lands)
    thru = [pltpu.HBM(a.shape, a.dtype) for a in list(srcs) + lands]
    res = pl.pallas_call(
        body, name=name,
        out_shape=(*[pltpu.SemaphoreType.DMA(())] * 6, *thru, jax.ShapeDtypeStruct((8, LANES), F32)),
        in_specs=[HBM] * (2 * n) + [ANY] * (after is not None), out_specs=(*[SEM] * 6, *[HBM] * (2 * n), pl.BlockSpec(memory_space=pltpu.VMEM)),
        input_output_aliases={i: 6 + i for i in range(2 * n)}, compiler_params=SPLIT,
    )(*[_in_hbm(s) for s in srcs], *lands, *([after] if after is not None else []))
    return res[:6], res[6:6 + n], res[6 + n:6 + 2 * n], res[-1]


def _split_wait(kind, sems, srcs, lands, after, name):
    n = len(srcs)

    def body(*refs):
        src_refs, land_refs, sem_refs = refs[:n], refs[n:2 * n], refs[2 * n:2 * n + 6]
        for t, j, s, d, to in _split_copies(kind, src_refs, land_refs):
            cp = _remote(s, d, sem_refs[j], sem_refs[3 + j], to)
            cp.wait_send()
            cp.wait_recv()

    res = pl.pallas_call(
        body, name=name, out_shape=[pltpu.HBM(a.shape, a.dtype) for a in list(srcs) + list(lands)],
        in_specs=[HBM] * (2 * n) + [SEM] * 6 + [ANY], out_specs=[HBM] * (2 * n),
        input_output_aliases={i: i for i in range(2 * n)}, compiler_params=SPLIT,
    )(*srcs, *lands, *sems, after)
    return res[:n], res[n:]


def _sum_all_devices(vec, name):
    n_rows = vec.shape[0]

    def body(v_ref, out_ref, buf, send, recv):
        x, y, c, _ = _place()
        me = 4 * x + 2 * y + c
        buf[me] = v_ref[...]
        flips = [(a, b, d) for a in (0, 1) for b in (0, 1) for d in (0, 1)][1:]
        copies = []
        for r, (a, b, d) in enumerate(flips):
            px, py, pc = (1 - x if a else x), (1 - y if b else y), (1 - c if d else c)
            copies.append(pltpu.make_async_remote_copy(src_ref=v_ref, dst_ref=buf.at[me], send_sem=send.at[r], recv_sem=recv.at[r],
                                                       device_id=(px, py, pc), device_id_type=MESH))
            copies[-1].start()
        for r, (a, b, d) in enumerate(flips):
            px, py, pc = (1 - x if a else x), (1 - y if b else y), (1 - c if d else c)
            pltpu.make_async_remote_copy(src_ref=v_ref, dst_ref=buf.at[4 * px + 2 * py + pc], send_sem=send.at[r], recv_sem=recv.at[r],
                                         device_id=(px, py, pc), device_id_type=MESH).wait_recv()
        for cp in copies:
            cp.wait_send()
        total = buf[0]
        for k in range(1, N_DEV):
            total = total + buf[k]
        out_ref[...] = total

    vmem = pl.BlockSpec(memory_space=pltpu.VMEM)
    return pl.pallas_call(
        body, name=name, in_specs=[vmem], out_specs=vmem, out_shape=jax.ShapeDtypeStruct(vec.shape, F32),
        scratch_shapes=[pltpu.VMEM((N_DEV, n_rows, LANES), F32), pltpu.SemaphoreType.DMA((N_DEV - 1,)), pltpu.SemaphoreType.DMA((N_DEV - 1,))],
        compiler_params=pltpu.CompilerParams(has_side_effects=True),
    )(vec)


def _half_tile(half, width):
    t = half
    while t * width * 4 > (2 << 20) and t % 32 == 0:
        t //= 2
    return t


def _pair_add(g, theirs, core, name):
    _, half, width = theirs.shape
    t = _half_tile(half, width)
    n = half // t

    def body(c_ref, a_ref, b_ref, o_ref):
        o_ref[...] = (a_ref[...] + b_ref[...]).astype(BF16)

    tile = pl.BlockSpec((1, t, width), lambda j, i, c_ref: (j, i, 0))
    if _half_axis(g.shape[1:]) == 0:
        mine = pl.BlockSpec((1, t, width), lambda j, i, c_ref: (j, c_ref[0] * n + i, 0))
    else:
        mine = pl.BlockSpec((1, t, width), lambda j, i, c_ref: (j, i, c_ref[0]))
    return pl.pallas_call(
        body, name=name,
        grid_spec=pltpu.PrefetchScalarGridSpec(num_scalar_prefetch=1, grid=(N_CHIPS, n), in_specs=[mine, tile], out_specs=tile),
        out_shape=jax.ShapeDtypeStruct(theirs.shape, BF16), compiler_params=_params(("parallel", "parallel")),
    )(core, g, theirs)


def _chip_sum(part, others, chip, name, after=None):
    _, half, width = part.shape
    t = _half_tile(half, width)

    def body(s_ref, mine, p0, p1, p2, *rest):
        o_ref = rest[-1]
        o_ref[...] = ((mine[0].astype(F32) + p0[0].astype(F32)) + p1[0].astype(F32)) + p2[0].astype(F32)

    return pl.pallas_call(
        body, name=name,
        grid_spec=pltpu.PrefetchScalarGridSpec(
            num_scalar_prefetch=1, grid=(half // t,),
            in_specs=[pl.BlockSpec((1, t, width), lambda i, s_ref: (s_ref[0], i, 0))]
            + [pl.BlockSpec((1, t, width), lambda i, s_ref, j=j: (j, i, 0)) for j in range(3)] + [pl.BlockSpec(memory_space=pl.ANY)] * (after is not None),
            out_specs=pl.BlockSpec((t, width), lambda i, s_ref: (i, 0))),
        out_shape=jax.ShapeDtypeStruct((half, width), F32), compiler_params=_params(("parallel",)),
    )(chip, part, others, others, others, *([after] if after is not None else []))


EARLY = ("w_in", "w_uq", "w_ukv")
LATE = ("w_o_mla", "w_o_dil", "w_out", "w_ff1", "w_ff2")


def _chip_partials(grads, names, core, tag):
    gs = [grads[n] for n in names]
    theirs = _pair_split(gs, "pair_split_" + tag)
    return [_pair_add(g, th, core, "pair_add_" + n) for g, th, n in zip(gs, theirs, names)]


def _sum_small(vals):
    n_in = len(vals)
    n_rows = sum(a.shape[0] * a.shape[1] // LANES for a in vals)
    pad_rows = -(-n_rows // 8) * 8

    def chunks(refs):
        return [(ref, a, j) for ref in refs for a in range(ref.shape[0]) for j in range(ref.shape[1] // LANES)]

    def body(*refs):
        ins, outs, (buf, send, recv) = refs[:n_in], refs[n_in:2 * n_in], refs[2 * n_in:]
        x, y, c, _ = _place()
        me = 4 * x + 2 * y + c
        for r, (ref, a, j) in enumerate(chunks(ins)):
            buf[me, r:r + 1, :] = ref[a:a + 1, j * LANES:(j + 1) * LANES]
        if pad_rows > n_rows:
            buf[me, n_rows:pad_rows, :] = jnp.zeros((pad_rows - n_rows, LANES), F32)
        flips = [(a, b, d) for a in (0, 1) for b in (0, 1) for d in (0, 1)][1:]
        peers = [((1 - x if a else x), (1 - y if b else y), (1 - c if d else c)) for a, b, d in flips]
        copies = [_remote(buf.at[me], buf.at[me], send.at[r], recv.at[r], peer) for r, peer in enumerate(peers)]
        for cp in copies:
            cp.start()
        for r, (px, py, pc) in enumerate(peers):
            _remote(buf.at[me], buf.at[4 * px + 2 * py + pc], send.at[r], recv.at[r], (px, py, pc)).wait_recv()
        for cp in copies:
            cp.wait_send()
        total = buf[0]
        for k in range(1, N_DEV):
            total = total + buf[k]
        for r, (ref, a, j) in enumerate(chunks(outs)):
            ref[a:a + 1, j * LANES:(j + 1) * LANES] = total[r:r + 1, :]

    vmem = pl.BlockSpec(memory_space=pltpu.VMEM)
    return pl.pallas_call(
        body, name="sum_small", in_specs=[vmem] * n_in, out_specs=[vmem] * n_in,
        out_shape=[jax.ShapeDtypeStruct(a.shape, F32) for a in vals],
        scratch_shapes=[pltpu.VMEM((N_DEV, pad_rows, LANES), F32), pltpu.SemaphoreType.DMA((N_DEV - 1,)), pltpu.SemaphoreType.DMA((N_DEV - 1,))],
        compiler_params=SIDE_EFFECTS,
    )(*vals)


def _adam_math(w, g, m, v):
    nm = B1 * m + (1.0 - B1) * g
    nv = B2 * v + (1.0 - B2) * (g * g)
    m_hat = nm / (1.0 - B1 ** ADAM_STEP)
    v_hat = nv / (1.0 - B2 ** ADAM_STEP)
    return -LR * (m_hat / (jnp.sqrt(v_hat) + ADAM_EPS) + WD * w), nm, nv


def _adamw_big(w, mine, theirs, m, v, core, name, side_by_side=False):
    rows, width = w.shape
    if side_by_side:
        t = next(c for c in (152, 96, 64, 32, 16, 8) if rows % c == 0)
        hb = None
        half_spec = pl.BlockSpec((t, width // 2), lambda i, c_ref: (i, 0))
    else:
        t = next(c for c in (256, 128, 64, 32, 16, 8) if (rows // 2) % c == 0)
        hb = rows // 2 // t
        half_spec = pl.BlockSpec((t, width), lambda i, c_ref: (i % hb, 0))

    def body(c_ref, w_ref, a_ref, b_ref, m_ref, v_ref, g_ref, d_ref, nm_ref, nv_ref):
        south = c_ref[0] == 0
        if side_by_side:
            g = jnp.where(south, jnp.concatenate([a_ref[...], b_ref[...]], axis=1), jnp.concatenate([b_ref[...], a_ref[...]], axis=1))
        else:
            g = jnp.where((pl.program_id(0) < hb) == south, a_ref[...], b_ref[...])
        g_ref[...] = g
        d_ref[...], nm_ref[...], nv_ref[...] = _adam_math(w_ref[...], g, m_ref[...], v_ref[...])

    spec = pl.BlockSpec((t, width), lambda i, c_ref: (i, 0))
    return pl.pallas_call(
        body, name=name,
        grid_spec=pltpu.PrefetchScalarGridSpec(num_scalar_prefetch=1, grid=(rows // t,),
                                               in_specs=[spec, half_spec, half_spec, spec, spec], out_specs=[spec] * 4),
        out_shape=[jax.ShapeDtypeStruct(w.shape, F32)] * 4, compiler_params=_params(("parallel",)),
    )(core, w, mine, theirs, m, v)


def _adamw_small(ws, gs, ms, vs):
    n = len(ws)

    def body(*refs):
        for t in range(n):
            w_ref, g_ref, m_ref, v_ref = (refs[k * n + t] for k in range(4))
            d, nm, nv = _adam_math(w_ref[...], g_ref[...], m_ref[...], v_ref[...])
            refs[4 * n + t][...] = d
            refs[5 * n + t][...] = nm
            refs[6 * n + t][...] = nv

    vmem = pl.BlockSpec(memory_space=pltpu.VMEM)
    res = pl.pallas_call(body, name="adamw_small", in_specs=[vmem] * (4 * n), out_specs=[vmem] * (3 * n),
                         out_shape=[jax.ShapeDtypeStruct(a.shape, F32) for a in ws] * 3)(*ws, *gs, *ms, *vs)
    return res[:n], res[n:2 * n], res[2 * n:]


def kernel(x, w_in, b_gate, g_q_a, w_uq, g_kv_a, w_ukv, w_o_mla, w_o_dil, w_out, ln1_g, ln1_b, w_ff1, w_ff2, ln2_g, ln2_b, loss_target, m_w_in, m_b_gate, m_g_q_a, m_w_uq, m_g_kv_a, m_w_ukv, m_w_o_mla, m_w_o_dil, m_w_out, m_ln1_g, m_ln1_b, m_w_ff1, m_w_ff2, m_ln2_g, m_ln2_b, v_w_in, v_b_gate, v_g_q_a, v_w_uq, v_g_kv_a, v_w_ukv, v_w_o_mla, v_w_o_dil, v_w_out, v_ln1_g, v_ln1_b, v_w_ff1, v_w_ff2, v_ln2_g, v_ln2_b):
    order = ("w_in", "b_gate", "g_q_a", "w_uq", "g_kv_a", "w_ukv", "w_o_mla", "w_o_dil", "w_out", "ln1_g", "ln1_b", "w_ff1", "w_ff2", "ln2_g", "ln2_b")
    w = dict(w_in=w_in, b_gate=b_gate, g_q_a=g_q_a, w_uq=w_uq, g_kv_a=g_kv_a, w_ukv=w_ukv, w_o_mla=w_o_mla, w_o_dil=w_o_dil, w_out=w_out,
             ln1_g=ln1_g, ln1_b=ln1_b, w_ff1=w_ff1, w_ff2=w_ff2, ln2_g=ln2_g, ln2_b=ln2_b)
    m = dict(w_in=m_w_in, b_gate=m_b_gate, g_q_a=m_g_q_a, w_uq=m_w_uq, g_kv_a=m_g_kv_a, w_ukv=m_w_ukv, w_o_mla=m_w_o_mla, w_o_dil=m_w_o_dil,
             w_out=m_w_out, ln1_g=m_ln1_g, ln1_b=m_ln1_b, w_ff1=m_w_ff1, w_ff2=m_w_ff2, ln2_g=m_ln2_g, ln2_b=m_ln2_b)
    v = dict(w_in=v_w_in, b_gate=v_b_gate, g_q_a=v_g_q_a, w_uq=v_w_uq, g_kv_a=v_g_kv_a, w_ukv=v_w_ukv, w_o_mla=v_w_o_mla, w_o_dil=v_w_o_dil,
             w_out=v_w_out, ln1_g=v_ln1_g, ln1_b=v_ln1_b, w_ff1=v_w_ff1, w_ff2=v_w_ff2, ln2_g=v_ln2_g, ln2_b=v_ln2_b)
    chip = 2 * lax.axis_index("x") + lax.axis_index("y")
    south = (lax.axis_index("c") == 0).astype(F32)
    gate_w = D_MODEL // N_CHIPS

    core = lax.axis_index("c").astype(jnp.int32).reshape(1)
    turn = lambda n, a: a.T if n in TRANSPOSED else a
    shards = {n: turn(n, w[n][0]).astype(BF16) for n in BIG}
    b_mine = lax.dynamic_update_slice(jnp.zeros((2, D_MODEL), F32), b_gate[0] * south, (0, chip * gate_w))
    b_full = _sum_all_devices(b_mine.reshape(-1, LANES), "gather_b_gate").reshape(2, D_MODEL)
    early_shards, late_shards = [shards[n] for n in EARLY], [shards[n] for n in LATE]
    gathered = lambda group: [jax.ShapeDtypeStruct((N_CHIPS,) + s.shape, BF16) for s in group]
    e_sems, e_srcs, e_lands, e_token = _split_start("gather", early_shards, gathered(early_shards), "gather_early_start", after=b_full)
    g_sems, g_srcs, g_lands, g_token = _split_start("gather", late_shards, gathered(late_shards), "gather_late_start", after=e_token)
    tables = _attention_tables(x.shape[1])
    behind = tables[1][0][1]
    e_srcs, e_lands = _split_wait("gather", e_sems, e_srcs, e_lands, behind, "gather_early_wait")
    e_forward = _split_start("forward", e_srcs, None, "gather_early_forward_start", lands=e_lands)
    first = dict(zip(EARLY, _split_wait("forward", *e_forward[:3], e_forward[-1], "gather_early_forward_wait")[1]))

    sent = {}

    def late_arrived(after):
        srcs, lands = _split_wait("gather", g_sems, g_srcs, g_lands, after, "gather_late_wait")
        sent["forward"] = _split_start("forward", srcs, None, "gather_late_forward_start", lands=lands)
        return sent["forward"][-1]

    def late_weights(after):
        return dict(zip(LATE, _split_wait("forward", *sent["forward"][:3], after, "gather_late_forward_wait")[1]))

    exchange_shapes = lambda parts: [jax.ShapeDtypeStruct((3,) + p.shape[1:], BF16) for p in parts]

    def early_grads(grads_late):
        gs = [grads_late[n] for n in LATE]
        shapes = [jax.ShapeDtypeStruct((N_CHIPS,) + _half_shape(g.shape[1:]), F32) for g in gs]
        sent["pair"] = _split_start("pair", gs, shapes, "pair_split_late_start")
        return sent["pair"][-1]

    def early_grads_go(after):
        gs, theirs = _split_wait("pair", *sent["pair"][:3], after, "pair_split_late_wait")
        parts = [_pair_add(g, th, core, "pair_add_" + n) for g, th, n in zip(gs, theirs, LATE)]
        sent["late"] = _split_start("scatter", parts, exchange_shapes(parts), "exchange_late_start")
        return sent["late"][-1]

    def last_grads(grads_early):
        parts = _chip_partials(grads_early, EARLY, core, "early")
        sent["early"] = _split_start("scatter", parts, exchange_shapes(parts), "exchange_early_start")
        return sent["early"][-1]

    loss_part, grad_x, grads = _local_step(x, loss_target, first, b_full, g_q_a, g_kv_a, ln1_g, ln1_b, ln2_g, ln2_b, token=g_token,
                                           late_arrived=late_arrived, late_weights=late_weights, early_grads=early_grads, early_grads_go=early_grads_go,
                                           last_grads=last_grads, tables=tables)

    g_out, delta, new_m, new_v = {}, {}, {}, {}
    chip1 = chip.astype(jnp.int32).reshape(1)

    def sum_and_send(names, parts, others, tag):
        totals = [_chip_sum(p, o, chip1, "chip_sum_" + n) for n, p, o in zip(names, parts, others)]
        return _split_start("join", totals, [jax.ShapeDtypeStruct(t.shape, F32) for t in totals], "pair_join_" + tag + "_start")

    def adam(names, joined, after, tag):
        totals, halves = _split_wait("join", *joined[:3], after, "pair_join_" + tag + "_wait")
        for n, mine, theirs in zip(names, totals, halves):
            res = _adamw_big(turn(n, w[n][0]), mine, theirs, turn(n, m[n][0]), turn(n, v[n][0]), core, "adamw_" + n,
                             side_by_side=mine.shape[0] == shards[n].shape[0])
            g_out[n], delta[n], new_m[n], new_v[n] = (turn(n, r) for r in res)

    late_joined = sum_and_send(LATE, *_split_wait("scatter", *sent["late"][:3], grad_x, "exchange_late_wait"), "late")
    early_joined = sum_and_send(EARLY, *_split_wait("scatter", *sent["early"][:3], late_joined[-1], "exchange_early_wait"), "early")
    small_names = [name for name, _ in SMALL]
    sums = _sum_small([grads[name] for name in small_names] + [loss_part])
    loss = sums[-1][0, 0]
    g_small = dict(zip(small_names, sums))
    g_small["b_gate"] = lax.dynamic_slice(g_small["b_gate"], (0, chip * gate_w), (2, gate_w))
    flat = lambda a: a.reshape(-1, a.shape[-1])
    res = _adamw_small(*[[flat(d[name]) for name in small_names] for d in (w, g_small, m, v)])
    g_out.update(g_small)
    for d, r in zip((delta, new_m, new_v), res):
        d.update(zip(small_names, r))
    adam(LATE, late_joined, res[0][0], "late")
    adam(EARLY, early_joined, delta[LATE[-1]], "early")

    lead = lambda d: [d[name].reshape(w[name].shape) for name in order]
    return (loss, grad_x, *lead(g_out), *lead(delta), *lead(new_m), *lead(new_v))
```

```python
import functools
import math

import jax
import jax.numpy as jnp
from jax import lax
from jax.experimental import pallas as pl
from jax.experimental.pallas import tpu as pltpu

F32 = jnp.float32
BF16 = jnp.bfloat16
MESH = pl.DeviceIdType.MESH

D_MODEL = 1024
N_HEADS = 8
LANES = 128
NOPE, ROPE, V_DIM = 64, 32, 64
MLA_QK = NOPE + ROPE
Q_LORA, KV_LORA = 384, 256
DIL_DIM = 64
DIL_PATTERNS = ((128, 1), (512, 4), (2048, 16))
D_FF = 4096
N_CHIPS = 4
N_DEV = 8
IN_WIDTH = 4256
LN_EPS, RMS_EPS = 1e-5, 1e-6
NEG = -1e30
LOG2E, LN2 = 1.4426950408889634, 0.6931471805599453
ALPHA = 2.0 ** 0.25
ROPE_THETA = 10000.0
LR, B1, B2, ADAM_EPS, WD, ADAM_STEP = 0.001, 0.9, 0.999, 1e-8, 0.01, 10

P_LORA, P_KR, P_GATE, P_HALF = 0, 640, 1024, 3072
DIL_GROUP = 4 * LANES
P_DIL = N_HEADS // 2 * DIL_GROUP
LORA_W = Q_LORA + KV_LORA
KR_LANE = NOPE

ATT_T = 512
ROW_T = 512
VMEM_LIMIT = 56 * 1024 * 1024

NN = (((1,), (0,)), ((), ()))
NT = (((1,), (1,)), ((), ()))
TN = (((0,), (0,)), ((), ()))


def _params(sem=None, **kw):
    return pltpu.CompilerParams(dimension_semantics=sem, vmem_limit_bytes=VMEM_LIMIT, **kw)


def _matmul(a, b, *, mode, name, tm, tn, tk, out_dtypes=(F32,), extras=(), epilogue=None, b_shards=False, out_shards=False, after=None):
    pieces = list(a) if isinstance(a, (list, tuple)) else [a]
    n_pc = len(pieces)
    a_shape = (pieces[0].shape[0], sum(p.shape[1] for p in pieces))
    if b_shards:
        n_sh, rows_b, cols_b = b.shape
        b_shape = (rows_b, n_sh * cols_b)
    else:
        b_shape = b.shape
    if mode == "nn":
        (m, k), (k2, n) = a_shape, b_shape
    elif mode == "nt":
        (m, k), (n, k2) = a_shape, b_shape
    else:
        (k, m), (k2, n) = a_shape, b_shape
    assert k == k2, (a_shape, b.shape, mode)
    tm, tn, tk = min(tm, m), min(tn, n), min(tk, k)
    assert m % tm == 0 and n % tn == 0 and k % tk == 0, (name, m, n, k, tm, tn, tk)
    nk = k // tk
    n_ex, n_out = len(extras), len(out_dtypes)
    n_in = n_pc + 1 + n_ex + (after is not None)
    dims = {"nn": NN, "nt": NT, "tn": TN}[mode]
    col_tile = tm if mode == "tn" else tk
    blocks = [p.shape[1] // col_tile for p in pieces]
    firsts = [sum(blocks[:p]) for p in range(n_pc)]
    assert all(p.shape[1] % col_tile == 0 for p in pieces), (name, col_tile)

    def body(*refs):
        a_refs, b_ref = refs[:n_pc], refs[n_pc]
        ex_refs = refs[n_pc + 1:n_pc + 1 + n_ex]
        out_refs = refs[n_in:n_in + n_out]

        def finish(acc):
            outs = epilogue(acc, *[r[...] for r in ex_refs]) if epilogue is not None else (acc,)
            for r, o in zip(out_refs, outs):
                r[...] = o.astype(r.dtype)

        kk = pl.program_id(2)

        def step(a_ref):
            part = lax.dot_general(a_ref[...].astype(BF16), b_ref[...].astype(BF16), dims, preferred_element_type=F32)
            if nk == 1:
                finish(part)
                return
            acc_ref = refs[-1]

            @pl.when(kk == 0)
            def _():
                acc_ref[...] = part

            @pl.when(kk > 0)
            def _():
                acc_ref[...] += part

            @pl.when(kk == nk - 1)
            def _():
                finish(acc_ref[...])

        if n_pc == 1:
            step(a_refs[0])
        else:
            at = pl.program_id(0) if mode == "tn" else kk
            for p in range(n_pc):
                pl.when(jnp.logical_and(at >= firsts[p], at < firsts[p] + blocks[p]))(functools.partial(step, a_refs[p]))

    def a_spec_of(p):
        if n_pc == 1:
            return pl.BlockSpec((tk, tm), lambda i, j, kk: (kk, i)) if mode == "tn" else pl.BlockSpec((tm, tk), lambda i, j, kk: (i, kk))
        col = lambda at: jnp.clip(at - firsts[p], 0, blocks[p] - 1)
        mine = lambda at: jnp.logical_and(at >= firsts[p], at < firsts[p] + blocks[p])
        if mode == "tn":
            return pl.BlockSpec((tk, tm), lambda i, j, kk: (jnp.where(mine(i), kk, 0), col(i)))
        return pl.BlockSpec((tm, tk), lambda i, j, kk: (i, col(kk)))

    b_spec = {"nn": pl.BlockSpec((tk, tn), lambda i, j, kk: (kk, j)),
              "nt": pl.BlockSpec((tn, tk), lambda i, j, kk: (j, kk)),
              "tn": pl.BlockSpec((tk, tn), lambda i, j, kk: (kk, j))}[mode]
    tile = pl.BlockSpec((tm, tn), lambda i, j, kk: (i, j))
    out_spec, out_dims = tile, (m, n)
    if b_shards and mode == "nn":
        per = cols_b // tn
        b_spec = pl.BlockSpec((None, tk, tn), lambda i, j, kk: (j // per, kk, j % per))
    elif b_shards:
        assert mode == "nt"
        per = cols_b // tk
        b_spec = pl.BlockSpec((None, tn, tk), lambda i, j, kk: (kk // per, j, kk % per))
    if out_shards:
        assert not extras and epilogue is None
        per_out = n // N_CHIPS // tn
        out_spec = pl.BlockSpec((None, tm, tn), lambda i, j, kk: (j // per_out, i, j % per_out))
        out_dims = (N_CHIPS, m, n // N_CHIPS)
    outs = pl.pallas_call(
        body, name=name,
        grid=(m // tm, n // tn, nk),
        in_specs=[a_spec_of(p) for p in range(n_pc)] + [b_spec] + [tile] * n_ex + [pl.BlockSpec(memory_space=pl.ANY)] * (after is not None),
        out_specs=[out_spec] * n_out,
        out_shape=[jax.ShapeDtypeStruct(out_dims, dt) for dt in out_dtypes],
        scratch_shapes=[pltpu.VMEM((tm, tn), F32)] if nk > 1 else [],
        compiler_params=_params(("parallel", "parallel", "arbitrary")),
    )(*pieces, b, *extras, *([after] if after is not None else []))
    return outs[0] if n_out == 1 else outs


def _rowwise(fn, *, name, rows, seq, ins, outs, sums=()):
    tm = min(ROW_T, seq)
    n_pos = seq // tm
    n_in, n_out, n_sum = len(ins), len(outs), len(sums)

    def body(*refs):
        vals = fn(*[r[...] for r in refs[:n_in]])
        for r, v in zip(refs[n_in:n_in + n_out], vals[:n_out]):
            r[...] = v.astype(r.dtype)
        first = pl.program_id(0) == 0
        for r, v in zip(refs[n_in + n_out:], vals[n_out:]):
            @pl.when(first)
            def _(r=r, v=v):
                r[...] = v

            @pl.when(jnp.logical_not(first))
            def _(r=r, v=v):
                r[...] += v

    def spec(arr, width, col, kind):
        if kind == "row":
            return pl.BlockSpec((tm, width), lambda i, col=col: (i, col))
        if kind == "pos":
            return pl.BlockSpec((tm, width), lambda i, col=col: (i % n_pos, col))
        return pl.BlockSpec(arr.shape, lambda i: (0,) * arr.ndim)

    res = pl.pallas_call(
        body, name=name,
        grid=(rows // tm,),
        in_specs=[spec(*t) for t in ins],
        out_specs=[pl.BlockSpec((tm, w), lambda i: (i, 0)) for w, _ in outs]
        + [pl.BlockSpec((1, w), lambda i: (0, 0)) for w in sums],
        out_shape=[jax.ShapeDtypeStruct((rows, w), dt) for w, dt in outs]
        + [jax.ShapeDtypeStruct((1, w), F32) for w in sums],
        compiler_params=_params(("arbitrary",)),
    )(*[t[0] for t in ins])
    return res


def _colsum(v):
    return jnp.sum(v, axis=0, keepdims=True)


def _rope_fwd(t, c, s_up, s_dn):
    return t * c + pltpu.roll(t, LANES - 16, 1) * s_up + pltpu.roll(t, 16, 1) * s_dn


def _rope_bwd(d, c, s_up, s_dn):
    return d * c + pltpu.roll(d * s_up, 16, 1) + pltpu.roll(d * s_dn, LANES - 16, 1)


def _rope_tables(seq):
    half = ROPE // 2
    inv = jnp.power(ROPE_THETA, -jnp.arange(half, dtype=F32) / half)
    ang = jnp.arange(seq, dtype=F32)[:, None] * inv[None, :]
    cos, sin = jnp.cos(ang), jnp.sin(ang)
    zeros = jnp.zeros((seq, half), F32)
    lo, hi = jnp.ones((seq, KR_LANE), F32), jnp.ones((seq, LANES - KR_LANE - ROPE), F32)
    c = jnp.concatenate([lo, cos, cos, hi], axis=1)
    c_rope_only = jnp.concatenate([0 * lo, cos, cos, 0 * hi], axis=1)
    s_up = jnp.concatenate([0 * lo, -sin, zeros, 0 * hi], axis=1)
    s_dn = jnp.concatenate([0 * lo, zeros, sin, 0 * hi], axis=1)
    return c, s_up, s_dn, c_rope_only


def _rms(x, g):
    r = lax.rsqrt(jnp.mean(x * x, axis=1, keepdims=True) + RMS_EPS)
    return x * r * g


def _rms_bwd(x, g, dy):
    r = lax.rsqrt(jnp.mean(x * x, axis=1, keepdims=True) + RMS_EPS)
    xh = x * r
    dxh = dy * g
    dx = r * (dxh - xh * jnp.mean(dxh * xh, axis=1, keepdims=True))
    return dx, _colsum(dy * xh)


def _ln_stats(x):
    mu = jnp.mean(x, axis=1, keepdims=True)
    xc = x - mu
    r = lax.rsqrt(jnp.mean(xc * xc, axis=1, keepdims=True) + LN_EPS)
    return xc * r, r


def _ln_bwd(xh, r, g, dy):
    dxh = dy * g
    dx = r * (dxh - jnp.mean(dxh, axis=1, keepdims=True) - xh * jnp.mean(dxh * xh, axis=1, keepdims=True))
    return dx, _colsum(dy * xh), _colsum(dy)


def _table_specs(tables, sub):
    whole = lambda a: pl.BlockSpec(a.shape, lambda b, g: (0,) * a.ndim)
    if len(tables) == 1:
        return [whole(tables[0])]
    return [whole(tables[0]), whole(tables[1]), pl.BlockSpec((sub, 1, LANES), lambda b, g: (g, 0, 0))]


def _biased(s, table_refs, delta, head, rows, cols):
    if delta < table_refs[0].shape[0]:
        s = s + table_refs[0][delta, rows, cols]
    if len(table_refs) == 3:
        s = s - table_refs[2][head, 0:1, 0:1] * table_refs[1][delta, rows, cols]
    return s


def _key_blocks(qi, t):
    i, upper = divmod(qi, 2)
    return [(j * t, t, i - j) for j in range(i)] + [(i * t, t if upper else t // 2, 0)]


def _lane_masks(sub):
    lane = lax.broadcasted_iota(jnp.int32, (1, LANES), 1)
    return [(lane // (LANES // sub) == a).astype(F32) for a in range(sub)]


def _attn_fwd(q, qb0, k, kb0, v, vb0, tables, scale, *, name, batch, seq, sub=1, stride=1, after=None):
    t, tq = ATT_T, ATT_T // 2
    nq = seq // t
    rows = batch * seq
    n_tab = len(tables)

    def body(q_ref, k_ref, v_ref, *rest):
        table_refs = rest[:n_tab]
        o_ref, lse_ref, kb, vtb = rest[n_tab + (after is not None):][:4]
        qbs = rest[n_tab + (after is not None) + 4:]
        masks = _lane_masks(sub)
        for a in range(sub):
            qbs[a][...] = (q_ref[...].astype(F32) * masks[a]).astype(BF16) if sub > 1 else q_ref[...].astype(BF16)
        kb[...] = k_ref[...].astype(BF16)
        vtb[...] = v_ref[...].astype(F32).T.astype(BF16)
        for qi in range(2 * nq):
            at = slice(qi * tq, (qi + 1) * tq)
            window = slice(qi % 2 * tq, (qi % 2 + 1) * tq)
            out_t = None
            for a in range(sub):
                qt = qbs[a][at, :]
                logits = [_biased(lax.dot_general(kb[k0:k0 + kn, :], qt, NT, preferred_element_type=F32) * (scale * LOG2E),
                                  table_refs, delta, a, slice(0, kn), window) for k0, kn, delta in _key_blocks(qi, t)]
                m = functools.reduce(jnp.maximum, [jnp.max(s, axis=0, keepdims=True) for s in logits])
                ps = [jnp.exp2(s - m) for s in logits]
                l = functools.reduce(jnp.add, [jnp.sum(p, axis=0, keepdims=True) for p in ps])
                acc = functools.reduce(jnp.add, [lax.dot_general(vtb[:, k0:k0 + kn], p.astype(BF16), NN, preferred_element_type=F32)
                                                 for (k0, kn, _), p in zip(_key_blocks(qi, t), ps)])
                part = acc / l if sub == 1 else (acc / l) * masks[a].T
                out_t = part if out_t is None else out_t + part
                lse_ref[at, a * LANES:(a + 1) * LANES] = jnp.broadcast_to((m + jnp.log2(l)) * LN2, (LANES, tq)).T
            o_ref[at, :] = out_t.T

    slab = lambda b0, step: pl.BlockSpec((seq, LANES), lambda b, g: (b, b0 + step * g))
    groups = N_HEADS // sub
    return pl.pallas_call(
        body, name=name,
        grid=(batch, groups),
        in_specs=[slab(qb0, stride), slab(kb0, stride), slab(vb0, stride)] + _table_specs(tables, sub)
        + [pl.BlockSpec(memory_space=pl.ANY)] * (after is not None),
        out_specs=[slab(0, 1), pl.BlockSpec((seq, sub * LANES), lambda b, g: (b, g))],
        out_shape=[jax.ShapeDtypeStruct((rows, groups * LANES), F32), jax.ShapeDtypeStruct((rows, N_HEADS * LANES), F32)],
        scratch_shapes=[pltpu.VMEM((seq, LANES), BF16), pltpu.VMEM((LANES, seq), BF16)] + [pltpu.VMEM((seq, LANES), BF16)] * sub,
        compiler_params=_params(("arbitrary", "arbitrary")),
    )(q, k, v, *tables, *([after] if after is not None else []))


def _attn_bwd(q, qb0, k, kb0, v, vb0, o, do, lse, tables, scale, *, name, batch, seq, out_dtype, sub=1, stride=1, after=None):
    t, tq = ATT_T, ATT_T // 2
    nq = seq // t
    rows = batch * seq
    n_tab = len(tables)
    groups = N_HEADS // sub
    n_out = 3 if sub == 1 else 1

    def body(q_ref, k_ref, v_ref, o_ref, do_ref, lse_ref, *rest):
        table_refs = rest[:n_tab]
        rest = rest[n_tab + (after is not None):]
        out_refs, (kb, vb, dka, dva), per_head = rest[:n_out], rest[n_out:n_out + 4], rest[n_out + 4:]
        qbs, dobs, qtbs, dotbs = (per_head[g * sub:(g + 1) * sub] for g in range(4))
        masks = _lane_masks(sub)
        kb[...] = k_ref[...].astype(BF16)
        vb[...] = v_ref[...].astype(BF16)
        for a in range(sub):
            qa = q_ref[...].astype(F32) * masks[a] if sub > 1 else q_ref[...].astype(F32)
            doa = do_ref[...] * masks[a] if sub > 1 else do_ref[...]
            qbs[a][...] = qa.astype(BF16)
            dobs[a][...] = doa.astype(BF16)
            qtbs[a][...] = qa.T.astype(BF16)
            dotbs[a][...] = doa.T.astype(BF16)
        fresh = [True] * (2 * nq)
        for qi in range(2 * nq):
            at = slice(qi * tq, (qi + 1) * tq)
            window = slice(qi % 2 * tq, (qi % 2 + 1) * tq)
            dq_all = None
            for a in range(sub):
                qt, dot = qbs[a][at, :], dobs[a][at, :]
                lse_t = lse_ref[at, a * LANES:a * LANES + 1] * LOG2E
                od = o_ref[at, :] * do_ref[at, :]
                delta = jnp.sum(od * masks[a] if sub > 1 else od, axis=1, keepdims=True)
                dq = None
                for k0, kn, dist in _key_blocks(qi, t):
                    kt, vt = kb[k0:k0 + kn, :], vb[k0:k0 + kn, :]
                    s = lax.dot_general(qt, kt, NT, preferred_element_type=F32) * (scale * LOG2E)
                    p = jnp.exp2(_biased(s, table_refs, dist, a, window, slice(0, kn)) - lse_t)
                    dp = lax.dot_general(dot, vt, NT, preferred_element_type=F32)
                    ds = (p * (dp - delta) * scale).astype(BF16)
                    dk_part = lax.dot_general(qtbs[a][:, at], ds, NN, preferred_element_type=F32)
                    dv_part = lax.dot_general(dotbs[a][:, at], p.astype(BF16), NN, preferred_element_type=F32)
                    for c in range(kn // tq):
                        half, cols = k0 // tq + c, slice(c * tq, (c + 1) * tq)
                        keys = slice(half * tq, (half + 1) * tq)
                        if fresh[half]:
                            dka[:, keys] = dk_part[:, cols]
                            dva[:, keys] = dv_part[:, cols]
                        else:
                            dka[:, keys] += dk_part[:, cols]
                            dva[:, keys] += dv_part[:, cols]
                    dq_part = lax.dot_general(ds, kt, NN, preferred_element_type=F32)
                    dq = dq_part if dq is None else dq + dq_part
                for k0, kn, _ in _key_blocks(qi, t):
                    for c in range(kn // tq):
                        fresh[k0 // tq + c] = False
                dq = dq * masks[a] if sub > 1 else dq
                dq_all = dq if dq_all is None else dq_all + dq
            out_refs[0][at, 0:LANES] = dq_all.astype(out_refs[0].dtype)
        if sub == 1:
            out_refs[1][...] = dka[...].T.astype(out_refs[1].dtype)
            out_refs[2][...] = dva[...].T.astype(out_refs[2].dtype)
        else:
            out_refs[0][:, LANES:2 * LANES] = dka[...].T.astype(out_refs[0].dtype)
            out_refs[0][:, 2 * LANES:3 * LANES] = dva[...].T.astype(out_refs[0].dtype)
            out_refs[0][:, 3 * LANES:] = jnp.zeros((seq, LANES), out_refs[0].dtype)

    slab = lambda b0, step: pl.BlockSpec((seq, LANES), lambda b, g: (b, b0 + step * g))
    if sub == 1:
        out_specs = [slab(0, 1)] * 3
        out_shape = [jax.ShapeDtypeStruct((rows, N_HEADS * LANES), out_dtype)] * 3
    else:
        out_specs = [pl.BlockSpec((seq, 4 * LANES), lambda b, g: (b, g))]
        out_shape = [jax.ShapeDtypeStruct((rows, groups * 4 * LANES), out_dtype)]
    res = pl.pallas_call(
        body, name=name,
        grid=(batch, groups),
        in_specs=[slab(qb0, stride), slab(kb0, stride), slab(vb0, stride), slab(0, 1), slab(0, 1),
                  pl.BlockSpec((seq, sub * LANES), lambda b, g: (b, g))] + _table_specs(tables, sub)
        + [pl.BlockSpec(memory_space=pl.ANY)] * (after is not None),
        out_specs=out_specs, out_shape=out_shape,
        scratch_shapes=[pltpu.VMEM((seq, LANES), BF16)] * 2 + [pltpu.VMEM((LANES, seq), F32)] * 2
        + [pltpu.VMEM((seq, LANES), BF16)] * (2 * sub) + [pltpu.VMEM((LANES, seq), BF16)] * (2 * sub),
        compiler_params=_params(("arbitrary", "arbitrary")),
    )(q, k, v, o, do, lse, *tables, *([after] if after is not None else []))
    return res if sub == 1 else res[0]


def _attention_tables(seq):
    n = seq // ATT_T
    pos = jnp.arange(ATT_T, dtype=jnp.int32)
    tile = jnp.arange(n, dtype=jnp.int32)[:, None, None] * ATT_T

    def of(dist):
        count = jnp.zeros(dist.shape, F32)
        for window, dilation in DIL_PATTERNS:
            count += ((dist >= 0) & (dist <= window) & (dist % dilation == 0)).astype(F32)
        held = jnp.where(count > 0, jnp.log2(jnp.maximum(count, 1.0)), NEG).astype(F32)
        return jnp.where(dist[:1] >= 0, 0.0, NEG).astype(F32), held, dist.astype(F32) * LOG2E

    slopes = jnp.asarray([2.0 ** (-8.0 * (i + 1) / N_HEADS) for i in range(N_HEADS)], F32)
    slopes = jnp.broadcast_to(slopes[:, None, None], (N_HEADS, 1, LANES))
    (causal, held, far), (causal_t, held_t, far_t) = of(tile + pos[None, :, None] - pos[None, None, :]), of(tile + pos[None, None, :] - pos[None, :, None])
    return ((causal,), (causal_t,)), ((held, far, slopes), (held_t, far_t, slopes))


def _pad_heads(w, width):
    kdim, n = w.shape[0], w.shape[1] // width
    return jnp.pad(w.reshape(kdim, n, width), ((0, 0), (0, 0), (0, LANES - width))).reshape(kdim, n * LANES)


def _unpad_heads(w, width):
    kdim, n = w.shape[0], w.shape[1] // LANES
    return w.reshape(kdim, n, LANES)[:, :, :width].reshape(kdim, n * width)


def _pad_head_rows(w, width):
    n, kdim = w.shape[0] // width, w.shape[1]
    return jnp.pad(w.reshape(n, width, kdim), ((0, 0), (0, LANES - width), (0, 0))).reshape(n * LANES, kdim)


def _unpad_head_rows(w, width):
    n, kdim = w.shape[0] // LANES, w.shape[1]
    return w.reshape(n, LANES, kdim)[:, :width].reshape(n * width, kdim)


def _pad_w_in_t(wt):
    n_qkv, pair = 3 * N_HEADS * DIL_DIM, 2 * DIL_DIM
    zeros = lambda n: jnp.zeros((n, wt.shape[1]), wt.dtype)
    main = jnp.concatenate([wt[:LORA_W], zeros(KR_LANE), wt[LORA_W:LORA_W + ROPE], zeros(P_GATE - P_KR - KR_LANE - ROPE),
                            wt[LORA_W + ROPE + n_qkv:]], axis=0)
    qkv = wt[LORA_W + ROPE:LORA_W + ROPE + n_qkv].reshape(3, N_HEADS // 2, pair, wt.shape[1]).transpose(1, 0, 2, 3)
    dil = jnp.pad(qkv, ((0, 0), (0, 1), (0, 0), (0, 0))).reshape(P_DIL, wt.shape[1])
    return main, dil


def _unpad_w_in_t(gt):
    qkv = gt[P_HALF:].reshape(N_HEADS // 2, 4, 2 * DIL_DIM, gt.shape[1])[:, :3].transpose(1, 0, 2, 3).reshape(3 * N_HEADS * DIL_DIM, gt.shape[1])
    return jnp.concatenate([gt[P_LORA:P_KR], gt[P_KR + KR_LANE:P_KR + KR_LANE + ROPE], qkv, gt[P_GATE:P_HALF]], axis=0)


def _split_ukv(w):
    w3 = w.reshape(w.shape[0], N_HEADS, NOPE + V_DIM)
    return (_pad_heads(w3[:, :, :NOPE].reshape(w.shape[0], -1), NOPE),
            _pad_heads(w3[:, :, NOPE:].reshape(w.shape[0], -1), V_DIM))


def _merge_ukv(g_k, g_v):
    kdim = g_k.shape[0]
    k3 = _unpad_heads(g_k, NOPE).reshape(kdim, N_HEADS, NOPE)
    v3 = _unpad_heads(g_v, V_DIM).reshape(kdim, N_HEADS, V_DIM)
    return jnp.concatenate([k3, v3], axis=2).reshape(kdim, N_HEADS * (NOPE + V_DIM))


def _pad_rows(w, width):
    return _pad_heads(w.T, width).T


def _unpad_rows(g, width):
    return _unpad_heads(g.T, width).T


def _join_cols(w):
    return w.transpose(1, 0, 2).reshape(w.shape[1], N_CHIPS * w.shape[2])


def _split_cols(g):
    return g.reshape(g.shape[0], N_CHIPS, g.shape[1] // N_CHIPS).transpose(1, 0, 2)


def _local_step(x3, target3, wg, b_gate, g_q_a, g_kv_a, ln1_g, ln1_b, ln2_g, ln2_b, token=None, late_arrived=None, late_weights=None,
                early_grads=None, early_grads_go=None, last_grads=None, tables=None):
    w_main_t, w_dil_t = _pad_w_in_t(wg["w_in"].reshape(IN_WIDTH, D_MODEL))
    w_uq_pt = _pad_head_rows(wg["w_uq"].reshape(N_HEADS * MLA_QK, Q_LORA), MLA_QK)
    w_ukv = _join_cols(wg["w_ukv"])
    batch, seq, _ = x3.shape
    rows = batch * seq
    x = x3.reshape(rows, D_MODEL)
    target = target3.reshape(rows, D_MODEL)
    row = functools.partial(_rowwise, rows=rows, seq=seq)
    mm = _matmul

    w_uk_p, w_uv_p = _split_ukv(w_ukv)
    b0, b1 = b_gate[0:1], b_gate[1:2]
    rope_c, rope_up, rope_dn, rope_c_only = _rope_tables(seq)
    (mla_bwd_tables, mla_fwd_tables), (dil_bwd_tables, dil_fwd_tables) = _attention_tables(seq) if tables is None else tables
    scale_mla, scale_dil = MLA_QK ** -0.5, DIL_DIM ** -0.5
    lora0, kr0, gate0 = P_LORA // LORA_W, P_KR // LANES, P_GATE // D_MODEL

    proj = mm(x, w_main_t, mode="nt", name="proj", tm=1024, tn=1536, tk=1024, after=token)
    proj_d = mm(x, w_dil_t, mode="nt", name="proj_dil", tm=1024, tn=1024, tk=1024, out_dtypes=(BF16,))

    def prep(lora, gq, gkv):
        return _rms(lora[:, :Q_LORA], gq), _rms(lora[:, Q_LORA:], gkv)

    qn, kvn = row(prep, name="mla_rms", ins=[(proj, LORA_W, lora0, "row"), (g_q_a, 0, 0, "full"), (g_kv_a, 0, 0, "full")],
                  outs=[(Q_LORA, BF16), (KV_LORA, BF16)])
    q_lin = mm(qn, w_uq_pt, mode="nt", name="q_up", tm=1024, tn=1024, tk=Q_LORA)
    k_lin = mm(kvn, w_uk_p, mode="nn", name="k_up", tm=1024, tn=1024, tk=KV_LORA)
    v_a = mm(kvn, w_uv_p, mode="nn", name="v_up", tm=1024, tn=1024, tk=KV_LORA, out_dtypes=(BF16,))

    def rope_qk(ql, kl, kr, c, up, dn):
        k_rot = _rope_fwd(kr, c, up, dn)
        qs = [_rope_fwd(ql[:, h * LANES:(h + 1) * LANES], c, up, dn) for h in range(N_HEADS)]
        ks = [kl[:, h * LANES:(h + 1) * LANES] + k_rot for h in range(N_HEADS)]
        return jnp.concatenate(qs, axis=1), jnp.concatenate(ks, axis=1)

    pos = lambda tab: (tab, LANES, 0, "pos")
    q_a, k_a = row(rope_qk, name="rope_qk",
                   ins=[(q_lin, D_MODEL, 0, "row"), (k_lin, D_MODEL, 0, "row"), (proj, LANES, kr0, "row"), pos(rope_c), pos(rope_up), pos(rope_dn)],
                   outs=[(N_HEADS * LANES, BF16), (N_HEADS * LANES, BF16)])
    o_a, lse_a = _attn_fwd(q_a, 0, k_a, 0, v_a, 0, mla_fwd_tables, scale_mla, name="mla_fwd", batch=batch, seq=seq)
    arrived = None if late_arrived is None else late_arrived(o_a)
    o_b, lse_b = _attn_fwd(proj_d, 0, proj_d, 1, proj_d, 2, dil_fwd_tables, scale_dil, name="dil_fwd", batch=batch, seq=seq, sub=2, stride=4, after=arrived)
    late = wg if late_weights is None else late_weights(o_b)
    w_oa_p = _pad_rows(_join_cols(late["w_o_mla"]), V_DIM)
    w_ob = _join_cols(late["w_o_dil"])
    w_out, w_ff1, w_ff2 = late["w_out"].reshape(D_MODEL, D_MODEL), late["w_ff1"], late["w_ff2"].reshape(D_FF, D_MODEL)
    y_a = mm(o_a, w_oa_p, mode="nn", name="o_mla", tm=1024, tn=1024, tk=1024, out_dtypes=(BF16,))
    y_b = mm(o_b, w_ob, mode="nn", name="o_dil", tm=1024, tn=1024, tk=1024, out_dtypes=(BF16,))

    def gate(t0, t1, c0, c1, ya, yb):
        return (jax.nn.sigmoid(t0 + c0) * ya + jax.nn.sigmoid(t1 + c1) * yb,)

    gate_ins = [(proj, D_MODEL, gate0, "row"), (proj, D_MODEL, gate0 + 1, "row"), (b0, 0, 0, "full"), (b1, 0, 0, "full")]
    (u,) = row(gate, name="gate", ins=gate_ins + [(y_a, D_MODEL, 0, "row"), (y_b, D_MODEL, 0, "row")], outs=[(D_MODEL, BF16)])
    mixed = mm(u, w_out, mode="nn", name="mix", tm=1024, tn=1024, tk=1024)

    def ln1(xv, mv, g, b):
        r1 = ALPHA * xv + mv
        xh, _ = _ln_stats(r1)
        return r1, xh * g + b

    r1, h = row(ln1, name="ln1", ins=[(x, D_MODEL, 0, "row"), (mixed, D_MODEL, 0, "row"), (ln1_g, 0, 0, "full"), (ln1_b, 0, 0, "full")],
                outs=[(D_MODEL, F32), (D_MODEL, F32)])

    def relu2(acc):
        r = jnp.maximum(acc, 0.0)
        return acc, r * r

    a_ff, z = mm(h, w_ff1, mode="nn", name="ff1", tm=1024, tn=1024, tk=1024, out_dtypes=(BF16, BF16), epilogue=relu2, b_shards=True)
    f = mm(z, w_ff2, mode="nn", name="ff2", tm=1024, tn=1024, tk=2048)

    def ln2_loss(hv, fv, tv, g, b):
        xh, r = _ln_stats(ALPHA * hv + fv)
        err = xh * g + b - tv
        dy = err * (1.0 / D_MODEL)
        dr2, dg, db = _ln_bwd(xh, r, g, dy)
        loss = jnp.sum(_colsum(err * err), axis=1, keepdims=True) * (0.5 / D_MODEL)
        return dr2, jnp.broadcast_to(loss, (1, LANES)), dg, db

    dr2, loss_l, d_ln2_g, d_ln2_b = row(
        ln2_loss, name="ln2_loss",
        ins=[(h, D_MODEL, 0, "row"), (f, D_MODEL, 0, "row"), (target, D_MODEL, 0, "row"), (ln2_g, 0, 0, "full"), (ln2_b, 0, 0, "full")],
        outs=[(D_MODEL, F32)], sums=[LANES, D_MODEL, D_MODEL])

    d_w_ff2 = mm(z, dr2, mode="tn", name="d_w_ff2", tm=1024, tn=1024, tk=2048)
    da = mm(dr2, w_ff2, mode="nt", name="d_ff_act", tm=1024, tn=1024, tk=1024, out_dtypes=(BF16,), extras=(a_ff,),
            epilogue=lambda acc, av: (acc * (2.0 * jnp.maximum(av.astype(F32), 0.0)),))
    d_w_ff1 = mm(h, da, mode="tn", name="d_w_ff1", tm=1024, tn=1024, tk=2048, out_shards=True)
    dh = mm(da, w_ff1, mode="nt", name="d_h", tm=1024, tn=1024, tk=1024, extras=(dr2,), epilogue=lambda acc, rv: (acc + ALPHA * rv,), b_shards=True)

    def ln1_bwd(dhv, r1v, g):
        xh, r = _ln_stats(r1v)
        return _ln_bwd(xh, r, g, dhv)

    dr1, d_ln1_g, d_ln1_b = row(ln1_bwd, name="ln1_bwd", ins=[(dh, D_MODEL, 0, "row"), (r1, D_MODEL, 0, "row"), (ln1_g, 0, 0, "full")],
                                outs=[(D_MODEL, F32)], sums=[D_MODEL, D_MODEL])
    d_w_out = mm(u, dr1, mode="tn", name="d_w_out", tm=1024, tn=1024, tk=1024)
    du = mm(dr1, w_out, mode="nt", name="d_u", tm=1024, tn=1024, tk=1024, out_dtypes=(BF16,))

    def gate_bwd(t0, t1, c0, c1, ya, yb, duv):
        s0, s1 = jax.nn.sigmoid(t0 + c0), jax.nn.sigmoid(t1 + c1)
        dt0 = duv * ya * s0 * (1.0 - s0)
        dt1 = duv * yb * s1 * (1.0 - s1)
        return duv * s0, duv * s1, jnp.concatenate([dt0, dt1], axis=1), jnp.concatenate([_colsum(dt0), _colsum(dt1)], axis=1)

    dy_a, dy_b, d_gates, d_b_gate = row(
        gate_bwd, name="gate_bwd", ins=gate_ins + [(y_a, D_MODEL, 0, "row"), (y_b, D_MODEL, 0, "row"), (du, D_MODEL, 0, "row")],
        outs=[(D_MODEL, BF16), (D_MODEL, BF16), (2 * D_MODEL, BF16)], sums=[2 * D_MODEL])
    d_w_oa_p = mm(o_a, dy_a, mode="tn", name="d_w_o_mla", tm=1024, tn=1024, tk=1024)
    d_w_ob = mm(o_b, dy_b, mode="tn", name="d_w_o_dil", tm=1024, tn=1024, tk=1024)
    grads = dict(w_o_mla=_split_cols(_unpad_rows(d_w_oa_p, V_DIM)), w_o_dil=_split_cols(d_w_ob),
                 w_out=d_w_out.reshape(N_CHIPS, D_MODEL // N_CHIPS, D_MODEL), w_ff1=d_w_ff1, w_ff2=d_w_ff2.reshape(N_CHIPS, D_FF // N_CHIPS, D_MODEL))
    sent = None if early_grads is None else early_grads(grads)
    do_a = mm(dy_a, w_oa_p, mode="nt", name="d_o_mla", tm=1024, tn=1024, tk=1024, after=sent)
    do_b = mm(dy_b, w_ob, mode="nt", name="d_o_dil", tm=1024, tn=1024, tk=1024)
    dq_a, dk_a, dv_a = _attn_bwd(q_a, 0, k_a, 0, v_a, 0, o_a, do_a, lse_a, mla_bwd_tables, scale_mla,
                                 name="mla_bwd", batch=batch, seq=seq, out_dtype=F32)
    going = None if early_grads_go is None else early_grads_go(dq_a)
    d_qkv_d = _attn_bwd(proj_d, 0, proj_d, 1, proj_d, 2, o_b, do_b, lse_b, dil_bwd_tables, scale_dil,
                        name="dil_bwd", batch=batch, seq=seq, out_dtype=BF16, sub=2, stride=4, after=going)

    def mla_post(dq, dk, c, up, dn, c_only):
        dqs = [_rope_bwd(dq[:, h * LANES:(h + 1) * LANES], c, up, dn) for h in range(N_HEADS)]
        dk_sum = dk[:, :LANES]
        for h in range(1, N_HEADS):
            dk_sum = dk_sum + dk[:, h * LANES:(h + 1) * LANES]
        return jnp.concatenate(dqs, axis=1), _rope_bwd(dk_sum, c_only, up, dn)

    dq_lin, d_kr = row(mla_post, name="mla_unrope",
                       ins=[(dq_a, D_MODEL, 0, "row"), (dk_a, D_MODEL, 0, "row"), pos(rope_c), pos(rope_up), pos(rope_dn), pos(rope_c_only)],
                       outs=[(N_HEADS * LANES, BF16), (LANES, BF16)])
    d_w_uq_pt = mm(dq_lin, qn, mode="tn", name="d_w_uq", tm=1024, tn=Q_LORA, tk=1024)
    d_w_uk_p = mm(kvn, dk_a, mode="tn", name="d_w_uk", tm=KV_LORA, tn=1024, tk=1024)
    d_w_uv_p = mm(kvn, dv_a, mode="tn", name="d_w_uv", tm=KV_LORA, tn=1024, tk=1024)
    d_qn = mm(dq_lin, w_uq_pt, mode="nn", name="d_qn", tm=1024, tn=Q_LORA, tk=1024)
    d_kvn_k = mm(dk_a, w_uk_p, mode="nt", name="d_kvn_k", tm=1024, tn=KV_LORA, tk=1024)
    d_kvn = mm(dv_a, w_uv_p, mode="nt", name="d_kvn", tm=1024, tn=KV_LORA, tk=1024, extras=(d_kvn_k,), epilogue=lambda acc, e: (acc + e,))

    def rms_bwd(lora, dq, dkv, dkr, gq, gkv):
        dxq, dgq = _rms_bwd(lora[:, :Q_LORA], gq, dq)
        dxk, dgk = _rms_bwd(lora[:, Q_LORA:], gkv, dkv)
        tail = jnp.zeros((dxq.shape[0], P_GATE - P_KR - LANES), F32)
        return jnp.concatenate([dxq, dxk, dkr.astype(F32), tail], axis=1), dgq, dgk

    d_tail, d_g_q_a, d_g_kv_a = row(
        rms_bwd, name="mla_rms_bwd",
        ins=[(proj, LORA_W, lora0, "row"), (d_qn, Q_LORA, 0, "row"), (d_kvn, KV_LORA, 0, "row"), (d_kr, LANES, 0, "row"),
             (g_q_a, 0, 0, "full"), (g_kv_a, 0, 0, "full")],
        outs=[(P_GATE, BF16)], sums=[Q_LORA, KV_LORA])
    d_proj = [d_tail, d_gates, d_qkv_d]
    d_w_in_pt = mm(d_proj, x, mode="tn", name="d_w_in", tm=1024, tn=1024, tk=1024)
    grads.update(w_in=_unpad_w_in_t(d_w_in_pt).reshape(N_CHIPS, IN_WIDTH // N_CHIPS, D_MODEL),
                 w_uq=_unpad_head_rows(d_w_uq_pt, MLA_QK).reshape(N_CHIPS, N_HEADS * MLA_QK // N_CHIPS, Q_LORA),
                 w_ukv=_split_cols(_merge_ukv(d_w_uk_p, d_w_uv_p)))
    leaving = None if last_grads is None else last_grads(grads)
    grad_x = mm(d_proj, jnp.concatenate([w_main_t, w_dil_t], axis=0), mode="nn", name="d_x", tm=1024, tn=1024, tk=1024, extras=(dr1,), epilogue=lambda acc, rv: (acc + ALPHA * rv,),
                after=leaving)

    grads.update(
        b_gate=d_b_gate.reshape(2, D_MODEL), g_q_a=d_g_q_a, g_kv_a=d_g_kv_a, ln1_g=d_ln1_g, ln1_b=d_ln1_b, ln2_g=d_ln2_g, ln2_b=d_ln2_b)
    return loss_l, grad_x.reshape(batch, seq, D_MODEL), grads


BIG = ("w_in", "w_uq", "w_ukv", "w_o_mla", "w_o_dil", "w_out", "w_ff1", "w_ff2")
SMALL = (("b_gate", 2 * D_MODEL), ("g_q_a", Q_LORA), ("g_kv_a", KV_LORA), ("ln1_g", D_MODEL), ("ln1_b", D_MODEL),
         ("ln2_g", D_MODEL), ("ln2_b", D_MODEL))
TRANSPOSED = ("w_in", "w_uq")
D2D_PIECES = (4, 2, 1)
ANY = pl.BlockSpec(memory_space=pl.ANY)
SIDE_EFFECTS = pltpu.CompilerParams(has_side_effects=True)


def _place():
    x, y, c = lax.axis_index("x"), lax.axis_index("y"), lax.axis_index("c")
    return x, y, c, ((1 - x, y), (x, 1 - y), (1 - x, 1 - y))


def _half_axis(shape):
    return 0 if shape[0] % 32 == 0 else 1


def _half_shape(shape):
    return (shape[0] // 2, shape[1]) if _half_axis(shape) == 0 else (shape[0], shape[1] // 2)


def _window(ref, lead, shape, which=None, pieces=False):
    axis = _half_axis(shape)
    size = shape[axis] if which is None else shape[axis] // 2
    base = 0 if which is None else which * size
    tile = (16, LANES)[axis]
    count = next(c for c in D2D_PIECES if size % (tile * c) == 0) if pieces else 1
    step = size // count
    spans = [pl.ds(pl.multiple_of(base + i * step, tile), step) for i in range(count)]
    refs = [ref.at[(*lead, s)] if axis == 0 else ref.at[(*lead, slice(None), s)] for s in spans]
    return refs if pieces else refs[0]


def _remote(src, dst, send, recv, to):
    return pltpu.make_async_remote_copy(src_ref=src, dst_ref=dst, send_sem=send, recv_sem=recv, device_id=to, device_id_type=MESH)


def _pair_split(grads, name):
    n = len(grads)

    def body(*refs):
        srcs, outs, (send, recv) = refs[:n], refs[n:2 * n], refs[2 * n:]
        x, y, c, _ = _place()
        for t in range(n):
            for s in range(N_CHIPS):
                _remote(_window(srcs[t], (s,), grads[t].shape[1:], 1 - c), outs[t].at[s], send.at[t], recv.at[t], (x, y, 1 - c)).start()
        for t in range(n):
            _remote(_window(srcs[t], (slice(None),), grads[t].shape[1:], 1 - c), outs[t], send.at[t], recv.at[t], (x, y, 1 - c)).wait()

    return pl.pallas_call(
        body, name=name, in_specs=[ANY] * n, out_specs=[ANY] * n,
        out_shape=[jax.ShapeDtypeStruct((N_CHIPS,) + _half_shape(g.shape[1:]), g.dtype) for g in grads],
        scratch_shapes=[pltpu.SemaphoreType.DMA((n,)), pltpu.SemaphoreType.DMA((n,))],
        compiler_params=SIDE_EFFECTS,
    )(*grads)


HBM = pl.BlockSpec(memory_space=pltpu.HBM)
SEM = pl.BlockSpec(memory_space=pltpu.SEMAPHORE)
SPLIT = pltpu.CompilerParams(has_side_effects=pltpu.SideEffectType.DATAFLOW_SIDE_EFFECTING)


def _in_hbm(a):
    return pltpu.with_memory_space_constraint(a, pltpu.HBM)


def _split_copies(kind, srcs, lands):
    x, y, c, chips = _place()
    out = []
    for t in range(len(srcs)):
        if kind == "pair":
            out += [(t, s % 3, _window(srcs[t], (s,), srcs[t].shape[1:], 1 - c), lands[t].at[s], (x, y, 1 - c)) for s in range(N_CHIPS)]
            continue
        if kind == "join":
            out += [(t, 0, a, b, (x, y, 1 - c)) for a, b in zip(_window(srcs[t], (), srcs[t].shape, None, True), _window(lands[t], (), srcs[t].shape, None, True))]
            continue
        if kind == "forward":
            shape, sibling = srcs[t].shape, (x, y, 1 - c)
            out += [(t, 0, a, b, sibling) for a, b in zip(_window(srcs[t], (), shape, None, True), _window(lands[t], (2 * x + y,), shape, None, True))]
            out += [(t, j, a, a, sibling) for j, (cx, cy) in enumerate(chips) for a in _window(lands[t], (2 * cx + cy,), shape, c, True)]
            continue
        for j, (cx, cy) in enumerate(chips):
            if kind == "gather":
                shape = srcs[t].shape
                out.append((t, j, _window(srcs[t], (), shape, c), _window(lands[t], (2 * x + y,), shape, c), (cx, cy, c)))
            else:
                out.append((t, j, srcs[t].at[2 * cx + cy], lands[t].at[j], (cx, cy, c)))
    return out


def _split_start(kind, srcs, land_shapes, name, lands=None, after=None):
    n = len(srcs)

    def body(*refs):
        src_refs, land_refs, sems, token = refs[:n], refs[n:2 * n], refs[-7 - 2 * n:-1 - 2 * n], refs[-1]
        for t, j, s, d, to in _split_copies(kind, src_refs, land_refs):
            _remote(s, d, sems[j], sems[3 + j], to).start()
        token[...] = jnp.zeros_like(token)

    lands = [_in_hbm(lax.empty(s.shape, s.dtype)) for s in land_shapes] if lands is None else list(lands)
    thru = [pltpu.HBM(a.shape, a.dtype) for a in list(srcs) + lands]
    res = pl.pallas_call(
        body, name=name,
        out_shape=(*[pltpu.SemaphoreType.DMA(())] * 6, *thru, jax.ShapeDtypeStruct((8, LANES), F32)),
        in_specs=[HBM] * (2 * n) + [ANY] * (after is not None), out_specs=(*[SEM] * 6, *[HBM] * (2 * n), pl.BlockSpec(memory_space=pltpu.VMEM)),
        input_output_aliases={i: 6 + i for i in range(2 * n)}, compiler_params=SPLIT,
    )(*[_in_hbm(s) for s in srcs], *lands, *([after] if after is not None else []))
    return res[:6], res[6:6 + n], res[6 + n:6 + 2 * n], res[-1]


def _split_wait(kind, sems, srcs, lands, after, name):
    n = len(srcs)

    def body(*refs):
        src_refs, land_refs, sem_refs = refs[:n], refs[n:2 * n], refs[2 * n:2 * n + 6]
        for t, j, s, d, to in _split_copies(kind, src_refs, land_refs):
            cp = _remote(s, d, sem_refs[j], sem_refs[3 + j], to)
            cp.wait_send()
            cp.wait_recv()

    res = pl.pallas_call(
        body, name=name, out_shape=[pltpu.HBM(a.shape, a.dtype) for a in list(srcs) + list(lands)],
        in_specs=[HBM] * (2 * n) + [SEM] * 6 + [ANY], out_specs=[HBM] * (2 * n),
        input_output_aliases={i: i for i in range(2 * n)}, compiler_params=SPLIT,
    )(*srcs, *lands, *sems, after)
    return res[:n], res[n:]


def _sum_all_devices(vec, name):
    n_rows = vec.shape[0]

    def body(v_ref, out_ref, buf, send, recv):
        x, y, c, _ = _place()
        me = 4 * x + 2 * y + c
        buf[me] = v_ref[...]
        flips = [(a, b, d) for a in (0, 1) for b in (0, 1) for d in (0, 1)][1:]
        copies = []
        for r, (a, b, d) in enumerate(flips):
            px, py, pc = (1 - x if a else x), (1 - y if b else y), (1 - c if d else c)
            copies.append(pltpu.make_async_remote_copy(src_ref=v_ref, dst_ref=buf.at[me], send_sem=send.at[r], recv_sem=recv.at[r],
                                                       device_id=(px, py, pc), device_id_type=MESH))
            copies[-1].start()
        for r, (a, b, d) in enumerate(flips):
            px, py, pc = (1 - x if a else x), (1 - y if b else y), (1 - c if d else c)
            pltpu.make_async_remote_copy(src_ref=v_ref, dst_ref=buf.at[4 * px + 2 * py + pc], send_sem=send.at[r], recv_sem=recv.at[r],
                                         device_id=(px, py, pc), device_id_type=MESH).wait_recv()
        for cp in copies:
            cp.wait_send()
        total = buf[0]
        for k in range(1, N_DEV):
            total = total + buf[k]
        out_ref[...] = total

    vmem = pl.BlockSpec(memory_space=pltpu.VMEM)
    return pl.pallas_call(
        body, name=name, in_specs=[vmem], out_specs=vmem, out_shape=jax.ShapeDtypeStruct(vec.shape, F32),
        scratch_shapes=[pltpu.VMEM((N_DEV, n_rows, LANES), F32), pltpu.SemaphoreType.DMA((N_DEV - 1,)), pltpu.SemaphoreType.DMA((N_DEV - 1,))],
        compiler_params=pltpu.CompilerParams(has_side_effects=True),
    )(vec)


def _half_tile(half, width):
    t = half
    while t * width * 4 > (2 << 20) and t % 32 == 0:
        t //= 2
    return t


def _pair_add(g, theirs, core, name):
    _, half, width = theirs.shape
    t = _half_tile(half, width)
    n = half // t

    def body(c_ref, a_ref, b_ref, o_ref):
        o_ref[...] = (a_ref[...] + b_ref[...]).astype(BF16)

    tile = pl.BlockSpec((1, t, width), lambda j, i, c_ref: (j, i, 0))
    if _half_axis(g.shape[1:]) == 0:
        mine = pl.BlockSpec((1, t, width), lambda j, i, c_ref: (j, c_ref[0] * n + i, 0))
    else:
        mine = pl.BlockSpec((1, t, width), lambda j, i, c_ref: (j, i, c_ref[0]))
    return pl.pallas_call(
        body, name=name,
        grid_spec=pltpu.PrefetchScalarGridSpec(num_scalar_prefetch=1, grid=(N_CHIPS, n), in_specs=[mine, tile], out_specs=tile),
        out_shape=jax.ShapeDtypeStruct(theirs.shape, BF16), compiler_params=_params(("parallel", "parallel")),
    )(core, g, theirs)


def _chip_sum(part, others, chip, name, after=None):
    _, half, width = part.shape
    t = _half_tile(half, width)

    def body(s_ref, mine, p0, p1, p2, *rest):
        o_ref = rest[-1]
        o_ref[...] = ((mine[0].astype(F32) + p0[0].astype(F32)) + p1[0].astype(F32)) + p2[0].astype(F32)

    return pl.pallas_call(
        body, name=name,
        grid_spec=pltpu.PrefetchScalarGridSpec(
            num_scalar_prefetch=1, grid=(half // t,),
            in_specs=[pl.BlockSpec((1, t, width), lambda i, s_ref: (s_ref[0], i, 0))]
            + [pl.BlockSpec((1, t, width), lambda i, s_ref, j=j: (j, i, 0)) for j in range(3)] + [pl.BlockSpec(memory_space=pl.ANY)] * (after is not None),
            out_specs=pl.BlockSpec((t, width), lambda i, s_ref: (i, 0))),
        out_shape=jax.ShapeDtypeStruct((half, width), F32), compiler_params=_params(("parallel",)),
    )(chip, part, others, others, others, *([after] if after is not None else []))


EARLY = ("w_in", "w_uq", "w_ukv")
LATE = ("w_o_mla", "w_o_dil", "w_out", "w_ff1", "w_ff2")


def _chip_partials(grads, names, core, tag):
    gs = [grads[n] for n in names]
    theirs = _pair_split(gs, "pair_split_" + tag)
    return [_pair_add(g, th, core, "pair_add_" + n) for g, th, n in zip(gs, theirs, names)]


def _sum_small(vals):
    n_in = len(vals)
    n_rows = sum(a.shape[0] * a.shape[1] // LANES for a in vals)
    pad_rows = -(-n_rows // 8) * 8

    def chunks(refs):
        return [(ref, a, j) for ref in refs for a in range(ref.shape[0]) for j in range(ref.shape[1] // LANES)]

    def body(*refs):
        ins, outs, (buf, send, recv) = refs[:n_in], refs[n_in:2 * n_in], refs[2 * n_in:]
        x, y, c, _ = _place()
        me = 4 * x + 2 * y + c
        for r, (ref, a, j) in enumerate(chunks(ins)):
            buf[me, r:r + 1, :] = ref[a:a + 1, j * LANES:(j + 1) * LANES]
        if pad_rows > n_rows:
            buf[me, n_rows:pad_rows, :] = jnp.zeros((pad_rows - n_rows, LANES), F32)
        flips = [(a, b, d) for a in (0, 1) for b in (0, 1) for d in (0, 1)][1:]
        peers = [((1 - x if a else x), (1 - y if b else y), (1 - c if d else c)) for a, b, d in flips]
        copies = [_remote(buf.at[me], buf.at[me], send.at[r], recv.at[r], peer) for r, peer in enumerate(peers)]
        for cp in copies:
            cp.start()
        for r, (px, py, pc) in enumerate(peers):
            _remote(buf.at[me], buf.at[4 * px + 2 * py + pc], send.at[r], recv.at[r], (px, py, pc)).wait_recv()
        for cp in copies:
            cp.wait_send()
        total = buf[0]
        for k in range(1, N_DEV):
            total = total + buf[k]
        for r, (ref, a, j) in enumerate(chunks(outs)):
            ref[a:a + 1, j * LANES:(j + 1) * LANES] = total[r:r + 1, :]

    vmem = pl.BlockSpec(memory_space=pltpu.VMEM)
    return pl.pallas_call(
        body, name="sum_small", in_specs=[vmem] * n_in, out_specs=[vmem] * n_in,
        out_shape=[jax.ShapeDtypeStruct(a.shape, F32) for a in vals],
        scratch_shapes=[pltpu.VMEM((N_DEV, pad_rows, LANES), F32), pltpu.SemaphoreType.DMA((N_DEV - 1,)), pltpu.SemaphoreType.DMA((N_DEV - 1,))],
        compiler_params=SIDE_EFFECTS,
    )(*vals)


def _adam_math(w, g, m, v):
    nm = B1 * m + (1.0 - B1) * g
    nv = B2 * v + (1.0 - B2) * (g * g)
    m_hat = nm / (1.0 - B1 ** ADAM_STEP)
    v_hat = nv / (1.0 - B2 ** ADAM_STEP)
    return -LR * (m_hat / (jnp.sqrt(v_hat) + ADAM_EPS) + WD * w), nm, nv


def _adamw_big(w, mine, theirs, m, v, core, name, side_by_side=False):
    rows, width = w.shape
    if side_by_side:
        t = next(c for c in (152, 96, 64, 32, 16, 8) if rows % c == 0)
        hb = None
        half_spec = pl.BlockSpec((t, width // 2), lambda i, c_ref: (i, 0))
    else:
        t = next(c for c in (256, 128, 64, 32, 16, 8) if (rows // 2) % c == 0)
        hb = rows // 2 // t
        half_spec = pl.BlockSpec((t, width), lambda i, c_ref: (i % hb, 0))

    def body(c_ref, w_ref, a_ref, b_ref, m_ref, v_ref, g_ref, d_ref, nm_ref, nv_ref):
        south = c_ref[0] == 0
        if side_by_side:
            g = jnp.where(south, jnp.concatenate([a_ref[...], b_ref[...]], axis=1), jnp.concatenate([b_ref[...], a_ref[...]], axis=1))
        else:
            g = jnp.where((pl.program_id(0) < hb) == south, a_ref[...], b_ref[...])
        g_ref[...] = g
        d_ref[...], nm_ref[...], nv_ref[...] = _adam_math(w_ref[...], g, m_ref[...], v_ref[...])

    spec = pl.BlockSpec((t, width), lambda i, c_ref: (i, 0))
    return pl.pallas_call(
        body, name=name,
        grid_spec=pltpu.PrefetchScalarGridSpec(num_scalar_prefetch=1, grid=(rows // t,),
                                               in_specs=[spec, half_spec, half_spec, spec, spec], out_specs=[spec] * 4),
        out_shape=[jax.ShapeDtypeStruct(w.shape, F32)] * 4, compiler_params=_params(("parallel",)),
    )(core, w, mine, theirs, m, v)


def _adamw_small(ws, gs, ms, vs):
    n = len(ws)

    def body(*refs):
        for t in range(n):
            w_ref, g_ref, m_ref, v_ref = (refs[k * n + t] for k in range(4))
            d, nm, nv = _adam_math(w_ref[...], g_ref[...], m_ref[...], v_ref[...])
            refs[4 * n + t][...] = d
            refs[5 * n + t][...] = nm
            refs[6 * n + t][...] = nv

    vmem = pl.BlockSpec(memory_space=pltpu.VMEM)
    res = pl.pallas_call(body, name="adamw_small", in_specs=[vmem] * (4 * n), out_specs=[vmem] * (3 * n),
                         out_shape=[jax.ShapeDtypeStruct(a.shape, F32) for a in ws] * 3)(*ws, *gs, *ms, *vs)
    return res[:n], res[n:2 * n], res[2 * n:]


def kernel(x, w_in, b_gate, g_q_a, w_uq, g_kv_a, w_ukv, w_o_mla, w_o_dil, w_out, ln1_g, ln1_b, w_ff1, w_ff2, ln2_g, ln2_b, loss_target, m_w_in, m_b_gate, m_g_q_a, m_w_uq, m_g_kv_a, m_w_ukv, m_w_o_mla, m_w_o_dil, m_w_out, m_ln1_g, m_ln1_b, m_w_ff1, m_w_ff2, m_ln2_g, m_ln2_b, v_w_in, v_b_gate, v_g_q_a, v_w_uq, v_g_kv_a, v_w_ukv, v_w_o_mla, v_w_o_dil, v_w_out, v_ln1_g, v_ln1_b, v_w_ff1, v_w_ff2, v_ln2_g, v_ln2_b):
    order = ("w_in", "b_gate", "g_q_a", "w_uq", "g_kv_a", "w_ukv", "w_o_mla", "w_o_dil", "w_out", "ln1_g", "ln1_b", "w_ff1", "w_ff2", "ln2_g", "ln2_b")
    w = dict(w_in=w_in, b_gate=b_gate, g_q_a=g_q_a, w_uq=w_uq, g_kv_a=g_kv_a, w_ukv=w_ukv, w_o_mla=w_o_mla, w_o_dil=w_o_dil, w_out=w_out,
             ln1_g=ln1_g, ln1_b=ln1_b, w_ff1=w_ff1, w_ff2=w_ff2, ln2_g=ln2_g, ln2_b=ln2_b)
    m = dict(w_in=m_w_in, b_gate=m_b_gate, g_q_a=m_g_q_a, w_uq=m_w_uq, g_kv_a=m_g_kv_a, w_ukv=m_w_ukv, w_o_mla=m_w_o_mla, w_o_dil=m_w_o_dil,
             w_out=m_w_out, ln1_g=m_ln1_g, ln1_b=m_ln1_b, w_ff1=m_w_ff1, w_ff2=m_w_ff2, ln2_g=m_ln2_g, ln2_b=m_ln2_b)
    v = dict(w_in=v_w_in, b_gate=v_b_gate, g_q_a=v_g_q_a, w_uq=v_w_uq, g_kv_a=v_g_kv_a, w_ukv=v_w_ukv, w_o_mla=v_w_o_mla, w_o_dil=v_w_o_dil,
             w_out=v_w_out, ln1_g=v_ln1_g, ln1_b=v_ln1_b, w_ff1=v_w_ff1, w_ff2=v_w_ff2, ln2_g=v_ln2_g, ln2_b=v_ln2_b)
    chip = 2 * lax.axis_index("x") + lax.axis_index("y")
    south = (lax.axis_index("c") == 0).astype(F32)
    gate_w = D_MODEL // N_CHIPS

    core = lax.axis_index("c").astype(jnp.int32).reshape(1)
    turn = lambda n, a: a.T if n in TRANSPOSED else a
    shards = {n: turn(n, w[n][0]).astype(BF16) for n in BIG}
    early_shards, late_shards = [shards[n] for n in EARLY], [shards[n] for n in LATE]
    gathered = lambda group: [jax.ShapeDtypeStruct((N_CHIPS,) + s.shape, BF16) for s in group]
    e_sems, e_srcs, e_lands, e_token = _split_start("gather", early_shards, gathered(early_shards), "gather_early_start")
    g_sems, g_srcs, g_lands, g_token = _split_start("gather", late_shards, gathered(late_shards), "gather_late_start", after=e_token)
    tables = _attention_tables(x.shape[1])
    e_srcs, e_lands = _split_wait("gather", e_sems, e_srcs, e_lands, tables[1][0][1], "gather_early_wait")
    e_forward = _split_start("forward", e_srcs, None, "gather_early_forward_start", lands=e_lands)
    first = dict(zip(EARLY, _split_wait("forward", *e_forward[:3], e_forward[-1], "gather_early_forward_wait")[1]))
    b_mine = lax.dynamic_update_slice(jnp.zeros((2, D_MODEL), F32), b_gate[0] * south, (0, chip * gate_w))
    b_full = _sum_all_devices(b_mine.reshape(-1, LANES), "gather_b_gate").reshape(2, D_MODEL)

    sent = {}

    def late_arrived(after):
        srcs, lands = _split_wait("gather", g_sems, g_srcs, g_lands, after, "gather_late_wait")
        sent["forward"] = _split_start("forward", srcs, None, "gather_late_forward_start", lands=lands)
        return sent["forward"][-1]

    def late_weights(after):
        return dict(zip(LATE, _split_wait("forward", *sent["forward"][:3], after, "gather_late_forward_wait")[1]))

    exchange_shapes = lambda parts: [jax.ShapeDtypeStruct((3,) + p.shape[1:], BF16) for p in parts]

    def early_grads(grads_late):
        gs = [grads_late[n] for n in LATE]
        shapes = [jax.ShapeDtypeStruct((N_CHIPS,) + _half_shape(g.shape[1:]), F32) for g in gs]
        sent["pair"] = _split_start("pair", gs, shapes, "pair_split_late_start")
        return sent["pair"][-1]

    def early_grads_go(after):
        gs, theirs = _split_wait("pair", *sent["pair"][:3], after, "pair_split_late_wait")
        parts = [_pair_add(g, th, core, "pair_add_" + n) for g, th, n in zip(gs, theirs, LATE)]
        sent["late"] = _split_start("scatter", parts, exchange_shapes(parts), "exchange_late_start")
        return sent["late"][-1]

    def last_grads(grads_early):
        parts = _chip_partials(grads_early, EARLY, core, "early")
        sent["early"] = _split_start("scatter", parts, exchange_shapes(parts), "exchange_early_start")
        return sent["early"][-1]

    loss_part, grad_x, grads = _local_step(x, loss_target, first, b_full, g_q_a, g_kv_a, ln1_g, ln1_b, ln2_g, ln2_b, token=g_token,
                                           late_arrived=late_arrived, late_weights=late_weights, early_grads=early_grads, early_grads_go=early_grads_go,
                                           last_grads=last_grads, tables=tables)

    g_out, delta, new_m, new_v = {}, {}, {}, {}
    chip1 = chip.astype(jnp.int32).reshape(1)

    def sum_and_send(names, parts, others, tag):
        totals = [_chip_sum(p, o, chip1, "chip_sum_" + n) for n, p, o in zip(names, parts, others)]
        return _split_start("join", totals, [jax.ShapeDtypeStruct(t.shape, F32) for t in totals], "pair_join_" + tag + "_start")

    def adam(names, joined, after, tag):
        totals, halves = _split_wait("join", *joined[:3], after, "pair_join_" + tag + "_wait")
        for n, mine, theirs in zip(names, totals, halves):
            res = _adamw_big(turn(n, w[n][0]), mine, theirs, turn(n, m[n][0]), turn(n, v[n][0]), core, "adamw_" + n,
                             side_by_side=mine.shape[0] == shards[n].shape[0])
            g_out[n], delta[n], new_m[n], new_v[n] = (turn(n, r) for r in res)

    late_joined = sum_and_send(LATE, *_split_wait("scatter", *sent["late"][:3], grad_x, "exchange_late_wait"), "late")
    early_joined = sum_and_send(EARLY, *_split_wait("scatter", *sent["early"][:3], late_joined[-1], "exchange_early_wait"), "early")
    small_names = [name for name, _ in SMALL]
    sums = _sum_small([grads[name] for name in small_names] + [loss_part])
    loss = sums[-1][0, 0]
    g_small = dict(zip(small_names, sums))
    g_small["b_gate"] = lax.dynamic_slice(g_small["b_gate"], (0, chip * gate_w), (2, gate_w))
    flat = lambda a: a.reshape(-1, a.shape[-1])
    res = _adamw_small(*[[flat(d[name]) for name in small_names] for d in (w, g_small, m, v)])
    g_out.update(g_small)
    for d, r in zip((delta, new_m, new_v), res):
        d.update(zip(small_names, r))
    adam(LATE, late_joined, res[0][0], "late")
    adam(EARLY, early_joined, delta[LATE[-1]], "early")

    lead = lambda d: [d[name].reshape(w[name].shape) for name in order]
    return (loss, grad_x, *lead(g_out), *lead(delta), *lead(new_m), *lead(new_v))
```

```python
import functools
import math

import jax
import jax.numpy as jnp
from jax import lax
from jax.experimental import pallas as pl
from jax.experimental.pallas import tpu as pltpu

F32 = jnp.float32
BF16 = jnp.bfloat16
MESH = pl.DeviceIdType.MESH

D_MODEL = 1024
N_HEADS = 8
LANES = 128
NOPE, ROPE, V_DIM = 64, 32, 64
MLA_QK = NOPE + ROPE
Q_LORA, KV_LORA = 384, 256
DIL_DIM = 64
DIL_PATTERNS = ((128, 1), (512, 4), (2048, 16))
D_FF = 4096
N_CHIPS = 4
N_DEV = 8
IN_WIDTH = 4256
LN_EPS, RMS_EPS = 1e-5, 1e-6
NEG = -1e30
LOG2E, LN2 = 1.4426950408889634, 0.6931471805599453
ALPHA = 2.0 ** 0.25
ROPE_THETA = 10000.0
LR, B1, B2, ADAM_EPS, WD, ADAM_STEP = 0.001, 0.9, 0.999, 1e-8, 0.01, 10

P_LORA, P_KR, P_GATE, P_HALF = 0, 640, 1024, 3072
DIL_GROUP = 4 * LANES
P_DIL = N_HEADS // 2 * DIL_GROUP
LORA_W = Q_LORA + KV_LORA
KR_LANE = NOPE

ATT_T = 512
ROW_T = 512
VMEM_LIMIT = 56 * 1024 * 1024

NN = (((1,), (0,)), ((), ()))
NT = (((1,), (1,)), ((), ()))
TN = (((0,), (0,)), ((), ()))


def _params(sem=None, **kw):
    return pltpu.CompilerParams(dimension_semantics=sem, vmem_limit_bytes=VMEM_LIMIT, **kw)


def _matmul(a, b, *, mode, name, tm, tn, tk, out_dtypes=(F32,), extras=(), epilogue=None, b_shards=False, out_shards=False, after=None):
    pieces = list(a) if isinstance(a, (list, tuple)) else [a]
    n_pc = len(pieces)
    a_shape = (pieces[0].shape[0], sum(p.shape[1] for p in pieces))
    if b_shards:
        n_sh, rows_b, cols_b = b.shape
        b_shape = (rows_b, n_sh * cols_b)
    else:
        b_shape = b.shape
    if mode == "nn":
        (m, k), (k2, n) = a_shape, b_shape
    elif mode == "nt":
        (m, k), (n, k2) = a_shape, b_shape
    else:
        (k, m), (k2, n) = a_shape, b_shape
    assert k == k2, (a_shape, b.shape, mode)
    tm, tn, tk = min(tm, m), min(tn, n), min(tk, k)
    assert m % tm == 0 and n % tn == 0 and k % tk == 0, (name, m, n, k, tm, tn, tk)
    nk = k // tk
    n_ex, n_out = len(extras), len(out_dtypes)
    n_in = n_pc + 1 + n_ex + (after is not None)
    dims = {"nn": NN, "nt": NT, "tn": TN}[mode]
    col_tile = tm if mode == "tn" else tk
    blocks = [p.shape[1] // col_tile for p in pieces]
    firsts = [sum(blocks[:p]) for p in range(n_pc)]
    assert all(p.shape[1] % col_tile == 0 for p in pieces), (name, col_tile)

    def body(*refs):
        a_refs, b_ref = refs[:n_pc], refs[n_pc]
        ex_refs = refs[n_pc + 1:n_pc + 1 + n_ex]
        out_refs = refs[n_in:n_in + n_out]

        def finish(acc):
            outs = epilogue(acc, *[r[...] for r in ex_refs]) if epilogue is not None else (acc,)
            for r, o in zip(out_refs, outs):
                r[...] = o.astype(r.dtype)

        kk = pl.program_id(2)

        def step(a_ref):
            part = lax.dot_general(a_ref[...].astype(BF16), b_ref[...].astype(BF16), dims, preferred_element_type=F32)
            if nk == 1:
                finish(part)
                return
            acc_ref = refs[-1]

            @pl.when(kk == 0)
            def _():
                acc_ref[...] = part

            @pl.when(kk > 0)
            def _():
                acc_ref[...] += part

            @pl.when(kk == nk - 1)
            def _():
                finish(acc_ref[...])

        if n_pc == 1:
            step(a_refs[0])
        else:
            at = pl.program_id(0) if mode == "tn" else kk
            for p in range(n_pc):
                pl.when(jnp.logical_and(at >= firsts[p], at < firsts[p] + blocks[p]))(functools.partial(step, a_refs[p]))

    def a_spec_of(p):
        if n_pc == 1:
            return pl.BlockSpec((tk, tm), lambda i, j, kk: (kk, i)) if mode == "tn" else pl.BlockSpec((tm, tk), lambda i, j, kk: (i, kk))
        col = lambda at: jnp.clip(at - firsts[p], 0, blocks[p] - 1)
        mine = lambda at: jnp.logical_and(at >= firsts[p], at < firsts[p] + blocks[p])
        if mode == "tn":
            return pl.BlockSpec((tk, tm), lambda i, j, kk: (jnp.where(mine(i), kk, 0), col(i)))
        return pl.BlockSpec((tm, tk), lambda i, j, kk: (i, col(kk)))

    b_spec = {"nn": pl.BlockSpec((tk, tn), lambda i, j, kk: (kk, j)),
              "nt": pl.BlockSpec((tn, tk), lambda i, j, kk: (j, kk)),
              "tn": pl.BlockSpec((tk, tn), lambda i, j, kk: (kk, j))}[mode]
    tile = pl.BlockSpec((tm, tn), lambda i, j, kk: (i, j))
    out_spec, out_dims = tile, (m, n)
    if b_shards and mode == "nn":
        per = cols_b // tn
        b_spec = pl.BlockSpec((None, tk, tn), lambda i, j, kk: (j // per, kk, j % per))
    elif b_shards:
        assert mode == "nt"
        per = cols_b // tk
        b_spec = pl.BlockSpec((None, tn, tk), lambda i, j, kk: (kk // per, j, kk % per))
    if out_shards:
        assert not extras and epilogue is None
        per_out = n // N_CHIPS // tn
        out_spec = pl.BlockSpec((None, tm, tn), lambda i, j, kk: (j // per_out, i, j % per_out))
        out_dims = (N_CHIPS, m, n // N_CHIPS)
    outs = pl.pallas_call(
        body, name=name,
        grid=(m // tm, n // tn, nk),
        in_specs=[a_spec_of(p) for p in range(n_pc)] + [b_spec] + [tile] * n_ex + [pl.BlockSpec(memory_space=pl.ANY)] * (after is not None),
        out_specs=[out_spec] * n_out,
        out_shape=[jax.ShapeDtypeStruct(out_dims, dt) for dt in out_dtypes],
        scratch_shapes=[pltpu.VMEM((tm, tn), F32)] if nk > 1 else [],
        compiler_params=_params(("parallel", "parallel", "arbitrary")),
    )(*pieces, b, *extras, *([after] if after is not None else []))
    return outs[0] if n_out == 1 else outs


def _rowwise(fn, *, name, rows, seq, ins, outs, sums=()):
    tm = min(ROW_T, seq)
    n_pos = seq // tm
    n_in, n_out, n_sum = len(ins), len(outs), len(sums)

    def body(*refs):
        vals = fn(*[r[...] for r in refs[:n_in]])
        for r, v in zip(refs[n_in:n_in + n_out], vals[:n_out]):
            r[...] = v.astype(r.dtype)
        first = pl.program_id(0) == 0
        for r, v in zip(refs[n_in + n_out:], vals[n_out:]):
            @pl.when(first)
            def _(r=r, v=v):
                r[...] = v

            @pl.when(jnp.logical_not(first))
            def _(r=r, v=v):
                r[...] += v

    def spec(arr, width, col, kind):
        if kind == "row":
            return pl.BlockSpec((tm, width), lambda i, col=col: (i, col))
        if kind == "pos":
            return pl.BlockSpec((tm, width), lambda i, col=col: (i % n_pos, col))
        return pl.BlockSpec(arr.shape, lambda i: (0,) * arr.ndim)

    res = pl.pallas_call(
        body, name=name,
        grid=(rows // tm,),
        in_specs=[spec(*t) for t in ins],
        out_specs=[pl.BlockSpec((tm, w), lambda i: (i, 0)) for w, _ in outs]
        + [pl.BlockSpec((1, w), lambda i: (0, 0)) for w in sums],
        out_shape=[jax.ShapeDtypeStruct((rows, w), dt) for w, dt in outs]
        + [jax.ShapeDtypeStruct((1, w), F32) for w in sums],
        compiler_params=_params(("arbitrary",)),
    )(*[t[0] for t in ins])
    return res


def _colsum(v):
    return jnp.sum(v, axis=0, keepdims=True)


def _rope_fwd(t, c, s_up, s_dn):
    return t * c + pltpu.roll(t, LANES - 16, 1) * s_up + pltpu.roll(t, 16, 1) * s_dn


def _rope_bwd(d, c, s_up, s_dn):
    return d * c + pltpu.roll(d * s_up, 16, 1) + pltpu.roll(d * s_dn, LANES - 16, 1)


def _rope_tables(seq):
    half = ROPE // 2
    inv = jnp.power(ROPE_THETA, -jnp.arange(half, dtype=F32) / half)
    ang = jnp.arange(seq, dtype=F32)[:, None] * inv[None, :]
    cos, sin = jnp.cos(ang), jnp.sin(ang)
    zeros = jnp.zeros((seq, half), F32)
    lo, hi = jnp.ones((seq, KR_LANE), F32), jnp.ones((seq, LANES - KR_LANE - ROPE), F32)
    c = jnp.concatenate([lo, cos, cos, hi], axis=1)
    c_rope_only = jnp.concatenate([0 * lo, cos, cos, 0 * hi], axis=1)
    s_up = jnp.concatenate([0 * lo, -sin, zeros, 0 * hi], axis=1)
    s_dn = jnp.concatenate([0 * lo, zeros, sin, 0 * hi], axis=1)
    return c, s_up, s_dn, c_rope_only


def _rms(x, g):
    r = lax.rsqrt(jnp.mean(x * x, axis=1, keepdims=True) + RMS_EPS)
    return x * r * g


def _rms_bwd(x, g, dy):
    r = lax.rsqrt(jnp.mean(x * x, axis=1, keepdims=True) + RMS_EPS)
    xh = x * r
    dxh = dy * g
    dx = r * (dxh - xh * jnp.mean(dxh * xh, axis=1, keepdims=True))
    return dx, _colsum(dy * xh)


def _ln_stats(x):
    mu = jnp.mean(x, axis=1, keepdims=True)
    xc = x - mu
    r = lax.rsqrt(jnp.mean(xc * xc, axis=1, keepdims=True) + LN_EPS)
    return xc * r, r


def _ln_bwd(xh, r, g, dy):
    dxh = dy * g
    dx = r * (dxh - jnp.mean(dxh, axis=1, keepdims=True) - xh * jnp.mean(dxh * xh, axis=1, keepdims=True))
    return dx, _colsum(dy * xh), _colsum(dy)


def _table_specs(tables, sub):
    whole = lambda a: pl.BlockSpec(a.shape, lambda b, g: (0,) * a.ndim)
    if len(tables) == 1:
        return [whole(tables[0])]
    return [whole(tables[0]), whole(tables[1]), pl.BlockSpec((sub, 1, LANES), lambda b, g: (g, 0, 0))]


def _biased(s, table_refs, delta, head):
    if delta < table_refs[0].shape[0]:
        s = s + table_refs[0][delta]
    if len(table_refs) == 3:
        s = s - table_refs[2][head, 0:1, 0:1] * table_refs[1][delta]
    return s


def _lane_masks(sub):
    lane = lax.broadcasted_iota(jnp.int32, (1, LANES), 1)
    return [(lane // (LANES // sub) == a).astype(F32) for a in range(sub)]


def _attn_fwd(q, qb0, k, kb0, v, vb0, tables, scale, *, name, batch, seq, sub=1, stride=1, wide_qk=False, after=None):
    t = ATT_T
    nq = seq // t
    rows = batch * seq
    n_tab = len(tables)
    qk_w = sub * LANES if wide_qk else LANES

    def body(q_ref, k_ref, v_ref, *rest):
        table_refs = rest[:n_tab]
        o_ref, lse_ref, vtb = rest[n_tab + (after is not None):][:3]
        per_head = rest[n_tab + (after is not None) + 3:]
        qbs, kbs = per_head[:sub], per_head[sub:]
        masks = _lane_masks(sub)
        for a in range(sub):
            lanes = slice(a * LANES, (a + 1) * LANES) if wide_qk else slice(None)
            qa = q_ref[:, lanes]
            qbs[a][...] = (qa.astype(F32) * masks[a]).astype(BF16) if sub > 1 and not wide_qk else qa.astype(BF16)
            if wide_qk or a == 0:
                kbs[a][...] = k_ref[:, lanes].astype(BF16)
        vtb[...] = v_ref[...].astype(F32).T.astype(BF16)
        for i in range(nq):
            out_t = None
            for a in range(sub):
                qt, kb = qbs[a][i * t:(i + 1) * t, :], kbs[a if wide_qk else 0]
                logits = [_biased(lax.dot_general(kb[j * t:(j + 1) * t, :], qt, NT, preferred_element_type=F32) * (scale * LOG2E), table_refs, i - j, a)
                          for j in range(i + 1)]
                m = jnp.max(functools.reduce(jnp.maximum, logits), axis=0, keepdims=True)
                ps = [jnp.exp2(s - m) for s in logits]
                l = jnp.sum(functools.reduce(jnp.add, ps), axis=0, keepdims=True)
                acc = functools.reduce(jnp.add, [lax.dot_general(vtb[:, j * t:(j + 1) * t], p.astype(BF16), NN, preferred_element_type=F32)
                                                 for j, p in enumerate(ps)])
                part = acc / l if sub == 1 else (acc / l) * masks[a].T
                out_t = part if out_t is None else out_t + part
                lse_ref[i * t:(i + 1) * t, a * LANES:(a + 1) * LANES] = jnp.broadcast_to((m + jnp.log2(l)) * LN2, (LANES, t)).T
            o_ref[i * t:(i + 1) * t, :] = out_t.T

    slab = lambda b0, step, width=LANES: pl.BlockSpec((seq, width), lambda b, g: (b, b0 + step * g))
    groups = N_HEADS // sub
    n_k = sub if wide_qk else 1
    return pl.pallas_call(
        body, name=name,
        grid=(batch, groups),
        in_specs=[slab(qb0, stride, qk_w), slab(kb0, stride, qk_w), slab(vb0, stride)] + _table_specs(tables, sub)
        + [pl.BlockSpec(memory_space=pl.ANY)] * (after is not None),
        out_specs=[slab(0, 1), slab(0, 1, sub * LANES)],
        out_shape=[jax.ShapeDtypeStruct((rows, groups * LANES), F32), jax.ShapeDtypeStruct((rows, N_HEADS * LANES), F32)],
        scratch_shapes=[pltpu.VMEM((LANES, seq), BF16)] + [pltpu.VMEM((seq, LANES), BF16)] * (sub + n_k),
        compiler_params=_params(("arbitrary", "arbitrary")),
    )(q, k, v, *tables, *([after] if after is not None else []))


def _attn_bwd(q, qb0, k, kb0, v, vb0, o, do, lse, tables, scale, *, name, batch, seq, out_dtype, sub=1, stride=1, wide_qk=False, after=None):
    t = ATT_T
    nq = seq // t
    rows = batch * seq
    n_tab = len(tables)
    groups = N_HEADS // sub
    packed = sub > 1 and not wide_qk
    n_out = 1 if packed else 3
    n_k = sub if wide_qk else 1
    qk_w = sub * LANES if wide_qk else LANES

    def body(q_ref, k_ref, v_ref, o_ref, do_ref, lse_ref, *rest):
        table_refs = rest[:n_tab]
        rest = rest[n_tab + (after is not None):]
        out_refs, (vb, dva), rest = rest[:n_out], rest[n_out:n_out + 2], rest[n_out + 2:]
        kbs, dkas, rest = rest[:n_k], rest[n_k:2 * n_k], rest[2 * n_k:]
        qbs, dobs, qtbs, dotbs = (rest[g * sub:(g + 1) * sub] for g in range(4))
        masks = _lane_masks(sub)
        vb[...] = v_ref[...].astype(BF16)
        for a in range(sub):
            lanes = slice(a * LANES, (a + 1) * LANES) if wide_qk else slice(None)
            qa = q_ref[:, lanes].astype(F32) * masks[a] if packed else q_ref[:, lanes].astype(F32)
            doa = do_ref[...] * masks[a] if sub > 1 else do_ref[...]
            qbs[a][...] = qa.astype(BF16)
            dobs[a][...] = doa.astype(BF16)
            qtbs[a][...] = qa.T.astype(BF16)
            dotbs[a][...] = doa.T.astype(BF16)
            if wide_qk or a == 0:
                kbs[a][...] = k_ref[:, lanes].astype(BF16)
        first_k, first_v = [[True] * nq for _ in range(n_k)], [True] * nq
        for i in range(nq):
            at = slice(i * t, (i + 1) * t)
            dq_all = None
            for a in range(sub):
                qt, dot, kb, dka = qbs[a][at, :], dobs[a][at, :], kbs[a if wide_qk else 0], dkas[a if wide_qk else 0]
                lse_t = lse_ref[at, a * LANES:a * LANES + 1] * LOG2E
                od = o_ref[at, :] * do_ref[at, :]
                delta = jnp.sum(od * masks[a] if sub > 1 else od, axis=1, keepdims=True)
                dq = None
                for j in range(i + 1):
                    kat = slice(j * t, (j + 1) * t)
                    kt, vt = kb[kat, :], vb[kat, :]
                    p = jnp.exp2(_biased(lax.dot_general(qt, kt, NT, preferred_element_type=F32) * (scale * LOG2E), table_refs, i - j, a) - lse_t)
                    dp = lax.dot_general(dot, vt, NT, preferred_element_type=F32)
                    ds = (p * (dp - delta) * scale).astype(BF16)
                    dk_part = lax.dot_general(qtbs[a][:, at], ds, NN, preferred_element_type=F32)
                    dv_part = lax.dot_general(dotbs[a][:, at], p.astype(BF16), NN, preferred_element_type=F32)
                    firsts = first_k[a if wide_qk else 0]
                    if firsts[j]:
                        dka[:, kat] = dk_part
                        firsts[j] = False
                    else:
                        dka[:, kat] += dk_part
                    if first_v[j]:
                        dva[:, kat] = dv_part
                        first_v[j] = False
                    else:
                        dva[:, kat] += dv_part
                    dq_part = lax.dot_general(ds, kt, NN, preferred_element_type=F32)
                    dq = dq_part if dq is None else dq + dq_part
                if wide_qk:
                    out_refs[0][at, a * LANES:(a + 1) * LANES] = dq.astype(out_refs[0].dtype)
                else:
                    dq = dq * masks[a] if sub > 1 else dq
                    dq_all = dq if dq_all is None else dq_all + dq
            if not wide_qk:
                out_refs[0][at, 0:LANES] = dq_all.astype(out_refs[0].dtype)
        if packed:
            out_refs[0][:, LANES:2 * LANES] = dkas[0][...].T.astype(out_refs[0].dtype)
            out_refs[0][:, 2 * LANES:3 * LANES] = dva[...].T.astype(out_refs[0].dtype)
            out_refs[0][:, 3 * LANES:] = jnp.zeros((seq, LANES), out_refs[0].dtype)
        else:
            for a in range(n_k):
                out_refs[1][:, a * LANES:(a + 1) * LANES] = dkas[a][...].T.astype(out_refs[1].dtype)
            out_refs[2][...] = dva[...].T.astype(out_refs[2].dtype)

    slab = lambda b0, step, width=LANES: pl.BlockSpec((seq, width), lambda b, g: (b, b0 + step * g))
    if packed:
        out_specs = [slab(0, 1, 4 * LANES)]
        out_shape = [jax.ShapeDtypeStruct((rows, groups * 4 * LANES), out_dtype)]
    else:
        out_specs = [slab(0, 1, qk_w), slab(0, 1, qk_w), slab(0, 1)]
        out_shape = [jax.ShapeDtypeStruct((rows, N_HEADS * LANES), out_dtype)] * 2 + [jax.ShapeDtypeStruct((rows, groups * LANES), out_dtype)]
    res = pl.pallas_call(
        body, name=name,
        grid=(batch, groups),
        in_specs=[slab(qb0, stride, qk_w), slab(kb0, stride, qk_w), slab(vb0, stride), slab(0, 1), slab(0, 1), slab(0, 1, sub * LANES)]
        + _table_specs(tables, sub) + [pl.BlockSpec(memory_space=pl.ANY)] * (after is not None),
        out_specs=out_specs, out_shape=out_shape,
        scratch_shapes=[pltpu.VMEM((seq, LANES), BF16), pltpu.VMEM((LANES, seq), F32)]
        + [pltpu.VMEM((seq, LANES), BF16)] * n_k + [pltpu.VMEM((LANES, seq), F32)] * n_k
        + [pltpu.VMEM((seq, LANES), BF16)] * (2 * sub) + [pltpu.VMEM((LANES, seq), BF16)] * (2 * sub),
        compiler_params=_params(("arbitrary", "arbitrary")),
    )(q, k, v, o, do, lse, *tables, *([after] if after is not None else []))
    return res[0] if packed else res


def _attention_tables(seq):
    n = seq // ATT_T
    pos = jnp.arange(ATT_T, dtype=jnp.int32)
    dist = jnp.arange(n, dtype=jnp.int32)[:, None, None] * ATT_T + pos[None, :, None] - pos[None, None, :]
    causal = jnp.where(dist[:1] >= 0, 0.0, NEG).astype(F32)
    count = jnp.zeros(dist.shape, F32)
    for window, dilation in DIL_PATTERNS:
        count += ((dist >= 0) & (dist <= window) & (dist % dilation == 0)).astype(F32)
    held = jnp.where(count > 0, jnp.log2(jnp.maximum(count, 1.0)), NEG).astype(F32)
    slopes = jnp.asarray([2.0 ** (-8.0 * (i + 1) / N_HEADS) for i in range(N_HEADS)], F32)
    slopes = jnp.broadcast_to(slopes[:, None, None], (N_HEADS, 1, LANES))
    turned = lambda a: jnp.swapaxes(a, 1, 2)
    far = dist.astype(F32) * LOG2E
    return ((causal,), (turned(causal),)), ((held, far, slopes), (turned(held), turned(far), slopes))


def _pad_heads(w, width):
    kdim, n = w.shape[0], w.shape[1] // width
    return jnp.pad(w.reshape(kdim, n, width), ((0, 0), (0, 0), (0, LANES - width))).reshape(kdim, n * LANES)


def _unpad_heads(w, width):
    kdim, n = w.shape[0], w.shape[1] // LANES
    return w.reshape(kdim, n, LANES)[:, :, :width].reshape(kdim, n * width)


def _pad_head_rows(w, width):
    n, kdim = w.shape[0] // width, w.shape[1]
    return jnp.pad(w.reshape(n, width, kdim), ((0, 0), (0, LANES - width), (0, 0))).reshape(n * LANES, kdim)


def _unpad_head_rows(w, width):
    n, kdim = w.shape[0] // LANES, w.shape[1]
    return w.reshape(n, LANES, kdim)[:, :width].reshape(n * width, kdim)


def _pad_w_in_t(wt):
    n_qkv, pair = 3 * N_HEADS * DIL_DIM, 2 * DIL_DIM
    zeros = lambda n: jnp.zeros((n, wt.shape[1]), wt.dtype)
    main = jnp.concatenate([wt[:LORA_W], zeros(KR_LANE), wt[LORA_W:LORA_W + ROPE], zeros(P_GATE - P_KR - KR_LANE - ROPE),
                            wt[LORA_W + ROPE + n_qkv:]], axis=0)
    qkv = wt[LORA_W + ROPE:LORA_W + ROPE + n_qkv].reshape(3, N_HEADS // 2, pair, wt.shape[1]).transpose(1, 0, 2, 3)
    dil = jnp.pad(qkv, ((0, 0), (0, 1), (0, 0), (0, 0))).reshape(P_DIL, wt.shape[1])
    return main, dil


def _unpad_w_in_t(gt):
    qkv = gt[P_HALF:].reshape(N_HEADS // 2, 4, 2 * DIL_DIM, gt.shape[1])[:, :3].transpose(1, 0, 2, 3).reshape(3 * N_HEADS * DIL_DIM, gt.shape[1])
    return jnp.concatenate([gt[P_LORA:P_KR], gt[P_KR + KR_LANE:P_KR + KR_LANE + ROPE], qkv, gt[P_GATE:P_HALF]], axis=0)


def _split_ukv(w):
    w3 = w.reshape(w.shape[0], N_HEADS, NOPE + V_DIM)
    return _pad_heads(w3[:, :, :NOPE].reshape(w.shape[0], -1), NOPE), w3[:, :, NOPE:].reshape(w.shape[0], -1)


def _merge_ukv(g_k, g_v):
    kdim = g_k.shape[0]
    k3 = _unpad_heads(g_k, NOPE).reshape(kdim, N_HEADS, NOPE)
    return jnp.concatenate([k3, g_v.reshape(kdim, N_HEADS, V_DIM)], axis=2).reshape(kdim, N_HEADS * (NOPE + V_DIM))


def _join_cols(w):
    return w.transpose(1, 0, 2).reshape(w.shape[1], N_CHIPS * w.shape[2])


def _split_cols(g):
    return g.reshape(g.shape[0], N_CHIPS, g.shape[1] // N_CHIPS).transpose(1, 0, 2)


def _local_step(x3, target3, wg, b_gate, g_q_a, g_kv_a, ln1_g, ln1_b, ln2_g, ln2_b, token=None, late_arrived=None, late_weights=None,
                early_grads=None, early_grads_go=None, last_grads=None, tables=None):
    w_main_t, w_dil_t = _pad_w_in_t(wg["w_in"].reshape(IN_WIDTH, D_MODEL))
    w_uq_pt = _pad_head_rows(wg["w_uq"].reshape(N_HEADS * MLA_QK, Q_LORA), MLA_QK)
    w_ukv = _join_cols(wg["w_ukv"])
    batch, seq, _ = x3.shape
    rows = batch * seq
    x = x3.reshape(rows, D_MODEL)
    target = target3.reshape(rows, D_MODEL)
    row = functools.partial(_rowwise, rows=rows, seq=seq)
    mm = _matmul

    w_uk_p, w_uv = _split_ukv(w_ukv)
    b0, b1 = b_gate[0:1], b_gate[1:2]
    rope_c, rope_up, rope_dn, rope_c_only = _rope_tables(seq)
    (mla_bwd_tables, mla_fwd_tables), (dil_bwd_tables, dil_fwd_tables) = _attention_tables(seq) if tables is None else tables
    scale_mla, scale_dil = MLA_QK ** -0.5, DIL_DIM ** -0.5
    lora0, kr0, gate0 = P_LORA // LORA_W, P_KR // LANES, P_GATE // D_MODEL

    proj = mm(x, w_main_t, mode="nt", name="proj", tm=1024, tn=1536, tk=1024, after=token)
    proj_d = mm(x, w_dil_t, mode="nt", name="proj_dil", tm=1024, tn=1024, tk=1024, out_dtypes=(BF16,))

    def prep(lora, gq, gkv):
        return _rms(lora[:, :Q_LORA], gq), _rms(lora[:, Q_LORA:], gkv)

    qn, kvn = row(prep, name="mla_rms", ins=[(proj, LORA_W, lora0, "row"), (g_q_a, 0, 0, "full"), (g_kv_a, 0, 0, "full")],
                  outs=[(Q_LORA, BF16), (KV_LORA, BF16)])
    q_lin = mm(qn, w_uq_pt, mode="nt", name="q_up", tm=1024, tn=1024, tk=Q_LORA)
    k_lin = mm(kvn, w_uk_p, mode="nn", name="k_up", tm=1024, tn=1024, tk=KV_LORA)
    v_a = mm(kvn, w_uv, mode="nn", name="v_up", tm=1024, tn=1024, tk=KV_LORA, out_dtypes=(BF16,))

    def rope_qk(ql, kl, kr, c, up, dn):
        k_rot = _rope_fwd(kr, c, up, dn)
        qs = [_rope_fwd(ql[:, h * LANES:(h + 1) * LANES], c, up, dn) for h in range(N_HEADS)]
        ks = [kl[:, h * LANES:(h + 1) * LANES] + k_rot for h in range(N_HEADS)]
        return jnp.concatenate(qs, axis=1), jnp.concatenate(ks, axis=1)

    pos = lambda tab: (tab, LANES, 0, "pos")
    q_a, k_a = row(rope_qk, name="rope_qk",
                   ins=[(q_lin, D_MODEL, 0, "row"), (k_lin, D_MODEL, 0, "row"), (proj, LANES, kr0, "row"), pos(rope_c), pos(rope_up), pos(rope_dn)],
                   outs=[(N_HEADS * LANES, BF16), (N_HEADS * LANES, BF16)])
    o_a, lse_a = _attn_fwd(q_a, 0, k_a, 0, v_a, 0, mla_fwd_tables, scale_mla, name="mla_fwd", batch=batch, seq=seq, sub=2, wide_qk=True)
    arrived = None if late_arrived is None else late_arrived(o_a)
    o_b, lse_b = _attn_fwd(proj_d, 0, proj_d, 1, proj_d, 2, dil_fwd_tables, scale_dil, name="dil_fwd", batch=batch, seq=seq, sub=2, stride=4, after=arrived)
    late = wg if late_weights is None else late_weights(o_b)
    w_oa = _join_cols(late["w_o_mla"])
    w_ob = _join_cols(late["w_o_dil"])
    w_out, w_ff1, w_ff2 = late["w_out"].reshape(D_MODEL, D_MODEL), late["w_ff1"], late["w_ff2"].reshape(D_FF, D_MODEL)
    y_a = mm(o_a, w_oa, mode="nn", name="o_mla", tm=1024, tn=1024, tk=1024, out_dtypes=(BF16,))
    y_b = mm(o_b, w_ob, mode="nn", name="o_dil", tm=1024, tn=1024, tk=1024, out_dtypes=(BF16,))

    def gate(t0, t1, c0, c1, ya, yb):
        return (jax.nn.sigmoid(t0 + c0) * ya + jax.nn.sigmoid(t1 + c1) * yb,)

    gate_ins = [(proj, D_MODEL, gate0, "row"), (proj, D_MODEL, gate0 + 1, "row"), (b0, 0, 0, "full"), (b1, 0, 0, "full")]
    (u,) = row(gate, name="gate", ins=gate_ins + [(y_a, D_MODEL, 0, "row"), (y_b, D_MODEL, 0, "row")], outs=[(D_MODEL, BF16)])
    mixed = mm(u, w_out, mode="nn", name="mix", tm=1024, tn=1024, tk=1024)

    def ln1(xv, mv, g, b):
        r1 = ALPHA * xv + mv
        xh, _ = _ln_stats(r1)
        return r1, xh * g + b

    r1, h = row(ln1, name="ln1", ins=[(x, D_MODEL, 0, "row"), (mixed, D_MODEL, 0, "row"), (ln1_g, 0, 0, "full"), (ln1_b, 0, 0, "full")],
                outs=[(D_MODEL, F32), (D_MODEL, F32)])

    def relu2(acc):
        r = jnp.maximum(acc, 0.0)
        return acc, r * r

    a_ff, z = mm(h, w_ff1, mode="nn", name="ff1", tm=1024, tn=1024, tk=1024, out_dtypes=(BF16, BF16), epilogue=relu2, b_shards=True)
    f = mm(z, w_ff2, mode="nn", name="ff2", tm=1024, tn=1024, tk=2048)

    def ln2_loss(hv, fv, tv, g, b):
        xh, r = _ln_stats(ALPHA * hv + fv)
        err = xh * g + b - tv
        dy = err * (1.0 / D_MODEL)
        dr2, dg, db = _ln_bwd(xh, r, g, dy)
        loss = jnp.sum(_colsum(err * err), axis=1, keepdims=True) * (0.5 / D_MODEL)
        return dr2, jnp.broadcast_to(loss, (1, LANES)), dg, db

    dr2, loss_l, d_ln2_g, d_ln2_b = row(
        ln2_loss, name="ln2_loss",
        ins=[(h, D_MODEL, 0, "row"), (f, D_MODEL, 0, "row"), (target, D_MODEL, 0, "row"), (ln2_g, 0, 0, "full"), (ln2_b, 0, 0, "full")],
        outs=[(D_MODEL, F32)], sums=[LANES, D_MODEL, D_MODEL])

    d_w_ff2 = mm(z, dr2, mode="tn", name="d_w_ff2", tm=1024, tn=1024, tk=2048)
    da = mm(dr2, w_ff2, mode="nt", name="d_ff_act", tm=1024, tn=1024, tk=1024, out_dtypes=(BF16,), extras=(a_ff,),
            epilogue=lambda acc, av: (acc * (2.0 * jnp.maximum(av.astype(F32), 0.0)),))
    d_w_ff1 = mm(h, da, mode="tn", name="d_w_ff1", tm=1024, tn=1024, tk=2048, out_shards=True)
    dh = mm(da, w_ff1, mode="nt", name="d_h", tm=1024, tn=1024, tk=1024, extras=(dr2,), epilogue=lambda acc, rv: (acc + ALPHA * rv,), b_shards=True)

    def ln1_bwd(dhv, r1v, g):
        xh, r = _ln_stats(r1v)
        return _ln_bwd(xh, r, g, dhv)

    dr1, d_ln1_g, d_ln1_b = row(ln1_bwd, name="ln1_bwd", ins=[(dh, D_MODEL, 0, "row"), (r1, D_MODEL, 0, "row"), (ln1_g, 0, 0, "full")],
                                outs=[(D_MODEL, F32)], sums=[D_MODEL, D_MODEL])
    d_w_out = mm(u, dr1, mode="tn", name="d_w_out", tm=1024, tn=1024, tk=1024)
    du = mm(dr1, w_out, mode="nt", name="d_u", tm=1024, tn=1024, tk=1024, out_dtypes=(BF16,))

    def gate_bwd(t0, t1, c0, c1, ya, yb, duv):
        s0, s1 = jax.nn.sigmoid(t0 + c0), jax.nn.sigmoid(t1 + c1)
        dt0 = duv * ya * s0 * (1.0 - s0)
        dt1 = duv * yb * s1 * (1.0 - s1)
        return duv * s0, duv * s1, jnp.concatenate([dt0, dt1], axis=1), jnp.concatenate([_colsum(dt0), _colsum(dt1)], axis=1)

    dy_a, dy_b, d_gates, d_b_gate = row(
        gate_bwd, name="gate_bwd", ins=gate_ins + [(y_a, D_MODEL, 0, "row"), (y_b, D_MODEL, 0, "row"), (du, D_MODEL, 0, "row")],
        outs=[(D_MODEL, BF16), (D_MODEL, BF16), (2 * D_MODEL, BF16)], sums=[2 * D_MODEL])
    d_w_oa = mm(o_a, dy_a, mode="tn", name="d_w_o_mla", tm=1024, tn=1024, tk=1024)
    d_w_ob = mm(o_b, dy_b, mode="tn", name="d_w_o_dil", tm=1024, tn=1024, tk=1024)
    grads = dict(w_o_mla=_split_cols(d_w_oa), w_o_dil=_split_cols(d_w_ob),
                 w_out=d_w_out.reshape(N_CHIPS, D_MODEL // N_CHIPS, D_MODEL), w_ff1=d_w_ff1, w_ff2=d_w_ff2.reshape(N_CHIPS, D_FF // N_CHIPS, D_MODEL))
    sent = None if early_grads is None else early_grads(grads)
    do_a = mm(dy_a, w_oa, mode="nt", name="d_o_mla", tm=1024, tn=1024, tk=1024, after=sent)
    do_b = mm(dy_b, w_ob, mode="nt", name="d_o_dil", tm=1024, tn=1024, tk=1024)
    dq_a, dk_a, dv_a = _attn_bwd(q_a, 0, k_a, 0, v_a, 0, o_a, do_a, lse_a, mla_bwd_tables, scale_mla,
                                 name="mla_bwd", batch=batch, seq=seq, out_dtype=F32, sub=2, wide_qk=True)
    going = None if early_grads_go is None else early_grads_go(dq_a)
    d_qkv_d = _attn_bwd(proj_d, 0, proj_d, 1, proj_d, 2, o_b, do_b, lse_b, dil_bwd_tables, scale_dil,
                        name="dil_bwd", batch=batch, seq=seq, out_dtype=BF16, sub=2, stride=4, after=going)

    def mla_post(dq, dk, c, up, dn, c_only):
        dqs = [_rope_bwd(dq[:, h * LANES:(h + 1) * LANES], c, up, dn) for h in range(N_HEADS)]
        dk_sum = dk[:, :LANES]
        for h in range(1, N_HEADS):
            dk_sum = dk_sum + dk[:, h * LANES:(h + 1) * LANES]
        return jnp.concatenate(dqs, axis=1), _rope_bwd(dk_sum, c_only, up, dn)

    dq_lin, d_kr = row(mla_post, name="mla_unrope",
                       ins=[(dq_a, D_MODEL, 0, "row"), (dk_a, D_MODEL, 0, "row"), pos(rope_c), pos(rope_up), pos(rope_dn), pos(rope_c_only)],
                       outs=[(N_HEADS * LANES, BF16), (LANES, BF16)])
    d_w_uq_pt = mm(dq_lin, qn, mode="tn", name="d_w_uq", tm=1024, tn=Q_LORA, tk=1024)
    d_w_uk_p = mm(kvn, dk_a, mode="tn", name="d_w_uk", tm=KV_LORA, tn=1024, tk=1024)
    d_w_uv = mm(kvn, dv_a, mode="tn", name="d_w_uv", tm=KV_LORA, tn=1024, tk=1024)
    d_qn = mm(dq_lin, w_uq_pt, mode="nn", name="d_qn", tm=1024, tn=Q_LORA, tk=1024)
    d_kvn_k = mm(dk_a, w_uk_p, mode="nt", name="d_kvn_k", tm=1024, tn=KV_LORA, tk=1024)
    d_kvn = mm(dv_a, w_uv, mode="nt", name="d_kvn", tm=1024, tn=KV_LORA, tk=1024, extras=(d_kvn_k,), epilogue=lambda acc, e: (acc + e,))

    def rms_bwd(lora, dq, dkv, dkr, gq, gkv):
        dxq, dgq = _rms_bwd(lora[:, :Q_LORA], gq, dq)
        dxk, dgk = _rms_bwd(lora[:, Q_LORA:], gkv, dkv)
        tail = jnp.zeros((dxq.shape[0], P_GATE - P_KR - LANES), F32)
        return jnp.concatenate([dxq, dxk, dkr.astype(F32), tail], axis=1), dgq, dgk

    d_tail, d_g_q_a, d_g_kv_a = row(
        rms_bwd, name="mla_rms_bwd",
        ins=[(proj, LORA_W, lora0, "row"), (d_qn, Q_LORA, 0, "row"), (d_kvn, KV_LORA, 0, "row"), (d_kr, LANES, 0, "row"),
             (g_q_a, 0, 0, "full"), (g_kv_a, 0, 0, "full")],
        outs=[(P_GATE, BF16)], sums=[Q_LORA, KV_LORA])
    d_proj = [d_tail, d_gates, d_qkv_d]
    d_w_in_pt = mm(d_proj, x, mode="tn", name="d_w_in", tm=1024, tn=1024, tk=1024)
    grads.update(w_in=_unpad_w_in_t(d_w_in_pt).reshape(N_CHIPS, IN_WIDTH // N_CHIPS, D_MODEL),
                 w_uq=_unpad_head_rows(d_w_uq_pt, MLA_QK).reshape(N_CHIPS, N_HEADS * MLA_QK // N_CHIPS, Q_LORA),
                 w_ukv=_split_cols(_merge_ukv(d_w_uk_p, d_w_uv)))
    leaving = None if last_grads is None else last_grads(grads)
    grad_x = mm(d_proj, jnp.concatenate([w_main_t, w_dil_t], axis=0), mode="nn", name="d_x", tm=1024, tn=1024, tk=1024, extras=(dr1,), epilogue=lambda acc, rv: (acc + ALPHA * rv,),
                after=leaving)

    grads.update(
        b_gate=d_b_gate.reshape(2, D_MODEL), g_q_a=d_g_q_a, g_kv_a=d_g_kv_a, ln1_g=d_ln1_g, ln1_b=d_ln1_b, ln2_g=d_ln2_g, ln2_b=d_ln2_b)
    return loss_l, grad_x.reshape(batch, seq, D_MODEL), grads


BIG = ("w_in", "w_uq", "w_ukv", "w_o_mla", "w_o_dil", "w_out", "w_ff1", "w_ff2")
SMALL = (("b_gate", 2 * D_MODEL), ("g_q_a", Q_LORA), ("g_kv_a", KV_LORA), ("ln1_g", D_MODEL), ("ln1_b", D_MODEL),
         ("ln2_g", D_MODEL), ("ln2_b", D_MODEL))
TRANSPOSED = ("w_in", "w_uq")
D2D_PIECES = (4, 2, 1)
ANY = pl.BlockSpec(memory_space=pl.ANY)
SIDE_EFFECTS = pltpu.CompilerParams(has_side_effects=True)


def _place():
    x, y, c = lax.axis_index("x"), lax.axis_index("y"), lax.axis_index("c")
    return x, y, c, ((1 - x, y), (x, 1 - y), (1 - x, 1 - y))


def _half_axis(shape):
    return 0 if shape[0] % 32 == 0 else 1


def _half_shape(shape):
    return (shape[0] // 2, shape[1]) if _half_axis(shape) == 0 else (shape[0], shape[1] // 2)


def _window(ref, lead, shape, which=None, pieces=False):
    axis = _half_axis(shape)
    size = shape[axis] if which is None else shape[axis] // 2
    base = 0 if which is None else which * size
    tile = (16, LANES)[axis]
    count = next(c for c in D2D_PIECES if size % (tile * c) == 0) if pieces else 1
    step = size // count
    spans = [pl.ds(pl.multiple_of(base + i * step, tile), step) for i in range(count)]
    refs = [ref.at[(*lead, s)] if axis == 0 else ref.at[(*lead, slice(None), s)] for s in spans]
    return refs if pieces else refs[0]


def _remote(src, dst, send, recv, to):
    return pltpu.make_async_remote_copy(src_ref=src, dst_ref=dst, send_sem=send, recv_sem=recv, device_id=to, device_id_type=MESH)


def _pair_split(grads, name):
    n = len(grads)

    def body(*refs):
        srcs, outs, (send, recv) = refs[:n], refs[n:2 * n], refs[2 * n:]
        x, y, c, _ = _place()
        for t in range(n):
            for s in range(N_CHIPS):
                _remote(_window(srcs[t], (s,), grads[t].shape[1:], 1 - c), outs[t].at[s], send.at[t], recv.at[t], (x, y, 1 - c)).start()
        for t in range(n):
            _remote(_window(srcs[t], (slice(None),), grads[t].shape[1:], 1 - c), outs[t], send.at[t], recv.at[t], (x, y, 1 - c)).wait()

    return pl.pallas_call(
        body, name=name, in_specs=[ANY] * n, out_specs=[ANY] * n,
        out_shape=[jax.ShapeDtypeStruct((N_CHIPS,) + _half_shape(g.shape[1:]), g.dtype) for g in grads],
        scratch_shapes=[pltpu.SemaphoreType.DMA((n,)), pltpu.SemaphoreType.DMA((n,))],
        compiler_params=SIDE_EFFECTS,
    )(*grads)


HBM = pl.BlockSpec(memory_space=pltpu.HBM)
SEM = pl.BlockSpec(memory_space=pltpu.SEMAPHORE)
SPLIT = pltpu.CompilerParams(has_side_effects=pltpu.SideEffectType.DATAFLOW_SIDE_EFFECTING)


def _in_hbm(a):
    return pltpu.with_memory_space_constraint(a, pltpu.HBM)


def _split_copies(kind, srcs, lands):
    x, y, c, chips = _place()
    out = []
    for t in range(len(srcs)):
        if kind == "pair":
            out += [(t, s % 3, _window(srcs[t], (s,), srcs[t].shape[1:], 1 - c), lands[t].at[s], (x, y, 1 - c)) for s in range(N_CHIPS)]
            continue
        if kind == "join":
            out += [(t, 0, a, b, (x, y, 1 - c)) for a, b in zip(_window(srcs[t], (), srcs[t].shape, None, True), _window(lands[t], (), srcs[t].shape, None, True))]
            continue
        if kind == "forward":
            shape, sibling = srcs[t].shape, (x, y, 1 - c)
            out += [(t, 0, a, b, sibling) for a, b in zip(_window(srcs[t], (), shape, None, True), _window(lands[t], (2 * x + y,), shape, None, True))]
            out += [(t, j, a, a, sibling) for j, (cx, cy) in enumerate(chips) for a in _window(lands[t], (2 * cx + cy,), shape, c, True)]
            continue
        for j, (cx, cy) in enumerate(chips):
            if kind == "gather":
                shape = srcs[t].shape
                out.append((t, j, _window(srcs[t], (), shape, c), _window(lands[t], (2 * x + y,), shape, c), (cx, cy, c)))
            else:
                out.append((t, j, srcs[t].at[2 * cx + cy], lands[t].at[j], (cx, cy, c)))
    return out


def _split_start(kind, srcs, land_shapes, name, lands=None, after=None):
    n = len(srcs)

    def body(*refs):
        src_refs, land_refs, sems, token = refs[:n], refs[n:2 * n], refs[-7 - 2 * n:-1 - 2 * n], refs[-1]
        for t, j, s, d, to in _split_copies(kind, src_refs, land_refs):
            _remote(s, d, sems[j], sems[3 + j], to).start()
        token[...] = jnp.zeros_like(token)

    lands = [_in_hbm(lax.empty(s.shape, s.dtype)) for s in land_shapes] if lands is None else list(lands)
    thru = [pltpu.HBM(a.shape, a.dtype) for a in list(srcs) + lands]
    res = pl.pallas_call(
        body, name=name,
        out_shape=(*[pltpu.SemaphoreType.DMA(())] * 6, *thru, jax.ShapeDtypeStruct((8, LANES), F32)),
        in_specs=[HBM] * (2 * n) + [ANY] * (after is not None), out_specs=(*[SEM] * 6, *[HBM] * (2 * n), pl.BlockSpec(memory_space=pltpu.VMEM)),
        input_output_aliases={i: 6 + i for i in range(2 * n)}, compiler_params=SPLIT,
    )(*[_in_hbm(s) for s in srcs], *lands, *([after] if after is not None else []))
    return res[:6], res[6:6 + n], res[6 + n:6 + 2 * n], res[-1]


def _split_wait(kind, sems, srcs, lands, after, name):
    n = len(srcs)

    def body(*refs):
        src_refs, land_refs, sem_refs = refs[:n], refs[n:2 * n], refs[2 * n:2 * n + 6]
        for t, j, s, d, to in _split_copies(kind, src_refs, land_refs):
            cp = _remote(s, d, sem_refs[j], sem_refs[3 + j], to)
            cp.wait_send()
            cp.wait_recv()

    res = pl.pallas_call(
        body, name=name, out_shape=[pltpu.HBM(a.shape, a.dtype) for a in list(srcs) + list(lands)],
        in_specs=[HBM] * (2 * n) + [SEM] * 6 + [ANY], out_specs=[HBM] * (2 * n),
        input_output_aliases={i: i for i in range(2 * n)}, compiler_params=SPLIT,
    )(*srcs, *lands, *sems, after)
    return res[:n], res[n:]


def _sum_all_devices(vec, name):
    n_rows = vec.shape[0]

    def body(v_ref, out_ref, buf, send, recv):
        x, y, c, _ = _place()
        me = 4 * x + 2 * y + c
        buf[me] = v_ref[...]
        flips = [(a, b, d) for a in (0, 1) for b in (0, 1) for d in (0, 1)][1:]
        copies = []
        for r, (a, b, d) in enumerate(flips):
            px, py, pc = (1 - x if a else x), (1 - y if b else y), (1 - c if d else c)
            copies.append(pltpu.make_async_remote_copy(src_ref=v_ref, dst_ref=buf.at[me], send_sem=send.at[r], recv_sem=recv.at[r],
                                                       device_id=(px, py, pc), device_id_type=MESH))
            copies[-1].start()
        for r, (a, b, d) in enumerate(flips):
            px, py, pc = (1 - x if a else x), (1 - y if b else y), (1 - c if d else c)
            pltpu.make_async_remote_copy(src_ref=v_ref, dst_ref=buf.at[4 * px + 2 * py + pc], send_sem=send.at[r], recv_sem=recv.at[r],
                                         device_id=(px, py, pc), device_id_type=MESH).wait_recv()
        for cp in copies:
            cp.wait_send()
        total = buf[0]
        for k in range(1, N_DEV):
            total = total + buf[k]
        out_ref[...] = total

    vmem = pl.BlockSpec(memory_space=pltpu.VMEM)
    return pl.pallas_call(
        body, name=name, in_specs=[vmem], out_specs=vmem, out_shape=jax.ShapeDtypeStruct(vec.shape, F32),
        scratch_shapes=[pltpu.VMEM((N_DEV, n_rows, LANES), F32), pltpu.SemaphoreType.DMA((N_DEV - 1,)), pltpu.SemaphoreType.DMA((N_DEV - 1,))],
        compiler_params=pltpu.CompilerParams(has_side_effects=True),
    )(vec)


def _half_tile(half, width):
    t = half
    while t * width * 4 > (2 << 20) and t % 32 == 0:
        t //= 2
    return t


def _pair_add(g, theirs, core, name):
    _, half, width = theirs.shape
    t = _half_tile(half, width)
    n = half // t

    def body(c_ref, a_ref, b_ref, o_ref):
        o_ref[...] = (a_ref[...] + b_ref[...]).astype(BF16)

    tile = pl.BlockSpec((1, t, width), lambda j, i, c_ref: (j, i, 0))
    if _half_axis(g.shape[1:]) == 0:
        mine = pl.BlockSpec((1, t, width), lambda j, i, c_ref: (j, c_ref[0] * n + i, 0))
    else:
        mine = pl.BlockSpec((1, t, width), lambda j, i, c_ref: (j, i, c_ref[0]))
    return pl.pallas_call(
        body, name=name,
        grid_spec=pltpu.PrefetchScalarGridSpec(num_scalar_prefetch=1, grid=(N_CHIPS, n), in_specs=[mine, tile], out_specs=tile),
        out_shape=jax.ShapeDtypeStruct(theirs.shape, BF16), compiler_params=_params(("parallel", "parallel")),
    )(core, g, theirs)


def _chip_sum(part, others, chip, name, after=None):
    _, half, width = part.shape
    t = _half_tile(half, width)

    def body(s_ref, mine, p0, p1, p2, *rest):
        o_ref = rest[-1]
        o_ref[...] = ((mine[0].astype(F32) + p0[0].astype(F32)) + p1[0].astype(F32)) + p2[0].astype(F32)

    return pl.pallas_call(
        body, name=name,
        grid_spec=pltpu.PrefetchScalarGridSpec(
            num_scalar_prefetch=1, grid=(half // t,),
            in_specs=[pl.BlockSpec((1, t, width), lambda i, s_ref: (s_ref[0], i, 0))]
            + [pl.BlockSpec((1, t, width), lambda i, s_ref, j=j: (j, i, 0)) for j in range(3)] + [pl.BlockSpec(memory_space=pl.ANY)] * (after is not None),
            out_specs=pl.BlockSpec((t, width), lambda i, s_ref: (i, 0))),
        out_shape=jax.ShapeDtypeStruct((half, width), F32), compiler_params=_params(("parallel",)),
    )(chip, part, others, others, others, *([after] if after is not None else []))


EARLY = ("w_in", "w_uq", "w_ukv")
LATE = ("w_o_mla", "w_o_dil", "w_out", "w_ff1", "w_ff2")


def _chip_partials(grads, names, core, tag):
    gs = [grads[n] for n in names]
    theirs = _pair_split(gs, "pair_split_" + tag)
    return [_pair_add(g, th, core, "pair_add_" + n) for g, th, n in zip(gs, theirs, names)]


def _sum_small(vals):
    n_in = len(vals)
    n_rows = sum(a.shape[0] * a.shape[1] // LANES for a in vals)
    pad_rows = -(-n_rows // 8) * 8

    def chunks(refs):
        return [(ref, a, j) for ref in refs for a in range(ref.shape[0]) for j in range(ref.shape[1] // LANES)]

    def body(*refs):
        ins, outs, (buf, send, recv) = refs[:n_in], refs[n_in:2 * n_in], refs[2 * n_in:]
        x, y, c, _ = _place()
        me = 4 * x + 2 * y + c
        for r, (ref, a, j) in enumerate(chunks(ins)):
            buf[me, r:r + 1, :] = ref[a:a + 1, j * LANES:(j + 1) * LANES]
        if pad_rows > n_rows:
            buf[me, n_rows:pad_rows, :] = jnp.zeros((pad_rows - n_rows, LANES), F32)
        flips = [(a, b, d) for a in (0, 1) for b in (0, 1) for d in (0, 1)][1:]
        peers = [((1 - x if a else x), (1 - y if b else y), (1 - c if d else c)) for a, b, d in flips]
        copies = [_remote(buf.at[me], buf.at[me], send.at[r], recv.at[r], peer) for r, peer in enumerate(peers)]
        for cp in copies:
            cp.start()
        for r, (px, py, pc) in enumerate(peers):
            _remote(buf.at[me], buf.at[4 * px + 2 * py + pc], send.at[r], recv.at[r], (px, py, pc)).wait_recv()
        for cp in copies:
            cp.wait_send()
        total = buf[0]
        for k in range(1, N_DEV):
            total = total + buf[k]
        for r, (ref, a, j) in enumerate(chunks(outs)):
            ref[a:a + 1, j * LANES:(j + 1) * LANES] = total[r:r + 1, :]

    vmem = pl.BlockSpec(memory_space=pltpu.VMEM)
    return pl.pallas_call(
        body, name="sum_small", in_specs=[vmem] * n_in, out_specs=[vmem] * n_in,
        out_shape=[jax.ShapeDtypeStruct(a.shape, F32) for a in vals],
        scratch_shapes=[pltpu.VMEM((N_DEV, pad_rows, LANES), F32), pltpu.SemaphoreType.DMA((N_DEV - 1,)), pltpu.SemaphoreType.DMA((N_DEV - 1,))],
        compiler_params=SIDE_EFFECTS,
    )(*vals)


def _adam_math(w, g, m, v):
    nm = B1 * m + (1.0 - B1) * g
    nv = B2 * v + (1.0 - B2) * (g * g)
    m_hat = nm / (1.0 - B1 ** ADAM_STEP)
    v_hat = nv / (1.0 - B2 ** ADAM_STEP)
    return -LR * (m_hat / (jnp.sqrt(v_hat) + ADAM_EPS) + WD * w), nm, nv


def _adamw_big(w, mine, theirs, m, v, core, name, side_by_side=False):
    rows, width = w.shape
    if side_by_side:
        t = next(c for c in (152, 96, 64, 32, 16, 8) if rows % c == 0)
        hb = None
        half_spec = pl.BlockSpec((t, width // 2), lambda i, c_ref: (i, 0))
    else:
        t = next(c for c in (256, 128, 64, 32, 16, 8) if (rows // 2) % c == 0)
        hb = rows // 2 // t
        half_spec = pl.BlockSpec((t, width), lambda i, c_ref: (i % hb, 0))

    def body(c_ref, w_ref, a_ref, b_ref, m_ref, v_ref, g_ref, d_ref, nm_ref, nv_ref):
        south = c_ref[0] == 0
        if side_by_side:
            g = jnp.where(south, jnp.concatenate([a_ref[...], b_ref[...]], axis=1), jnp.concatenate([b_ref[...], a_ref[...]], axis=1))
        else:
            g = jnp.where((pl.program_id(0) < hb) == south, a_ref[...], b_ref[...])
        g_ref[...] = g
        d_ref[...], nm_ref[...], nv_ref[...] = _adam_math(w_ref[...], g, m_ref[...], v_ref[...])

    spec = pl.BlockSpec((t, width), lambda i, c_ref: (i, 0))
    return pl.pallas_call(
        body, name=name,
        grid_spec=pltpu.PrefetchScalarGridSpec(num_scalar_prefetch=1, grid=(rows // t,),
                                               in_specs=[spec, half_spec, half_spec, spec, spec], out_specs=[spec] * 4),
        out_shape=[jax.ShapeDtypeStruct(w.shape, F32)] * 4, compiler_params=_params(("parallel",)),
    )(core, w, mine, theirs, m, v)


def _adamw_small(ws, gs, ms, vs):
    n = len(ws)

    def body(*refs):
        for t in range(n):
            w_ref, g_ref, m_ref, v_ref = (refs[k * n + t] for k in range(4))
            d, nm, nv = _adam_math(w_ref[...], g_ref[...], m_ref[...], v_ref[...])
            refs[4 * n + t][...] = d
            refs[5 * n + t][...] = nm
            refs[6 * n + t][...] = nv

    vmem = pl.BlockSpec(memory_space=pltpu.VMEM)
    res = pl.pallas_call(body, name="adamw_small", in_specs=[vmem] * (4 * n), out_specs=[vmem] * (3 * n),
                         out_shape=[jax.ShapeDtypeStruct(a.shape, F32) for a in ws] * 3)(*ws, *gs, *ms, *vs)
    return res[:n], res[n:2 * n], res[2 * n:]


def kernel(x, w_in, b_gate, g_q_a, w_uq, g_kv_a, w_ukv, w_o_mla, w_o_dil, w_out, ln1_g, ln1_b, w_ff1, w_ff2, ln2_g, ln2_b, loss_target, m_w_in, m_b_gate, m_g_q_a, m_w_uq, m_g_kv_a, m_w_ukv, m_w_o_mla, m_w_o_dil, m_w_out, m_ln1_g, m_ln1_b, m_w_ff1, m_w_ff2, m_ln2_g, m_ln2_b, v_w_in, v_b_gate, v_g_q_a, v_w_uq, v_g_kv_a, v_w_ukv, v_w_o_mla, v_w_o_dil, v_w_out, v_ln1_g, v_ln1_b, v_w_ff1, v_w_ff2, v_ln2_g, v_ln2_b):
    order = ("w_in", "b_gate", "g_q_a", "w_uq", "g_kv_a", "w_ukv", "w_o_mla", "w_o_dil", "w_out", "ln1_g", "ln1_b", "w_ff1", "w_ff2", "ln2_g", "ln2_b")
    w = dict(w_in=w_in, b_gate=b_gate, g_q_a=g_q_a, w_uq=w_uq, g_kv_a=g_kv_a, w_ukv=w_ukv, w_o_mla=w_o_mla, w_o_dil=w_o_dil, w_out=w_out,
             ln1_g=ln1_g, ln1_b=ln1_b, w_ff1=w_ff1, w_ff2=w_ff2, ln2_g=ln2_g, ln2_b=ln2_b)
    m = dict(w_in=m_w_in, b_gate=m_b_gate, g_q_a=m_g_q_a, w_uq=m_w_uq, g_kv_a=m_g_kv_a, w_ukv=m_w_ukv, w_o_mla=m_w_o_mla, w_o_dil=m_w_o_dil,
             w_out=m_w_out, ln1_g=m_ln1_g, ln1_b=m_ln1_b, w_ff1=m_w_ff1, w_ff2=m_w_ff2, ln2_g=m_ln2_g, ln2_b=m_ln2_b)
    v = dict(w_in=v_w_in, b_gate=v_b_gate, g_q_a=v_g_q_a, w_uq=v_w_uq, g_kv_a=v_g_kv_a, w_ukv=v_w_ukv, w_o_mla=v_w_o_mla, w_o_dil=v_w_o_dil,
             w_out=v_w_out, ln1_g=v_ln1_g, ln1_b=v_ln1_b, w_ff1=v_w_ff1, w_ff2=v_w_ff2, ln2_g=v_ln2_g, ln2_b=v_ln2_b)
    chip = 2 * lax.axis_index("x") + lax.axis_index("y")
    south = (lax.axis_index("c") == 0).astype(F32)
    gate_w = D_MODEL // N_CHIPS

    core = lax.axis_index("c").astype(jnp.int32).reshape(1)
    turn = lambda n, a: a.T if n in TRANSPOSED else a
    shards = {n: turn(n, w[n][0]).astype(BF16) for n in BIG}
    early_shards, late_shards = [shards[n] for n in EARLY], [shards[n] for n in LATE]
    gathered = lambda group: [jax.ShapeDtypeStruct((N_CHIPS,) + s.shape, BF16) for s in group]
    e_sems, e_srcs, e_lands, e_token = _split_start("gather", early_shards, gathered(early_shards), "gather_early_start")
    g_sems, g_srcs, g_lands, g_token = _split_start("gather", late_shards, gathered(late_shards), "gather_late_start", after=e_token)
    tables = _attention_tables(x.shape[1])
    e_srcs, e_lands = _split_wait("gather", e_sems, e_srcs, e_lands, tables[1][0][1], "gather_early_wait")
    e_forward = _split_start("forward", e_srcs, None, "gather_early_forward_start", lands=e_lands)
    first = dict(zip(EARLY, _split_wait("forward", *e_forward[:3], e_forward[-1], "gather_early_forward_wait")[1]))
    b_mine = lax.dynamic_update_slice(jnp.zeros((2, D_MODEL), F32), b_gate[0] * south, (0, chip * gate_w))
    b_full = _sum_all_devices(b_mine.reshape(-1, LANES), "gather_b_gate").reshape(2, D_MODEL)

    sent = {}

    def late_arrived(after):
        srcs, lands = _split_wait("gather", g_sems, g_srcs, g_lands, after, "gather_late_wait")
        sent["forward"] = _split_start("forward", srcs, None, "gather_late_forward_start", lands=lands)
        return sent["forward"][-1]

    def late_weights(after):
        return dict(zip(LATE, _split_wait("forward", *sent["forward"][:3], after, "gather_late_forward_wait")[1]))

    exchange_shapes = lambda parts: [jax.ShapeDtypeStruct((3,) + p.shape[1:], BF16) for p in parts]

    def early_grads(grads_late):
        gs = [grads_late[n] for n in LATE]
        shapes = [jax.ShapeDtypeStruct((N_CHIPS,) + _half_shape(g.shape[1:]), F32) for g in gs]
        sent["pair"] = _split_start("pair", gs, shapes, "pair_split_late_start")
        return sent["pair"][-1]

    def early_grads_go(after):
        gs, theirs = _split_wait("pair", *sent["pair"][:3], after, "pair_split_late_wait")
        parts = [_pair_add(g, th, core, "pair_add_" + n) for g, th, n in zip(gs, theirs, LATE)]
        sent["late"] = _split_start("scatter", parts, exchange_shapes(parts), "exchange_late_start")
        return sent["late"][-1]

    def last_grads(grads_early):
        parts = _chip_partials(grads_early, EARLY, core, "early")
        sent["early"] = _split_start("scatter", parts, exchange_shapes(parts), "exchange_early_start")
        return sent["early"][-1]

    loss_part, grad_x, grads = _local_step(x, loss_target, first, b_full, g_q_a, g_kv_a, ln1_g, ln1_b, ln2_g, ln2_b, token=g_token,
                                           late_arrived=late_arrived, late_weights=late_weights, early_grads=early_grads, early_grads_go=early_grads_go,
                                           last_grads=last_grads, tables=tables)

    g_out, delta, new_m, new_v = {}, {}, {}, {}
    chip1 = chip.astype(jnp.int32).reshape(1)

    def sum_and_send(names, parts, others, tag):
        totals = [_chip_sum(p, o, chip1, "chip_sum_" + n) for n, p, o in zip(names, parts, others)]
        return _split_start("join", totals, [jax.ShapeDtypeStruct(t.shape, F32) for t in totals], "pair_join_" + tag + "_start")

    def adam(names, joined, after, tag):
        totals, halves = _split_wait("join", *joined[:3], after, "pair_join_" + tag + "_wait")
        for n, mine, theirs in zip(names, totals, halves):
            res = _adamw_big(turn(n, w[n][0]), mine, theirs, turn(n, m[n][0]), turn(n, v[n][0]), core, "adamw_" + n,
                             side_by_side=mine.shape[0] == shards[n].shape[0])
            g_out[n], delta[n], new_m[n], new_v[n] = (turn(n, r) for r in res)

    late_joined = sum_and_send(LATE, *_split_wait("scatter", *sent["late"][:3], grad_x, "exchange_late_wait"), "late")
    early_joined = sum_and_send(EARLY, *_split_wait("scatter", *sent["early"][:3], late_joined[-1], "exchange_early_wait"), "early")
    small_names = [name for name, _ in SMALL]
    sums = _sum_small([grads[name] for name in small_names] + [loss_part])
    loss = sums[-1][0, 0]
    g_small = dict(zip(small_names, sums))
    g_small["b_gate"] = lax.dynamic_slice(g_small["b_gate"], (0, chip * gate_w), (2, gate_w))
    flat = lambda a: a.reshape(-1, a.shape[-1])
    res = _adamw_small(*[[flat(d[name]) for name in small_names] for d in (w, g_small, m, v)])
    g_out.update(g_small)
    for d, r in zip((delta, new_m, new_v), res):
        d.update(zip(small_names, r))
    adam(LATE, late_joined, res[0][0], "late")
    adam(EARLY, early_joined, delta[LATE[-1]], "early")

    lead = lambda d: [d[name].reshape(w[name].shape) for name in order]
    return (loss, grad_x, *lead(g_out), *lead(delta), *lead(new_m), *lead(new_v))
```

```python
import functools
import math

import jax
import jax.numpy as jnp
import numpy as np
from jax import lax
from jax.experimental import pallas as pl
from jax.experimental.pallas import tpu as pltpu

F32 = jnp.float32
BF16 = jnp.bfloat16
MESH = pl.DeviceIdType.MESH

D_MODEL = 1024
N_HEADS = 8
LANES = 128
NOPE, ROPE, V_DIM = 64, 32, 64
MLA_QK = NOPE + ROPE
Q_LORA, KV_LORA = 384, 256
DIL_DIM = 64
DIL_PATTERNS = ((128, 1), (512, 4), (2048, 16))
D_FF = 4096
N_CHIPS = 4
N_DEV = 8
IN_WIDTH = 4256
LN_EPS, RMS_EPS = 1e-5, 1e-6
NEG = -1e30
LOG2E, LN2 = 1.4426950408889634, 0.6931471805599453
ALPHA = 2.0 ** 0.25
ROPE_THETA = 10000.0
LR, B1, B2, ADAM_EPS, WD, ADAM_STEP = 0.001, 0.9, 0.999, 1e-8, 0.01, 10

P_LORA, P_KR, P_GATE, P_HALF = 0, 640, 1024, 3072
DIL_GROUP = 4 * LANES
P_DIL = N_HEADS // 2 * DIL_GROUP
LORA_W = Q_LORA + KV_LORA
KR_LANE = NOPE

ATT_T = 512
ROW_T = 512
VMEM_LIMIT = 56 * 1024 * 1024

NN = (((1,), (0,)), ((), ()))
NT = (((1,), (1,)), ((), ()))
TN = (((0,), (0,)), ((), ()))


def _params(sem=None, **kw):
    return pltpu.CompilerParams(dimension_semantics=sem, vmem_limit_bytes=VMEM_LIMIT, **kw)


def _matmul(a, b, *, mode, name, tm, tn, tk, out_dtypes=(F32,), extras=(), epilogue=None, b_shards=False, out_shards=False, after=None):
    pieces = list(a) if isinstance(a, (list, tuple)) else [a]
    n_pc = len(pieces)
    a_shape = (pieces[0].shape[0], sum(p.shape[1] for p in pieces))
    if b_shards:
        n_sh, rows_b, cols_b = b.shape
        b_shape = (rows_b, n_sh * cols_b)
    else:
        b_shape = b.shape
    if mode == "nn":
        (m, k), (k2, n) = a_shape, b_shape
    elif mode == "nt":
        (m, k), (n, k2) = a_shape, b_shape
    else:
        (k, m), (k2, n) = a_shape, b_shape
    assert k == k2, (a_shape, b.shape, mode)
    tm, tn, tk = min(tm, m), min(tn, n), min(tk, k)
    assert m % tm == 0 and n % tn == 0 and k % tk == 0, (name, m, n, k, tm, tn, tk)
    nk = k // tk
    n_ex, n_out = len(extras), len(out_dtypes)
    n_in = n_pc + 1 + n_ex + (after is not None)
    dims = {"nn": NN, "nt": NT, "tn": TN}[mode]
    col_tile = tm if mode == "tn" else tk
    blocks = [p.shape[1] // col_tile for p in pieces]
    firsts = [sum(blocks[:p]) for p in range(n_pc)]
    assert all(p.shape[1] % col_tile == 0 for p in pieces), (name, col_tile)

    def body(*refs):
        a_refs, b_ref = refs[:n_pc], refs[n_pc]
        ex_refs = refs[n_pc + 1:n_pc + 1 + n_ex]
        out_refs = refs[n_in:n_in + n_out]

        def finish(acc):
            outs = epilogue(acc, *[r[...] for r in ex_refs]) if epilogue is not None else (acc,)
            for r, o in zip(out_refs, outs):
                r[...] = o.astype(r.dtype)

        kk = pl.program_id(2)

        def step(a_ref):
            part = lax.dot_general(a_ref[...].astype(BF16), b_ref[...].astype(BF16), dims, preferred_element_type=F32)
            if nk == 1:
                finish(part)
                return
            acc_ref = refs[-1]

            @pl.when(kk == 0)
            def _():
                acc_ref[...] = part

            @pl.when(kk > 0)
            def _():
                acc_ref[...] += part

            @pl.when(kk == nk - 1)
            def _():
                finish(acc_ref[...])

        if n_pc == 1:
            step(a_refs[0])
        else:
            at = pl.program_id(0) if mode == "tn" else kk
            for p in range(n_pc):
                pl.when(jnp.logical_and(at >= firsts[p], at < firsts[p] + blocks[p]))(functools.partial(step, a_refs[p]))

    def a_spec_of(p):
        if n_pc == 1:
            return pl.BlockSpec((tk, tm), lambda i, j, kk: (kk, i)) if mode == "tn" else pl.BlockSpec((tm, tk), lambda i, j, kk: (i, kk))
        col = lambda at: jnp.clip(at - firsts[p], 0, blocks[p] - 1)
        mine = lambda at: jnp.logical_and(at >= firsts[p], at < firsts[p] + blocks[p])
        if mode == "tn":
            return pl.BlockSpec((tk, tm), lambda i, j, kk: (jnp.where(mine(i), kk, 0), col(i)))
        return pl.BlockSpec((tm, tk), lambda i, j, kk: (i, col(kk)))

    b_spec = {"nn": pl.BlockSpec((tk, tn), lambda i, j, kk: (kk, j)),
              "nt": pl.BlockSpec((tn, tk), lambda i, j, kk: (j, kk)),
              "tn": pl.BlockSpec((tk, tn), lambda i, j, kk: (kk, j))}[mode]
    tile = pl.BlockSpec((tm, tn), lambda i, j, kk: (i, j))
    out_spec, out_dims = tile, (m, n)
    if b_shards and mode == "nn":
        per = cols_b // tn
        b_spec = pl.BlockSpec((None, tk, tn), lambda i, j, kk: (j // per, kk, j % per))
    elif b_shards:
        assert mode == "nt"
        per = cols_b // tk
        b_spec = pl.BlockSpec((None, tn, tk), lambda i, j, kk: (kk // per, j, kk % per))
    if out_shards:
        assert not extras and epilogue is None
        per_out = n // N_CHIPS // tn
        out_spec = pl.BlockSpec((None, tm, tn), lambda i, j, kk: (j // per_out, i, j % per_out))
        out_dims = (N_CHIPS, m, n // N_CHIPS)
    outs = pl.pallas_call(
        body, name=name,
        grid=(m // tm, n // tn, nk),
        in_specs=[a_spec_of(p) for p in range(n_pc)] + [b_spec] + [tile] * n_ex + [pl.BlockSpec(memory_space=pl.ANY)] * (after is not None),
        out_specs=[out_spec] * n_out,
        out_shape=[jax.ShapeDtypeStruct(out_dims, dt) for dt in out_dtypes],
        scratch_shapes=[pltpu.VMEM((tm, tn), F32)] if nk > 1 else [],
        compiler_params=_params(("parallel", "parallel", "arbitrary")),
    )(*pieces, b, *extras, *([after] if after is not None else []))
    return outs[0] if n_out == 1 else outs


def _rowwise(fn, *, name, rows, seq, ins, outs, sums=()):
    tm = min(ROW_T, seq)
    n_pos = seq // tm
    n_in, n_out, n_sum = len(ins), len(outs), len(sums)

    def body(*refs):
        vals = fn(*[r[...] for r in refs[:n_in]])
        for r, v in zip(refs[n_in:n_in + n_out], vals[:n_out]):
            r[...] = v.astype(r.dtype)
        first = pl.program_id(0) == 0
        for r, v in zip(refs[n_in + n_out:], vals[n_out:]):
            @pl.when(first)
            def _(r=r, v=v):
                r[...] = v

            @pl.when(jnp.logical_not(first))
            def _(r=r, v=v):
                r[...] += v

    def spec(arr, width, col, kind):
        if kind == "row":
            return pl.BlockSpec((tm, width), lambda i, col=col: (i, col))
        if kind == "pos":
            return pl.BlockSpec((tm, width), lambda i, col=col: (i % n_pos, col))
        return pl.BlockSpec(arr.shape, lambda i: (0,) * arr.ndim)

    res = pl.pallas_call(
        body, name=name,
        grid=(rows // tm,),
        in_specs=[spec(*t) for t in ins],
        out_specs=[pl.BlockSpec((tm, w), lambda i: (i, 0)) for w, _ in outs]
        + [pl.BlockSpec((1, w), lambda i: (0, 0)) for w in sums],
        out_shape=[jax.ShapeDtypeStruct((rows, w), dt) for w, dt in outs]
        + [jax.ShapeDtypeStruct((1, w), F32) for w in sums],
        compiler_params=_params(("arbitrary",)),
    )(*[t[0] for t in ins])
    return res


def _colsum(v):
    return jnp.sum(v, axis=0, keepdims=True)


def _rope_fwd(t, c, s_up, s_dn):
    return t * c + pltpu.roll(t, LANES - 16, 1) * s_up + pltpu.roll(t, 16, 1) * s_dn


def _rope_bwd(d, c, s_up, s_dn):
    return d * c + pltpu.roll(d * s_up, 16, 1) + pltpu.roll(d * s_dn, LANES - 16, 1)


def _rope_tables(seq):
    half = ROPE // 2
    inv = jnp.power(ROPE_THETA, -jnp.arange(half, dtype=F32) / half)
    ang = jnp.arange(seq, dtype=F32)[:, None] * inv[None, :]
    cos, sin = jnp.cos(ang), jnp.sin(ang)
    zeros = jnp.zeros((seq, half), F32)
    lo, hi = jnp.ones((seq, KR_LANE), F32), jnp.ones((seq, LANES - KR_LANE - ROPE), F32)
    c = jnp.concatenate([lo, cos, cos, hi], axis=1)
    c_rope_only = jnp.concatenate([0 * lo, cos, cos, 0 * hi], axis=1)
    s_up = jnp.concatenate([0 * lo, -sin, zeros, 0 * hi], axis=1)
    s_dn = jnp.concatenate([0 * lo, zeros, sin, 0 * hi], axis=1)
    return c, s_up, s_dn, c_rope_only


def _rms(x, g):
    r = lax.rsqrt(jnp.mean(x * x, axis=1, keepdims=True) + RMS_EPS)
    return x * r * g


def _rms_bwd(x, g, dy):
    r = lax.rsqrt(jnp.mean(x * x, axis=1, keepdims=True) + RMS_EPS)
    xh = x * r
    dxh = dy * g
    dx = r * (dxh - xh * jnp.mean(dxh * xh, axis=1, keepdims=True))
    return dx, _colsum(dy * xh)


def _ln_stats(x):
    mu = jnp.mean(x, axis=1, keepdims=True)
    xc = x - mu
    r = lax.rsqrt(jnp.mean(xc * xc, axis=1, keepdims=True) + LN_EPS)
    return xc * r, r


def _ln_bwd(xh, r, g, dy):
    dxh = dy * g
    dx = r * (dxh - jnp.mean(dxh, axis=1, keepdims=True) - xh * jnp.mean(dxh * xh, axis=1, keepdims=True))
    return dx, _colsum(dy * xh), _colsum(dy)


def _table_specs(tables, sub):
    whole = lambda a: pl.BlockSpec(a.shape, lambda b, g: (0,) * a.ndim)
    if len(tables) == 1:
        return [whole(tables[0])]
    return [whole(tables[0]), whole(tables[1]), pl.BlockSpec((sub, 1, LANES), lambda b, g: (g, 0, 0))]


def _biased(s, table_refs, delta, head):
    if delta < table_refs[0].shape[0]:
        s = s + table_refs[0][delta]
    if len(table_refs) == 3:
        s = s - table_refs[2][head, 0:1, 0:1] * table_refs[1][delta]
    return s


def _lane_masks(sub):
    lane = lax.broadcasted_iota(jnp.int32, (1, LANES), 1)
    return [(lane // (LANES // sub) == a).astype(F32) for a in range(sub)]


def _attn_fwd(q, qb0, k, kb0, v, vb0, tables, scale, *, name, batch, seq, sub=1, stride=1, wide_qk=False, after=None):
    t = ATT_T
    nq = seq // t
    rows = batch * seq
    n_tab = len(tables)
    qk_w = sub * LANES if wide_qk else LANES

    def body(q_ref, k_ref, v_ref, *rest):
        table_refs = rest[:n_tab]
        o_ref, lse_ref, vtb = rest[n_tab + (after is not None):][:3]
        per_head = rest[n_tab + (after is not None) + 3:]
        qbs, kbs = per_head[:sub], per_head[sub:]
        masks = _lane_masks(sub)
        for a in range(sub):
            lanes = slice(a * LANES, (a + 1) * LANES) if wide_qk else slice(None)
            qa = q_ref[:, lanes]
            qbs[a][...] = (qa.astype(F32) * masks[a]).astype(BF16) if sub > 1 and not wide_qk else qa.astype(BF16)
            if wide_qk or a == 0:
                kbs[a][...] = k_ref[:, lanes].astype(BF16)
        vtb[...] = v_ref[...].astype(F32).T.astype(BF16)
        for i in range(nq):
            out_t = None
            for a in range(sub):
                qt, kb = qbs[a][i * t:(i + 1) * t, :], kbs[a if wide_qk else 0]
                logits = [_biased(lax.dot_general(kb[j * t:(j + 1) * t, :], qt, NT, preferred_element_type=F32) * (scale * LOG2E), table_refs, i - j, a)
                          for j in range(i + 1)]
                m = jnp.max(functools.reduce(jnp.maximum, logits), axis=0, keepdims=True)
                ps = [jnp.exp2(s - m) for s in logits]
                l = jnp.sum(functools.reduce(jnp.add, ps), axis=0, keepdims=True)
                acc = functools.reduce(jnp.add, [lax.dot_general(vtb[:, j * t:(j + 1) * t], p.astype(BF16), NN, preferred_element_type=F32)
                                                 for j, p in enumerate(ps)])
                part = acc / l if sub == 1 else (acc / l) * masks[a].T
                out_t = part if out_t is None else out_t + part
                lse_ref[i * t:(i + 1) * t, a * LANES:(a + 1) * LANES] = jnp.broadcast_to((m + jnp.log2(l)) * LN2, (LANES, t)).T
            o_ref[i * t:(i + 1) * t, :] = out_t.T

    slab = lambda b0, step, width=LANES: pl.BlockSpec((seq, width), lambda b, g: (b, b0 + step * g))
    groups = N_HEADS // sub
    n_k = sub if wide_qk else 1
    return pl.pallas_call(
        body, name=name,
        grid=(batch, groups),
        in_specs=[slab(qb0, stride, qk_w), slab(kb0, stride, qk_w), slab(vb0, stride)] + _table_specs(tables, sub)
        + [pl.BlockSpec(memory_space=pl.ANY)] * (after is not None),
        out_specs=[slab(0, 1), slab(0, 1, sub * LANES)],
        out_shape=[jax.ShapeDtypeStruct((rows, groups * LANES), F32), jax.ShapeDtypeStruct((rows, N_HEADS * LANES), F32)],
        scratch_shapes=[pltpu.VMEM((LANES, seq), BF16)] + [pltpu.VMEM((seq, LANES), BF16)] * (sub + n_k),
        compiler_params=_params(("arbitrary", "arbitrary")),
    )(q, k, v, *tables, *([after] if after is not None else []))


def _attn_bwd(q, qb0, k, kb0, v, vb0, o, do, lse, tables, scale, *, name, batch, seq, out_dtype, sub=1, stride=1, wide_qk=False, after=None):
    t = ATT_T
    nq = seq // t
    rows = batch * seq
    n_tab = len(tables)
    groups = N_HEADS // sub
    packed = sub > 1 and not wide_qk
    n_out = 1 if packed else 3
    n_k = sub if wide_qk else 1
    qk_w = sub * LANES if wide_qk else LANES

    def body(q_ref, k_ref, v_ref, o_ref, do_ref, lse_ref, *rest):
        table_refs = rest[:n_tab]
        rest = rest[n_tab + (after is not None):]
        out_refs, (vb, dva), rest = rest[:n_out], rest[n_out:n_out + 2], rest[n_out + 2:]
        kbs, dkas, rest = rest[:n_k], rest[n_k:2 * n_k], rest[2 * n_k:]
        qbs, dobs, qtbs, dotbs = (rest[g * sub:(g + 1) * sub] for g in range(4))
        masks = _lane_masks(sub)
        vb[...] = v_ref[...].astype(BF16)
        for a in range(sub):
            lanes = slice(a * LANES, (a + 1) * LANES) if wide_qk else slice(None)
            qa = q_ref[:, lanes].astype(F32) * masks[a] if packed else q_ref[:, lanes].astype(F32)
            doa = do_ref[...] * masks[a] if sub > 1 else do_ref[...]
            qbs[a][...] = qa.astype(BF16)
            dobs[a][...] = doa.astype(BF16)
            qtbs[a][...] = qa.T.astype(BF16)
            dotbs[a][...] = doa.T.astype(BF16)
            if wide_qk or a == 0:
                kbs[a][...] = k_ref[:, lanes].astype(BF16)
        first_k, first_v = [[True] * nq for _ in range(n_k)], [True] * nq
        for i in range(nq):
            at = slice(i * t, (i + 1) * t)
            dq_all = None
            for a in range(sub):
                qt, dot, kb, dka = qbs[a][at, :], dobs[a][at, :], kbs[a if wide_qk else 0], dkas[a if wide_qk else 0]
                lse_t = lse_ref[at, a * LANES:a * LANES + 1] * LOG2E
                od = o_ref[at, :] * do_ref[at, :]
                delta = jnp.sum(od * masks[a] if sub > 1 else od, axis=1, keepdims=True)
                dq = None
                for j in range(i + 1):
                    kat = slice(j * t, (j + 1) * t)
                    kt, vt = kb[kat, :], vb[kat, :]
                    p = jnp.exp2(_biased(lax.dot_general(qt, kt, NT, preferred_element_type=F32) * (scale * LOG2E), table_refs, i - j, a) - lse_t)
                    dp = lax.dot_general(dot, vt, NT, preferred_element_type=F32)
                    ds = (p * (dp - delta) * scale).astype(BF16)
                    dk_part = lax.dot_general(qtbs[a][:, at], ds, NN, preferred_element_type=F32)
                    dv_part = lax.dot_general(dotbs[a][:, at], p.astype(BF16), NN, preferred_element_type=F32)
                    firsts = first_k[a if wide_qk else 0]
                    if firsts[j]:
                        dka[:, kat] = dk_part
                        firsts[j] = False
                    else:
                        dka[:, kat] += dk_part
                    if first_v[j]:
                        dva[:, kat] = dv_part
                        first_v[j] = False
                    else:
                        dva[:, kat] += dv_part
                    dq_part = lax.dot_general(ds, kt, NN, preferred_element_type=F32)
                    dq = dq_part if dq is None else dq + dq_part
                if wide_qk:
                    out_refs[0][at, a * LANES:(a + 1) * LANES] = dq.astype(out_refs[0].dtype)
                else:
                    dq = dq * masks[a] if sub > 1 else dq
                    dq_all = dq if dq_all is None else dq_all + dq
            if not wide_qk:
                out_refs[0][at, 0:LANES] = dq_all.astype(out_refs[0].dtype)
        if packed:
            out_refs[0][:, LANES:2 * LANES] = dkas[0][...].T.astype(out_refs[0].dtype)
            out_refs[0][:, 2 * LANES:3 * LANES] = dva[...].T.astype(out_refs[0].dtype)
            out_refs[0][:, 3 * LANES:] = jnp.zeros((seq, LANES), out_refs[0].dtype)
        else:
            for a in range(n_k):
                out_refs[1][:, a * LANES:(a + 1) * LANES] = dkas[a][...].T.astype(out_refs[1].dtype)
            out_refs[2][...] = dva[...].T.astype(out_refs[2].dtype)

    slab = lambda b0, step, width=LANES: pl.BlockSpec((seq, width), lambda b, g: (b, b0 + step * g))
    if packed:
        out_specs = [slab(0, 1, 4 * LANES)]
        out_shape = [jax.ShapeDtypeStruct((rows, groups * 4 * LANES), out_dtype)]
    else:
        out_specs = [slab(0, 1, qk_w), slab(0, 1, qk_w), slab(0, 1)]
        out_shape = [jax.ShapeDtypeStruct((rows, N_HEADS * LANES), out_dtype)] * 2 + [jax.ShapeDtypeStruct((rows, groups * LANES), out_dtype)]
    res = pl.pallas_call(
        body, name=name,
        grid=(batch, groups),
        in_specs=[slab(qb0, stride, qk_w), slab(kb0, stride, qk_w), slab(vb0, stride), slab(0, 1), slab(0, 1), slab(0, 1, sub * LANES)]
        + _table_specs(tables, sub) + [pl.BlockSpec(memory_space=pl.ANY)] * (after is not None),
        out_specs=out_specs, out_shape=out_shape,
        scratch_shapes=[pltpu.VMEM((seq, LANES), BF16), pltpu.VMEM((LANES, seq), F32)]
        + [pltpu.VMEM((seq, LANES), BF16)] * n_k + [pltpu.VMEM((LANES, seq), F32)] * n_k
        + [pltpu.VMEM((seq, LANES), BF16)] * (2 * sub) + [pltpu.VMEM((LANES, seq), BF16)] * (2 * sub),
        compiler_params=_params(("arbitrary", "arbitrary")),
    )(q, k, v, o, do, lse, *tables, *([after] if after is not None else []))
    return res[0] if packed else res


def _attention_tables(seq):
    n = seq // ATT_T
    pos = np.arange(ATT_T, dtype=np.int32)
    dist = np.arange(n, dtype=np.int32)[:, None, None] * ATT_T + pos[None, :, None] - pos[None, None, :]
    causal = np.where(dist[:1] >= 0, 0.0, NEG).astype(np.float32)
    count = np.zeros(dist.shape, np.float32)
    for window, dilation in DIL_PATTERNS:
        count += ((dist >= 0) & (dist <= window) & (dist % dilation == 0)).astype(np.float32)
    held = np.where(count > 0, np.log2(np.maximum(count, 1.0)), NEG).astype(np.float32)
    far = dist.astype(np.float32) * np.float32(LOG2E)
    slopes = np.asarray([2.0 ** (-8.0 * (i + 1) / N_HEADS) for i in range(N_HEADS)], np.float32)
    slopes = jnp.asarray(np.broadcast_to(slopes[:, None, None], (N_HEADS, 1, LANES)))
    flat = lambda a: jnp.asarray(np.ascontiguousarray(a))
    turned = lambda a: flat(np.swapaxes(a, 1, 2))
    return ((flat(causal),), (turned(causal),)), ((flat(held), flat(far), slopes), (turned(held), turned(far), slopes))


def _pad_heads(w, width):
    kdim, n = w.shape[0], w.shape[1] // width
    return jnp.pad(w.reshape(kdim, n, width), ((0, 0), (0, 0), (0, LANES - width))).reshape(kdim, n * LANES)


def _unpad_heads(w, width):
    kdim, n = w.shape[0], w.shape[1] // LANES
    return w.reshape(kdim, n, LANES)[:, :, :width].reshape(kdim, n * width)


def _pad_head_rows(w, width):
    n, kdim = w.shape[0] // width, w.shape[1]
    return jnp.pad(w.reshape(n, width, kdim), ((0, 0), (0, LANES - width), (0, 0))).reshape(n * LANES, kdim)


def _unpad_head_rows(w, width):
    n, kdim = w.shape[0] // LANES, w.shape[1]
    return w.reshape(n, LANES, kdim)[:, :width].reshape(n * width, kdim)


def _pad_w_in_t(wt):
    n_qkv, pair = 3 * N_HEADS * DIL_DIM, 2 * DIL_DIM
    zeros = lambda n: jnp.zeros((n, wt.shape[1]), wt.dtype)
    main = jnp.concatenate([wt[:LORA_W], zeros(KR_LANE), wt[LORA_W:LORA_W + ROPE], zeros(P_GATE - P_KR - KR_LANE - ROPE),
                            wt[LORA_W + ROPE + n_qkv:]], axis=0)
    qkv = wt[LORA_W + ROPE:LORA_W + ROPE + n_qkv].reshape(3, N_HEADS // 2, pair, wt.shape[1]).transpose(1, 0, 2, 3)
    dil = jnp.pad(qkv, ((0, 0), (0, 1), (0, 0), (0, 0))).reshape(P_DIL, wt.shape[1])
    return main, dil


def _unpad_w_in_t(gt):
    qkv = gt[P_HALF:].reshape(N_HEADS // 2, 4, 2 * DIL_DIM, gt.shape[1])[:, :3].transpose(1, 0, 2, 3).reshape(3 * N_HEADS * DIL_DIM, gt.shape[1])
    return jnp.concatenate([gt[P_LORA:P_KR], gt[P_KR + KR_LANE:P_KR + KR_LANE + ROPE], qkv, gt[P_GATE:P_HALF]], axis=0)


def _split_ukv(w):
    w3 = w.reshape(w.shape[0], N_HEADS, NOPE + V_DIM)
    return _pad_heads(w3[:, :, :NOPE].reshape(w.shape[0], -1), NOPE), w3[:, :, NOPE:].reshape(w.shape[0], -1)


def _merge_ukv(g_k, g_v):
    kdim = g_k.shape[0]
    k3 = _unpad_heads(g_k, NOPE).reshape(kdim, N_HEADS, NOPE)
    return jnp.concatenate([k3, g_v.reshape(kdim, N_HEADS, V_DIM)], axis=2).reshape(kdim, N_HEADS * (NOPE + V_DIM))


def _join_cols(w):
    return w.transpose(1, 0, 2).reshape(w.shape[1], N_CHIPS * w.shape[2])


def _split_cols(g):
    return g.reshape(g.shape[0], N_CHIPS, g.shape[1] // N_CHIPS).transpose(1, 0, 2)


def _local_step(x3, target3, wg, b_gate, g_q_a, g_kv_a, ln1_g, ln1_b, ln2_g, ln2_b, token=None, late_arrived=None, late_weights=None,
                early_grads=None, early_grads_go=None, last_grads=None, tables=None):
    w_main_t, w_dil_t = _pad_w_in_t(wg["w_in"].reshape(IN_WIDTH, D_MODEL))
    w_uq_pt = _pad_head_rows(wg["w_uq"].reshape(N_HEADS * MLA_QK, Q_LORA), MLA_QK)
    w_ukv = _join_cols(wg["w_ukv"])
    batch, seq, _ = x3.shape
    rows = batch * seq
    x = x3.reshape(rows, D_MODEL)
    target = target3.reshape(rows, D_MODEL)
    row = functools.partial(_rowwise, rows=rows, seq=seq)
    mm = _matmul

    w_uk_p, w_uv = _split_ukv(w_ukv)
    b0, b1 = b_gate[0:1], b_gate[1:2]
    rope_c, rope_up, rope_dn, rope_c_only = _rope_tables(seq)
    (mla_bwd_tables, mla_fwd_tables), (dil_bwd_tables, dil_fwd_tables) = _attention_tables(seq) if tables is None else tables
    scale_mla, scale_dil = MLA_QK ** -0.5, DIL_DIM ** -0.5
    lora0, kr0, gate0 = P_LORA // LORA_W, P_KR // LANES, P_GATE // D_MODEL

    proj = mm(x, w_main_t, mode="nt", name="proj", tm=1024, tn=1536, tk=1024, after=token)
    proj_d = mm(x, w_dil_t, mode="nt", name="proj_dil", tm=1024, tn=1024, tk=1024, out_dtypes=(BF16,))

    def prep(lora, gq, gkv):
        return _rms(lora[:, :Q_LORA], gq), _rms(lora[:, Q_LORA:], gkv)

    qn, kvn = row(prep, name="mla_rms", ins=[(proj, LORA_W, lora0, "row"), (g_q_a, 0, 0, "full"), (g_kv_a, 0, 0, "full")],
                  outs=[(Q_LORA, BF16), (KV_LORA, BF16)])
    q_lin = mm(qn, w_uq_pt, mode="nt", name="q_up", tm=1024, tn=1024, tk=Q_LORA)
    k_lin = mm(kvn, w_uk_p, mode="nn", name="k_up", tm=1024, tn=1024, tk=KV_LORA)
    v_a = mm(kvn, w_uv, mode="nn", name="v_up", tm=1024, tn=1024, tk=KV_LORA, out_dtypes=(BF16,))

    def rope_qk(ql, kl, kr, c, up, dn):
        k_rot = _rope_fwd(kr, c, up, dn)
        qs = [_rope_fwd(ql[:, h * LANES:(h + 1) * LANES], c, up, dn) for h in range(N_HEADS)]
        ks = [kl[:, h * LANES:(h + 1) * LANES] + k_rot for h in range(N_HEADS)]
        return jnp.concatenate(qs, axis=1), jnp.concatenate(ks, axis=1)

    pos = lambda tab: (tab, LANES, 0, "pos")
    q_a, k_a = row(rope_qk, name="rope_qk",
                   ins=[(q_lin, D_MODEL, 0, "row"), (k_lin, D_MODEL, 0, "row"), (proj, LANES, kr0, "row"), pos(rope_c), pos(rope_up), pos(rope_dn)],
                   outs=[(N_HEADS * LANES, BF16), (N_HEADS * LANES, BF16)])
    o_a, lse_a = _attn_fwd(q_a, 0, k_a, 0, v_a, 0, mla_fwd_tables, scale_mla, name="mla_fwd", batch=batch, seq=seq, sub=2, wide_qk=True)
    arrived = None if late_arrived is None else late_arrived(o_a)
    o_b, lse_b = _attn_fwd(proj_d, 0, proj_d, 1, proj_d, 2, dil_fwd_tables, scale_dil, name="dil_fwd", batch=batch, seq=seq, sub=2, stride=4, after=arrived)
    late = wg if late_weights is None else late_weights(o_b)
    w_oa = _join_cols(late["w_o_mla"])
    w_ob = _join_cols(late["w_o_dil"])
    w_out, w_ff1, w_ff2 = late["w_out"].reshape(D_MODEL, D_MODEL), late["w_ff1"], late["w_ff2"].reshape(D_FF, D_MODEL)
    y_a = mm(o_a, w_oa, mode="nn", name="o_mla", tm=1024, tn=1024, tk=1024, out_dtypes=(BF16,))
    y_b = mm(o_b, w_ob, mode="nn", name="o_dil", tm=1024, tn=1024, tk=1024, out_dtypes=(BF16,))

    def gate(t0, t1, c0, c1, ya, yb):
        return (jax.nn.sigmoid(t0 + c0) * ya + jax.nn.sigmoid(t1 + c1) * yb,)

    gate_ins = [(proj, D_MODEL, gate0, "row"), (proj, D_MODEL, gate0 + 1, "row"), (b0, 0, 0, "full"), (b1, 0, 0, "full")]
    (u,) = row(gate, name="gate", ins=gate_ins + [(y_a, D_MODEL, 0, "row"), (y_b, D_MODEL, 0, "row")], outs=[(D_MODEL, BF16)])
    mixed = mm(u, w_out, mode="nn", name="mix", tm=1024, tn=1024, tk=1024)

    def ln1(xv, mv, g, b):
        r1 = ALPHA * xv + mv
        xh, _ = _ln_stats(r1)
        hv = xh * g + b
        return r1, hv, hv

    r1, h, h_b = row(ln1, name="ln1", ins=[(x, D_MODEL, 0, "row"), (mixed, D_MODEL, 0, "row"), (ln1_g, 0, 0, "full"), (ln1_b, 0, 0, "full")],
                outs=[(D_MODEL, F32), (D_MODEL, F32), (D_MODEL, BF16)])

    def relu2(acc):
        r = jnp.maximum(acc, 0.0)
        return (r * r,)

    z = mm(h_b, w_ff1, mode="nn", name="ff1", tm=1024, tn=1024, tk=1024, out_dtypes=(BF16,), epilogue=relu2, b_shards=True)
    f = mm(z, w_ff2, mode="nn", name="ff2", tm=1024, tn=1024, tk=2048)

    def ln2_loss(hv, fv, tv, g, b):
        xh, r = _ln_stats(ALPHA * hv + fv)
        err = xh * g + b - tv
        dy = err * (1.0 / D_MODEL)
        dr2, dg, db = _ln_bwd(xh, r, g, dy)
        loss = jnp.sum(_colsum(err * err), axis=1, keepdims=True) * (0.5 / D_MODEL)
        return dr2, dr2, jnp.broadcast_to(loss, (1, LANES)), dg, db

    dr2, dr2_b, loss_l, d_ln2_g, d_ln2_b = row(
        ln2_loss, name="ln2_loss",
        ins=[(h, D_MODEL, 0, "row"), (f, D_MODEL, 0, "row"), (target, D_MODEL, 0, "row"), (ln2_g, 0, 0, "full"), (ln2_b, 0, 0, "full")],
        outs=[(D_MODEL, F32), (D_MODEL, BF16)], sums=[LANES, D_MODEL, D_MODEL])

    d_w_ff2 = mm(z, dr2_b, mode="tn", name="d_w_ff2", tm=1024, tn=1024, tk=2048)
    da = mm(dr2_b, w_ff2, mode="nt", name="d_ff_act", tm=1024, tn=1024, tk=1024, out_dtypes=(BF16,), extras=(z,),
            epilogue=lambda acc, zv: (acc * (2.0 * jnp.sqrt(zv.astype(F32))),))
    d_w_ff1 = mm(h_b, da, mode="tn", name="d_w_ff1", tm=1024, tn=1024, tk=2048, out_shards=True)
    dh = mm(da, w_ff1, mode="nt", name="d_h", tm=1024, tn=1024, tk=1024, extras=(dr2,), epilogue=lambda acc, rv: (acc + ALPHA * rv,), b_shards=True)

    def ln1_bwd(dhv, r1v, g):
        xh, r = _ln_stats(r1v)
        return _ln_bwd(xh, r, g, dhv)

    dr1, d_ln1_g, d_ln1_b = row(ln1_bwd, name="ln1_bwd", ins=[(dh, D_MODEL, 0, "row"), (r1, D_MODEL, 0, "row"), (ln1_g, 0, 0, "full")],
                                outs=[(D_MODEL, F32)], sums=[D_MODEL, D_MODEL])
    d_w_out = mm(u, dr1, mode="tn", name="d_w_out", tm=1024, tn=1024, tk=1024)
    du = mm(dr1, w_out, mode="nt", name="d_u", tm=1024, tn=1024, tk=1024, out_dtypes=(BF16,))

    def gate_bwd(t0, t1, c0, c1, ya, yb, duv):
        s0, s1 = jax.nn.sigmoid(t0 + c0), jax.nn.sigmoid(t1 + c1)
        dt0 = duv * ya * s0 * (1.0 - s0)
        dt1 = duv * yb * s1 * (1.0 - s1)
        return duv * s0, duv * s1, jnp.concatenate([dt0, dt1], axis=1), jnp.concatenate([_colsum(dt0), _colsum(dt1)], axis=1)

    dy_a, dy_b, d_gates, d_b_gate = row(
        gate_bwd, name="gate_bwd", ins=gate_ins + [(y_a, D_MODEL, 0, "row"), (y_b, D_MODEL, 0, "row"), (du, D_MODEL, 0, "row")],
        outs=[(D_MODEL, BF16), (D_MODEL, BF16), (2 * D_MODEL, BF16)], sums=[2 * D_MODEL])
    d_w_oa = mm(o_a, dy_a, mode="tn", name="d_w_o_mla", tm=1024, tn=1024, tk=1024)
    d_w_ob = mm(o_b, dy_b, mode="tn", name="d_w_o_dil", tm=1024, tn=1024, tk=1024)
    grads = dict(w_o_mla=_split_cols(d_w_oa), w_o_dil=_split_cols(d_w_ob),
                 w_out=d_w_out.reshape(N_CHIPS, D_MODEL // N_CHIPS, D_MODEL), w_ff1=d_w_ff1, w_ff2=d_w_ff2.reshape(N_CHIPS, D_FF // N_CHIPS, D_MODEL))
    sent = None if early_grads is None else early_grads(grads)
    do_a = mm(dy_a, w_oa, mode="nt", name="d_o_mla", tm=1024, tn=1024, tk=1024, after=sent)
    do_b = mm(dy_b, w_ob, mode="nt", name="d_o_dil", tm=1024, tn=1024, tk=1024)
    dq_a, dk_a, dv_a = _attn_bwd(q_a, 0, k_a, 0, v_a, 0, o_a, do_a, lse_a, mla_bwd_tables, scale_mla,
                                 name="mla_bwd", batch=batch, seq=seq, out_dtype=F32, sub=2, wide_qk=True)
    going = None if early_grads_go is None else early_grads_go(dq_a)
    d_qkv_d = _attn_bwd(proj_d, 0, proj_d, 1, proj_d, 2, o_b, do_b, lse_b, dil_bwd_tables, scale_dil,
                        name="dil_bwd", batch=batch, seq=seq, out_dtype=BF16, sub=2, stride=4, after=going)

    def mla_post(dq, dk, c, up, dn, c_only):
        dqs = [_rope_bwd(dq[:, h * LANES:(h + 1) * LANES], c, up, dn) for h in range(N_HEADS)]
        dk_sum = dk[:, :LANES]
        for h in range(1, N_HEADS):
            dk_sum = dk_sum + dk[:, h * LANES:(h + 1) * LANES]
        return jnp.concatenate(dqs, axis=1), _rope_bwd(dk_sum, c_only, up, dn)

    dq_lin, d_kr = row(mla_post, name="mla_unrope",
                       ins=[(dq_a, D_MODEL, 0, "row"), (dk_a, D_MODEL, 0, "row"), pos(rope_c), pos(rope_up), pos(rope_dn), pos(rope_c_only)],
                       outs=[(N_HEADS * LANES, BF16), (LANES, BF16)])
    d_w_uq_pt = mm(dq_lin, qn, mode="tn", name="d_w_uq", tm=1024, tn=Q_LORA, tk=1024)
    d_w_uk_p = mm(kvn, dk_a, mode="tn", name="d_w_uk", tm=KV_LORA, tn=1024, tk=1024)
    d_w_uv = mm(kvn, dv_a, mode="tn", name="d_w_uv", tm=KV_LORA, tn=1024, tk=1024)
    d_qn = mm(dq_lin, w_uq_pt, mode="nn", name="d_qn", tm=1024, tn=Q_LORA, tk=1024)
    d_kvn_k = mm(dk_a, w_uk_p, mode="nt", name="d_kvn_k", tm=1024, tn=KV_LORA, tk=1024)
    d_kvn = mm(dv_a, w_uv, mode="nt", name="d_kvn", tm=1024, tn=KV_LORA, tk=1024, extras=(d_kvn_k,), epilogue=lambda acc, e: (acc + e,))

    def rms_bwd(lora, dq, dkv, dkr, gq, gkv):
        dxq, dgq = _rms_bwd(lora[:, :Q_LORA], gq, dq)
        dxk, dgk = _rms_bwd(lora[:, Q_LORA:], gkv, dkv)
        tail = jnp.zeros((dxq.shape[0], P_GATE - P_KR - LANES), F32)
        return jnp.concatenate([dxq, dxk, dkr.astype(F32), tail], axis=1), dgq, dgk

    d_tail, d_g_q_a, d_g_kv_a = row(
        rms_bwd, name="mla_rms_bwd",
        ins=[(proj, LORA_W, lora0, "row"), (d_qn, Q_LORA, 0, "row"), (d_kvn, KV_LORA, 0, "row"), (d_kr, LANES, 0, "row"),
             (g_q_a, 0, 0, "full"), (g_kv_a, 0, 0, "full")],
        outs=[(P_GATE, BF16)], sums=[Q_LORA, KV_LORA])
    d_proj = [d_tail, d_gates, d_qkv_d]
    d_w_in_pt = mm(d_proj, x, mode="tn", name="d_w_in", tm=1024, tn=1024, tk=1024)
    grads.update(w_in=_unpad_w_in_t(d_w_in_pt).reshape(N_CHIPS, IN_WIDTH // N_CHIPS, D_MODEL),
                 w_uq=_unpad_head_rows(d_w_uq_pt, MLA_QK).reshape(N_CHIPS, N_HEADS * MLA_QK // N_CHIPS, Q_LORA),
                 w_ukv=_split_cols(_merge_ukv(d_w_uk_p, d_w_uv)))
    leaving = None if last_grads is None else last_grads(grads)
    grad_x = mm(d_proj, jnp.concatenate([w_main_t, w_dil_t], axis=0), mode="nn", name="d_x", tm=1024, tn=1024, tk=1024, extras=(dr1,), epilogue=lambda acc, rv: (acc + ALPHA * rv,),
                after=leaving)

    grads.update(
        b_gate=d_b_gate.reshape(2, D_MODEL), g_q_a=d_g_q_a, g_kv_a=d_g_kv_a, ln1_g=d_ln1_g, ln1_b=d_ln1_b, ln2_g=d_ln2_g, ln2_b=d_ln2_b)
    return loss_l, grad_x.reshape(batch, seq, D_MODEL), grads


BIG = ("w_in", "w_uq", "w_ukv", "w_o_mla", "w_o_dil", "w_out", "w_ff1", "w_ff2")
SMALL = (("b_gate", 2 * D_MODEL), ("g_q_a", Q_LORA), ("g_kv_a", KV_LORA), ("ln1_g", D_MODEL), ("ln1_b", D_MODEL),
         ("ln2_g", D_MODEL), ("ln2_b", D_MODEL))
TRANSPOSED = ("w_in", "w_uq")
D2D_PIECES = (4, 2, 1)
ANY = pl.BlockSpec(memory_space=pl.ANY)
SIDE_EFFECTS = pltpu.CompilerParams(has_side_effects=True)


def _place():
    x, y, c = lax.axis_index("x"), lax.axis_index("y"), lax.axis_index("c")
    return x, y, c, ((1 - x, y), (x, 1 - y), (1 - x, 1 - y))


def _half_axis(shape):
    return 0 if shape[0] % 32 == 0 else 1


def _half_shape(shape):
    return (shape[0] // 2, shape[1]) if _half_axis(shape) == 0 else (shape[0], shape[1] // 2)


def _window(ref, lead, shape, which=None, pieces=False):
    axis = _half_axis(shape)
    size = shape[axis] if which is None else shape[axis] // 2
    base = 0 if which is None else which * size
    tile = (16, LANES)[axis]
    count = next(c for c in D2D_PIECES if size % (tile * c) == 0) if pieces else 1
    step = size // count
    spans = [pl.ds(pl.multiple_of(base + i * step, tile), step) for i in range(count)]
    refs = [ref.at[(*lead, s)] if axis == 0 else ref.at[(*lead, slice(None), s)] for s in spans]
    return refs if pieces else refs[0]


def _remote(src, dst, send, recv, to):
    return pltpu.make_async_remote_copy(src_ref=src, dst_ref=dst, send_sem=send, recv_sem=recv, device_id=to, device_id_type=MESH)


def _pair_split(grads, name):
    n = len(grads)

    def body(*refs):
        srcs, outs, (send, recv) = refs[:n], refs[n:2 * n], refs[2 * n:]
        x, y, c, _ = _place()
        for t in range(n):
            for s in range(N_CHIPS):
                _remote(_window(srcs[t], (s,), grads[t].shape[1:], 1 - c), outs[t].at[s], send.at[t], recv.at[t], (x, y, 1 - c)).start()
        for t in range(n):
            _remote(_window(srcs[t], (slice(None),), grads[t].shape[1:], 1 - c), outs[t], send.at[t], recv.at[t], (x, y, 1 - c)).wait()

    return pl.pallas_call(
        body, name=name, in_specs=[ANY] * n, out_specs=[ANY] * n,
        out_shape=[jax.ShapeDtypeStruct((N_CHIPS,) + _half_shape(g.shape[1:]), g.dtype) for g in grads],
        scratch_shapes=[pltpu.SemaphoreType.DMA((n,)), pltpu.SemaphoreType.DMA((n,))],
        compiler_params=SIDE_EFFECTS,
    )(*grads)


HBM = pl.BlockSpec(memory_space=pltpu.HBM)
SEM = pl.BlockSpec(memory_space=pltpu.SEMAPHORE)
SPLIT = pltpu.CompilerParams(has_side_effects=pltpu.SideEffectType.DATAFLOW_SIDE_EFFECTING)


def _in_hbm(a):
    return pltpu.with_memory_space_constraint(a, pltpu.HBM)


def _split_copies(kind, srcs, lands):
    x, y, c, chips = _place()
    out = []
    for t in range(len(srcs)):
        if kind == "pair":
            out += [(t, s % 3, _window(srcs[t], (s,), srcs[t].shape[1:], 1 - c), lands[t].at[s], (x, y, 1 - c)) for s in range(N_CHIPS)]
            continue
        if kind == "join":
            out += [(t, 0, a, b, (x, y, 1 - c)) for a, b in zip(_window(srcs[t], (), srcs[t].shape, None, True), _window(lands[t], (), srcs[t].shape, None, True))]
            continue
        if kind == "forward":
            shape, sibling = srcs[t].shape, (x, y, 1 - c)
            out += [(t, 0, a, b, sibling) for a, b in zip(_window(srcs[t], (), shape, None, True), _window(lands[t], (2 * x + y,), shape, None, True))]
            out += [(t, j, a, a, sibling) for j, (cx, cy) in enumerate(chips) for a in _window(lands[t], (2 * cx + cy,), shape, c, True)]
            continue
        for j, (cx, cy) in enumerate(chips):
            if kind == "gather":
                shape = srcs[t].shape
                out.append((t, j, _window(srcs[t], (), shape, c), _window(lands[t], (2 * x + y,), shape, c), (cx, cy, c)))
            else:
                out.append((t, j, srcs[t].at[2 * cx + cy], lands[t].at[j], (cx, cy, c)))
    return out


def _split_start(kind, srcs, land_shapes, name, lands=None, after=None):
    n = len(srcs)

    def body(*refs):
        src_refs, land_refs, sems, token = refs[:n], refs[n:2 * n], refs[-7 - 2 * n:-1 - 2 * n], refs[-1]
        for t, j, s, d, to in _split_copies(kind, src_refs, land_refs):
            _remote(s, d, sems[j], sems[3 + j], to).start()
        token[...] = jnp.zeros_like(token)

    lands = [_in_hbm(lax.empty(s.shape, s.dtype)) for s in land_shapes] if lands is None else list(lands)
    thru = [pltpu.HBM(a.shape, a.dtype) for a in list(srcs) + lands]
    res = pl.pallas_call(
        body, name=name,
        out_shape=(*[pltpu.SemaphoreType.DMA(())] * 6, *thru, jax.ShapeDtypeStruct((8, LANES), F32)),
        in_specs=[HBM] * (2 * n) + [ANY] * (after is not None), out_specs=(*[SEM] * 6, *[HBM] * (2 * n), pl.BlockSpec(memory_space=pltpu.VMEM)),
        input_output_aliases={i: 6 + i for i in range(2 * n)}, compiler_params=SPLIT,
    )(*[_in_hbm(s) for s in srcs], *lands, *([after] if after is not None else []))
    return res[:6], res[6:6 + n], res[6 + n:6 + 2 * n], res[-1]


def _split_wait(kind, sems, srcs, lands, after, name):
    n = len(srcs)

    def body(*refs):
        src_refs, land_refs, sem_refs = refs[:n], refs[n:2 * n], refs[2 * n:2 * n + 6]
        for t, j, s, d, to in _split_copies(kind, src_refs, land_refs):
            cp = _remote(s, d, sem_refs[j], sem_refs[3 + j], to)
            cp.wait_send()
            cp.wait_recv()

    res = pl.pallas_call(
        body, name=name, out_shape=[pltpu.HBM(a.shape, a.dtype) for a in list(srcs) + list(lands)],
        in_specs=[HBM] * (2 * n) + [SEM] * 6 + [ANY], out_specs=[HBM] * (2 * n),
        input_output_aliases={i: i for i in range(2 * n)}, compiler_params=SPLIT,
    )(*srcs, *lands, *sems, after)
    return res[:n], res[n:]


def _sum_all_devices(vec, name):
    n_rows = vec.shape[0]

    def body(v_ref, out_ref, buf, send, recv):
        x, y, c, _ = _place()
        me = 4 * x + 2 * y + c
        buf[me] = v_ref[...]
        flips = [(a, b, d) for a in (0, 1) for b in (0, 1) for d in (0, 1)][1:]
        copies = []
        for r, (a, b, d) in enumerate(flips):
            px, py, pc = (1 - x if a else x), (1 - y if b else y), (1 - c if d else c)
            copies.append(pltpu.make_async_remote_copy(src_ref=v_ref, dst_ref=buf.at[me], send_sem=send.at[r], recv_sem=recv.at[r],
                                                       device_id=(px, py, pc), device_id_type=MESH))
            copies[-1].start()
        for r, (a, b, d) in enumerate(flips):
            px, py, pc = (1 - x if a else x), (1 - y if b else y), (1 - c if d else c)
            pltpu.make_async_remote_copy(src_ref=v_ref, dst_ref=buf.at[4 * px + 2 * py + pc], send_sem=send.at[r], recv_sem=recv.at[r],
                                         device_id=(px, py, pc), device_id_type=MESH).wait_recv()
        for cp in copies:
            cp.wait_send()
        total = buf[0]
        for k in range(1, N_DEV):
            total = total + buf[k]
        out_ref[...] = total

    vmem = pl.BlockSpec(memory_space=pltpu.VMEM)
    return pl.pallas_call(
        body, name=name, in_specs=[vmem], out_specs=vmem, out_shape=jax.ShapeDtypeStruct(vec.shape, F32),
        scratch_shapes=[pltpu.VMEM((N_DEV, n_rows, LANES), F32), pltpu.SemaphoreType.DMA((N_DEV - 1,)), pltpu.SemaphoreType.DMA((N_DEV - 1,))],
        compiler_params=pltpu.CompilerParams(has_side_effects=True),
    )(vec)


def _half_tile(half, width):
    t = half
    while t * width * 4 > (2 << 20) and t % 32 == 0:
        t //= 2
    return t


def _pair_add(g, theirs, core, name):
    _, half, width = theirs.shape
    t = _half_tile(half, width)
    n = half // t

    def body(c_ref, a_ref, b_ref, o_ref):
        o_ref[...] = (a_ref[...] + b_ref[...]).astype(BF16)

    tile = pl.BlockSpec((1, t, width), lambda j, i, c_ref: (j, i, 0))
    if _half_axis(g.shape[1:]) == 0:
        mine = pl.BlockSpec((1, t, width), lambda j, i, c_ref: (j, c_ref[0] * n + i, 0))
    else:
        mine = pl.BlockSpec((1, t, width), lambda j, i, c_ref: (j, i, c_ref[0]))
    return pl.pallas_call(
        body, name=name,
        grid_spec=pltpu.PrefetchScalarGridSpec(num_scalar_prefetch=1, grid=(N_CHIPS, n), in_specs=[mine, tile], out_specs=tile),
        out_shape=jax.ShapeDtypeStruct(theirs.shape, BF16), compiler_params=_params(("parallel", "parallel")),
    )(core, g, theirs)


def _chip_sum(part, others, chip, name, after=None):
    _, half, width = part.shape
    t = _half_tile(half, width)

    def body(s_ref, mine, p0, p1, p2, *rest):
        o_ref = rest[-1]
        o_ref[...] = ((mine[0].astype(F32) + p0[0].astype(F32)) + p1[0].astype(F32)) + p2[0].astype(F32)

    return pl.pallas_call(
        body, name=name,
        grid_spec=pltpu.PrefetchScalarGridSpec(
            num_scalar_prefetch=1, grid=(half // t,),
            in_specs=[pl.BlockSpec((1, t, width), lambda i, s_ref: (s_ref[0], i, 0))]
            + [pl.BlockSpec((1, t, width), lambda i, s_ref, j=j: (j, i, 0)) for j in range(3)] + [pl.BlockSpec(memory_space=pl.ANY)] * (after is not None),
            out_specs=pl.BlockSpec((t, width), lambda i, s_ref: (i, 0))),
        out_shape=jax.ShapeDtypeStruct((half, width), F32), compiler_params=_params(("parallel",)),
    )(chip, part, others, others, others, *([after] if after is not None else []))


EARLY = ("w_in", "w_uq", "w_ukv")
LATE = ("w_o_mla", "w_o_dil", "w_out", "w_ff1", "w_ff2")


def _chip_partials(grads, names, core, tag):
    gs = [grads[n] for n in names]
    theirs = _pair_split(gs, "pair_split_" + tag)
    return [_pair_add(g, th, core, "pair_add_" + n) for g, th, n in zip(gs, theirs, names)]


def _sum_small(vals):
    n_in = len(vals)
    n_rows = sum(a.shape[0] * a.shape[1] // LANES for a in vals)
    pad_rows = -(-n_rows // 8) * 8

    def chunks(refs):
        return [(ref, a, j) for ref in refs for a in range(ref.shape[0]) for j in range(ref.shape[1] // LANES)]

    def body(*refs):
        ins, outs, (buf, send, recv) = refs[:n_in], refs[n_in:2 * n_in], refs[2 * n_in:]
        x, y, c, _ = _place()
        me = 4 * x + 2 * y + c
        for r, (ref, a, j) in enumerate(chunks(ins)):
            buf[me, r:r + 1, :] = ref[a:a + 1, j * LANES:(j + 1) * LANES]
        if pad_rows > n_rows:
            buf[me, n_rows:pad_rows, :] = jnp.zeros((pad_rows - n_rows, LANES), F32)
        flips = [(a, b, d) for a in (0, 1) for b in (0, 1) for d in (0, 1)][1:]
        peers = [((1 - x if a else x), (1 - y if b else y), (1 - c if d else c)) for a, b, d in flips]
        copies = [_remote(buf.at[me], buf.at[me], send.at[r], recv.at[r], peer) for r, peer in enumerate(peers)]
        for cp in copies:
            cp.start()
        for r, (px, py, pc) in enumerate(peers):
            _remote(buf.at[me], buf.at[4 * px + 2 * py + pc], send.at[r], recv.at[r], (px, py, pc)).wait_recv()
        for cp in copies:
            cp.wait_send()
        total = buf[0]
        for k in range(1, N_DEV):
            total = total + buf[k]
        for r, (ref, a, j) in enumerate(chunks(outs)):
            ref[a:a + 1, j * LANES:(j + 1) * LANES] = total[r:r + 1, :]

    vmem = pl.BlockSpec(memory_space=pltpu.VMEM)
    return pl.pallas_call(
        body, name="sum_small", in_specs=[vmem] * n_in, out_specs=[vmem] * n_in,
        out_shape=[jax.ShapeDtypeStruct(a.shape, F32) for a in vals],
        scratch_shapes=[pltpu.VMEM((N_DEV, pad_rows, LANES), F32), pltpu.SemaphoreType.DMA((N_DEV - 1,)), pltpu.SemaphoreType.DMA((N_DEV - 1,))],
        compiler_params=SIDE_EFFECTS,
    )(*vals)


def _adam_math(w, g, m, v):
    nm = B1 * m + (1.0 - B1) * g
    nv = B2 * v + (1.0 - B2) * (g * g)
    m_hat = nm / (1.0 - B1 ** ADAM_STEP)
    v_hat = nv / (1.0 - B2 ** ADAM_STEP)
    return -LR * (m_hat / (jnp.sqrt(v_hat) + ADAM_EPS) + WD * w), nm, nv


def _adamw_big(w, mine, theirs, m, v, core, name, side_by_side=False):
    rows, width = w.shape
    if side_by_side:
        t = next(c for c in (152, 96, 64, 32, 16, 8) if rows % c == 0)
        hb = None
        half_spec = pl.BlockSpec((t, width // 2), lambda i, c_ref: (i, 0))
    else:
        t = next(c for c in (256, 128, 64, 32, 16, 8) if (rows // 2) % c == 0)
        hb = rows // 2 // t
        half_spec = pl.BlockSpec((t, width), lambda i, c_ref: (i % hb, 0))

    def body(c_ref, w_ref, a_ref, b_ref, m_ref, v_ref, g_ref, d_ref, nm_ref, nv_ref):
        south = c_ref[0] == 0
        if side_by_side:
            g = jnp.where(south, jnp.concatenate([a_ref[...], b_ref[...]], axis=1), jnp.concatenate([b_ref[...], a_ref[...]], axis=1))
        else:
            g = jnp.where((pl.program_id(0) < hb) == south, a_ref[...], b_ref[...])
        g_ref[...] = g
        d_ref[...], nm_ref[...], nv_ref[...] = _adam_math(w_ref[...], g, m_ref[...], v_ref[...])

    spec = pl.BlockSpec((t, width), lambda i, c_ref: (i, 0))
    return pl.pallas_call(
        body, name=name,
        grid_spec=pltpu.PrefetchScalarGridSpec(num_scalar_prefetch=1, grid=(rows // t,),
                                               in_specs=[spec, half_spec, half_spec, spec, spec], out_specs=[spec] * 4),
        out_shape=[jax.ShapeDtypeStruct(w.shape, F32)] * 4, compiler_params=_params(("parallel",)),
    )(core, w, mine, theirs, m, v)


def _adamw_small(ws, gs, ms, vs):
    n = len(ws)

    def body(*refs):
        for t in range(n):
            w_ref, g_ref, m_ref, v_ref = (refs[k * n + t] for k in range(4))
            d, nm, nv = _adam_math(w_ref[...], g_ref[...], m_ref[...], v_ref[...])
            refs[4 * n + t][...] = d
            refs[5 * n + t][...] = nm
            refs[6 * n + t][...] = nv

    vmem = pl.BlockSpec(memory_space=pltpu.VMEM)
    res = pl.pallas_call(body, name="adamw_small", in_specs=[vmem] * (4 * n), out_specs=[vmem] * (3 * n),
                         out_shape=[jax.ShapeDtypeStruct(a.shape, F32) for a in ws] * 3)(*ws, *gs, *ms, *vs)
    return res[:n], res[n:2 * n], res[2 * n:]


def kernel(x, w_in, b_gate, g_q_a, w_uq, g_kv_a, w_ukv, w_o_mla, w_o_dil, w_out, ln1_g, ln1_b, w_ff1, w_ff2, ln2_g, ln2_b, loss_target, m_w_in, m_b_gate, m_g_q_a, m_w_uq, m_g_kv_a, m_w_ukv, m_w_o_mla, m_w_o_dil, m_w_out, m_ln1_g, m_ln1_b, m_w_ff1, m_w_ff2, m_ln2_g, m_ln2_b, v_w_in, v_b_gate, v_g_q_a, v_w_uq, v_g_kv_a, v_w_ukv, v_w_o_mla, v_w_o_dil, v_w_out, v_ln1_g, v_ln1_b, v_w_ff1, v_w_ff2, v_ln2_g, v_ln2_b):
    order = ("w_in", "b_gate", "g_q_a", "w_uq", "g_kv_a", "w_ukv", "w_o_mla", "w_o_dil", "w_out", "ln1_g", "ln1_b", "w_ff1", "w_ff2", "ln2_g", "ln2_b")
    w = dict(w_in=w_in, b_gate=b_gate, g_q_a=g_q_a, w_uq=w_uq, g_kv_a=g_kv_a, w_ukv=w_ukv, w_o_mla=w_o_mla, w_o_dil=w_o_dil, w_out=w_out,
             ln1_g=ln1_g, ln1_b=ln1_b, w_ff1=w_ff1, w_ff2=w_ff2, ln2_g=ln2_g, ln2_b=ln2_b)
    m = dict(w_in=m_w_in, b_gate=m_b_gate, g_q_a=m_g_q_a, w_uq=m_w_uq, g_kv_a=m_g_kv_a, w_ukv=m_w_ukv, w_o_mla=m_w_o_mla, w_o_dil=m_w_o_dil,
             w_out=m_w_out, ln1_g=m_ln1_g, ln1_b=m_ln1_b, w_ff1=m_w_ff1, w_ff2=m_w_ff2, ln2_g=m_ln2_g, ln2_b=m_ln2_b)
    v = dict(w_in=v_w_in, b_gate=v_b_gate, g_q_a=v_g_q_a, w_uq=v_w_uq, g_kv_a=v_g_kv_a, w_ukv=v_w_ukv, w_o_mla=v_w_o_mla, w_o_dil=v_w_o_dil,
             w_out=v_w_out, ln1_g=v_ln1_g, ln1_b=v_ln1_b, w_ff1=v_w_ff1, w_ff2=v_w_ff2, ln2_g=v_ln2_g, ln2_b=v_ln2_b)
    chip = 2 * lax.axis_index("x") + lax.axis_index("y")
    south = (lax.axis_index("c") == 0).astype(F32)
    gate_w = D_MODEL // N_CHIPS

    core = lax.axis_index("c").astype(jnp.int32).reshape(1)
    turn = lambda n, a: a.T if n in TRANSPOSED else a
    shards = {n: turn(n, w[n][0]).astype(BF16) for n in BIG}
    early_shards, late_shards = [shards[n] for n in EARLY], [shards[n] for n in LATE]
    gathered = lambda group: [jax.ShapeDtypeStruct((N_CHIPS,) + s.shape, BF16) for s in group]
    e_sems, e_srcs, e_lands, e_token = _split_start("gather", early_shards, gathered(early_shards), "gather_early_start")
    g_sems, g_srcs, g_lands, g_token = _split_start("gather", late_shards, gathered(late_shards), "gather_late_start", after=e_token)
    tables = _attention_tables(x.shape[1])
    e_srcs, e_lands = _split_wait("gather", e_sems, e_srcs, e_lands, tables[1][0][1], "gather_early_wait")
    e_forward = _split_start("forward", e_srcs, None, "gather_early_forward_start", lands=e_lands)
    first = dict(zip(EARLY, _split_wait("forward", *e_forward[:3], e_forward[-1], "gather_early_forward_wait")[1]))
    b_mine = lax.dynamic_update_slice(jnp.zeros((2, D_MODEL), F32), b_gate[0] * south, (0, chip * gate_w))
    b_full = _sum_all_devices(b_mine.reshape(-1, LANES), "gather_b_gate").reshape(2, D_MODEL)

    sent = {}

    def late_arrived(after):
        srcs, lands = _split_wait("gather", g_sems, g_srcs, g_lands, after, "gather_late_wait")
        sent["forward"] = _split_start("forward", srcs, None, "gather_late_forward_start", lands=lands)
        return sent["forward"][-1]

    def late_weights(after):
        return dict(zip(LATE, _split_wait("forward", *sent["forward"][:3], after, "gather_late_forward_wait")[1]))

    exchange_shapes = lambda parts: [jax.ShapeDtypeStruct((3,) + p.shape[1:], BF16) for p in parts]

    def early_grads(grads_late):
        gs = [grads_late[n] for n in LATE]
        shapes = [jax.ShapeDtypeStruct((N_CHIPS,) + _half_shape(g.shape[1:]), F32) for g in gs]
        sent["pair"] = _split_start("pair", gs, shapes, "pair_split_late_start")
        return sent["pair"][-1]

    def early_grads_go(after):
        gs, theirs = _split_wait("pair", *sent["pair"][:3], after, "pair_split_late_wait")
        parts = [_pair_add(g, th, core, "pair_add_" + n) for g, th, n in zip(gs, theirs, LATE)]
        sent["late"] = _split_start("scatter", parts, exchange_shapes(parts), "exchange_late_start")
        return sent["late"][-1]

    def last_grads(grads_early):
        parts = _chip_partials(grads_early, EARLY, core, "early")
        sent["early"] = _split_start("scatter", parts, exchange_shapes(parts), "exchange_early_start")
        return sent["early"][-1]

    loss_part, grad_x, grads = _local_step(x, loss_target, first, b_full, g_q_a, g_kv_a, ln1_g, ln1_b, ln2_g, ln2_b, token=g_token,
                                           late_arrived=late_arrived, late_weights=late_weights, early_grads=early_grads, early_grads_go=early_grads_go,
                                           last_grads=last_grads, tables=tables)

    g_out, delta, new_m, new_v = {}, {}, {}, {}
    chip1 = chip.astype(jnp.int32).reshape(1)

    def sum_and_send(names, parts, others, tag):
        totals = [_chip_sum(p, o, chip1, "chip_sum_" + n) for n, p, o in zip(names, parts, others)]
        return _split_start("join", totals, [jax.ShapeDtypeStruct(t.shape, F32) for t in totals], "pair_join_" + tag + "_start")

    def adam(names, joined, after, tag):
        totals, halves = _split_wait("join", *joined[:3], after, "pair_join_" + tag + "_wait")
        for n, mine, theirs in zip(names, totals, halves):
            res = _adamw_big(turn(n, w[n][0]), mine, theirs, turn(n, m[n][0]), turn(n, v[n][0]), core, "adamw_" + n,
                             side_by_side=mine.shape[0] == shards[n].shape[0])
            g_out[n], delta[n], new_m[n], new_v[n] = (turn(n, r) for r in res)

    late_joined = sum_and_send(LATE, *_split_wait("scatter", *sent["late"][:3], grad_x, "exchange_late_wait"), "late")
    early_joined = sum_and_send(EARLY, *_split_wait("scatter", *sent["early"][:3], late_joined[-1], "exchange_early_wait"), "early")
    small_names = [name for name, _ in SMALL]
    sums = _sum_small([grads[name] for name in small_names] + [loss_part])
    loss = sums[-1][0, 0]
    g_small = dict(zip(small_names, sums))
    g_small["b_gate"] = lax.dynamic_slice(g_small["b_gate"], (0, chip * gate_w), (2, gate_w))
    flat = lambda a: a.reshape(-1, a.shape[-1])
    res = _adamw_small(*[[flat(d[name]) for name in small_names] for d in (w, g_small, m, v)])
    g_out.update(g_small)
    for d, r in zip((delta, new_m, new_v), res):
        d.update(zip(small_names, r))
    adam(LATE, late_joined, res[0][0], "late")
    adam(EARLY, early_joined, delta[LATE[-1]], "early")

    lead = lambda d: [d[name].reshape(w[name].shape) for name in order]
    return (loss, grad_x, *lead(g_out), *lead(delta), *lead(new_m), *lead(new_v))
```

```python
import functools
import math

import jax
import jax.numpy as jnp
import numpy as np
from jax import lax
from jax.experimental import pallas as pl
from jax.experimental.pallas import tpu as pltpu

F32 = jnp.float32
BF16 = jnp.bfloat16
MESH = pl.DeviceIdType.MESH

D_MODEL = 1024
N_HEADS = 8
LANES = 128
NOPE, ROPE, V_DIM = 64, 32, 64
MLA_QK = NOPE + ROPE
Q_LORA, KV_LORA = 384, 256
DIL_DIM = 64
DIL_PATTERNS = ((128, 1), (512, 4), (2048, 16))
D_FF = 4096
N_CHIPS = 4
N_DEV = 8
IN_WIDTH = 4256
LN_EPS, RMS_EPS = 1e-5, 1e-6
NEG = -1e30
LOG2E, LN2 = 1.4426950408889634, 0.6931471805599453
ALPHA = 2.0 ** 0.25
ROPE_THETA = 10000.0
LR, B1, B2, ADAM_EPS, WD, ADAM_STEP = 0.001, 0.9, 0.999, 1e-8, 0.01, 10

P_LORA, P_KR, P_GATE, P_HALF = 0, 640, 1024, 3072
DIL_GROUP = 4 * LANES
P_DIL = N_HEADS // 2 * DIL_GROUP
LORA_W = Q_LORA + KV_LORA
KR_LANE = NOPE

ATT_T = 512
ROW_T = 512
VMEM_LIMIT = 56 * 1024 * 1024

NN = (((1,), (0,)), ((), ()))
NT = (((1,), (1,)), ((), ()))
TN = (((0,), (0,)), ((), ()))


def _params(sem=None, **kw):
    return pltpu.CompilerParams(dimension_semantics=sem, vmem_limit_bytes=VMEM_LIMIT, **kw)


def _matmul(a, b, *, mode, name, tm, tn, tk, out_dtypes=(F32,), extras=(), epilogue=None, b_shards=False, out_shards=False, after=None):
    pieces = list(a) if isinstance(a, (list, tuple)) else [a]
    n_pc = len(pieces)
    a_shape = (pieces[0].shape[0], sum(p.shape[1] for p in pieces))
    if b_shards:
        n_sh, rows_b, cols_b = b.shape
        b_shape = (rows_b, n_sh * cols_b)
    else:
        b_shape = b.shape
    if mode == "nn":
        (m, k), (k2, n) = a_shape, b_shape
    elif mode == "nt":
        (m, k), (n, k2) = a_shape, b_shape
    else:
        (k, m), (k2, n) = a_shape, b_shape
    assert k == k2, (a_shape, b.shape, mode)
    tm, tn, tk = min(tm, m), min(tn, n), min(tk, k)
    assert m % tm == 0 and n % tn == 0 and k % tk == 0, (name, m, n, k, tm, tn, tk)
    nk = k // tk
    n_ex, n_out = len(extras), len(out_dtypes)
    n_in = n_pc + 1 + n_ex + (after is not None)
    dims = {"nn": NN, "nt": NT, "tn": TN}[mode]
    col_tile = tm if mode == "tn" else tk
    blocks = [p.shape[1] // col_tile for p in pieces]
    firsts = [sum(blocks[:p]) for p in range(n_pc)]
    assert all(p.shape[1] % col_tile == 0 for p in pieces), (name, col_tile)

    def body(*refs):
        a_refs, b_ref = refs[:n_pc], refs[n_pc]
        ex_refs = refs[n_pc + 1:n_pc + 1 + n_ex]
        out_refs = refs[n_in:n_in + n_out]

        def finish(acc):
            outs = epilogue(acc, *[r[...] for r in ex_refs]) if epilogue is not None else (acc,)
            for r, o in zip(out_refs, outs):
                r[...] = o.astype(r.dtype)

        kk = pl.program_id(2)

        def step(a_ref):
            part = lax.dot_general(a_ref[...].astype(BF16), b_ref[...].astype(BF16), dims, preferred_element_type=F32)
            if nk == 1:
                finish(part)
                return
            acc_ref = refs[-1]

            @pl.when(kk == 0)
            def _():
                acc_ref[...] = part

            @pl.when(kk > 0)
            def _():
                acc_ref[...] += part

            @pl.when(kk == nk - 1)
            def _():
                finish(acc_ref[...])

        if n_pc == 1:
            step(a_refs[0])
        else:
            at = pl.program_id(0) if mode == "tn" else kk
            for p in range(n_pc):
                pl.when(jnp.logical_and(at >= firsts[p], at < firsts[p] + blocks[p]))(functools.partial(step, a_refs[p]))

    def a_spec_of(p):
        if n_pc == 1:
            return pl.BlockSpec((tk, tm), lambda i, j, kk: (kk, i)) if mode == "tn" else pl.BlockSpec((tm, tk), lambda i, j, kk: (i, kk))
        col = lambda at: jnp.clip(at - firsts[p], 0, blocks[p] - 1)
        mine = lambda at: jnp.logical_and(at >= firsts[p], at < firsts[p] + blocks[p])
        if mode == "tn":
            return pl.BlockSpec((tk, tm), lambda i, j, kk: (jnp.where(mine(i), kk, 0), col(i)))
        return pl.BlockSpec((tm, tk), lambda i, j, kk: (i, col(kk)))

    b_spec = {"nn": pl.BlockSpec((tk, tn), lambda i, j, kk: (kk, j)),
              "nt": pl.BlockSpec((tn, tk), lambda i, j, kk: (j, kk)),
              "tn": pl.BlockSpec((tk, tn), lambda i, j, kk: (kk, j))}[mode]
    tile = pl.BlockSpec((tm, tn), lambda i, j, kk: (i, j))
    out_spec, out_dims = tile, (m, n)
    if b_shards and mode == "nn":
        per = cols_b // tn
        b_spec = pl.BlockSpec((None, tk, tn), lambda i, j, kk: (j // per, kk, j % per))
    elif b_shards:
        assert mode == "nt"
        per = cols_b // tk
        b_spec = pl.BlockSpec((None, tn, tk), lambda i, j, kk: (kk // per, j, kk % per))
    if out_shards:
        assert not extras and epilogue is None
        per_out = n // N_CHIPS // tn
        out_spec = pl.BlockSpec((None, tm, tn), lambda i, j, kk: (j // per_out, i, j % per_out))
        out_dims = (N_CHIPS, m, n // N_CHIPS)
    outs = pl.pallas_call(
        body, name=name,
        grid=(m // tm, n // tn, nk),
        in_specs=[a_spec_of(p) for p in range(n_pc)] + [b_spec] + [tile] * n_ex + [pl.BlockSpec(memory_space=pl.ANY)] * (after is not None),
        out_specs=[out_spec] * n_out,
        out_shape=[jax.ShapeDtypeStruct(out_dims, dt) for dt in out_dtypes],
        scratch_shapes=[pltpu.VMEM((tm, tn), F32)] if nk > 1 else [],
        compiler_params=_params(("parallel", "parallel", "arbitrary")),
    )(*pieces, b, *extras, *([after] if after is not None else []))
    return outs[0] if n_out == 1 else outs


def _rowwise(fn, *, name, rows, seq, ins, outs, sums=()):
    tm = min(ROW_T, seq)
    n_pos = seq // tm
    n_in, n_out, n_sum = len(ins), len(outs), len(sums)

    def body(*refs):
        vals = fn(*[r[...] for r in refs[:n_in]])
        for r, v in zip(refs[n_in:n_in + n_out], vals[:n_out]):
            r[...] = v.astype(r.dtype)
        first = pl.program_id(0) == 0
        for r, v in zip(refs[n_in + n_out:], vals[n_out:]):
            @pl.when(first)
            def _(r=r, v=v):
                r[...] = v

            @pl.when(jnp.logical_not(first))
            def _(r=r, v=v):
                r[...] += v

    def spec(arr, width, col, kind):
        if kind == "row":
            return pl.BlockSpec((tm, width), lambda i, col=col: (i, col))
        if kind == "pos":
            return pl.BlockSpec((tm, width), lambda i, col=col: (i % n_pos, col))
        return pl.BlockSpec(arr.shape, lambda i: (0,) * arr.ndim)

    res = pl.pallas_call(
        body, name=name,
        grid=(rows // tm,),
        in_specs=[spec(*t) for t in ins],
        out_specs=[pl.BlockSpec((tm, w), lambda i: (i, 0)) for w, _ in outs]
        + [pl.BlockSpec((1, w), lambda i: (0, 0)) for w in sums],
        out_shape=[jax.ShapeDtypeStruct((rows, w), dt) for w, dt in outs]
        + [jax.ShapeDtypeStruct((1, w), F32) for w in sums],
        compiler_params=_params(("arbitrary",)),
    )(*[t[0] for t in ins])
    return res


def _colsum(v):
    return jnp.sum(v, axis=0, keepdims=True)


def _rope_fwd(t, c, s_up, s_dn):
    return t * c + pltpu.roll(t, LANES - 16, 1) * s_up + pltpu.roll(t, 16, 1) * s_dn


def _rope_bwd(d, c, s_up, s_dn):
    return d * c + pltpu.roll(d * s_up, 16, 1) + pltpu.roll(d * s_dn, LANES - 16, 1)


def _rope_tables(seq):
    half = ROPE // 2
    inv = jnp.power(ROPE_THETA, -jnp.arange(half, dtype=F32) / half)
    ang = jnp.arange(seq, dtype=F32)[:, None] * inv[None, :]
    cos, sin = jnp.cos(ang), jnp.sin(ang)
    zeros = jnp.zeros((seq, half), F32)
    lo, hi = jnp.ones((seq, KR_LANE), F32), jnp.ones((seq, LANES - KR_LANE - ROPE), F32)
    c = jnp.concatenate([lo, cos, cos, hi], axis=1)
    c_rope_only = jnp.concatenate([0 * lo, cos, cos, 0 * hi], axis=1)
    s_up = jnp.concatenate([0 * lo, -sin, zeros, 0 * hi], axis=1)
    s_dn = jnp.concatenate([0 * lo, zeros, sin, 0 * hi], axis=1)
    return c, s_up, s_dn, c_rope_only


def _rms(x, g):
    r = lax.rsqrt(jnp.mean(x * x, axis=1, keepdims=True) + RMS_EPS)
    return x * r * g


def _rms_bwd(x, g, dy):
    r = lax.rsqrt(jnp.mean(x * x, axis=1, keepdims=True) + RMS_EPS)
    xh = x * r
    dxh = dy * g
    dx = r * (dxh - xh * jnp.mean(dxh * xh, axis=1, keepdims=True))
    return dx, _colsum(dy * xh)


def _ln_stats(x):
    mu = jnp.mean(x, axis=1, keepdims=True)
    xc = x - mu
    r = lax.rsqrt(jnp.mean(xc * xc, axis=1, keepdims=True) + LN_EPS)
    return xc * r, r


def _ln_bwd(xh, r, g, dy):
    dxh = dy * g
    dx = r * (dxh - jnp.mean(dxh, axis=1, keepdims=True) - xh * jnp.mean(dxh * xh, axis=1, keepdims=True))
    return dx, _colsum(dy * xh), _colsum(dy)


def _table_specs(tables, sub):
    whole = lambda a: pl.BlockSpec(a.shape, lambda b, g: (0,) * a.ndim)
    if len(tables) == 1:
        return [whole(tables[0])]
    return [whole(tables[0]), whole(tables[1]), pl.BlockSpec((sub, 1, LANES), lambda b, g: (g, 0, 0))]


def _biased(s, table_refs, delta, head):
    if delta < table_refs[0].shape[0]:
        s = s + table_refs[0][delta]
    if len(table_refs) == 3:
        s = s - table_refs[2][head, 0:1, 0:1] * table_refs[1][delta]
    return s


def _lane_masks(sub):
    lane = lax.broadcasted_iota(jnp.int32, (1, LANES), 1)
    return [(lane // (LANES // sub) == a).astype(F32) for a in range(sub)]


def _attn_fwd(q, qb0, k, kb0, v, vb0, tables, scale, *, name, batch, seq, sub=1, stride=1, wide_qk=False, after=None):
    t = ATT_T
    nq = seq // t
    rows = batch * seq
    n_tab = len(tables)
    qk_w = sub * LANES if wide_qk else LANES

    def body(q_ref, k_ref, v_ref, *rest):
        table_refs = rest[:n_tab]
        o_ref, lse_ref, vtb = rest[n_tab + (after is not None):][:3]
        per_head = rest[n_tab + (after is not None) + 3:]
        qbs, kbs = per_head[:sub], per_head[sub:]
        masks = _lane_masks(sub)
        for a in range(sub):
            lanes = slice(a * LANES, (a + 1) * LANES) if wide_qk else slice(None)
            qa = q_ref[:, lanes]
            qbs[a][...] = (qa.astype(F32) * masks[a]).astype(BF16) if sub > 1 and not wide_qk else qa.astype(BF16)
            if wide_qk or a == 0:
                kbs[a][...] = k_ref[:, lanes].astype(BF16)
        vtb[...] = v_ref[...].astype(F32).T.astype(BF16)
        for i in range(nq):
            out_t = None
            for a in range(sub):
                qt, kb = qbs[a][i * t:(i + 1) * t, :], kbs[a if wide_qk else 0]
                logits = [_biased(lax.dot_general(kb[j * t:(j + 1) * t, :], qt, NT, preferred_element_type=F32) * (scale * LOG2E), table_refs, i - j, a)
                          for j in range(i + 1)]
                m = jnp.max(functools.reduce(jnp.maximum, logits), axis=0, keepdims=True)
                ps = [jnp.exp2(s - m) for s in logits]
                l = jnp.sum(functools.reduce(jnp.add, ps), axis=0, keepdims=True)
                acc = functools.reduce(jnp.add, [lax.dot_general(vtb[:, j * t:(j + 1) * t], p.astype(BF16), NN, preferred_element_type=F32)
                                                 for j, p in enumerate(ps)])
                part = acc / l if sub == 1 else (acc / l) * masks[a].T
                out_t = part if out_t is None else out_t + part
                lse_ref[i * t:(i + 1) * t, a * LANES:(a + 1) * LANES] = jnp.broadcast_to((m + jnp.log2(l)) * LN2, (LANES, t)).T
            o_ref[i * t:(i + 1) * t, :] = out_t.T

    slab = lambda b0, step, width=LANES: pl.BlockSpec((seq, width), lambda b, g: (b, b0 + step * g))
    groups = N_HEADS // sub
    n_k = sub if wide_qk else 1
    return pl.pallas_call(
        body, name=name,
        grid=(batch, groups),
        in_specs=[slab(qb0, stride, qk_w), slab(kb0, stride, qk_w), slab(vb0, stride)] + _table_specs(tables, sub)
        + [pl.BlockSpec(memory_space=pl.ANY)] * (after is not None),
        out_specs=[slab(0, 1), slab(0, 1, sub * LANES)],
        out_shape=[jax.ShapeDtypeStruct((rows, groups * LANES), F32), jax.ShapeDtypeStruct((rows, N_HEADS * LANES), F32)],
        scratch_shapes=[pltpu.VMEM((LANES, seq), BF16)] + [pltpu.VMEM((seq, LANES), BF16)] * (sub + n_k),
        compiler_params=_params(("arbitrary", "arbitrary")),
    )(q, k, v, *tables, *([after] if after is not None else []))


def _attn_bwd(q, qb0, k, kb0, v, vb0, o, do, lse, tables, scale, *, name, batch, seq, out_dtype, sub=1, stride=1, wide_qk=False, after=None):
    t = ATT_T
    nq = seq // t
    rows = batch * seq
    n_tab = len(tables)
    groups = N_HEADS // sub
    packed = sub > 1 and not wide_qk
    n_out = 1 if packed else 3
    n_k = sub if wide_qk else 1
    qk_w = sub * LANES if wide_qk else LANES

    def body(q_ref, k_ref, v_ref, o_ref, do_ref, lse_ref, *rest):
        table_refs = rest[:n_tab]
        rest = rest[n_tab + (after is not None):]
        out_refs, (vb, dva), rest = rest[:n_out], rest[n_out:n_out + 2], rest[n_out + 2:]
        kbs, dkas, rest = rest[:n_k], rest[n_k:2 * n_k], rest[2 * n_k:]
        qbs, dobs, qtbs, dotbs = (rest[g * sub:(g + 1) * sub] for g in range(4))
        masks = _lane_masks(sub)
        vb[...] = v_ref[...].astype(BF16)
        for a in range(sub):
            lanes = slice(a * LANES, (a + 1) * LANES) if wide_qk else slice(None)
            qa = q_ref[:, lanes].astype(F32) * masks[a] if packed else q_ref[:, lanes].astype(F32)
            doa = do_ref[...] * masks[a] if sub > 1 else do_ref[...]
            qbs[a][...] = qa.astype(BF16)
            dobs[a][...] = doa.astype(BF16)
            qtbs[a][...] = qa.T.astype(BF16)
            dotbs[a][...] = doa.T.astype(BF16)
            if wide_qk or a == 0:
                kbs[a][...] = k_ref[:, lanes].astype(BF16)
        first_k, first_v = [[True] * nq for _ in range(n_k)], [True] * nq
        for i in range(nq):
            at = slice(i * t, (i + 1) * t)
            dq_all = None
            for a in range(sub):
                qt, dot, kb, dka = qbs[a][at, :], dobs[a][at, :], kbs[a if wide_qk else 0], dkas[a if wide_qk else 0]
                lse_t = lse_ref[at, a * LANES:a * LANES + 1] * LOG2E
                od = o_ref[at, :] * do_ref[at, :]
                delta = jnp.sum(od * masks[a] if sub > 1 else od, axis=1, keepdims=True)
                dq = None
                for j in range(i + 1):
                    kat = slice(j * t, (j + 1) * t)
                    kt, vt = kb[kat, :], vb[kat, :]
                    p = jnp.exp2(_biased(lax.dot_general(qt, kt, NT, preferred_element_type=F32) * (scale * LOG2E), table_refs, i - j, a) - lse_t)
                    dp = lax.dot_general(dot, vt, NT, preferred_element_type=F32)
                    ds = (p * (dp - delta) * scale).astype(BF16)
                    dk_part = lax.dot_general(qtbs[a][:, at], ds, NN, preferred_element_type=F32)
                    dv_part = lax.dot_general(dotbs[a][:, at], p.astype(BF16), NN, preferred_element_type=F32)
                    firsts = first_k[a if wide_qk else 0]
                    if firsts[j]:
                        dka[:, kat] = dk_part
                        firsts[j] = False
                    else:
                        dka[:, kat] += dk_part
                    if first_v[j]:
                        dva[:, kat] = dv_part
                        first_v[j] = False
                    else:
                        dva[:, kat] += dv_part
                    dq_part = lax.dot_general(ds, kt, NN, preferred_element_type=F32)
                    dq = dq_part if dq is None else dq + dq_part
                if wide_qk:
                    out_refs[0][at, a * LANES:(a + 1) * LANES] = dq.astype(out_refs[0].dtype)
                else:
                    dq = dq * masks[a] if sub > 1 else dq
                    dq_all = dq if dq_all is None else dq_all + dq
            if not wide_qk:
                out_refs[0][at, 0:LANES] = dq_all.astype(out_refs[0].dtype)
        if packed:
            out_refs[0][:, LANES:2 * LANES] = dkas[0][...].T.astype(out_refs[0].dtype)
            out_refs[0][:, 2 * LANES:3 * LANES] = dva[...].T.astype(out_refs[0].dtype)
            out_refs[0][:, 3 * LANES:] = jnp.zeros((seq, LANES), out_refs[0].dtype)
        else:
            for a in range(n_k):
                out_refs[1][:, a * LANES:(a + 1) * LANES] = dkas[a][...].T.astype(out_refs[1].dtype)
            out_refs[2][...] = dva[...].T.astype(out_refs[2].dtype)

    slab = lambda b0, step, width=LANES: pl.BlockSpec((seq, width), lambda b, g: (b, b0 + step * g))
    if packed:
        out_specs = [slab(0, 1, 4 * LANES)]
        out_shape = [jax.ShapeDtypeStruct((rows, groups * 4 * LANES), out_dtype)]
    else:
        out_specs = [slab(0, 1, qk_w), slab(0, 1, qk_w), slab(0, 1)]
        out_shape = [jax.ShapeDtypeStruct((rows, N_HEADS * LANES), out_dtype)] * 2 + [jax.ShapeDtypeStruct((rows, groups * LANES), out_dtype)]
    res = pl.pallas_call(
        body, name=name,
        grid=(batch, groups),
        in_specs=[slab(qb0, stride, qk_w), slab(kb0, stride, qk_w), slab(vb0, stride), slab(0, 1), slab(0, 1), slab(0, 1, sub * LANES)]
        + _table_specs(tables, sub) + [pl.BlockSpec(memory_space=pl.ANY)] * (after is not None),
        out_specs=out_specs, out_shape=out_shape,
        scratch_shapes=[pltpu.VMEM((seq, LANES), BF16), pltpu.VMEM((LANES, seq), F32)]
        + [pltpu.VMEM((seq, LANES), BF16)] * n_k + [pltpu.VMEM((LANES, seq), F32)] * n_k
        + [pltpu.VMEM((seq, LANES), BF16)] * (2 * sub) + [pltpu.VMEM((LANES, seq), BF16)] * (2 * sub),
        compiler_params=_params(("arbitrary", "arbitrary")),
    )(q, k, v, o, do, lse, *tables, *([after] if after is not None else []))
    return res[0] if packed else res


def _attention_tables(seq):
    n = seq // ATT_T
    pos = np.arange(ATT_T, dtype=np.int32)
    dist = np.arange(n, dtype=np.int32)[:, None, None] * ATT_T + pos[None, :, None] - pos[None, None, :]
    causal = np.where(dist[:1] >= 0, 0.0, NEG).astype(np.float32)
    count = np.zeros(dist.shape, np.float32)
    for window, dilation in DIL_PATTERNS:
        count += ((dist >= 0) & (dist <= window) & (dist % dilation == 0)).astype(np.float32)
    held = np.where(count > 0, np.log2(np.maximum(count, 1.0)), NEG).astype(np.float32)
    far = dist.astype(np.float32) * np.float32(LOG2E)
    slopes = np.asarray([2.0 ** (-8.0 * (i + 1) / N_HEADS) for i in range(N_HEADS)], np.float32)
    slopes = jnp.asarray(np.broadcast_to(slopes[:, None, None], (N_HEADS, 1, LANES)))
    flat = lambda a: jnp.asarray(np.ascontiguousarray(a))
    turned = lambda a: flat(np.swapaxes(a, 1, 2))
    return ((flat(causal),), (turned(causal),)), ((flat(held), flat(far), slopes), (turned(held), turned(far), slopes))


def _pad_heads(w, width):
    kdim, n = w.shape[0], w.shape[1] // width
    return jnp.pad(w.reshape(kdim, n, width), ((0, 0), (0, 0), (0, LANES - width))).reshape(kdim, n * LANES)


def _unpad_heads(w, width):
    kdim, n = w.shape[0], w.shape[1] // LANES
    return w.reshape(kdim, n, LANES)[:, :, :width].reshape(kdim, n * width)


def _pad_head_rows(w, width):
    n, kdim = w.shape[0] // width, w.shape[1]
    return jnp.pad(w.reshape(n, width, kdim), ((0, 0), (0, LANES - width), (0, 0))).reshape(n * LANES, kdim)


def _unpad_head_rows(w, width):
    n, kdim = w.shape[0] // LANES, w.shape[1]
    return w.reshape(n, LANES, kdim)[:, :width].reshape(n * width, kdim)


def _pad_w_in_t(wt):
    n_qkv, pair = 3 * N_HEADS * DIL_DIM, 2 * DIL_DIM
    zeros = lambda n: jnp.zeros((n, wt.shape[1]), wt.dtype)
    main = jnp.concatenate([wt[:LORA_W], zeros(KR_LANE), wt[LORA_W:LORA_W + ROPE], zeros(P_GATE - P_KR - KR_LANE - ROPE),
                            wt[LORA_W + ROPE + n_qkv:]], axis=0)
    qkv = wt[LORA_W + ROPE:LORA_W + ROPE + n_qkv].reshape(3, N_HEADS // 2, pair, wt.shape[1]).transpose(1, 0, 2, 3)
    dil = jnp.pad(qkv, ((0, 0), (0, 1), (0, 0), (0, 0))).reshape(P_DIL, wt.shape[1])
    return main, dil


def _unpad_w_in_t(gt):
    n_qkv, pair = 3 * N_HEADS * DIL_DIM, 2 * DIL_DIM
    moves = [(P_LORA, 0, LORA_W), (P_KR + KR_LANE, LORA_W, ROPE)]
    moves += [(P_HALF + (g * 4 + j) * pair, LORA_W + ROPE + (j * (N_HEADS // 2) + g) * pair, pair) for j in range(3) for g in range(N_HEADS // 2)]
    moves += [(P_GATE + r, LORA_W + ROPE + n_qkv + r, ROW_T) for r in range(0, P_HALF - P_GATE, ROW_T)]
    n = len(moves)

    def body(src, out, buf, sem_in, sem_out):
        ins = [pltpu.make_async_copy(src.at[pl.ds(a, k)], buf.at[pl.ds(b, k)], sem_in.at[i]) for i, (a, b, k) in enumerate(moves)]
        outs = [pltpu.make_async_copy(buf.at[pl.ds(b, k)], out.at[pl.ds(b, k)], sem_out.at[i]) for i, (a, b, k) in enumerate(moves)]
        for cp in ins:
            cp.start()
        for cp_in, cp_out in zip(ins, outs):
            cp_in.wait()
            cp_out.start()
        for cp in outs:
            cp.wait()

    return pl.pallas_call(
        body, name="unpad_d_w_in", in_specs=[pl.BlockSpec(memory_space=pl.ANY)], out_specs=pl.BlockSpec(memory_space=pl.ANY),
        out_shape=jax.ShapeDtypeStruct((IN_WIDTH, gt.shape[1]), gt.dtype),
        scratch_shapes=[pltpu.VMEM((IN_WIDTH, gt.shape[1]), gt.dtype), pltpu.SemaphoreType.DMA((n,)), pltpu.SemaphoreType.DMA((n,))],
        compiler_params=_params(),
    )(gt)


def _split_ukv(w):
    w3 = w.reshape(w.shape[0], N_HEADS, NOPE + V_DIM)
    return _pad_heads(w3[:, :, :NOPE].reshape(w.shape[0], -1), NOPE), w3[:, :, NOPE:].reshape(w.shape[0], -1)


def _merge_ukv(g_k, g_v):
    kdim = g_k.shape[0]
    k3 = _unpad_heads(g_k, NOPE).reshape(kdim, N_HEADS, NOPE)
    return jnp.concatenate([k3, g_v.reshape(kdim, N_HEADS, V_DIM)], axis=2).reshape(kdim, N_HEADS * (NOPE + V_DIM))


def _join_cols(w):
    return w.transpose(1, 0, 2).reshape(w.shape[1], N_CHIPS * w.shape[2])


def _split_cols(g):
    return g.reshape(g.shape[0], N_CHIPS, g.shape[1] // N_CHIPS).transpose(1, 0, 2)


def _local_step(x3, target3, wg, b_gate, g_q_a, g_kv_a, ln1_g, ln1_b, ln2_g, ln2_b, token=None, late_arrived=None, late_weights=None,
                early_grads=None, early_grads_go=None, first_grad=None, last_grads=None, tables=None):
    w_main_t, w_dil_t = _pad_w_in_t(wg["w_in"].reshape(IN_WIDTH, D_MODEL))
    w_uq_pt = _pad_head_rows(wg["w_uq"].reshape(N_HEADS * MLA_QK, Q_LORA), MLA_QK)
    w_ukv = _join_cols(wg["w_ukv"])
    batch, seq, _ = x3.shape
    rows = batch * seq
    x = x3.reshape(rows, D_MODEL)
    target = target3.reshape(rows, D_MODEL)
    row = functools.partial(_rowwise, rows=rows, seq=seq)
    mm = _matmul

    w_uk_p, w_uv = _split_ukv(w_ukv)
    b0, b1 = b_gate[0:1], b_gate[1:2]
    rope_c, rope_up, rope_dn, rope_c_only = _rope_tables(seq)
    (mla_bwd_tables, mla_fwd_tables), (dil_bwd_tables, dil_fwd_tables) = _attention_tables(seq) if tables is None else tables
    scale_mla, scale_dil = MLA_QK ** -0.5, DIL_DIM ** -0.5
    lora0, kr0, gate0 = P_LORA // LORA_W, P_KR // LANES, P_GATE // D_MODEL

    proj = mm(x, w_main_t, mode="nt", name="proj", tm=1024, tn=1536, tk=1024, after=token)
    proj_d = mm(x, w_dil_t, mode="nt", name="proj_dil", tm=1024, tn=1024, tk=1024, out_dtypes=(BF16,))

    def prep(lora, gq, gkv):
        return _rms(lora[:, :Q_LORA], gq), _rms(lora[:, Q_LORA:], gkv)

    qn, kvn = row(prep, name="mla_rms", ins=[(proj, LORA_W, lora0, "row"), (g_q_a, 0, 0, "full"), (g_kv_a, 0, 0, "full")],
                  outs=[(Q_LORA, BF16), (KV_LORA, BF16)])
    q_lin = mm(qn, w_uq_pt, mode="nt", name="q_up", tm=1024, tn=1024, tk=Q_LORA)
    k_lin = mm(kvn, w_uk_p, mode="nn", name="k_up", tm=1024, tn=1024, tk=KV_LORA)
    v_a = mm(kvn, w_uv, mode="nn", name="v_up", tm=1024, tn=1024, tk=KV_LORA, out_dtypes=(BF16,))

    def rope_qk(ql, kl, kr, c, up, dn):
        k_rot = _rope_fwd(kr, c, up, dn)
        qs = [_rope_fwd(ql[:, h * LANES:(h + 1) * LANES], c, up, dn) for h in range(N_HEADS)]
        ks = [kl[:, h * LANES:(h + 1) * LANES] + k_rot for h in range(N_HEADS)]
        return jnp.concatenate(qs, axis=1), jnp.concatenate(ks, axis=1)

    pos = lambda tab: (tab, LANES, 0, "pos")
    q_a, k_a = row(rope_qk, name="rope_qk",
                   ins=[(q_lin, D_MODEL, 0, "row"), (k_lin, D_MODEL, 0, "row"), (proj, LANES, kr0, "row"), pos(rope_c), pos(rope_up), pos(rope_dn)],
                   outs=[(N_HEADS * LANES, BF16), (N_HEADS * LANES, BF16)])
    o_a, lse_a = _attn_fwd(q_a, 0, k_a, 0, v_a, 0, mla_fwd_tables, scale_mla, name="mla_fwd", batch=batch, seq=seq, sub=2, wide_qk=True)
    arrived = None if late_arrived is None else late_arrived(o_a)
    o_b, lse_b = _attn_fwd(proj_d, 0, proj_d, 1, proj_d, 2, dil_fwd_tables, scale_dil, name="dil_fwd", batch=batch, seq=seq, sub=2, stride=4, after=arrived)
    late = wg if late_weights is None else late_weights(o_b)
    w_oa = _join_cols(late["w_o_mla"])
    w_ob = _join_cols(late["w_o_dil"])
    w_out, w_ff1, w_ff2 = late["w_out"].reshape(D_MODEL, D_MODEL), late["w_ff1"], late["w_ff2"].reshape(D_FF, D_MODEL)
    y_a = mm(o_a, w_oa, mode="nn", name="o_mla", tm=1024, tn=1024, tk=1024, out_dtypes=(BF16,))
    y_b = mm(o_b, w_ob, mode="nn", name="o_dil", tm=1024, tn=1024, tk=1024, out_dtypes=(BF16,))

    def gate(t0, t1, c0, c1, ya, yb):
        return (jax.nn.sigmoid(t0 + c0) * ya + jax.nn.sigmoid(t1 + c1) * yb,)

    gate_ins = [(proj, D_MODEL, gate0, "row"), (proj, D_MODEL, gate0 + 1, "row"), (b0, 0, 0, "full"), (b1, 0, 0, "full")]
    (u,) = row(gate, name="gate", ins=gate_ins + [(y_a, D_MODEL, 0, "row"), (y_b, D_MODEL, 0, "row")], outs=[(D_MODEL, BF16)])
    mixed = mm(u, w_out, mode="nn", name="mix", tm=1024, tn=1024, tk=1024)

    def ln1(xv, mv, g, b):
        r1 = ALPHA * xv + mv
        xh, _ = _ln_stats(r1)
        hv = xh * g + b
        return r1, hv, hv

    r1, h, h_b = row(ln1, name="ln1", ins=[(x, D_MODEL, 0, "row"), (mixed, D_MODEL, 0, "row"), (ln1_g, 0, 0, "full"), (ln1_b, 0, 0, "full")],
                outs=[(D_MODEL, F32), (D_MODEL, F32), (D_MODEL, BF16)])

    def relu2(acc):
        r = jnp.maximum(acc, 0.0)
        return (r * r,)

    z = mm(h_b, w_ff1, mode="nn", name="ff1", tm=1024, tn=1024, tk=1024, out_dtypes=(BF16,), epilogue=relu2, b_shards=True)
    f = mm(z, w_ff2, mode="nn", name="ff2", tm=1024, tn=1024, tk=2048)

    def ln2_loss(hv, fv, tv, g, b):
        xh, r = _ln_stats(ALPHA * hv + fv)
        err = xh * g + b - tv
        dy = err * (1.0 / D_MODEL)
        dr2, dg, db = _ln_bwd(xh, r, g, dy)
        loss = jnp.sum(_colsum(err * err), axis=1, keepdims=True) * (0.5 / D_MODEL)
        return dr2, dr2, jnp.broadcast_to(loss, (1, LANES)), dg, db

    dr2, dr2_b, loss_l, d_ln2_g, d_ln2_b = row(
        ln2_loss, name="ln2_loss",
        ins=[(h, D_MODEL, 0, "row"), (f, D_MODEL, 0, "row"), (target, D_MODEL, 0, "row"), (ln2_g, 0, 0, "full"), (ln2_b, 0, 0, "full")],
        outs=[(D_MODEL, F32), (D_MODEL, BF16)], sums=[LANES, D_MODEL, D_MODEL])

    d_w_ff2 = mm(z, dr2_b, mode="tn", name="d_w_ff2", tm=1024, tn=1024, tk=2048)
    da = mm(dr2_b, w_ff2, mode="nt", name="d_ff_act", tm=1024, tn=1024, tk=1024, out_dtypes=(BF16,), extras=(z,),
            epilogue=lambda acc, zv: (acc * (2.0 * jnp.sqrt(zv.astype(F32))),))
    d_w_ff1 = mm(h_b, da, mode="tn", name="d_w_ff1", tm=1024, tn=1024, tk=2048, out_shards=True)
    dh = mm(da, w_ff1, mode="nt", name="d_h", tm=1024, tn=1024, tk=1024, extras=(dr2,), epilogue=lambda acc, rv: (acc + ALPHA * rv,), b_shards=True)

    def ln1_bwd(dhv, r1v, g):
        xh, r = _ln_stats(r1v)
        return _ln_bwd(xh, r, g, dhv)

    dr1, d_ln1_g, d_ln1_b = row(ln1_bwd, name="ln1_bwd", ins=[(dh, D_MODEL, 0, "row"), (r1, D_MODEL, 0, "row"), (ln1_g, 0, 0, "full")],
                                outs=[(D_MODEL, F32)], sums=[D_MODEL, D_MODEL])
    d_w_out = mm(u, dr1, mode="tn", name="d_w_out", tm=1024, tn=1024, tk=1024)
    du = mm(dr1, w_out, mode="nt", name="d_u", tm=1024, tn=1024, tk=1024, out_dtypes=(BF16,))

    def gate_bwd(t0, t1, c0, c1, ya, yb, duv):
        s0, s1 = jax.nn.sigmoid(t0 + c0), jax.nn.sigmoid(t1 + c1)
        dt0 = duv * ya * s0 * (1.0 - s0)
        dt1 = duv * yb * s1 * (1.0 - s1)
        return duv * s0, duv * s1, jnp.concatenate([dt0, dt1], axis=1), jnp.concatenate([_colsum(dt0), _colsum(dt1)], axis=1)

    dy_a, dy_b, d_gates, d_b_gate = row(
        gate_bwd, name="gate_bwd", ins=gate_ins + [(y_a, D_MODEL, 0, "row"), (y_b, D_MODEL, 0, "row"), (du, D_MODEL, 0, "row")],
        outs=[(D_MODEL, BF16), (D_MODEL, BF16), (2 * D_MODEL, BF16)], sums=[2 * D_MODEL])
    d_w_oa = mm(o_a, dy_a, mode="tn", name="d_w_o_mla", tm=1024, tn=1024, tk=1024)
    d_w_ob = mm(o_b, dy_b, mode="tn", name="d_w_o_dil", tm=1024, tn=1024, tk=1024)
    grads = dict(w_o_mla=_split_cols(d_w_oa), w_o_dil=_split_cols(d_w_ob),
                 w_out=d_w_out.reshape(N_CHIPS, D_MODEL // N_CHIPS, D_MODEL), w_ff1=d_w_ff1, w_ff2=d_w_ff2.reshape(N_CHIPS, D_FF // N_CHIPS, D_MODEL))
    sent = None if early_grads is None else early_grads(grads)
    do_a = mm(dy_a, w_oa, mode="nt", name="d_o_mla", tm=1024, tn=1024, tk=1024, after=sent)
    do_b = mm(dy_b, w_ob, mode="nt", name="d_o_dil", tm=1024, tn=1024, tk=1024)
    dq_a, dk_a, dv_a = _attn_bwd(q_a, 0, k_a, 0, v_a, 0, o_a, do_a, lse_a, mla_bwd_tables, scale_mla,
                                 name="mla_bwd", batch=batch, seq=seq, out_dtype=F32, sub=2, wide_qk=True)
    going = None if early_grads_go is None else early_grads_go(dq_a)
    d_qkv_d = _attn_bwd(proj_d, 0, proj_d, 1, proj_d, 2, o_b, do_b, lse_b, dil_bwd_tables, scale_dil,
                        name="dil_bwd", batch=batch, seq=seq, out_dtype=BF16, sub=2, stride=4, after=going)

    def mla_post(dq, dk, c, up, dn, c_only):
        dqs = [_rope_bwd(dq[:, h * LANES:(h + 1) * LANES], c, up, dn) for h in range(N_HEADS)]
        dk_sum = dk[:, :LANES]
        for h in range(1, N_HEADS):
            dk_sum = dk_sum + dk[:, h * LANES:(h + 1) * LANES]
        return jnp.concatenate(dqs, axis=1), _rope_bwd(dk_sum, c_only, up, dn)

    dq_lin, d_kr = row(mla_post, name="mla_unrope",
                       ins=[(dq_a, D_MODEL, 0, "row"), (dk_a, D_MODEL, 0, "row"), pos(rope_c), pos(rope_up), pos(rope_dn), pos(rope_c_only)],
                       outs=[(N_HEADS * LANES, BF16), (LANES, BF16)])
    d_qn = mm(dq_lin, w_uq_pt, mode="nn", name="d_qn", tm=1024, tn=Q_LORA, tk=1024)
    d_kvn_k = mm(dk_a, w_uk_p, mode="nt", name="d_kvn_k", tm=1024, tn=KV_LORA, tk=1024)
    d_kvn = mm(dv_a, w_uv, mode="nt", name="d_kvn", tm=1024, tn=KV_LORA, tk=1024, extras=(d_kvn_k,), epilogue=lambda acc, e: (acc + e,))

    def rms_bwd(lora, dq, dkv, dkr, gq, gkv):
        dxq, dgq = _rms_bwd(lora[:, :Q_LORA], gq, dq)
        dxk, dgk = _rms_bwd(lora[:, Q_LORA:], gkv, dkv)
        tail = jnp.zeros((dxq.shape[0], P_GATE - P_KR - LANES), F32)
        return jnp.concatenate([dxq, dxk, dkr.astype(F32), tail], axis=1), dgq, dgk

    d_tail, d_g_q_a, d_g_kv_a = row(
        rms_bwd, name="mla_rms_bwd",
        ins=[(proj, LORA_W, lora0, "row"), (d_qn, Q_LORA, 0, "row"), (d_kvn, KV_LORA, 0, "row"), (d_kr, LANES, 0, "row"),
             (g_q_a, 0, 0, "full"), (g_kv_a, 0, 0, "full")],
        outs=[(P_GATE, BF16)], sums=[Q_LORA, KV_LORA])
    d_proj = [d_tail, d_gates, d_qkv_d]
    d_w_in_pt = mm(d_proj, x, mode="tn", name="d_w_in", tm=1024, tn=1024, tk=1024)
    d_w_in = _unpad_w_in_t(d_w_in_pt).reshape(N_CHIPS, IN_WIDTH // N_CHIPS, D_MODEL)
    moving = None if first_grad is None else first_grad(d_w_in)
    d_w_uq_pt = mm(dq_lin, qn, mode="tn", name="d_w_uq", tm=1024, tn=Q_LORA, tk=1024, after=moving)
    d_w_uk_p = mm(kvn, dk_a, mode="tn", name="d_w_uk", tm=KV_LORA, tn=1024, tk=1024, after=moving)
    d_w_uv = mm(kvn, dv_a, mode="tn", name="d_w_uv", tm=KV_LORA, tn=1024, tk=1024, after=moving)
    grads.update(w_in=d_w_in,
                 w_uq=_unpad_head_rows(d_w_uq_pt, MLA_QK).reshape(N_CHIPS, N_HEADS * MLA_QK // N_CHIPS, Q_LORA),
                 w_ukv=_split_cols(_merge_ukv(d_w_uk_p, d_w_uv)))
    leaving = None if last_grads is None else last_grads(grads)
    grad_x = mm(d_proj, jnp.concatenate([w_main_t, w_dil_t], axis=0), mode="nn", name="d_x", tm=1024, tn=1024, tk=1024, extras=(dr1,), epilogue=lambda acc, rv: (acc + ALPHA * rv,),
                after=leaving)

    grads.update(
        b_gate=d_b_gate.reshape(2, D_MODEL), g_q_a=d_g_q_a, g_kv_a=d_g_kv_a, ln1_g=d_ln1_g, ln1_b=d_ln1_b, ln2_g=d_ln2_g, ln2_b=d_ln2_b)
    return loss_l, grad_x.reshape(batch, seq, D_MODEL), grads


BIG = ("w_in", "w_uq", "w_ukv", "w_o_mla", "w_o_dil", "w_out", "w_ff1", "w_ff2")
SMALL = (("b_gate", 2 * D_MODEL), ("g_q_a", Q_LORA), ("g_kv_a", KV_LORA), ("ln1_g", D_MODEL), ("ln1_b", D_MODEL),
         ("ln2_g", D_MODEL), ("ln2_b", D_MODEL))
TRANSPOSED = ("w_in", "w_uq")
D2D_PIECES = (4, 2, 1)
ANY = pl.BlockSpec(memory_space=pl.ANY)
SIDE_EFFECTS = pltpu.CompilerParams(has_side_effects=True)


def _place():
    x, y, c = lax.axis_index("x"), lax.axis_index("y"), lax.axis_index("c")
    return x, y, c, ((1 - x, y), (x, 1 - y), (1 - x, 1 - y))


def _half_axis(shape):
    return 0 if shape[0] % 32 == 0 else 1


def _half_shape(shape):
    return (shape[0] // 2, shape[1]) if _half_axis(shape) == 0 else (shape[0], shape[1] // 2)


def _window(ref, lead, shape, which=None, pieces=False):
    axis = _half_axis(shape)
    size = shape[axis] if which is None else shape[axis] // 2
    base = 0 if which is None else which * size
    tile = (16, LANES)[axis]
    count = next(c for c in D2D_PIECES if size % (tile * c) == 0) if pieces else 1
    step = size // count
    spans = [pl.ds(pl.multiple_of(base + i * step, tile), step) for i in range(count)]
    refs = [ref.at[(*lead, s)] if axis == 0 else ref.at[(*lead, slice(None), s)] for s in spans]
    return refs if pieces else refs[0]


def _remote(src, dst, send, recv, to):
    return pltpu.make_async_remote_copy(src_ref=src, dst_ref=dst, send_sem=send, recv_sem=recv, device_id=to, device_id_type=MESH)


def _pair_split(grads, name):
    n = len(grads)

    def body(*refs):
        srcs, outs, (send, recv) = refs[:n], refs[n:2 * n], refs[2 * n:]
        x, y, c, _ = _place()
        for t in range(n):
            for s in range(N_CHIPS):
                _remote(_window(srcs[t], (s,), grads[t].shape[1:], 1 - c), outs[t].at[s], send.at[t], recv.at[t], (x, y, 1 - c)).start()
        for t in range(n):
            _remote(_window(srcs[t], (slice(None),), grads[t].shape[1:], 1 - c), outs[t], send.at[t], recv.at[t], (x, y, 1 - c)).wait()

    return pl.pallas_call(
        body, name=name, in_specs=[ANY] * n, out_specs=[ANY] * n,
        out_shape=[jax.ShapeDtypeStruct((N_CHIPS,) + _half_shape(g.shape[1:]), g.dtype) for g in grads],
        scratch_shapes=[pltpu.SemaphoreType.DMA((n,)), pltpu.SemaphoreType.DMA((n,))],
        compiler_params=SIDE_EFFECTS,
    )(*grads)


HBM = pl.BlockSpec(memory_space=pltpu.HBM)
SEM = pl.BlockSpec(memory_space=pltpu.SEMAPHORE)
SPLIT = pltpu.CompilerParams(has_side_effects=pltpu.SideEffectType.DATAFLOW_SIDE_EFFECTING)


def _in_hbm(a):
    return pltpu.with_memory_space_constraint(a, pltpu.HBM)


def _split_copies(kind, srcs, lands):
    x, y, c, chips = _place()
    out = []
    for t in range(len(srcs)):
        if kind == "pair":
            out += [(t, s % 3, _window(srcs[t], (s,), srcs[t].shape[1:], 1 - c), lands[t].at[s], (x, y, 1 - c)) for s in range(N_CHIPS)]
            continue
        if kind == "join":
            out += [(t, 0, a, b, (x, y, 1 - c)) for a, b in zip(_window(srcs[t], (), srcs[t].shape, None, True), _window(lands[t], (), srcs[t].shape, None, True))]
            continue
        if kind == "forward":
            shape, sibling = srcs[t].shape, (x, y, 1 - c)
            out += [(t, 0, a, b, sibling) for a, b in zip(_window(srcs[t], (), shape, None, True), _window(lands[t], (2 * x + y,), shape, None, True))]
            out += [(t, j, a, a, sibling) for j, (cx, cy) in enumerate(chips) for a in _window(lands[t], (2 * cx + cy,), shape, c, True)]
            continue
        for j, (cx, cy) in enumerate(chips):
            if kind == "gather":
                shape = srcs[t].shape
                out.append((t, j, _window(srcs[t], (), shape, c), _window(lands[t], (2 * x + y,), shape, c), (cx, cy, c)))
            else:
                out.append((t, j, srcs[t].at[2 * cx + cy], lands[t].at[j], (cx, cy, c)))
    return out


def _split_start(kind, srcs, land_shapes, name, lands=None, after=None):
    n = len(srcs)

    def body(*refs):
        src_refs, land_refs, sems, token = refs[:n], refs[n:2 * n], refs[-7 - 2 * n:-1 - 2 * n], refs[-1]
        for t, j, s, d, to in _split_copies(kind, src_refs, land_refs):
            _remote(s, d, sems[j], sems[3 + j], to).start()
        token[...] = jnp.zeros_like(token)

    lands = [_in_hbm(lax.empty(s.shape, s.dtype)) for s in land_shapes] if lands is None else list(lands)
    thru = [pltpu.HBM(a.shape, a.dtype) for a in list(srcs) + lands]
    res = pl.pallas_call(
        body, name=name,
        out_shape=(*[pltpu.SemaphoreType.DMA(())] * 6, *thru, jax.ShapeDtypeStruct((8, LANES), F32)),
        in_specs=[HBM] * (2 * n) + [ANY] * (after is not None), out_specs=(*[SEM] * 6, *[HBM] * (2 * n), pl.BlockSpec(memory_space=pltpu.VMEM)),
        input_output_aliases={i: 6 + i for i in range(2 * n)}, compiler_params=SPLIT,
    )(*[_in_hbm(s) for s in srcs], *lands, *([after] if after is not None else []))
    return res[:6], res[6:6 + n], res[6 + n:6 + 2 * n], res[-1]


def _split_wait(kind, sems, srcs, lands, after, name):
    n = len(srcs)

    def body(*refs):
        src_refs, land_refs, sem_refs = refs[:n], refs[n:2 * n], refs[2 * n:2 * n + 6]
        for t, j, s, d, to in _split_copies(kind, src_refs, land_refs):
            cp = _remote(s, d, sem_refs[j], sem_refs[3 + j], to)
            cp.wait_send()
            cp.wait_recv()

    res = pl.pallas_call(
        body, name=name, out_shape=[pltpu.HBM(a.shape, a.dtype) for a in list(srcs) + list(lands)],
        in_specs=[HBM] * (2 * n) + [SEM] * 6 + [ANY], out_specs=[HBM] * (2 * n),
        input_output_aliases={i: i for i in range(2 * n)}, compiler_params=SPLIT,
    )(*srcs, *lands, *sems, after)
    return res[:n], res[n:]


def _sum_all_devices(vec, name):
    n_rows = vec.shape[0]

    def body(v_ref, out_ref, buf, send, recv):
        x, y, c, _ = _place()
        me = 4 * x + 2 * y + c
        buf[me] = v_ref[...]
        flips = [(a, b, d) for a in (0, 1) for b in (0, 1) for d in (0, 1)][1:]
        copies = []
        for r, (a, b, d) in enumerate(flips):
            px, py, pc = (1 - x if a else x), (1 - y if b else y), (1 - c if d else c)
            copies.append(pltpu.make_async_remote_copy(src_ref=v_ref, dst_ref=buf.at[me], send_sem=send.at[r], recv_sem=recv.at[r],
                                                       device_id=(px, py, pc), device_id_type=MESH))
            copies[-1].start()
        for r, (a, b, d) in enumerate(flips):
            px, py, pc = (1 - x if a else x), (1 - y if b else y), (1 - c if d else c)
            pltpu.make_async_remote_copy(src_ref=v_ref, dst_ref=buf.at[4 * px + 2 * py + pc], send_sem=send.at[r], recv_sem=recv.at[r],
                                         device_id=(px, py, pc), device_id_type=MESH).wait_recv()
        for cp in copies:
            cp.wait_send()
        total = buf[0]
        for k in range(1, N_DEV):
            total = total + buf[k]
        out_ref[...] = total

    vmem = pl.BlockSpec(memory_space=pltpu.VMEM)
    return pl.pallas_call(
        body, name=name, in_specs=[vmem], out_specs=vmem, out_shape=jax.ShapeDtypeStruct(vec.shape, F32),
        scratch_shapes=[pltpu.VMEM((N_DEV, n_rows, LANES), F32), pltpu.SemaphoreType.DMA((N_DEV - 1,)), pltpu.SemaphoreType.DMA((N_DEV - 1,))],
        compiler_params=pltpu.CompilerParams(has_side_effects=True),
    )(vec)


def _half_tile(half, width):
    t = half
    while t * width * 4 > (2 << 20) and t % 32 == 0:
        t //= 2
    return t


def _pair_add(g, theirs, core, name):
    _, half, width = theirs.shape
    t = _half_tile(half, width)
    n = half // t

    def body(c_ref, a_ref, b_ref, o_ref):
        o_ref[...] = (a_ref[...] + b_ref[...]).astype(BF16)

    tile = pl.BlockSpec((1, t, width), lambda j, i, c_ref: (j, i, 0))
    if _half_axis(g.shape[1:]) == 0:
        mine = pl.BlockSpec((1, t, width), lambda j, i, c_ref: (j, c_ref[0] * n + i, 0))
    else:
        mine = pl.BlockSpec((1, t, width), lambda j, i, c_ref: (j, i, c_ref[0]))
    return pl.pallas_call(
        body, name=name,
        grid_spec=pltpu.PrefetchScalarGridSpec(num_scalar_prefetch=1, grid=(N_CHIPS, n), in_specs=[mine, tile], out_specs=tile),
        out_shape=jax.ShapeDtypeStruct(theirs.shape, BF16), compiler_params=_params(("parallel", "parallel")),
    )(core, g, theirs)


def _chip_sum(part, others, chip, name, after=None):
    _, half, width = part.shape
    t = _half_tile(half, width)

    def body(s_ref, mine, p0, p1, p2, *rest):
        o_ref = rest[-1]
        o_ref[...] = ((mine[0].astype(F32) + p0[0].astype(F32)) + p1[0].astype(F32)) + p2[0].astype(F32)

    return pl.pallas_call(
        body, name=name,
        grid_spec=pltpu.PrefetchScalarGridSpec(
            num_scalar_prefetch=1, grid=(half // t,),
            in_specs=[pl.BlockSpec((1, t, width), lambda i, s_ref: (s_ref[0], i, 0))]
            + [pl.BlockSpec((1, t, width), lambda i, s_ref, j=j: (j, i, 0)) for j in range(3)] + [pl.BlockSpec(memory_space=pl.ANY)] * (after is not None),
            out_specs=pl.BlockSpec((t, width), lambda i, s_ref: (i, 0))),
        out_shape=jax.ShapeDtypeStruct((half, width), F32), compiler_params=_params(("parallel",)),
    )(chip, part, others, others, others, *([after] if after is not None else []))


EARLY = ("w_in", "w_uq", "w_ukv")
LATE = ("w_o_mla", "w_o_dil", "w_out", "w_ff1", "w_ff2")


def _sum_small(vals):
    n_in = len(vals)
    n_rows = sum(a.shape[0] * a.shape[1] // LANES for a in vals)
    pad_rows = -(-n_rows // 8) * 8

    def chunks(refs):
        return [(ref, a, j) for ref in refs for a in range(ref.shape[0]) for j in range(ref.shape[1] // LANES)]

    def body(*refs):
        ins, outs, (buf, send, recv) = refs[:n_in], refs[n_in:2 * n_in], refs[2 * n_in:]
        x, y, c, _ = _place()
        me = 4 * x + 2 * y + c
        for r, (ref, a, j) in enumerate(chunks(ins)):
            buf[me, r:r + 1, :] = ref[a:a + 1, j * LANES:(j + 1) * LANES]
        if pad_rows > n_rows:
            buf[me, n_rows:pad_rows, :] = jnp.zeros((pad_rows - n_rows, LANES), F32)
        flips = [(a, b, d) for a in (0, 1) for b in (0, 1) for d in (0, 1)][1:]
        peers = [((1 - x if a else x), (1 - y if b else y), (1 - c if d else c)) for a, b, d in flips]
        copies = [_remote(buf.at[me], buf.at[me], send.at[r], recv.at[r], peer) for r, peer in enumerate(peers)]
        for cp in copies:
            cp.start()
        for r, (px, py, pc) in enumerate(peers):
            _remote(buf.at[me], buf.at[4 * px + 2 * py + pc], send.at[r], recv.at[r], (px, py, pc)).wait_recv()
        for cp in copies:
            cp.wait_send()
        total = buf[0]
        for k in range(1, N_DEV):
            total = total + buf[k]
        for r, (ref, a, j) in enumerate(chunks(outs)):
            ref[a:a + 1, j * LANES:(j + 1) * LANES] = total[r:r + 1, :]

    vmem = pl.BlockSpec(memory_space=pltpu.VMEM)
    return pl.pallas_call(
        body, name="sum_small", in_specs=[vmem] * n_in, out_specs=[vmem] * n_in,
        out_shape=[jax.ShapeDtypeStruct(a.shape, F32) for a in vals],
        scratch_shapes=[pltpu.VMEM((N_DEV, pad_rows, LANES), F32), pltpu.SemaphoreType.DMA((N_DEV - 1,)), pltpu.SemaphoreType.DMA((N_DEV - 1,))],
        compiler_params=SIDE_EFFECTS,
    )(*vals)


def _adam_math(w, g, m, v):
    nm = B1 * m + (1.0 - B1) * g
    nv = B2 * v + (1.0 - B2) * (g * g)
    m_hat = nm / (1.0 - B1 ** ADAM_STEP)
    v_hat = nv / (1.0 - B2 ** ADAM_STEP)
    return -LR * (m_hat / (jnp.sqrt(v_hat) + ADAM_EPS) + WD * w), nm, nv


def _adamw_big(w, mine, theirs, m, v, core, name, side_by_side=False):
    rows, width = w.shape
    if side_by_side:
        t = next(c for c in (152, 96, 64, 32, 16, 8) if rows % c == 0)
        hb = None
        half_spec = pl.BlockSpec((t, width // 2), lambda i, c_ref: (i, 0))
    else:
        t = next(c for c in (256, 128, 64, 32, 16, 8) if (rows // 2) % c == 0)
        hb = rows // 2 // t
        half_spec = pl.BlockSpec((t, width), lambda i, c_ref: (i % hb, 0))

    def body(c_ref, w_ref, a_ref, b_ref, m_ref, v_ref, g_ref, d_ref, nm_ref, nv_ref):
        south = c_ref[0] == 0
        if side_by_side:
            g = jnp.where(south, jnp.concatenate([a_ref[...], b_ref[...]], axis=1), jnp.concatenate([b_ref[...], a_ref[...]], axis=1))
        else:
            g = jnp.where((pl.program_id(0) < hb) == south, a_ref[...], b_ref[...])
        g_ref[...] = g
        d_ref[...], nm_ref[...], nv_ref[...] = _adam_math(w_ref[...], g, m_ref[...], v_ref[...])

    spec = pl.BlockSpec((t, width), lambda i, c_ref: (i, 0))
    return pl.pallas_call(
        body, name=name,
        grid_spec=pltpu.PrefetchScalarGridSpec(num_scalar_prefetch=1, grid=(rows // t,),
                                               in_specs=[spec, half_spec, half_spec, spec, spec], out_specs=[spec] * 4),
        out_shape=[jax.ShapeDtypeStruct(w.shape, F32)] * 4, compiler_params=_params(("parallel",)),
    )(core, w, mine, theirs, m, v)


def _adamw_small(ws, gs, ms, vs):
    n = len(ws)

    def body(*refs):
        for t in range(n):
            w_ref, g_ref, m_ref, v_ref = (refs[k * n + t] for k in range(4))
            d, nm, nv = _adam_math(w_ref[...], g_ref[...], m_ref[...], v_ref[...])
            refs[4 * n + t][...] = d
            refs[5 * n + t][...] = nm
            refs[6 * n + t][...] = nv

    vmem = pl.BlockSpec(memory_space=pltpu.VMEM)
    res = pl.pallas_call(body, name="adamw_small", in_specs=[vmem] * (4 * n), out_specs=[vmem] * (3 * n),
                         out_shape=[jax.ShapeDtypeStruct(a.shape, F32) for a in ws] * 3)(*ws, *gs, *ms, *vs)
    return res[:n], res[n:2 * n], res[2 * n:]


def kernel(x, w_in, b_gate, g_q_a, w_uq, g_kv_a, w_ukv, w_o_mla, w_o_dil, w_out, ln1_g, ln1_b, w_ff1, w_ff2, ln2_g, ln2_b, loss_target, m_w_in, m_b_gate, m_g_q_a, m_w_uq, m_g_kv_a, m_w_ukv, m_w_o_mla, m_w_o_dil, m_w_out, m_ln1_g, m_ln1_b, m_w_ff1, m_w_ff2, m_ln2_g, m_ln2_b, v_w_in, v_b_gate, v_g_q_a, v_w_uq, v_g_kv_a, v_w_ukv, v_w_o_mla, v_w_o_dil, v_w_out, v_ln1_g, v_ln1_b, v_w_ff1, v_w_ff2, v_ln2_g, v_ln2_b):
    order = ("w_in", "b_gate", "g_q_a", "w_uq", "g_kv_a", "w_ukv", "w_o_mla", "w_o_dil", "w_out", "ln1_g", "ln1_b", "w_ff1", "w_ff2", "ln2_g", "ln2_b")
    w = dict(w_in=w_in, b_gate=b_gate, g_q_a=g_q_a, w_uq=w_uq, g_kv_a=g_kv_a, w_ukv=w_ukv, w_o_mla=w_o_mla, w_o_dil=w_o_dil, w_out=w_out,
             ln1_g=ln1_g, ln1_b=ln1_b, w_ff1=w_ff1, w_ff2=w_ff2, ln2_g=ln2_g, ln2_b=ln2_b)
    m = dict(w_in=m_w_in, b_gate=m_b_gate, g_q_a=m_g_q_a, w_uq=m_w_uq, g_kv_a=m_g_kv_a, w_ukv=m_w_ukv, w_o_mla=m_w_o_mla, w_o_dil=m_w_o_dil,
             w_out=m_w_out, ln1_g=m_ln1_g, ln1_b=m_ln1_b, w_ff1=m_w_ff1, w_ff2=m_w_ff2, ln2_g=m_ln2_g, ln2_b=m_ln2_b)
    v = dict(w_in=v_w_in, b_gate=v_b_gate, g_q_a=v_g_q_a, w_uq=v_w_uq, g_kv_a=v_g_kv_a, w_ukv=v_w_ukv, w_o_mla=v_w_o_mla, w_o_dil=v_w_o_dil,
             w_out=v_w_out, ln1_g=v_ln1_g, ln1_b=v_ln1_b, w_ff1=v_w_ff1, w_ff2=v_w_ff2, ln2_g=v_ln2_g, ln2_b=v_ln2_b)
    chip = 2 * lax.axis_index("x") + lax.axis_index("y")
    south = (lax.axis_index("c") == 0).astype(F32)
    gate_w = D_MODEL // N_CHIPS

    core = lax.axis_index("c").astype(jnp.int32).reshape(1)
    turn = lambda n, a: a.T if n in TRANSPOSED else a
    shards = {n: turn(n, w[n][0]).astype(BF16) for n in BIG}
    early_shards, late_shards = [shards[n] for n in EARLY], [shards[n] for n in LATE]
    gathered = lambda group: [jax.ShapeDtypeStruct((N_CHIPS,) + s.shape, BF16) for s in group]
    e_sems, e_srcs, e_lands, e_token = _split_start("gather", early_shards, gathered(early_shards), "gather_early_start")
    g_sems, g_srcs, g_lands, g_token = _split_start("gather", late_shards, gathered(late_shards), "gather_late_start", after=e_token)
    tables = _attention_tables(x.shape[1])
    e_srcs, e_lands = _split_wait("gather", e_sems, e_srcs, e_lands, tables[1][0][1], "gather_early_wait")
    e_forward = _split_start("forward", e_srcs, None, "gather_early_forward_start", lands=e_lands)
    first = dict(zip(EARLY, _split_wait("forward", *e_forward[:3], e_forward[-1], "gather_early_forward_wait")[1]))
    b_mine = lax.dynamic_update_slice(jnp.zeros((2, D_MODEL), F32), b_gate[0] * south, (0, chip * gate_w))
    b_full = _sum_all_devices(b_mine.reshape(-1, LANES), "gather_b_gate").reshape(2, D_MODEL)

    sent = {}

    def late_arrived(after):
        srcs, lands = _split_wait("gather", g_sems, g_srcs, g_lands, after, "gather_late_wait")
        sent["forward"] = _split_start("forward", srcs, None, "gather_late_forward_start", lands=lands)
        return sent["forward"][-1]

    def late_weights(after):
        return dict(zip(LATE, _split_wait("forward", *sent["forward"][:3], after, "gather_late_forward_wait")[1]))

    exchange_shapes = lambda parts: [jax.ShapeDtypeStruct((3,) + p.shape[1:], BF16) for p in parts]

    def early_grads(grads_late):
        gs = [grads_late[n] for n in LATE]
        shapes = [jax.ShapeDtypeStruct((N_CHIPS,) + _half_shape(g.shape[1:]), F32) for g in gs]
        sent["pair"] = _split_start("pair", gs, shapes, "pair_split_late_start")
        return sent["pair"][-1]

    def early_grads_go(after):
        gs, theirs = _split_wait("pair", *sent["pair"][:3], after, "pair_split_late_wait")
        parts = [_pair_add(g, th, core, "pair_add_" + n) for g, th, n in zip(gs, theirs, LATE)]
        sent["late"] = _split_start("scatter", parts, exchange_shapes(parts), "exchange_late_start")
        return sent["late"][-1]

    def first_grad(g):
        sent["first"] = _split_start("pair", [g], [jax.ShapeDtypeStruct((N_CHIPS,) + _half_shape(g.shape[1:]), F32)], "pair_split_w_in_start")
        return sent["first"][-1]

    def last_grads(grads_early):
        rest = [grads_early[n] for n in EARLY[1:]]
        gs, theirs = _split_wait("pair", *sent["first"][:3], rest[-1], "pair_split_w_in_wait")
        gs, theirs = list(gs) + rest, list(theirs) + list(_pair_split(rest, "pair_split_early"))
        parts = [_pair_add(g, th, core, "pair_add_" + n) for g, th, n in zip(gs, theirs, EARLY)]
        sent["early"] = _split_start("scatter", parts, exchange_shapes(parts), "exchange_early_start")
        return sent["early"][-1]

    loss_part, grad_x, grads = _local_step(x, loss_target, first, b_full, g_q_a, g_kv_a, ln1_g, ln1_b, ln2_g, ln2_b, token=g_token,
                                           late_arrived=late_arrived, late_weights=late_weights, early_grads=early_grads, early_grads_go=early_grads_go,
                                           first_grad=first_grad, last_grads=last_grads, tables=tables)

    g_out, delta, new_m, new_v = {}, {}, {}, {}
    chip1 = chip.astype(jnp.int32).reshape(1)

    def sum_and_send(names, parts, others, tag):
        totals = [_chip_sum(p, o, chip1, "chip_sum_" + n) for n, p, o in zip(names, parts, others)]
        return _split_start("join", totals, [jax.ShapeDtypeStruct(t.shape, F32) for t in totals], "pair_join_" + tag + "_start")

    def adam(names, joined, after, tag):
        totals, halves = _split_wait("join", *joined[:3], after, "pair_join_" + tag + "_wait")
        for n, mine, theirs in zip(names, totals, halves):
            res = _adamw_big(turn(n, w[n][0]), mine, theirs, turn(n, m[n][0]), turn(n, v[n][0]), core, "adamw_" + n,
                             side_by_side=mine.shape[0] == shards[n].shape[0])
            g_out[n], delta[n], new_m[n], new_v[n] = (turn(n, r) for r in res)

    late_joined = sum_and_send(LATE, *_split_wait("scatter", *sent["late"][:3], grad_x, "exchange_late_wait"), "late")
    early_joined = sum_and_send(EARLY, *_split_wait("scatter", *sent["early"][:3], late_joined[-1], "exchange_early_wait"), "early")
    small_names = [name for name, _ in SMALL]
    sums = _sum_small([grads[name] for name in small_names] + [loss_part])
    loss = sums[-1][0, 0]
    g_small = dict(zip(small_names, sums))
    g_small["b_gate"] = lax.dynamic_slice(g_small["b_gate"], (0, chip * gate_w), (2, gate_w))
    flat = lambda a: a.reshape(-1, a.shape[-1])
    res = _adamw_small(*[[flat(d[name]) for name in small_names] for d in (w, g_small, m, v)])
    g_out.update(g_small)
    for d, r in zip((delta, new_m, new_v), res):
        d.update(zip(small_names, r))
    adam(LATE, late_joined, res[0][0], "late")
    adam(EARLY, early_joined, delta[LATE[-1]], "early")

    lead = lambda d: [d[name].reshape(w[name].shape) for name in order]
    return (loss, grad_x, *lead(g_out), *lead(delta), *lead(new_m), *lead(new_v))
```

```python
import functools
import math

import jax
import jax.numpy as jnp
import numpy as np
from jax import lax
from jax.experimental import pallas as pl
from jax.experimental.pallas import tpu as pltpu

F32 = jnp.float32
BF16 = jnp.bfloat16
MESH = pl.DeviceIdType.MESH

D_MODEL = 1024
N_HEADS = 8
LANES = 128
NOPE, ROPE, V_DIM = 64, 32, 64
MLA_QK = NOPE + ROPE
Q_LORA, KV_LORA = 384, 256
DIL_DIM = 64
DIL_PATTERNS = ((128, 1), (512, 4), (2048, 16))
D_FF = 4096
N_CHIPS = 4
N_DEV = 8
IN_WIDTH = 4256
LN_EPS, RMS_EPS = 1e-5, 1e-6
NEG = -1e30
LOG2E, LN2 = 1.4426950408889634, 0.6931471805599453
ALPHA = 2.0 ** 0.25
ROPE_THETA = 10000.0
LR, B1, B2, ADAM_EPS, WD, ADAM_STEP = 0.001, 0.9, 0.999, 1e-8, 0.01, 10

P_LORA, P_KR, P_GATE, P_HALF = 0, 640, 1024, 3072
DIL_GROUP = 4 * LANES
P_DIL = N_HEADS // 2 * DIL_GROUP
LORA_W = Q_LORA + KV_LORA
KR_LANE = NOPE

ATT_T = 512
ROW_T = 512
VMEM_LIMIT = 56 * 1024 * 1024

NN = (((1,), (0,)), ((), ()))
NT = (((1,), (1,)), ((), ()))
TN = (((0,), (0,)), ((), ()))


def _params(sem=None, **kw):
    return pltpu.CompilerParams(dimension_semantics=sem, vmem_limit_bytes=VMEM_LIMIT, **kw)


def _matmul(a, b, *, mode, name, tm, tn, tk, out_dtypes=(F32,), extras=(), epilogue=None, b_shards=False, out_shards=False, after=None):
    pieces = list(a) if isinstance(a, (list, tuple)) else [a]
    n_pc = len(pieces)
    a_shape = (pieces[0].shape[0], sum(p.shape[1] for p in pieces))
    if b_shards:
        n_sh, rows_b, cols_b = b.shape
        b_shape = (rows_b, n_sh * cols_b)
    else:
        b_shape = b.shape
    if mode == "nn":
        (m, k), (k2, n) = a_shape, b_shape
    elif mode == "nt":
        (m, k), (n, k2) = a_shape, b_shape
    else:
        (k, m), (k2, n) = a_shape, b_shape
    assert k == k2, (a_shape, b.shape, mode)
    tm, tn, tk = min(tm, m), min(tn, n), min(tk, k)
    assert m % tm == 0 and n % tn == 0 and k % tk == 0, (name, m, n, k, tm, tn, tk)
    nk = k // tk
    n_ex, n_out = len(extras), len(out_dtypes)
    n_in = n_pc + 1 + n_ex + (after is not None)
    dims = {"nn": NN, "nt": NT, "tn": TN}[mode]
    col_tile = tm if mode == "tn" else tk
    blocks = [p.shape[1] // col_tile for p in pieces]
    firsts = [sum(blocks[:p]) for p in range(n_pc)]
    assert all(p.shape[1] % col_tile == 0 for p in pieces), (name, col_tile)

    def body(*refs):
        a_refs, b_ref = refs[:n_pc], refs[n_pc]
        ex_refs = refs[n_pc + 1:n_pc + 1 + n_ex]
        out_refs = refs[n_in:n_in + n_out]

        def finish(acc):
            outs = epilogue(acc, *[r[...] for r in ex_refs]) if epilogue is not None else (acc,)
            for r, o in zip(out_refs, outs):
                r[...] = o.astype(r.dtype)

        kk = pl.program_id(2)

        def step(a_ref):
            part = lax.dot_general(a_ref[...].astype(BF16), b_ref[...].astype(BF16), dims, preferred_element_type=F32)
            if nk == 1:
                finish(part)
                return
            acc_ref = refs[-1]

            @pl.when(kk == 0)
            def _():
                acc_ref[...] = part

            @pl.when(kk > 0)
            def _():
                acc_ref[...] += part

            @pl.when(kk == nk - 1)
            def _():
                finish(acc_ref[...])

        if n_pc == 1:
            step(a_refs[0])
        else:
            at = pl.program_id(0) if mode == "tn" else kk
            for p in range(n_pc):
                pl.when(jnp.logical_and(at >= firsts[p], at < firsts[p] + blocks[p]))(functools.partial(step, a_refs[p]))

    def a_spec_of(p):
        if n_pc == 1:
            return pl.BlockSpec((tk, tm), lambda i, j, kk: (kk, i)) if mode == "tn" else pl.BlockSpec((tm, tk), lambda i, j, kk: (i, kk))
        col = lambda at: jnp.clip(at - firsts[p], 0, blocks[p] - 1)
        mine = lambda at: jnp.logical_and(at >= firsts[p], at < firsts[p] + blocks[p])
        if mode == "tn":
            return pl.BlockSpec((tk, tm), lambda i, j, kk: (jnp.where(mine(i), kk, 0), col(i)))
        return pl.BlockSpec((tm, tk), lambda i, j, kk: (i, col(kk)))

    b_spec = {"nn": pl.BlockSpec((tk, tn), lambda i, j, kk: (kk, j)),
              "nt": pl.BlockSpec((tn, tk), lambda i, j, kk: (j, kk)),
              "tn": pl.BlockSpec((tk, tn), lambda i, j, kk: (kk, j))}[mode]
    tile = pl.BlockSpec((tm, tn), lambda i, j, kk: (i, j))
    out_spec, out_dims = tile, (m, n)
    if b_shards and mode == "nn":
        per = cols_b // tn
        b_spec = pl.BlockSpec((None, tk, tn), lambda i, j, kk: (j // per, kk, j % per))
    elif b_shards:
        assert mode == "nt"
        per = cols_b // tk
        b_spec = pl.BlockSpec((None, tn, tk), lambda i, j, kk: (kk // per, j, kk % per))
    if out_shards:
        assert not extras and epilogue is None
        per_out = n // N_CHIPS // tn
        out_spec = pl.BlockSpec((None, tm, tn), lambda i, j, kk: (j // per_out, i, j % per_out))
        out_dims = (N_CHIPS, m, n // N_CHIPS)
    outs = pl.pallas_call(
        body, name=name,
        grid=(m // tm, n // tn, nk),
        in_specs=[a_spec_of(p) for p in range(n_pc)] + [b_spec] + [tile] * n_ex + [pl.BlockSpec(memory_space=pl.ANY)] * (after is not None),
        out_specs=[out_spec] * n_out,
        out_shape=[jax.ShapeDtypeStruct(out_dims, dt) for dt in out_dtypes],
        scratch_shapes=[pltpu.VMEM((tm, tn), F32)] if nk > 1 else [],
        compiler_params=_params(("parallel", "parallel", "arbitrary")),
    )(*pieces, b, *extras, *([after] if after is not None else []))
    return outs[0] if n_out == 1 else outs


def _rowwise(fn, *, name, rows, seq, ins, outs, sums=()):
    tm = min(ROW_T, seq)
    n_pos = seq // tm
    n_in, n_out, n_sum = len(ins), len(outs), len(sums)

    def body(*refs):
        vals = fn(*[r[...] for r in refs[:n_in]])
        for r, v in zip(refs[n_in:n_in + n_out], vals[:n_out]):
            r[...] = v.astype(r.dtype)
        first = pl.program_id(0) == 0
        for r, v in zip(refs[n_in + n_out:], vals[n_out:]):
            @pl.when(first)
            def _(r=r, v=v):
                r[...] = v

            @pl.when(jnp.logical_not(first))
            def _(r=r, v=v):
                r[...] += v

    def spec(arr, width, col, kind):
        if kind == "row":
            return pl.BlockSpec((tm, width), lambda i, col=col: (i, col))
        if kind == "pos":
            return pl.BlockSpec((tm, width), lambda i, col=col: (i % n_pos, col))
        return pl.BlockSpec(arr.shape, lambda i: (0,) * arr.ndim)

    res = pl.pallas_call(
        body, name=name,
        grid=(rows // tm,),
        in_specs=[spec(*t) for t in ins],
        out_specs=[pl.BlockSpec((tm, w), lambda i: (i, 0)) for w, _ in outs]
        + [pl.BlockSpec((1, w), lambda i: (0, 0)) for w in sums],
        out_shape=[jax.ShapeDtypeStruct((rows, w), dt) for w, dt in outs]
        + [jax.ShapeDtypeStruct((1, w), F32) for w in sums],
        compiler_params=_params(("arbitrary",)),
    )(*[t[0] for t in ins])
    return res


def _colsum(v):
    return jnp.sum(v, axis=0, keepdims=True)


def _rope_fwd(t, c, s_up, s_dn):
    return t * c + pltpu.roll(t, LANES - 16, 1) * s_up + pltpu.roll(t, 16, 1) * s_dn


def _rope_bwd(d, c, s_up, s_dn):
    return d * c + pltpu.roll(d * s_up, 16, 1) + pltpu.roll(d * s_dn, LANES - 16, 1)


def _rope_tables(seq):
    half = ROPE // 2
    inv = jnp.power(ROPE_THETA, -jnp.arange(half, dtype=F32) / half)
    ang = jnp.arange(seq, dtype=F32)[:, None] * inv[None, :]
    cos, sin = jnp.cos(ang), jnp.sin(ang)
    zeros = jnp.zeros((seq, half), F32)
    lo, hi = jnp.ones((seq, KR_LANE), F32), jnp.ones((seq, LANES - KR_LANE - ROPE), F32)
    c = jnp.concatenate([lo, cos, cos, hi], axis=1)
    c_rope_only = jnp.concatenate([0 * lo, cos, cos, 0 * hi], axis=1)
    s_up = jnp.concatenate([0 * lo, -sin, zeros, 0 * hi], axis=1)
    s_dn = jnp.concatenate([0 * lo, zeros, sin, 0 * hi], axis=1)
    return c, s_up, s_dn, c_rope_only


def _rms(x, g):
    r = lax.rsqrt(jnp.mean(x * x, axis=1, keepdims=True) + RMS_EPS)
    return x * r * g


def _rms_bwd(x, g, dy):
    r = lax.rsqrt(jnp.mean(x * x, axis=1, keepdims=True) + RMS_EPS)
    xh = x * r
    dxh = dy * g
    dx = r * (dxh - xh * jnp.mean(dxh * xh, axis=1, keepdims=True))
    return dx, _colsum(dy * xh)


def _ln_stats(x):
    mu = jnp.mean(x, axis=1, keepdims=True)
    xc = x - mu
    r = lax.rsqrt(jnp.mean(xc * xc, axis=1, keepdims=True) + LN_EPS)
    return xc * r, r


def _ln_bwd(xh, r, g, dy):
    dxh = dy * g
    dx = r * (dxh - jnp.mean(dxh, axis=1, keepdims=True) - xh * jnp.mean(dxh * xh, axis=1, keepdims=True))
    return dx, _colsum(dy * xh), _colsum(dy)


def _table_specs(tables, sub):
    whole = lambda a: pl.BlockSpec(a.shape, lambda b, g: (0,) * a.ndim)
    if len(tables) == 1:
        return [whole(tables[0])]
    return [whole(tables[0]), whole(tables[1]), pl.BlockSpec((sub, 1, LANES), lambda b, g: (g, 0, 0))]


def _biased(s, table_refs, delta, head):
    if delta < table_refs[0].shape[0]:
        s = s + table_refs[0][delta]
    if len(table_refs) == 3:
        s = s - table_refs[2][head, 0:1, 0:1] * table_refs[1][delta]
    return s


def _lane_masks(sub):
    lane = lax.broadcasted_iota(jnp.int32, (1, LANES), 1)
    return [(lane // (LANES // sub) == a).astype(F32) for a in range(sub)]


def _attn_fwd(q, qb0, k, kb0, v, vb0, tables, scale, *, name, batch, seq, sub=1, stride=1, wide_qk=False, after=None):
    t = ATT_T
    nq = seq // t
    rows = batch * seq
    n_tab = len(tables)
    qk_w = sub * LANES if wide_qk else LANES

    def body(q_ref, k_ref, v_ref, *rest):
        table_refs = rest[:n_tab]
        o_ref, lse_ref, vtb = rest[n_tab + (after is not None):][:3]
        per_head = rest[n_tab + (after is not None) + 3:]
        qbs, kbs = per_head[:sub], per_head[sub:]
        masks = _lane_masks(sub)
        for a in range(sub):
            lanes = slice(a * LANES, (a + 1) * LANES) if wide_qk else slice(None)
            qa = q_ref[:, lanes]
            qbs[a][...] = (qa.astype(F32) * masks[a]).astype(BF16) if sub > 1 and not wide_qk else qa.astype(BF16)
            if wide_qk or a == 0:
                kbs[a][...] = k_ref[:, lanes].astype(BF16)
        vtb[...] = v_ref[...].astype(F32).T.astype(BF16)
        for i in range(nq):
            out_t = None
            for a in range(sub):
                qt, kb = qbs[a][i * t:(i + 1) * t, :], kbs[a if wide_qk else 0]
                logits = [_biased(lax.dot_general(kb[j * t:(j + 1) * t, :], qt, NT, preferred_element_type=F32) * (scale * LOG2E), table_refs, i - j, a)
                          for j in range(i + 1)]
                m = jnp.max(functools.reduce(jnp.maximum, logits), axis=0, keepdims=True)
                ps = [jnp.exp2(s - m) for s in logits]
                l = jnp.sum(functools.reduce(jnp.add, ps), axis=0, keepdims=True)
                acc = functools.reduce(jnp.add, [lax.dot_general(vtb[:, j * t:(j + 1) * t], p.astype(BF16), NN, preferred_element_type=F32)
                                                 for j, p in enumerate(ps)])
                part = acc / l if sub == 1 else (acc / l) * masks[a].T
                out_t = part if out_t is None else out_t + part
                lse_ref[i * t:(i + 1) * t, a * LANES:(a + 1) * LANES] = jnp.broadcast_to((m + jnp.log2(l)) * LN2, (LANES, t)).T
            o_ref[i * t:(i + 1) * t, :] = out_t.T

    slab = lambda b0, step, width=LANES: pl.BlockSpec((seq, width), lambda b, g: (b, b0 + step * g))
    groups = N_HEADS // sub
    n_k = sub if wide_qk else 1
    return pl.pallas_call(
        body, name=name,
        grid=(batch, groups),
        in_specs=[slab(qb0, stride, qk_w), slab(kb0, stride, qk_w), slab(vb0, stride)] + _table_specs(tables, sub)
        + [pl.BlockSpec(memory_space=pl.ANY)] * (after is not None),
        out_specs=[slab(0, 1), slab(0, 1, sub * LANES)],
        out_shape=[jax.ShapeDtypeStruct((rows, groups * LANES), F32), jax.ShapeDtypeStruct((rows, N_HEADS * LANES), F32)],
        scratch_shapes=[pltpu.VMEM((LANES, seq), BF16)] + [pltpu.VMEM((seq, LANES), BF16)] * (sub + n_k),
        compiler_params=_params(("arbitrary", "arbitrary")),
    )(q, k, v, *tables, *([after] if after is not None else []))


def _attn_bwd(q, qb0, k, kb0, v, vb0, o, do, lse, tables, scale, *, name, batch, seq, out_dtype, sub=1, stride=1, wide_qk=False, after=None):
    t = ATT_T
    nq = seq // t
    rows = batch * seq
    n_tab = len(tables)
    groups = N_HEADS // sub
    packed = sub > 1 and not wide_qk
    n_out = 1 if packed else 3
    n_k = sub if wide_qk else 1
    qk_w = sub * LANES if wide_qk else LANES

    def body(q_ref, k_ref, v_ref, o_ref, do_ref, lse_ref, *rest):
        table_refs = rest[:n_tab]
        rest = rest[n_tab + (after is not None):]
        out_refs, (vb, dva), rest = rest[:n_out], rest[n_out:n_out + 2], rest[n_out + 2:]
        kbs, dkas, rest = rest[:n_k], rest[n_k:2 * n_k], rest[2 * n_k:]
        qbs, dobs, qtbs, dotbs = (rest[g * sub:(g + 1) * sub] for g in range(4))
        masks = _lane_masks(sub)
        vb[...] = v_ref[...].astype(BF16)
        for a in range(sub):
            lanes = slice(a * LANES, (a + 1) * LANES) if wide_qk else slice(None)
            qa = q_ref[:, lanes].astype(F32) * masks[a] if packed else q_ref[:, lanes].astype(F32)
            doa = do_ref[...] * masks[a] if sub > 1 else do_ref[...]
            qbs[a][...] = qa.astype(BF16)
            dobs[a][...] = doa.astype(BF16)
            qtbs[a][...] = qa.T.astype(BF16)
            dotbs[a][...] = doa.T.astype(BF16)
            if wide_qk or a == 0:
                kbs[a][...] = k_ref[:, lanes].astype(BF16)
        first_k, first_v = [[True] * nq for _ in range(n_k)], [True] * nq
        for i in range(nq):
            at = slice(i * t, (i + 1) * t)
            dq_all = None
            for a in range(sub):
                qt, dot, kb, dka = qbs[a][at, :], dobs[a][at, :], kbs[a if wide_qk else 0], dkas[a if wide_qk else 0]
                lse_t = lse_ref[at, a * LANES:a * LANES + 1] * LOG2E
                od = o_ref[at, :] * do_ref[at, :]
                delta = jnp.sum(od * masks[a] if sub > 1 else od, axis=1, keepdims=True)
                dq = None
                for j in range(i + 1):
                    kat = slice(j * t, (j + 1) * t)
                    kt, vt = kb[kat, :], vb[kat, :]
                    p = jnp.exp2(_biased(lax.dot_general(qt, kt, NT, preferred_element_type=F32) * (scale * LOG2E), table_refs, i - j, a) - lse_t)
                    dp = lax.dot_general(dot, vt, NT, preferred_element_type=F32)
                    ds = (p * (dp - delta) * scale).astype(BF16)
                    dk_part = lax.dot_general(qtbs[a][:, at], ds, NN, preferred_element_type=F32)
                    dv_part = lax.dot_general(dotbs[a][:, at], p.astype(BF16), NN, preferred_element_type=F32)
                    firsts = first_k[a if wide_qk else 0]
                    if firsts[j]:
                        dka[:, kat] = dk_part
                        firsts[j] = False
                    else:
                        dka[:, kat] += dk_part
                    if first_v[j]:
                        dva[:, kat] = dv_part
                        first_v[j] = False
                    else:
                        dva[:, kat] += dv_part
                    dq_part = lax.dot_general(ds, kt, NN, preferred_element_type=F32)
                    dq = dq_part if dq is None else dq + dq_part
                if wide_qk:
                    out_refs[0][at, a * LANES:(a + 1) * LANES] = dq.astype(out_refs[0].dtype)
                else:
                    dq = dq * masks[a] if sub > 1 else dq
                    dq_all = dq if dq_all is None else dq_all + dq
            if not wide_qk:
                out_refs[0][at, 0:LANES] = dq_all.astype(out_refs[0].dtype)
        if packed:
            out_refs[0][:, LANES:2 * LANES] = dkas[0][...].T.astype(out_refs[0].dtype)
            out_refs[0][:, 2 * LANES:3 * LANES] = dva[...].T.astype(out_refs[0].dtype)
            out_refs[0][:, 3 * LANES:] = jnp.zeros((seq, LANES), out_refs[0].dtype)
        else:
            for a in range(n_k):
                out_refs[1][:, a * LANES:(a + 1) * LANES] = dkas[a][...].T.astype(out_refs[1].dtype)
            out_refs[2][...] = dva[...].T.astype(out_refs[2].dtype)

    slab = lambda b0, step, width=LANES: pl.BlockSpec((seq, width), lambda b, g: (b, b0 + step * g))
    if packed:
        out_specs = [slab(0, 1, 4 * LANES)]
        out_shape = [jax.ShapeDtypeStruct((rows, groups * 4 * LANES), out_dtype)]
    else:
        out_specs = [slab(0, 1, qk_w), slab(0, 1, qk_w), slab(0, 1)]
        out_shape = [jax.ShapeDtypeStruct((rows, N_HEADS * LANES), out_dtype)] * 2 + [jax.ShapeDtypeStruct((rows, groups * LANES), out_dtype)]
    res = pl.pallas_call(
        body, name=name,
        grid=(batch, groups),
        in_specs=[slab(qb0, stride, qk_w), slab(kb0, stride, qk_w), slab(vb0, stride), slab(0, 1), slab(0, 1), slab(0, 1, sub * LANES)]
        + _table_specs(tables, sub) + [pl.BlockSpec(memory_space=pl.ANY)] * (after is not None),
        out_specs=out_specs, out_shape=out_shape,
        scratch_shapes=[pltpu.VMEM((seq, LANES), BF16), pltpu.VMEM((LANES, seq), F32)]
        + [pltpu.VMEM((seq, LANES), BF16)] * n_k + [pltpu.VMEM((LANES, seq), F32)] * n_k
        + [pltpu.VMEM((seq, LANES), BF16)] * (2 * sub) + [pltpu.VMEM((LANES, seq), BF16)] * (2 * sub),
        compiler_params=_params(("arbitrary", "arbitrary")),
    )(q, k, v, o, do, lse, *tables, *([after] if after is not None else []))
    return res[0] if packed else res


def _attention_tables(seq):
    n = seq // ATT_T
    pos = np.arange(ATT_T, dtype=np.int32)
    dist = np.arange(n, dtype=np.int32)[:, None, None] * ATT_T + pos[None, :, None] - pos[None, None, :]
    causal = np.where(dist[:1] >= 0, 0.0, NEG).astype(np.float32)
    count = np.zeros(dist.shape, np.float32)
    for window, dilation in DIL_PATTERNS:
        count += ((dist >= 0) & (dist <= window) & (dist % dilation == 0)).astype(np.float32)
    held = np.where(count > 0, np.log2(np.maximum(count, 1.0)), NEG).astype(np.float32)
    far = dist.astype(np.float32) * np.float32(LOG2E)
    slopes = np.asarray([2.0 ** (-8.0 * (i + 1) / N_HEADS) for i in range(N_HEADS)], np.float32)
    slopes = jnp.asarray(np.broadcast_to(slopes[:, None, None], (N_HEADS, 1, LANES)))
    flat = lambda a: jnp.asarray(np.ascontiguousarray(a))
    turned = lambda a: flat(np.swapaxes(a, 1, 2))
    return ((flat(causal),), (turned(causal),)), ((flat(held), flat(far), slopes), (turned(held), turned(far), slopes))


def _pad_heads(w, width):
    kdim, n = w.shape[0], w.shape[1] // width
    return jnp.pad(w.reshape(kdim, n, width), ((0, 0), (0, 0), (0, LANES - width))).reshape(kdim, n * LANES)


def _unpad_heads(w, width):
    kdim, n = w.shape[0], w.shape[1] // LANES
    return w.reshape(kdim, n, LANES)[:, :, :width].reshape(kdim, n * width)


def _pad_head_rows(w, width):
    n, kdim = w.shape[0] // width, w.shape[1]
    return jnp.pad(w.reshape(n, width, kdim), ((0, 0), (0, LANES - width), (0, 0))).reshape(n * LANES, kdim)


def _unpad_head_rows(w, width):
    n, kdim = w.shape[0] // LANES, w.shape[1]
    return w.reshape(n, LANES, kdim)[:, :width].reshape(n * width, kdim)


def _pad_w_in_t(wt):
    n_qkv, pair = 3 * N_HEADS * DIL_DIM, 2 * DIL_DIM
    zeros = lambda n: jnp.zeros((n, wt.shape[1]), wt.dtype)
    main = jnp.concatenate([wt[:LORA_W], zeros(KR_LANE), wt[LORA_W:LORA_W + ROPE], zeros(P_GATE - P_KR - KR_LANE - ROPE),
                            wt[LORA_W + ROPE + n_qkv:]], axis=0)
    qkv = wt[LORA_W + ROPE:LORA_W + ROPE + n_qkv].reshape(3, N_HEADS // 2, pair, wt.shape[1]).transpose(1, 0, 2, 3)
    dil = jnp.pad(qkv, ((0, 0), (0, 1), (0, 0), (0, 0))).reshape(P_DIL, wt.shape[1])
    return main, dil


def _unpad_w_in_t(gt):
    n_qkv, pair = 3 * N_HEADS * DIL_DIM, 2 * DIL_DIM
    moves = [(P_LORA, 0, LORA_W), (P_KR + KR_LANE, LORA_W, ROPE)]
    moves += [(P_HALF + (g * 4 + j) * pair, LORA_W + ROPE + (j * (N_HEADS // 2) + g) * pair, pair) for j in range(3) for g in range(N_HEADS // 2)]
    moves += [(P_GATE + r, LORA_W + ROPE + n_qkv + r, ROW_T) for r in range(0, P_HALF - P_GATE, ROW_T)]
    n = len(moves)

    def body(src, out, buf, sem_in, sem_out):
        ins = [pltpu.make_async_copy(src.at[pl.ds(a, k)], buf.at[pl.ds(b, k)], sem_in.at[i]) for i, (a, b, k) in enumerate(moves)]
        outs = [pltpu.make_async_copy(buf.at[pl.ds(b, k)], out.at[pl.ds(b, k)], sem_out.at[i]) for i, (a, b, k) in enumerate(moves)]
        for cp in ins:
            cp.start()
        for cp_in, cp_out in zip(ins, outs):
            cp_in.wait()
            cp_out.start()
        for cp in outs:
            cp.wait()

    return pl.pallas_call(
        body, name="unpad_d_w_in", in_specs=[pl.BlockSpec(memory_space=pl.ANY)], out_specs=pl.BlockSpec(memory_space=pl.ANY),
        out_shape=jax.ShapeDtypeStruct((IN_WIDTH, gt.shape[1]), gt.dtype),
        scratch_shapes=[pltpu.VMEM((IN_WIDTH, gt.shape[1]), gt.dtype), pltpu.SemaphoreType.DMA((n,)), pltpu.SemaphoreType.DMA((n,))],
        compiler_params=_params(),
    )(gt)


def _split_ukv(w):
    w3 = w.reshape(w.shape[0], N_HEADS, NOPE + V_DIM)
    return _pad_heads(w3[:, :, :NOPE].reshape(w.shape[0], -1), NOPE), w3[:, :, NOPE:].reshape(w.shape[0], -1)


def _merge_ukv(g_k, g_v):
    kdim = g_k.shape[0]
    k3 = _unpad_heads(g_k, NOPE).reshape(kdim, N_HEADS, NOPE)
    return jnp.concatenate([k3, g_v.reshape(kdim, N_HEADS, V_DIM)], axis=2).reshape(kdim, N_HEADS * (NOPE + V_DIM))


def _join_cols(w):
    return w.transpose(1, 0, 2).reshape(w.shape[1], N_CHIPS * w.shape[2])


def _split_cols(g):
    return g.reshape(g.shape[0], N_CHIPS, g.shape[1] // N_CHIPS).transpose(1, 0, 2)


def _local_step(x3, target3, wg, b_gate, g_q_a, g_kv_a, ln1_g, ln1_b, ln2_g, ln2_b, token=None, late_arrived=None, late_weights=None,
                early_grads=None, early_grads_go=None, first_grad=None, last_grads=None, tables=None):
    w_main_t, w_dil_t = _pad_w_in_t(wg["w_in"].reshape(IN_WIDTH, D_MODEL))
    w_uq_pt = _pad_head_rows(wg["w_uq"].reshape(N_HEADS * MLA_QK, Q_LORA), MLA_QK)
    w_ukv = _join_cols(wg["w_ukv"])
    batch, seq, _ = x3.shape
    rows = batch * seq
    x = x3.reshape(rows, D_MODEL)
    target = target3.reshape(rows, D_MODEL)
    row = functools.partial(_rowwise, rows=rows, seq=seq)
    mm = _matmul

    w_uk_p, w_uv = _split_ukv(w_ukv)
    b0, b1 = b_gate[0:1], b_gate[1:2]
    rope_c, rope_up, rope_dn, rope_c_only = _rope_tables(seq)
    (mla_bwd_tables, mla_fwd_tables), (dil_bwd_tables, dil_fwd_tables) = _attention_tables(seq) if tables is None else tables
    scale_mla, scale_dil = MLA_QK ** -0.5, DIL_DIM ** -0.5
    lora0, kr0, gate0 = P_LORA // LORA_W, P_KR // LANES, P_GATE // D_MODEL

    proj = mm(x, w_main_t, mode="nt", name="proj", tm=1024, tn=1536, tk=1024, after=token)
    proj_d = mm(x, w_dil_t, mode="nt", name="proj_dil", tm=1024, tn=1024, tk=1024, out_dtypes=(BF16,))

    def prep(lora, gq, gkv):
        return _rms(lora[:, :Q_LORA], gq), _rms(lora[:, Q_LORA:], gkv)

    qn, kvn = row(prep, name="mla_rms", ins=[(proj, LORA_W, lora0, "row"), (g_q_a, 0, 0, "full"), (g_kv_a, 0, 0, "full")],
                  outs=[(Q_LORA, BF16), (KV_LORA, BF16)])
    q_lin = mm(qn, w_uq_pt, mode="nt", name="q_up", tm=1024, tn=1024, tk=Q_LORA)
    k_lin = mm(kvn, w_uk_p, mode="nn", name="k_up", tm=1024, tn=1024, tk=KV_LORA)
    v_a = mm(kvn, w_uv, mode="nn", name="v_up", tm=1024, tn=1024, tk=KV_LORA, out_dtypes=(BF16,))

    def rope_qk(ql, kl, kr, c, up, dn):
        k_rot = _rope_fwd(kr, c, up, dn)
        qs = [_rope_fwd(ql[:, h * LANES:(h + 1) * LANES], c, up, dn) for h in range(N_HEADS)]
        ks = [kl[:, h * LANES:(h + 1) * LANES] + k_rot for h in range(N_HEADS)]
        return jnp.concatenate(qs, axis=1), jnp.concatenate(ks, axis=1)

    pos = lambda tab: (tab, LANES, 0, "pos")
    q_a, k_a = row(rope_qk, name="rope_qk",
                   ins=[(q_lin, D_MODEL, 0, "row"), (k_lin, D_MODEL, 0, "row"), (proj, LANES, kr0, "row"), pos(rope_c), pos(rope_up), pos(rope_dn)],
                   outs=[(N_HEADS * LANES, BF16), (N_HEADS * LANES, BF16)])
    o_a, lse_a = _attn_fwd(q_a, 0, k_a, 0, v_a, 0, mla_fwd_tables, scale_mla, name="mla_fwd", batch=batch, seq=seq, sub=2, wide_qk=True)
    arrived = None if late_arrived is None else late_arrived(o_a)
    o_b, lse_b = _attn_fwd(proj_d, 0, proj_d, 1, proj_d, 2, dil_fwd_tables, scale_dil, name="dil_fwd", batch=batch, seq=seq, sub=2, stride=4, after=arrived)
    late = wg if late_weights is None else late_weights(o_b)
    w_oa = _join_cols(late["w_o_mla"])
    w_ob = _join_cols(late["w_o_dil"])
    w_out, w_ff1, w_ff2 = late["w_out"].reshape(D_MODEL, D_MODEL), late["w_ff1"], late["w_ff2"].reshape(D_FF, D_MODEL)
    y_a = mm(o_a, w_oa, mode="nn", name="o_mla", tm=1024, tn=1024, tk=1024, out_dtypes=(BF16,))
    y_b = mm(o_b, w_ob, mode="nn", name="o_dil", tm=1024, tn=1024, tk=1024, out_dtypes=(BF16,))

    def gate(t0, t1, c0, c1, ya, yb):
        return (jax.nn.sigmoid(t0 + c0) * ya + jax.nn.sigmoid(t1 + c1) * yb,)

    gate_ins = [(proj, D_MODEL, gate0, "row"), (proj, D_MODEL, gate0 + 1, "row"), (b0, 0, 0, "full"), (b1, 0, 0, "full")]
    (u,) = row(gate, name="gate", ins=gate_ins + [(y_a, D_MODEL, 0, "row"), (y_b, D_MODEL, 0, "row")], outs=[(D_MODEL, BF16)])
    mixed = mm(u, w_out, mode="nn", name="mix", tm=1024, tn=1024, tk=1024)

    def ln1(xv, mv, g, b):
        r1 = ALPHA * xv + mv
        xh, _ = _ln_stats(r1)
        hv = xh * g + b
        return r1, hv, hv

    r1, h, h_b = row(ln1, name="ln1", ins=[(x, D_MODEL, 0, "row"), (mixed, D_MODEL, 0, "row"), (ln1_g, 0, 0, "full"), (ln1_b, 0, 0, "full")],
                outs=[(D_MODEL, F32), (D_MODEL, F32), (D_MODEL, BF16)])

    def relu2(acc):
        r = jnp.maximum(acc, 0.0)
        return (r * r,)

    z = mm(h_b, w_ff1, mode="nn", name="ff1", tm=1024, tn=1024, tk=1024, out_dtypes=(BF16,), epilogue=relu2, b_shards=True)
    f = mm(z, w_ff2, mode="nn", name="ff2", tm=1024, tn=1024, tk=2048)

    def ln2_loss(hv, fv, tv, g, b):
        xh, r = _ln_stats(ALPHA * hv + fv)
        err = xh * g + b - tv
        dy = err * (1.0 / D_MODEL)
        dr2, dg, db = _ln_bwd(xh, r, g, dy)
        loss = jnp.sum(_colsum(err * err), axis=1, keepdims=True) * (0.5 / D_MODEL)
        return dr2, dr2, jnp.broadcast_to(loss, (1, LANES)), dg, db

    dr2, dr2_b, loss_l, d_ln2_g, d_ln2_b = row(
        ln2_loss, name="ln2_loss",
        ins=[(h, D_MODEL, 0, "row"), (f, D_MODEL, 0, "row"), (target, D_MODEL, 0, "row"), (ln2_g, 0, 0, "full"), (ln2_b, 0, 0, "full")],
        outs=[(D_MODEL, F32), (D_MODEL, BF16)], sums=[LANES, D_MODEL, D_MODEL])

    d_w_ff2 = mm(z, dr2_b, mode="tn", name="d_w_ff2", tm=1024, tn=1024, tk=2048)
    da = mm(dr2_b, w_ff2, mode="nt", name="d_ff_act", tm=1024, tn=1024, tk=1024, out_dtypes=(BF16,), extras=(z,),
            epilogue=lambda acc, zv: (acc * (2.0 * jnp.sqrt(zv.astype(F32))),))
    d_w_ff1 = mm(h_b, da, mode="tn", name="d_w_ff1", tm=1024, tn=1024, tk=2048, out_shards=True)
    dh = mm(da, w_ff1, mode="nt", name="d_h", tm=1024, tn=1024, tk=1024, extras=(dr2,), epilogue=lambda acc, rv: (acc + ALPHA * rv,), b_shards=True)

    def ln1_bwd(dhv, r1v, g):
        xh, r = _ln_stats(r1v)
        return _ln_bwd(xh, r, g, dhv)

    dr1, d_ln1_g, d_ln1_b = row(ln1_bwd, name="ln1_bwd", ins=[(dh, D_MODEL, 0, "row"), (r1, D_MODEL, 0, "row"), (ln1_g, 0, 0, "full")],
                                outs=[(D_MODEL, F32)], sums=[D_MODEL, D_MODEL])
    d_w_out = mm(u, dr1, mode="tn", name="d_w_out", tm=1024, tn=1024, tk=1024)
    du = mm(dr1, w_out, mode="nt", name="d_u", tm=1024, tn=1024, tk=1024, out_dtypes=(BF16,))

    def gate_bwd(t0, t1, c0, c1, ya, yb, duv):
        s0, s1 = jax.nn.sigmoid(t0 + c0), jax.nn.sigmoid(t1 + c1)
        dt0 = duv * ya * s0 * (1.0 - s0)
        dt1 = duv * yb * s1 * (1.0 - s1)
        return duv * s0, duv * s1, jnp.concatenate([dt0, dt1], axis=1), jnp.concatenate([_colsum(dt0), _colsum(dt1)], axis=1)

    dy_a, dy_b, d_gates, d_b_gate = row(
        gate_bwd, name="gate_bwd", ins=gate_ins + [(y_a, D_MODEL, 0, "row"), (y_b, D_MODEL, 0, "row"), (du, D_MODEL, 0, "row")],
        outs=[(D_MODEL, BF16), (D_MODEL, BF16), (2 * D_MODEL, BF16)], sums=[2 * D_MODEL])
    d_w_oa = mm(o_a, dy_a, mode="tn", name="d_w_o_mla", tm=1024, tn=1024, tk=1024)
    d_w_ob = mm(o_b, dy_b, mode="tn", name="d_w_o_dil", tm=1024, tn=1024, tk=1024)
    grads = dict(w_o_mla=_split_cols(d_w_oa), w_o_dil=_split_cols(d_w_ob),
                 w_out=d_w_out.reshape(N_CHIPS, D_MODEL // N_CHIPS, D_MODEL), w_ff1=d_w_ff1, w_ff2=d_w_ff2.reshape(N_CHIPS, D_FF // N_CHIPS, D_MODEL))
    sent = None if early_grads is None else early_grads(grads)
    do_a = mm(dy_a, w_oa, mode="nt", name="d_o_mla", tm=1024, tn=1024, tk=1024, after=sent)
    do_b = mm(dy_b, w_ob, mode="nt", name="d_o_dil", tm=1024, tn=1024, tk=1024)
    dq_a, dk_a, dv_a = _attn_bwd(q_a, 0, k_a, 0, v_a, 0, o_a, do_a, lse_a, mla_bwd_tables, scale_mla,
                                 name="mla_bwd", batch=batch, seq=seq, out_dtype=F32, sub=2, wide_qk=True)
    going = None if early_grads_go is None else early_grads_go(dq_a)
    d_qkv_d = _attn_bwd(proj_d, 0, proj_d, 1, proj_d, 2, o_b, do_b, lse_b, dil_bwd_tables, scale_dil,
                        name="dil_bwd", batch=batch, seq=seq, out_dtype=BF16, sub=2, stride=4, after=going)

    def mla_post(dq, dk, c, up, dn, c_only):
        dqs = [_rope_bwd(dq[:, h * LANES:(h + 1) * LANES], c, up, dn) for h in range(N_HEADS)]
        dk_sum = dk[:, :LANES]
        for h in range(1, N_HEADS):
            dk_sum = dk_sum + dk[:, h * LANES:(h + 1) * LANES]
        return jnp.concatenate(dqs, axis=1), _rope_bwd(dk_sum, c_only, up, dn)

    dq_lin, d_kr = row(mla_post, name="mla_unrope",
                       ins=[(dq_a, D_MODEL, 0, "row"), (dk_a, D_MODEL, 0, "row"), pos(rope_c), pos(rope_up), pos(rope_dn), pos(rope_c_only)],
                       outs=[(N_HEADS * LANES, BF16), (LANES, BF16)])
    d_qn = mm(dq_lin, w_uq_pt, mode="nn", name="d_qn", tm=1024, tn=Q_LORA, tk=1024)
    d_kvn_k = mm(dk_a, w_uk_p, mode="nt", name="d_kvn_k", tm=1024, tn=KV_LORA, tk=1024)
    d_kvn = mm(dv_a, w_uv, mode="nt", name="d_kvn", tm=1024, tn=KV_LORA, tk=1024, extras=(d_kvn_k,), epilogue=lambda acc, e: (acc + e,))

    def rms_bwd(lora, dq, dkv, dkr, gq, gkv):
        dxq, dgq = _rms_bwd(lora[:, :Q_LORA], gq, dq)
        dxk, dgk = _rms_bwd(lora[:, Q_LORA:], gkv, dkv)
        tail = jnp.zeros((dxq.shape[0], P_GATE - P_KR - LANES), F32)
        return jnp.concatenate([dxq, dxk, dkr.astype(F32), tail], axis=1), dgq, dgk

    d_tail, d_g_q_a, d_g_kv_a = row(
        rms_bwd, name="mla_rms_bwd",
        ins=[(proj, LORA_W, lora0, "row"), (d_qn, Q_LORA, 0, "row"), (d_kvn, KV_LORA, 0, "row"), (d_kr, LANES, 0, "row"),
             (g_q_a, 0, 0, "full"), (g_kv_a, 0, 0, "full")],
        outs=[(P_GATE, BF16)], sums=[Q_LORA, KV_LORA])
    d_proj = [d_tail, d_gates, d_qkv_d]
    d_w_in_pt = mm(d_proj, x, mode="tn", name="d_w_in", tm=1024, tn=1024, tk=1024)
    d_w_in = _unpad_w_in_t(d_w_in_pt).reshape(N_CHIPS, IN_WIDTH // N_CHIPS, D_MODEL)
    moving = None if first_grad is None else first_grad(d_w_in)
    d_w_uq_pt = mm(dq_lin, qn, mode="tn", name="d_w_uq", tm=1024, tn=Q_LORA, tk=1024, after=moving)
    d_w_uk_p = mm(kvn, dk_a, mode="tn", name="d_w_uk", tm=KV_LORA, tn=1024, tk=1024, after=moving)
    d_w_uv = mm(kvn, dv_a, mode="tn", name="d_w_uv", tm=KV_LORA, tn=1024, tk=1024, after=moving)
    grads.update(w_in=d_w_in,
                 w_uq=_unpad_head_rows(d_w_uq_pt, MLA_QK).reshape(N_CHIPS, N_HEADS * MLA_QK // N_CHIPS, Q_LORA),
                 w_ukv=_split_cols(_merge_ukv(d_w_uk_p, d_w_uv)))
    leaving = None if last_grads is None else last_grads(grads)
    grad_x = mm(d_proj, jnp.concatenate([w_main_t, w_dil_t], axis=0), mode="nn", name="d_x", tm=1024, tn=1024, tk=1024, extras=(dr1,), epilogue=lambda acc, rv: (acc + ALPHA * rv,),
                after=leaving)

    grads.update(
        b_gate=d_b_gate.reshape(2, D_MODEL), g_q_a=d_g_q_a, g_kv_a=d_g_kv_a, ln1_g=d_ln1_g, ln1_b=d_ln1_b, ln2_g=d_ln2_g, ln2_b=d_ln2_b)
    return loss_l, grad_x.reshape(batch, seq, D_MODEL), grads


BIG = ("w_in", "w_uq", "w_ukv", "w_o_mla", "w_o_dil", "w_out", "w_ff1", "w_ff2")
SMALL = (("b_gate", 2 * D_MODEL), ("g_q_a", Q_LORA), ("g_kv_a", KV_LORA), ("ln1_g", D_MODEL), ("ln1_b", D_MODEL),
         ("ln2_g", D_MODEL), ("ln2_b", D_MODEL))
TRANSPOSED = ("w_in", "w_uq")
D2D_PIECES = (4, 2, 1)
ANY = pl.BlockSpec(memory_space=pl.ANY)
SIDE_EFFECTS = pltpu.CompilerParams(has_side_effects=True)


def _place():
    x, y, c = lax.axis_index("x"), lax.axis_index("y"), lax.axis_index("c")
    return x, y, c, ((1 - x, y), (x, 1 - y), (1 - x, 1 - y))


def _half_axis(shape):
    return 0 if shape[0] % 32 == 0 else 1


def _half_shape(shape):
    return (shape[0] // 2, shape[1]) if _half_axis(shape) == 0 else (shape[0], shape[1] // 2)


def _window(ref, lead, shape, which=None, pieces=False):
    axis = _half_axis(shape)
    size = shape[axis] if which is None else shape[axis] // 2
    base = 0 if which is None else which * size
    tile = (16, LANES)[axis]
    count = next(c for c in D2D_PIECES if size % (tile * c) == 0) if pieces else 1
    step = size // count
    spans = [pl.ds(pl.multiple_of(base + i * step, tile), step) for i in range(count)]
    refs = [ref.at[(*lead, s)] if axis == 0 else ref.at[(*lead, slice(None), s)] for s in spans]
    return refs if pieces else refs[0]


def _remote(src, dst, send, recv, to):
    return pltpu.make_async_remote_copy(src_ref=src, dst_ref=dst, send_sem=send, recv_sem=recv, device_id=to, device_id_type=MESH)


def _pair_split(grads, name):
    n = len(grads)

    def body(*refs):
        srcs, outs, (send, recv) = refs[:n], refs[n:2 * n], refs[2 * n:]
        x, y, c, _ = _place()
        for t in range(n):
            for s in range(N_CHIPS):
                _remote(_window(srcs[t], (s,), grads[t].shape[1:], 1 - c), outs[t].at[s], send.at[t], recv.at[t], (x, y, 1 - c)).start()
        for t in range(n):
            _remote(_window(srcs[t], (slice(None),), grads[t].shape[1:], 1 - c), outs[t], send.at[t], recv.at[t], (x, y, 1 - c)).wait()

    return pl.pallas_call(
        body, name=name, in_specs=[ANY] * n, out_specs=[ANY] * n,
        out_shape=[jax.ShapeDtypeStruct((N_CHIPS,) + _half_shape(g.shape[1:]), g.dtype) for g in grads],
        scratch_shapes=[pltpu.SemaphoreType.DMA((n,)), pltpu.SemaphoreType.DMA((n,))],
        compiler_params=SIDE_EFFECTS,
    )(*grads)


HBM = pl.BlockSpec(memory_space=pltpu.HBM)
SEM = pl.BlockSpec(memory_space=pltpu.SEMAPHORE)
SPLIT = pltpu.CompilerParams(has_side_effects=pltpu.SideEffectType.DATAFLOW_SIDE_EFFECTING)


def _in_hbm(a):
    return pltpu.with_memory_space_constraint(a, pltpu.HBM)


def _split_copies(kind, srcs, lands):
    x, y, c, chips = _place()
    out = []
    for t in range(len(srcs)):
        if kind == "pair":
            out += [(t, s % 3, _window(srcs[t], (s,), srcs[t].shape[1:], 1 - c), lands[t].at[s], (x, y, 1 - c)) for s in range(N_CHIPS)]
            continue
        if kind == "join":
            out += [(t, 0, a, b, (x, y, 1 - c)) for a, b in zip(_window(srcs[t], (), srcs[t].shape, None, True), _window(lands[t], (), srcs[t].shape, None, True))]
            continue
        if kind == "forward":
            shape, sibling = srcs[t].shape, (x, y, 1 - c)
            out += [(t, 0, a, b, sibling) for a, b in zip(_window(srcs[t], (), shape, None, True), _window(lands[t], (2 * x + y,), shape, None, True))]
            out += [(t, j, a, a, sibling) for j, (cx, cy) in enumerate(chips) for a in _window(lands[t], (2 * cx + cy,), shape, c, True)]
            continue
        for j, (cx, cy) in enumerate(chips):
            if kind == "gather":
                shape = srcs[t].shape
                out.append((t, j, _window(srcs[t], (), shape, c), _window(lands[t], (2 * x + y,), shape, c), (cx, cy, c)))
            else:
                out.append((t, j, srcs[t].at[2 * cx + cy], lands[t].at[j], (cx, cy, c)))
    return out


def _split_start(kind, srcs, land_shapes, name, lands=None, after=None):
    n = len(srcs)

    def body(*refs):
        src_refs, land_refs, sems, token = refs[:n], refs[n:2 * n], refs[-7 - 2 * n:-1 - 2 * n], refs[-1]
        for t, j, s, d, to in _split_copies(kind, src_refs, land_refs):
            _remote(s, d, sems[j], sems[3 + j], to).start()
        token[...] = jnp.zeros_like(token)

    lands = [_in_hbm(lax.empty(s.shape, s.dtype)) for s in land_shapes] if lands is None else list(lands)
    thru = [pltpu.HBM(a.shape, a.dtype) for a in list(srcs) + lands]
    res = pl.pallas_call(
        body, name=name,
        out_shape=(*[pltpu.SemaphoreType.DMA(())] * 6, *thru, jax.ShapeDtypeStruct((8, LANES), F32)),
        in_specs=[HBM] * (2 * n) + [ANY] * (after is not None), out_specs=(*[SEM] * 6, *[HBM] * (2 * n), pl.BlockSpec(memory_space=pltpu.VMEM)),
        input_output_aliases={i: 6 + i for i in range(2 * n)}, compiler_params=SPLIT,
    )(*[_in_hbm(s) for s in srcs], *lands, *([after] if after is not None else []))
    return res[:6], res[6:6 + n], res[6 + n:6 + 2 * n], res[-1]


def _split_wait(kind, sems, srcs, lands, after, name):
    n = len(srcs)

    def body(*refs):
        src_refs, land_refs, sem_refs = refs[:n], refs[n:2 * n], refs[2 * n:2 * n + 6]
        for t, j, s, d, to in _split_copies(kind, src_refs, land_refs):
            cp = _remote(s, d, sem_refs[j], sem_refs[3 + j], to)
            cp.wait_send()
            cp.wait_recv()

    res = pl.pallas_call(
        body, name=name, out_shape=[pltpu.HBM(a.shape, a.dtype) for a in list(srcs) + list(lands)],
        in_specs=[HBM] * (2 * n) + [SEM] * 6 + [ANY], out_specs=[HBM] * (2 * n),
        input_output_aliases={i: i for i in range(2 * n)}, compiler_params=SPLIT,
    )(*srcs, *lands, *sems, after)
    return res[:n], res[n:]


def _sum_all_devices(vec, name, after):
    n_rows = vec.shape[0]

    def body(v_ref, after_ref, out_ref, buf, send, recv):
        x, y, c, _ = _place()
        me = 4 * x + 2 * y + c
        buf[me] = v_ref[...]
        flips = [(a, b, d) for a in (0, 1) for b in (0, 1) for d in (0, 1)][1:]
        copies = []
        for r, (a, b, d) in enumerate(flips):
            px, py, pc = (1 - x if a else x), (1 - y if b else y), (1 - c if d else c)
            copies.append(pltpu.make_async_remote_copy(src_ref=v_ref, dst_ref=buf.at[me], send_sem=send.at[r], recv_sem=recv.at[r],
                                                       device_id=(px, py, pc), device_id_type=MESH))
            copies[-1].start()
        for r, (a, b, d) in enumerate(flips):
            px, py, pc = (1 - x if a else x), (1 - y if b else y), (1 - c if d else c)
            pltpu.make_async_remote_copy(src_ref=v_ref, dst_ref=buf.at[4 * px + 2 * py + pc], send_sem=send.at[r], recv_sem=recv.at[r],
                                         device_id=(px, py, pc), device_id_type=MESH).wait_recv()
        for cp in copies:
            cp.wait_send()
        total = buf[0]
        for k in range(1, N_DEV):
            total = total + buf[k]
        out_ref[...] = total

    vmem = pl.BlockSpec(memory_space=pltpu.VMEM)
    return pl.pallas_call(
        body, name=name, in_specs=[vmem, ANY], out_specs=vmem, out_shape=jax.ShapeDtypeStruct(vec.shape, F32),
        scratch_shapes=[pltpu.VMEM((N_DEV, n_rows, LANES), F32), pltpu.SemaphoreType.DMA((N_DEV - 1,)), pltpu.SemaphoreType.DMA((N_DEV - 1,))],
        compiler_params=pltpu.CompilerParams(has_side_effects=True),
    )(vec, after)


def _half_tile(half, width):
    t = half
    while t * width * 4 > (2 << 20) and t % 32 == 0:
        t //= 2
    return t


def _pair_add(g, theirs, core, name):
    _, half, width = theirs.shape
    t = _half_tile(half, width)
    n = half // t

    def body(c_ref, a_ref, b_ref, o_ref):
        o_ref[...] = (a_ref[...] + b_ref[...]).astype(BF16)

    tile = pl.BlockSpec((1, t, width), lambda j, i, c_ref: (j, i, 0))
    if _half_axis(g.shape[1:]) == 0:
        mine = pl.BlockSpec((1, t, width), lambda j, i, c_ref: (j, c_ref[0] * n + i, 0))
    else:
        mine = pl.BlockSpec((1, t, width), lambda j, i, c_ref: (j, i, c_ref[0]))
    return pl.pallas_call(
        body, name=name,
        grid_spec=pltpu.PrefetchScalarGridSpec(num_scalar_prefetch=1, grid=(N_CHIPS, n), in_specs=[mine, tile], out_specs=tile),
        out_shape=jax.ShapeDtypeStruct(theirs.shape, BF16), compiler_params=_params(("parallel", "parallel")),
    )(core, g, theirs)


def _chip_sum(part, others, chip, name, after=None):
    _, half, width = part.shape
    t = _half_tile(half, width)

    def body(s_ref, mine, p0, p1, p2, *rest):
        o_ref = rest[-1]
        o_ref[...] = ((mine[0].astype(F32) + p0[0].astype(F32)) + p1[0].astype(F32)) + p2[0].astype(F32)

    return pl.pallas_call(
        body, name=name,
        grid_spec=pltpu.PrefetchScalarGridSpec(
            num_scalar_prefetch=1, grid=(half // t,),
            in_specs=[pl.BlockSpec((1, t, width), lambda i, s_ref: (s_ref[0], i, 0))]
            + [pl.BlockSpec((1, t, width), lambda i, s_ref, j=j: (j, i, 0)) for j in range(3)] + [pl.BlockSpec(memory_space=pl.ANY)] * (after is not None),
            out_specs=pl.BlockSpec((t, width), lambda i, s_ref: (i, 0))),
        out_shape=jax.ShapeDtypeStruct((half, width), F32), compiler_params=_params(("parallel",)),
    )(chip, part, others, others, others, *([after] if after is not None else []))


EARLY = ("w_in", "w_uq", "w_ukv")
LATE = ("w_o_mla", "w_o_dil", "w_out", "w_ff1", "w_ff2")


def _sum_small(vals):
    n_in = len(vals)
    n_rows = sum(a.shape[0] * a.shape[1] // LANES for a in vals)
    pad_rows = -(-n_rows // 8) * 8

    def chunks(refs):
        return [(ref, a, j) for ref in refs for a in range(ref.shape[0]) for j in range(ref.shape[1] // LANES)]

    def body(*refs):
        ins, outs, (buf, send, recv) = refs[:n_in], refs[n_in:2 * n_in], refs[2 * n_in:]
        x, y, c, _ = _place()
        me = 4 * x + 2 * y + c
        for r, (ref, a, j) in enumerate(chunks(ins)):
            buf[me, r:r + 1, :] = ref[a:a + 1, j * LANES:(j + 1) * LANES]
        if pad_rows > n_rows:
            buf[me, n_rows:pad_rows, :] = jnp.zeros((pad_rows - n_rows, LANES), F32)
        flips = [(a, b, d) for a in (0, 1) for b in (0, 1) for d in (0, 1)][1:]
        peers = [((1 - x if a else x), (1 - y if b else y), (1 - c if d else c)) for a, b, d in flips]
        copies = [_remote(buf.at[me], buf.at[me], send.at[r], recv.at[r], peer) for r, peer in enumerate(peers)]
        for cp in copies:
            cp.start()
        for r, (px, py, pc) in enumerate(peers):
            _remote(buf.at[me], buf.at[4 * px + 2 * py + pc], send.at[r], recv.at[r], (px, py, pc)).wait_recv()
        for cp in copies:
            cp.wait_send()
        total = buf[0]
        for k in range(1, N_DEV):
            total = total + buf[k]
        for r, (ref, a, j) in enumerate(chunks(outs)):
            ref[a:a + 1, j * LANES:(j + 1) * LANES] = total[r:r + 1, :]

    vmem = pl.BlockSpec(memory_space=pltpu.VMEM)
    return pl.pallas_call(
        body, name="sum_small", in_specs=[vmem] * n_in, out_specs=[vmem] * n_in,
        out_shape=[jax.ShapeDtypeStruct(a.shape, F32) for a in vals],
        scratch_shapes=[pltpu.VMEM((N_DEV, pad_rows, LANES), F32), pltpu.SemaphoreType.DMA((N_DEV - 1,)), pltpu.SemaphoreType.DMA((N_DEV - 1,))],
        compiler_params=SIDE_EFFECTS,
    )(*vals)


def _adam_math(w, g, m, v):
    nm = B1 * m + (1.0 - B1) * g
    nv = B2 * v + (1.0 - B2) * (g * g)
    m_hat = nm / (1.0 - B1 ** ADAM_STEP)
    v_hat = nv / (1.0 - B2 ** ADAM_STEP)
    return -LR * (m_hat / (jnp.sqrt(v_hat) + ADAM_EPS) + WD * w), nm, nv


def _adamw_big(w, mine, theirs, m, v, core, name, side_by_side=False):
    rows, width = w.shape
    if side_by_side:
        t = next(c for c in (152, 96, 64, 32, 16, 8) if rows % c == 0)
        hb = None
        half_spec = pl.BlockSpec((t, width // 2), lambda i, c_ref: (i, 0))
    else:
        t = next(c for c in (256, 128, 64, 32, 16, 8) if (rows // 2) % c == 0)
        hb = rows // 2 // t
        half_spec = pl.BlockSpec((t, width), lambda i, c_ref: (i % hb, 0))

    def body(c_ref, w_ref, a_ref, b_ref, m_ref, v_ref, g_ref, d_ref, nm_ref, nv_ref):
        south = c_ref[0] == 0
        if side_by_side:
            g = jnp.where(south, jnp.concatenate([a_ref[...], b_ref[...]], axis=1), jnp.concatenate([b_ref[...], a_ref[...]], axis=1))
        else:
            g = jnp.where((pl.program_id(0) < hb) == south, a_ref[...], b_ref[...])
        g_ref[...] = g
        d_ref[...], nm_ref[...], nv_ref[...] = _adam_math(w_ref[...], g, m_ref[...], v_ref[...])

    spec = pl.BlockSpec((t, width), lambda i, c_ref: (i, 0))
    return pl.pallas_call(
        body, name=name,
        grid_spec=pltpu.PrefetchScalarGridSpec(num_scalar_prefetch=1, grid=(rows // t,),
                                               in_specs=[spec, half_spec, half_spec, spec, spec], out_specs=[spec] * 4),
        out_shape=[jax.ShapeDtypeStruct(w.shape, F32)] * 4, compiler_params=_params(("parallel",)),
    )(core, w, mine, theirs, m, v)


def _adamw_small(ws, gs, ms, vs):
    n = len(ws)

    def body(*refs):
        for t in range(n):
            w_ref, g_ref, m_ref, v_ref = (refs[k * n + t] for k in range(4))
            d, nm, nv = _adam_math(w_ref[...], g_ref[...], m_ref[...], v_ref[...])
            refs[4 * n + t][...] = d
            refs[5 * n + t][...] = nm
            refs[6 * n + t][...] = nv

    vmem = pl.BlockSpec(memory_space=pltpu.VMEM)
    res = pl.pallas_call(body, name="adamw_small", in_specs=[vmem] * (4 * n), out_specs=[vmem] * (3 * n),
                         out_shape=[jax.ShapeDtypeStruct(a.shape, F32) for a in ws] * 3)(*ws, *gs, *ms, *vs)
    return res[:n], res[n:2 * n], res[2 * n:]


def kernel(x, w_in, b_gate, g_q_a, w_uq, g_kv_a, w_ukv, w_o_mla, w_o_dil, w_out, ln1_g, ln1_b, w_ff1, w_ff2, ln2_g, ln2_b, loss_target, m_w_in, m_b_gate, m_g_q_a, m_w_uq, m_g_kv_a, m_w_ukv, m_w_o_mla, m_w_o_dil, m_w_out, m_ln1_g, m_ln1_b, m_w_ff1, m_w_ff2, m_ln2_g, m_ln2_b, v_w_in, v_b_gate, v_g_q_a, v_w_uq, v_g_kv_a, v_w_ukv, v_w_o_mla, v_w_o_dil, v_w_out, v_ln1_g, v_ln1_b, v_w_ff1, v_w_ff2, v_ln2_g, v_ln2_b):
    order = ("w_in", "b_gate", "g_q_a", "w_uq", "g_kv_a", "w_ukv", "w_o_mla", "w_o_dil", "w_out", "ln1_g", "ln1_b", "w_ff1", "w_ff2", "ln2_g", "ln2_b")
    w = dict(w_in=w_in, b_gate=b_gate, g_q_a=g_q_a, w_uq=w_uq, g_kv_a=g_kv_a, w_ukv=w_ukv, w_o_mla=w_o_mla, w_o_dil=w_o_dil, w_out=w_out,
             ln1_g=ln1_g, ln1_b=ln1_b, w_ff1=w_ff1, w_ff2=w_ff2, ln2_g=ln2_g, ln2_b=ln2_b)
    m = dict(w_in=m_w_in, b_gate=m_b_gate, g_q_a=m_g_q_a, w_uq=m_w_uq, g_kv_a=m_g_kv_a, w_ukv=m_w_ukv, w_o_mla=m_w_o_mla, w_o_dil=m_w_o_dil,
             w_out=m_w_out, ln1_g=m_ln1_g, ln1_b=m_ln1_b, w_ff1=m_w_ff1, w_ff2=m_w_ff2, ln2_g=m_ln2_g, ln2_b=m_ln2_b)
    v = dict(w_in=v_w_in, b_gate=v_b_gate, g_q_a=v_g_q_a, w_uq=v_w_uq, g_kv_a=v_g_kv_a, w_ukv=v_w_ukv, w_o_mla=v_w_o_mla, w_o_dil=v_w_o_dil,
             w_out=v_w_out, ln1_g=v_ln1_g, ln1_b=v_ln1_b, w_ff1=v_w_ff1, w_ff2=v_w_ff2, ln2_g=v_ln2_g, ln2_b=v_ln2_b)
    chip = 2 * lax.axis_index("x") + lax.axis_index("y")
    south = (lax.axis_index("c") == 0).astype(F32)
    gate_w = D_MODEL // N_CHIPS

    core = lax.axis_index("c").astype(jnp.int32).reshape(1)
    turn = lambda n, a: a.T if n in TRANSPOSED else a
    shards = {n: turn(n, w[n][0]).astype(BF16) for n in BIG}
    early_shards, late_shards = [shards[n] for n in EARLY], [shards[n] for n in LATE]
    gathered = lambda group: [jax.ShapeDtypeStruct((N_CHIPS,) + s.shape, BF16) for s in group]
    e_sems, e_srcs, e_lands, e_token = _split_start("gather", early_shards, gathered(early_shards), "gather_early_start")
    b_mine = lax.dynamic_update_slice(jnp.zeros((2, D_MODEL), F32), b_gate[0] * south, (0, chip * gate_w))
    b_full = _sum_all_devices(b_mine.reshape(-1, LANES), "gather_b_gate", e_token).reshape(2, D_MODEL)
    e_srcs, e_lands = _split_wait("gather", e_sems, e_srcs, e_lands, b_full, "gather_early_wait")
    e_forward = _split_start("forward", e_srcs, None, "gather_early_forward_start", lands=e_lands)
    first = dict(zip(EARLY, _split_wait("forward", *e_forward[:3], e_forward[-1], "gather_early_forward_wait")[1]))
    g_sems, g_srcs, g_lands, g_token = _split_start("gather", late_shards, gathered(late_shards), "gather_late_start", after=first[EARLY[-1]])
    tables = _attention_tables(x.shape[1])

    sent = {}

    def late_arrived(after):
        srcs, lands = _split_wait("gather", g_sems, g_srcs, g_lands, after, "gather_late_wait")
        sent["forward"] = _split_start("forward", srcs, None, "gather_late_forward_start", lands=lands)
        return sent["forward"][-1]

    def late_weights(after):
        return dict(zip(LATE, _split_wait("forward", *sent["forward"][:3], after, "gather_late_forward_wait")[1]))

    exchange_shapes = lambda parts: [jax.ShapeDtypeStruct((3,) + p.shape[1:], BF16) for p in parts]

    def early_grads(grads_late):
        gs = [grads_late[n] for n in LATE]
        shapes = [jax.ShapeDtypeStruct((N_CHIPS,) + _half_shape(g.shape[1:]), F32) for g in gs]
        sent["pair"] = _split_start("pair", gs, shapes, "pair_split_late_start")
        return sent["pair"][-1]

    def early_grads_go(after):
        gs, theirs = _split_wait("pair", *sent["pair"][:3], after, "pair_split_late_wait")
        parts = [_pair_add(g, th, core, "pair_add_" + n) for g, th, n in zip(gs, theirs, LATE)]
        sent["late"] = _split_start("scatter", parts, exchange_shapes(parts), "exchange_late_start")
        return sent["late"][-1]

    def first_grad(g):
        sent["first"] = _split_start("pair", [g], [jax.ShapeDtypeStruct((N_CHIPS,) + _half_shape(g.shape[1:]), F32)], "pair_split_w_in_start")
        return sent["first"][-1]

    def last_grads(grads_early):
        rest = [grads_early[n] for n in EARLY[1:]]
        gs, theirs = _split_wait("pair", *sent["first"][:3], rest[-1], "pair_split_w_in_wait")
        gs, theirs = list(gs) + rest, list(theirs) + list(_pair_split(rest, "pair_split_early"))
        parts = [_pair_add(g, th, core, "pair_add_" + n) for g, th, n in zip(gs, theirs, EARLY)]
        sent["early"] = _split_start("scatter", parts, exchange_shapes(parts), "exchange_early_start")
        return sent["early"][-1]

    loss_part, grad_x, grads = _local_step(x, loss_target, first, b_full, g_q_a, g_kv_a, ln1_g, ln1_b, ln2_g, ln2_b, token=g_token,
                                           late_arrived=late_arrived, late_weights=late_weights, early_grads=early_grads, early_grads_go=early_grads_go,
                                           first_grad=first_grad, last_grads=last_grads, tables=tables)

    g_out, delta, new_m, new_v = {}, {}, {}, {}
    chip1 = chip.astype(jnp.int32).reshape(1)

    def sum_and_send(names, parts, others, tag):
        totals = [_chip_sum(p, o, chip1, "chip_sum_" + n) for n, p, o in zip(names, parts, others)]
        return _split_start("join", totals, [jax.ShapeDtypeStruct(t.shape, F32) for t in totals], "pair_join_" + tag + "_start")

    def adam(names, joined, after, tag):
        totals, halves = _split_wait("join", *joined[:3], after, "pair_join_" + tag + "_wait")
        for n, mine, theirs in zip(names, totals, halves):
            res = _adamw_big(turn(n, w[n][0]), mine, theirs, turn(n, m[n][0]), turn(n, v[n][0]), core, "adamw_" + n,
                             side_by_side=mine.shape[0] == shards[n].shape[0])
            g_out[n], delta[n], new_m[n], new_v[n] = (turn(n, r) for r in res)

    late_joined = sum_and_send(LATE, *_split_wait("scatter", *sent["late"][:3], grad_x, "exchange_late_wait"), "late")
    early_joined = sum_and_send(EARLY, *_split_wait("scatter", *sent["early"][:3], late_joined[-1], "exchange_early_wait"), "early")
    small_names = [name for name, _ in SMALL]
    sums = _sum_small([grads[name] for name in small_names] + [loss_part])
    loss = sums[-1][0, 0]
    g_small = dict(zip(small_names, sums))
    g_small["b_gate"] = lax.dynamic_slice(g_small["b_gate"], (0, chip * gate_w), (2, gate_w))
    flat = lambda a: a.reshape(-1, a.shape[-1])
    res = _adamw_small(*[[flat(d[name]) for name in small_names] for d in (w, g_small, m, v)])
    g_out.update(g_small)
    for d, r in zip((delta, new_m, new_v), res):
        d.update(zip(small_names, r))
    adam(LATE, late_joined, res[0][0], "late")
    adam(EARLY, early_joined, delta[LATE[-1]], "early")

    lead = lambda d: [d[name].reshape(w[name].shape) for name in order]
    return (loss, grad_x, *lead(g_out), *lead(delta), *lead(new_m), *lead(new_v))
```

```python
import functools
import math

import jax
import jax.numpy as jnp
import numpy as np
from jax import lax
from jax.experimental import pallas as pl
from jax.experimental.pallas import tpu as pltpu

F32 = jnp.float32
BF16 = jnp.bfloat16
MESH = pl.DeviceIdType.MESH

D_MODEL = 1024
N_HEADS = 8
LANES = 128
NOPE, ROPE, V_DIM = 64, 32, 64
MLA_QK = NOPE + ROPE
Q_LORA, KV_LORA = 384, 256
DIL_DIM = 64
DIL_PATTERNS = ((128, 1), (512, 4), (2048, 16))
D_FF = 4096
N_CHIPS = 4
N_DEV = 8
IN_WIDTH = 4256
LN_EPS, RMS_EPS = 1e-5, 1e-6
NEG = -1e30
LOG2E, LN2 = 1.4426950408889634, 0.6931471805599453
ALPHA = 2.0 ** 0.25
ROPE_THETA = 10000.0
LR, B1, B2, ADAM_EPS, WD, ADAM_STEP = 0.001, 0.9, 0.999, 1e-8, 0.01, 10

P_LORA, P_KR, P_GATE, P_HALF = 0, 640, 1024, 3072
DIL_GROUP = 4 * LANES
P_DIL = N_HEADS // 2 * DIL_GROUP
LORA_W = Q_LORA + KV_LORA
KR_LANE = NOPE

ATT_T = 512
ROW_T = 512
ACC_COLS = 256
VMEM_LIMIT = 56 * 1024 * 1024

NN = (((1,), (0,)), ((), ()))
NT = (((1,), (1,)), ((), ()))
TN = (((0,), (0,)), ((), ()))


def _params(sem=None, **kw):
    return pltpu.CompilerParams(dimension_semantics=sem, vmem_limit_bytes=VMEM_LIMIT, **kw)


def _matmul(a, b, *, mode, name, tm, tn, tk, out_dtypes=(F32,), extras=(), epilogue=None, b_shards=False, out_shards=False, after=None):
    pieces = list(a) if isinstance(a, (list, tuple)) else [a]
    n_pc = len(pieces)
    a_shape = (pieces[0].shape[0], sum(p.shape[1] for p in pieces))
    if b_shards:
        n_sh, rows_b, cols_b = b.shape
        b_shape = (rows_b, n_sh * cols_b)
    else:
        b_shape = b.shape
    if mode == "nn":
        (m, k), (k2, n) = a_shape, b_shape
    elif mode == "nt":
        (m, k), (n, k2) = a_shape, b_shape
    else:
        (k, m), (k2, n) = a_shape, b_shape
    assert k == k2, (a_shape, b.shape, mode)
    tm, tn, tk = min(tm, m), min(tn, n), min(tk, k)
    assert m % tm == 0 and n % tn == 0 and k % tk == 0, (name, m, n, k, tm, tn, tk)
    nk = k // tk
    n_ex, n_out = len(extras), len(out_dtypes)
    n_in = n_pc + 1 + n_ex + (after is not None)
    dims = {"nn": NN, "nt": NT, "tn": TN}[mode]
    col_tile = tm if mode == "tn" else tk
    blocks = [p.shape[1] // col_tile for p in pieces]
    firsts = [sum(blocks[:p]) for p in range(n_pc)]
    assert all(p.shape[1] % col_tile == 0 for p in pieces), (name, col_tile)

    def body(*refs):
        a_refs, b_ref = refs[:n_pc], refs[n_pc]
        ex_refs = refs[n_pc + 1:n_pc + 1 + n_ex]
        out_refs = refs[n_in:n_in + n_out]

        def finish(acc):
            outs = epilogue(acc, *[r[...] for r in ex_refs]) if epilogue is not None else (acc,)
            for r, o in zip(out_refs, outs):
                r[...] = o.astype(r.dtype)

        kk = pl.program_id(2)

        def step(a_ref):
            if nk == 1:
                finish(lax.dot_general(a_ref[...].astype(BF16), b_ref[...].astype(BF16), dims, preferred_element_type=F32))
                return
            acc_ref = refs[-1]

            @pl.when(kk == 0)
            def _():
                acc_ref[...] = jnp.zeros_like(acc_ref)

            for c in range(0, tn, ACC_COLS):
                b_blk = b_ref[c:c + ACC_COLS, :] if mode == "nt" else b_ref[:, c:c + ACC_COLS]
                acc_ref[:, c:c + ACC_COLS] += lax.dot_general(a_ref[...].astype(BF16), b_blk.astype(BF16), dims, preferred_element_type=F32)

            @pl.when(kk == nk - 1)
            def _():
                finish(acc_ref[...])

        if n_pc == 1:
            step(a_refs[0])
        else:
            at = pl.program_id(0) if mode == "tn" else kk
            for p in range(n_pc):
                pl.when(jnp.logical_and(at >= firsts[p], at < firsts[p] + blocks[p]))(functools.partial(step, a_refs[p]))

    def a_spec_of(p):
        if n_pc == 1:
            return pl.BlockSpec((tk, tm), lambda i, j, kk: (kk, i)) if mode == "tn" else pl.BlockSpec((tm, tk), lambda i, j, kk: (i, kk))
        col = lambda at: jnp.clip(at - firsts[p], 0, blocks[p] - 1)
        mine = lambda at: jnp.logical_and(at >= firsts[p], at < firsts[p] + blocks[p])
        if mode == "tn":
            return pl.BlockSpec((tk, tm), lambda i, j, kk: (jnp.where(mine(i), kk, 0), col(i)))
        return pl.BlockSpec((tm, tk), lambda i, j, kk: (i, col(kk)))

    b_spec = {"nn": pl.BlockSpec((tk, tn), lambda i, j, kk: (kk, j)),
              "nt": pl.BlockSpec((tn, tk), lambda i, j, kk: (j, kk)),
              "tn": pl.BlockSpec((tk, tn), lambda i, j, kk: (kk, j))}[mode]
    tile = pl.BlockSpec((tm, tn), lambda i, j, kk: (i, j))
    out_spec, out_dims = tile, (m, n)
    if b_shards and mode == "nn":
        per = cols_b // tn
        b_spec = pl.BlockSpec((None, tk, tn), lambda i, j, kk: (j // per, kk, j % per))
    elif b_shards:
        assert mode == "nt"
        per = cols_b // tk
        b_spec = pl.BlockSpec((None, tn, tk), lambda i, j, kk: (kk // per, j, kk % per))
    if out_shards:
        assert not extras and epilogue is None
        per_out = n // N_CHIPS // tn
        out_spec = pl.BlockSpec((None, tm, tn), lambda i, j, kk: (j // per_out, i, j % per_out))
        out_dims = (N_CHIPS, m, n // N_CHIPS)
    outs = pl.pallas_call(
        body, name=name,
        grid=(m // tm, n // tn, nk),
        in_specs=[a_spec_of(p) for p in range(n_pc)] + [b_spec] + [tile] * n_ex + [pl.BlockSpec(memory_space=pl.ANY)] * (after is not None),
        out_specs=[out_spec] * n_out,
        out_shape=[jax.ShapeDtypeStruct(out_dims, dt) for dt in out_dtypes],
        scratch_shapes=[pltpu.VMEM((tm, tn), F32)] if nk > 1 else [],
        compiler_params=_params(("parallel", "parallel", "arbitrary")),
    )(*pieces, b, *extras, *([after] if after is not None else []))
    return outs[0] if n_out == 1 else outs


def _rowwise(fn, *, name, rows, seq, ins, outs, sums=()):
    tm = min(ROW_T, seq)
    n_pos = seq // tm
    n_in, n_out, n_sum = len(ins), len(outs), len(sums)

    def body(*refs):
        vals = fn(*[r[...] for r in refs[:n_in]])
        for r, v in zip(refs[n_in:n_in + n_out], vals[:n_out]):
            r[...] = v.astype(r.dtype)
        first = pl.program_id(0) == 0
        for r, v in zip(refs[n_in + n_out:], vals[n_out:]):
            @pl.when(first)
            def _(r=r, v=v):
                r[...] = v

            @pl.when(jnp.logical_not(first))
            def _(r=r, v=v):
                r[...] += v

    def spec(arr, width, col, kind):
        if kind == "row":
            return pl.BlockSpec((tm, width), lambda i, col=col: (i, col))
        if kind == "pos":
            return pl.BlockSpec((tm, width), lambda i, col=col: (i % n_pos, col))
        return pl.BlockSpec(arr.shape, lambda i: (0,) * arr.ndim)

    res = pl.pallas_call(
        body, name=name,
        grid=(rows // tm,),
        in_specs=[spec(*t) for t in ins],
        out_specs=[pl.BlockSpec((tm, w), lambda i: (i, 0)) for w, _ in outs]
        + [pl.BlockSpec((1, w), lambda i: (0, 0)) for w in sums],
        out_shape=[jax.ShapeDtypeStruct((rows, w), dt) for w, dt in outs]
        + [jax.ShapeDtypeStruct((1, w), F32) for w in sums],
        compiler_params=_params(("arbitrary",)),
    )(*[t[0] for t in ins])
    return res


def _colsum(v):
    return jnp.sum(v, axis=0, keepdims=True)


def _rope_fwd(t, c, s_up, s_dn):
    return t * c + pltpu.roll(t, LANES - 16, 1) * s_up + pltpu.roll(t, 16, 1) * s_dn


def _rope_bwd(d, c, s_up, s_dn):
    return d * c + pltpu.roll(d * s_up, 16, 1) + pltpu.roll(d * s_dn, LANES - 16, 1)


def _rope_tables(seq):
    half = ROPE // 2
    inv = jnp.power(ROPE_THETA, -jnp.arange(half, dtype=F32) / half)
    ang = jnp.arange(seq, dtype=F32)[:, None] * inv[None, :]
    cos, sin = jnp.cos(ang), jnp.sin(ang)
    zeros = jnp.zeros((seq, half), F32)
    lo, hi = jnp.ones((seq, KR_LANE), F32), jnp.ones((seq, LANES - KR_LANE - ROPE), F32)
    c = jnp.concatenate([lo, cos, cos, hi], axis=1)
    c_rope_only = jnp.concatenate([0 * lo, cos, cos, 0 * hi], axis=1)
    s_up = jnp.concatenate([0 * lo, -sin, zeros, 0 * hi], axis=1)
    s_dn = jnp.concatenate([0 * lo, zeros, sin, 0 * hi], axis=1)
    return c, s_up, s_dn, c_rope_only


def _rms(x, g):
    r = lax.rsqrt(jnp.mean(x * x, axis=1, keepdims=True) + RMS_EPS)
    return x * r * g


def _rms_bwd(x, g, dy):
    r = lax.rsqrt(jnp.mean(x * x, axis=1, keepdims=True) + RMS_EPS)
    xh = x * r
    dxh = dy * g
    dx = r * (dxh - xh * jnp.mean(dxh * xh, axis=1, keepdims=True))
    return dx, _colsum(dy * xh)


def _ln_stats(x):
    mu = jnp.mean(x, axis=1, keepdims=True)
    xc = x - mu
    r = lax.rsqrt(jnp.mean(xc * xc, axis=1, keepdims=True) + LN_EPS)
    return xc * r, r


def _ln_bwd(xh, r, g, dy):
    dxh = dy * g
    dx = r * (dxh - jnp.mean(dxh, axis=1, keepdims=True) - xh * jnp.mean(dxh * xh, axis=1, keepdims=True))
    return dx, _colsum(dy * xh), _colsum(dy)


def _table_specs(tables, sub):
    whole = lambda a: pl.BlockSpec(a.shape, lambda b, g: (0,) * a.ndim)
    if len(tables) == 1:
        return [whole(tables[0])]
    return [whole(tables[0]), whole(tables[1]), pl.BlockSpec((sub, 1, LANES), lambda b, g: (g, 0, 0))]


def _biased(s, table_refs, delta, head):
    if delta < table_refs[0].shape[0]:
        s = s + table_refs[0][delta]
    if len(table_refs) == 3:
        s = s - table_refs[2][head, 0:1, 0:1] * table_refs[1][delta]
    return s


def _lane_masks(sub):
    lane = lax.broadcasted_iota(jnp.int32, (1, LANES), 1)
    return [(lane // (LANES // sub) == a).astype(F32) for a in range(sub)]


def _attn_fwd(q, qb0, k, kb0, v, vb0, tables, scale, *, name, batch, seq, sub=1, stride=1, wide_qk=False, after=None):
    t = ATT_T
    nq = seq // t
    rows = batch * seq
    n_tab = len(tables)
    qk_w = sub * LANES if wide_qk else LANES

    def body(q_ref, k_ref, v_ref, *rest):
        table_refs = rest[:n_tab]
        o_ref, lse_ref, vtb = rest[n_tab + (after is not None):][:3]
        per_head = rest[n_tab + (after is not None) + 3:]
        qbs, kbs = per_head[:sub], per_head[sub:]
        masks = _lane_masks(sub)
        for a in range(sub):
            lanes = slice(a * LANES, (a + 1) * LANES) if wide_qk else slice(None)
            qa = q_ref[:, lanes]
            qbs[a][...] = (qa.astype(F32) * masks[a]).astype(BF16) if sub > 1 and not wide_qk else qa.astype(BF16)
            if wide_qk or a == 0:
                kbs[a][...] = k_ref[:, lanes].astype(BF16)
        vtb[...] = v_ref[...].astype(F32).T.astype(BF16)
        for i in range(nq):
            out_t = None
            for a in range(sub):
                qt, kb = qbs[a][i * t:(i + 1) * t, :], kbs[a if wide_qk else 0]
                logits = [_biased(lax.dot_general(kb[j * t:(j + 1) * t, :], qt, NT, preferred_element_type=F32) * (scale * LOG2E), table_refs, i - j, a)
                          for j in range(i + 1)]
                m = jnp.max(functools.reduce(jnp.maximum, logits), axis=0, keepdims=True)
                ps = [jnp.exp2(s - m) for s in logits]
                l = jnp.sum(functools.reduce(jnp.add, ps), axis=0, keepdims=True)
                acc = functools.reduce(jnp.add, [lax.dot_general(vtb[:, j * t:(j + 1) * t], p.astype(BF16), NN, preferred_element_type=F32)
                                                 for j, p in enumerate(ps)])
                part = acc / l if sub == 1 else (acc / l) * masks[a].T
                out_t = part if out_t is None else out_t + part
                lse_ref[i * t:(i + 1) * t, a * LANES:(a + 1) * LANES] = jnp.broadcast_to((m + jnp.log2(l)) * LN2, (LANES, t)).T
            o_ref[i * t:(i + 1) * t, :] = out_t.T

    slab = lambda b0, step, width=LANES: pl.BlockSpec((seq, width), lambda b, g: (b, b0 + step * g))
    groups = N_HEADS // sub
    n_k = sub if wide_qk else 1
    return pl.pallas_call(
        body, name=name,
        grid=(batch, groups),
        in_specs=[slab(qb0, stride, qk_w), slab(kb0, stride, qk_w), slab(vb0, stride)] + _table_specs(tables, sub)
        + [pl.BlockSpec(memory_space=pl.ANY)] * (after is not None),
        out_specs=[slab(0, 1), slab(0, 1, sub * LANES)],
        out_shape=[jax.ShapeDtypeStruct((rows, groups * LANES), F32), jax.ShapeDtypeStruct((rows, N_HEADS * LANES), F32)],
        scratch_shapes=[pltpu.VMEM((LANES, seq), BF16)] + [pltpu.VMEM((seq, LANES), BF16)] * (sub + n_k),
        compiler_params=_params(("arbitrary", "arbitrary")),
    )(q, k, v, *tables, *([after] if after is not None else []))


def _attn_bwd(q, qb0, k, kb0, v, vb0, o, do, lse, tables, scale, *, name, batch, seq, out_dtype, sub=1, stride=1, wide_qk=False, after=None):
    t = ATT_T
    nq = seq // t
    rows = batch * seq
    n_tab = len(tables)
    groups = N_HEADS // sub
    packed = sub > 1 and not wide_qk
    n_out = 1 if packed else 3
    n_k = sub if wide_qk else 1
    qk_w = sub * LANES if wide_qk else LANES

    def body(q_ref, k_ref, v_ref, o_ref, do_ref, lse_ref, *rest):
        table_refs = rest[:n_tab]
        rest = rest[n_tab + (after is not None):]
        out_refs, (vb, dva), rest = rest[:n_out], rest[n_out:n_out + 2], rest[n_out + 2:]
        kbs, dkas, rest = rest[:n_k], rest[n_k:2 * n_k], rest[2 * n_k:]
        qbs, dobs, qtbs, dotbs = (rest[g * sub:(g + 1) * sub] for g in range(4))
        masks = _lane_masks(sub)
        vb[...] = v_ref[...].astype(BF16)
        for a in range(sub):
            lanes = slice(a * LANES, (a + 1) * LANES) if wide_qk else slice(None)
            qa = q_ref[:, lanes].astype(F32) * masks[a] if packed else q_ref[:, lanes].astype(F32)
            doa = do_ref[...] * masks[a] if sub > 1 else do_ref[...]
            qbs[a][...] = qa.astype(BF16)
            dobs[a][...] = doa.astype(BF16)
            qtbs[a][...] = qa.T.astype(BF16)
            dotbs[a][...] = doa.T.astype(BF16)
            if wide_qk or a == 0:
                kbs[a][...] = k_ref[:, lanes].astype(BF16)
        first_k, first_v = [[True] * nq for _ in range(n_k)], [True] * nq
        for i in range(nq):
            at = slice(i * t, (i + 1) * t)
            dq_all = None
            for a in range(sub):
                qt, dot, kb, dka = qbs[a][at, :], dobs[a][at, :], kbs[a if wide_qk else 0], dkas[a if wide_qk else 0]
                lse_t = lse_ref[at, a * LANES:a * LANES + 1] * LOG2E
                od = o_ref[at, :] * do_ref[at, :]
                delta = jnp.sum(od * masks[a] if sub > 1 else od, axis=1, keepdims=True)
                dq = None
                for j in range(i + 1):
                    kat = slice(j * t, (j + 1) * t)
                    kt, vt = kb[kat, :], vb[kat, :]
                    p = jnp.exp2(_biased(lax.dot_general(qt, kt, NT, preferred_element_type=F32) * (scale * LOG2E), table_refs, i - j, a) - lse_t)
                    dp = lax.dot_general(dot, vt, NT, preferred_element_type=F32)
                    ds = (p * (dp - delta) * scale).astype(BF16)
                    dk_part = lax.dot_general(qtbs[a][:, at], ds, NN, preferred_element_type=F32)
                    dv_part = lax.dot_general(dotbs[a][:, at], p.astype(BF16), NN, preferred_element_type=F32)
                    firsts = first_k[a if wide_qk else 0]
                    if firsts[j]:
                        dka[:, kat] = dk_part
                        firsts[j] = False
                    else:
                        dka[:, kat] += dk_part
                    if first_v[j]:
                        dva[:, kat] = dv_part
                        first_v[j] = False
                    else:
                        dva[:, kat] += dv_part
                    dq_part = lax.dot_general(ds, kt, NN, preferred_element_type=F32)
                    dq = dq_part if dq is None else dq + dq_part
                if wide_qk:
                    out_refs[0][at, a * LANES:(a + 1) * LANES] = dq.astype(out_refs[0].dtype)
                else:
                    dq = dq * masks[a] if sub > 1 else dq
                    dq_all = dq if dq_all is None else dq_all + dq
            if not wide_qk:
                out_refs[0][at, 0:LANES] = dq_all.astype(out_refs[0].dtype)
        if packed:
            out_refs[0][:, LANES:2 * LANES] = dkas[0][...].T.astype(out_refs[0].dtype)
            out_refs[0][:, 2 * LANES:3 * LANES] = dva[...].T.astype(out_refs[0].dtype)
            out_refs[0][:, 3 * LANES:] = jnp.zeros((seq, LANES), out_refs[0].dtype)
        else:
            for a in range(n_k):
                out_refs[1][:, a * LANES:(a + 1) * LANES] = dkas[a][...].T.astype(out_refs[1].dtype)
            out_refs[2][...] = dva[...].T.astype(out_refs[2].dtype)

    slab = lambda b0, step, width=LANES: pl.BlockSpec((seq, width), lambda b, g: (b, b0 + step * g))
    if packed:
        out_specs = [slab(0, 1, 4 * LANES)]
        out_shape = [jax.ShapeDtypeStruct((rows, groups * 4 * LANES), out_dtype)]
    else:
        out_specs = [slab(0, 1, qk_w), slab(0, 1, qk_w), slab(0, 1)]
        out_shape = [jax.ShapeDtypeStruct((rows, N_HEADS * LANES), out_dtype)] * 2 + [jax.ShapeDtypeStruct((rows, groups * LANES), out_dtype)]
    res = pl.pallas_call(
        body, name=name,
        grid=(batch, groups),
        in_specs=[slab(qb0, stride, qk_w), slab(kb0, stride, qk_w), slab(vb0, stride), slab(0, 1), slab(0, 1), slab(0, 1, sub * LANES)]
        + _table_specs(tables, sub) + [pl.BlockSpec(memory_space=pl.ANY)] * (after is not None),
        out_specs=out_specs, out_shape=out_shape,
        scratch_shapes=[pltpu.VMEM((seq, LANES), BF16), pltpu.VMEM((LANES, seq), F32)]
        + [pltpu.VMEM((seq, LANES), BF16)] * n_k + [pltpu.VMEM((LANES, seq), F32)] * n_k
        + [pltpu.VMEM((seq, LANES), BF16)] * (2 * sub) + [pltpu.VMEM((LANES, seq), BF16)] * (2 * sub),
        compiler_params=_params(("arbitrary", "arbitrary")),
    )(q, k, v, o, do, lse, *tables, *([after] if after is not None else []))
    return res[0] if packed else res


def _attention_tables(seq):
    n = seq // ATT_T
    pos = np.arange(ATT_T, dtype=np.int32)
    dist = np.arange(n, dtype=np.int32)[:, None, None] * ATT_T + pos[None, :, None] - pos[None, None, :]
    causal = np.where(dist[:1] >= 0, 0.0, NEG).astype(np.float32)
    count = np.zeros(dist.shape, np.float32)
    for window, dilation in DIL_PATTERNS:
        count += ((dist >= 0) & (dist <= window) & (dist % dilation == 0)).astype(np.float32)
    held = np.where(count > 0, np.log2(np.maximum(count, 1.0)), NEG).astype(np.float32)
    far = dist.astype(np.float32) * np.float32(LOG2E)
    slopes = np.asarray([2.0 ** (-8.0 * (i + 1) / N_HEADS) for i in range(N_HEADS)], np.float32)
    slopes = jnp.asarray(np.broadcast_to(slopes[:, None, None], (N_HEADS, 1, LANES)))
    flat = lambda a: jnp.asarray(np.ascontiguousarray(a))
    turned = lambda a: flat(np.swapaxes(a, 1, 2))
    return ((flat(causal),), (turned(causal),)), ((flat(held), flat(far), slopes), (turned(held), turned(far), slopes))


def _pad_heads(w, width):
    kdim, n = w.shape[0], w.shape[1] // width
    return jnp.pad(w.reshape(kdim, n, width), ((0, 0), (0, 0), (0, LANES - width))).reshape(kdim, n * LANES)


def _unpad_heads(w, width):
    kdim, n = w.shape[0], w.shape[1] // LANES
    return w.reshape(kdim, n, LANES)[:, :, :width].reshape(kdim, n * width)


def _pad_head_rows(w, width):
    n, kdim = w.shape[0] // width, w.shape[1]
    return jnp.pad(w.reshape(n, width, kdim), ((0, 0), (0, LANES - width), (0, 0))).reshape(n * LANES, kdim)


def _unpad_head_rows(w, width):
    n, kdim = w.shape[0] // LANES, w.shape[1]
    return w.reshape(n, LANES, kdim)[:, :width].reshape(n * width, kdim)


def _pad_w_in_t(wt):
    n_qkv, pair = 3 * N_HEADS * DIL_DIM, 2 * DIL_DIM
    zeros = lambda n: jnp.zeros((n, wt.shape[1]), wt.dtype)
    main = jnp.concatenate([wt[:LORA_W], zeros(KR_LANE), wt[LORA_W:LORA_W + ROPE], zeros(P_GATE - P_KR - KR_LANE - ROPE),
                            wt[LORA_W + ROPE + n_qkv:]], axis=0)
    qkv = wt[LORA_W + ROPE:LORA_W + ROPE + n_qkv].reshape(3, N_HEADS // 2, pair, wt.shape[1]).transpose(1, 0, 2, 3)
    dil = jnp.pad(qkv, ((0, 0), (0, 1), (0, 0), (0, 0))).reshape(P_DIL, wt.shape[1])
    return main, dil


def _unpad_w_in_t(gt):
    n_qkv, pair = 3 * N_HEADS * DIL_DIM, 2 * DIL_DIM
    moves = [(P_LORA, 0, LORA_W), (P_KR + KR_LANE, LORA_W, ROPE)]
    moves += [(P_HALF + (g * 4 + j) * pair, LORA_W + ROPE + (j * (N_HEADS // 2) + g) * pair, pair) for j in range(3) for g in range(N_HEADS // 2)]
    moves += [(P_GATE + r, LORA_W + ROPE + n_qkv + r, ROW_T) for r in range(0, P_HALF - P_GATE, ROW_T)]
    n = len(moves)

    def body(src, out, buf, sem_in, sem_out):
        ins = [pltpu.make_async_copy(src.at[pl.ds(a, k)], buf.at[pl.ds(b, k)], sem_in.at[i]) for i, (a, b, k) in enumerate(moves)]
        outs = [pltpu.make_async_copy(buf.at[pl.ds(b, k)], out.at[pl.ds(b, k)], sem_out.at[i]) for i, (a, b, k) in enumerate(moves)]
        for cp in ins:
            cp.start()
        for cp_in, cp_out in zip(ins, outs):
            cp_in.wait()
            cp_out.start()
        for cp in outs:
            cp.wait()

    return pl.pallas_call(
        body, name="unpad_d_w_in", in_specs=[pl.BlockSpec(memory_space=pl.ANY)], out_specs=pl.BlockSpec(memory_space=pl.ANY),
        out_shape=jax.ShapeDtypeStruct((IN_WIDTH, gt.shape[1]), gt.dtype),
        scratch_shapes=[pltpu.VMEM((IN_WIDTH, gt.shape[1]), gt.dtype), pltpu.SemaphoreType.DMA((n,)), pltpu.SemaphoreType.DMA((n,))],
        compiler_params=_params(),
    )(gt)


def _split_ukv(w):
    w3 = w.reshape(w.shape[0], N_HEADS, NOPE + V_DIM)
    return _pad_heads(w3[:, :, :NOPE].reshape(w.shape[0], -1), NOPE), w3[:, :, NOPE:].reshape(w.shape[0], -1)


def _merge_ukv(g_k, g_v):
    kdim = g_k.shape[0]
    k3 = _unpad_heads(g_k, NOPE).reshape(kdim, N_HEADS, NOPE)
    return jnp.concatenate([k3, g_v.reshape(kdim, N_HEADS, V_DIM)], axis=2).reshape(kdim, N_HEADS * (NOPE + V_DIM))


def _join_cols(w):
    return w.transpose(1, 0, 2).reshape(w.shape[1], N_CHIPS * w.shape[2])


def _split_cols(g):
    return g.reshape(g.shape[0], N_CHIPS, g.shape[1] // N_CHIPS).transpose(1, 0, 2)


def _local_step(x3, target3, wg, b_gate, g_q_a, g_kv_a, ln1_g, ln1_b, ln2_g, ln2_b, token=None, late_arrived=None, late_weights=None,
                early_grads=None, early_grads_go=None, first_grad=None, last_grads=None, tables=None):
    w_main_t, w_dil_t = _pad_w_in_t(wg["w_in"].reshape(IN_WIDTH, D_MODEL))
    w_uq_pt = _pad_head_rows(wg["w_uq"].reshape(N_HEADS * MLA_QK, Q_LORA), MLA_QK)
    w_ukv = _join_cols(wg["w_ukv"])
    batch, seq, _ = x3.shape
    rows = batch * seq
    x = x3.reshape(rows, D_MODEL)
    target = target3.reshape(rows, D_MODEL)
    row = functools.partial(_rowwise, rows=rows, seq=seq)
    mm = _matmul

    w_uk_p, w_uv = _split_ukv(w_ukv)
    b0, b1 = b_gate[0:1], b_gate[1:2]
    rope_c, rope_up, rope_dn, rope_c_only = _rope_tables(seq)
    (mla_bwd_tables, mla_fwd_tables), (dil_bwd_tables, dil_fwd_tables) = _attention_tables(seq) if tables is None else tables
    scale_mla, scale_dil = MLA_QK ** -0.5, DIL_DIM ** -0.5
    lora0, kr0, gate0 = P_LORA // LORA_W, P_KR // LANES, P_GATE // D_MODEL

    proj = mm(x, w_main_t, mode="nt", name="proj", tm=1024, tn=1536, tk=1024, after=token)
    proj_d = mm(x, w_dil_t, mode="nt", name="proj_dil", tm=1024, tn=1024, tk=1024, out_dtypes=(BF16,))

    def prep(lora, gq, gkv):
        return _rms(lora[:, :Q_LORA], gq), _rms(lora[:, Q_LORA:], gkv)

    qn, kvn = row(prep, name="mla_rms", ins=[(proj, LORA_W, lora0, "row"), (g_q_a, 0, 0, "full"), (g_kv_a, 0, 0, "full")],
                  outs=[(Q_LORA, BF16), (KV_LORA, BF16)])
    q_lin = mm(qn, w_uq_pt, mode="nt", name="q_up", tm=1024, tn=1024, tk=Q_LORA)
    k_lin = mm(kvn, w_uk_p, mode="nn", name="k_up", tm=1024, tn=1024, tk=KV_LORA)
    v_a = mm(kvn, w_uv, mode="nn", name="v_up", tm=1024, tn=1024, tk=KV_LORA, out_dtypes=(BF16,))

    def rope_qk(ql, kl, kr, c, up, dn):
        k_rot = _rope_fwd(kr, c, up, dn)
        qs = [_rope_fwd(ql[:, h * LANES:(h + 1) * LANES], c, up, dn) for h in range(N_HEADS)]
        ks = [kl[:, h * LANES:(h + 1) * LANES] + k_rot for h in range(N_HEADS)]
        return jnp.concatenate(qs, axis=1), jnp.concatenate(ks, axis=1)

    pos = lambda tab: (tab, LANES, 0, "pos")
    q_a, k_a = row(rope_qk, name="rope_qk",
                   ins=[(q_lin, D_MODEL, 0, "row"), (k_lin, D_MODEL, 0, "row"), (proj, LANES, kr0, "row"), pos(rope_c), pos(rope_up), pos(rope_dn)],
                   outs=[(N_HEADS * LANES, BF16), (N_HEADS * LANES, BF16)])
    o_a, lse_a = _attn_fwd(q_a, 0, k_a, 0, v_a, 0, mla_fwd_tables, scale_mla, name="mla_fwd", batch=batch, seq=seq, sub=2, wide_qk=True)
    arrived = None if late_arrived is None else late_arrived(o_a)
    o_b, lse_b = _attn_fwd(proj_d, 0, proj_d, 1, proj_d, 2, dil_fwd_tables, scale_dil, name="dil_fwd", batch=batch, seq=seq, sub=2, stride=4, after=arrived)
    late = wg if late_weights is None else late_weights(o_b)
    w_oa = _join_cols(late["w_o_mla"])
    w_ob = _join_cols(late["w_o_dil"])
    w_out, w_ff1, w_ff2 = late["w_out"].reshape(D_MODEL, D_MODEL), late["w_ff1"], late["w_ff2"].reshape(D_FF, D_MODEL)
    y_a = mm(o_a, w_oa, mode="nn", name="o_mla", tm=1024, tn=1024, tk=1024, out_dtypes=(BF16,))
    y_b = mm(o_b, w_ob, mode="nn", name="o_dil", tm=1024, tn=1024, tk=1024, out_dtypes=(BF16,))

    def gate(t0, t1, c0, c1, ya, yb):
        return (jax.nn.sigmoid(t0 + c0) * ya + jax.nn.sigmoid(t1 + c1) * yb,)

    gate_ins = [(proj, D_MODEL, gate0, "row"), (proj, D_MODEL, gate0 + 1, "row"), (b0, 0, 0, "full"), (b1, 0, 0, "full")]
    (u,) = row(gate, name="gate", ins=gate_ins + [(y_a, D_MODEL, 0, "row"), (y_b, D_MODEL, 0, "row")], outs=[(D_MODEL, BF16)])
    mixed = mm(u, w_out, mode="nn", name="mix", tm=1024, tn=1024, tk=1024)

    def ln1(xv, mv, g, b):
        r1 = ALPHA * xv + mv
        xh, _ = _ln_stats(r1)
        hv = xh * g + b
        return r1, hv, hv

    r1, h, h_b = row(ln1, name="ln1", ins=[(x, D_MODEL, 0, "row"), (mixed, D_MODEL, 0, "row"), (ln1_g, 0, 0, "full"), (ln1_b, 0, 0, "full")],
                outs=[(D_MODEL, F32), (D_MODEL, F32), (D_MODEL, BF16)])

    def relu2(acc):
        r = jnp.maximum(acc, 0.0)
        return (r * r,)

    z = mm(h_b, w_ff1, mode="nn", name="ff1", tm=1024, tn=1024, tk=1024, out_dtypes=(BF16,), epilogue=relu2, b_shards=True)
    f = mm(z, w_ff2, mode="nn", name="ff2", tm=1024, tn=1024, tk=2048)

    def ln2_loss(hv, fv, tv, g, b):
        xh, r = _ln_stats(ALPHA * hv + fv)
        err = xh * g + b - tv
        dy = err * (1.0 / D_MODEL)
        dr2, dg, db = _ln_bwd(xh, r, g, dy)
        loss = jnp.sum(_colsum(err * err), axis=1, keepdims=True) * (0.5 / D_MODEL)
        return dr2, dr2, jnp.broadcast_to(loss, (1, LANES)), dg, db

    dr2, dr2_b, loss_l, d_ln2_g, d_ln2_b = row(
        ln2_loss, name="ln2_loss",
        ins=[(h, D_MODEL, 0, "row"), (f, D_MODEL, 0, "row"), (target, D_MODEL, 0, "row"), (ln2_g, 0, 0, "full"), (ln2_b, 0, 0, "full")],
        outs=[(D_MODEL, F32), (D_MODEL, BF16)], sums=[LANES, D_MODEL, D_MODEL])

    d_w_ff2 = mm(z, dr2_b, mode="tn", name="d_w_ff2", tm=1024, tn=1024, tk=2048)
    da = mm(dr2_b, w_ff2, mode="nt", name="d_ff_act", tm=1024, tn=1024, tk=1024, out_dtypes=(BF16,), extras=(z,),
            epilogue=lambda acc, zv: (acc * (2.0 * jnp.sqrt(zv.astype(F32))),))
    d_w_ff1 = mm(h_b, da, mode="tn", name="d_w_ff1", tm=1024, tn=1024, tk=2048, out_shards=True)
    dh = mm(da, w_ff1, mode="nt", name="d_h", tm=1024, tn=1024, tk=1024, extras=(dr2,), epilogue=lambda acc, rv: (acc + ALPHA * rv,), b_shards=True)

    def ln1_bwd(dhv, r1v, g):
        xh, r = _ln_stats(r1v)
        return _ln_bwd(xh, r, g, dhv)

    dr1, d_ln1_g, d_ln1_b = row(ln1_bwd, name="ln1_bwd", ins=[(dh, D_MODEL, 0, "row"), (r1, D_MODEL, 0, "row"), (ln1_g, 0, 0, "full")],
                                outs=[(D_MODEL, F32)], sums=[D_MODEL, D_MODEL])
    d_w_out = mm(u, dr1, mode="tn", name="d_w_out", tm=1024, tn=1024, tk=1024)
    du = mm(dr1, w_out, mode="nt", name="d_u", tm=1024, tn=1024, tk=1024, out_dtypes=(BF16,))

    def gate_bwd(t0, t1, c0, c1, ya, yb, duv):
        s0, s1 = jax.nn.sigmoid(t0 + c0), jax.nn.sigmoid(t1 + c1)
        dt0 = duv * ya * s0 * (1.0 - s0)
        dt1 = duv * yb * s1 * (1.0 - s1)
        return duv * s0, duv * s1, jnp.concatenate([dt0, dt1], axis=1), jnp.concatenate([_colsum(dt0), _colsum(dt1)], axis=1)

    dy_a, dy_b, d_gates, d_b_gate = row(
        gate_bwd, name="gate_bwd", ins=gate_ins + [(y_a, D_MODEL, 0, "row"), (y_b, D_MODEL, 0, "row"), (du, D_MODEL, 0, "row")],
        outs=[(D_MODEL, BF16), (D_MODEL, BF16), (2 * D_MODEL, BF16)], sums=[2 * D_MODEL])
    d_w_oa = mm(o_a, dy_a, mode="tn", name="d_w_o_mla", tm=1024, tn=1024, tk=1024)
    d_w_ob = mm(o_b, dy_b, mode="tn", name="d_w_o_dil", tm=1024, tn=1024, tk=1024)
    grads = dict(w_o_mla=_split_cols(d_w_oa), w_o_dil=_split_cols(d_w_ob),
                 w_out=d_w_out.reshape(N_CHIPS, D_MODEL // N_CHIPS, D_MODEL), w_ff1=d_w_ff1, w_ff2=d_w_ff2.reshape(N_CHIPS, D_FF // N_CHIPS, D_MODEL))
    sent = None if early_grads is None else early_grads(grads)
    do_a = mm(dy_a, w_oa, mode="nt", name="d_o_mla", tm=1024, tn=1024, tk=1024, after=sent)
    do_b = mm(dy_b, w_ob, mode="nt", name="d_o_dil", tm=1024, tn=1024, tk=1024)
    dq_a, dk_a, dv_a = _attn_bwd(q_a, 0, k_a, 0, v_a, 0, o_a, do_a, lse_a, mla_bwd_tables, scale_mla,
                                 name="mla_bwd", batch=batch, seq=seq, out_dtype=F32, sub=2, wide_qk=True)
    going = None if early_grads_go is None else early_grads_go(dq_a)
    d_qkv_d = _attn_bwd(proj_d, 0, proj_d, 1, proj_d, 2, o_b, do_b, lse_b, dil_bwd_tables, scale_dil,
                        name="dil_bwd", batch=batch, seq=seq, out_dtype=BF16, sub=2, stride=4, after=going)

    def mla_post(dq, dk, c, up, dn, c_only):
        dqs = [_rope_bwd(dq[:, h * LANES:(h + 1) * LANES], c, up, dn) for h in range(N_HEADS)]
        dk_sum = dk[:, :LANES]
        for h in range(1, N_HEADS):
            dk_sum = dk_sum + dk[:, h * LANES:(h + 1) * LANES]
        return jnp.concatenate(dqs, axis=1), _rope_bwd(dk_sum, c_only, up, dn)

    dq_lin, d_kr = row(mla_post, name="mla_unrope",
                       ins=[(dq_a, D_MODEL, 0, "row"), (dk_a, D_MODEL, 0, "row"), pos(rope_c), pos(rope_up), pos(rope_dn), pos(rope_c_only)],
                       outs=[(N_HEADS * LANES, BF16), (LANES, BF16)])
    d_qn = mm(dq_lin, w_uq_pt, mode="nn", name="d_qn", tm=1024, tn=Q_LORA, tk=1024)
    d_kvn_k = mm(dk_a, w_uk_p, mode="nt", name="d_kvn_k", tm=1024, tn=KV_LORA, tk=1024)
    d_kvn = mm(dv_a, w_uv, mode="nt", name="d_kvn", tm=1024, tn=KV_LORA, tk=1024, extras=(d_kvn_k,), epilogue=lambda acc, e: (acc + e,))

    def rms_bwd(lora, dq, dkv, dkr, gq, gkv):
        dxq, dgq = _rms_bwd(lora[:, :Q_LORA], gq, dq)
        dxk, dgk = _rms_bwd(lora[:, Q_LORA:], gkv, dkv)
        tail = jnp.zeros((dxq.shape[0], P_GATE - P_KR - LANES), F32)
        return jnp.concatenate([dxq, dxk, dkr.astype(F32), tail], axis=1), dgq, dgk

    d_tail, d_g_q_a, d_g_kv_a = row(
        rms_bwd, name="mla_rms_bwd",
        ins=[(proj, LORA_W, lora0, "row"), (d_qn, Q_LORA, 0, "row"), (d_kvn, KV_LORA, 0, "row"), (d_kr, LANES, 0, "row"),
             (g_q_a, 0, 0, "full"), (g_kv_a, 0, 0, "full")],
        outs=[(P_GATE, BF16)], sums=[Q_LORA, KV_LORA])
    d_proj = [d_tail, d_gates, d_qkv_d]
    d_w_in_pt = mm(d_proj, x, mode="tn", name="d_w_in", tm=1024, tn=1024, tk=1024)
    d_w_in = _unpad_w_in_t(d_w_in_pt).reshape(N_CHIPS, IN_WIDTH // N_CHIPS, D_MODEL)
    moving = None if first_grad is None else first_grad(d_w_in)
    d_w_uq_pt = mm(dq_lin, qn, mode="tn", name="d_w_uq", tm=1024, tn=Q_LORA, tk=1024, after=moving)
    d_w_uk_p = mm(kvn, dk_a, mode="tn", name="d_w_uk", tm=KV_LORA, tn=1024, tk=1024, after=moving)
    d_w_uv = mm(kvn, dv_a, mode="tn", name="d_w_uv", tm=KV_LORA, tn=1024, tk=1024, after=moving)
    grads.update(w_in=d_w_in,
                 w_uq=_unpad_head_rows(d_w_uq_pt, MLA_QK).reshape(N_CHIPS, N_HEADS * MLA_QK // N_CHIPS, Q_LORA),
                 w_ukv=_split_cols(_merge_ukv(d_w_uk_p, d_w_uv)))
    leaving = None if last_grads is None else last_grads(grads)
    grad_x = mm(d_proj, jnp.concatenate([w_main_t, w_dil_t], axis=0), mode="nn", name="d_x", tm=1024, tn=1024, tk=1024, extras=(dr1,), epilogue=lambda acc, rv: (acc + ALPHA * rv,),
                after=leaving)

    grads.update(
        b_gate=d_b_gate.reshape(2, D_MODEL), g_q_a=d_g_q_a, g_kv_a=d_g_kv_a, ln1_g=d_ln1_g, ln1_b=d_ln1_b, ln2_g=d_ln2_g, ln2_b=d_ln2_b)
    return loss_l, grad_x.reshape(batch, seq, D_MODEL), grads


BIG = ("w_in", "w_uq", "w_ukv", "w_o_mla", "w_o_dil", "w_out", "w_ff1", "w_ff2")
SMALL = (("b_gate", 2 * D_MODEL), ("g_q_a", Q_LORA), ("g_kv_a", KV_LORA), ("ln1_g", D_MODEL), ("ln1_b", D_MODEL),
         ("ln2_g", D_MODEL), ("ln2_b", D_MODEL))
TRANSPOSED = ("w_in", "w_uq")
D2D_PIECES = (4, 2, 1)
ANY = pl.BlockSpec(memory_space=pl.ANY)
SIDE_EFFECTS = pltpu.CompilerParams(has_side_effects=True)


def _place():
    x, y, c = lax.axis_index("x"), lax.axis_index("y"), lax.axis_index("c")
    return x, y, c, ((1 - x, y), (x, 1 - y), (1 - x, 1 - y))


def _half_axis(shape):
    return 0 if shape[0] % 32 == 0 else 1


def _half_shape(shape):
    return (shape[0] // 2, shape[1]) if _half_axis(shape) == 0 else (shape[0], shape[1] // 2)


def _window(ref, lead, shape, which=None, pieces=False):
    axis = _half_axis(shape)
    size = shape[axis] if which is None else shape[axis] // 2
    base = 0 if which is None else which * size
    tile = (16, LANES)[axis]
    count = next(c for c in D2D_PIECES if size % (tile * c) == 0) if pieces else 1
    step = size // count
    spans = [pl.ds(pl.multiple_of(base + i * step, tile), step) for i in range(count)]
    refs = [ref.at[(*lead, s)] if axis == 0 else ref.at[(*lead, slice(None), s)] for s in spans]
    return refs if pieces else refs[0]


def _remote(src, dst, send, recv, to):
    return pltpu.make_async_remote_copy(src_ref=src, dst_ref=dst, send_sem=send, recv_sem=recv, device_id=to, device_id_type=MESH)


def _pair_split(grads, name):
    n = len(grads)

    def body(*refs):
        srcs, outs, (send, recv) = refs[:n], refs[n:2 * n], refs[2 * n:]
        x, y, c, _ = _place()
        for t in range(n):
            for s in range(N_CHIPS):
                _remote(_window(srcs[t], (s,), grads[t].shape[1:], 1 - c), outs[t].at[s], send.at[t], recv.at[t], (x, y, 1 - c)).start()
        for t in range(n):
            _remote(_window(srcs[t], (slice(None),), grads[t].shape[1:], 1 - c), outs[t], send.at[t], recv.at[t], (x, y, 1 - c)).wait()

    return pl.pallas_call(
        body, name=name, in_specs=[ANY] * n, out_specs=[ANY] * n,
        out_shape=[jax.ShapeDtypeStruct((N_CHIPS,) + _half_shape(g.shape[1:]), g.dtype) for g in grads],
        scratch_shapes=[pltpu.SemaphoreType.DMA((n,)), pltpu.SemaphoreType.DMA((n,))],
        compiler_params=SIDE_EFFECTS,
    )(*grads)


HBM = pl.BlockSpec(memory_space=pltpu.HBM)
SEM = pl.BlockSpec(memory_space=pltpu.SEMAPHORE)
SPLIT = pltpu.CompilerParams(has_side_effects=pltpu.SideEffectType.DATAFLOW_SIDE_EFFECTING)


def _in_hbm(a):
    return pltpu.with_memory_space_constraint(a, pltpu.HBM)


def _split_copies(kind, srcs, lands):
    x, y, c, chips = _place()
    out = []
    for t in range(len(srcs)):
        if kind == "pair":
            out += [(t, s % 3, _window(srcs[t], (s,), srcs[t].shape[1:], 1 - c), lands[t].at[s], (x, y, 1 - c)) for s in range(N_CHIPS)]
            continue
        if kind == "join":
            out += [(t, 0, a, b, (x, y, 1 - c)) for a, b in zip(_window(srcs[t], (), srcs[t].shape, None, True), _window(lands[t], (), srcs[t].shape, None, True))]
            continue
        if kind == "forward":
            shape, sibling = srcs[t].shape, (x, y, 1 - c)
            out += [(t, 0, a, b, sibling) for a, b in zip(_window(srcs[t], (), shape, None, True), _window(lands[t], (2 * x + y,), shape, None, True))]
            out += [(t, j, a, a, sibling) for j, (cx, cy) in enumerate(chips) for a in _window(lands[t], (2 * cx + cy,), shape, c, True)]
            continue
        for j, (cx, cy) in enumerate(chips):
            if kind == "gather":
                shape = srcs[t].shape
                out.append((t, j, _window(srcs[t], (), shape, c), _window(lands[t], (2 * x + y,), shape, c), (cx, cy, c)))
            else:
                out.append((t, j, srcs[t].at[2 * cx + cy], lands[t].at[j], (cx, cy, c)))
    return out


def _split_start(kind, srcs, land_shapes, name, lands=None, after=None):
    n = len(srcs)

    def body(*refs):
        src_refs, land_refs, sems, token = refs[:n], refs[n:2 * n], refs[-7 - 2 * n:-1 - 2 * n], refs[-1]
        for t, j, s, d, to in _split_copies(kind, src_refs, land_refs):
            _remote(s, d, sems[j], sems[3 + j], to).start()
        token[...] = jnp.zeros_like(token)

    lands = [_in_hbm(lax.empty(s.shape, s.dtype)) for s in land_shapes] if lands is None else list(lands)
    thru = [pltpu.HBM(a.shape, a.dtype) for a in list(srcs) + lands]
    res = pl.pallas_call(
        body, name=name,
        out_shape=(*[pltpu.SemaphoreType.DMA(())] * 6, *thru, jax.ShapeDtypeStruct((8, LANES), F32)),
        in_specs=[HBM] * (2 * n) + [ANY] * (after is not None), out_specs=(*[SEM] * 6, *[HBM] * (2 * n), pl.BlockSpec(memory_space=pltpu.VMEM)),
        input_output_aliases={i: 6 + i for i in range(2 * n)}, compiler_params=SPLIT,
    )(*[_in_hbm(s) for s in srcs], *lands, *([after] if after is not None else []))
    return res[:6], res[6:6 + n], res[6 + n:6 + 2 * n], res[-1]


def _split_wait(kind, sems, srcs, lands, after, name):
    n = len(srcs)

    def body(*refs):
        src_refs, land_refs, sem_refs = refs[:n], refs[n:2 * n], refs[2 * n:2 * n + 6]
        for t, j, s, d, to in _split_copies(kind, src_refs, land_refs):
            cp = _remote(s, d, sem_refs[j], sem_refs[3 + j], to)
            cp.wait_send()
            cp.wait_recv()

    res = pl.pallas_call(
        body, name=name, out_shape=[pltpu.HBM(a.shape, a.dtype) for a in list(srcs) + list(lands)],
        in_specs=[HBM] * (2 * n) + [SEM] * 6 + [ANY], out_specs=[HBM] * (2 * n),
        input_output_aliases={i: i for i in range(2 * n)}, compiler_params=SPLIT,
    )(*srcs, *lands, *sems, after)
    return res[:n], res[n:]


def _sum_all_devices(vec, name, after):
    n_rows = vec.shape[0]

    def body(v_ref, after_ref, out_ref, buf, send, recv):
        x, y, c, _ = _place()
        me = 4 * x + 2 * y + c
        buf[me] = v_ref[...]
        flips = [(a, b, d) for a in (0, 1) for b in (0, 1) for d in (0, 1)][1:]
        copies = []
        for r, (a, b, d) in enumerate(flips):
            px, py, pc = (1 - x if a else x), (1 - y if b else y), (1 - c if d else c)
            copies.append(pltpu.make_async_remote_copy(src_ref=v_ref, dst_ref=buf.at[me], send_sem=send.at[r], recv_sem=recv.at[r],
                                                       device_id=(px, py, pc), device_id_type=MESH))
            copies[-1].start()
        for r, (a, b, d) in enumerate(flips):
            px, py, pc = (1 - x if a else x), (1 - y if b else y), (1 - c if d else c)
            pltpu.make_async_remote_copy(src_ref=v_ref, dst_ref=buf.at[4 * px + 2 * py + pc], send_sem=send.at[r], recv_sem=recv.at[r],
                                         device_id=(px, py, pc), device_id_type=MESH).wait_recv()
        for cp in copies:
            cp.wait_send()
        total = buf[0]
        for k in range(1, N_DEV):
            total = total + buf[k]
        out_ref[...] = total

    vmem = pl.BlockSpec(memory_space=pltpu.VMEM)
    return pl.pallas_call(
        body, name=name, in_specs=[vmem, ANY], out_specs=vmem, out_shape=jax.ShapeDtypeStruct(vec.shape, F32),
        scratch_shapes=[pltpu.VMEM((N_DEV, n_rows, LANES), F32), pltpu.SemaphoreType.DMA((N_DEV - 1,)), pltpu.SemaphoreType.DMA((N_DEV - 1,))],
        compiler_params=pltpu.CompilerParams(has_side_effects=True),
    )(vec, after)


def _half_tile(half, width):
    t = half
    while t * width * 4 > (2 << 20) and t % 32 == 0:
        t //= 2
    return t


def _pair_add(g, theirs, core, name):
    _, half, width = theirs.shape
    t = _half_tile(half, width)
    n = half // t

    def body(c_ref, a_ref, b_ref, o_ref):
        o_ref[...] = (a_ref[...] + b_ref[...]).astype(BF16)

    tile = pl.BlockSpec((1, t, width), lambda j, i, c_ref: (j, i, 0))
    if _half_axis(g.shape[1:]) == 0:
        mine = pl.BlockSpec((1, t, width), lambda j, i, c_ref: (j, c_ref[0] * n + i, 0))
    else:
        mine = pl.BlockSpec((1, t, width), lambda j, i, c_ref: (j, i, c_ref[0]))
    return pl.pallas_call(
        body, name=name,
        grid_spec=pltpu.PrefetchScalarGridSpec(num_scalar_prefetch=1, grid=(N_CHIPS, n), in_specs=[mine, tile], out_specs=tile),
        out_shape=jax.ShapeDtypeStruct(theirs.shape, BF16), compiler_params=_params(("parallel", "parallel")),
    )(core, g, theirs)


def _chip_sum(part, others, chip, name, after=None):
    _, half, width = part.shape
    t = _half_tile(half, width)

    def body(s_ref, mine, p0, p1, p2, *rest):
        o_ref = rest[-1]
        o_ref[...] = ((mine[0].astype(F32) + p0[0].astype(F32)) + p1[0].astype(F32)) + p2[0].astype(F32)

    return pl.pallas_call(
        body, name=name,
        grid_spec=pltpu.PrefetchScalarGridSpec(
            num_scalar_prefetch=1, grid=(half // t,),
            in_specs=[pl.BlockSpec((1, t, width), lambda i, s_ref: (s_ref[0], i, 0))]
            + [pl.BlockSpec((1, t, width), lambda i, s_ref, j=j: (j, i, 0)) for j in range(3)] + [pl.BlockSpec(memory_space=pl.ANY)] * (after is not None),
            out_specs=pl.BlockSpec((t, width), lambda i, s_ref: (i, 0))),
        out_shape=jax.ShapeDtypeStruct((half, width), F32), compiler_params=_params(("parallel",)),
    )(chip, part, others, others, others, *([after] if after is not None else []))


EARLY = ("w_in", "w_uq", "w_ukv")
LATE = ("w_o_mla", "w_o_dil", "w_out", "w_ff1", "w_ff2")


def _sum_small(vals):
    n_in = len(vals)
    n_rows = sum(a.shape[0] * a.shape[1] // LANES for a in vals)
    pad_rows = -(-n_rows // 8) * 8

    def chunks(refs):
        return [(ref, a, j) for ref in refs for a in range(ref.shape[0]) for j in range(ref.shape[1] // LANES)]

    def body(*refs):
        ins, outs, (buf, send, recv) = refs[:n_in], refs[n_in:2 * n_in], refs[2 * n_in:]
        x, y, c, _ = _place()
        me = 4 * x + 2 * y + c
        for r, (ref, a, j) in enumerate(chunks(ins)):
            buf[me, r:r + 1, :] = ref[a:a + 1, j * LANES:(j + 1) * LANES]
        if pad_rows > n_rows:
            buf[me, n_rows:pad_rows, :] = jnp.zeros((pad_rows - n_rows, LANES), F32)
        flips = [(a, b, d) for a in (0, 1) for b in (0, 1) for d in (0, 1)][1:]
        peers = [((1 - x if a else x), (1 - y if b else y), (1 - c if d else c)) for a, b, d in flips]
        copies = [_remote(buf.at[me], buf.at[me], send.at[r], recv.at[r], peer) for r, peer in enumerate(peers)]
        for cp in copies:
            cp.start()
        for r, (px, py, pc) in enumerate(peers):
            _remote(buf.at[me], buf.at[4 * px + 2 * py + pc], send.at[r], recv.at[r], (px, py, pc)).wait_recv()
        for cp in copies:
            cp.wait_send()
        total = buf[0]
        for k in range(1, N_DEV):
            total = total + buf[k]
        for r, (ref, a, j) in enumerate(chunks(outs)):
            ref[a:a + 1, j * LANES:(j + 1) * LANES] = total[r:r + 1, :]

    vmem = pl.BlockSpec(memory_space=pltpu.VMEM)
    return pl.pallas_call(
        body, name="sum_small", in_specs=[vmem] * n_in, out_specs=[vmem] * n_in,
        out_shape=[jax.ShapeDtypeStruct(a.shape, F32) for a in vals],
        scratch_shapes=[pltpu.VMEM((N_DEV, pad_rows, LANES), F32), pltpu.SemaphoreType.DMA((N_DEV - 1,)), pltpu.SemaphoreType.DMA((N_DEV - 1,))],
        compiler_params=SIDE_EFFECTS,
    )(*vals)


def _adam_math(w, g, m, v):
    nm = B1 * m + (1.0 - B1) * g
    nv = B2 * v + (1.0 - B2) * (g * g)
    m_hat = nm / (1.0 - B1 ** ADAM_STEP)
    v_hat = nv / (1.0 - B2 ** ADAM_STEP)
    return -LR * (m_hat / (jnp.sqrt(v_hat) + ADAM_EPS) + WD * w), nm, nv


def _adamw_big(w, mine, theirs, m, v, core, name, side_by_side=False):
    rows, width = w.shape
    if side_by_side:
        t = next(c for c in (152, 96, 64, 32, 16, 8) if rows % c == 0)
        hb = None
        half_spec = pl.BlockSpec((t, width // 2), lambda i, c_ref: (i, 0))
    else:
        t = next(c for c in (256, 128, 64, 32, 16, 8) if (rows // 2) % c == 0)
        hb = rows // 2 // t
        half_spec = pl.BlockSpec((t, width), lambda i, c_ref: (i % hb, 0))

    def body(c_ref, w_ref, a_ref, b_ref, m_ref, v_ref, g_ref, d_ref, nm_ref, nv_ref):
        south = c_ref[0] == 0
        if side_by_side:
            g = jnp.where(south, jnp.concatenate([a_ref[...], b_ref[...]], axis=1), jnp.concatenate([b_ref[...], a_ref[...]], axis=1))
        else:
            g = jnp.where((pl.program_id(0) < hb) == south, a_ref[...], b_ref[...])
        g_ref[...] = g
        d_ref[...], nm_ref[...], nv_ref[...] = _adam_math(w_ref[...], g, m_ref[...], v_ref[...])

    spec = pl.BlockSpec((t, width), lambda i, c_ref: (i, 0))
    return pl.pallas_call(
        body, name=name,
        grid_spec=pltpu.PrefetchScalarGridSpec(num_scalar_prefetch=1, grid=(rows // t,),
                                               in_specs=[spec, half_spec, half_spec, spec, spec], out_specs=[spec] * 4),
        out_shape=[jax.ShapeDtypeStruct(w.shape, F32)] * 4, compiler_params=_params(("parallel",)),
    )(core, w, mine, theirs, m, v)


def _adamw_small(ws, gs, ms, vs):
    n = len(ws)

    def body(*refs):
        for t in range(n):
            w_ref, g_ref, m_ref, v_ref = (refs[k * n + t] for k in range(4))
            d, nm, nv = _adam_math(w_ref[...], g_ref[...], m_ref[...], v_ref[...])
            refs[4 * n + t][...] = d
            refs[5 * n + t][...] = nm
            refs[6 * n + t][...] = nv

    vmem = pl.BlockSpec(memory_space=pltpu.VMEM)
    res = pl.pallas_call(body, name="adamw_small", in_specs=[vmem] * (4 * n), out_specs=[vmem] * (3 * n),
                         out_shape=[jax.ShapeDtypeStruct(a.shape, F32) for a in ws] * 3)(*ws, *gs, *ms, *vs)
    return res[:n], res[n:2 * n], res[2 * n:]


def kernel(x, w_in, b_gate, g_q_a, w_uq, g_kv_a, w_ukv, w_o_mla, w_o_dil, w_out, ln1_g, ln1_b, w_ff1, w_ff2, ln2_g, ln2_b, loss_target, m_w_in, m_b_gate, m_g_q_a, m_w_uq, m_g_kv_a, m_w_ukv, m_w_o_mla, m_w_o_dil, m_w_out, m_ln1_g, m_ln1_b, m_w_ff1, m_w_ff2, m_ln2_g, m_ln2_b, v_w_in, v_b_gate, v_g_q_a, v_w_uq, v_g_kv_a, v_w_ukv, v_w_o_mla, v_w_o_dil, v_w_out, v_ln1_g, v_ln1_b, v_w_ff1, v_w_ff2, v_ln2_g, v_ln2_b):
    order = ("w_in", "b_gate", "g_q_a", "w_uq", "g_kv_a", "w_ukv", "w_o_mla", "w_o_dil", "w_out", "ln1_g", "ln1_b", "w_ff1", "w_ff2", "ln2_g", "ln2_b")
    w = dict(w_in=w_in, b_gate=b_gate, g_q_a=g_q_a, w_uq=w_uq, g_kv_a=g_kv_a, w_ukv=w_ukv, w_o_mla=w_o_mla, w_o_dil=w_o_dil, w_out=w_out,
             ln1_g=ln1_g, ln1_b=ln1_b, w_ff1=w_ff1, w_ff2=w_ff2, ln2_g=ln2_g, ln2_b=ln2_b)
    m = dict(w_in=m_w_in, b_gate=m_b_gate, g_q_a=m_g_q_a, w_uq=m_w_uq, g_kv_a=m_g_kv_a, w_ukv=m_w_ukv, w_o_mla=m_w_o_mla, w_o_dil=m_w_o_dil,
             w_out=m_w_out, ln1_g=m_ln1_g, ln1_b=m_ln1_b, w_ff1=m_w_ff1, w_ff2=m_w_ff2, ln2_g=m_ln2_g, ln2_b=m_ln2_b)
    v = dict(w_in=v_w_in, b_gate=v_b_gate, g_q_a=v_g_q_a, w_uq=v_w_uq, g_kv_a=v_g_kv_a, w_ukv=v_w_ukv, w_o_mla=v_w_o_mla, w_o_dil=v_w_o_dil,
             w_out=v_w_out, ln1_g=v_ln1_g, ln1_b=v_ln1_b, w_ff1=v_w_ff1, w_ff2=v_w_ff2, ln2_g=v_ln2_g, ln2_b=v_ln2_b)
    chip = 2 * lax.axis_index("x") + lax.axis_index("y")
    south = (lax.axis_index("c") == 0).astype(F32)
    gate_w = D_MODEL // N_CHIPS

    core = lax.axis_index("c").astype(jnp.int32).reshape(1)
    turn = lambda n, a: a.T if n in TRANSPOSED else a
    shards = {n: turn(n, w[n][0]).astype(BF16) for n in BIG}
    early_shards, late_shards = [shards[n] for n in EARLY], [shards[n] for n in LATE]
    gathered = lambda group: [jax.ShapeDtypeStruct((N_CHIPS,) + s.shape, BF16) for s in group]
    e_sems, e_srcs, e_lands, e_token = _split_start("gather", early_shards, gathered(early_shards), "gather_early_start")
    b_mine = lax.dynamic_update_slice(jnp.zeros((2, D_MODEL), F32), b_gate[0] * south, (0, chip * gate_w))
    b_full = _sum_all_devices(b_mine.reshape(-1, LANES), "gather_b_gate", e_token).reshape(2, D_MODEL)
    e_srcs, e_lands = _split_wait("gather", e_sems, e_srcs, e_lands, b_full, "gather_early_wait")
    e_forward = _split_start("forward", e_srcs, None, "gather_early_forward_start", lands=e_lands)
    first = dict(zip(EARLY, _split_wait("forward", *e_forward[:3], e_forward[-1], "gather_early_forward_wait")[1]))
    g_sems, g_srcs, g_lands, g_token = _split_start("gather", late_shards, gathered(late_shards), "gather_late_start", after=first[EARLY[-1]])
    tables = _attention_tables(x.shape[1])

    sent = {}

    def late_arrived(after):
        srcs, lands = _split_wait("gather", g_sems, g_srcs, g_lands, after, "gather_late_wait")
        sent["forward"] = _split_start("forward", srcs, None, "gather_late_forward_start", lands=lands)
        return sent["forward"][-1]

    def late_weights(after):
        return dict(zip(LATE, _split_wait("forward", *sent["forward"][:3], after, "gather_late_forward_wait")[1]))

    exchange_shapes = lambda parts: [jax.ShapeDtypeStruct((3,) + p.shape[1:], BF16) for p in parts]

    def early_grads(grads_late):
        gs = [grads_late[n] for n in LATE]
        shapes = [jax.ShapeDtypeStruct((N_CHIPS,) + _half_shape(g.shape[1:]), F32) for g in gs]
        sent["pair"] = _split_start("pair", gs, shapes, "pair_split_late_start")
        return sent["pair"][-1]

    def early_grads_go(after):
        gs, theirs = _split_wait("pair", *sent["pair"][:3], after, "pair_split_late_wait")
        parts = [_pair_add(g, th, core, "pair_add_" + n) for g, th, n in zip(gs, theirs, LATE)]
        sent["late"] = _split_start("scatter", parts, exchange_shapes(parts), "exchange_late_start")
        return sent["late"][-1]

    def first_grad(g):
        sent["first"] = _split_start("pair", [g], [jax.ShapeDtypeStruct((N_CHIPS,) + _half_shape(g.shape[1:]), F32)], "pair_split_w_in_start")
        return sent["first"][-1]

    def last_grads(grads_early):
        rest = [grads_early[n] for n in EARLY[1:]]
        gs, theirs = _split_wait("pair", *sent["first"][:3], rest[-1], "pair_split_w_in_wait")
        gs, theirs = list(gs) + rest, list(theirs) + list(_pair_split(rest, "pair_split_early"))
        parts = [_pair_add(g, th, core, "pair_add_" + n) for g, th, n in zip(gs, theirs, EARLY)]
        sent["early"] = _split_start("scatter", parts, exchange_shapes(parts), "exchange_early_start")
        return sent["early"][-1]

    loss_part, grad_x, grads = _local_step(x, loss_target, first, b_full, g_q_a, g_kv_a, ln1_g, ln1_b, ln2_g, ln2_b, token=g_token,
                                           late_arrived=late_arrived, late_weights=late_weights, early_grads=early_grads, early_grads_go=early_grads_go,
                                           first_grad=first_grad, last_grads=last_grads, tables=tables)

    g_out, delta, new_m, new_v = {}, {}, {}, {}
    chip1 = chip.astype(jnp.int32).reshape(1)

    def sum_and_send(names, parts, others, tag):
        totals = [_chip_sum(p, o, chip1, "chip_sum_" + n) for n, p, o in zip(names, parts, others)]
        return _split_start("join", totals, [jax.ShapeDtypeStruct(t.shape, F32) for t in totals], "pair_join_" + tag + "_start")

    def adam(names, joined, after, tag):
        totals, halves = _split_wait("join", *joined[:3], after, "pair_join_" + tag + "_wait")
        for n, mine, theirs in zip(names, totals, halves):
            res = _adamw_big(turn(n, w[n][0]), mine, theirs, turn(n, m[n][0]), turn(n, v[n][0]), core, "adamw_" + n,
                             side_by_side=mine.shape[0] == shards[n].shape[0])
            g_out[n], delta[n], new_m[n], new_v[n] = (turn(n, r) for r in res)

    late_joined = sum_and_send(LATE, *_split_wait("scatter", *sent["late"][:3], grad_x, "exchange_late_wait"), "late")
    early_joined = sum_and_send(EARLY, *_split_wait("scatter", *sent["early"][:3], late_joined[-1], "exchange_early_wait"), "early")
    small_names = [name for name, _ in SMALL]
    sums = _sum_small([grads[name] for name in small_names] + [loss_part])
    loss = sums[-1][0, 0]
    g_small = dict(zip(small_names, sums))
    g_small["b_gate"] = lax.dynamic_slice(g_small["b_gate"], (0, chip * gate_w), (2, gate_w))
    flat = lambda a: a.reshape(-1, a.shape[-1])
    res = _adamw_small(*[[flat(d[name]) for name in small_names] for d in (w, g_small, m, v)])
    g_out.update(g_small)
    for d, r in zip((delta, new_m, new_v), res):
        d.update(zip(small_names, r))
    adam(LATE, late_joined, res[0][0], "late")
    adam(EARLY, early_joined, delta[LATE[-1]], "early")

    lead = lambda d: [d[name].reshape(w[name].shape) for name in order]
    return (loss, grad_x, *lead(g_out), *lead(delta), *lead(new_m), *lead(new_v))
```

```python
import functools
import math

import jax
import jax.numpy as jnp
import numpy as np
from jax import lax
from jax.experimental import pallas as pl
from jax.experimental.pallas import tpu as pltpu

F32 = jnp.float32
BF16 = jnp.bfloat16
MESH = pl.DeviceIdType.MESH

D_MODEL = 1024
N_HEADS = 8
LANES = 128
NOPE, ROPE, V_DIM = 64, 32, 64
MLA_QK = NOPE + ROPE
Q_LORA, KV_LORA = 384, 256
DIL_DIM = 64
DIL_PATTERNS = ((128, 1), (512, 4), (2048, 16))
D_FF = 4096
N_CHIPS = 4
N_DEV = 8
IN_WIDTH = 4256
LN_EPS, RMS_EPS = 1e-5, 1e-6
NEG = -1e30
LOG2E, LN2 = 1.4426950408889634, 0.6931471805599453
ALPHA = 2.0 ** 0.25
ROPE_THETA = 10000.0
LR, B1, B2, ADAM_EPS, WD, ADAM_STEP = 0.001, 0.9, 0.999, 1e-8, 0.01, 10

P_LORA, P_KR, P_GATE, P_HALF = 0, 640, 1024, 3072
DIL_GROUP = 4 * LANES
P_DIL = N_HEADS // 2 * DIL_GROUP
LORA_W = Q_LORA + KV_LORA
KR_LANE = NOPE

ATT_T = 512
ROW_T = 512
ACC_COLS = 256
VMEM_LIMIT = 56 * 1024 * 1024

NN = (((1,), (0,)), ((), ()))
NT = (((1,), (1,)), ((), ()))
TN = (((0,), (0,)), ((), ()))


def _params(sem=None, **kw):
    return pltpu.CompilerParams(dimension_semantics=sem, vmem_limit_bytes=VMEM_LIMIT, **kw)


def _matmul(a, b, *, mode, name, tm, tn, tk, out_dtypes=(F32,), extras=(), epilogue=None, b_shards=False, out_shards=False, after=None):
    pieces = list(a) if isinstance(a, (list, tuple)) else [a]
    n_pc = len(pieces)
    a_shape = (pieces[0].shape[0], sum(p.shape[1] for p in pieces))
    if b_shards:
        n_sh, rows_b, cols_b = b.shape
        b_shape = (rows_b, n_sh * cols_b)
    else:
        b_shape = b.shape
    if mode == "nn":
        (m, k), (k2, n) = a_shape, b_shape
    elif mode == "nt":
        (m, k), (n, k2) = a_shape, b_shape
    else:
        (k, m), (k2, n) = a_shape, b_shape
    assert k == k2, (a_shape, b.shape, mode)
    tm, tn, tk = min(tm, m), min(tn, n), min(tk, k)
    assert m % tm == 0 and n % tn == 0 and k % tk == 0, (name, m, n, k, tm, tn, tk)
    nk = k // tk
    n_ex, n_out = len(extras), len(out_dtypes)
    n_in = n_pc + 1 + n_ex + (after is not None)
    dims = {"nn": NN, "nt": NT, "tn": TN}[mode]
    col_tile = tm if mode == "tn" else tk
    blocks = [p.shape[1] // col_tile for p in pieces]
    firsts = [sum(blocks[:p]) for p in range(n_pc)]
    assert all(p.shape[1] % col_tile == 0 for p in pieces), (name, col_tile)

    def body(*refs):
        a_refs, b_ref = refs[:n_pc], refs[n_pc]
        ex_refs = refs[n_pc + 1:n_pc + 1 + n_ex]
        out_refs = refs[n_in:n_in + n_out]

        def finish(acc):
            outs = epilogue(acc, *[r[...] for r in ex_refs]) if epilogue is not None else (acc,)
            for r, o in zip(out_refs, outs):
                r[...] = o.astype(r.dtype)

        kk = pl.program_id(2)

        def step(a_ref):
            if nk == 1:
                finish(lax.dot_general(a_ref[...].astype(BF16), b_ref[...].astype(BF16), dims, preferred_element_type=F32))
                return
            acc_ref = refs[-1]

            @pl.when(kk == 0)
            def _():
                acc_ref[...] = jnp.zeros_like(acc_ref)

            for c in range(0, tn, ACC_COLS):
                b_blk = b_ref[c:c + ACC_COLS, :] if mode == "nt" else b_ref[:, c:c + ACC_COLS]
                acc_ref[:, c:c + ACC_COLS] += lax.dot_general(a_ref[...].astype(BF16), b_blk.astype(BF16), dims, preferred_element_type=F32)

            @pl.when(kk == nk - 1)
            def _():
                finish(acc_ref[...])

        if n_pc == 1:
            step(a_refs[0])
        else:
            at = pl.program_id(0) if mode == "tn" else kk
            for p in range(n_pc):
                pl.when(jnp.logical_and(at >= firsts[p], at < firsts[p] + blocks[p]))(functools.partial(step, a_refs[p]))

    def a_spec_of(p):
        if n_pc == 1:
            return pl.BlockSpec((tk, tm), lambda i, j, kk: (kk, i)) if mode == "tn" else pl.BlockSpec((tm, tk), lambda i, j, kk: (i, kk))
        col = lambda at: jnp.clip(at - firsts[p], 0, blocks[p] - 1)
        mine = lambda at: jnp.logical_and(at >= firsts[p], at < firsts[p] + blocks[p])
        if mode == "tn":
            return pl.BlockSpec((tk, tm), lambda i, j, kk: (jnp.where(mine(i), kk, 0), col(i)))
        return pl.BlockSpec((tm, tk), lambda i, j, kk: (i, col(kk)))

    b_spec = {"nn": pl.BlockSpec((tk, tn), lambda i, j, kk: (kk, j)),
              "nt": pl.BlockSpec((tn, tk), lambda i, j, kk: (j, kk)),
              "tn": pl.BlockSpec((tk, tn), lambda i, j, kk: (kk, j))}[mode]
    tile = pl.BlockSpec((tm, tn), lambda i, j, kk: (i, j))
    out_spec, out_dims = tile, (m, n)
    if b_shards and mode == "nn":
        per = cols_b // tn
        b_spec = pl.BlockSpec((None, tk, tn), lambda i, j, kk: (j // per, kk, j % per))
    elif b_shards:
        assert mode == "nt"
        per = cols_b // tk
        b_spec = pl.BlockSpec((None, tn, tk), lambda i, j, kk: (kk // per, j, kk % per))
    if out_shards:
        assert not extras and epilogue is None
        per_out = n // N_CHIPS // tn
        out_spec = pl.BlockSpec((None, tm, tn), lambda i, j, kk: (j // per_out, i, j % per_out))
        out_dims = (N_CHIPS, m, n // N_CHIPS)
    outs = pl.pallas_call(
        body, name=name,
        grid=(m // tm, n // tn, nk),
        in_specs=[a_spec_of(p) for p in range(n_pc)] + [b_spec] + [tile] * n_ex + [pl.BlockSpec(memory_space=pl.ANY)] * (after is not None),
        out_specs=[out_spec] * n_out,
        out_shape=[jax.ShapeDtypeStruct(out_dims, dt) for dt in out_dtypes],
        scratch_shapes=[pltpu.VMEM((tm, tn), F32)] if nk > 1 else [],
        compiler_params=_params(("parallel", "parallel", "arbitrary")),
    )(*pieces, b, *extras, *([after] if after is not None else []))
    return outs[0] if n_out == 1 else outs


def _rowwise(fn, *, name, rows, seq, ins, outs, sums=()):
    tm = min(ROW_T, seq)
    n_pos = seq // tm
    n_in, n_out, n_sum = len(ins), len(outs), len(sums)

    def body(*refs):
        vals = fn(*[r[...] for r in refs[:n_in]])
        for r, v in zip(refs[n_in:n_in + n_out], vals[:n_out]):
            r[...] = v.astype(r.dtype)
        first = pl.program_id(0) == 0
        for r, v in zip(refs[n_in + n_out:], vals[n_out:]):
            @pl.when(first)
            def _(r=r, v=v):
                r[...] = v

            @pl.when(jnp.logical_not(first))
            def _(r=r, v=v):
                r[...] += v

    def spec(arr, width, col, kind):
        if kind == "row":
            return pl.BlockSpec((tm, width), lambda i, col=col: (i, col))
        if kind == "pos":
            return pl.BlockSpec((tm, width), lambda i, col=col: (i % n_pos, col))
        return pl.BlockSpec(arr.shape, lambda i: (0,) * arr.ndim)

    res = pl.pallas_call(
        body, name=name,
        grid=(rows // tm,),
        in_specs=[spec(*t) for t in ins],
        out_specs=[pl.BlockSpec((tm, w), lambda i: (i, 0)) for w, _ in outs]
        + [pl.BlockSpec((1, w), lambda i: (0, 0)) for w in sums],
        out_shape=[jax.ShapeDtypeStruct((rows, w), dt) for w, dt in outs]
        + [jax.ShapeDtypeStruct((1, w), F32) for w in sums],
        compiler_params=_params(("arbitrary",)),
    )(*[t[0] for t in ins])
    return res


def _colsum(v):
    return jnp.sum(v, axis=0, keepdims=True)


def _rope_fwd(t, c, s_up, s_dn):
    return t * c + pltpu.roll(t, LANES - 16, 1) * s_up + pltpu.roll(t, 16, 1) * s_dn


def _rope_bwd(d, c, s_up, s_dn):
    return d * c + pltpu.roll(d * s_up, 16, 1) + pltpu.roll(d * s_dn, LANES - 16, 1)


def _rope_tables(positions):
    seq, half = positions.shape[0], ROPE // 2
    inv = jnp.power(ROPE_THETA, -jnp.arange(half, dtype=F32) / half)
    ang = positions[:, None] * inv[None, :]
    cos, sin = jnp.cos(ang), jnp.sin(ang)
    zeros = jnp.zeros((seq, half), F32)
    lo, hi = jnp.ones((seq, KR_LANE), F32), jnp.ones((seq, LANES - KR_LANE - ROPE), F32)
    c = jnp.concatenate([lo, cos, cos, hi], axis=1)
    c_rope_only = jnp.concatenate([0 * lo, cos, cos, 0 * hi], axis=1)
    s_up = jnp.concatenate([0 * lo, -sin, zeros, 0 * hi], axis=1)
    s_dn = jnp.concatenate([0 * lo, zeros, sin, 0 * hi], axis=1)
    return c, s_up, s_dn, c_rope_only


def _rms(x, g):
    r = lax.rsqrt(jnp.mean(x * x, axis=1, keepdims=True) + RMS_EPS)
    return x * r * g


def _rms_bwd(x, g, dy):
    r = lax.rsqrt(jnp.mean(x * x, axis=1, keepdims=True) + RMS_EPS)
    xh = x * r
    dxh = dy * g
    dx = r * (dxh - xh * jnp.mean(dxh * xh, axis=1, keepdims=True))
    return dx, _colsum(dy * xh)


def _ln_stats(x):
    mu = jnp.mean(x, axis=1, keepdims=True)
    xc = x - mu
    r = lax.rsqrt(jnp.mean(xc * xc, axis=1, keepdims=True) + LN_EPS)
    return xc * r, r


def _ln_bwd(xh, r, g, dy):
    dxh = dy * g
    dx = r * (dxh - jnp.mean(dxh, axis=1, keepdims=True) - xh * jnp.mean(dxh * xh, axis=1, keepdims=True))
    return dx, _colsum(dy * xh), _colsum(dy)


def _table_specs(tables, sub):
    whole = lambda a: pl.BlockSpec(a.shape, lambda b, g: (0,) * a.ndim)
    if len(tables) == 1:
        return [whole(tables[0])]
    return [whole(tables[0]), whole(tables[1]), pl.BlockSpec((sub, 1, LANES), lambda b, g: (g, 0, 0))]


def _biased(s, table_refs, delta, head):
    if delta < table_refs[0].shape[0]:
        s = s + table_refs[0][delta]
    if len(table_refs) == 3:
        s = s - table_refs[2][head, 0:1, 0:1] * table_refs[1][delta]
    return s


def _lane_masks(sub):
    lane = lax.broadcasted_iota(jnp.int32, (1, LANES), 1)
    return [(lane // (LANES // sub) == a).astype(F32) for a in range(sub)]


def _attn_fwd(q, qb0, k, kb0, v, vb0, tables, scale, *, name, batch, seq, sub=1, stride=1, wide_qk=False, after=None):
    t = ATT_T
    nq = seq // t
    rows = batch * seq
    n_tab = len(tables)
    qk_w = sub * LANES if wide_qk else LANES

    def body(q_ref, k_ref, v_ref, *rest):
        table_refs = rest[:n_tab]
        o_ref, lse_ref, vtb = rest[n_tab + (after is not None):][:3]
        per_head = rest[n_tab + (after is not None) + 3:]
        qbs, kbs = per_head[:sub], per_head[sub:]
        masks = _lane_masks(sub)
        for a in range(sub):
            lanes = slice(a * LANES, (a + 1) * LANES) if wide_qk else slice(None)
            qa = q_ref[:, lanes]
            qbs[a][...] = (qa.astype(F32) * masks[a]).astype(BF16) if sub > 1 and not wide_qk else qa.astype(BF16)
            if wide_qk or a == 0:
                kbs[a][...] = k_ref[:, lanes].astype(BF16)
        vtb[...] = v_ref[...].astype(F32).T.astype(BF16)
        for i in range(nq):
            out_t = None
            for a in range(sub):
                qt, kb = qbs[a][i * t:(i + 1) * t, :], kbs[a if wide_qk else 0]
                logits = [_biased(lax.dot_general(kb[j * t:(j + 1) * t, :], qt, NT, preferred_element_type=F32) * (scale * LOG2E), table_refs, i - j, a)
                          for j in range(i + 1)]
                m = jnp.max(functools.reduce(jnp.maximum, logits), axis=0, keepdims=True)
                ps = [jnp.exp2(s - m) for s in logits]
                l = jnp.sum(functools.reduce(jnp.add, ps), axis=0, keepdims=True)
                acc = functools.reduce(jnp.add, [lax.dot_general(vtb[:, j * t:(j + 1) * t], p.astype(BF16), NN, preferred_element_type=F32)
                                                 for j, p in enumerate(ps)])
                part = acc / l if sub == 1 else (acc / l) * masks[a].T
                out_t = part if out_t is None else out_t + part
                lse_ref[i * t:(i + 1) * t, a * LANES:(a + 1) * LANES] = jnp.broadcast_to((m + jnp.log2(l)) * LN2, (LANES, t)).T
            o_ref[i * t:(i + 1) * t, :] = out_t.T

    slab = lambda b0, step, width=LANES: pl.BlockSpec((seq, width), lambda b, g: (b, b0 + step * g))
    groups = N_HEADS // sub
    n_k = sub if wide_qk else 1
    return pl.pallas_call(
        body, name=name,
        grid=(batch, groups),
        in_specs=[slab(qb0, stride, qk_w), slab(kb0, stride, qk_w), slab(vb0, stride)] + _table_specs(tables, sub)
        + [pl.BlockSpec(memory_space=pl.ANY)] * (after is not None),
        out_specs=[slab(0, 1), slab(0, 1, sub * LANES)],
        out_shape=[jax.ShapeDtypeStruct((rows, groups * LANES), F32), jax.ShapeDtypeStruct((rows, N_HEADS * LANES), F32)],
        scratch_shapes=[pltpu.VMEM((LANES, seq), BF16)] + [pltpu.VMEM((seq, LANES), BF16)] * (sub + n_k),
        compiler_params=_params(("arbitrary", "arbitrary")),
    )(q, k, v, *tables, *([after] if after is not None else []))


def _attn_bwd(q, qb0, k, kb0, v, vb0, o, do, lse, tables, scale, *, name, batch, seq, out_dtype, sub=1, stride=1, wide_qk=False, after=None):
    t = ATT_T
    nq = seq // t
    rows = batch * seq
    n_tab = len(tables)
    groups = N_HEADS // sub
    packed = sub > 1 and not wide_qk
    n_out = 1 if packed else 3
    n_k = sub if wide_qk else 1
    qk_w = sub * LANES if wide_qk else LANES

    def body(q_ref, k_ref, v_ref, o_ref, do_ref, lse_ref, *rest):
        table_refs = rest[:n_tab]
        rest = rest[n_tab + (after is not None):]
        out_refs, (vb, dva), rest = rest[:n_out], rest[n_out:n_out + 2], rest[n_out + 2:]
        kbs, dkas, rest = rest[:n_k], rest[n_k:2 * n_k], rest[2 * n_k:]
        qbs, dobs, qtbs, dotbs = (rest[g * sub:(g + 1) * sub] for g in range(4))
        masks = _lane_masks(sub)
        vb[...] = v_ref[...].astype(BF16)
        for a in range(sub):
            lanes = slice(a * LANES, (a + 1) * LANES) if wide_qk else slice(None)
            qa = q_ref[:, lanes].astype(F32) * masks[a] if packed else q_ref[:, lanes].astype(F32)
            doa = do_ref[...] * masks[a] if sub > 1 else do_ref[...]
            qbs[a][...] = qa.astype(BF16)
            dobs[a][...] = doa.astype(BF16)
            qtbs[a][...] = qa.T.astype(BF16)
            dotbs[a][...] = doa.T.astype(BF16)
            if wide_qk or a == 0:
                kbs[a][...] = k_ref[:, lanes].astype(BF16)
        first_k, first_v = [[True] * nq for _ in range(n_k)], [True] * nq
        for i in range(nq):
            at = slice(i * t, (i + 1) * t)
            dq_all = None
            for a in range(sub):
                qt, dot, kb, dka = qbs[a][at, :], dobs[a][at, :], kbs[a if wide_qk else 0], dkas[a if wide_qk else 0]
                lse_t = lse_ref[at, a * LANES:a * LANES + 1] * LOG2E
                od = o_ref[at, :] * do_ref[at, :]
                delta = jnp.sum(od * masks[a] if sub > 1 else od, axis=1, keepdims=True)
                dq = None
                for j in range(i + 1):
                    kat = slice(j * t, (j + 1) * t)
                    kt, vt = kb[kat, :], vb[kat, :]
                    p = jnp.exp2(_biased(lax.dot_general(qt, kt, NT, preferred_element_type=F32) * (scale * LOG2E), table_refs, i - j, a) - lse_t)
                    dp = lax.dot_general(dot, vt, NT, preferred_element_type=F32)
                    ds = (p * (dp - delta) * scale).astype(BF16)
                    dk_part = lax.dot_general(qtbs[a][:, at], ds, NN, preferred_element_type=F32)
                    dv_part = lax.dot_general(dotbs[a][:, at], p.astype(BF16), NN, preferred_element_type=F32)
                    firsts = first_k[a if wide_qk else 0]
                    if firsts[j]:
                        dka[:, kat] = dk_part
                        firsts[j] = False
                    else:
                        dka[:, kat] += dk_part
                    if first_v[j]:
                        dva[:, kat] = dv_part
                        first_v[j] = False
                    else:
                        dva[:, kat] += dv_part
                    dq_part = lax.dot_general(ds, kt, NN, preferred_element_type=F32)
                    dq = dq_part if dq is None else dq + dq_part
                if wide_qk:
                    out_refs[0][at, a * LANES:(a + 1) * LANES] = dq.astype(out_refs[0].dtype)
                else:
                    dq = dq * masks[a] if sub > 1 else dq
                    dq_all = dq if dq_all is None else dq_all + dq
            if not wide_qk:
                out_refs[0][at, 0:LANES] = dq_all.astype(out_refs[0].dtype)
        if packed:
            out_refs[0][:, LANES:2 * LANES] = dkas[0][...].T.astype(out_refs[0].dtype)
            out_refs[0][:, 2 * LANES:3 * LANES] = dva[...].T.astype(out_refs[0].dtype)
            out_refs[0][:, 3 * LANES:] = jnp.zeros((seq, LANES), out_refs[0].dtype)
        else:
            for a in range(n_k):
                out_refs[1][:, a * LANES:(a + 1) * LANES] = dkas[a][...].T.astype(out_refs[1].dtype)
            out_refs[2][...] = dva[...].T.astype(out_refs[2].dtype)

    slab = lambda b0, step, width=LANES: pl.BlockSpec((seq, width), lambda b, g: (b, b0 + step * g))
    if packed:
        out_specs = [slab(0, 1, 4 * LANES)]
        out_shape = [jax.ShapeDtypeStruct((rows, groups * 4 * LANES), out_dtype)]
    else:
        out_specs = [slab(0, 1, qk_w), slab(0, 1, qk_w), slab(0, 1)]
        out_shape = [jax.ShapeDtypeStruct((rows, N_HEADS * LANES), out_dtype)] * 2 + [jax.ShapeDtypeStruct((rows, groups * LANES), out_dtype)]
    res = pl.pallas_call(
        body, name=name,
        grid=(batch, groups),
        in_specs=[slab(qb0, stride, qk_w), slab(kb0, stride, qk_w), slab(vb0, stride), slab(0, 1), slab(0, 1), slab(0, 1, sub * LANES)]
        + _table_specs(tables, sub) + [pl.BlockSpec(memory_space=pl.ANY)] * (after is not None),
        out_specs=out_specs, out_shape=out_shape,
        scratch_shapes=[pltpu.VMEM((seq, LANES), BF16), pltpu.VMEM((LANES, seq), F32)]
        + [pltpu.VMEM((seq, LANES), BF16)] * n_k + [pltpu.VMEM((LANES, seq), F32)] * n_k
        + [pltpu.VMEM((seq, LANES), BF16)] * (2 * sub) + [pltpu.VMEM((LANES, seq), BF16)] * (2 * sub),
        compiler_params=_params(("arbitrary", "arbitrary")),
    )(q, k, v, o, do, lse, *tables, *([after] if after is not None else []))
    return res[0] if packed else res


def _attention_tables(seq):
    n = seq // ATT_T
    pos = np.arange(ATT_T, dtype=np.int32)
    dist = np.arange(n, dtype=np.int32)[:, None, None] * ATT_T + pos[None, :, None] - pos[None, None, :]
    causal = np.where(dist[:1] >= 0, 0.0, NEG).astype(np.float32)
    count = np.zeros(dist.shape, np.float32)
    for window, dilation in DIL_PATTERNS:
        count += ((dist >= 0) & (dist <= window) & (dist % dilation == 0)).astype(np.float32)
    held = np.where(count > 0, np.log2(np.maximum(count, 1.0)), NEG).astype(np.float32)
    far = dist.astype(np.float32) * np.float32(LOG2E)
    slopes = np.asarray([2.0 ** (-8.0 * (i + 1) / N_HEADS) for i in range(N_HEADS)], np.float32)
    slopes = jnp.asarray(np.broadcast_to(slopes[:, None, None], (N_HEADS, 1, LANES)))
    flat = lambda a: jnp.asarray(np.ascontiguousarray(a))
    turned = lambda a: flat(np.swapaxes(a, 1, 2))
    return ((flat(causal),), (turned(causal),)), ((flat(held), flat(far), slopes), (turned(held), turned(far), slopes))


def _pad_heads(w, width):
    kdim, n = w.shape[0], w.shape[1] // width
    return jnp.pad(w.reshape(kdim, n, width), ((0, 0), (0, 0), (0, LANES - width))).reshape(kdim, n * LANES)


def _unpad_heads(w, width):
    kdim, n = w.shape[0], w.shape[1] // LANES
    return w.reshape(kdim, n, LANES)[:, :, :width].reshape(kdim, n * width)


def _pad_head_rows(w, width):
    n, kdim = w.shape[0] // width, w.shape[1]
    return jnp.pad(w.reshape(n, width, kdim), ((0, 0), (0, LANES - width), (0, 0))).reshape(n * LANES, kdim)


def _unpad_head_rows(w, width):
    n, kdim = w.shape[0] // LANES, w.shape[1]
    return w.reshape(n, LANES, kdim)[:, :width].reshape(n * width, kdim)


def _pad_w_in_t(wt):
    n_qkv, pair = 3 * N_HEADS * DIL_DIM, 2 * DIL_DIM
    zeros = lambda n: jnp.zeros((n, wt.shape[1]), wt.dtype)
    main = jnp.concatenate([wt[:LORA_W], zeros(KR_LANE), wt[LORA_W:LORA_W + ROPE], zeros(P_GATE - P_KR - KR_LANE - ROPE),
                            wt[LORA_W + ROPE + n_qkv:]], axis=0)
    qkv = wt[LORA_W + ROPE:LORA_W + ROPE + n_qkv].reshape(3, N_HEADS // 2, pair, wt.shape[1]).transpose(1, 0, 2, 3)
    dil = jnp.pad(qkv, ((0, 0), (0, 1), (0, 0), (0, 0))).reshape(P_DIL, wt.shape[1])
    return main, dil


def _unpad_w_in_t(gt):
    n_qkv, pair = 3 * N_HEADS * DIL_DIM, 2 * DIL_DIM
    moves = [(P_LORA, 0, LORA_W), (P_KR + KR_LANE, LORA_W, ROPE)]
    moves += [(P_HALF + (g * 4 + j) * pair, LORA_W + ROPE + (j * (N_HEADS // 2) + g) * pair, pair) for j in range(3) for g in range(N_HEADS // 2)]
    moves += [(P_GATE + r, LORA_W + ROPE + n_qkv + r, ROW_T) for r in range(0, P_HALF - P_GATE, ROW_T)]
    n = len(moves)

    def body(src, out, buf, sem_in, sem_out):
        ins = [pltpu.make_async_copy(src.at[pl.ds(a, k)], buf.at[pl.ds(b, k)], sem_in.at[i]) for i, (a, b, k) in enumerate(moves)]
        outs = [pltpu.make_async_copy(buf.at[pl.ds(b, k)], out.at[pl.ds(b, k)], sem_out.at[i]) for i, (a, b, k) in enumerate(moves)]
        for cp in ins:
            cp.start()
        for cp_in, cp_out in zip(ins, outs):
            cp_in.wait()
            cp_out.start()
        for cp in outs:
            cp.wait()

    return pl.pallas_call(
        body, name="unpad_d_w_in", in_specs=[pl.BlockSpec(memory_space=pl.ANY)], out_specs=pl.BlockSpec(memory_space=pl.ANY),
        out_shape=jax.ShapeDtypeStruct((IN_WIDTH, gt.shape[1]), gt.dtype),
        scratch_shapes=[pltpu.VMEM((IN_WIDTH, gt.shape[1]), gt.dtype), pltpu.SemaphoreType.DMA((n,)), pltpu.SemaphoreType.DMA((n,))],
        compiler_params=_params(),
    )(gt)


def _split_ukv(w):
    w3 = w.reshape(w.shape[0], N_HEADS, NOPE + V_DIM)
    return _pad_heads(w3[:, :, :NOPE].reshape(w.shape[0], -1), NOPE), w3[:, :, NOPE:].reshape(w.shape[0], -1)


def _merge_ukv(g_k, g_v):
    kdim = g_k.shape[0]
    k3 = _unpad_heads(g_k, NOPE).reshape(kdim, N_HEADS, NOPE)
    return jnp.concatenate([k3, g_v.reshape(kdim, N_HEADS, V_DIM)], axis=2).reshape(kdim, N_HEADS * (NOPE + V_DIM))


def _join_cols(w):
    return w.transpose(1, 0, 2).reshape(w.shape[1], N_CHIPS * w.shape[2])


def _split_cols(g):
    return g.reshape(g.shape[0], N_CHIPS, g.shape[1] // N_CHIPS).transpose(1, 0, 2)


def _local_step(x3, target3, wg, b_gate, g_q_a, g_kv_a, ln1_g, ln1_b, ln2_g, ln2_b, token=None, late_arrived=None, late_weights=None,
                early_grads=None, early_grads_go=None, first_grad=None, last_grads=None, tables=None, rope=None):
    w_main_t, w_dil_t = _pad_w_in_t(wg["w_in"].reshape(IN_WIDTH, D_MODEL))
    w_uq_pt = _pad_head_rows(wg["w_uq"].reshape(N_HEADS * MLA_QK, Q_LORA), MLA_QK)
    w_ukv = _join_cols(wg["w_ukv"])
    batch, seq, _ = x3.shape
    rows = batch * seq
    x = x3.reshape(rows, D_MODEL)
    target = target3.reshape(rows, D_MODEL)
    row = functools.partial(_rowwise, rows=rows, seq=seq)
    mm = _matmul

    w_uk_p, w_uv = _split_ukv(w_ukv)
    b0, b1 = b_gate[0:1], b_gate[1:2]
    rope_c, rope_up, rope_dn, rope_c_only = _rope_tables(jnp.arange(seq, dtype=F32)) if rope is None else rope
    (mla_bwd_tables, mla_fwd_tables), (dil_bwd_tables, dil_fwd_tables) = _attention_tables(seq) if tables is None else tables
    scale_mla, scale_dil = MLA_QK ** -0.5, DIL_DIM ** -0.5
    lora0, kr0, gate0 = P_LORA // LORA_W, P_KR // LANES, P_GATE // D_MODEL

    proj = mm(x, w_main_t, mode="nt", name="proj", tm=1024, tn=1536, tk=1024, after=token)
    proj_d = mm(x, w_dil_t, mode="nt", name="proj_dil", tm=1024, tn=1024, tk=1024, out_dtypes=(BF16,))

    def prep(lora, gq, gkv):
        return _rms(lora[:, :Q_LORA], gq), _rms(lora[:, Q_LORA:], gkv)

    qn, kvn = row(prep, name="mla_rms", ins=[(proj, LORA_W, lora0, "row"), (g_q_a, 0, 0, "full"), (g_kv_a, 0, 0, "full")],
                  outs=[(Q_LORA, BF16), (KV_LORA, BF16)])
    q_lin = mm(qn, w_uq_pt, mode="nt", name="q_up", tm=1024, tn=1024, tk=Q_LORA)
    k_lin = mm(kvn, w_uk_p, mode="nn", name="k_up", tm=1024, tn=1024, tk=KV_LORA)
    v_a = mm(kvn, w_uv, mode="nn", name="v_up", tm=1024, tn=1024, tk=KV_LORA, out_dtypes=(BF16,))

    def rope_qk(ql, kl, kr, c, up, dn):
        k_rot = _rope_fwd(kr, c, up, dn)
        qs = [_rope_fwd(ql[:, h * LANES:(h + 1) * LANES], c, up, dn) for h in range(N_HEADS)]
        ks = [kl[:, h * LANES:(h + 1) * LANES] + k_rot for h in range(N_HEADS)]
        return jnp.concatenate(qs, axis=1), jnp.concatenate(ks, axis=1)

    pos = lambda tab: (tab, LANES, 0, "pos")
    q_a, k_a = row(rope_qk, name="rope_qk",
                   ins=[(q_lin, D_MODEL, 0, "row"), (k_lin, D_MODEL, 0, "row"), (proj, LANES, kr0, "row"), pos(rope_c), pos(rope_up), pos(rope_dn)],
                   outs=[(N_HEADS * LANES, BF16), (N_HEADS * LANES, BF16)])
    o_a, lse_a = _attn_fwd(q_a, 0, k_a, 0, v_a, 0, mla_fwd_tables, scale_mla, name="mla_fwd", batch=batch, seq=seq, sub=2, wide_qk=True)
    arrived = None if late_arrived is None else late_arrived(o_a)
    o_b, lse_b = _attn_fwd(proj_d, 0, proj_d, 1, proj_d, 2, dil_fwd_tables, scale_dil, name="dil_fwd", batch=batch, seq=seq, sub=2, stride=4, after=arrived)
    late = wg if late_weights is None else late_weights(o_b)
    w_oa = _join_cols(late["w_o_mla"])
    w_ob = _join_cols(late["w_o_dil"])
    w_out, w_ff1, w_ff2 = late["w_out"].reshape(D_MODEL, D_MODEL), late["w_ff1"], late["w_ff2"].reshape(D_FF, D_MODEL)
    y_a = mm(o_a, w_oa, mode="nn", name="o_mla", tm=1024, tn=1024, tk=1024, out_dtypes=(BF16,))
    y_b = mm(o_b, w_ob, mode="nn", name="o_dil", tm=1024, tn=1024, tk=1024, out_dtypes=(BF16,))

    def gate(t0, t1, c0, c1, ya, yb):
        return (jax.nn.sigmoid(t0 + c0) * ya + jax.nn.sigmoid(t1 + c1) * yb,)

    gate_ins = [(proj, D_MODEL, gate0, "row"), (proj, D_MODEL, gate0 + 1, "row"), (b0, 0, 0, "full"), (b1, 0, 0, "full")]
    (u,) = row(gate, name="gate", ins=gate_ins + [(y_a, D_MODEL, 0, "row"), (y_b, D_MODEL, 0, "row")], outs=[(D_MODEL, BF16)])
    mixed = mm(u, w_out, mode="nn", name="mix", tm=1024, tn=1024, tk=1024)

    def ln1(xv, mv, g, b):
        r1 = ALPHA * xv + mv
        xh, _ = _ln_stats(r1)
        hv = xh * g + b
        return r1, hv, hv

    r1, h, h_b = row(ln1, name="ln1", ins=[(x, D_MODEL, 0, "row"), (mixed, D_MODEL, 0, "row"), (ln1_g, 0, 0, "full"), (ln1_b, 0, 0, "full")],
                outs=[(D_MODEL, F32), (D_MODEL, F32), (D_MODEL, BF16)])

    def relu2(acc):
        r = jnp.maximum(acc, 0.0)
        return (r * r,)

    z = mm(h_b, w_ff1, mode="nn", name="ff1", tm=1024, tn=1024, tk=1024, out_dtypes=(BF16,), epilogue=relu2, b_shards=True)
    f = mm(z, w_ff2, mode="nn", name="ff2", tm=1024, tn=1024, tk=2048)

    def ln2_loss(hv, fv, tv, g, b):
        xh, r = _ln_stats(ALPHA * hv + fv)
        err = xh * g + b - tv
        dy = err * (1.0 / D_MODEL)
        dr2, dg, db = _ln_bwd(xh, r, g, dy)
        loss = jnp.sum(_colsum(err * err), axis=1, keepdims=True) * (0.5 / D_MODEL)
        return dr2, dr2, jnp.broadcast_to(loss, (1, LANES)), dg, db

    dr2, dr2_b, loss_l, d_ln2_g, d_ln2_b = row(
        ln2_loss, name="ln2_loss",
        ins=[(h, D_MODEL, 0, "row"), (f, D_MODEL, 0, "row"), (target, D_MODEL, 0, "row"), (ln2_g, 0, 0, "full"), (ln2_b, 0, 0, "full")],
        outs=[(D_MODEL, F32), (D_MODEL, BF16)], sums=[LANES, D_MODEL, D_MODEL])

    d_w_ff2 = mm(z, dr2_b, mode="tn", name="d_w_ff2", tm=1024, tn=1024, tk=2048)
    da = mm(dr2_b, w_ff2, mode="nt", name="d_ff_act", tm=1024, tn=1024, tk=1024, out_dtypes=(BF16,), extras=(z,),
            epilogue=lambda acc, zv: (acc * (2.0 * jnp.sqrt(zv.astype(F32))),))
    d_w_ff1 = mm(h_b, da, mode="tn", name="d_w_ff1", tm=1024, tn=1024, tk=2048, out_shards=True)
    dh = mm(da, w_ff1, mode="nt", name="d_h", tm=1024, tn=1024, tk=1024, extras=(dr2,), epilogue=lambda acc, rv: (acc + ALPHA * rv,), b_shards=True)

    def ln1_bwd(dhv, r1v, g):
        xh, r = _ln_stats(r1v)
        return _ln_bwd(xh, r, g, dhv)

    dr1, d_ln1_g, d_ln1_b = row(ln1_bwd, name="ln1_bwd", ins=[(dh, D_MODEL, 0, "row"), (r1, D_MODEL, 0, "row"), (ln1_g, 0, 0, "full")],
                                outs=[(D_MODEL, F32)], sums=[D_MODEL, D_MODEL])
    d_w_out = mm(u, dr1, mode="tn", name="d_w_out", tm=1024, tn=1024, tk=1024)
    du = mm(dr1, w_out, mode="nt", name="d_u", tm=1024, tn=1024, tk=1024, out_dtypes=(BF16,))

    def gate_bwd(t0, t1, c0, c1, ya, yb, duv):
        s0, s1 = jax.nn.sigmoid(t0 + c0), jax.nn.sigmoid(t1 + c1)
        dt0 = duv * ya * s0 * (1.0 - s0)
        dt1 = duv * yb * s1 * (1.0 - s1)
        return duv * s0, duv * s1, jnp.concatenate([dt0, dt1], axis=1), jnp.concatenate([_colsum(dt0), _colsum(dt1)], axis=1)

    dy_a, dy_b, d_gates, d_b_gate = row(
        gate_bwd, name="gate_bwd", ins=gate_ins + [(y_a, D_MODEL, 0, "row"), (y_b, D_MODEL, 0, "row"), (du, D_MODEL, 0, "row")],
        outs=[(D_MODEL, BF16), (D_MODEL, BF16), (2 * D_MODEL, BF16)], sums=[2 * D_MODEL])
    d_w_oa = mm(o_a, dy_a, mode="tn", name="d_w_o_mla", tm=1024, tn=1024, tk=1024)
    d_w_ob = mm(o_b, dy_b, mode="tn", name="d_w_o_dil", tm=1024, tn=1024, tk=1024)
    grads = dict(w_o_mla=_split_cols(d_w_oa), w_o_dil=_split_cols(d_w_ob),
                 w_out=d_w_out.reshape(N_CHIPS, D_MODEL // N_CHIPS, D_MODEL), w_ff1=d_w_ff1, w_ff2=d_w_ff2.reshape(N_CHIPS, D_FF // N_CHIPS, D_MODEL))
    sent = None if early_grads is None else early_grads(grads)
    do_a = mm(dy_a, w_oa, mode="nt", name="d_o_mla", tm=1024, tn=1024, tk=1024, after=sent)
    do_b = mm(dy_b, w_ob, mode="nt", name="d_o_dil", tm=1024, tn=1024, tk=1024)
    dq_a, dk_a, dv_a = _attn_bwd(q_a, 0, k_a, 0, v_a, 0, o_a, do_a, lse_a, mla_bwd_tables, scale_mla,
                                 name="mla_bwd", batch=batch, seq=seq, out_dtype=F32, sub=2, wide_qk=True)
    going = None if early_grads_go is None else early_grads_go(dq_a)
    d_qkv_d = _attn_bwd(proj_d, 0, proj_d, 1, proj_d, 2, o_b, do_b, lse_b, dil_bwd_tables, scale_dil,
                        name="dil_bwd", batch=batch, seq=seq, out_dtype=BF16, sub=2, stride=4, after=going)

    def mla_post(dq, dk, c, up, dn, c_only):
        dqs = [_rope_bwd(dq[:, h * LANES:(h + 1) * LANES], c, up, dn) for h in range(N_HEADS)]
        dk_sum = dk[:, :LANES]
        for h in range(1, N_HEADS):
            dk_sum = dk_sum + dk[:, h * LANES:(h + 1) * LANES]
        return jnp.concatenate(dqs, axis=1), _rope_bwd(dk_sum, c_only, up, dn)

    dq_lin, d_kr = row(mla_post, name="mla_unrope",
                       ins=[(dq_a, D_MODEL, 0, "row"), (dk_a, D_MODEL, 0, "row"), pos(rope_c), pos(rope_up), pos(rope_dn), pos(rope_c_only)],
                       outs=[(N_HEADS * LANES, BF16), (LANES, BF16)])
    d_qn = mm(dq_lin, w_uq_pt, mode="nn", name="d_qn", tm=1024, tn=Q_LORA, tk=1024)
    d_kvn_k = mm(dk_a, w_uk_p, mode="nt", name="d_kvn_k", tm=1024, tn=KV_LORA, tk=1024)
    d_kvn = mm(dv_a, w_uv, mode="nt", name="d_kvn", tm=1024, tn=KV_LORA, tk=1024, extras=(d_kvn_k,), epilogue=lambda acc, e: (acc + e,))

    def rms_bwd(lora, dq, dkv, dkr, gq, gkv):
        dxq, dgq = _rms_bwd(lora[:, :Q_LORA], gq, dq)
        dxk, dgk = _rms_bwd(lora[:, Q_LORA:], gkv, dkv)
        tail = jnp.zeros((dxq.shape[0], P_GATE - P_KR - LANES), F32)
        return jnp.concatenate([dxq, dxk, dkr.astype(F32), tail], axis=1), dgq, dgk

    d_tail, d_g_q_a, d_g_kv_a = row(
        rms_bwd, name="mla_rms_bwd",
        ins=[(proj, LORA_W, lora0, "row"), (d_qn, Q_LORA, 0, "row"), (d_kvn, KV_LORA, 0, "row"), (d_kr, LANES, 0, "row"),
             (g_q_a, 0, 0, "full"), (g_kv_a, 0, 0, "full")],
        outs=[(P_GATE, BF16)], sums=[Q_LORA, KV_LORA])
    d_proj = [d_tail, d_gates, d_qkv_d]
    d_w_in_pt = mm(d_proj, x, mode="tn", name="d_w_in", tm=1024, tn=1024, tk=1024)
    d_w_in = _unpad_w_in_t(d_w_in_pt).reshape(N_CHIPS, IN_WIDTH // N_CHIPS, D_MODEL)
    moving = None if first_grad is None else first_grad(d_w_in)
    d_w_uq_pt = mm(dq_lin, qn, mode="tn", name="d_w_uq", tm=1024, tn=Q_LORA, tk=1024, after=moving)
    d_w_uk_p = mm(kvn, dk_a, mode="tn", name="d_w_uk", tm=KV_LORA, tn=1024, tk=1024, after=moving)
    d_w_uv = mm(kvn, dv_a, mode="tn", name="d_w_uv", tm=KV_LORA, tn=1024, tk=1024, after=moving)
    grads.update(w_in=d_w_in,
                 w_uq=_unpad_head_rows(d_w_uq_pt, MLA_QK).reshape(N_CHIPS, N_HEADS * MLA_QK // N_CHIPS, Q_LORA),
                 w_ukv=_split_cols(_merge_ukv(d_w_uk_p, d_w_uv)))
    leaving = None if last_grads is None else last_grads(grads)
    grad_x = mm(d_proj, jnp.concatenate([w_main_t, w_dil_t], axis=0), mode="nn", name="d_x", tm=1024, tn=1024, tk=1024, extras=(dr1,), epilogue=lambda acc, rv: (acc + ALPHA * rv,),
                after=leaving)

    grads.update(
        b_gate=d_b_gate.reshape(2, D_MODEL), g_q_a=d_g_q_a, g_kv_a=d_g_kv_a, ln1_g=d_ln1_g, ln1_b=d_ln1_b, ln2_g=d_ln2_g, ln2_b=d_ln2_b)
    return loss_l, grad_x.reshape(batch, seq, D_MODEL), grads


BIG = ("w_in", "w_uq", "w_ukv", "w_o_mla", "w_o_dil", "w_out", "w_ff1", "w_ff2")
SMALL = (("b_gate", 2 * D_MODEL), ("g_q_a", Q_LORA), ("g_kv_a", KV_LORA), ("ln1_g", D_MODEL), ("ln1_b", D_MODEL),
         ("ln2_g", D_MODEL), ("ln2_b", D_MODEL))
TRANSPOSED = ("w_in", "w_uq")
D2D_PIECES = (4, 2, 1)
ANY = pl.BlockSpec(memory_space=pl.ANY)
SIDE_EFFECTS = pltpu.CompilerParams(has_side_effects=True)


def _place():
    x, y, c = lax.axis_index("x"), lax.axis_index("y"), lax.axis_index("c")
    return x, y, c, ((1 - x, y), (x, 1 - y), (1 - x, 1 - y))


def _half_axis(shape):
    return 0 if shape[0] % 32 == 0 else 1


def _half_shape(shape):
    return (shape[0] // 2, shape[1]) if _half_axis(shape) == 0 else (shape[0], shape[1] // 2)


def _window(ref, lead, shape, which=None, pieces=False):
    axis = _half_axis(shape)
    size = shape[axis] if which is None else shape[axis] // 2
    base = 0 if which is None else which * size
    tile = (16, LANES)[axis]
    count = next(c for c in D2D_PIECES if size % (tile * c) == 0) if pieces else 1
    step = size // count
    spans = [pl.ds(pl.multiple_of(base + i * step, tile), step) for i in range(count)]
    refs = [ref.at[(*lead, s)] if axis == 0 else ref.at[(*lead, slice(None), s)] for s in spans]
    return refs if pieces else refs[0]


def _remote(src, dst, send, recv, to):
    return pltpu.make_async_remote_copy(src_ref=src, dst_ref=dst, send_sem=send, recv_sem=recv, device_id=to, device_id_type=MESH)


def _pair_split(grads, name):
    n = len(grads)

    def body(*refs):
        srcs, outs, (send, recv) = refs[:n], refs[n:2 * n], refs[2 * n:]
        x, y, c, _ = _place()
        for t in range(n):
            for s in range(N_CHIPS):
                _remote(_window(srcs[t], (s,), grads[t].shape[1:], 1 - c), outs[t].at[s], send.at[t], recv.at[t], (x, y, 1 - c)).start()
        for t in range(n):
            _remote(_window(srcs[t], (slice(None),), grads[t].shape[1:], 1 - c), outs[t], send.at[t], recv.at[t], (x, y, 1 - c)).wait()

    return pl.pallas_call(
        body, name=name, in_specs=[ANY] * n, out_specs=[ANY] * n,
        out_shape=[jax.ShapeDtypeStruct((N_CHIPS,) + _half_shape(g.shape[1:]), g.dtype) for g in grads],
        scratch_shapes=[pltpu.SemaphoreType.DMA((n,)), pltpu.SemaphoreType.DMA((n,))],
        compiler_params=SIDE_EFFECTS,
    )(*grads)


HBM = pl.BlockSpec(memory_space=pltpu.HBM)
SEM = pl.BlockSpec(memory_space=pltpu.SEMAPHORE)
SPLIT = pltpu.CompilerParams(has_side_effects=pltpu.SideEffectType.DATAFLOW_SIDE_EFFECTING)


def _in_hbm(a):
    return pltpu.with_memory_space_constraint(a, pltpu.HBM)


def _split_copies(kind, srcs, lands):
    x, y, c, chips = _place()
    out = []
    for t in range(len(srcs)):
        if kind == "pair":
            out += [(t, s % 3, _window(srcs[t], (s,), srcs[t].shape[1:], 1 - c), lands[t].at[s], (x, y, 1 - c)) for s in range(N_CHIPS)]
            continue
        if kind == "join":
            out += [(t, 0, a, b, (x, y, 1 - c)) for a, b in zip(_window(srcs[t], (), srcs[t].shape, None, True), _window(lands[t], (), srcs[t].shape, None, True))]
            continue
        if kind == "forward":
            shape, sibling = srcs[t].shape, (x, y, 1 - c)
            out += [(t, 0, a, b, sibling) for a, b in zip(_window(srcs[t], (), shape, None, True), _window(lands[t], (2 * x + y,), shape, None, True))]
            out += [(t, j, a, a, sibling) for j, (cx, cy) in enumerate(chips) for a in _window(lands[t], (2 * cx + cy,), shape, c, True)]
            continue
        for j, (cx, cy) in enumerate(chips):
            if kind == "gather":
                shape = srcs[t].shape
                out.append((t, j, _window(srcs[t], (), shape, c), _window(lands[t], (2 * x + y,), shape, c), (cx, cy, c)))
            else:
                out.append((t, j, srcs[t].at[2 * cx + cy], lands[t].at[j], (cx, cy, c)))
    return out


def _split_start(kind, srcs, land_shapes, name, lands=None, after=None):
    n = len(srcs)

    def body(*refs):
        src_refs, land_refs, sems, token = refs[:n], refs[n:2 * n], refs[-7 - 2 * n:-1 - 2 * n], refs[-1]
        for t, j, s, d, to in _split_copies(kind, src_refs, land_refs):
            _remote(s, d, sems[j], sems[3 + j], to).start()
        token[...] = jnp.zeros_like(token)

    lands = [_in_hbm(lax.empty(s.shape, s.dtype)) for s in land_shapes] if lands is None else list(lands)
    thru = [pltpu.HBM(a.shape, a.dtype) for a in list(srcs) + lands]
    res = pl.pallas_call(
        body, name=name,
        out_shape=(*[pltpu.SemaphoreType.DMA(())] * 6, *thru, jax.ShapeDtypeStruct((8, LANES), F32)),
        in_specs=[HBM] * (2 * n) + [ANY] * (after is not None), out_specs=(*[SEM] * 6, *[HBM] * (2 * n), pl.BlockSpec(memory_space=pltpu.VMEM)),
        input_output_aliases={i: 6 + i for i in range(2 * n)}, compiler_params=SPLIT,
    )(*[_in_hbm(s) for s in srcs], *lands, *([after] if after is not None else []))
    return res[:6], res[6:6 + n], res[6 + n:6 + 2 * n], res[-1]


def _split_wait(kind, sems, srcs, lands, after, name):
    n = len(srcs)
    afters = list(after) if isinstance(after, (list, tuple)) else [after]

    def body(*refs):
        src_refs, land_refs, sem_refs = refs[:n], refs[n:2 * n], refs[2 * n:2 * n + 6]
        for t, j, s, d, to in _split_copies(kind, src_refs, land_refs):
            cp = _remote(s, d, sem_refs[j], sem_refs[3 + j], to)
            cp.wait_send()
            cp.wait_recv()

    res = pl.pallas_call(
        body, name=name, out_shape=[pltpu.HBM(a.shape, a.dtype) for a in list(srcs) + list(lands)],
        in_specs=[HBM] * (2 * n) + [SEM] * 6 + [ANY] * len(afters), out_specs=[HBM] * (2 * n),
        input_output_aliases={i: i for i in range(2 * n)}, compiler_params=SPLIT,
    )(*srcs, *lands, *sems, *afters)
    return res[:n], res[n:]


def _sum_all_devices(vec, name, after):
    n_rows = vec.shape[0]

    def body(v_ref, after_ref, out_ref, buf, send, recv):
        x, y, c, _ = _place()
        me = 4 * x + 2 * y + c
        buf[me] = v_ref[...]
        flips = [(a, b, d) for a in (0, 1) for b in (0, 1) for d in (0, 1)][1:]
        copies = []
        for r, (a, b, d) in enumerate(flips):
            px, py, pc = (1 - x if a else x), (1 - y if b else y), (1 - c if d else c)
            copies.append(pltpu.make_async_remote_copy(src_ref=v_ref, dst_ref=buf.at[me], send_sem=send.at[r], recv_sem=recv.at[r],
                                                       device_id=(px, py, pc), device_id_type=MESH))
            copies[-1].start()
        for r, (a, b, d) in enumerate(flips):
            px, py, pc = (1 - x if a else x), (1 - y if b else y), (1 - c if d else c)
            pltpu.make_async_remote_copy(src_ref=v_ref, dst_ref=buf.at[4 * px + 2 * py + pc], send_sem=send.at[r], recv_sem=recv.at[r],
                                         device_id=(px, py, pc), device_id_type=MESH).wait_recv()
        for cp in copies:
            cp.wait_send()
        total = buf[0]
        for k in range(1, N_DEV):
            total = total + buf[k]
        out_ref[...] = total

    vmem = pl.BlockSpec(memory_space=pltpu.VMEM)
    return pl.pallas_call(
        body, name=name, in_specs=[vmem, ANY], out_specs=vmem, out_shape=jax.ShapeDtypeStruct(vec.shape, F32),
        scratch_shapes=[pltpu.VMEM((N_DEV, n_rows, LANES), F32), pltpu.SemaphoreType.DMA((N_DEV - 1,)), pltpu.SemaphoreType.DMA((N_DEV - 1,))],
        compiler_params=pltpu.CompilerParams(has_side_effects=True),
    )(vec, after)


def _half_tile(half, width):
    t = half
    while t * width * 4 > (2 << 20) and t % 32 == 0:
        t //= 2
    return t


def _pair_add(g, theirs, core, name):
    _, half, width = theirs.shape
    t = _half_tile(half, width)
    n = half // t

    def body(c_ref, a_ref, b_ref, o_ref):
        o_ref[...] = (a_ref[...] + b_ref[...]).astype(BF16)

    tile = pl.BlockSpec((1, t, width), lambda j, i, c_ref: (j, i, 0))
    if _half_axis(g.shape[1:]) == 0:
        mine = pl.BlockSpec((1, t, width), lambda j, i, c_ref: (j, c_ref[0] * n + i, 0))
    else:
        mine = pl.BlockSpec((1, t, width), lambda j, i, c_ref: (j, i, c_ref[0]))
    return pl.pallas_call(
        body, name=name,
        grid_spec=pltpu.PrefetchScalarGridSpec(num_scalar_prefetch=1, grid=(N_CHIPS, n), in_specs=[mine, tile], out_specs=tile),
        out_shape=jax.ShapeDtypeStruct(theirs.shape, BF16), compiler_params=_params(("parallel", "parallel")),
    )(core, g, theirs)


def _chip_sum(part, others, chip, name, after=None):
    _, half, width = part.shape
    t = _half_tile(half, width)

    def body(s_ref, mine, p0, p1, p2, *rest):
        o_ref = rest[-1]
        o_ref[...] = ((mine[0].astype(F32) + p0[0].astype(F32)) + p1[0].astype(F32)) + p2[0].astype(F32)

    return pl.pallas_call(
        body, name=name,
        grid_spec=pltpu.PrefetchScalarGridSpec(
            num_scalar_prefetch=1, grid=(half // t,),
            in_specs=[pl.BlockSpec((1, t, width), lambda i, s_ref: (s_ref[0], i, 0))]
            + [pl.BlockSpec((1, t, width), lambda i, s_ref, j=j: (j, i, 0)) for j in range(3)] + [pl.BlockSpec(memory_space=pl.ANY)] * (after is not None),
            out_specs=pl.BlockSpec((t, width), lambda i, s_ref: (i, 0))),
        out_shape=jax.ShapeDtypeStruct((half, width), F32), compiler_params=_params(("parallel",)),
    )(chip, part, others, others, others, *([after] if after is not None else []))


EARLY = ("w_in", "w_uq", "w_ukv")
LATE = ("w_o_mla", "w_o_dil", "w_out", "w_ff1", "w_ff2")


def _sum_small(vals):
    n_in = len(vals)
    n_rows = sum(a.shape[0] * a.shape[1] // LANES for a in vals)
    pad_rows = -(-n_rows // 8) * 8

    def chunks(refs):
        return [(ref, a, j) for ref in refs for a in range(ref.shape[0]) for j in range(ref.shape[1] // LANES)]

    def body(*refs):
        ins, outs, (buf, send, recv) = refs[:n_in], refs[n_in:2 * n_in], refs[2 * n_in:]
        x, y, c, _ = _place()
        me = 4 * x + 2 * y + c
        for r, (ref, a, j) in enumerate(chunks(ins)):
            buf[me, r:r + 1, :] = ref[a:a + 1, j * LANES:(j + 1) * LANES]
        if pad_rows > n_rows:
            buf[me, n_rows:pad_rows, :] = jnp.zeros((pad_rows - n_rows, LANES), F32)
        flips = [(a, b, d) for a in (0, 1) for b in (0, 1) for d in (0, 1)][1:]
        peers = [((1 - x if a else x), (1 - y if b else y), (1 - c if d else c)) for a, b, d in flips]
        copies = [_remote(buf.at[me], buf.at[me], send.at[r], recv.at[r], peer) for r, peer in enumerate(peers)]
        for cp in copies:
            cp.start()
        for r, (px, py, pc) in enumerate(peers):
            _remote(buf.at[me], buf.at[4 * px + 2 * py + pc], send.at[r], recv.at[r], (px, py, pc)).wait_recv()
        for cp in copies:
            cp.wait_send()
        total = buf[0]
        for k in range(1, N_DEV):
            total = total + buf[k]
        for r, (ref, a, j) in enumerate(chunks(outs)):
            ref[a:a + 1, j * LANES:(j + 1) * LANES] = total[r:r + 1, :]

    vmem = pl.BlockSpec(memory_space=pltpu.VMEM)
    return pl.pallas_call(
        body, name="sum_small", in_specs=[vmem] * n_in, out_specs=[vmem] * n_in,
        out_shape=[jax.ShapeDtypeStruct(a.shape, F32) for a in vals],
        scratch_shapes=[pltpu.VMEM((N_DEV, pad_rows, LANES), F32), pltpu.SemaphoreType.DMA((N_DEV - 1,)), pltpu.SemaphoreType.DMA((N_DEV - 1,))],
        compiler_params=SIDE_EFFECTS,
    )(*vals)


def _adam_math(w, g, m, v):
    nm = B1 * m + (1.0 - B1) * g
    nv = B2 * v + (1.0 - B2) * (g * g)
    m_hat = nm / (1.0 - B1 ** ADAM_STEP)
    v_hat = nv / (1.0 - B2 ** ADAM_STEP)
    return -LR * (m_hat / (jnp.sqrt(v_hat) + ADAM_EPS) + WD * w), nm, nv


def _adamw_big(w, mine, theirs, m, v, core, name, side_by_side=False):
    rows, width = w.shape
    if side_by_side:
        t = next(c for c in (152, 96, 64, 32, 16, 8) if rows % c == 0)
        hb = None
        half_spec = pl.BlockSpec((t, width // 2), lambda i, c_ref: (i, 0))
    else:
        t = next(c for c in (256, 128, 64, 32, 16, 8) if (rows // 2) % c == 0)
        hb = rows // 2 // t
        half_spec = pl.BlockSpec((t, width), lambda i, c_ref: (i % hb, 0))

    def body(c_ref, w_ref, a_ref, b_ref, m_ref, v_ref, g_ref, d_ref, nm_ref, nv_ref):
        south = c_ref[0] == 0
        if side_by_side:
            g = jnp.where(south, jnp.concatenate([a_ref[...], b_ref[...]], axis=1), jnp.concatenate([b_ref[...], a_ref[...]], axis=1))
        else:
            g = jnp.where((pl.program_id(0) < hb) == south, a_ref[...], b_ref[...])
        g_ref[...] = g
        d_ref[...], nm_ref[...], nv_ref[...] = _adam_math(w_ref[...], g, m_ref[...], v_ref[...])

    spec = pl.BlockSpec((t, width), lambda i, c_ref: (i, 0))
    return pl.pallas_call(
        body, name=name,
        grid_spec=pltpu.PrefetchScalarGridSpec(num_scalar_prefetch=1, grid=(rows // t,),
                                               in_specs=[spec, half_spec, half_spec, spec, spec], out_specs=[spec] * 4),
        out_shape=[jax.ShapeDtypeStruct(w.shape, F32)] * 4, compiler_params=_params(("parallel",)),
    )(core, w, mine, theirs, m, v)


def _adamw_small(ws, gs, ms, vs):
    n = len(ws)

    def body(*refs):
        for t in range(n):
            w_ref, g_ref, m_ref, v_ref = (refs[k * n + t] for k in range(4))
            d, nm, nv = _adam_math(w_ref[...], g_ref[...], m_ref[...], v_ref[...])
            refs[4 * n + t][...] = d
            refs[5 * n + t][...] = nm
            refs[6 * n + t][...] = nv

    vmem = pl.BlockSpec(memory_space=pltpu.VMEM)
    res = pl.pallas_call(body, name="adamw_small", in_specs=[vmem] * (4 * n), out_specs=[vmem] * (3 * n),
                         out_shape=[jax.ShapeDtypeStruct(a.shape, F32) for a in ws] * 3)(*ws, *gs, *ms, *vs)
    return res[:n], res[n:2 * n], res[2 * n:]


def kernel(x, w_in, b_gate, g_q_a, w_uq, g_kv_a, w_ukv, w_o_mla, w_o_dil, w_out, ln1_g, ln1_b, w_ff1, w_ff2, ln2_g, ln2_b, loss_target, m_w_in, m_b_gate, m_g_q_a, m_w_uq, m_g_kv_a, m_w_ukv, m_w_o_mla, m_w_o_dil, m_w_out, m_ln1_g, m_ln1_b, m_w_ff1, m_w_ff2, m_ln2_g, m_ln2_b, v_w_in, v_b_gate, v_g_q_a, v_w_uq, v_g_kv_a, v_w_ukv, v_w_o_mla, v_w_o_dil, v_w_out, v_ln1_g, v_ln1_b, v_w_ff1, v_w_ff2, v_ln2_g, v_ln2_b):
    order = ("w_in", "b_gate", "g_q_a", "w_uq", "g_kv_a", "w_ukv", "w_o_mla", "w_o_dil", "w_out", "ln1_g", "ln1_b", "w_ff1", "w_ff2", "ln2_g", "ln2_b")
    w = dict(w_in=w_in, b_gate=b_gate, g_q_a=g_q_a, w_uq=w_uq, g_kv_a=g_kv_a, w_ukv=w_ukv, w_o_mla=w_o_mla, w_o_dil=w_o_dil, w_out=w_out,
             ln1_g=ln1_g, ln1_b=ln1_b, w_ff1=w_ff1, w_ff2=w_ff2, ln2_g=ln2_g, ln2_b=ln2_b)
    m = dict(w_in=m_w_in, b_gate=m_b_gate, g_q_a=m_g_q_a, w_uq=m_w_uq, g_kv_a=m_g_kv_a, w_ukv=m_w_ukv, w_o_mla=m_w_o_mla, w_o_dil=m_w_o_dil,
             w_out=m_w_out, ln1_g=m_ln1_g, ln1_b=m_ln1_b, w_ff1=m_w_ff1, w_ff2=m_w_ff2, ln2_g=m_ln2_g, ln2_b=m_ln2_b)
    v = dict(w_in=v_w_in, b_gate=v_b_gate, g_q_a=v_g_q_a, w_uq=v_w_uq, g_kv_a=v_g_kv_a, w_ukv=v_w_ukv, w_o_mla=v_w_o_mla, w_o_dil=v_w_o_dil,
             w_out=v_w_out, ln1_g=v_ln1_g, ln1_b=v_ln1_b, w_ff1=v_w_ff1, w_ff2=v_w_ff2, ln2_g=v_ln2_g, ln2_b=v_ln2_b)
    chip = 2 * lax.axis_index("x") + lax.axis_index("y")
    south = (lax.axis_index("c") == 0).astype(F32)
    gate_w = D_MODEL // N_CHIPS

    core = lax.axis_index("c").astype(jnp.int32).reshape(1)
    turn = lambda n, a: a.T if n in TRANSPOSED else a
    early_shards = [turn(n, w[n][0]).astype(BF16) for n in EARLY]
    gathered = lambda group: [jax.ShapeDtypeStruct((N_CHIPS,) + s.shape, BF16) for s in group]
    e_sems, e_srcs, e_lands, e_token = _split_start("gather", early_shards, gathered(early_shards), "gather_early_start")
    e_token, late_f32, positions = lax.optimization_barrier((e_token, [w[n][0] for n in LATE], jnp.arange(x.shape[1], dtype=F32)))
    late_shards = [a.astype(BF16) for a in late_f32]
    rope = _rope_tables(positions)
    shards = dict(zip(EARLY + LATE, early_shards + late_shards))
    b_mine = lax.dynamic_update_slice(jnp.zeros((2, D_MODEL), F32), b_gate[0] * south, (0, chip * gate_w))
    b_full = _sum_all_devices(b_mine.reshape(-1, LANES), "gather_b_gate", e_token).reshape(2, D_MODEL)
    e_srcs, e_lands = _split_wait("gather", e_sems, e_srcs, e_lands, [b_full, *late_shards, *rope], "gather_early_wait")
    e_forward = _split_start("forward", e_srcs, None, "gather_early_forward_start", lands=e_lands)
    first = dict(zip(EARLY, _split_wait("forward", *e_forward[:3], e_forward[-1], "gather_early_forward_wait")[1]))
    g_sems, g_srcs, g_lands, g_token = _split_start("gather", late_shards, gathered(late_shards), "gather_late_start", after=first[EARLY[-1]])
    tables = _attention_tables(x.shape[1])

    sent = {}

    def late_arrived(after):
        srcs, lands = _split_wait("gather", g_sems, g_srcs, g_lands, after, "gather_late_wait")
        sent["forward"] = _split_start("forward", srcs, None, "gather_late_forward_start", lands=lands)
        return sent["forward"][-1]

    def late_weights(after):
        return dict(zip(LATE, _split_wait("forward", *sent["forward"][:3], after, "gather_late_forward_wait")[1]))

    exchange_shapes = lambda parts: [jax.ShapeDtypeStruct((3,) + p.shape[1:], BF16) for p in parts]

    def early_grads(grads_late):
        gs = [grads_late[n] for n in LATE]
        shapes = [jax.ShapeDtypeStruct((N_CHIPS,) + _half_shape(g.shape[1:]), F32) for g in gs]
        sent["pair"] = _split_start("pair", gs, shapes, "pair_split_late_start")
        return sent["pair"][-1]

    def early_grads_go(after):
        gs, theirs = _split_wait("pair", *sent["pair"][:3], after, "pair_split_late_wait")
        parts = [_pair_add(g, th, core, "pair_add_" + n) for g, th, n in zip(gs, theirs, LATE)]
        sent["late"] = _split_start("scatter", parts, exchange_shapes(parts), "exchange_late_start")
        return sent["late"][-1]

    def first_grad(g):
        sent["first"] = _split_start("pair", [g], [jax.ShapeDtypeStruct((N_CHIPS,) + _half_shape(g.shape[1:]), F32)], "pair_split_w_in_start")
        return sent["first"][-1]

    def last_grads(grads_early):
        rest = [grads_early[n] for n in EARLY[1:]]
        gs, theirs = _split_wait("pair", *sent["first"][:3], rest[-1], "pair_split_w_in_wait")
        gs, theirs = list(gs) + rest, list(theirs) + list(_pair_split(rest, "pair_split_early"))
        parts = [_pair_add(g, th, core, "pair_add_" + n) for g, th, n in zip(gs, theirs, EARLY)]
        sent["early"] = _split_start("scatter", parts, exchange_shapes(parts), "exchange_early_start")
        return sent["early"][-1]

    loss_part, grad_x, grads = _local_step(x, loss_target, first, b_full, g_q_a, g_kv_a, ln1_g, ln1_b, ln2_g, ln2_b, token=g_token,
                                           late_arrived=late_arrived, late_weights=late_weights, early_grads=early_grads, early_grads_go=early_grads_go,
                                           first_grad=first_grad, last_grads=last_grads, tables=tables, rope=rope)

    g_out, delta, new_m, new_v = {}, {}, {}, {}
    chip1 = chip.astype(jnp.int32).reshape(1)

    def sum_and_send(names, parts, others, tag):
        totals = [_chip_sum(p, o, chip1, "chip_sum_" + n) for n, p, o in zip(names, parts, others)]
        return _split_start("join", totals, [jax.ShapeDtypeStruct(t.shape, F32) for t in totals], "pair_join_" + tag + "_start")

    def adam(names, joined, after, tag):
        totals, halves = _split_wait("join", *joined[:3], after, "pair_join_" + tag + "_wait")
        for n, mine, theirs in zip(names, totals, halves):
            res = _adamw_big(turn(n, w[n][0]), mine, theirs, turn(n, m[n][0]), turn(n, v[n][0]), core, "adamw_" + n,
                             side_by_side=mine.shape[0] == shards[n].shape[0])
            g_out[n], delta[n], new_m[n], new_v[n] = (turn(n, r) for r in res)

    late_joined = sum_and_send(LATE, *_split_wait("scatter", *sent["late"][:3], grad_x, "exchange_late_wait"), "late")
    early_joined = sum_and_send(EARLY, *_split_wait("scatter", *sent["early"][:3], late_joined[-1], "exchange_early_wait"), "early")
    small_names = [name for name, _ in SMALL]
    sums = _sum_small([grads[name] for name in small_names] + [loss_part])
    loss = sums[-1][0, 0]
    g_small = dict(zip(small_names, sums))
    g_small["b_gate"] = lax.dynamic_slice(g_small["b_gate"], (0, chip * gate_w), (2, gate_w))
    flat = lambda a: a.reshape(-1, a.shape[-1])
    res = _adamw_small(*[[flat(d[name]) for name in small_names] for d in (w, g_small, m, v)])
    g_out.update(g_small)
    for d, r in zip((delta, new_m, new_v), res):
        d.update(zip(small_names, r))
    adam(LATE, late_joined, res[0][0], "late")
    adam(EARLY, early_joined, delta[LATE[-1]], "early")

    lead = lambda d: [d[name].reshape(w[name].shape) for name in order]
    return (loss, grad_x, *lead(g_out), *lead(delta), *lead(new_m), *lead(new_v))
```

```python
import functools
import math

import jax
import jax.numpy as jnp
import numpy as np
from jax import lax
from jax.experimental import pallas as pl
from jax.experimental.pallas import tpu as pltpu

F32 = jnp.float32
BF16 = jnp.bfloat16
MESH = pl.DeviceIdType.MESH

D_MODEL = 1024
N_HEADS = 8
LANES = 128
NOPE, ROPE, V_DIM = 64, 32, 64
MLA_QK = NOPE + ROPE
Q_LORA, KV_LORA = 384, 256
DIL_DIM = 64
DIL_PATTERNS = ((128, 1), (512, 4), (2048, 16))
D_FF = 4096
N_CHIPS = 4
N_DEV = 8
IN_WIDTH = 4256
LN_EPS, RMS_EPS = 1e-5, 1e-6
NEG = -1e30
LOG2E, LN2 = 1.4426950408889634, 0.6931471805599453
ALPHA = 2.0 ** 0.25
ROPE_THETA = 10000.0
LR, B1, B2, ADAM_EPS, WD, ADAM_STEP = 0.001, 0.9, 0.999, 1e-8, 0.01, 10

P_LORA, P_KR, P_GATE, P_HALF = 0, 640, 1024, 3072
DIL_GROUP = 4 * LANES
P_DIL = N_HEADS // 2 * DIL_GROUP
LORA_W = Q_LORA + KV_LORA
KR_LANE = NOPE

ATT_T = 512
ROW_T = 512
ACC_COLS = 256
VMEM_LIMIT = 56 * 1024 * 1024

NN = (((1,), (0,)), ((), ()))
NT = (((1,), (1,)), ((), ()))
TN = (((0,), (0,)), ((), ()))


def _params(sem=None, **kw):
    return pltpu.CompilerParams(dimension_semantics=sem, vmem_limit_bytes=VMEM_LIMIT, **kw)


def _matmul(a, b, *, mode, name, tm, tn, tk, out_dtypes=(F32,), extras=(), epilogue=None, b_shards=False, out_shards=False, after=None):
    pieces = list(a) if isinstance(a, (list, tuple)) else [a]
    n_pc = len(pieces)
    a_shape = (pieces[0].shape[0], sum(p.shape[1] for p in pieces))
    if b_shards:
        n_sh, rows_b, cols_b = b.shape
        b_shape = (rows_b, n_sh * cols_b)
    else:
        b_shape = b.shape
    if mode == "nn":
        (m, k), (k2, n) = a_shape, b_shape
    elif mode == "nt":
        (m, k), (n, k2) = a_shape, b_shape
    else:
        (k, m), (k2, n) = a_shape, b_shape
    assert k == k2, (a_shape, b.shape, mode)
    tm, tn, tk = min(tm, m), min(tn, n), min(tk, k)
    assert m % tm == 0 and n % tn == 0 and k % tk == 0, (name, m, n, k, tm, tn, tk)
    nk = k // tk
    n_ex, n_out = len(extras), len(out_dtypes)
    n_in = n_pc + 1 + n_ex + (after is not None)
    dims = {"nn": NN, "nt": NT, "tn": TN}[mode]
    col_tile = tm if mode == "tn" else tk
    blocks = [p.shape[1] // col_tile for p in pieces]
    firsts = [sum(blocks[:p]) for p in range(n_pc)]
    assert all(p.shape[1] % col_tile == 0 for p in pieces), (name, col_tile)

    def body(*refs):
        a_refs, b_ref = refs[:n_pc], refs[n_pc]
        ex_refs = refs[n_pc + 1:n_pc + 1 + n_ex]
        out_refs = refs[n_in:n_in + n_out]

        def finish(acc):
            outs = epilogue(acc, *[r[...] for r in ex_refs]) if epilogue is not None else (acc,)
            for r, o in zip(out_refs, outs):
                r[...] = o.astype(r.dtype)

        kk = pl.program_id(2)

        def step(a_ref):
            if nk == 1:
                finish(lax.dot_general(a_ref[...].astype(BF16), b_ref[...].astype(BF16), dims, preferred_element_type=F32))
                return
            acc_ref = refs[-1]

            @pl.when(kk == 0)
            def _():
                acc_ref[...] = jnp.zeros_like(acc_ref)

            for c in range(0, tn, ACC_COLS):
                b_blk = b_ref[c:c + ACC_COLS, :] if mode == "nt" else b_ref[:, c:c + ACC_COLS]
                acc_ref[:, c:c + ACC_COLS] += lax.dot_general(a_ref[...].astype(BF16), b_blk.astype(BF16), dims, preferred_element_type=F32)

            @pl.when(kk == nk - 1)
            def _():
                finish(acc_ref[...])

        if n_pc == 1:
            step(a_refs[0])
        else:
            at = pl.program_id(0) if mode == "tn" else kk
            for p in range(n_pc):
                pl.when(jnp.logical_and(at >= firsts[p], at < firsts[p] + blocks[p]))(functools.partial(step, a_refs[p]))

    def a_spec_of(p):
        if n_pc == 1:
            return pl.BlockSpec((tk, tm), lambda i, j, kk: (kk, i)) if mode == "tn" else pl.BlockSpec((tm, tk), lambda i, j, kk: (i, kk))
        col = lambda at: jnp.clip(at - firsts[p], 0, blocks[p] - 1)
        mine = lambda at: jnp.logical_and(at >= firsts[p], at < firsts[p] + blocks[p])
        if mode == "tn":
            return pl.BlockSpec((tk, tm), lambda i, j, kk: (jnp.where(mine(i), kk, 0), col(i)))
        return pl.BlockSpec((tm, tk), lambda i, j, kk: (i, col(kk)))

    b_spec = {"nn": pl.BlockSpec((tk, tn), lambda i, j, kk: (kk, j)),
              "nt": pl.BlockSpec((tn, tk), lambda i, j, kk: (j, kk)),
              "tn": pl.BlockSpec((tk, tn), lambda i, j, kk: (kk, j))}[mode]
    tile = pl.BlockSpec((tm, tn), lambda i, j, kk: (i, j))
    out_spec, out_dims = tile, (m, n)
    if b_shards and mode == "nn":
        per = cols_b // tn
        b_spec = pl.BlockSpec((None, tk, tn), lambda i, j, kk: (j // per, kk, j % per))
    elif b_shards:
        assert mode == "nt"
        per = cols_b // tk
        b_spec = pl.BlockSpec((None, tn, tk), lambda i, j, kk: (kk // per, j, kk % per))
    if out_shards:
        assert not extras and epilogue is None
        per_out = n // N_CHIPS // tn
        out_spec = pl.BlockSpec((None, tm, tn), lambda i, j, kk: (j // per_out, i, j % per_out))
        out_dims = (N_CHIPS, m, n // N_CHIPS)
    outs = pl.pallas_call(
        body, name=name,
        grid=(m // tm, n // tn, nk),
        in_specs=[a_spec_of(p) for p in range(n_pc)] + [b_spec] + [tile] * n_ex + [pl.BlockSpec(memory_space=pl.ANY)] * (after is not None),
        out_specs=[out_spec] * n_out,
        out_shape=[jax.ShapeDtypeStruct(out_dims, dt) for dt in out_dtypes],
        scratch_shapes=[pltpu.VMEM((tm, tn), F32)] if nk > 1 else [],
        compiler_params=_params(("parallel", "parallel", "arbitrary")),
    )(*pieces, b, *extras, *([after] if after is not None else []))
    return outs[0] if n_out == 1 else outs


def _rowwise(fn, *, name, rows, seq, ins, outs, sums=()):
    tm = min(ROW_T, seq)
    n_pos = seq // tm
    n_in, n_out, n_sum = len(ins), len(outs), len(sums)

    def body(*refs):
        vals = fn(*[r[...] for r in refs[:n_in]])
        for r, v in zip(refs[n_in:n_in + n_out], vals[:n_out]):
            r[...] = v.astype(r.dtype)
        first = pl.program_id(0) == 0
        for r, v in zip(refs[n_in + n_out:], vals[n_out:]):
            @pl.when(first)
            def _(r=r, v=v):
                r[...] = v

            @pl.when(jnp.logical_not(first))
            def _(r=r, v=v):
                r[...] += v

    def spec(arr, width, col, kind):
        if kind == "row":
            return pl.BlockSpec((tm, width), lambda i, col=col: (i, col))
        if kind == "pos":
            return pl.BlockSpec((tm, width), lambda i, col=col: (i % n_pos, col))
        return pl.BlockSpec(arr.shape, lambda i: (0,) * arr.ndim)

    res = pl.pallas_call(
        body, name=name,
        grid=(rows // tm,),
        in_specs=[spec(*t) for t in ins],
        out_specs=[pl.BlockSpec((tm, w), lambda i: (i, 0)) for w, _ in outs]
        + [pl.BlockSpec((1, w), lambda i: (0, 0)) for w in sums],
        out_shape=[jax.ShapeDtypeStruct((rows, w), dt) for w, dt in outs]
        + [jax.ShapeDtypeStruct((1, w), F32) for w in sums],
        compiler_params=_params(("arbitrary",)),
    )(*[t[0] for t in ins])
    return res


def _colsum(v):
    return jnp.sum(v, axis=0, keepdims=True)


def _rope_fwd(t, c, s_up, s_dn):
    return t * c + pltpu.roll(t, LANES - 16, 1) * s_up + pltpu.roll(t, 16, 1) * s_dn


def _rope_bwd(d, c, s_up, s_dn):
    return d * c + pltpu.roll(d * s_up, 16, 1) + pltpu.roll(d * s_dn, LANES - 16, 1)


def _rope_tables(positions):
    seq, half = positions.shape[0], ROPE // 2
    inv = jnp.power(ROPE_THETA, -jnp.arange(half, dtype=F32) / half)
    ang = positions[:, None] * inv[None, :]
    cos, sin = jnp.cos(ang), jnp.sin(ang)
    zeros = jnp.zeros((seq, half), F32)
    lo, hi = jnp.ones((seq, KR_LANE), F32), jnp.ones((seq, LANES - KR_LANE - ROPE), F32)
    c = jnp.concatenate([lo, cos, cos, hi], axis=1)
    c_rope_only = jnp.concatenate([0 * lo, cos, cos, 0 * hi], axis=1)
    s_up = jnp.concatenate([0 * lo, -sin, zeros, 0 * hi], axis=1)
    s_dn = jnp.concatenate([0 * lo, zeros, sin, 0 * hi], axis=1)
    return c, s_up, s_dn, c_rope_only


def _rms(x, g):
    r = lax.rsqrt(jnp.mean(x * x, axis=1, keepdims=True) + RMS_EPS)
    return x * r * g


def _rms_bwd(x, g, dy):
    r = lax.rsqrt(jnp.mean(x * x, axis=1, keepdims=True) + RMS_EPS)
    xh = x * r
    dxh = dy * g
    dx = r * (dxh - xh * jnp.mean(dxh * xh, axis=1, keepdims=True))
    return dx, _colsum(dy * xh)


def _ln_stats(x):
    mu = jnp.mean(x, axis=1, keepdims=True)
    xc = x - mu
    r = lax.rsqrt(jnp.mean(xc * xc, axis=1, keepdims=True) + LN_EPS)
    return xc * r, r


def _ln_bwd(xh, r, g, dy):
    dxh = dy * g
    dx = r * (dxh - jnp.mean(dxh, axis=1, keepdims=True) - xh * jnp.mean(dxh * xh, axis=1, keepdims=True))
    return dx, _colsum(dy * xh), _colsum(dy)


def _table_specs(tables, sub):
    whole = lambda a: pl.BlockSpec(a.shape, lambda b, g: (0,) * a.ndim)
    if len(tables) == 1:
        return [whole(tables[0])]
    return [whole(tables[0]), whole(tables[1]), pl.BlockSpec((sub, 1, LANES), lambda b, g: (g, 0, 0))]


def _biased(s, table_refs, delta, head):
    if delta < table_refs[0].shape[0]:
        s = s + table_refs[0][delta]
    if len(table_refs) == 3:
        s = s - table_refs[2][head, 0:1, 0:1] * table_refs[1][delta]
    return s


def _lane_masks(sub):
    lane = lax.broadcasted_iota(jnp.int32, (1, LANES), 1)
    return [(lane // (LANES // sub) == a).astype(F32) for a in range(sub)]


def _attn_fwd(q, qb0, k, kb0, v, vb0, tables, scale, *, name, batch, seq, sub=1, stride=1, wide_qk=False, after=None):
    t = ATT_T
    nq = seq // t
    rows = batch * seq
    n_tab = len(tables)
    qk_w = sub * LANES if wide_qk else LANES

    def body(q_ref, k_ref, v_ref, *rest):
        table_refs = rest[:n_tab]
        o_ref, lse_ref, vtb = rest[n_tab + (after is not None):][:3]
        per_head = rest[n_tab + (after is not None) + 3:]
        qbs, kbs = per_head[:sub], per_head[sub:]
        masks = _lane_masks(sub)
        for a in range(sub):
            lanes = slice(a * LANES, (a + 1) * LANES) if wide_qk else slice(None)
            qa = q_ref[:, lanes]
            qbs[a][...] = (qa.astype(F32) * masks[a]).astype(BF16) if sub > 1 and not wide_qk else qa.astype(BF16)
            if wide_qk or a == 0:
                kbs[a][...] = k_ref[:, lanes].astype(BF16)
        vtb[...] = v_ref[...].astype(F32).T.astype(BF16)
        for i in range(nq):
            out_t = None
            for a in range(sub):
                qt, kb = qbs[a][i * t:(i + 1) * t, :], kbs[a if wide_qk else 0]
                logits = [_biased(lax.dot_general(kb[j * t:(j + 1) * t, :], qt, NT, preferred_element_type=F32) * (scale * LOG2E), table_refs, i - j, a)
                          for j in range(i + 1)]
                m = jnp.max(functools.reduce(jnp.maximum, logits), axis=0, keepdims=True)
                ps = [jnp.exp2(s - m) for s in logits]
                l = jnp.sum(functools.reduce(jnp.add, ps), axis=0, keepdims=True)
                acc = functools.reduce(jnp.add, [lax.dot_general(vtb[:, j * t:(j + 1) * t], p.astype(BF16), NN, preferred_element_type=F32)
                                                 for j, p in enumerate(ps)])
                part = acc / l if sub == 1 else (acc / l) * masks[a].T
                out_t = part if out_t is None else out_t + part
                lse_ref[i * t:(i + 1) * t, a * LANES:(a + 1) * LANES] = jnp.broadcast_to((m + jnp.log2(l)) * LN2, (LANES, t)).T
            o_ref[i * t:(i + 1) * t, :] = out_t.T

    slab = lambda b0, step, width=LANES: pl.BlockSpec((seq, width), lambda b, g: (b, b0 + step * g))
    groups = N_HEADS // sub
    n_k = sub if wide_qk else 1
    return pl.pallas_call(
        body, name=name,
        grid=(batch, groups),
        in_specs=[slab(qb0, stride, qk_w), slab(kb0, stride, qk_w), slab(vb0, stride)] + _table_specs(tables, sub)
        + [pl.BlockSpec(memory_space=pl.ANY)] * (after is not None),
        out_specs=[slab(0, 1), slab(0, 1, sub * LANES)],
        out_shape=[jax.ShapeDtypeStruct((rows, groups * LANES), F32), jax.ShapeDtypeStruct((rows, N_HEADS * LANES), F32)],
        scratch_shapes=[pltpu.VMEM((LANES, seq), BF16)] + [pltpu.VMEM((seq, LANES), BF16)] * (sub + n_k),
        compiler_params=_params(("arbitrary", "arbitrary")),
    )(q, k, v, *tables, *([after] if after is not None else []))


def _attn_bwd(q, qb0, k, kb0, v, vb0, o, do, lse, tables, scale, *, name, batch, seq, out_dtype, sub=1, stride=1, wide_qk=False, after=None):
    t = ATT_T
    nq = seq // t
    rows = batch * seq
    n_tab = len(tables)
    groups = N_HEADS // sub
    packed = sub > 1 and not wide_qk
    n_out = 1 if packed else 3
    n_k = sub if wide_qk else 1
    qk_w = sub * LANES if wide_qk else LANES

    def body(q_ref, k_ref, v_ref, o_ref, do_ref, lse_ref, *rest):
        table_refs = rest[:n_tab]
        rest = rest[n_tab + (after is not None):]
        out_refs, (vb, dva), rest = rest[:n_out], rest[n_out:n_out + 2], rest[n_out + 2:]
        kbs, dkas, rest = rest[:n_k], rest[n_k:2 * n_k], rest[2 * n_k:]
        qbs, dobs, qtbs, dotbs = (rest[g * sub:(g + 1) * sub] for g in range(4))
        masks = _lane_masks(sub)
        vb[...] = v_ref[...].astype(BF16)
        for a in range(sub):
            lanes = slice(a * LANES, (a + 1) * LANES) if wide_qk else slice(None)
            qa = q_ref[:, lanes].astype(F32) * masks[a] if packed else q_ref[:, lanes].astype(F32)
            doa = do_ref[...] * masks[a] if sub > 1 else do_ref[...]
            qbs[a][...] = qa.astype(BF16)
            dobs[a][...] = doa.astype(BF16)
            qtbs[a][...] = qa.T.astype(BF16)
            dotbs[a][...] = doa.T.astype(BF16)
            if wide_qk or a == 0:
                kbs[a][...] = k_ref[:, lanes].astype(BF16)
        first_k, first_v = [[True] * nq for _ in range(n_k)], [True] * nq
        for i in range(nq):
            at = slice(i * t, (i + 1) * t)
            dq_all = None
            for a in range(sub):
                qt, dot, kb, dka = qbs[a][at, :], dobs[a][at, :], kbs[a if wide_qk else 0], dkas[a if wide_qk else 0]
                lse_t = lse_ref[at, a * LANES:a * LANES + 1] * LOG2E
                od = o_ref[at, :] * do_ref[at, :]
                delta = jnp.sum(od * masks[a] if sub > 1 else od, axis=1, keepdims=True)
                dq = None
                for j in range(i + 1):
                    kat = slice(j * t, (j + 1) * t)
                    kt, vt = kb[kat, :], vb[kat, :]
                    p = jnp.exp2(_biased(lax.dot_general(qt, kt, NT, preferred_element_type=F32) * (scale * LOG2E), table_refs, i - j, a) - lse_t)
                    dp = lax.dot_general(dot, vt, NT, preferred_element_type=F32)
                    ds = (p * (dp - delta) * scale).astype(BF16)
                    dk_part = lax.dot_general(qtbs[a][:, at], ds, NN, preferred_element_type=F32)
                    dv_part = lax.dot_general(dotbs[a][:, at], p.astype(BF16), NN, preferred_element_type=F32)
                    firsts = first_k[a if wide_qk else 0]
                    if firsts[j]:
                        dka[:, kat] = dk_part
                        firsts[j] = False
                    else:
                        dka[:, kat] += dk_part
                    if first_v[j]:
                        dva[:, kat] = dv_part
                        first_v[j] = False
                    else:
                        dva[:, kat] += dv_part
                    dq_part = lax.dot_general(ds, kt, NN, preferred_element_type=F32)
                    dq = dq_part if dq is None else dq + dq_part
                if wide_qk:
                    out_refs[0][at, a * LANES:(a + 1) * LANES] = dq.astype(out_refs[0].dtype)
                else:
                    dq = dq * masks[a] if sub > 1 else dq
                    dq_all = dq if dq_all is None else dq_all + dq
            if not wide_qk:
                out_refs[0][at, 0:LANES] = dq_all.astype(out_refs[0].dtype)
        if packed:
            out_refs[0][:, LANES:2 * LANES] = dkas[0][...].T.astype(out_refs[0].dtype)
            out_refs[0][:, 2 * LANES:3 * LANES] = dva[...].T.astype(out_refs[0].dtype)
            out_refs[0][:, 3 * LANES:] = jnp.zeros((seq, LANES), out_refs[0].dtype)
        else:
            for a in range(n_k):
                out_refs[1][:, a * LANES:(a + 1) * LANES] = dkas[a][...].T.astype(out_refs[1].dtype)
            out_refs[2][...] = dva[...].T.astype(out_refs[2].dtype)

    slab = lambda b0, step, width=LANES: pl.BlockSpec((seq, width), lambda b, g: (b, b0 + step * g))
    if packed:
        out_specs = [slab(0, 1, 4 * LANES)]
        out_shape = [jax.ShapeDtypeStruct((rows, groups * 4 * LANES), out_dtype)]
    else:
        out_specs = [slab(0, 1, qk_w), slab(0, 1, qk_w), slab(0, 1)]
        out_shape = [jax.ShapeDtypeStruct((rows, N_HEADS * LANES), out_dtype)] * 2 + [jax.ShapeDtypeStruct((rows, groups * LANES), out_dtype)]
    res = pl.pallas_call(
        body, name=name,
        grid=(batch, groups),
        in_specs=[slab(qb0, stride, qk_w), slab(kb0, stride, qk_w), slab(vb0, stride), slab(0, 1), slab(0, 1), slab(0, 1, sub * LANES)]
        + _table_specs(tables, sub) + [pl.BlockSpec(memory_space=pl.ANY)] * (after is not None),
        out_specs=out_specs, out_shape=out_shape,
        scratch_shapes=[pltpu.VMEM((seq, LANES), BF16), pltpu.VMEM((LANES, seq), F32)]
        + [pltpu.VMEM((seq, LANES), BF16)] * n_k + [pltpu.VMEM((LANES, seq), F32)] * n_k
        + [pltpu.VMEM((seq, LANES), BF16)] * (2 * sub) + [pltpu.VMEM((LANES, seq), BF16)] * (2 * sub),
        compiler_params=_params(("arbitrary", "arbitrary")),
    )(q, k, v, o, do, lse, *tables, *([after] if after is not None else []))
    return res[0] if packed else res


def _attention_tables(seq):
    n = seq // ATT_T
    pos = np.arange(ATT_T, dtype=np.int32)
    dist = np.arange(n, dtype=np.int32)[:, None, None] * ATT_T + pos[None, :, None] - pos[None, None, :]
    causal = np.where(dist[:1] >= 0, 0.0, NEG).astype(np.float32)
    count = np.zeros(dist.shape, np.float32)
    for window, dilation in DIL_PATTERNS:
        count += ((dist >= 0) & (dist <= window) & (dist % dilation == 0)).astype(np.float32)
    held = np.where(count > 0, np.log2(np.maximum(count, 1.0)), NEG).astype(np.float32)
    far = dist.astype(np.float32) * np.float32(LOG2E)
    slopes = np.asarray([2.0 ** (-8.0 * (i + 1) / N_HEADS) for i in range(N_HEADS)], np.float32)
    slopes = jnp.asarray(np.broadcast_to(slopes[:, None, None], (N_HEADS, 1, LANES)))
    flat = lambda a: jnp.asarray(np.ascontiguousarray(a))
    turned = lambda a: flat(np.swapaxes(a, 1, 2))
    return ((flat(causal),), (turned(causal),)), ((flat(held), flat(far), slopes), (turned(held), turned(far), slopes))


def _pad_heads(w, width):
    kdim, n = w.shape[0], w.shape[1] // width
    return jnp.pad(w.reshape(kdim, n, width), ((0, 0), (0, 0), (0, LANES - width))).reshape(kdim, n * LANES)


def _unpad_heads(w, width):
    kdim, n = w.shape[0], w.shape[1] // LANES
    return w.reshape(kdim, n, LANES)[:, :, :width].reshape(kdim, n * width)


def _pad_head_rows(w, width):
    n, kdim = w.shape[0] // width, w.shape[1]
    return jnp.pad(w.reshape(n, width, kdim), ((0, 0), (0, LANES - width), (0, 0))).reshape(n * LANES, kdim)


def _unpad_head_rows(w, width):
    n, kdim = w.shape[0] // LANES, w.shape[1]
    return w.reshape(n, LANES, kdim)[:, :width].reshape(n * width, kdim)


def _pad_w_in_t(wt):
    n_qkv, pair = 3 * N_HEADS * DIL_DIM, 2 * DIL_DIM
    zeros = lambda n: jnp.zeros((n, wt.shape[1]), wt.dtype)
    main = jnp.concatenate([wt[:LORA_W], zeros(KR_LANE), wt[LORA_W:LORA_W + ROPE], zeros(P_GATE - P_KR - KR_LANE - ROPE),
                            wt[LORA_W + ROPE + n_qkv:]], axis=0)
    qkv = wt[LORA_W + ROPE:LORA_W + ROPE + n_qkv].reshape(3, N_HEADS // 2, pair, wt.shape[1]).transpose(1, 0, 2, 3)
    dil = jnp.pad(qkv, ((0, 0), (0, 1), (0, 0), (0, 0))).reshape(P_DIL, wt.shape[1])
    return main, dil


def _unpad_w_in_t(gt):
    n_qkv, pair = 3 * N_HEADS * DIL_DIM, 2 * DIL_DIM
    moves = [(P_LORA, 0, LORA_W), (P_KR + KR_LANE, LORA_W, ROPE)]
    moves += [(P_HALF + (g * 4 + j) * pair, LORA_W + ROPE + (j * (N_HEADS // 2) + g) * pair, pair) for j in range(3) for g in range(N_HEADS // 2)]
    moves += [(P_GATE + r, LORA_W + ROPE + n_qkv + r, ROW_T) for r in range(0, P_HALF - P_GATE, ROW_T)]
    n = len(moves)

    def body(src, out, buf, sem_in, sem_out):
        ins = [pltpu.make_async_copy(src.at[pl.ds(a, k)], buf.at[pl.ds(b, k)], sem_in.at[i]) for i, (a, b, k) in enumerate(moves)]
        outs = [pltpu.make_async_copy(buf.at[pl.ds(b, k)], out.at[pl.ds(b, k)], sem_out.at[i]) for i, (a, b, k) in enumerate(moves)]
        for cp in ins:
            cp.start()
        for cp_in, cp_out in zip(ins, outs):
            cp_in.wait()
            cp_out.start()
        for cp in outs:
            cp.wait()

    return pl.pallas_call(
        body, name="unpad_d_w_in", in_specs=[pl.BlockSpec(memory_space=pl.ANY)], out_specs=pl.BlockSpec(memory_space=pl.ANY),
        out_shape=jax.ShapeDtypeStruct((IN_WIDTH, gt.shape[1]), gt.dtype),
        scratch_shapes=[pltpu.VMEM((IN_WIDTH, gt.shape[1]), gt.dtype), pltpu.SemaphoreType.DMA((n,)), pltpu.SemaphoreType.DMA((n,))],
        compiler_params=_params(),
    )(gt)


def _split_ukv(w):
    w3 = w.reshape(w.shape[0], N_HEADS, NOPE + V_DIM)
    return _pad_heads(w3[:, :, :NOPE].reshape(w.shape[0], -1), NOPE), w3[:, :, NOPE:].reshape(w.shape[0], -1)


def _merge_ukv(g_k, g_v):
    kdim = g_k.shape[0]
    k3 = _unpad_heads(g_k, NOPE).reshape(kdim, N_HEADS, NOPE)
    return jnp.concatenate([k3, g_v.reshape(kdim, N_HEADS, V_DIM)], axis=2).reshape(kdim, N_HEADS * (NOPE + V_DIM))


def _join_cols(w):
    return w.transpose(1, 0, 2).reshape(w.shape[1], N_CHIPS * w.shape[2])


def _split_cols(g):
    return g.reshape(g.shape[0], N_CHIPS, g.shape[1] // N_CHIPS).transpose(1, 0, 2)


def _local_step(x3, target3, wg, b_gate, g_q_a, g_kv_a, ln1_g, ln1_b, ln2_g, ln2_b, token=None, late_arrived=None, late_weights=None,
                early_grads=None, early_grads_go=None, first_grad=None, last_grads=None, tables=None, rope=None, x_lo=None):
    w_main_t, w_dil_t = _pad_w_in_t(wg["w_in"].reshape(IN_WIDTH, D_MODEL))
    w_uq_pt = _pad_head_rows(wg["w_uq"].reshape(N_HEADS * MLA_QK, Q_LORA), MLA_QK)
    w_ukv = _join_cols(wg["w_ukv"])
    batch, seq, _ = x3.shape
    rows = batch * seq
    x = x3.reshape(rows, D_MODEL)
    x_mm = x if x_lo is None else x_lo.reshape(rows, D_MODEL)
    target = target3.reshape(rows, D_MODEL)
    row = functools.partial(_rowwise, rows=rows, seq=seq)
    mm = _matmul

    w_uk_p, w_uv = _split_ukv(w_ukv)
    b0, b1 = b_gate[0:1], b_gate[1:2]
    rope_c, rope_up, rope_dn, rope_c_only = _rope_tables(jnp.arange(seq, dtype=F32)) if rope is None else rope
    (mla_bwd_tables, mla_fwd_tables), (dil_bwd_tables, dil_fwd_tables) = _attention_tables(seq) if tables is None else tables
    scale_mla, scale_dil = MLA_QK ** -0.5, DIL_DIM ** -0.5
    lora0, kr0, gate0 = P_LORA // LORA_W, P_KR // LANES, P_GATE // D_MODEL

    proj = mm(x_mm, w_main_t, mode="nt", name="proj", tm=1024, tn=1536, tk=1024, after=token)
    proj_d = mm(x_mm, w_dil_t, mode="nt", name="proj_dil", tm=1024, tn=1024, tk=1024, out_dtypes=(BF16,))

    def prep(lora, gq, gkv):
        return _rms(lora[:, :Q_LORA], gq), _rms(lora[:, Q_LORA:], gkv)

    qn, kvn = row(prep, name="mla_rms", ins=[(proj, LORA_W, lora0, "row"), (g_q_a, 0, 0, "full"), (g_kv_a, 0, 0, "full")],
                  outs=[(Q_LORA, BF16), (KV_LORA, BF16)])
    q_lin = mm(qn, w_uq_pt, mode="nt", name="q_up", tm=1024, tn=1024, tk=Q_LORA)
    k_lin = mm(kvn, w_uk_p, mode="nn", name="k_up", tm=1024, tn=1024, tk=KV_LORA)
    v_a = mm(kvn, w_uv, mode="nn", name="v_up", tm=1024, tn=1024, tk=KV_LORA, out_dtypes=(BF16,))

    def rope_qk(ql, kl, kr, c, up, dn):
        k_rot = _rope_fwd(kr, c, up, dn)
        qs = [_rope_fwd(ql[:, h * LANES:(h + 1) * LANES], c, up, dn) for h in range(N_HEADS)]
        ks = [kl[:, h * LANES:(h + 1) * LANES] + k_rot for h in range(N_HEADS)]
        return jnp.concatenate(qs, axis=1), jnp.concatenate(ks, axis=1)

    pos = lambda tab: (tab, LANES, 0, "pos")
    q_a, k_a = row(rope_qk, name="rope_qk",
                   ins=[(q_lin, D_MODEL, 0, "row"), (k_lin, D_MODEL, 0, "row"), (proj, LANES, kr0, "row"), pos(rope_c), pos(rope_up), pos(rope_dn)],
                   outs=[(N_HEADS * LANES, BF16), (N_HEADS * LANES, BF16)])
    o_a, lse_a = _attn_fwd(q_a, 0, k_a, 0, v_a, 0, mla_fwd_tables, scale_mla, name="mla_fwd", batch=batch, seq=seq, sub=2, wide_qk=True)
    arrived = None if late_arrived is None else late_arrived(o_a)
    o_b, lse_b = _attn_fwd(proj_d, 0, proj_d, 1, proj_d, 2, dil_fwd_tables, scale_dil, name="dil_fwd", batch=batch, seq=seq, sub=2, stride=4, after=arrived)
    late = wg if late_weights is None else late_weights(o_b)
    w_oa = _join_cols(late["w_o_mla"])
    w_ob = _join_cols(late["w_o_dil"])
    w_out, w_ff1, w_ff2 = late["w_out"].reshape(D_MODEL, D_MODEL), late["w_ff1"], late["w_ff2"].reshape(D_FF, D_MODEL)
    y_a = mm(o_a, w_oa, mode="nn", name="o_mla", tm=1024, tn=1024, tk=1024, out_dtypes=(BF16,))
    y_b = mm(o_b, w_ob, mode="nn", name="o_dil", tm=1024, tn=1024, tk=1024, out_dtypes=(BF16,))

    def gate(t0, t1, c0, c1, ya, yb):
        return (jax.nn.sigmoid(t0 + c0) * ya + jax.nn.sigmoid(t1 + c1) * yb,)

    gate_ins = [(proj, D_MODEL, gate0, "row"), (proj, D_MODEL, gate0 + 1, "row"), (b0, 0, 0, "full"), (b1, 0, 0, "full")]
    (u,) = row(gate, name="gate", ins=gate_ins + [(y_a, D_MODEL, 0, "row"), (y_b, D_MODEL, 0, "row")], outs=[(D_MODEL, BF16)])
    mixed = mm(u, w_out, mode="nn", name="mix", tm=1024, tn=1024, tk=1024)

    def ln1(xv, mv, g, b):
        r1 = ALPHA * xv + mv
        xh, _ = _ln_stats(r1)
        hv = xh * g + b
        return r1, hv, hv

    r1, h, h_b = row(ln1, name="ln1", ins=[(x, D_MODEL, 0, "row"), (mixed, D_MODEL, 0, "row"), (ln1_g, 0, 0, "full"), (ln1_b, 0, 0, "full")],
                outs=[(D_MODEL, F32), (D_MODEL, F32), (D_MODEL, BF16)])

    def relu2(acc):
        r = jnp.maximum(acc, 0.0)
        return (r * r,)

    z = mm(h_b, w_ff1, mode="nn", name="ff1", tm=1024, tn=1024, tk=1024, out_dtypes=(BF16,), epilogue=relu2, b_shards=True)
    f = mm(z, w_ff2, mode="nn", name="ff2", tm=1024, tn=1024, tk=2048)

    def ln2_loss(hv, fv, tv, g, b):
        xh, r = _ln_stats(ALPHA * hv + fv)
        err = xh * g + b - tv
        dy = err * (1.0 / D_MODEL)
        dr2, dg, db = _ln_bwd(xh, r, g, dy)
        loss = jnp.sum(_colsum(err * err), axis=1, keepdims=True) * (0.5 / D_MODEL)
        return dr2, dr2, jnp.broadcast_to(loss, (1, LANES)), dg, db

    dr2, dr2_b, loss_l, d_ln2_g, d_ln2_b = row(
        ln2_loss, name="ln2_loss",
        ins=[(h, D_MODEL, 0, "row"), (f, D_MODEL, 0, "row"), (target, D_MODEL, 0, "row"), (ln2_g, 0, 0, "full"), (ln2_b, 0, 0, "full")],
        outs=[(D_MODEL, F32), (D_MODEL, BF16)], sums=[LANES, D_MODEL, D_MODEL])

    d_w_ff2 = mm(z, dr2_b, mode="tn", name="d_w_ff2", tm=1024, tn=1024, tk=2048)
    da = mm(dr2_b, w_ff2, mode="nt", name="d_ff_act", tm=1024, tn=1024, tk=1024, out_dtypes=(BF16,), extras=(z,),
            epilogue=lambda acc, zv: (acc * (2.0 * jnp.sqrt(zv.astype(F32))),))
    d_w_ff1 = mm(h_b, da, mode="tn", name="d_w_ff1", tm=1024, tn=1024, tk=2048, out_shards=True)
    dh = mm(da, w_ff1, mode="nt", name="d_h", tm=1024, tn=1024, tk=1024, extras=(dr2,), epilogue=lambda acc, rv: (acc + ALPHA * rv,), b_shards=True)

    def ln1_bwd(dhv, r1v, g):
        xh, r = _ln_stats(r1v)
        return _ln_bwd(xh, r, g, dhv)

    dr1, d_ln1_g, d_ln1_b = row(ln1_bwd, name="ln1_bwd", ins=[(dh, D_MODEL, 0, "row"), (r1, D_MODEL, 0, "row"), (ln1_g, 0, 0, "full")],
                                outs=[(D_MODEL, F32)], sums=[D_MODEL, D_MODEL])
    d_w_out = mm(u, dr1, mode="tn", name="d_w_out", tm=1024, tn=1024, tk=1024)
    du = mm(dr1, w_out, mode="nt", name="d_u", tm=1024, tn=1024, tk=1024, out_dtypes=(BF16,))

    def gate_bwd(t0, t1, c0, c1, ya, yb, duv):
        s0, s1 = jax.nn.sigmoid(t0 + c0), jax.nn.sigmoid(t1 + c1)
        dt0 = duv * ya * s0 * (1.0 - s0)
        dt1 = duv * yb * s1 * (1.0 - s1)
        return duv * s0, duv * s1, jnp.concatenate([dt0, dt1], axis=1), jnp.concatenate([_colsum(dt0), _colsum(dt1)], axis=1)

    dy_a, dy_b, d_gates, d_b_gate = row(
        gate_bwd, name="gate_bwd", ins=gate_ins + [(y_a, D_MODEL, 0, "row"), (y_b, D_MODEL, 0, "row"), (du, D_MODEL, 0, "row")],
        outs=[(D_MODEL, BF16), (D_MODEL, BF16), (2 * D_MODEL, BF16)], sums=[2 * D_MODEL])
    d_w_oa = mm(o_a, dy_a, mode="tn", name="d_w_o_mla", tm=1024, tn=1024, tk=1024)
    d_w_ob = mm(o_b, dy_b, mode="tn", name="d_w_o_dil", tm=1024, tn=1024, tk=1024)
    grads = dict(w_o_mla=_split_cols(d_w_oa), w_o_dil=_split_cols(d_w_ob),
                 w_out=d_w_out.reshape(N_CHIPS, D_MODEL // N_CHIPS, D_MODEL), w_ff1=d_w_ff1, w_ff2=d_w_ff2.reshape(N_CHIPS, D_FF // N_CHIPS, D_MODEL))
    sent = None if early_grads is None else early_grads(grads)
    do_a = mm(dy_a, w_oa, mode="nt", name="d_o_mla", tm=1024, tn=1024, tk=1024, after=sent)
    do_b = mm(dy_b, w_ob, mode="nt", name="d_o_dil", tm=1024, tn=1024, tk=1024)
    dq_a, dk_a, dv_a = _attn_bwd(q_a, 0, k_a, 0, v_a, 0, o_a, do_a, lse_a, mla_bwd_tables, scale_mla,
                                 name="mla_bwd", batch=batch, seq=seq, out_dtype=F32, sub=2, wide_qk=True)
    going = None if early_grads_go is None else early_grads_go(dq_a)
    d_qkv_d = _attn_bwd(proj_d, 0, proj_d, 1, proj_d, 2, o_b, do_b, lse_b, dil_bwd_tables, scale_dil,
                        name="dil_bwd", batch=batch, seq=seq, out_dtype=BF16, sub=2, stride=4, after=going)

    def mla_post(dq, dk, c, up, dn, c_only):
        dqs = [_rope_bwd(dq[:, h * LANES:(h + 1) * LANES], c, up, dn) for h in range(N_HEADS)]
        dk_sum = dk[:, :LANES]
        for h in range(1, N_HEADS):
            dk_sum = dk_sum + dk[:, h * LANES:(h + 1) * LANES]
        return jnp.concatenate(dqs, axis=1), _rope_bwd(dk_sum, c_only, up, dn)

    dq_lin, d_kr = row(mla_post, name="mla_unrope",
                       ins=[(dq_a, D_MODEL, 0, "row"), (dk_a, D_MODEL, 0, "row"), pos(rope_c), pos(rope_up), pos(rope_dn), pos(rope_c_only)],
                       outs=[(N_HEADS * LANES, BF16), (LANES, BF16)])
    d_qn = mm(dq_lin, w_uq_pt, mode="nn", name="d_qn", tm=1024, tn=Q_LORA, tk=1024)
    d_kvn_k = mm(dk_a, w_uk_p, mode="nt", name="d_kvn_k", tm=1024, tn=KV_LORA, tk=1024)
    d_kvn = mm(dv_a, w_uv, mode="nt", name="d_kvn", tm=1024, tn=KV_LORA, tk=1024, extras=(d_kvn_k,), epilogue=lambda acc, e: (acc + e,))

    def rms_bwd(lora, dq, dkv, dkr, gq, gkv):
        dxq, dgq = _rms_bwd(lora[:, :Q_LORA], gq, dq)
        dxk, dgk = _rms_bwd(lora[:, Q_LORA:], gkv, dkv)
        tail = jnp.zeros((dxq.shape[0], P_GATE - P_KR - LANES), F32)
        return jnp.concatenate([dxq, dxk, dkr.astype(F32), tail], axis=1), dgq, dgk

    d_tail, d_g_q_a, d_g_kv_a = row(
        rms_bwd, name="mla_rms_bwd",
        ins=[(proj, LORA_W, lora0, "row"), (d_qn, Q_LORA, 0, "row"), (d_kvn, KV_LORA, 0, "row"), (d_kr, LANES, 0, "row"),
             (g_q_a, 0, 0, "full"), (g_kv_a, 0, 0, "full")],
        outs=[(P_GATE, BF16)], sums=[Q_LORA, KV_LORA])
    d_proj = [d_tail, d_gates, d_qkv_d]
    d_w_in_pt = mm(d_proj, x_mm, mode="tn", name="d_w_in", tm=1024, tn=1024, tk=2048)
    d_w_in = _unpad_w_in_t(d_w_in_pt).reshape(N_CHIPS, IN_WIDTH // N_CHIPS, D_MODEL)
    moving = None if first_grad is None else first_grad(d_w_in)
    d_w_uq_pt = mm(dq_lin, qn, mode="tn", name="d_w_uq", tm=1024, tn=Q_LORA, tk=1024, after=moving)
    d_w_uk_p = mm(kvn, dk_a, mode="tn", name="d_w_uk", tm=KV_LORA, tn=1024, tk=1024, after=moving)
    d_w_uv = mm(kvn, dv_a, mode="tn", name="d_w_uv", tm=KV_LORA, tn=1024, tk=1024, after=moving)
    grads.update(w_in=d_w_in,
                 w_uq=_unpad_head_rows(d_w_uq_pt, MLA_QK).reshape(N_CHIPS, N_HEADS * MLA_QK // N_CHIPS, Q_LORA),
                 w_ukv=_split_cols(_merge_ukv(d_w_uk_p, d_w_uv)))
    leaving = None if last_grads is None else last_grads(grads)
    grad_x = mm(d_proj, jnp.concatenate([w_main_t, w_dil_t], axis=0), mode="nn", name="d_x", tm=1024, tn=1024, tk=1024, extras=(dr1,), epilogue=lambda acc, rv: (acc + ALPHA * rv,),
                after=leaving)

    grads.update(
        b_gate=d_b_gate.reshape(2, D_MODEL), g_q_a=d_g_q_a, g_kv_a=d_g_kv_a, ln1_g=d_ln1_g, ln1_b=d_ln1_b, ln2_g=d_ln2_g, ln2_b=d_ln2_b)
    return loss_l, grad_x.reshape(batch, seq, D_MODEL), grads


BIG = ("w_in", "w_uq", "w_ukv", "w_o_mla", "w_o_dil", "w_out", "w_ff1", "w_ff2")
SMALL = (("b_gate", 2 * D_MODEL), ("g_q_a", Q_LORA), ("g_kv_a", KV_LORA), ("ln1_g", D_MODEL), ("ln1_b", D_MODEL),
         ("ln2_g", D_MODEL), ("ln2_b", D_MODEL))
TRANSPOSED = ("w_in", "w_uq")
D2D_PIECES = (4, 2, 1)
ANY = pl.BlockSpec(memory_space=pl.ANY)
SIDE_EFFECTS = pltpu.CompilerParams(has_side_effects=True)


def _place():
    x, y, c = lax.axis_index("x"), lax.axis_index("y"), lax.axis_index("c")
    return x, y, c, ((1 - x, y), (x, 1 - y), (1 - x, 1 - y))


def _half_axis(shape):
    return 0 if shape[0] % 32 == 0 else 1


def _half_shape(shape):
    return (shape[0] // 2, shape[1]) if _half_axis(shape) == 0 else (shape[0], shape[1] // 2)


def _window(ref, lead, shape, which=None, pieces=False):
    axis = _half_axis(shape)
    size = shape[axis] if which is None else shape[axis] // 2
    base = 0 if which is None else which * size
    tile = (16, LANES)[axis]
    count = next(c for c in D2D_PIECES if size % (tile * c) == 0) if pieces else 1
    step = size // count
    spans = [pl.ds(pl.multiple_of(base + i * step, tile), step) for i in range(count)]
    refs = [ref.at[(*lead, s)] if axis == 0 else ref.at[(*lead, slice(None), s)] for s in spans]
    return refs if pieces else refs[0]


def _remote(src, dst, send, recv, to):
    return pltpu.make_async_remote_copy(src_ref=src, dst_ref=dst, send_sem=send, recv_sem=recv, device_id=to, device_id_type=MESH)


def _pair_split(grads, name):
    n = len(grads)

    def body(*refs):
        srcs, outs, (send, recv) = refs[:n], refs[n:2 * n], refs[2 * n:]
        x, y, c, _ = _place()
        for t in range(n):
            for s in range(N_CHIPS):
                _remote(_window(srcs[t], (s,), grads[t].shape[1:], 1 - c), outs[t].at[s], send.at[t], recv.at[t], (x, y, 1 - c)).start()
        for t in range(n):
            _remote(_window(srcs[t], (slice(None),), grads[t].shape[1:], 1 - c), outs[t], send.at[t], recv.at[t], (x, y, 1 - c)).wait()

    return pl.pallas_call(
        body, name=name, in_specs=[ANY] * n, out_specs=[ANY] * n,
        out_shape=[jax.ShapeDtypeStruct((N_CHIPS,) + _half_shape(g.shape[1:]), g.dtype) for g in grads],
        scratch_shapes=[pltpu.SemaphoreType.DMA((n,)), pltpu.SemaphoreType.DMA((n,))],
        compiler_params=SIDE_EFFECTS,
    )(*grads)


HBM = pl.BlockSpec(memory_space=pltpu.HBM)
SEM = pl.BlockSpec(memory_space=pltpu.SEMAPHORE)
SPLIT = pltpu.CompilerParams(has_side_effects=pltpu.SideEffectType.DATAFLOW_SIDE_EFFECTING)


def _in_hbm(a):
    return pltpu.with_memory_space_constraint(a, pltpu.HBM)


def _split_copies(kind, srcs, lands):
    x, y, c, chips = _place()
    out = []
    for t in range(len(srcs)):
        if kind == "pair":
            out += [(t, s % 3, _window(srcs[t], (s,), srcs[t].shape[1:], 1 - c), lands[t].at[s], (x, y, 1 - c)) for s in range(N_CHIPS)]
            continue
        if kind == "join":
            out += [(t, 0, a, b, (x, y, 1 - c)) for a, b in zip(_window(srcs[t], (), srcs[t].shape, None, True), _window(lands[t], (), srcs[t].shape, None, True))]
            continue
        if kind == "forward":
            shape, sibling = srcs[t].shape, (x, y, 1 - c)
            out += [(t, 0, a, b, sibling) for a, b in zip(_window(srcs[t], (), shape, None, True), _window(lands[t], (2 * x + y,), shape, None, True))]
            out += [(t, j, a, a, sibling) for j, (cx, cy) in enumerate(chips) for a in _window(lands[t], (2 * cx + cy,), shape, c, True)]
            continue
        for j, (cx, cy) in enumerate(chips):
            if kind == "gather":
                shape = srcs[t].shape
                out.append((t, j, _window(srcs[t], (), shape, c), _window(lands[t], (2 * x + y,), shape, c), (cx, cy, c)))
            else:
                out.append((t, j, srcs[t].at[2 * cx + cy], lands[t].at[j], (cx, cy, c)))
    return out


def _split_start(kind, srcs, land_shapes, name, lands=None, after=None):
    n = len(srcs)

    def body(*refs):
        src_refs, land_refs, sems, token = refs[:n], refs[n:2 * n], refs[-7 - 2 * n:-1 - 2 * n], refs[-1]
        for t, j, s, d, to in _split_copies(kind, src_refs, land_refs):
            _remote(s, d, sems[j], sems[3 + j], to).start()
        token[...] = jnp.zeros_like(token)

    lands = [_in_hbm(lax.empty(s.shape, s.dtype)) for s in land_shapes] if lands is None else list(lands)
    thru = [pltpu.HBM(a.shape, a.dtype) for a in list(srcs) + lands]
    res = pl.pallas_call(
        body, name=name,
        out_shape=(*[pltpu.SemaphoreType.DMA(())] * 6, *thru, jax.ShapeDtypeStruct((8, LANES), F32)),
        in_specs=[HBM] * (2 * n) + [ANY] * (after is not None), out_specs=(*[SEM] * 6, *[HBM] * (2 * n), pl.BlockSpec(memory_space=pltpu.VMEM)),
        input_output_aliases={i: 6 + i for i in range(2 * n)}, compiler_params=SPLIT,
    )(*[_in_hbm(s) for s in srcs], *lands, *([after] if after is not None else []))
    return res[:6], res[6:6 + n], res[6 + n:6 + 2 * n], res[-1]


def _split_wait(kind, sems, srcs, lands, after, name):
    n = len(srcs)
    afters = list(after) if isinstance(after, (list, tuple)) else [after]

    def body(*refs):
        src_refs, land_refs, sem_refs = refs[:n], refs[n:2 * n], refs[2 * n:2 * n + 6]
        for t, j, s, d, to in _split_copies(kind, src_refs, land_refs):
            cp = _remote(s, d, sem_refs[j], sem_refs[3 + j], to)
            cp.wait_send()
            cp.wait_recv()

    res = pl.pallas_call(
        body, name=name, out_shape=[pltpu.HBM(a.shape, a.dtype) for a in list(srcs) + list(lands)],
        in_specs=[HBM] * (2 * n) + [SEM] * 6 + [ANY] * len(afters), out_specs=[HBM] * (2 * n),
        input_output_aliases={i: i for i in range(2 * n)}, compiler_params=SPLIT,
    )(*srcs, *lands, *sems, *afters)
    return res[:n], res[n:]


def _sum_all_devices(vec, name, after):
    n_rows = vec.shape[0]

    def body(v_ref, *rest):
        out_ref, buf, send, recv = rest[len(after):]
        x, y, c, _ = _place()
        me = 4 * x + 2 * y + c
        buf[me] = v_ref[...]
        flips = [(a, b, d) for a in (0, 1) for b in (0, 1) for d in (0, 1)][1:]
        copies = []
        for r, (a, b, d) in enumerate(flips):
            px, py, pc = (1 - x if a else x), (1 - y if b else y), (1 - c if d else c)
            copies.append(pltpu.make_async_remote_copy(src_ref=v_ref, dst_ref=buf.at[me], send_sem=send.at[r], recv_sem=recv.at[r],
                                                       device_id=(px, py, pc), device_id_type=MESH))
            copies[-1].start()
        for r, (a, b, d) in enumerate(flips):
            px, py, pc = (1 - x if a else x), (1 - y if b else y), (1 - c if d else c)
            pltpu.make_async_remote_copy(src_ref=v_ref, dst_ref=buf.at[4 * px + 2 * py + pc], send_sem=send.at[r], recv_sem=recv.at[r],
                                         device_id=(px, py, pc), device_id_type=MESH).wait_recv()
        for cp in copies:
            cp.wait_send()
        total = buf[0]
        for k in range(1, N_DEV):
            total = total + buf[k]
        out_ref[...] = total

    vmem = pl.BlockSpec(memory_space=pltpu.VMEM)
    return pl.pallas_call(
        body, name=name, in_specs=[vmem] + [ANY] * len(after), out_specs=vmem, out_shape=jax.ShapeDtypeStruct(vec.shape, F32),
        scratch_shapes=[pltpu.VMEM((N_DEV, n_rows, LANES), F32), pltpu.SemaphoreType.DMA((N_DEV - 1,)), pltpu.SemaphoreType.DMA((N_DEV - 1,))],
        compiler_params=pltpu.CompilerParams(has_side_effects=True),
    )(vec, *after)


def _half_tile(half, width):
    t = half
    while t * width * 4 > (2 << 20) and t % 32 == 0:
        t //= 2
    return t


def _pair_add(g, theirs, core, name):
    _, half, width = theirs.shape
    t = _half_tile(half, width)
    n = half // t

    def body(c_ref, a_ref, b_ref, o_ref):
        o_ref[...] = (a_ref[...] + b_ref[...]).astype(BF16)

    tile = pl.BlockSpec((1, t, width), lambda j, i, c_ref: (j, i, 0))
    if _half_axis(g.shape[1:]) == 0:
        mine = pl.BlockSpec((1, t, width), lambda j, i, c_ref: (j, c_ref[0] * n + i, 0))
    else:
        mine = pl.BlockSpec((1, t, width), lambda j, i, c_ref: (j, i, c_ref[0]))
    return pl.pallas_call(
        body, name=name,
        grid_spec=pltpu.PrefetchScalarGridSpec(num_scalar_prefetch=1, grid=(N_CHIPS, n), in_specs=[mine, tile], out_specs=tile),
        out_shape=jax.ShapeDtypeStruct(theirs.shape, BF16), compiler_params=_params(("parallel", "parallel")),
    )(core, g, theirs)


def _chip_sum(part, others, chip, name, after=None):
    _, half, width = part.shape
    t = _half_tile(half, width)

    def body(s_ref, mine, p0, p1, p2, *rest):
        o_ref = rest[-1]
        o_ref[...] = ((mine[0].astype(F32) + p0[0].astype(F32)) + p1[0].astype(F32)) + p2[0].astype(F32)

    return pl.pallas_call(
        body, name=name,
        grid_spec=pltpu.PrefetchScalarGridSpec(
            num_scalar_prefetch=1, grid=(half // t,),
            in_specs=[pl.BlockSpec((1, t, width), lambda i, s_ref: (s_ref[0], i, 0))]
            + [pl.BlockSpec((1, t, width), lambda i, s_ref, j=j: (j, i, 0)) for j in range(3)] + [pl.BlockSpec(memory_space=pl.ANY)] * (after is not None),
            out_specs=pl.BlockSpec((t, width), lambda i, s_ref: (i, 0))),
        out_shape=jax.ShapeDtypeStruct((half, width), F32), compiler_params=_params(("parallel",)),
    )(chip, part, others, others, others, *([after] if after is not None else []))


EARLY = ("w_in", "w_uq", "w_ukv")
LATE = ("w_o_mla", "w_o_dil", "w_out", "w_ff1", "w_ff2")


def _sum_small(vals):
    n_in = len(vals)
    n_rows = sum(a.shape[0] * a.shape[1] // LANES for a in vals)
    pad_rows = -(-n_rows // 8) * 8

    def chunks(refs):
        return [(ref, a, j) for ref in refs for a in range(ref.shape[0]) for j in range(ref.shape[1] // LANES)]

    def body(*refs):
        ins, outs, (buf, send, recv) = refs[:n_in], refs[n_in:2 * n_in], refs[2 * n_in:]
        x, y, c, _ = _place()
        me = 4 * x + 2 * y + c
        for r, (ref, a, j) in enumerate(chunks(ins)):
            buf[me, r:r + 1, :] = ref[a:a + 1, j * LANES:(j + 1) * LANES]
        if pad_rows > n_rows:
            buf[me, n_rows:pad_rows, :] = jnp.zeros((pad_rows - n_rows, LANES), F32)
        flips = [(a, b, d) for a in (0, 1) for b in (0, 1) for d in (0, 1)][1:]
        peers = [((1 - x if a else x), (1 - y if b else y), (1 - c if d else c)) for a, b, d in flips]
        copies = [_remote(buf.at[me], buf.at[me], send.at[r], recv.at[r], peer) for r, peer in enumerate(peers)]
        for cp in copies:
            cp.start()
        for r, (px, py, pc) in enumerate(peers):
            _remote(buf.at[me], buf.at[4 * px + 2 * py + pc], send.at[r], recv.at[r], (px, py, pc)).wait_recv()
        for cp in copies:
            cp.wait_send()
        total = buf[0]
        for k in range(1, N_DEV):
            total = total + buf[k]
        for r, (ref, a, j) in enumerate(chunks(outs)):
            ref[a:a + 1, j * LANES:(j + 1) * LANES] = total[r:r + 1, :]

    vmem = pl.BlockSpec(memory_space=pltpu.VMEM)
    return pl.pallas_call(
        body, name="sum_small", in_specs=[vmem] * n_in, out_specs=[vmem] * n_in,
        out_shape=[jax.ShapeDtypeStruct(a.shape, F32) for a in vals],
        scratch_shapes=[pltpu.VMEM((N_DEV, pad_rows, LANES), F32), pltpu.SemaphoreType.DMA((N_DEV - 1,)), pltpu.SemaphoreType.DMA((N_DEV - 1,))],
        compiler_params=SIDE_EFFECTS,
    )(*vals)


def _adam_math(w, g, m, v):
    nm = B1 * m + (1.0 - B1) * g
    nv = B2 * v + (1.0 - B2) * (g * g)
    m_hat = nm / (1.0 - B1 ** ADAM_STEP)
    v_hat = nv / (1.0 - B2 ** ADAM_STEP)
    return -LR * (m_hat / (jnp.sqrt(v_hat) + ADAM_EPS) + WD * w), nm, nv


def _adamw_big(w, mine, theirs, m, v, core, name, side_by_side=False):
    rows, width = w.shape
    if side_by_side:
        t = next(c for c in (152, 96, 64, 32, 16, 8) if rows % c == 0)
        hb = None
        half_spec = pl.BlockSpec((t, width // 2), lambda i, c_ref: (i, 0))
    else:
        t = next(c for c in (256, 128, 64, 32, 16, 8) if (rows // 2) % c == 0)
        hb = rows // 2 // t
        half_spec = pl.BlockSpec((t, width), lambda i, c_ref: (i % hb, 0))

    def body(c_ref, w_ref, a_ref, b_ref, m_ref, v_ref, g_ref, d_ref, nm_ref, nv_ref):
        south = c_ref[0] == 0
        if side_by_side:
            g = jnp.where(south, jnp.concatenate([a_ref[...], b_ref[...]], axis=1), jnp.concatenate([b_ref[...], a_ref[...]], axis=1))
        else:
            g = jnp.where((pl.program_id(0) < hb) == south, a_ref[...], b_ref[...])
        g_ref[...] = g
        d_ref[...], nm_ref[...], nv_ref[...] = _adam_math(w_ref[...], g, m_ref[...], v_ref[...])

    spec = pl.BlockSpec((t, width), lambda i, c_ref: (i, 0))
    return pl.pallas_call(
        body, name=name,
        grid_spec=pltpu.PrefetchScalarGridSpec(num_scalar_prefetch=1, grid=(rows // t,),
                                               in_specs=[spec, half_spec, half_spec, spec, spec], out_specs=[spec] * 4),
        out_shape=[jax.ShapeDtypeStruct(w.shape, F32)] * 4, compiler_params=_params(("parallel",)),
    )(core, w, mine, theirs, m, v)


def _adamw_small(ws, gs, ms, vs):
    n = len(ws)

    def body(*refs):
        for t in range(n):
            w_ref, g_ref, m_ref, v_ref = (refs[k * n + t] for k in range(4))
            d, nm, nv = _adam_math(w_ref[...], g_ref[...], m_ref[...], v_ref[...])
            refs[4 * n + t][...] = d
            refs[5 * n + t][...] = nm
            refs[6 * n + t][...] = nv

    vmem = pl.BlockSpec(memory_space=pltpu.VMEM)
    res = pl.pallas_call(body, name="adamw_small", in_specs=[vmem] * (4 * n), out_specs=[vmem] * (3 * n),
                         out_shape=[jax.ShapeDtypeStruct(a.shape, F32) for a in ws] * 3)(*ws, *gs, *ms, *vs)
    return res[:n], res[n:2 * n], res[2 * n:]


def kernel(x, w_in, b_gate, g_q_a, w_uq, g_kv_a, w_ukv, w_o_mla, w_o_dil, w_out, ln1_g, ln1_b, w_ff1, w_ff2, ln2_g, ln2_b, loss_target, m_w_in, m_b_gate, m_g_q_a, m_w_uq, m_g_kv_a, m_w_ukv, m_w_o_mla, m_w_o_dil, m_w_out, m_ln1_g, m_ln1_b, m_w_ff1, m_w_ff2, m_ln2_g, m_ln2_b, v_w_in, v_b_gate, v_g_q_a, v_w_uq, v_g_kv_a, v_w_ukv, v_w_o_mla, v_w_o_dil, v_w_out, v_ln1_g, v_ln1_b, v_w_ff1, v_w_ff2, v_ln2_g, v_ln2_b):
    order = ("w_in", "b_gate", "g_q_a", "w_uq", "g_kv_a", "w_ukv", "w_o_mla", "w_o_dil", "w_out", "ln1_g", "ln1_b", "w_ff1", "w_ff2", "ln2_g", "ln2_b")
    w = dict(w_in=w_in, b_gate=b_gate, g_q_a=g_q_a, w_uq=w_uq, g_kv_a=g_kv_a, w_ukv=w_ukv, w_o_mla=w_o_mla, w_o_dil=w_o_dil, w_out=w_out,
             ln1_g=ln1_g, ln1_b=ln1_b, w_ff1=w_ff1, w_ff2=w_ff2, ln2_g=ln2_g, ln2_b=ln2_b)
    m = dict(w_in=m_w_in, b_gate=m_b_gate, g_q_a=m_g_q_a, w_uq=m_w_uq, g_kv_a=m_g_kv_a, w_ukv=m_w_ukv, w_o_mla=m_w_o_mla, w_o_dil=m_w_o_dil,
             w_out=m_w_out, ln1_g=m_ln1_g, ln1_b=m_ln1_b, w_ff1=m_w_ff1, w_ff2=m_w_ff2, ln2_g=m_ln2_g, ln2_b=m_ln2_b)
    v = dict(w_in=v_w_in, b_gate=v_b_gate, g_q_a=v_g_q_a, w_uq=v_w_uq, g_kv_a=v_g_kv_a, w_ukv=v_w_ukv, w_o_mla=v_w_o_mla, w_o_dil=v_w_o_dil,
             w_out=v_w_out, ln1_g=v_ln1_g, ln1_b=v_ln1_b, w_ff1=v_w_ff1, w_ff2=v_w_ff2, ln2_g=v_ln2_g, ln2_b=v_ln2_b)
    chip = 2 * lax.axis_index("x") + lax.axis_index("y")
    south = (lax.axis_index("c") == 0).astype(F32)
    gate_w = D_MODEL // N_CHIPS

    core = lax.axis_index("c").astype(jnp.int32).reshape(1)
    turn = lambda n, a: a.T if n in TRANSPOSED else a
    early_shards = [turn(n, w[n][0]).astype(BF16) for n in EARLY]
    gathered = lambda group: [jax.ShapeDtypeStruct((N_CHIPS,) + s.shape, BF16) for s in group]
    e_sems, e_srcs, e_lands, e_token = _split_start("gather", early_shards, gathered(early_shards), "gather_early_start")
    e_token, late_f32, x_f32, positions = lax.optimization_barrier((e_token, [w[n][0] for n in LATE], x, jnp.arange(x.shape[1], dtype=F32)))
    late_shards, x_lo = [a.astype(BF16) for a in late_f32], x_f32.astype(BF16)
    rope = _rope_tables(positions)
    shards = dict(zip(EARLY + LATE, early_shards + late_shards))
    b_mine = lax.dynamic_update_slice(jnp.zeros((2, D_MODEL), F32), b_gate[0] * south, (0, chip * gate_w))
    b_full = _sum_all_devices(b_mine.reshape(-1, LANES), "gather_b_gate", [e_token, *late_shards, *rope, x_lo]).reshape(2, D_MODEL)
    e_srcs, e_lands = _split_wait("gather", e_sems, e_srcs, e_lands, b_full, "gather_early_wait")
    e_forward = _split_start("forward", e_srcs, None, "gather_early_forward_start", lands=e_lands)
    first = dict(zip(EARLY, _split_wait("forward", *e_forward[:3], e_forward[-1], "gather_early_forward_wait")[1]))
    g_sems, g_srcs, g_lands, g_token = _split_start("gather", late_shards, gathered(late_shards), "gather_late_start", after=first[EARLY[-1]])
    tables = _attention_tables(x.shape[1])

    sent = {}

    def late_arrived(after):
        srcs, lands = _split_wait("gather", g_sems, g_srcs, g_lands, after, "gather_late_wait")
        sent["forward"] = _split_start("forward", srcs, None, "gather_late_forward_start", lands=lands)
        return sent["forward"][-1]

    def late_weights(after):
        return dict(zip(LATE, _split_wait("forward", *sent["forward"][:3], after, "gather_late_forward_wait")[1]))

    exchange_shapes = lambda parts: [jax.ShapeDtypeStruct((3,) + p.shape[1:], BF16) for p in parts]

    def early_grads(grads_late):
        gs = [grads_late[n] for n in LATE]
        shapes = [jax.ShapeDtypeStruct((N_CHIPS,) + _half_shape(g.shape[1:]), F32) for g in gs]
        sent["pair"] = _split_start("pair", gs, shapes, "pair_split_late_start")
        return sent["pair"][-1]

    def early_grads_go(after):
        gs, theirs = _split_wait("pair", *sent["pair"][:3], after, "pair_split_late_wait")
        parts = [_pair_add(g, th, core, "pair_add_" + n) for g, th, n in zip(gs, theirs, LATE)]
        sent["late"] = _split_start("scatter", parts, exchange_shapes(parts), "exchange_late_start")
        return sent["late"][-1]

    def first_grad(g):
        sent["first"] = _split_start("pair", [g], [jax.ShapeDtypeStruct((N_CHIPS,) + _half_shape(g.shape[1:]), F32)], "pair_split_w_in_start")
        return sent["first"][-1]

    def last_grads(grads_early):
        rest = [grads_early[n] for n in EARLY[1:]]
        gs, theirs = _split_wait("pair", *sent["first"][:3], rest[-1], "pair_split_w_in_wait")
        gs, theirs = list(gs) + rest, list(theirs) + list(_pair_split(rest, "pair_split_early"))
        parts = [_pair_add(g, th, core, "pair_add_" + n) for g, th, n in zip(gs, theirs, EARLY)]
        sent["early"] = _split_start("scatter", parts, exchange_shapes(parts), "exchange_early_start")
        return sent["early"][-1]

    loss_part, grad_x, grads = _local_step(x, loss_target, first, b_full, g_q_a, g_kv_a, ln1_g, ln1_b, ln2_g, ln2_b, token=g_token,
                                           late_arrived=late_arrived, late_weights=late_weights, early_grads=early_grads, early_grads_go=early_grads_go,
                                           first_grad=first_grad, last_grads=last_grads, tables=tables, rope=rope, x_lo=x_lo)

    g_out, delta, new_m, new_v = {}, {}, {}, {}
    chip1 = chip.astype(jnp.int32).reshape(1)

    def sum_and_send(names, parts, others, tag):
        totals = [_chip_sum(p, o, chip1, "chip_sum_" + n) for n, p, o in zip(names, parts, others)]
        return _split_start("join", totals, [jax.ShapeDtypeStruct(t.shape, F32) for t in totals], "pair_join_" + tag + "_start")

    def adam(names, joined, after, tag):
        totals, halves = _split_wait("join", *joined[:3], after, "pair_join_" + tag + "_wait")
        for n, mine, theirs in zip(names, totals, halves):
            res = _adamw_big(turn(n, w[n][0]), mine, theirs, turn(n, m[n][0]), turn(n, v[n][0]), core, "adamw_" + n,
                             side_by_side=mine.shape[0] == shards[n].shape[0])
            g_out[n], delta[n], new_m[n], new_v[n] = (turn(n, r) for r in res)

    late_joined = sum_and_send(LATE, *_split_wait("scatter", *sent["late"][:3], grad_x, "exchange_late_wait"), "late")
    early_joined = sum_and_send(EARLY, *_split_wait("scatter", *sent["early"][:3], late_joined[-1], "exchange_early_wait"), "early")
    small_names = [name for name, _ in SMALL]
    sums = _sum_small([grads[name] for name in small_names] + [loss_part])
    loss = sums[-1][0, 0]
    g_small = dict(zip(small_names, sums))
    g_small["b_gate"] = lax.dynamic_slice(g_small["b_gate"], (0, chip * gate_w), (2, gate_w))
    flat = lambda a: a.reshape(-1, a.shape[-1])
    res = _adamw_small(*[[flat(d[name]) for name in small_names] for d in (w, g_small, m, v)])
    g_out.update(g_small)
    for d, r in zip((delta, new_m, new_v), res):
        d.update(zip(small_names, r))
    adam(LATE, late_joined, res[0][0], "late")
    adam(EARLY, early_joined, delta[LATE[-1]], "early")

    lead = lambda d: [d[name].reshape(w[name].shape) for name in order]
    return (loss, grad_x, *lead(g_out), *lead(delta), *lead(new_m), *lead(new_v))
```

```python
import functools
import math

import jax
import jax.numpy as jnp
import numpy as np
from jax import lax
from jax.experimental import pallas as pl
from jax.experimental.pallas import tpu as pltpu

F32 = jnp.float32
BF16 = jnp.bfloat16
MESH = pl.DeviceIdType.MESH

D_MODEL = 1024
N_HEADS = 8
LANES = 128
NOPE, ROPE, V_DIM = 64, 32, 64
MLA_QK = NOPE + ROPE
Q_LORA, KV_LORA = 384, 256
DIL_DIM = 64
DIL_PATTERNS = ((128, 1), (512, 4), (2048, 16))
D_FF = 4096
N_CHIPS = 4
N_DEV = 8
IN_WIDTH = 4256
LN_EPS, RMS_EPS = 1e-5, 1e-6
NEG = -1e30
LOG2E, LN2 = 1.4426950408889634, 0.6931471805599453
ALPHA = 2.0 ** 0.25
ROPE_THETA = 10000.0
LR, B1, B2, ADAM_EPS, WD, ADAM_STEP = 0.001, 0.9, 0.999, 1e-8, 0.01, 10

P_LORA, P_KR, P_GATE, P_HALF = 0, 640, 1024, 3072
DIL_GROUP = 4 * LANES
P_DIL = N_HEADS // 2 * DIL_GROUP
LORA_W = Q_LORA + KV_LORA
KR_LANE = NOPE

ATT_T = 512
ROW_T = 512
ACC_COLS = 256
VMEM_LIMIT = 56 * 1024 * 1024

NN = (((1,), (0,)), ((), ()))
NT = (((1,), (1,)), ((), ()))
TN = (((0,), (0,)), ((), ()))


def _params(sem=None, **kw):
    return pltpu.CompilerParams(dimension_semantics=sem, vmem_limit_bytes=VMEM_LIMIT, **kw)


def _matmul(a, b, *, mode, name, tm, tn, tk, out_dtypes=(F32,), extras=(), row_extras=(), epilogue=None, b_shards=False, out_shards=False, after=None):
    pieces = list(a) if isinstance(a, (list, tuple)) else [a]
    n_pc = len(pieces)
    a_shape = (pieces[0].shape[0], sum(p.shape[1] for p in pieces))
    if b_shards:
        n_sh, rows_b, cols_b = b.shape
        b_shape = (rows_b, n_sh * cols_b)
    else:
        b_shape = b.shape
    if mode == "nn":
        (m, k), (k2, n) = a_shape, b_shape
    elif mode == "nt":
        (m, k), (n, k2) = a_shape, b_shape
    else:
        (k, m), (k2, n) = a_shape, b_shape
    assert k == k2, (a_shape, b.shape, mode)
    tm, tn, tk = min(tm, m), min(tn, n), min(tk, k)
    assert m % tm == 0 and n % tn == 0 and k % tk == 0, (name, m, n, k, tm, tn, tk)
    nk = k // tk
    n_ex, n_out = len(extras) + len(row_extras), len(out_dtypes)
    n_in = n_pc + 1 + n_ex + (after is not None)
    dims = {"nn": NN, "nt": NT, "tn": TN}[mode]
    col_tile = tm if mode == "tn" else tk
    blocks = [p.shape[1] // col_tile for p in pieces]
    firsts = [sum(blocks[:p]) for p in range(n_pc)]
    assert all(p.shape[1] % col_tile == 0 for p in pieces), (name, col_tile)

    def body(*refs):
        a_refs, b_ref = refs[:n_pc], refs[n_pc]
        ex_refs = refs[n_pc + 1:n_pc + 1 + n_ex]
        out_refs = refs[n_in:n_in + n_out]

        def finish(acc):
            outs = epilogue(acc, *[r[...] for r in ex_refs]) if epilogue is not None else (acc,)
            for r, o in zip(out_refs, outs):
                r[...] = o.astype(r.dtype)

        kk = pl.program_id(2)

        def step(a_ref):
            if nk == 1:
                finish(lax.dot_general(a_ref[...].astype(BF16), b_ref[...].astype(BF16), dims, preferred_element_type=F32))
                return
            acc_ref = refs[-1]

            @pl.when(kk == 0)
            def _():
                acc_ref[...] = jnp.zeros_like(acc_ref)

            for c in range(0, tn, ACC_COLS):
                b_blk = b_ref[c:c + ACC_COLS, :] if mode == "nt" else b_ref[:, c:c + ACC_COLS]
                acc_ref[:, c:c + ACC_COLS] += lax.dot_general(a_ref[...].astype(BF16), b_blk.astype(BF16), dims, preferred_element_type=F32)

            @pl.when(kk == nk - 1)
            def _():
                finish(acc_ref[...])

        if n_pc == 1:
            step(a_refs[0])
        else:
            at = pl.program_id(0) if mode == "tn" else kk
            for p in range(n_pc):
                pl.when(jnp.logical_and(at >= firsts[p], at < firsts[p] + blocks[p]))(functools.partial(step, a_refs[p]))

    def a_spec_of(p):
        if n_pc == 1:
            return pl.BlockSpec((tk, tm), lambda i, j, kk: (kk, i)) if mode == "tn" else pl.BlockSpec((tm, tk), lambda i, j, kk: (i, kk))
        col = lambda at: jnp.clip(at - firsts[p], 0, blocks[p] - 1)
        mine = lambda at: jnp.logical_and(at >= firsts[p], at < firsts[p] + blocks[p])
        if mode == "tn":
            return pl.BlockSpec((tk, tm), lambda i, j, kk: (jnp.where(mine(i), kk, 0), col(i)))
        return pl.BlockSpec((tm, tk), lambda i, j, kk: (i, col(kk)))

    b_spec = {"nn": pl.BlockSpec((tk, tn), lambda i, j, kk: (kk, j)),
              "nt": pl.BlockSpec((tn, tk), lambda i, j, kk: (j, kk)),
              "tn": pl.BlockSpec((tk, tn), lambda i, j, kk: (kk, j))}[mode]
    tile = pl.BlockSpec((tm, tn), lambda i, j, kk: (i, j))
    out_spec, out_dims = tile, (m, n)
    if b_shards and mode == "nn":
        per = cols_b // tn
        b_spec = pl.BlockSpec((None, tk, tn), lambda i, j, kk: (j // per, kk, j % per))
    elif b_shards:
        assert mode == "nt"
        per = cols_b // tk
        b_spec = pl.BlockSpec((None, tn, tk), lambda i, j, kk: (kk // per, j, kk % per))
    if out_shards:
        assert not extras and epilogue is None
        per_out = n // N_CHIPS // tn
        out_spec = pl.BlockSpec((None, tm, tn), lambda i, j, kk: (j // per_out, i, j % per_out))
        out_dims = (N_CHIPS, m, n // N_CHIPS)
    outs = pl.pallas_call(
        body, name=name,
        grid=(m // tm, n // tn, nk),
        in_specs=[a_spec_of(p) for p in range(n_pc)] + [b_spec] + [tile] * len(extras) + [pl.BlockSpec((1, tn), lambda i, j, kk: (0, j))] * len(row_extras) + [pl.BlockSpec(memory_space=pl.ANY)] * (after is not None),
        out_specs=[out_spec] * n_out,
        out_shape=[jax.ShapeDtypeStruct(out_dims, dt) for dt in out_dtypes],
        scratch_shapes=[pltpu.VMEM((tm, tn), F32)] if nk > 1 else [],
        compiler_params=_params(("parallel", "parallel", "arbitrary")),
    )(*pieces, b, *extras, *row_extras, *([after] if after is not None else []))
    return outs[0] if n_out == 1 else outs


def _rowwise(fn, *, name, rows, seq, ins, outs, sums=()):
    tm = min(ROW_T, seq)
    n_pos = seq // tm
    n_in, n_out, n_sum = len(ins), len(outs), len(sums)

    def body(*refs):
        vals = fn(*[r[...] for r in refs[:n_in]])
        for r, v in zip(refs[n_in:n_in + n_out], vals[:n_out]):
            r[...] = v.astype(r.dtype)
        first = pl.program_id(0) == 0
        for r, v in zip(refs[n_in + n_out:], vals[n_out:]):
            @pl.when(first)
            def _(r=r, v=v):
                r[...] = v

            @pl.when(jnp.logical_not(first))
            def _(r=r, v=v):
                r[...] += v

    def spec(arr, width, col, kind):
        if kind == "row":
            return pl.BlockSpec((tm, width), lambda i, col=col: (i, col))
        if kind == "pos":
            return pl.BlockSpec((tm, width), lambda i, col=col: (i % n_pos, col))
        return pl.BlockSpec(arr.shape, lambda i: (0,) * arr.ndim)

    res = pl.pallas_call(
        body, name=name,
        grid=(rows // tm,),
        in_specs=[spec(*t) for t in ins],
        out_specs=[pl.BlockSpec((tm, w), lambda i: (i, 0)) for w, _ in outs]
        + [pl.BlockSpec((1, w), lambda i: (0, 0)) for w in sums],
        out_shape=[jax.ShapeDtypeStruct((rows, w), dt) for w, dt in outs]
        + [jax.ShapeDtypeStruct((1, w), F32) for w in sums],
        compiler_params=_params(("arbitrary",)),
    )(*[t[0] for t in ins])
    return res


def _colsum(v):
    return jnp.sum(v, axis=0, keepdims=True)


def _rope_fwd(t, c, s_up, s_dn):
    return t * c + pltpu.roll(t, LANES - 16, 1) * s_up + pltpu.roll(t, 16, 1) * s_dn


def _rope_bwd(d, c, s_up, s_dn):
    return d * c + pltpu.roll(d * s_up, 16, 1) + pltpu.roll(d * s_dn, LANES - 16, 1)


def _rope_tables(positions):
    seq, half = positions.shape[0], ROPE // 2
    inv = jnp.power(ROPE_THETA, -jnp.arange(half, dtype=F32) / half)
    ang = positions[:, None] * inv[None, :]
    cos, sin = jnp.cos(ang), jnp.sin(ang)
    zeros = jnp.zeros((seq, half), F32)
    lo, hi = jnp.ones((seq, KR_LANE), F32), jnp.ones((seq, LANES - KR_LANE - ROPE), F32)
    c = jnp.concatenate([lo, cos, cos, hi], axis=1)
    c_rope_only = jnp.concatenate([0 * lo, cos, cos, 0 * hi], axis=1)
    s_up = jnp.concatenate([0 * lo, -sin, zeros, 0 * hi], axis=1)
    s_dn = jnp.concatenate([0 * lo, zeros, sin, 0 * hi], axis=1)
    return c, s_up, s_dn, c_rope_only


def _rms(x, g):
    r = lax.rsqrt(jnp.mean(x * x, axis=1, keepdims=True) + RMS_EPS)
    return x * r * g


def _rms_bwd(x, g, dy):
    r = lax.rsqrt(jnp.mean(x * x, axis=1, keepdims=True) + RMS_EPS)
    xh = x * r
    dxh = dy * g
    dx = r * (dxh - xh * jnp.mean(dxh * xh, axis=1, keepdims=True))
    return dx, _colsum(dy * xh)


def _ln_stats(x):
    mu = jnp.mean(x, axis=1, keepdims=True)
    xc = x - mu
    r = lax.rsqrt(jnp.mean(xc * xc, axis=1, keepdims=True) + LN_EPS)
    return xc * r, r


def _ln_bwd(xh, r, g, dy):
    dxh = dy * g
    dx = r * (dxh - jnp.mean(dxh, axis=1, keepdims=True) - xh * jnp.mean(dxh * xh, axis=1, keepdims=True))
    return dx, _colsum(dy * xh), _colsum(dy)


def _table_specs(tables, sub):
    whole = lambda a: pl.BlockSpec(a.shape, lambda b, g: (0,) * a.ndim)
    if len(tables) == 1:
        return [whole(tables[0])]
    return [whole(tables[0]), whole(tables[1]), pl.BlockSpec((sub, 1, LANES), lambda b, g: (g, 0, 0))]


def _biased(s, table_refs, delta, head):
    if delta < table_refs[0].shape[0]:
        s = s + table_refs[0][delta]
    if len(table_refs) == 3:
        s = s - table_refs[2][head, 0:1, 0:1] * table_refs[1][delta]
    return s


def _lane_masks(sub):
    lane = lax.broadcasted_iota(jnp.int32, (1, LANES), 1)
    return [(lane // (LANES // sub) == a).astype(F32) for a in range(sub)]


def _attn_fwd(q, qb0, k, kb0, v, vb0, tables, scale, *, name, batch, seq, sub=1, stride=1, wide_qk=False, after=None):
    t = ATT_T
    nq = seq // t
    rows = batch * seq
    n_tab = len(tables)
    qk_w = sub * LANES if wide_qk else LANES

    def body(q_ref, k_ref, v_ref, *rest):
        table_refs = rest[:n_tab]
        o_ref, lse_ref, vtb = rest[n_tab + (after is not None):][:3]
        per_head = rest[n_tab + (after is not None) + 3:]
        qbs, kbs = per_head[:sub], per_head[sub:]
        masks = _lane_masks(sub)
        for a in range(sub):
            lanes = slice(a * LANES, (a + 1) * LANES) if wide_qk else slice(None)
            qa = q_ref[:, lanes]
            qbs[a][...] = (qa.astype(F32) * masks[a]).astype(BF16) if sub > 1 and not wide_qk else qa.astype(BF16)
            if wide_qk or a == 0:
                kbs[a][...] = k_ref[:, lanes].astype(BF16)
        vtb[...] = v_ref[...].astype(F32).T.astype(BF16)
        for i in range(nq):
            out_t = None
            for a in range(sub):
                qt, kb = qbs[a][i * t:(i + 1) * t, :], kbs[a if wide_qk else 0]
                logits = [_biased(lax.dot_general(kb[j * t:(j + 1) * t, :], qt, NT, preferred_element_type=F32) * (scale * LOG2E), table_refs, i - j, a)
                          for j in range(i + 1)]
                m = jnp.max(functools.reduce(jnp.maximum, logits), axis=0, keepdims=True)
                ps = [jnp.exp2(s - m) for s in logits]
                l = jnp.sum(functools.reduce(jnp.add, ps), axis=0, keepdims=True)
                acc = functools.reduce(jnp.add, [lax.dot_general(vtb[:, j * t:(j + 1) * t], p.astype(BF16), NN, preferred_element_type=F32)
                                                 for j, p in enumerate(ps)])
                part = acc / l if sub == 1 else (acc / l) * masks[a].T
                out_t = part if out_t is None else out_t + part
                lse_ref[i * t:(i + 1) * t, a * LANES:(a + 1) * LANES] = jnp.broadcast_to((m + jnp.log2(l)) * LN2, (LANES, t)).T
            o_ref[i * t:(i + 1) * t, :] = out_t.T

    slab = lambda b0, step, width=LANES: pl.BlockSpec((seq, width), lambda b, g: (b, b0 + step * g))
    groups = N_HEADS // sub
    n_k = sub if wide_qk else 1
    return pl.pallas_call(
        body, name=name,
        grid=(batch, groups),
        in_specs=[slab(qb0, stride, qk_w), slab(kb0, stride, qk_w), slab(vb0, stride)] + _table_specs(tables, sub)
        + [pl.BlockSpec(memory_space=pl.ANY)] * (after is not None),
        out_specs=[slab(0, 1), slab(0, 1, sub * LANES)],
        out_shape=[jax.ShapeDtypeStruct((rows, groups * LANES), F32), jax.ShapeDtypeStruct((rows, N_HEADS * LANES), F32)],
        scratch_shapes=[pltpu.VMEM((LANES, seq), BF16)] + [pltpu.VMEM((seq, LANES), BF16)] * (sub + n_k),
        compiler_params=_params(("arbitrary", "arbitrary")),
    )(q, k, v, *tables, *([after] if after is not None else []))


def _attn_bwd(q, qb0, k, kb0, v, vb0, o, do, lse, tables, scale, *, name, batch, seq, out_dtype, sub=1, stride=1, wide_qk=False, after=None):
    t = ATT_T
    nq = seq // t
    rows = batch * seq
    n_tab = len(tables)
    groups = N_HEADS // sub
    packed = sub > 1 and not wide_qk
    n_out = 1 if packed else 3
    n_k = sub if wide_qk else 1
    qk_w = sub * LANES if wide_qk else LANES

    def body(q_ref, k_ref, v_ref, o_ref, do_ref, lse_ref, *rest):
        table_refs = rest[:n_tab]
        rest = rest[n_tab + (after is not None):]
        out_refs, (vb, dva), rest = rest[:n_out], rest[n_out:n_out + 2], rest[n_out + 2:]
        kbs, dkas, rest = rest[:n_k], rest[n_k:2 * n_k], rest[2 * n_k:]
        qbs, dobs, qtbs, dotbs = (rest[g * sub:(g + 1) * sub] for g in range(4))
        masks = _lane_masks(sub)
        vb[...] = v_ref[...].astype(BF16)
        for a in range(sub):
            lanes = slice(a * LANES, (a + 1) * LANES) if wide_qk else slice(None)
            qa = q_ref[:, lanes].astype(F32) * masks[a] if packed else q_ref[:, lanes].astype(F32)
            doa = do_ref[...] * masks[a] if sub > 1 else do_ref[...]
            qbs[a][...] = qa.astype(BF16)
            dobs[a][...] = doa.astype(BF16)
            qtbs[a][...] = qa.T.astype(BF16)
            dotbs[a][...] = doa.T.astype(BF16)
            if wide_qk or a == 0:
                kbs[a][...] = k_ref[:, lanes].astype(BF16)
        first_k, first_v = [[True] * nq for _ in range(n_k)], [True] * nq
        for i in range(nq):
            at = slice(i * t, (i + 1) * t)
            dq_all = None
            for a in range(sub):
                qt, dot, kb, dka = qbs[a][at, :], dobs[a][at, :], kbs[a if wide_qk else 0], dkas[a if wide_qk else 0]
                lse_t = lse_ref[at, a * LANES:a * LANES + 1] * LOG2E
                od = o_ref[at, :] * do_ref[at, :]
                delta = jnp.sum(od * masks[a] if sub > 1 else od, axis=1, keepdims=True)
                dq = None
                for j in range(i + 1):
                    kat = slice(j * t, (j + 1) * t)
                    kt, vt = kb[kat, :], vb[kat, :]
                    p = jnp.exp2(_biased(lax.dot_general(qt, kt, NT, preferred_element_type=F32) * (scale * LOG2E), table_refs, i - j, a) - lse_t)
                    dp = lax.dot_general(dot, vt, NT, preferred_element_type=F32)
                    ds = (p * (dp - delta) * scale).astype(BF16)
                    dk_part = lax.dot_general(qtbs[a][:, at], ds, NN, preferred_element_type=F32)
                    dv_part = lax.dot_general(dotbs[a][:, at], p.astype(BF16), NN, preferred_element_type=F32)
                    firsts = first_k[a if wide_qk else 0]
                    if firsts[j]:
                        dka[:, kat] = dk_part
                        firsts[j] = False
                    else:
                        dka[:, kat] += dk_part
                    if first_v[j]:
                        dva[:, kat] = dv_part
                        first_v[j] = False
                    else:
                        dva[:, kat] += dv_part
                    dq_part = lax.dot_general(ds, kt, NN, preferred_element_type=F32)
                    dq = dq_part if dq is None else dq + dq_part
                if wide_qk:
                    out_refs[0][at, a * LANES:(a + 1) * LANES] = dq.astype(out_refs[0].dtype)
                else:
                    dq = dq * masks[a] if sub > 1 else dq
                    dq_all = dq if dq_all is None else dq_all + dq
            if not wide_qk:
                out_refs[0][at, 0:LANES] = dq_all.astype(out_refs[0].dtype)
        if packed:
            out_refs[0][:, LANES:2 * LANES] = dkas[0][...].T.astype(out_refs[0].dtype)
            out_refs[0][:, 2 * LANES:3 * LANES] = dva[...].T.astype(out_refs[0].dtype)
            out_refs[0][:, 3 * LANES:] = jnp.zeros((seq, LANES), out_refs[0].dtype)
        else:
            for a in range(n_k):
                out_refs[1][:, a * LANES:(a + 1) * LANES] = dkas[a][...].T.astype(out_refs[1].dtype)
            out_refs[2][...] = dva[...].T.astype(out_refs[2].dtype)

    slab = lambda b0, step, width=LANES: pl.BlockSpec((seq, width), lambda b, g: (b, b0 + step * g))
    if packed:
        out_specs = [slab(0, 1, 4 * LANES)]
        out_shape = [jax.ShapeDtypeStruct((rows, groups * 4 * LANES), out_dtype)]
    else:
        out_specs = [slab(0, 1, qk_w), slab(0, 1, qk_w), slab(0, 1)]
        out_shape = [jax.ShapeDtypeStruct((rows, N_HEADS * LANES), out_dtype)] * 2 + [jax.ShapeDtypeStruct((rows, groups * LANES), out_dtype)]
    res = pl.pallas_call(
        body, name=name,
        grid=(batch, groups),
        in_specs=[slab(qb0, stride, qk_w), slab(kb0, stride, qk_w), slab(vb0, stride), slab(0, 1), slab(0, 1), slab(0, 1, sub * LANES)]
        + _table_specs(tables, sub) + [pl.BlockSpec(memory_space=pl.ANY)] * (after is not None),
        out_specs=out_specs, out_shape=out_shape,
        scratch_shapes=[pltpu.VMEM((seq, LANES), BF16), pltpu.VMEM((LANES, seq), F32)]
        + [pltpu.VMEM((seq, LANES), BF16)] * n_k + [pltpu.VMEM((LANES, seq), F32)] * n_k
        + [pltpu.VMEM((seq, LANES), BF16)] * (2 * sub) + [pltpu.VMEM((LANES, seq), BF16)] * (2 * sub),
        compiler_params=_params(("arbitrary", "arbitrary")),
    )(q, k, v, o, do, lse, *tables, *([after] if after is not None else []))
    return res[0] if packed else res


def _attention_tables(seq):
    n = seq // ATT_T
    pos = np.arange(ATT_T, dtype=np.int32)
    dist = np.arange(n, dtype=np.int32)[:, None, None] * ATT_T + pos[None, :, None] - pos[None, None, :]
    causal = np.where(dist[:1] >= 0, 0.0, NEG).astype(np.float32)
    count = np.zeros(dist.shape, np.float32)
    for window, dilation in DIL_PATTERNS:
        count += ((dist >= 0) & (dist <= window) & (dist % dilation == 0)).astype(np.float32)
    held = np.where(count > 0, np.log2(np.maximum(count, 1.0)), NEG).astype(np.float32)
    far = dist.astype(np.float32) * np.float32(LOG2E)
    slopes = np.asarray([2.0 ** (-8.0 * (i + 1) / N_HEADS) for i in range(N_HEADS)], np.float32)
    slopes = jnp.asarray(np.broadcast_to(slopes[:, None, None], (N_HEADS, 1, LANES)))
    flat = lambda a: jnp.asarray(np.ascontiguousarray(a))
    turned = lambda a: flat(np.swapaxes(a, 1, 2))
    return ((flat(causal),), (turned(causal),)), ((flat(held), flat(far), slopes), (turned(held), turned(far), slopes))


def _pad_heads(w, width):
    kdim, n = w.shape[0], w.shape[1] // width
    return jnp.pad(w.reshape(kdim, n, width), ((0, 0), (0, 0), (0, LANES - width))).reshape(kdim, n * LANES)


def _unpad_heads(w, width):
    kdim, n = w.shape[0], w.shape[1] // LANES
    return w.reshape(kdim, n, LANES)[:, :, :width].reshape(kdim, n * width)


def _pad_head_rows(w, width):
    n, kdim = w.shape[0] // width, w.shape[1]
    return jnp.pad(w.reshape(n, width, kdim), ((0, 0), (0, LANES - width), (0, 0))).reshape(n * LANES, kdim)


def _unpad_head_rows(w, width):
    n, kdim = w.shape[0] // LANES, w.shape[1]
    return w.reshape(n, LANES, kdim)[:, :width].reshape(n * width, kdim)


def _pad_w_in_t(wt):
    n_qkv, pair = 3 * N_HEADS * DIL_DIM, 2 * DIL_DIM
    zeros = lambda n: jnp.zeros((n, wt.shape[1]), wt.dtype)
    main = jnp.concatenate([wt[:LORA_W], zeros(KR_LANE), wt[LORA_W:LORA_W + ROPE], zeros(P_GATE - P_KR - KR_LANE - ROPE),
                            wt[LORA_W + ROPE + n_qkv:]], axis=0)
    qkv = wt[LORA_W + ROPE:LORA_W + ROPE + n_qkv].reshape(3, N_HEADS // 2, pair, wt.shape[1]).transpose(1, 0, 2, 3)
    dil = jnp.pad(qkv, ((0, 0), (0, 1), (0, 0), (0, 0))).reshape(P_DIL, wt.shape[1])
    return main, dil


def _unpad_w_in_t(gt):
    n_qkv, pair = 3 * N_HEADS * DIL_DIM, 2 * DIL_DIM
    moves = [(P_LORA, 0, LORA_W), (P_KR + KR_LANE, LORA_W, ROPE)]
    moves += [(P_HALF + (g * 4 + j) * pair, LORA_W + ROPE + (j * (N_HEADS // 2) + g) * pair, pair) for j in range(3) for g in range(N_HEADS // 2)]
    moves += [(P_GATE + r, LORA_W + ROPE + n_qkv + r, ROW_T) for r in range(0, P_HALF - P_GATE, ROW_T)]
    n = len(moves)

    def body(src, out, buf, sem_in, sem_out):
        ins = [pltpu.make_async_copy(src.at[pl.ds(a, k)], buf.at[pl.ds(b, k)], sem_in.at[i]) for i, (a, b, k) in enumerate(moves)]
        outs = [pltpu.make_async_copy(buf.at[pl.ds(b, k)], out.at[pl.ds(b, k)], sem_out.at[i]) for i, (a, b, k) in enumerate(moves)]
        for cp in ins:
            cp.start()
        for cp_in, cp_out in zip(ins, outs):
            cp_in.wait()
            cp_out.start()
        for cp in outs:
            cp.wait()

    return pl.pallas_call(
        body, name="unpad_d_w_in", in_specs=[pl.BlockSpec(memory_space=pl.ANY)], out_specs=pl.BlockSpec(memory_space=pl.ANY),
        out_shape=jax.ShapeDtypeStruct((IN_WIDTH, gt.shape[1]), gt.dtype),
        scratch_shapes=[pltpu.VMEM((IN_WIDTH, gt.shape[1]), gt.dtype), pltpu.SemaphoreType.DMA((n,)), pltpu.SemaphoreType.DMA((n,))],
        compiler_params=_params(),
    )(gt)


def _split_ukv(w):
    w3 = w.reshape(w.shape[0], N_HEADS, NOPE + V_DIM)
    return _pad_heads(w3[:, :, :NOPE].reshape(w.shape[0], -1), NOPE), w3[:, :, NOPE:].reshape(w.shape[0], -1)


def _merge_ukv(g_k, g_v):
    kdim = g_k.shape[0]
    k3 = _unpad_heads(g_k, NOPE).reshape(kdim, N_HEADS, NOPE)
    return jnp.concatenate([k3, g_v.reshape(kdim, N_HEADS, V_DIM)], axis=2).reshape(kdim, N_HEADS * (NOPE + V_DIM))


def _join_cols(w):
    return w.transpose(1, 0, 2).reshape(w.shape[1], N_CHIPS * w.shape[2])


def _split_cols(g):
    return g.reshape(g.shape[0], N_CHIPS, g.shape[1] // N_CHIPS).transpose(1, 0, 2)


def _local_step(x3, target3, wg, b_gate, g_q_a, g_kv_a, ln1_g, ln1_b, ln2_g, ln2_b, token=None, late_arrived=None, late_weights=None,
                early_grads=None, early_grads_go=None, first_grad=None, last_grads=None, tables=None, rope=None, x_lo=None):
    w_main_t, w_dil_t = _pad_w_in_t(wg["w_in"].reshape(IN_WIDTH, D_MODEL))
    w_uq_pt = _pad_head_rows(wg["w_uq"].reshape(N_HEADS * MLA_QK, Q_LORA), MLA_QK)
    w_ukv = _join_cols(wg["w_ukv"])
    batch, seq, _ = x3.shape
    rows = batch * seq
    x = x3.reshape(rows, D_MODEL)
    x_mm = x if x_lo is None else x_lo.reshape(rows, D_MODEL)
    target = target3.reshape(rows, D_MODEL)
    row = functools.partial(_rowwise, rows=rows, seq=seq)
    mm = _matmul

    w_uk_p, w_uv = _split_ukv(w_ukv)
    b0, b1 = b_gate[0:1], b_gate[1:2]
    rope_c, rope_up, rope_dn, rope_c_only = _rope_tables(jnp.arange(seq, dtype=F32)) if rope is None else rope
    (mla_bwd_tables, mla_fwd_tables), (dil_bwd_tables, dil_fwd_tables) = _attention_tables(seq) if tables is None else tables
    scale_mla, scale_dil = MLA_QK ** -0.5, DIL_DIM ** -0.5
    lora0, kr0, gate0 = P_LORA // LORA_W, P_KR // LANES, P_GATE // D_MODEL

    proj = mm(x_mm, w_main_t, mode="nt", name="proj", tm=1024, tn=1536, tk=1024, after=token)
    proj_d = mm(x_mm, w_dil_t, mode="nt", name="proj_dil", tm=1024, tn=1024, tk=1024, out_dtypes=(BF16,))

    def prep(lora, gq, gkv):
        return _rms(lora[:, :Q_LORA], gq), _rms(lora[:, Q_LORA:], gkv)

    qn, kvn = row(prep, name="mla_rms", ins=[(proj, LORA_W, lora0, "row"), (g_q_a, 0, 0, "full"), (g_kv_a, 0, 0, "full")],
                  outs=[(Q_LORA, BF16), (KV_LORA, BF16)])
    q_lin = mm(qn, w_uq_pt, mode="nt", name="q_up", tm=1024, tn=1024, tk=Q_LORA)
    k_lin = mm(kvn, w_uk_p, mode="nn", name="k_up", tm=1024, tn=1024, tk=KV_LORA)
    v_a = mm(kvn, w_uv, mode="nn", name="v_up", tm=1024, tn=1024, tk=KV_LORA, out_dtypes=(BF16,))

    def rope_qk(ql, kl, kr, c, up, dn):
        k_rot = _rope_fwd(kr, c, up, dn)
        qs = [_rope_fwd(ql[:, h * LANES:(h + 1) * LANES], c, up, dn) for h in range(N_HEADS)]
        ks = [kl[:, h * LANES:(h + 1) * LANES] + k_rot for h in range(N_HEADS)]
        return jnp.concatenate(qs, axis=1), jnp.concatenate(ks, axis=1)

    pos = lambda tab: (tab, LANES, 0, "pos")
    q_a, k_a = row(rope_qk, name="rope_qk",
                   ins=[(q_lin, D_MODEL, 0, "row"), (k_lin, D_MODEL, 0, "row"), (proj, LANES, kr0, "row"), pos(rope_c), pos(rope_up), pos(rope_dn)],
                   outs=[(N_HEADS * LANES, BF16), (N_HEADS * LANES, BF16)])
    o_a, lse_a = _attn_fwd(q_a, 0, k_a, 0, v_a, 0, mla_fwd_tables, scale_mla, name="mla_fwd", batch=batch, seq=seq, sub=2, wide_qk=True)
    arrived = None if late_arrived is None else late_arrived(o_a)
    o_b, lse_b = _attn_fwd(proj_d, 0, proj_d, 1, proj_d, 2, dil_fwd_tables, scale_dil, name="dil_fwd", batch=batch, seq=seq, sub=2, stride=4, after=arrived)
    late = wg if late_weights is None else late_weights(o_b)
    w_oa = _join_cols(late["w_o_mla"])
    w_ob = _join_cols(late["w_o_dil"])
    w_out, w_ff1, w_ff2 = late["w_out"].reshape(D_MODEL, D_MODEL), late["w_ff1"], late["w_ff2"].reshape(D_FF, D_MODEL)
    y_a = mm(o_a, w_oa, mode="nn", name="o_mla", tm=1024, tn=1024, tk=1024, out_dtypes=(BF16,))
    y_b = mm(o_b, w_ob, mode="nn", name="o_dil", tm=1024, tn=1024, tk=1024, out_dtypes=(BF16,))

    def gate(t0, t1, c0, c1, ya, yb):
        return (jax.nn.sigmoid(t0 + c0) * ya + jax.nn.sigmoid(t1 + c1) * yb,)

    gate_ins = [(proj, D_MODEL, gate0, "row"), (proj, D_MODEL, gate0 + 1, "row"), (b0, 0, 0, "full"), (b1, 0, 0, "full")]
    (u,) = row(gate, name="gate", ins=gate_ins + [(y_a, D_MODEL, 0, "row"), (y_b, D_MODEL, 0, "row")], outs=[(D_MODEL, BF16)])
    def ln1(mv, xv, g, b):
        r1 = ALPHA * xv + mv
        xh, _ = _ln_stats(r1)
        hv = xh * g + b
        return r1, hv, hv

    r1, h, h_b = mm(u, w_out, mode="nn", name="mix_ln1", tm=ROW_T, tn=D_MODEL, tk=D_MODEL, out_dtypes=(F32, F32, BF16),
                    extras=(x,), row_extras=(ln1_g, ln1_b), epilogue=ln1)

    def relu2(acc):
        r = jnp.maximum(acc, 0.0)
        return (r * r,)

    z = mm(h_b, w_ff1, mode="nn", name="ff1", tm=1024, tn=1024, tk=1024, out_dtypes=(BF16,), epilogue=relu2, b_shards=True)
    f = mm(z, w_ff2, mode="nn", name="ff2", tm=1024, tn=1024, tk=2048)

    def ln2_loss(hv, fv, tv, g, b):
        xh, r = _ln_stats(ALPHA * hv + fv)
        err = xh * g + b - tv
        dy = err * (1.0 / D_MODEL)
        dr2, dg, db = _ln_bwd(xh, r, g, dy)
        loss = jnp.sum(_colsum(err * err), axis=1, keepdims=True) * (0.5 / D_MODEL)
        return dr2, dr2, jnp.broadcast_to(loss, (1, LANES)), dg, db

    dr2, dr2_b, loss_l, d_ln2_g, d_ln2_b = row(
        ln2_loss, name="ln2_loss",
        ins=[(h, D_MODEL, 0, "row"), (f, D_MODEL, 0, "row"), (target, D_MODEL, 0, "row"), (ln2_g, 0, 0, "full"), (ln2_b, 0, 0, "full")],
        outs=[(D_MODEL, F32), (D_MODEL, BF16)], sums=[LANES, D_MODEL, D_MODEL])

    d_w_ff2 = mm(z, dr2_b, mode="tn", name="d_w_ff2", tm=1024, tn=1024, tk=2048)
    da = mm(dr2_b, w_ff2, mode="nt", name="d_ff_act", tm=1024, tn=1024, tk=1024, out_dtypes=(BF16,), extras=(z,),
            epilogue=lambda acc, zv: (acc * (2.0 * jnp.sqrt(zv.astype(F32))),))
    d_w_ff1 = mm(h_b, da, mode="tn", name="d_w_ff1", tm=1024, tn=1024, tk=2048, out_shards=True)
    dh = mm(da, w_ff1, mode="nt", name="d_h", tm=1024, tn=1024, tk=1024, extras=(dr2,), epilogue=lambda acc, rv: (acc + ALPHA * rv,), b_shards=True)

    def ln1_bwd(dhv, r1v, g):
        xh, r = _ln_stats(r1v)
        return _ln_bwd(xh, r, g, dhv)

    dr1, d_ln1_g, d_ln1_b = row(ln1_bwd, name="ln1_bwd", ins=[(dh, D_MODEL, 0, "row"), (r1, D_MODEL, 0, "row"), (ln1_g, 0, 0, "full")],
                                outs=[(D_MODEL, F32)], sums=[D_MODEL, D_MODEL])
    d_w_out = mm(u, dr1, mode="tn", name="d_w_out", tm=1024, tn=1024, tk=1024)
    du = mm(dr1, w_out, mode="nt", name="d_u", tm=1024, tn=1024, tk=1024, out_dtypes=(BF16,))

    def gate_bwd(t0, t1, c0, c1, ya, yb, duv):
        s0, s1 = jax.nn.sigmoid(t0 + c0), jax.nn.sigmoid(t1 + c1)
        dt0 = duv * ya * s0 * (1.0 - s0)
        dt1 = duv * yb * s1 * (1.0 - s1)
        return duv * s0, duv * s1, jnp.concatenate([dt0, dt1], axis=1), jnp.concatenate([_colsum(dt0), _colsum(dt1)], axis=1)

    dy_a, dy_b, d_gates, d_b_gate = row(
        gate_bwd, name="gate_bwd", ins=gate_ins + [(y_a, D_MODEL, 0, "row"), (y_b, D_MODEL, 0, "row"), (du, D_MODEL, 0, "row")],
        outs=[(D_MODEL, BF16), (D_MODEL, BF16), (2 * D_MODEL, BF16)], sums=[2 * D_MODEL])
    d_w_oa = mm(o_a, dy_a, mode="tn", name="d_w_o_mla", tm=1024, tn=1024, tk=1024)
    d_w_ob = mm(o_b, dy_b, mode="tn", name="d_w_o_dil", tm=1024, tn=1024, tk=1024)
    grads = dict(w_o_mla=_split_cols(d_w_oa), w_o_dil=_split_cols(d_w_ob),
                 w_out=d_w_out.reshape(N_CHIPS, D_MODEL // N_CHIPS, D_MODEL), w_ff1=d_w_ff1, w_ff2=d_w_ff2.reshape(N_CHIPS, D_FF // N_CHIPS, D_MODEL))
    sent = None if early_grads is None else early_grads(grads)
    do_a = mm(dy_a, w_oa, mode="nt", name="d_o_mla", tm=1024, tn=1024, tk=1024, after=sent)
    do_b = mm(dy_b, w_ob, mode="nt", name="d_o_dil", tm=1024, tn=1024, tk=1024)
    dq_a, dk_a, dv_a = _attn_bwd(q_a, 0, k_a, 0, v_a, 0, o_a, do_a, lse_a, mla_bwd_tables, scale_mla,
                                 name="mla_bwd", batch=batch, seq=seq, out_dtype=F32, sub=2, wide_qk=True)
    going = None if early_grads_go is None else early_grads_go(dq_a)
    d_qkv_d = _attn_bwd(proj_d, 0, proj_d, 1, proj_d, 2, o_b, do_b, lse_b, dil_bwd_tables, scale_dil,
                        name="dil_bwd", batch=batch, seq=seq, out_dtype=BF16, sub=2, stride=4, after=going)

    def mla_post(dq, dk, c, up, dn, c_only):
        dqs = [_rope_bwd(dq[:, h * LANES:(h + 1) * LANES], c, up, dn) for h in range(N_HEADS)]
        dk_sum = dk[:, :LANES]
        for h in range(1, N_HEADS):
            dk_sum = dk_sum + dk[:, h * LANES:(h + 1) * LANES]
        return jnp.concatenate(dqs, axis=1), _rope_bwd(dk_sum, c_only, up, dn)

    dq_lin, d_kr = row(mla_post, name="mla_unrope",
                       ins=[(dq_a, D_MODEL, 0, "row"), (dk_a, D_MODEL, 0, "row"), pos(rope_c), pos(rope_up), pos(rope_dn), pos(rope_c_only)],
                       outs=[(N_HEADS * LANES, BF16), (LANES, BF16)])
    d_qn = mm(dq_lin, w_uq_pt, mode="nn", name="d_qn", tm=1024, tn=Q_LORA, tk=1024)
    d_kvn_k = mm(dk_a, w_uk_p, mode="nt", name="d_kvn_k", tm=1024, tn=KV_LORA, tk=1024)
    d_kvn = mm(dv_a, w_uv, mode="nt", name="d_kvn", tm=1024, tn=KV_LORA, tk=1024, extras=(d_kvn_k,), epilogue=lambda acc, e: (acc + e,))

    def rms_bwd(lora, dq, dkv, dkr, gq, gkv):
        dxq, dgq = _rms_bwd(lora[:, :Q_LORA], gq, dq)
        dxk, dgk = _rms_bwd(lora[:, Q_LORA:], gkv, dkv)
        tail = jnp.zeros((dxq.shape[0], P_GATE - P_KR - LANES), F32)
        return jnp.concatenate([dxq, dxk, dkr.astype(F32), tail], axis=1), dgq, dgk

    d_tail, d_g_q_a, d_g_kv_a = row(
        rms_bwd, name="mla_rms_bwd",
        ins=[(proj, LORA_W, lora0, "row"), (d_qn, Q_LORA, 0, "row"), (d_kvn, KV_LORA, 0, "row"), (d_kr, LANES, 0, "row"),
             (g_q_a, 0, 0, "full"), (g_kv_a, 0, 0, "full")],
        outs=[(P_GATE, BF16)], sums=[Q_LORA, KV_LORA])
    d_proj = [d_tail, d_gates, d_qkv_d]
    d_w_in_pt = mm(d_proj, x_mm, mode="tn", name="d_w_in", tm=1024, tn=1024, tk=2048)
    d_w_in = _unpad_w_in_t(d_w_in_pt).reshape(N_CHIPS, IN_WIDTH // N_CHIPS, D_MODEL)
    moving = None if first_grad is None else first_grad(d_w_in)
    d_w_uq_pt = mm(dq_lin, qn, mode="tn", name="d_w_uq", tm=1024, tn=Q_LORA, tk=1024, after=moving)
    d_w_uk_p = mm(kvn, dk_a, mode="tn", name="d_w_uk", tm=KV_LORA, tn=1024, tk=1024, after=moving)
    d_w_uv = mm(kvn, dv_a, mode="tn", name="d_w_uv", tm=KV_LORA, tn=1024, tk=1024, after=moving)
    grads.update(w_in=d_w_in,
                 w_uq=_unpad_head_rows(d_w_uq_pt, MLA_QK).reshape(N_CHIPS, N_HEADS * MLA_QK // N_CHIPS, Q_LORA),
                 w_ukv=_split_cols(_merge_ukv(d_w_uk_p, d_w_uv)))
    leaving = None if last_grads is None else last_grads(grads)
    grad_x = mm(d_proj, jnp.concatenate([w_main_t, w_dil_t], axis=0), mode="nn", name="d_x", tm=1024, tn=1024, tk=1024, extras=(dr1,), epilogue=lambda acc, rv: (acc + ALPHA * rv,),
                after=leaving)

    grads.update(
        b_gate=d_b_gate.reshape(2, D_MODEL), g_q_a=d_g_q_a, g_kv_a=d_g_kv_a, ln1_g=d_ln1_g, ln1_b=d_ln1_b, ln2_g=d_ln2_g, ln2_b=d_ln2_b)
    return loss_l, grad_x.reshape(batch, seq, D_MODEL), grads


BIG = ("w_in", "w_uq", "w_ukv", "w_o_mla", "w_o_dil", "w_out", "w_ff1", "w_ff2")
SMALL = (("b_gate", 2 * D_MODEL), ("g_q_a", Q_LORA), ("g_kv_a", KV_LORA), ("ln1_g", D_MODEL), ("ln1_b", D_MODEL),
         ("ln2_g", D_MODEL), ("ln2_b", D_MODEL))
TRANSPOSED = ("w_in", "w_uq")
D2D_PIECES = (4, 2, 1)
ANY = pl.BlockSpec(memory_space=pl.ANY)
SIDE_EFFECTS = pltpu.CompilerParams(has_side_effects=True)


def _place():
    x, y, c = lax.axis_index("x"), lax.axis_index("y"), lax.axis_index("c")
    return x, y, c, ((1 - x, y), (x, 1 - y), (1 - x, 1 - y))


def _half_axis(shape):
    return 0 if shape[0] % 32 == 0 else 1


def _half_shape(shape):
    return (shape[0] // 2, shape[1]) if _half_axis(shape) == 0 else (shape[0], shape[1] // 2)


def _window(ref, lead, shape, which=None, pieces=False):
    axis = _half_axis(shape)
    size = shape[axis] if which is None else shape[axis] // 2
    base = 0 if which is None else which * size
    tile = (16, LANES)[axis]
    count = next(c for c in D2D_PIECES if size % (tile * c) == 0) if pieces else 1
    step = size // count
    spans = [pl.ds(pl.multiple_of(base + i * step, tile), step) for i in range(count)]
    refs = [ref.at[(*lead, s)] if axis == 0 else ref.at[(*lead, slice(None), s)] for s in spans]
    return refs if pieces else refs[0]


def _remote(src, dst, send, recv, to):
    return pltpu.make_async_remote_copy(src_ref=src, dst_ref=dst, send_sem=send, recv_sem=recv, device_id=to, device_id_type=MESH)


def _pair_split(grads, name):
    n = len(grads)

    def body(*refs):
        srcs, outs, (send, recv) = refs[:n], refs[n:2 * n], refs[2 * n:]
        x, y, c, _ = _place()
        for t in range(n):
            for s in range(N_CHIPS):
                _remote(_window(srcs[t], (s,), grads[t].shape[1:], 1 - c), outs[t].at[s], send.at[t], recv.at[t], (x, y, 1 - c)).start()
        for t in range(n):
            _remote(_window(srcs[t], (slice(None),), grads[t].shape[1:], 1 - c), outs[t], send.at[t], recv.at[t], (x, y, 1 - c)).wait()

    return pl.pallas_call(
        body, name=name, in_specs=[ANY] * n, out_specs=[ANY] * n,
        out_shape=[jax.ShapeDtypeStruct((N_CHIPS,) + _half_shape(g.shape[1:]), g.dtype) for g in grads],
        scratch_shapes=[pltpu.SemaphoreType.DMA((n,)), pltpu.SemaphoreType.DMA((n,))],
        compiler_params=SIDE_EFFECTS,
    )(*grads)


HBM = pl.BlockSpec(memory_space=pltpu.HBM)
SEM = pl.BlockSpec(memory_space=pltpu.SEMAPHORE)
SPLIT = pltpu.CompilerParams(has_side_effects=pltpu.SideEffectType.DATAFLOW_SIDE_EFFECTING)


def _in_hbm(a):
    return pltpu.with_memory_space_constraint(a, pltpu.HBM)


def _split_copies(kind, srcs, lands):
    x, y, c, chips = _place()
    out = []
    for t in range(len(srcs)):
        if kind == "pair":
            out += [(t, s % 3, _window(srcs[t], (s,), srcs[t].shape[1:], 1 - c), lands[t].at[s], (x, y, 1 - c)) for s in range(N_CHIPS)]
            continue
        if kind == "join":
            out += [(t, 0, a, b, (x, y, 1 - c)) for a, b in zip(_window(srcs[t], (), srcs[t].shape, None, True), _window(lands[t], (), srcs[t].shape, None, True))]
            continue
        if kind == "forward":
            shape, sibling = srcs[t].shape, (x, y, 1 - c)
            out += [(t, 0, a, b, sibling) for a, b in zip(_window(srcs[t], (), shape, None, True), _window(lands[t], (2 * x + y,), shape, None, True))]
            out += [(t, j, a, a, sibling) for j, (cx, cy) in enumerate(chips) for a in _window(lands[t], (2 * cx + cy,), shape, c, True)]
            continue
        for j, (cx, cy) in enumerate(chips):
            if kind == "gather":
                shape = srcs[t].shape
                out.append((t, j, _window(srcs[t], (), shape, c), _window(lands[t], (2 * x + y,), shape, c), (cx, cy, c)))
            else:
                out.append((t, j, srcs[t].at[2 * cx + cy], lands[t].at[j], (cx, cy, c)))
    return out


def _split_start(kind, srcs, land_shapes, name, lands=None, after=None):
    n = len(srcs)

    def body(*refs):
        src_refs, land_refs, sems, token = refs[:n], refs[n:2 * n], refs[-7 - 2 * n:-1 - 2 * n], refs[-1]
        for t, j, s, d, to in _split_copies(kind, src_refs, land_refs):
            _remote(s, d, sems[j], sems[3 + j], to).start()
        token[...] = jnp.zeros_like(token)

    lands = [_in_hbm(lax.empty(s.shape, s.dtype)) for s in land_shapes] if lands is None else list(lands)
    thru = [pltpu.HBM(a.shape, a.dtype) for a in list(srcs) + lands]
    res = pl.pallas_call(
        body, name=name,
        out_shape=(*[pltpu.SemaphoreType.DMA(())] * 6, *thru, jax.ShapeDtypeStruct((8, LANES), F32)),
        in_specs=[HBM] * (2 * n) + [ANY] * (after is not None), out_specs=(*[SEM] * 6, *[HBM] * (2 * n), pl.BlockSpec(memory_space=pltpu.VMEM)),
        input_output_aliases={i: 6 + i for i in range(2 * n)}, compiler_params=SPLIT,
    )(*[_in_hbm(s) for s in srcs], *lands, *([after] if after is not None else []))
    return res[:6], res[6:6 + n], res[6 + n:6 + 2 * n], res[-1]


def _split_wait(kind, sems, srcs, lands, after, name):
    n = len(srcs)
    afters = list(after) if isinstance(after, (list, tuple)) else [after]

    def body(*refs):
        src_refs, land_refs, sem_refs = refs[:n], refs[n:2 * n], refs[2 * n:2 * n + 6]
        for t, j, s, d, to in _split_copies(kind, src_refs, land_refs):
            cp = _remote(s, d, sem_refs[j], sem_refs[3 + j], to)
            cp.wait_send()
            cp.wait_recv()

    res = pl.pallas_call(
        body, name=name, out_shape=[pltpu.HBM(a.shape, a.dtype) for a in list(srcs) + list(lands)],
        in_specs=[HBM] * (2 * n) + [SEM] * 6 + [ANY] * len(afters), out_specs=[HBM] * (2 * n),
        input_output_aliases={i: i for i in range(2 * n)}, compiler_params=SPLIT,
    )(*srcs, *lands, *sems, *afters)
    return res[:n], res[n:]


def _sum_all_devices(vec, name, after):
    n_rows = vec.shape[0]

    def body(v_ref, *rest):
        out_ref, buf, send, recv = rest[len(after):]
        x, y, c, _ = _place()
        me = 4 * x + 2 * y + c
        buf[me] = v_ref[...]
        flips = [(a, b, d) for a in (0, 1) for b in (0, 1) for d in (0, 1)][1:]
        copies = []
        for r, (a, b, d) in enumerate(flips):
            px, py, pc = (1 - x if a else x), (1 - y if b else y), (1 - c if d else c)
            copies.append(pltpu.make_async_remote_copy(src_ref=v_ref, dst_ref=buf.at[me], send_sem=send.at[r], recv_sem=recv.at[r],
                                                       device_id=(px, py, pc), device_id_type=MESH))
            copies[-1].start()
        for r, (a, b, d) in enumerate(flips):
            px, py, pc = (1 - x if a else x), (1 - y if b else y), (1 - c if d else c)
            pltpu.make_async_remote_copy(src_ref=v_ref, dst_ref=buf.at[4 * px + 2 * py + pc], send_sem=send.at[r], recv_sem=recv.at[r],
                                         device_id=(px, py, pc), device_id_type=MESH).wait_recv()
        for cp in copies:
            cp.wait_send()
        total = buf[0]
        for k in range(1, N_DEV):
            total = total + buf[k]
        out_ref[...] = total

    vmem = pl.BlockSpec(memory_space=pltpu.VMEM)
    return pl.pallas_call(
        body, name=name, in_specs=[vmem] + [ANY] * len(after), out_specs=vmem, out_shape=jax.ShapeDtypeStruct(vec.shape, F32),
        scratch_shapes=[pltpu.VMEM((N_DEV, n_rows, LANES), F32), pltpu.SemaphoreType.DMA((N_DEV - 1,)), pltpu.SemaphoreType.DMA((N_DEV - 1,))],
        compiler_params=pltpu.CompilerParams(has_side_effects=True),
    )(vec, *after)


def _half_tile(half, width):
    t = half
    while t * width * 4 > (2 << 20) and t % 32 == 0:
        t //= 2
    return t


def _pair_add(g, theirs, core, name):
    _, half, width = theirs.shape
    t = _half_tile(half, width)
    n = half // t

    def body(c_ref, a_ref, b_ref, o_ref):
        o_ref[...] = (a_ref[...] + b_ref[...]).astype(BF16)

    tile = pl.BlockSpec((1, t, width), lambda j, i, c_ref: (j, i, 0))
    if _half_axis(g.shape[1:]) == 0:
        mine = pl.BlockSpec((1, t, width), lambda j, i, c_ref: (j, c_ref[0] * n + i, 0))
    else:
        mine = pl.BlockSpec((1, t, width), lambda j, i, c_ref: (j, i, c_ref[0]))
    return pl.pallas_call(
        body, name=name,
        grid_spec=pltpu.PrefetchScalarGridSpec(num_scalar_prefetch=1, grid=(N_CHIPS, n), in_specs=[mine, tile], out_specs=tile),
        out_shape=jax.ShapeDtypeStruct(theirs.shape, BF16), compiler_params=_params(("parallel", "parallel")),
    )(core, g, theirs)


def _chip_sum(part, others, chip, name, after=None):
    _, half, width = part.shape
    t = _half_tile(half, width)

    def body(s_ref, mine, p0, p1, p2, *rest):
        o_ref = rest[-1]
        o_ref[...] = ((mine[0].astype(F32) + p0[0].astype(F32)) + p1[0].astype(F32)) + p2[0].astype(F32)

    return pl.pallas_call(
        body, name=name,
        grid_spec=pltpu.PrefetchScalarGridSpec(
            num_scalar_prefetch=1, grid=(half // t,),
            in_specs=[pl.BlockSpec((1, t, width), lambda i, s_ref: (s_ref[0], i, 0))]
            + [pl.BlockSpec((1, t, width), lambda i, s_ref, j=j: (j, i, 0)) for j in range(3)] + [pl.BlockSpec(memory_space=pl.ANY)] * (after is not None),
            out_specs=pl.BlockSpec((t, width), lambda i, s_ref: (i, 0))),
        out_shape=jax.ShapeDtypeStruct((half, width), F32), compiler_params=_params(("parallel",)),
    )(chip, part, others, others, others, *([after] if after is not None else []))


EARLY = ("w_in", "w_uq", "w_ukv")
LATE = ("w_o_mla", "w_o_dil", "w_out", "w_ff1", "w_ff2")


def _sum_small(vals):
    n_in = len(vals)
    n_rows = sum(a.shape[0] * a.shape[1] // LANES for a in vals)
    pad_rows = -(-n_rows // 8) * 8

    def chunks(refs):
        return [(ref, a, j) for ref in refs for a in range(ref.shape[0]) for j in range(ref.shape[1] // LANES)]

    def body(*refs):
        ins, outs, (buf, send, recv) = refs[:n_in], refs[n_in:2 * n_in], refs[2 * n_in:]
        x, y, c, _ = _place()
        me = 4 * x + 2 * y + c
        for r, (ref, a, j) in enumerate(chunks(ins)):
            buf[me, r:r + 1, :] = ref[a:a + 1, j * LANES:(j + 1) * LANES]
        if pad_rows > n_rows:
            buf[me, n_rows:pad_rows, :] = jnp.zeros((pad_rows - n_rows, LANES), F32)
        flips = [(a, b, d) for a in (0, 1) for b in (0, 1) for d in (0, 1)][1:]
        peers = [((1 - x if a else x), (1 - y if b else y), (1 - c if d else c)) for a, b, d in flips]
        copies = [_remote(buf.at[me], buf.at[me], send.at[r], recv.at[r], peer) for r, peer in enumerate(peers)]
        for cp in copies:
            cp.start()
        for r, (px, py, pc) in enumerate(peers):
            _remote(buf.at[me], buf.at[4 * px + 2 * py + pc], send.at[r], recv.at[r], (px, py, pc)).wait_recv()
        for cp in copies:
            cp.wait_send()
        total = buf[0]
        for k in range(1, N_DEV):
            total = total + buf[k]
        for r, (ref, a, j) in enumerate(chunks(outs)):
            ref[a:a + 1, j * LANES:(j + 1) * LANES] = total[r:r + 1, :]

    vmem = pl.BlockSpec(memory_space=pltpu.VMEM)
    return pl.pallas_call(
        body, name="sum_small", in_specs=[vmem] * n_in, out_specs=[vmem] * n_in,
        out_shape=[jax.ShapeDtypeStruct(a.shape, F32) for a in vals],
        scratch_shapes=[pltpu.VMEM((N_DEV, pad_rows, LANES), F32), pltpu.SemaphoreType.DMA((N_DEV - 1,)), pltpu.SemaphoreType.DMA((N_DEV - 1,))],
        compiler_params=SIDE_EFFECTS,
    )(*vals)


def _adam_math(w, g, m, v):
    nm = B1 * m + (1.0 - B1) * g
    nv = B2 * v + (1.0 - B2) * (g * g)
    m_hat = nm / (1.0 - B1 ** ADAM_STEP)
    v_hat = nv / (1.0 - B2 ** ADAM_STEP)
    return -LR * (m_hat / (jnp.sqrt(v_hat) + ADAM_EPS) + WD * w), nm, nv


def _adamw_big(w, mine, theirs, m, v, core, name, side_by_side=False):
    rows, width = w.shape
    if side_by_side:
        t = next(c for c in (152, 96, 64, 32, 16, 8) if rows % c == 0)
        hb = None
        half_spec = pl.BlockSpec((t, width // 2), lambda i, c_ref: (i, 0))
    else:
        t = next(c for c in (256, 128, 64, 32, 16, 8) if (rows // 2) % c == 0)
        hb = rows // 2 // t
        half_spec = pl.BlockSpec((t, width), lambda i, c_ref: (i % hb, 0))

    def body(c_ref, w_ref, a_ref, b_ref, m_ref, v_ref, g_ref, d_ref, nm_ref, nv_ref):
        south = c_ref[0] == 0
        if side_by_side:
            g = jnp.where(south, jnp.concatenate([a_ref[...], b_ref[...]], axis=1), jnp.concatenate([b_ref[...], a_ref[...]], axis=1))
        else:
            g = jnp.where((pl.program_id(0) < hb) == south, a_ref[...], b_ref[...])
        g_ref[...] = g
        d_ref[...], nm_ref[...], nv_ref[...] = _adam_math(w_ref[...], g, m_ref[...], v_ref[...])

    spec = pl.BlockSpec((t, width), lambda i, c_ref: (i, 0))
    return pl.pallas_call(
        body, name=name,
        grid_spec=pltpu.PrefetchScalarGridSpec(num_scalar_prefetch=1, grid=(rows // t,),
                                               in_specs=[spec, half_spec, half_spec, spec, spec], out_specs=[spec] * 4),
        out_shape=[jax.ShapeDtypeStruct(w.shape, F32)] * 4, compiler_params=_params(("parallel",)),
    )(core, w, mine, theirs, m, v)


def _adamw_small(ws, gs, ms, vs):
    n = len(ws)

    def body(*refs):
        for t in range(n):
            w_ref, g_ref, m_ref, v_ref = (refs[k * n + t] for k in range(4))
            d, nm, nv = _adam_math(w_ref[...], g_ref[...], m_ref[...], v_ref[...])
            refs[4 * n + t][...] = d
            refs[5 * n + t][...] = nm
            refs[6 * n + t][...] = nv

    vmem = pl.BlockSpec(memory_space=pltpu.VMEM)
    res = pl.pallas_call(body, name="adamw_small", in_specs=[vmem] * (4 * n), out_specs=[vmem] * (3 * n),
                         out_shape=[jax.ShapeDtypeStruct(a.shape, F32) for a in ws] * 3)(*ws, *gs, *ms, *vs)
    return res[:n], res[n:2 * n], res[2 * n:]


def kernel(x, w_in, b_gate, g_q_a, w_uq, g_kv_a, w_ukv, w_o_mla, w_o_dil, w_out, ln1_g, ln1_b, w_ff1, w_ff2, ln2_g, ln2_b, loss_target, m_w_in, m_b_gate, m_g_q_a, m_w_uq, m_g_kv_a, m_w_ukv, m_w_o_mla, m_w_o_dil, m_w_out, m_ln1_g, m_ln1_b, m_w_ff1, m_w_ff2, m_ln2_g, m_ln2_b, v_w_in, v_b_gate, v_g_q_a, v_w_uq, v_g_kv_a, v_w_ukv, v_w_o_mla, v_w_o_dil, v_w_out, v_ln1_g, v_ln1_b, v_w_ff1, v_w_ff2, v_ln2_g, v_ln2_b):
    order = ("w_in", "b_gate", "g_q_a", "w_uq", "g_kv_a", "w_ukv", "w_o_mla", "w_o_dil", "w_out", "ln1_g", "ln1_b", "w_ff1", "w_ff2", "ln2_g", "ln2_b")
    w = dict(w_in=w_in, b_gate=b_gate, g_q_a=g_q_a, w_uq=w_uq, g_kv_a=g_kv_a, w_ukv=w_ukv, w_o_mla=w_o_mla, w_o_dil=w_o_dil, w_out=w_out,
             ln1_g=ln1_g, ln1_b=ln1_b, w_ff1=w_ff1, w_ff2=w_ff2, ln2_g=ln2_g, ln2_b=ln2_b)
    m = dict(w_in=m_w_in, b_gate=m_b_gate, g_q_a=m_g_q_a, w_uq=m_w_uq, g_kv_a=m_g_kv_a, w_ukv=m_w_ukv, w_o_mla=m_w_o_mla, w_o_dil=m_w_o_dil,
             w_out=m_w_out, ln1_g=m_ln1_g, ln1_b=m_ln1_b, w_ff1=m_w_ff1, w_ff2=m_w_ff2, ln2_g=m_ln2_g, ln2_b=m_ln2_b)
    v = dict(w_in=v_w_in, b_gate=v_b_gate, g_q_a=v_g_q_a, w_uq=v_w_uq, g_kv_a=v_g_kv_a, w_ukv=v_w_ukv, w_o_mla=v_w_o_mla, w_o_dil=v_w_o_dil,
             w_out=v_w_out, ln1_g=v_ln1_g, ln1_b=v_ln1_b, w_ff1=v_w_ff1, w_ff2=v_w_ff2, ln2_g=v_ln2_g, ln2_b=v_ln2_b)
    chip = 2 * lax.axis_index("x") + lax.axis_index("y")
    south = (lax.axis_index("c") == 0).astype(F32)
    gate_w = D_MODEL // N_CHIPS

    core = lax.axis_index("c").astype(jnp.int32).reshape(1)
    turn = lambda n, a: a.T if n in TRANSPOSED else a
    early_shards = [turn(n, w[n][0]).astype(BF16) for n in EARLY]
    gathered = lambda group: [jax.ShapeDtypeStruct((N_CHIPS,) + s.shape, BF16) for s in group]
    e_sems, e_srcs, e_lands, e_token = _split_start("gather", early_shards, gathered(early_shards), "gather_early_start")
    e_token, late_f32, x_f32, positions = lax.optimization_barrier((e_token, [w[n][0] for n in LATE], x, jnp.arange(x.shape[1], dtype=F32)))
    late_shards, x_lo = [a.astype(BF16) for a in late_f32], x_f32.astype(BF16)
    rope = _rope_tables(positions)
    shards = dict(zip(EARLY + LATE, early_shards + late_shards))
    b_mine = lax.dynamic_update_slice(jnp.zeros((2, D_MODEL), F32), b_gate[0] * south, (0, chip * gate_w))
    b_full = _sum_all_devices(b_mine.reshape(-1, LANES), "gather_b_gate", [e_token, *late_shards, *rope, x_lo]).reshape(2, D_MODEL)
    e_srcs, e_lands = _split_wait("gather", e_sems, e_srcs, e_lands, b_full, "gather_early_wait")
    e_forward = _split_start("forward", e_srcs, None, "gather_early_forward_start", lands=e_lands)
    first = dict(zip(EARLY, _split_wait("forward", *e_forward[:3], e_forward[-1], "gather_early_forward_wait")[1]))
    g_sems, g_srcs, g_lands, g_token = _split_start("gather", late_shards, gathered(late_shards), "gather_late_start", after=first[EARLY[-1]])
    tables = _attention_tables(x.shape[1])

    sent = {}

    def late_arrived(after):
        srcs, lands = _split_wait("gather", g_sems, g_srcs, g_lands, after, "gather_late_wait")
        sent["forward"] = _split_start("forward", srcs, None, "gather_late_forward_start", lands=lands)
        return sent["forward"][-1]

    def late_weights(after):
        return dict(zip(LATE, _split_wait("forward", *sent["forward"][:3], after, "gather_late_forward_wait")[1]))

    exchange_shapes = lambda parts: [jax.ShapeDtypeStruct((3,) + p.shape[1:], BF16) for p in parts]

    def early_grads(grads_late):
        gs = [grads_late[n] for n in LATE]
        shapes = [jax.ShapeDtypeStruct((N_CHIPS,) + _half_shape(g.shape[1:]), F32) for g in gs]
        sent["pair"] = _split_start("pair", gs, shapes, "pair_split_late_start")
        return sent["pair"][-1]

    def early_grads_go(after):
        gs, theirs = _split_wait("pair", *sent["pair"][:3], after, "pair_split_late_wait")
        parts = [_pair_add(g, th, core, "pair_add_" + n) for g, th, n in zip(gs, theirs, LATE)]
        sent["late"] = _split_start("scatter", parts, exchange_shapes(parts), "exchange_late_start")
        return sent["late"][-1]

    def first_grad(g):
        sent["first"] = _split_start("pair", [g], [jax.ShapeDtypeStruct((N_CHIPS,) + _half_shape(g.shape[1:]), F32)], "pair_split_w_in_start")
        return sent["first"][-1]

    def last_grads(grads_early):
        rest = [grads_early[n] for n in EARLY[1:]]
        gs, theirs = _split_wait("pair", *sent["first"][:3], rest[-1], "pair_split_w_in_wait")
        gs, theirs = list(gs) + rest, list(theirs) + list(_pair_split(rest, "pair_split_early"))
        parts = [_pair_add(g, th, core, "pair_add_" + n) for g, th, n in zip(gs, theirs, EARLY)]
        sent["early"] = _split_start("scatter", parts, exchange_shapes(parts), "exchange_early_start")
        return sent["early"][-1]

    loss_part, grad_x, grads = _local_step(x, loss_target, first, b_full, g_q_a, g_kv_a, ln1_g, ln1_b, ln2_g, ln2_b, token=g_token,
                                           late_arrived=late_arrived, late_weights=late_weights, early_grads=early_grads, early_grads_go=early_grads_go,
                                           first_grad=first_grad, last_grads=last_grads, tables=tables, rope=rope, x_lo=x_lo)

    g_out, delta, new_m, new_v = {}, {}, {}, {}
    chip1 = chip.astype(jnp.int32).reshape(1)

    def sum_and_send(names, parts, others, tag):
        totals = [_chip_sum(p, o, chip1, "chip_sum_" + n) for n, p, o in zip(names, parts, others)]
        return _split_start("join", totals, [jax.ShapeDtypeStruct(t.shape, F32) for t in totals], "pair_join_" + tag + "_start")

    def adam(names, joined, after, tag):
        totals, halves = _split_wait("join", *joined[:3], after, "pair_join_" + tag + "_wait")
        for n, mine, theirs in zip(names, totals, halves):
            res = _adamw_big(turn(n, w[n][0]), mine, theirs, turn(n, m[n][0]), turn(n, v[n][0]), core, "adamw_" + n,
                             side_by_side=mine.shape[0] == shards[n].shape[0])
            g_out[n], delta[n], new_m[n], new_v[n] = (turn(n, r) for r in res)

    late_joined = sum_and_send(LATE, *_split_wait("scatter", *sent["late"][:3], grad_x, "exchange_late_wait"), "late")
    early_joined = sum_and_send(EARLY, *_split_wait("scatter", *sent["early"][:3], late_joined[-1], "exchange_early_wait"), "early")
    small_names = [name for name, _ in SMALL]
    sums = _sum_small([grads[name] for name in small_names] + [loss_part])
    loss = sums[-1][0, 0]
    g_small = dict(zip(small_names, sums))
    g_small["b_gate"] = lax.dynamic_slice(g_small["b_gate"], (0, chip * gate_w), (2, gate_w))
    flat = lambda a: a.reshape(-1, a.shape[-1])
    res = _adamw_small(*[[flat(d[name]) for name in small_names] for d in (w, g_small, m, v)])
    g_out.update(g_small)
    for d, r in zip((delta, new_m, new_v), res):
        d.update(zip(small_names, r))
    adam(LATE, late_joined, res[0][0], "late")
    adam(EARLY, early_joined, delta[LATE[-1]], "early")

    lead = lambda d: [d[name].reshape(w[name].shape) for name in order]
    return (loss, grad_x, *lead(g_out), *lead(delta), *lead(new_m), *lead(new_v))
```

```python
import functools
import math

import jax
import jax.numpy as jnp
import numpy as np
from jax import lax
from jax.experimental import pallas as pl
from jax.experimental.pallas import tpu as pltpu

F32 = jnp.float32
BF16 = jnp.bfloat16
MESH = pl.DeviceIdType.MESH

D_MODEL = 1024
N_HEADS = 8
LANES = 128
NOPE, ROPE, V_DIM = 64, 32, 64
MLA_QK = NOPE + ROPE
Q_LORA, KV_LORA = 384, 256
DIL_DIM = 64
DIL_PATTERNS = ((128, 1), (512, 4), (2048, 16))
D_FF = 4096
N_CHIPS = 4
N_DEV = 8
IN_WIDTH = 4256
LN_EPS, RMS_EPS = 1e-5, 1e-6
NEG = -1e30
LOG2E, LN2 = 1.4426950408889634, 0.6931471805599453
ALPHA = 2.0 ** 0.25
ROPE_THETA = 10000.0
LR, B1, B2, ADAM_EPS, WD, ADAM_STEP = 0.001, 0.9, 0.999, 1e-8, 0.01, 10

P_LORA, P_KR, P_GATE, P_HALF = 0, 640, 1024, 3072
DIL_GROUP = 4 * LANES
P_DIL = N_HEADS // 2 * DIL_GROUP
LORA_W = Q_LORA + KV_LORA
KR_LANE = NOPE

ATT_T = 512
ROW_T = 512
ACC_COLS = 256
VMEM_LIMIT = 56 * 1024 * 1024

NN = (((1,), (0,)), ((), ()))
NT = (((1,), (1,)), ((), ()))
TN = (((0,), (0,)), ((), ()))


def _params(sem=None, **kw):
    return pltpu.CompilerParams(dimension_semantics=sem, vmem_limit_bytes=VMEM_LIMIT, **kw)


def _matmul(a, b, *, mode, name, tm, tn, tk, out_dtypes=(F32,), extras=(), row_extras=(), epilogue=None, b_shards=False, out_shards=False, after=None):
    pieces = list(a) if isinstance(a, (list, tuple)) else [a]
    n_pc = len(pieces)
    a_shape = (pieces[0].shape[0], sum(p.shape[1] for p in pieces))
    if b_shards:
        n_sh, rows_b, cols_b = b.shape
        b_shape = (rows_b, n_sh * cols_b)
    else:
        b_shape = b.shape
    if mode == "nn":
        (m, k), (k2, n) = a_shape, b_shape
    elif mode == "nt":
        (m, k), (n, k2) = a_shape, b_shape
    else:
        (k, m), (k2, n) = a_shape, b_shape
    assert k == k2, (a_shape, b.shape, mode)
    tm, tn, tk = min(tm, m), min(tn, n), min(tk, k)
    assert m % tm == 0 and n % tn == 0 and k % tk == 0, (name, m, n, k, tm, tn, tk)
    nk = k // tk
    n_ex, n_out = len(extras) + len(row_extras), len(out_dtypes)
    n_in = n_pc + 1 + n_ex + (after is not None)
    dims = {"nn": NN, "nt": NT, "tn": TN}[mode]
    col_tile = tm if mode == "tn" else tk
    blocks = [p.shape[1] // col_tile for p in pieces]
    firsts = [sum(blocks[:p]) for p in range(n_pc)]
    assert all(p.shape[1] % col_tile == 0 for p in pieces), (name, col_tile)

    def body(*refs):
        a_refs, b_ref = refs[:n_pc], refs[n_pc]
        ex_refs = refs[n_pc + 1:n_pc + 1 + n_ex]
        out_refs = refs[n_in:n_in + n_out]

        def finish(acc):
            outs = epilogue(acc, *[r[...] for r in ex_refs]) if epilogue is not None else (acc,)
            for r, o in zip(out_refs, outs):
                r[...] = o.astype(r.dtype)

        kk = pl.program_id(2)

        def step(a_ref):
            if nk == 1:
                finish(lax.dot_general(a_ref[...].astype(BF16), b_ref[...].astype(BF16), dims, preferred_element_type=F32))
                return
            acc_ref = refs[-1]

            @pl.when(kk == 0)
            def _():
                acc_ref[...] = jnp.zeros_like(acc_ref)

            for c in range(0, tn, ACC_COLS):
                b_blk = b_ref[c:c + ACC_COLS, :] if mode == "nt" else b_ref[:, c:c + ACC_COLS]
                acc_ref[:, c:c + ACC_COLS] += lax.dot_general(a_ref[...].astype(BF16), b_blk.astype(BF16), dims, preferred_element_type=F32)

            @pl.when(kk == nk - 1)
            def _():
                finish(acc_ref[...])

        if n_pc == 1:
            step(a_refs[0])
        else:
            at = pl.program_id(0) if mode == "tn" else kk
            for p in range(n_pc):
                pl.when(jnp.logical_and(at >= firsts[p], at < firsts[p] + blocks[p]))(functools.partial(step, a_refs[p]))

    def a_spec_of(p):
        if n_pc == 1:
            return pl.BlockSpec((tk, tm), lambda i, j, kk: (kk, i)) if mode == "tn" else pl.BlockSpec((tm, tk), lambda i, j, kk: (i, kk))
        col = lambda at: jnp.clip(at - firsts[p], 0, blocks[p] - 1)
        mine = lambda at: jnp.logical_and(at >= firsts[p], at < firsts[p] + blocks[p])
        if mode == "tn":
            return pl.BlockSpec((tk, tm), lambda i, j, kk: (jnp.where(mine(i), kk, 0), col(i)))
        return pl.BlockSpec((tm, tk), lambda i, j, kk: (i, col(kk)))

    b_spec = {"nn": pl.BlockSpec((tk, tn), lambda i, j, kk: (kk, j)),
              "nt": pl.BlockSpec((tn, tk), lambda i, j, kk: (j, kk)),
              "tn": pl.BlockSpec((tk, tn), lambda i, j, kk: (kk, j))}[mode]
    tile = pl.BlockSpec((tm, tn), lambda i, j, kk: (i, j))
    out_spec, out_dims = tile, (m, n)
    if b_shards and mode == "nn":
        per = cols_b // tn
        b_spec = pl.BlockSpec((None, tk, tn), lambda i, j, kk: (j // per, kk, j % per))
    elif b_shards:
        assert mode == "nt"
        per = cols_b // tk
        b_spec = pl.BlockSpec((None, tn, tk), lambda i, j, kk: (kk // per, j, kk % per))
    if out_shards:
        assert not extras and epilogue is None
        per_out = n // N_CHIPS // tn
        out_spec = pl.BlockSpec((None, tm, tn), lambda i, j, kk: (j // per_out, i, j % per_out))
        out_dims = (N_CHIPS, m, n // N_CHIPS)
    outs = pl.pallas_call(
        body, name=name,
        grid=(m // tm, n // tn, nk),
        in_specs=[a_spec_of(p) for p in range(n_pc)] + [b_spec] + [tile] * len(extras) + [pl.BlockSpec((1, tn), lambda i, j, kk: (0, j))] * len(row_extras) + [pl.BlockSpec(memory_space=pl.ANY)] * (after is not None),
        out_specs=[out_spec] * n_out,
        out_shape=[jax.ShapeDtypeStruct(out_dims, dt) for dt in out_dtypes],
        scratch_shapes=[pltpu.VMEM((tm, tn), F32)] if nk > 1 else [],
        compiler_params=_params(("parallel", "parallel", "arbitrary")),
    )(*pieces, b, *extras, *row_extras, *([after] if after is not None else []))
    return outs[0] if n_out == 1 else outs


def _rowwise(fn, *, name, rows, seq, ins, outs, sums=()):
    tm = min(ROW_T, seq)
    n_pos = seq // tm
    n_in, n_out, n_sum = len(ins), len(outs), len(sums)

    def body(*refs):
        vals = fn(*[r[...] for r in refs[:n_in]])
        for r, v in zip(refs[n_in:n_in + n_out], vals[:n_out]):
            r[...] = v.astype(r.dtype)
        first = pl.program_id(0) == 0
        for r, v in zip(refs[n_in + n_out:], vals[n_out:]):
            @pl.when(first)
            def _(r=r, v=v):
                r[...] = v

            @pl.when(jnp.logical_not(first))
            def _(r=r, v=v):
                r[...] += v

    def spec(arr, width, col, kind):
        if kind == "row":
            return pl.BlockSpec((tm, width), lambda i, col=col: (i, col))
        if kind == "pos":
            return pl.BlockSpec((tm, width), lambda i, col=col: (i % n_pos, col))
        return pl.BlockSpec(arr.shape, lambda i: (0,) * arr.ndim)

    res = pl.pallas_call(
        body, name=name,
        grid=(rows // tm,),
        in_specs=[spec(*t) for t in ins],
        out_specs=[pl.BlockSpec((tm, w), lambda i: (i, 0)) for w, _ in outs]
        + [pl.BlockSpec((1, w), lambda i: (0, 0)) for w in sums],
        out_shape=[jax.ShapeDtypeStruct((rows, w), dt) for w, dt in outs]
        + [jax.ShapeDtypeStruct((1, w), F32) for w in sums],
        compiler_params=_params(("arbitrary",)),
    )(*[t[0] for t in ins])
    return res


def _colsum(v):
    return jnp.sum(v, axis=0, keepdims=True)


def _rope_fwd(t, c, s_up, s_dn):
    return t * c + pltpu.roll(t, LANES - 16, 1) * s_up + pltpu.roll(t, 16, 1) * s_dn


def _rope_bwd(d, c, s_up, s_dn):
    return d * c + pltpu.roll(d * s_up, 16, 1) + pltpu.roll(d * s_dn, LANES - 16, 1)


def _rope_tables(positions):
    seq, half = positions.shape[0], ROPE // 2
    inv = jnp.power(ROPE_THETA, -jnp.arange(half, dtype=F32) / half)
    ang = positions[:, None] * inv[None, :]
    cos, sin = jnp.cos(ang), jnp.sin(ang)
    zeros = jnp.zeros((seq, half), F32)
    lo, hi = jnp.ones((seq, KR_LANE), F32), jnp.ones((seq, LANES - KR_LANE - ROPE), F32)
    c = jnp.concatenate([lo, cos, cos, hi], axis=1)
    c_rope_only = jnp.concatenate([0 * lo, cos, cos, 0 * hi], axis=1)
    s_up = jnp.concatenate([0 * lo, -sin, zeros, 0 * hi], axis=1)
    s_dn = jnp.concatenate([0 * lo, zeros, sin, 0 * hi], axis=1)
    return c, s_up, s_dn, c_rope_only


def _rms(x, g):
    r = lax.rsqrt(jnp.mean(x * x, axis=1, keepdims=True) + RMS_EPS)
    return x * r * g


def _rms_bwd(x, g, dy):
    r = lax.rsqrt(jnp.mean(x * x, axis=1, keepdims=True) + RMS_EPS)
    xh = x * r
    dxh = dy * g
    dx = r * (dxh - xh * jnp.mean(dxh * xh, axis=1, keepdims=True))
    return dx, _colsum(dy * xh)


def _ln_stats(x):
    mu = jnp.mean(x, axis=1, keepdims=True)
    xc = x - mu
    r = lax.rsqrt(jnp.mean(xc * xc, axis=1, keepdims=True) + LN_EPS)
    return xc * r, r


def _ln_bwd(xh, r, g, dy):
    dxh = dy * g
    dx = r * (dxh - jnp.mean(dxh, axis=1, keepdims=True) - xh * jnp.mean(dxh * xh, axis=1, keepdims=True))
    return dx, _colsum(dy * xh), _colsum(dy)


def _table_specs(tables, sub):
    whole = lambda a: pl.BlockSpec(a.shape, lambda b, g: (0,) * a.ndim)
    if len(tables) == 1:
        return [whole(tables[0])]
    return [whole(tables[0]), whole(tables[1]), pl.BlockSpec((sub, 1, LANES), lambda b, g: (g, 0, 0))]


def _biased(s, table_refs, delta, head):
    if delta < table_refs[0].shape[0]:
        s = s + table_refs[0][delta]
    if len(table_refs) == 3:
        s = s - table_refs[2][head, 0:1, 0:1] * table_refs[1][delta]
    return s


def _lane_masks(sub):
    lane = lax.broadcasted_iota(jnp.int32, (1, LANES), 1)
    return [(lane // (LANES // sub) == a).astype(F32) for a in range(sub)]


def _attn_fwd(q, qb0, k, kb0, v, vb0, tables, scale, *, name, batch, seq, sub=1, stride=1, wide_qk=False, after=None):
    t = ATT_T
    nq = seq // t
    rows = batch * seq
    n_tab = len(tables)
    qk_w = sub * LANES if wide_qk else LANES

    def body(q_ref, k_ref, v_ref, *rest):
        table_refs = rest[:n_tab]
        o_ref, lse_ref, vtb = rest[n_tab + (after is not None):][:3]
        per_head = rest[n_tab + (after is not None) + 3:]
        qbs, kbs = per_head[:sub], per_head[sub:]
        masks = _lane_masks(sub)
        for a in range(sub):
            lanes = slice(a * LANES, (a + 1) * LANES) if wide_qk else slice(None)
            qa = q_ref[:, lanes]
            qbs[a][...] = (qa.astype(F32) * masks[a]).astype(BF16) if sub > 1 and not wide_qk else qa.astype(BF16)
            if wide_qk or a == 0:
                kbs[a][...] = k_ref[:, lanes].astype(BF16)
        vtb[...] = v_ref[...].astype(F32).T.astype(BF16)
        for i in range(nq):
            out_t = None
            for a in range(sub):
                qt, kb = qbs[a][i * t:(i + 1) * t, :], kbs[a if wide_qk else 0]
                logits = [_biased(lax.dot_general(kb[j * t:(j + 1) * t, :], qt, NT, preferred_element_type=F32) * (scale * LOG2E), table_refs, i - j, a)
                          for j in range(i + 1)]
                m = jnp.max(functools.reduce(jnp.maximum, logits), axis=0, keepdims=True)
                ps = [jnp.exp2(s - m) for s in logits]
                l = jnp.sum(functools.reduce(jnp.add, ps), axis=0, keepdims=True)
                acc = functools.reduce(jnp.add, [lax.dot_general(vtb[:, j * t:(j + 1) * t], p.astype(BF16), NN, preferred_element_type=F32)
                                                 for j, p in enumerate(ps)])
                part = acc / l if sub == 1 else (acc / l) * masks[a].T
                out_t = part if out_t is None else out_t + part
                lse_ref[i * t:(i + 1) * t, a * LANES:(a + 1) * LANES] = jnp.broadcast_to((m + jnp.log2(l)) * LN2, (LANES, t)).T
            o_ref[i * t:(i + 1) * t, :] = out_t.T

    slab = lambda b0, step, width=LANES: pl.BlockSpec((seq, width), lambda b, g: (b, b0 + step * g))
    groups = N_HEADS // sub
    n_k = sub if wide_qk else 1
    return pl.pallas_call(
        body, name=name,
        grid=(batch, groups),
        in_specs=[slab(qb0, stride, qk_w), slab(kb0, stride, qk_w), slab(vb0, stride)] + _table_specs(tables, sub)
        + [pl.BlockSpec(memory_space=pl.ANY)] * (after is not None),
        out_specs=[slab(0, 1), slab(0, 1, sub * LANES)],
        out_shape=[jax.ShapeDtypeStruct((rows, groups * LANES), F32), jax.ShapeDtypeStruct((rows, N_HEADS * LANES), F32)],
        scratch_shapes=[pltpu.VMEM((LANES, seq), BF16)] + [pltpu.VMEM((seq, LANES), BF16)] * (sub + n_k),
        compiler_params=_params(("arbitrary", "arbitrary")),
    )(q, k, v, *tables, *([after] if after is not None else []))


def _attn_bwd(q, qb0, k, kb0, v, vb0, o, do, lse, tables, scale, *, name, batch, seq, out_dtype, sub=1, stride=1, wide_qk=False, after=None):
    t = ATT_T
    nq = seq // t
    rows = batch * seq
    n_tab = len(tables)
    groups = N_HEADS // sub
    packed = sub > 1 and not wide_qk
    n_out = 1 if packed else 3
    n_k = sub if wide_qk else 1
    qk_w = sub * LANES if wide_qk else LANES

    def body(q_ref, k_ref, v_ref, o_ref, do_ref, lse_ref, *rest):
        table_refs = rest[:n_tab]
        rest = rest[n_tab + (after is not None):]
        out_refs, (vb, dva), rest = rest[:n_out], rest[n_out:n_out + 2], rest[n_out + 2:]
        kbs, dkas, rest = rest[:n_k], rest[n_k:2 * n_k], rest[2 * n_k:]
        qbs, dobs, qtbs, dotbs = (rest[g * sub:(g + 1) * sub] for g in range(4))
        masks = _lane_masks(sub)
        vb[...] = v_ref[...].astype(BF16)
        for a in range(sub):
            lanes = slice(a * LANES, (a + 1) * LANES) if wide_qk else slice(None)
            qa = q_ref[:, lanes].astype(F32) * masks[a] if packed else q_ref[:, lanes].astype(F32)
            doa = do_ref[...] * masks[a] if sub > 1 else do_ref[...]
            qbs[a][...] = qa.astype(BF16)
            dobs[a][...] = doa.astype(BF16)
            qtbs[a][...] = qa.T.astype(BF16)
            dotbs[a][...] = doa.T.astype(BF16)
            if wide_qk or a == 0:
                kbs[a][...] = k_ref[:, lanes].astype(BF16)
        first_k, first_v = [[True] * nq for _ in range(n_k)], [True] * nq
        for i in range(nq):
            at = slice(i * t, (i + 1) * t)
            dq_all = None
            for a in range(sub):
                qt, dot, kb, dka = qbs[a][at, :], dobs[a][at, :], kbs[a if wide_qk else 0], dkas[a if wide_qk else 0]
                lse_t = lse_ref[at, a * LANES:a * LANES + 1] * LOG2E
                od = o_ref[at, :] * do_ref[at, :]
                delta = jnp.sum(od * masks[a] if sub > 1 else od, axis=1, keepdims=True)
                dq = None
                for j in range(i + 1):
                    kat = slice(j * t, (j + 1) * t)
                    kt, vt = kb[kat, :], vb[kat, :]
                    p = jnp.exp2(_biased(lax.dot_general(qt, kt, NT, preferred_element_type=F32) * (scale * LOG2E), table_refs, i - j, a) - lse_t)
                    dp = lax.dot_general(dot, vt, NT, preferred_element_type=F32)
                    ds = (p * (dp - delta) * scale).astype(BF16)
                    dk_part = lax.dot_general(qtbs[a][:, at], ds, NN, preferred_element_type=F32)
                    dv_part = lax.dot_general(dotbs[a][:, at], p.astype(BF16), NN, preferred_element_type=F32)
                    firsts = first_k[a if wide_qk else 0]
                    if firsts[j]:
                        dka[:, kat] = dk_part
                        firsts[j] = False
                    else:
                        dka[:, kat] += dk_part
                    if first_v[j]:
                        dva[:, kat] = dv_part
                        first_v[j] = False
                    else:
                        dva[:, kat] += dv_part
                    dq_part = lax.dot_general(ds, kt, NN, preferred_element_type=F32)
                    dq = dq_part if dq is None else dq + dq_part
                if wide_qk:
                    out_refs[0][at, a * LANES:(a + 1) * LANES] = dq.astype(out_refs[0].dtype)
                else:
                    dq = dq * masks[a] if sub > 1 else dq
                    dq_all = dq if dq_all is None else dq_all + dq
            if not wide_qk:
                out_refs[0][at, 0:LANES] = dq_all.astype(out_refs[0].dtype)
        if packed:
            out_refs[0][:, LANES:2 * LANES] = dkas[0][...].T.astype(out_refs[0].dtype)
            out_refs[0][:, 2 * LANES:3 * LANES] = dva[...].T.astype(out_refs[0].dtype)
            out_refs[0][:, 3 * LANES:] = jnp.zeros((seq, LANES), out_refs[0].dtype)
        else:
            for a in range(n_k):
                out_refs[1][:, a * LANES:(a + 1) * LANES] = dkas[a][...].T.astype(out_refs[1].dtype)
            out_refs[2][...] = dva[...].T.astype(out_refs[2].dtype)

    slab = lambda b0, step, width=LANES: pl.BlockSpec((seq, width), lambda b, g: (b, b0 + step * g))
    if packed:
        out_specs = [slab(0, 1, 4 * LANES)]
        out_shape = [jax.ShapeDtypeStruct((rows, groups * 4 * LANES), out_dtype)]
    else:
        out_specs = [slab(0, 1, qk_w), slab(0, 1, qk_w), slab(0, 1)]
        out_shape = [jax.ShapeDtypeStruct((rows, N_HEADS * LANES), out_dtype)] * 2 + [jax.ShapeDtypeStruct((rows, groups * LANES), out_dtype)]
    res = pl.pallas_call(
        body, name=name,
        grid=(batch, groups),
        in_specs=[slab(qb0, stride, qk_w), slab(kb0, stride, qk_w), slab(vb0, stride), slab(0, 1), slab(0, 1), slab(0, 1, sub * LANES)]
        + _table_specs(tables, sub) + [pl.BlockSpec(memory_space=pl.ANY)] * (after is not None),
        out_specs=out_specs, out_shape=out_shape,
        scratch_shapes=[pltpu.VMEM((seq, LANES), BF16), pltpu.VMEM((LANES, seq), F32)]
        + [pltpu.VMEM((seq, LANES), BF16)] * n_k + [pltpu.VMEM((LANES, seq), F32)] * n_k
        + [pltpu.VMEM((seq, LANES), BF16)] * (2 * sub) + [pltpu.VMEM((LANES, seq), BF16)] * (2 * sub),
        compiler_params=_params(("arbitrary", "arbitrary")),
    )(q, k, v, o, do, lse, *tables, *([after] if after is not None else []))
    return res[0] if packed else res


def _attention_tables(seq):
    n = seq // ATT_T
    pos = np.arange(ATT_T, dtype=np.int32)
    dist = np.arange(n, dtype=np.int32)[:, None, None] * ATT_T + pos[None, :, None] - pos[None, None, :]
    causal = np.where(dist[:1] >= 0, 0.0, NEG).astype(np.float32)
    count = np.zeros(dist.shape, np.float32)
    for window, dilation in DIL_PATTERNS:
        count += ((dist >= 0) & (dist <= window) & (dist % dilation == 0)).astype(np.float32)
    held = np.where(count > 0, np.log2(np.maximum(count, 1.0)), NEG).astype(np.float32)
    far = dist.astype(np.float32) * np.float32(LOG2E)
    slopes = np.asarray([2.0 ** (-8.0 * (i + 1) / N_HEADS) for i in range(N_HEADS)], np.float32)
    slopes = jnp.asarray(np.broadcast_to(slopes[:, None, None], (N_HEADS, 1, LANES)))
    flat = lambda a: jnp.asarray(np.ascontiguousarray(a))
    turned = lambda a: flat(np.swapaxes(a, 1, 2))
    return ((flat(causal),), (turned(causal),)), ((flat(held), flat(far), slopes), (turned(held), turned(far), slopes))


def _pad_heads(w, width):
    kdim, n = w.shape[0], w.shape[1] // width
    return jnp.pad(w.reshape(kdim, n, width), ((0, 0), (0, 0), (0, LANES - width))).reshape(kdim, n * LANES)


def _unpad_heads(w, width):
    kdim, n = w.shape[0], w.shape[1] // LANES
    return w.reshape(kdim, n, LANES)[:, :, :width].reshape(kdim, n * width)


def _pad_head_rows(w, width):
    n, kdim = w.shape[0] // width, w.shape[1]
    return jnp.pad(w.reshape(n, width, kdim), ((0, 0), (0, LANES - width), (0, 0))).reshape(n * LANES, kdim)


def _unpad_head_rows(w, width):
    n, kdim = w.shape[0] // LANES, w.shape[1]
    return w.reshape(n, LANES, kdim)[:, :width].reshape(n * width, kdim)


def _pad_w_in_t(wt):
    n_qkv, pair = 3 * N_HEADS * DIL_DIM, 2 * DIL_DIM
    zeros = lambda n: jnp.zeros((n, wt.shape[1]), wt.dtype)
    main = jnp.concatenate([wt[:LORA_W], zeros(KR_LANE), wt[LORA_W:LORA_W + ROPE], zeros(P_GATE - P_KR - KR_LANE - ROPE),
                            wt[LORA_W + ROPE + n_qkv:]], axis=0)
    qkv = wt[LORA_W + ROPE:LORA_W + ROPE + n_qkv].reshape(3, N_HEADS // 2, pair, wt.shape[1]).transpose(1, 0, 2, 3)
    dil = jnp.pad(qkv, ((0, 0), (0, 1), (0, 0), (0, 0))).reshape(P_DIL, wt.shape[1])
    return main, dil


def _unpad_w_in_t(gt):
    n_qkv, pair = 3 * N_HEADS * DIL_DIM, 2 * DIL_DIM
    moves = [(P_LORA, 0, LORA_W), (P_KR + KR_LANE, LORA_W, ROPE)]
    moves += [(P_HALF + (g * 4 + j) * pair, LORA_W + ROPE + (j * (N_HEADS // 2) + g) * pair, pair) for j in range(3) for g in range(N_HEADS // 2)]
    moves += [(P_GATE + r, LORA_W + ROPE + n_qkv + r, ROW_T) for r in range(0, P_HALF - P_GATE, ROW_T)]
    n = len(moves)

    def body(src, out, buf, sem_in, sem_out):
        ins = [pltpu.make_async_copy(src.at[pl.ds(a, k)], buf.at[pl.ds(b, k)], sem_in.at[i]) for i, (a, b, k) in enumerate(moves)]
        outs = [pltpu.make_async_copy(buf.at[pl.ds(b, k)], out.at[pl.ds(b, k)], sem_out.at[i]) for i, (a, b, k) in enumerate(moves)]
        for cp in ins:
            cp.start()
        for cp_in, cp_out in zip(ins, outs):
            cp_in.wait()
            cp_out.start()
        for cp in outs:
            cp.wait()

    return pl.pallas_call(
        body, name="unpad_d_w_in", in_specs=[pl.BlockSpec(memory_space=pl.ANY)], out_specs=pl.BlockSpec(memory_space=pl.ANY),
        out_shape=jax.ShapeDtypeStruct((IN_WIDTH, gt.shape[1]), gt.dtype),
        scratch_shapes=[pltpu.VMEM((IN_WIDTH, gt.shape[1]), gt.dtype), pltpu.SemaphoreType.DMA((n,)), pltpu.SemaphoreType.DMA((n,))],
        compiler_params=_params(),
    )(gt)


def _split_ukv(w):
    w3 = w.reshape(w.shape[0], N_HEADS, NOPE + V_DIM)
    return _pad_heads(w3[:, :, :NOPE].reshape(w.shape[0], -1), NOPE), w3[:, :, NOPE:].reshape(w.shape[0], -1)


def _merge_ukv(g_k, g_v):
    kdim = g_k.shape[0]
    k3 = _unpad_heads(g_k, NOPE).reshape(kdim, N_HEADS, NOPE)
    return jnp.concatenate([k3, g_v.reshape(kdim, N_HEADS, V_DIM)], axis=2).reshape(kdim, N_HEADS * (NOPE + V_DIM))


def _join_cols(w):
    return w.transpose(1, 0, 2).reshape(w.shape[1], N_CHIPS * w.shape[2])


def _split_cols(g):
    return g.reshape(g.shape[0], N_CHIPS, g.shape[1] // N_CHIPS).transpose(1, 0, 2)


def _local_step(x3, target3, wg, b_gate, g_q_a, g_kv_a, ln1_g, ln1_b, ln2_g, ln2_b, token=None, late_arrived=None, late_weights=None,
                early_grads=None, early_grads_go=None, first_grad=None, last_grads=None, tables=None, rope=None, x_lo=None):
    w_main_t, w_dil_t = _pad_w_in_t(wg["w_in"].reshape(IN_WIDTH, D_MODEL))
    w_uq_pt = _pad_head_rows(wg["w_uq"].reshape(N_HEADS * MLA_QK, Q_LORA), MLA_QK)
    w_ukv = _join_cols(wg["w_ukv"])
    batch, seq, _ = x3.shape
    rows = batch * seq
    x = x3.reshape(rows, D_MODEL)
    x_mm = x if x_lo is None else x_lo.reshape(rows, D_MODEL)
    target = target3.reshape(rows, D_MODEL)
    row = functools.partial(_rowwise, rows=rows, seq=seq)
    mm = _matmul

    w_uk_p, w_uv = _split_ukv(w_ukv)
    b0, b1 = b_gate[0:1], b_gate[1:2]
    rope_c, rope_up, rope_dn, rope_c_only = _rope_tables(jnp.arange(seq, dtype=F32)) if rope is None else rope
    (mla_bwd_tables, mla_fwd_tables), (dil_bwd_tables, dil_fwd_tables) = _attention_tables(seq) if tables is None else tables
    scale_mla, scale_dil = MLA_QK ** -0.5, DIL_DIM ** -0.5
    lora0, kr0, gate0 = P_LORA // LORA_W, P_KR // LANES, P_GATE // D_MODEL

    proj = mm(x_mm, w_main_t, mode="nt", name="proj", tm=1024, tn=1536, tk=1024, after=token)
    proj_d = mm(x_mm, w_dil_t, mode="nt", name="proj_dil", tm=1024, tn=1024, tk=1024, out_dtypes=(BF16,))

    def prep(lora, gq, gkv):
        return _rms(lora[:, :Q_LORA], gq), _rms(lora[:, Q_LORA:], gkv)

    qn, kvn = row(prep, name="mla_rms", ins=[(proj, LORA_W, lora0, "row"), (g_q_a, 0, 0, "full"), (g_kv_a, 0, 0, "full")],
                  outs=[(Q_LORA, BF16), (KV_LORA, BF16)])
    q_lin = mm(qn, w_uq_pt, mode="nt", name="q_up", tm=1024, tn=1024, tk=Q_LORA)
    k_lin = mm(kvn, w_uk_p, mode="nn", name="k_up", tm=1024, tn=1024, tk=KV_LORA)
    v_a = mm(kvn, w_uv, mode="nn", name="v_up", tm=1024, tn=1024, tk=KV_LORA, out_dtypes=(BF16,))

    def rope_qk(ql, kl, kr, c, up, dn):
        k_rot = _rope_fwd(kr, c, up, dn)
        qs = [_rope_fwd(ql[:, h * LANES:(h + 1) * LANES], c, up, dn) for h in range(N_HEADS)]
        ks = [kl[:, h * LANES:(h + 1) * LANES] + k_rot for h in range(N_HEADS)]
        return jnp.concatenate(qs, axis=1), jnp.concatenate(ks, axis=1)

    pos = lambda tab: (tab, LANES, 0, "pos")
    q_a, k_a = row(rope_qk, name="rope_qk",
                   ins=[(q_lin, D_MODEL, 0, "row"), (k_lin, D_MODEL, 0, "row"), (proj, LANES, kr0, "row"), pos(rope_c), pos(rope_up), pos(rope_dn)],
                   outs=[(N_HEADS * LANES, BF16), (N_HEADS * LANES, BF16)])
    o_a, lse_a = _attn_fwd(q_a, 0, k_a, 0, v_a, 0, mla_fwd_tables, scale_mla, name="mla_fwd", batch=batch, seq=seq, sub=2, wide_qk=True)
    arrived = None if late_arrived is None else late_arrived(o_a)
    o_b, lse_b = _attn_fwd(proj_d, 0, proj_d, 1, proj_d, 2, dil_fwd_tables, scale_dil, name="dil_fwd", batch=batch, seq=seq, sub=2, stride=4, after=arrived)
    late = wg if late_weights is None else late_weights(o_b)
    w_oa = _join_cols(late["w_o_mla"])
    w_ob = _join_cols(late["w_o_dil"])
    w_out, w_ff1, w_ff2 = late["w_out"].reshape(D_MODEL, D_MODEL), late["w_ff1"], late["w_ff2"].reshape(D_FF, D_MODEL)
    y_a = mm(o_a, w_oa, mode="nn", name="o_mla", tm=1024, tn=1024, tk=1024, out_dtypes=(BF16,))
    y_b = mm(o_b, w_ob, mode="nn", name="o_dil", tm=1024, tn=1024, tk=1024, out_dtypes=(BF16,))

    def gate(t0, t1, c0, c1, ya, yb):
        return (jax.nn.sigmoid(t0 + c0) * ya + jax.nn.sigmoid(t1 + c1) * yb,)

    gate_ins = [(proj, D_MODEL, gate0, "row"), (proj, D_MODEL, gate0 + 1, "row"), (b0, 0, 0, "full"), (b1, 0, 0, "full")]
    (u,) = row(gate, name="gate", ins=gate_ins + [(y_a, D_MODEL, 0, "row"), (y_b, D_MODEL, 0, "row")], outs=[(D_MODEL, BF16)])
    def ln1(mv, xv, g, b):
        r1 = ALPHA * xv + mv
        xh, _ = _ln_stats(r1)
        hv = xh * g + b
        return r1, hv, hv

    r1, h, h_b = mm(u, w_out, mode="nn", name="mix_ln1", tm=ROW_T, tn=D_MODEL, tk=D_MODEL, out_dtypes=(F32, F32, BF16),
                    extras=(x,), row_extras=(ln1_g, ln1_b), epilogue=ln1)

    def relu2(acc):
        r = jnp.maximum(acc, 0.0)
        return (r * r,)

    z = mm(h_b, w_ff1, mode="nn", name="ff1", tm=1024, tn=1024, tk=1024, out_dtypes=(BF16,), epilogue=relu2, b_shards=True)
    f = mm(z, w_ff2, mode="nn", name="ff2", tm=1024, tn=1024, tk=D_FF)

    def ln2_loss(hv, fv, tv, g, b):
        xh, r = _ln_stats(ALPHA * hv + fv)
        err = xh * g + b - tv
        dy = err * (1.0 / D_MODEL)
        dr2, dg, db = _ln_bwd(xh, r, g, dy)
        loss = jnp.sum(_colsum(err * err), axis=1, keepdims=True) * (0.5 / D_MODEL)
        return dr2, dr2, jnp.broadcast_to(loss, (1, LANES)), dg, db

    dr2, dr2_b, loss_l, d_ln2_g, d_ln2_b = row(
        ln2_loss, name="ln2_loss",
        ins=[(h, D_MODEL, 0, "row"), (f, D_MODEL, 0, "row"), (target, D_MODEL, 0, "row"), (ln2_g, 0, 0, "full"), (ln2_b, 0, 0, "full")],
        outs=[(D_MODEL, F32), (D_MODEL, BF16)], sums=[LANES, D_MODEL, D_MODEL])

    d_w_ff2 = mm(z, dr2_b, mode="tn", name="d_w_ff2", tm=1024, tn=1024, tk=rows)
    da = mm(dr2_b, w_ff2, mode="nt", name="d_ff_act", tm=1024, tn=1024, tk=1024, out_dtypes=(BF16,), extras=(z,),
            epilogue=lambda acc, zv: (acc * (2.0 * jnp.sqrt(zv.astype(F32))),))
    d_w_ff1 = mm(h_b, da, mode="tn", name="d_w_ff1", tm=1024, tn=1024, tk=rows, out_shards=True)
    dh = mm(da, w_ff1, mode="nt", name="d_h", tm=1024, tn=1024, tk=1024, extras=(dr2,), epilogue=lambda acc, rv: (acc + ALPHA * rv,), b_shards=True)

    def ln1_bwd(dhv, r1v, g):
        xh, r = _ln_stats(r1v)
        return _ln_bwd(xh, r, g, dhv)

    dr1, d_ln1_g, d_ln1_b = row(ln1_bwd, name="ln1_bwd", ins=[(dh, D_MODEL, 0, "row"), (r1, D_MODEL, 0, "row"), (ln1_g, 0, 0, "full")],
                                outs=[(D_MODEL, F32)], sums=[D_MODEL, D_MODEL])
    d_w_out = mm(u, dr1, mode="tn", name="d_w_out", tm=1024, tn=1024, tk=1024)
    du = mm(dr1, w_out, mode="nt", name="d_u", tm=1024, tn=1024, tk=1024, out_dtypes=(BF16,))

    def gate_bwd(t0, t1, c0, c1, ya, yb, duv):
        s0, s1 = jax.nn.sigmoid(t0 + c0), jax.nn.sigmoid(t1 + c1)
        dt0 = duv * ya * s0 * (1.0 - s0)
        dt1 = duv * yb * s1 * (1.0 - s1)
        return duv * s0, duv * s1, jnp.concatenate([dt0, dt1], axis=1), jnp.concatenate([_colsum(dt0), _colsum(dt1)], axis=1)

    dy_a, dy_b, d_gates, d_b_gate = row(
        gate_bwd, name="gate_bwd", ins=gate_ins + [(y_a, D_MODEL, 0, "row"), (y_b, D_MODEL, 0, "row"), (du, D_MODEL, 0, "row")],
        outs=[(D_MODEL, BF16), (D_MODEL, BF16), (2 * D_MODEL, BF16)], sums=[2 * D_MODEL])
    d_w_oa = mm(o_a, dy_a, mode="tn", name="d_w_o_mla", tm=1024, tn=1024, tk=1024)
    d_w_ob = mm(o_b, dy_b, mode="tn", name="d_w_o_dil", tm=1024, tn=1024, tk=1024)
    grads = dict(w_o_mla=_split_cols(d_w_oa), w_o_dil=_split_cols(d_w_ob),
                 w_out=d_w_out.reshape(N_CHIPS, D_MODEL // N_CHIPS, D_MODEL), w_ff1=d_w_ff1, w_ff2=d_w_ff2.reshape(N_CHIPS, D_FF // N_CHIPS, D_MODEL))
    sent = None if early_grads is None else early_grads(grads)
    do_a = mm(dy_a, w_oa, mode="nt", name="d_o_mla", tm=1024, tn=1024, tk=1024, after=sent)
    do_b = mm(dy_b, w_ob, mode="nt", name="d_o_dil", tm=1024, tn=1024, tk=1024)
    dq_a, dk_a, dv_a = _attn_bwd(q_a, 0, k_a, 0, v_a, 0, o_a, do_a, lse_a, mla_bwd_tables, scale_mla,
                                 name="mla_bwd", batch=batch, seq=seq, out_dtype=F32, sub=2, wide_qk=True)
    going = None if early_grads_go is None else early_grads_go(dq_a)
    d_qkv_d = _attn_bwd(proj_d, 0, proj_d, 1, proj_d, 2, o_b, do_b, lse_b, dil_bwd_tables, scale_dil,
                        name="dil_bwd", batch=batch, seq=seq, out_dtype=BF16, sub=2, stride=4, after=going)

    def mla_post(dq, dk, c, up, dn, c_only):
        dqs = [_rope_bwd(dq[:, h * LANES:(h + 1) * LANES], c, up, dn) for h in range(N_HEADS)]
        dk_sum = dk[:, :LANES]
        for h in range(1, N_HEADS):
            dk_sum = dk_sum + dk[:, h * LANES:(h + 1) * LANES]
        return jnp.concatenate(dqs, axis=1), _rope_bwd(dk_sum, c_only, up, dn)

    dq_lin, d_kr = row(mla_post, name="mla_unrope",
                       ins=[(dq_a, D_MODEL, 0, "row"), (dk_a, D_MODEL, 0, "row"), pos(rope_c), pos(rope_up), pos(rope_dn), pos(rope_c_only)],
                       outs=[(N_HEADS * LANES, BF16), (LANES, BF16)])
    d_qn = mm(dq_lin, w_uq_pt, mode="nn", name="d_qn", tm=1024, tn=Q_LORA, tk=1024)
    d_kvn_k = mm(dk_a, w_uk_p, mode="nt", name="d_kvn_k", tm=1024, tn=KV_LORA, tk=1024)
    d_kvn = mm(dv_a, w_uv, mode="nt", name="d_kvn", tm=1024, tn=KV_LORA, tk=1024, extras=(d_kvn_k,), epilogue=lambda acc, e: (acc + e,))

    def rms_bwd(lora, dq, dkv, dkr, gq, gkv):
        dxq, dgq = _rms_bwd(lora[:, :Q_LORA], gq, dq)
        dxk, dgk = _rms_bwd(lora[:, Q_LORA:], gkv, dkv)
        tail = jnp.zeros((dxq.shape[0], P_GATE - P_KR - LANES), F32)
        return jnp.concatenate([dxq, dxk, dkr.astype(F32), tail], axis=1), dgq, dgk

    d_tail, d_g_q_a, d_g_kv_a = row(
        rms_bwd, name="mla_rms_bwd",
        ins=[(proj, LORA_W, lora0, "row"), (d_qn, Q_LORA, 0, "row"), (d_kvn, KV_LORA, 0, "row"), (d_kr, LANES, 0, "row"),
             (g_q_a, 0, 0, "full"), (g_kv_a, 0, 0, "full")],
        outs=[(P_GATE, BF16)], sums=[Q_LORA, KV_LORA])
    d_proj = [d_tail, d_gates, d_qkv_d]
    d_w_in_pt = mm(d_proj, x_mm, mode="tn", name="d_w_in", tm=1024, tn=1024, tk=2048)
    d_w_in = _unpad_w_in_t(d_w_in_pt).reshape(N_CHIPS, IN_WIDTH // N_CHIPS, D_MODEL)
    moving = None if first_grad is None else first_grad(d_w_in)
    d_w_uq_pt = mm(dq_lin, qn, mode="tn", name="d_w_uq", tm=1024, tn=Q_LORA, tk=1024, after=moving)
    d_w_uk_p = mm(kvn, dk_a, mode="tn", name="d_w_uk", tm=KV_LORA, tn=1024, tk=1024, after=moving)
    d_w_uv = mm(kvn, dv_a, mode="tn", name="d_w_uv", tm=KV_LORA, tn=1024, tk=1024, after=moving)
    grads.update(w_in=d_w_in,
                 w_uq=_unpad_head_rows(d_w_uq_pt, MLA_QK).reshape(N_CHIPS, N_HEADS * MLA_QK // N_CHIPS, Q_LORA),
                 w_ukv=_split_cols(_merge_ukv(d_w_uk_p, d_w_uv)))
    leaving = None if last_grads is None else last_grads(grads)
    grad_x = mm(d_proj, jnp.concatenate([w_main_t, w_dil_t], axis=0), mode="nn", name="d_x", tm=1024, tn=1024, tk=1024, extras=(dr1,), epilogue=lambda acc, rv: (acc + ALPHA * rv,),
                after=leaving)

    grads.update(
        b_gate=d_b_gate.reshape(2, D_MODEL), g_q_a=d_g_q_a, g_kv_a=d_g_kv_a, ln1_g=d_ln1_g, ln1_b=d_ln1_b, ln2_g=d_ln2_g, ln2_b=d_ln2_b)
    return loss_l, grad_x.reshape(batch, seq, D_MODEL), grads


BIG = ("w_in", "w_uq", "w_ukv", "w_o_mla", "w_o_dil", "w_out", "w_ff1", "w_ff2")
SMALL = (("b_gate", 2 * D_MODEL), ("g_q_a", Q_LORA), ("g_kv_a", KV_LORA), ("ln1_g", D_MODEL), ("ln1_b", D_MODEL),
         ("ln2_g", D_MODEL), ("ln2_b", D_MODEL))
TRANSPOSED = ("w_in", "w_uq")
D2D_PIECES = (4, 2, 1)
ANY = pl.BlockSpec(memory_space=pl.ANY)
SIDE_EFFECTS = pltpu.CompilerParams(has_side_effects=True)


def _place():
    x, y, c = lax.axis_index("x"), lax.axis_index("y"), lax.axis_index("c")
    return x, y, c, ((1 - x, y), (x, 1 - y), (1 - x, 1 - y))


def _half_axis(shape):
    return 0 if shape[0] % 32 == 0 else 1


def _half_shape(shape):
    return (shape[0] // 2, shape[1]) if _half_axis(shape) == 0 else (shape[0], shape[1] // 2)


def _window(ref, lead, shape, which=None, pieces=False):
    axis = _half_axis(shape)
    size = shape[axis] if which is None else shape[axis] // 2
    base = 0 if which is None else which * size
    tile = (16, LANES)[axis]
    count = next(c for c in D2D_PIECES if size % (tile * c) == 0) if pieces else 1
    step = size // count
    spans = [pl.ds(pl.multiple_of(base + i * step, tile), step) for i in range(count)]
    refs = [ref.at[(*lead, s)] if axis == 0 else ref.at[(*lead, slice(None), s)] for s in spans]
    return refs if pieces else refs[0]


def _remote(src, dst, send, recv, to):
    return pltpu.make_async_remote_copy(src_ref=src, dst_ref=dst, send_sem=send, recv_sem=recv, device_id=to, device_id_type=MESH)


def _pair_split(grads, name):
    n = len(grads)

    def body(*refs):
        srcs, outs, (send, recv) = refs[:n], refs[n:2 * n], refs[2 * n:]
        x, y, c, _ = _place()
        for t in range(n):
            for s in range(N_CHIPS):
                _remote(_window(srcs[t], (s,), grads[t].shape[1:], 1 - c), outs[t].at[s], send.at[t], recv.at[t], (x, y, 1 - c)).start()
        for t in range(n):
            _remote(_window(srcs[t], (slice(None),), grads[t].shape[1:], 1 - c), outs[t], send.at[t], recv.at[t], (x, y, 1 - c)).wait()

    return pl.pallas_call(
        body, name=name, in_specs=[ANY] * n, out_specs=[ANY] * n,
        out_shape=[jax.ShapeDtypeStruct((N_CHIPS,) + _half_shape(g.shape[1:]), g.dtype) for g in grads],
        scratch_shapes=[pltpu.SemaphoreType.DMA((n,)), pltpu.SemaphoreType.DMA((n,))],
        compiler_params=SIDE_EFFECTS,
    )(*grads)


HBM = pl.BlockSpec(memory_space=pltpu.HBM)
SEM = pl.BlockSpec(memory_space=pltpu.SEMAPHORE)
SPLIT = pltpu.CompilerParams(has_side_effects=pltpu.SideEffectType.DATAFLOW_SIDE_EFFECTING)


def _in_hbm(a):
    return pltpu.with_memory_space_constraint(a, pltpu.HBM)


def _split_copies(kind, srcs, lands):
    x, y, c, chips = _place()
    out = []
    for t in range(len(srcs)):
        if kind == "pair":
            out += [(t, s % 3, _window(srcs[t], (s,), srcs[t].shape[1:], 1 - c), lands[t].at[s], (x, y, 1 - c)) for s in range(N_CHIPS)]
            continue
        if kind == "join":
            out += [(t, 0, a, b, (x, y, 1 - c)) for a, b in zip(_window(srcs[t], (), srcs[t].shape, None, True), _window(lands[t], (), srcs[t].shape, None, True))]
            continue
        if kind == "forward":
            shape, sibling = srcs[t].shape, (x, y, 1 - c)
            out += [(t, 0, a, b, sibling) for a, b in zip(_window(srcs[t], (), shape, None, True), _window(lands[t], (2 * x + y,), shape, None, True))]
            out += [(t, j, a, a, sibling) for j, (cx, cy) in enumerate(chips) for a in _window(lands[t], (2 * cx + cy,), shape, c, True)]
            continue
        for j, (cx, cy) in enumerate(chips):
            if kind == "gather":
                shape = srcs[t].shape
                out.append((t, j, _window(srcs[t], (), shape, c), _window(lands[t], (2 * x + y,), shape, c), (cx, cy, c)))
            else:
                out.append((t, j, srcs[t].at[2 * cx + cy], lands[t].at[j], (cx, cy, c)))
    return out


def _split_start(kind, srcs, land_shapes, name, lands=None, after=None):
    n = len(srcs)

    def body(*refs):
        src_refs, land_refs, sems, token = refs[:n], refs[n:2 * n], refs[-7 - 2 * n:-1 - 2 * n], refs[-1]
        for t, j, s, d, to in _split_copies(kind, src_refs, land_refs):
            _remote(s, d, sems[j], sems[3 + j], to).start()
        token[...] = jnp.zeros_like(token)

    lands = [_in_hbm(lax.empty(s.shape, s.dtype)) for s in land_shapes] if lands is None else list(lands)
    thru = [pltpu.HBM(a.shape, a.dtype) for a in list(srcs) + lands]
    res = pl.pallas_call(
        body, name=name,
        out_shape=(*[pltpu.SemaphoreType.DMA(())] * 6, *thru, jax.ShapeDtypeStruct((8, LANES), F32)),
        in_specs=[HBM] * (2 * n) + [ANY] * (after is not None), out_specs=(*[SEM] * 6, *[HBM] * (2 * n), pl.BlockSpec(memory_space=pltpu.VMEM)),
        input_output_aliases={i: 6 + i for i in range(2 * n)}, compiler_params=SPLIT,
    )(*[_in_hbm(s) for s in srcs], *lands, *([after] if after is not None else []))
    return res[:6], res[6:6 + n], res[6 + n:6 + 2 * n], res[-1]


def _split_wait(kind, sems, srcs, lands, after, name):
    n = len(srcs)
    afters = list(after) if isinstance(after, (list, tuple)) else [after]

    def body(*refs):
        src_refs, land_refs, sem_refs = refs[:n], refs[n:2 * n], refs[2 * n:2 * n + 6]
        for t, j, s, d, to in _split_copies(kind, src_refs, land_refs):
            cp = _remote(s, d, sem_refs[j], sem_refs[3 + j], to)
            cp.wait_send()
            cp.wait_recv()

    res = pl.pallas_call(
        body, name=name, out_shape=[pltpu.HBM(a.shape, a.dtype) for a in list(srcs) + list(lands)],
        in_specs=[HBM] * (2 * n) + [SEM] * 6 + [ANY] * len(afters), out_specs=[HBM] * (2 * n),
        input_output_aliases={i: i for i in range(2 * n)}, compiler_params=SPLIT,
    )(*srcs, *lands, *sems, *afters)
    return res[:n], res[n:]


def _sum_all_devices(vec, name, after):
    n_rows = vec.shape[0]

    def body(v_ref, *rest):
        out_ref, buf, send, recv = rest[len(after):]
        x, y, c, _ = _place()
        me = 4 * x + 2 * y + c
        buf[me] = v_ref[...]
        flips = [(a, b, d) for a in (0, 1) for b in (0, 1) for d in (0, 1)][1:]
        copies = []
        for r, (a, b, d) in enumerate(flips):
            px, py, pc = (1 - x if a else x), (1 - y if b else y), (1 - c if d else c)
            copies.append(pltpu.make_async_remote_copy(src_ref=v_ref, dst_ref=buf.at[me], send_sem=send.at[r], recv_sem=recv.at[r],
                                                       device_id=(px, py, pc), device_id_type=MESH))
            copies[-1].start()
        for r, (a, b, d) in enumerate(flips):
            px, py, pc = (1 - x if a else x), (1 - y if b else y), (1 - c if d else c)
            pltpu.make_async_remote_copy(src_ref=v_ref, dst_ref=buf.at[4 * px + 2 * py + pc], send_sem=send.at[r], recv_sem=recv.at[r],
                                         device_id=(px, py, pc), device_id_type=MESH).wait_recv()
        for cp in copies:
            cp.wait_send()
        total = buf[0]
        for k in range(1, N_DEV):
            total = total + buf[k]
        out_ref[...] = total

    vmem = pl.BlockSpec(memory_space=pltpu.VMEM)
    return pl.pallas_call(
        body, name=name, in_specs=[vmem] + [ANY] * len(after), out_specs=vmem, out_shape=jax.ShapeDtypeStruct(vec.shape, F32),
        scratch_shapes=[pltpu.VMEM((N_DEV, n_rows, LANES), F32), pltpu.SemaphoreType.DMA((N_DEV - 1,)), pltpu.SemaphoreType.DMA((N_DEV - 1,))],
        compiler_params=pltpu.CompilerParams(has_side_effects=True),
    )(vec, *after)


def _half_tile(half, width):
    t = half
    while t * width * 4 > (2 << 20) and t % 32 == 0:
        t //= 2
    return t


def _pair_add(g, theirs, core, name):
    _, half, width = theirs.shape
    t = _half_tile(half, width)
    n = half // t

    def body(c_ref, a_ref, b_ref, o_ref):
        o_ref[...] = (a_ref[...] + b_ref[...]).astype(BF16)

    tile = pl.BlockSpec((1, t, width), lambda j, i, c_ref: (j, i, 0))
    if _half_axis(g.shape[1:]) == 0:
        mine = pl.BlockSpec((1, t, width), lambda j, i, c_ref: (j, c_ref[0] * n + i, 0))
    else:
        mine = pl.BlockSpec((1, t, width), lambda j, i, c_ref: (j, i, c_ref[0]))
    return pl.pallas_call(
        body, name=name,
        grid_spec=pltpu.PrefetchScalarGridSpec(num_scalar_prefetch=1, grid=(N_CHIPS, n), in_specs=[mine, tile], out_specs=tile),
        out_shape=jax.ShapeDtypeStruct(theirs.shape, BF16), compiler_params=_params(("parallel", "parallel")),
    )(core, g, theirs)


def _chip_sum(part, others, chip, name, after=None):
    _, half, width = part.shape
    t = _half_tile(half, width)

    def body(s_ref, mine, p0, p1, p2, *rest):
        o_ref = rest[-1]
        o_ref[...] = ((mine[0].astype(F32) + p0[0].astype(F32)) + p1[0].astype(F32)) + p2[0].astype(F32)

    return pl.pallas_call(
        body, name=name,
        grid_spec=pltpu.PrefetchScalarGridSpec(
            num_scalar_prefetch=1, grid=(half // t,),
            in_specs=[pl.BlockSpec((1, t, width), lambda i, s_ref: (s_ref[0], i, 0))]
            + [pl.BlockSpec((1, t, width), lambda i, s_ref, j=j: (j, i, 0)) for j in range(3)] + [pl.BlockSpec(memory_space=pl.ANY)] * (after is not None),
            out_specs=pl.BlockSpec((t, width), lambda i, s_ref: (i, 0))),
        out_shape=jax.ShapeDtypeStruct((half, width), F32), compiler_params=_params(("parallel",)),
    )(chip, part, others, others, others, *([after] if after is not None else []))


EARLY = ("w_in", "w_uq", "w_ukv")
LATE = ("w_o_mla", "w_o_dil", "w_out", "w_ff1", "w_ff2")


def _sum_small(vals):
    n_in = len(vals)
    n_rows = sum(a.shape[0] * a.shape[1] // LANES for a in vals)
    pad_rows = -(-n_rows // 8) * 8

    def chunks(refs):
        return [(ref, a, j) for ref in refs for a in range(ref.shape[0]) for j in range(ref.shape[1] // LANES)]

    def body(*refs):
        ins, outs, (buf, send, recv) = refs[:n_in], refs[n_in:2 * n_in], refs[2 * n_in:]
        x, y, c, _ = _place()
        me = 4 * x + 2 * y + c
        for r, (ref, a, j) in enumerate(chunks(ins)):
            buf[me, r:r + 1, :] = ref[a:a + 1, j * LANES:(j + 1) * LANES]
        if pad_rows > n_rows:
            buf[me, n_rows:pad_rows, :] = jnp.zeros((pad_rows - n_rows, LANES), F32)
        flips = [(a, b, d) for a in (0, 1) for b in (0, 1) for d in (0, 1)][1:]
        peers = [((1 - x if a else x), (1 - y if b else y), (1 - c if d else c)) for a, b, d in flips]
        copies = [_remote(buf.at[me], buf.at[me], send.at[r], recv.at[r], peer) for r, peer in enumerate(peers)]
        for cp in copies:
            cp.start()
        for r, (px, py, pc) in enumerate(peers):
            _remote(buf.at[me], buf.at[4 * px + 2 * py + pc], send.at[r], recv.at[r], (px, py, pc)).wait_recv()
        for cp in copies:
            cp.wait_send()
        total = buf[0]
        for k in range(1, N_DEV):
            total = total + buf[k]
        for r, (ref, a, j) in enumerate(chunks(outs)):
            ref[a:a + 1, j * LANES:(j + 1) * LANES] = total[r:r + 1, :]

    vmem = pl.BlockSpec(memory_space=pltpu.VMEM)
    return pl.pallas_call(
        body, name="sum_small", in_specs=[vmem] * n_in, out_specs=[vmem] * n_in,
        out_shape=[jax.ShapeDtypeStruct(a.shape, F32) for a in vals],
        scratch_shapes=[pltpu.VMEM((N_DEV, pad_rows, LANES), F32), pltpu.SemaphoreType.DMA((N_DEV - 1,)), pltpu.SemaphoreType.DMA((N_DEV - 1,))],
        compiler_params=SIDE_EFFECTS,
    )(*vals)


def _adam_math(w, g, m, v):
    nm = B1 * m + (1.0 - B1) * g
    nv = B2 * v + (1.0 - B2) * (g * g)
    m_hat = nm / (1.0 - B1 ** ADAM_STEP)
    v_hat = nv / (1.0 - B2 ** ADAM_STEP)
    return -LR * (m_hat / (jnp.sqrt(v_hat) + ADAM_EPS) + WD * w), nm, nv


def _adamw_big(w, mine, theirs, m, v, core, name, side_by_side=False):
    rows, width = w.shape
    if side_by_side:
        t = next(c for c in (152, 96, 64, 32, 16, 8) if rows % c == 0)
        hb = None
        half_spec = pl.BlockSpec((t, width // 2), lambda i, c_ref: (i, 0))
    else:
        t = next(c for c in (256, 128, 64, 32, 16, 8) if (rows // 2) % c == 0)
        hb = rows // 2 // t
        half_spec = pl.BlockSpec((t, width), lambda i, c_ref: (i % hb, 0))

    def body(c_ref, w_ref, a_ref, b_ref, m_ref, v_ref, g_ref, d_ref, nm_ref, nv_ref):
        south = c_ref[0] == 0
        if side_by_side:
            g = jnp.where(south, jnp.concatenate([a_ref[...], b_ref[...]], axis=1), jnp.concatenate([b_ref[...], a_ref[...]], axis=1))
        else:
            g = jnp.where((pl.program_id(0) < hb) == south, a_ref[...], b_ref[...])
        g_ref[...] = g
        d_ref[...], nm_ref[...], nv_ref[...] = _adam_math(w_ref[...], g, m_ref[...], v_ref[...])

    spec = pl.BlockSpec((t, width), lambda i, c_ref: (i, 0))
    return pl.pallas_call(
        body, name=name,
        grid_spec=pltpu.PrefetchScalarGridSpec(num_scalar_prefetch=1, grid=(rows // t,),
                                               in_specs=[spec, half_spec, half_spec, spec, spec], out_specs=[spec] * 4),
        out_shape=[jax.ShapeDtypeStruct(w.shape, F32)] * 4, compiler_params=_params(("parallel",)),
    )(core, w, mine, theirs, m, v)


def _adamw_small(ws, gs, ms, vs):
    n = len(ws)

    def body(*refs):
        for t in range(n):
            w_ref, g_ref, m_ref, v_ref = (refs[k * n + t] for k in range(4))
            d, nm, nv = _adam_math(w_ref[...], g_ref[...], m_ref[...], v_ref[...])
            refs[4 * n + t][...] = d
            refs[5 * n + t][...] = nm
            refs[6 * n + t][...] = nv

    vmem = pl.BlockSpec(memory_space=pltpu.VMEM)
    res = pl.pallas_call(body, name="adamw_small", in_specs=[vmem] * (4 * n), out_specs=[vmem] * (3 * n),
                         out_shape=[jax.ShapeDtypeStruct(a.shape, F32) for a in ws] * 3)(*ws, *gs, *ms, *vs)
    return res[:n], res[n:2 * n], res[2 * n:]


def kernel(x, w_in, b_gate, g_q_a, w_uq, g_kv_a, w_ukv, w_o_mla, w_o_dil, w_out, ln1_g, ln1_b, w_ff1, w_ff2, ln2_g, ln2_b, loss_target, m_w_in, m_b_gate, m_g_q_a, m_w_uq, m_g_kv_a, m_w_ukv, m_w_o_mla, m_w_o_dil, m_w_out, m_ln1_g, m_ln1_b, m_w_ff1, m_w_ff2, m_ln2_g, m_ln2_b, v_w_in, v_b_gate, v_g_q_a, v_w_uq, v_g_kv_a, v_w_ukv, v_w_o_mla, v_w_o_dil, v_w_out, v_ln1_g, v_ln1_b, v_w_ff1, v_w_ff2, v_ln2_g, v_ln2_b):
    order = ("w_in", "b_gate", "g_q_a", "w_uq", "g_kv_a", "w_ukv", "w_o_mla", "w_o_dil", "w_out", "ln1_g", "ln1_b", "w_ff1", "w_ff2", "ln2_g", "ln2_b")
    w = dict(w_in=w_in, b_gate=b_gate, g_q_a=g_q_a, w_uq=w_uq, g_kv_a=g_kv_a, w_ukv=w_ukv, w_o_mla=w_o_mla, w_o_dil=w_o_dil, w_out=w_out,
             ln1_g=ln1_g, ln1_b=ln1_b, w_ff1=w_ff1, w_ff2=w_ff2, ln2_g=ln2_g, ln2_b=ln2_b)
    m = dict(w_in=m_w_in, b_gate=m_b_gate, g_q_a=m_g_q_a, w_uq=m_w_uq, g_kv_a=m_g_kv_a, w_ukv=m_w_ukv, w_o_mla=m_w_o_mla, w_o_dil=m_w_o_dil,
             w_out=m_w_out, ln1_g=m_ln1_g, ln1_b=m_ln1_b, w_ff1=m_w_ff1, w_ff2=m_w_ff2, ln2_g=m_ln2_g, ln2_b=m_ln2_b)
    v = dict(w_in=v_w_in, b_gate=v_b_gate, g_q_a=v_g_q_a, w_uq=v_w_uq, g_kv_a=v_g_kv_a, w_ukv=v_w_ukv, w_o_mla=v_w_o_mla, w_o_dil=v_w_o_dil,
             w_out=v_w_out, ln1_g=v_ln1_g, ln1_b=v_ln1_b, w_ff1=v_w_ff1, w_ff2=v_w_ff2, ln2_g=v_ln2_g, ln2_b=v_ln2_b)
    chip = 2 * lax.axis_index("x") + lax.axis_index("y")
    south = (lax.axis_index("c") == 0).astype(F32)
    gate_w = D_MODEL // N_CHIPS

    core = lax.axis_index("c").astype(jnp.int32).reshape(1)
    turn = lambda n, a: a.T if n in TRANSPOSED else a
    early_shards = [turn(n, w[n][0]).astype(BF16) for n in EARLY]
    gathered = lambda group: [jax.ShapeDtypeStruct((N_CHIPS,) + s.shape, BF16) for s in group]
    e_sems, e_srcs, e_lands, e_token = _split_start("gather", early_shards, gathered(early_shards), "gather_early_start")
    e_token, late_f32, x_f32, positions = lax.optimization_barrier((e_token, [w[n][0] for n in LATE], x, jnp.arange(x.shape[1], dtype=F32)))
    late_shards, x_lo = [a.astype(BF16) for a in late_f32], x_f32.astype(BF16)
    rope = _rope_tables(positions)
    shards = dict(zip(EARLY + LATE, early_shards + late_shards))
    b_mine = lax.dynamic_update_slice(jnp.zeros((2, D_MODEL), F32), b_gate[0] * south, (0, chip * gate_w))
    b_full = _sum_all_devices(b_mine.reshape(-1, LANES), "gather_b_gate", [e_token, *late_shards, *rope, x_lo]).reshape(2, D_MODEL)
    e_srcs, e_lands = _split_wait("gather", e_sems, e_srcs, e_lands, b_full, "gather_early_wait")
    e_forward = _split_start("forward", e_srcs, None, "gather_early_forward_start", lands=e_lands)
    first = dict(zip(EARLY, _split_wait("forward", *e_forward[:3], e_forward[-1], "gather_early_forward_wait")[1]))
    g_sems, g_srcs, g_lands, g_token = _split_start("gather", late_shards, gathered(late_shards), "gather_late_start", after=first[EARLY[-1]])
    tables = _attention_tables(x.shape[1])

    sent = {}

    def late_arrived(after):
        srcs, lands = _split_wait("gather", g_sems, g_srcs, g_lands, after, "gather_late_wait")
        sent["forward"] = _split_start("forward", srcs, None, "gather_late_forward_start", lands=lands)
        return sent["forward"][-1]

    def late_weights(after):
        return dict(zip(LATE, _split_wait("forward", *sent["forward"][:3], after, "gather_late_forward_wait")[1]))

    exchange_shapes = lambda parts: [jax.ShapeDtypeStruct((3,) + p.shape[1:], BF16) for p in parts]

    def early_grads(grads_late):
        gs = [grads_late[n] for n in LATE]
        shapes = [jax.ShapeDtypeStruct((N_CHIPS,) + _half_shape(g.shape[1:]), F32) for g in gs]
        sent["pair"] = _split_start("pair", gs, shapes, "pair_split_late_start")
        return sent["pair"][-1]

    def early_grads_go(after):
        gs, theirs = _split_wait("pair", *sent["pair"][:3], after, "pair_split_late_wait")
        parts = [_pair_add(g, th, core, "pair_add_" + n) for g, th, n in zip(gs, theirs, LATE)]
        sent["late"] = _split_start("scatter", parts, exchange_shapes(parts), "exchange_late_start")
        return sent["late"][-1]

    def first_grad(g):
        sent["first"] = _split_start("pair", [g], [jax.ShapeDtypeStruct((N_CHIPS,) + _half_shape(g.shape[1:]), F32)], "pair_split_w_in_start")
        return sent["first"][-1]

    def last_grads(grads_early):
        rest = [grads_early[n] for n in EARLY[1:]]
        gs, theirs = _split_wait("pair", *sent["first"][:3], rest[-1], "pair_split_w_in_wait")
        gs, theirs = list(gs) + rest, list(theirs) + list(_pair_split(rest, "pair_split_early"))
        parts = [_pair_add(g, th, core, "pair_add_" + n) for g, th, n in zip(gs, theirs, EARLY)]
        sent["early"] = _split_start("scatter", parts, exchange_shapes(parts), "exchange_early_start")
        return sent["early"][-1]

    loss_part, grad_x, grads = _local_step(x, loss_target, first, b_full, g_q_a, g_kv_a, ln1_g, ln1_b, ln2_g, ln2_b, token=g_token,
                                           late_arrived=late_arrived, late_weights=late_weights, early_grads=early_grads, early_grads_go=early_grads_go,
                                           first_grad=first_grad, last_grads=last_grads, tables=tables, rope=rope, x_lo=x_lo)

    g_out, delta, new_m, new_v = {}, {}, {}, {}
    chip1 = chip.astype(jnp.int32).reshape(1)

    def sum_and_send(names, parts, others, tag):
        totals = [_chip_sum(p, o, chip1, "chip_sum_" + n) for n, p, o in zip(names, parts, others)]
        return _split_start("join", totals, [jax.ShapeDtypeStruct(t.shape, F32) for t in totals], "pair_join_" + tag + "_start")

    def adam(names, joined, after, tag):
        totals, halves = _split_wait("join", *joined[:3], after, "pair_join_" + tag + "_wait")
        for n, mine, theirs in zip(names, totals, halves):
            res = _adamw_big(turn(n, w[n][0]), mine, theirs, turn(n, m[n][0]), turn(n, v[n][0]), core, "adamw_" + n,
                             side_by_side=mine.shape[0] == shards[n].shape[0])
            g_out[n], delta[n], new_m[n], new_v[n] = (turn(n, r) for r in res)

    late_joined = sum_and_send(LATE, *_split_wait("scatter", *sent["late"][:3], grad_x, "exchange_late_wait"), "late")
    early_joined = sum_and_send(EARLY, *_split_wait("scatter", *sent["early"][:3], late_joined[-1], "exchange_early_wait"), "early")
    small_names = [name for name, _ in SMALL]
    sums = _sum_small([grads[name] for name in small_names] + [loss_part])
    loss = sums[-1][0, 0]
    g_small = dict(zip(small_names, sums))
    g_small["b_gate"] = lax.dynamic_slice(g_small["b_gate"], (0, chip * gate_w), (2, gate_w))
    flat = lambda a: a.reshape(-1, a.shape[-1])
    res = _adamw_small(*[[flat(d[name]) for name in small_names] for d in (w, g_small, m, v)])
    g_out.update(g_small)
    for d, r in zip((delta, new_m, new_v), res):
        d.update(zip(small_names, r))
    adam(LATE, late_joined, res[0][0], "late")
    adam(EARLY, early_joined, delta[LATE[-1]], "early")

    lead = lambda d: [d[name].reshape(w[name].shape) for name in order]
    return (loss, grad_x, *lead(g_out), *lead(delta), *lead(new_m), *lead(new_v))
```

```python
import functools
import math

import jax
import jax.numpy as jnp
import numpy as np
from jax import lax
from jax.experimental import pallas as pl
from jax.experimental.pallas import tpu as pltpu

F32 = jnp.float32
BF16 = jnp.bfloat16
MESH = pl.DeviceIdType.MESH

D_MODEL = 1024
N_HEADS = 8
LANES = 128
NOPE, ROPE, V_DIM = 64, 32, 64
MLA_QK = NOPE + ROPE
Q_LORA, KV_LORA = 384, 256
DIL_DIM = 64
DIL_PATTERNS = ((128, 1), (512, 4), (2048, 16))
D_FF = 4096
N_CHIPS = 4
N_DEV = 8
IN_WIDTH = 4256
LN_EPS, RMS_EPS = 1e-5, 1e-6
NEG = -1e30
LOG2E, LN2 = 1.4426950408889634, 0.6931471805599453
ALPHA = 2.0 ** 0.25
ROPE_THETA = 10000.0
LR, B1, B2, ADAM_EPS, WD, ADAM_STEP = 0.001, 0.9, 0.999, 1e-8, 0.01, 10

P_LORA, P_KR, P_GATE, P_HALF = 0, 640, 1024, 3072
DIL_GROUP = 4 * LANES
P_DIL = N_HEADS // 2 * DIL_GROUP
LORA_W = Q_LORA + KV_LORA
KR_LANE = NOPE

ATT_T = 512
ROW_T = 512
ACC_COLS = 256
SQRT_FLOOR = 1e-36
VMEM_LIMIT = 56 * 1024 * 1024

NN = (((1,), (0,)), ((), ()))
NT = (((1,), (1,)), ((), ()))
TN = (((0,), (0,)), ((), ()))


def _params(sem=None, **kw):
    return pltpu.CompilerParams(dimension_semantics=sem, vmem_limit_bytes=VMEM_LIMIT, **kw)


def _matmul(a, b, *, mode, name, tm, tn, tk, out_dtypes=(F32,), extras=(), row_extras=(), epilogue=None, b_shards=False, out_shards=False, after=None):
    pieces = list(a) if isinstance(a, (list, tuple)) else [a]
    n_pc = len(pieces)
    a_shape = (pieces[0].shape[0], sum(p.shape[1] for p in pieces))
    if b_shards:
        n_sh, rows_b, cols_b = b.shape
        b_shape = (rows_b, n_sh * cols_b)
    else:
        b_shape = b.shape
    if mode == "nn":
        (m, k), (k2, n) = a_shape, b_shape
    elif mode == "nt":
        (m, k), (n, k2) = a_shape, b_shape
    else:
        (k, m), (k2, n) = a_shape, b_shape
    assert k == k2, (a_shape, b.shape, mode)
    tm, tn, tk = min(tm, m), min(tn, n), min(tk, k)
    assert m % tm == 0 and n % tn == 0 and k % tk == 0, (name, m, n, k, tm, tn, tk)
    nk = k // tk
    n_ex, n_out = len(extras) + len(row_extras), len(out_dtypes)
    n_in = n_pc + 1 + n_ex + (after is not None)
    dims = {"nn": NN, "nt": NT, "tn": TN}[mode]
    col_tile = tm if mode == "tn" else tk
    blocks = [p.shape[1] // col_tile for p in pieces]
    firsts = [sum(blocks[:p]) for p in range(n_pc)]
    assert all(p.shape[1] % col_tile == 0 for p in pieces), (name, col_tile)

    def body(*refs):
        a_refs, b_ref = refs[:n_pc], refs[n_pc]
        ex_refs = refs[n_pc + 1:n_pc + 1 + n_ex]
        out_refs = refs[n_in:n_in + n_out]

        def finish(acc):
            outs = epilogue(acc, *[r[...] for r in ex_refs]) if epilogue is not None else (acc,)
            for r, o in zip(out_refs, outs):
                r[...] = o.astype(r.dtype)

        kk = pl.program_id(2)

        def step(a_ref):
            if nk == 1:
                finish(lax.dot_general(a_ref[...].astype(BF16), b_ref[...].astype(BF16), dims, preferred_element_type=F32))
                return
            acc_ref = refs[-1]

            @pl.when(kk == 0)
            def _():
                acc_ref[...] = jnp.zeros_like(acc_ref)

            for c in range(0, tn, ACC_COLS):
                b_blk = b_ref[c:c + ACC_COLS, :] if mode == "nt" else b_ref[:, c:c + ACC_COLS]
                acc_ref[:, c:c + ACC_COLS] += lax.dot_general(a_ref[...].astype(BF16), b_blk.astype(BF16), dims, preferred_element_type=F32)

            @pl.when(kk == nk - 1)
            def _():
                finish(acc_ref[...])

        if n_pc == 1:
            step(a_refs[0])
        else:
            at = pl.program_id(0) if mode == "tn" else kk
            for p in range(n_pc):
                pl.when(jnp.logical_and(at >= firsts[p], at < firsts[p] + blocks[p]))(functools.partial(step, a_refs[p]))

    def a_spec_of(p):
        if n_pc == 1:
            return pl.BlockSpec((tk, tm), lambda i, j, kk: (kk, i)) if mode == "tn" else pl.BlockSpec((tm, tk), lambda i, j, kk: (i, kk))
        col = lambda at: jnp.clip(at - firsts[p], 0, blocks[p] - 1)
        mine = lambda at: jnp.logical_and(at >= firsts[p], at < firsts[p] + blocks[p])
        if mode == "tn":
            return pl.BlockSpec((tk, tm), lambda i, j, kk: (jnp.where(mine(i), kk, 0), col(i)))
        return pl.BlockSpec((tm, tk), lambda i, j, kk: (i, col(kk)))

    b_spec = {"nn": pl.BlockSpec((tk, tn), lambda i, j, kk: (kk, j)),
              "nt": pl.BlockSpec((tn, tk), lambda i, j, kk: (j, kk)),
              "tn": pl.BlockSpec((tk, tn), lambda i, j, kk: (kk, j))}[mode]
    tile = pl.BlockSpec((tm, tn), lambda i, j, kk: (i, j))
    out_spec, out_dims = tile, (m, n)
    if b_shards and mode == "nn":
        per = cols_b // tn
        b_spec = pl.BlockSpec((None, tk, tn), lambda i, j, kk: (j // per, kk, j % per))
    elif b_shards:
        assert mode == "nt"
        per = cols_b // tk
        b_spec = pl.BlockSpec((None, tn, tk), lambda i, j, kk: (kk // per, j, kk % per))
    if out_shards:
        assert not extras and epilogue is None
        per_out = n // N_CHIPS // tn
        out_spec = pl.BlockSpec((None, tm, tn), lambda i, j, kk: (j // per_out, i, j % per_out))
        out_dims = (N_CHIPS, m, n // N_CHIPS)
    outs = pl.pallas_call(
        body, name=name,
        grid=(m // tm, n // tn, nk),
        in_specs=[a_spec_of(p) for p in range(n_pc)] + [b_spec] + [tile] * len(extras) + [pl.BlockSpec((1, tn), lambda i, j, kk: (0, j))] * len(row_extras) + [pl.BlockSpec(memory_space=pl.ANY)] * (after is not None),
        out_specs=[out_spec] * n_out,
        out_shape=[jax.ShapeDtypeStruct(out_dims, dt) for dt in out_dtypes],
        scratch_shapes=[pltpu.VMEM((tm, tn), F32)] if nk > 1 else [],
        compiler_params=_params(("parallel", "parallel", "arbitrary")),
    )(*pieces, b, *extras, *row_extras, *([after] if after is not None else []))
    return outs[0] if n_out == 1 else outs


def _rowwise(fn, *, name, rows, seq, ins, outs, sums=()):
    tm = min(ROW_T, seq)
    n_pos = seq // tm
    n_in, n_out, n_sum = len(ins), len(outs), len(sums)

    def body(*refs):
        vals = fn(*[r[...] for r in refs[:n_in]])
        for r, v in zip(refs[n_in:n_in + n_out], vals[:n_out]):
            r[...] = v.astype(r.dtype)
        first = pl.program_id(0) == 0
        for r, v in zip(refs[n_in + n_out:], vals[n_out:]):
            @pl.when(first)
            def _(r=r, v=v):
                r[...] = v

            @pl.when(jnp.logical_not(first))
            def _(r=r, v=v):
                r[...] += v

    def spec(arr, width, col, kind):
        if kind == "row":
            return pl.BlockSpec((tm, width), lambda i, col=col: (i, col))
        if kind == "pos":
            return pl.BlockSpec((tm, width), lambda i, col=col: (i % n_pos, col))
        return pl.BlockSpec(arr.shape, lambda i: (0,) * arr.ndim)

    res = pl.pallas_call(
        body, name=name,
        grid=(rows // tm,),
        in_specs=[spec(*t) for t in ins],
        out_specs=[pl.BlockSpec((tm, w), lambda i: (i, 0)) for w, _ in outs]
        + [pl.BlockSpec((1, w), lambda i: (0, 0)) for w in sums],
        out_shape=[jax.ShapeDtypeStruct((rows, w), dt) for w, dt in outs]
        + [jax.ShapeDtypeStruct((1, w), F32) for w in sums],
        compiler_params=_params(("arbitrary",)),
    )(*[t[0] for t in ins])
    return res


def _colsum(v):
    return jnp.sum(v, axis=0, keepdims=True)


def _sqrt(v):
    return v * lax.rsqrt(jnp.maximum(v, SQRT_FLOOR))


def _rope_fwd(t, c, s_up, s_dn):
    return t * c + pltpu.roll(t, LANES - 16, 1) * s_up + pltpu.roll(t, 16, 1) * s_dn


def _rope_bwd(d, c, s_up, s_dn):
    return d * c + pltpu.roll(d * s_up, 16, 1) + pltpu.roll(d * s_dn, LANES - 16, 1)


def _rope_tables(positions):
    seq, half = positions.shape[0], ROPE // 2
    inv = jnp.power(ROPE_THETA, -jnp.arange(half, dtype=F32) / half)
    ang = positions[:, None] * inv[None, :]
    cos, sin = jnp.cos(ang), jnp.sin(ang)
    zeros = jnp.zeros((seq, half), F32)
    lo, hi = jnp.ones((seq, KR_LANE), F32), jnp.ones((seq, LANES - KR_LANE - ROPE), F32)
    c = jnp.concatenate([lo, cos, cos, hi], axis=1)
    c_rope_only = jnp.concatenate([0 * lo, cos, cos, 0 * hi], axis=1)
    s_up = jnp.concatenate([0 * lo, -sin, zeros, 0 * hi], axis=1)
    s_dn = jnp.concatenate([0 * lo, zeros, sin, 0 * hi], axis=1)
    return c, s_up, s_dn, c_rope_only


def _rms(x, g):
    r = lax.rsqrt(jnp.mean(x * x, axis=1, keepdims=True) + RMS_EPS)
    return x * r * g


def _rms_bwd(x, g, dy):
    r = lax.rsqrt(jnp.mean(x * x, axis=1, keepdims=True) + RMS_EPS)
    xh = x * r
    dxh = dy * g
    dx = r * (dxh - xh * jnp.mean(dxh * xh, axis=1, keepdims=True))
    return dx, _colsum(dy * xh)


def _ln_stats(x):
    mu = jnp.mean(x, axis=1, keepdims=True)
    xc = x - mu
    r = lax.rsqrt(jnp.mean(xc * xc, axis=1, keepdims=True) + LN_EPS)
    return xc * r, r


def _ln_bwd(xh, r, g, dy):
    dxh = dy * g
    dx = r * (dxh - jnp.mean(dxh, axis=1, keepdims=True) - xh * jnp.mean(dxh * xh, axis=1, keepdims=True))
    return dx, _colsum(dy * xh), _colsum(dy)


def _table_specs(tables, sub):
    whole = lambda a: pl.BlockSpec(a.shape, lambda b, g: (0,) * a.ndim)
    if len(tables) == 1:
        return [whole(tables[0])]
    return [whole(tables[0]), whole(tables[1]), pl.BlockSpec((sub, 1, LANES), lambda b, g: (g, 0, 0))]


def _biased(s, table_refs, delta, head):
    if delta < table_refs[0].shape[0]:
        s = s + table_refs[0][delta]
    if len(table_refs) == 3:
        s = s - table_refs[2][head, 0:1, 0:1] * table_refs[1][delta]
    return s


def _lane_masks(sub):
    lane = lax.broadcasted_iota(jnp.int32, (1, LANES), 1)
    return [(lane // (LANES // sub) == a).astype(F32) for a in range(sub)]


def _attn_fwd(q, qb0, k, kb0, v, vb0, tables, scale, *, name, batch, seq, sub=1, stride=1, wide_qk=False, after=None):
    t = ATT_T
    nq = seq // t
    rows = batch * seq
    n_tab = len(tables)
    qk_w = sub * LANES if wide_qk else LANES

    def body(q_ref, k_ref, v_ref, *rest):
        table_refs = rest[:n_tab]
        o_ref, lse_ref, vtb = rest[n_tab + (after is not None):][:3]
        per_head = rest[n_tab + (after is not None) + 3:]
        qbs, kbs = per_head[:sub], per_head[sub:]
        masks = _lane_masks(sub)
        for a in range(sub):
            lanes = slice(a * LANES, (a + 1) * LANES) if wide_qk else slice(None)
            qa = q_ref[:, lanes]
            qbs[a][...] = (qa.astype(F32) * masks[a]).astype(BF16) if sub > 1 and not wide_qk else qa.astype(BF16)
            if wide_qk or a == 0:
                kbs[a][...] = k_ref[:, lanes].astype(BF16)
        vtb[...] = v_ref[...].astype(F32).T.astype(BF16)
        for i in range(nq):
            out_t = None
            for a in range(sub):
                qt, kb = qbs[a][i * t:(i + 1) * t, :], kbs[a if wide_qk else 0]
                logits = [_biased(lax.dot_general(kb[j * t:(j + 1) * t, :], qt, NT, preferred_element_type=F32) * (scale * LOG2E), table_refs, i - j, a)
                          for j in range(i + 1)]
                m = jnp.max(functools.reduce(jnp.maximum, logits), axis=0, keepdims=True)
                ps = [jnp.exp2(s - m) for s in logits]
                l = jnp.sum(functools.reduce(jnp.add, ps), axis=0, keepdims=True)
                acc = functools.reduce(jnp.add, [lax.dot_general(vtb[:, j * t:(j + 1) * t], p.astype(BF16), NN, preferred_element_type=F32)
                                                 for j, p in enumerate(ps)])
                part = acc / l if sub == 1 else (acc / l) * masks[a].T
                out_t = part if out_t is None else out_t + part
                lse_ref[i * t:(i + 1) * t, a * LANES:(a + 1) * LANES] = jnp.broadcast_to((m + jnp.log2(l)) * LN2, (LANES, t)).T
            o_ref[i * t:(i + 1) * t, :] = out_t.T

    slab = lambda b0, step, width=LANES: pl.BlockSpec((seq, width), lambda b, g: (b, b0 + step * g))
    groups = N_HEADS // sub
    n_k = sub if wide_qk else 1
    return pl.pallas_call(
        body, name=name,
        grid=(batch, groups),
        in_specs=[slab(qb0, stride, qk_w), slab(kb0, stride, qk_w), slab(vb0, stride)] + _table_specs(tables, sub)
        + [pl.BlockSpec(memory_space=pl.ANY)] * (after is not None),
        out_specs=[slab(0, 1), slab(0, 1, sub * LANES)],
        out_shape=[jax.ShapeDtypeStruct((rows, groups * LANES), F32), jax.ShapeDtypeStruct((rows, N_HEADS * LANES), F32)],
        scratch_shapes=[pltpu.VMEM((LANES, seq), BF16)] + [pltpu.VMEM((seq, LANES), BF16)] * (sub + n_k),
        compiler_params=_params(("arbitrary", "arbitrary")),
    )(q, k, v, *tables, *([after] if after is not None else []))


def _attn_bwd(q, qb0, k, kb0, v, vb0, o, do, lse, tables, scale, *, name, batch, seq, out_dtype, sub=1, stride=1, wide_qk=False, after=None):
    t = ATT_T
    nq = seq // t
    rows = batch * seq
    n_tab = len(tables)
    groups = N_HEADS // sub
    packed = sub > 1 and not wide_qk
    n_out = 1 if packed else 3
    n_k = sub if wide_qk else 1
    qk_w = sub * LANES if wide_qk else LANES

    def body(q_ref, k_ref, v_ref, o_ref, do_ref, lse_ref, *rest):
        table_refs = rest[:n_tab]
        rest = rest[n_tab + (after is not None):]
        out_refs, (vb, dva), rest = rest[:n_out], rest[n_out:n_out + 2], rest[n_out + 2:]
        kbs, dkas, rest = rest[:n_k], rest[n_k:2 * n_k], rest[2 * n_k:]
        qbs, dobs, qtbs, dotbs = (rest[g * sub:(g + 1) * sub] for g in range(4))
        masks = _lane_masks(sub)
        vb[...] = v_ref[...].astype(BF16)
        for a in range(sub):
            lanes = slice(a * LANES, (a + 1) * LANES) if wide_qk else slice(None)
            qa = q_ref[:, lanes].astype(F32) * masks[a] if packed else q_ref[:, lanes].astype(F32)
            doa = do_ref[...] * masks[a] if sub > 1 else do_ref[...]
            qbs[a][...] = qa.astype(BF16)
            dobs[a][...] = doa.astype(BF16)
            qtbs[a][...] = qa.T.astype(BF16)
            dotbs[a][...] = doa.T.astype(BF16)
            if wide_qk or a == 0:
                kbs[a][...] = k_ref[:, lanes].astype(BF16)
        first_k, first_v = [[True] * nq for _ in range(n_k)], [True] * nq
        for i in range(nq):
            at = slice(i * t, (i + 1) * t)
            dq_all = None
            for a in range(sub):
                qt, dot, kb, dka = qbs[a][at, :], dobs[a][at, :], kbs[a if wide_qk else 0], dkas[a if wide_qk else 0]
                lse_t = lse_ref[at, a * LANES:a * LANES + 1] * LOG2E
                od = o_ref[at, :] * do_ref[at, :]
                delta = jnp.sum(od * masks[a] if sub > 1 else od, axis=1, keepdims=True)
                dq = None
                for j in range(i + 1):
                    kat = slice(j * t, (j + 1) * t)
                    kt, vt = kb[kat, :], vb[kat, :]
                    p = jnp.exp2(_biased(lax.dot_general(qt, kt, NT, preferred_element_type=F32) * (scale * LOG2E), table_refs, i - j, a) - lse_t)
                    dp = lax.dot_general(dot, vt, NT, preferred_element_type=F32)
                    ds = (p * (dp - delta) * scale).astype(BF16)
                    dk_part = lax.dot_general(qtbs[a][:, at], ds, NN, preferred_element_type=F32)
                    dv_part = lax.dot_general(dotbs[a][:, at], p.astype(BF16), NN, preferred_element_type=F32)
                    firsts = first_k[a if wide_qk else 0]
                    if firsts[j]:
                        dka[:, kat] = dk_part
                        firsts[j] = False
                    else:
                        dka[:, kat] += dk_part
                    if first_v[j]:
                        dva[:, kat] = dv_part
                        first_v[j] = False
                    else:
                        dva[:, kat] += dv_part
                    dq_part = lax.dot_general(ds, kt, NN, preferred_element_type=F32)
                    dq = dq_part if dq is None else dq + dq_part
                if wide_qk:
                    out_refs[0][at, a * LANES:(a + 1) * LANES] = dq.astype(out_refs[0].dtype)
                else:
                    dq = dq * masks[a] if sub > 1 else dq
                    dq_all = dq if dq_all is None else dq_all + dq
            if not wide_qk:
                out_refs[0][at, 0:LANES] = dq_all.astype(out_refs[0].dtype)
        if packed:
            out_refs[0][:, LANES:2 * LANES] = dkas[0][...].T.astype(out_refs[0].dtype)
            out_refs[0][:, 2 * LANES:3 * LANES] = dva[...].T.astype(out_refs[0].dtype)
            out_refs[0][:, 3 * LANES:] = jnp.zeros((seq, LANES), out_refs[0].dtype)
        else:
            for a in range(n_k):
                out_refs[1][:, a * LANES:(a + 1) * LANES] = dkas[a][...].T.astype(out_refs[1].dtype)
            out_refs[2][...] = dva[...].T.astype(out_refs[2].dtype)

    slab = lambda b0, step, width=LANES: pl.BlockSpec((seq, width), lambda b, g: (b, b0 + step * g))
    if packed:
        out_specs = [slab(0, 1, 4 * LANES)]
        out_shape = [jax.ShapeDtypeStruct((rows, groups * 4 * LANES), out_dtype)]
    else:
        out_specs = [slab(0, 1, qk_w), slab(0, 1, qk_w), slab(0, 1)]
        out_shape = [jax.ShapeDtypeStruct((rows, N_HEADS * LANES), out_dtype)] * 2 + [jax.ShapeDtypeStruct((rows, groups * LANES), out_dtype)]
    res = pl.pallas_call(
        body, name=name,
        grid=(batch, groups),
        in_specs=[slab(qb0, stride, qk_w), slab(kb0, stride, qk_w), slab(vb0, stride), slab(0, 1), slab(0, 1), slab(0, 1, sub * LANES)]
        + _table_specs(tables, sub) + [pl.BlockSpec(memory_space=pl.ANY)] * (after is not None),
        out_specs=out_specs, out_shape=out_shape,
        scratch_shapes=[pltpu.VMEM((seq, LANES), BF16), pltpu.VMEM((LANES, seq), F32)]
        + [pltpu.VMEM((seq, LANES), BF16)] * n_k + [pltpu.VMEM((LANES, seq), F32)] * n_k
        + [pltpu.VMEM((seq, LANES), BF16)] * (2 * sub) + [pltpu.VMEM((LANES, seq), BF16)] * (2 * sub),
        compiler_params=_params(("arbitrary", "arbitrary")),
    )(q, k, v, o, do, lse, *tables, *([after] if after is not None else []))
    return res[0] if packed else res


def _attention_tables(seq):
    n = seq // ATT_T
    pos = np.arange(ATT_T, dtype=np.int32)
    dist = np.arange(n, dtype=np.int32)[:, None, None] * ATT_T + pos[None, :, None] - pos[None, None, :]
    causal = np.where(dist[:1] >= 0, 0.0, NEG).astype(np.float32)
    count = np.zeros(dist.shape, np.float32)
    for window, dilation in DIL_PATTERNS:
        count += ((dist >= 0) & (dist <= window) & (dist % dilation == 0)).astype(np.float32)
    held = np.where(count > 0, np.log2(np.maximum(count, 1.0)), NEG).astype(np.float32)
    far = dist.astype(np.float32) * np.float32(LOG2E)
    slopes = np.asarray([2.0 ** (-8.0 * (i + 1) / N_HEADS) for i in range(N_HEADS)], np.float32)
    slopes = jnp.asarray(np.broadcast_to(slopes[:, None, None], (N_HEADS, 1, LANES)))
    flat = lambda a: jnp.asarray(np.ascontiguousarray(a))
    turned = lambda a: flat(np.swapaxes(a, 1, 2))
    return ((flat(causal),), (turned(causal),)), ((flat(held), flat(far), slopes), (turned(held), turned(far), slopes))


def _pad_heads(w, width):
    kdim, n = w.shape[0], w.shape[1] // width
    return jnp.pad(w.reshape(kdim, n, width), ((0, 0), (0, 0), (0, LANES - width))).reshape(kdim, n * LANES)


def _unpad_heads(w, width):
    kdim, n = w.shape[0], w.shape[1] // LANES
    return w.reshape(kdim, n, LANES)[:, :, :width].reshape(kdim, n * width)


def _pad_head_rows(w, width):
    n, kdim = w.shape[0] // width, w.shape[1]
    return jnp.pad(w.reshape(n, width, kdim), ((0, 0), (0, LANES - width), (0, 0))).reshape(n * LANES, kdim)


def _unpad_head_rows(w, width):
    n, kdim = w.shape[0] // LANES, w.shape[1]
    return w.reshape(n, LANES, kdim)[:, :width].reshape(n * width, kdim)


def _pad_w_in_t(wt):
    n_qkv, pair = 3 * N_HEADS * DIL_DIM, 2 * DIL_DIM
    zeros = lambda n: jnp.zeros((n, wt.shape[1]), wt.dtype)
    main = jnp.concatenate([wt[:LORA_W], zeros(KR_LANE), wt[LORA_W:LORA_W + ROPE], zeros(P_GATE - P_KR - KR_LANE - ROPE),
                            wt[LORA_W + ROPE + n_qkv:]], axis=0)
    qkv = wt[LORA_W + ROPE:LORA_W + ROPE + n_qkv].reshape(3, N_HEADS // 2, pair, wt.shape[1]).transpose(1, 0, 2, 3)
    dil = jnp.pad(qkv, ((0, 0), (0, 1), (0, 0), (0, 0))).reshape(P_DIL, wt.shape[1])
    return main, dil


def _unpad_w_in_t(gt):
    n_qkv, pair = 3 * N_HEADS * DIL_DIM, 2 * DIL_DIM
    moves = [(P_LORA, 0, LORA_W), (P_KR + KR_LANE, LORA_W, ROPE)]
    moves += [(P_HALF + (g * 4 + j) * pair, LORA_W + ROPE + (j * (N_HEADS // 2) + g) * pair, pair) for j in range(3) for g in range(N_HEADS // 2)]
    moves += [(P_GATE + r, LORA_W + ROPE + n_qkv + r, ROW_T) for r in range(0, P_HALF - P_GATE, ROW_T)]
    n = len(moves)

    def body(src, out, buf, sem_in, sem_out):
        ins = [pltpu.make_async_copy(src.at[pl.ds(a, k)], buf.at[pl.ds(b, k)], sem_in.at[i]) for i, (a, b, k) in enumerate(moves)]
        outs = [pltpu.make_async_copy(buf.at[pl.ds(b, k)], out.at[pl.ds(b, k)], sem_out.at[i]) for i, (a, b, k) in enumerate(moves)]
        for cp in ins:
            cp.start()
        for cp_in, cp_out in zip(ins, outs):
            cp_in.wait()
            cp_out.start()
        for cp in outs:
            cp.wait()

    return pl.pallas_call(
        body, name="unpad_d_w_in", in_specs=[pl.BlockSpec(memory_space=pl.ANY)], out_specs=pl.BlockSpec(memory_space=pl.ANY),
        out_shape=jax.ShapeDtypeStruct((IN_WIDTH, gt.shape[1]), gt.dtype),
        scratch_shapes=[pltpu.VMEM((IN_WIDTH, gt.shape[1]), gt.dtype), pltpu.SemaphoreType.DMA((n,)), pltpu.SemaphoreType.DMA((n,))],
        compiler_params=_params(),
    )(gt)


def _split_ukv(w):
    w3 = w.reshape(w.shape[0], N_HEADS, NOPE + V_DIM)
    return _pad_heads(w3[:, :, :NOPE].reshape(w.shape[0], -1), NOPE), w3[:, :, NOPE:].reshape(w.shape[0], -1)


def _merge_ukv(g_k, g_v):
    kdim = g_k.shape[0]
    k3 = _unpad_heads(g_k, NOPE).reshape(kdim, N_HEADS, NOPE)
    return jnp.concatenate([k3, g_v.reshape(kdim, N_HEADS, V_DIM)], axis=2).reshape(kdim, N_HEADS * (NOPE + V_DIM))


def _join_cols(w):
    return w.transpose(1, 0, 2).reshape(w.shape[1], N_CHIPS * w.shape[2])


def _split_cols(g):
    return g.reshape(g.shape[0], N_CHIPS, g.shape[1] // N_CHIPS).transpose(1, 0, 2)


def _local_step(x3, target3, wg, b_gate, g_q_a, g_kv_a, ln1_g, ln1_b, ln2_g, ln2_b, token=None, late_arrived=None, late_weights=None,
                early_grads=None, early_grads_go=None, first_grad=None, last_grads=None, tables=None, rope=None, x_lo=None):
    w_main_t, w_dil_t = _pad_w_in_t(wg["w_in"].reshape(IN_WIDTH, D_MODEL))
    w_uq_pt = _pad_head_rows(wg["w_uq"].reshape(N_HEADS * MLA_QK, Q_LORA), MLA_QK)
    w_ukv = _join_cols(wg["w_ukv"])
    batch, seq, _ = x3.shape
    rows = batch * seq
    x = x3.reshape(rows, D_MODEL)
    x_mm = x if x_lo is None else x_lo.reshape(rows, D_MODEL)
    target = target3.reshape(rows, D_MODEL)
    row = functools.partial(_rowwise, rows=rows, seq=seq)
    mm = _matmul

    w_uk_p, w_uv = _split_ukv(w_ukv)
    b0, b1 = b_gate[0:1], b_gate[1:2]
    rope_c, rope_up, rope_dn, rope_c_only = _rope_tables(jnp.arange(seq, dtype=F32)) if rope is None else rope
    (mla_bwd_tables, mla_fwd_tables), (dil_bwd_tables, dil_fwd_tables) = _attention_tables(seq) if tables is None else tables
    scale_mla, scale_dil = MLA_QK ** -0.5, DIL_DIM ** -0.5
    lora0, kr0, gate0 = P_LORA // LORA_W, P_KR // LANES, P_GATE // D_MODEL

    proj = mm(x_mm, w_main_t, mode="nt", name="proj", tm=1024, tn=1536, tk=1024, after=token)
    proj_d = mm(x_mm, w_dil_t, mode="nt", name="proj_dil", tm=1024, tn=1024, tk=1024, out_dtypes=(BF16,))

    def prep(lora, gq, gkv):
        return _rms(lora[:, :Q_LORA], gq), _rms(lora[:, Q_LORA:], gkv)

    qn, kvn = row(prep, name="mla_rms", ins=[(proj, LORA_W, lora0, "row"), (g_q_a, 0, 0, "full"), (g_kv_a, 0, 0, "full")],
                  outs=[(Q_LORA, BF16), (KV_LORA, BF16)])
    q_lin = mm(qn, w_uq_pt, mode="nt", name="q_up", tm=1024, tn=1024, tk=Q_LORA)
    k_lin = mm(kvn, w_uk_p, mode="nn", name="k_up", tm=1024, tn=1024, tk=KV_LORA)
    v_a = mm(kvn, w_uv, mode="nn", name="v_up", tm=1024, tn=1024, tk=KV_LORA, out_dtypes=(BF16,))

    def rope_qk(ql, kl, kr, c, up, dn):
        k_rot = _rope_fwd(kr, c, up, dn)
        qs = [_rope_fwd(ql[:, h * LANES:(h + 1) * LANES], c, up, dn) for h in range(N_HEADS)]
        ks = [kl[:, h * LANES:(h + 1) * LANES] + k_rot for h in range(N_HEADS)]
        return jnp.concatenate(qs, axis=1), jnp.concatenate(ks, axis=1)

    pos = lambda tab: (tab, LANES, 0, "pos")
    q_a, k_a = row(rope_qk, name="rope_qk",
                   ins=[(q_lin, D_MODEL, 0, "row"), (k_lin, D_MODEL, 0, "row"), (proj, LANES, kr0, "row"), pos(rope_c), pos(rope_up), pos(rope_dn)],
                   outs=[(N_HEADS * LANES, BF16), (N_HEADS * LANES, BF16)])
    o_a, lse_a = _attn_fwd(q_a, 0, k_a, 0, v_a, 0, mla_fwd_tables, scale_mla, name="mla_fwd", batch=batch, seq=seq, sub=2, wide_qk=True)
    arrived = None if late_arrived is None else late_arrived(o_a)
    o_b, lse_b = _attn_fwd(proj_d, 0, proj_d, 1, proj_d, 2, dil_fwd_tables, scale_dil, name="dil_fwd", batch=batch, seq=seq, sub=2, stride=4, after=arrived)
    late = wg if late_weights is None else late_weights(o_b)
    w_oa = _join_cols(late["w_o_mla"])
    w_ob = _join_cols(late["w_o_dil"])
    w_out, w_ff1, w_ff2 = late["w_out"].reshape(D_MODEL, D_MODEL), late["w_ff1"], late["w_ff2"].reshape(D_FF, D_MODEL)
    y_a = mm(o_a, w_oa, mode="nn", name="o_mla", tm=1024, tn=1024, tk=1024, out_dtypes=(BF16,))
    y_b = mm(o_b, w_ob, mode="nn", name="o_dil", tm=1024, tn=1024, tk=1024, out_dtypes=(BF16,))

    def gate(t0, t1, c0, c1, ya, yb):
        return (jax.nn.sigmoid(t0 + c0) * ya + jax.nn.sigmoid(t1 + c1) * yb,)

    gate_ins = [(proj, D_MODEL, gate0, "row"), (proj, D_MODEL, gate0 + 1, "row"), (b0, 0, 0, "full"), (b1, 0, 0, "full")]
    (u,) = row(gate, name="gate", ins=gate_ins + [(y_a, D_MODEL, 0, "row"), (y_b, D_MODEL, 0, "row")], outs=[(D_MODEL, BF16)])
    def ln1(mv, xv, g, b):
        r1 = ALPHA * xv + mv
        xh, _ = _ln_stats(r1)
        hv = xh * g + b
        return r1, hv, hv

    r1, h, h_b = mm(u, w_out, mode="nn", name="mix_ln1", tm=ROW_T, tn=D_MODEL, tk=D_MODEL, out_dtypes=(F32, F32, BF16),
                    extras=(x,), row_extras=(ln1_g, ln1_b), epilogue=ln1)

    def relu2(acc):
        r = jnp.maximum(acc, 0.0)
        return (r * r,)

    z = mm(h_b, w_ff1, mode="nn", name="ff1", tm=1024, tn=1024, tk=1024, out_dtypes=(BF16,), epilogue=relu2, b_shards=True)
    f = mm(z, w_ff2, mode="nn", name="ff2", tm=1024, tn=1024, tk=D_FF)

    def ln2_loss(hv, fv, tv, g, b):
        xh, r = _ln_stats(ALPHA * hv + fv)
        err = xh * g + b - tv
        dy = err * (1.0 / D_MODEL)
        dr2, dg, db = _ln_bwd(xh, r, g, dy)
        loss = jnp.sum(_colsum(err * err), axis=1, keepdims=True) * (0.5 / D_MODEL)
        return dr2, dr2, jnp.broadcast_to(loss, (1, LANES)), dg, db

    dr2, dr2_b, loss_l, d_ln2_g, d_ln2_b = row(
        ln2_loss, name="ln2_loss",
        ins=[(h, D_MODEL, 0, "row"), (f, D_MODEL, 0, "row"), (target, D_MODEL, 0, "row"), (ln2_g, 0, 0, "full"), (ln2_b, 0, 0, "full")],
        outs=[(D_MODEL, F32), (D_MODEL, BF16)], sums=[LANES, D_MODEL, D_MODEL])

    d_w_ff2 = mm(z, dr2_b, mode="tn", name="d_w_ff2", tm=1024, tn=1024, tk=rows)
    da = mm(dr2_b, w_ff2, mode="nt", name="d_ff_act", tm=1024, tn=1024, tk=1024, out_dtypes=(BF16,), extras=(z,),
            epilogue=lambda acc, zv: (acc * (2.0 * _sqrt(zv.astype(F32))),))
    d_w_ff1 = mm(h_b, da, mode="tn", name="d_w_ff1", tm=1024, tn=1024, tk=rows, out_shards=True)
    dh = mm(da, w_ff1, mode="nt", name="d_h", tm=1024, tn=1024, tk=1024, extras=(dr2,), epilogue=lambda acc, rv: (acc + ALPHA * rv,), b_shards=True)

    def ln1_bwd(dhv, r1v, g):
        xh, r = _ln_stats(r1v)
        return _ln_bwd(xh, r, g, dhv)

    dr1, d_ln1_g, d_ln1_b = row(ln1_bwd, name="ln1_bwd", ins=[(dh, D_MODEL, 0, "row"), (r1, D_MODEL, 0, "row"), (ln1_g, 0, 0, "full")],
                                outs=[(D_MODEL, F32)], sums=[D_MODEL, D_MODEL])
    d_w_out = mm(u, dr1, mode="tn", name="d_w_out", tm=1024, tn=1024, tk=1024)
    du = mm(dr1, w_out, mode="nt", name="d_u", tm=1024, tn=1024, tk=1024, out_dtypes=(BF16,))

    def gate_bwd(t0, t1, c0, c1, ya, yb, duv):
        s0, s1 = jax.nn.sigmoid(t0 + c0), jax.nn.sigmoid(t1 + c1)
        dt0 = duv * ya * s0 * (1.0 - s0)
        dt1 = duv * yb * s1 * (1.0 - s1)
        return duv * s0, duv * s1, jnp.concatenate([dt0, dt1], axis=1), jnp.concatenate([_colsum(dt0), _colsum(dt1)], axis=1)

    dy_a, dy_b, d_gates, d_b_gate = row(
        gate_bwd, name="gate_bwd", ins=gate_ins + [(y_a, D_MODEL, 0, "row"), (y_b, D_MODEL, 0, "row"), (du, D_MODEL, 0, "row")],
        outs=[(D_MODEL, BF16), (D_MODEL, BF16), (2 * D_MODEL, BF16)], sums=[2 * D_MODEL])
    d_w_oa = mm(o_a, dy_a, mode="tn", name="d_w_o_mla", tm=1024, tn=1024, tk=1024)
    d_w_ob = mm(o_b, dy_b, mode="tn", name="d_w_o_dil", tm=1024, tn=1024, tk=1024)
    grads = dict(w_o_mla=_split_cols(d_w_oa), w_o_dil=_split_cols(d_w_ob),
                 w_out=d_w_out.reshape(N_CHIPS, D_MODEL // N_CHIPS, D_MODEL), w_ff1=d_w_ff1, w_ff2=d_w_ff2.reshape(N_CHIPS, D_FF // N_CHIPS, D_MODEL))
    sent = None if early_grads is None else early_grads(grads)
    do_a = mm(dy_a, w_oa, mode="nt", name="d_o_mla", tm=1024, tn=1024, tk=1024, after=sent)
    do_b = mm(dy_b, w_ob, mode="nt", name="d_o_dil", tm=1024, tn=1024, tk=1024)
    dq_a, dk_a, dv_a = _attn_bwd(q_a, 0, k_a, 0, v_a, 0, o_a, do_a, lse_a, mla_bwd_tables, scale_mla,
                                 name="mla_bwd", batch=batch, seq=seq, out_dtype=F32, sub=2, wide_qk=True)
    going = None if early_grads_go is None else early_grads_go(dq_a)
    d_qkv_d = _attn_bwd(proj_d, 0, proj_d, 1, proj_d, 2, o_b, do_b, lse_b, dil_bwd_tables, scale_dil,
                        name="dil_bwd", batch=batch, seq=seq, out_dtype=BF16, sub=2, stride=4, after=going)

    def mla_post(dq, dk, c, up, dn, c_only):
        dqs = [_rope_bwd(dq[:, h * LANES:(h + 1) * LANES], c, up, dn) for h in range(N_HEADS)]
        dk_sum = dk[:, :LANES]
        for h in range(1, N_HEADS):
            dk_sum = dk_sum + dk[:, h * LANES:(h + 1) * LANES]
        return jnp.concatenate(dqs, axis=1), _rope_bwd(dk_sum, c_only, up, dn)

    dq_lin, d_kr = row(mla_post, name="mla_unrope",
                       ins=[(dq_a, D_MODEL, 0, "row"), (dk_a, D_MODEL, 0, "row"), pos(rope_c), pos(rope_up), pos(rope_dn), pos(rope_c_only)],
                       outs=[(N_HEADS * LANES, BF16), (LANES, BF16)])
    d_qn = mm(dq_lin, w_uq_pt, mode="nn", name="d_qn", tm=1024, tn=Q_LORA, tk=1024)
    d_kvn_k = mm(dk_a, w_uk_p, mode="nt", name="d_kvn_k", tm=1024, tn=KV_LORA, tk=1024)
    d_kvn = mm(dv_a, w_uv, mode="nt", name="d_kvn", tm=1024, tn=KV_LORA, tk=1024, extras=(d_kvn_k,), epilogue=lambda acc, e: (acc + e,))

    def rms_bwd(lora, dq, dkv, dkr, gq, gkv):
        dxq, dgq = _rms_bwd(lora[:, :Q_LORA], gq, dq)
        dxk, dgk = _rms_bwd(lora[:, Q_LORA:], gkv, dkv)
        tail = jnp.zeros((dxq.shape[0], P_GATE - P_KR - LANES), F32)
        return jnp.concatenate([dxq, dxk, dkr.astype(F32), tail], axis=1), dgq, dgk

    d_tail, d_g_q_a, d_g_kv_a = row(
        rms_bwd, name="mla_rms_bwd",
        ins=[(proj, LORA_W, lora0, "row"), (d_qn, Q_LORA, 0, "row"), (d_kvn, KV_LORA, 0, "row"), (d_kr, LANES, 0, "row"),
             (g_q_a, 0, 0, "full"), (g_kv_a, 0, 0, "full")],
        outs=[(P_GATE, BF16)], sums=[Q_LORA, KV_LORA])
    d_proj = [d_tail, d_gates, d_qkv_d]
    d_w_in_pt = mm(d_proj, x_mm, mode="tn", name="d_w_in", tm=1024, tn=1024, tk=2048)
    d_w_in = _unpad_w_in_t(d_w_in_pt).reshape(N_CHIPS, IN_WIDTH // N_CHIPS, D_MODEL)
    moving = None if first_grad is None else first_grad(d_w_in)
    d_w_uq_pt = mm(dq_lin, qn, mode="tn", name="d_w_uq", tm=1024, tn=Q_LORA, tk=1024, after=moving)
    d_w_uk_p = mm(kvn, dk_a, mode="tn", name="d_w_uk", tm=KV_LORA, tn=1024, tk=1024, after=moving)
    d_w_uv = mm(kvn, dv_a, mode="tn", name="d_w_uv", tm=KV_LORA, tn=1024, tk=1024, after=moving)
    grads.update(w_in=d_w_in,
                 w_uq=_unpad_head_rows(d_w_uq_pt, MLA_QK).reshape(N_CHIPS, N_HEADS * MLA_QK // N_CHIPS, Q_LORA),
                 w_ukv=_split_cols(_merge_ukv(d_w_uk_p, d_w_uv)))
    leaving = None if last_grads is None else last_grads(grads)
    grad_x = mm(d_proj, jnp.concatenate([w_main_t, w_dil_t], axis=0), mode="nn", name="d_x", tm=1024, tn=1024, tk=1024, extras=(dr1,), epilogue=lambda acc, rv: (acc + ALPHA * rv,),
                after=leaving)

    grads.update(
        b_gate=d_b_gate.reshape(2, D_MODEL), g_q_a=d_g_q_a, g_kv_a=d_g_kv_a, ln1_g=d_ln1_g, ln1_b=d_ln1_b, ln2_g=d_ln2_g, ln2_b=d_ln2_b)
    return loss_l, grad_x.reshape(batch, seq, D_MODEL), grads


BIG = ("w_in", "w_uq", "w_ukv", "w_o_mla", "w_o_dil", "w_out", "w_ff1", "w_ff2")
SMALL = (("b_gate", 2 * D_MODEL), ("g_q_a", Q_LORA), ("g_kv_a", KV_LORA), ("ln1_g", D_MODEL), ("ln1_b", D_MODEL),
         ("ln2_g", D_MODEL), ("ln2_b", D_MODEL))
TRANSPOSED = ("w_in", "w_uq")
D2D_PIECES = (4, 2, 1)
ANY = pl.BlockSpec(memory_space=pl.ANY)
SIDE_EFFECTS = pltpu.CompilerParams(has_side_effects=True)


def _place():
    x, y, c = lax.axis_index("x"), lax.axis_index("y"), lax.axis_index("c")
    return x, y, c, ((1 - x, y), (x, 1 - y), (1 - x, 1 - y))


def _half_axis(shape):
    return 0 if shape[0] % 32 == 0 else 1


def _half_shape(shape):
    return (shape[0] // 2, shape[1]) if _half_axis(shape) == 0 else (shape[0], shape[1] // 2)


def _window(ref, lead, shape, which=None, pieces=False):
    axis = _half_axis(shape)
    size = shape[axis] if which is None else shape[axis] // 2
    base = 0 if which is None else which * size
    tile = (16, LANES)[axis]
    count = next(c for c in D2D_PIECES if size % (tile * c) == 0) if pieces else 1
    step = size // count
    spans = [pl.ds(pl.multiple_of(base + i * step, tile), step) for i in range(count)]
    refs = [ref.at[(*lead, s)] if axis == 0 else ref.at[(*lead, slice(None), s)] for s in spans]
    return refs if pieces else refs[0]


def _remote(src, dst, send, recv, to):
    return pltpu.make_async_remote_copy(src_ref=src, dst_ref=dst, send_sem=send, recv_sem=recv, device_id=to, device_id_type=MESH)


def _pair_split(grads, name):
    n = len(grads)

    def body(*refs):
        srcs, outs, (send, recv) = refs[:n], refs[n:2 * n], refs[2 * n:]
        x, y, c, _ = _place()
        for t in range(n):
            for s in range(N_CHIPS):
                _remote(_window(srcs[t], (s,), grads[t].shape[1:], 1 - c), outs[t].at[s], send.at[t], recv.at[t], (x, y, 1 - c)).start()
        for t in range(n):
            _remote(_window(srcs[t], (slice(None),), grads[t].shape[1:], 1 - c), outs[t], send.at[t], recv.at[t], (x, y, 1 - c)).wait()

    return pl.pallas_call(
        body, name=name, in_specs=[ANY] * n, out_specs=[ANY] * n,
        out_shape=[jax.ShapeDtypeStruct((N_CHIPS,) + _half_shape(g.shape[1:]), g.dtype) for g in grads],
        scratch_shapes=[pltpu.SemaphoreType.DMA((n,)), pltpu.SemaphoreType.DMA((n,))],
        compiler_params=SIDE_EFFECTS,
    )(*grads)


HBM = pl.BlockSpec(memory_space=pltpu.HBM)
SEM = pl.BlockSpec(memory_space=pltpu.SEMAPHORE)
SPLIT = pltpu.CompilerParams(has_side_effects=pltpu.SideEffectType.DATAFLOW_SIDE_EFFECTING)


def _in_hbm(a):
    return pltpu.with_memory_space_constraint(a, pltpu.HBM)


def _split_copies(kind, srcs, lands):
    x, y, c, chips = _place()
    out = []
    for t in range(len(srcs)):
        if kind == "pair":
            out += [(t, s % 3, _window(srcs[t], (s,), srcs[t].shape[1:], 1 - c), lands[t].at[s], (x, y, 1 - c)) for s in range(N_CHIPS)]
            continue
        if kind == "join":
            out += [(t, 0, a, b, (x, y, 1 - c)) for a, b in zip(_window(srcs[t], (), srcs[t].shape, None, True), _window(lands[t], (), srcs[t].shape, None, True))]
            continue
        if kind == "forward":
            shape, sibling = srcs[t].shape, (x, y, 1 - c)
            out += [(t, 0, a, b, sibling) for a, b in zip(_window(srcs[t], (), shape, None, True), _window(lands[t], (2 * x + y,), shape, None, True))]
            out += [(t, j, a, a, sibling) for j, (cx, cy) in enumerate(chips) for a in _window(lands[t], (2 * cx + cy,), shape, c, True)]
            continue
        for j, (cx, cy) in enumerate(chips):
            if kind == "gather":
                shape = srcs[t].shape
                out.append((t, j, _window(srcs[t], (), shape, c), _window(lands[t], (2 * x + y,), shape, c), (cx, cy, c)))
            else:
                out.append((t, j, srcs[t].at[2 * cx + cy], lands[t].at[j], (cx, cy, c)))
    return out


def _split_start(kind, srcs, land_shapes, name, lands=None, after=None):
    n = len(srcs)

    def body(*refs):
        src_refs, land_refs, sems, token = refs[:n], refs[n:2 * n], refs[-7 - 2 * n:-1 - 2 * n], refs[-1]
        for t, j, s, d, to in _split_copies(kind, src_refs, land_refs):
            _remote(s, d, sems[j], sems[3 + j], to).start()
        token[...] = jnp.zeros_like(token)

    lands = [_in_hbm(lax.empty(s.shape, s.dtype)) for s in land_shapes] if lands is None else list(lands)
    thru = [pltpu.HBM(a.shape, a.dtype) for a in list(srcs) + lands]
    res = pl.pallas_call(
        body, name=name,
        out_shape=(*[pltpu.SemaphoreType.DMA(())] * 6, *thru, jax.ShapeDtypeStruct((8, LANES), F32)),
        in_specs=[HBM] * (2 * n) + [ANY] * (after is not None), out_specs=(*[SEM] * 6, *[HBM] * (2 * n), pl.BlockSpec(memory_space=pltpu.VMEM)),
        input_output_aliases={i: 6 + i for i in range(2 * n)}, compiler_params=SPLIT,
    )(*[_in_hbm(s) for s in srcs], *lands, *([after] if after is not None else []))
    return res[:6], res[6:6 + n], res[6 + n:6 + 2 * n], res[-1]


def _split_wait(kind, sems, srcs, lands, after, name):
    n = len(srcs)
    afters = list(after) if isinstance(after, (list, tuple)) else [after]

    def body(*refs):
        src_refs, land_refs, sem_refs = refs[:n], refs[n:2 * n], refs[2 * n:2 * n + 6]
        for t, j, s, d, to in _split_copies(kind, src_refs, land_refs):
            cp = _remote(s, d, sem_refs[j], sem_refs[3 + j], to)
            cp.wait_send()
            cp.wait_recv()

    res = pl.pallas_call(
        body, name=name, out_shape=[pltpu.HBM(a.shape, a.dtype) for a in list(srcs) + list(lands)],
        in_specs=[HBM] * (2 * n) + [SEM] * 6 + [ANY] * len(afters), out_specs=[HBM] * (2 * n),
        input_output_aliases={i: i for i in range(2 * n)}, compiler_params=SPLIT,
    )(*srcs, *lands, *sems, *afters)
    return res[:n], res[n:]


def _sum_all_devices(vec, name, after):
    n_rows = vec.shape[0]

    def body(v_ref, *rest):
        out_ref, buf, send, recv = rest[len(after):]
        x, y, c, _ = _place()
        me = 4 * x + 2 * y + c
        buf[me] = v_ref[...]
        flips = [(a, b, d) for a in (0, 1) for b in (0, 1) for d in (0, 1)][1:]
        copies = []
        for r, (a, b, d) in enumerate(flips):
            px, py, pc = (1 - x if a else x), (1 - y if b else y), (1 - c if d else c)
            copies.append(pltpu.make_async_remote_copy(src_ref=v_ref, dst_ref=buf.at[me], send_sem=send.at[r], recv_sem=recv.at[r],
                                                       device_id=(px, py, pc), device_id_type=MESH))
            copies[-1].start()
        for r, (a, b, d) in enumerate(flips):
            px, py, pc = (1 - x if a else x), (1 - y if b else y), (1 - c if d else c)
            pltpu.make_async_remote_copy(src_ref=v_ref, dst_ref=buf.at[4 * px + 2 * py + pc], send_sem=send.at[r], recv_sem=recv.at[r],
                                         device_id=(px, py, pc), device_id_type=MESH).wait_recv()
        for cp in copies:
            cp.wait_send()
        total = buf[0]
        for k in range(1, N_DEV):
            total = total + buf[k]
        out_ref[...] = total

    vmem = pl.BlockSpec(memory_space=pltpu.VMEM)
    return pl.pallas_call(
        body, name=name, in_specs=[vmem] + [ANY] * len(after), out_specs=vmem, out_shape=jax.ShapeDtypeStruct(vec.shape, F32),
        scratch_shapes=[pltpu.VMEM((N_DEV, n_rows, LANES), F32), pltpu.SemaphoreType.DMA((N_DEV - 1,)), pltpu.SemaphoreType.DMA((N_DEV - 1,))],
        compiler_params=pltpu.CompilerParams(has_side_effects=True),
    )(vec, *after)


def _half_tile(half, width):
    t = half
    while t * width * 4 > (2 << 20) and t % 32 == 0:
        t //= 2
    return t


def _pair_add(g, theirs, core, name):
    _, half, width = theirs.shape
    t = _half_tile(half, width)
    n = half // t

    def body(c_ref, a_ref, b_ref, o_ref):
        o_ref[...] = (a_ref[...] + b_ref[...]).astype(BF16)

    tile = pl.BlockSpec((1, t, width), lambda j, i, c_ref: (j, i, 0))
    if _half_axis(g.shape[1:]) == 0:
        mine = pl.BlockSpec((1, t, width), lambda j, i, c_ref: (j, c_ref[0] * n + i, 0))
    else:
        mine = pl.BlockSpec((1, t, width), lambda j, i, c_ref: (j, i, c_ref[0]))
    return pl.pallas_call(
        body, name=name,
        grid_spec=pltpu.PrefetchScalarGridSpec(num_scalar_prefetch=1, grid=(N_CHIPS, n), in_specs=[mine, tile], out_specs=tile),
        out_shape=jax.ShapeDtypeStruct(theirs.shape, BF16), compiler_params=_params(("parallel", "parallel")),
    )(core, g, theirs)


def _chip_sum(part, others, chip, name, after=None):
    _, half, width = part.shape
    t = _half_tile(half, width)

    def body(s_ref, mine, p0, p1, p2, *rest):
        o_ref = rest[-1]
        o_ref[...] = ((mine[0].astype(F32) + p0[0].astype(F32)) + p1[0].astype(F32)) + p2[0].astype(F32)

    return pl.pallas_call(
        body, name=name,
        grid_spec=pltpu.PrefetchScalarGridSpec(
            num_scalar_prefetch=1, grid=(half // t,),
            in_specs=[pl.BlockSpec((1, t, width), lambda i, s_ref: (s_ref[0], i, 0))]
            + [pl.BlockSpec((1, t, width), lambda i, s_ref, j=j: (j, i, 0)) for j in range(3)] + [pl.BlockSpec(memory_space=pl.ANY)] * (after is not None),
            out_specs=pl.BlockSpec((t, width), lambda i, s_ref: (i, 0))),
        out_shape=jax.ShapeDtypeStruct((half, width), F32), compiler_params=_params(("parallel",)),
    )(chip, part, others, others, others, *([after] if after is not None else []))


EARLY = ("w_in", "w_uq", "w_ukv")
LATE = ("w_o_mla", "w_o_dil", "w_out", "w_ff1", "w_ff2")


def _sum_small(vals):
    n_in = len(vals)
    n_rows = sum(a.shape[0] * a.shape[1] // LANES for a in vals)
    pad_rows = -(-n_rows // 8) * 8

    def chunks(refs):
        return [(ref, a, j) for ref in refs for a in range(ref.shape[0]) for j in range(ref.shape[1] // LANES)]

    def body(*refs):
        ins, outs, (buf, send, recv) = refs[:n_in], refs[n_in:2 * n_in], refs[2 * n_in:]
        x, y, c, _ = _place()
        me = 4 * x + 2 * y + c
        for r, (ref, a, j) in enumerate(chunks(ins)):
            buf[me, r:r + 1, :] = ref[a:a + 1, j * LANES:(j + 1) * LANES]
        if pad_rows > n_rows:
            buf[me, n_rows:pad_rows, :] = jnp.zeros((pad_rows - n_rows, LANES), F32)
        flips = [(a, b, d) for a in (0, 1) for b in (0, 1) for d in (0, 1)][1:]
        peers = [((1 - x if a else x), (1 - y if b else y), (1 - c if d else c)) for a, b, d in flips]
        copies = [_remote(buf.at[me], buf.at[me], send.at[r], recv.at[r], peer) for r, peer in enumerate(peers)]
        for cp in copies:
            cp.start()
        for r, (px, py, pc) in enumerate(peers):
            _remote(buf.at[me], buf.at[4 * px + 2 * py + pc], send.at[r], recv.at[r], (px, py, pc)).wait_recv()
        for cp in copies:
            cp.wait_send()
        total = buf[0]
        for k in range(1, N_DEV):
            total = total + buf[k]
        for r, (ref, a, j) in enumerate(chunks(outs)):
            ref[a:a + 1, j * LANES:(j + 1) * LANES] = total[r:r + 1, :]

    vmem = pl.BlockSpec(memory_space=pltpu.VMEM)
    return pl.pallas_call(
        body, name="sum_small", in_specs=[vmem] * n_in, out_specs=[vmem] * n_in,
        out_shape=[jax.ShapeDtypeStruct(a.shape, F32) for a in vals],
        scratch_shapes=[pltpu.VMEM((N_DEV, pad_rows, LANES), F32), pltpu.SemaphoreType.DMA((N_DEV - 1,)), pltpu.SemaphoreType.DMA((N_DEV - 1,))],
        compiler_params=SIDE_EFFECTS,
    )(*vals)


def _adam_math(w, g, m, v):
    nm = B1 * m + (1.0 - B1) * g
    nv = B2 * v + (1.0 - B2) * (g * g)
    m_hat = nm / (1.0 - B1 ** ADAM_STEP)
    v_hat = nv / (1.0 - B2 ** ADAM_STEP)
    return -LR * (m_hat / (jnp.sqrt(v_hat) + ADAM_EPS) + WD * w), nm, nv


def _adamw_big(w, mine, theirs, m, v, core, name, side_by_side=False):
    rows, width = w.shape
    if side_by_side:
        t = next(c for c in (152, 96, 64, 32, 16, 8) if rows % c == 0)
        hb = None
        half_spec = pl.BlockSpec((t, width // 2), lambda i, c_ref: (i, 0))
    else:
        t = next(c for c in (256, 128, 64, 32, 16, 8) if (rows // 2) % c == 0)
        hb = rows // 2 // t
        half_spec = pl.BlockSpec((t, width), lambda i, c_ref: (i % hb, 0))

    def body(c_ref, w_ref, a_ref, b_ref, m_ref, v_ref, g_ref, d_ref, nm_ref, nv_ref):
        south = c_ref[0] == 0
        if side_by_side:
            g = jnp.where(south, jnp.concatenate([a_ref[...], b_ref[...]], axis=1), jnp.concatenate([b_ref[...], a_ref[...]], axis=1))
        else:
            g = jnp.where((pl.program_id(0) < hb) == south, a_ref[...], b_ref[...])
        g_ref[...] = g
        d_ref[...], nm_ref[...], nv_ref[...] = _adam_math(w_ref[...], g, m_ref[...], v_ref[...])

    spec = pl.BlockSpec((t, width), lambda i, c_ref: (i, 0))
    return pl.pallas_call(
        body, name=name,
        grid_spec=pltpu.PrefetchScalarGridSpec(num_scalar_prefetch=1, grid=(rows // t,),
                                               in_specs=[spec, half_spec, half_spec, spec, spec], out_specs=[spec] * 4),
        out_shape=[jax.ShapeDtypeStruct(w.shape, F32)] * 4, compiler_params=_params(("parallel",)),
    )(core, w, mine, theirs, m, v)


def _adamw_small(ws, gs, ms, vs):
    n = len(ws)

    def body(*refs):
        for t in range(n):
            w_ref, g_ref, m_ref, v_ref = (refs[k * n + t] for k in range(4))
            d, nm, nv = _adam_math(w_ref[...], g_ref[...], m_ref[...], v_ref[...])
            refs[4 * n + t][...] = d
            refs[5 * n + t][...] = nm
            refs[6 * n + t][...] = nv

    vmem = pl.BlockSpec(memory_space=pltpu.VMEM)
    res = pl.pallas_call(body, name="adamw_small", in_specs=[vmem] * (4 * n), out_specs=[vmem] * (3 * n),
                         out_shape=[jax.ShapeDtypeStruct(a.shape, F32) for a in ws] * 3)(*ws, *gs, *ms, *vs)
    return res[:n], res[n:2 * n], res[2 * n:]


def kernel(x, w_in, b_gate, g_q_a, w_uq, g_kv_a, w_ukv, w_o_mla, w_o_dil, w_out, ln1_g, ln1_b, w_ff1, w_ff2, ln2_g, ln2_b, loss_target, m_w_in, m_b_gate, m_g_q_a, m_w_uq, m_g_kv_a, m_w_ukv, m_w_o_mla, m_w_o_dil, m_w_out, m_ln1_g, m_ln1_b, m_w_ff1, m_w_ff2, m_ln2_g, m_ln2_b, v_w_in, v_b_gate, v_g_q_a, v_w_uq, v_g_kv_a, v_w_ukv, v_w_o_mla, v_w_o_dil, v_w_out, v_ln1_g, v_ln1_b, v_w_ff1, v_w_ff2, v_ln2_g, v_ln2_b):
    order = ("w_in", "b_gate", "g_q_a", "w_uq", "g_kv_a", "w_ukv", "w_o_mla", "w_o_dil", "w_out", "ln1_g", "ln1_b", "w_ff1", "w_ff2", "ln2_g", "ln2_b")
    w = dict(w_in=w_in, b_gate=b_gate, g_q_a=g_q_a, w_uq=w_uq, g_kv_a=g_kv_a, w_ukv=w_ukv, w_o_mla=w_o_mla, w_o_dil=w_o_dil, w_out=w_out,
             ln1_g=ln1_g, ln1_b=ln1_b, w_ff1=w_ff1, w_ff2=w_ff2, ln2_g=ln2_g, ln2_b=ln2_b)
    m = dict(w_in=m_w_in, b_gate=m_b_gate, g_q_a=m_g_q_a, w_uq=m_w_uq, g_kv_a=m_g_kv_a, w_ukv=m_w_ukv, w_o_mla=m_w_o_mla, w_o_dil=m_w_o_dil,
             w_out=m_w_out, ln1_g=m_ln1_g, ln1_b=m_ln1_b, w_ff1=m_w_ff1, w_ff2=m_w_ff2, ln2_g=m_ln2_g, ln2_b=m_ln2_b)
    v = dict(w_in=v_w_in, b_gate=v_b_gate, g_q_a=v_g_q_a, w_uq=v_w_uq, g_kv_a=v_g_kv_a, w_ukv=v_w_ukv, w_o_mla=v_w_o_mla, w_o_dil=v_w_o_dil,
             w_out=v_w_out, ln1_g=v_ln1_g, ln1_b=v_ln1_b, w_ff1=v_w_ff1, w_ff2=v_w_ff2, ln2_g=v_ln2_g, ln2_b=v_ln2_b)
    chip = 2 * lax.axis_index("x") + lax.axis_index("y")
    south = (lax.axis_index("c") == 0).astype(F32)
    gate_w = D_MODEL // N_CHIPS

    core = lax.axis_index("c").astype(jnp.int32).reshape(1)
    turn = lambda n, a: a.T if n in TRANSPOSED else a
    early_shards = [turn(n, w[n][0]).astype(BF16) for n in EARLY]
    gathered = lambda group: [jax.ShapeDtypeStruct((N_CHIPS,) + s.shape, BF16) for s in group]
    e_sems, e_srcs, e_lands, e_token = _split_start("gather", early_shards, gathered(early_shards), "gather_early_start")
    e_token, late_f32, x_f32, positions = lax.optimization_barrier((e_token, [w[n][0] for n in LATE], x, jnp.arange(x.shape[1], dtype=F32)))
    late_shards, x_lo = [a.astype(BF16) for a in late_f32], x_f32.astype(BF16)
    rope = _rope_tables(positions)
    shards = dict(zip(EARLY + LATE, early_shards + late_shards))
    b_mine = lax.dynamic_update_slice(jnp.zeros((2, D_MODEL), F32), b_gate[0] * south, (0, chip * gate_w))
    b_full = _sum_all_devices(b_mine.reshape(-1, LANES), "gather_b_gate", [e_token, *late_shards, *rope, x_lo]).reshape(2, D_MODEL)
    e_srcs, e_lands = _split_wait("gather", e_sems, e_srcs, e_lands, b_full, "gather_early_wait")
    e_forward = _split_start("forward", e_srcs, None, "gather_early_forward_start", lands=e_lands)
    first = dict(zip(EARLY, _split_wait("forward", *e_forward[:3], e_forward[-1], "gather_early_forward_wait")[1]))
    g_sems, g_srcs, g_lands, g_token = _split_start("gather", late_shards, gathered(late_shards), "gather_late_start", after=first[EARLY[-1]])
    tables = _attention_tables(x.shape[1])

    sent = {}

    def late_arrived(after):
        srcs, lands = _split_wait("gather", g_sems, g_srcs, g_lands, after, "gather_late_wait")
        sent["forward"] = _split_start("forward", srcs, None, "gather_late_forward_start", lands=lands)
        return sent["forward"][-1]

    def late_weights(after):
        return dict(zip(LATE, _split_wait("forward", *sent["forward"][:3], after, "gather_late_forward_wait")[1]))

    exchange_shapes = lambda parts: [jax.ShapeDtypeStruct((3,) + p.shape[1:], BF16) for p in parts]

    def early_grads(grads_late):
        gs = [grads_late[n] for n in LATE]
        shapes = [jax.ShapeDtypeStruct((N_CHIPS,) + _half_shape(g.shape[1:]), F32) for g in gs]
        sent["pair"] = _split_start("pair", gs, shapes, "pair_split_late_start")
        return sent["pair"][-1]

    def early_grads_go(after):
        gs, theirs = _split_wait("pair", *sent["pair"][:3], after, "pair_split_late_wait")
        parts = [_pair_add(g, th, core, "pair_add_" + n) for g, th, n in zip(gs, theirs, LATE)]
        sent["late"] = _split_start("scatter", parts, exchange_shapes(parts), "exchange_late_start")
        return sent["late"][-1]

    def first_grad(g):
        sent["first"] = _split_start("pair", [g], [jax.ShapeDtypeStruct((N_CHIPS,) + _half_shape(g.shape[1:]), F32)], "pair_split_w_in_start")
        return sent["first"][-1]

    def last_grads(grads_early):
        rest = [grads_early[n] for n in EARLY[1:]]
        gs, theirs = _split_wait("pair", *sent["first"][:3], rest[-1], "pair_split_w_in_wait")
        gs, theirs = list(gs) + rest, list(theirs) + list(_pair_split(rest, "pair_split_early"))
        parts = [_pair_add(g, th, core, "pair_add_" + n) for g, th, n in zip(gs, theirs, EARLY)]
        sent["early"] = _split_start("scatter", parts, exchange_shapes(parts), "exchange_early_start")
        return sent["early"][-1]

    loss_part, grad_x, grads = _local_step(x, loss_target, first, b_full, g_q_a, g_kv_a, ln1_g, ln1_b, ln2_g, ln2_b, token=g_token,
                                           late_arrived=late_arrived, late_weights=late_weights, early_grads=early_grads, early_grads_go=early_grads_go,
                                           first_grad=first_grad, last_grads=last_grads, tables=tables, rope=rope, x_lo=x_lo)

    g_out, delta, new_m, new_v = {}, {}, {}, {}
    chip1 = chip.astype(jnp.int32).reshape(1)

    def sum_and_send(names, parts, others, tag):
        totals = [_chip_sum(p, o, chip1, "chip_sum_" + n) for n, p, o in zip(names, parts, others)]
        return _split_start("join", totals, [jax.ShapeDtypeStruct(t.shape, F32) for t in totals], "pair_join_" + tag + "_start")

    def adam(names, joined, after, tag):
        totals, halves = _split_wait("join", *joined[:3], after, "pair_join_" + tag + "_wait")
        for n, mine, theirs in zip(names, totals, halves):
            res = _adamw_big(turn(n, w[n][0]), mine, theirs, turn(n, m[n][0]), turn(n, v[n][0]), core, "adamw_" + n,
                             side_by_side=mine.shape[0] == shards[n].shape[0])
            g_out[n], delta[n], new_m[n], new_v[n] = (turn(n, r) for r in res)

    late_joined = sum_and_send(LATE, *_split_wait("scatter", *sent["late"][:3], grad_x, "exchange_late_wait"), "late")
    early_joined = sum_and_send(EARLY, *_split_wait("scatter", *sent["early"][:3], late_joined[-1], "exchange_early_wait"), "early")
    small_names = [name for name, _ in SMALL]
    sums = _sum_small([grads[name] for name in small_names] + [loss_part])
    loss = sums[-1][0, 0]
    g_small = dict(zip(small_names, sums))
    g_small["b_gate"] = lax.dynamic_slice(g_small["b_gate"], (0, chip * gate_w), (2, gate_w))
    flat = lambda a: a.reshape(-1, a.shape[-1])
    res = _adamw_small(*[[flat(d[name]) for name in small_names] for d in (w, g_small, m, v)])
    g_out.update(g_small)
    for d, r in zip((delta, new_m, new_v), res):
        d.update(zip(small_names, r))
    adam(LATE, late_joined, res[0][0], "late")
    adam(EARLY, early_joined, delta[LATE[-1]], "early")

    lead = lambda d: [d[name].reshape(w[name].shape) for name in order]
    return (loss, grad_x, *lead(g_out), *lead(delta), *lead(new_m), *lead(new_v))
```

```python
import functools
import math

import jax
import jax.numpy as jnp
import numpy as np
from jax import lax
from jax.experimental import pallas as pl
from jax.experimental.pallas import tpu as pltpu

F32 = jnp.float32
BF16 = jnp.bfloat16
MESH = pl.DeviceIdType.MESH

D_MODEL = 1024
N_HEADS = 8
LANES = 128
NOPE, ROPE, V_DIM = 64, 32, 64
MLA_QK = NOPE + ROPE
Q_LORA, KV_LORA = 384, 256
DIL_DIM = 64
DIL_PATTERNS = ((128, 1), (512, 4), (2048, 16))
D_FF = 4096
N_CHIPS = 4
N_DEV = 8
IN_WIDTH = 4256
LN_EPS, RMS_EPS = 1e-5, 1e-6
NEG = -1e30
LOG2E, LN2 = 1.4426950408889634, 0.6931471805599453
ALPHA = 2.0 ** 0.25
ROPE_THETA = 10000.0
LR, B1, B2, ADAM_EPS, WD, ADAM_STEP = 0.001, 0.9, 0.999, 1e-8, 0.01, 10

P_LORA, P_KR, P_GATE, P_HALF = 0, 640, 1024, 3072
DIL_GROUP = 4 * LANES
P_DIL = N_HEADS // 2 * DIL_GROUP
LORA_W = Q_LORA + KV_LORA
KR_LANE = NOPE

ATT_T = 512
ROW_T = 512
ACC_COLS = 256
SQRT_FLOOR = 1e-36
VMEM_LIMIT = 56 * 1024 * 1024

NN = (((1,), (0,)), ((), ()))
NT = (((1,), (1,)), ((), ()))
TN = (((0,), (0,)), ((), ()))


def _params(sem=None, **kw):
    return pltpu.CompilerParams(dimension_semantics=sem, vmem_limit_bytes=VMEM_LIMIT, **kw)


def _matmul(a, b, *, mode, name, tm, tn, tk, out_dtypes=(F32,), extras=(), row_extras=(), epilogue=None, b_shards=False, out_shards=False, after=None):
    pieces = list(a) if isinstance(a, (list, tuple)) else [a]
    n_pc = len(pieces)
    a_shape = (pieces[0].shape[0], sum(p.shape[1] for p in pieces))
    if b_shards:
        n_sh, rows_b, cols_b = b.shape
        b_shape = (rows_b, n_sh * cols_b)
    else:
        b_shape = b.shape
    if mode == "nn":
        (m, k), (k2, n) = a_shape, b_shape
    elif mode == "nt":
        (m, k), (n, k2) = a_shape, b_shape
    else:
        (k, m), (k2, n) = a_shape, b_shape
    assert k == k2, (a_shape, b.shape, mode)
    tm, tn, tk = min(tm, m), min(tn, n), min(tk, k)
    assert m % tm == 0 and n % tn == 0 and k % tk == 0, (name, m, n, k, tm, tn, tk)
    nk = k // tk
    n_ex, n_out = len(extras) + len(row_extras), len(out_dtypes)
    n_in = n_pc + 1 + n_ex + (after is not None)
    dims = {"nn": NN, "nt": NT, "tn": TN}[mode]
    col_tile = tm if mode == "tn" else tk
    blocks = [p.shape[1] // col_tile for p in pieces]
    firsts = [sum(blocks[:p]) for p in range(n_pc)]
    assert all(p.shape[1] % col_tile == 0 for p in pieces), (name, col_tile)

    def body(*refs):
        a_refs, b_ref = refs[:n_pc], refs[n_pc]
        ex_refs = refs[n_pc + 1:n_pc + 1 + n_ex]
        out_refs = refs[n_in:n_in + n_out]

        def finish(acc):
            outs = epilogue(acc, *[r[...] for r in ex_refs]) if epilogue is not None else (acc,)
            for r, o in zip(out_refs, outs):
                r[...] = o.astype(r.dtype)

        kk = pl.program_id(2)

        def step(a_ref):
            if nk == 1:
                finish(lax.dot_general(a_ref[...].astype(BF16), b_ref[...].astype(BF16), dims, preferred_element_type=F32))
                return
            acc_ref = refs[-1]

            @pl.when(kk == 0)
            def _():
                acc_ref[...] = jnp.zeros_like(acc_ref)

            for c in range(0, tn, ACC_COLS):
                b_blk = b_ref[c:c + ACC_COLS, :] if mode == "nt" else b_ref[:, c:c + ACC_COLS]
                acc_ref[:, c:c + ACC_COLS] += lax.dot_general(a_ref[...].astype(BF16), b_blk.astype(BF16), dims, preferred_element_type=F32)

            @pl.when(kk == nk - 1)
            def _():
                finish(acc_ref[...])

        if n_pc == 1:
            step(a_refs[0])
        else:
            at = pl.program_id(0) if mode == "tn" else kk
            for p in range(n_pc):
                pl.when(jnp.logical_and(at >= firsts[p], at < firsts[p] + blocks[p]))(functools.partial(step, a_refs[p]))

    def a_spec_of(p):
        if n_pc == 1:
            return pl.BlockSpec((tk, tm), lambda i, j, kk: (kk, i)) if mode == "tn" else pl.BlockSpec((tm, tk), lambda i, j, kk: (i, kk))
        col = lambda at: jnp.clip(at - firsts[p], 0, blocks[p] - 1)
        mine = lambda at: jnp.logical_and(at >= firsts[p], at < firsts[p] + blocks[p])
        if mode == "tn":
            return pl.BlockSpec((tk, tm), lambda i, j, kk: (jnp.where(mine(i), kk, 0), col(i)))
        return pl.BlockSpec((tm, tk), lambda i, j, kk: (i, col(kk)))

    b_spec = {"nn": pl.BlockSpec((tk, tn), lambda i, j, kk: (kk, j)),
              "nt": pl.BlockSpec((tn, tk), lambda i, j, kk: (j, kk)),
              "tn": pl.BlockSpec((tk, tn), lambda i, j, kk: (kk, j))}[mode]
    tile = pl.BlockSpec((tm, tn), lambda i, j, kk: (i, j))
    out_spec, out_dims = tile, (m, n)
    if b_shards and mode == "nn":
        per = cols_b // tn
        b_spec = pl.BlockSpec((None, tk, tn), lambda i, j, kk: (j // per, kk, j % per))
    elif b_shards:
        assert mode == "nt"
        per = cols_b // tk
        b_spec = pl.BlockSpec((None, tn, tk), lambda i, j, kk: (kk // per, j, kk % per))
    if out_shards:
        assert not extras and epilogue is None
        per_out = n // N_CHIPS // tn
        out_spec = pl.BlockSpec((None, tm, tn), lambda i, j, kk: (j // per_out, i, j % per_out))
        out_dims = (N_CHIPS, m, n // N_CHIPS)
    outs = pl.pallas_call(
        body, name=name,
        grid=(m // tm, n // tn, nk),
        in_specs=[a_spec_of(p) for p in range(n_pc)] + [b_spec] + [tile] * len(extras) + [pl.BlockSpec((1, tn), lambda i, j, kk: (0, j))] * len(row_extras) + [pl.BlockSpec(memory_space=pl.ANY)] * (after is not None),
        out_specs=[out_spec] * n_out,
        out_shape=[jax.ShapeDtypeStruct(out_dims, dt) for dt in out_dtypes],
        scratch_shapes=[pltpu.VMEM((tm, tn), F32)] if nk > 1 else [],
        compiler_params=_params(("parallel", "parallel", "arbitrary")),
    )(*pieces, b, *extras, *row_extras, *([after] if after is not None else []))
    return outs[0] if n_out == 1 else outs


def _rowwise(fn, *, name, rows, seq, ins, outs, sums=()):
    tm = min(ROW_T, seq)
    n_pos = seq // tm
    n_in, n_out, n_sum = len(ins), len(outs), len(sums)

    def body(*refs):
        vals = fn(*[r[...] for r in refs[:n_in]])
        for r, v in zip(refs[n_in:n_in + n_out], vals[:n_out]):
            r[...] = v.astype(r.dtype)
        first = pl.program_id(0) == 0
        for r, v in zip(refs[n_in + n_out:], vals[n_out:]):
            @pl.when(first)
            def _(r=r, v=v):
                r[...] = v

            @pl.when(jnp.logical_not(first))
            def _(r=r, v=v):
                r[...] += v

    def spec(arr, width, col, kind):
        if kind == "row":
            return pl.BlockSpec((tm, width), lambda i, col=col: (i, col))
        if kind == "pos":
            return pl.BlockSpec((tm, width), lambda i, col=col: (i % n_pos, col))
        return pl.BlockSpec(arr.shape, lambda i: (0,) * arr.ndim)

    res = pl.pallas_call(
        body, name=name,
        grid=(rows // tm,),
        in_specs=[spec(*t) for t in ins],
        out_specs=[pl.BlockSpec((tm, w), lambda i: (i, 0)) for w, _ in outs]
        + [pl.BlockSpec((1, w), lambda i: (0, 0)) for w in sums],
        out_shape=[jax.ShapeDtypeStruct((rows, w), dt) for w, dt in outs]
        + [jax.ShapeDtypeStruct((1, w), F32) for w in sums],
        compiler_params=_params(("arbitrary",)),
    )(*[t[0] for t in ins])
    return res


def _colsum(v):
    return jnp.sum(v, axis=0, keepdims=True)


def _sqrt(v):
    return v * lax.rsqrt(jnp.maximum(v, SQRT_FLOOR))


def _rope_fwd(t, c, s_up, s_dn):
    return t * c + pltpu.roll(t, LANES - 16, 1) * s_up + pltpu.roll(t, 16, 1) * s_dn


def _rope_bwd(d, c, s_up, s_dn):
    return d * c + pltpu.roll(d * s_up, 16, 1) + pltpu.roll(d * s_dn, LANES - 16, 1)


def _rope_tables(positions):
    seq, half = positions.shape[0], ROPE // 2
    inv = jnp.power(ROPE_THETA, -jnp.arange(half, dtype=F32) / half)
    ang = positions[:, None] * inv[None, :]
    cos, sin = jnp.cos(ang), jnp.sin(ang)
    zeros = jnp.zeros((seq, half), F32)
    lo, hi = jnp.ones((seq, KR_LANE), F32), jnp.ones((seq, LANES - KR_LANE - ROPE), F32)
    c = jnp.concatenate([lo, cos, cos, hi], axis=1)
    c_rope_only = jnp.concatenate([0 * lo, cos, cos, 0 * hi], axis=1)
    s_up = jnp.concatenate([0 * lo, -sin, zeros, 0 * hi], axis=1)
    s_dn = jnp.concatenate([0 * lo, zeros, sin, 0 * hi], axis=1)
    return c, s_up, s_dn, c_rope_only


def _rms(x, g):
    r = lax.rsqrt(jnp.mean(x * x, axis=1, keepdims=True) + RMS_EPS)
    return x * r * g


def _rms_bwd(x, g, dy):
    r = lax.rsqrt(jnp.mean(x * x, axis=1, keepdims=True) + RMS_EPS)
    xh = x * r
    dxh = dy * g
    dx = r * (dxh - xh * jnp.mean(dxh * xh, axis=1, keepdims=True))
    return dx, _colsum(dy * xh)


def _ln_stats(x):
    mu = jnp.mean(x, axis=1, keepdims=True)
    xc = x - mu
    r = lax.rsqrt(jnp.mean(xc * xc, axis=1, keepdims=True) + LN_EPS)
    return xc * r, r


def _ln_bwd(xh, r, g, dy):
    dxh = dy * g
    dx = r * (dxh - jnp.mean(dxh, axis=1, keepdims=True) - xh * jnp.mean(dxh * xh, axis=1, keepdims=True))
    return dx, _colsum(dy * xh), _colsum(dy)


def _table_specs(tables, sub):
    whole = lambda a: pl.BlockSpec(a.shape, lambda b, g: (0,) * a.ndim)
    if len(tables) == 1:
        return [whole(tables[0])]
    return [whole(tables[0]), whole(tables[1]), pl.BlockSpec((sub, 1, LANES), lambda b, g: (g, 0, 0))]


def _biased(s, table_refs, delta, head):
    if delta < table_refs[0].shape[0]:
        s = s + table_refs[0][delta]
    if len(table_refs) == 3:
        s = s - table_refs[2][head, 0:1, 0:1] * table_refs[1][delta]
    return s


def _lane_masks(sub):
    lane = lax.broadcasted_iota(jnp.int32, (1, LANES), 1)
    return [(lane // (LANES // sub) == a).astype(F32) for a in range(sub)]


def _attn_fwd(q, qb0, k, kb0, v, vb0, tables, scale, *, name, batch, seq, sub=1, stride=1, wide_qk=False, after=None):
    t = ATT_T
    nq = seq // t
    rows = batch * seq
    n_tab = len(tables)
    qk_w = sub * LANES if wide_qk else LANES

    def body(q_ref, k_ref, v_ref, *rest):
        table_refs = rest[:n_tab]
        o_ref, lse_ref, vtb = rest[n_tab + (after is not None):][:3]
        per_head = rest[n_tab + (after is not None) + 3:]
        qbs, kbs = per_head[:sub], per_head[sub:]
        masks = _lane_masks(sub)
        for a in range(sub):
            lanes = slice(a * LANES, (a + 1) * LANES) if wide_qk else slice(None)
            qa = q_ref[:, lanes]
            qbs[a][...] = (qa.astype(F32) * masks[a]).astype(BF16) if sub > 1 and not wide_qk else qa.astype(BF16)
            if wide_qk or a == 0:
                kbs[a][...] = k_ref[:, lanes].astype(BF16)
        vtb[...] = v_ref[...].astype(F32).T.astype(BF16)
        for i in range(nq):
            out_t = None
            for a in range(sub):
                qt, kb = qbs[a][i * t:(i + 1) * t, :], kbs[a if wide_qk else 0]
                logits = [_biased(lax.dot_general(kb[j * t:(j + 1) * t, :], qt, NT, preferred_element_type=F32) * (scale * LOG2E), table_refs, i - j, a)
                          for j in range(i + 1)]
                m = jnp.max(functools.reduce(jnp.maximum, logits), axis=0, keepdims=True)
                ps = [jnp.exp2(s - m) for s in logits]
                l = jnp.sum(functools.reduce(jnp.add, ps), axis=0, keepdims=True)
                acc = functools.reduce(jnp.add, [lax.dot_general(vtb[:, j * t:(j + 1) * t], p.astype(BF16), NN, preferred_element_type=F32)
                                                 for j, p in enumerate(ps)])
                part = acc / l if sub == 1 else (acc / l) * masks[a].T
                out_t = part if out_t is None else out_t + part
                lse_ref[i * t:(i + 1) * t, a * LANES:(a + 1) * LANES] = jnp.broadcast_to((m + jnp.log2(l)) * LN2, (LANES, t)).T
            o_ref[i * t:(i + 1) * t, :] = out_t.T

    slab = lambda b0, step, width=LANES: pl.BlockSpec((seq, width), lambda b, g: (b, b0 + step * g))
    groups = N_HEADS // sub
    n_k = sub if wide_qk else 1
    return pl.pallas_call(
        body, name=name,
        grid=(batch, groups),
        in_specs=[slab(qb0, stride, qk_w), slab(kb0, stride, qk_w), slab(vb0, stride)] + _table_specs(tables, sub)
        + [pl.BlockSpec(memory_space=pl.ANY)] * (after is not None),
        out_specs=[slab(0, 1), slab(0, 1, sub * LANES)],
        out_shape=[jax.ShapeDtypeStruct((rows, groups * LANES), F32), jax.ShapeDtypeStruct((rows, N_HEADS * LANES), F32)],
        scratch_shapes=[pltpu.VMEM((LANES, seq), BF16)] + [pltpu.VMEM((seq, LANES), BF16)] * (sub + n_k),
        compiler_params=_params(("arbitrary", "arbitrary")),
    )(q, k, v, *tables, *([after] if after is not None else []))


def _attn_bwd(q, qb0, k, kb0, v, vb0, o, do, lse, tables, scale, *, name, batch, seq, out_dtype, sub=1, stride=1, wide_qk=False, after=None):
    t = ATT_T
    nq = seq // t
    rows = batch * seq
    n_tab = len(tables)
    groups = N_HEADS // sub
    packed = sub > 1 and not wide_qk
    n_out = 1 if packed else 3
    n_k = sub if wide_qk else 1
    qk_w = sub * LANES if wide_qk else LANES

    def body(q_ref, k_ref, v_ref, o_ref, do_ref, lse_ref, *rest):
        table_refs = rest[:n_tab]
        rest = rest[n_tab + (after is not None):]
        out_refs, (vb, dva), rest = rest[:n_out], rest[n_out:n_out + 2], rest[n_out + 2:]
        kbs, dkas, rest = rest[:n_k], rest[n_k:2 * n_k], rest[2 * n_k:]
        qbs, dobs, qtbs, dotbs = (rest[g * sub:(g + 1) * sub] for g in range(4))
        masks = _lane_masks(sub)
        vb[...] = v_ref[...].astype(BF16)
        for a in range(sub):
            lanes = slice(a * LANES, (a + 1) * LANES) if wide_qk else slice(None)
            qa = q_ref[:, lanes].astype(F32) * masks[a] if packed else q_ref[:, lanes].astype(F32)
            doa = do_ref[...] * masks[a] if sub > 1 else do_ref[...]
            qbs[a][...] = qa.astype(BF16)
            dobs[a][...] = doa.astype(BF16)
            qtbs[a][...] = qa.T.astype(BF16)
            dotbs[a][...] = doa.T.astype(BF16)
            if wide_qk or a == 0:
                kbs[a][...] = k_ref[:, lanes].astype(BF16)
        first_k, first_v = [[True] * nq for _ in range(n_k)], [True] * nq
        for i in range(nq):
            at = slice(i * t, (i + 1) * t)
            dq_all = None
            for a in range(sub):
                qt, dot, kb, dka = qbs[a][at, :], dobs[a][at, :], kbs[a if wide_qk else 0], dkas[a if wide_qk else 0]
                lse_t = lse_ref[at, a * LANES:a * LANES + 1] * LOG2E
                od = o_ref[at, :] * do_ref[at, :]
                delta = jnp.sum(od * masks[a] if sub > 1 else od, axis=1, keepdims=True)
                dq = None
                for j in range(i + 1):
                    kat = slice(j * t, (j + 1) * t)
                    kt, vt = kb[kat, :], vb[kat, :]
                    p = jnp.exp2(_biased(lax.dot_general(qt, kt, NT, preferred_element_type=F32) * (scale * LOG2E), table_refs, i - j, a) - lse_t)
                    dp = lax.dot_general(dot, vt, NT, preferred_element_type=F32)
                    ds = (p * (dp - delta) * scale).astype(BF16)
                    dk_part = lax.dot_general(qtbs[a][:, at], ds, NN, preferred_element_type=F32)
                    dv_part = lax.dot_general(dotbs[a][:, at], p.astype(BF16), NN, preferred_element_type=F32)
                    firsts = first_k[a if wide_qk else 0]
                    if firsts[j]:
                        dka[:, kat] = dk_part
                        firsts[j] = False
                    else:
                        dka[:, kat] += dk_part
                    if first_v[j]:
                        dva[:, kat] = dv_part
                        first_v[j] = False
                    else:
                        dva[:, kat] += dv_part
                    dq_part = lax.dot_general(ds, kt, NN, preferred_element_type=F32)
                    dq = dq_part if dq is None else dq + dq_part
                if wide_qk:
                    out_refs[0][at, a * LANES:(a + 1) * LANES] = dq.astype(out_refs[0].dtype)
                else:
                    dq = dq * masks[a] if sub > 1 else dq
                    dq_all = dq if dq_all is None else dq_all + dq
            if not wide_qk:
                out_refs[0][at, 0:LANES] = dq_all.astype(out_refs[0].dtype)
        if packed:
            out_refs[0][:, LANES:2 * LANES] = dkas[0][...].T.astype(out_refs[0].dtype)
            out_refs[0][:, 2 * LANES:3 * LANES] = dva[...].T.astype(out_refs[0].dtype)
            out_refs[0][:, 3 * LANES:] = jnp.zeros((seq, LANES), out_refs[0].dtype)
        else:
            for a in range(n_k):
                out_refs[1][:, a * LANES:(a + 1) * LANES] = dkas[a][...].T.astype(out_refs[1].dtype)
            out_refs[2][...] = dva[...].T.astype(out_refs[2].dtype)

    slab = lambda b0, step, width=LANES: pl.BlockSpec((seq, width), lambda b, g: (b, b0 + step * g))
    if packed:
        out_specs = [slab(0, 1, 4 * LANES)]
        out_shape = [jax.ShapeDtypeStruct((rows, groups * 4 * LANES), out_dtype)]
    else:
        out_specs = [slab(0, 1, qk_w), slab(0, 1, qk_w), slab(0, 1)]
        out_shape = [jax.ShapeDtypeStruct((rows, N_HEADS * LANES), out_dtype)] * 2 + [jax.ShapeDtypeStruct((rows, groups * LANES), out_dtype)]
    res = pl.pallas_call(
        body, name=name,
        grid=(batch, groups),
        in_specs=[slab(qb0, stride, qk_w), slab(kb0, stride, qk_w), slab(vb0, stride), slab(0, 1), slab(0, 1), slab(0, 1, sub * LANES)]
        + _table_specs(tables, sub) + [pl.BlockSpec(memory_space=pl.ANY)] * (after is not None),
        out_specs=out_specs, out_shape=out_shape,
        scratch_shapes=[pltpu.VMEM((seq, LANES), BF16), pltpu.VMEM((LANES, seq), F32)]
        + [pltpu.VMEM((seq, LANES), BF16)] * n_k + [pltpu.VMEM((LANES, seq), F32)] * n_k
        + [pltpu.VMEM((seq, LANES), BF16)] * (2 * sub) + [pltpu.VMEM((LANES, seq), BF16)] * (2 * sub),
        compiler_params=_params(("arbitrary", "arbitrary")),
    )(q, k, v, o, do, lse, *tables, *([after] if after is not None else []))
    return res[0] if packed else res


def _attention_tables(seq):
    n = seq // ATT_T
    pos = np.arange(ATT_T, dtype=np.int32)
    dist = np.arange(n, dtype=np.int32)[:, None, None] * ATT_T + pos[None, :, None] - pos[None, None, :]
    causal = np.where(dist[:1] >= 0, 0.0, NEG).astype(np.float32)
    count = np.zeros(dist.shape, np.float32)
    for window, dilation in DIL_PATTERNS:
        count += ((dist >= 0) & (dist <= window) & (dist % dilation == 0)).astype(np.float32)
    held = np.where(count > 0, np.log2(np.maximum(count, 1.0)), NEG).astype(np.float32)
    far = dist.astype(np.float32) * np.float32(LOG2E)
    slopes = np.asarray([2.0 ** (-8.0 * (i + 1) / N_HEADS) for i in range(N_HEADS)], np.float32)
    slopes = jnp.asarray(np.broadcast_to(slopes[:, None, None], (N_HEADS, 1, LANES)))
    flat = lambda a: jnp.asarray(np.ascontiguousarray(a))
    turned = lambda a: flat(np.swapaxes(a, 1, 2))
    return ((flat(causal),), (turned(causal),)), ((flat(held), flat(far), slopes), (turned(held), turned(far), slopes))


def _pad_heads(w, width):
    kdim, n = w.shape[0], w.shape[1] // width
    return jnp.pad(w.reshape(kdim, n, width), ((0, 0), (0, 0), (0, LANES - width))).reshape(kdim, n * LANES)


def _unpad_heads(w, width):
    kdim, n = w.shape[0], w.shape[1] // LANES
    return w.reshape(kdim, n, LANES)[:, :, :width].reshape(kdim, n * width)


def _pad_head_rows(w, width):
    n, kdim = w.shape[0] // width, w.shape[1]
    return jnp.pad(w.reshape(n, width, kdim), ((0, 0), (0, LANES - width), (0, 0))).reshape(n * LANES, kdim)


def _unpad_head_rows(w, width):
    n, kdim = w.shape[0] // LANES, w.shape[1]
    return w.reshape(n, LANES, kdim)[:, :width].reshape(n * width, kdim)


def _pad_w_in_t(wt):
    n_qkv, pair = 3 * N_HEADS * DIL_DIM, 2 * DIL_DIM
    zeros = lambda n: jnp.zeros((n, wt.shape[1]), wt.dtype)
    main = jnp.concatenate([wt[:LORA_W], zeros(KR_LANE), wt[LORA_W:LORA_W + ROPE], zeros(P_GATE - P_KR - KR_LANE - ROPE),
                            wt[LORA_W + ROPE + n_qkv:]], axis=0)
    qkv = wt[LORA_W + ROPE:LORA_W + ROPE + n_qkv].reshape(3, N_HEADS // 2, pair, wt.shape[1]).transpose(1, 0, 2, 3)
    dil = jnp.pad(qkv, ((0, 0), (0, 1), (0, 0), (0, 0))).reshape(P_DIL, wt.shape[1])
    return main, dil


def _unpad_w_in_t(gt):
    n_qkv, pair = 3 * N_HEADS * DIL_DIM, 2 * DIL_DIM
    moves = [(P_LORA, 0, LORA_W), (P_KR + KR_LANE, LORA_W, ROPE)]
    moves += [(P_HALF + (g * 4 + j) * pair, LORA_W + ROPE + (j * (N_HEADS // 2) + g) * pair, pair) for j in range(3) for g in range(N_HEADS // 2)]
    moves += [(P_GATE + r, LORA_W + ROPE + n_qkv + r, ROW_T) for r in range(0, P_HALF - P_GATE, ROW_T)]
    n = len(moves)

    def body(src, out, buf, sem_in, sem_out):
        ins = [pltpu.make_async_copy(src.at[pl.ds(a, k)], buf.at[pl.ds(b, k)], sem_in.at[i]) for i, (a, b, k) in enumerate(moves)]
        outs = [pltpu.make_async_copy(buf.at[pl.ds(b, k)], out.at[pl.ds(b, k)], sem_out.at[i]) for i, (a, b, k) in enumerate(moves)]
        for cp in ins:
            cp.start()
        for cp_in, cp_out in zip(ins, outs):
            cp_in.wait()
            cp_out.start()
        for cp in outs:
            cp.wait()

    return pl.pallas_call(
        body, name="unpad_d_w_in", in_specs=[pl.BlockSpec(memory_space=pl.ANY)], out_specs=pl.BlockSpec(memory_space=pl.ANY),
        out_shape=jax.ShapeDtypeStruct((IN_WIDTH, gt.shape[1]), gt.dtype),
        scratch_shapes=[pltpu.VMEM((IN_WIDTH, gt.shape[1]), gt.dtype), pltpu.SemaphoreType.DMA((n,)), pltpu.SemaphoreType.DMA((n,))],
        compiler_params=_params(),
    )(gt)


def _split_ukv(w):
    w3 = w.reshape(w.shape[0], N_HEADS, NOPE + V_DIM)
    return _pad_heads(w3[:, :, :NOPE].reshape(w.shape[0], -1), NOPE), w3[:, :, NOPE:].reshape(w.shape[0], -1)


def _merge_ukv(g_k, g_v):
    kdim = g_k.shape[0]
    k3 = _unpad_heads(g_k, NOPE).reshape(kdim, N_HEADS, NOPE)
    return jnp.concatenate([k3, g_v.reshape(kdim, N_HEADS, V_DIM)], axis=2).reshape(kdim, N_HEADS * (NOPE + V_DIM))


def _join_cols(w):
    return w.transpose(1, 0, 2).reshape(w.shape[1], N_CHIPS * w.shape[2])


def _split_cols(g):
    return g.reshape(g.shape[0], N_CHIPS, g.shape[1] // N_CHIPS).transpose(1, 0, 2)


def _local_step(x3, target3, wg, b_gate, g_q_a, g_kv_a, ln1_g, ln1_b, ln2_g, ln2_b, token=None, late_arrived=None, late_weights=None,
                early_grads=None, early_grads_go=None, first_grad=None, last_grads=None, tables=None, rope=None, x_lo=None):
    w_main_t, w_dil_t = _pad_w_in_t(wg["w_in"].reshape(IN_WIDTH, D_MODEL))
    w_uq_pt = _pad_head_rows(wg["w_uq"].reshape(N_HEADS * MLA_QK, Q_LORA), MLA_QK)
    w_ukv = _join_cols(wg["w_ukv"])
    batch, seq, _ = x3.shape
    rows = batch * seq
    x = x3.reshape(rows, D_MODEL)
    x_mm = x if x_lo is None else x_lo.reshape(rows, D_MODEL)
    target = target3.reshape(rows, D_MODEL)
    row = functools.partial(_rowwise, rows=rows, seq=seq)
    mm = _matmul

    w_uk_p, w_uv = _split_ukv(w_ukv)
    b0, b1 = b_gate[0:1], b_gate[1:2]
    rope_c, rope_up, rope_dn, rope_c_only = _rope_tables(jnp.arange(seq, dtype=F32)) if rope is None else rope
    (mla_bwd_tables, mla_fwd_tables), (dil_bwd_tables, dil_fwd_tables) = _attention_tables(seq) if tables is None else tables
    scale_mla, scale_dil = MLA_QK ** -0.5, DIL_DIM ** -0.5
    lora0, kr0, gate0 = P_LORA // LORA_W, P_KR // LANES, P_GATE // D_MODEL

    proj = mm(x_mm, w_main_t, mode="nt", name="proj", tm=1024, tn=1536, tk=1024, after=token)
    proj_d = mm(x_mm, w_dil_t, mode="nt", name="proj_dil", tm=1024, tn=1024, tk=1024, out_dtypes=(BF16,))

    def prep(lora, gq, gkv):
        return _rms(lora[:, :Q_LORA], gq), _rms(lora[:, Q_LORA:], gkv)

    qn, kvn = row(prep, name="mla_rms", ins=[(proj, LORA_W, lora0, "row"), (g_q_a, 0, 0, "full"), (g_kv_a, 0, 0, "full")],
                  outs=[(Q_LORA, BF16), (KV_LORA, BF16)])
    q_lin = mm(qn, w_uq_pt, mode="nt", name="q_up", tm=1024, tn=1024, tk=Q_LORA)
    k_lin = mm(kvn, w_uk_p, mode="nn", name="k_up", tm=1024, tn=1024, tk=KV_LORA)
    v_a = mm(kvn, w_uv, mode="nn", name="v_up", tm=1024, tn=1024, tk=KV_LORA, out_dtypes=(BF16,))

    def rope_qk(ql, kl, kr, c, up, dn):
        k_rot = _rope_fwd(kr, c, up, dn)
        qs = [_rope_fwd(ql[:, h * LANES:(h + 1) * LANES], c, up, dn) for h in range(N_HEADS)]
        ks = [kl[:, h * LANES:(h + 1) * LANES] + k_rot for h in range(N_HEADS)]
        return jnp.concatenate(qs, axis=1), jnp.concatenate(ks, axis=1)

    pos = lambda tab: (tab, LANES, 0, "pos")
    q_a, k_a = row(rope_qk, name="rope_qk",
                   ins=[(q_lin, D_MODEL, 0, "row"), (k_lin, D_MODEL, 0, "row"), (proj, LANES, kr0, "row"), pos(rope_c), pos(rope_up), pos(rope_dn)],
                   outs=[(N_HEADS * LANES, BF16), (N_HEADS * LANES, BF16)])
    o_a, lse_a = _attn_fwd(q_a, 0, k_a, 0, v_a, 0, mla_fwd_tables, scale_mla, name="mla_fwd", batch=batch, seq=seq, sub=2, wide_qk=True)
    arrived = None if late_arrived is None else late_arrived(o_a)
    o_b, lse_b = _attn_fwd(proj_d, 0, proj_d, 1, proj_d, 2, dil_fwd_tables, scale_dil, name="dil_fwd", batch=batch, seq=seq, sub=2, stride=4, after=arrived)
    late = wg if late_weights is None else late_weights(o_b)
    w_oa = _join_cols(late["w_o_mla"])
    w_ob = _join_cols(late["w_o_dil"])
    w_out, w_ff1, w_ff2 = late["w_out"].reshape(D_MODEL, D_MODEL), late["w_ff1"], late["w_ff2"].reshape(D_FF, D_MODEL)
    y_a = mm(o_a, w_oa, mode="nn", name="o_mla", tm=1024, tn=1024, tk=1024, out_dtypes=(BF16,))
    y_b = mm(o_b, w_ob, mode="nn", name="o_dil", tm=1024, tn=1024, tk=1024, out_dtypes=(BF16,))

    def gate(t0, t1, c0, c1, ya, yb):
        return (jax.nn.sigmoid(t0 + c0) * ya + jax.nn.sigmoid(t1 + c1) * yb,)

    gate_ins = [(proj, D_MODEL, gate0, "row"), (proj, D_MODEL, gate0 + 1, "row"), (b0, 0, 0, "full"), (b1, 0, 0, "full")]
    (u,) = row(gate, name="gate", ins=gate_ins + [(y_a, D_MODEL, 0, "row"), (y_b, D_MODEL, 0, "row")], outs=[(D_MODEL, BF16)])
    def ln1(mv, xv, g, b):
        r1 = ALPHA * xv + mv
        xh, _ = _ln_stats(r1)
        hv = xh * g + b
        return r1, hv, hv

    r1, h, h_b = mm(u, w_out, mode="nn", name="mix_ln1", tm=ROW_T, tn=D_MODEL, tk=D_MODEL, out_dtypes=(F32, F32, BF16),
                    extras=(x,), row_extras=(ln1_g, ln1_b), epilogue=ln1)

    def relu2(acc):
        r = jnp.maximum(acc, 0.0)
        return (r * r,)

    z = mm(h_b, w_ff1, mode="nn", name="ff1", tm=2048, tn=1024, tk=1024, out_dtypes=(BF16,), epilogue=relu2, b_shards=True)
    f = mm(z, w_ff2, mode="nn", name="ff2", tm=1024, tn=1024, tk=D_FF)

    def ln2_loss(hv, fv, tv, g, b):
        xh, r = _ln_stats(ALPHA * hv + fv)
        err = xh * g + b - tv
        dy = err * (1.0 / D_MODEL)
        dr2, dg, db = _ln_bwd(xh, r, g, dy)
        loss = jnp.sum(_colsum(err * err), axis=1, keepdims=True) * (0.5 / D_MODEL)
        return dr2, dr2, jnp.broadcast_to(loss, (1, LANES)), dg, db

    dr2, dr2_b, loss_l, d_ln2_g, d_ln2_b = row(
        ln2_loss, name="ln2_loss",
        ins=[(h, D_MODEL, 0, "row"), (f, D_MODEL, 0, "row"), (target, D_MODEL, 0, "row"), (ln2_g, 0, 0, "full"), (ln2_b, 0, 0, "full")],
        outs=[(D_MODEL, F32), (D_MODEL, BF16)], sums=[LANES, D_MODEL, D_MODEL])

    d_w_ff2 = mm(z, dr2_b, mode="tn", name="d_w_ff2", tm=1024, tn=1024, tk=rows)
    da = mm(dr2_b, w_ff2, mode="nt", name="d_ff_act", tm=2048, tn=1024, tk=1024, out_dtypes=(BF16,), extras=(z,),
            epilogue=lambda acc, zv: (acc * (2.0 * _sqrt(zv.astype(F32))),))
    d_w_ff1 = mm(h_b, da, mode="tn", name="d_w_ff1", tm=1024, tn=1024, tk=rows, out_shards=True)
    dh = mm(da, w_ff1, mode="nt", name="d_h", tm=1024, tn=1024, tk=1024, extras=(dr2,), epilogue=lambda acc, rv: (acc + ALPHA * rv,), b_shards=True)

    def ln1_bwd(dhv, r1v, g):
        xh, r = _ln_stats(r1v)
        return _ln_bwd(xh, r, g, dhv)

    dr1, d_ln1_g, d_ln1_b = row(ln1_bwd, name="ln1_bwd", ins=[(dh, D_MODEL, 0, "row"), (r1, D_MODEL, 0, "row"), (ln1_g, 0, 0, "full")],
                                outs=[(D_MODEL, F32)], sums=[D_MODEL, D_MODEL])
    d_w_out = mm(u, dr1, mode="tn", name="d_w_out", tm=1024, tn=1024, tk=1024)
    du = mm(dr1, w_out, mode="nt", name="d_u", tm=1024, tn=1024, tk=1024, out_dtypes=(BF16,))

    def gate_bwd(t0, t1, c0, c1, ya, yb, duv):
        s0, s1 = jax.nn.sigmoid(t0 + c0), jax.nn.sigmoid(t1 + c1)
        dt0 = duv * ya * s0 * (1.0 - s0)
        dt1 = duv * yb * s1 * (1.0 - s1)
        return duv * s0, duv * s1, jnp.concatenate([dt0, dt1], axis=1), jnp.concatenate([_colsum(dt0), _colsum(dt1)], axis=1)

    dy_a, dy_b, d_gates, d_b_gate = row(
        gate_bwd, name="gate_bwd", ins=gate_ins + [(y_a, D_MODEL, 0, "row"), (y_b, D_MODEL, 0, "row"), (du, D_MODEL, 0, "row")],
        outs=[(D_MODEL, BF16), (D_MODEL, BF16), (2 * D_MODEL, BF16)], sums=[2 * D_MODEL])
    d_w_oa = mm(o_a, dy_a, mode="tn", name="d_w_o_mla", tm=1024, tn=1024, tk=1024)
    d_w_ob = mm(o_b, dy_b, mode="tn", name="d_w_o_dil", tm=1024, tn=1024, tk=1024)
    grads = dict(w_o_mla=_split_cols(d_w_oa), w_o_dil=_split_cols(d_w_ob),
                 w_out=d_w_out.reshape(N_CHIPS, D_MODEL // N_CHIPS, D_MODEL), w_ff1=d_w_ff1, w_ff2=d_w_ff2.reshape(N_CHIPS, D_FF // N_CHIPS, D_MODEL))
    sent = None if early_grads is None else early_grads(grads)
    do_a = mm(dy_a, w_oa, mode="nt", name="d_o_mla", tm=1024, tn=1024, tk=1024, after=sent)
    do_b = mm(dy_b, w_ob, mode="nt", name="d_o_dil", tm=1024, tn=1024, tk=1024)
    dq_a, dk_a, dv_a = _attn_bwd(q_a, 0, k_a, 0, v_a, 0, o_a, do_a, lse_a, mla_bwd_tables, scale_mla,
                                 name="mla_bwd", batch=batch, seq=seq, out_dtype=F32, sub=2, wide_qk=True)
    going = None if early_grads_go is None else early_grads_go(dq_a)
    d_qkv_d = _attn_bwd(proj_d, 0, proj_d, 1, proj_d, 2, o_b, do_b, lse_b, dil_bwd_tables, scale_dil,
                        name="dil_bwd", batch=batch, seq=seq, out_dtype=BF16, sub=2, stride=4, after=going)

    def mla_post(dq, dk, c, up, dn, c_only):
        dqs = [_rope_bwd(dq[:, h * LANES:(h + 1) * LANES], c, up, dn) for h in range(N_HEADS)]
        dk_sum = dk[:, :LANES]
        for h in range(1, N_HEADS):
            dk_sum = dk_sum + dk[:, h * LANES:(h + 1) * LANES]
        return jnp.concatenate(dqs, axis=1), _rope_bwd(dk_sum, c_only, up, dn)

    dq_lin, d_kr = row(mla_post, name="mla_unrope",
                       ins=[(dq_a, D_MODEL, 0, "row"), (dk_a, D_MODEL, 0, "row"), pos(rope_c), pos(rope_up), pos(rope_dn), pos(rope_c_only)],
                       outs=[(N_HEADS * LANES, BF16), (LANES, BF16)])
    d_qn = mm(dq_lin, w_uq_pt, mode="nn", name="d_qn", tm=1024, tn=Q_LORA, tk=1024)
    d_kvn_k = mm(dk_a, w_uk_p, mode="nt", name="d_kvn_k", tm=1024, tn=KV_LORA, tk=1024)
    d_kvn = mm(dv_a, w_uv, mode="nt", name="d_kvn", tm=1024, tn=KV_LORA, tk=1024, extras=(d_kvn_k,), epilogue=lambda acc, e: (acc + e,))

    def rms_bwd(lora, dq, dkv, dkr, gq, gkv):
        dxq, dgq = _rms_bwd(lora[:, :Q_LORA], gq, dq)
        dxk, dgk = _rms_bwd(lora[:, Q_LORA:], gkv, dkv)
        tail = jnp.zeros((dxq.shape[0], P_GATE - P_KR - LANES), F32)
        return jnp.concatenate([dxq, dxk, dkr.astype(F32), tail], axis=1), dgq, dgk

    d_tail, d_g_q_a, d_g_kv_a = row(
        rms_bwd, name="mla_rms_bwd",
        ins=[(proj, LORA_W, lora0, "row"), (d_qn, Q_LORA, 0, "row"), (d_kvn, KV_LORA, 0, "row"), (d_kr, LANES, 0, "row"),
             (g_q_a, 0, 0, "full"), (g_kv_a, 0, 0, "full")],
        outs=[(P_GATE, BF16)], sums=[Q_LORA, KV_LORA])
    d_proj = [d_tail, d_gates, d_qkv_d]
    d_w_in_pt = mm(d_proj, x_mm, mode="tn", name="d_w_in", tm=1024, tn=1024, tk=2048)
    d_w_in = _unpad_w_in_t(d_w_in_pt).reshape(N_CHIPS, IN_WIDTH // N_CHIPS, D_MODEL)
    moving = None if first_grad is None else first_grad(d_w_in)
    d_w_uq_pt = mm(dq_lin, qn, mode="tn", name="d_w_uq", tm=1024, tn=Q_LORA, tk=1024, after=moving)
    d_w_uk_p = mm(kvn, dk_a, mode="tn", name="d_w_uk", tm=KV_LORA, tn=1024, tk=1024, after=moving)
    d_w_uv = mm(kvn, dv_a, mode="tn", name="d_w_uv", tm=KV_LORA, tn=1024, tk=1024, after=moving)
    grads.update(w_in=d_w_in,
                 w_uq=_unpad_head_rows(d_w_uq_pt, MLA_QK).reshape(N_CHIPS, N_HEADS * MLA_QK // N_CHIPS, Q_LORA),
                 w_ukv=_split_cols(_merge_ukv(d_w_uk_p, d_w_uv)))
    leaving = None if last_grads is None else last_grads(grads)
    grad_x = mm(d_proj, jnp.concatenate([w_main_t, w_dil_t], axis=0), mode="nn", name="d_x", tm=1024, tn=1024, tk=1024, extras=(dr1,), epilogue=lambda acc, rv: (acc + ALPHA * rv,),
                after=leaving)

    grads.update(
        b_gate=d_b_gate.reshape(2, D_MODEL), g_q_a=d_g_q_a, g_kv_a=d_g_kv_a, ln1_g=d_ln1_g, ln1_b=d_ln1_b, ln2_g=d_ln2_g, ln2_b=d_ln2_b)
    return loss_l, grad_x.reshape(batch, seq, D_MODEL), grads


BIG = ("w_in", "w_uq", "w_ukv", "w_o_mla", "w_o_dil", "w_out", "w_ff1", "w_ff2")
SMALL = (("b_gate", 2 * D_MODEL), ("g_q_a", Q_LORA), ("g_kv_a", KV_LORA), ("ln1_g", D_MODEL), ("ln1_b", D_MODEL),
         ("ln2_g", D_MODEL), ("ln2_b", D_MODEL))
TRANSPOSED = ("w_in", "w_uq")
D2D_PIECES = (4, 2, 1)
ANY = pl.BlockSpec(memory_space=pl.ANY)
SIDE_EFFECTS = pltpu.CompilerParams(has_side_effects=True)


def _place():
    x, y, c = lax.axis_index("x"), lax.axis_index("y"), lax.axis_index("c")
    return x, y, c, ((1 - x, y), (x, 1 - y), (1 - x, 1 - y))


def _half_axis(shape):
    return 0 if shape[0] % 32 == 0 else 1


def _half_shape(shape):
    return (shape[0] // 2, shape[1]) if _half_axis(shape) == 0 else (shape[0], shape[1] // 2)


def _window(ref, lead, shape, which=None, pieces=False):
    axis = _half_axis(shape)
    size = shape[axis] if which is None else shape[axis] // 2
    base = 0 if which is None else which * size
    tile = (16, LANES)[axis]
    count = next(c for c in D2D_PIECES if size % (tile * c) == 0) if pieces else 1
    step = size // count
    spans = [pl.ds(pl.multiple_of(base + i * step, tile), step) for i in range(count)]
    refs = [ref.at[(*lead, s)] if axis == 0 else ref.at[(*lead, slice(None), s)] for s in spans]
    return refs if pieces else refs[0]


def _remote(src, dst, send, recv, to):
    return pltpu.make_async_remote_copy(src_ref=src, dst_ref=dst, send_sem=send, recv_sem=recv, device_id=to, device_id_type=MESH)


def _pair_split(grads, name):
    n = len(grads)

    def body(*refs):
        srcs, outs, (send, recv) = refs[:n], refs[n:2 * n], refs[2 * n:]
        x, y, c, _ = _place()
        for t in range(n):
            for s in range(N_CHIPS):
                _remote(_window(srcs[t], (s,), grads[t].shape[1:], 1 - c), outs[t].at[s], send.at[t], recv.at[t], (x, y, 1 - c)).start()
        for t in range(n):
            _remote(_window(srcs[t], (slice(None),), grads[t].shape[1:], 1 - c), outs[t], send.at[t], recv.at[t], (x, y, 1 - c)).wait()

    return pl.pallas_call(
        body, name=name, in_specs=[ANY] * n, out_specs=[ANY] * n,
        out_shape=[jax.ShapeDtypeStruct((N_CHIPS,) + _half_shape(g.shape[1:]), g.dtype) for g in grads],
        scratch_shapes=[pltpu.SemaphoreType.DMA((n,)), pltpu.SemaphoreType.DMA((n,))],
        compiler_params=SIDE_EFFECTS,
    )(*grads)


HBM = pl.BlockSpec(memory_space=pltpu.HBM)
SEM = pl.BlockSpec(memory_space=pltpu.SEMAPHORE)
SPLIT = pltpu.CompilerParams(has_side_effects=pltpu.SideEffectType.DATAFLOW_SIDE_EFFECTING)


def _in_hbm(a):
    return pltpu.with_memory_space_constraint(a, pltpu.HBM)


def _split_copies(kind, srcs, lands):
    x, y, c, chips = _place()
    out = []
    for t in range(len(srcs)):
        if kind == "pair":
            out += [(t, s % 3, _window(srcs[t], (s,), srcs[t].shape[1:], 1 - c), lands[t].at[s], (x, y, 1 - c)) for s in range(N_CHIPS)]
            continue
        if kind == "join":
            out += [(t, 0, a, b, (x, y, 1 - c)) for a, b in zip(_window(srcs[t], (), srcs[t].shape, None, True), _window(lands[t], (), srcs[t].shape, None, True))]
            continue
        if kind == "forward":
            shape, sibling = srcs[t].shape, (x, y, 1 - c)
            out += [(t, 0, a, b, sibling) for a, b in zip(_window(srcs[t], (), shape, None, True), _window(lands[t], (2 * x + y,), shape, None, True))]
            out += [(t, j, a, a, sibling) for j, (cx, cy) in enumerate(chips) for a in _window(lands[t], (2 * cx + cy,), shape, c, True)]
            continue
        for j, (cx, cy) in enumerate(chips):
            if kind == "gather":
                shape = srcs[t].shape
                out.append((t, j, _window(srcs[t], (), shape, c), _window(lands[t], (2 * x + y,), shape, c), (cx, cy, c)))
            else:
                out.append((t, j, srcs[t].at[2 * cx + cy], lands[t].at[j], (cx, cy, c)))
    return out


def _split_start(kind, srcs, land_shapes, name, lands=None, after=None):
    n = len(srcs)

    def body(*refs):
        src_refs, land_refs, sems, token = refs[:n], refs[n:2 * n], refs[-7 - 2 * n:-1 - 2 * n], refs[-1]
        for t, j, s, d, to in _split_copies(kind, src_refs, land_refs):
            _remote(s, d, sems[j], sems[3 + j], to).start()
        token[...] = jnp.zeros_like(token)

    lands = [_in_hbm(lax.empty(s.shape, s.dtype)) for s in land_shapes] if lands is None else list(lands)
    thru = [pltpu.HBM(a.shape, a.dtype) for a in list(srcs) + lands]
    res = pl.pallas_call(
        body, name=name,
        out_shape=(*[pltpu.SemaphoreType.DMA(())] * 6, *thru, jax.ShapeDtypeStruct((8, LANES), F32)),
        in_specs=[HBM] * (2 * n) + [ANY] * (after is not None), out_specs=(*[SEM] * 6, *[HBM] * (2 * n), pl.BlockSpec(memory_space=pltpu.VMEM)),
        input_output_aliases={i: 6 + i for i in range(2 * n)}, compiler_params=SPLIT,
    )(*[_in_hbm(s) for s in srcs], *lands, *([after] if after is not None else []))
    return res[:6], res[6:6 + n], res[6 + n:6 + 2 * n], res[-1]


def _split_wait(kind, sems, srcs, lands, after, name):
    n = len(srcs)
    afters = list(after) if isinstance(after, (list, tuple)) else [after]

    def body(*refs):
        src_refs, land_refs, sem_refs = refs[:n], refs[n:2 * n], refs[2 * n:2 * n + 6]
        for t, j, s, d, to in _split_copies(kind, src_refs, land_refs):
            cp = _remote(s, d, sem_refs[j], sem_refs[3 + j], to)
            cp.wait_send()
            cp.wait_recv()

    res = pl.pallas_call(
        body, name=name, out_shape=[pltpu.HBM(a.shape, a.dtype) for a in list(srcs) + list(lands)],
        in_specs=[HBM] * (2 * n) + [SEM] * 6 + [ANY] * len(afters), out_specs=[HBM] * (2 * n),
        input_output_aliases={i: i for i in range(2 * n)}, compiler_params=SPLIT,
    )(*srcs, *lands, *sems, *afters)
    return res[:n], res[n:]


def _sum_all_devices(vec, name, after):
    n_rows = vec.shape[0]

    def body(v_ref, *rest):
        out_ref, buf, send, recv = rest[len(after):]
        x, y, c, _ = _place()
        me = 4 * x + 2 * y + c
        buf[me] = v_ref[...]
        flips = [(a, b, d) for a in (0, 1) for b in (0, 1) for d in (0, 1)][1:]
        copies = []
        for r, (a, b, d) in enumerate(flips):
            px, py, pc = (1 - x if a else x), (1 - y if b else y), (1 - c if d else c)
            copies.append(pltpu.make_async_remote_copy(src_ref=v_ref, dst_ref=buf.at[me], send_sem=send.at[r], recv_sem=recv.at[r],
                                                       device_id=(px, py, pc), device_id_type=MESH))
            copies[-1].start()
        for r, (a, b, d) in enumerate(flips):
            px, py, pc = (1 - x if a else x), (1 - y if b else y), (1 - c if d else c)
            pltpu.make_async_remote_copy(src_ref=v_ref, dst_ref=buf.at[4 * px + 2 * py + pc], send_sem=send.at[r], recv_sem=recv.at[r],
                                         device_id=(px, py, pc), device_id_type=MESH).wait_recv()
        for cp in copies:
            cp.wait_send()
        total = buf[0]
        for k in range(1, N_DEV):
            total = total + buf[k]
        out_ref[...] = total

    vmem = pl.BlockSpec(memory_space=pltpu.VMEM)
    return pl.pallas_call(
        body, name=name, in_specs=[vmem] + [ANY] * len(after), out_specs=vmem, out_shape=jax.ShapeDtypeStruct(vec.shape, F32),
        scratch_shapes=[pltpu.VMEM((N_DEV, n_rows, LANES), F32), pltpu.SemaphoreType.DMA((N_DEV - 1,)), pltpu.SemaphoreType.DMA((N_DEV - 1,))],
        compiler_params=pltpu.CompilerParams(has_side_effects=True),
    )(vec, *after)


def _half_tile(half, width):
    t = half
    while t * width * 4 > (2 << 20) and t % 32 == 0:
        t //= 2
    return t


def _pair_add(g, theirs, core, name):
    _, half, width = theirs.shape
    t = _half_tile(half, width)
    n = half // t

    def body(c_ref, a_ref, b_ref, o_ref):
        o_ref[...] = (a_ref[...] + b_ref[...]).astype(BF16)

    tile = pl.BlockSpec((1, t, width), lambda j, i, c_ref: (j, i, 0))
    if _half_axis(g.shape[1:]) == 0:
        mine = pl.BlockSpec((1, t, width), lambda j, i, c_ref: (j, c_ref[0] * n + i, 0))
    else:
        mine = pl.BlockSpec((1, t, width), lambda j, i, c_ref: (j, i, c_ref[0]))
    return pl.pallas_call(
        body, name=name,
        grid_spec=pltpu.PrefetchScalarGridSpec(num_scalar_prefetch=1, grid=(N_CHIPS, n), in_specs=[mine, tile], out_specs=tile),
        out_shape=jax.ShapeDtypeStruct(theirs.shape, BF16), compiler_params=_params(("parallel", "parallel")),
    )(core, g, theirs)


def _chip_sum(part, others, chip, name, after=None):
    _, half, width = part.shape
    t = _half_tile(half, width)

    def body(s_ref, mine, p0, p1, p2, *rest):
        o_ref = rest[-1]
        o_ref[...] = ((mine[0].astype(F32) + p0[0].astype(F32)) + p1[0].astype(F32)) + p2[0].astype(F32)

    return pl.pallas_call(
        body, name=name,
        grid_spec=pltpu.PrefetchScalarGridSpec(
            num_scalar_prefetch=1, grid=(half // t,),
            in_specs=[pl.BlockSpec((1, t, width), lambda i, s_ref: (s_ref[0], i, 0))]
            + [pl.BlockSpec((1, t, width), lambda i, s_ref, j=j: (j, i, 0)) for j in range(3)] + [pl.BlockSpec(memory_space=pl.ANY)] * (after is not None),
            out_specs=pl.BlockSpec((t, width), lambda i, s_ref: (i, 0))),
        out_shape=jax.ShapeDtypeStruct((half, width), F32), compiler_params=_params(("parallel",)),
    )(chip, part, others, others, others, *([after] if after is not None else []))


EARLY = ("w_in", "w_uq", "w_ukv")
LATE = ("w_o_mla", "w_o_dil", "w_out", "w_ff1", "w_ff2")


def _sum_small(vals):
    n_in = len(vals)
    n_rows = sum(a.shape[0] * a.shape[1] // LANES for a in vals)
    pad_rows = -(-n_rows // 8) * 8

    def chunks(refs):
        return [(ref, a, j) for ref in refs for a in range(ref.shape[0]) for j in range(ref.shape[1] // LANES)]

    def body(*refs):
        ins, outs, (buf, send, recv) = refs[:n_in], refs[n_in:2 * n_in], refs[2 * n_in:]
        x, y, c, _ = _place()
        me = 4 * x + 2 * y + c
        for r, (ref, a, j) in enumerate(chunks(ins)):
            buf[me, r:r + 1, :] = ref[a:a + 1, j * LANES:(j + 1) * LANES]
        if pad_rows > n_rows:
            buf[me, n_rows:pad_rows, :] = jnp.zeros((pad_rows - n_rows, LANES), F32)
        flips = [(a, b, d) for a in (0, 1) for b in (0, 1) for d in (0, 1)][1:]
        peers = [((1 - x if a else x), (1 - y if b else y), (1 - c if d else c)) for a, b, d in flips]
        copies = [_remote(buf.at[me], buf.at[me], send.at[r], recv.at[r], peer) for r, peer in enumerate(peers)]
        for cp in copies:
            cp.start()
        for r, (px, py, pc) in enumerate(peers):
            _remote(buf.at[me], buf.at[4 * px + 2 * py + pc], send.at[r], recv.at[r], (px, py, pc)).wait_recv()
        for cp in copies:
            cp.wait_send()
        total = buf[0]
        for k in range(1, N_DEV):
            total = total + buf[k]
        for r, (ref, a, j) in enumerate(chunks(outs)):
            ref[a:a + 1, j * LANES:(j + 1) * LANES] = total[r:r + 1, :]

    vmem = pl.BlockSpec(memory_space=pltpu.VMEM)
    return pl.pallas_call(
        body, name="sum_small", in_specs=[vmem] * n_in, out_specs=[vmem] * n_in,
        out_shape=[jax.ShapeDtypeStruct(a.shape, F32) for a in vals],
        scratch_shapes=[pltpu.VMEM((N_DEV, pad_rows, LANES), F32), pltpu.SemaphoreType.DMA((N_DEV - 1,)), pltpu.SemaphoreType.DMA((N_DEV - 1,))],
        compiler_params=SIDE_EFFECTS,
    )(*vals)


def _adam_math(w, g, m, v):
    nm = B1 * m + (1.0 - B1) * g
    nv = B2 * v + (1.0 - B2) * (g * g)
    m_hat = nm / (1.0 - B1 ** ADAM_STEP)
    v_hat = nv / (1.0 - B2 ** ADAM_STEP)
    return -LR * (m_hat / (jnp.sqrt(v_hat) + ADAM_EPS) + WD * w), nm, nv


def _adamw_big(w, mine, theirs, m, v, core, name, side_by_side=False):
    rows, width = w.shape
    if side_by_side:
        t = next(c for c in (152, 96, 64, 32, 16, 8) if rows % c == 0)
        hb = None
        half_spec = pl.BlockSpec((t, width // 2), lambda i, c_ref: (i, 0))
    else:
        t = next(c for c in (256, 128, 64, 32, 16, 8) if (rows // 2) % c == 0)
        hb = rows // 2 // t
        half_spec = pl.BlockSpec((t, width), lambda i, c_ref: (i % hb, 0))

    def body(c_ref, w_ref, a_ref, b_ref, m_ref, v_ref, g_ref, d_ref, nm_ref, nv_ref):
        south = c_ref[0] == 0
        if side_by_side:
            g = jnp.where(south, jnp.concatenate([a_ref[...], b_ref[...]], axis=1), jnp.concatenate([b_ref[...], a_ref[...]], axis=1))
        else:
            g = jnp.where((pl.program_id(0) < hb) == south, a_ref[...], b_ref[...])
        g_ref[...] = g
        d_ref[...], nm_ref[...], nv_ref[...] = _adam_math(w_ref[...], g, m_ref[...], v_ref[...])

    spec = pl.BlockSpec((t, width), lambda i, c_ref: (i, 0))
    return pl.pallas_call(
        body, name=name,
        grid_spec=pltpu.PrefetchScalarGridSpec(num_scalar_prefetch=1, grid=(rows // t,),
                                               in_specs=[spec, half_spec, half_spec, spec, spec], out_specs=[spec] * 4),
        out_shape=[jax.ShapeDtypeStruct(w.shape, F32)] * 4, compiler_params=_params(("parallel",)),
    )(core, w, mine, theirs, m, v)


def _adamw_small(ws, gs, ms, vs):
    n = len(ws)

    def body(*refs):
        for t in range(n):
            w_ref, g_ref, m_ref, v_ref = (refs[k * n + t] for k in range(4))
            d, nm, nv = _adam_math(w_ref[...], g_ref[...], m_ref[...], v_ref[...])
            refs[4 * n + t][...] = d
            refs[5 * n + t][...] = nm
            refs[6 * n + t][...] = nv

    vmem = pl.BlockSpec(memory_space=pltpu.VMEM)
    res = pl.pallas_call(body, name="adamw_small", in_specs=[vmem] * (4 * n), out_specs=[vmem] * (3 * n),
                         out_shape=[jax.ShapeDtypeStruct(a.shape, F32) for a in ws] * 3)(*ws, *gs, *ms, *vs)
    return res[:n], res[n:2 * n], res[2 * n:]


def kernel(x, w_in, b_gate, g_q_a, w_uq, g_kv_a, w_ukv, w_o_mla, w_o_dil, w_out, ln1_g, ln1_b, w_ff1, w_ff2, ln2_g, ln2_b, loss_target, m_w_in, m_b_gate, m_g_q_a, m_w_uq, m_g_kv_a, m_w_ukv, m_w_o_mla, m_w_o_dil, m_w_out, m_ln1_g, m_ln1_b, m_w_ff1, m_w_ff2, m_ln2_g, m_ln2_b, v_w_in, v_b_gate, v_g_q_a, v_w_uq, v_g_kv_a, v_w_ukv, v_w_o_mla, v_w_o_dil, v_w_out, v_ln1_g, v_ln1_b, v_w_ff1, v_w_ff2, v_ln2_g, v_ln2_b):
    order = ("w_in", "b_gate", "g_q_a", "w_uq", "g_kv_a", "w_ukv", "w_o_mla", "w_o_dil", "w_out", "ln1_g", "ln1_b", "w_ff1", "w_ff2", "ln2_g", "ln2_b")
    w = dict(w_in=w_in, b_gate=b_gate, g_q_a=g_q_a, w_uq=w_uq, g_kv_a=g_kv_a, w_ukv=w_ukv, w_o_mla=w_o_mla, w_o_dil=w_o_dil, w_out=w_out,
             ln1_g=ln1_g, ln1_b=ln1_b, w_ff1=w_ff1, w_ff2=w_ff2, ln2_g=ln2_g, ln2_b=ln2_b)
    m = dict(w_in=m_w_in, b_gate=m_b_gate, g_q_a=m_g_q_a, w_uq=m_w_uq, g_kv_a=m_g_kv_a, w_ukv=m_w_ukv, w_o_mla=m_w_o_mla, w_o_dil=m_w_o_dil,
             w_out=m_w_out, ln1_g=m_ln1_g, ln1_b=m_ln1_b, w_ff1=m_w_ff1, w_ff2=m_w_ff2, ln2_g=m_ln2_g, ln2_b=m_ln2_b)
    v = dict(w_in=v_w_in, b_gate=v_b_gate, g_q_a=v_g_q_a, w_uq=v_w_uq, g_kv_a=v_g_kv_a, w_ukv=v_w_ukv, w_o_mla=v_w_o_mla, w_o_dil=v_w_o_dil,
             w_out=v_w_out, ln1_g=v_ln1_g, ln1_b=v_ln1_b, w_ff1=v_w_ff1, w_ff2=v_w_ff2, ln2_g=v_ln2_g, ln2_b=v_ln2_b)
    chip = 2 * lax.axis_index("x") + lax.axis_index("y")
    south = (lax.axis_index("c") == 0).astype(F32)
    gate_w = D_MODEL // N_CHIPS

    core = lax.axis_index("c").astype(jnp.int32).reshape(1)
    turn = lambda n, a: a.T if n in TRANSPOSED else a
    early_shards = [turn(n, w[n][0]).astype(BF16) for n in EARLY]
    gathered = lambda group: [jax.ShapeDtypeStruct((N_CHIPS,) + s.shape, BF16) for s in group]
    e_sems, e_srcs, e_lands, e_token = _split_start("gather", early_shards, gathered(early_shards), "gather_early_start")
    e_token, late_f32, x_f32, positions = lax.optimization_barrier((e_token, [w[n][0] for n in LATE], x, jnp.arange(x.shape[1], dtype=F32)))
    late_shards, x_lo = [a.astype(BF16) for a in late_f32], x_f32.astype(BF16)
    rope = _rope_tables(positions)
    shards = dict(zip(EARLY + LATE, early_shards + late_shards))
    b_mine = lax.dynamic_update_slice(jnp.zeros((2, D_MODEL), F32), b_gate[0] * south, (0, chip * gate_w))
    b_full = _sum_all_devices(b_mine.reshape(-1, LANES), "gather_b_gate", [e_token, *late_shards, *rope, x_lo]).reshape(2, D_MODEL)
    e_srcs, e_lands = _split_wait("gather", e_sems, e_srcs, e_lands, b_full, "gather_early_wait")
    e_forward = _split_start("forward", e_srcs, None, "gather_early_forward_start", lands=e_lands)
    first = dict(zip(EARLY, _split_wait("forward", *e_forward[:3], e_forward[-1], "gather_early_forward_wait")[1]))
    g_sems, g_srcs, g_lands, g_token = _split_start("gather", late_shards, gathered(late_shards), "gather_late_start", after=first[EARLY[-1]])
    tables = _attention_tables(x.shape[1])

    sent = {}

    def late_arrived(after):
        srcs, lands = _split_wait("gather", g_sems, g_srcs, g_lands, after, "gather_late_wait")
        sent["forward"] = _split_start("forward", srcs, None, "gather_late_forward_start", lands=lands)
        return sent["forward"][-1]

    def late_weights(after):
        return dict(zip(LATE, _split_wait("forward", *sent["forward"][:3], after, "gather_late_forward_wait")[1]))

    exchange_shapes = lambda parts: [jax.ShapeDtypeStruct((3,) + p.shape[1:], BF16) for p in parts]

    def early_grads(grads_late):
        gs = [grads_late[n] for n in LATE]
        shapes = [jax.ShapeDtypeStruct((N_CHIPS,) + _half_shape(g.shape[1:]), F32) for g in gs]
        sent["pair"] = _split_start("pair", gs, shapes, "pair_split_late_start")
        return sent["pair"][-1]

    def early_grads_go(after):
        gs, theirs = _split_wait("pair", *sent["pair"][:3], after, "pair_split_late_wait")
        parts = [_pair_add(g, th, core, "pair_add_" + n) for g, th, n in zip(gs, theirs, LATE)]
        sent["late"] = _split_start("scatter", parts, exchange_shapes(parts), "exchange_late_start")
        return sent["late"][-1]

    def first_grad(g):
        sent["first"] = _split_start("pair", [g], [jax.ShapeDtypeStruct((N_CHIPS,) + _half_shape(g.shape[1:]), F32)], "pair_split_w_in_start")
        return sent["first"][-1]

    def last_grads(grads_early):
        rest = [grads_early[n] for n in EARLY[1:]]
        gs, theirs = _split_wait("pair", *sent["first"][:3], rest[-1], "pair_split_w_in_wait")
        gs, theirs = list(gs) + rest, list(theirs) + list(_pair_split(rest, "pair_split_early"))
        parts = [_pair_add(g, th, core, "pair_add_" + n) for g, th, n in zip(gs, theirs, EARLY)]
        sent["early"] = _split_start("scatter", parts, exchange_shapes(parts), "exchange_early_start")
        return sent["early"][-1]

    loss_part, grad_x, grads = _local_step(x, loss_target, first, b_full, g_q_a, g_kv_a, ln1_g, ln1_b, ln2_g, ln2_b, token=g_token,
                                           late_arrived=late_arrived, late_weights=late_weights, early_grads=early_grads, early_grads_go=early_grads_go,
                                           first_grad=first_grad, last_grads=last_grads, tables=tables, rope=rope, x_lo=x_lo)

    g_out, delta, new_m, new_v = {}, {}, {}, {}
    chip1 = chip.astype(jnp.int32).reshape(1)

    def sum_and_send(names, parts, others, tag):
        totals = [_chip_sum(p, o, chip1, "chip_sum_" + n) for n, p, o in zip(names, parts, others)]
        return _split_start("join", totals, [jax.ShapeDtypeStruct(t.shape, F32) for t in totals], "pair_join_" + tag + "_start")

    def adam(names, joined, after, tag):
        totals, halves = _split_wait("join", *joined[:3], after, "pair_join_" + tag + "_wait")
        for n, mine, theirs in zip(names, totals, halves):
            res = _adamw_big(turn(n, w[n][0]), mine, theirs, turn(n, m[n][0]), turn(n, v[n][0]), core, "adamw_" + n,
                             side_by_side=mine.shape[0] == shards[n].shape[0])
            g_out[n], delta[n], new_m[n], new_v[n] = (turn(n, r) for r in res)

    late_joined = sum_and_send(LATE, *_split_wait("scatter", *sent["late"][:3], grad_x, "exchange_late_wait"), "late")
    early_joined = sum_and_send(EARLY, *_split_wait("scatter", *sent["early"][:3], late_joined[-1], "exchange_early_wait"), "early")
    small_names = [name for name, _ in SMALL]
    sums = _sum_small([grads[name] for name in small_names] + [loss_part])
    loss = sums[-1][0, 0]
    g_small = dict(zip(small_names, sums))
    g_small["b_gate"] = lax.dynamic_slice(g_small["b_gate"], (0, chip * gate_w), (2, gate_w))
    flat = lambda a: a.reshape(-1, a.shape[-1])
    res = _adamw_small(*[[flat(d[name]) for name in small_names] for d in (w, g_small, m, v)])
    g_out.update(g_small)
    for d, r in zip((delta, new_m, new_v), res):
        d.update(zip(small_names, r))
    adam(LATE, late_joined, res[0][0], "late")
    adam(EARLY, early_joined, delta[LATE[-1]], "early")

    lead = lambda d: [d[name].reshape(w[name].shape) for name in order]
    return (loss, grad_x, *lead(g_out), *lead(delta), *lead(new_m), *lead(new_v))
```
